```python
import jax
import jax.numpy as jnp
from jax import lax
import numpy as np

D_MODEL = 1024
BATCH = 8
SEQ = 4096
DEPTH = 2

HEAD_DIM = 64
SB_HEADS = D_MODEL // 256
SB_WIDTH = SB_HEADS * HEAD_DIM
CV_WIDTH = D_MODEL // 4
CV_KERNEL = 31
DL_HEADS = D_MODEL // 128
DL_WIDTH = DL_HEADS * HEAD_DIM
MIX_WIDTH = SB_WIDTH + CV_WIDTH + DL_WIDTH
IN_WIDTH = 3 * SB_WIDTH + 2 * CV_WIDTH + 3 * DL_WIDTH
DL_PATTERN = ((128, 1), (512, 4), (2048, 16))
BLOCK = 128
ROPE_THETA = 10000.0
N_MEM = 256
X_HEADS = 4
X_HEAD_DIM = D_MODEL // X_HEADS
D_FF = 2816
FFN_KERNEL = 3
EPS = 1e-6

kernel_name = 'hybrid_stickbreak_conformer_dilated'


def rms_norm(x, g):
    xf = x.astype(jnp.float32)
    y = xf * lax.rsqrt(jnp.mean(xf * xf, axis=-1, keepdims=True) + EPS)
    return (y * g.astype(jnp.float32)).astype(x.dtype)


def layer_norm(x, g, b):
    xf = x.astype(jnp.float32)
    mu = jnp.mean(xf, axis=-1, keepdims=True)
    var = jnp.mean(jnp.square(xf - mu), axis=-1, keepdims=True)
    y = (xf - mu) * lax.rsqrt(var + EPS)
    return (y * g.astype(jnp.float32) + b.astype(jnp.float32)).astype(x.dtype)


def causal_depthwise_conv(x, w, b):
    k_width, ch = w.shape
    y = lax.conv_general_dilated(
        x, w[:, None, :].astype(x.dtype), window_strides=(1,),
        padding=((k_width - 1, 0),), dimension_numbers=('NWC', 'WIO', 'NWC'),
        feature_group_count=ch)
    return y + b.astype(x.dtype)


def rope(x, positions):
    half = x.shape[-1] // 2
    inv_freq = ROPE_THETA ** (-jnp.arange(half, dtype=jnp.float32) / half)
    ang = positions.astype(jnp.float32)[..., None] * inv_freq
    cos = jnp.cos(ang)[:, :, None, :]
    sin = jnp.sin(ang)[:, :, None, :]
    xf = x.astype(jnp.float32)
    x1, x2 = xf[..., :half], xf[..., half:]
    return jnp.concatenate([x1 * cos - x2 * sin, x2 * cos + x1 * sin], axis=-1).astype(x.dtype)


def stick_breaking_attention(q, k, v):
    b_, s_, h_, hd = q.shape
    scale = hd ** -0.5
    qh, kh, vh = (t.transpose(0, 2, 1, 3) for t in (q, k, v))
    outs = []
    for n in range(s_ // BLOCK):
        t0, t1 = n * BLOCK, (n + 1) * BLOCK
        z = jnp.einsum('bhqc,bhkc->bhqk', qh[:, :, t0:t1], kh[:, :, :t1]).astype(jnp.float32) * scale
        t_idx = t0 + jnp.arange(BLOCK)[:, None]
        s_idx = jnp.arange(t1)[None, :]
        before = s_idx < t_idx
        log_keep = jnp.where(before, jax.nn.log_sigmoid(-z), 0.0)
        between = lax.cumsum(log_keep, axis=3, reverse=True) - log_keep
        a = jnp.where(before, jnp.exp(jax.nn.log_sigmoid(z) + between), 0.0)
        outs.append(jnp.einsum('bhqk,bhkc->bhqc', a.astype(vh.dtype), vh[:, :, :t1]))
    o = jnp.concatenate(outs, axis=2)
    return o.transpose(0, 2, 1, 3)


def dilated_branch(q, k, v, window, dilation):
    b_, s_, h_, hd = q.shape
    w_steps = window // dilation
    assert w_steps <= BLOCK
    seq_sub = s_ // dilation
    nb = -(-seq_sub // BLOCK)
    pad_len = nb * BLOCK - seq_sub
    scale = hd ** -0.5

    def to_sub(t):
        t = t.reshape(b_, seq_sub, dilation, h_, hd).transpose(0, 2, 3, 1, 4)
        t = jnp.pad(t, ((0, 0), (0, 0), (0, 0), (0, pad_len), (0, 0)))
        return t.reshape(b_, dilation, h_, nb, BLOCK, hd)

    def with_prev(t):
        prev = jnp.pad(t, ((0, 0), (0, 0), (0, 0), (1, 0), (0, 0), (0, 0)))[:, :, :, :-1]
        return jnp.concatenate([prev, t], axis=4)

    qs = to_sub(q)
    kw = with_prev(to_sub(k))
    vw = with_prev(to_sub(v))
    s = jnp.einsum('brhnqc,brhnkc->brhnqk', qs, kw).astype(jnp.float32) * scale
    qi = jnp.arange(nb)[:, None, None] * BLOCK + jnp.arange(BLOCK)[None, :, None]
    ki = (jnp.arange(nb)[:, None, None] - 1) * BLOCK + jnp.arange(2 * BLOCK)[None, None, :]
    dist = qi - ki
    valid = (dist >= 0) & (dist <= w_steps) & (ki >= 0)
    s = jnp.where(valid, s, -jnp.inf)
    m = jnp.max(s, axis=-1, keepdims=True)
    p = jnp.exp(s - m)
    den = jnp.sum(p, axis=-1, keepdims=True)
    o = jnp.einsum('brhnqk,brhnkc->brhnqc', (p / den).astype(v.dtype), vw)
    lse = (m + jnp.log(den))[..., 0]
    o = o.reshape(b_, dilation, h_, nb * BLOCK, hd)[:, :, :, :seq_sub]
    o = o.transpose(0, 3, 1, 2, 4).reshape(b_, s_, h_, hd)
    lse = lse.reshape(b_, dilation, h_, nb * BLOCK)[:, :, :, :seq_sub]
    lse = lse.transpose(0, 3, 1, 2).reshape(b_, s_, h_)
    return o, lse


def dilated_mixture(q, k, v):
    outs, lses = [], []
    for window, dilation in DL_PATTERN:
        o, lse = dilated_branch(q, k, v, window, dilation)
        outs.append(o)
        lses.append(lse)
    wts = jax.nn.softmax(jnp.stack(lses, axis=-1), axis=-1)
    o = jnp.einsum('bshn,nbshc->bshc', wts, jnp.stack(outs, axis=0).astype(jnp.float32))
    return o.astype(v.dtype)


def conformer_conv(val, gate, cv_w, cv_b, cv_ln_g, cv_ln_b, cv_pw_w, cv_pw_b):
    g = val * jax.nn.sigmoid(gate)
    c = causal_depthwise_conv(g, cv_w, cv_b)
    c = jax.nn.silu(layer_norm(c, cv_ln_g, cv_ln_b))
    return c @ cv_pw_w + cv_pw_b


def hybrid_mixer(h, positions, w_in, cv_w, cv_b, cv_ln_g, cv_ln_b, cv_pw_w, cv_pw_b, w_out):
    b_, s_, _ = h.shape
    u = h @ w_in
    o1 = 3 * SB_WIDTH
    o2 = o1 + 2 * CV_WIDTH
    sb = u[..., :o1].reshape(b_, s_, 3, SB_HEADS, HEAD_DIM)
    cv = u[..., o1:o2]
    dl = u[..., o2:].reshape(b_, s_, 3, DL_HEADS, HEAD_DIM)
    a_out = stick_breaking_attention(sb[:, :, 0], sb[:, :, 1], sb[:, :, 2]).reshape(b_, s_, SB_WIDTH)
    b_out = conformer_conv(cv[..., :CV_WIDTH], cv[..., CV_WIDTH:], cv_w, cv_b, cv_ln_g, cv_ln_b, cv_pw_w, cv_pw_b)
    q = rope(dl[:, :, 0], positions)
    k = rope(dl[:, :, 1], positions)
    c_out = dilated_mixture(q, k, dl[:, :, 2]).reshape(b_, s_, DL_WIDTH)
    return jnp.concatenate([a_out, b_out, c_out], axis=-1) @ w_out


def memory_cross_attention(h, mem_n, wq, wk, wv, wo):
    b_, s_, _ = h.shape
    q = (h @ wq).reshape(b_, s_, X_HEADS, X_HEAD_DIM)
    k = (mem_n @ wk).reshape(b_, -1, X_HEADS, X_HEAD_DIM)
    v = (mem_n @ wv).reshape(b_, -1, X_HEADS, X_HEAD_DIM)
    s = jnp.einsum('bshc,bmhc->bhsm', q, k).astype(jnp.float32) * (X_HEAD_DIM ** -0.5)
    p = jax.nn.softmax(s, axis=-1).astype(v.dtype)
    o = jnp.einsum('bhsm,bmhc->bshc', p, v).reshape(b_, s_, D_MODEL)
    return o @ wo


def conv_ffn(h, w_up, conv_w, conv_b, w_down):
    u = causal_depthwise_conv(h @ w_up, conv_w, conv_b)
    gate, val = u[..., :D_FF], u[..., D_FF:]
    return (jax.nn.gelu(gate, approximate=True) * val) @ w_down


def _fwd_setup_inputs(seed: int = 0) -> dict:
    key = jax.random.key(seed)
    ks = jax.random.split(key, 32)
    f32 = jnp.float32

    def dense(k, shape, fan_in):
        return jax.random.normal(k, shape, f32) * (fan_in ** -0.5)

    def gain(k, shape):
        return 1.0 + 0.02 * jax.random.normal(k, shape, f32)

    def bias(k, shape):
        return 0.02 * jax.random.normal(k, shape, f32)

    offset = jax.random.randint(ks[2], (BATCH, 1), 0, 1024, dtype=jnp.int32)
    positions = (offset + jnp.arange(SEQ, dtype=jnp.int32)[None, :]).astype(jnp.int32)
    return {
        'x': jax.random.normal(ks[0], (BATCH, SEQ, D_MODEL), f32),
        'mem': jax.random.normal(ks[1], (BATCH, N_MEM, D_MODEL), f32),
        'positions': positions,
        'mix_norm_pre': gain(ks[3], (DEPTH, D_MODEL)),
        'w_in': dense(ks[4], (DEPTH, D_MODEL, IN_WIDTH), D_MODEL),
        'cv_w': dense(ks[5], (DEPTH, CV_KERNEL, CV_WIDTH), CV_KERNEL),
        'cv_b': bias(ks[6], (DEPTH, CV_WIDTH)),
        'cv_ln_g': gain(ks[7], (DEPTH, CV_WIDTH)),
        'cv_ln_b': bias(ks[8], (DEPTH, CV_WIDTH)),
        'cv_pw_w': dense(ks[9], (DEPTH, CV_WIDTH, CV_WIDTH), CV_WIDTH),
        'cv_pw_b': bias(ks[10], (DEPTH, CV_WIDTH)),
        'w_out': dense(ks[11], (DEPTH, MIX_WIDTH, D_MODEL), MIX_WIDTH),
        'mix_norm_post': gain(ks[12], (DEPTH, D_MODEL)),
        'x_norm_pre': gain(ks[13], (DEPTH, D_MODEL)),
        'mem_norm': gain(ks[14], (DEPTH, D_MODEL)),
        'x_wq': dense(ks[15], (DEPTH, D_MODEL, D_MODEL), D_MODEL),
        'x_wk': dense(ks[16], (DEPTH, D_MODEL, D_MODEL), D_MODEL),
        'x_wv': dense(ks[17], (DEPTH, D_MODEL, D_MODEL), D_MODEL),
        'x_wo': dense(ks[18], (DEPTH, D_MODEL, D_MODEL), D_MODEL),
        'x_norm_post': gain(ks[19], (DEPTH, D_MODEL)),
        'ffn_norm_pre': gain(ks[20], (DEPTH, D_MODEL)),
        'ffn_w_up': dense(ks[21], (DEPTH, D_MODEL, 2 * D_FF), D_MODEL),
        'ffn_conv_w': dense(ks[22], (DEPTH, FFN_KERNEL, 2 * D_FF), FFN_KERNEL),
        'ffn_conv_b': bias(ks[23], (DEPTH, 2 * D_FF)),
        'ffn_w_down': dense(ks[24], (DEPTH, D_FF, D_MODEL), D_FF),
        'ffn_norm_post': gain(ks[25], (DEPTH, D_MODEL)),
    }


def _fwd_reference(x, mem, positions, mix_norm_pre, w_in, cv_w, cv_b, cv_ln_g, cv_ln_b, cv_pw_w, cv_pw_b,
              w_out, mix_norm_post, x_norm_pre, mem_norm, x_wq, x_wk, x_wv, x_wo, x_norm_post,
              ffn_norm_pre, ffn_w_up, ffn_conv_w, ffn_conv_b, ffn_w_down, ffn_norm_post):
    h = x
    for l in range(DEPTH):
        y = hybrid_mixer(rms_norm(h, mix_norm_pre[l]), positions, w_in[l], cv_w[l], cv_b[l],
                         cv_ln_g[l], cv_ln_b[l], cv_pw_w[l], cv_pw_b[l], w_out[l])
        h = h + rms_norm(y, mix_norm_post[l])
        y = memory_cross_attention(rms_norm(h, x_norm_pre[l]), rms_norm(mem, mem_norm[l]),
                                   x_wq[l], x_wk[l], x_wv[l], x_wo[l])
        h = h + rms_norm(y, x_norm_post[l])
        y = conv_ffn(rms_norm(h, ffn_norm_pre[l]), ffn_w_up[l], ffn_conv_w[l], ffn_conv_b[l], ffn_w_down[l])
        h = h + rms_norm(y, ffn_norm_post[l])
    return h


import jax as _jax
import jax.numpy as _jnp

TWIN_FORMAT = 'train_step'
FWD_PARAMS = ['x', 'mem', 'positions', 'mix_norm_pre', 'w_in', 'cv_w', 'cv_b', 'cv_ln_g', 'cv_ln_b', 'cv_pw_w', 'cv_pw_b', 'w_out', 'mix_norm_post', 'x_norm_pre', 'mem_norm', 'x_wq', 'x_wk', 'x_wv', 'x_wo', 'x_norm_post', 'ffn_norm_pre', 'ffn_w_up', 'ffn_conv_w', 'ffn_conv_b', 'ffn_w_down', 'ffn_norm_post']
TWIN_WEIGHTS = ['mix_norm_pre', 'w_in', 'cv_w', 'cv_b', 'cv_ln_g', 'cv_ln_b', 'cv_pw_w', 'cv_pw_b', 'w_out', 'mix_norm_post', 'x_norm_pre', 'mem_norm', 'x_wq', 'x_wk', 'x_wv', 'x_wo', 'x_norm_post', 'ffn_norm_pre', 'ffn_w_up', 'ffn_conv_w', 'ffn_conv_b', 'ffn_w_down', 'ffn_norm_post']
TWIN_DIFF_INPUT = 'x'
TWIN_INPUTS = ['x', 'mem', 'positions', 'mix_norm_pre', 'w_in', 'cv_w', 'cv_b', 'cv_ln_g', 'cv_ln_b', 'cv_pw_w', 'cv_pw_b', 'w_out', 'mix_norm_post', 'x_norm_pre', 'mem_norm', 'x_wq', 'x_wk', 'x_wv', 'x_wo', 'x_norm_post', 'ffn_norm_pre', 'ffn_w_up', 'ffn_conv_w', 'ffn_conv_b', 'ffn_w_down', 'ffn_norm_post', 'loss_target', 'm_mix_norm_pre', 'm_w_in', 'm_cv_w', 'm_cv_b', 'm_cv_ln_g', 'm_cv_ln_b', 'm_cv_pw_w', 'm_cv_pw_b', 'm_w_out', 'm_mix_norm_post', 'm_x_norm_pre', 'm_mem_norm', 'm_x_wq', 'm_x_wk', 'm_x_wv', 'm_x_wo', 'm_x_norm_post', 'm_ffn_norm_pre', 'm_ffn_w_up', 'm_ffn_conv_w', 'm_ffn_conv_b', 'm_ffn_w_down', 'm_ffn_norm_post', 'v_mix_norm_pre', 'v_w_in', 'v_cv_w', 'v_cv_b', 'v_cv_ln_g', 'v_cv_ln_b', 'v_cv_pw_w', 'v_cv_pw_b', 'v_w_out', 'v_mix_norm_post', 'v_x_norm_pre', 'v_mem_norm', 'v_x_wq', 'v_x_wk', 'v_x_wv', 'v_x_wo', 'v_x_norm_post', 'v_ffn_norm_pre', 'v_ffn_w_up', 'v_ffn_conv_w', 'v_ffn_conv_b', 'v_ffn_w_down', 'v_ffn_norm_post']
TWIN_OUTPUTS = ['loss', 'grad_x', 'grad_mix_norm_pre', 'grad_w_in', 'grad_cv_w', 'grad_cv_b', 'grad_cv_ln_g', 'grad_cv_ln_b', 'grad_cv_pw_w', 'grad_cv_pw_b', 'grad_w_out', 'grad_mix_norm_post', 'grad_x_norm_pre', 'grad_mem_norm', 'grad_x_wq', 'grad_x_wk', 'grad_x_wv', 'grad_x_wo', 'grad_x_norm_post', 'grad_ffn_norm_pre', 'grad_ffn_w_up', 'grad_ffn_conv_w', 'grad_ffn_conv_b', 'grad_ffn_w_down', 'grad_ffn_norm_post', 'delta_mix_norm_pre', 'delta_w_in', 'delta_cv_w', 'delta_cv_b', 'delta_cv_ln_g', 'delta_cv_ln_b', 'delta_cv_pw_w', 'delta_cv_pw_b', 'delta_w_out', 'delta_mix_norm_post', 'delta_x_norm_pre', 'delta_mem_norm', 'delta_x_wq', 'delta_x_wk', 'delta_x_wv', 'delta_x_wo', 'delta_x_norm_post', 'delta_ffn_norm_pre', 'delta_ffn_w_up', 'delta_ffn_conv_w', 'delta_ffn_conv_b', 'delta_ffn_w_down', 'delta_ffn_norm_post', 'new_m_mix_norm_pre', 'new_m_w_in', 'new_m_cv_w', 'new_m_cv_b', 'new_m_cv_ln_g', 'new_m_cv_ln_b', 'new_m_cv_pw_w', 'new_m_cv_pw_b', 'new_m_w_out', 'new_m_mix_norm_post', 'new_m_x_norm_pre', 'new_m_mem_norm', 'new_m_x_wq', 'new_m_x_wk', 'new_m_x_wv', 'new_m_x_wo', 'new_m_x_norm_post', 'new_m_ffn_norm_pre', 'new_m_ffn_w_up', 'new_m_ffn_conv_w', 'new_m_ffn_conv_b', 'new_m_ffn_w_down', 'new_m_ffn_norm_post', 'new_v_mix_norm_pre', 'new_v_w_in', 'new_v_cv_w', 'new_v_cv_b', 'new_v_cv_ln_g', 'new_v_cv_ln_b', 'new_v_cv_pw_w', 'new_v_cv_pw_b', 'new_v_w_out', 'new_v_mix_norm_post', 'new_v_x_norm_pre', 'new_v_mem_norm', 'new_v_x_wq', 'new_v_x_wk', 'new_v_x_wv', 'new_v_x_wo', 'new_v_x_norm_post', 'new_v_ffn_norm_pre', 'new_v_ffn_w_up', 'new_v_ffn_conv_w', 'new_v_ffn_conv_b', 'new_v_ffn_w_down', 'new_v_ffn_norm_post']
TWIN_LEAF_KINDS = {'loss': 'loss', 'grad_x': 'grad_x', 'grad_mix_norm_pre': 'grad_w', 'grad_w_in': 'grad_w', 'grad_cv_w': 'grad_w', 'grad_cv_b': 'grad_w', 'grad_cv_ln_g': 'grad_w', 'grad_cv_ln_b': 'grad_w', 'grad_cv_pw_w': 'grad_w', 'grad_cv_pw_b': 'grad_w', 'grad_w_out': 'grad_w', 'grad_mix_norm_post': 'grad_w', 'grad_x_norm_pre': 'grad_w', 'grad_mem_norm': 'grad_w', 'grad_x_wq': 'grad_w', 'grad_x_wk': 'grad_w', 'grad_x_wv': 'grad_w', 'grad_x_wo': 'grad_w', 'grad_x_norm_post': 'grad_w', 'grad_ffn_norm_pre': 'grad_w', 'grad_ffn_w_up': 'grad_w', 'grad_ffn_conv_w': 'grad_w', 'grad_ffn_conv_b': 'grad_w', 'grad_ffn_w_down': 'grad_w', 'grad_ffn_norm_post': 'grad_w', 'delta_mix_norm_pre': 'delta_w', 'delta_w_in': 'delta_w', 'delta_cv_w': 'delta_w', 'delta_cv_b': 'delta_w', 'delta_cv_ln_g': 'delta_w', 'delta_cv_ln_b': 'delta_w', 'delta_cv_pw_w': 'delta_w', 'delta_cv_pw_b': 'delta_w', 'delta_w_out': 'delta_w', 'delta_mix_norm_post': 'delta_w', 'delta_x_norm_pre': 'delta_w', 'delta_mem_norm': 'delta_w', 'delta_x_wq': 'delta_w', 'delta_x_wk': 'delta_w', 'delta_x_wv': 'delta_w', 'delta_x_wo': 'delta_w', 'delta_x_norm_post': 'delta_w', 'delta_ffn_norm_pre': 'delta_w', 'delta_ffn_w_up': 'delta_w', 'delta_ffn_conv_w': 'delta_w', 'delta_ffn_conv_b': 'delta_w', 'delta_ffn_w_down': 'delta_w', 'delta_ffn_norm_post': 'delta_w', 'new_m_mix_norm_pre': 'new_m', 'new_m_w_in': 'new_m', 'new_m_cv_w': 'new_m', 'new_m_cv_b': 'new_m', 'new_m_cv_ln_g': 'new_m', 'new_m_cv_ln_b': 'new_m', 'new_m_cv_pw_w': 'new_m', 'new_m_cv_pw_b': 'new_m', 'new_m_w_out': 'new_m', 'new_m_mix_norm_post': 'new_m', 'new_m_x_norm_pre': 'new_m', 'new_m_mem_norm': 'new_m', 'new_m_x_wq': 'new_m', 'new_m_x_wk': 'new_m', 'new_m_x_wv': 'new_m', 'new_m_x_wo': 'new_m', 'new_m_x_norm_post': 'new_m', 'new_m_ffn_norm_pre': 'new_m', 'new_m_ffn_w_up': 'new_m', 'new_m_ffn_conv_w': 'new_m', 'new_m_ffn_conv_b': 'new_m', 'new_m_ffn_w_down': 'new_m', 'new_m_ffn_norm_post': 'new_m', 'new_v_mix_norm_pre': 'new_v', 'new_v_w_in': 'new_v', 'new_v_cv_w': 'new_v', 'new_v_cv_b': 'new_v', 'new_v_cv_ln_g': 'new_v', 'new_v_cv_ln_b': 'new_v', 'new_v_cv_pw_w': 'new_v', 'new_v_cv_pw_b': 'new_v', 'new_v_w_out': 'new_v', 'new_v_mix_norm_post': 'new_v', 'new_v_x_norm_pre': 'new_v', 'new_v_mem_norm': 'new_v', 'new_v_x_wq': 'new_v', 'new_v_x_wk': 'new_v', 'new_v_x_wv': 'new_v', 'new_v_x_wo': 'new_v', 'new_v_x_norm_post': 'new_v', 'new_v_ffn_norm_pre': 'new_v', 'new_v_ffn_w_up': 'new_v', 'new_v_ffn_conv_w': 'new_v', 'new_v_ffn_conv_b': 'new_v', 'new_v_ffn_w_down': 'new_v', 'new_v_ffn_norm_post': 'new_v'}


def _forward(args):
    return _fwd_reference(*[args[k] for k in FWD_PARAMS])


def _output_shape():
    out = _jax.eval_shape(lambda: _forward(_fwd_setup_inputs(0)))
    return out.shape, out.dtype

N_MICROBATCH = 1
ADAM_LR = 0.001
ADAM_B1 = 0.9
ADAM_B2 = 0.999
ADAM_EPS = 1e-08
ADAM_WD = 0.01
ADAM_STEP = 10
PER_EXAMPLE_BATCH_AXIS = {'x': 0, 'mem': 0, 'positions': 0, 'loss_target': 0}
SHARED_INPUTS = []
_WEIGHT_DTYPES = {'mix_norm_pre': _jnp.float32, 'w_in': _jnp.float32, 'cv_w': _jnp.float32, 'cv_b': _jnp.float32, 'cv_ln_g': _jnp.float32, 'cv_ln_b': _jnp.float32, 'cv_pw_w': _jnp.float32, 'cv_pw_b': _jnp.float32, 'w_out': _jnp.float32, 'mix_norm_post': _jnp.float32, 'x_norm_pre': _jnp.float32, 'mem_norm': _jnp.float32, 'x_wq': _jnp.float32, 'x_wk': _jnp.float32, 'x_wv': _jnp.float32, 'x_wo': _jnp.float32, 'x_norm_post': _jnp.float32, 'ffn_norm_pre': _jnp.float32, 'ffn_w_up': _jnp.float32, 'ffn_conv_w': _jnp.float32, 'ffn_conv_b': _jnp.float32, 'ffn_w_down': _jnp.float32, 'ffn_norm_post': _jnp.float32}
MOMENT_SCALE = {'mix_norm_pre': 5.216061e+00, 'w_in': 3.224351e+00, 'cv_w': 3.622196e+00, 'cv_b': 4.309268e+01, 'cv_ln_g': 1.720558e+01, 'cv_ln_b': 2.318519e+01, 'cv_pw_w': 1.033118e+01, 'cv_pw_b': 4.752930e+01, 'w_out': 7.592994e+00, 'mix_norm_post': 3.348552e+01, 'x_norm_pre': 2.905475e+00, 'mem_norm': 1.201593e+01, 'x_wq': 2.949433e+00, 'x_wk': 2.967743e+00, 'x_wv': 1.160632e+01, 'x_wo': 1.170498e+01, 'x_norm_post': 3.458064e+01, 'ffn_norm_pre': 4.491227e+00, 'ffn_w_up': 1.844864e+00, 'ffn_conv_w': 2.301448e+00, 'ffn_conv_b': 5.794906e+00, 'ffn_w_down': 4.114114e+00, 'ffn_norm_post': 3.231548e+01}


def _to_microbatches(a, axis):
    t = _jnp.moveaxis(a, axis, 0)
    t = t.reshape((N_MICROBATCH, t.shape[0] // N_MICROBATCH) + t.shape[1:])
    return _jnp.moveaxis(t, 1, axis + 1)


def setup_inputs(seed: int = 0) -> dict:
    inp = _fwd_setup_inputs(seed)
    key = _jax.random.fold_in(_jax.random.key(seed), 7919)
    shape, _ = _output_shape()
    out = dict(inp)
    out["loss_target"] = _jax.random.normal(_jax.random.fold_in(key, 0), shape, _jnp.float32)
    for i, name in enumerate(TWIN_WEIGHTS):
        w = inp[name].astype(_jnp.float32)
        if MOMENT_SCALE is None:
            s = _jnp.sqrt(_jnp.mean(_jnp.square(w)) + 1e-30)
        else:
            s = MOMENT_SCALE[name]
        km, kv = _jax.random.split(_jax.random.fold_in(key, i + 1))
        out[name] = w
        out["m_" + name] = s * _jax.random.normal(km, w.shape, _jnp.float32)
        out["v_" + name] = (s * s) * _jax.random.uniform(kv, w.shape, _jnp.float32, 0.5, 1.5)
    if N_MICROBATCH > 1:
        for name, axis in PER_EXAMPLE_BATCH_AXIS.items():
            out[name] = _to_microbatches(out[name], axis)
    return {'x': out['x'], 'mem': out['mem'], 'positions': out['positions'], 'mix_norm_pre': out['mix_norm_pre'], 'w_in': out['w_in'], 'cv_w': out['cv_w'], 'cv_b': out['cv_b'], 'cv_ln_g': out['cv_ln_g'], 'cv_ln_b': out['cv_ln_b'], 'cv_pw_w': out['cv_pw_w'], 'cv_pw_b': out['cv_pw_b'], 'w_out': out['w_out'], 'mix_norm_post': out['mix_norm_post'], 'x_norm_pre': out['x_norm_pre'], 'mem_norm': out['mem_norm'], 'x_wq': out['x_wq'], 'x_wk': out['x_wk'], 'x_wv': out['x_wv'], 'x_wo': out['x_wo'], 'x_norm_post': out['x_norm_post'], 'ffn_norm_pre': out['ffn_norm_pre'], 'ffn_w_up': out['ffn_w_up'], 'ffn_conv_w': out['ffn_conv_w'], 'ffn_conv_b': out['ffn_conv_b'], 'ffn_w_down': out['ffn_w_down'], 'ffn_norm_post': out['ffn_norm_post'], 'loss_target': out['loss_target'], 'm_mix_norm_pre': out['m_mix_norm_pre'], 'm_w_in': out['m_w_in'], 'm_cv_w': out['m_cv_w'], 'm_cv_b': out['m_cv_b'], 'm_cv_ln_g': out['m_cv_ln_g'], 'm_cv_ln_b': out['m_cv_ln_b'], 'm_cv_pw_w': out['m_cv_pw_w'], 'm_cv_pw_b': out['m_cv_pw_b'], 'm_w_out': out['m_w_out'], 'm_mix_norm_post': out['m_mix_norm_post'], 'm_x_norm_pre': out['m_x_norm_pre'], 'm_mem_norm': out['m_mem_norm'], 'm_x_wq': out['m_x_wq'], 'm_x_wk': out['m_x_wk'], 'm_x_wv': out['m_x_wv'], 'm_x_wo': out['m_x_wo'], 'm_x_norm_post': out['m_x_norm_post'], 'm_ffn_norm_pre': out['m_ffn_norm_pre'], 'm_ffn_w_up': out['m_ffn_w_up'], 'm_ffn_conv_w': out['m_ffn_conv_w'], 'm_ffn_conv_b': out['m_ffn_conv_b'], 'm_ffn_w_down': out['m_ffn_w_down'], 'm_ffn_norm_post': out['m_ffn_norm_post'], 'v_mix_norm_pre': out['v_mix_norm_pre'], 'v_w_in': out['v_w_in'], 'v_cv_w': out['v_cv_w'], 'v_cv_b': out['v_cv_b'], 'v_cv_ln_g': out['v_cv_ln_g'], 'v_cv_ln_b': out['v_cv_ln_b'], 'v_cv_pw_w': out['v_cv_pw_w'], 'v_cv_pw_b': out['v_cv_pw_b'], 'v_w_out': out['v_w_out'], 'v_mix_norm_post': out['v_mix_norm_post'], 'v_x_norm_pre': out['v_x_norm_pre'], 'v_mem_norm': out['v_mem_norm'], 'v_x_wq': out['v_x_wq'], 'v_x_wk': out['v_x_wk'], 'v_x_wv': out['v_x_wv'], 'v_x_wo': out['v_x_wo'], 'v_x_norm_post': out['v_x_norm_post'], 'v_ffn_norm_pre': out['v_ffn_norm_pre'], 'v_ffn_w_up': out['v_ffn_w_up'], 'v_ffn_conv_w': out['v_ffn_conv_w'], 'v_ffn_conv_b': out['v_ffn_conv_b'], 'v_ffn_w_down': out['v_ffn_w_down'], 'v_ffn_norm_post': out['v_ffn_norm_post']}


def _loss(weights, diff, rest, loss_target):
    with _jax.named_scope("forward"):
        args = {**rest, TWIN_DIFF_INPUT: diff, **{k: w.astype(_WEIGHT_DTYPES[k]) for k, w in weights.items()}}
        y = _forward(args)
    with _jax.named_scope("loss_head"):
        err = _jnp.square(y.astype(_jnp.float32) - loss_target)
        return 0.5 * _jnp.sum(_jnp.mean(err, axis=-1)) if err.ndim else 0.5 * err


def _adamw(w, g, m, v):
    m = ADAM_B1 * m + (1.0 - ADAM_B1) * g
    v = ADAM_B2 * v + (1.0 - ADAM_B2) * _jnp.square(g)
    m_hat = m / (1.0 - ADAM_B1 ** ADAM_STEP)
    v_hat = v / (1.0 - ADAM_B2 ** ADAM_STEP)
    delta = -ADAM_LR * (m_hat / (_jnp.sqrt(v_hat) + ADAM_EPS) + ADAM_WD * w)
    return delta, m, v


def reference(x, mem, positions, mix_norm_pre, w_in, cv_w, cv_b, cv_ln_g, cv_ln_b, cv_pw_w, cv_pw_b, w_out, mix_norm_post, x_norm_pre, mem_norm, x_wq, x_wk, x_wv, x_wo, x_norm_post, ffn_norm_pre, ffn_w_up, ffn_conv_w, ffn_conv_b, ffn_w_down, ffn_norm_post, loss_target, m_mix_norm_pre, m_w_in, m_cv_w, m_cv_b, m_cv_ln_g, m_cv_ln_b, m_cv_pw_w, m_cv_pw_b, m_w_out, m_mix_norm_post, m_x_norm_pre, m_mem_norm, m_x_wq, m_x_wk, m_x_wv, m_x_wo, m_x_norm_post, m_ffn_norm_pre, m_ffn_w_up, m_ffn_conv_w, m_ffn_conv_b, m_ffn_w_down, m_ffn_norm_post, v_mix_norm_pre, v_w_in, v_cv_w, v_cv_b, v_cv_ln_g, v_cv_ln_b, v_cv_pw_w, v_cv_pw_b, v_w_out, v_mix_norm_post, v_x_norm_pre, v_mem_norm, v_x_wq, v_x_wk, v_x_wv, v_x_wo, v_x_norm_post, v_ffn_norm_pre, v_ffn_w_up, v_ffn_conv_w, v_ffn_conv_b, v_ffn_w_down, v_ffn_norm_post):
    given = dict(x=x, mem=mem, positions=positions, mix_norm_pre=mix_norm_pre, w_in=w_in, cv_w=cv_w, cv_b=cv_b, cv_ln_g=cv_ln_g, cv_ln_b=cv_ln_b, cv_pw_w=cv_pw_w, cv_pw_b=cv_pw_b, w_out=w_out, mix_norm_post=mix_norm_post, x_norm_pre=x_norm_pre, mem_norm=mem_norm, x_wq=x_wq, x_wk=x_wk, x_wv=x_wv, x_wo=x_wo, x_norm_post=x_norm_post, ffn_norm_pre=ffn_norm_pre, ffn_w_up=ffn_w_up, ffn_conv_w=ffn_conv_w, ffn_conv_b=ffn_conv_b, ffn_w_down=ffn_w_down, ffn_norm_post=ffn_norm_post, loss_target=loss_target, m_mix_norm_pre=m_mix_norm_pre, m_w_in=m_w_in, m_cv_w=m_cv_w, m_cv_b=m_cv_b, m_cv_ln_g=m_cv_ln_g, m_cv_ln_b=m_cv_ln_b, m_cv_pw_w=m_cv_pw_w, m_cv_pw_b=m_cv_pw_b, m_w_out=m_w_out, m_mix_norm_post=m_mix_norm_post, m_x_norm_pre=m_x_norm_pre, m_mem_norm=m_mem_norm, m_x_wq=m_x_wq, m_x_wk=m_x_wk, m_x_wv=m_x_wv, m_x_wo=m_x_wo, m_x_norm_post=m_x_norm_post, m_ffn_norm_pre=m_ffn_norm_pre, m_ffn_w_up=m_ffn_w_up, m_ffn_conv_w=m_ffn_conv_w, m_ffn_conv_b=m_ffn_conv_b, m_ffn_w_down=m_ffn_w_down, m_ffn_norm_post=m_ffn_norm_post, v_mix_norm_pre=v_mix_norm_pre, v_w_in=v_w_in, v_cv_w=v_cv_w, v_cv_b=v_cv_b, v_cv_ln_g=v_cv_ln_g, v_cv_ln_b=v_cv_ln_b, v_cv_pw_w=v_cv_pw_w, v_cv_pw_b=v_cv_pw_b, v_w_out=v_w_out, v_mix_norm_post=v_mix_norm_post, v_x_norm_pre=v_x_norm_pre, v_mem_norm=v_mem_norm, v_x_wq=v_x_wq, v_x_wk=v_x_wk, v_x_wv=v_x_wv, v_x_wo=v_x_wo, v_x_norm_post=v_x_norm_post, v_ffn_norm_pre=v_ffn_norm_pre, v_ffn_w_up=v_ffn_w_up, v_ffn_conv_w=v_ffn_conv_w, v_ffn_conv_b=v_ffn_conv_b, v_ffn_w_down=v_ffn_w_down, v_ffn_norm_post=v_ffn_norm_post)
    weights = {n: given[n] for n in TWIN_WEIGHTS}
    shared = {n: given[n] for n in SHARED_INPUTS}
    per_example = {n: given[n] for n in ['x', 'mem', 'positions']}
    grad_fn = _jax.value_and_grad(_loss, argnums=(0, 1))

    def one_microbatch(ex, loss_target):
        ex = dict(ex)
        diff = ex.pop(TWIN_DIFF_INPUT)
        return grad_fn(weights, diff, {**shared, **ex}, loss_target)

    if N_MICROBATCH == 1:
        loss, (grad_w, grad_x) = one_microbatch(per_example, given["loss_target"])
    else:
        def body(carry, xs):
            loss_sum, grad_sum = carry
            l_k, (gw_k, gx_k) = one_microbatch(xs[0], xs[1])
            with _jax.named_scope("update"):
                return (loss_sum + l_k, _jax.tree.map(_jnp.add, grad_sum, gw_k)), gx_k

        init = (_jnp.zeros((), _jnp.float32), _jax.tree.map(_jnp.zeros_like, weights))
        (loss, grad_w), grad_x = _jax.lax.scan(body, init, (per_example, given["loss_target"]))
    with _jax.named_scope("update"):
        delta_w, new_m, new_v = {}, {}, {}
        for n in TWIN_WEIGHTS:
            delta_w[n], new_m[n], new_v[n] = _adamw(weights[n], grad_w[n], given["m_" + n], given["v_" + n])
    return (loss, grad_x, *[grad_w[n] for n in TWIN_WEIGHTS], *[delta_w[n] for n in TWIN_WEIGHTS],
            *[new_m[n] for n in TWIN_WEIGHTS], *[new_v[n] for n in TWIN_WEIGHTS])
```

```python
import functools

import jax
import jax.numpy as jnp
import numpy as np
from jax import lax
from jax.experimental import pallas as pl
from jax.experimental.pallas import tpu as pltpu

F32, BF16 = jnp.float32, jnp.bfloat16
MESH = pl.DeviceIdType.MESH
EPS = 1e-6
LANES = 128
BLK = 128
HD = 64
D_MODEL = 1024
D_FF = 2816
SB_W, CV_W, DL_W = 256, 256, 512
CV_K = 31
ROPE_THETA = 10000.0
DILATIONS = (1, 4, 16)
X_HEADS, X_HD = 4, 256
ADAM_LR, ADAM_B1, ADAM_B2, ADAM_EPS, ADAM_WD, ADAM_STEP = 0.001, 0.9, 0.999, 1e-08, 0.01, 10
NEG_INF = float("-inf")
MIB = 1 << 20

PACK_ROWS = (("w_in", 704), ("w_out", 256), ("x_wq", 256), ("x_wk", 256), ("x_wv", 256), ("x_wo", 256),
             ("ffn_w_up", 1408), ("ffn_w_down", 704))
PACK_RL = sum(r for _, r in PACK_ROWS)


def _cp(vmem_mb=48):
    return pltpu.CompilerParams(vmem_limit_bytes=vmem_mb * MIB)


def _dot(a, b):
    return jnp.dot(a, b, preferred_element_type=F32)


def _dot_nt(a, b):
    return lax.dot_general(a, b, (((1,), (1,)), ((), ())), preferred_element_type=F32)


def _dot_tn(a, b):
    return lax.dot_general(a, b, (((0,), (0,)), ((), ())), preferred_element_type=F32)


def _dot_hilo(x, m):
    hi = x.astype(BF16)
    lo = (x - hi.astype(F32)).astype(BF16)
    return _dot(hi, m) + _dot(lo, m)


def _rowsum8(x):
    t, c = x.shape
    return x.reshape(t // 8, 8, c).sum(axis=0)


def _acc_out(ref, i, val):
    @pl.when(i == 0)
    def _():
        ref[...] = val

    @pl.when(i > 0)
    def _():
        ref[...] += val


def _tile(n, cap, mult=8):
    t = min(n, cap)
    while n % t or t % mult:
        t -= 1
    return t


def _rms_mm(x, g, w, *, tm, tn, out_dtype, name):
    m, d = x.shape
    n_out = w.shape[1]

    def body(x_ref, g_ref, w_ref, n_ref, o_ref):
        @pl.when(pl.program_id(1) == 0)
        def _():
            xv = x_ref[...]
            r = lax.rsqrt(jnp.mean(xv * xv, axis=-1, keepdims=True) + EPS)
            n_ref[...] = (xv * r * g_ref[...]).astype(BF16)

        o_ref[...] = _dot(n_ref[...], w_ref[...]).astype(out_dtype)

    return pl.pallas_call(
        body, grid=(m // tm, n_out // tn), name=name,
        in_specs=[pl.BlockSpec((tm, d), lambda i, j: (i, 0)), pl.BlockSpec((1, d), lambda i, j: (0, 0)),
                  pl.BlockSpec((d, tn), lambda i, j: (0, j))],
        out_specs=[pl.BlockSpec((tm, d), lambda i, j: (i, 0)), pl.BlockSpec((tm, tn), lambda i, j: (i, j))],
        out_shape=[jax.ShapeDtypeStruct((m, d), BF16), jax.ShapeDtypeStruct((m, n_out), out_dtype)],
        compiler_params=_cp())(x, g, w)


def _mm_post(a, w, h, g, *, tm, name):
    m, k = a.shape
    d = w.shape[1]

    def body(a_ref, w_ref, h_ref, g_ref, y_ref, ho_ref):
        y = _dot(a_ref[...], w_ref[...])
        y_ref[...] = y
        r = lax.rsqrt(jnp.mean(y * y, axis=-1, keepdims=True) + EPS)
        ho_ref[...] = h_ref[...] + y * r * g_ref[...]

    return pl.pallas_call(
        body, grid=(m // tm,), name=name,
        in_specs=[pl.BlockSpec((tm, k), lambda i: (i, 0)), pl.BlockSpec((k, d), lambda i: (0, 0)),
                  pl.BlockSpec((tm, d), lambda i: (i, 0)), pl.BlockSpec((1, d), lambda i: (0, 0))],
        out_specs=[pl.BlockSpec((tm, d), lambda i: (i, 0)), pl.BlockSpec((tm, d), lambda i: (i, 0))],
        out_shape=[jax.ShapeDtypeStruct((m, d), F32), jax.ShapeDtypeStruct((m, d), F32)],
        compiler_params=_cp())(a, w, h, g)


def _mm_nt(a, w, *, tm, tn, out_dtype, name):
    m, k = a.shape
    n_out = w.shape[0]

    def body(a_ref, w_ref, o_ref):
        o_ref[...] = _dot_nt(a_ref[...], w_ref[...]).astype(out_dtype)

    return pl.pallas_call(
        body, grid=(m // tm, n_out // tn), name=name,
        in_specs=[pl.BlockSpec((tm, k), lambda i, j: (i, 0)), pl.BlockSpec((tn, k), lambda i, j: (j, 0))],
        out_specs=pl.BlockSpec((tm, tn), lambda i, j: (i, j)),
        out_shape=jax.ShapeDtypeStruct((m, n_out), out_dtype),
        compiler_params=_cp())(a, w)


def _mm_tn(x, dy, *, tk, tn, tm, name):
    m, k = x.shape
    n_out = dy.shape[1]

    def body(x_ref, d_ref, o_ref):
        _acc_out(o_ref, pl.program_id(2), _dot_tn(x_ref[...], d_ref[...]))

    return pl.pallas_call(
        body, grid=(k // tk, n_out // tn, m // tm), name=name,
        in_specs=[pl.BlockSpec((tm, tk), lambda a, b, c: (c, a)), pl.BlockSpec((tm, tn), lambda a, b, c: (c, b))],
        out_specs=pl.BlockSpec((tk, tn), lambda a, b, c: (a, b)),
        out_shape=jax.ShapeDtypeStruct((k, n_out), F32),
        compiler_params=_cp())(x, dy)


def _rms_bwd(x, g, dout, res, *, out_dtype, tm, name):
    m, d = x.shape
    has_res = res is not None

    def body(*refs):
        if has_res:
            x_ref, g_ref, d_ref, r_ref, dx_ref, dg_ref = refs
        else:
            x_ref, g_ref, d_ref, dx_ref, dg_ref = refs
        xv = x_ref[...]
        dv = d_ref[...].astype(F32)
        r = lax.rsqrt(jnp.mean(xv * xv, axis=-1, keepdims=True) + EPS)
        xh = xv * r
        dxh = dv * g_ref[...]
        dx = r * (dxh - xh * jnp.mean(dxh * xh, axis=-1, keepdims=True))
        if has_res:
            dx = dx + r_ref[...]
        dx_ref[...] = dx.astype(out_dtype)
        _acc_out(dg_ref, pl.program_id(0), _rowsum8(dv * xh))

    row = pl.BlockSpec((tm, d), lambda i: (i, 0))
    ins = [row, pl.BlockSpec((1, d), lambda i: (0, 0)), row] + ([row] if has_res else [])
    args = (x, g, dout) + ((res,) if has_res else ())
    return pl.pallas_call(
        body, grid=(m // tm,), name=name, in_specs=ins,
        out_specs=[row, pl.BlockSpec((8, d), lambda i: (0, 0))],
        out_shape=[jax.ShapeDtypeStruct((m, d), out_dtype), jax.ShapeDtypeStruct((8, d), F32)],
        compiler_params=_cp())(*args)


def _loss_grad(h, tgt, *, tm, name):
    m, d = h.shape

    def body(h_ref, t_ref, dh_ref, p_ref):
        e = h_ref[...] - t_ref[...]
        dh_ref[...] = e / d
        _acc_out(p_ref, pl.program_id(0), _rowsum8(e * e))

    row = pl.BlockSpec((tm, d), lambda i: (i, 0))
    return pl.pallas_call(
        body, grid=(m // tm,), name=name, in_specs=[row, row],
        out_specs=[row, pl.BlockSpec((8, d), lambda i: (0, 0))],
        out_shape=[jax.ShapeDtypeStruct((m, d), F32), jax.ShapeDtypeStruct((8, d), F32)],
        compiler_params=_cp())(h, tgt)


def _adamw(w, g, m, v, *, name):
    r, c = w.shape
    tr = _tile(r, 256)

    def body(w_ref, g_ref, m_ref, v_ref, d_ref, mo_ref, vo_ref):
        gv = g_ref[...]
        m2 = ADAM_B1 * m_ref[...] + (1.0 - ADAM_B1) * gv
        v2 = ADAM_B2 * v_ref[...] + (1.0 - ADAM_B2) * jnp.square(gv)
        m_hat = m2 / (1.0 - ADAM_B1 ** ADAM_STEP)
        v_hat = v2 / (1.0 - ADAM_B2 ** ADAM_STEP)
        d_ref[...] = -ADAM_LR * (m_hat / (jnp.sqrt(v_hat) + ADAM_EPS) + ADAM_WD * w_ref[...])
        mo_ref[...] = m2
        vo_ref[...] = v2

    blk = pl.BlockSpec((tr, c), lambda i: (i, 0))
    return pl.pallas_call(
        body, grid=(r // tr,), name=name, in_specs=[blk] * 4, out_specs=[blk] * 3,
        out_shape=[jax.ShapeDtypeStruct((r, c), F32)] * 3, compiler_params=_cp())(w, g, m, v)


def _head_masks():
    lane = lax.broadcasted_iota(jnp.int32, (BLK, LANES), 1)
    row = lax.broadcasted_iota(jnp.int32, (BLK, LANES), 0)
    return lane, row, lane < HD


def _sb_scores(q_a, k, before):
    z = _dot_nt(q_a, k)
    sp = jnp.log1p(jnp.exp(-jnp.abs(z)))
    ls_pos = jnp.minimum(z, 0.0) - sp
    lkeep = jnp.where(before, ls_pos - z, 0.0)
    return ls_pos, lkeep


def _sb_fwd(u, *, name):
    s_len = u.shape[0]
    nq = s_len // BLK

    def body(q_ref, k_ref, v_ref, o_ref):
        i = pl.program_id(1)
        lane, row, lane_h = _head_masks()
        suffix = (row > lane).astype(BF16)
        q = q_ref[...] * 0.125
        q_heads = (jnp.where(lane_h, q, 0.0).astype(BF16), jnp.where(lane_h, 0.0, q).astype(BF16))

        def step(jj, carry):
            off = pl.multiple_of((i - jj) * BLK, BLK)
            k = k_ref[pl.ds(off, BLK), :].astype(BF16)
            v = v_ref[pl.ds(off, BLK), :].astype(BF16)
            before = jnp.logical_or(jj > 0, lane < row)
            out = []
            for a in range(2):
                cc, acc = carry[2 * a], carry[2 * a + 1]
                ls_pos, lkeep = _sb_scores(q_heads[a], k, before)
                between = _dot_hilo(lkeep, suffix) + cc
                att = jnp.where(before, jnp.exp(ls_pos + between), 0.0)
                out += [cc + jnp.sum(lkeep, axis=1, keepdims=True), acc + _dot(att.astype(BF16), v)]
            return tuple(out)

        zc, za = jnp.zeros((BLK, 1), F32), jnp.zeros((BLK, LANES), F32)
        res = lax.fori_loop(0, i + 1, step, (zc, za, zc, za))
        o_ref[...] = jnp.where(lane_h, res[1], res[3]).astype(BF16)

    return pl.pallas_call(
        body, grid=(2, nq), name=name,
        in_specs=[pl.BlockSpec((BLK, LANES), lambda hp, i: (i, hp)),
                  pl.BlockSpec((s_len, LANES), lambda hp, i: (0, 2 + hp)),
                  pl.BlockSpec((s_len, LANES), lambda hp, i: (0, 4 + hp))],
        out_specs=pl.BlockSpec((BLK, LANES), lambda hp, i: (i, hp)),
        out_shape=jax.ShapeDtypeStruct((s_len, SB_W), BF16), compiler_params=_cp())(u, u, u)


def _sb_bwd(u, dcat, *, name):
    s_len = u.shape[0]
    nq = s_len // BLK

    def body(q_ref, k_ref, v_ref, do_ref, dq_ref, dk_ref, dv_ref, g_scr, b_scr):
        i = pl.program_id(1)
        lane, row, lane_h = _head_masks()
        suffix = (row > lane).astype(BF16)
        prefix = (row < lane).astype(BF16)
        qf = q_ref[...]
        q = qf * 0.125
        q_heads = (jnp.where(lane_h, q, 0.0).astype(BF16), jnp.where(lane_h, 0.0, q).astype(BF16))
        qb = qf.astype(BF16)
        dof = do_ref[...]
        do_heads = (jnp.where(lane_h, dof, 0.0).astype(BF16), jnp.where(lane_h, 0.0, dof).astype(BF16))
        dob = dof.astype(BF16)

        @pl.when(i == 0)
        def _():
            dk_ref[...] = jnp.zeros_like(dk_ref)
            dv_ref[...] = jnp.zeros_like(dv_ref)

        def down(jj, carry):
            j = i - jj
            off = pl.multiple_of(j * BLK, BLK)
            k = k_ref[pl.ds(off, BLK), :].astype(BF16)
            v = v_ref[pl.ds(off, BLK), :].astype(BF16)
            before = jnp.logical_or(jj > 0, lane < row)
            out, dvs = [], []
            for a in range(2):
                ls_pos, lkeep = _sb_scores(q_heads[a], k, before)
                between = _dot_hilo(lkeep, suffix) + carry[a]
                att = jnp.where(before, jnp.exp(ls_pos + between), 0.0)
                g_scr[a * nq + j] = att * _dot_nt(do_heads[a], v)
                b_scr[a * nq + j] = jnp.exp(ls_pos)
                dvs.append(_dot_tn(att.astype(BF16), dob))
                out.append(carry[a] + jnp.sum(lkeep, axis=1, keepdims=True))
            dv_ref[pl.ds(off, BLK), :] += jnp.where(lane_h, dvs[0], dvs[1])
            return tuple(out)

        zc = jnp.zeros((BLK, 1), F32)
        lax.fori_loop(0, i + 1, down, (zc, zc))

        def up(j, carry):
            off = pl.multiple_of(j * BLK, BLK)
            k = k_ref[pl.ds(off, BLK), :].astype(BF16)
            before = jnp.logical_or(j < i, lane < row)
            out, dks = [], []
            for a in range(2):
                pc, dq = carry[2 * a], carry[2 * a + 1]
                g = g_scr[a * nq + j]
                beta = b_scr[a * nq + j]
                below = _dot_hilo(g, prefix) + pc
                dz = (jnp.where(before, g * (1.0 - beta) - beta * below, 0.0) * 0.125).astype(BF16)
                dks.append(_dot_tn(dz, qb))
                out += [pc + jnp.sum(g, axis=1, keepdims=True), dq + _dot(dz, k)]
            dk_ref[pl.ds(off, BLK), :] += jnp.where(lane_h, dks[0], dks[1])
            return tuple(out)

        za = jnp.zeros((BLK, LANES), F32)
        res = lax.fori_loop(0, i + 1, up, (zc, za, zc, za))
        dq_ref[...] = jnp.where(lane_h, res[1], res[3])

    col = lambda c0: pl.BlockSpec((s_len, LANES), lambda hp, i: (0, c0 + hp))
    blk = pl.BlockSpec((BLK, LANES), lambda hp, i: (i, hp))
    acc = pl.BlockSpec((s_len, LANES), lambda hp, i: (0, hp))
    return pl.pallas_call(
        body, grid=(2, nq), name=name, in_specs=[blk, col(2), col(4), blk],
        out_specs=[blk, acc, acc], out_shape=[jax.ShapeDtypeStruct((s_len, SB_W), F32)] * 3,
        scratch_shapes=[pltpu.VMEM((2 * nq, BLK, LANES), F32), pltpu.VMEM((2 * nq, BLK, LANES), F32)],
        compiler_params=_cp())(u, u, u, dcat)


CV_T = 512
CV_H = 32


def _cv_specs(s_len):
    cur = lambda c: pl.BlockSpec((CV_T, CV_W), lambda i: (i, c))
    prev = lambda c: pl.BlockSpec((CV_H, CV_W), lambda i: (jnp.maximum(i * (CV_T // CV_H) - 1, 0), c))
    nxt = lambda c: pl.BlockSpec((CV_H, CV_W),
                                 lambda i: (jnp.minimum((i + 1) * (CV_T // CV_H), s_len // CV_H - 1), c))
    full = lambda r: pl.BlockSpec((r, CV_W), lambda i: (0, 0))
    return cur, prev, nxt, full


def _glu_into(gp_ref, val_ref, gate_ref, valp_ref, gatep_ref, i):
    gp_ref[0:CV_H, :] = jnp.where(i > 0, valp_ref[...] * jax.nn.sigmoid(gatep_ref[...]), 0.0)
    gp_ref[CV_H:, :] = val_ref[...] * jax.nn.sigmoid(gate_ref[...])


def _cv_fwd(u, cv_w, cv_b, ln_g, ln_b, pw_w, pw_b, *, name):
    s_len = u.shape[0]
    cur, prev, _, full = _cv_specs(s_len)

    def body(val_ref, gate_ref, valp_ref, gatep_ref, w_ref, b_ref, g_ref, be_ref, pw_ref, pb_ref,
             o_ref, c_ref, gp_ref):
        _glu_into(gp_ref, val_ref, gate_ref, valp_ref, gatep_ref, pl.program_id(0))
        acc = jnp.zeros((CV_T, CV_W), F32) + b_ref[...]
        for k in range(CV_K):
            acc = acc + w_ref[k:k + 1, :] * gp_ref[pl.ds(CV_H - CV_K + 1 + k, CV_T), :]
        c_ref[...] = acc
        mu = jnp.mean(acc, axis=-1, keepdims=True)
        xc = acc - mu
        xh = xc * lax.rsqrt(jnp.mean(xc * xc, axis=-1, keepdims=True) + EPS)
        a = xh * g_ref[...] + be_ref[...]
        s = a * jax.nn.sigmoid(a)
        o_ref[...] = (_dot(s.astype(BF16), pw_ref[...]) + pb_ref[...]).astype(BF16)

    return pl.pallas_call(
        body, grid=(s_len // CV_T,), name=name,
        in_specs=[cur(3), cur(4), prev(3), prev(4), full(CV_K), full(1), full(1), full(1), full(CV_W), full(1)],
        out_specs=[cur(0), cur(0)],
        out_shape=[jax.ShapeDtypeStruct((s_len, CV_W), BF16), jax.ShapeDtypeStruct((s_len, CV_W), F32)],
        scratch_shapes=[pltpu.VMEM((CV_T + CV_H, CV_W), F32)], compiler_params=_cp())(
            u, u, u, u, cv_w, cv_b, ln_g, ln_b, pw_w, pw_b)


def _cv_bwd_local(c, dcat, ln_g, ln_b, pw_w, *, name):
    s_len = c.shape[0]
    cur, _, _, full = _cv_specs(s_len)

    def body(c_ref, db_ref, g_ref, be_ref, pw_ref, dc_ref, dpw_ref, vec_ref):
        i = pl.program_id(0)
        cv = c_ref[...]
        db = db_ref[...]
        mu = jnp.mean(cv, axis=-1, keepdims=True)
        xc = cv - mu
        rstd = lax.rsqrt(jnp.mean(xc * xc, axis=-1, keepdims=True) + EPS)
        xh = xc * rstd
        a = xh * g_ref[...] + be_ref[...]
        sg = jax.nn.sigmoid(a)
        s = a * sg
        dbb = db.astype(BF16)
        ds = _dot_nt(dbb, pw_ref[...])
        da = ds * (sg * (1.0 + a * (1.0 - sg)))
        dxh = da * g_ref[...]
        dc_ref[...] = rstd * (dxh - jnp.mean(dxh, axis=-1, keepdims=True)
                              - xh * jnp.mean(dxh * xh, axis=-1, keepdims=True))
        _acc_out(dpw_ref, i, _dot_tn(s.astype(BF16), dbb))
        _acc_out(vec_ref, i, jnp.concatenate([_rowsum8(db), _rowsum8(da * xh), _rowsum8(da)], axis=0))

    return pl.pallas_call(
        body, grid=(s_len // CV_T,), name=name,
        in_specs=[cur(0), cur(1), full(1), full(1), full(CV_W)],
        out_specs=[cur(0), full(CV_W), full(24)],
        out_shape=[jax.ShapeDtypeStruct((s_len, CV_W), F32), jax.ShapeDtypeStruct((CV_W, CV_W), F32),
                   jax.ShapeDtypeStruct((24, CV_W), F32)], compiler_params=_cp())(c, dcat, ln_g, ln_b, pw_w)


def _cv_bwd_conv(u, dc, cv_w, *, name):
    s_len = u.shape[0]
    cur, prev, nxt, full = _cv_specs(s_len)
    last = s_len // CV_T - 1

    def body(val_ref, gate_ref, valp_ref, gatep_ref, dc_ref, dcn_ref, w_ref, du_ref, dw_ref, dbias_ref,
             gp_ref, dcp_ref):
        i = pl.program_id(0)
        _glu_into(gp_ref, val_ref, gate_ref, valp_ref, gatep_ref, i)
        dcv = dc_ref[...]
        dcp_ref[0:CV_T, :] = dcv
        dcp_ref[CV_T:, :] = jnp.where(i < last, dcn_ref[...], 0.0)
        dg = jnp.zeros((CV_T, CV_W), F32)
        parts = []
        for k in range(CV_K):
            dg = dg + w_ref[k:k + 1, :] * dcp_ref[pl.ds(CV_K - 1 - k, CV_T), :]
            parts.append(_rowsum8(dcv * gp_ref[pl.ds(CV_H - CV_K + 1 + k, CV_T), :]))
        _acc_out(dw_ref, i, jnp.concatenate(parts, axis=0))
        _acc_out(dbias_ref, i, _rowsum8(dcv))
        val = val_ref[...]
        sg = jax.nn.sigmoid(gate_ref[...])
        du_ref[:, 0:CV_W] = (dg * sg).astype(BF16)
        du_ref[:, CV_W:] = (dg * val * sg * (1.0 - sg)).astype(BF16)

    return pl.pallas_call(
        body, grid=(s_len // CV_T,), name=name,
        in_specs=[cur(3), cur(4), prev(3), prev(4), cur(0), nxt(0), full(CV_K)],
        out_specs=[pl.BlockSpec((CV_T, 2 * CV_W), lambda i: (i, 0)), full(CV_K * 8), full(8)],
        out_shape=[jax.ShapeDtypeStruct((s_len, 2 * CV_W), BF16), jax.ShapeDtypeStruct((CV_K * 8, CV_W), F32),
                   jax.ShapeDtypeStruct((8, CV_W), F32)],
        scratch_shapes=[pltpu.VMEM((CV_T + CV_H, CV_W), F32), pltpu.VMEM((CV_T + CV_H, CV_W), F32)],
        compiler_params=_cp())(u, u, u, u, dc, dc, cv_w)


def _rope_tables(pos_col, inv_freq_row, *, name):
    s_len = pos_col.shape[0]

    def body(p_ref, f_ref, cos_ref, sin_ref):
        ang = p_ref[...].astype(F32) * f_ref[...]
        lane = lax.broadcasted_iota(jnp.int32, (s_len, LANES), 1)
        sn = jnp.sin(ang)
        cos_ref[...] = jnp.cos(ang)
        sin_ref[...] = jnp.where(lane % HD < HD // 2, -sn, sn)

    return pl.pallas_call(body, name=name, out_shape=[jax.ShapeDtypeStruct((s_len, LANES), F32)] * 2,
                          compiler_params=_cp())(pos_col, inv_freq_row)


def _rot_half(x):
    lane = lax.broadcasted_iota(jnp.int32, x.shape, 1)
    return jnp.where(lane % HD < HD // 2, pltpu.roll(x, LANES - HD // 2, 1), pltpu.roll(x, HD // 2, 1))


def _permute_rows(dst_ref, src_ref, d, dtype):
    s_len = src_ref.shape[0]
    seg = s_len // d
    if d == 1:
        dst_ref[...] = src_ref[...].astype(dtype)
        return
    for r in range(d):
        dst_ref[r * seg:(r + 1) * seg, :] = src_ref[pl.ds(r, seg, stride=d), :].astype(dtype)


def _unpermute_rows(dst_ref, src_ref, d):
    s_len = src_ref.shape[0]
    seg = s_len // d
    if d == 1:
        dst_ref[...] = src_ref[...]
        return
    for r in range(d):
        dst_ref[pl.ds(r, seg, stride=d), :] = src_ref[r * seg:(r + 1) * seg, :]


def _rope_perm(u, cos, sin, *, name):
    s_len = u.shape[0]

    def body(x_ref, cos_ref, sin_ref, o_ref, scr):
        a = pl.program_id(0)
        x = x_ref[...]
        rot = a < 2
        scr[...] = x * jnp.where(rot, cos_ref[...], 1.0) + _rot_half(x) * jnp.where(rot, sin_ref[...], 0.0)
        for n, d in enumerate(DILATIONS):
            _permute_rows(o_ref.at[n], scr, d, BF16)

    tab = pl.BlockSpec((s_len, LANES), lambda a, cb: (0, 0))
    return pl.pallas_call(
        body, grid=(3, 4), name=name,
        in_specs=[pl.BlockSpec((s_len, LANES), lambda a, cb: (0, 10 + 4 * a + cb)), tab, tab],
        out_specs=pl.BlockSpec((None, 3, s_len, LANES), lambda a, cb: (a, 0, 0, cb)),
        out_shape=jax.ShapeDtypeStruct((3, 3, s_len, DL_W), BF16),
        scratch_shapes=[pltpu.VMEM((s_len, LANES), F32)], compiler_params=_cp())(u, cos, sin)


def _dl_masks(s_len, n, i):
    lane, row, lane_h = _head_masks()
    nb = jnp.where(n == 0, s_len // BLK, jnp.where(n == 1, s_len // (BLK * DILATIONS[1]),
                                                   s_len // (BLK * DILATIONS[2])))
    first = lax.rem(i, nb) == 0
    return lane_h, lane <= row, jnp.logical_and(lane >= row, jnp.logical_not(first))


def _dl_specs():
    blk = lambda a, back: pl.BlockSpec(
        (None, None, BLK, LANES), lambda n, hp, i: (a, n, jnp.maximum(i - back, 0), hp))
    return blk(0, 0), blk(1, 0), blk(1, 1), blk(2, 0), blk(2, 1)


def _dl_fwd(qkv, *, name):
    s_len = qkv.shape[2]

    def body(q_ref, kc_ref, kp_ref, vc_ref, vp_ref, o_ref, l_ref):
        lane_h, valid_c, valid_p = _dl_masks(s_len, pl.program_id(0), pl.program_id(2))
        q = q_ref[...] * 0.125
        kc, kp, vc, vp = kc_ref[...], kp_ref[...], vc_ref[...], vp_ref[...]
        outs, lses = [], []
        for a in range(2):
            qa = jnp.where(lane_h == (a == 0), q, 0.0).astype(BF16)
            sc = jnp.where(valid_c, _dot_nt(qa, kc), NEG_INF)
            sp = jnp.where(valid_p, _dot_nt(qa, kp), NEG_INF)
            m = jnp.maximum(jnp.max(sc, axis=1, keepdims=True), jnp.max(sp, axis=1, keepdims=True))
            pc, pp = jnp.exp(sc - m), jnp.exp(sp - m)
            den = jnp.sum(pc, axis=1, keepdims=True) + jnp.sum(pp, axis=1, keepdims=True)
            outs.append(_dot((pc / den).astype(BF16), vc) + _dot((pp / den).astype(BF16), vp))
            lses.append(jnp.broadcast_to(m + jnp.log(den), (BLK, LANES)))
        o_ref[...] = jnp.where(lane_h, outs[0], outs[1])
        l_ref[...] = jnp.where(lane_h, lses[0], lses[1])

    out = pl.BlockSpec((None, BLK, LANES), lambda n, hp, i: (n, i, hp))
    return pl.pallas_call(
        body, grid=(3, 4, s_len // BLK), name=name, in_specs=list(_dl_specs()), out_specs=[out, out],
        out_shape=[jax.ShapeDtypeStruct((3, s_len, DL_W), F32)] * 2, compiler_params=_cp())(qkv, qkv, qkv, qkv, qkv)


def _dl_mix(o_p, l_p, *, name):
    s_len = o_p.shape[1]

    def body(o_ref, l_ref, ob_ref, of_ref, lt_ref, o_scr, l_scr):
        n = pl.program_id(1)
        for k, d in enumerate(DILATIONS):
            @pl.when(n == k)
            def _(k=k, d=d):
                _unpermute_rows(o_scr.at[k], o_ref, d)
                _unpermute_rows(l_scr.at[k], l_ref, d)

        @pl.when(n == 2)
        def _():
            l0, l1, l2 = l_scr[0], l_scr[1], l_scr[2]
            m = jnp.maximum(jnp.maximum(l0, l1), l2)
            e0, e1, e2 = jnp.exp(l0 - m), jnp.exp(l1 - m), jnp.exp(l2 - m)
            den = e0 + e1 + e2
            o = (e0 / den) * o_scr[0] + (e1 / den) * o_scr[1] + (e2 / den) * o_scr[2]
            of_ref[...] = o
            ob_ref[...] = o.astype(BF16)
            lt_ref[...] = m + jnp.log(den)

    inb = pl.BlockSpec((None, s_len, LANES), lambda cb, n: (n, 0, cb))
    outb = pl.BlockSpec((s_len, LANES), lambda cb, n: (0, cb))
    return pl.pallas_call(
        body, grid=(4, 3), name=name, in_specs=[inb, inb], out_specs=[outb, outb, outb],
        out_shape=[jax.ShapeDtypeStruct((s_len, DL_W), BF16), jax.ShapeDtypeStruct((s_len, DL_W), F32),
                   jax.ShapeDtypeStruct((s_len, DL_W), F32)],
        scratch_shapes=[pltpu.VMEM((3, s_len, LANES), F32), pltpu.VMEM((3, s_len, LANES), F32)],
        compiler_params=_cp())(o_p, l_p)


def _dl_bwd_prep(dcat, o, lse, *, name):
    s_len = o.shape[0]

    def body(do_ref, o_ref, l_ref, dop_ref, st_ref, d_scr):
        n = pl.program_id(1)

        @pl.when(n == 0)
        def _():
            r0 = lax.broadcasted_iota(jnp.int32, (LANES, LANES), 0) // HD
            r1 = lax.broadcasted_iota(jnp.int32, (LANES, LANES), 1) // HD
            d_scr[...] = _dot_hilo(do_ref[...] * o_ref[...], (r0 == r1).astype(BF16))

        for k, d in enumerate(DILATIONS):
            @pl.when(n == k)
            def _(d=d):
                _permute_rows(dop_ref, do_ref, d, BF16)
                _permute_rows(st_ref.at[0], d_scr, d, F32)
                _permute_rows(st_ref.at[1], l_ref, d, F32)

    nat = lambda c0: pl.BlockSpec((s_len, LANES), lambda cb, n: (0, c0 + cb))
    return pl.pallas_call(
        body, grid=(4, 3), name=name, in_specs=[nat(4), nat(0), nat(0)],
        out_specs=[pl.BlockSpec((None, s_len, LANES), lambda cb, n: (n, 0, cb)),
                   pl.BlockSpec((2, None, s_len, LANES), lambda cb, n: (0, n, 0, cb))],
        out_shape=[jax.ShapeDtypeStruct((3, s_len, DL_W), BF16), jax.ShapeDtypeStruct((2, 3, s_len, DL_W), F32)],
        scratch_shapes=[pltpu.VMEM((s_len, LANES), F32)], compiler_params=_cp())(dcat, o, lse)


def _dl_bwd(qkv, dop, stats, *, name):
    s_len = qkv.shape[2]

    def body(q_ref, kc_ref, kp_ref, vc_ref, vp_ref, do_ref, st_ref, cur_ref, prev_ref):
        lane_h, valid_c, valid_p = _dl_masks(s_len, pl.program_id(0), pl.program_id(2))
        qb = q_ref[...]
        qs = qb * 0.125
        kc, kp, vc, vp, dob = kc_ref[...], kp_ref[...], vc_ref[...], vp_ref[...], do_ref[...]
        delta, lse = st_ref[0], st_ref[1]
        delta_r, lse_r = pltpu.roll(delta, HD, 1), pltpu.roll(lse, HD, 1)
        res = []
        for a in range(2):
            mine = lane_h == (a == 0)
            qa = jnp.where(mine, qs, 0.0).astype(BF16)
            doa = jnp.where(mine, dob, 0.0).astype(BF16)
            lse_a, delta_a = jnp.where(mine, lse, lse_r), jnp.where(mine, delta, delta_r)
            pc = jnp.where(valid_c, jnp.exp(_dot_nt(qa, kc) - lse_a), 0.0)
            pp = jnp.where(valid_p, jnp.exp(_dot_nt(qa, kp) - lse_a), 0.0)
            dsc = (pc * (_dot_nt(doa, vc) - delta_a)).astype(BF16)
            dsp = (pp * (_dot_nt(doa, vp) - delta_a)).astype(BF16)
            res.append(((_dot(dsc, kc) + _dot(dsp, kp)) * 0.125, _dot_tn(dsc, qs), _dot_tn(pc.astype(BF16), dob),
                        _dot_tn(dsp, qs), _dot_tn(pp.astype(BF16), dob)))
        pick = lambda t: jnp.where(lane_h, res[0][t], res[1][t])
        cur_ref[0], cur_ref[1], cur_ref[2] = pick(0), pick(1), pick(2)
        prev_ref[0], prev_ref[1] = pick(3), pick(4)

    sel = lambda n, hp, i: (n, i, hp)
    return pl.pallas_call(
        body, grid=(3, 4, s_len // BLK), name=name,
        in_specs=list(_dl_specs()) + [pl.BlockSpec((None, BLK, LANES), sel),
                                      pl.BlockSpec((2, None, BLK, LANES), lambda n, hp, i: (0, n, i, hp))],
        out_specs=[pl.BlockSpec((3, None, BLK, LANES), lambda n, hp, i: (0, n, i, hp)),
                   pl.BlockSpec((2, None, BLK, LANES), lambda n, hp, i: (0, n, i, hp))],
        out_shape=[jax.ShapeDtypeStruct((3, 3, s_len, DL_W), F32), jax.ShapeDtypeStruct((2, 3, s_len, DL_W), F32)],
        compiler_params=_cp())(qkv, qkv, qkv, qkv, qkv, dop, stats)


def _dl_bwd_finish(cur, prev, cos, sin, *, name):
    s_len = cur.shape[2]

    def body(c_ref, p_ref, cos_ref, sin_ref, o_ref, p_scr, u_scr, acc):
        a, n = pl.program_id(0), pl.program_id(2)
        has_prev = jnp.where(a > 0, 1.0, 0.0)
        p_scr[...] = c_ref[...]
        p_scr[0:s_len - BLK, :] += has_prev * p_ref[BLK:, :]
        for k, d in enumerate(DILATIONS):
            @pl.when(n == k)
            def _(k=k, d=d):
                if k == 0:
                    acc[...] = p_scr[...]
                else:
                    _unpermute_rows(u_scr, p_scr, d)
                    acc[...] += u_scr[...]

        @pl.when(n == 2)
        def _():
            dy = acc[...]
            rot = a < 2
            o_ref[...] = (dy * jnp.where(rot, cos_ref[...], 1.0)
                          + _rot_half(dy * jnp.where(rot, sin_ref[...], 0.0))).astype(BF16)

    tab = pl.BlockSpec((s_len, LANES), lambda a, cb, n: (0, 0))
    return pl.pallas_call(
        body, grid=(3, 4, 3), name=name,
        in_specs=[pl.BlockSpec((None, None, s_len, LANES), lambda a, cb, n: (a, n, 0, cb)),
                  pl.BlockSpec((None, None, s_len, LANES), lambda a, cb, n: (jnp.maximum(a - 1, 0), n, 0, cb)),
                  tab, tab],
        out_specs=pl.BlockSpec((s_len, LANES), lambda a, cb, n: (0, 4 * a + cb)),
        out_shape=jax.ShapeDtypeStruct((s_len, 3 * DL_W), BF16),
        scratch_shapes=[pltpu.VMEM((s_len, LANES), F32)] * 3, compiler_params=_cp())(cur, prev, cos, sin)


XA_T = 256


def _xa_probs(q, k):
    s = _dot_nt(q, k) * (X_HD ** -0.5)
    e = jnp.exp(s - jnp.max(s, axis=1, keepdims=True))
    return e / jnp.sum(e, axis=1, keepdims=True)


def _xa_fwd(q, k, v, *, name):
    s_len, d = q.shape
    nm = k.shape[0]

    def body(q_ref, k_ref, v_ref, o_ref):
        for h in range(X_HEADS):
            cs = slice(h * X_HD, (h + 1) * X_HD)
            p = _xa_probs(q_ref[:, cs], k_ref[:, cs])
            o_ref[:, cs] = _dot(p.astype(BF16), v_ref[:, cs]).astype(BF16)

    row = pl.BlockSpec((XA_T, d), lambda i: (i, 0))
    full = pl.BlockSpec((nm, d), lambda i: (0, 0))
    return pl.pallas_call(body, grid=(s_len // XA_T,), name=name, in_specs=[row, full, full], out_specs=row,
                          out_shape=jax.ShapeDtypeStruct((s_len, d), BF16), compiler_params=_cp())(q, k, v)


def _xa_bwd(q, k, v, do, *, name):
    s_len, d = q.shape
    nm = k.shape[0]

    def body(q_ref, k_ref, v_ref, do_ref, dq_ref, dk_ref, dv_ref):
        i = pl.program_id(0)
        for h in range(X_HEADS):
            cs = slice(h * X_HD, (h + 1) * X_HD)
            qh, kh, vh, doh = q_ref[:, cs], k_ref[:, cs], v_ref[:, cs], do_ref[:, cs]
            p = _xa_probs(qh, kh)
            dp = _dot_nt(doh, vh)
            ds = (p * (dp - jnp.sum(dp * p, axis=1, keepdims=True)) * (X_HD ** -0.5)).astype(BF16)
            dq_ref[:, cs] = _dot(ds, kh).astype(BF16)
            dkh, dvh = _dot_tn(ds, qh), _dot_tn(p.astype(BF16), doh)

            @pl.when(i == 0)
            def _(cs=cs, dkh=dkh, dvh=dvh):
                dk_ref[:, cs] = dkh
                dv_ref[:, cs] = dvh

            @pl.when(i > 0)
            def _(cs=cs, dkh=dkh, dvh=dvh):
                dk_ref[:, cs] += dkh
                dv_ref[:, cs] += dvh

    row = pl.BlockSpec((XA_T, d), lambda i: (i, 0))
    full = pl.BlockSpec((nm, d), lambda i: (0, 0))
    return pl.pallas_call(
        body, grid=(s_len // XA_T,), name=name, in_specs=[row, full, full, row], out_specs=[row, full, full],
        out_shape=[jax.ShapeDtypeStruct((s_len, d), BF16), jax.ShapeDtypeStruct((nm, d), F32),
                   jax.ShapeDtypeStruct((nm, d), F32)], compiler_params=_cp())(q, k, v, do)


FF_TM, FF_TN, FF_H = 512, 256, 8
GELU_K, GELU_C = 0.7978845608028654, 0.044715


def _ff_conv(e_ref, w_ref, b_ref, rows):
    return (w_ref[0:1, :] * e_ref[pl.ds(FF_H - 2, rows), :] + w_ref[1:2, :] * e_ref[pl.ds(FF_H - 1, rows), :]
            + w_ref[2:3, :] * e_ref[pl.ds(FF_H, rows), :] + b_ref[...])


def _ff_gate_fwd(up, conv_w, conv_b, *, name):
    s_len = up.shape[0]
    nj = D_FF // FF_TN

    def body(g_ref, v_ref, gp_ref, vp_ref, wg_ref, wv_ref, bg_ref, bv_ref, o_ref, eg, ev):
        i = pl.program_id(0)
        for e, cur, prev in ((eg, g_ref, gp_ref), (ev, v_ref, vp_ref)):
            e[0:FF_H, :] = jnp.where(i > 0, prev[...], 0.0)
            e[FF_H:, :] = cur[...]
        gate = _ff_conv(eg, wg_ref, bg_ref, FF_TM)
        val = _ff_conv(ev, wv_ref, bv_ref, FF_TM)
        t = jnp.tanh(GELU_K * (gate + GELU_C * gate * gate * gate))
        o_ref[...] = (0.5 * gate * (1.0 + t) * val).astype(BF16)

    cur = lambda c0: pl.BlockSpec((FF_TM, FF_TN), lambda i, j: (i, c0 + j))
    prev = lambda c0: pl.BlockSpec((FF_H, FF_TN), lambda i, j: (jnp.maximum(i * (FF_TM // FF_H) - 1, 0), c0 + j))
    par = lambda r, c0: pl.BlockSpec((r, FF_TN), lambda i, j: (0, c0 + j))
    return pl.pallas_call(
        body, grid=(s_len // FF_TM, nj), name=name,
        in_specs=[cur(0), cur(nj), prev(0), prev(nj), par(3, 0), par(3, nj), par(1, 0), par(1, nj)],
        out_specs=cur(0), out_shape=jax.ShapeDtypeStruct((s_len, D_FF), BF16),
        scratch_shapes=[pltpu.VMEM((FF_TM + FF_H, FF_TN), F32)] * 2, compiler_params=_cp())(
            up, up, up, up, conv_w, conv_w, conv_b, conv_b)


def _ff_gate_bwd(up, dact, conv_w, conv_b, *, name):
    s_len = up.shape[0]
    nj = D_FF // FF_TN
    last = s_len // FF_TM - 1
    ext = FF_TM + FF_H

    def body(g_ref, v_ref, gp_ref, vp_ref, gn_ref, vn_ref, da_ref, dan_ref, wg_ref, wv_ref, bg_ref, bv_ref,
             dg_ref, dv_ref, dw_ref, db_ref, eg, ev, sg, sv):
        i = pl.program_id(1)
        for e, cur, prev, nxt in ((eg, g_ref, gp_ref, gn_ref), (ev, v_ref, vp_ref, vn_ref)):
            e[0:FF_H, :] = jnp.where(i > 0, prev[...], 0.0)
            e[FF_H:FF_H + FF_TM, :] = cur[...]
            e[FF_H + FF_TM:, :] = nxt[...]
        gate = _ff_conv(eg, wg_ref, bg_ref, ext)
        val = _ff_conv(ev, wv_ref, bv_ref, ext)
        dact_e = jnp.concatenate([da_ref[...], jnp.where(i < last, dan_ref[...], 0.0)], axis=0)
        inner = GELU_K * (gate + GELU_C * gate * gate * gate)
        t = jnp.tanh(inner)
        gelu = 0.5 * gate * (1.0 + t)
        dgelu = 0.5 * (1.0 + t) + 0.5 * gate * (1.0 - t * t) * GELU_K * (1.0 + 3.0 * GELU_C * gate * gate)
        sg[...] = dact_e * val * dgelu
        sv[...] = dact_e * gelu
        for part, (s, e, w_ref, out) in enumerate(((sg, eg, wg_ref, dg_ref), (sv, ev, wv_ref, dv_ref))):
            out[...] = (w_ref[2:3, :] * s[pl.ds(0, FF_TM), :] + w_ref[1:2, :] * s[pl.ds(1, FF_TM), :]
                        + w_ref[0:1, :] * s[pl.ds(2, FF_TM), :]).astype(BF16)
            d0 = s[pl.ds(0, FF_TM), :]
            taps = [_rowsum8(d0 * e[pl.ds(FF_H - 2 + k, FF_TM), :]) for k in range(3)]
            _acc_out(dw_ref.at[part], i, jnp.concatenate(taps, axis=0))
            _acc_out(db_ref.at[part], i, _rowsum8(d0))

    cur = lambda c0: pl.BlockSpec((FF_TM, FF_TN), lambda j, i: (i, c0 + j))
    prev = lambda c0: pl.BlockSpec((FF_H, FF_TN), lambda j, i: (jnp.maximum(i * (FF_TM // FF_H) - 1, 0), c0 + j))
    nxt = lambda c0: pl.BlockSpec(
        (FF_H, FF_TN), lambda j, i: (jnp.minimum((i + 1) * (FF_TM // FF_H), s_len // FF_H - 1), c0 + j))
    par = lambda r, c0: pl.BlockSpec((r, FF_TN), lambda j, i: (0, c0 + j))
    return pl.pallas_call(
        body, grid=(nj, s_len // FF_TM), name=name,
        in_specs=[cur(0), cur(nj), prev(0), prev(nj), nxt(0), nxt(nj), cur(0), nxt(0),
                  par(3, 0), par(3, nj), par(1, 0), par(1, nj)],
        out_specs=[cur(0), cur(0), pl.BlockSpec((2, 24, FF_TN), lambda j, i: (0, 0, j)),
                   pl.BlockSpec((2, 8, FF_TN), lambda j, i: (0, 0, j))],
        out_shape=[jax.ShapeDtypeStruct((s_len, D_FF), BF16), jax.ShapeDtypeStruct((s_len, D_FF), BF16),
                   jax.ShapeDtypeStruct((2, 24, D_FF), F32), jax.ShapeDtypeStruct((2, 8, D_FF), F32)],
        scratch_shapes=[pltpu.VMEM((FF_TM + 2 * FF_H, FF_TN), F32)] * 2 + [pltpu.VMEM((ext, FF_TN), F32)] * 2,
        compiler_params=_cp())(up, up, up, up, up, up, dact, dact, conv_w, conv_w, conv_b, conv_b)


def _place():
    x, y, c = lax.axis_index("x"), lax.axis_index("y"), lax.axis_index("c")
    return x, y, c, [(1 - x, y), (x, 1 - y), (1 - x, 1 - y)]


def _remote(src, dst, send_sem, recv_sem, dev):
    return pltpu.make_async_remote_copy(src_ref=src, dst_ref=dst, send_sem=send_sem, recv_sem=recv_sem,
                                        device_id=dev, device_id_type=MESH)


_ANY = pl.BlockSpec(memory_space=pl.ANY)


def _gather_weights(pack, *, name):
    def body(p_ref, out_ref, send_sems, recv_sems, local_sem):
        x, y, c, chips = _place()
        sibling = (x, y, 1 - c)
        mine = pltpu.make_async_copy(p_ref, out_ref.at[2 * x + y], local_sem)
        mine.start()
        first = [_remote(p_ref.at[c], out_ref.at[2 * x + y, c], send_sems.at[k], recv_sems.at[k], (px, py, c))
                 for k, (px, py) in enumerate(chips)]
        for cp in first:
            cp.start()
        passed = []
        for k, (px, py) in enumerate(chips):
            slab = out_ref.at[2 * px + py, c]
            _remote(slab, slab, send_sems.at[k], recv_sems.at[k], (px, py, c)).wait_recv()
            passed.append(_remote(slab, slab, send_sems.at[3 + k], recv_sems.at[3 + k], sibling))
            passed[-1].start()
        for k, (px, py) in enumerate(chips):
            slab = out_ref.at[2 * px + py, 1 - c]
            _remote(slab, slab, send_sems.at[3 + k], recv_sems.at[3 + k], sibling).wait_recv()
        for cp in first + passed:
            cp.wait_send()
        mine.wait()

    return pl.pallas_call(
        body, name=name, in_specs=[_ANY], out_specs=_ANY,
        out_shape=jax.ShapeDtypeStruct((4,) + pack.shape, pack.dtype),
        scratch_shapes=[pltpu.SemaphoreType.DMA((6,)), pltpu.SemaphoreType.DMA((6,)), pltpu.SemaphoreType.DMA],
        compiler_params=_cp(16))(pack)


def _swap_other_layer(gw, *, name):
    def body(g_ref, out_ref, send_sem, recv_sem):
        x, y, c, _ = _place()
        cp = _remote(g_ref.at[1 - c], out_ref, send_sem, recv_sem, (x, y, 1 - c))
        cp.start()
        cp.wait()

    return pl.pallas_call(
        body, name=name, in_specs=[_ANY], out_specs=_ANY, out_shape=jax.ShapeDtypeStruct(gw.shape[1:], gw.dtype),
        scratch_shapes=[pltpu.SemaphoreType.DMA, pltpu.SemaphoreType.DMA],
        compiler_params=_cp(16))(gw)


def _chip_sum(gw, got, c_arr, *, name):
    _, nchip, rl, d = gw.shape
    tr = 512

    def body(c_ref, a_ref, b_ref, o32_ref, o16_ref):
        s = a_ref[...] + b_ref[...]
        o32_ref[...] = s
        o16_ref[...] = s.astype(BF16)

    blk = pl.BlockSpec((None, tr, d), lambda j, i, c_ref: (j, i, 0))
    return pl.pallas_call(
        body, name=name,
        grid_spec=pltpu.PrefetchScalarGridSpec(
            num_scalar_prefetch=1, grid=(nchip, rl // tr),
            in_specs=[pl.BlockSpec((None, None, tr, d), lambda j, i, c_ref: (c_ref[0], j, i, 0)), blk],
            out_specs=[blk, blk]),
        out_shape=[jax.ShapeDtypeStruct((nchip, rl, d), F32), jax.ShapeDtypeStruct((nchip, rl, d), BF16)],
        compiler_params=_cp())(c_arr, gw, got)


def _scatter_chip_sums(s16, *, name):
    def body(s_ref, out_ref, send_sems, recv_sems):
        x, y, c, chips = _place()
        sends = [_remote(s_ref.at[2 * px + py], out_ref.at[k], send_sems.at[k], recv_sems.at[k], (px, py, c))
                 for k, (px, py) in enumerate(chips)]
        for cp in sends:
            cp.start()
        for cp in sends:
            cp.wait()

    return pl.pallas_call(
        body, name=name, in_specs=[_ANY], out_specs=_ANY,
        out_shape=jax.ShapeDtypeStruct((3,) + s16.shape[1:], s16.dtype),
        scratch_shapes=[pltpu.SemaphoreType.DMA((3,)), pltpu.SemaphoreType.DMA((3,))],
        compiler_params=_cp(16))(s16)


def _mesh_sum(s32, got, j_arr, *, name):
    _, rl, d = s32.shape
    tr = 512

    def body(j_ref, a_ref, b_ref, o_ref):
        o_ref[...] = ((a_ref[...] + b_ref[0].astype(F32)) + b_ref[1].astype(F32)) + b_ref[2].astype(F32)

    return pl.pallas_call(
        body, name=name,
        grid_spec=pltpu.PrefetchScalarGridSpec(
            num_scalar_prefetch=1, grid=(rl // tr,),
            in_specs=[pl.BlockSpec((None, tr, d), lambda i, j_ref: (j_ref[0], i, 0)),
                      pl.BlockSpec((3, tr, d), lambda i, j_ref: (0, i, 0))],
            out_specs=pl.BlockSpec((tr, d), lambda i, j_ref: (i, 0))),
        out_shape=jax.ShapeDtypeStruct((rl, d), F32), compiler_params=_cp())(j_arr, s32, got)


def _share_layers(ghalf, *, name):
    def body(g_ref, out_ref, send_sem, recv_sem, local_sem):
        x, y, c, _ = _place()
        mine = pltpu.make_async_copy(g_ref, out_ref.at[c], local_sem)
        mine.start()
        cp = _remote(g_ref, out_ref.at[c], send_sem, recv_sem, (x, y, 1 - c))
        cp.start()
        _remote(g_ref, out_ref.at[1 - c], send_sem, recv_sem, (x, y, 1 - c)).wait_recv()
        cp.wait_send()
        mine.wait()

    return pl.pallas_call(
        body, name=name, in_specs=[_ANY], out_specs=_ANY,
        out_shape=jax.ShapeDtypeStruct((2,) + ghalf.shape, ghalf.dtype),
        scratch_shapes=[pltpu.SemaphoreType.DMA, pltpu.SemaphoreType.DMA, pltpu.SemaphoreType.DMA],
        compiler_params=_cp(16))(ghalf)


def _all_reduce_small(vec, *, name):
    rows, d = vec.shape

    def body(x_ref, o_ref, gat, send_sems, recv_sems, local_sem):
        x, y, c, chips = _place()
        me, sibling = (x, y, c), (x, y, 1 - c)

        def slot(px, py, pc):
            return gat.at[4 * px + 2 * py + pc]

        def copy(k, block, to, src=None):
            return _remote(slot(*block) if src is None else src, slot(*block), send_sems.at[k], recv_sems.at[k], to)

        mine = pltpu.make_async_copy(x_ref, slot(*me), local_sem)
        mine.start()
        first = [copy(0, me, sibling, src=x_ref)]
        first += [copy(1 + j, me, (*chip, c), src=x_ref) for j, chip in enumerate(chips)]
        for cp in first:
            cp.start()
        passed = [copy(4 + j, (*chip, c), sibling) for j, chip in enumerate(chips)]
        for j, chip in enumerate(chips):
            copy(1 + j, (*chip, c), me).wait_recv()
            passed[j].start()
        copy(0, sibling, me).wait_recv()
        for j, chip in enumerate(chips):
            copy(4 + j, (*chip, 1 - c), me).wait_recv()
        for cp in first + passed:
            cp.wait_send()
        mine.wait()
        acc = gat[0]
        for dev in range(1, 8):
            acc = acc + gat[dev]
        o_ref[...] = acc

    vm = pl.BlockSpec(memory_space=pltpu.VMEM)
    return pl.pallas_call(
        body, name=name, in_specs=[vm], out_specs=vm, out_shape=jax.ShapeDtypeStruct((rows, d), F32),
        scratch_shapes=[pltpu.VMEM((8, rows, d), F32), pltpu.SemaphoreType.DMA((7,)), pltpu.SemaphoreType.DMA((7,)),
                        pltpu.SemaphoreType.DMA],
        compiler_params=_cp(32))(vec)


COL_SHARDED = ("w_in", "ffn_w_up")


def _to_pack_rows(name, shard):
    return shard.reshape(-1, D_MODEL)


def _full_from_blocks(name, blocks):
    rows = blocks.shape[1]
    if name in COL_SHARDED:
        return blocks.reshape(4, D_MODEL, rows).transpose(1, 0, 2).reshape(D_MODEL, 4 * rows)
    return blocks.reshape(4 * rows, D_MODEL)


def _blocks_from_full(name, full):
    if name in COL_SHARDED:
        cols = full.shape[1] // 4
        return full.reshape(D_MODEL, 4, cols).transpose(1, 0, 2).reshape(4, cols, D_MODEL)
    return full.reshape(4, full.shape[0] // 4, D_MODEL)


def _row(v):
    return v.reshape(1, -1)


SMALL = (("mix_norm_pre", (1024,), None), ("cv_w", (31, 256), 1), ("cv_b", (256,), None), ("cv_ln_g", (256,), None),
         ("cv_ln_b", (256,), None), ("cv_pw_w", (256, 256), 0), ("cv_pw_b", (256,), None),
         ("mix_norm_post", (1024,), None), ("x_norm_pre", (1024,), None), ("mem_norm", (1024,), None),
         ("x_norm_post", (1024,), None), ("ffn_norm_pre", (1024,), None), ("ffn_conv_w", (3, 5632), 1),
         ("ffn_conv_b", (5632,), None), ("ffn_norm_post", (1024,), None))
BIG = tuple(n for n, _ in PACK_ROWS)
WEIGHT_ORDER = ("mix_norm_pre", "w_in", "cv_w", "cv_b", "cv_ln_g", "cv_ln_b", "cv_pw_w", "cv_pw_b", "w_out",
                "mix_norm_post", "x_norm_pre", "mem_norm", "x_wq", "x_wk", "x_wv", "x_wo", "x_norm_post",
                "ffn_norm_pre", "ffn_w_up", "ffn_conv_w", "ffn_conv_b", "ffn_w_down", "ffn_norm_post")


def _flat_rows(parts):
    v = jnp.concatenate([p.reshape(-1) for p in parts])
    rows = -(-v.shape[0] // (8 * D_MODEL)) * 8
    return jnp.pad(v, (0, rows * D_MODEL - v.shape[0])).reshape(rows, D_MODEL)


def _layer_fwd(h0, mem, p, cos, sin, tag):
    sv = {"h0": h0}
    n1, u = _rms_mm(h0, _row(p["mix_norm_pre"]), p["w_in"], tm=512, tn=1408, out_dtype=F32, name=f"mix_in{tag}")
    a_out = _sb_fwd(u, name=f"sb_fwd{tag}")
    b_out, c = _cv_fwd(u, p["cv_w"], _row(p["cv_b"]), _row(p["cv_ln_g"]), _row(p["cv_ln_b"]),
                       p["cv_pw_w"].astype(BF16), _row(p["cv_pw_b"]), name=f"cv_fwd{tag}")
    qkv = _rope_perm(u, cos, sin, name=f"rope_perm{tag}")
    o_p, l_p = _dl_fwd(qkv, name=f"dl_fwd{tag}")
    c_out, o_dl, lse = _dl_mix(o_p, l_p, name=f"dl_mix{tag}")
    cat = jnp.concatenate([a_out, b_out, c_out], axis=1)
    y1, h1 = _mm_post(cat, p["w_out"], h0, _row(p["mix_norm_post"]), tm=256, name=f"mix_out{tag}")
    sv.update(n1=n1, u=u, c=c, qkv=qkv, o_dl=o_dl, lse=lse, cat=cat, y1=y1, h1=h1)

    n2, q = _rms_mm(h1, _row(p["x_norm_pre"]), p["x_wq"], tm=512, tn=1024, out_dtype=BF16, name=f"xa_q{tag}")
    wkv = jnp.concatenate([p["x_wk"], p["x_wv"]], axis=1)
    mem_n, kv = _rms_mm(mem, _row(p["mem_norm"]), wkv, tm=mem.shape[0], tn=1024, out_dtype=BF16, name=f"xa_kv{tag}")
    k, v = kv[:, :D_MODEL], kv[:, D_MODEL:]
    o_x = _xa_fwd(q, k, v, name=f"xa_fwd{tag}")
    y2, h2 = _mm_post(o_x, p["x_wo"], h1, _row(p["x_norm_post"]), tm=256, name=f"xa_out{tag}")
    sv.update(n2=n2, q=q, mem_n=mem_n, k=k, v=v, o_x=o_x, y2=y2, h2=h2, wkv=wkv)

    n3, up = _rms_mm(h2, _row(p["ffn_norm_pre"]), p["ffn_w_up"], tm=512, tn=1408, out_dtype=F32, name=f"ffn_up{tag}")
    act = _ff_gate_fwd(up, p["ffn_conv_w"], _row(p["ffn_conv_b"]), name=f"ffn_gate{tag}")
    y3, h3 = _mm_post(act, p["ffn_w_down"], h2, _row(p["ffn_norm_post"]), tm=256, name=f"ffn_down{tag}")
    sv.update(n3=n3, up=up, act=act, y3=y3)
    return h3, sv


def _layer_bwd(dh3, mem, p, sv, cos, sin, tag):
    g = {}
    s8 = lambda part: part.sum(axis=0)

    dy3, dgp = _rms_bwd(sv["y3"], _row(p["ffn_norm_post"]), dh3, None, out_dtype=BF16, tm=256, name=f"ffn_post_b{tag}")
    g["ffn_norm_post"] = s8(dgp)
    dact = _mm_nt(dy3, p["ffn_w_down"], tm=512, tn=1408, out_dtype=F32, name=f"ffn_down_bx{tag}")
    g["ffn_w_down"] = _mm_tn(sv["act"], dy3, tk=1408, tn=1024, tm=512, name=f"ffn_down_bw{tag}")
    dgu, dvu, dcw, dcb = _ff_gate_bwd(sv["up"], dact, p["ffn_conv_w"], _row(p["ffn_conv_b"]), name=f"ffn_gate_b{tag}")
    g["ffn_conv_w"] = jnp.concatenate([dcw[0], dcw[1]], axis=1).reshape(3, 8, 2 * D_FF).sum(axis=1)
    g["ffn_conv_b"] = jnp.concatenate([dcb[0], dcb[1]], axis=1).sum(axis=0)
    dup = jnp.concatenate([dgu, dvu], axis=1)
    dn3 = _mm_nt(dup, p["ffn_w_up"], tm=256, tn=512, out_dtype=F32, name=f"ffn_up_bx{tag}")
    g["ffn_w_up"] = _mm_tn(sv["n3"], dup, tk=512, tn=1408, tm=512, name=f"ffn_up_bw{tag}")
    dh2, dgp = _rms_bwd(sv["h2"], _row(p["ffn_norm_pre"]), dn3, dh3, out_dtype=F32, tm=256, name=f"ffn_pre_b{tag}")
    g["ffn_norm_pre"] = s8(dgp)

    dy2, dgp = _rms_bwd(sv["y2"], _row(p["x_norm_post"]), dh2, None, out_dtype=BF16, tm=256, name=f"xa_post_b{tag}")
    g["x_norm_post"] = s8(dgp)
    do_x = _mm_nt(dy2, p["x_wo"], tm=512, tn=1024, out_dtype=BF16, name=f"xa_out_bx{tag}")
    g["x_wo"] = _mm_tn(sv["o_x"], dy2, tk=512, tn=1024, tm=512, name=f"xa_out_bw{tag}")
    dq, dk, dv = _xa_bwd(sv["q"], sv["k"], sv["v"], do_x, name=f"xa_bwd{tag}")
    dn2 = _mm_nt(dq, p["x_wq"], tm=512, tn=1024, out_dtype=F32, name=f"xa_q_bx{tag}")
    g["x_wq"] = _mm_tn(sv["n2"], dq, tk=512, tn=1024, tm=512, name=f"xa_q_bw{tag}")
    dkv = jnp.concatenate([dk, dv], axis=1).astype(BF16)
    nm = mem.shape[0]
    dmem_n = _mm_nt(dkv, sv["wkv"], tm=nm, tn=1024, out_dtype=F32, name=f"xa_kv_bx{tag}")
    dwkv = _mm_tn(sv["mem_n"], dkv, tk=512, tn=2048, tm=nm, name=f"xa_kv_bw{tag}")
    g["x_wk"], g["x_wv"] = dwkv[:, :D_MODEL], dwkv[:, D_MODEL:]
    _, dgp = _rms_bwd(mem, _row(p["mem_norm"]), dmem_n, None, out_dtype=BF16, tm=nm, name=f"xa_mem_b{tag}")
    g["mem_norm"] = s8(dgp)
    dh1, dgp = _rms_bwd(sv["h1"], _row(p["x_norm_pre"]), dn2, dh2, out_dtype=F32, tm=256, name=f"xa_pre_b{tag}")
    g["x_norm_pre"] = s8(dgp)

    dy1, dgp = _rms_bwd(sv["y1"], _row(p["mix_norm_post"]), dh1, None, out_dtype=BF16, tm=256, name=f"mix_post_b{tag}")
    g["mix_norm_post"] = s8(dgp)
    dcat = _mm_nt(dy1, p["w_out"], tm=512, tn=1024, out_dtype=F32, name=f"mix_out_bx{tag}")
    g["w_out"] = _mm_tn(sv["cat"], dy1, tk=512, tn=1024, tm=512, name=f"mix_out_bw{tag}")
    u = sv["u"]
    dq_sb, dk_sb, dv_sb = _sb_bwd(u, dcat, name=f"sb_bwd{tag}")
    pw_b16 = p["cv_pw_w"].astype(BF16)
    dc, dpw, vec = _cv_bwd_local(sv["c"], dcat, _row(p["cv_ln_g"]), _row(p["cv_ln_b"]), pw_b16, name=f"cv_bwd_a{tag}")
    g["cv_pw_w"] = dpw
    vec = vec.reshape(3, 8, CV_W).sum(axis=1)
    g["cv_pw_b"], g["cv_ln_g"], g["cv_ln_b"] = vec[0], vec[1], vec[2]
    du_cv, dcw, dcb = _cv_bwd_conv(u, dc, p["cv_w"], name=f"cv_bwd_b{tag}")
    g["cv_w"] = dcw.reshape(CV_K, 8, CV_W).sum(axis=1)
    g["cv_b"] = dcb.sum(axis=0)
    dop, stats = _dl_bwd_prep(dcat, sv["o_dl"], sv["lse"], name=f"dl_prep_b{tag}")
    cur, prev = _dl_bwd(sv["qkv"], dop, stats, name=f"dl_bwd{tag}")
    du_dl = _dl_bwd_finish(cur, prev, cos, sin, name=f"dl_fin_b{tag}")
    du = jnp.concatenate([dq_sb.astype(BF16), dk_sb.astype(BF16), dv_sb.astype(BF16), du_cv, du_dl], axis=1)
    dn1 = _mm_nt(du, p["w_in"], tm=512, tn=512, out_dtype=F32, name=f"mix_in_bx{tag}")
    g["w_in"] = _mm_tn(sv["n1"], du, tk=512, tn=1408, tm=512, name=f"mix_in_bw{tag}")
    dh0, dgp = _rms_bwd(sv["h0"], _row(p["mix_norm_pre"]), dn1, dh1, out_dtype=F32, tm=256, name=f"mix_pre_b{tag}")
    g["mix_norm_pre"] = s8(dgp)
    return dh0, g


def _step(x, mem, positions, loss_target, w, m, v):
    depth = w["w_in"].shape[0]
    xi, yi, ci = lax.axis_index("x"), lax.axis_index("y"), lax.axis_index("c")
    chip = 2 * xi + yi
    h = x[0]
    mem0 = mem[0]
    s_len = h.shape[0]

    pack = jnp.stack([jnp.concatenate([_to_pack_rows(n, w[n][l]) for n in BIG], axis=0) for l in range(depth)])
    gathered = _gather_weights(pack.astype(BF16), name="gather_weights")
    params = []
    for l in range(depth):
        p, off = {}, 0
        for n, rows in PACK_ROWS:
            p[n] = _full_from_blocks(n, gathered[:, l, off:off + rows, :])
            off += rows
        for n, _, _ in SMALL:
            p[n] = w[n][l]
        params.append(p)
    small_w = []
    for l in range(depth):
        for n, shape, axis in SMALL:
            if axis is not None:
                full = jnp.zeros(shape, F32)
                full = lax.dynamic_update_slice_in_dim(full, w[n][l], chip * w[n][l].shape[axis], axis)
                small_w.append(full * jnp.where(ci == 0, 1.0, 0.0))
    small_w_sum = _all_reduce_small(_flat_rows(small_w), name="gather_small_weights")
    off = 0
    for l in range(depth):
        for n, shape, axis in SMALL:
            if axis is not None:
                size = int(np.prod(shape))
                params[l][n] = small_w_sum.reshape(-1)[off:off + size].reshape(shape)
                off += size

    inv_freq = ROPE_THETA ** (-jnp.arange(HD // 2, dtype=F32) / (HD // 2))
    cos, sin = _rope_tables(positions.reshape(s_len, 1), jnp.tile(inv_freq, 4).reshape(1, LANES), name="rope_tables")

    saved = []
    for l in range(depth):
        h, sv = _layer_fwd(h, mem0, params[l], cos, sin, f"_l{l}")
        saved.append(sv)
    dh, sq = _loss_grad(h, loss_target[0], tm=256, name="loss_grad")
    loss = lax.psum(0.5 * jnp.sum(sq) / D_MODEL, ("x", "y", "c"))
    grads = [None] * depth
    for l in reversed(range(depth)):
        dh, grads[l] = _layer_bwd(dh, mem0, params[l], saved[l], cos, sin, f"_l{l}")
    grad_x = dh[None]

    gw = jnp.stack([jnp.concatenate([_blocks_from_full(n, grads[l][n]) for n in BIG], axis=1) for l in range(depth)])
    c_arr, j_arr = jnp.reshape(ci, (1,)).astype(jnp.int32), jnp.reshape(chip, (1,)).astype(jnp.int32)
    got = _swap_other_layer(gw, name="rs_swap_layers")
    s32, s16 = _chip_sum(gw, got, c_arr, name="rs_chip_sum")
    got16 = _scatter_chip_sums(s16, name="rs_scatter")
    ghalf = _mesh_sum(s32, got16, j_arr, name="rs_mesh_sum")
    gfull = _share_layers(ghalf, name="rs_share_layers")

    out_g, out_d, out_m, out_v = {}, {}, {}, {}
    off = 0
    for n, rows in PACK_ROWS:
        shard_shape = w[n].shape
        g_n = gfull[:, off:off + rows, :].reshape(shard_shape)
        off += rows
        flat = lambda a: a.reshape(-1, shard_shape[-1])
        d_n, m_n, v_n = _adamw(flat(w[n]), flat(g_n), flat(m[n]), flat(v[n]), name=f"adamw_{n}")
        out_g[n], out_d[n], out_m[n], out_v[n] = g_n, d_n.reshape(shard_shape), m_n.reshape(shard_shape), v_n.reshape(shard_shape)

    g_small = _all_reduce_small(_flat_rows([grads[l][n] for l in range(depth) for n, _, _ in SMALL]),
                                name="all_reduce_small_grads").reshape(-1)
    local_g, off = {}, 0
    for l in range(depth):
        for n, shape, axis in SMALL:
            size = int(np.prod(shape))
            full = g_small[off:off + size].reshape(shape)
            off += size
            if axis is not None:
                blk = w[n].shape[1 + axis]
                full = lax.dynamic_slice_in_dim(full, chip * blk, blk, axis)
            local_g.setdefault(n, []).append(full)
    names = [n for n, _, _ in SMALL]
    g_loc = {n: jnp.stack(local_g[n]) for n in names}
    d_s, m_s, v_s = _adamw(_flat_rows([w[n] for n in names]), _flat_rows([g_loc[n] for n in names]),
                           _flat_rows([m[n] for n in names]), _flat_rows([v[n] for n in names]), name="adamw_small")
    off = 0
    for n in names:
        size = int(np.prod(w[n].shape))
        take = lambda a: a.reshape(-1)[off:off + size].reshape(w[n].shape)
        out_g[n], out_d[n], out_m[n], out_v[n] = g_loc[n], take(d_s), take(m_s), take(v_s)
        off += size

    outs = [loss, grad_x]
    for group in (out_g, out_d, out_m, out_v):
        outs += [group[n] for n in WEIGHT_ORDER]
    return tuple(outs)


def kernel(x, mem, positions, mix_norm_pre, w_in, cv_w, cv_b, cv_ln_g, cv_ln_b, cv_pw_w, cv_pw_b, w_out, mix_norm_post, x_norm_pre, mem_norm, x_wq, x_wk, x_wv, x_wo, x_norm_post, ffn_norm_pre, ffn_w_up, ffn_conv_w, ffn_conv_b, ffn_w_down, ffn_norm_post, loss_target, m_mix_norm_pre, m_w_in, m_cv_w, m_cv_b, m_cv_ln_g, m_cv_ln_b, m_cv_pw_w, m_cv_pw_b, m_w_out, m_mix_norm_post, m_x_norm_pre, m_mem_norm, m_x_wq, m_x_wk, m_x_wv, m_x_wo, m_x_norm_post, m_ffn_norm_pre, m_ffn_w_up, m_ffn_conv_w, m_ffn_conv_b, m_ffn_w_down, m_ffn_norm_post, v_mix_norm_pre, v_w_in, v_cv_w, v_cv_b, v_cv_ln_g, v_cv_ln_b, v_cv_pw_w, v_cv_pw_b, v_w_out, v_mix_norm_post, v_x_norm_pre, v_mem_norm, v_x_wq, v_x_wk, v_x_wv, v_x_wo, v_x_norm_post, v_ffn_norm_pre, v_ffn_w_up, v_ffn_conv_w, v_ffn_conv_b, v_ffn_w_down, v_ffn_norm_post):
    w = dict(zip(WEIGHT_ORDER, (mix_norm_pre, w_in, cv_w, cv_b, cv_ln_g, cv_ln_b, cv_pw_w, cv_pw_b, w_out, mix_norm_post, x_norm_pre, mem_norm, x_wq, x_wk, x_wv, x_wo, x_norm_post, ffn_norm_pre, ffn_w_up, ffn_conv_w, ffn_conv_b, ffn_w_down, ffn_norm_post)))
    m = dict(zip(WEIGHT_ORDER, (m_mix_norm_pre, m_w_in, m_cv_w, m_cv_b, m_cv_ln_g, m_cv_ln_b, m_cv_pw_w, m_cv_pw_b, m_w_out, m_mix_norm_post, m_x_norm_pre, m_mem_norm, m_x_wq, m_x_wk, m_x_wv, m_x_wo, m_x_norm_post, m_ffn_norm_pre, m_ffn_w_up, m_ffn_conv_w, m_ffn_conv_b, m_ffn_w_down, m_ffn_norm_post)))
    v = dict(zip(WEIGHT_ORDER, (v_mix_norm_pre, v_w_in, v_cv_w, v_cv_b, v_cv_ln_g, v_cv_ln_b, v_cv_pw_w, v_cv_pw_b, v_w_out, v_mix_norm_post, v_x_norm_pre, v_mem_norm, v_x_wq, v_x_wk, v_x_wv, v_x_wo, v_x_norm_post, v_ffn_norm_pre, v_ffn_w_up, v_ffn_conv_w, v_ffn_conv_b, v_ffn_w_down, v_ffn_norm_post)))
    return _step(x, mem, positions, loss_target, w, m, v)
```

```python
import functools

import jax
import jax.numpy as jnp
import numpy as np
from jax import lax
from jax.experimental import pallas as pl
from jax.experimental.pallas import tpu as pltpu

F32, BF16 = jnp.float32, jnp.bfloat16
MESH = pl.DeviceIdType.MESH
EPS = 1e-6
LANES = 128
BLK = 128
HD = 64
D_MODEL = 1024
D_FF = 2816
SB_W, CV_W, DL_W = 256, 256, 512
CV_K = 31
ROPE_THETA = 10000.0
DILATIONS = (1, 4, 16)
X_HEADS, X_HD = 4, 256
ADAM_LR, ADAM_B1, ADAM_B2, ADAM_EPS, ADAM_WD, ADAM_STEP = 0.001, 0.9, 0.999, 1e-08, 0.01, 10
NEG_INF = float("-inf")
MIB = 1 << 20

PACK_ROWS = (("w_in", 704), ("w_out", 256), ("x_wq", 256), ("x_wk", 256), ("x_wv", 256), ("x_wo", 256),
             ("ffn_w_up", 1408), ("ffn_w_down", 704))
PACK_RL = sum(r for _, r in PACK_ROWS)


def _cp(vmem_mb=48):
    return pltpu.CompilerParams(vmem_limit_bytes=vmem_mb * MIB)


def _dot(a, b):
    return jnp.dot(a, b, preferred_element_type=F32)


def _dot_nt(a, b):
    return lax.dot_general(a, b, (((1,), (1,)), ((), ())), preferred_element_type=F32)


def _dot_tn(a, b):
    return lax.dot_general(a, b, (((0,), (0,)), ((), ())), preferred_element_type=F32)


def _dot_hilo(x, m):
    hi = x.astype(BF16)
    lo = (x - hi.astype(F32)).astype(BF16)
    return _dot(hi, m) + _dot(lo, m)


def _rowsum8(x):
    t, c = x.shape
    return x.reshape(t // 8, 8, c).sum(axis=0)


def _acc_out(ref, i, val):
    @pl.when(i == 0)
    def _():
        ref[...] = val

    @pl.when(i > 0)
    def _():
        ref[...] += val


def _tile(n, cap, mult=8):
    t = min(n, cap)
    while n % t or t % mult:
        t -= 1
    return t


def _rms_mm(x, g, w, *, tm, tn, out_dtype, name):
    m, d = x.shape
    n_out = w.shape[1]

    def body(x_ref, g_ref, w_ref, n_ref, o_ref):
        @pl.when(pl.program_id(1) == 0)
        def _():
            xv = x_ref[...]
            r = lax.rsqrt(jnp.mean(xv * xv, axis=-1, keepdims=True) + EPS)
            n_ref[...] = (xv * r * g_ref[...]).astype(BF16)

        o_ref[...] = _dot(n_ref[...], w_ref[...]).astype(out_dtype)

    return pl.pallas_call(
        body, grid=(m // tm, n_out // tn), name=name,
        in_specs=[pl.BlockSpec((tm, d), lambda i, j: (i, 0)), pl.BlockSpec((1, d), lambda i, j: (0, 0)),
                  pl.BlockSpec((d, tn), lambda i, j: (0, j))],
        out_specs=[pl.BlockSpec((tm, d), lambda i, j: (i, 0)), pl.BlockSpec((tm, tn), lambda i, j: (i, j))],
        out_shape=[jax.ShapeDtypeStruct((m, d), BF16), jax.ShapeDtypeStruct((m, n_out), out_dtype)],
        compiler_params=_cp())(x, g, w)


def _mm_post(a, w, h, g, *, tm, name):
    m, k = a.shape
    d = w.shape[1]

    def body(a_ref, w_ref, h_ref, g_ref, y_ref, ho_ref):
        y = _dot(a_ref[...], w_ref[...])
        y_ref[...] = y
        r = lax.rsqrt(jnp.mean(y * y, axis=-1, keepdims=True) + EPS)
        ho_ref[...] = h_ref[...] + y * r * g_ref[...]

    return pl.pallas_call(
        body, grid=(m // tm,), name=name,
        in_specs=[pl.BlockSpec((tm, k), lambda i: (i, 0)), pl.BlockSpec((k, d), lambda i: (0, 0)),
                  pl.BlockSpec((tm, d), lambda i: (i, 0)), pl.BlockSpec((1, d), lambda i: (0, 0))],
        out_specs=[pl.BlockSpec((tm, d), lambda i: (i, 0)), pl.BlockSpec((tm, d), lambda i: (i, 0))],
        out_shape=[jax.ShapeDtypeStruct((m, d), F32), jax.ShapeDtypeStruct((m, d), F32)],
        compiler_params=_cp())(a, w, h, g)


def _mm_nt(a, w, *, tm, tn, out_dtype, name):
    m, k = a.shape
    n_out = w.shape[0]

    def body(a_ref, w_ref, o_ref):
        o_ref[...] = _dot_nt(a_ref[...], w_ref[...]).astype(out_dtype)

    return pl.pallas_call(
        body, grid=(m // tm, n_out // tn), name=name,
        in_specs=[pl.BlockSpec((tm, k), lambda i, j: (i, 0)), pl.BlockSpec((tn, k), lambda i, j: (j, 0))],
        out_specs=pl.BlockSpec((tm, tn), lambda i, j: (i, j)),
        out_shape=jax.ShapeDtypeStruct((m, n_out), out_dtype),
        compiler_params=_cp())(a, w)


def _mm_tn(x, dy, *, tk, tn, tm, name):
    m, k = x.shape
    n_out = dy.shape[1]

    def body(x_ref, d_ref, o_ref):
        _acc_out(o_ref, pl.program_id(2), _dot_tn(x_ref[...], d_ref[...]))

    return pl.pallas_call(
        body, grid=(k // tk, n_out // tn, m // tm), name=name,
        in_specs=[pl.BlockSpec((tm, tk), lambda a, b, c: (c, a)), pl.BlockSpec((tm, tn), lambda a, b, c: (c, b))],
        out_specs=pl.BlockSpec((tk, tn), lambda a, b, c: (a, b)),
        out_shape=jax.ShapeDtypeStruct((k, n_out), F32),
        compiler_params=_cp())(x, dy)


def _rms_bwd(x, g, dout, res, *, out_dtype, tm, name):
    m, d = x.shape
    has_res = res is not None

    def body(*refs):
        if has_res:
            x_ref, g_ref, d_ref, r_ref, dx_ref, dg_ref = refs
        else:
            x_ref, g_ref, d_ref, dx_ref, dg_ref = refs
        xv = x_ref[...]
        dv = d_ref[...].astype(F32)
        r = lax.rsqrt(jnp.mean(xv * xv, axis=-1, keepdims=True) + EPS)
        xh = xv * r
        dxh = dv * g_ref[...]
        dx = r * (dxh - xh * jnp.mean(dxh * xh, axis=-1, keepdims=True))
        if has_res:
            dx = dx + r_ref[...]
        dx_ref[...] = dx.astype(out_dtype)
        _acc_out(dg_ref, pl.program_id(0), _rowsum8(dv * xh))

    row = pl.BlockSpec((tm, d), lambda i: (i, 0))
    ins = [row, pl.BlockSpec((1, d), lambda i: (0, 0)), row] + ([row] if has_res else [])
    args = (x, g, dout) + ((res,) if has_res else ())
    return pl.pallas_call(
        body, grid=(m // tm,), name=name, in_specs=ins,
        out_specs=[row, pl.BlockSpec((8, d), lambda i: (0, 0))],
        out_shape=[jax.ShapeDtypeStruct((m, d), out_dtype), jax.ShapeDtypeStruct((8, d), F32)],
        compiler_params=_cp())(*args)


def _loss_grad(h, tgt, *, tm, name):
    m, d = h.shape

    def body(h_ref, t_ref, dh_ref, p_ref):
        e = h_ref[...] - t_ref[...]
        dh_ref[...] = e / d
        _acc_out(p_ref, pl.program_id(0), _rowsum8(e * e))

    row = pl.BlockSpec((tm, d), lambda i: (i, 0))
    return pl.pallas_call(
        body, grid=(m // tm,), name=name, in_specs=[row, row],
        out_specs=[row, pl.BlockSpec((8, d), lambda i: (0, 0))],
        out_shape=[jax.ShapeDtypeStruct((m, d), F32), jax.ShapeDtypeStruct((8, d), F32)],
        compiler_params=_cp())(h, tgt)


def _adamw(w, g, m, v, *, name):
    r, c = w.shape
    tr = _tile(r, 256)

    def body(w_ref, g_ref, m_ref, v_ref, d_ref, mo_ref, vo_ref):
        gv = g_ref[...]
        m2 = ADAM_B1 * m_ref[...] + (1.0 - ADAM_B1) * gv
        v2 = ADAM_B2 * v_ref[...] + (1.0 - ADAM_B2) * jnp.square(gv)
        m_hat = m2 / (1.0 - ADAM_B1 ** ADAM_STEP)
        v_hat = v2 / (1.0 - ADAM_B2 ** ADAM_STEP)
        d_ref[...] = -ADAM_LR * (m_hat / (jnp.sqrt(v_hat) + ADAM_EPS) + ADAM_WD * w_ref[...])
        mo_ref[...] = m2
        vo_ref[...] = v2

    blk = pl.BlockSpec((tr, c), lambda i: (i, 0))
    return pl.pallas_call(
        body, grid=(r // tr,), name=name, in_specs=[blk] * 4, out_specs=[blk] * 3,
        out_shape=[jax.ShapeDtypeStruct((r, c), F32)] * 3, compiler_params=_cp())(w, g, m, v)


def _head_masks():
    lane = lax.broadcasted_iota(jnp.int32, (BLK, LANES), 1)
    row = lax.broadcasted_iota(jnp.int32, (BLK, LANES), 0)
    return lane, row, lane < HD


def _sb_scores(q_a, k, before):
    z = _dot_nt(q_a, k)
    sp = jnp.log1p(jnp.exp(-jnp.abs(z)))
    ls_pos = jnp.minimum(z, 0.0) - sp
    lkeep = jnp.where(before, ls_pos - z, 0.0)
    return ls_pos, lkeep


SB_DEAD = -104.0


def _sb_alive(jj, i, c0, c1):
    return jnp.logical_and(jj <= i, jnp.max(jnp.maximum(c0, c1)) > SB_DEAD)


def _sb_fwd(u, *, name):
    s_len = u.shape[0]
    nq = s_len // BLK

    def body(q_ref, k_ref, v_ref, o_ref):
        i = pl.program_id(1)
        lane, row, lane_h = _head_masks()
        suffix = (row > lane).astype(BF16)
        q = q_ref[...] * 0.125
        q_heads = (jnp.where(lane_h, q, 0.0).astype(BF16), jnp.where(lane_h, 0.0, q).astype(BF16))

        def step(state):
            jj, carry = state[0], state[1:]
            off = pl.multiple_of((i - jj) * BLK, BLK)
            k = k_ref[pl.ds(off, BLK), :].astype(BF16)
            v = v_ref[pl.ds(off, BLK), :].astype(BF16)
            before = jnp.logical_or(jj > 0, lane < row)
            out = []
            for a in range(2):
                cc, acc = carry[2 * a], carry[2 * a + 1]
                ls_pos, lkeep = _sb_scores(q_heads[a], k, before)
                between = _dot_hilo(lkeep, suffix) + cc
                att = jnp.where(before, jnp.exp(ls_pos + between), 0.0)
                out += [cc + jnp.sum(lkeep, axis=1, keepdims=True), acc + _dot(att.astype(BF16), v)]
            return (jj + 1,) + tuple(out)

        zc, za = jnp.zeros((BLK, 1), F32), jnp.zeros((BLK, LANES), F32)
        res = lax.while_loop(lambda st: _sb_alive(st[0], i, st[1], st[3]), step, (jnp.int32(0), zc, za, zc, za))
        o_ref[...] = jnp.where(lane_h, res[2], res[4]).astype(BF16)

    return pl.pallas_call(
        body, grid=(2, nq), name=name,
        in_specs=[pl.BlockSpec((BLK, LANES), lambda hp, i: (i, hp)),
                  pl.BlockSpec((s_len, LANES), lambda hp, i: (0, 2 + hp)),
                  pl.BlockSpec((s_len, LANES), lambda hp, i: (0, 4 + hp))],
        out_specs=pl.BlockSpec((BLK, LANES), lambda hp, i: (i, hp)),
        out_shape=jax.ShapeDtypeStruct((s_len, SB_W), BF16), compiler_params=_cp())(u, u, u)


def _sb_bwd(u, dcat, *, name):
    s_len = u.shape[0]
    nq = s_len // BLK

    def body(q_ref, k_ref, v_ref, do_ref, dq_ref, dk_ref, dv_ref, g_scr, b_scr):
        i = pl.program_id(1)
        lane, row, lane_h = _head_masks()
        suffix = (row > lane).astype(BF16)
        prefix = (row < lane).astype(BF16)
        qf = q_ref[...]
        q = qf * 0.125
        q_heads = (jnp.where(lane_h, q, 0.0).astype(BF16), jnp.where(lane_h, 0.0, q).astype(BF16))
        qb = qf.astype(BF16)
        dof = do_ref[...]
        do_heads = (jnp.where(lane_h, dof, 0.0).astype(BF16), jnp.where(lane_h, 0.0, dof).astype(BF16))
        dob = dof.astype(BF16)

        @pl.when(i == 0)
        def _():
            dk_ref[...] = jnp.zeros_like(dk_ref)
            dv_ref[...] = jnp.zeros_like(dv_ref)

        def down(state):
            jj, carry = state[0], state[1:]
            j = i - jj
            off = pl.multiple_of(j * BLK, BLK)
            k = k_ref[pl.ds(off, BLK), :].astype(BF16)
            v = v_ref[pl.ds(off, BLK), :].astype(BF16)
            before = jnp.logical_or(jj > 0, lane < row)
            out, dvs = [], []
            for a in range(2):
                ls_pos, lkeep = _sb_scores(q_heads[a], k, before)
                between = _dot_hilo(lkeep, suffix) + carry[a]
                att = jnp.where(before, jnp.exp(ls_pos + between), 0.0)
                g_scr[a * nq + j] = att * _dot_nt(do_heads[a], v)
                b_scr[a * nq + j] = jnp.exp(ls_pos)
                dvs.append(_dot_tn(att.astype(BF16), dob))
                out.append(carry[a] + jnp.sum(lkeep, axis=1, keepdims=True))
            dv_ref[pl.ds(off, BLK), :] += jnp.where(lane_h, dvs[0], dvs[1])
            return (jj + 1,) + tuple(out)

        zc = jnp.zeros((BLK, 1), F32)
        visited = lax.while_loop(lambda st: _sb_alive(st[0], i, st[1], st[2]), down, (jnp.int32(0), zc, zc))[0]

        def up(j, carry):
            off = pl.multiple_of(j * BLK, BLK)
            k = k_ref[pl.ds(off, BLK), :].astype(BF16)
            before = jnp.logical_or(j < i, lane < row)
            out, dks = [], []
            for a in range(2):
                pc, dq = carry[2 * a], carry[2 * a + 1]
                g = g_scr[a * nq + j]
                beta = b_scr[a * nq + j]
                below = _dot_hilo(g, prefix) + pc
                dz = (jnp.where(before, g * (1.0 - beta) - beta * below, 0.0) * 0.125).astype(BF16)
                dks.append(_dot_tn(dz, qb))
                out += [pc + jnp.sum(g, axis=1, keepdims=True), dq + _dot(dz, k)]
            dk_ref[pl.ds(off, BLK), :] += jnp.where(lane_h, dks[0], dks[1])
            return tuple(out)

        za = jnp.zeros((BLK, LANES), F32)
        res = lax.fori_loop(i + 1 - visited, i + 1, up, (zc, za, zc, za))
        dq_ref[...] = jnp.where(lane_h, res[1], res[3])

    col = lambda c0: pl.BlockSpec((s_len, LANES), lambda hp, i: (0, c0 + hp))
    blk = pl.BlockSpec((BLK, LANES), lambda hp, i: (i, hp))
    acc = pl.BlockSpec((s_len, LANES), lambda hp, i: (0, hp))
    return pl.pallas_call(
        body, grid=(2, nq), name=name, in_specs=[blk, col(2), col(4), blk],
        out_specs=[blk, acc, acc], out_shape=[jax.ShapeDtypeStruct((s_len, SB_W), F32)] * 3,
        scratch_shapes=[pltpu.VMEM((2 * nq, BLK, LANES), F32), pltpu.VMEM((2 * nq, BLK, LANES), F32)],
        compiler_params=_cp())(u, u, u, dcat)


CV_T = 512
CV_H = 32


def _cv_specs(s_len):
    cur = lambda c: pl.BlockSpec((CV_T, CV_W), lambda i: (i, c))
    prev = lambda c: pl.BlockSpec((CV_H, CV_W), lambda i: (jnp.maximum(i * (CV_T // CV_H) - 1, 0), c))
    nxt = lambda c: pl.BlockSpec((CV_H, CV_W),
                                 lambda i: (jnp.minimum((i + 1) * (CV_T // CV_H), s_len // CV_H - 1), c))
    full = lambda r: pl.BlockSpec((r, CV_W), lambda i: (0, 0))
    return cur, prev, nxt, full


def _glu_into(gp_ref, val_ref, gate_ref, valp_ref, gatep_ref, i):
    gp_ref[0:CV_H, :] = jnp.where(i > 0, valp_ref[...] * jax.nn.sigmoid(gatep_ref[...]), 0.0)
    gp_ref[CV_H:, :] = val_ref[...] * jax.nn.sigmoid(gate_ref[...])


def _cv_fwd(u, cv_w, cv_b, ln_g, ln_b, pw_w, pw_b, *, name):
    s_len = u.shape[0]
    cur, prev, _, full = _cv_specs(s_len)

    def body(val_ref, gate_ref, valp_ref, gatep_ref, w_ref, b_ref, g_ref, be_ref, pw_ref, pb_ref,
             o_ref, c_ref, gp_ref):
        _glu_into(gp_ref, val_ref, gate_ref, valp_ref, gatep_ref, pl.program_id(0))
        acc = jnp.zeros((CV_T, CV_W), F32) + b_ref[...]
        for k in range(CV_K):
            acc = acc + w_ref[k:k + 1, :] * gp_ref[pl.ds(CV_H - CV_K + 1 + k, CV_T), :]
        c_ref[...] = acc
        mu = jnp.mean(acc, axis=-1, keepdims=True)
        xc = acc - mu
        xh = xc * lax.rsqrt(jnp.mean(xc * xc, axis=-1, keepdims=True) + EPS)
        a = xh * g_ref[...] + be_ref[...]
        s = a * jax.nn.sigmoid(a)
        o_ref[...] = (_dot(s.astype(BF16), pw_ref[...]) + pb_ref[...]).astype(BF16)

    return pl.pallas_call(
        body, grid=(s_len // CV_T,), name=name,
        in_specs=[cur(3), cur(4), prev(3), prev(4), full(CV_K), full(1), full(1), full(1), full(CV_W), full(1)],
        out_specs=[cur(0), cur(0)],
        out_shape=[jax.ShapeDtypeStruct((s_len, CV_W), BF16), jax.ShapeDtypeStruct((s_len, CV_W), F32)],
        scratch_shapes=[pltpu.VMEM((CV_T + CV_H, CV_W), F32)], compiler_params=_cp())(
            u, u, u, u, cv_w, cv_b, ln_g, ln_b, pw_w, pw_b)


def _cv_bwd_local(c, dcat, ln_g, ln_b, pw_w, *, name):
    s_len = c.shape[0]
    cur, _, _, full = _cv_specs(s_len)

    def body(c_ref, db_ref, g_ref, be_ref, pw_ref, dc_ref, dpw_ref, vec_ref):
        i = pl.program_id(0)
        cv = c_ref[...]
        db = db_ref[...]
        mu = jnp.mean(cv, axis=-1, keepdims=True)
        xc = cv - mu
        rstd = lax.rsqrt(jnp.mean(xc * xc, axis=-1, keepdims=True) + EPS)
        xh = xc * rstd
        a = xh * g_ref[...] + be_ref[...]
        sg = jax.nn.sigmoid(a)
        s = a * sg
        dbb = db.astype(BF16)
        ds = _dot_nt(dbb, pw_ref[...])
        da = ds * (sg * (1.0 + a * (1.0 - sg)))
        dxh = da * g_ref[...]
        dc_ref[...] = rstd * (dxh - jnp.mean(dxh, axis=-1, keepdims=True)
                              - xh * jnp.mean(dxh * xh, axis=-1, keepdims=True))
        _acc_out(dpw_ref, i, _dot_tn(s.astype(BF16), dbb))
        _acc_out(vec_ref, i, jnp.concatenate([_rowsum8(db), _rowsum8(da * xh), _rowsum8(da)], axis=0))

    return pl.pallas_call(
        body, grid=(s_len // CV_T,), name=name,
        in_specs=[cur(0), cur(1), full(1), full(1), full(CV_W)],
        out_specs=[cur(0), full(CV_W), full(24)],
        out_shape=[jax.ShapeDtypeStruct((s_len, CV_W), F32), jax.ShapeDtypeStruct((CV_W, CV_W), F32),
                   jax.ShapeDtypeStruct((24, CV_W), F32)], compiler_params=_cp())(c, dcat, ln_g, ln_b, pw_w)


def _cv_bwd_conv(u, dc, cv_w, *, name):
    s_len = u.shape[0]
    cur, prev, nxt, full = _cv_specs(s_len)
    last = s_len // CV_T - 1

    def body(val_ref, gate_ref, valp_ref, gatep_ref, dc_ref, dcn_ref, w_ref, du_ref, dw_ref, dbias_ref,
             gp_ref, dcp_ref):
        i = pl.program_id(0)
        _glu_into(gp_ref, val_ref, gate_ref, valp_ref, gatep_ref, i)
        dcv = dc_ref[...]
        dcp_ref[0:CV_T, :] = dcv
        dcp_ref[CV_T:, :] = jnp.where(i < last, dcn_ref[...], 0.0)
        dg = jnp.zeros((CV_T, CV_W), F32)
        parts = []
        for k in range(CV_K):
            dg = dg + w_ref[k:k + 1, :] * dcp_ref[pl.ds(CV_K - 1 - k, CV_T), :]
            parts.append(_rowsum8(dcv * gp_ref[pl.ds(CV_H - CV_K + 1 + k, CV_T), :]))
        _acc_out(dw_ref, i, jnp.concatenate(parts, axis=0))
        _acc_out(dbias_ref, i, _rowsum8(dcv))
        val = val_ref[...]
        sg = jax.nn.sigmoid(gate_ref[...])
        du_ref[:, 0:CV_W] = (dg * sg).astype(BF16)
        du_ref[:, CV_W:] = (dg * val * sg * (1.0 - sg)).astype(BF16)

    return pl.pallas_call(
        body, grid=(s_len // CV_T,), name=name,
        in_specs=[cur(3), cur(4), prev(3), prev(4), cur(0), nxt(0), full(CV_K)],
        out_specs=[pl.BlockSpec((CV_T, 2 * CV_W), lambda i: (i, 0)), full(CV_K * 8), full(8)],
        out_shape=[jax.ShapeDtypeStruct((s_len, 2 * CV_W), BF16), jax.ShapeDtypeStruct((CV_K * 8, CV_W), F32),
                   jax.ShapeDtypeStruct((8, CV_W), F32)],
        scratch_shapes=[pltpu.VMEM((CV_T + CV_H, CV_W), F32), pltpu.VMEM((CV_T + CV_H, CV_W), F32)],
        compiler_params=_cp())(u, u, u, u, dc, dc, cv_w)


def _rope_tables(pos_col, inv_freq_row, *, name):
    s_len = pos_col.shape[0]

    def body(p_ref, f_ref, cos_ref, sin_ref):
        ang = p_ref[...].astype(F32) * f_ref[...]
        lane = lax.broadcasted_iota(jnp.int32, (s_len, LANES), 1)
        sn = jnp.sin(ang)
        cos_ref[...] = jnp.cos(ang)
        sin_ref[...] = jnp.where(lane % HD < HD // 2, -sn, sn)

    return pl.pallas_call(body, name=name, out_shape=[jax.ShapeDtypeStruct((s_len, LANES), F32)] * 2,
                          compiler_params=_cp())(pos_col, inv_freq_row)


def _rot_half(x):
    lane = lax.broadcasted_iota(jnp.int32, x.shape, 1)
    return jnp.where(lane % HD < HD // 2, pltpu.roll(x, LANES - HD // 2, 1), pltpu.roll(x, HD // 2, 1))


def _permute_rows(dst_ref, src_ref, d, dtype):
    s_len = src_ref.shape[0]
    seg = s_len // d
    if d == 1:
        dst_ref[...] = src_ref[...].astype(dtype)
        return
    for r in range(d):
        dst_ref[r * seg:(r + 1) * seg, :] = src_ref[pl.ds(r, seg, stride=d), :].astype(dtype)


def _unpermute_rows(dst_ref, src_ref, d):
    s_len = src_ref.shape[0]
    seg = s_len // d
    if d == 1:
        dst_ref[...] = src_ref[...]
        return
    for r in range(d):
        dst_ref[pl.ds(r, seg, stride=d), :] = src_ref[r * seg:(r + 1) * seg, :]


def _rope_perm(u, cos, sin, *, name):
    s_len = u.shape[0]

    def body(x_ref, cos_ref, sin_ref, o_ref, scr):
        a = pl.program_id(0)
        x = x_ref[...]
        rot = a < 2
        scr[...] = x * jnp.where(rot, cos_ref[...], 1.0) + _rot_half(x) * jnp.where(rot, sin_ref[...], 0.0)
        for n, d in enumerate(DILATIONS):
            _permute_rows(o_ref.at[n], scr, d, BF16)

    tab = pl.BlockSpec((s_len, LANES), lambda a, cb: (0, 0))
    return pl.pallas_call(
        body, grid=(3, 4), name=name,
        in_specs=[pl.BlockSpec((s_len, LANES), lambda a, cb: (0, 10 + 4 * a + cb)), tab, tab],
        out_specs=pl.BlockSpec((None, 3, s_len, LANES), lambda a, cb: (a, 0, 0, cb)),
        out_shape=jax.ShapeDtypeStruct((3, 3, s_len, DL_W), BF16),
        scratch_shapes=[pltpu.VMEM((s_len, LANES), F32)], compiler_params=_cp())(u, cos, sin)


DL_UNROLL = 2


def _dl_masks(s_len, n, i):
    lane, row, lane_h = _head_masks()
    nb = jnp.where(n == 0, s_len // BLK, jnp.where(n == 1, s_len // (BLK * DILATIONS[1]),
                                                   s_len // (BLK * DILATIONS[2])))
    first = lax.rem(i, nb) == 0
    return lane_h, lane <= row, jnp.logical_and(lane >= row, jnp.logical_not(first))


def _dl_rows(i):
    cur = pl.ds(pl.multiple_of(i * BLK, BLK), BLK)
    prev = pl.ds(pl.multiple_of(jnp.maximum(i - 1, 0) * BLK, BLK), BLK)
    return cur, prev


def _dl_in_specs(s_len):
    return [pl.BlockSpec((None, None, s_len, LANES), functools.partial(lambda a, n, hp: (a, n, 0, hp), a))
            for a in range(3)]


def _dl_fwd(qkv, *, name):
    s_len = qkv.shape[2]

    def body(q_ref, k_ref, v_ref, o_ref, l_ref):
        def block(i):
            lane_h, valid_c, valid_p = _dl_masks(s_len, pl.program_id(0), i)
            cur, prev = _dl_rows(i)
            q = q_ref[cur, :] * 0.125
            kc, kp, vc, vp = k_ref[cur, :], k_ref[prev, :], v_ref[cur, :], v_ref[prev, :]
            outs, lses = [], []
            for a in range(2):
                qa = jnp.where(lane_h == (a == 0), q, 0.0).astype(BF16)
                sc = jnp.where(valid_c, _dot_nt(qa, kc), NEG_INF)
                sp = jnp.where(valid_p, _dot_nt(qa, kp), NEG_INF)
                m = jnp.maximum(jnp.max(sc, axis=1, keepdims=True), jnp.max(sp, axis=1, keepdims=True))
                pc, pp = jnp.exp(sc - m), jnp.exp(sp - m)
                den = jnp.sum(pc, axis=1, keepdims=True) + jnp.sum(pp, axis=1, keepdims=True)
                outs.append(_dot((pc / den).astype(BF16), vc) + _dot((pp / den).astype(BF16), vp))
                lses.append(jnp.broadcast_to(m + jnp.log(den), (BLK, LANES)))
            o_ref[cur, :] = jnp.where(lane_h, outs[0], outs[1])
            l_ref[cur, :] = jnp.where(lane_h, lses[0], lses[1])

        @pl.loop(0, s_len // BLK, step=DL_UNROLL)
        def _(i0):
            for t in range(DL_UNROLL):
                block(i0 + t)

    out = pl.BlockSpec((None, s_len, LANES), lambda n, hp: (n, 0, hp))
    return pl.pallas_call(
        body, grid=(3, 4), name=name, in_specs=_dl_in_specs(s_len), out_specs=[out, out],
        out_shape=[jax.ShapeDtypeStruct((3, s_len, DL_W), F32)] * 2, compiler_params=_cp())(qkv, qkv, qkv)


def _dl_mix(o_p, l_p, *, name):
    s_len = o_p.shape[1]

    def body(o_ref, l_ref, ob_ref, of_ref, lt_ref, o_scr, l_scr):
        n = pl.program_id(1)
        for k, d in enumerate(DILATIONS):
            @pl.when(n == k)
            def _(k=k, d=d):
                _unpermute_rows(o_scr.at[k], o_ref, d)
                _unpermute_rows(l_scr.at[k], l_ref, d)

        @pl.when(n == 2)
        def _():
            l0, l1, l2 = l_scr[0], l_scr[1], l_scr[2]
            m = jnp.maximum(jnp.maximum(l0, l1), l2)
            e0, e1, e2 = jnp.exp(l0 - m), jnp.exp(l1 - m), jnp.exp(l2 - m)
            den = e0 + e1 + e2
            o = (e0 / den) * o_scr[0] + (e1 / den) * o_scr[1] + (e2 / den) * o_scr[2]
            of_ref[...] = o
            ob_ref[...] = o.astype(BF16)
            lt_ref[...] = m + jnp.log(den)

    inb = pl.BlockSpec((None, s_len, LANES), lambda cb, n: (n, 0, cb))
    outb = pl.BlockSpec((s_len, LANES), lambda cb, n: (0, cb))
    return pl.pallas_call(
        body, grid=(4, 3), name=name, in_specs=[inb, inb], out_specs=[outb, outb, outb],
        out_shape=[jax.ShapeDtypeStruct((s_len, DL_W), BF16), jax.ShapeDtypeStruct((s_len, DL_W), F32),
                   jax.ShapeDtypeStruct((s_len, DL_W), F32)],
        scratch_shapes=[pltpu.VMEM((3, s_len, LANES), F32), pltpu.VMEM((3, s_len, LANES), F32)],
        compiler_params=_cp())(o_p, l_p)


def _dl_bwd_prep(dcat, o, lse, *, name):
    s_len = o.shape[0]

    def body(do_ref, o_ref, l_ref, dop_ref, st_ref, d_scr):
        n = pl.program_id(1)

        @pl.when(n == 0)
        def _():
            r0 = lax.broadcasted_iota(jnp.int32, (LANES, LANES), 0) // HD
            r1 = lax.broadcasted_iota(jnp.int32, (LANES, LANES), 1) // HD
            d_scr[...] = _dot_hilo(do_ref[...] * o_ref[...], (r0 == r1).astype(BF16))

        for k, d in enumerate(DILATIONS):
            @pl.when(n == k)
            def _(d=d):
                _permute_rows(dop_ref, do_ref, d, BF16)
                _permute_rows(st_ref.at[0], d_scr, d, F32)
                _permute_rows(st_ref.at[1], l_ref, d, F32)

    nat = lambda c0: pl.BlockSpec((s_len, LANES), lambda cb, n: (0, c0 + cb))
    return pl.pallas_call(
        body, grid=(4, 3), name=name, in_specs=[nat(4), nat(0), nat(0)],
        out_specs=[pl.BlockSpec((None, s_len, LANES), lambda cb, n: (n, 0, cb)),
                   pl.BlockSpec((2, None, s_len, LANES), lambda cb, n: (0, n, 0, cb))],
        out_shape=[jax.ShapeDtypeStruct((3, s_len, DL_W), BF16), jax.ShapeDtypeStruct((2, 3, s_len, DL_W), F32)],
        scratch_shapes=[pltpu.VMEM((s_len, LANES), F32)], compiler_params=_cp())(dcat, o, lse)


def _dl_bwd(qkv, dop, stats, *, name):
    s_len = qkv.shape[2]

    def body(q_ref, k_ref, v_ref, do_ref, st_ref, cur_ref, prev_ref):
        def block(i):
            lane_h, valid_c, valid_p = _dl_masks(s_len, pl.program_id(0), i)
            cur, prev = _dl_rows(i)
            qs = q_ref[cur, :] * 0.125
            kc, kp, vc, vp, dob = k_ref[cur, :], k_ref[prev, :], v_ref[cur, :], v_ref[prev, :], do_ref[cur, :]
            delta, lse = st_ref[0, cur, :], st_ref[1, cur, :]
            delta_r, lse_r = pltpu.roll(delta, HD, 1), pltpu.roll(lse, HD, 1)
            res = []
            for a in range(2):
                mine = lane_h == (a == 0)
                qa = jnp.where(mine, qs, 0.0).astype(BF16)
                doa = jnp.where(mine, dob, 0.0).astype(BF16)
                lse_a, delta_a = jnp.where(mine, lse, lse_r), jnp.where(mine, delta, delta_r)
                pc = jnp.where(valid_c, jnp.exp(_dot_nt(qa, kc) - lse_a), 0.0)
                pp = jnp.where(valid_p, jnp.exp(_dot_nt(qa, kp) - lse_a), 0.0)
                dsc = (pc * (_dot_nt(doa, vc) - delta_a)).astype(BF16)
                dsp = (pp * (_dot_nt(doa, vp) - delta_a)).astype(BF16)
                res.append(((_dot(dsc, kc) + _dot(dsp, kp)) * 0.125, _dot_tn(dsc, qs), _dot_tn(pc.astype(BF16), dob),
                            _dot_tn(dsp, qs), _dot_tn(pp.astype(BF16), dob)))
            pick = lambda t: jnp.where(lane_h, res[0][t], res[1][t])
            for t in range(3):
                cur_ref[t, cur, :] = pick(t)
            for t in range(2):
                prev_ref[t, cur, :] = pick(3 + t)

        @pl.loop(0, s_len // BLK, step=DL_UNROLL)
        def _(i0):
            for t in range(DL_UNROLL):
                block(i0 + t)

    return pl.pallas_call(
        body, grid=(3, 4), name=name,
        in_specs=_dl_in_specs(s_len) + [pl.BlockSpec((None, s_len, LANES), lambda n, hp: (n, 0, hp)),
                                        pl.BlockSpec((2, None, s_len, LANES), lambda n, hp: (0, n, 0, hp))],
        out_specs=[pl.BlockSpec((3, None, s_len, LANES), lambda n, hp: (0, n, 0, hp)),
                   pl.BlockSpec((2, None, s_len, LANES), lambda n, hp: (0, n, 0, hp))],
        out_shape=[jax.ShapeDtypeStruct((3, 3, s_len, DL_W), F32), jax.ShapeDtypeStruct((2, 3, s_len, DL_W), F32)],
        compiler_params=_cp(56))(qkv, qkv, qkv, dop, stats)


def _dl_bwd_finish(cur, prev, cos, sin, *, name):
    s_len = cur.shape[2]

    def body(c_ref, p_ref, cos_ref, sin_ref, o_ref, p_scr, u_scr, acc):
        a, n = pl.program_id(0), pl.program_id(2)
        has_prev = jnp.where(a > 0, 1.0, 0.0)
        p_scr[...] = c_ref[...]
        p_scr[0:s_len - BLK, :] += has_prev * p_ref[BLK:, :]
        for k, d in enumerate(DILATIONS):
            @pl.when(n == k)
            def _(k=k, d=d):
                if k == 0:
                    acc[...] = p_scr[...]
                else:
                    _unpermute_rows(u_scr, p_scr, d)
                    acc[...] += u_scr[...]

        @pl.when(n == 2)
        def _():
            dy = acc[...]
            rot = a < 2
            o_ref[...] = (dy * jnp.where(rot, cos_ref[...], 1.0)
                          + _rot_half(dy * jnp.where(rot, sin_ref[...], 0.0))).astype(BF16)

    tab = pl.BlockSpec((s_len, LANES), lambda a, cb, n: (0, 0))
    return pl.pallas_call(
        body, grid=(3, 4, 3), name=name,
        in_specs=[pl.BlockSpec((None, None, s_len, LANES), lambda a, cb, n: (a, n, 0, cb)),
                  pl.BlockSpec((None, None, s_len, LANES), lambda a, cb, n: (jnp.maximum(a - 1, 0), n, 0, cb)),
                  tab, tab],
        out_specs=pl.BlockSpec((s_len, LANES), lambda a, cb, n: (0, 4 * a + cb)),
        out_shape=jax.ShapeDtypeStruct((s_len, 3 * DL_W), BF16),
        scratch_shapes=[pltpu.VMEM((s_len, LANES), F32)] * 3, compiler_params=_cp())(cur, prev, cos, sin)


XA_T = 256


def _xa_probs(q, k):
    s = _dot_nt(q, k) * (X_HD ** -0.5)
    e = jnp.exp(s - jnp.max(s, axis=1, keepdims=True))
    return e / jnp.sum(e, axis=1, keepdims=True)


def _xa_fwd(q, k, v, *, name):
    s_len, d = q.shape
    nm = k.shape[0]

    def body(q_ref, k_ref, v_ref, o_ref):
        for h in range(X_HEADS):
            cs = slice(h * X_HD, (h + 1) * X_HD)
            p = _xa_probs(q_ref[:, cs], k_ref[:, cs])
            o_ref[:, cs] = _dot(p.astype(BF16), v_ref[:, cs]).astype(BF16)

    row = pl.BlockSpec((XA_T, d), lambda i: (i, 0))
    full = pl.BlockSpec((nm, d), lambda i: (0, 0))
    return pl.pallas_call(body, grid=(s_len // XA_T,), name=name, in_specs=[row, full, full], out_specs=row,
                          out_shape=jax.ShapeDtypeStruct((s_len, d), BF16), compiler_params=_cp())(q, k, v)


def _xa_bwd(q, k, v, do, *, name):
    s_len, d = q.shape
    nm = k.shape[0]

    def body(q_ref, k_ref, v_ref, do_ref, dq_ref, dk_ref, dv_ref):
        i = pl.program_id(0)
        for h in range(X_HEADS):
            cs = slice(h * X_HD, (h + 1) * X_HD)
            qh, kh, vh, doh = q_ref[:, cs], k_ref[:, cs], v_ref[:, cs], do_ref[:, cs]
            p = _xa_probs(qh, kh)
            dp = _dot_nt(doh, vh)
            ds = (p * (dp - jnp.sum(dp * p, axis=1, keepdims=True)) * (X_HD ** -0.5)).astype(BF16)
            dq_ref[:, cs] = _dot(ds, kh).astype(BF16)
            dkh, dvh = _dot_tn(ds, qh), _dot_tn(p.astype(BF16), doh)

            @pl.when(i == 0)
            def _(cs=cs, dkh=dkh, dvh=dvh):
                dk_ref[:, cs] = dkh
                dv_ref[:, cs] = dvh

            @pl.when(i > 0)
            def _(cs=cs, dkh=dkh, dvh=dvh):
                dk_ref[:, cs] += dkh
                dv_ref[:, cs] += dvh

    row = pl.BlockSpec((XA_T, d), lambda i: (i, 0))
    full = pl.BlockSpec((nm, d), lambda i: (0, 0))
    return pl.pallas_call(
        body, grid=(s_len // XA_T,), name=name, in_specs=[row, full, full, row], out_specs=[row, full, full],
        out_shape=[jax.ShapeDtypeStruct((s_len, d), BF16), jax.ShapeDtypeStruct((nm, d), F32),
                   jax.ShapeDtypeStruct((nm, d), F32)], compiler_params=_cp())(q, k, v, do)


FF_TM, FF_TN, FF_H = 512, 256, 8
GELU_K, GELU_C = 0.7978845608028654, 0.044715


def _ff_conv(e_ref, w_ref, b_ref, rows):
    return (w_ref[0:1, :] * e_ref[pl.ds(FF_H - 2, rows), :] + w_ref[1:2, :] * e_ref[pl.ds(FF_H - 1, rows), :]
            + w_ref[2:3, :] * e_ref[pl.ds(FF_H, rows), :] + b_ref[...])


def _ff_gate_fwd(up, conv_w, conv_b, *, name):
    s_len = up.shape[0]
    nj = D_FF // FF_TN

    def body(g_ref, v_ref, gp_ref, vp_ref, wg_ref, wv_ref, bg_ref, bv_ref, o_ref, eg, ev):
        i = pl.program_id(0)
        for e, cur, prev in ((eg, g_ref, gp_ref), (ev, v_ref, vp_ref)):
            e[0:FF_H, :] = jnp.where(i > 0, prev[...], 0.0)
            e[FF_H:, :] = cur[...]
        gate = _ff_conv(eg, wg_ref, bg_ref, FF_TM)
        val = _ff_conv(ev, wv_ref, bv_ref, FF_TM)
        t = jnp.tanh(GELU_K * (gate + GELU_C * gate * gate * gate))
        o_ref[...] = (0.5 * gate * (1.0 + t) * val).astype(BF16)

    cur = lambda c0: pl.BlockSpec((FF_TM, FF_TN), lambda i, j: (i, c0 + j))
    prev = lambda c0: pl.BlockSpec((FF_H, FF_TN), lambda i, j: (jnp.maximum(i * (FF_TM // FF_H) - 1, 0), c0 + j))
    par = lambda r, c0: pl.BlockSpec((r, FF_TN), lambda i, j: (0, c0 + j))
    return pl.pallas_call(
        body, grid=(s_len // FF_TM, nj), name=name,
        in_specs=[cur(0), cur(nj), prev(0), prev(nj), par(3, 0), par(3, nj), par(1, 0), par(1, nj)],
        out_specs=cur(0), out_shape=jax.ShapeDtypeStruct((s_len, D_FF), BF16),
        scratch_shapes=[pltpu.VMEM((FF_TM + FF_H, FF_TN), F32)] * 2, compiler_params=_cp())(
            up, up, up, up, conv_w, conv_w, conv_b, conv_b)


def _ff_gate_bwd(up, dact, conv_w, conv_b, *, name):
    s_len = up.shape[0]
    nj = D_FF // FF_TN
    last = s_len // FF_TM - 1
    ext = FF_TM + FF_H

    def body(g_ref, v_ref, gp_ref, vp_ref, gn_ref, vn_ref, da_ref, dan_ref, wg_ref, wv_ref, bg_ref, bv_ref,
             dg_ref, dv_ref, dw_ref, db_ref, eg, ev, sg, sv):
        i = pl.program_id(1)
        for e, cur, prev, nxt in ((eg, g_ref, gp_ref, gn_ref), (ev, v_ref, vp_ref, vn_ref)):
            e[0:FF_H, :] = jnp.where(i > 0, prev[...], 0.0)
            e[FF_H:FF_H + FF_TM, :] = cur[...]
            e[FF_H + FF_TM:, :] = nxt[...]
        gate = _ff_conv(eg, wg_ref, bg_ref, ext)
        val = _ff_conv(ev, wv_ref, bv_ref, ext)
        dact_e = jnp.concatenate([da_ref[...], jnp.where(i < last, dan_ref[...], 0.0)], axis=0)
        inner = GELU_K * (gate + GELU_C * gate * gate * gate)
        t = jnp.tanh(inner)
        gelu = 0.5 * gate * (1.0 + t)
        dgelu = 0.5 * (1.0 + t) + 0.5 * gate * (1.0 - t * t) * GELU_K * (1.0 + 3.0 * GELU_C * gate * gate)
        sg[...] = dact_e * val * dgelu
        sv[...] = dact_e * gelu
        for part, (s, e, w_ref, out) in enumerate(((sg, eg, wg_ref, dg_ref), (sv, ev, wv_ref, dv_ref))):
            out[...] = (w_ref[2:3, :] * s[pl.ds(0, FF_TM), :] + w_ref[1:2, :] * s[pl.ds(1, FF_TM), :]
                        + w_ref[0:1, :] * s[pl.ds(2, FF_TM), :]).astype(BF16)
            d0 = s[pl.ds(0, FF_TM), :]
            taps = [_rowsum8(d0 * e[pl.ds(FF_H - 2 + k, FF_TM), :]) for k in range(3)]
            _acc_out(dw_ref.at[part], i, jnp.concatenate(taps, axis=0))
            _acc_out(db_ref.at[part], i, _rowsum8(d0))

    cur = lambda c0: pl.BlockSpec((FF_TM, FF_TN), lambda j, i: (i, c0 + j))
    prev = lambda c0: pl.BlockSpec((FF_H, FF_TN), lambda j, i: (jnp.maximum(i * (FF_TM // FF_H) - 1, 0), c0 + j))
    nxt = lambda c0: pl.BlockSpec(
        (FF_H, FF_TN), lambda j, i: (jnp.minimum((i + 1) * (FF_TM // FF_H), s_len // FF_H - 1), c0 + j))
    par = lambda r, c0: pl.BlockSpec((r, FF_TN), lambda j, i: (0, c0 + j))
    return pl.pallas_call(
        body, grid=(nj, s_len // FF_TM), name=name,
        in_specs=[cur(0), cur(nj), prev(0), prev(nj), nxt(0), nxt(nj), cur(0), nxt(0),
                  par(3, 0), par(3, nj), par(1, 0), par(1, nj)],
        out_specs=[cur(0), cur(0), pl.BlockSpec((2, 24, FF_TN), lambda j, i: (0, 0, j)),
                   pl.BlockSpec((2, 8, FF_TN), lambda j, i: (0, 0, j))],
        out_shape=[jax.ShapeDtypeStruct((s_len, D_FF), BF16), jax.ShapeDtypeStruct((s_len, D_FF), BF16),
                   jax.ShapeDtypeStruct((2, 24, D_FF), F32), jax.ShapeDtypeStruct((2, 8, D_FF), F32)],
        scratch_shapes=[pltpu.VMEM((FF_TM + 2 * FF_H, FF_TN), F32)] * 2 + [pltpu.VMEM((ext, FF_TN), F32)] * 2,
        compiler_params=_cp())(up, up, up, up, up, up, dact, dact, conv_w, conv_w, conv_b, conv_b)


def _place():
    x, y, c = lax.axis_index("x"), lax.axis_index("y"), lax.axis_index("c")
    return x, y, c, [(1 - x, y), (x, 1 - y), (1 - x, 1 - y)]


def _remote(src, dst, send_sem, recv_sem, dev):
    return pltpu.make_async_remote_copy(src_ref=src, dst_ref=dst, send_sem=send_sem, recv_sem=recv_sem,
                                        device_id=dev, device_id_type=MESH)


_ANY = pl.BlockSpec(memory_space=pl.ANY)


def _gather_weights(pack, *, name):
    def body(p_ref, out_ref, send_sems, recv_sems, local_sem):
        x, y, c, chips = _place()
        sibling = (x, y, 1 - c)
        mine = pltpu.make_async_copy(p_ref, out_ref.at[2 * x + y], local_sem)
        mine.start()
        first = [_remote(p_ref.at[c], out_ref.at[2 * x + y, c], send_sems.at[k], recv_sems.at[k], (px, py, c))
                 for k, (px, py) in enumerate(chips)]
        for cp in first:
            cp.start()
        passed = []
        for k, (px, py) in enumerate(chips):
            slab = out_ref.at[2 * px + py, c]
            _remote(slab, slab, send_sems.at[k], recv_sems.at[k], (px, py, c)).wait_recv()
            passed.append(_remote(slab, slab, send_sems.at[3 + k], recv_sems.at[3 + k], sibling))
            passed[-1].start()
        for k, (px, py) in enumerate(chips):
            slab = out_ref.at[2 * px + py, 1 - c]
            _remote(slab, slab, send_sems.at[3 + k], recv_sems.at[3 + k], sibling).wait_recv()
        for cp in first + passed:
            cp.wait_send()
        mine.wait()

    return pl.pallas_call(
        body, name=name, in_specs=[_ANY], out_specs=_ANY,
        out_shape=jax.ShapeDtypeStruct((4,) + pack.shape, pack.dtype),
        scratch_shapes=[pltpu.SemaphoreType.DMA((6,)), pltpu.SemaphoreType.DMA((6,)), pltpu.SemaphoreType.DMA],
        compiler_params=_cp(16))(pack)


def _swap_other_layer(gw, *, name):
    def body(g_ref, out_ref, send_sem, recv_sem):
        x, y, c, _ = _place()
        cp = _remote(g_ref.at[1 - c], out_ref, send_sem, recv_sem, (x, y, 1 - c))
        cp.start()
        cp.wait()

    return pl.pallas_call(
        body, name=name, in_specs=[_ANY], out_specs=_ANY, out_shape=jax.ShapeDtypeStruct(gw.shape[1:], gw.dtype),
        scratch_shapes=[pltpu.SemaphoreType.DMA, pltpu.SemaphoreType.DMA],
        compiler_params=_cp(16))(gw)


def _chip_sum(gw, got, c_arr, *, name):
    _, nchip, rl, d = gw.shape
    tr = 512

    def body(c_ref, a_ref, b_ref, o32_ref, o16_ref):
        s = a_ref[...] + b_ref[...]
        o32_ref[...] = s
        o16_ref[...] = s.astype(BF16)

    blk = pl.BlockSpec((None, tr, d), lambda j, i, c_ref: (j, i, 0))
    return pl.pallas_call(
        body, name=name,
        grid_spec=pltpu.PrefetchScalarGridSpec(
            num_scalar_prefetch=1, grid=(nchip, rl // tr),
            in_specs=[pl.BlockSpec((None, None, tr, d), lambda j, i, c_ref: (c_ref[0], j, i, 0)), blk],
            out_specs=[blk, blk]),
        out_shape=[jax.ShapeDtypeStruct((nchip, rl, d), F32), jax.ShapeDtypeStruct((nchip, rl, d), BF16)],
        compiler_params=_cp())(c_arr, gw, got)


def _scatter_chip_sums(s16, *, name):
    def body(s_ref, out_ref, send_sems, recv_sems):
        x, y, c, chips = _place()
        sends = [_remote(s_ref.at[2 * px + py], out_ref.at[k], send_sems.at[k], recv_sems.at[k], (px, py, c))
                 for k, (px, py) in enumerate(chips)]
        for cp in sends:
            cp.start()
        for cp in sends:
            cp.wait()

    return pl.pallas_call(
        body, name=name, in_specs=[_ANY], out_specs=_ANY,
        out_shape=jax.ShapeDtypeStruct((3,) + s16.shape[1:], s16.dtype),
        scratch_shapes=[pltpu.SemaphoreType.DMA((3,)), pltpu.SemaphoreType.DMA((3,))],
        compiler_params=_cp(16))(s16)


def _mesh_sum(s32, got, j_arr, *, name):
    _, rl, d = s32.shape
    tr = 512

    def body(j_ref, a_ref, b_ref, o_ref):
        o_ref[...] = ((a_ref[...] + b_ref[0].astype(F32)) + b_ref[1].astype(F32)) + b_ref[2].astype(F32)

    return pl.pallas_call(
        body, name=name,
        grid_spec=pltpu.PrefetchScalarGridSpec(
            num_scalar_prefetch=1, grid=(rl // tr,),
            in_specs=[pl.BlockSpec((None, tr, d), lambda i, j_ref: (j_ref[0], i, 0)),
                      pl.BlockSpec((3, tr, d), lambda i, j_ref: (0, i, 0))],
            out_specs=pl.BlockSpec((tr, d), lambda i, j_ref: (i, 0))),
        out_shape=jax.ShapeDtypeStruct((rl, d), F32), compiler_params=_cp())(j_arr, s32, got)


def _share_layers(ghalf, *, name):
    def body(g_ref, out_ref, send_sem, recv_sem, local_sem):
        x, y, c, _ = _place()
        mine = pltpu.make_async_copy(g_ref, out_ref.at[c], local_sem)
        mine.start()
        cp = _remote(g_ref, out_ref.at[c], send_sem, recv_sem, (x, y, 1 - c))
        cp.start()
        _remote(g_ref, out_ref.at[1 - c], send_sem, recv_sem, (x, y, 1 - c)).wait_recv()
        cp.wait_send()
        mine.wait()

    return pl.pallas_call(
        body, name=name, in_specs=[_ANY], out_specs=_ANY,
        out_shape=jax.ShapeDtypeStruct((2,) + ghalf.shape, ghalf.dtype),
        scratch_shapes=[pltpu.SemaphoreType.DMA, pltpu.SemaphoreType.DMA, pltpu.SemaphoreType.DMA],
        compiler_params=_cp(16))(ghalf)


def _all_reduce_small(vec, *, name):
    rows, d = vec.shape

    def body(x_ref, o_ref, gat, send_sems, recv_sems, local_sem):
        x, y, c, chips = _place()
        me, sibling = (x, y, c), (x, y, 1 - c)

        def slot(px, py, pc):
            return gat.at[4 * px + 2 * py + pc]

        def copy(k, block, to, src=None):
            return _remote(slot(*block) if src is None else src, slot(*block), send_sems.at[k], recv_sems.at[k], to)

        mine = pltpu.make_async_copy(x_ref, slot(*me), local_sem)
        mine.start()
        first = [copy(0, me, sibling, src=x_ref)]
        first += [copy(1 + j, me, (*chip, c), src=x_ref) for j, chip in enumerate(chips)]
        for cp in first:
            cp.start()
        passed = [copy(4 + j, (*chip, c), sibling) for j, chip in enumerate(chips)]
        for j, chip in enumerate(chips):
            copy(1 + j, (*chip, c), me).wait_recv()
            passed[j].start()
        copy(0, sibling, me).wait_recv()
        for j, chip in enumerate(chips):
            copy(4 + j, (*chip, 1 - c), me).wait_recv()
        for cp in first + passed:
            cp.wait_send()
        mine.wait()
        acc = gat[0]
        for dev in range(1, 8):
            acc = acc + gat[dev]
        o_ref[...] = acc

    vm = pl.BlockSpec(memory_space=pltpu.VMEM)
    return pl.pallas_call(
        body, name=name, in_specs=[vm], out_specs=vm, out_shape=jax.ShapeDtypeStruct((rows, d), F32),
        scratch_shapes=[pltpu.VMEM((8, rows, d), F32), pltpu.SemaphoreType.DMA((7,)), pltpu.SemaphoreType.DMA((7,)),
                        pltpu.SemaphoreType.DMA],
        compiler_params=_cp(32))(vec)


COL_SHARDED = ("w_in", "ffn_w_up")


def _to_pack_rows(name, shard):
    return shard.reshape(-1, D_MODEL)


def _full_from_blocks(name, blocks):
    rows = blocks.shape[1]
    if name in COL_SHARDED:
        return blocks.reshape(4, D_MODEL, rows).transpose(1, 0, 2).reshape(D_MODEL, 4 * rows)
    return blocks.reshape(4 * rows, D_MODEL)


def _blocks_from_full(name, full):
    if name in COL_SHARDED:
        cols = full.shape[1] // 4
        return full.reshape(D_MODEL, 4, cols).transpose(1, 0, 2).reshape(4, cols, D_MODEL)
    return full.reshape(4, full.shape[0] // 4, D_MODEL)


def _row(v):
    return v.reshape(1, -1)


SMALL = (("mix_norm_pre", (1024,), None), ("cv_w", (31, 256), 1), ("cv_b", (256,), None), ("cv_ln_g", (256,), None),
         ("cv_ln_b", (256,), None), ("cv_pw_w", (256, 256), 0), ("cv_pw_b", (256,), None),
         ("mix_norm_post", (1024,), None), ("x_norm_pre", (1024,), None), ("mem_norm", (1024,), None),
         ("x_norm_post", (1024,), None), ("ffn_norm_pre", (1024,), None), ("ffn_conv_w", (3, 5632), 1),
         ("ffn_conv_b", (5632,), None), ("ffn_norm_post", (1024,), None))
BIG = tuple(n for n, _ in PACK_ROWS)
WEIGHT_ORDER = ("mix_norm_pre", "w_in", "cv_w", "cv_b", "cv_ln_g", "cv_ln_b", "cv_pw_w", "cv_pw_b", "w_out",
                "mix_norm_post", "x_norm_pre", "mem_norm", "x_wq", "x_wk", "x_wv", "x_wo", "x_norm_post",
                "ffn_norm_pre", "ffn_w_up", "ffn_conv_w", "ffn_conv_b", "ffn_w_down", "ffn_norm_post")


def _flat_rows(parts):
    v = jnp.concatenate([p.reshape(-1) for p in parts])
    rows = -(-v.shape[0] // (8 * D_MODEL)) * 8
    return jnp.pad(v, (0, rows * D_MODEL - v.shape[0])).reshape(rows, D_MODEL)


def _layer_fwd(h0, mem, p, cos, sin, tag):
    sv = {"h0": h0}
    n1, u = _rms_mm(h0, _row(p["mix_norm_pre"]), p["w_in"], tm=512, tn=1408, out_dtype=F32, name=f"mix_in{tag}")
    a_out = _sb_fwd(u, name=f"sb_fwd{tag}")
    b_out, c = _cv_fwd(u, p["cv_w"], _row(p["cv_b"]), _row(p["cv_ln_g"]), _row(p["cv_ln_b"]),
                       p["cv_pw_w"].astype(BF16), _row(p["cv_pw_b"]), name=f"cv_fwd{tag}")
    qkv = _rope_perm(u, cos, sin, name=f"rope_perm{tag}")
    o_p, l_p = _dl_fwd(qkv, name=f"dl_fwd{tag}")
    c_out, o_dl, lse = _dl_mix(o_p, l_p, name=f"dl_mix{tag}")
    cat = jnp.concatenate([a_out, b_out, c_out], axis=1)
    y1, h1 = _mm_post(cat, p["w_out"], h0, _row(p["mix_norm_post"]), tm=256, name=f"mix_out{tag}")
    sv.update(n1=n1, u=u, c=c, qkv=qkv, o_dl=o_dl, lse=lse, cat=cat, y1=y1, h1=h1)

    n2, q = _rms_mm(h1, _row(p["x_norm_pre"]), p["x_wq"], tm=512, tn=1024, out_dtype=BF16, name=f"xa_q{tag}")
    wkv = jnp.concatenate([p["x_wk"], p["x_wv"]], axis=1)
    mem_n, kv = _rms_mm(mem, _row(p["mem_norm"]), wkv, tm=mem.shape[0], tn=1024, out_dtype=BF16, name=f"xa_kv{tag}")
    k, v = kv[:, :D_MODEL], kv[:, D_MODEL:]
    o_x = _xa_fwd(q, k, v, name=f"xa_fwd{tag}")
    y2, h2 = _mm_post(o_x, p["x_wo"], h1, _row(p["x_norm_post"]), tm=256, name=f"xa_out{tag}")
    sv.update(n2=n2, q=q, mem_n=mem_n, k=k, v=v, o_x=o_x, y2=y2, h2=h2, wkv=wkv)

    n3, up = _rms_mm(h2, _row(p["ffn_norm_pre"]), p["ffn_w_up"], tm=512, tn=1408, out_dtype=F32, name=f"ffn_up{tag}")
    act = _ff_gate_fwd(up, p["ffn_conv_w"], _row(p["ffn_conv_b"]), name=f"ffn_gate{tag}")
    y3, h3 = _mm_post(act, p["ffn_w_down"], h2, _row(p["ffn_norm_post"]), tm=256, name=f"ffn_down{tag}")
    sv.update(n3=n3, up=up, act=act, y3=y3)
    return h3, sv


def _layer_bwd(dh3, mem, p, sv, cos, sin, tag):
    g = {}
    s8 = lambda part: part.sum(axis=0)

    dy3, dgp = _rms_bwd(sv["y3"], _row(p["ffn_norm_post"]), dh3, None, out_dtype=BF16, tm=256, name=f"ffn_post_b{tag}")
    g["ffn_norm_post"] = s8(dgp)
    dact = _mm_nt(dy3, p["ffn_w_down"], tm=512, tn=1408, out_dtype=F32, name=f"ffn_down_bx{tag}")
    g["ffn_w_down"] = _mm_tn(sv["act"], dy3, tk=1408, tn=1024, tm=512, name=f"ffn_down_bw{tag}")
    dgu, dvu, dcw, dcb = _ff_gate_bwd(sv["up"], dact, p["ffn_conv_w"], _row(p["ffn_conv_b"]), name=f"ffn_gate_b{tag}")
    g["ffn_conv_w"] = jnp.concatenate([dcw[0], dcw[1]], axis=1).reshape(3, 8, 2 * D_FF).sum(axis=1)
    g["ffn_conv_b"] = jnp.concatenate([dcb[0], dcb[1]], axis=1).sum(axis=0)
    dup = jnp.concatenate([dgu, dvu], axis=1)
    dn3 = _mm_nt(dup, p["ffn_w_up"], tm=256, tn=512, out_dtype=F32, name=f"ffn_up_bx{tag}")
    g["ffn_w_up"] = _mm_tn(sv["n3"], dup, tk=512, tn=1408, tm=512, name=f"ffn_up_bw{tag}")
    dh2, dgp = _rms_bwd(sv["h2"], _row(p["ffn_norm_pre"]), dn3, dh3, out_dtype=F32, tm=256, name=f"ffn_pre_b{tag}")
    g["ffn_norm_pre"] = s8(dgp)

    dy2, dgp = _rms_bwd(sv["y2"], _row(p["x_norm_post"]), dh2, None, out_dtype=BF16, tm=256, name=f"xa_post_b{tag}")
    g["x_norm_post"] = s8(dgp)
    do_x = _mm_nt(dy2, p["x_wo"], tm=512, tn=1024, out_dtype=BF16, name=f"xa_out_bx{tag}")
    g["x_wo"] = _mm_tn(sv["o_x"], dy2, tk=512, tn=1024, tm=512, name=f"xa_out_bw{tag}")
    dq, dk, dv = _xa_bwd(sv["q"], sv["k"], sv["v"], do_x, name=f"xa_bwd{tag}")
    dn2 = _mm_nt(dq, p["x_wq"], tm=512, tn=1024, out_dtype=F32, name=f"xa_q_bx{tag}")
    g["x_wq"] = _mm_tn(sv["n2"], dq, tk=512, tn=1024, tm=512, name=f"xa_q_bw{tag}")
    dkv = jnp.concatenate([dk, dv], axis=1).astype(BF16)
    nm = mem.shape[0]
    dmem_n = _mm_nt(dkv, sv["wkv"], tm=nm, tn=1024, out_dtype=F32, name=f"xa_kv_bx{tag}")
    dwkv = _mm_tn(sv["mem_n"], dkv, tk=512, tn=2048, tm=nm, name=f"xa_kv_bw{tag}")
    g["x_wk"], g["x_wv"] = dwkv[:, :D_MODEL], dwkv[:, D_MODEL:]
    _, dgp = _rms_bwd(mem, _row(p["mem_norm"]), dmem_n, None, out_dtype=BF16, tm=nm, name=f"xa_mem_b{tag}")
    g["mem_norm"] = s8(dgp)
    dh1, dgp = _rms_bwd(sv["h1"], _row(p["x_norm_pre"]), dn2, dh2, out_dtype=F32, tm=256, name=f"xa_pre_b{tag}")
    g["x_norm_pre"] = s8(dgp)

    dy1, dgp = _rms_bwd(sv["y1"], _row(p["mix_norm_post"]), dh1, None, out_dtype=BF16, tm=256, name=f"mix_post_b{tag}")
    g["mix_norm_post"] = s8(dgp)
    dcat = _mm_nt(dy1, p["w_out"], tm=512, tn=1024, out_dtype=F32, name=f"mix_out_bx{tag}")
    g["w_out"] = _mm_tn(sv["cat"], dy1, tk=512, tn=1024, tm=512, name=f"mix_out_bw{tag}")
    u = sv["u"]
    dq_sb, dk_sb, dv_sb = _sb_bwd(u, dcat, name=f"sb_bwd{tag}")
    pw_b16 = p["cv_pw_w"].astype(BF16)
    dc, dpw, vec = _cv_bwd_local(sv["c"], dcat, _row(p["cv_ln_g"]), _row(p["cv_ln_b"]), pw_b16, name=f"cv_bwd_a{tag}")
    g["cv_pw_w"] = dpw
    vec = vec.reshape(3, 8, CV_W).sum(axis=1)
    g["cv_pw_b"], g["cv_ln_g"], g["cv_ln_b"] = vec[0], vec[1], vec[2]
    du_cv, dcw, dcb = _cv_bwd_conv(u, dc, p["cv_w"], name=f"cv_bwd_b{tag}")
    g["cv_w"] = dcw.reshape(CV_K, 8, CV_W).sum(axis=1)
    g["cv_b"] = dcb.sum(axis=0)
    dop, stats = _dl_bwd_prep(dcat, sv["o_dl"], sv["lse"], name=f"dl_prep_b{tag}")
    cur, prev = _dl_bwd(sv["qkv"], dop, stats, name=f"dl_bwd{tag}")
    du_dl = _dl_bwd_finish(cur, prev, cos, sin, name=f"dl_fin_b{tag}")
    du = jnp.concatenate([dq_sb.astype(BF16), dk_sb.astype(BF16), dv_sb.astype(BF16), du_cv, du_dl], axis=1)
    dn1 = _mm_nt(du, p["w_in"], tm=512, tn=512, out_dtype=F32, name=f"mix_in_bx{tag}")
    g["w_in"] = _mm_tn(sv["n1"], du, tk=512, tn=1408, tm=512, name=f"mix_in_bw{tag}")
    dh0, dgp = _rms_bwd(sv["h0"], _row(p["mix_norm_pre"]), dn1, dh1, out_dtype=F32, tm=256, name=f"mix_pre_b{tag}")
    g["mix_norm_pre"] = s8(dgp)
    return dh0, g


def _step(x, mem, positions, loss_target, w, m, v):
    depth = w["w_in"].shape[0]
    xi, yi, ci = lax.axis_index("x"), lax.axis_index("y"), lax.axis_index("c")
    chip = 2 * xi + yi
    h = x[0]
    mem0 = mem[0]
    s_len = h.shape[0]

    pack = jnp.stack([jnp.concatenate([_to_pack_rows(n, w[n][l]) for n in BIG], axis=0) for l in range(depth)])
    gathered = _gather_weights(pack.astype(BF16), name="gather_weights")
    params = []
    for l in range(depth):
        p, off = {}, 0
        for n, rows in PACK_ROWS:
            p[n] = _full_from_blocks(n, gathered[:, l, off:off + rows, :])
            off += rows
        for n, _, _ in SMALL:
            p[n] = w[n][l]
        params.append(p)
    small_w = []
    for l in range(depth):
        for n, shape, axis in SMALL:
            if axis is not None:
                full = jnp.zeros(shape, F32)
                full = lax.dynamic_update_slice_in_dim(full, w[n][l], chip * w[n][l].shape[axis], axis)
                small_w.append(full * jnp.where(ci == 0, 1.0, 0.0))
    small_w_sum = _all_reduce_small(_flat_rows(small_w), name="gather_small_weights")
    off = 0
    for l in range(depth):
        for n, shape, axis in SMALL:
            if axis is not None:
                size = int(np.prod(shape))
                params[l][n] = small_w_sum.reshape(-1)[off:off + size].reshape(shape)
                off += size

    inv_freq = ROPE_THETA ** (-jnp.arange(HD // 2, dtype=F32) / (HD // 2))
    cos, sin = _rope_tables(positions.reshape(s_len, 1), jnp.tile(inv_freq, 4).reshape(1, LANES), name="rope_tables")

    saved = []
    for l in range(depth):
        h, sv = _layer_fwd(h, mem0, params[l], cos, sin, f"_l{l}")
        saved.append(sv)
    dh, sq = _loss_grad(h, loss_target[0], tm=256, name="loss_grad")
    loss = lax.psum(0.5 * jnp.sum(sq) / D_MODEL, ("x", "y", "c"))
    grads = [None] * depth
    for l in reversed(range(depth)):
        dh, grads[l] = _layer_bwd(dh, mem0, params[l], saved[l], cos, sin, f"_l{l}")
    grad_x = dh[None]

    gw = jnp.stack([jnp.concatenate([_blocks_from_full(n, grads[l][n]) for n in BIG], axis=1) for l in range(depth)])
    c_arr, j_arr = jnp.reshape(ci, (1,)).astype(jnp.int32), jnp.reshape(chip, (1,)).astype(jnp.int32)
    got = _swap_other_layer(gw, name="rs_swap_layers")
    s32, s16 = _chip_sum(gw, got, c_arr, name="rs_chip_sum")
    got16 = _scatter_chip_sums(s16, name="rs_scatter")
    ghalf = _mesh_sum(s32, got16, j_arr, name="rs_mesh_sum")
    gfull = _share_layers(ghalf, name="rs_share_layers")

    out_g, out_d, out_m, out_v = {}, {}, {}, {}
    off = 0
    for n, rows in PACK_ROWS:
        shard_shape = w[n].shape
        g_n = gfull[:, off:off + rows, :].reshape(shard_shape)
        off += rows
        flat = lambda a: a.reshape(-1, shard_shape[-1])
        d_n, m_n, v_n = _adamw(flat(w[n]), flat(g_n), flat(m[n]), flat(v[n]), name=f"adamw_{n}")
        out_g[n], out_d[n], out_m[n], out_v[n] = g_n, d_n.reshape(shard_shape), m_n.reshape(shard_shape), v_n.reshape(shard_shape)

    g_small = _all_reduce_small(_flat_rows([grads[l][n] for l in range(depth) for n, _, _ in SMALL]),
                                name="all_reduce_small_grads").reshape(-1)
    local_g, off = {}, 0
    for l in range(depth):
        for n, shape, axis in SMALL:
            size = int(np.prod(shape))
            full = g_small[off:off + size].reshape(shape)
            off += size
            if axis is not None:
                blk = w[n].shape[1 + axis]
                full = lax.dynamic_slice_in_dim(full, chip * blk, blk, axis)
            local_g.setdefault(n, []).append(full)
    names = [n for n, _, _ in SMALL]
    g_loc = {n: jnp.stack(local_g[n]) for n in names}
    d_s, m_s, v_s = _adamw(_flat_rows([w[n] for n in names]), _flat_rows([g_loc[n] for n in names]),
                           _flat_rows([m[n] for n in names]), _flat_rows([v[n] for n in names]), name="adamw_small")
    off = 0
    for n in names:
        size = int(np.prod(w[n].shape))
        take = lambda a: a.reshape(-1)[off:off + size].reshape(w[n].shape)
        out_g[n], out_d[n], out_m[n], out_v[n] = g_loc[n], take(d_s), take(m_s), take(v_s)
        off += size

    outs = [loss, grad_x]
    for group in (out_g, out_d, out_m, out_v):
        outs += [group[n] for n in WEIGHT_ORDER]
    return tuple(outs)


def kernel(x, mem, positions, mix_norm_pre, w_in, cv_w, cv_b, cv_ln_g, cv_ln_b, cv_pw_w, cv_pw_b, w_out, mix_norm_post, x_norm_pre, mem_norm, x_wq, x_wk, x_wv, x_wo, x_norm_post, ffn_norm_pre, ffn_w_up, ffn_conv_w, ffn_conv_b, ffn_w_down, ffn_norm_post, loss_target, m_mix_norm_pre, m_w_in, m_cv_w, m_cv_b, m_cv_ln_g, m_cv_ln_b, m_cv_pw_w, m_cv_pw_b, m_w_out, m_mix_norm_post, m_x_norm_pre, m_mem_norm, m_x_wq, m_x_wk, m_x_wv, m_x_wo, m_x_norm_post, m_ffn_norm_pre, m_ffn_w_up, m_ffn_conv_w, m_ffn_conv_b, m_ffn_w_down, m_ffn_norm_post, v_mix_norm_pre, v_w_in, v_cv_w, v_cv_b, v_cv_ln_g, v_cv_ln_b, v_cv_pw_w, v_cv_pw_b, v_w_out, v_mix_norm_post, v_x_norm_pre, v_mem_norm, v_x_wq, v_x_wk, v_x_wv, v_x_wo, v_x_norm_post, v_ffn_norm_pre, v_ffn_w_up, v_ffn_conv_w, v_ffn_conv_b, v_ffn_w_down, v_ffn_norm_post):
    w = dict(zip(WEIGHT_ORDER, (mix_norm_pre, w_in, cv_w, cv_b, cv_ln_g, cv_ln_b, cv_pw_w, cv_pw_b, w_out, mix_norm_post, x_norm_pre, mem_norm, x_wq, x_wk, x_wv, x_wo, x_norm_post, ffn_norm_pre, ffn_w_up, ffn_conv_w, ffn_conv_b, ffn_w_down, ffn_norm_post)))
    m = dict(zip(WEIGHT_ORDER, (m_mix_norm_pre, m_w_in, m_cv_w, m_cv_b, m_cv_ln_g, m_cv_ln_b, m_cv_pw_w, m_cv_pw_b, m_w_out, m_mix_norm_post, m_x_norm_pre, m_mem_norm, m_x_wq, m_x_wk, m_x_wv, m_x_wo, m_x_norm_post, m_ffn_norm_pre, m_ffn_w_up, m_ffn_conv_w, m_ffn_conv_b, m_ffn_w_down, m_ffn_norm_post)))
    v = dict(zip(WEIGHT_ORDER, (v_mix_norm_pre, v_w_in, v_cv_w, v_cv_b, v_cv_ln_g, v_cv_ln_b, v_cv_pw_w, v_cv_pw_b, v_w_out, v_mix_norm_post, v_x_norm_pre, v_mem_norm, v_x_wq, v_x_wk, v_x_wv, v_x_wo, v_x_norm_post, v_ffn_norm_pre, v_ffn_w_up, v_ffn_conv_w, v_ffn_conv_b, v_ffn_w_down, v_ffn_norm_post)))
    return _step(x, mem, positions, loss_target, w, m, v)
```

```python
import functools

import jax
import jax.numpy as jnp
import numpy as np
from jax import lax
from jax.experimental import pallas as pl
from jax.experimental.pallas import tpu as pltpu

F32, BF16 = jnp.float32, jnp.bfloat16
MESH = pl.DeviceIdType.MESH
EPS = 1e-6
LANES = 128
BLK = 128
HD = 64
D_MODEL = 1024
D_FF = 2816
SB_W, CV_W, DL_W = 256, 256, 512
CV_K = 31
ROPE_THETA = 10000.0
DILATIONS = (1, 4, 16)
X_HEADS, X_HD = 4, 256
ADAM_LR, ADAM_B1, ADAM_B2, ADAM_EPS, ADAM_WD, ADAM_STEP = 0.001, 0.9, 0.999, 1e-08, 0.01, 10
NEG_INF = float("-inf")
MIB = 1 << 20

PACK_ROWS = (("w_in", 704), ("w_out", 256), ("x_wq", 256), ("x_wk", 256), ("x_wv", 256), ("x_wo", 256),
             ("ffn_w_up", 1408), ("ffn_w_down", 704))
PACK_RL = sum(r for _, r in PACK_ROWS)


def _cp(vmem_mb=48):
    return pltpu.CompilerParams(vmem_limit_bytes=vmem_mb * MIB)


def _dot(a, b):
    return jnp.dot(a, b, preferred_element_type=F32)


def _dot_nt(a, b):
    return lax.dot_general(a, b, (((1,), (1,)), ((), ())), preferred_element_type=F32)


def _dot_tn(a, b):
    return lax.dot_general(a, b, (((0,), (0,)), ((), ())), preferred_element_type=F32)


def _dot_hilo(x, m):
    hi = x.astype(BF16)
    lo = (x - hi.astype(F32)).astype(BF16)
    return _dot(hi, m) + _dot(lo, m)


def _rowsum8(x):
    t, c = x.shape
    return x.reshape(t // 8, 8, c).sum(axis=0)


def _acc_out(ref, i, val):
    @pl.when(i == 0)
    def _():
        ref[...] = val

    @pl.when(i > 0)
    def _():
        ref[...] += val


def _tile(n, cap, mult=8):
    t = min(n, cap)
    while n % t or t % mult:
        t -= 1
    return t


def _rms_mm(x, g, w, *, tm, tn, out_dtype, name):
    m, d = x.shape
    n_out = w.shape[1]

    def body(x_ref, g_ref, w_ref, n_ref, o_ref):
        @pl.when(pl.program_id(1) == 0)
        def _():
            xv = x_ref[...]
            r = lax.rsqrt(jnp.mean(xv * xv, axis=-1, keepdims=True) + EPS)
            n_ref[...] = (xv * r * g_ref[...]).astype(BF16)

        o_ref[...] = _dot(n_ref[...], w_ref[...]).astype(out_dtype)

    return pl.pallas_call(
        body, grid=(m // tm, n_out // tn), name=name,
        in_specs=[pl.BlockSpec((tm, d), lambda i, j: (i, 0)), pl.BlockSpec((1, d), lambda i, j: (0, 0)),
                  pl.BlockSpec((d, tn), lambda i, j: (0, j))],
        out_specs=[pl.BlockSpec((tm, d), lambda i, j: (i, 0)), pl.BlockSpec((tm, tn), lambda i, j: (i, j))],
        out_shape=[jax.ShapeDtypeStruct((m, d), BF16), jax.ShapeDtypeStruct((m, n_out), out_dtype)],
        compiler_params=_cp())(x, g, w)


def _mm_post(a, w, h, g, *, tm, name):
    m, k = a.shape
    d = w.shape[1]

    def body(a_ref, w_ref, h_ref, g_ref, y_ref, ho_ref):
        y = _dot(a_ref[...], w_ref[...])
        y_ref[...] = y
        r = lax.rsqrt(jnp.mean(y * y, axis=-1, keepdims=True) + EPS)
        ho_ref[...] = h_ref[...] + y * r * g_ref[...]

    return pl.pallas_call(
        body, grid=(m // tm,), name=name,
        in_specs=[pl.BlockSpec((tm, k), lambda i: (i, 0)), pl.BlockSpec((k, d), lambda i: (0, 0)),
                  pl.BlockSpec((tm, d), lambda i: (i, 0)), pl.BlockSpec((1, d), lambda i: (0, 0))],
        out_specs=[pl.BlockSpec((tm, d), lambda i: (i, 0)), pl.BlockSpec((tm, d), lambda i: (i, 0))],
        out_shape=[jax.ShapeDtypeStruct((m, d), F32), jax.ShapeDtypeStruct((m, d), F32)],
        compiler_params=_cp())(a, w, h, g)


def _mm_nt(a, w, *, tm, tn, out_dtype, name):
    m, k = a.shape
    n_out = w.shape[0]

    def body(a_ref, w_ref, o_ref):
        o_ref[...] = _dot_nt(a_ref[...], w_ref[...]).astype(out_dtype)

    return pl.pallas_call(
        body, grid=(m // tm, n_out // tn), name=name,
        in_specs=[pl.BlockSpec((tm, k), lambda i, j: (i, 0)), pl.BlockSpec((tn, k), lambda i, j: (j, 0))],
        out_specs=pl.BlockSpec((tm, tn), lambda i, j: (i, j)),
        out_shape=jax.ShapeDtypeStruct((m, n_out), out_dtype),
        compiler_params=_cp())(a, w)


def _mm_tn(x, dy, *, tk, tn, tm, name):
    m, k = x.shape
    n_out = dy.shape[1]

    def body(x_ref, d_ref, o_ref):
        _acc_out(o_ref, pl.program_id(2), _dot_tn(x_ref[...], d_ref[...]))

    return pl.pallas_call(
        body, grid=(k // tk, n_out // tn, m // tm), name=name,
        in_specs=[pl.BlockSpec((tm, tk), lambda a, b, c: (c, a)), pl.BlockSpec((tm, tn), lambda a, b, c: (c, b))],
        out_specs=pl.BlockSpec((tk, tn), lambda a, b, c: (a, b)),
        out_shape=jax.ShapeDtypeStruct((k, n_out), F32),
        compiler_params=_cp())(x, dy)


def _rms_bwd(x, g, dout, res, *, out_dtype, tm, name):
    m, d = x.shape
    has_res = res is not None

    def body(*refs):
        if has_res:
            x_ref, g_ref, d_ref, r_ref, dx_ref, dg_ref = refs
        else:
            x_ref, g_ref, d_ref, dx_ref, dg_ref = refs
        xv = x_ref[...]
        dv = d_ref[...].astype(F32)
        r = lax.rsqrt(jnp.mean(xv * xv, axis=-1, keepdims=True) + EPS)
        xh = xv * r
        dxh = dv * g_ref[...]
        dx = r * (dxh - xh * jnp.mean(dxh * xh, axis=-1, keepdims=True))
        if has_res:
            dx = dx + r_ref[...]
        dx_ref[...] = dx.astype(out_dtype)
        _acc_out(dg_ref, pl.program_id(0), _rowsum8(dv * xh))

    row = pl.BlockSpec((tm, d), lambda i: (i, 0))
    ins = [row, pl.BlockSpec((1, d), lambda i: (0, 0)), row] + ([row] if has_res else [])
    args = (x, g, dout) + ((res,) if has_res else ())
    return pl.pallas_call(
        body, grid=(m // tm,), name=name, in_specs=ins,
        out_specs=[row, pl.BlockSpec((8, d), lambda i: (0, 0))],
        out_shape=[jax.ShapeDtypeStruct((m, d), out_dtype), jax.ShapeDtypeStruct((8, d), F32)],
        compiler_params=_cp())(*args)


def _loss_grad(h, tgt, *, tm, name):
    m, d = h.shape

    def body(h_ref, t_ref, dh_ref, p_ref):
        e = h_ref[...] - t_ref[...]
        dh_ref[...] = e / d
        _acc_out(p_ref, pl.program_id(0), _rowsum8(e * e))

    row = pl.BlockSpec((tm, d), lambda i: (i, 0))
    return pl.pallas_call(
        body, grid=(m // tm,), name=name, in_specs=[row, row],
        out_specs=[row, pl.BlockSpec((8, d), lambda i: (0, 0))],
        out_shape=[jax.ShapeDtypeStruct((m, d), F32), jax.ShapeDtypeStruct((8, d), F32)],
        compiler_params=_cp())(h, tgt)


def _adamw(w, g, m, v, *, name):
    r, c = w.shape
    tr = _tile(r, 256)

    def body(w_ref, g_ref, m_ref, v_ref, d_ref, mo_ref, vo_ref):
        gv = g_ref[...]
        m2 = ADAM_B1 * m_ref[...] + (1.0 - ADAM_B1) * gv
        v2 = ADAM_B2 * v_ref[...] + (1.0 - ADAM_B2) * jnp.square(gv)
        m_hat = m2 / (1.0 - ADAM_B1 ** ADAM_STEP)
        v_hat = v2 / (1.0 - ADAM_B2 ** ADAM_STEP)
        d_ref[...] = -ADAM_LR * (m_hat / (jnp.sqrt(v_hat) + ADAM_EPS) + ADAM_WD * w_ref[...])
        mo_ref[...] = m2
        vo_ref[...] = v2

    blk = pl.BlockSpec((tr, c), lambda i: (i, 0))
    return pl.pallas_call(
        body, grid=(r // tr,), name=name, in_specs=[blk] * 4, out_specs=[blk] * 3,
        out_shape=[jax.ShapeDtypeStruct((r, c), F32)] * 3, compiler_params=_cp())(w, g, m, v)


def _head_masks():
    lane = lax.broadcasted_iota(jnp.int32, (BLK, LANES), 1)
    row = lax.broadcasted_iota(jnp.int32, (BLK, LANES), 0)
    return lane, row, lane < HD


def _sb_scores(q_a, k, before):
    z = _dot_nt(q_a, k)
    sp = jnp.log1p(jnp.exp(-jnp.abs(z)))
    ls_pos = jnp.minimum(z, 0.0) - sp
    lkeep = jnp.where(before, ls_pos - z, 0.0)
    return ls_pos, lkeep


SB_DEAD = -104.0


def _sb_alive(jj, i, carry):
    return jnp.logical_and(jj <= i, jnp.max(carry) > SB_DEAD)


def _sb_before(diagonal):
    lane = lax.broadcasted_iota(jnp.int32, (2 * BLK, LANES), 1)
    row = lax.broadcasted_iota(jnp.int32, (2 * BLK, LANES), 0) % BLK
    return jnp.logical_or(jnp.logical_not(diagonal), lane < row)


def _sb_fwd(u, *, name):
    s_len = u.shape[0]
    nq = s_len // BLK

    def body(q_ref, k_ref, v_ref, o_ref):
        i = pl.program_id(1)
        lane, row, lane_h = _head_masks()
        suffix = (row > lane).astype(BF16)
        qs = _stack_heads(q_ref[...] * 0.125, lane_h)

        def step(state):
            jj, cc, acc = state
            off = pl.multiple_of((i - jj) * BLK, BLK)
            k = k_ref[pl.ds(off, BLK), :].astype(BF16)
            v = v_ref[pl.ds(off, BLK), :].astype(BF16)
            before = _sb_before(jj == 0)
            ls_pos, lkeep = _sb_scores(qs, k, before)
            between = _dot_hilo(lkeep, suffix) + cc
            att = jnp.where(before, jnp.exp(ls_pos + between), 0.0)
            return jj + 1, cc + jnp.sum(lkeep, axis=1, keepdims=True), acc + _dot(att.astype(BF16), v)

        init = (jnp.int32(0), jnp.zeros((2 * BLK, 1), F32), jnp.zeros((2 * BLK, LANES), F32))
        acc = lax.while_loop(lambda st: _sb_alive(st[0], i, st[1]), step, init)[2]
        o_ref[...] = jnp.where(lane_h, acc[:BLK], acc[BLK:]).astype(BF16)

    return pl.pallas_call(
        body, grid=(2, nq), name=name,
        in_specs=[pl.BlockSpec((BLK, LANES), lambda hp, i: (i, hp)),
                  pl.BlockSpec((s_len, LANES), lambda hp, i: (0, 2 + hp)),
                  pl.BlockSpec((s_len, LANES), lambda hp, i: (0, 4 + hp))],
        out_specs=pl.BlockSpec((BLK, LANES), lambda hp, i: (i, hp)),
        out_shape=jax.ShapeDtypeStruct((s_len, SB_W), BF16), compiler_params=_cp())(u, u, u)


def _sb_bwd(u, dcat, *, name):
    s_len = u.shape[0]
    nq = s_len // BLK

    def body(q_ref, k_ref, v_ref, do_ref, dq_ref, dk_ref, dv_ref, g_scr, b_scr):
        i = pl.program_id(1)
        lane, row, lane_h = _head_masks()
        suffix = (row > lane).astype(BF16)
        prefix = (row < lane).astype(BF16)
        qf = q_ref[...]
        qs = _stack_heads(qf * 0.125, lane_h)
        qu = _stack_heads(qf, lane_h)
        dos = _stack_heads(do_ref[...], lane_h)

        @pl.when(i == 0)
        def _():
            dk_ref[...] = jnp.zeros_like(dk_ref)
            dv_ref[...] = jnp.zeros_like(dv_ref)

        def down(state):
            jj, cc = state
            j = i - jj
            off = pl.multiple_of(j * BLK, BLK)
            k = k_ref[pl.ds(off, BLK), :].astype(BF16)
            v = v_ref[pl.ds(off, BLK), :].astype(BF16)
            before = _sb_before(jj == 0)
            ls_pos, lkeep = _sb_scores(qs, k, before)
            between = _dot_hilo(lkeep, suffix) + cc
            att = jnp.where(before, jnp.exp(ls_pos + between), 0.0)
            g_scr[j] = att * _dot_nt(dos, v)
            b_scr[j] = jnp.exp(ls_pos)
            dv_ref[pl.ds(off, BLK), :] += _dot_tn(att.astype(BF16), dos)
            return jj + 1, cc + jnp.sum(lkeep, axis=1, keepdims=True)

        zc = jnp.zeros((2 * BLK, 1), F32)
        visited = lax.while_loop(lambda st: _sb_alive(st[0], i, st[1]), down, (jnp.int32(0), zc))[0]

        def up(j, carry):
            pc, dq = carry
            off = pl.multiple_of(j * BLK, BLK)
            k = k_ref[pl.ds(off, BLK), :].astype(BF16)
            g, beta = g_scr[j], b_scr[j]
            below = _dot_hilo(g, prefix) + pc
            dz = (jnp.where(_sb_before(j == i), g * (1.0 - beta) - beta * below, 0.0) * 0.125).astype(BF16)
            dk_ref[pl.ds(off, BLK), :] += _dot_tn(dz, qu)
            return pc + jnp.sum(g, axis=1, keepdims=True), dq + _dot(dz, k)

        dq = lax.fori_loop(i + 1 - visited, i + 1, up, (zc, jnp.zeros((2 * BLK, LANES), F32)))[1]
        dq_ref[...] = jnp.where(lane_h, dq[:BLK], dq[BLK:])

    col = lambda c0: pl.BlockSpec((s_len, LANES), lambda hp, i: (0, c0 + hp))
    blk = pl.BlockSpec((BLK, LANES), lambda hp, i: (i, hp))
    acc = pl.BlockSpec((s_len, LANES), lambda hp, i: (0, hp))
    return pl.pallas_call(
        body, grid=(2, nq), name=name, in_specs=[blk, col(2), col(4), blk],
        out_specs=[blk, acc, acc], out_shape=[jax.ShapeDtypeStruct((s_len, SB_W), F32)] * 3,
        scratch_shapes=[pltpu.VMEM((nq, 2 * BLK, LANES), F32), pltpu.VMEM((nq, 2 * BLK, LANES), F32)],
        compiler_params=_cp())(u, u, u, dcat)


CV_T = 512
CV_H = 32


def _cv_specs(s_len):
    cur = lambda c: pl.BlockSpec((CV_T, CV_W), lambda i: (i, c))
    prev = lambda c: pl.BlockSpec((CV_H, CV_W), lambda i: (jnp.maximum(i * (CV_T // CV_H) - 1, 0), c))
    nxt = lambda c: pl.BlockSpec((CV_H, CV_W),
                                 lambda i: (jnp.minimum((i + 1) * (CV_T // CV_H), s_len // CV_H - 1), c))
    full = lambda r: pl.BlockSpec((r, CV_W), lambda i: (0, 0))
    return cur, prev, nxt, full


def _glu_into(gp_ref, val_ref, gate_ref, valp_ref, gatep_ref, i):
    gp_ref[0:CV_H, :] = jnp.where(i > 0, valp_ref[...] * jax.nn.sigmoid(gatep_ref[...]), 0.0)
    gp_ref[CV_H:, :] = val_ref[...] * jax.nn.sigmoid(gate_ref[...])


def _cv_fwd(u, cv_w, cv_b, ln_g, ln_b, pw_w, pw_b, *, name):
    s_len = u.shape[0]
    cur, prev, _, full = _cv_specs(s_len)

    def body(val_ref, gate_ref, valp_ref, gatep_ref, w_ref, b_ref, g_ref, be_ref, pw_ref, pb_ref,
             o_ref, c_ref, gp_ref):
        _glu_into(gp_ref, val_ref, gate_ref, valp_ref, gatep_ref, pl.program_id(0))
        acc = jnp.zeros((CV_T, CV_W), F32) + b_ref[...]
        for k in range(CV_K):
            acc = acc + w_ref[k:k + 1, :] * gp_ref[pl.ds(CV_H - CV_K + 1 + k, CV_T), :]
        c_ref[...] = acc
        mu = jnp.mean(acc, axis=-1, keepdims=True)
        xc = acc - mu
        xh = xc * lax.rsqrt(jnp.mean(xc * xc, axis=-1, keepdims=True) + EPS)
        a = xh * g_ref[...] + be_ref[...]
        s = a * jax.nn.sigmoid(a)
        o_ref[...] = (_dot(s.astype(BF16), pw_ref[...]) + pb_ref[...]).astype(BF16)

    return pl.pallas_call(
        body, grid=(s_len // CV_T,), name=name,
        in_specs=[cur(3), cur(4), prev(3), prev(4), full(CV_K), full(1), full(1), full(1), full(CV_W), full(1)],
        out_specs=[cur(0), cur(0)],
        out_shape=[jax.ShapeDtypeStruct((s_len, CV_W), BF16), jax.ShapeDtypeStruct((s_len, CV_W), F32)],
        scratch_shapes=[pltpu.VMEM((CV_T + CV_H, CV_W), F32)], compiler_params=_cp())(
            u, u, u, u, cv_w, cv_b, ln_g, ln_b, pw_w, pw_b)


def _cv_bwd_local(c, dcat, ln_g, ln_b, pw_w, *, name):
    s_len = c.shape[0]
    cur, _, _, full = _cv_specs(s_len)

    def body(c_ref, db_ref, g_ref, be_ref, pw_ref, dc_ref, dpw_ref, vec_ref):
        i = pl.program_id(0)
        cv = c_ref[...]
        db = db_ref[...]
        mu = jnp.mean(cv, axis=-1, keepdims=True)
        xc = cv - mu
        rstd = lax.rsqrt(jnp.mean(xc * xc, axis=-1, keepdims=True) + EPS)
        xh = xc * rstd
        a = xh * g_ref[...] + be_ref[...]
        sg = jax.nn.sigmoid(a)
        s = a * sg
        dbb = db.astype(BF16)
        ds = _dot_nt(dbb, pw_ref[...])
        da = ds * (sg * (1.0 + a * (1.0 - sg)))
        dxh = da * g_ref[...]
        dc_ref[...] = rstd * (dxh - jnp.mean(dxh, axis=-1, keepdims=True)
                              - xh * jnp.mean(dxh * xh, axis=-1, keepdims=True))
        _acc_out(dpw_ref, i, _dot_tn(s.astype(BF16), dbb))
        _acc_out(vec_ref, i, jnp.concatenate([_rowsum8(db), _rowsum8(da * xh), _rowsum8(da)], axis=0))

    return pl.pallas_call(
        body, grid=(s_len // CV_T,), name=name,
        in_specs=[cur(0), cur(1), full(1), full(1), full(CV_W)],
        out_specs=[cur(0), full(CV_W), full(24)],
        out_shape=[jax.ShapeDtypeStruct((s_len, CV_W), F32), jax.ShapeDtypeStruct((CV_W, CV_W), F32),
                   jax.ShapeDtypeStruct((24, CV_W), F32)], compiler_params=_cp())(c, dcat, ln_g, ln_b, pw_w)


def _cv_bwd_conv(u, dc, cv_w, *, name):
    s_len = u.shape[0]
    cur, prev, nxt, full = _cv_specs(s_len)
    last = s_len // CV_T - 1

    def body(val_ref, gate_ref, valp_ref, gatep_ref, dc_ref, dcn_ref, w_ref, du_ref, dw_ref, dbias_ref,
             gp_ref, dcp_ref):
        i = pl.program_id(0)
        _glu_into(gp_ref, val_ref, gate_ref, valp_ref, gatep_ref, i)
        dcv = dc_ref[...]
        dcp_ref[0:CV_T, :] = dcv
        dcp_ref[CV_T:, :] = jnp.where(i < last, dcn_ref[...], 0.0)
        dg = jnp.zeros((CV_T, CV_W), F32)
        parts = []
        for k in range(CV_K):
            dg = dg + w_ref[k:k + 1, :] * dcp_ref[pl.ds(CV_K - 1 - k, CV_T), :]
            parts.append(_rowsum8(dcv * gp_ref[pl.ds(CV_H - CV_K + 1 + k, CV_T), :]))
        _acc_out(dw_ref, i, jnp.concatenate(parts, axis=0))
        _acc_out(dbias_ref, i, _rowsum8(dcv))
        val = val_ref[...]
        sg = jax.nn.sigmoid(gate_ref[...])
        du_ref[:, 0:CV_W] = (dg * sg).astype(BF16)
        du_ref[:, CV_W:] = (dg * val * sg * (1.0 - sg)).astype(BF16)

    return pl.pallas_call(
        body, grid=(s_len // CV_T,), name=name,
        in_specs=[cur(3), cur(4), prev(3), prev(4), cur(0), nxt(0), full(CV_K)],
        out_specs=[pl.BlockSpec((CV_T, 2 * CV_W), lambda i: (i, 0)), full(CV_K * 8), full(8)],
        out_shape=[jax.ShapeDtypeStruct((s_len, 2 * CV_W), BF16), jax.ShapeDtypeStruct((CV_K * 8, CV_W), F32),
                   jax.ShapeDtypeStruct((8, CV_W), F32)],
        scratch_shapes=[pltpu.VMEM((CV_T + CV_H, CV_W), F32), pltpu.VMEM((CV_T + CV_H, CV_W), F32)],
        compiler_params=_cp())(u, u, u, u, dc, dc, cv_w)


def _rope_tables(pos_col, inv_freq_row, *, name):
    s_len = pos_col.shape[0]

    def body(p_ref, f_ref, cos_ref, sin_ref):
        ang = p_ref[...].astype(F32) * f_ref[...]
        lane = lax.broadcasted_iota(jnp.int32, (s_len, LANES), 1)
        sn = jnp.sin(ang)
        cos_ref[...] = jnp.cos(ang)
        sin_ref[...] = jnp.where(lane % HD < HD // 2, -sn, sn)

    return pl.pallas_call(body, name=name, out_shape=[jax.ShapeDtypeStruct((s_len, LANES), F32)] * 2,
                          compiler_params=_cp())(pos_col, inv_freq_row)


def _rot_half(x):
    lane = lax.broadcasted_iota(jnp.int32, x.shape, 1)
    return jnp.where(lane % HD < HD // 2, pltpu.roll(x, LANES - HD // 2, 1), pltpu.roll(x, HD // 2, 1))


def _permute_rows(dst_ref, src_ref, d, dtype):
    s_len = src_ref.shape[0]
    seg = s_len // d
    if d == 1:
        dst_ref[...] = src_ref[...].astype(dtype)
        return
    for r in range(d):
        dst_ref[r * seg:(r + 1) * seg, :] = src_ref[pl.ds(r, seg, stride=d), :].astype(dtype)


def _unpermute_rows(dst_ref, src_ref, d):
    s_len = src_ref.shape[0]
    seg = s_len // d
    if d == 1:
        dst_ref[...] = src_ref[...]
        return
    for r in range(d):
        dst_ref[pl.ds(r, seg, stride=d), :] = src_ref[r * seg:(r + 1) * seg, :]


def _rope_perm(u, cos, sin, *, name):
    s_len = u.shape[0]

    def body(x_ref, cos_ref, sin_ref, o_ref, scr):
        a = pl.program_id(0)
        x = x_ref[...]
        rot = a < 2
        scr[...] = x * jnp.where(rot, cos_ref[...], 1.0) + _rot_half(x) * jnp.where(rot, sin_ref[...], 0.0)
        for n, d in enumerate(DILATIONS):
            _permute_rows(o_ref.at[n], scr, d, BF16)

    tab = pl.BlockSpec((s_len, LANES), lambda a, cb: (0, 0))
    return pl.pallas_call(
        body, grid=(3, 4), name=name,
        in_specs=[pl.BlockSpec((s_len, LANES), lambda a, cb: (0, 10 + 4 * a + cb)), tab, tab],
        out_specs=pl.BlockSpec((None, 3, s_len, LANES), lambda a, cb: (a, 0, 0, cb)),
        out_shape=jax.ShapeDtypeStruct((3, 3, s_len, DL_W), BF16),
        scratch_shapes=[pltpu.VMEM((s_len, LANES), F32)], compiler_params=_cp())(u, cos, sin)


DL_UNROLL = 4


def _dl_band(rows):
    lane = lax.broadcasted_iota(jnp.int32, (rows, LANES), 1)
    row = lax.broadcasted_iota(jnp.int32, (rows, LANES), 0) % BLK
    return lane <= row, lane >= row


def _dl_first(s_len, n, i):
    nb = jnp.where(n == 0, s_len // BLK, jnp.where(n == 1, s_len // (BLK * DILATIONS[1]),
                                                   s_len // (BLK * DILATIONS[2])))
    return lax.rem(i, nb) == 0


def _stack_heads(x, lane_h):
    return jnp.concatenate([jnp.where(lane_h, x, 0.0), jnp.where(lane_h, 0.0, x)], axis=0).astype(BF16)


def _dl_rows(i):
    cur = pl.ds(pl.multiple_of(i * BLK, BLK), BLK)
    prev = pl.ds(pl.multiple_of(jnp.maximum(i - 1, 0) * BLK, BLK), BLK)
    return cur, prev


def _dl_in_specs(s_len):
    return [pl.BlockSpec((None, None, s_len, LANES), functools.partial(lambda a, n, hp: (a, n, 0, hp), a))
            for a in range(3)]


def _dl_fwd(qkv, *, name):
    s_len = qkv.shape[2]

    def body(q_ref, k_ref, v_ref, o_ref, l_ref):
        n = pl.program_id(0)
        lane_h = _head_masks()[2]
        band_c, band_p = _dl_band(2 * BLK)
        ones = jnp.ones((BLK, LANES), BF16)

        @pl.loop(0, s_len // BLK, step=DL_UNROLL)
        def _(i0):
            blocks = [i0 + t for t in range(DL_UNROLL)]
            rows = [_dl_rows(i) for i in blocks]
            scores = []
            for cur, prev in rows:
                qs = _stack_heads(q_ref[cur, :] * 0.125, lane_h)
                scores.append((_dot_nt(qs, k_ref[cur, :]), _dot_nt(qs, k_ref[prev, :])))
            probs = []
            for i, (sc, sp) in zip(blocks, scores):
                sc = jnp.where(band_c, sc, NEG_INF)
                sp = jnp.where(jnp.logical_and(band_p, jnp.logical_not(_dl_first(s_len, n, i))), sp, NEG_INF)
                m = jnp.max(jnp.maximum(sc, sp), axis=1, keepdims=True)
                probs.append((jnp.exp(sc - m).astype(BF16), jnp.exp(sp - m).astype(BF16), m))
            for (cur, prev), (pc, pp, m) in zip(rows, probs):
                r = (_dot(pc, jnp.concatenate([v_ref[cur, :], ones], axis=1))
                     + _dot(pp, jnp.concatenate([v_ref[prev, :], ones], axis=1)))
                den = jnp.where(lane_h, r[:BLK, LANES:], r[BLK:, LANES:])
                o_ref[cur, :] = jnp.where(lane_h, r[:BLK, :LANES], r[BLK:, :LANES]) / den
                l_ref[cur, :] = jnp.where(lane_h, m[:BLK], m[BLK:]) + jnp.log(den)

    out = pl.BlockSpec((None, s_len, LANES), lambda n, hp: (n, 0, hp))
    return pl.pallas_call(
        body, grid=(3, 4), name=name, in_specs=_dl_in_specs(s_len), out_specs=[out, out],
        out_shape=[jax.ShapeDtypeStruct((3, s_len, DL_W), F32)] * 2, compiler_params=_cp())(qkv, qkv, qkv)


def _dl_mix(o_p, l_p, *, name):
    s_len = o_p.shape[1]

    def body(o_ref, l_ref, ob_ref, of_ref, lt_ref, o_scr, l_scr):
        n = pl.program_id(1)
        for k, d in enumerate(DILATIONS):
            @pl.when(n == k)
            def _(k=k, d=d):
                _unpermute_rows(o_scr.at[k], o_ref, d)
                _unpermute_rows(l_scr.at[k], l_ref, d)

        @pl.when(n == 2)
        def _():
            l0, l1, l2 = l_scr[0], l_scr[1], l_scr[2]
            m = jnp.maximum(jnp.maximum(l0, l1), l2)
            e0, e1, e2 = jnp.exp(l0 - m), jnp.exp(l1 - m), jnp.exp(l2 - m)
            den = e0 + e1 + e2
            o = (e0 / den) * o_scr[0] + (e1 / den) * o_scr[1] + (e2 / den) * o_scr[2]
            of_ref[...] = o
            ob_ref[...] = o.astype(BF16)
            lt_ref[...] = m + jnp.log(den)

    inb = pl.BlockSpec((None, s_len, LANES), lambda cb, n: (n, 0, cb))
    outb = pl.BlockSpec((s_len, LANES), lambda cb, n: (0, cb))
    return pl.pallas_call(
        body, grid=(4, 3), name=name, in_specs=[inb, inb], out_specs=[outb, outb, outb],
        out_shape=[jax.ShapeDtypeStruct((s_len, DL_W), BF16), jax.ShapeDtypeStruct((s_len, DL_W), F32),
                   jax.ShapeDtypeStruct((s_len, DL_W), F32)],
        scratch_shapes=[pltpu.VMEM((3, s_len, LANES), F32), pltpu.VMEM((3, s_len, LANES), F32)],
        compiler_params=_cp())(o_p, l_p)


def _dl_bwd_prep(dcat, o, lse, *, name):
    s_len = o.shape[0]

    def body(do_ref, o_ref, l_ref, dop_ref, st_ref, d_scr):
        n = pl.program_id(1)

        @pl.when(n == 0)
        def _():
            r0 = lax.broadcasted_iota(jnp.int32, (LANES, LANES), 0) // HD
            r1 = lax.broadcasted_iota(jnp.int32, (LANES, LANES), 1) // HD
            d_scr[...] = _dot_hilo(do_ref[...] * o_ref[...], (r0 == r1).astype(BF16))

        for k, d in enumerate(DILATIONS):
            @pl.when(n == k)
            def _(d=d):
                _permute_rows(dop_ref, do_ref, d, BF16)
                _permute_rows(st_ref.at[0], d_scr, d, F32)
                _permute_rows(st_ref.at[1], l_ref, d, F32)

    nat = lambda c0: pl.BlockSpec((s_len, LANES), lambda cb, n: (0, c0 + cb))
    return pl.pallas_call(
        body, grid=(4, 3), name=name, in_specs=[nat(4), nat(0), nat(0)],
        out_specs=[pl.BlockSpec((None, s_len, LANES), lambda cb, n: (n, 0, cb)),
                   pl.BlockSpec((2, None, s_len, LANES), lambda cb, n: (0, n, 0, cb))],
        out_shape=[jax.ShapeDtypeStruct((3, s_len, DL_W), BF16), jax.ShapeDtypeStruct((2, 3, s_len, DL_W), F32)],
        scratch_shapes=[pltpu.VMEM((s_len, LANES), F32)], compiler_params=_cp())(dcat, o, lse)


def _dl_bwd(qkv, dop, stats, *, name):
    s_len = qkv.shape[2]

    def body(q_ref, k_ref, v_ref, do_ref, st_ref, cur_ref, prev_ref):
        n = pl.program_id(0)
        lane_h = _head_masks()[2]
        band_c, band_p = _dl_band(2 * BLK)

        def per_head(x):
            xr = pltpu.roll(x, HD, 1)
            return jnp.concatenate([jnp.where(lane_h, x, xr), jnp.where(lane_h, xr, x)], axis=0)

        @pl.loop(0, s_len // BLK, step=DL_UNROLL)
        def _(i0):
            blocks = [i0 + t for t in range(DL_UNROLL)]
            rows = [_dl_rows(i) for i in blocks]
            stage1 = []
            for cur, prev in rows:
                qs = _stack_heads(q_ref[cur, :] * 0.125, lane_h)
                dos = _stack_heads(do_ref[cur, :], lane_h)
                kc, kp, vc, vp = k_ref[cur, :], k_ref[prev, :], v_ref[cur, :], v_ref[prev, :]
                stage1.append((qs, dos, _dot_nt(qs, kc), _dot_nt(qs, kp), _dot_nt(dos, vc), _dot_nt(dos, vp)))
            stage2 = []
            for i, (cur, prev), (qs, dos, sc, sp, dpc, dpp) in zip(blocks, rows, stage1):
                lse, delta = per_head(st_ref[1, cur, :]), per_head(st_ref[0, cur, :])
                pc = jnp.where(band_c, jnp.exp(sc - lse), 0.0)
                pp = jnp.where(jnp.logical_and(band_p, jnp.logical_not(_dl_first(s_len, n, i))), jnp.exp(sp - lse), 0.0)
                stage2.append((pc.astype(BF16), pp.astype(BF16), (pc * (dpc - delta)).astype(BF16),
                               (pp * (dpp - delta)).astype(BF16)))
            for (cur, prev), (qs, dos, *_), (pc, pp, dsc, dsp) in zip(rows, stage1, stage2):
                dq = _dot(dsc, k_ref[cur, :]) + _dot(dsp, k_ref[prev, :])
                cur_ref[0, cur, :] = jnp.where(lane_h, dq[:BLK], dq[BLK:]) * 0.125
                cur_ref[1, cur, :] = _dot_tn(dsc, qs)
                cur_ref[2, cur, :] = _dot_tn(pc, dos)
                prev_ref[0, cur, :] = _dot_tn(dsp, qs)
                prev_ref[1, cur, :] = _dot_tn(pp, dos)

    return pl.pallas_call(
        body, grid=(3, 4), name=name,
        in_specs=_dl_in_specs(s_len) + [pl.BlockSpec((None, s_len, LANES), lambda n, hp: (n, 0, hp)),
                                        pl.BlockSpec((2, None, s_len, LANES), lambda n, hp: (0, n, 0, hp))],
        out_specs=[pl.BlockSpec((3, None, s_len, LANES), lambda n, hp: (0, n, 0, hp)),
                   pl.BlockSpec((2, None, s_len, LANES), lambda n, hp: (0, n, 0, hp))],
        out_shape=[jax.ShapeDtypeStruct((3, 3, s_len, DL_W), F32), jax.ShapeDtypeStruct((2, 3, s_len, DL_W), F32)],
        compiler_params=_cp(56))(qkv, qkv, qkv, dop, stats)


def _dl_bwd_finish(cur, prev, cos, sin, *, name):
    s_len = cur.shape[2]

    def body(c_ref, p_ref, cos_ref, sin_ref, o_ref, p_scr, u_scr, acc):
        a, n = pl.program_id(0), pl.program_id(2)
        has_prev = jnp.where(a > 0, 1.0, 0.0)
        p_scr[...] = c_ref[...]
        p_scr[0:s_len - BLK, :] += has_prev * p_ref[BLK:, :]
        for k, d in enumerate(DILATIONS):
            @pl.when(n == k)
            def _(k=k, d=d):
                if k == 0:
                    acc[...] = p_scr[...]
                else:
                    _unpermute_rows(u_scr, p_scr, d)
                    acc[...] += u_scr[...]

        @pl.when(n == 2)
        def _():
            dy = acc[...]
            rot = a < 2
            o_ref[...] = (dy * jnp.where(rot, cos_ref[...], 1.0)
                          + _rot_half(dy * jnp.where(rot, sin_ref[...], 0.0))).astype(BF16)

    tab = pl.BlockSpec((s_len, LANES), lambda a, cb, n: (0, 0))
    return pl.pallas_call(
        body, grid=(3, 4, 3), name=name,
        in_specs=[pl.BlockSpec((None, None, s_len, LANES), lambda a, cb, n: (a, n, 0, cb)),
                  pl.BlockSpec((None, None, s_len, LANES), lambda a, cb, n: (jnp.maximum(a - 1, 0), n, 0, cb)),
                  tab, tab],
        out_specs=pl.BlockSpec((s_len, LANES), lambda a, cb, n: (0, 4 * a + cb)),
        out_shape=jax.ShapeDtypeStruct((s_len, 3 * DL_W), BF16),
        scratch_shapes=[pltpu.VMEM((s_len, LANES), F32)] * 3, compiler_params=_cp())(cur, prev, cos, sin)


XA_T = 256


def _xa_probs(q, k):
    s = _dot_nt(q, k) * (X_HD ** -0.5)
    e = jnp.exp(s - jnp.max(s, axis=1, keepdims=True))
    return e / jnp.sum(e, axis=1, keepdims=True)


def _xa_fwd(q, k, v, *, name):
    s_len, d = q.shape
    nm = k.shape[0]

    def body(q_ref, k_ref, v_ref, o_ref):
        for h in range(X_HEADS):
            cs = slice(h * X_HD, (h + 1) * X_HD)
            p = _xa_probs(q_ref[:, cs], k_ref[:, cs])
            o_ref[:, cs] = _dot(p.astype(BF16), v_ref[:, cs]).astype(BF16)

    row = pl.BlockSpec((XA_T, d), lambda i: (i, 0))
    full = pl.BlockSpec((nm, d), lambda i: (0, 0))
    return pl.pallas_call(body, grid=(s_len // XA_T,), name=name, in_specs=[row, full, full], out_specs=row,
                          out_shape=jax.ShapeDtypeStruct((s_len, d), BF16), compiler_params=_cp())(q, k, v)


def _xa_bwd(q, k, v, do, *, name):
    s_len, d = q.shape
    nm = k.shape[0]

    def body(q_ref, k_ref, v_ref, do_ref, dq_ref, dk_ref, dv_ref):
        i = pl.program_id(0)
        for h in range(X_HEADS):
            cs = slice(h * X_HD, (h + 1) * X_HD)
            qh, kh, vh, doh = q_ref[:, cs], k_ref[:, cs], v_ref[:, cs], do_ref[:, cs]
            p = _xa_probs(qh, kh)
            dp = _dot_nt(doh, vh)
            ds = (p * (dp - jnp.sum(dp * p, axis=1, keepdims=True)) * (X_HD ** -0.5)).astype(BF16)
            dq_ref[:, cs] = _dot(ds, kh).astype(BF16)
            dkh, dvh = _dot_tn(ds, qh), _dot_tn(p.astype(BF16), doh)

            @pl.when(i == 0)
            def _(cs=cs, dkh=dkh, dvh=dvh):
                dk_ref[:, cs] = dkh
                dv_ref[:, cs] = dvh

            @pl.when(i > 0)
            def _(cs=cs, dkh=dkh, dvh=dvh):
                dk_ref[:, cs] += dkh
                dv_ref[:, cs] += dvh

    row = pl.BlockSpec((XA_T, d), lambda i: (i, 0))
    full = pl.BlockSpec((nm, d), lambda i: (0, 0))
    return pl.pallas_call(
        body, grid=(s_len // XA_T,), name=name, in_specs=[row, full, full, row], out_specs=[row, full, full],
        out_shape=[jax.ShapeDtypeStruct((s_len, d), BF16), jax.ShapeDtypeStruct((nm, d), F32),
                   jax.ShapeDtypeStruct((nm, d), F32)], compiler_params=_cp())(q, k, v, do)


FF_TM, FF_TN, FF_H = 512, 256, 8
GELU_K, GELU_C = 0.7978845608028654, 0.044715


def _ff_conv(e_ref, w_ref, b_ref, rows):
    return (w_ref[0:1, :] * e_ref[pl.ds(FF_H - 2, rows), :] + w_ref[1:2, :] * e_ref[pl.ds(FF_H - 1, rows), :]
            + w_ref[2:3, :] * e_ref[pl.ds(FF_H, rows), :] + b_ref[...])


def _ff_gate_fwd(up, conv_w, conv_b, *, name):
    s_len = up.shape[0]
    nj = D_FF // FF_TN

    def body(g_ref, v_ref, gp_ref, vp_ref, wg_ref, wv_ref, bg_ref, bv_ref, o_ref, eg, ev):
        i = pl.program_id(0)
        for e, cur, prev in ((eg, g_ref, gp_ref), (ev, v_ref, vp_ref)):
            e[0:FF_H, :] = jnp.where(i > 0, prev[...], 0.0)
            e[FF_H:, :] = cur[...]
        gate = _ff_conv(eg, wg_ref, bg_ref, FF_TM)
        val = _ff_conv(ev, wv_ref, bv_ref, FF_TM)
        t = jnp.tanh(GELU_K * (gate + GELU_C * gate * gate * gate))
        o_ref[...] = (0.5 * gate * (1.0 + t) * val).astype(BF16)

    cur = lambda c0: pl.BlockSpec((FF_TM, FF_TN), lambda i, j: (i, c0 + j))
    prev = lambda c0: pl.BlockSpec((FF_H, FF_TN), lambda i, j: (jnp.maximum(i * (FF_TM // FF_H) - 1, 0), c0 + j))
    par = lambda r, c0: pl.BlockSpec((r, FF_TN), lambda i, j: (0, c0 + j))
    return pl.pallas_call(
        body, grid=(s_len // FF_TM, nj), name=name,
        in_specs=[cur(0), cur(nj), prev(0), prev(nj), par(3, 0), par(3, nj), par(1, 0), par(1, nj)],
        out_specs=cur(0), out_shape=jax.ShapeDtypeStruct((s_len, D_FF), BF16),
        scratch_shapes=[pltpu.VMEM((FF_TM + FF_H, FF_TN), F32)] * 2, compiler_params=_cp())(
            up, up, up, up, conv_w, conv_w, conv_b, conv_b)


def _ff_gate_bwd(up, dact, conv_w, conv_b, *, name):
    s_len = up.shape[0]
    nj = D_FF // FF_TN
    last = s_len // FF_TM - 1
    ext = FF_TM + FF_H

    def body(g_ref, v_ref, gp_ref, vp_ref, gn_ref, vn_ref, da_ref, dan_ref, wg_ref, wv_ref, bg_ref, bv_ref,
             dg_ref, dv_ref, dw_ref, db_ref, eg, ev, sg, sv):
        i = pl.program_id(1)
        for e, cur, prev, nxt in ((eg, g_ref, gp_ref, gn_ref), (ev, v_ref, vp_ref, vn_ref)):
            e[0:FF_H, :] = jnp.where(i > 0, prev[...], 0.0)
            e[FF_H:FF_H + FF_TM, :] = cur[...]
            e[FF_H + FF_TM:, :] = nxt[...]
        gate = _ff_conv(eg, wg_ref, bg_ref, ext)
        val = _ff_conv(ev, wv_ref, bv_ref, ext)
        dact_e = jnp.concatenate([da_ref[...], jnp.where(i < last, dan_ref[...], 0.0)], axis=0)
        inner = GELU_K * (gate + GELU_C * gate * gate * gate)
        t = jnp.tanh(inner)
        gelu = 0.5 * gate * (1.0 + t)
        dgelu = 0.5 * (1.0 + t) + 0.5 * gate * (1.0 - t * t) * GELU_K * (1.0 + 3.0 * GELU_C * gate * gate)
        sg[...] = dact_e * val * dgelu
        sv[...] = dact_e * gelu
        for part, (s, e, w_ref, out) in enumerate(((sg, eg, wg_ref, dg_ref), (sv, ev, wv_ref, dv_ref))):
            out[...] = (w_ref[2:3, :] * s[pl.ds(0, FF_TM), :] + w_ref[1:2, :] * s[pl.ds(1, FF_TM), :]
                        + w_ref[0:1, :] * s[pl.ds(2, FF_TM), :]).astype(BF16)
            d0 = s[pl.ds(0, FF_TM), :]
            taps = [_rowsum8(d0 * e[pl.ds(FF_H - 2 + k, FF_TM), :]) for k in range(3)]
            _acc_out(dw_ref.at[part], i, jnp.concatenate(taps, axis=0))
            _acc_out(db_ref.at[part], i, _rowsum8(d0))

    cur = lambda c0: pl.BlockSpec((FF_TM, FF_TN), lambda j, i: (i, c0 + j))
    prev = lambda c0: pl.BlockSpec((FF_H, FF_TN), lambda j, i: (jnp.maximum(i * (FF_TM // FF_H) - 1, 0), c0 + j))
    nxt = lambda c0: pl.BlockSpec(
        (FF_H, FF_TN), lambda j, i: (jnp.minimum((i + 1) * (FF_TM // FF_H), s_len // FF_H - 1), c0 + j))
    par = lambda r, c0: pl.BlockSpec((r, FF_TN), lambda j, i: (0, c0 + j))
    return pl.pallas_call(
        body, grid=(nj, s_len // FF_TM), name=name,
        in_specs=[cur(0), cur(nj), prev(0), prev(nj), nxt(0), nxt(nj), cur(0), nxt(0),
                  par(3, 0), par(3, nj), par(1, 0), par(1, nj)],
        out_specs=[cur(0), cur(0), pl.BlockSpec((2, 24, FF_TN), lambda j, i: (0, 0, j)),
                   pl.BlockSpec((2, 8, FF_TN), lambda j, i: (0, 0, j))],
        out_shape=[jax.ShapeDtypeStruct((s_len, D_FF), BF16), jax.ShapeDtypeStruct((s_len, D_FF), BF16),
                   jax.ShapeDtypeStruct((2, 24, D_FF), F32), jax.ShapeDtypeStruct((2, 8, D_FF), F32)],
        scratch_shapes=[pltpu.VMEM((FF_TM + 2 * FF_H, FF_TN), F32)] * 2 + [pltpu.VMEM((ext, FF_TN), F32)] * 2,
        compiler_params=_cp())(up, up, up, up, up, up, dact, dact, conv_w, conv_w, conv_b, conv_b)


def _place():
    x, y, c = lax.axis_index("x"), lax.axis_index("y"), lax.axis_index("c")
    return x, y, c, [(1 - x, y), (x, 1 - y), (1 - x, 1 - y)]


def _remote(src, dst, send_sem, recv_sem, dev):
    return pltpu.make_async_remote_copy(src_ref=src, dst_ref=dst, send_sem=send_sem, recv_sem=recv_sem,
                                        device_id=dev, device_id_type=MESH)


_ANY = pl.BlockSpec(memory_space=pl.ANY)


def _gather_weights(pack, *, name):
    def body(p_ref, out_ref, send_sems, recv_sems, local_sem):
        x, y, c, chips = _place()
        sibling = (x, y, 1 - c)
        mine = pltpu.make_async_copy(p_ref, out_ref.at[2 * x + y], local_sem)
        mine.start()
        first = [_remote(p_ref.at[c], out_ref.at[2 * x + y, c], send_sems.at[k], recv_sems.at[k], (px, py, c))
                 for k, (px, py) in enumerate(chips)]
        for cp in first:
            cp.start()
        passed = []
        for k, (px, py) in enumerate(chips):
            slab = out_ref.at[2 * px + py, c]
            _remote(slab, slab, send_sems.at[k], recv_sems.at[k], (px, py, c)).wait_recv()
            passed.append(_remote(slab, slab, send_sems.at[3 + k], recv_sems.at[3 + k], sibling))
            passed[-1].start()
        for k, (px, py) in enumerate(chips):
            slab = out_ref.at[2 * px + py, 1 - c]
            _remote(slab, slab, send_sems.at[3 + k], recv_sems.at[3 + k], sibling).wait_recv()
        for cp in first + passed:
            cp.wait_send()
        mine.wait()

    return pl.pallas_call(
        body, name=name, in_specs=[_ANY], out_specs=_ANY,
        out_shape=jax.ShapeDtypeStruct((4,) + pack.shape, pack.dtype),
        scratch_shapes=[pltpu.SemaphoreType.DMA((6,)), pltpu.SemaphoreType.DMA((6,)), pltpu.SemaphoreType.DMA],
        compiler_params=_cp(16))(pack)


def _swap_other_layer(gw, *, name):
    def body(g_ref, out_ref, send_sem, recv_sem):
        x, y, c, _ = _place()
        cp = _remote(g_ref.at[1 - c], out_ref, send_sem, recv_sem, (x, y, 1 - c))
        cp.start()
        cp.wait()

    return pl.pallas_call(
        body, name=name, in_specs=[_ANY], out_specs=_ANY, out_shape=jax.ShapeDtypeStruct(gw.shape[1:], gw.dtype),
        scratch_shapes=[pltpu.SemaphoreType.DMA, pltpu.SemaphoreType.DMA],
        compiler_params=_cp(16))(gw)


def _chip_sum(gw, got, c_arr, *, name):
    _, nchip, rl, d = gw.shape
    tr = 512

    def body(c_ref, a_ref, b_ref, o32_ref, o16_ref):
        s = a_ref[...] + b_ref[...]
        o32_ref[...] = s
        o16_ref[...] = s.astype(BF16)

    blk = pl.BlockSpec((None, tr, d), lambda j, i, c_ref: (j, i, 0))
    return pl.pallas_call(
        body, name=name,
        grid_spec=pltpu.PrefetchScalarGridSpec(
            num_scalar_prefetch=1, grid=(nchip, rl // tr),
            in_specs=[pl.BlockSpec((None, None, tr, d), lambda j, i, c_ref: (c_ref[0], j, i, 0)), blk],
            out_specs=[blk, blk]),
        out_shape=[jax.ShapeDtypeStruct((nchip, rl, d), F32), jax.ShapeDtypeStruct((nchip, rl, d), BF16)],
        compiler_params=_cp())(c_arr, gw, got)


def _scatter_chip_sums(s16, *, name):
    def body(s_ref, out_ref, send_sems, recv_sems):
        x, y, c, chips = _place()
        sends = [_remote(s_ref.at[2 * px + py], out_ref.at[k], send_sems.at[k], recv_sems.at[k], (px, py, c))
                 for k, (px, py) in enumerate(chips)]
        for cp in sends:
            cp.start()
        for cp in sends:
            cp.wait()

    return pl.pallas_call(
        body, name=name, in_specs=[_ANY], out_specs=_ANY,
        out_shape=jax.ShapeDtypeStruct((3,) + s16.shape[1:], s16.dtype),
        scratch_shapes=[pltpu.SemaphoreType.DMA((3,)), pltpu.SemaphoreType.DMA((3,))],
        compiler_params=_cp(16))(s16)


def _mesh_sum(s32, got, j_arr, *, name):
    _, rl, d = s32.shape
    tr = 512

    def body(j_ref, a_ref, b_ref, o_ref):
        o_ref[...] = ((a_ref[...] + b_ref[0].astype(F32)) + b_ref[1].astype(F32)) + b_ref[2].astype(F32)

    return pl.pallas_call(
        body, name=name,
        grid_spec=pltpu.PrefetchScalarGridSpec(
            num_scalar_prefetch=1, grid=(rl // tr,),
            in_specs=[pl.BlockSpec((None, tr, d), lambda i, j_ref: (j_ref[0], i, 0)),
                      pl.BlockSpec((3, tr, d), lambda i, j_ref: (0, i, 0))],
            out_specs=pl.BlockSpec((tr, d), lambda i, j_ref: (i, 0))),
        out_shape=jax.ShapeDtypeStruct((rl, d), F32), compiler_params=_cp())(j_arr, s32, got)


def _share_layers(ghalf, *, name):
    def body(g_ref, out_ref, send_sem, recv_sem, local_sem):
        x, y, c, _ = _place()
        mine = pltpu.make_async_copy(g_ref, out_ref.at[c], local_sem)
        mine.start()
        cp = _remote(g_ref, out_ref.at[c], send_sem, recv_sem, (x, y, 1 - c))
        cp.start()
        _remote(g_ref, out_ref.at[1 - c], send_sem, recv_sem, (x, y, 1 - c)).wait_recv()
        cp.wait_send()
        mine.wait()

    return pl.pallas_call(
        body, name=name, in_specs=[_ANY], out_specs=_ANY,
        out_shape=jax.ShapeDtypeStruct((2,) + ghalf.shape, ghalf.dtype),
        scratch_shapes=[pltpu.SemaphoreType.DMA, pltpu.SemaphoreType.DMA, pltpu.SemaphoreType.DMA],
        compiler_params=_cp(16))(ghalf)


def _all_reduce_small(vec, *, name):
    rows, d = vec.shape

    def body(x_ref, o_ref, gat, send_sems, recv_sems, local_sem):
        x, y, c, chips = _place()
        me, sibling = (x, y, c), (x, y, 1 - c)

        def slot(px, py, pc):
            return gat.at[4 * px + 2 * py + pc]

        def copy(k, block, to, src=None):
            return _remote(slot(*block) if src is None else src, slot(*block), send_sems.at[k], recv_sems.at[k], to)

        mine = pltpu.make_async_copy(x_ref, slot(*me), local_sem)
        mine.start()
        first = [copy(0, me, sibling, src=x_ref)]
        first += [copy(1 + j, me, (*chip, c), src=x_ref) for j, chip in enumerate(chips)]
        for cp in first:
            cp.start()
        passed = [copy(4 + j, (*chip, c), sibling) for j, chip in enumerate(chips)]
        for j, chip in enumerate(chips):
            copy(1 + j, (*chip, c), me).wait_recv()
            passed[j].start()
        copy(0, sibling, me).wait_recv()
        for j, chip in enumerate(chips):
            copy(4 + j, (*chip, 1 - c), me).wait_recv()
        for cp in first + passed:
            cp.wait_send()
        mine.wait()
        acc = gat[0]
        for dev in range(1, 8):
            acc = acc + gat[dev]
        o_ref[...] = acc

    vm = pl.BlockSpec(memory_space=pltpu.VMEM)
    return pl.pallas_call(
        body, name=name, in_specs=[vm], out_specs=vm, out_shape=jax.ShapeDtypeStruct((rows, d), F32),
        scratch_shapes=[pltpu.VMEM((8, rows, d), F32), pltpu.SemaphoreType.DMA((7,)), pltpu.SemaphoreType.DMA((7,)),
                        pltpu.SemaphoreType.DMA],
        compiler_params=_cp(32))(vec)


COL_SHARDED = ("w_in", "ffn_w_up")


def _to_pack_rows(name, shard):
    return shard.reshape(-1, D_MODEL)


def _full_from_blocks(name, blocks):
    rows = blocks.shape[1]
    if name in COL_SHARDED:
        return blocks.reshape(4, D_MODEL, rows).transpose(1, 0, 2).reshape(D_MODEL, 4 * rows)
    return blocks.reshape(4 * rows, D_MODEL)


def _blocks_from_full(name, full):
    if name in COL_SHARDED:
        cols = full.shape[1] // 4
        return full.reshape(D_MODEL, 4, cols).transpose(1, 0, 2).reshape(4, cols, D_MODEL)
    return full.reshape(4, full.shape[0] // 4, D_MODEL)


def _row(v):
    return v.reshape(1, -1)


SMALL = (("mix_norm_pre", (1024,), None), ("cv_w", (31, 256), 1), ("cv_b", (256,), None), ("cv_ln_g", (256,), None),
         ("cv_ln_b", (256,), None), ("cv_pw_w", (256, 256), 0), ("cv_pw_b", (256,), None),
         ("mix_norm_post", (1024,), None), ("x_norm_pre", (1024,), None), ("mem_norm", (1024,), None),
         ("x_norm_post", (1024,), None), ("ffn_norm_pre", (1024,), None), ("ffn_conv_w", (3, 5632), 1),
         ("ffn_conv_b", (5632,), None), ("ffn_norm_post", (1024,), None))
BIG = tuple(n for n, _ in PACK_ROWS)
WEIGHT_ORDER = ("mix_norm_pre", "w_in", "cv_w", "cv_b", "cv_ln_g", "cv_ln_b", "cv_pw_w", "cv_pw_b", "w_out",
                "mix_norm_post", "x_norm_pre", "mem_norm", "x_wq", "x_wk", "x_wv", "x_wo", "x_norm_post",
                "ffn_norm_pre", "ffn_w_up", "ffn_conv_w", "ffn_conv_b", "ffn_w_down", "ffn_norm_post")


def _flat_rows(parts):
    v = jnp.concatenate([p.reshape(-1) for p in parts])
    rows = -(-v.shape[0] // (8 * D_MODEL)) * 8
    return jnp.pad(v, (0, rows * D_MODEL - v.shape[0])).reshape(rows, D_MODEL)


def _layer_fwd(h0, mem, p, cos, sin, tag):
    sv = {"h0": h0}
    n1, u = _rms_mm(h0, _row(p["mix_norm_pre"]), p["w_in"], tm=512, tn=1408, out_dtype=F32, name=f"mix_in{tag}")
    a_out = _sb_fwd(u, name=f"sb_fwd{tag}")
    b_out, c = _cv_fwd(u, p["cv_w"], _row(p["cv_b"]), _row(p["cv_ln_g"]), _row(p["cv_ln_b"]),
                       p["cv_pw_w"].astype(BF16), _row(p["cv_pw_b"]), name=f"cv_fwd{tag}")
    qkv = _rope_perm(u, cos, sin, name=f"rope_perm{tag}")
    o_p, l_p = _dl_fwd(qkv, name=f"dl_fwd{tag}")
    c_out, o_dl, lse = _dl_mix(o_p, l_p, name=f"dl_mix{tag}")
    cat = jnp.concatenate([a_out, b_out, c_out], axis=1)
    y1, h1 = _mm_post(cat, p["w_out"], h0, _row(p["mix_norm_post"]), tm=256, name=f"mix_out{tag}")
    sv.update(n1=n1, u=u, c=c, qkv=qkv, o_dl=o_dl, lse=lse, cat=cat, y1=y1, h1=h1)

    n2, q = _rms_mm(h1, _row(p["x_norm_pre"]), p["x_wq"], tm=512, tn=1024, out_dtype=BF16, name=f"xa_q{tag}")
    wkv = jnp.concatenate([p["x_wk"], p["x_wv"]], axis=1)
    mem_n, kv = _rms_mm(mem, _row(p["mem_norm"]), wkv, tm=mem.shape[0], tn=1024, out_dtype=BF16, name=f"xa_kv{tag}")
    k, v = kv[:, :D_MODEL], kv[:, D_MODEL:]
    o_x = _xa_fwd(q, k, v, name=f"xa_fwd{tag}")
    y2, h2 = _mm_post(o_x, p["x_wo"], h1, _row(p["x_norm_post"]), tm=256, name=f"xa_out{tag}")
    sv.update(n2=n2, q=q, mem_n=mem_n, k=k, v=v, o_x=o_x, y2=y2, h2=h2, wkv=wkv)

    n3, up = _rms_mm(h2, _row(p["ffn_norm_pre"]), p["ffn_w_up"], tm=512, tn=1408, out_dtype=F32, name=f"ffn_up{tag}")
    act = _ff_gate_fwd(up, p["ffn_conv_w"], _row(p["ffn_conv_b"]), name=f"ffn_gate{tag}")
    y3, h3 = _mm_post(act, p["ffn_w_down"], h2, _row(p["ffn_norm_post"]), tm=256, name=f"ffn_down{tag}")
    sv.update(n3=n3, up=up, act=act, y3=y3)
    return h3, sv


def _layer_bwd(dh3, mem, p, sv, cos, sin, tag):
    g = {}
    s8 = lambda part: part.sum(axis=0)

    dy3, dgp = _rms_bwd(sv["y3"], _row(p["ffn_norm_post"]), dh3, None, out_dtype=BF16, tm=256, name=f"ffn_post_b{tag}")
    g["ffn_norm_post"] = s8(dgp)
    dact = _mm_nt(dy3, p["ffn_w_down"], tm=512, tn=1408, out_dtype=F32, name=f"ffn_down_bx{tag}")
    g["ffn_w_down"] = _mm_tn(sv["act"], dy3, tk=1408, tn=1024, tm=512, name=f"ffn_down_bw{tag}")
    dgu, dvu, dcw, dcb = _ff_gate_bwd(sv["up"], dact, p["ffn_conv_w"], _row(p["ffn_conv_b"]), name=f"ffn_gate_b{tag}")
    g["ffn_conv_w"] = jnp.concatenate([dcw[0], dcw[1]], axis=1).reshape(3, 8, 2 * D_FF).sum(axis=1)
    g["ffn_conv_b"] = jnp.concatenate([dcb[0], dcb[1]], axis=1).sum(axis=0)
    dup = jnp.concatenate([dgu, dvu], axis=1)
    dn3 = _mm_nt(dup, p["ffn_w_up"], tm=256, tn=512, out_dtype=F32, name=f"ffn_up_bx{tag}")
    g["ffn_w_up"] = _mm_tn(sv["n3"], dup, tk=512, tn=1408, tm=512, name=f"ffn_up_bw{tag}")
    dh2, dgp = _rms_bwd(sv["h2"], _row(p["ffn_norm_pre"]), dn3, dh3, out_dtype=F32, tm=256, name=f"ffn_pre_b{tag}")
    g["ffn_norm_pre"] = s8(dgp)

    dy2, dgp = _rms_bwd(sv["y2"], _row(p["x_norm_post"]), dh2, None, out_dtype=BF16, tm=256, name=f"xa_post_b{tag}")
    g["x_norm_post"] = s8(dgp)
    do_x = _mm_nt(dy2, p["x_wo"], tm=512, tn=1024, out_dtype=BF16, name=f"xa_out_bx{tag}")
    g["x_wo"] = _mm_tn(sv["o_x"], dy2, tk=512, tn=1024, tm=512, name=f"xa_out_bw{tag}")
    dq, dk, dv = _xa_bwd(sv["q"], sv["k"], sv["v"], do_x, name=f"xa_bwd{tag}")
    dn2 = _mm_nt(dq, p["x_wq"], tm=512, tn=1024, out_dtype=F32, name=f"xa_q_bx{tag}")
    g["x_wq"] = _mm_tn(sv["n2"], dq, tk=512, tn=1024, tm=512, name=f"xa_q_bw{tag}")
    dkv = jnp.concatenate([dk, dv], axis=1).astype(BF16)
    nm = mem.shape[0]
    dmem_n = _mm_nt(dkv, sv["wkv"], tm=nm, tn=1024, out_dtype=F32, name=f"xa_kv_bx{tag}")
    dwkv = _mm_tn(sv["mem_n"], dkv, tk=512, tn=2048, tm=nm, name=f"xa_kv_bw{tag}")
    g["x_wk"], g["x_wv"] = dwkv[:, :D_MODEL], dwkv[:, D_MODEL:]
    _, dgp = _rms_bwd(mem, _row(p["mem_norm"]), dmem_n, None, out_dtype=BF16, tm=nm, name=f"xa_mem_b{tag}")
    g["mem_norm"] = s8(dgp)
    dh1, dgp = _rms_bwd(sv["h1"], _row(p["x_norm_pre"]), dn2, dh2, out_dtype=F32, tm=256, name=f"xa_pre_b{tag}")
    g["x_norm_pre"] = s8(dgp)

    dy1, dgp = _rms_bwd(sv["y1"], _row(p["mix_norm_post"]), dh1, None, out_dtype=BF16, tm=256, name=f"mix_post_b{tag}")
    g["mix_norm_post"] = s8(dgp)
    dcat = _mm_nt(dy1, p["w_out"], tm=512, tn=1024, out_dtype=F32, name=f"mix_out_bx{tag}")
    g["w_out"] = _mm_tn(sv["cat"], dy1, tk=512, tn=1024, tm=512, name=f"mix_out_bw{tag}")
    u = sv["u"]
    dq_sb, dk_sb, dv_sb = _sb_bwd(u, dcat, name=f"sb_bwd{tag}")
    pw_b16 = p["cv_pw_w"].astype(BF16)
    dc, dpw, vec = _cv_bwd_local(sv["c"], dcat, _row(p["cv_ln_g"]), _row(p["cv_ln_b"]), pw_b16, name=f"cv_bwd_a{tag}")
    g["cv_pw_w"] = dpw
    vec = vec.reshape(3, 8, CV_W).sum(axis=1)
    g["cv_pw_b"], g["cv_ln_g"], g["cv_ln_b"] = vec[0], vec[1], vec[2]
    du_cv, dcw, dcb = _cv_bwd_conv(u, dc, p["cv_w"], name=f"cv_bwd_b{tag}")
    g["cv_w"] = dcw.reshape(CV_K, 8, CV_W).sum(axis=1)
    g["cv_b"] = dcb.sum(axis=0)
    dop, stats = _dl_bwd_prep(dcat, sv["o_dl"], sv["lse"], name=f"dl_prep_b{tag}")
    cur, prev = _dl_bwd(sv["qkv"], dop, stats, name=f"dl_bwd{tag}")
    du_dl = _dl_bwd_finish(cur, prev, cos, sin, name=f"dl_fin_b{tag}")
    du = jnp.concatenate([dq_sb.astype(BF16), dk_sb.astype(BF16), dv_sb.astype(BF16), du_cv, du_dl], axis=1)
    dn1 = _mm_nt(du, p["w_in"], tm=512, tn=512, out_dtype=F32, name=f"mix_in_bx{tag}")
    g["w_in"] = _mm_tn(sv["n1"], du, tk=512, tn=1408, tm=512, name=f"mix_in_bw{tag}")
    dh0, dgp = _rms_bwd(sv["h0"], _row(p["mix_norm_pre"]), dn1, dh1, out_dtype=F32, tm=256, name=f"mix_pre_b{tag}")
    g["mix_norm_pre"] = s8(dgp)
    return dh0, g


def _step(x, mem, positions, loss_target, w, m, v):
    depth = w["w_in"].shape[0]
    xi, yi, ci = lax.axis_index("x"), lax.axis_index("y"), lax.axis_index("c")
    chip = 2 * xi + yi
    h = x[0]
    mem0 = mem[0]
    s_len = h.shape[0]

    pack = jnp.stack([jnp.concatenate([_to_pack_rows(n, w[n][l]) for n in BIG], axis=0) for l in range(depth)])
    gathered = _gather_weights(pack.astype(BF16), name="gather_weights")
    params = []
    for l in range(depth):
        p, off = {}, 0
        for n, rows in PACK_ROWS:
            p[n] = _full_from_blocks(n, gathered[:, l, off:off + rows, :])
            off += rows
        for n, _, _ in SMALL:
            p[n] = w[n][l]
        params.append(p)
    small_w = []
    for l in range(depth):
        for n, shape, axis in SMALL:
            if axis is not None:
                full = jnp.zeros(shape, F32)
                full = lax.dynamic_update_slice_in_dim(full, w[n][l], chip * w[n][l].shape[axis], axis)
                small_w.append(full * jnp.where(ci == 0, 1.0, 0.0))
    small_w_sum = _all_reduce_small(_flat_rows(small_w), name="gather_small_weights")
    off = 0
    for l in range(depth):
        for n, shape, axis in SMALL:
            if axis is not None:
                size = int(np.prod(shape))
                params[l][n] = small_w_sum.reshape(-1)[off:off + size].reshape(shape)
                off += size

    inv_freq = ROPE_THETA ** (-jnp.arange(HD // 2, dtype=F32) / (HD // 2))
    cos, sin = _rope_tables(positions.reshape(s_len, 1), jnp.tile(inv_freq, 4).reshape(1, LANES), name="rope_tables")

    saved = []
    for l in range(depth):
        h, sv = _layer_fwd(h, mem0, params[l], cos, sin, f"_l{l}")
        saved.append(sv)
    dh, sq = _loss_grad(h, loss_target[0], tm=256, name="loss_grad")
    loss = lax.psum(0.5 * jnp.sum(sq) / D_MODEL, ("x", "y", "c"))
    grads = [None] * depth
    for l in reversed(range(depth)):
        dh, grads[l] = _layer_bwd(dh, mem0, params[l], saved[l], cos, sin, f"_l{l}")
    grad_x = dh[None]

    gw = jnp.stack([jnp.concatenate([_blocks_from_full(n, grads[l][n]) for n in BIG], axis=1) for l in range(depth)])
    c_arr, j_arr = jnp.reshape(ci, (1,)).astype(jnp.int32), jnp.reshape(chip, (1,)).astype(jnp.int32)
    got = _swap_other_layer(gw, name="rs_swap_layers")
    s32, s16 = _chip_sum(gw, got, c_arr, name="rs_chip_sum")
    got16 = _scatter_chip_sums(s16, name="rs_scatter")
    ghalf = _mesh_sum(s32, got16, j_arr, name="rs_mesh_sum")
    gfull = _share_layers(ghalf, name="rs_share_layers")

    out_g, out_d, out_m, out_v = {}, {}, {}, {}
    off = 0
    for n, rows in PACK_ROWS:
        shard_shape = w[n].shape
        g_n = gfull[:, off:off + rows, :].reshape(shard_shape)
        off += rows
        flat = lambda a: a.reshape(-1, shard_shape[-1])
        d_n, m_n, v_n = _adamw(flat(w[n]), flat(g_n), flat(m[n]), flat(v[n]), name=f"adamw_{n}")
        out_g[n], out_d[n], out_m[n], out_v[n] = g_n, d_n.reshape(shard_shape), m_n.reshape(shard_shape), v_n.reshape(shard_shape)

    g_small = _all_reduce_small(_flat_rows([grads[l][n] for l in range(depth) for n, _, _ in SMALL]),
                                name="all_reduce_small_grads").reshape(-1)
    local_g, off = {}, 0
    for l in range(depth):
        for n, shape, axis in SMALL:
            size = int(np.prod(shape))
            full = g_small[off:off + size].reshape(shape)
            off += size
            if axis is not None:
                blk = w[n].shape[1 + axis]
                full = lax.dynamic_slice_in_dim(full, chip * blk, blk, axis)
            local_g.setdefault(n, []).append(full)
    names = [n for n, _, _ in SMALL]
    g_loc = {n: jnp.stack(local_g[n]) for n in names}
    d_s, m_s, v_s = _adamw(_flat_rows([w[n] for n in names]), _flat_rows([g_loc[n] for n in names]),
                           _flat_rows([m[n] for n in names]), _flat_rows([v[n] for n in names]), name="adamw_small")
    off = 0
    for n in names:
        size = int(np.prod(w[n].shape))
        take = lambda a: a.reshape(-1)[off:off + size].reshape(w[n].shape)
        out_g[n], out_d[n], out_m[n], out_v[n] = g_loc[n], take(d_s), take(m_s), take(v_s)
        off += size

    outs = [loss, grad_x]
    for group in (out_g, out_d, out_m, out_v):
        outs += [group[n] for n in WEIGHT_ORDER]
    return tuple(outs)


def kernel(x, mem, positions, mix_norm_pre, w_in, cv_w, cv_b, cv_ln_g, cv_ln_b, cv_pw_w, cv_pw_b, w_out, mix_norm_post, x_norm_pre, mem_norm, x_wq, x_wk, x_wv, x_wo, x_norm_post, ffn_norm_pre, ffn_w_up, ffn_conv_w, ffn_conv_b, ffn_w_down, ffn_norm_post, loss_target, m_mix_norm_pre, m_w_in, m_cv_w, m_cv_b, m_cv_ln_g, m_cv_ln_b, m_cv_pw_w, m_cv_pw_b, m_w_out, m_mix_norm_post, m_x_norm_pre, m_mem_norm, m_x_wq, m_x_wk, m_x_wv, m_x_wo, m_x_norm_post, m_ffn_norm_pre, m_ffn_w_up, m_ffn_conv_w, m_ffn_conv_b, m_ffn_w_down, m_ffn_norm_post, v_mix_norm_pre, v_w_in, v_cv_w, v_cv_b, v_cv_ln_g, v_cv_ln_b, v_cv_pw_w, v_cv_pw_b, v_w_out, v_mix_norm_post, v_x_norm_pre, v_mem_norm, v_x_wq, v_x_wk, v_x_wv, v_x_wo, v_x_norm_post, v_ffn_norm_pre, v_ffn_w_up, v_ffn_conv_w, v_ffn_conv_b, v_ffn_w_down, v_ffn_norm_post):
    w = dict(zip(WEIGHT_ORDER, (mix_norm_pre, w_in, cv_w, cv_b, cv_ln_g, cv_ln_b, cv_pw_w, cv_pw_b, w_out, mix_norm_post, x_norm_pre, mem_norm, x_wq, x_wk, x_wv, x_wo, x_norm_post, ffn_norm_pre, ffn_w_up, ffn_conv_w, ffn_conv_b, ffn_w_down, ffn_norm_post)))
    m = dict(zip(WEIGHT_ORDER, (m_mix_norm_pre, m_w_in, m_cv_w, m_cv_b, m_cv_ln_g, m_cv_ln_b, m_cv_pw_w, m_cv_pw_b, m_w_out, m_mix_norm_post, m_x_norm_pre, m_mem_norm, m_x_wq, m_x_wk, m_x_wv, m_x_wo, m_x_norm_post, m_ffn_norm_pre, m_ffn_w_up, m_ffn_conv_w, m_ffn_conv_b, m_ffn_w_down, m_ffn_norm_post)))
    v = dict(zip(WEIGHT_ORDER, (v_mix_norm_pre, v_w_in, v_cv_w, v_cv_b, v_cv_ln_g, v_cv_ln_b, v_cv_pw_w, v_cv_pw_b, v_w_out, v_mix_norm_post, v_x_norm_pre, v_mem_norm, v_x_wq, v_x_wk, v_x_wv, v_x_wo, v_x_norm_post, v_ffn_norm_pre, v_ffn_w_up, v_ffn_conv_w, v_ffn_conv_b, v_ffn_w_down, v_ffn_norm_post)))
    return _step(x, mem, positions, loss_target, w, m, v)
```

```python
import functools

import jax
import jax.numpy as jnp
import numpy as np
from jax import lax
from jax.experimental import pallas as pl
from jax.experimental.pallas import tpu as pltpu

F32, BF16 = jnp.float32, jnp.bfloat16
MESH = pl.DeviceIdType.MESH
EPS = 1e-6
LANES = 128
BLK = 128
HD = 64
D_MODEL = 1024
D_FF = 2816
SB_W, CV_W, DL_W = 256, 256, 512
CV_K = 31
ROPE_THETA = 10000.0
DILATIONS = (1, 4, 16)
X_HEADS, X_HD = 4, 256
ADAM_LR, ADAM_B1, ADAM_B2, ADAM_EPS, ADAM_WD, ADAM_STEP = 0.001, 0.9, 0.999, 1e-08, 0.01, 10
NEG_INF = float("-inf")
MIB = 1 << 20

PACK_ROWS = (("w_in", 704), ("w_out", 256), ("x_wq", 256), ("x_wk", 256), ("x_wv", 256), ("x_wo", 256),
             ("ffn_w_up", 1408), ("ffn_w_down", 704))
PACK_RL = sum(r for _, r in PACK_ROWS)


def _cp(vmem_mb=48):
    return pltpu.CompilerParams(vmem_limit_bytes=vmem_mb * MIB)


def _dot(a, b):
    return jnp.dot(a, b, preferred_element_type=F32)


def _dot_nt(a, b):
    return lax.dot_general(a, b, (((1,), (1,)), ((), ())), preferred_element_type=F32)


def _dot_tn(a, b):
    return lax.dot_general(a, b, (((0,), (0,)), ((), ())), preferred_element_type=F32)


def _dot_hilo(x, m):
    hi = x.astype(BF16)
    lo = (x - hi.astype(F32)).astype(BF16)
    return _dot(hi, m) + _dot(lo, m)


def _rowsum8(x):
    t, c = x.shape
    return x.reshape(t // 8, 8, c).sum(axis=0)


def _acc_out(ref, i, val):
    @pl.when(i == 0)
    def _():
        ref[...] = val

    @pl.when(i > 0)
    def _():
        ref[...] += val


def _tile(n, cap, mult=8):
    t = min(n, cap)
    while n % t or t % mult:
        t -= 1
    return t


def _rms_mm(x, g, w, *, tm, tn, out_dtype, name):
    m, d = x.shape
    n_out = w.shape[1]

    def body(x_ref, g_ref, w_ref, n_ref, o_ref):
        @pl.when(pl.program_id(1) == 0)
        def _():
            xv = x_ref[...]
            r = lax.rsqrt(jnp.mean(xv * xv, axis=-1, keepdims=True) + EPS)
            n_ref[...] = (xv * r * g_ref[...]).astype(BF16)

        o_ref[...] = _dot(n_ref[...], w_ref[...]).astype(out_dtype)

    return pl.pallas_call(
        body, grid=(m // tm, n_out // tn), name=name,
        in_specs=[pl.BlockSpec((tm, d), lambda i, j: (i, 0)), pl.BlockSpec((1, d), lambda i, j: (0, 0)),
                  pl.BlockSpec((d, tn), lambda i, j: (0, j))],
        out_specs=[pl.BlockSpec((tm, d), lambda i, j: (i, 0)), pl.BlockSpec((tm, tn), lambda i, j: (i, j))],
        out_shape=[jax.ShapeDtypeStruct((m, d), BF16), jax.ShapeDtypeStruct((m, n_out), out_dtype)],
        compiler_params=_cp())(x, g, w)


def _mm_post(a, w, h, g, *, tm, name):
    m, k = a.shape
    d = w.shape[1]

    def body(a_ref, w_ref, h_ref, g_ref, y_ref, ho_ref):
        y = _dot(a_ref[...], w_ref[...])
        y_ref[...] = y
        r = lax.rsqrt(jnp.mean(y * y, axis=-1, keepdims=True) + EPS)
        ho_ref[...] = h_ref[...] + y * r * g_ref[...]

    return pl.pallas_call(
        body, grid=(m // tm,), name=name,
        in_specs=[pl.BlockSpec((tm, k), lambda i: (i, 0)), pl.BlockSpec((k, d), lambda i: (0, 0)),
                  pl.BlockSpec((tm, d), lambda i: (i, 0)), pl.BlockSpec((1, d), lambda i: (0, 0))],
        out_specs=[pl.BlockSpec((tm, d), lambda i: (i, 0)), pl.BlockSpec((tm, d), lambda i: (i, 0))],
        out_shape=[jax.ShapeDtypeStruct((m, d), F32), jax.ShapeDtypeStruct((m, d), F32)],
        compiler_params=_cp())(a, w, h, g)


def _mm_nt(a, w, *, tm, tn, out_dtype, name):
    m, k = a.shape
    n_out = w.shape[0]

    def body(a_ref, w_ref, o_ref):
        o_ref[...] = _dot_nt(a_ref[...], w_ref[...]).astype(out_dtype)

    return pl.pallas_call(
        body, grid=(n_out // tn, m // tm), name=name,
        in_specs=[pl.BlockSpec((tm, k), lambda j, i: (i, 0)), pl.BlockSpec((tn, k), lambda j, i: (j, 0))],
        out_specs=pl.BlockSpec((tm, tn), lambda j, i: (i, j)),
        out_shape=jax.ShapeDtypeStruct((m, n_out), out_dtype),
        compiler_params=_cp())(a, w)


def _mm_tn(x, dy, *, tk, tn, tm, name):
    m, k = x.shape
    n_out = dy.shape[1]

    def body(x_ref, d_ref, o_ref):
        _acc_out(o_ref, pl.program_id(2), _dot_tn(x_ref[...], d_ref[...]))

    return pl.pallas_call(
        body, grid=(k // tk, n_out // tn, m // tm), name=name,
        in_specs=[pl.BlockSpec((tm, tk), lambda a, b, c: (c, a)), pl.BlockSpec((tm, tn), lambda a, b, c: (c, b))],
        out_specs=pl.BlockSpec((tk, tn), lambda a, b, c: (a, b)),
        out_shape=jax.ShapeDtypeStruct((k, n_out), F32),
        compiler_params=_cp())(x, dy)


def _rms_bwd(x, g, dout, res, *, out_dtype, tm, name):
    m, d = x.shape
    has_res = res is not None

    def body(*refs):
        if has_res:
            x_ref, g_ref, d_ref, r_ref, dx_ref, dg_ref = refs
        else:
            x_ref, g_ref, d_ref, dx_ref, dg_ref = refs
        xv = x_ref[...]
        dv = d_ref[...].astype(F32)
        r = lax.rsqrt(jnp.mean(xv * xv, axis=-1, keepdims=True) + EPS)
        xh = xv * r
        dxh = dv * g_ref[...]
        dx = r * (dxh - xh * jnp.mean(dxh * xh, axis=-1, keepdims=True))
        if has_res:
            dx = dx + r_ref[...]
        dx_ref[...] = dx.astype(out_dtype)
        _acc_out(dg_ref, pl.program_id(0), _rowsum8(dv * xh))

    row = pl.BlockSpec((tm, d), lambda i: (i, 0))
    ins = [row, pl.BlockSpec((1, d), lambda i: (0, 0)), row] + ([row] if has_res else [])
    args = (x, g, dout) + ((res,) if has_res else ())
    return pl.pallas_call(
        body, grid=(m // tm,), name=name, in_specs=ins,
        out_specs=[row, pl.BlockSpec((8, d), lambda i: (0, 0))],
        out_shape=[jax.ShapeDtypeStruct((m, d), out_dtype), jax.ShapeDtypeStruct((8, d), F32)],
        compiler_params=_cp())(*args)


def _loss_grad(h, tgt, *, tm, name):
    m, d = h.shape

    def body(h_ref, t_ref, dh_ref, p_ref):
        e = h_ref[...] - t_ref[...]
        dh_ref[...] = e / d
        _acc_out(p_ref, pl.program_id(0), _rowsum8(e * e))

    row = pl.BlockSpec((tm, d), lambda i: (i, 0))
    return pl.pallas_call(
        body, grid=(m // tm,), name=name, in_specs=[row, row],
        out_specs=[row, pl.BlockSpec((8, d), lambda i: (0, 0))],
        out_shape=[jax.ShapeDtypeStruct((m, d), F32), jax.ShapeDtypeStruct((8, d), F32)],
        compiler_params=_cp())(h, tgt)


def _adamw(w, g, m, v, *, name):
    r, c = w.shape
    tr = _tile(r, 256)

    def body(w_ref, g_ref, m_ref, v_ref, d_ref, mo_ref, vo_ref):
        gv = g_ref[...]
        m2 = ADAM_B1 * m_ref[...] + (1.0 - ADAM_B1) * gv
        v2 = ADAM_B2 * v_ref[...] + (1.0 - ADAM_B2) * jnp.square(gv)
        m_hat = m2 / (1.0 - ADAM_B1 ** ADAM_STEP)
        v_hat = v2 / (1.0 - ADAM_B2 ** ADAM_STEP)
        d_ref[...] = -ADAM_LR * (m_hat / (jnp.sqrt(v_hat) + ADAM_EPS) + ADAM_WD * w_ref[...])
        mo_ref[...] = m2
        vo_ref[...] = v2

    blk = pl.BlockSpec((tr, c), lambda i: (i, 0))
    return pl.pallas_call(
        body, grid=(r // tr,), name=name, in_specs=[blk] * 4, out_specs=[blk] * 3,
        out_shape=[jax.ShapeDtypeStruct((r, c), F32)] * 3, compiler_params=_cp())(w, g, m, v)


def _head_masks():
    lane = lax.broadcasted_iota(jnp.int32, (BLK, LANES), 1)
    row = lax.broadcasted_iota(jnp.int32, (BLK, LANES), 0)
    return lane, row, lane < HD


def _sb_scores(q_a, k, before):
    z = _dot_nt(q_a, k)
    sp = jnp.log1p(jnp.exp(-jnp.abs(z)))
    ls_pos = jnp.minimum(z, 0.0) - sp
    lkeep = jnp.where(before, ls_pos - z, 0.0)
    return ls_pos, lkeep


SB_DEAD = -104.0


def _sb_alive(jj, i, carry):
    return jnp.logical_and(jj <= i, jnp.max(carry) > SB_DEAD)


SB_QB = 2
SB_ROWS = SB_QB * 2 * BLK


def _sb_before(jj):
    lane = lax.broadcasted_iota(jnp.int32, (SB_ROWS, LANES), 1)
    row = lax.broadcasted_iota(jnp.int32, (SB_ROWS, LANES), 0)
    below_diag = jj - (SB_QB - 1) + row // (2 * BLK)
    return jnp.logical_or(below_diag > 0, jnp.logical_and(below_diag == 0, lane < row % BLK))


def _sb_stack(x, lane_h):
    return jnp.concatenate([_stack_heads(x[b * BLK:(b + 1) * BLK], lane_h) for b in range(SB_QB)], axis=0)


def _sb_unstack(x, lane_h):
    return jnp.concatenate([jnp.where(lane_h, x[2 * b * BLK:(2 * b + 1) * BLK], x[(2 * b + 1) * BLK:(2 * b + 2) * BLK])
                            for b in range(SB_QB)], axis=0)


def _sb_fwd(u, *, name):
    s_len = u.shape[0]
    qrows = SB_QB * BLK

    def body(q_ref, k_ref, v_ref, o_ref):
        top = pl.program_id(1) * SB_QB + SB_QB - 1
        lane, row, lane_h = _head_masks()
        suffix = (row > lane).astype(BF16)
        qs = _sb_stack(q_ref[...] * 0.125, lane_h)

        def step(state):
            jj, cc, acc = state
            off = pl.multiple_of((top - jj) * BLK, BLK)
            k = k_ref[pl.ds(off, BLK), :].astype(BF16)
            v = v_ref[pl.ds(off, BLK), :].astype(BF16)
            before = _sb_before(jj)
            ls_pos, lkeep = _sb_scores(qs, k, before)
            between = _dot_hilo(lkeep, suffix) + cc
            att = jnp.where(before, jnp.exp(ls_pos + between), 0.0)
            return jj + 1, cc + jnp.sum(lkeep, axis=1, keepdims=True), acc + _dot(att.astype(BF16), v)

        init = (jnp.int32(0), jnp.zeros((SB_ROWS, 1), F32), jnp.zeros((SB_ROWS, LANES), F32))
        acc = lax.while_loop(lambda st: _sb_alive(st[0], top, st[1]), step, init)[2]
        o_ref[...] = _sb_unstack(acc, lane_h).astype(BF16)

    return pl.pallas_call(
        body, grid=(2, s_len // qrows), name=name,
        in_specs=[pl.BlockSpec((qrows, LANES), lambda hp, i: (i, hp)),
                  pl.BlockSpec((s_len, LANES), lambda hp, i: (0, 2 + hp)),
                  pl.BlockSpec((s_len, LANES), lambda hp, i: (0, 4 + hp))],
        out_specs=pl.BlockSpec((qrows, LANES), lambda hp, i: (i, hp)),
        out_shape=jax.ShapeDtypeStruct((s_len, SB_W), BF16), compiler_params=_cp())(u, u, u)


def _sb_bwd(u, dcat, *, name):
    s_len = u.shape[0]
    nq = s_len // BLK
    qrows = SB_QB * BLK

    def body(q_ref, k_ref, v_ref, do_ref, dq_ref, dk_ref, dv_ref, g_scr, b_scr):
        step = pl.program_id(1)
        top = step * SB_QB + SB_QB - 1
        lane, row, lane_h = _head_masks()
        suffix = (row > lane).astype(BF16)
        prefix = (row < lane).astype(BF16)
        qf = q_ref[...]
        qs = _sb_stack(qf * 0.125, lane_h)
        qu = _sb_stack(qf, lane_h)
        dos = _sb_stack(do_ref[...], lane_h)

        @pl.when(step == 0)
        def _():
            dk_ref[...] = jnp.zeros_like(dk_ref)
            dv_ref[...] = jnp.zeros_like(dv_ref)

        def down(state):
            jj, cc = state
            j = top - jj
            off = pl.multiple_of(j * BLK, BLK)
            k = k_ref[pl.ds(off, BLK), :].astype(BF16)
            v = v_ref[pl.ds(off, BLK), :].astype(BF16)
            before = _sb_before(jj)
            ls_pos, lkeep = _sb_scores(qs, k, before)
            between = _dot_hilo(lkeep, suffix) + cc
            att = jnp.where(before, jnp.exp(ls_pos + between), 0.0)
            g_scr[j] = att * _dot_nt(dos, v)
            b_scr[j] = jnp.exp(ls_pos)
            dv_ref[pl.ds(off, BLK), :] += _dot_tn(att.astype(BF16), dos)
            return jj + 1, cc + jnp.sum(lkeep, axis=1, keepdims=True)

        zc = jnp.zeros((SB_ROWS, 1), F32)
        visited = lax.while_loop(lambda st: _sb_alive(st[0], top, st[1]), down, (jnp.int32(0), zc))[0]

        def up(j, carry):
            pc, dq = carry
            off = pl.multiple_of(j * BLK, BLK)
            k = k_ref[pl.ds(off, BLK), :].astype(BF16)
            g, beta = g_scr[j], b_scr[j]
            below = _dot_hilo(g, prefix) + pc
            dz = (jnp.where(_sb_before(top - j), g * (1.0 - beta) - beta * below, 0.0) * 0.125).astype(BF16)
            dk_ref[pl.ds(off, BLK), :] += _dot_tn(dz, qu)
            return pc + jnp.sum(g, axis=1, keepdims=True), dq + _dot(dz, k)

        dq = lax.fori_loop(top + 1 - visited, top + 1, up, (zc, jnp.zeros((SB_ROWS, LANES), F32)))[1]
        dq_ref[...] = _sb_unstack(dq, lane_h)

    col = lambda c0: pl.BlockSpec((s_len, LANES), lambda hp, i: (0, c0 + hp))
    blk = pl.BlockSpec((qrows, LANES), lambda hp, i: (i, hp))
    acc = pl.BlockSpec((s_len, LANES), lambda hp, i: (0, hp))
    return pl.pallas_call(
        body, grid=(2, s_len // qrows), name=name, in_specs=[blk, col(2), col(4), blk],
        out_specs=[blk, acc, acc], out_shape=[jax.ShapeDtypeStruct((s_len, SB_W), F32)] * 3,
        scratch_shapes=[pltpu.VMEM((nq, SB_ROWS, LANES), F32), pltpu.VMEM((nq, SB_ROWS, LANES), F32)],
        compiler_params=_cp(56))(u, u, u, dcat)


CV_T = 512
CV_H = 32


def _cv_specs(s_len):
    cur = lambda c: pl.BlockSpec((CV_T, CV_W), lambda i: (i, c))
    prev = lambda c: pl.BlockSpec((CV_H, CV_W), lambda i: (jnp.maximum(i * (CV_T // CV_H) - 1, 0), c))
    nxt = lambda c: pl.BlockSpec((CV_H, CV_W),
                                 lambda i: (jnp.minimum((i + 1) * (CV_T // CV_H), s_len // CV_H - 1), c))
    full = lambda r: pl.BlockSpec((r, CV_W), lambda i: (0, 0))
    return cur, prev, nxt, full


def _glu_into(gp_ref, val_ref, gate_ref, valp_ref, gatep_ref, i):
    gp_ref[0:CV_H, :] = jnp.where(i > 0, valp_ref[...] * jax.nn.sigmoid(gatep_ref[...]), 0.0)
    gp_ref[CV_H:, :] = val_ref[...] * jax.nn.sigmoid(gate_ref[...])


def _cv_fwd(u, cv_w, cv_b, ln_g, ln_b, pw_w, pw_b, *, name):
    s_len = u.shape[0]
    cur, prev, _, full = _cv_specs(s_len)

    def body(val_ref, gate_ref, valp_ref, gatep_ref, w_ref, b_ref, g_ref, be_ref, pw_ref, pb_ref,
             o_ref, c_ref, gp_ref):
        _glu_into(gp_ref, val_ref, gate_ref, valp_ref, gatep_ref, pl.program_id(0))
        acc = jnp.zeros((CV_T, CV_W), F32) + b_ref[...]
        for k in range(CV_K):
            acc = acc + w_ref[k:k + 1, :] * gp_ref[pl.ds(CV_H - CV_K + 1 + k, CV_T), :]
        c_ref[...] = acc
        mu = jnp.mean(acc, axis=-1, keepdims=True)
        xc = acc - mu
        xh = xc * lax.rsqrt(jnp.mean(xc * xc, axis=-1, keepdims=True) + EPS)
        a = xh * g_ref[...] + be_ref[...]
        s = a * jax.nn.sigmoid(a)
        o_ref[...] = (_dot(s.astype(BF16), pw_ref[...]) + pb_ref[...]).astype(BF16)

    return pl.pallas_call(
        body, grid=(s_len // CV_T,), name=name,
        in_specs=[cur(3), cur(4), prev(3), prev(4), full(CV_K), full(1), full(1), full(1), full(CV_W), full(1)],
        out_specs=[cur(0), cur(0)],
        out_shape=[jax.ShapeDtypeStruct((s_len, CV_W), BF16), jax.ShapeDtypeStruct((s_len, CV_W), F32)],
        scratch_shapes=[pltpu.VMEM((CV_T + CV_H, CV_W), F32)], compiler_params=_cp())(
            u, u, u, u, cv_w, cv_b, ln_g, ln_b, pw_w, pw_b)


def _cv_bwd_local(c, dcat, ln_g, ln_b, pw_w, *, name):
    s_len = c.shape[0]
    cur, _, _, full = _cv_specs(s_len)

    def body(c_ref, db_ref, g_ref, be_ref, pw_ref, dc_ref, dpw_ref, vec_ref):
        i = pl.program_id(0)
        cv = c_ref[...]
        db = db_ref[...]
        mu = jnp.mean(cv, axis=-1, keepdims=True)
        xc = cv - mu
        rstd = lax.rsqrt(jnp.mean(xc * xc, axis=-1, keepdims=True) + EPS)
        xh = xc * rstd
        a = xh * g_ref[...] + be_ref[...]
        sg = jax.nn.sigmoid(a)
        s = a * sg
        dbb = db.astype(BF16)
        ds = _dot_nt(dbb, pw_ref[...])
        da = ds * (sg * (1.0 + a * (1.0 - sg)))
        dxh = da * g_ref[...]
        dc_ref[...] = rstd * (dxh - jnp.mean(dxh, axis=-1, keepdims=True)
                              - xh * jnp.mean(dxh * xh, axis=-1, keepdims=True))
        _acc_out(dpw_ref, i, _dot_tn(s.astype(BF16), dbb))
        _acc_out(vec_ref, i, jnp.concatenate([_rowsum8(db), _rowsum8(da * xh), _rowsum8(da)], axis=0))

    return pl.pallas_call(
        body, grid=(s_len // CV_T,), name=name,
        in_specs=[cur(0), cur(1), full(1), full(1), full(CV_W)],
        out_specs=[cur(0), full(CV_W), full(24)],
        out_shape=[jax.ShapeDtypeStruct((s_len, CV_W), F32), jax.ShapeDtypeStruct((CV_W, CV_W), F32),
                   jax.ShapeDtypeStruct((24, CV_W), F32)], compiler_params=_cp())(c, dcat, ln_g, ln_b, pw_w)


def _cv_bwd_conv(u, dc, cv_w, *, name):
    s_len = u.shape[0]
    cur, prev, nxt, full = _cv_specs(s_len)
    last = s_len // CV_T - 1

    def body(val_ref, gate_ref, valp_ref, gatep_ref, dc_ref, dcn_ref, w_ref, du_ref, dw_ref, dbias_ref,
             gp_ref, dcp_ref):
        i = pl.program_id(0)
        _glu_into(gp_ref, val_ref, gate_ref, valp_ref, gatep_ref, i)
        dcv = dc_ref[...]
        dcp_ref[0:CV_T, :] = dcv
        dcp_ref[CV_T:, :] = jnp.where(i < last, dcn_ref[...], 0.0)
        dg = jnp.zeros((CV_T, CV_W), F32)
        parts = []
        for k in range(CV_K):
            dg = dg + w_ref[k:k + 1, :] * dcp_ref[pl.ds(CV_K - 1 - k, CV_T), :]
            parts.append(_rowsum8(dcv * gp_ref[pl.ds(CV_H - CV_K + 1 + k, CV_T), :]))
        _acc_out(dw_ref, i, jnp.concatenate(parts, axis=0))
        _acc_out(dbias_ref, i, _rowsum8(dcv))
        val = val_ref[...]
        sg = jax.nn.sigmoid(gate_ref[...])
        du_ref[:, 0:CV_W] = (dg * sg).astype(BF16)
        du_ref[:, CV_W:] = (dg * val * sg * (1.0 - sg)).astype(BF16)

    return pl.pallas_call(
        body, grid=(s_len // CV_T,), name=name,
        in_specs=[cur(3), cur(4), prev(3), prev(4), cur(0), nxt(0), full(CV_K)],
        out_specs=[pl.BlockSpec((CV_T, 2 * CV_W), lambda i: (i, 0)), full(CV_K * 8), full(8)],
        out_shape=[jax.ShapeDtypeStruct((s_len, 2 * CV_W), BF16), jax.ShapeDtypeStruct((CV_K * 8, CV_W), F32),
                   jax.ShapeDtypeStruct((8, CV_W), F32)],
        scratch_shapes=[pltpu.VMEM((CV_T + CV_H, CV_W), F32), pltpu.VMEM((CV_T + CV_H, CV_W), F32)],
        compiler_params=_cp())(u, u, u, u, dc, dc, cv_w)


def _rope_tables(pos_col, inv_freq_row, *, name):
    s_len = pos_col.shape[0]

    def body(p_ref, f_ref, cos_ref, sin_ref):
        ang = p_ref[...].astype(F32) * f_ref[...]
        lane = lax.broadcasted_iota(jnp.int32, (s_len, LANES), 1)
        sn = jnp.sin(ang)
        cos_ref[...] = jnp.cos(ang)
        sin_ref[...] = jnp.where(lane % HD < HD // 2, -sn, sn)

    return pl.pallas_call(body, name=name, out_shape=[jax.ShapeDtypeStruct((s_len, LANES), F32)] * 2,
                          compiler_params=_cp())(pos_col, inv_freq_row)


def _rot_half(x):
    lane = lax.broadcasted_iota(jnp.int32, x.shape, 1)
    return jnp.where(lane % HD < HD // 2, pltpu.roll(x, LANES - HD // 2, 1), pltpu.roll(x, HD // 2, 1))


def _permute_rows(dst_ref, src_ref, d, dtype):
    s_len = src_ref.shape[0]
    seg = s_len // d
    if d == 1:
        dst_ref[...] = src_ref[...].astype(dtype)
        return
    for r in range(d):
        dst_ref[r * seg:(r + 1) * seg, :] = src_ref[pl.ds(r, seg, stride=d), :].astype(dtype)


def _unpermute_rows(dst_ref, src_ref, d):
    s_len = src_ref.shape[0]
    seg = s_len // d
    if d == 1:
        dst_ref[...] = src_ref[...]
        return
    for r in range(d):
        dst_ref[pl.ds(r, seg, stride=d), :] = src_ref[r * seg:(r + 1) * seg, :]


def _rope_perm(u, cos, sin, *, name):
    s_len = u.shape[0]

    def body(x_ref, cos_ref, sin_ref, o_ref, scr):
        a = pl.program_id(0)
        x = x_ref[...]
        rot = a < 2
        scr[...] = x * jnp.where(rot, cos_ref[...], 1.0) + _rot_half(x) * jnp.where(rot, sin_ref[...], 0.0)
        for n, d in enumerate(DILATIONS):
            _permute_rows(o_ref.at[n], scr, d, BF16)

    tab = pl.BlockSpec((s_len, LANES), lambda a, cb: (0, 0))
    return pl.pallas_call(
        body, grid=(3, 4), name=name,
        in_specs=[pl.BlockSpec((s_len, LANES), lambda a, cb: (0, 10 + 4 * a + cb)), tab, tab],
        out_specs=pl.BlockSpec((None, 3, s_len, LANES), lambda a, cb: (a, 0, 0, cb)),
        out_shape=jax.ShapeDtypeStruct((3, 3, s_len, DL_W), BF16),
        scratch_shapes=[pltpu.VMEM((s_len, LANES), F32)], compiler_params=_cp())(u, cos, sin)


DL_UNROLL = 4


def _dl_band(rows):
    lane = lax.broadcasted_iota(jnp.int32, (rows, LANES), 1)
    row = lax.broadcasted_iota(jnp.int32, (rows, LANES), 0) % BLK
    return lane <= row, lane >= row


def _dl_first(s_len, n, i):
    nb = jnp.where(n == 0, s_len // BLK, jnp.where(n == 1, s_len // (BLK * DILATIONS[1]),
                                                   s_len // (BLK * DILATIONS[2])))
    return lax.rem(i, nb) == 0


def _stack_heads(x, lane_h):
    return jnp.concatenate([jnp.where(lane_h, x, 0.0), jnp.where(lane_h, 0.0, x)], axis=0).astype(BF16)


def _dl_rows(i):
    cur = pl.ds(pl.multiple_of(i * BLK, BLK), BLK)
    prev = pl.ds(pl.multiple_of(jnp.maximum(i - 1, 0) * BLK, BLK), BLK)
    return cur, prev


def _dl_in_specs(s_len):
    return [pl.BlockSpec((None, None, s_len, LANES), functools.partial(lambda a, n, hp: (a, n, 0, hp), a))
            for a in range(3)]


def _dl_fwd(qkv, *, name):
    s_len = qkv.shape[2]

    def body(q_ref, k_ref, v_ref, o_ref, l_ref):
        n = pl.program_id(0)
        lane_h = _head_masks()[2]
        band_c, band_p = _dl_band(2 * BLK)
        ones = jnp.ones((BLK, LANES), BF16)

        @pl.loop(0, s_len // BLK, step=DL_UNROLL)
        def _(i0):
            blocks = [i0 + t for t in range(DL_UNROLL)]
            rows = [_dl_rows(i) for i in blocks]
            scores = []
            for cur, prev in rows:
                qs = _stack_heads(q_ref[cur, :] * 0.125, lane_h)
                scores.append((_dot_nt(qs, k_ref[cur, :]), _dot_nt(qs, k_ref[prev, :])))
            probs = []
            for i, (sc, sp) in zip(blocks, scores):
                sc = jnp.where(band_c, sc, NEG_INF)
                sp = jnp.where(jnp.logical_and(band_p, jnp.logical_not(_dl_first(s_len, n, i))), sp, NEG_INF)
                m = jnp.max(jnp.maximum(sc, sp), axis=1, keepdims=True)
                probs.append((jnp.exp(sc - m).astype(BF16), jnp.exp(sp - m).astype(BF16), m))
            for (cur, prev), (pc, pp, m) in zip(rows, probs):
                r = (_dot(pc, jnp.concatenate([v_ref[cur, :], ones], axis=1))
                     + _dot(pp, jnp.concatenate([v_ref[prev, :], ones], axis=1)))
                den = jnp.where(lane_h, r[:BLK, LANES:], r[BLK:, LANES:])
                o_ref[cur, :] = jnp.where(lane_h, r[:BLK, :LANES], r[BLK:, :LANES]) / den
                l_ref[cur, :] = jnp.where(lane_h, m[:BLK], m[BLK:]) + jnp.log(den)

    out = pl.BlockSpec((None, s_len, LANES), lambda n, hp: (n, 0, hp))
    return pl.pallas_call(
        body, grid=(3, 4), name=name, in_specs=_dl_in_specs(s_len), out_specs=[out, out],
        out_shape=[jax.ShapeDtypeStruct((3, s_len, DL_W), F32)] * 2, compiler_params=_cp())(qkv, qkv, qkv)


def _dl_mix(o_p, l_p, *, name):
    s_len = o_p.shape[1]

    def body(o_ref, l_ref, ob_ref, of_ref, lt_ref, o_scr, l_scr):
        n = pl.program_id(1)
        for k, d in enumerate(DILATIONS):
            @pl.when(n == k)
            def _(k=k, d=d):
                _unpermute_rows(o_scr.at[k], o_ref, d)
                _unpermute_rows(l_scr.at[k], l_ref, d)

        @pl.when(n == 2)
        def _():
            l0, l1, l2 = l_scr[0], l_scr[1], l_scr[2]
            m = jnp.maximum(jnp.maximum(l0, l1), l2)
            e0, e1, e2 = jnp.exp(l0 - m), jnp.exp(l1 - m), jnp.exp(l2 - m)
            den = e0 + e1 + e2
            o = (e0 / den) * o_scr[0] + (e1 / den) * o_scr[1] + (e2 / den) * o_scr[2]
            of_ref[...] = o
            ob_ref[...] = o.astype(BF16)
            lt_ref[...] = m + jnp.log(den)

    inb = pl.BlockSpec((None, s_len, LANES), lambda cb, n: (n, 0, cb))
    outb = pl.BlockSpec((s_len, LANES), lambda cb, n: (0, cb))
    return pl.pallas_call(
        body, grid=(4, 3), name=name, in_specs=[inb, inb], out_specs=[outb, outb, outb],
        out_shape=[jax.ShapeDtypeStruct((s_len, DL_W), BF16), jax.ShapeDtypeStruct((s_len, DL_W), F32),
                   jax.ShapeDtypeStruct((s_len, DL_W), F32)],
        scratch_shapes=[pltpu.VMEM((3, s_len, LANES), F32), pltpu.VMEM((3, s_len, LANES), F32)],
        compiler_params=_cp())(o_p, l_p)


def _dl_bwd_prep(dcat, o, lse, *, name):
    s_len = o.shape[0]

    def body(do_ref, o_ref, l_ref, dop_ref, st_ref, d_scr):
        n = pl.program_id(1)

        @pl.when(n == 0)
        def _():
            r0 = lax.broadcasted_iota(jnp.int32, (LANES, LANES), 0) // HD
            r1 = lax.broadcasted_iota(jnp.int32, (LANES, LANES), 1) // HD
            d_scr[...] = _dot_hilo(do_ref[...] * o_ref[...], (r0 == r1).astype(BF16))

        for k, d in enumerate(DILATIONS):
            @pl.when(n == k)
            def _(d=d):
                _permute_rows(dop_ref, do_ref, d, BF16)
                _permute_rows(st_ref.at[0], d_scr, d, F32)
                _permute_rows(st_ref.at[1], l_ref, d, F32)

    nat = lambda c0: pl.BlockSpec((s_len, LANES), lambda cb, n: (0, c0 + cb))
    return pl.pallas_call(
        body, grid=(4, 3), name=name, in_specs=[nat(4), nat(0), nat(0)],
        out_specs=[pl.BlockSpec((None, s_len, LANES), lambda cb, n: (n, 0, cb)),
                   pl.BlockSpec((2, None, s_len, LANES), lambda cb, n: (0, n, 0, cb))],
        out_shape=[jax.ShapeDtypeStruct((3, s_len, DL_W), BF16), jax.ShapeDtypeStruct((2, 3, s_len, DL_W), F32)],
        scratch_shapes=[pltpu.VMEM((s_len, LANES), F32)], compiler_params=_cp())(dcat, o, lse)


def _dl_bwd(qkv, dop, stats, *, name):
    s_len = qkv.shape[2]

    def body(q_ref, k_ref, v_ref, do_ref, st_ref, cur_ref, prev_ref):
        n = pl.program_id(0)
        lane_h = _head_masks()[2]
        band_c, band_p = _dl_band(2 * BLK)

        def per_head(x):
            xr = pltpu.roll(x, HD, 1)
            return jnp.concatenate([jnp.where(lane_h, x, xr), jnp.where(lane_h, xr, x)], axis=0)

        @pl.loop(0, s_len // BLK, step=DL_UNROLL)
        def _(i0):
            blocks = [i0 + t for t in range(DL_UNROLL)]
            rows = [_dl_rows(i) for i in blocks]
            stage1 = []
            for cur, prev in rows:
                qs = _stack_heads(q_ref[cur, :] * 0.125, lane_h)
                dos = _stack_heads(do_ref[cur, :], lane_h)
                kc, kp, vc, vp = k_ref[cur, :], k_ref[prev, :], v_ref[cur, :], v_ref[prev, :]
                stage1.append((qs, dos, _dot_nt(qs, kc), _dot_nt(qs, kp), _dot_nt(dos, vc), _dot_nt(dos, vp)))
            stage2 = []
            for i, (cur, prev), (qs, dos, sc, sp, dpc, dpp) in zip(blocks, rows, stage1):
                lse, delta = per_head(st_ref[1, cur, :]), per_head(st_ref[0, cur, :])
                pc = jnp.where(band_c, jnp.exp(sc - lse), 0.0)
                pp = jnp.where(jnp.logical_and(band_p, jnp.logical_not(_dl_first(s_len, n, i))), jnp.exp(sp - lse), 0.0)
                stage2.append((pc.astype(BF16), pp.astype(BF16), (pc * (dpc - delta)).astype(BF16),
                               (pp * (dpp - delta)).astype(BF16)))
            for (cur, prev), (qs, dos, *_), (pc, pp, dsc, dsp) in zip(rows, stage1, stage2):
                dq = _dot(dsc, k_ref[cur, :]) + _dot(dsp, k_ref[prev, :])
                cur_ref[0, cur, :] = jnp.where(lane_h, dq[:BLK], dq[BLK:]) * 0.125
                cur_ref[1, cur, :] = _dot_tn(dsc, qs)
                cur_ref[2, cur, :] = _dot_tn(pc, dos)
                prev_ref[0, cur, :] = _dot_tn(dsp, qs)
                prev_ref[1, cur, :] = _dot_tn(pp, dos)

    return pl.pallas_call(
        body, grid=(3, 4), name=name,
        in_specs=_dl_in_specs(s_len) + [pl.BlockSpec((None, s_len, LANES), lambda n, hp: (n, 0, hp)),
                                        pl.BlockSpec((2, None, s_len, LANES), lambda n, hp: (0, n, 0, hp))],
        out_specs=[pl.BlockSpec((3, None, s_len, LANES), lambda n, hp: (0, n, 0, hp)),
                   pl.BlockSpec((2, None, s_len, LANES), lambda n, hp: (0, n, 0, hp))],
        out_shape=[jax.ShapeDtypeStruct((3, 3, s_len, DL_W), F32), jax.ShapeDtypeStruct((2, 3, s_len, DL_W), F32)],
        compiler_params=_cp(56))(qkv, qkv, qkv, dop, stats)


def _dl_bwd_finish(cur, prev, cos, sin, *, name):
    s_len = cur.shape[2]

    def body(c_ref, p_ref, cos_ref, sin_ref, o_ref, p_scr, u_scr, acc):
        a, n = pl.program_id(0), pl.program_id(2)
        has_prev = jnp.where(a > 0, 1.0, 0.0)
        p_scr[...] = c_ref[...]
        p_scr[0:s_len - BLK, :] += has_prev * p_ref[BLK:, :]
        for k, d in enumerate(DILATIONS):
            @pl.when(n == k)
            def _(k=k, d=d):
                if k == 0:
                    acc[...] = p_scr[...]
                else:
                    _unpermute_rows(u_scr, p_scr, d)
                    acc[...] += u_scr[...]

        @pl.when(n == 2)
        def _():
            dy = acc[...]
            rot = a < 2
            o_ref[...] = (dy * jnp.where(rot, cos_ref[...], 1.0)
                          + _rot_half(dy * jnp.where(rot, sin_ref[...], 0.0))).astype(BF16)

    tab = pl.BlockSpec((s_len, LANES), lambda a, cb, n: (0, 0))
    return pl.pallas_call(
        body, grid=(3, 4, 3), name=name,
        in_specs=[pl.BlockSpec((None, None, s_len, LANES), lambda a, cb, n: (a, n, 0, cb)),
                  pl.BlockSpec((None, None, s_len, LANES), lambda a, cb, n: (jnp.maximum(a - 1, 0), n, 0, cb)),
                  tab, tab],
        out_specs=pl.BlockSpec((s_len, LANES), lambda a, cb, n: (0, 4 * a + cb)),
        out_shape=jax.ShapeDtypeStruct((s_len, 3 * DL_W), BF16),
        scratch_shapes=[pltpu.VMEM((s_len, LANES), F32)] * 3, compiler_params=_cp())(cur, prev, cos, sin)


XA_T = 256


def _xa_probs(q, k):
    s = _dot_nt(q, k) * (X_HD ** -0.5)
    e = jnp.exp(s - jnp.max(s, axis=1, keepdims=True))
    return e / jnp.sum(e, axis=1, keepdims=True)


def _xa_fwd(q, k, v, *, name):
    s_len, d = q.shape
    nm = k.shape[0]

    def body(q_ref, k_ref, v_ref, o_ref):
        for h in range(X_HEADS):
            cs = slice(h * X_HD, (h + 1) * X_HD)
            p = _xa_probs(q_ref[:, cs], k_ref[:, cs])
            o_ref[:, cs] = _dot(p.astype(BF16), v_ref[:, cs]).astype(BF16)

    row = pl.BlockSpec((XA_T, d), lambda i: (i, 0))
    full = pl.BlockSpec((nm, d), lambda i: (0, 0))
    return pl.pallas_call(body, grid=(s_len // XA_T,), name=name, in_specs=[row, full, full], out_specs=row,
                          out_shape=jax.ShapeDtypeStruct((s_len, d), BF16), compiler_params=_cp())(q, k, v)


def _xa_bwd(q, k, v, do, *, name):
    s_len, d = q.shape
    nm = k.shape[0]

    def body(q_ref, k_ref, v_ref, do_ref, dq_ref, dk_ref, dv_ref):
        i = pl.program_id(0)
        for h in range(X_HEADS):
            cs = slice(h * X_HD, (h + 1) * X_HD)
            qh, kh, vh, doh = q_ref[:, cs], k_ref[:, cs], v_ref[:, cs], do_ref[:, cs]
            p = _xa_probs(qh, kh)
            dp = _dot_nt(doh, vh)
            ds = (p * (dp - jnp.sum(dp * p, axis=1, keepdims=True)) * (X_HD ** -0.5)).astype(BF16)
            dq_ref[:, cs] = _dot(ds, kh).astype(BF16)
            dkh, dvh = _dot_tn(ds, qh), _dot_tn(p.astype(BF16), doh)

            @pl.when(i == 0)
            def _(cs=cs, dkh=dkh, dvh=dvh):
                dk_ref[:, cs] = dkh
                dv_ref[:, cs] = dvh

            @pl.when(i > 0)
            def _(cs=cs, dkh=dkh, dvh=dvh):
                dk_ref[:, cs] += dkh
                dv_ref[:, cs] += dvh

    row = pl.BlockSpec((XA_T, d), lambda i: (i, 0))
    full = pl.BlockSpec((nm, d), lambda i: (0, 0))
    return pl.pallas_call(
        body, grid=(s_len // XA_T,), name=name, in_specs=[row, full, full, row], out_specs=[row, full, full],
        out_shape=[jax.ShapeDtypeStruct((s_len, d), BF16), jax.ShapeDtypeStruct((nm, d), F32),
                   jax.ShapeDtypeStruct((nm, d), F32)], compiler_params=_cp())(q, k, v, do)


FF_TM, FF_TN, FF_H = 512, 256, 8
GELU_K, GELU_C = 0.7978845608028654, 0.044715


FF_STRIP = 64


def _ff_conv(e_ref, w_ref, b_ref, rows, r0=0):
    return (w_ref[0:1, :] * e_ref[pl.ds(FF_H - 2 + r0, rows), :] + w_ref[1:2, :] * e_ref[pl.ds(FF_H - 1 + r0, rows), :]
            + w_ref[2:3, :] * e_ref[pl.ds(FF_H + r0, rows), :] + b_ref[...])


def _strips(total, size):
    return [(r0, min(size, total - r0)) for r0 in range(0, total, size)]


def _ff_gate_fwd(up, conv_w, conv_b, *, name):
    s_len = up.shape[0]
    nj = D_FF // FF_TN

    def body(g_ref, v_ref, gp_ref, vp_ref, wg_ref, wv_ref, bg_ref, bv_ref, o_ref, eg, ev):
        i = pl.program_id(0)
        for e, cur, prev in ((eg, g_ref, gp_ref), (ev, v_ref, vp_ref)):
            e[0:FF_H, :] = jnp.where(i > 0, prev[...], 0.0)
            e[FF_H:, :] = cur[...]
        for r0, rows in _strips(FF_TM, FF_STRIP):
            gate = _ff_conv(eg, wg_ref, bg_ref, rows, r0)
            val = _ff_conv(ev, wv_ref, bv_ref, rows, r0)
            t = jnp.tanh(GELU_K * (gate + GELU_C * gate * gate * gate))
            o_ref[r0:r0 + rows, :] = (0.5 * gate * (1.0 + t) * val).astype(BF16)

    cur = lambda c0: pl.BlockSpec((FF_TM, FF_TN), lambda i, j: (i, c0 + j))
    prev = lambda c0: pl.BlockSpec((FF_H, FF_TN), lambda i, j: (jnp.maximum(i * (FF_TM // FF_H) - 1, 0), c0 + j))
    par = lambda r, c0: pl.BlockSpec((r, FF_TN), lambda i, j: (0, c0 + j))
    return pl.pallas_call(
        body, grid=(s_len // FF_TM, nj), name=name,
        in_specs=[cur(0), cur(nj), prev(0), prev(nj), par(3, 0), par(3, nj), par(1, 0), par(1, nj)],
        out_specs=cur(0), out_shape=jax.ShapeDtypeStruct((s_len, D_FF), BF16),
        scratch_shapes=[pltpu.VMEM((FF_TM + FF_H, FF_TN), F32)] * 2, compiler_params=_cp())(
            up, up, up, up, conv_w, conv_w, conv_b, conv_b)


def _ff_gate_bwd(up, dact, conv_w, conv_b, *, name):
    s_len = up.shape[0]
    nj = D_FF // FF_TN
    last = s_len // FF_TM - 1
    ext = FF_TM + FF_H

    def body(g_ref, v_ref, gp_ref, vp_ref, gn_ref, vn_ref, da_ref, dan_ref, wg_ref, wv_ref, bg_ref, bv_ref,
             dg_ref, dv_ref, dw_ref, db_ref, eg, ev, sg, sv):
        i = pl.program_id(1)
        for e, cur, prev, nxt in ((eg, g_ref, gp_ref, gn_ref), (ev, v_ref, vp_ref, vn_ref)):
            e[0:FF_H, :] = jnp.where(i > 0, prev[...], 0.0)
            e[FF_H:FF_H + FF_TM, :] = cur[...]
            e[FF_H + FF_TM:, :] = nxt[...]
        for r0, rows in _strips(ext, FF_STRIP):
            gate = _ff_conv(eg, wg_ref, bg_ref, rows, r0)
            val = _ff_conv(ev, wv_ref, bv_ref, rows, r0)
            dact = da_ref[r0:r0 + rows, :] if r0 < FF_TM else jnp.where(i < last, dan_ref[...], 0.0)
            t = jnp.tanh(GELU_K * (gate + GELU_C * gate * gate * gate))
            half = 0.5 * (1.0 + t)
            dgelu = half + 0.5 * gate * (1.0 - t * t) * GELU_K * (1.0 + 3.0 * GELU_C * gate * gate)
            sg[r0:r0 + rows, :] = dact * val * dgelu
            sv[r0:r0 + rows, :] = dact * (gate * half)
        for part, (s, e, w_ref, out) in enumerate(((sg, eg, wg_ref, dg_ref), (sv, ev, wv_ref, dv_ref))):
            taps, bias = [jnp.zeros((8, FF_TN), F32)] * 3, jnp.zeros((8, FF_TN), F32)
            for r0, rows in _strips(FF_TM, FF_STRIP):
                d0 = s[pl.ds(r0, rows), :]
                out[r0:r0 + rows, :] = (w_ref[2:3, :] * d0 + w_ref[1:2, :] * s[pl.ds(r0 + 1, rows), :]
                                        + w_ref[0:1, :] * s[pl.ds(r0 + 2, rows), :]).astype(BF16)
                taps = [taps[k] + _rowsum8(d0 * e[pl.ds(FF_H - 2 + k + r0, rows), :]) for k in range(3)]
                bias = bias + _rowsum8(d0)
            _acc_out(dw_ref.at[part], i, jnp.concatenate(taps, axis=0))
            _acc_out(db_ref.at[part], i, bias)

    cur = lambda c0: pl.BlockSpec((FF_TM, FF_TN), lambda j, i: (i, c0 + j))
    prev = lambda c0: pl.BlockSpec((FF_H, FF_TN), lambda j, i: (jnp.maximum(i * (FF_TM // FF_H) - 1, 0), c0 + j))
    nxt = lambda c0: pl.BlockSpec(
        (FF_H, FF_TN), lambda j, i: (jnp.minimum((i + 1) * (FF_TM // FF_H), s_len // FF_H - 1), c0 + j))
    par = lambda r, c0: pl.BlockSpec((r, FF_TN), lambda j, i: (0, c0 + j))
    return pl.pallas_call(
        body, grid=(nj, s_len // FF_TM), name=name,
        in_specs=[cur(0), cur(nj), prev(0), prev(nj), nxt(0), nxt(nj), cur(0), nxt(0),
                  par(3, 0), par(3, nj), par(1, 0), par(1, nj)],
        out_specs=[cur(0), cur(0), pl.BlockSpec((2, 24, FF_TN), lambda j, i: (0, 0, j)),
                   pl.BlockSpec((2, 8, FF_TN), lambda j, i: (0, 0, j))],
        out_shape=[jax.ShapeDtypeStruct((s_len, D_FF), BF16), jax.ShapeDtypeStruct((s_len, D_FF), BF16),
                   jax.ShapeDtypeStruct((2, 24, D_FF), F32), jax.ShapeDtypeStruct((2, 8, D_FF), F32)],
        scratch_shapes=[pltpu.VMEM((FF_TM + 2 * FF_H, FF_TN), F32)] * 2 + [pltpu.VMEM((ext, FF_TN), F32)] * 2,
        compiler_params=_cp())(up, up, up, up, up, up, dact, dact, conv_w, conv_w, conv_b, conv_b)


def _place():
    x, y, c = lax.axis_index("x"), lax.axis_index("y"), lax.axis_index("c")
    return x, y, c, [(1 - x, y), (x, 1 - y), (1 - x, 1 - y)]


def _remote(src, dst, send_sem, recv_sem, dev):
    return pltpu.make_async_remote_copy(src_ref=src, dst_ref=dst, send_sem=send_sem, recv_sem=recv_sem,
                                        device_id=dev, device_id_type=MESH)


_ANY = pl.BlockSpec(memory_space=pl.ANY)


def _gather_weights(pack, *, name):
    def body(p_ref, out_ref, send_sems, recv_sems, local_sem):
        x, y, c, chips = _place()
        sibling = (x, y, 1 - c)
        mine = pltpu.make_async_copy(p_ref, out_ref.at[2 * x + y], local_sem)
        mine.start()
        first = [_remote(p_ref.at[c], out_ref.at[2 * x + y, c], send_sems.at[k], recv_sems.at[k], (px, py, c))
                 for k, (px, py) in enumerate(chips)]
        for cp in first:
            cp.start()
        passed = []
        for k, (px, py) in enumerate(chips):
            slab = out_ref.at[2 * px + py, c]
            _remote(slab, slab, send_sems.at[k], recv_sems.at[k], (px, py, c)).wait_recv()
            passed.append(_remote(slab, slab, send_sems.at[3 + k], recv_sems.at[3 + k], sibling))
            passed[-1].start()
        for k, (px, py) in enumerate(chips):
            slab = out_ref.at[2 * px + py, 1 - c]
            _remote(slab, slab, send_sems.at[3 + k], recv_sems.at[3 + k], sibling).wait_recv()
        for cp in first + passed:
            cp.wait_send()
        mine.wait()

    return pl.pallas_call(
        body, name=name, in_specs=[_ANY], out_specs=_ANY,
        out_shape=jax.ShapeDtypeStruct((4,) + pack.shape, pack.dtype),
        scratch_shapes=[pltpu.SemaphoreType.DMA((6,)), pltpu.SemaphoreType.DMA((6,)), pltpu.SemaphoreType.DMA],
        compiler_params=_cp(16))(pack)


def _swap_other_layer(gw, *, name):
    def body(g_ref, out_ref, send_sem, recv_sem):
        x, y, c, _ = _place()
        cp = _remote(g_ref.at[1 - c], out_ref, send_sem, recv_sem, (x, y, 1 - c))
        cp.start()
        cp.wait()

    return pl.pallas_call(
        body, name=name, in_specs=[_ANY], out_specs=_ANY, out_shape=jax.ShapeDtypeStruct(gw.shape[1:], gw.dtype),
        scratch_shapes=[pltpu.SemaphoreType.DMA, pltpu.SemaphoreType.DMA],
        compiler_params=_cp(16))(gw)


def _chip_sum(gw, got, c_arr, *, name):
    _, nchip, rl, d = gw.shape
    tr = 512

    def body(c_ref, a_ref, b_ref, o32_ref, o16_ref):
        s = a_ref[...] + b_ref[...]
        o32_ref[...] = s
        o16_ref[...] = s.astype(BF16)

    blk = pl.BlockSpec((None, tr, d), lambda j, i, c_ref: (j, i, 0))
    return pl.pallas_call(
        body, name=name,
        grid_spec=pltpu.PrefetchScalarGridSpec(
            num_scalar_prefetch=1, grid=(nchip, rl // tr),
            in_specs=[pl.BlockSpec((None, None, tr, d), lambda j, i, c_ref: (c_ref[0], j, i, 0)), blk],
            out_specs=[blk, blk]),
        out_shape=[jax.ShapeDtypeStruct((nchip, rl, d), F32), jax.ShapeDtypeStruct((nchip, rl, d), BF16)],
        compiler_params=_cp())(c_arr, gw, got)


def _scatter_chip_sums(s16, *, name):
    def body(s_ref, out_ref, send_sems, recv_sems):
        x, y, c, chips = _place()
        sends = [_remote(s_ref.at[2 * px + py], out_ref.at[k], send_sems.at[k], recv_sems.at[k], (px, py, c))
                 for k, (px, py) in enumerate(chips)]
        for cp in sends:
            cp.start()
        for cp in sends:
            cp.wait()

    return pl.pallas_call(
        body, name=name, in_specs=[_ANY], out_specs=_ANY,
        out_shape=jax.ShapeDtypeStruct((3,) + s16.shape[1:], s16.dtype),
        scratch_shapes=[pltpu.SemaphoreType.DMA((3,)), pltpu.SemaphoreType.DMA((3,))],
        compiler_params=_cp(16))(s16)


def _mesh_sum(s32, got, j_arr, *, name):
    _, rl, d = s32.shape
    tr = 512

    def body(j_ref, a_ref, b_ref, o_ref):
        o_ref[...] = ((a_ref[...] + b_ref[0].astype(F32)) + b_ref[1].astype(F32)) + b_ref[2].astype(F32)

    return pl.pallas_call(
        body, name=name,
        grid_spec=pltpu.PrefetchScalarGridSpec(
            num_scalar_prefetch=1, grid=(rl // tr,),
            in_specs=[pl.BlockSpec((None, tr, d), lambda i, j_ref: (j_ref[0], i, 0)),
                      pl.BlockSpec((3, tr, d), lambda i, j_ref: (0, i, 0))],
            out_specs=pl.BlockSpec((tr, d), lambda i, j_ref: (i, 0))),
        out_shape=jax.ShapeDtypeStruct((rl, d), F32), compiler_params=_cp())(j_arr, s32, got)


def _share_layers(ghalf, *, name):
    def body(g_ref, out_ref, send_sem, recv_sem, local_sem):
        x, y, c, _ = _place()
        mine = pltpu.make_async_copy(g_ref, out_ref.at[c], local_sem)
        mine.start()
        cp = _remote(g_ref, out_ref.at[c], send_sem, recv_sem, (x, y, 1 - c))
        cp.start()
        _remote(g_ref, out_ref.at[1 - c], send_sem, recv_sem, (x, y, 1 - c)).wait_recv()
        cp.wait_send()
        mine.wait()

    return pl.pallas_call(
        body, name=name, in_specs=[_ANY], out_specs=_ANY,
        out_shape=jax.ShapeDtypeStruct((2,) + ghalf.shape, ghalf.dtype),
        scratch_shapes=[pltpu.SemaphoreType.DMA, pltpu.SemaphoreType.DMA, pltpu.SemaphoreType.DMA],
        compiler_params=_cp(16))(ghalf)


def _all_reduce_small(vec, *, name):
    rows, d = vec.shape

    def body(x_ref, o_ref, gat, send_sems, recv_sems, local_sem):
        x, y, c, chips = _place()
        me, sibling = (x, y, c), (x, y, 1 - c)

        def slot(px, py, pc):
            return gat.at[4 * px + 2 * py + pc]

        def copy(k, block, to, src=None):
            return _remote(slot(*block) if src is None else src, slot(*block), send_sems.at[k], recv_sems.at[k], to)

        mine = pltpu.make_async_copy(x_ref, slot(*me), local_sem)
        mine.start()
        first = [copy(0, me, sibling, src=x_ref)]
        first += [copy(1 + j, me, (*chip, c), src=x_ref) for j, chip in enumerate(chips)]
        for cp in first:
            cp.start()
        passed = [copy(4 + j, (*chip, c), sibling) for j, chip in enumerate(chips)]
        for j, chip in enumerate(chips):
            copy(1 + j, (*chip, c), me).wait_recv()
            passed[j].start()
        copy(0, sibling, me).wait_recv()
        for j, chip in enumerate(chips):
            copy(4 + j, (*chip, 1 - c), me).wait_recv()
        for cp in first + passed:
            cp.wait_send()
        mine.wait()
        acc = gat[0]
        for dev in range(1, 8):
            acc = acc + gat[dev]
        o_ref[...] = acc

    vm = pl.BlockSpec(memory_space=pltpu.VMEM)
    return pl.pallas_call(
        body, name=name, in_specs=[vm], out_specs=vm, out_shape=jax.ShapeDtypeStruct((rows, d), F32),
        scratch_shapes=[pltpu.VMEM((8, rows, d), F32), pltpu.SemaphoreType.DMA((7,)), pltpu.SemaphoreType.DMA((7,)),
                        pltpu.SemaphoreType.DMA],
        compiler_params=_cp(32))(vec)


COL_SHARDED = ("w_in", "ffn_w_up")


def _to_pack_rows(name, shard):
    return shard.reshape(-1, D_MODEL)


def _full_from_blocks(name, blocks):
    rows = blocks.shape[1]
    if name in COL_SHARDED:
        return blocks.reshape(4, D_MODEL, rows).transpose(1, 0, 2).reshape(D_MODEL, 4 * rows)
    return blocks.reshape(4 * rows, D_MODEL)


def _blocks_from_full(name, full):
    if name in COL_SHARDED:
        cols = full.shape[1] // 4
        return full.reshape(D_MODEL, 4, cols).transpose(1, 0, 2).reshape(4, cols, D_MODEL)
    return full.reshape(4, full.shape[0] // 4, D_MODEL)


def _row(v):
    return v.reshape(1, -1)


SMALL = (("mix_norm_pre", (1024,), None), ("cv_w", (31, 256), 1), ("cv_b", (256,), None), ("cv_ln_g", (256,), None),
         ("cv_ln_b", (256,), None), ("cv_pw_w", (256, 256), 0), ("cv_pw_b", (256,), None),
         ("mix_norm_post", (1024,), None), ("x_norm_pre", (1024,), None), ("mem_norm", (1024,), None),
         ("x_norm_post", (1024,), None), ("ffn_norm_pre", (1024,), None), ("ffn_conv_w", (3, 5632), 1),
         ("ffn_conv_b", (5632,), None), ("ffn_norm_post", (1024,), None))
BIG = tuple(n for n, _ in PACK_ROWS)
WEIGHT_ORDER = ("mix_norm_pre", "w_in", "cv_w", "cv_b", "cv_ln_g", "cv_ln_b", "cv_pw_w", "cv_pw_b", "w_out",
                "mix_norm_post", "x_norm_pre", "mem_norm", "x_wq", "x_wk", "x_wv", "x_wo", "x_norm_post",
                "ffn_norm_pre", "ffn_w_up", "ffn_conv_w", "ffn_conv_b", "ffn_w_down", "ffn_norm_post")


def _flat_rows(parts):
    v = jnp.concatenate([p.reshape(-1) for p in parts])
    rows = -(-v.shape[0] // (8 * D_MODEL)) * 8
    return jnp.pad(v, (0, rows * D_MODEL - v.shape[0])).reshape(rows, D_MODEL)


def _layer_fwd(h0, mem, p, cos, sin, tag):
    sv = {"h0": h0}
    n1, u = _rms_mm(h0, _row(p["mix_norm_pre"]), p["w_in"], tm=1024, tn=1408, out_dtype=F32, name=f"mix_in{tag}")
    a_out = _sb_fwd(u, name=f"sb_fwd{tag}")
    b_out, c = _cv_fwd(u, p["cv_w"], _row(p["cv_b"]), _row(p["cv_ln_g"]), _row(p["cv_ln_b"]),
                       p["cv_pw_w"].astype(BF16), _row(p["cv_pw_b"]), name=f"cv_fwd{tag}")
    qkv = _rope_perm(u, cos, sin, name=f"rope_perm{tag}")
    o_p, l_p = _dl_fwd(qkv, name=f"dl_fwd{tag}")
    c_out, o_dl, lse = _dl_mix(o_p, l_p, name=f"dl_mix{tag}")
    cat = jnp.concatenate([a_out, b_out, c_out], axis=1)
    y1, h1 = _mm_post(cat, p["w_out"], h0, _row(p["mix_norm_post"]), tm=256, name=f"mix_out{tag}")
    sv.update(n1=n1, u=u, c=c, qkv=qkv, o_dl=o_dl, lse=lse, cat=cat, y1=y1, h1=h1)

    n2, q = _rms_mm(h1, _row(p["x_norm_pre"]), p["x_wq"], tm=512, tn=1024, out_dtype=BF16, name=f"xa_q{tag}")
    wkv = jnp.concatenate([p["x_wk"], p["x_wv"]], axis=1)
    mem_n, kv = _rms_mm(mem, _row(p["mem_norm"]), wkv, tm=mem.shape[0], tn=1024, out_dtype=BF16, name=f"xa_kv{tag}")
    k, v = kv[:, :D_MODEL], kv[:, D_MODEL:]
    o_x = _xa_fwd(q, k, v, name=f"xa_fwd{tag}")
    y2, h2 = _mm_post(o_x, p["x_wo"], h1, _row(p["x_norm_post"]), tm=256, name=f"xa_out{tag}")
    sv.update(n2=n2, q=q, mem_n=mem_n, k=k, v=v, o_x=o_x, y2=y2, h2=h2, wkv=wkv)

    n3, up = _rms_mm(h2, _row(p["ffn_norm_pre"]), p["ffn_w_up"], tm=1024, tn=1408, out_dtype=F32, name=f"ffn_up{tag}")
    act = _ff_gate_fwd(up, p["ffn_conv_w"], _row(p["ffn_conv_b"]), name=f"ffn_gate{tag}")
    y3, h3 = _mm_post(act, p["ffn_w_down"], h2, _row(p["ffn_norm_post"]), tm=256, name=f"ffn_down{tag}")
    sv.update(n3=n3, up=up, act=act, y3=y3)
    return h3, sv


def _layer_bwd(dh3, mem, p, sv, cos, sin, tag):
    g = {}
    s8 = lambda part: part.sum(axis=0)

    dy3, dgp = _rms_bwd(sv["y3"], _row(p["ffn_norm_post"]), dh3, None, out_dtype=BF16, tm=256, name=f"ffn_post_b{tag}")
    g["ffn_norm_post"] = s8(dgp)
    dact = _mm_nt(dy3, p["ffn_w_down"], tm=512, tn=1408, out_dtype=F32, name=f"ffn_down_bx{tag}")
    g["ffn_w_down"] = _mm_tn(sv["act"], dy3, tk=1408, tn=1024, tm=512, name=f"ffn_down_bw{tag}")
    dgu, dvu, dcw, dcb = _ff_gate_bwd(sv["up"], dact, p["ffn_conv_w"], _row(p["ffn_conv_b"]), name=f"ffn_gate_b{tag}")
    g["ffn_conv_w"] = jnp.concatenate([dcw[0], dcw[1]], axis=1).reshape(3, 8, 2 * D_FF).sum(axis=1)
    g["ffn_conv_b"] = jnp.concatenate([dcb[0], dcb[1]], axis=1).sum(axis=0)
    dup = jnp.concatenate([dgu, dvu], axis=1)
    dn3 = _mm_nt(dup, p["ffn_w_up"], tm=256, tn=512, out_dtype=F32, name=f"ffn_up_bx{tag}")
    g["ffn_w_up"] = _mm_tn(sv["n3"], dup, tk=512, tn=1408, tm=512, name=f"ffn_up_bw{tag}")
    dh2, dgp = _rms_bwd(sv["h2"], _row(p["ffn_norm_pre"]), dn3, dh3, out_dtype=F32, tm=256, name=f"ffn_pre_b{tag}")
    g["ffn_norm_pre"] = s8(dgp)

    dy2, dgp = _rms_bwd(sv["y2"], _row(p["x_norm_post"]), dh2, None, out_dtype=BF16, tm=256, name=f"xa_post_b{tag}")
    g["x_norm_post"] = s8(dgp)
    do_x = _mm_nt(dy2, p["x_wo"], tm=512, tn=1024, out_dtype=BF16, name=f"xa_out_bx{tag}")
    g["x_wo"] = _mm_tn(sv["o_x"], dy2, tk=512, tn=1024, tm=512, name=f"xa_out_bw{tag}")
    dq, dk, dv = _xa_bwd(sv["q"], sv["k"], sv["v"], do_x, name=f"xa_bwd{tag}")
    dn2 = _mm_nt(dq, p["x_wq"], tm=512, tn=1024, out_dtype=F32, name=f"xa_q_bx{tag}")
    g["x_wq"] = _mm_tn(sv["n2"], dq, tk=512, tn=1024, tm=512, name=f"xa_q_bw{tag}")
    dkv = jnp.concatenate([dk, dv], axis=1).astype(BF16)
    nm = mem.shape[0]
    dmem_n = _mm_nt(dkv, sv["wkv"], tm=nm, tn=1024, out_dtype=F32, name=f"xa_kv_bx{tag}")
    dwkv = _mm_tn(sv["mem_n"], dkv, tk=512, tn=2048, tm=nm, name=f"xa_kv_bw{tag}")
    g["x_wk"], g["x_wv"] = dwkv[:, :D_MODEL], dwkv[:, D_MODEL:]
    _, dgp = _rms_bwd(mem, _row(p["mem_norm"]), dmem_n, None, out_dtype=BF16, tm=nm, name=f"xa_mem_b{tag}")
    g["mem_norm"] = s8(dgp)
    dh1, dgp = _rms_bwd(sv["h1"], _row(p["x_norm_pre"]), dn2, dh2, out_dtype=F32, tm=256, name=f"xa_pre_b{tag}")
    g["x_norm_pre"] = s8(dgp)

    dy1, dgp = _rms_bwd(sv["y1"], _row(p["mix_norm_post"]), dh1, None, out_dtype=BF16, tm=256, name=f"mix_post_b{tag}")
    g["mix_norm_post"] = s8(dgp)
    dcat = _mm_nt(dy1, p["w_out"], tm=512, tn=1024, out_dtype=F32, name=f"mix_out_bx{tag}")
    g["w_out"] = _mm_tn(sv["cat"], dy1, tk=512, tn=1024, tm=512, name=f"mix_out_bw{tag}")
    u = sv["u"]
    dq_sb, dk_sb, dv_sb = _sb_bwd(u, dcat, name=f"sb_bwd{tag}")
    pw_b16 = p["cv_pw_w"].astype(BF16)
    dc, dpw, vec = _cv_bwd_local(sv["c"], dcat, _row(p["cv_ln_g"]), _row(p["cv_ln_b"]), pw_b16, name=f"cv_bwd_a{tag}")
    g["cv_pw_w"] = dpw
    vec = vec.reshape(3, 8, CV_W).sum(axis=1)
    g["cv_pw_b"], g["cv_ln_g"], g["cv_ln_b"] = vec[0], vec[1], vec[2]
    du_cv, dcw, dcb = _cv_bwd_conv(u, dc, p["cv_w"], name=f"cv_bwd_b{tag}")
    g["cv_w"] = dcw.reshape(CV_K, 8, CV_W).sum(axis=1)
    g["cv_b"] = dcb.sum(axis=0)
    dop, stats = _dl_bwd_prep(dcat, sv["o_dl"], sv["lse"], name=f"dl_prep_b{tag}")
    cur, prev = _dl_bwd(sv["qkv"], dop, stats, name=f"dl_bwd{tag}")
    du_dl = _dl_bwd_finish(cur, prev, cos, sin, name=f"dl_fin_b{tag}")
    du = jnp.concatenate([dq_sb.astype(BF16), dk_sb.astype(BF16), dv_sb.astype(BF16), du_cv, du_dl], axis=1)
    dn1 = _mm_nt(du, p["w_in"], tm=512, tn=512, out_dtype=F32, name=f"mix_in_bx{tag}")
    g["w_in"] = _mm_tn(sv["n1"], du, tk=512, tn=1408, tm=512, name=f"mix_in_bw{tag}")
    dh0, dgp = _rms_bwd(sv["h0"], _row(p["mix_norm_pre"]), dn1, dh1, out_dtype=F32, tm=256, name=f"mix_pre_b{tag}")
    g["mix_norm_pre"] = s8(dgp)
    return dh0, g


def _step(x, mem, positions, loss_target, w, m, v):
    depth = w["w_in"].shape[0]
    xi, yi, ci = lax.axis_index("x"), lax.axis_index("y"), lax.axis_index("c")
    chip = 2 * xi + yi
    h = x[0]
    mem0 = mem[0]
    s_len = h.shape[0]

    pack = jnp.stack([jnp.concatenate([_to_pack_rows(n, w[n][l]) for n in BIG], axis=0) for l in range(depth)])
    gathered = _gather_weights(pack.astype(BF16), name="gather_weights")
    params = []
    for l in range(depth):
        p, off = {}, 0
        for n, rows in PACK_ROWS:
            p[n] = _full_from_blocks(n, gathered[:, l, off:off + rows, :])
            off += rows
        for n, _, _ in SMALL:
            p[n] = w[n][l]
        params.append(p)
    small_w = []
    for l in range(depth):
        for n, shape, axis in SMALL:
            if axis is not None:
                full = jnp.zeros(shape, F32)
                full = lax.dynamic_update_slice_in_dim(full, w[n][l], chip * w[n][l].shape[axis], axis)
                small_w.append(full * jnp.where(ci == 0, 1.0, 0.0))
    small_w_sum = _all_reduce_small(_flat_rows(small_w), name="gather_small_weights")
    off = 0
    for l in range(depth):
        for n, shape, axis in SMALL:
            if axis is not None:
                size = int(np.prod(shape))
                params[l][n] = small_w_sum.reshape(-1)[off:off + size].reshape(shape)
                off += size

    inv_freq = ROPE_THETA ** (-jnp.arange(HD // 2, dtype=F32) / (HD // 2))
    cos, sin = _rope_tables(positions.reshape(s_len, 1), jnp.tile(inv_freq, 4).reshape(1, LANES), name="rope_tables")

    saved = []
    for l in range(depth):
        h, sv = _layer_fwd(h, mem0, params[l], cos, sin, f"_l{l}")
        saved.append(sv)
    dh, sq = _loss_grad(h, loss_target[0], tm=256, name="loss_grad")
    loss = lax.psum(0.5 * jnp.sum(sq) / D_MODEL, ("x", "y", "c"))
    grads = [None] * depth
    for l in reversed(range(depth)):
        dh, grads[l] = _layer_bwd(dh, mem0, params[l], saved[l], cos, sin, f"_l{l}")
    grad_x = dh[None]

    gw = jnp.stack([jnp.concatenate([_blocks_from_full(n, grads[l][n]) for n in BIG], axis=1) for l in range(depth)])
    c_arr, j_arr = jnp.reshape(ci, (1,)).astype(jnp.int32), jnp.reshape(chip, (1,)).astype(jnp.int32)
    got = _swap_other_layer(gw, name="rs_swap_layers")
    s32, s16 = _chip_sum(gw, got, c_arr, name="rs_chip_sum")
    got16 = _scatter_chip_sums(s16, name="rs_scatter")
    ghalf = _mesh_sum(s32, got16, j_arr, name="rs_mesh_sum")
    gfull = _share_layers(ghalf, name="rs_share_layers")

    out_g, out_d, out_m, out_v = {}, {}, {}, {}
    off = 0
    for n, rows in PACK_ROWS:
        shard_shape = w[n].shape
        g_n = gfull[:, off:off + rows, :].reshape(shard_shape)
        off += rows
        flat = lambda a: a.reshape(-1, shard_shape[-1])
        d_n, m_n, v_n = _adamw(flat(w[n]), flat(g_n), flat(m[n]), flat(v[n]), name=f"adamw_{n}")
        out_g[n], out_d[n], out_m[n], out_v[n] = g_n, d_n.reshape(shard_shape), m_n.reshape(shard_shape), v_n.reshape(shard_shape)

    g_small = _all_reduce_small(_flat_rows([grads[l][n] for l in range(depth) for n, _, _ in SMALL]),
                                name="all_reduce_small_grads").reshape(-1)
    local_g, off = {}, 0
    for l in range(depth):
        for n, shape, axis in SMALL:
            size = int(np.prod(shape))
            full = g_small[off:off + size].reshape(shape)
            off += size
            if axis is not None:
                blk = w[n].shape[1 + axis]
                full = lax.dynamic_slice_in_dim(full, chip * blk, blk, axis)
            local_g.setdefault(n, []).append(full)
    names = [n for n, _, _ in SMALL]
    g_loc = {n: jnp.stack(local_g[n]) for n in names}
    d_s, m_s, v_s = _adamw(_flat_rows([w[n] for n in names]), _flat_rows([g_loc[n] for n in names]),
                           _flat_rows([m[n] for n in names]), _flat_rows([v[n] for n in names]), name="adamw_small")
    off = 0
    for n in names:
        size = int(np.prod(w[n].shape))
        take = lambda a: a.reshape(-1)[off:off + size].reshape(w[n].shape)
        out_g[n], out_d[n], out_m[n], out_v[n] = g_loc[n], take(d_s), take(m_s), take(v_s)
        off += size

    outs = [loss, grad_x]
    for group in (out_g, out_d, out_m, out_v):
        outs += [group[n] for n in WEIGHT_ORDER]
    return tuple(outs)


def kernel(x, mem, positions, mix_norm_pre, w_in, cv_w, cv_b, cv_ln_g, cv_ln_b, cv_pw_w, cv_pw_b, w_out, mix_norm_post, x_norm_pre, mem_norm, x_wq, x_wk, x_wv, x_wo, x_norm_post, ffn_norm_pre, ffn_w_up, ffn_conv_w, ffn_conv_b, ffn_w_down, ffn_norm_post, loss_target, m_mix_norm_pre, m_w_in, m_cv_w, m_cv_b, m_cv_ln_g, m_cv_ln_b, m_cv_pw_w, m_cv_pw_b, m_w_out, m_mix_norm_post, m_x_norm_pre, m_mem_norm, m_x_wq, m_x_wk, m_x_wv, m_x_wo, m_x_norm_post, m_ffn_norm_pre, m_ffn_w_up, m_ffn_conv_w, m_ffn_conv_b, m_ffn_w_down, m_ffn_norm_post, v_mix_norm_pre, v_w_in, v_cv_w, v_cv_b, v_cv_ln_g, v_cv_ln_b, v_cv_pw_w, v_cv_pw_b, v_w_out, v_mix_norm_post, v_x_norm_pre, v_mem_norm, v_x_wq, v_x_wk, v_x_wv, v_x_wo, v_x_norm_post, v_ffn_norm_pre, v_ffn_w_up, v_ffn_conv_w, v_ffn_conv_b, v_ffn_w_down, v_ffn_norm_post):
    w = dict(zip(WEIGHT_ORDER, (mix_norm_pre, w_in, cv_w, cv_b, cv_ln_g, cv_ln_b, cv_pw_w, cv_pw_b, w_out, mix_norm_post, x_norm_pre, mem_norm, x_wq, x_wk, x_wv, x_wo, x_norm_post, ffn_norm_pre, ffn_w_up, ffn_conv_w, ffn_conv_b, ffn_w_down, ffn_norm_post)))
    m = dict(zip(WEIGHT_ORDER, (m_mix_norm_pre, m_w_in, m_cv_w, m_cv_b, m_cv_ln_g, m_cv_ln_b, m_cv_pw_w, m_cv_pw_b, m_w_out, m_mix_norm_post, m_x_norm_pre, m_mem_norm, m_x_wq, m_x_wk, m_x_wv, m_x_wo, m_x_norm_post, m_ffn_norm_pre, m_ffn_w_up, m_ffn_conv_w, m_ffn_conv_b, m_ffn_w_down, m_ffn_norm_post)))
    v = dict(zip(WEIGHT_ORDER, (v_mix_norm_pre, v_w_in, v_cv_w, v_cv_b, v_cv_ln_g, v_cv_ln_b, v_cv_pw_w, v_cv_pw_b, v_w_out, v_mix_norm_post, v_x_norm_pre, v_mem_norm, v_x_wq, v_x_wk, v_x_wv, v_x_wo, v_x_norm_post, v_ffn_norm_pre, v_ffn_w_up, v_ffn_conv_w, v_ffn_conv_b, v_ffn_w_down, v_ffn_norm_post)))
    return _step(x, mem, positions, loss_target, w, m, v)
```

```python
import functools

import jax
import jax.numpy as jnp
import numpy as np
from jax import lax
from jax.experimental import pallas as pl
from jax.experimental.pallas import tpu as pltpu

F32, BF16 = jnp.float32, jnp.bfloat16
MESH = pl.DeviceIdType.MESH
EPS = 1e-6
LANES = 128
BLK = 128
HD = 64
D_MODEL = 1024
D_FF = 2816
SB_W, CV_W, DL_W = 256, 256, 512
CV_K = 31
ROPE_THETA = 10000.0
DILATIONS = (1, 4, 16)
X_HEADS, X_HD = 4, 256
ADAM_LR, ADAM_B1, ADAM_B2, ADAM_EPS, ADAM_WD, ADAM_STEP = 0.001, 0.9, 0.999, 1e-08, 0.01, 10
NEG_INF = float("-inf")
MIB = 1 << 20

PACK_ROWS = (("w_in", 704), ("w_out", 256), ("x_wq", 256), ("x_wk", 256), ("x_wv", 256), ("x_wo", 256),
             ("ffn_w_up", 1408), ("ffn_w_down", 704))
PACK_RL = sum(r for _, r in PACK_ROWS)


def _cp(vmem_mb=48):
    return pltpu.CompilerParams(vmem_limit_bytes=vmem_mb * MIB)


def _dot(a, b):
    return jnp.dot(a, b, preferred_element_type=F32)


def _dot_nt(a, b):
    return lax.dot_general(a, b, (((1,), (1,)), ((), ())), preferred_element_type=F32)


def _dot_tn(a, b):
    return lax.dot_general(a, b, (((0,), (0,)), ((), ())), preferred_element_type=F32)


def _dot_hilo(x, m):
    hi = x.astype(BF16)
    lo = (x - hi.astype(F32)).astype(BF16)
    return _dot(hi, m) + _dot(lo, m)


def _rowsum8(x):
    t, c = x.shape
    return x.reshape(t // 8, 8, c).sum(axis=0)


def _acc_out(ref, i, val):
    @pl.when(i == 0)
    def _():
        ref[...] = val

    @pl.when(i > 0)
    def _():
        ref[...] += val


def _tile(n, cap, mult=8):
    t = min(n, cap)
    while n % t or t % mult:
        t -= 1
    return t


def _rms_mm(x, g, w, *, tm, tn, out_dtype, name):
    m, d = x.shape
    n_out = w.shape[1]

    def body(x_ref, g_ref, w_ref, n_ref, o_ref):
        @pl.when(pl.program_id(1) == 0)
        def _():
            xv = x_ref[...]
            r = lax.rsqrt(jnp.mean(xv * xv, axis=-1, keepdims=True) + EPS)
            n_ref[...] = (xv * r * g_ref[...]).astype(BF16)

        o_ref[...] = _dot(n_ref[...], w_ref[...]).astype(out_dtype)

    return pl.pallas_call(
        body, grid=(m // tm, n_out // tn), name=name,
        in_specs=[pl.BlockSpec((tm, d), lambda i, j: (i, 0)), pl.BlockSpec((1, d), lambda i, j: (0, 0)),
                  pl.BlockSpec((d, tn), lambda i, j: (0, j))],
        out_specs=[pl.BlockSpec((tm, d), lambda i, j: (i, 0)), pl.BlockSpec((tm, tn), lambda i, j: (i, j))],
        out_shape=[jax.ShapeDtypeStruct((m, d), BF16), jax.ShapeDtypeStruct((m, n_out), out_dtype)],
        compiler_params=_cp())(x, g, w)


def _mm_post(a, w, h, g, *, tm, name):
    m, k = a.shape
    d = w.shape[1]

    def body(a_ref, w_ref, h_ref, g_ref, y_ref, ho_ref):
        y = _dot(a_ref[...], w_ref[...])
        y_ref[...] = y
        r = lax.rsqrt(jnp.mean(y * y, axis=-1, keepdims=True) + EPS)
        ho_ref[...] = h_ref[...] + y * r * g_ref[...]

    return pl.pallas_call(
        body, grid=(m // tm,), name=name,
        in_specs=[pl.BlockSpec((tm, k), lambda i: (i, 0)), pl.BlockSpec((k, d), lambda i: (0, 0)),
                  pl.BlockSpec((tm, d), lambda i: (i, 0)), pl.BlockSpec((1, d), lambda i: (0, 0))],
        out_specs=[pl.BlockSpec((tm, d), lambda i: (i, 0)), pl.BlockSpec((tm, d), lambda i: (i, 0))],
        out_shape=[jax.ShapeDtypeStruct((m, d), F32), jax.ShapeDtypeStruct((m, d), F32)],
        compiler_params=_cp())(a, w, h, g)


def _mm_nt(a, w, *, tm, tn, out_dtype, name):
    m, k = a.shape
    n_out = w.shape[0]

    def body(a_ref, w_ref, o_ref):
        o_ref[...] = _dot_nt(a_ref[...], w_ref[...]).astype(out_dtype)

    return pl.pallas_call(
        body, grid=(n_out // tn, m // tm), name=name,
        in_specs=[pl.BlockSpec((tm, k), lambda j, i: (i, 0)), pl.BlockSpec((tn, k), lambda j, i: (j, 0))],
        out_specs=pl.BlockSpec((tm, tn), lambda j, i: (i, j)),
        out_shape=jax.ShapeDtypeStruct((m, n_out), out_dtype),
        compiler_params=_cp())(a, w)


def _mm_tn(x, dy, *, tk, tn, tm, name):
    m, k = x.shape
    n_out = dy.shape[1]

    def body(x_ref, d_ref, o_ref):
        _acc_out(o_ref, pl.program_id(2), _dot_tn(x_ref[...], d_ref[...]))

    return pl.pallas_call(
        body, grid=(k // tk, n_out // tn, m // tm), name=name,
        in_specs=[pl.BlockSpec((tm, tk), lambda a, b, c: (c, a)), pl.BlockSpec((tm, tn), lambda a, b, c: (c, b))],
        out_specs=pl.BlockSpec((tk, tn), lambda a, b, c: (a, b)),
        out_shape=jax.ShapeDtypeStruct((k, n_out), F32),
        compiler_params=_cp())(x, dy)


def _rms_bwd(x, g, dout, res, *, out_dtype, tm, name):
    m, d = x.shape
    has_res = res is not None

    def body(*refs):
        if has_res:
            x_ref, g_ref, d_ref, r_ref, dx_ref, dg_ref = refs
        else:
            x_ref, g_ref, d_ref, dx_ref, dg_ref = refs
        xv = x_ref[...]
        dv = d_ref[...].astype(F32)
        r = lax.rsqrt(jnp.mean(xv * xv, axis=-1, keepdims=True) + EPS)
        xh = xv * r
        dxh = dv * g_ref[...]
        dx = r * (dxh - xh * jnp.mean(dxh * xh, axis=-1, keepdims=True))
        if has_res:
            dx = dx + r_ref[...]
        dx_ref[...] = dx.astype(out_dtype)
        _acc_out(dg_ref, pl.program_id(0), _rowsum8(dv * xh))

    row = pl.BlockSpec((tm, d), lambda i: (i, 0))
    ins = [row, pl.BlockSpec((1, d), lambda i: (0, 0)), row] + ([row] if has_res else [])
    args = (x, g, dout) + ((res,) if has_res else ())
    return pl.pallas_call(
        body, grid=(m // tm,), name=name, in_specs=ins,
        out_specs=[row, pl.BlockSpec((8, d), lambda i: (0, 0))],
        out_shape=[jax.ShapeDtypeStruct((m, d), out_dtype), jax.ShapeDtypeStruct((8, d), F32)],
        compiler_params=_cp())(*args)


def _loss_grad(h, tgt, *, tm, name):
    m, d = h.shape

    def body(h_ref, t_ref, dh_ref, p_ref):
        e = h_ref[...] - t_ref[...]
        dh_ref[...] = e / d
        _acc_out(p_ref, pl.program_id(0), _rowsum8(e * e))

    row = pl.BlockSpec((tm, d), lambda i: (i, 0))
    return pl.pallas_call(
        body, grid=(m // tm,), name=name, in_specs=[row, row],
        out_specs=[row, pl.BlockSpec((8, d), lambda i: (0, 0))],
        out_shape=[jax.ShapeDtypeStruct((m, d), F32), jax.ShapeDtypeStruct((8, d), F32)],
        compiler_params=_cp())(h, tgt)


def _adamw(w, g, m, v, *, name):
    r, c = w.shape
    tr = _tile(r, 256)

    def body(w_ref, g_ref, m_ref, v_ref, d_ref, mo_ref, vo_ref):
        gv = g_ref[...]
        m2 = ADAM_B1 * m_ref[...] + (1.0 - ADAM_B1) * gv
        v2 = ADAM_B2 * v_ref[...] + (1.0 - ADAM_B2) * jnp.square(gv)
        m_hat = m2 / (1.0 - ADAM_B1 ** ADAM_STEP)
        v_hat = v2 / (1.0 - ADAM_B2 ** ADAM_STEP)
        d_ref[...] = -ADAM_LR * (m_hat / (jnp.sqrt(v_hat) + ADAM_EPS) + ADAM_WD * w_ref[...])
        mo_ref[...] = m2
        vo_ref[...] = v2

    blk = pl.BlockSpec((tr, c), lambda i: (i, 0))
    return pl.pallas_call(
        body, grid=(r // tr,), name=name, in_specs=[blk] * 4, out_specs=[blk] * 3,
        out_shape=[jax.ShapeDtypeStruct((r, c), F32)] * 3, compiler_params=_cp())(w, g, m, v)


def _head_masks():
    lane = lax.broadcasted_iota(jnp.int32, (BLK, LANES), 1)
    row = lax.broadcasted_iota(jnp.int32, (BLK, LANES), 0)
    return lane, row, lane < HD


def _sb_scores(q_a, k, before):
    z = _dot_nt(q_a, k)
    sp = jnp.log1p(jnp.exp(-jnp.abs(z)))
    ls_pos = jnp.minimum(z, 0.0) - sp
    lkeep = jnp.where(before, ls_pos - z, 0.0)
    return ls_pos, lkeep


SB_DEAD = -104.0


def _sb_alive(jj, i, carry):
    return jnp.logical_and(jj <= i, jnp.max(carry) > SB_DEAD)


SB_QB = 2
SB_ROWS = SB_QB * 2 * BLK


def _sb_before(jj):
    lane = lax.broadcasted_iota(jnp.int32, (SB_ROWS, LANES), 1)
    row = lax.broadcasted_iota(jnp.int32, (SB_ROWS, LANES), 0)
    below_diag = jj - (SB_QB - 1) + row // (2 * BLK)
    return jnp.logical_or(below_diag > 0, jnp.logical_and(below_diag == 0, lane < row % BLK))


def _sb_stack(x, lane_h):
    return jnp.concatenate([_stack_heads(x[b * BLK:(b + 1) * BLK], lane_h) for b in range(SB_QB)], axis=0)


def _sb_unstack(x, lane_h):
    return jnp.concatenate([jnp.where(lane_h, x[2 * b * BLK:(2 * b + 1) * BLK], x[(2 * b + 1) * BLK:(2 * b + 2) * BLK])
                            for b in range(SB_QB)], axis=0)


def _sb_fwd(u, *, name, carry=None):
    s_len = u.shape[0]
    qrows = SB_QB * BLK

    def body(q_ref, k_ref, v_ref, o_ref):
        top = pl.program_id(1) * SB_QB + SB_QB - 1
        lane, row, lane_h = _head_masks()
        suffix = (row > lane).astype(BF16)
        qs = _sb_stack(q_ref[...] * 0.125, lane_h)

        def step(state):
            jj, cc, acc = state
            off = pl.multiple_of((top - jj) * BLK, BLK)
            k = k_ref[pl.ds(off, BLK), :].astype(BF16)
            v = v_ref[pl.ds(off, BLK), :].astype(BF16)
            before = _sb_before(jj)
            ls_pos, lkeep = _sb_scores(qs, k, before)
            between = _dot_hilo(lkeep, suffix) + cc
            att = jnp.where(before, jnp.exp(ls_pos + between), 0.0)
            return jj + 1, cc + jnp.sum(lkeep, axis=1, keepdims=True), acc + _dot(att.astype(BF16), v)

        init = (jnp.int32(0), jnp.zeros((SB_ROWS, 1), F32), jnp.zeros((SB_ROWS, LANES), F32))
        acc = lax.while_loop(lambda st: _sb_alive(st[0], top, st[1]), step, init)[2]
        o_ref[...] = _sb_unstack(acc, lane_h).astype(BF16)

    return _call(
        body, grid=(2, s_len // qrows), name=name, carry=carry,
        in_specs=[pl.BlockSpec((qrows, LANES), lambda hp, i: (i, hp)),
                  pl.BlockSpec((s_len, LANES), lambda hp, i: (0, 2 + hp)),
                  pl.BlockSpec((s_len, LANES), lambda hp, i: (0, 4 + hp))],
        out_specs=[pl.BlockSpec((qrows, LANES), lambda hp, i: (i, hp))],
        out_shape=[jax.ShapeDtypeStruct((s_len, SB_W), BF16)], args=(u, u, u))


def _sb_bwd(u, dcat, *, name, carry=None):
    s_len = u.shape[0]
    nq = s_len // BLK
    qrows = SB_QB * BLK

    def body(q_ref, k_ref, v_ref, do_ref, dq_ref, dk_ref, dv_ref, g_scr, b_scr):
        step = pl.program_id(1)
        top = step * SB_QB + SB_QB - 1
        lane, row, lane_h = _head_masks()
        suffix = (row > lane).astype(BF16)
        prefix = (row < lane).astype(BF16)
        qf = q_ref[...]
        qs = _sb_stack(qf * 0.125, lane_h)
        qu = _sb_stack(qf, lane_h)
        dos = _sb_stack(do_ref[...], lane_h)

        @pl.when(step == 0)
        def _():
            dk_ref[...] = jnp.zeros_like(dk_ref)
            dv_ref[...] = jnp.zeros_like(dv_ref)

        def down(state):
            jj, cc = state
            j = top - jj
            off = pl.multiple_of(j * BLK, BLK)
            k = k_ref[pl.ds(off, BLK), :].astype(BF16)
            v = v_ref[pl.ds(off, BLK), :].astype(BF16)
            before = _sb_before(jj)
            ls_pos, lkeep = _sb_scores(qs, k, before)
            between = _dot_hilo(lkeep, suffix) + cc
            att = jnp.where(before, jnp.exp(ls_pos + between), 0.0)
            g_scr[j] = att * _dot_nt(dos, v)
            b_scr[j] = jnp.exp(ls_pos)
            dv_ref[pl.ds(off, BLK), :] += _dot_tn(att.astype(BF16), dos)
            return jj + 1, cc + jnp.sum(lkeep, axis=1, keepdims=True)

        zc = jnp.zeros((SB_ROWS, 1), F32)
        visited = lax.while_loop(lambda st: _sb_alive(st[0], top, st[1]), down, (jnp.int32(0), zc))[0]

        def up(j, carry):
            pc, dq = carry
            off = pl.multiple_of(j * BLK, BLK)
            k = k_ref[pl.ds(off, BLK), :].astype(BF16)
            g, beta = g_scr[j], b_scr[j]
            below = _dot_hilo(g, prefix) + pc
            dz = (jnp.where(_sb_before(top - j), g * (1.0 - beta) - beta * below, 0.0) * 0.125).astype(BF16)
            dk_ref[pl.ds(off, BLK), :] += _dot_tn(dz, qu)
            return pc + jnp.sum(g, axis=1, keepdims=True), dq + _dot(dz, k)

        dq = lax.fori_loop(top + 1 - visited, top + 1, up, (zc, jnp.zeros((SB_ROWS, LANES), F32)))[1]
        dq_ref[...] = _sb_unstack(dq, lane_h)

    col = lambda c0: pl.BlockSpec((s_len, LANES), lambda hp, i: (0, c0 + hp))
    blk = pl.BlockSpec((qrows, LANES), lambda hp, i: (i, hp))
    acc = pl.BlockSpec((s_len, LANES), lambda hp, i: (0, hp))
    return _call(
        body, grid=(2, s_len // qrows), name=name, carry=carry, in_specs=[blk, col(2), col(4), blk],
        out_specs=[blk, acc, acc], out_shape=[jax.ShapeDtypeStruct((s_len, SB_W), F32)] * 3,
        scratch_shapes=[pltpu.VMEM((nq, SB_ROWS, LANES), F32), pltpu.VMEM((nq, SB_ROWS, LANES), F32)],
        vmem_mb=56, args=(u, u, u, dcat))


CV_T = 512
CV_H = 32


def _cv_specs(s_len):
    cur = lambda c: pl.BlockSpec((CV_T, CV_W), lambda i: (i, c))
    prev = lambda c: pl.BlockSpec((CV_H, CV_W), lambda i: (jnp.maximum(i * (CV_T // CV_H) - 1, 0), c))
    nxt = lambda c: pl.BlockSpec((CV_H, CV_W),
                                 lambda i: (jnp.minimum((i + 1) * (CV_T // CV_H), s_len // CV_H - 1), c))
    full = lambda r: pl.BlockSpec((r, CV_W), lambda i: (0, 0))
    return cur, prev, nxt, full


def _glu_into(gp_ref, val_ref, gate_ref, valp_ref, gatep_ref, i):
    gp_ref[0:CV_H, :] = jnp.where(i > 0, valp_ref[...] * jax.nn.sigmoid(gatep_ref[...]), 0.0)
    gp_ref[CV_H:, :] = val_ref[...] * jax.nn.sigmoid(gate_ref[...])


def _cv_fwd(u, cv_w, cv_b, ln_g, ln_b, pw_w, pw_b, *, name):
    s_len = u.shape[0]
    cur, prev, _, full = _cv_specs(s_len)

    def body(val_ref, gate_ref, valp_ref, gatep_ref, w_ref, b_ref, g_ref, be_ref, pw_ref, pb_ref,
             o_ref, c_ref, gp_ref):
        _glu_into(gp_ref, val_ref, gate_ref, valp_ref, gatep_ref, pl.program_id(0))
        acc = jnp.zeros((CV_T, CV_W), F32) + b_ref[...]
        for k in range(CV_K):
            acc = acc + w_ref[k:k + 1, :] * gp_ref[pl.ds(CV_H - CV_K + 1 + k, CV_T), :]
        c_ref[...] = acc
        mu = jnp.mean(acc, axis=-1, keepdims=True)
        xc = acc - mu
        xh = xc * lax.rsqrt(jnp.mean(xc * xc, axis=-1, keepdims=True) + EPS)
        a = xh * g_ref[...] + be_ref[...]
        s = a * jax.nn.sigmoid(a)
        o_ref[...] = (_dot(s.astype(BF16), pw_ref[...]) + pb_ref[...]).astype(BF16)

    return pl.pallas_call(
        body, grid=(s_len // CV_T,), name=name,
        in_specs=[cur(3), cur(4), prev(3), prev(4), full(CV_K), full(1), full(1), full(1), full(CV_W), full(1)],
        out_specs=[cur(0), cur(0)],
        out_shape=[jax.ShapeDtypeStruct((s_len, CV_W), BF16), jax.ShapeDtypeStruct((s_len, CV_W), F32)],
        scratch_shapes=[pltpu.VMEM((CV_T + CV_H, CV_W), F32)], compiler_params=_cp())(
            u, u, u, u, cv_w, cv_b, ln_g, ln_b, pw_w, pw_b)


def _cv_bwd_local(c, dcat, ln_g, ln_b, pw_w, *, name):
    s_len = c.shape[0]
    cur, _, _, full = _cv_specs(s_len)

    def body(c_ref, db_ref, g_ref, be_ref, pw_ref, dc_ref, dpw_ref, vec_ref):
        i = pl.program_id(0)
        cv = c_ref[...]
        db = db_ref[...]
        mu = jnp.mean(cv, axis=-1, keepdims=True)
        xc = cv - mu
        rstd = lax.rsqrt(jnp.mean(xc * xc, axis=-1, keepdims=True) + EPS)
        xh = xc * rstd
        a = xh * g_ref[...] + be_ref[...]
        sg = jax.nn.sigmoid(a)
        s = a * sg
        dbb = db.astype(BF16)
        ds = _dot_nt(dbb, pw_ref[...])
        da = ds * (sg * (1.0 + a * (1.0 - sg)))
        dxh = da * g_ref[...]
        dc_ref[...] = rstd * (dxh - jnp.mean(dxh, axis=-1, keepdims=True)
                              - xh * jnp.mean(dxh * xh, axis=-1, keepdims=True))
        _acc_out(dpw_ref, i, _dot_tn(s.astype(BF16), dbb))
        _acc_out(vec_ref, i, jnp.concatenate([_rowsum8(db), _rowsum8(da * xh), _rowsum8(da)], axis=0))

    return pl.pallas_call(
        body, grid=(s_len // CV_T,), name=name,
        in_specs=[cur(0), cur(1), full(1), full(1), full(CV_W)],
        out_specs=[cur(0), full(CV_W), full(24)],
        out_shape=[jax.ShapeDtypeStruct((s_len, CV_W), F32), jax.ShapeDtypeStruct((CV_W, CV_W), F32),
                   jax.ShapeDtypeStruct((24, CV_W), F32)], compiler_params=_cp())(c, dcat, ln_g, ln_b, pw_w)


def _cv_bwd_conv(u, dc, cv_w, *, name):
    s_len = u.shape[0]
    cur, prev, nxt, full = _cv_specs(s_len)
    last = s_len // CV_T - 1

    def body(val_ref, gate_ref, valp_ref, gatep_ref, dc_ref, dcn_ref, w_ref, du_ref, dw_ref, dbias_ref,
             gp_ref, dcp_ref):
        i = pl.program_id(0)
        _glu_into(gp_ref, val_ref, gate_ref, valp_ref, gatep_ref, i)
        dcv = dc_ref[...]
        dcp_ref[0:CV_T, :] = dcv
        dcp_ref[CV_T:, :] = jnp.where(i < last, dcn_ref[...], 0.0)
        dg = jnp.zeros((CV_T, CV_W), F32)
        parts = []
        for k in range(CV_K):
            dg = dg + w_ref[k:k + 1, :] * dcp_ref[pl.ds(CV_K - 1 - k, CV_T), :]
            parts.append(_rowsum8(dcv * gp_ref[pl.ds(CV_H - CV_K + 1 + k, CV_T), :]))
        _acc_out(dw_ref, i, jnp.concatenate(parts, axis=0))
        _acc_out(dbias_ref, i, _rowsum8(dcv))
        val = val_ref[...]
        sg = jax.nn.sigmoid(gate_ref[...])
        du_ref[:, 0:CV_W] = (dg * sg).astype(BF16)
        du_ref[:, CV_W:] = (dg * val * sg * (1.0 - sg)).astype(BF16)

    return pl.pallas_call(
        body, grid=(s_len // CV_T,), name=name,
        in_specs=[cur(3), cur(4), prev(3), prev(4), cur(0), nxt(0), full(CV_K)],
        out_specs=[pl.BlockSpec((CV_T, 2 * CV_W), lambda i: (i, 0)), full(CV_K * 8), full(8)],
        out_shape=[jax.ShapeDtypeStruct((s_len, 2 * CV_W), BF16), jax.ShapeDtypeStruct((CV_K * 8, CV_W), F32),
                   jax.ShapeDtypeStruct((8, CV_W), F32)],
        scratch_shapes=[pltpu.VMEM((CV_T + CV_H, CV_W), F32), pltpu.VMEM((CV_T + CV_H, CV_W), F32)],
        compiler_params=_cp())(u, u, u, u, dc, dc, cv_w)


def _rope_tables(pos_col, inv_freq_row, *, name):
    s_len = pos_col.shape[0]

    def body(p_ref, f_ref, cos_ref, sin_ref):
        ang = p_ref[...].astype(F32) * f_ref[...]
        lane = lax.broadcasted_iota(jnp.int32, (s_len, LANES), 1)
        sn = jnp.sin(ang)
        cos_ref[...] = jnp.cos(ang)
        sin_ref[...] = jnp.where(lane % HD < HD // 2, -sn, sn)

    return pl.pallas_call(body, name=name, out_shape=[jax.ShapeDtypeStruct((s_len, LANES), F32)] * 2,
                          compiler_params=_cp())(pos_col, inv_freq_row)


def _rot_half(x):
    lane = lax.broadcasted_iota(jnp.int32, x.shape, 1)
    return jnp.where(lane % HD < HD // 2, pltpu.roll(x, LANES - HD // 2, 1), pltpu.roll(x, HD // 2, 1))


def _permute_rows(dst_ref, src_ref, d, dtype):
    s_len = src_ref.shape[0]
    seg = s_len // d
    if d == 1:
        dst_ref[...] = src_ref[...].astype(dtype)
        return
    for r in range(d):
        dst_ref[r * seg:(r + 1) * seg, :] = src_ref[pl.ds(r, seg, stride=d), :].astype(dtype)


def _unpermute_rows(dst_ref, src_ref, d):
    s_len = src_ref.shape[0]
    seg = s_len // d
    if d == 1:
        dst_ref[...] = src_ref[...]
        return
    for r in range(d):
        dst_ref[pl.ds(r, seg, stride=d), :] = src_ref[r * seg:(r + 1) * seg, :]


def _rope_perm(u, cos, sin, *, name):
    s_len = u.shape[0]

    def body(x_ref, cos_ref, sin_ref, o_ref, scr):
        a = pl.program_id(0)
        x = x_ref[...]
        rot = a < 2
        scr[...] = x * jnp.where(rot, cos_ref[...], 1.0) + _rot_half(x) * jnp.where(rot, sin_ref[...], 0.0)
        for n, d in enumerate(DILATIONS):
            _permute_rows(o_ref.at[n], scr, d, BF16)

    tab = pl.BlockSpec((s_len, LANES), lambda a, cb: (0, 0))
    return pl.pallas_call(
        body, grid=(3, 4), name=name,
        in_specs=[pl.BlockSpec((s_len, LANES), lambda a, cb: (0, 10 + 4 * a + cb)), tab, tab],
        out_specs=pl.BlockSpec((None, 3, s_len, LANES), lambda a, cb: (a, 0, 0, cb)),
        out_shape=jax.ShapeDtypeStruct((3, 3, s_len, DL_W), BF16),
        scratch_shapes=[pltpu.VMEM((s_len, LANES), F32)], compiler_params=_cp())(u, cos, sin)


DL_UNROLL = 4


def _dl_band(rows):
    lane = lax.broadcasted_iota(jnp.int32, (rows, LANES), 1)
    row = lax.broadcasted_iota(jnp.int32, (rows, LANES), 0) % BLK
    return lane <= row, lane >= row


def _dl_first(s_len, n, i):
    nb = jnp.where(n == 0, s_len // BLK, jnp.where(n == 1, s_len // (BLK * DILATIONS[1]),
                                                   s_len // (BLK * DILATIONS[2])))
    return lax.rem(i, nb) == 0


def _stack_heads(x, lane_h):
    return jnp.concatenate([jnp.where(lane_h, x, 0.0), jnp.where(lane_h, 0.0, x)], axis=0).astype(BF16)


def _dl_rows(i):
    cur = pl.ds(pl.multiple_of(i * BLK, BLK), BLK)
    prev = pl.ds(pl.multiple_of(jnp.maximum(i - 1, 0) * BLK, BLK), BLK)
    return cur, prev


def _dl_in_specs(s_len):
    return [pl.BlockSpec((None, None, s_len, LANES), functools.partial(lambda a, n, hp: (a, n, 0, hp), a))
            for a in range(3)]


def _dl_fwd(qkv, *, name):
    s_len = qkv.shape[2]

    def body(q_ref, k_ref, v_ref, o_ref, l_ref):
        n = pl.program_id(0)
        lane_h = _head_masks()[2]
        band_c, band_p = _dl_band(2 * BLK)
        ones = jnp.ones((BLK, LANES), BF16)

        @pl.loop(0, s_len // BLK, step=DL_UNROLL)
        def _(i0):
            blocks = [i0 + t for t in range(DL_UNROLL)]
            rows = [_dl_rows(i) for i in blocks]
            scores = []
            for cur, prev in rows:
                qs = _stack_heads(q_ref[cur, :] * 0.125, lane_h)
                scores.append((_dot_nt(qs, k_ref[cur, :]), _dot_nt(qs, k_ref[prev, :])))
            probs = []
            for i, (sc, sp) in zip(blocks, scores):
                sc = jnp.where(band_c, sc, NEG_INF)
                sp = jnp.where(jnp.logical_and(band_p, jnp.logical_not(_dl_first(s_len, n, i))), sp, NEG_INF)
                m = jnp.max(jnp.maximum(sc, sp), axis=1, keepdims=True)
                probs.append((jnp.exp(sc - m).astype(BF16), jnp.exp(sp - m).astype(BF16), m))
            for (cur, prev), (pc, pp, m) in zip(rows, probs):
                r = (_dot(pc, jnp.concatenate([v_ref[cur, :], ones], axis=1))
                     + _dot(pp, jnp.concatenate([v_ref[prev, :], ones], axis=1)))
                den = jnp.where(lane_h, r[:BLK, LANES:], r[BLK:, LANES:])
                o_ref[cur, :] = jnp.where(lane_h, r[:BLK, :LANES], r[BLK:, :LANES]) / den
                l_ref[cur, :] = jnp.where(lane_h, m[:BLK], m[BLK:]) + jnp.log(den)

    out = pl.BlockSpec((None, s_len, LANES), lambda n, hp: (n, 0, hp))
    return pl.pallas_call(
        body, grid=(3, 4), name=name, in_specs=_dl_in_specs(s_len), out_specs=[out, out],
        out_shape=[jax.ShapeDtypeStruct((3, s_len, DL_W), F32)] * 2, compiler_params=_cp())(qkv, qkv, qkv)


def _dl_mix(o_p, l_p, *, name):
    s_len = o_p.shape[1]

    def body(o_ref, l_ref, ob_ref, of_ref, lt_ref, o_scr, l_scr):
        n = pl.program_id(1)
        for k, d in enumerate(DILATIONS):
            @pl.when(n == k)
            def _(k=k, d=d):
                _unpermute_rows(o_scr.at[k], o_ref, d)
                _unpermute_rows(l_scr.at[k], l_ref, d)

        @pl.when(n == 2)
        def _():
            l0, l1, l2 = l_scr[0], l_scr[1], l_scr[2]
            m = jnp.maximum(jnp.maximum(l0, l1), l2)
            e0, e1, e2 = jnp.exp(l0 - m), jnp.exp(l1 - m), jnp.exp(l2 - m)
            den = e0 + e1 + e2
            o = (e0 / den) * o_scr[0] + (e1 / den) * o_scr[1] + (e2 / den) * o_scr[2]
            of_ref[...] = o
            ob_ref[...] = o.astype(BF16)
            lt_ref[...] = m + jnp.log(den)

    inb = pl.BlockSpec((None, s_len, LANES), lambda cb, n: (n, 0, cb))
    outb = pl.BlockSpec((s_len, LANES), lambda cb, n: (0, cb))
    return pl.pallas_call(
        body, grid=(4, 3), name=name, in_specs=[inb, inb], out_specs=[outb, outb, outb],
        out_shape=[jax.ShapeDtypeStruct((s_len, DL_W), BF16), jax.ShapeDtypeStruct((s_len, DL_W), F32),
                   jax.ShapeDtypeStruct((s_len, DL_W), F32)],
        scratch_shapes=[pltpu.VMEM((3, s_len, LANES), F32), pltpu.VMEM((3, s_len, LANES), F32)],
        compiler_params=_cp())(o_p, l_p)


def _dl_bwd_prep(dcat, o, lse, *, name):
    s_len = o.shape[0]

    def body(do_ref, o_ref, l_ref, dop_ref, st_ref, d_scr):
        n = pl.program_id(1)

        @pl.when(n == 0)
        def _():
            r0 = lax.broadcasted_iota(jnp.int32, (LANES, LANES), 0) // HD
            r1 = lax.broadcasted_iota(jnp.int32, (LANES, LANES), 1) // HD
            d_scr[...] = _dot_hilo(do_ref[...] * o_ref[...], (r0 == r1).astype(BF16))

        for k, d in enumerate(DILATIONS):
            @pl.when(n == k)
            def _(d=d):
                _permute_rows(dop_ref, do_ref, d, BF16)
                _permute_rows(st_ref.at[0], d_scr, d, F32)
                _permute_rows(st_ref.at[1], l_ref, d, F32)

    nat = lambda c0: pl.BlockSpec((s_len, LANES), lambda cb, n: (0, c0 + cb))
    return pl.pallas_call(
        body, grid=(4, 3), name=name, in_specs=[nat(4), nat(0), nat(0)],
        out_specs=[pl.BlockSpec((None, s_len, LANES), lambda cb, n: (n, 0, cb)),
                   pl.BlockSpec((2, None, s_len, LANES), lambda cb, n: (0, n, 0, cb))],
        out_shape=[jax.ShapeDtypeStruct((3, s_len, DL_W), BF16), jax.ShapeDtypeStruct((2, 3, s_len, DL_W), F32)],
        scratch_shapes=[pltpu.VMEM((s_len, LANES), F32)], compiler_params=_cp())(dcat, o, lse)


def _dl_bwd(qkv, dop, stats, *, name):
    s_len = qkv.shape[2]

    def body(q_ref, k_ref, v_ref, do_ref, st_ref, cur_ref, prev_ref):
        n = pl.program_id(0)
        lane_h = _head_masks()[2]
        band_c, band_p = _dl_band(2 * BLK)

        def per_head(x):
            xr = pltpu.roll(x, HD, 1)
            return jnp.concatenate([jnp.where(lane_h, x, xr), jnp.where(lane_h, xr, x)], axis=0)

        @pl.loop(0, s_len // BLK, step=DL_UNROLL)
        def _(i0):
            blocks = [i0 + t for t in range(DL_UNROLL)]
            rows = [_dl_rows(i) for i in blocks]
            stage1 = []
            for cur, prev in rows:
                qs = _stack_heads(q_ref[cur, :] * 0.125, lane_h)
                dos = _stack_heads(do_ref[cur, :], lane_h)
                kc, kp, vc, vp = k_ref[cur, :], k_ref[prev, :], v_ref[cur, :], v_ref[prev, :]
                stage1.append((qs, dos, _dot_nt(qs, kc), _dot_nt(qs, kp), _dot_nt(dos, vc), _dot_nt(dos, vp)))
            stage2 = []
            for i, (cur, prev), (qs, dos, sc, sp, dpc, dpp) in zip(blocks, rows, stage1):
                lse, delta = per_head(st_ref[1, cur, :]), per_head(st_ref[0, cur, :])
                pc = jnp.where(band_c, jnp.exp(sc - lse), 0.0)
                pp = jnp.where(jnp.logical_and(band_p, jnp.logical_not(_dl_first(s_len, n, i))), jnp.exp(sp - lse), 0.0)
                stage2.append((pc.astype(BF16), pp.astype(BF16), (pc * (dpc - delta)).astype(BF16),
                               (pp * (dpp - delta)).astype(BF16)))
            for (cur, prev), (qs, dos, *_), (pc, pp, dsc, dsp) in zip(rows, stage1, stage2):
                dq = _dot(dsc, k_ref[cur, :]) + _dot(dsp, k_ref[prev, :])
                cur_ref[0, cur, :] = jnp.where(lane_h, dq[:BLK], dq[BLK:]) * 0.125
                cur_ref[1, cur, :] = _dot_tn(dsc, qs)
                cur_ref[2, cur, :] = _dot_tn(pc, dos)
                prev_ref[0, cur, :] = _dot_tn(dsp, qs)
                prev_ref[1, cur, :] = _dot_tn(pp, dos)

    return pl.pallas_call(
        body, grid=(3, 4), name=name,
        in_specs=_dl_in_specs(s_len) + [pl.BlockSpec((None, s_len, LANES), lambda n, hp: (n, 0, hp)),
                                        pl.BlockSpec((2, None, s_len, LANES), lambda n, hp: (0, n, 0, hp))],
        out_specs=[pl.BlockSpec((3, None, s_len, LANES), lambda n, hp: (0, n, 0, hp)),
                   pl.BlockSpec((2, None, s_len, LANES), lambda n, hp: (0, n, 0, hp))],
        out_shape=[jax.ShapeDtypeStruct((3, 3, s_len, DL_W), F32), jax.ShapeDtypeStruct((2, 3, s_len, DL_W), F32)],
        compiler_params=_cp(56))(qkv, qkv, qkv, dop, stats)


def _dl_bwd_finish(cur, prev, cos, sin, *, name):
    s_len = cur.shape[2]

    def body(c_ref, p_ref, cos_ref, sin_ref, o_ref, p_scr, u_scr, acc):
        a, n = pl.program_id(0), pl.program_id(2)
        has_prev = jnp.where(a > 0, 1.0, 0.0)
        p_scr[...] = c_ref[...]
        p_scr[0:s_len - BLK, :] += has_prev * p_ref[BLK:, :]
        for k, d in enumerate(DILATIONS):
            @pl.when(n == k)
            def _(k=k, d=d):
                if k == 0:
                    acc[...] = p_scr[...]
                else:
                    _unpermute_rows(u_scr, p_scr, d)
                    acc[...] += u_scr[...]

        @pl.when(n == 2)
        def _():
            dy = acc[...]
            rot = a < 2
            o_ref[...] = (dy * jnp.where(rot, cos_ref[...], 1.0)
                          + _rot_half(dy * jnp.where(rot, sin_ref[...], 0.0))).astype(BF16)

    tab = pl.BlockSpec((s_len, LANES), lambda a, cb, n: (0, 0))
    return pl.pallas_call(
        body, grid=(3, 4, 3), name=name,
        in_specs=[pl.BlockSpec((None, None, s_len, LANES), lambda a, cb, n: (a, n, 0, cb)),
                  pl.BlockSpec((None, None, s_len, LANES), lambda a, cb, n: (jnp.maximum(a - 1, 0), n, 0, cb)),
                  tab, tab],
        out_specs=pl.BlockSpec((s_len, LANES), lambda a, cb, n: (0, 4 * a + cb)),
        out_shape=jax.ShapeDtypeStruct((s_len, 3 * DL_W), BF16),
        scratch_shapes=[pltpu.VMEM((s_len, LANES), F32)] * 3, compiler_params=_cp())(cur, prev, cos, sin)


XA_T = 256


def _xa_probs(q, k):
    s = _dot_nt(q, k) * (X_HD ** -0.5)
    e = jnp.exp(s - jnp.max(s, axis=1, keepdims=True))
    return e / jnp.sum(e, axis=1, keepdims=True)


def _xa_fwd(q, k, v, *, name):
    s_len, d = q.shape
    nm = k.shape[0]

    def body(q_ref, k_ref, v_ref, o_ref):
        for h in range(X_HEADS):
            cs = slice(h * X_HD, (h + 1) * X_HD)
            p = _xa_probs(q_ref[:, cs], k_ref[:, cs])
            o_ref[:, cs] = _dot(p.astype(BF16), v_ref[:, cs]).astype(BF16)

    row = pl.BlockSpec((XA_T, d), lambda i: (i, 0))
    full = pl.BlockSpec((nm, d), lambda i: (0, 0))
    return pl.pallas_call(body, grid=(s_len // XA_T,), name=name, in_specs=[row, full, full], out_specs=row,
                          out_shape=jax.ShapeDtypeStruct((s_len, d), BF16), compiler_params=_cp())(q, k, v)


def _xa_bwd(q, k, v, do, *, name):
    s_len, d = q.shape
    nm = k.shape[0]

    def body(q_ref, k_ref, v_ref, do_ref, dq_ref, dk_ref, dv_ref):
        i = pl.program_id(0)
        for h in range(X_HEADS):
            cs = slice(h * X_HD, (h + 1) * X_HD)
            qh, kh, vh, doh = q_ref[:, cs], k_ref[:, cs], v_ref[:, cs], do_ref[:, cs]
            p = _xa_probs(qh, kh)
            dp = _dot_nt(doh, vh)
            ds = (p * (dp - jnp.sum(dp * p, axis=1, keepdims=True)) * (X_HD ** -0.5)).astype(BF16)
            dq_ref[:, cs] = _dot(ds, kh).astype(BF16)
            dkh, dvh = _dot_tn(ds, qh), _dot_tn(p.astype(BF16), doh)

            @pl.when(i == 0)
            def _(cs=cs, dkh=dkh, dvh=dvh):
                dk_ref[:, cs] = dkh
                dv_ref[:, cs] = dvh

            @pl.when(i > 0)
            def _(cs=cs, dkh=dkh, dvh=dvh):
                dk_ref[:, cs] += dkh
                dv_ref[:, cs] += dvh

    row = pl.BlockSpec((XA_T, d), lambda i: (i, 0))
    full = pl.BlockSpec((nm, d), lambda i: (0, 0))
    return pl.pallas_call(
        body, grid=(s_len // XA_T,), name=name, in_specs=[row, full, full, row], out_specs=[row, full, full],
        out_shape=[jax.ShapeDtypeStruct((s_len, d), BF16), jax.ShapeDtypeStruct((nm, d), F32),
                   jax.ShapeDtypeStruct((nm, d), F32)], compiler_params=_cp())(q, k, v, do)


FF_TM, FF_TN, FF_H = 512, 256, 8
GELU_K, GELU_C = 0.7978845608028654, 0.044715


FF_STRIP = 64


def _ff_conv(e_ref, w_ref, b_ref, rows, r0=0):
    return (w_ref[0:1, :] * e_ref[pl.ds(FF_H - 2 + r0, rows), :] + w_ref[1:2, :] * e_ref[pl.ds(FF_H - 1 + r0, rows), :]
            + w_ref[2:3, :] * e_ref[pl.ds(FF_H + r0, rows), :] + b_ref[...])


def _strips(total, size):
    return [(r0, min(size, total - r0)) for r0 in range(0, total, size)]


def _ff_gate_fwd(up, conv_w, conv_b, *, name):
    s_len = up.shape[0]
    nj = D_FF // FF_TN

    def body(g_ref, v_ref, gp_ref, vp_ref, wg_ref, wv_ref, bg_ref, bv_ref, o_ref, eg, ev):
        i = pl.program_id(0)
        for e, cur, prev in ((eg, g_ref, gp_ref), (ev, v_ref, vp_ref)):
            e[0:FF_H, :] = jnp.where(i > 0, prev[...], 0.0)
            e[FF_H:, :] = cur[...]
        for r0, rows in _strips(FF_TM, FF_STRIP):
            gate = _ff_conv(eg, wg_ref, bg_ref, rows, r0)
            val = _ff_conv(ev, wv_ref, bv_ref, rows, r0)
            t = jnp.tanh(GELU_K * (gate + GELU_C * gate * gate * gate))
            o_ref[r0:r0 + rows, :] = (0.5 * gate * (1.0 + t) * val).astype(BF16)

    cur = lambda c0: pl.BlockSpec((FF_TM, FF_TN), lambda i, j: (i, c0 + j))
    prev = lambda c0: pl.BlockSpec((FF_H, FF_TN), lambda i, j: (jnp.maximum(i * (FF_TM // FF_H) - 1, 0), c0 + j))
    par = lambda r, c0: pl.BlockSpec((r, FF_TN), lambda i, j: (0, c0 + j))
    return pl.pallas_call(
        body, grid=(s_len // FF_TM, nj), name=name,
        in_specs=[cur(0), cur(nj), prev(0), prev(nj), par(3, 0), par(3, nj), par(1, 0), par(1, nj)],
        out_specs=cur(0), out_shape=jax.ShapeDtypeStruct((s_len, D_FF), BF16),
        scratch_shapes=[pltpu.VMEM((FF_TM + FF_H, FF_TN), F32)] * 2, compiler_params=_cp())(
            up, up, up, up, conv_w, conv_w, conv_b, conv_b)


def _ff_gate_bwd(up, dact, conv_w, conv_b, *, name, carry=None):
    s_len = up.shape[0]
    nj = D_FF // FF_TN
    last = s_len // FF_TM - 1
    ext = FF_TM + FF_H

    def body(g_ref, v_ref, gp_ref, vp_ref, gn_ref, vn_ref, da_ref, dan_ref, wg_ref, wv_ref, bg_ref, bv_ref,
             dg_ref, dv_ref, dw_ref, db_ref, eg, ev, sg, sv):
        i = pl.program_id(1)
        for e, cur, prev, nxt in ((eg, g_ref, gp_ref, gn_ref), (ev, v_ref, vp_ref, vn_ref)):
            e[0:FF_H, :] = jnp.where(i > 0, prev[...], 0.0)
            e[FF_H:FF_H + FF_TM, :] = cur[...]
            e[FF_H + FF_TM:, :] = nxt[...]
        for r0, rows in _strips(ext, FF_STRIP):
            gate = _ff_conv(eg, wg_ref, bg_ref, rows, r0)
            val = _ff_conv(ev, wv_ref, bv_ref, rows, r0)
            dact = da_ref[r0:r0 + rows, :] if r0 < FF_TM else jnp.where(i < last, dan_ref[...], 0.0)
            t = jnp.tanh(GELU_K * (gate + GELU_C * gate * gate * gate))
            half = 0.5 * (1.0 + t)
            dgelu = half + 0.5 * gate * (1.0 - t * t) * GELU_K * (1.0 + 3.0 * GELU_C * gate * gate)
            sg[r0:r0 + rows, :] = dact * val * dgelu
            sv[r0:r0 + rows, :] = dact * (gate * half)
        for part, (s, e, w_ref, out) in enumerate(((sg, eg, wg_ref, dg_ref), (sv, ev, wv_ref, dv_ref))):
            taps, bias = [jnp.zeros((8, FF_TN), F32)] * 3, jnp.zeros((8, FF_TN), F32)
            for r0, rows in _strips(FF_TM, FF_STRIP):
                d0 = s[pl.ds(r0, rows), :]
                out[r0:r0 + rows, :] = (w_ref[2:3, :] * d0 + w_ref[1:2, :] * s[pl.ds(r0 + 1, rows), :]
                                        + w_ref[0:1, :] * s[pl.ds(r0 + 2, rows), :]).astype(BF16)
                taps = [taps[k] + _rowsum8(d0 * e[pl.ds(FF_H - 2 + k + r0, rows), :]) for k in range(3)]
                bias = bias + _rowsum8(d0)
            _acc_out(dw_ref.at[part], i, jnp.concatenate(taps, axis=0))
            _acc_out(db_ref.at[part], i, bias)

    cur = lambda c0: pl.BlockSpec((FF_TM, FF_TN), lambda j, i: (i, c0 + j))
    prev = lambda c0: pl.BlockSpec((FF_H, FF_TN), lambda j, i: (jnp.maximum(i * (FF_TM // FF_H) - 1, 0), c0 + j))
    nxt = lambda c0: pl.BlockSpec(
        (FF_H, FF_TN), lambda j, i: (jnp.minimum((i + 1) * (FF_TM // FF_H), s_len // FF_H - 1), c0 + j))
    par = lambda r, c0: pl.BlockSpec((r, FF_TN), lambda j, i: (0, c0 + j))
    return _call(
        body, grid=(nj, s_len // FF_TM), name=name, carry=carry,
        in_specs=[cur(0), cur(nj), prev(0), prev(nj), nxt(0), nxt(nj), cur(0), nxt(0),
                  par(3, 0), par(3, nj), par(1, 0), par(1, nj)],
        out_specs=[cur(0), cur(0), pl.BlockSpec((2, 24, FF_TN), lambda j, i: (0, 0, j)),
                   pl.BlockSpec((2, 8, FF_TN), lambda j, i: (0, 0, j))],
        out_shape=[jax.ShapeDtypeStruct((s_len, D_FF), BF16), jax.ShapeDtypeStruct((s_len, D_FF), BF16),
                   jax.ShapeDtypeStruct((2, 24, D_FF), F32), jax.ShapeDtypeStruct((2, 8, D_FF), F32)],
        scratch_shapes=[pltpu.VMEM((FF_TM + 2 * FF_H, FF_TN), F32)] * 2 + [pltpu.VMEM((ext, FF_TN), F32)] * 2,
        args=(up, up, up, up, up, up, dact, dact, conv_w, conv_w, conv_b, conv_b))


def _place():
    x, y, c = lax.axis_index("x"), lax.axis_index("y"), lax.axis_index("c")
    return x, y, c, [(1 - x, y), (x, 1 - y), (1 - x, 1 - y)]


def _remote(src, dst, send_sem, recv_sem, dev):
    return pltpu.make_async_remote_copy(src_ref=src, dst_ref=dst, send_sem=send_sem, recv_sem=recv_sem,
                                        device_id=dev, device_id_type=MESH)


_ANY = pl.BlockSpec(memory_space=pl.ANY)


N_SEMS = 4


class _Exchange:
    def __init__(self, operands, out_shapes, start, wait, aliases=None):
        self.operands, self.out_shapes, self.start, self.wait = list(operands), list(out_shapes), start, wait
        self.aliases = aliases or {}


def _sem_scratch():
    return [pltpu.SemaphoreType.DMA((N_SEMS,)), pltpu.SemaphoreType.DMA((N_SEMS,)), pltpu.SemaphoreType.DMA]


def _run_exchange(ex, *, name):
    k, n = len(ex.operands), len(ex.out_shapes)

    def body(*refs):
        ins, outs, sems = refs[:k], refs[k:k + n], refs[k + n:]
        ex.start(ins, outs, *sems)
        ex.wait(ins, outs, *sems)

    return pl.pallas_call(body, name=name, in_specs=[_ANY] * k, out_specs=[_ANY] * n, out_shape=ex.out_shapes,
                          scratch_shapes=_sem_scratch(), input_output_aliases=ex.aliases,
                          compiler_params=_cp(16))(*ex.operands)


def _call(body, *, grid, in_specs, out_specs, out_shape, args, name, scratch_shapes=(), vmem_mb=48, carry=None):
    scratch_shapes = list(scratch_shapes)
    if carry is None:
        return pl.pallas_call(body, grid=grid, name=name, in_specs=in_specs, out_specs=out_specs, out_shape=out_shape,
                              scratch_shapes=scratch_shapes, compiler_params=_cp(vmem_mb))(*args)
    n_in, n_out, n_scr = len(in_specs), len(out_shape), len(scratch_shapes)
    k_in, k_out = len(carry.operands), len(carry.out_shapes)

    def wrapped(*refs):
        ins, refs = refs[:n_in], refs[n_in:]
        cin, refs = refs[:k_in], refs[k_in:]
        outs, refs = refs[:n_out], refs[n_out:]
        cout, refs = refs[:k_out], refs[k_out:]
        scratch, sems = refs[:n_scr], refs[n_scr:]
        ids = [pl.program_id(a) for a in range(len(grid))]
        first = functools.reduce(jnp.logical_and, [i == 0 for i in ids])
        last = functools.reduce(jnp.logical_and, [i == g - 1 for i, g in zip(ids, grid)])

        @pl.when(first)
        def _():
            carry.start(cin, cout, *sems)

        body(*ins, *outs, *scratch)

        @pl.when(last)
        def _():
            carry.wait(cin, cout, *sems)

    aliases = {n_in + i: n_out + o for i, o in carry.aliases.items()}
    return pl.pallas_call(
        wrapped, grid=grid, name=name, in_specs=list(in_specs) + [_ANY] * k_in,
        out_specs=list(out_specs) + [_ANY] * k_out, out_shape=list(out_shape) + carry.out_shapes,
        scratch_shapes=scratch_shapes + _sem_scratch(), input_output_aliases=aliases,
        compiler_params=_cp(vmem_mb))(*args, *carry.operands)


def _half_rows(ref_rows, c):
    half = ref_rows // 2
    return pl.ds(c * half, half)


def _ex_gather(pack):
    rl = pack.shape[0]

    def copies(ins, outs, send, recv):
        x, y, c, chips = _place()
        rows = _half_rows(rl, c)
        sends = [_remote(ins[0].at[rows], outs[0].at[2 * x + y, rows], send.at[k], recv.at[k], (px, py, c))
                 for k, (px, py) in enumerate(chips)]
        lands = [_remote(ins[0].at[rows], outs[0].at[2 * px + py, rows], send.at[k], recv.at[k], (px, py, c))
                 for k, (px, py) in enumerate(chips)]
        return sends, lands

    def mine(ins, outs, local):
        x, y, _, _ = _place()
        return pltpu.make_async_copy(ins[0], outs[0].at[2 * x + y], local)

    def start(ins, outs, send, recv, local):
        mine(ins, outs, local).start()
        for cp in copies(ins, outs, send, recv)[0]:
            cp.start()

    def wait(ins, outs, send, recv, local):
        sends, lands = copies(ins, outs, send, recv)
        for cp in lands:
            cp.wait_recv()
        for cp in sends:
            cp.wait_send()
        mine(ins, outs, local).wait()

    return _Exchange([pack], [jax.ShapeDtypeStruct((4,) + pack.shape, pack.dtype)], start, wait)


def _ex_gather_forward(g):
    rl = g.shape[1]

    def copies(outs, send, recv):
        x, y, c, chips = _place()
        slabs = [(outs[0].at[2 * px + py, _half_rows(rl, c)], outs[0].at[2 * px + py, _half_rows(rl, 1 - c)])
                 for px, py in chips]
        sends = [_remote(a, a, send.at[k], recv.at[k], (x, y, 1 - c)) for k, (a, _) in enumerate(slabs)]
        lands = [_remote(b, b, send.at[k], recv.at[k], (x, y, 1 - c)) for k, (_, b) in enumerate(slabs)]
        return sends, lands

    def start(ins, outs, send, recv, local):
        for cp in copies(outs, send, recv)[0]:
            cp.start()

    def wait(ins, outs, send, recv, local):
        sends, lands = copies(outs, send, recv)
        for cp in lands:
            cp.wait_recv()
        for cp in sends:
            cp.wait_send()

    return _Exchange([g], [jax.ShapeDtypeStruct(g.shape, g.dtype)], start, wait, aliases={0: 0})


def _ex_swap_halves(gw):
    nb, rl, d = gw.shape

    def copies(ins, outs, send, recv):
        x, y, c, _ = _place()
        return [_remote(ins[0].at[j, _half_rows(rl, 1 - c)], outs[0].at[j], send.at[j], recv.at[j], (x, y, 1 - c))
                for j in range(nb)]

    def start(ins, outs, send, recv, local):
        for cp in copies(ins, outs, send, recv):
            cp.start()

    def wait(ins, outs, send, recv, local):
        for cp in copies(ins, outs, send, recv):
            cp.wait()

    return _Exchange([gw], [jax.ShapeDtypeStruct((nb, rl // 2, d), gw.dtype)], start, wait)


def _chip_sum(gw, got, c_arr, *, name):
    nchip, half, d = got.shape
    tr = 512

    def body(c_ref, a_ref, b_ref, o32_ref, o16_ref):
        s = a_ref[...] + b_ref[...]
        o32_ref[...] = s
        o16_ref[...] = s.astype(BF16)

    blk = pl.BlockSpec((None, tr, d), lambda j, i, c_ref: (j, i, 0))
    return pl.pallas_call(
        body, name=name,
        grid_spec=pltpu.PrefetchScalarGridSpec(
            num_scalar_prefetch=1, grid=(nchip, half // tr),
            in_specs=[pl.BlockSpec((None, tr, d), lambda j, i, c_ref: (j, c_ref[0] * (half // tr) + i, 0)), blk],
            out_specs=[blk, blk]),
        out_shape=[jax.ShapeDtypeStruct((nchip, half, d), F32), jax.ShapeDtypeStruct((nchip, half, d), BF16)],
        compiler_params=_cp())(c_arr, gw, got)


def _ex_scatter(s16):
    def copies(ins, outs, send, recv):
        x, y, c, chips = _place()
        return [_remote(ins[0].at[2 * px + py], outs[0].at[k], send.at[k], recv.at[k], (px, py, c))
                for k, (px, py) in enumerate(chips)]

    def start(ins, outs, send, recv, local):
        for cp in copies(ins, outs, send, recv):
            cp.start()

    def wait(ins, outs, send, recv, local):
        for cp in copies(ins, outs, send, recv):
            cp.wait()

    return _Exchange([s16], [jax.ShapeDtypeStruct((3,) + s16.shape[1:], s16.dtype)], start, wait)


def _mesh_sum(s32, got, j_arr, *, name):
    _, rl, d = s32.shape
    tr = 512

    def body(j_ref, a_ref, b_ref, o_ref):
        o_ref[...] = ((a_ref[...] + b_ref[0].astype(F32)) + b_ref[1].astype(F32)) + b_ref[2].astype(F32)

    return pl.pallas_call(
        body, name=name,
        grid_spec=pltpu.PrefetchScalarGridSpec(
            num_scalar_prefetch=1, grid=(rl // tr,),
            in_specs=[pl.BlockSpec((None, tr, d), lambda i, j_ref: (j_ref[0], i, 0)),
                      pl.BlockSpec((3, tr, d), lambda i, j_ref: (0, i, 0))],
            out_specs=pl.BlockSpec((tr, d), lambda i, j_ref: (i, 0))),
        out_shape=jax.ShapeDtypeStruct((rl, d), F32), compiler_params=_cp())(j_arr, s32, got)


def _ex_share_halves(ghalf):
    half, d = ghalf.shape

    def copies(ins, outs, send, recv, local):
        x, y, c, _ = _place()
        there = outs[0].at[_half_rows(2 * half, c)]
        back = outs[0].at[_half_rows(2 * half, 1 - c)]
        return (_remote(ins[0], there, send.at[0], recv.at[0], (x, y, 1 - c)),
                _remote(ins[0], back, send.at[0], recv.at[0], (x, y, 1 - c)), pltpu.make_async_copy(ins[0], there, local))

    def start(ins, outs, send, recv, local):
        out, _, mine = copies(ins, outs, send, recv, local)
        mine.start()
        out.start()

    def wait(ins, outs, send, recv, local):
        out, back, mine = copies(ins, outs, send, recv, local)
        back.wait_recv()
        out.wait_send()
        mine.wait()

    return _Exchange([ghalf], [jax.ShapeDtypeStruct((2 * half, d), ghalf.dtype)], start, wait)


class _ReduceScatter:
    def __init__(self, gw, c_arr, j_arr, tag):
        self.gw, self.c_arr, self.j_arr, self.tag = gw, c_arr, j_arr, tag

    def swap(self):
        return _ex_swap_halves(self.gw)

    def after_swap(self, got):
        self.s32, s16 = _chip_sum(self.gw, got, self.c_arr, name=f"rs_chip_sum{self.tag}")
        return _ex_scatter(s16)

    def after_scatter(self, got16):
        ghalf = _mesh_sum(self.s32, got16, self.j_arr, name=f"rs_mesh_sum{self.tag}")
        return _run_exchange(_ex_share_halves(ghalf), name=f"rs_share{self.tag}")[0]

    def run(self):
        got, = _run_exchange(self.swap(), name=f"rs_swap{self.tag}")
        got16, = _run_exchange(self.after_swap(got), name=f"rs_scatter{self.tag}")
        return self.after_scatter(got16)


def _all_reduce_small(vec, *, name):
    rows, d = vec.shape

    def body(x_ref, o_ref, gat, send_sems, recv_sems, local_sem):
        x, y, c, chips = _place()
        me, sibling = (x, y, c), (x, y, 1 - c)

        def slot(px, py, pc):
            return gat.at[4 * px + 2 * py + pc]

        def copy(k, block, to, src=None):
            return _remote(slot(*block) if src is None else src, slot(*block), send_sems.at[k], recv_sems.at[k], to)

        mine = pltpu.make_async_copy(x_ref, slot(*me), local_sem)
        mine.start()
        first = [copy(0, me, sibling, src=x_ref)]
        first += [copy(1 + j, me, (*chip, c), src=x_ref) for j, chip in enumerate(chips)]
        for cp in first:
            cp.start()
        passed = [copy(4 + j, (*chip, c), sibling) for j, chip in enumerate(chips)]
        for j, chip in enumerate(chips):
            copy(1 + j, (*chip, c), me).wait_recv()
            passed[j].start()
        copy(0, sibling, me).wait_recv()
        for j, chip in enumerate(chips):
            copy(4 + j, (*chip, 1 - c), me).wait_recv()
        for cp in first + passed:
            cp.wait_send()
        mine.wait()
        acc = gat[0]
        for dev in range(1, 8):
            acc = acc + gat[dev]
        o_ref[...] = acc

    vm = pl.BlockSpec(memory_space=pltpu.VMEM)
    return pl.pallas_call(
        body, name=name, in_specs=[vm], out_specs=vm, out_shape=jax.ShapeDtypeStruct((rows, d), F32),
        scratch_shapes=[pltpu.VMEM((8, rows, d), F32), pltpu.SemaphoreType.DMA((7,)), pltpu.SemaphoreType.DMA((7,)),
                        pltpu.SemaphoreType.DMA],
        compiler_params=_cp(32))(vec)


COL_SHARDED = ("w_in", "ffn_w_up")


def _to_pack_rows(name, shard):
    return shard.reshape(-1, D_MODEL)


def _full_from_blocks(name, blocks):
    rows = blocks.shape[1]
    if name in COL_SHARDED:
        return blocks.reshape(4, D_MODEL, rows).transpose(1, 0, 2).reshape(D_MODEL, 4 * rows)
    return blocks.reshape(4 * rows, D_MODEL)


def _blocks_from_full(name, full):
    if name in COL_SHARDED:
        cols = full.shape[1] // 4
        return full.reshape(D_MODEL, 4, cols).transpose(1, 0, 2).reshape(4, cols, D_MODEL)
    return full.reshape(4, full.shape[0] // 4, D_MODEL)


def _row(v):
    return v.reshape(1, -1)


SMALL = (("mix_norm_pre", (1024,), None), ("cv_w", (31, 256), 1), ("cv_b", (256,), None), ("cv_ln_g", (256,), None),
         ("cv_ln_b", (256,), None), ("cv_pw_w", (256, 256), 0), ("cv_pw_b", (256,), None),
         ("mix_norm_post", (1024,), None), ("x_norm_pre", (1024,), None), ("mem_norm", (1024,), None),
         ("x_norm_post", (1024,), None), ("ffn_norm_pre", (1024,), None), ("ffn_conv_w", (3, 5632), 1),
         ("ffn_conv_b", (5632,), None), ("ffn_norm_post", (1024,), None))
BIG = tuple(n for n, _ in PACK_ROWS)
WEIGHT_ORDER = ("mix_norm_pre", "w_in", "cv_w", "cv_b", "cv_ln_g", "cv_ln_b", "cv_pw_w", "cv_pw_b", "w_out",
                "mix_norm_post", "x_norm_pre", "mem_norm", "x_wq", "x_wk", "x_wv", "x_wo", "x_norm_post",
                "ffn_norm_pre", "ffn_w_up", "ffn_conv_w", "ffn_conv_b", "ffn_w_down", "ffn_norm_post")


def _flat_rows(parts):
    v = jnp.concatenate([p.reshape(-1) for p in parts])
    rows = -(-v.shape[0] // (8 * D_MODEL)) * 8
    return jnp.pad(v, (0, rows * D_MODEL - v.shape[0])).reshape(rows, D_MODEL)


def _layer_fwd(h0, mem, p, cos, sin, tag, riding=None):
    sv = {"h0": h0}
    n1, u = _rms_mm(h0, _row(p["mix_norm_pre"]), p["w_in"], tm=1024, tn=1408, out_dtype=F32, name=f"mix_in{tag}")
    a_out, *sv["rode"] = _sb_fwd(u, name=f"sb_fwd{tag}", carry=riding)
    b_out, c = _cv_fwd(u, p["cv_w"], _row(p["cv_b"]), _row(p["cv_ln_g"]), _row(p["cv_ln_b"]),
                       p["cv_pw_w"].astype(BF16), _row(p["cv_pw_b"]), name=f"cv_fwd{tag}")
    qkv = _rope_perm(u, cos, sin, name=f"rope_perm{tag}")
    o_p, l_p = _dl_fwd(qkv, name=f"dl_fwd{tag}")
    c_out, o_dl, lse = _dl_mix(o_p, l_p, name=f"dl_mix{tag}")
    cat = jnp.concatenate([a_out, b_out, c_out], axis=1)
    y1, h1 = _mm_post(cat, p["w_out"], h0, _row(p["mix_norm_post"]), tm=256, name=f"mix_out{tag}")
    sv.update(n1=n1, u=u, c=c, qkv=qkv, o_dl=o_dl, lse=lse, cat=cat, y1=y1, h1=h1)

    n2, q = _rms_mm(h1, _row(p["x_norm_pre"]), p["x_wq"], tm=512, tn=1024, out_dtype=BF16, name=f"xa_q{tag}")
    wkv = jnp.concatenate([p["x_wk"], p["x_wv"]], axis=1)
    mem_n, kv = _rms_mm(mem, _row(p["mem_norm"]), wkv, tm=mem.shape[0], tn=1024, out_dtype=BF16, name=f"xa_kv{tag}")
    k, v = kv[:, :D_MODEL], kv[:, D_MODEL:]
    o_x = _xa_fwd(q, k, v, name=f"xa_fwd{tag}")
    y2, h2 = _mm_post(o_x, p["x_wo"], h1, _row(p["x_norm_post"]), tm=256, name=f"xa_out{tag}")
    sv.update(n2=n2, q=q, mem_n=mem_n, k=k, v=v, o_x=o_x, y2=y2, h2=h2, wkv=wkv)

    n3, up = _rms_mm(h2, _row(p["ffn_norm_pre"]), p["ffn_w_up"], tm=1024, tn=1408, out_dtype=F32, name=f"ffn_up{tag}")
    act = _ff_gate_fwd(up, p["ffn_conv_w"], _row(p["ffn_conv_b"]), name=f"ffn_gate{tag}")
    y3, h3 = _mm_post(act, p["ffn_w_down"], h2, _row(p["ffn_norm_post"]), tm=256, name=f"ffn_down{tag}")
    sv.update(n3=n3, up=up, act=act, y3=y3)
    return h3, sv


def _layer_bwd(dh3, mem, p, sv, cos, sin, tag, riding=None):
    g = {}
    s8 = lambda part: part.sum(axis=0)
    rode = None

    dy3, dgp = _rms_bwd(sv["y3"], _row(p["ffn_norm_post"]), dh3, None, out_dtype=BF16, tm=256, name=f"ffn_post_b{tag}")
    g["ffn_norm_post"] = s8(dgp)
    dact = _mm_nt(dy3, p["ffn_w_down"], tm=512, tn=1408, out_dtype=F32, name=f"ffn_down_bx{tag}")
    g["ffn_w_down"] = _mm_tn(sv["act"], dy3, tk=1408, tn=1024, tm=512, name=f"ffn_down_bw{tag}")
    dgu, dvu, dcw, dcb, *got = _ff_gate_bwd(sv["up"], dact, p["ffn_conv_w"], _row(p["ffn_conv_b"]),
                                            name=f"ffn_gate_b{tag}", carry=riding.swap() if riding else None)
    scatter = riding.after_swap(got[0]) if riding else None
    g["ffn_conv_w"] = jnp.concatenate([dcw[0], dcw[1]], axis=1).reshape(3, 8, 2 * D_FF).sum(axis=1)
    g["ffn_conv_b"] = jnp.concatenate([dcb[0], dcb[1]], axis=1).sum(axis=0)
    dup = jnp.concatenate([dgu, dvu], axis=1)
    dn3 = _mm_nt(dup, p["ffn_w_up"], tm=256, tn=512, out_dtype=F32, name=f"ffn_up_bx{tag}")
    g["ffn_w_up"] = _mm_tn(sv["n3"], dup, tk=512, tn=1408, tm=512, name=f"ffn_up_bw{tag}")
    dh2, dgp = _rms_bwd(sv["h2"], _row(p["ffn_norm_pre"]), dn3, dh3, out_dtype=F32, tm=256, name=f"ffn_pre_b{tag}")
    g["ffn_norm_pre"] = s8(dgp)

    dy2, dgp = _rms_bwd(sv["y2"], _row(p["x_norm_post"]), dh2, None, out_dtype=BF16, tm=256, name=f"xa_post_b{tag}")
    g["x_norm_post"] = s8(dgp)
    do_x = _mm_nt(dy2, p["x_wo"], tm=512, tn=1024, out_dtype=BF16, name=f"xa_out_bx{tag}")
    g["x_wo"] = _mm_tn(sv["o_x"], dy2, tk=512, tn=1024, tm=512, name=f"xa_out_bw{tag}")
    dq, dk, dv = _xa_bwd(sv["q"], sv["k"], sv["v"], do_x, name=f"xa_bwd{tag}")
    dn2 = _mm_nt(dq, p["x_wq"], tm=512, tn=1024, out_dtype=F32, name=f"xa_q_bx{tag}")
    g["x_wq"] = _mm_tn(sv["n2"], dq, tk=512, tn=1024, tm=512, name=f"xa_q_bw{tag}")
    dkv = jnp.concatenate([dk, dv], axis=1).astype(BF16)
    nm = mem.shape[0]
    dmem_n = _mm_nt(dkv, sv["wkv"], tm=nm, tn=1024, out_dtype=F32, name=f"xa_kv_bx{tag}")
    dwkv = _mm_tn(sv["mem_n"], dkv, tk=512, tn=2048, tm=nm, name=f"xa_kv_bw{tag}")
    g["x_wk"], g["x_wv"] = dwkv[:, :D_MODEL], dwkv[:, D_MODEL:]
    _, dgp = _rms_bwd(mem, _row(p["mem_norm"]), dmem_n, None, out_dtype=BF16, tm=nm, name=f"xa_mem_b{tag}")
    g["mem_norm"] = s8(dgp)
    dh1, dgp = _rms_bwd(sv["h1"], _row(p["x_norm_pre"]), dn2, dh2, out_dtype=F32, tm=256, name=f"xa_pre_b{tag}")
    g["x_norm_pre"] = s8(dgp)

    dy1, dgp = _rms_bwd(sv["y1"], _row(p["mix_norm_post"]), dh1, None, out_dtype=BF16, tm=256, name=f"mix_post_b{tag}")
    g["mix_norm_post"] = s8(dgp)
    dcat = _mm_nt(dy1, p["w_out"], tm=512, tn=1024, out_dtype=F32, name=f"mix_out_bx{tag}")
    g["w_out"] = _mm_tn(sv["cat"], dy1, tk=512, tn=1024, tm=512, name=f"mix_out_bw{tag}")
    u = sv["u"]
    dq_sb, dk_sb, dv_sb, *got = _sb_bwd(u, dcat, name=f"sb_bwd{tag}", carry=scatter)
    if riding:
        rode = riding.after_scatter(got[0])
    pw_b16 = p["cv_pw_w"].astype(BF16)
    dc, dpw, vec = _cv_bwd_local(sv["c"], dcat, _row(p["cv_ln_g"]), _row(p["cv_ln_b"]), pw_b16, name=f"cv_bwd_a{tag}")
    g["cv_pw_w"] = dpw
    vec = vec.reshape(3, 8, CV_W).sum(axis=1)
    g["cv_pw_b"], g["cv_ln_g"], g["cv_ln_b"] = vec[0], vec[1], vec[2]
    du_cv, dcw, dcb = _cv_bwd_conv(u, dc, p["cv_w"], name=f"cv_bwd_b{tag}")
    g["cv_w"] = dcw.reshape(CV_K, 8, CV_W).sum(axis=1)
    g["cv_b"] = dcb.sum(axis=0)
    dop, stats = _dl_bwd_prep(dcat, sv["o_dl"], sv["lse"], name=f"dl_prep_b{tag}")
    cur, prev = _dl_bwd(sv["qkv"], dop, stats, name=f"dl_bwd{tag}")
    du_dl = _dl_bwd_finish(cur, prev, cos, sin, name=f"dl_fin_b{tag}")
    du = jnp.concatenate([dq_sb.astype(BF16), dk_sb.astype(BF16), dv_sb.astype(BF16), du_cv, du_dl], axis=1)
    dn1 = _mm_nt(du, p["w_in"], tm=512, tn=512, out_dtype=F32, name=f"mix_in_bx{tag}")
    g["w_in"] = _mm_tn(sv["n1"], du, tk=512, tn=1408, tm=512, name=f"mix_in_bw{tag}")
    dh0, dgp = _rms_bwd(sv["h0"], _row(p["mix_norm_pre"]), dn1, dh1, out_dtype=F32, tm=256, name=f"mix_pre_b{tag}")
    g["mix_norm_pre"] = s8(dgp)
    return dh0, g, rode


def _step(x, mem, positions, loss_target, w, m, v):
    depth = w["w_in"].shape[0]
    xi, yi, ci = lax.axis_index("x"), lax.axis_index("y"), lax.axis_index("c")
    chip = 2 * xi + yi
    h = x[0]
    mem0 = mem[0]
    s_len = h.shape[0]

    packs = [jnp.concatenate([_to_pack_rows(n, w[n][l]) for n in BIG], axis=0).astype(BF16) for l in range(depth)]

    def unpack(gathered, l):
        p, off = {}, 0
        for n, rows in PACK_ROWS:
            p[n] = _full_from_blocks(n, gathered[:, off:off + rows, :])
            off += rows
        for n, _, _ in SMALL:
            p[n] = small_full[l].get(n, w[n][l])
        return p

    def forwarded(landed, l):
        return _run_exchange(_ex_gather_forward(landed), name=f"gather_forward_l{l}")[0]

    small_w = []
    for l in range(depth):
        for n, shape, axis in SMALL:
            if axis is not None:
                full = jnp.zeros(shape, F32)
                full = lax.dynamic_update_slice_in_dim(full, w[n][l], chip * w[n][l].shape[axis], axis)
                small_w.append(full * jnp.where(ci == 0, 1.0, 0.0))
    small_w_sum = _all_reduce_small(_flat_rows(small_w), name="gather_small_weights")
    small_full, off = [{} for _ in range(depth)], 0
    for l in range(depth):
        for n, shape, axis in SMALL:
            if axis is not None:
                size = int(np.prod(shape))
                small_full[l][n] = small_w_sum.reshape(-1)[off:off + size].reshape(shape)
                off += size
    params = [unpack(forwarded(_run_exchange(_ex_gather(packs[0]), name="gather_l0")[0], 0), 0)]

    inv_freq = ROPE_THETA ** (-jnp.arange(HD // 2, dtype=F32) / (HD // 2))
    cos, sin = _rope_tables(positions.reshape(s_len, 1), jnp.tile(inv_freq, 4).reshape(1, LANES), name="rope_tables")

    saved = []
    for l in range(depth):
        nxt = _ex_gather(packs[l + 1]) if l + 1 < depth else None
        h, sv = _layer_fwd(h, mem0, params[l], cos, sin, f"_l{l}", riding=nxt)
        saved.append(sv)
        if nxt is not None:
            params.append(unpack(forwarded(sv["rode"][0], l + 1), l + 1))
    dh, sq = _loss_grad(h, loss_target[0], tm=256, name="loss_grad")
    loss = lax.psum(0.5 * jnp.sum(sq) / D_MODEL, ("x", "y", "c"))

    c_arr, j_arr = jnp.reshape(ci, (1,)).astype(jnp.int32), jnp.reshape(chip, (1,)).astype(jnp.int32)
    grads, gfull, pending = [None] * depth, [None] * depth, None
    for l in reversed(range(depth)):
        dh, grads[l], rode = _layer_bwd(dh, mem0, params[l], saved[l], cos, sin, f"_l{l}", riding=pending)
        if pending is not None:
            gfull[l + 1] = rode
        gw = jnp.concatenate([_blocks_from_full(n, grads[l][n]) for n in BIG], axis=1)
        pending = _ReduceScatter(gw, c_arr, j_arr, f"_l{l}")
    gfull[0] = pending.run()
    grad_x = dh[None]

    out_g, out_d, out_m, out_v = {}, {}, {}, {}
    off = 0
    for n, rows in PACK_ROWS:
        shard_shape = w[n].shape
        g_n = jnp.stack([gl[off:off + rows, :] for gl in gfull]).reshape(shard_shape)
        off += rows
        flat = lambda a: a.reshape(-1, shard_shape[-1])
        d_n, m_n, v_n = _adamw(flat(w[n]), flat(g_n), flat(m[n]), flat(v[n]), name=f"adamw_{n}")
        out_g[n], out_d[n], out_m[n], out_v[n] = g_n, d_n.reshape(shard_shape), m_n.reshape(shard_shape), v_n.reshape(shard_shape)

    g_small = _all_reduce_small(_flat_rows([grads[l][n] for l in range(depth) for n, _, _ in SMALL]),
                                name="all_reduce_small_grads").reshape(-1)
    local_g, off = {}, 0
    for l in range(depth):
        for n, shape, axis in SMALL:
            size = int(np.prod(shape))
            full = g_small[off:off + size].reshape(shape)
            off += size
            if axis is not None:
                blk = w[n].shape[1 + axis]
                full = lax.dynamic_slice_in_dim(full, chip * blk, blk, axis)
            local_g.setdefault(n, []).append(full)
    names = [n for n, _, _ in SMALL]
    g_loc = {n: jnp.stack(local_g[n]) for n in names}
    d_s, m_s, v_s = _adamw(_flat_rows([w[n] for n in names]), _flat_rows([g_loc[n] for n in names]),
                           _flat_rows([m[n] for n in names]), _flat_rows([v[n] for n in names]), name="adamw_small")
    off = 0
    for n in names:
        size = int(np.prod(w[n].shape))
        take = lambda a: a.reshape(-1)[off:off + size].reshape(w[n].shape)
        out_g[n], out_d[n], out_m[n], out_v[n] = g_loc[n], take(d_s), take(m_s), take(v_s)
        off += size

    outs = [loss, grad_x]
    for group in (out_g, out_d, out_m, out_v):
        outs += [group[n] for n in WEIGHT_ORDER]
    return tuple(outs)


def kernel(x, mem, positions, mix_norm_pre, w_in, cv_w, cv_b, cv_ln_g, cv_ln_b, cv_pw_w, cv_pw_b, w_out, mix_norm_post, x_norm_pre, mem_norm, x_wq, x_wk, x_wv, x_wo, x_norm_post, ffn_norm_pre, ffn_w_up, ffn_conv_w, ffn_conv_b, ffn_w_down, ffn_norm_post, loss_target, m_mix_norm_pre, m_w_in, m_cv_w, m_cv_b, m_cv_ln_g, m_cv_ln_b, m_cv_pw_w, m_cv_pw_b, m_w_out, m_mix_norm_post, m_x_norm_pre, m_mem_norm, m_x_wq, m_x_wk, m_x_wv, m_x_wo, m_x_norm_post, m_ffn_norm_pre, m_ffn_w_up, m_ffn_conv_w, m_ffn_conv_b, m_ffn_w_down, m_ffn_norm_post, v_mix_norm_pre, v_w_in, v_cv_w, v_cv_b, v_cv_ln_g, v_cv_ln_b, v_cv_pw_w, v_cv_pw_b, v_w_out, v_mix_norm_post, v_x_norm_pre, v_mem_norm, v_x_wq, v_x_wk, v_x_wv, v_x_wo, v_x_norm_post, v_ffn_norm_pre, v_ffn_w_up, v_ffn_conv_w, v_ffn_conv_b, v_ffn_w_down, v_ffn_norm_post):
    w = dict(zip(WEIGHT_ORDER, (mix_norm_pre, w_in, cv_w, cv_b, cv_ln_g, cv_ln_b, cv_pw_w, cv_pw_b, w_out, mix_norm_post, x_norm_pre, mem_norm, x_wq, x_wk, x_wv, x_wo, x_norm_post, ffn_norm_pre, ffn_w_up, ffn_conv_w, ffn_conv_b, ffn_w_down, ffn_norm_post)))
    m = dict(zip(WEIGHT_ORDER, (m_mix_norm_pre, m_w_in, m_cv_w, m_cv_b, m_cv_ln_g, m_cv_ln_b, m_cv_pw_w, m_cv_pw_b, m_w_out, m_mix_norm_post, m_x_norm_pre, m_mem_norm, m_x_wq, m_x_wk, m_x_wv, m_x_wo, m_x_norm_post, m_ffn_norm_pre, m_ffn_w_up, m_ffn_conv_w, m_ffn_conv_b, m_ffn_w_down, m_ffn_norm_post)))
    v = dict(zip(WEIGHT_ORDER, (v_mix_norm_pre, v_w_in, v_cv_w, v_cv_b, v_cv_ln_g, v_cv_ln_b, v_cv_pw_w, v_cv_pw_b, v_w_out, v_mix_norm_post, v_x_norm_pre, v_mem_norm, v_x_wq, v_x_wk, v_x_wv, v_x_wo, v_x_norm_post, v_ffn_norm_pre, v_ffn_w_up, v_ffn_conv_w, v_ffn_conv_b, v_ffn_w_down, v_ffn_norm_post)))
    return _step(x, mem, positions, loss_target, w, m, v)
```

```python
import functools

import jax
import jax.numpy as jnp
import numpy as np
from jax import lax
from jax.experimental import pallas as pl
from jax.experimental.pallas import tpu as pltpu

F32, BF16 = jnp.float32, jnp.bfloat16
MESH = pl.DeviceIdType.MESH
EPS = 1e-6
LANES = 128
BLK = 128
HD = 64
D_MODEL = 1024
D_FF = 2816
SB_W, CV_W, DL_W = 256, 256, 512
CV_K = 31
ROPE_THETA = 10000.0
DILATIONS = (1, 4, 16)
X_HEADS, X_HD = 4, 256
ADAM_LR, ADAM_B1, ADAM_B2, ADAM_EPS, ADAM_WD, ADAM_STEP = 0.001, 0.9, 0.999, 1e-08, 0.01, 10
NEG_INF = float("-inf")
MIB = 1 << 20

PACK_ROWS = (("w_in", 704), ("w_out", 256), ("x_wq", 256), ("x_wk", 256), ("x_wv", 256), ("x_wo", 256),
             ("ffn_w_up", 1408), ("ffn_w_down", 704))
PACK_RL = sum(r for _, r in PACK_ROWS)


def _cp(vmem_mb=48):
    return pltpu.CompilerParams(vmem_limit_bytes=vmem_mb * MIB)


def _dot(a, b):
    return jnp.dot(a, b, preferred_element_type=F32)


def _dot_nt(a, b):
    return lax.dot_general(a, b, (((1,), (1,)), ((), ())), preferred_element_type=F32)


def _dot_tn(a, b):
    return lax.dot_general(a, b, (((0,), (0,)), ((), ())), preferred_element_type=F32)


def _dot_hilo(x, m):
    hi = x.astype(BF16)
    lo = (x - hi.astype(F32)).astype(BF16)
    return _dot(hi, m) + _dot(lo, m)


def _rowsum8(x):
    t, c = x.shape
    return x.reshape(t // 8, 8, c).sum(axis=0)


def _acc_out(ref, i, val):
    @pl.when(i == 0)
    def _():
        ref[...] = val

    @pl.when(i > 0)
    def _():
        ref[...] += val


def _tile(n, cap, mult=8):
    t = min(n, cap)
    while n % t or t % mult:
        t -= 1
    return t


def _rms_mm(x, g, w, *, tm, tn, out_dtype, name, carry=None):
    m, d = x.shape
    n_out = w.shape[1]

    def body(x_ref, g_ref, w_ref, n_ref, o_ref):
        @pl.when(pl.program_id(1) == 0)
        def _():
            xv = x_ref[...]
            r = lax.rsqrt(jnp.mean(xv * xv, axis=-1, keepdims=True) + EPS)
            n_ref[...] = (xv * r * g_ref[...]).astype(BF16)

        o_ref[...] = _dot(n_ref[...], w_ref[...]).astype(out_dtype)

    return _call(
        body, grid=(m // tm, n_out // tn), name=name, carry=carry,
        in_specs=[pl.BlockSpec((tm, d), lambda i, j: (i, 0)), pl.BlockSpec((1, d), lambda i, j: (0, 0)),
                  pl.BlockSpec((d, tn), lambda i, j: (0, j))],
        out_specs=[pl.BlockSpec((tm, d), lambda i, j: (i, 0)), pl.BlockSpec((tm, tn), lambda i, j: (i, j))],
        out_shape=[jax.ShapeDtypeStruct((m, d), BF16), jax.ShapeDtypeStruct((m, n_out), out_dtype)],
        args=(x, g, w))


def _mm_post(a, w, h, g, *, tm, name, carry=None):
    m, k = a.shape
    d = w.shape[1]

    def body(a_ref, w_ref, h_ref, g_ref, y_ref, ho_ref):
        y = _dot(a_ref[...], w_ref[...])
        y_ref[...] = y
        r = lax.rsqrt(jnp.mean(y * y, axis=-1, keepdims=True) + EPS)
        ho_ref[...] = h_ref[...] + y * r * g_ref[...]

    return _call(
        body, grid=(m // tm,), name=name, carry=carry,
        in_specs=[pl.BlockSpec((tm, k), lambda i: (i, 0)), pl.BlockSpec((k, d), lambda i: (0, 0)),
                  pl.BlockSpec((tm, d), lambda i: (i, 0)), pl.BlockSpec((1, d), lambda i: (0, 0))],
        out_specs=[pl.BlockSpec((tm, d), lambda i: (i, 0)), pl.BlockSpec((tm, d), lambda i: (i, 0))],
        out_shape=[jax.ShapeDtypeStruct((m, d), F32), jax.ShapeDtypeStruct((m, d), F32)],
        args=(a, w, h, g))


def _mm_nt(a, w, *, tm, tn, out_dtype, name):
    m, k = a.shape
    n_out = w.shape[0]

    def body(a_ref, w_ref, o_ref):
        o_ref[...] = _dot_nt(a_ref[...], w_ref[...]).astype(out_dtype)

    return pl.pallas_call(
        body, grid=(n_out // tn, m // tm), name=name,
        in_specs=[pl.BlockSpec((tm, k), lambda j, i: (i, 0)), pl.BlockSpec((tn, k), lambda j, i: (j, 0))],
        out_specs=pl.BlockSpec((tm, tn), lambda j, i: (i, j)),
        out_shape=jax.ShapeDtypeStruct((m, n_out), out_dtype),
        compiler_params=_cp())(a, w)


def _mm_tn(x, dy, *, tk, tn, tm, name):
    m, k = x.shape
    n_out = dy.shape[1]

    def body(x_ref, d_ref, o_ref):
        _acc_out(o_ref, pl.program_id(2), _dot_tn(x_ref[...], d_ref[...]))

    return pl.pallas_call(
        body, grid=(k // tk, n_out // tn, m // tm), name=name,
        in_specs=[pl.BlockSpec((tm, tk), lambda a, b, c: (c, a)), pl.BlockSpec((tm, tn), lambda a, b, c: (c, b))],
        out_specs=pl.BlockSpec((tk, tn), lambda a, b, c: (a, b)),
        out_shape=jax.ShapeDtypeStruct((k, n_out), F32),
        compiler_params=_cp())(x, dy)


def _rms_bwd(x, g, dout, res, *, out_dtype, tm, name):
    m, d = x.shape
    has_res = res is not None

    def body(*refs):
        if has_res:
            x_ref, g_ref, d_ref, r_ref, dx_ref, dg_ref = refs
        else:
            x_ref, g_ref, d_ref, dx_ref, dg_ref = refs
        xv = x_ref[...]
        dv = d_ref[...].astype(F32)
        r = lax.rsqrt(jnp.mean(xv * xv, axis=-1, keepdims=True) + EPS)
        xh = xv * r
        dxh = dv * g_ref[...]
        dx = r * (dxh - xh * jnp.mean(dxh * xh, axis=-1, keepdims=True))
        if has_res:
            dx = dx + r_ref[...]
        dx_ref[...] = dx.astype(out_dtype)
        _acc_out(dg_ref, pl.program_id(0), _rowsum8(dv * xh))

    row = pl.BlockSpec((tm, d), lambda i: (i, 0))
    ins = [row, pl.BlockSpec((1, d), lambda i: (0, 0)), row] + ([row] if has_res else [])
    args = (x, g, dout) + ((res,) if has_res else ())
    return pl.pallas_call(
        body, grid=(m // tm,), name=name, in_specs=ins,
        out_specs=[row, pl.BlockSpec((8, d), lambda i: (0, 0))],
        out_shape=[jax.ShapeDtypeStruct((m, d), out_dtype), jax.ShapeDtypeStruct((8, d), F32)],
        compiler_params=_cp())(*args)


def _loss_grad(h, tgt, *, tm, name):
    m, d = h.shape

    def body(h_ref, t_ref, dh_ref, p_ref):
        e = h_ref[...] - t_ref[...]
        dh_ref[...] = e / d
        _acc_out(p_ref, pl.program_id(0), _rowsum8(e * e))

    row = pl.BlockSpec((tm, d), lambda i: (i, 0))
    return pl.pallas_call(
        body, grid=(m // tm,), name=name, in_specs=[row, row],
        out_specs=[row, pl.BlockSpec((8, d), lambda i: (0, 0))],
        out_shape=[jax.ShapeDtypeStruct((m, d), F32), jax.ShapeDtypeStruct((8, d), F32)],
        compiler_params=_cp())(h, tgt)


def _adamw(w, g, m, v, *, name):
    r, c = w.shape
    tr = _tile(r, 256)

    def body(w_ref, g_ref, m_ref, v_ref, d_ref, mo_ref, vo_ref):
        gv = g_ref[...]
        m2 = ADAM_B1 * m_ref[...] + (1.0 - ADAM_B1) * gv
        v2 = ADAM_B2 * v_ref[...] + (1.0 - ADAM_B2) * jnp.square(gv)
        m_hat = m2 / (1.0 - ADAM_B1 ** ADAM_STEP)
        v_hat = v2 / (1.0 - ADAM_B2 ** ADAM_STEP)
        d_ref[...] = -ADAM_LR * (m_hat / (jnp.sqrt(v_hat) + ADAM_EPS) + ADAM_WD * w_ref[...])
        mo_ref[...] = m2
        vo_ref[...] = v2

    blk = pl.BlockSpec((tr, c), lambda i: (i, 0))
    return pl.pallas_call(
        body, grid=(r // tr,), name=name, in_specs=[blk] * 4, out_specs=[blk] * 3,
        out_shape=[jax.ShapeDtypeStruct((r, c), F32)] * 3, compiler_params=_cp())(w, g, m, v)


def _head_masks():
    lane = lax.broadcasted_iota(jnp.int32, (BLK, LANES), 1)
    row = lax.broadcasted_iota(jnp.int32, (BLK, LANES), 0)
    return lane, row, lane < HD


def _sb_scores(q_a, k, before):
    z = _dot_nt(q_a, k)
    sp = jnp.log1p(jnp.exp(-jnp.abs(z)))
    ls_pos = jnp.minimum(z, 0.0) - sp
    lkeep = jnp.where(before, ls_pos - z, 0.0)
    return ls_pos, lkeep


SB_DEAD = -104.0


def _sb_alive(jj, i, carry):
    return jnp.logical_and(jj <= i, jnp.max(carry) > SB_DEAD)


SB_QB = 2
SB_ROWS = SB_QB * 2 * BLK


def _sb_before(jj):
    lane = lax.broadcasted_iota(jnp.int32, (SB_ROWS, LANES), 1)
    row = lax.broadcasted_iota(jnp.int32, (SB_ROWS, LANES), 0)
    below_diag = jj - (SB_QB - 1) + row // (2 * BLK)
    return jnp.logical_or(below_diag > 0, jnp.logical_and(below_diag == 0, lane < row % BLK))


def _sb_stack(x, lane_h):
    return jnp.concatenate([_stack_heads(x[b * BLK:(b + 1) * BLK], lane_h) for b in range(SB_QB)], axis=0)


def _sb_unstack(x, lane_h):
    return jnp.concatenate([jnp.where(lane_h, x[2 * b * BLK:(2 * b + 1) * BLK], x[(2 * b + 1) * BLK:(2 * b + 2) * BLK])
                            for b in range(SB_QB)], axis=0)


def _sb_fwd(u, *, name, carry=None):
    s_len = u.shape[0]
    qrows = SB_QB * BLK

    def body(q_ref, k_ref, v_ref, o_ref):
        top = pl.program_id(1) * SB_QB + SB_QB - 1
        lane, row, lane_h = _head_masks()
        suffix = (row > lane).astype(BF16)
        qs = _sb_stack(q_ref[...] * 0.125, lane_h)

        def step(state):
            jj, cc, acc = state
            off = pl.multiple_of((top - jj) * BLK, BLK)
            k = k_ref[pl.ds(off, BLK), :].astype(BF16)
            v = v_ref[pl.ds(off, BLK), :].astype(BF16)
            before = _sb_before(jj)
            ls_pos, lkeep = _sb_scores(qs, k, before)
            between = _dot_hilo(lkeep, suffix) + cc
            att = jnp.where(before, jnp.exp(ls_pos + between), 0.0)
            return jj + 1, cc + jnp.sum(lkeep, axis=1, keepdims=True), acc + _dot(att.astype(BF16), v)

        init = (jnp.int32(0), jnp.zeros((SB_ROWS, 1), F32), jnp.zeros((SB_ROWS, LANES), F32))
        acc = lax.while_loop(lambda st: _sb_alive(st[0], top, st[1]), step, init)[2]
        o_ref[...] = _sb_unstack(acc, lane_h).astype(BF16)

    return _call(
        body, grid=(2, s_len // qrows), name=name, carry=carry,
        in_specs=[pl.BlockSpec((qrows, LANES), lambda hp, i: (i, hp)),
                  pl.BlockSpec((s_len, LANES), lambda hp, i: (0, 2 + hp)),
                  pl.BlockSpec((s_len, LANES), lambda hp, i: (0, 4 + hp))],
        out_specs=[pl.BlockSpec((qrows, LANES), lambda hp, i: (i, hp))],
        out_shape=[jax.ShapeDtypeStruct((s_len, SB_W), BF16)], args=(u, u, u))


def _sb_bwd(u, dcat, *, name, carry=None):
    s_len = u.shape[0]
    nq = s_len // BLK
    qrows = SB_QB * BLK

    def body(q_ref, k_ref, v_ref, do_ref, dq_ref, dk_ref, dv_ref, g_scr, b_scr):
        step = pl.program_id(1)
        top = step * SB_QB + SB_QB - 1
        lane, row, lane_h = _head_masks()
        suffix = (row > lane).astype(BF16)
        prefix = (row < lane).astype(BF16)
        qf = q_ref[...]
        qs = _sb_stack(qf * 0.125, lane_h)
        qu = _sb_stack(qf, lane_h)
        dos = _sb_stack(do_ref[...], lane_h)

        @pl.when(step == 0)
        def _():
            dk_ref[...] = jnp.zeros_like(dk_ref)
            dv_ref[...] = jnp.zeros_like(dv_ref)

        def down(state):
            jj, cc = state
            j = top - jj
            off = pl.multiple_of(j * BLK, BLK)
            k = k_ref[pl.ds(off, BLK), :].astype(BF16)
            v = v_ref[pl.ds(off, BLK), :].astype(BF16)
            before = _sb_before(jj)
            ls_pos, lkeep = _sb_scores(qs, k, before)
            between = _dot_hilo(lkeep, suffix) + cc
            att = jnp.where(before, jnp.exp(ls_pos + between), 0.0)
            g_scr[j] = att * _dot_nt(dos, v)
            b_scr[j] = jnp.exp(ls_pos)
            dv_ref[pl.ds(off, BLK), :] += _dot_tn(att.astype(BF16), dos)
            return jj + 1, cc + jnp.sum(lkeep, axis=1, keepdims=True)

        zc = jnp.zeros((SB_ROWS, 1), F32)
        visited = lax.while_loop(lambda st: _sb_alive(st[0], top, st[1]), down, (jnp.int32(0), zc))[0]

        def up(j, carry):
            pc, dq = carry
            off = pl.multiple_of(j * BLK, BLK)
            k = k_ref[pl.ds(off, BLK), :].astype(BF16)
            g, beta = g_scr[j], b_scr[j]
            below = _dot_hilo(g, prefix) + pc
            dz = (jnp.where(_sb_before(top - j), g * (1.0 - beta) - beta * below, 0.0) * 0.125).astype(BF16)
            dk_ref[pl.ds(off, BLK), :] += _dot_tn(dz, qu)
            return pc + jnp.sum(g, axis=1, keepdims=True), dq + _dot(dz, k)

        dq = lax.fori_loop(top + 1 - visited, top + 1, up, (zc, jnp.zeros((SB_ROWS, LANES), F32)))[1]
        dq_ref[...] = _sb_unstack(dq, lane_h)

    col = lambda c0: pl.BlockSpec((s_len, LANES), lambda hp, i: (0, c0 + hp))
    blk = pl.BlockSpec((qrows, LANES), lambda hp, i: (i, hp))
    acc = pl.BlockSpec((s_len, LANES), lambda hp, i: (0, hp))
    return _call(
        body, grid=(2, s_len // qrows), name=name, carry=carry, in_specs=[blk, col(2), col(4), blk],
        out_specs=[blk, acc, acc], out_shape=[jax.ShapeDtypeStruct((s_len, SB_W), F32)] * 3,
        scratch_shapes=[pltpu.VMEM((nq, SB_ROWS, LANES), F32), pltpu.VMEM((nq, SB_ROWS, LANES), F32)],
        vmem_mb=56, args=(u, u, u, dcat))


CV_T = 512
CV_H = 32


def _cv_specs(s_len):
    cur = lambda c: pl.BlockSpec((CV_T, CV_W), lambda i: (i, c))
    prev = lambda c: pl.BlockSpec((CV_H, CV_W), lambda i: (jnp.maximum(i * (CV_T // CV_H) - 1, 0), c))
    nxt = lambda c: pl.BlockSpec((CV_H, CV_W),
                                 lambda i: (jnp.minimum((i + 1) * (CV_T // CV_H), s_len // CV_H - 1), c))
    full = lambda r: pl.BlockSpec((r, CV_W), lambda i: (0, 0))
    return cur, prev, nxt, full


def _glu_into(gp_ref, val_ref, gate_ref, valp_ref, gatep_ref, i):
    gp_ref[0:CV_H, :] = jnp.where(i > 0, valp_ref[...] * jax.nn.sigmoid(gatep_ref[...]), 0.0)
    gp_ref[CV_H:, :] = val_ref[...] * jax.nn.sigmoid(gate_ref[...])


def _cv_fwd(u, cv_w, cv_b, ln_g, ln_b, pw_w, pw_b, *, name):
    s_len = u.shape[0]
    cur, prev, _, full = _cv_specs(s_len)

    def body(val_ref, gate_ref, valp_ref, gatep_ref, w_ref, b_ref, g_ref, be_ref, pw_ref, pb_ref,
             o_ref, c_ref, gp_ref):
        _glu_into(gp_ref, val_ref, gate_ref, valp_ref, gatep_ref, pl.program_id(0))
        acc = jnp.zeros((CV_T, CV_W), F32) + b_ref[...]
        for k in range(CV_K):
            acc = acc + w_ref[k:k + 1, :] * gp_ref[pl.ds(CV_H - CV_K + 1 + k, CV_T), :]
        c_ref[...] = acc
        mu = jnp.mean(acc, axis=-1, keepdims=True)
        xc = acc - mu
        xh = xc * lax.rsqrt(jnp.mean(xc * xc, axis=-1, keepdims=True) + EPS)
        a = xh * g_ref[...] + be_ref[...]
        s = a * jax.nn.sigmoid(a)
        o_ref[...] = (_dot(s.astype(BF16), pw_ref[...]) + pb_ref[...]).astype(BF16)

    return pl.pallas_call(
        body, grid=(s_len // CV_T,), name=name,
        in_specs=[cur(3), cur(4), prev(3), prev(4), full(CV_K), full(1), full(1), full(1), full(CV_W), full(1)],
        out_specs=[cur(0), cur(0)],
        out_shape=[jax.ShapeDtypeStruct((s_len, CV_W), BF16), jax.ShapeDtypeStruct((s_len, CV_W), F32)],
        scratch_shapes=[pltpu.VMEM((CV_T + CV_H, CV_W), F32)], compiler_params=_cp())(
            u, u, u, u, cv_w, cv_b, ln_g, ln_b, pw_w, pw_b)


def _cv_bwd_local(c, dcat, ln_g, ln_b, pw_w, *, name):
    s_len = c.shape[0]
    cur, _, _, full = _cv_specs(s_len)

    def body(c_ref, db_ref, g_ref, be_ref, pw_ref, dc_ref, dpw_ref, vec_ref):
        i = pl.program_id(0)
        cv = c_ref[...]
        db = db_ref[...]
        mu = jnp.mean(cv, axis=-1, keepdims=True)
        xc = cv - mu
        rstd = lax.rsqrt(jnp.mean(xc * xc, axis=-1, keepdims=True) + EPS)
        xh = xc * rstd
        a = xh * g_ref[...] + be_ref[...]
        sg = jax.nn.sigmoid(a)
        s = a * sg
        dbb = db.astype(BF16)
        ds = _dot_nt(dbb, pw_ref[...])
        da = ds * (sg * (1.0 + a * (1.0 - sg)))
        dxh = da * g_ref[...]
        dc_ref[...] = rstd * (dxh - jnp.mean(dxh, axis=-1, keepdims=True)
                              - xh * jnp.mean(dxh * xh, axis=-1, keepdims=True))
        _acc_out(dpw_ref, i, _dot_tn(s.astype(BF16), dbb))
        _acc_out(vec_ref, i, jnp.concatenate([_rowsum8(db), _rowsum8(da * xh), _rowsum8(da)], axis=0))

    return pl.pallas_call(
        body, grid=(s_len // CV_T,), name=name,
        in_specs=[cur(0), cur(1), full(1), full(1), full(CV_W)],
        out_specs=[cur(0), full(CV_W), full(24)],
        out_shape=[jax.ShapeDtypeStruct((s_len, CV_W), F32), jax.ShapeDtypeStruct((CV_W, CV_W), F32),
                   jax.ShapeDtypeStruct((24, CV_W), F32)], compiler_params=_cp())(c, dcat, ln_g, ln_b, pw_w)


def _cv_bwd_conv(u, dc, cv_w, *, name):
    s_len = u.shape[0]
    cur, prev, nxt, full = _cv_specs(s_len)
    last = s_len // CV_T - 1

    def body(val_ref, gate_ref, valp_ref, gatep_ref, dc_ref, dcn_ref, w_ref, du_ref, dw_ref, dbias_ref,
             gp_ref, dcp_ref):
        i = pl.program_id(0)
        _glu_into(gp_ref, val_ref, gate_ref, valp_ref, gatep_ref, i)
        dcv = dc_ref[...]
        dcp_ref[0:CV_T, :] = dcv
        dcp_ref[CV_T:, :] = jnp.where(i < last, dcn_ref[...], 0.0)
        dg = jnp.zeros((CV_T, CV_W), F32)
        parts = []
        for k in range(CV_K):
            dg = dg + w_ref[k:k + 1, :] * dcp_ref[pl.ds(CV_K - 1 - k, CV_T), :]
            parts.append(_rowsum8(dcv * gp_ref[pl.ds(CV_H - CV_K + 1 + k, CV_T), :]))
        _acc_out(dw_ref, i, jnp.concatenate(parts, axis=0))
        _acc_out(dbias_ref, i, _rowsum8(dcv))
        val = val_ref[...]
        sg = jax.nn.sigmoid(gate_ref[...])
        du_ref[:, 0:CV_W] = (dg * sg).astype(BF16)
        du_ref[:, CV_W:] = (dg * val * sg * (1.0 - sg)).astype(BF16)

    return pl.pallas_call(
        body, grid=(s_len // CV_T,), name=name,
        in_specs=[cur(3), cur(4), prev(3), prev(4), cur(0), nxt(0), full(CV_K)],
        out_specs=[pl.BlockSpec((CV_T, 2 * CV_W), lambda i: (i, 0)), full(CV_K * 8), full(8)],
        out_shape=[jax.ShapeDtypeStruct((s_len, 2 * CV_W), BF16), jax.ShapeDtypeStruct((CV_K * 8, CV_W), F32),
                   jax.ShapeDtypeStruct((8, CV_W), F32)],
        scratch_shapes=[pltpu.VMEM((CV_T + CV_H, CV_W), F32), pltpu.VMEM((CV_T + CV_H, CV_W), F32)],
        compiler_params=_cp())(u, u, u, u, dc, dc, cv_w)


def _rope_tables(pos_col, inv_freq_row, *, name):
    s_len = pos_col.shape[0]

    def body(p_ref, f_ref, cos_ref, sin_ref):
        ang = p_ref[...].astype(F32) * f_ref[...]
        lane = lax.broadcasted_iota(jnp.int32, (s_len, LANES), 1)
        sn = jnp.sin(ang)
        cos_ref[...] = jnp.cos(ang)
        sin_ref[...] = jnp.where(lane % HD < HD // 2, -sn, sn)

    return pl.pallas_call(body, name=name, out_shape=[jax.ShapeDtypeStruct((s_len, LANES), F32)] * 2,
                          compiler_params=_cp())(pos_col, inv_freq_row)


def _rot_half(x):
    lane = lax.broadcasted_iota(jnp.int32, x.shape, 1)
    return jnp.where(lane % HD < HD // 2, pltpu.roll(x, LANES - HD // 2, 1), pltpu.roll(x, HD // 2, 1))


def _permute_rows(dst_ref, src_ref, d, dtype):
    s_len = src_ref.shape[0]
    seg = s_len // d
    if d == 1:
        dst_ref[...] = src_ref[...].astype(dtype)
        return
    for r in range(d):
        dst_ref[r * seg:(r + 1) * seg, :] = src_ref[pl.ds(r, seg, stride=d), :].astype(dtype)


def _unpermute_rows(dst_ref, src_ref, d):
    s_len = src_ref.shape[0]
    seg = s_len // d
    if d == 1:
        dst_ref[...] = src_ref[...]
        return
    for r in range(d):
        dst_ref[pl.ds(r, seg, stride=d), :] = src_ref[r * seg:(r + 1) * seg, :]


def _rope_perm(u, cos, sin, *, name):
    s_len = u.shape[0]

    def body(x_ref, cos_ref, sin_ref, o_ref, scr):
        a = pl.program_id(0)
        x = x_ref[...]
        rot = a < 2
        scr[...] = x * jnp.where(rot, cos_ref[...], 1.0) + _rot_half(x) * jnp.where(rot, sin_ref[...], 0.0)
        for n, d in enumerate(DILATIONS):
            _permute_rows(o_ref.at[n], scr, d, BF16)

    tab = pl.BlockSpec((s_len, LANES), lambda a, cb: (0, 0))
    return pl.pallas_call(
        body, grid=(3, 4), name=name,
        in_specs=[pl.BlockSpec((s_len, LANES), lambda a, cb: (0, 10 + 4 * a + cb)), tab, tab],
        out_specs=pl.BlockSpec((None, 3, s_len, LANES), lambda a, cb: (a, 0, 0, cb)),
        out_shape=jax.ShapeDtypeStruct((3, 3, s_len, DL_W), BF16),
        scratch_shapes=[pltpu.VMEM((s_len, LANES), F32)], compiler_params=_cp())(u, cos, sin)


DL_UNROLL = 4


def _dl_band(rows):
    lane = lax.broadcasted_iota(jnp.int32, (rows, LANES), 1)
    row = lax.broadcasted_iota(jnp.int32, (rows, LANES), 0) % BLK
    return lane <= row, lane >= row


def _dl_first(s_len, n, i):
    nb = jnp.where(n == 0, s_len // BLK, jnp.where(n == 1, s_len // (BLK * DILATIONS[1]),
                                                   s_len // (BLK * DILATIONS[2])))
    return lax.rem(i, nb) == 0


def _stack_heads(x, lane_h):
    return jnp.concatenate([jnp.where(lane_h, x, 0.0), jnp.where(lane_h, 0.0, x)], axis=0).astype(BF16)


def _dl_rows(i):
    cur = pl.ds(pl.multiple_of(i * BLK, BLK), BLK)
    prev = pl.ds(pl.multiple_of(jnp.maximum(i - 1, 0) * BLK, BLK), BLK)
    return cur, prev


def _dl_in_specs(s_len):
    return [pl.BlockSpec((None, None, s_len, LANES), functools.partial(lambda a, n, hp: (a, n, 0, hp), a))
            for a in range(3)]


def _dl_fwd(qkv, *, name, carry=None):
    s_len = qkv.shape[2]

    def body(q_ref, k_ref, v_ref, o_ref, l_ref):
        n = pl.program_id(0)
        lane_h = _head_masks()[2]
        band_c, band_p = _dl_band(2 * BLK)
        ones = jnp.ones((BLK, LANES), BF16)

        @pl.loop(0, s_len // BLK, step=DL_UNROLL)
        def _(i0):
            blocks = [i0 + t for t in range(DL_UNROLL)]
            rows = [_dl_rows(i) for i in blocks]
            scores = []
            for cur, prev in rows:
                qs = _stack_heads(q_ref[cur, :] * 0.125, lane_h)
                scores.append((_dot_nt(qs, k_ref[cur, :]), _dot_nt(qs, k_ref[prev, :])))
            probs = []
            for i, (sc, sp) in zip(blocks, scores):
                sc = jnp.where(band_c, sc, NEG_INF)
                sp = jnp.where(jnp.logical_and(band_p, jnp.logical_not(_dl_first(s_len, n, i))), sp, NEG_INF)
                m = jnp.max(jnp.maximum(sc, sp), axis=1, keepdims=True)
                probs.append((jnp.exp(sc - m).astype(BF16), jnp.exp(sp - m).astype(BF16), m))
            for (cur, prev), (pc, pp, m) in zip(rows, probs):
                r = (_dot(pc, jnp.concatenate([v_ref[cur, :], ones], axis=1))
                     + _dot(pp, jnp.concatenate([v_ref[prev, :], ones], axis=1)))
                den = jnp.where(lane_h, r[:BLK, LANES:], r[BLK:, LANES:])
                o_ref[cur, :] = jnp.where(lane_h, r[:BLK, :LANES], r[BLK:, :LANES]) / den
                l_ref[cur, :] = jnp.where(lane_h, m[:BLK], m[BLK:]) + jnp.log(den)

    out = pl.BlockSpec((None, s_len, LANES), lambda n, hp: (n, 0, hp))
    return _call(
        body, grid=(3, 4), name=name, carry=carry, in_specs=_dl_in_specs(s_len), out_specs=[out, out],
        out_shape=[jax.ShapeDtypeStruct((3, s_len, DL_W), F32)] * 2, args=(qkv, qkv, qkv))


def _dl_mix(o_p, l_p, *, name):
    s_len = o_p.shape[1]

    def body(o_ref, l_ref, ob_ref, of_ref, lt_ref, o_scr, l_scr):
        n = pl.program_id(1)
        for k, d in enumerate(DILATIONS):
            @pl.when(n == k)
            def _(k=k, d=d):
                _unpermute_rows(o_scr.at[k], o_ref, d)
                _unpermute_rows(l_scr.at[k], l_ref, d)

        @pl.when(n == 2)
        def _():
            l0, l1, l2 = l_scr[0], l_scr[1], l_scr[2]
            m = jnp.maximum(jnp.maximum(l0, l1), l2)
            e0, e1, e2 = jnp.exp(l0 - m), jnp.exp(l1 - m), jnp.exp(l2 - m)
            den = e0 + e1 + e2
            o = (e0 / den) * o_scr[0] + (e1 / den) * o_scr[1] + (e2 / den) * o_scr[2]
            of_ref[...] = o
            ob_ref[...] = o.astype(BF16)
            lt_ref[...] = m + jnp.log(den)

    inb = pl.BlockSpec((None, s_len, LANES), lambda cb, n: (n, 0, cb))
    outb = pl.BlockSpec((s_len, LANES), lambda cb, n: (0, cb))
    return pl.pallas_call(
        body, grid=(4, 3), name=name, in_specs=[inb, inb], out_specs=[outb, outb, outb],
        out_shape=[jax.ShapeDtypeStruct((s_len, DL_W), BF16), jax.ShapeDtypeStruct((s_len, DL_W), F32),
                   jax.ShapeDtypeStruct((s_len, DL_W), F32)],
        scratch_shapes=[pltpu.VMEM((3, s_len, LANES), F32), pltpu.VMEM((3, s_len, LANES), F32)],
        compiler_params=_cp())(o_p, l_p)


def _dl_bwd_prep(dcat, o, lse, *, name):
    s_len = o.shape[0]

    def body(do_ref, o_ref, l_ref, dop_ref, st_ref, d_scr):
        n = pl.program_id(1)

        @pl.when(n == 0)
        def _():
            r0 = lax.broadcasted_iota(jnp.int32, (LANES, LANES), 0) // HD
            r1 = lax.broadcasted_iota(jnp.int32, (LANES, LANES), 1) // HD
            d_scr[...] = _dot_hilo(do_ref[...] * o_ref[...], (r0 == r1).astype(BF16))

        for k, d in enumerate(DILATIONS):
            @pl.when(n == k)
            def _(d=d):
                _permute_rows(dop_ref, do_ref, d, BF16)
                _permute_rows(st_ref.at[0], d_scr, d, F32)
                _permute_rows(st_ref.at[1], l_ref, d, F32)

    nat = lambda c0: pl.BlockSpec((s_len, LANES), lambda cb, n: (0, c0 + cb))
    return pl.pallas_call(
        body, grid=(4, 3), name=name, in_specs=[nat(4), nat(0), nat(0)],
        out_specs=[pl.BlockSpec((None, s_len, LANES), lambda cb, n: (n, 0, cb)),
                   pl.BlockSpec((2, None, s_len, LANES), lambda cb, n: (0, n, 0, cb))],
        out_shape=[jax.ShapeDtypeStruct((3, s_len, DL_W), BF16), jax.ShapeDtypeStruct((2, 3, s_len, DL_W), F32)],
        scratch_shapes=[pltpu.VMEM((s_len, LANES), F32)], compiler_params=_cp())(dcat, o, lse)


def _dl_bwd(qkv, dop, stats, *, name, carry=None):
    s_len = qkv.shape[2]

    def body(q_ref, k_ref, v_ref, do_ref, st_ref, cur_ref, prev_ref):
        n = pl.program_id(0)
        lane_h = _head_masks()[2]
        band_c, band_p = _dl_band(2 * BLK)

        def per_head(x):
            xr = pltpu.roll(x, HD, 1)
            return jnp.concatenate([jnp.where(lane_h, x, xr), jnp.where(lane_h, xr, x)], axis=0)

        @pl.loop(0, s_len // BLK, step=DL_UNROLL)
        def _(i0):
            blocks = [i0 + t for t in range(DL_UNROLL)]
            rows = [_dl_rows(i) for i in blocks]
            stage1 = []
            for cur, prev in rows:
                qs = _stack_heads(q_ref[cur, :] * 0.125, lane_h)
                dos = _stack_heads(do_ref[cur, :], lane_h)
                kc, kp, vc, vp = k_ref[cur, :], k_ref[prev, :], v_ref[cur, :], v_ref[prev, :]
                stage1.append((qs, dos, _dot_nt(qs, kc), _dot_nt(qs, kp), _dot_nt(dos, vc), _dot_nt(dos, vp)))
            stage2 = []
            for i, (cur, prev), (qs, dos, sc, sp, dpc, dpp) in zip(blocks, rows, stage1):
                lse, delta = per_head(st_ref[1, cur, :]), per_head(st_ref[0, cur, :])
                pc = jnp.where(band_c, jnp.exp(sc - lse), 0.0)
                pp = jnp.where(jnp.logical_and(band_p, jnp.logical_not(_dl_first(s_len, n, i))), jnp.exp(sp - lse), 0.0)
                stage2.append((pc.astype(BF16), pp.astype(BF16), (pc * (dpc - delta)).astype(BF16),
                               (pp * (dpp - delta)).astype(BF16)))
            for (cur, prev), (qs, dos, *_), (pc, pp, dsc, dsp) in zip(rows, stage1, stage2):
                dq = _dot(dsc, k_ref[cur, :]) + _dot(dsp, k_ref[prev, :])
                cur_ref[0, cur, :] = jnp.where(lane_h, dq[:BLK], dq[BLK:]) * 0.125
                cur_ref[1, cur, :] = _dot_tn(dsc, qs)
                cur_ref[2, cur, :] = _dot_tn(pc, dos)
                prev_ref[0, cur, :] = _dot_tn(dsp, qs)
                prev_ref[1, cur, :] = _dot_tn(pp, dos)

    return _call(
        body, grid=(3, 4), name=name, carry=carry,
        in_specs=_dl_in_specs(s_len) + [pl.BlockSpec((None, s_len, LANES), lambda n, hp: (n, 0, hp)),
                                        pl.BlockSpec((2, None, s_len, LANES), lambda n, hp: (0, n, 0, hp))],
        out_specs=[pl.BlockSpec((3, None, s_len, LANES), lambda n, hp: (0, n, 0, hp)),
                   pl.BlockSpec((2, None, s_len, LANES), lambda n, hp: (0, n, 0, hp))],
        out_shape=[jax.ShapeDtypeStruct((3, 3, s_len, DL_W), F32), jax.ShapeDtypeStruct((2, 3, s_len, DL_W), F32)],
        vmem_mb=56, args=(qkv, qkv, qkv, dop, stats))


def _dl_bwd_finish(cur, prev, cos, sin, *, name):
    s_len = cur.shape[2]

    def body(c_ref, p_ref, cos_ref, sin_ref, o_ref, p_scr, u_scr, acc):
        a, n = pl.program_id(0), pl.program_id(2)
        has_prev = jnp.where(a > 0, 1.0, 0.0)
        p_scr[...] = c_ref[...]
        p_scr[0:s_len - BLK, :] += has_prev * p_ref[BLK:, :]
        for k, d in enumerate(DILATIONS):
            @pl.when(n == k)
            def _(k=k, d=d):
                if k == 0:
                    acc[...] = p_scr[...]
                else:
                    _unpermute_rows(u_scr, p_scr, d)
                    acc[...] += u_scr[...]

        @pl.when(n == 2)
        def _():
            dy = acc[...]
            rot = a < 2
            o_ref[...] = (dy * jnp.where(rot, cos_ref[...], 1.0)
                          + _rot_half(dy * jnp.where(rot, sin_ref[...], 0.0))).astype(BF16)

    tab = pl.BlockSpec((s_len, LANES), lambda a, cb, n: (0, 0))
    return pl.pallas_call(
        body, grid=(3, 4, 3), name=name,
        in_specs=[pl.BlockSpec((None, None, s_len, LANES), lambda a, cb, n: (a, n, 0, cb)),
                  pl.BlockSpec((None, None, s_len, LANES), lambda a, cb, n: (jnp.maximum(a - 1, 0), n, 0, cb)),
                  tab, tab],
        out_specs=pl.BlockSpec((s_len, LANES), lambda a, cb, n: (0, 4 * a + cb)),
        out_shape=jax.ShapeDtypeStruct((s_len, 3 * DL_W), BF16),
        scratch_shapes=[pltpu.VMEM((s_len, LANES), F32)] * 3, compiler_params=_cp())(cur, prev, cos, sin)


XA_T = 256


def _xa_probs(q, k):
    s = _dot_nt(q, k) * (X_HD ** -0.5)
    e = jnp.exp(s - jnp.max(s, axis=1, keepdims=True))
    return e / jnp.sum(e, axis=1, keepdims=True)


def _xa_fwd(q, k, v, *, name):
    s_len, d = q.shape
    nm = k.shape[0]

    def body(q_ref, k_ref, v_ref, o_ref):
        for h in range(X_HEADS):
            cs = slice(h * X_HD, (h + 1) * X_HD)
            p = _xa_probs(q_ref[:, cs], k_ref[:, cs])
            o_ref[:, cs] = _dot(p.astype(BF16), v_ref[:, cs]).astype(BF16)

    row = pl.BlockSpec((XA_T, d), lambda i: (i, 0))
    full = pl.BlockSpec((nm, d), lambda i: (0, 0))
    return pl.pallas_call(body, grid=(s_len // XA_T,), name=name, in_specs=[row, full, full], out_specs=row,
                          out_shape=jax.ShapeDtypeStruct((s_len, d), BF16), compiler_params=_cp())(q, k, v)


def _xa_bwd(q, k, v, do, *, name, carry=None):
    s_len, d = q.shape
    nm = k.shape[0]

    def body(q_ref, k_ref, v_ref, do_ref, dq_ref, dk_ref, dv_ref):
        i = pl.program_id(0)
        for h in range(X_HEADS):
            cs = slice(h * X_HD, (h + 1) * X_HD)
            qh, kh, vh, doh = q_ref[:, cs], k_ref[:, cs], v_ref[:, cs], do_ref[:, cs]
            p = _xa_probs(qh, kh)
            dp = _dot_nt(doh, vh)
            ds = (p * (dp - jnp.sum(dp * p, axis=1, keepdims=True)) * (X_HD ** -0.5)).astype(BF16)
            dq_ref[:, cs] = _dot(ds, kh).astype(BF16)
            dkh, dvh = _dot_tn(ds, qh), _dot_tn(p.astype(BF16), doh)

            @pl.when(i == 0)
            def _(cs=cs, dkh=dkh, dvh=dvh):
                dk_ref[:, cs] = dkh
                dv_ref[:, cs] = dvh

            @pl.when(i > 0)
            def _(cs=cs, dkh=dkh, dvh=dvh):
                dk_ref[:, cs] += dkh
                dv_ref[:, cs] += dvh

    row = pl.BlockSpec((XA_T, d), lambda i: (i, 0))
    full = pl.BlockSpec((nm, d), lambda i: (0, 0))
    return _call(
        body, grid=(s_len // XA_T,), name=name, carry=carry, in_specs=[row, full, full, row],
        out_specs=[row, full, full],
        out_shape=[jax.ShapeDtypeStruct((s_len, d), BF16), jax.ShapeDtypeStruct((nm, d), F32),
                   jax.ShapeDtypeStruct((nm, d), F32)], args=(q, k, v, do))


FF_TM, FF_TN, FF_H = 512, 256, 8
GELU_K, GELU_C = 0.7978845608028654, 0.044715


FF_STRIP = 64


def _ff_conv(e_ref, w_ref, b_ref, rows, r0=0):
    return (w_ref[0:1, :] * e_ref[pl.ds(FF_H - 2 + r0, rows), :] + w_ref[1:2, :] * e_ref[pl.ds(FF_H - 1 + r0, rows), :]
            + w_ref[2:3, :] * e_ref[pl.ds(FF_H + r0, rows), :] + b_ref[...])


def _strips(total, size):
    return [(r0, min(size, total - r0)) for r0 in range(0, total, size)]


def _ff_gate_fwd(up, conv_w, conv_b, *, name, carry=None):
    s_len = up.shape[0]
    nj = D_FF // FF_TN

    def body(g_ref, v_ref, gp_ref, vp_ref, wg_ref, wv_ref, bg_ref, bv_ref, o_ref, eg, ev):
        i = pl.program_id(0)
        for e, cur, prev in ((eg, g_ref, gp_ref), (ev, v_ref, vp_ref)):
            e[0:FF_H, :] = jnp.where(i > 0, prev[...], 0.0)
            e[FF_H:, :] = cur[...]
        for r0, rows in _strips(FF_TM, FF_STRIP):
            gate = _ff_conv(eg, wg_ref, bg_ref, rows, r0)
            val = _ff_conv(ev, wv_ref, bv_ref, rows, r0)
            t = jnp.tanh(GELU_K * (gate + GELU_C * gate * gate * gate))
            o_ref[r0:r0 + rows, :] = (0.5 * gate * (1.0 + t) * val).astype(BF16)

    cur = lambda c0: pl.BlockSpec((FF_TM, FF_TN), lambda i, j: (i, c0 + j))
    prev = lambda c0: pl.BlockSpec((FF_H, FF_TN), lambda i, j: (jnp.maximum(i * (FF_TM // FF_H) - 1, 0), c0 + j))
    par = lambda r, c0: pl.BlockSpec((r, FF_TN), lambda i, j: (0, c0 + j))
    return _call(
        body, grid=(s_len // FF_TM, nj), name=name, carry=carry,
        in_specs=[cur(0), cur(nj), prev(0), prev(nj), par(3, 0), par(3, nj), par(1, 0), par(1, nj)],
        out_specs=[cur(0)], out_shape=[jax.ShapeDtypeStruct((s_len, D_FF), BF16)],
        scratch_shapes=[pltpu.VMEM((FF_TM + FF_H, FF_TN), F32)] * 2,
        args=(up, up, up, up, conv_w, conv_w, conv_b, conv_b))


def _ff_gate_bwd(up, dact, conv_w, conv_b, *, name, carry=None):
    s_len = up.shape[0]
    nj = D_FF // FF_TN
    last = s_len // FF_TM - 1
    ext = FF_TM + FF_H

    def body(g_ref, v_ref, gp_ref, vp_ref, gn_ref, vn_ref, da_ref, dan_ref, wg_ref, wv_ref, bg_ref, bv_ref,
             dg_ref, dv_ref, dw_ref, db_ref, eg, ev, sg, sv):
        i = pl.program_id(1)
        for e, cur, prev, nxt in ((eg, g_ref, gp_ref, gn_ref), (ev, v_ref, vp_ref, vn_ref)):
            e[0:FF_H, :] = jnp.where(i > 0, prev[...], 0.0)
            e[FF_H:FF_H + FF_TM, :] = cur[...]
            e[FF_H + FF_TM:, :] = nxt[...]
        for r0, rows in _strips(ext, FF_STRIP):
            gate = _ff_conv(eg, wg_ref, bg_ref, rows, r0)
            val = _ff_conv(ev, wv_ref, bv_ref, rows, r0)
            dact = da_ref[r0:r0 + rows, :] if r0 < FF_TM else jnp.where(i < last, dan_ref[...], 0.0)
            t = jnp.tanh(GELU_K * (gate + GELU_C * gate * gate * gate))
            half = 0.5 * (1.0 + t)
            dgelu = half + 0.5 * gate * (1.0 - t * t) * GELU_K * (1.0 + 3.0 * GELU_C * gate * gate)
            sg[r0:r0 + rows, :] = dact * val * dgelu
            sv[r0:r0 + rows, :] = dact * (gate * half)
        for part, (s, e, w_ref, out) in enumerate(((sg, eg, wg_ref, dg_ref), (sv, ev, wv_ref, dv_ref))):
            taps, bias = [jnp.zeros((8, FF_TN), F32)] * 3, jnp.zeros((8, FF_TN), F32)
            for r0, rows in _strips(FF_TM, FF_STRIP):
                d0 = s[pl.ds(r0, rows), :]
                out[r0:r0 + rows, :] = (w_ref[2:3, :] * d0 + w_ref[1:2, :] * s[pl.ds(r0 + 1, rows), :]
                                        + w_ref[0:1, :] * s[pl.ds(r0 + 2, rows), :]).astype(BF16)
                taps = [taps[k] + _rowsum8(d0 * e[pl.ds(FF_H - 2 + k + r0, rows), :]) for k in range(3)]
                bias = bias + _rowsum8(d0)
            _acc_out(dw_ref.at[part], i, jnp.concatenate(taps, axis=0))
            _acc_out(db_ref.at[part], i, bias)

    cur = lambda c0: pl.BlockSpec((FF_TM, FF_TN), lambda j, i: (i, c0 + j))
    prev = lambda c0: pl.BlockSpec((FF_H, FF_TN), lambda j, i: (jnp.maximum(i * (FF_TM // FF_H) - 1, 0), c0 + j))
    nxt = lambda c0: pl.BlockSpec(
        (FF_H, FF_TN), lambda j, i: (jnp.minimum((i + 1) * (FF_TM // FF_H), s_len // FF_H - 1), c0 + j))
    par = lambda r, c0: pl.BlockSpec((r, FF_TN), lambda j, i: (0, c0 + j))
    return _call(
        body, grid=(nj, s_len // FF_TM), name=name, carry=carry,
        in_specs=[cur(0), cur(nj), prev(0), prev(nj), nxt(0), nxt(nj), cur(0), nxt(0),
                  par(3, 0), par(3, nj), par(1, 0), par(1, nj)],
        out_specs=[cur(0), cur(0), pl.BlockSpec((2, 24, FF_TN), lambda j, i: (0, 0, j)),
                   pl.BlockSpec((2, 8, FF_TN), lambda j, i: (0, 0, j))],
        out_shape=[jax.ShapeDtypeStruct((s_len, D_FF), BF16), jax.ShapeDtypeStruct((s_len, D_FF), BF16),
                   jax.ShapeDtypeStruct((2, 24, D_FF), F32), jax.ShapeDtypeStruct((2, 8, D_FF), F32)],
        scratch_shapes=[pltpu.VMEM((FF_TM + 2 * FF_H, FF_TN), F32)] * 2 + [pltpu.VMEM((ext, FF_TN), F32)] * 2,
        args=(up, up, up, up, up, up, dact, dact, conv_w, conv_w, conv_b, conv_b))


def _place():
    x, y, c = lax.axis_index("x"), lax.axis_index("y"), lax.axis_index("c")
    return x, y, c, [(1 - x, y), (x, 1 - y), (1 - x, 1 - y)]


def _remote(src, dst, send_sem, recv_sem, dev):
    return pltpu.make_async_remote_copy(src_ref=src, dst_ref=dst, send_sem=send_sem, recv_sem=recv_sem,
                                        device_id=dev, device_id_type=MESH)


_ANY = pl.BlockSpec(memory_space=pl.ANY)


N_SEMS = 8
SEM_BASE_2 = 4


class _Exchange:
    def __init__(self, operands, out_shapes, start, wait, aliases=None):
        self.operands, self.out_shapes, self.start, self.wait = list(operands), list(out_shapes), start, wait
        self.aliases = aliases or {}


def _sem_scratch():
    return [pltpu.SemaphoreType.DMA((N_SEMS,)), pltpu.SemaphoreType.DMA((N_SEMS,)), pltpu.SemaphoreType.DMA]


def _run_exchange(ex, *, name):
    k, n = len(ex.operands), len(ex.out_shapes)

    def body(*refs):
        ins, outs, sems = refs[:k], refs[k:k + n], refs[k + n:]
        ex.start(ins, outs, *sems)
        ex.wait(ins, outs, *sems)

    return pl.pallas_call(body, name=name, in_specs=[_ANY] * k, out_specs=[_ANY] * n, out_shape=ex.out_shapes,
                          scratch_shapes=_sem_scratch(), input_output_aliases=ex.aliases,
                          compiler_params=_cp(16))(*ex.operands)


def _call(body, *, grid, in_specs, out_specs, out_shape, args, name, scratch_shapes=(), vmem_mb=48, carry=None):
    scratch_shapes = list(scratch_shapes)
    if carry is None:
        return pl.pallas_call(body, grid=grid, name=name, in_specs=in_specs, out_specs=out_specs, out_shape=out_shape,
                              scratch_shapes=scratch_shapes, compiler_params=_cp(vmem_mb))(*args)
    n_in, n_out, n_scr = len(in_specs), len(out_shape), len(scratch_shapes)
    k_in, k_out = len(carry.operands), len(carry.out_shapes)

    def wrapped(*refs):
        ins, refs = refs[:n_in], refs[n_in:]
        cin, refs = refs[:k_in], refs[k_in:]
        outs, refs = refs[:n_out], refs[n_out:]
        cout, refs = refs[:k_out], refs[k_out:]
        scratch, sems = refs[:n_scr], refs[n_scr:]
        ids = [pl.program_id(a) for a in range(len(grid))]
        first = functools.reduce(jnp.logical_and, [i == 0 for i in ids])
        last = functools.reduce(jnp.logical_and, [i == g - 1 for i, g in zip(ids, grid)])

        @pl.when(first)
        def _():
            carry.start(cin, cout, *sems)

        body(*ins, *outs, *scratch)

        @pl.when(last)
        def _():
            carry.wait(cin, cout, *sems)

    aliases = {n_in + i: n_out + o for i, o in carry.aliases.items()}
    return pl.pallas_call(
        wrapped, grid=grid, name=name, in_specs=list(in_specs) + [_ANY] * k_in,
        out_specs=list(out_specs) + [_ANY] * k_out, out_shape=list(out_shape) + carry.out_shapes,
        scratch_shapes=scratch_shapes + _sem_scratch(), input_output_aliases=aliases,
        compiler_params=_cp(vmem_mb))(*args, *carry.operands)


def _half_rows(ref_rows, c):
    half = ref_rows // 2
    return pl.ds(c * half, half)


def _ex_join(a, b):
    ka, na = len(a.operands), len(a.out_shapes)

    def start(ins, outs, *sems):
        a.start(ins[:ka], outs[:na], *sems)
        b.start(ins[ka:], outs[na:], *sems)

    def wait(ins, outs, *sems):
        a.wait(ins[:ka], outs[:na], *sems)
        b.wait(ins[ka:], outs[na:], *sems)

    aliases = dict(a.aliases)
    aliases.update({ka + i: na + o for i, o in b.aliases.items()})
    return _Exchange(a.operands + b.operands, a.out_shapes + b.out_shapes, start, wait, aliases)


def _ex_gather(pack, r0, rl, base=0):
    def copies(ins, outs, send, recv):
        x, y, c, chips = _place()
        rows = _half_rows(rl, c)
        src = ins[0].at[pl.ds(r0 + c * (rl // 2), rl // 2)]
        sends = [_remote(src, outs[0].at[2 * x + y, rows], send.at[base + k], recv.at[base + k], (px, py, c))
                 for k, (px, py) in enumerate(chips)]
        lands = [_remote(src, outs[0].at[2 * px + py, rows], send.at[base + k], recv.at[base + k], (px, py, c))
                 for k, (px, py) in enumerate(chips)]
        return sends, lands

    def mine(ins, outs, local):
        x, y, _, _ = _place()
        return pltpu.make_async_copy(ins[0].at[pl.ds(r0, rl)], outs[0].at[2 * x + y], local)

    def start(ins, outs, send, recv, local):
        mine(ins, outs, local).start()
        for cp in copies(ins, outs, send, recv)[0]:
            cp.start()

    def wait(ins, outs, send, recv, local):
        sends, lands = copies(ins, outs, send, recv)
        for cp in lands:
            cp.wait_recv()
        for cp in sends:
            cp.wait_send()
        mine(ins, outs, local).wait()

    return _Exchange([pack], [jax.ShapeDtypeStruct((4, rl, pack.shape[1]), pack.dtype)], start, wait)


def _ex_gather_forward(g, base=0):
    rl = g.shape[1]

    def copies(outs, send, recv):
        x, y, c, chips = _place()
        slabs = [(outs[0].at[2 * px + py, _half_rows(rl, c)], outs[0].at[2 * px + py, _half_rows(rl, 1 - c)])
                 for px, py in chips]
        sends = [_remote(a, a, send.at[base + k], recv.at[base + k], (x, y, 1 - c)) for k, (a, _) in enumerate(slabs)]
        lands = [_remote(b, b, send.at[base + k], recv.at[base + k], (x, y, 1 - c)) for k, (_, b) in enumerate(slabs)]
        return sends, lands

    def start(ins, outs, send, recv, local):
        for cp in copies(outs, send, recv)[0]:
            cp.start()

    def wait(ins, outs, send, recv, local):
        sends, lands = copies(outs, send, recv)
        for cp in lands:
            cp.wait_recv()
        for cp in sends:
            cp.wait_send()

    return _Exchange([g], [jax.ShapeDtypeStruct(g.shape, g.dtype)], start, wait, aliases={0: 0})


def _ex_swap_halves(gw, base=0):
    nb, rl, d = gw.shape

    def copies(ins, outs, send, recv):
        x, y, c, _ = _place()
        return [_remote(ins[0].at[j, _half_rows(rl, 1 - c)], outs[0].at[j], send.at[base + j], recv.at[base + j],
                        (x, y, 1 - c)) for j in range(nb)]

    def start(ins, outs, send, recv, local):
        for cp in copies(ins, outs, send, recv):
            cp.start()

    def wait(ins, outs, send, recv, local):
        for cp in copies(ins, outs, send, recv):
            cp.wait()

    return _Exchange([gw], [jax.ShapeDtypeStruct((nb, rl // 2, d), gw.dtype)], start, wait)


def _chip_sum(gw, got, c_arr, *, name):
    nchip, half, d = got.shape
    tr = _tile(half, 512)

    def body(c_ref, a_ref, b_ref, o32_ref, o16_ref):
        s = a_ref[...] + b_ref[...]
        o32_ref[...] = s
        o16_ref[...] = s.astype(BF16)

    blk = pl.BlockSpec((None, tr, d), lambda j, i, c_ref: (j, i, 0))
    return pl.pallas_call(
        body, name=name,
        grid_spec=pltpu.PrefetchScalarGridSpec(
            num_scalar_prefetch=1, grid=(nchip, half // tr),
            in_specs=[pl.BlockSpec((None, tr, d), lambda j, i, c_ref: (j, c_ref[0] * (half // tr) + i, 0)), blk],
            out_specs=[blk, blk]),
        out_shape=[jax.ShapeDtypeStruct((nchip, half, d), F32), jax.ShapeDtypeStruct((nchip, half, d), BF16)],
        compiler_params=_cp())(c_arr, gw, got)


def _ex_scatter(s16, base=0):
    def copies(ins, outs, send, recv):
        x, y, c, chips = _place()
        return [_remote(ins[0].at[2 * px + py], outs[0].at[k], send.at[base + k], recv.at[base + k], (px, py, c))
                for k, (px, py) in enumerate(chips)]

    def start(ins, outs, send, recv, local):
        for cp in copies(ins, outs, send, recv):
            cp.start()

    def wait(ins, outs, send, recv, local):
        for cp in copies(ins, outs, send, recv):
            cp.wait()

    return _Exchange([s16], [jax.ShapeDtypeStruct((3,) + s16.shape[1:], s16.dtype)], start, wait)


def _mesh_sum(s32, got, j_arr, *, name):
    _, rl, d = s32.shape
    tr = _tile(rl, 512)

    def body(j_ref, a_ref, b_ref, o_ref):
        o_ref[...] = ((a_ref[...] + b_ref[0].astype(F32)) + b_ref[1].astype(F32)) + b_ref[2].astype(F32)

    return pl.pallas_call(
        body, name=name,
        grid_spec=pltpu.PrefetchScalarGridSpec(
            num_scalar_prefetch=1, grid=(rl // tr,),
            in_specs=[pl.BlockSpec((None, tr, d), lambda i, j_ref: (j_ref[0], i, 0)),
                      pl.BlockSpec((3, tr, d), lambda i, j_ref: (0, i, 0))],
            out_specs=pl.BlockSpec((tr, d), lambda i, j_ref: (i, 0))),
        out_shape=jax.ShapeDtypeStruct((rl, d), F32), compiler_params=_cp())(j_arr, s32, got)


def _ex_share_halves(ghalf):
    half, d = ghalf.shape

    def copies(ins, outs, send, recv, local):
        x, y, c, _ = _place()
        there = outs[0].at[_half_rows(2 * half, c)]
        back = outs[0].at[_half_rows(2 * half, 1 - c)]
        return (_remote(ins[0], there, send.at[0], recv.at[0], (x, y, 1 - c)),
                _remote(ins[0], back, send.at[0], recv.at[0], (x, y, 1 - c)), pltpu.make_async_copy(ins[0], there, local))

    def start(ins, outs, send, recv, local):
        out, _, mine = copies(ins, outs, send, recv, local)
        mine.start()
        out.start()

    def wait(ins, outs, send, recv, local):
        out, back, mine = copies(ins, outs, send, recv, local)
        back.wait_recv()
        out.wait_send()
        mine.wait()

    return _Exchange([ghalf], [jax.ShapeDtypeStruct((2 * half, d), ghalf.dtype)], start, wait)


class _ReduceScatter:
    def __init__(self, gw, c_arr, j_arr, tag):
        self.gw, self.c_arr, self.j_arr, self.tag = gw, c_arr, j_arr, tag

    def swap(self, base=0):
        return _ex_swap_halves(self.gw, base)

    def after_swap(self, got, base=0):
        self.s32, s16 = _chip_sum(self.gw, got, self.c_arr, name=f"rs_chip_sum{self.tag}")
        return _ex_scatter(s16, base)

    def after_scatter(self, got16):
        ghalf = _mesh_sum(self.s32, got16, self.j_arr, name=f"rs_mesh_sum{self.tag}")
        return _run_exchange(_ex_share_halves(ghalf), name=f"rs_share{self.tag}")[0]

    def run(self):
        got, = _run_exchange(self.swap(), name=f"rs_swap{self.tag}")
        got16, = _run_exchange(self.after_swap(got), name=f"rs_scatter{self.tag}")
        return self.after_scatter(got16)


def _all_reduce_small(vec, *, name):
    rows, d = vec.shape

    def body(x_ref, o_ref, gat, send_sems, recv_sems, local_sem):
        x, y, c, chips = _place()
        me, sibling = (x, y, c), (x, y, 1 - c)

        def slot(px, py, pc):
            return gat.at[4 * px + 2 * py + pc]

        def copy(k, block, to, src=None):
            return _remote(slot(*block) if src is None else src, slot(*block), send_sems.at[k], recv_sems.at[k], to)

        mine = pltpu.make_async_copy(x_ref, slot(*me), local_sem)
        mine.start()
        first = [copy(0, me, sibling, src=x_ref)]
        first += [copy(1 + j, me, (*chip, c), src=x_ref) for j, chip in enumerate(chips)]
        for cp in first:
            cp.start()
        passed = [copy(4 + j, (*chip, c), sibling) for j, chip in enumerate(chips)]
        for j, chip in enumerate(chips):
            copy(1 + j, (*chip, c), me).wait_recv()
            passed[j].start()
        copy(0, sibling, me).wait_recv()
        for j, chip in enumerate(chips):
            copy(4 + j, (*chip, 1 - c), me).wait_recv()
        for cp in first + passed:
            cp.wait_send()
        mine.wait()
        acc = gat[0]
        for dev in range(1, 8):
            acc = acc + gat[dev]
        o_ref[...] = acc

    vm = pl.BlockSpec(memory_space=pltpu.VMEM)
    return pl.pallas_call(
        body, name=name, in_specs=[vm], out_specs=vm, out_shape=jax.ShapeDtypeStruct((rows, d), F32),
        scratch_shapes=[pltpu.VMEM((8, rows, d), F32), pltpu.SemaphoreType.DMA((7,)), pltpu.SemaphoreType.DMA((7,)),
                        pltpu.SemaphoreType.DMA],
        compiler_params=_cp(32))(vec)


COL_SHARDED = ("w_in", "ffn_w_up")


def _to_pack_rows(name, shard):
    return shard.reshape(-1, D_MODEL)


def _full_from_blocks(name, blocks):
    rows = blocks.shape[1]
    if name in COL_SHARDED:
        return blocks.reshape(4, D_MODEL, rows).transpose(1, 0, 2).reshape(D_MODEL, 4 * rows)
    return blocks.reshape(4 * rows, D_MODEL)


def _blocks_from_full(name, full):
    if name in COL_SHARDED:
        cols = full.shape[1] // 4
        return full.reshape(D_MODEL, 4, cols).transpose(1, 0, 2).reshape(4, cols, D_MODEL)
    return full.reshape(4, full.shape[0] // 4, D_MODEL)


def _row(v):
    return v.reshape(1, -1)


SMALL = (("mix_norm_pre", (1024,), None), ("cv_w", (31, 256), 1), ("cv_b", (256,), None), ("cv_ln_g", (256,), None),
         ("cv_ln_b", (256,), None), ("cv_pw_w", (256, 256), 0), ("cv_pw_b", (256,), None),
         ("mix_norm_post", (1024,), None), ("x_norm_pre", (1024,), None), ("mem_norm", (1024,), None),
         ("x_norm_post", (1024,), None), ("ffn_norm_pre", (1024,), None), ("ffn_conv_w", (3, 5632), 1),
         ("ffn_conv_b", (5632,), None), ("ffn_norm_post", (1024,), None))
BIG = tuple(n for n, _ in PACK_ROWS)
WEIGHT_ORDER = ("mix_norm_pre", "w_in", "cv_w", "cv_b", "cv_ln_g", "cv_ln_b", "cv_pw_w", "cv_pw_b", "w_out",
                "mix_norm_post", "x_norm_pre", "mem_norm", "x_wq", "x_wk", "x_wv", "x_wo", "x_norm_post",
                "ffn_norm_pre", "ffn_w_up", "ffn_conv_w", "ffn_conv_b", "ffn_w_down", "ffn_norm_post")


def _flat_rows(parts):
    v = jnp.concatenate([p.reshape(-1) for p in parts])
    rows = -(-v.shape[0] // (8 * D_MODEL)) * 8
    return jnp.pad(v, (0, rows * D_MODEL - v.shape[0])).reshape(rows, D_MODEL)


REST_GROUP = ("w_in", "w_out", "x_wq", "x_wk", "x_wv", "x_wo")
FFN_GROUP = ("ffn_w_up", "ffn_w_down")


class _Weights:
    FIRST = (0, 704)
    OWN = ((704, 768), (1472, 2624))
    NEXT = ((0, 960), (960, 1024), (1984, 1408), (3392, 704))
    SLOTS = ("mix_in", "sb_fwd", "dl_fwd", "ffn_up", "ffn_gate", "ffn_down")

    def __init__(self, packs):
        self.packs, self.pieces, self.landed, self.plan = packs, {}, None, {}
        for slot, piece in zip(self.SLOTS[:2], self.OWN):
            self.plan[(0, slot)] = (0,) + piece
        for l in range(len(packs) - 1):
            for slot, piece in zip(self.SLOTS[2:], self.NEXT):
                self.plan[(l, slot)] = (l + 1,) + piece
        first = _run_exchange(_ex_gather(packs[0], *self.FIRST), name="gather_first")[0]
        self.pieces[(0,) + self.FIRST] = _run_exchange(_ex_gather_forward(first), name="gather_first_forward")[0]

    def ride(self, layer, slot, call):
        start, todo, ex = self.plan.get((layer, slot)), [], None
        if start is not None:
            ex = _ex_gather(self.packs[start[0]], start[1], start[2])
            todo.append(("landed", start))
        if self.landed is not None:
            key, buf = self.landed
            forward = _ex_gather_forward(buf, SEM_BASE_2 if ex is not None else 0)
            ex = forward if ex is None else _ex_join(ex, forward)
            todo.append(("piece", key))
            self.landed = None
        outs = list(call(carry=ex))
        n = len(outs) - len(todo)
        for (kind, key), buf in zip(todo, outs[n:]):
            if kind == "landed":
                self.landed = (key, buf)
            else:
                self.pieces[key] = buf
        return outs[:n]

    def weight(self, layer, name):
        off = 0
        for n, rows in PACK_ROWS:
            if n == name:
                break
            off += rows
        for (l, r0, nrows), buf in self.pieces.items():
            if l == layer and r0 <= off < r0 + nrows:
                return _full_from_blocks(name, buf[:, off - r0:off - r0 + rows, :])
        raise KeyError(f"{name} of layer {layer} is not gathered yet")


class _Params:
    def __init__(self, weights, layer, small):
        self.weights, self.layer, self.small, self.cache = weights, layer, small, {}

    def __getitem__(self, name):
        if name in self.small:
            return self.small[name]
        if name not in self.cache:
            self.cache[name] = self.weights.weight(self.layer, name)
        return self.cache[name]


def _layer_fwd(h0, mem, p, cos, sin, tag, ride):
    sv = {"h0": h0}
    n1, u = ride("mix_in", functools.partial(_rms_mm, h0, _row(p["mix_norm_pre"]), p["w_in"], tm=1024, tn=1408,
                                             out_dtype=F32, name=f"mix_in{tag}"))
    a_out, = ride("sb_fwd", functools.partial(_sb_fwd, u, name=f"sb_fwd{tag}"))
    b_out, c = _cv_fwd(u, p["cv_w"], _row(p["cv_b"]), _row(p["cv_ln_g"]), _row(p["cv_ln_b"]),
                       p["cv_pw_w"].astype(BF16), _row(p["cv_pw_b"]), name=f"cv_fwd{tag}")
    qkv = _rope_perm(u, cos, sin, name=f"rope_perm{tag}")
    o_p, l_p = ride("dl_fwd", functools.partial(_dl_fwd, qkv, name=f"dl_fwd{tag}"))
    c_out, o_dl, lse = _dl_mix(o_p, l_p, name=f"dl_mix{tag}")
    cat = jnp.concatenate([a_out, b_out, c_out], axis=1)
    y1, h1 = _mm_post(cat, p["w_out"], h0, _row(p["mix_norm_post"]), tm=256, name=f"mix_out{tag}")
    sv.update(n1=n1, u=u, c=c, qkv=qkv, o_dl=o_dl, lse=lse, cat=cat, y1=y1, h1=h1)

    n2, q = _rms_mm(h1, _row(p["x_norm_pre"]), p["x_wq"], tm=512, tn=1024, out_dtype=BF16, name=f"xa_q{tag}")
    wkv = jnp.concatenate([p["x_wk"], p["x_wv"]], axis=1)
    mem_n, kv = _rms_mm(mem, _row(p["mem_norm"]), wkv, tm=mem.shape[0], tn=1024, out_dtype=BF16, name=f"xa_kv{tag}")
    k, v = kv[:, :D_MODEL], kv[:, D_MODEL:]
    o_x = _xa_fwd(q, k, v, name=f"xa_fwd{tag}")
    y2, h2 = _mm_post(o_x, p["x_wo"], h1, _row(p["x_norm_post"]), tm=256, name=f"xa_out{tag}")
    sv.update(n2=n2, q=q, mem_n=mem_n, k=k, v=v, o_x=o_x, y2=y2, h2=h2, wkv=wkv)

    n3, up = ride("ffn_up", functools.partial(_rms_mm, h2, _row(p["ffn_norm_pre"]), p["ffn_w_up"], tm=1024, tn=1408,
                                              out_dtype=F32, name=f"ffn_up{tag}"))
    act, = ride("ffn_gate", functools.partial(_ff_gate_fwd, up, p["ffn_conv_w"], _row(p["ffn_conv_b"]),
                                              name=f"ffn_gate{tag}"))
    y3, h3 = ride("ffn_down", functools.partial(_mm_post, act, p["ffn_w_down"], h2, _row(p["ffn_norm_post"]), tm=256,
                                                name=f"ffn_down{tag}"))
    sv.update(n3=n3, up=up, act=act, y3=y3)
    return h3, sv


def _layer_bwd(dh3, mem, p, sv, cos, sin, tag, riding, new_rs):
    g = {}
    s8 = lambda part: part.sum(axis=0)
    rode = None

    dy3, dgp = _rms_bwd(sv["y3"], _row(p["ffn_norm_post"]), dh3, None, out_dtype=BF16, tm=256, name=f"ffn_post_b{tag}")
    g["ffn_norm_post"] = s8(dgp)
    dact = _mm_nt(dy3, p["ffn_w_down"], tm=512, tn=1408, out_dtype=F32, name=f"ffn_down_bx{tag}")
    g["ffn_w_down"] = _mm_tn(sv["act"], dy3, tk=1408, tn=1024, tm=512, name=f"ffn_down_bw{tag}")
    dgu, dvu, dcw, dcb, *got = _ff_gate_bwd(sv["up"], dact, p["ffn_conv_w"], _row(p["ffn_conv_b"]),
                                            name=f"ffn_gate_b{tag}", carry=riding.swap() if riding else None)
    scatter = riding.after_swap(got[0]) if riding else None
    g["ffn_conv_w"] = jnp.concatenate([dcw[0], dcw[1]], axis=1).reshape(3, 8, 2 * D_FF).sum(axis=1)
    g["ffn_conv_b"] = jnp.concatenate([dcb[0], dcb[1]], axis=1).sum(axis=0)
    dup = jnp.concatenate([dgu, dvu], axis=1)
    dn3 = _mm_nt(dup, p["ffn_w_up"], tm=256, tn=512, out_dtype=F32, name=f"ffn_up_bx{tag}")
    g["ffn_w_up"] = _mm_tn(sv["n3"], dup, tk=512, tn=1408, tm=512, name=f"ffn_up_bw{tag}")
    ffn_rs = new_rs(FFN_GROUP, g, f"{tag}_ffn")
    dh2, dgp = _rms_bwd(sv["h2"], _row(p["ffn_norm_pre"]), dn3, dh3, out_dtype=F32, tm=256, name=f"ffn_pre_b{tag}")
    g["ffn_norm_pre"] = s8(dgp)

    dy2, dgp = _rms_bwd(sv["y2"], _row(p["x_norm_post"]), dh2, None, out_dtype=BF16, tm=256, name=f"xa_post_b{tag}")
    g["x_norm_post"] = s8(dgp)
    do_x = _mm_nt(dy2, p["x_wo"], tm=512, tn=1024, out_dtype=BF16, name=f"xa_out_bx{tag}")
    g["x_wo"] = _mm_tn(sv["o_x"], dy2, tk=512, tn=1024, tm=512, name=f"xa_out_bw{tag}")
    dq, dk, dv, got = _xa_bwd(sv["q"], sv["k"], sv["v"], do_x, name=f"xa_bwd{tag}", carry=ffn_rs.swap())
    ffn_scatter = ffn_rs.after_swap(got)
    dn2 = _mm_nt(dq, p["x_wq"], tm=512, tn=1024, out_dtype=F32, name=f"xa_q_bx{tag}")
    g["x_wq"] = _mm_tn(sv["n2"], dq, tk=512, tn=1024, tm=512, name=f"xa_q_bw{tag}")
    dkv = jnp.concatenate([dk, dv], axis=1).astype(BF16)
    nm = mem.shape[0]
    dmem_n = _mm_nt(dkv, sv["wkv"], tm=nm, tn=1024, out_dtype=F32, name=f"xa_kv_bx{tag}")
    dwkv = _mm_tn(sv["mem_n"], dkv, tk=512, tn=2048, tm=nm, name=f"xa_kv_bw{tag}")
    g["x_wk"], g["x_wv"] = dwkv[:, :D_MODEL], dwkv[:, D_MODEL:]
    _, dgp = _rms_bwd(mem, _row(p["mem_norm"]), dmem_n, None, out_dtype=BF16, tm=nm, name=f"xa_mem_b{tag}")
    g["mem_norm"] = s8(dgp)
    dh1, dgp = _rms_bwd(sv["h1"], _row(p["x_norm_pre"]), dn2, dh2, out_dtype=F32, tm=256, name=f"xa_pre_b{tag}")
    g["x_norm_pre"] = s8(dgp)

    dy1, dgp = _rms_bwd(sv["y1"], _row(p["mix_norm_post"]), dh1, None, out_dtype=BF16, tm=256, name=f"mix_post_b{tag}")
    g["mix_norm_post"] = s8(dgp)
    dcat = _mm_nt(dy1, p["w_out"], tm=512, tn=1024, out_dtype=F32, name=f"mix_out_bx{tag}")
    g["w_out"] = _mm_tn(sv["cat"], dy1, tk=512, tn=1024, tm=512, name=f"mix_out_bw{tag}")
    u = sv["u"]
    dq_sb, dk_sb, dv_sb, *got = _sb_bwd(u, dcat, name=f"sb_bwd{tag}", carry=scatter)
    if riding:
        rode = riding.after_scatter(got[0])
    pw_b16 = p["cv_pw_w"].astype(BF16)
    dc, dpw, vec = _cv_bwd_local(sv["c"], dcat, _row(p["cv_ln_g"]), _row(p["cv_ln_b"]), pw_b16, name=f"cv_bwd_a{tag}")
    g["cv_pw_w"] = dpw
    vec = vec.reshape(3, 8, CV_W).sum(axis=1)
    g["cv_pw_b"], g["cv_ln_g"], g["cv_ln_b"] = vec[0], vec[1], vec[2]
    du_cv, dcw, dcb = _cv_bwd_conv(u, dc, p["cv_w"], name=f"cv_bwd_b{tag}")
    g["cv_w"] = dcw.reshape(CV_K, 8, CV_W).sum(axis=1)
    g["cv_b"] = dcb.sum(axis=0)
    dop, stats = _dl_bwd_prep(dcat, sv["o_dl"], sv["lse"], name=f"dl_prep_b{tag}")
    cur, prev, got = _dl_bwd(sv["qkv"], dop, stats, name=f"dl_bwd{tag}", carry=ffn_scatter)
    ffn_rows = ffn_rs.after_scatter(got)
    du_dl = _dl_bwd_finish(cur, prev, cos, sin, name=f"dl_fin_b{tag}")
    du = jnp.concatenate([dq_sb.astype(BF16), dk_sb.astype(BF16), dv_sb.astype(BF16), du_cv, du_dl], axis=1)
    dn1 = _mm_nt(du, p["w_in"], tm=512, tn=512, out_dtype=F32, name=f"mix_in_bx{tag}")
    g["w_in"] = _mm_tn(sv["n1"], du, tk=512, tn=1408, tm=512, name=f"mix_in_bw{tag}")
    dh0, dgp = _rms_bwd(sv["h0"], _row(p["mix_norm_pre"]), dn1, dh1, out_dtype=F32, tm=256, name=f"mix_pre_b{tag}")
    g["mix_norm_pre"] = s8(dgp)
    return dh0, g, ffn_rows, rode


def _step(x, mem, positions, loss_target, w, m, v):
    depth = w["w_in"].shape[0]
    xi, yi, ci = lax.axis_index("x"), lax.axis_index("y"), lax.axis_index("c")
    chip = 2 * xi + yi
    h = x[0]
    mem0 = mem[0]
    s_len = h.shape[0]

    packs = [jnp.concatenate([_to_pack_rows(n, w[n][l]) for n in BIG], axis=0).astype(BF16) for l in range(depth)]

    small_w = []
    for l in range(depth):
        for n, shape, axis in SMALL:
            if axis is not None:
                full = jnp.zeros(shape, F32)
                full = lax.dynamic_update_slice_in_dim(full, w[n][l], chip * w[n][l].shape[axis], axis)
                small_w.append(full * jnp.where(ci == 0, 1.0, 0.0))
    small_w_sum = _all_reduce_small(_flat_rows(small_w), name="gather_small_weights")
    small_full, off = [{} for _ in range(depth)], 0
    for l in range(depth):
        for n, shape, axis in SMALL:
            if axis is not None:
                size = int(np.prod(shape))
                small_full[l][n] = small_w_sum.reshape(-1)[off:off + size].reshape(shape)
                off += size
    weights = _Weights(packs)
    params = [_Params(weights, l, {n: small_full[l].get(n, w[n][l]) for n, _, _ in SMALL}) for l in range(depth)]

    inv_freq = ROPE_THETA ** (-jnp.arange(HD // 2, dtype=F32) / (HD // 2))
    cos, sin = _rope_tables(positions.reshape(s_len, 1), jnp.tile(inv_freq, 4).reshape(1, LANES), name="rope_tables")

    saved = []
    for l in range(depth):
        h, sv = _layer_fwd(h, mem0, params[l], cos, sin, f"_l{l}", functools.partial(weights.ride, l))
        saved.append(sv)
    dh, sq = _loss_grad(h, loss_target[0], tm=256, name="loss_grad")
    loss = lax.psum(0.5 * jnp.sum(sq) / D_MODEL, ("x", "y", "c"))

    c_arr, j_arr = jnp.reshape(ci, (1,)).astype(jnp.int32), jnp.reshape(chip, (1,)).astype(jnp.int32)

    def new_rs(names, g, tag):
        gw = jnp.concatenate([_blocks_from_full(n, g[n]) for n in names], axis=1)
        return _ReduceScatter(gw, c_arr, j_arr, tag)

    grads, ffn_rows, rest_rows, pending = [None] * depth, [None] * depth, [None] * depth, None
    for l in reversed(range(depth)):
        dh, grads[l], ffn_rows[l], rode = _layer_bwd(dh, mem0, params[l], saved[l], cos, sin, f"_l{l}", pending, new_rs)
        if pending is not None:
            rest_rows[l + 1] = rode
        pending = new_rs(REST_GROUP, grads[l], f"_l{l}_rest")
    rest_rows[0] = pending.run()
    gfull = [jnp.concatenate([rest_rows[l], ffn_rows[l]], axis=0) for l in range(depth)]
    grad_x = dh[None]

    out_g, out_d, out_m, out_v = {}, {}, {}, {}
    off = 0
    for n, rows in PACK_ROWS:
        shard_shape = w[n].shape
        g_n = jnp.stack([gl[off:off + rows, :] for gl in gfull]).reshape(shard_shape)
        off += rows
        flat = lambda a: a.reshape(-1, shard_shape[-1])
        d_n, m_n, v_n = _adamw(flat(w[n]), flat(g_n), flat(m[n]), flat(v[n]), name=f"adamw_{n}")
        out_g[n], out_d[n], out_m[n], out_v[n] = g_n, d_n.reshape(shard_shape), m_n.reshape(shard_shape), v_n.reshape(shard_shape)

    g_small = _all_reduce_small(_flat_rows([grads[l][n] for l in range(depth) for n, _, _ in SMALL]),
                                name="all_reduce_small_grads").reshape(-1)
    local_g, off = {}, 0
    for l in range(depth):
        for n, shape, axis in SMALL:
            size = int(np.prod(shape))
            full = g_small[off:off + size].reshape(shape)
            off += size
            if axis is not None:
                blk = w[n].shape[1 + axis]
                full = lax.dynamic_slice_in_dim(full, chip * blk, blk, axis)
            local_g.setdefault(n, []).append(full)
    names = [n for n, _, _ in SMALL]
    g_loc = {n: jnp.stack(local_g[n]) for n in names}
    d_s, m_s, v_s = _adamw(_flat_rows([w[n] for n in names]), _flat_rows([g_loc[n] for n in names]),
                           _flat_rows([m[n] for n in names]), _flat_rows([v[n] for n in names]), name="adamw_small")
    off = 0
    for n in names:
        size = int(np.prod(w[n].shape))
        take = lambda a: a.reshape(-1)[off:off + size].reshape(w[n].shape)
        out_g[n], out_d[n], out_m[n], out_v[n] = g_loc[n], take(d_s), take(m_s), take(v_s)
        off += size

    outs = [loss, grad_x]
    for group in (out_g, out_d, out_m, out_v):
        outs += [group[n] for n in WEIGHT_ORDER]
    return tuple(outs)


def kernel(x, mem, positions, mix_norm_pre, w_in, cv_w, cv_b, cv_ln_g, cv_ln_b, cv_pw_w, cv_pw_b, w_out, mix_norm_post, x_norm_pre, mem_norm, x_wq, x_wk, x_wv, x_wo, x_norm_post, ffn_norm_pre, ffn_w_up, ffn_conv_w, ffn_conv_b, ffn_w_down, ffn_norm_post, loss_target, m_mix_norm_pre, m_w_in, m_cv_w, m_cv_b, m_cv_ln_g, m_cv_ln_b, m_cv_pw_w, m_cv_pw_b, m_w_out, m_mix_norm_post, m_x_norm_pre, m_mem_norm, m_x_wq, m_x_wk, m_x_wv, m_x_wo, m_x_norm_post, m_ffn_norm_pre, m_ffn_w_up, m_ffn_conv_w, m_ffn_conv_b, m_ffn_w_down, m_ffn_norm_post, v_mix_norm_pre, v_w_in, v_cv_w, v_cv_b, v_cv_ln_g, v_cv_ln_b, v_cv_pw_w, v_cv_pw_b, v_w_out, v_mix_norm_post, v_x_norm_pre, v_mem_norm, v_x_wq, v_x_wk, v_x_wv, v_x_wo, v_x_norm_post, v_ffn_norm_pre, v_ffn_w_up, v_ffn_conv_w, v_ffn_conv_b, v_ffn_w_down, v_ffn_norm_post):
    w = dict(zip(WEIGHT_ORDER, (mix_norm_pre, w_in, cv_w, cv_b, cv_ln_g, cv_ln_b, cv_pw_w, cv_pw_b, w_out, mix_norm_post, x_norm_pre, mem_norm, x_wq, x_wk, x_wv, x_wo, x_norm_post, ffn_norm_pre, ffn_w_up, ffn_conv_w, ffn_conv_b, ffn_w_down, ffn_norm_post)))
    m = dict(zip(WEIGHT_ORDER, (m_mix_norm_pre, m_w_in, m_cv_w, m_cv_b, m_cv_ln_g, m_cv_ln_b, m_cv_pw_w, m_cv_pw_b, m_w_out, m_mix_norm_post, m_x_norm_pre, m_mem_norm, m_x_wq, m_x_wk, m_x_wv, m_x_wo, m_x_norm_post, m_ffn_norm_pre, m_ffn_w_up, m_ffn_conv_w, m_ffn_conv_b, m_ffn_w_down, m_ffn_norm_post)))
    v = dict(zip(WEIGHT_ORDER, (v_mix_norm_pre, v_w_in, v_cv_w, v_cv_b, v_cv_ln_g, v_cv_ln_b, v_cv_pw_w, v_cv_pw_b, v_w_out, v_mix_norm_post, v_x_norm_pre, v_mem_norm, v_x_wq, v_x_wk, v_x_wv, v_x_wo, v_x_norm_post, v_ffn_norm_pre, v_ffn_w_up, v_ffn_conv_w, v_ffn_conv_b, v_ffn_w_down, v_ffn_norm_post)))
    return _step(x, mem, positions, loss_target, w, m, v)
```

```python
import functools

import jax
import jax.numpy as jnp
import numpy as np
from jax import lax
from jax.experimental import pallas as pl
from jax.experimental.pallas import tpu as pltpu

F32, BF16 = jnp.float32, jnp.bfloat16
MESH = pl.DeviceIdType.MESH
EPS = 1e-6
LANES = 128
BLK = 128
HD = 64
D_MODEL = 1024
D_FF = 2816
SB_W, CV_W, DL_W = 256, 256, 512
CV_K = 31
ROPE_THETA = 10000.0
DILATIONS = (1, 4, 16)
X_HEADS, X_HD = 4, 256
ADAM_LR, ADAM_B1, ADAM_B2, ADAM_EPS, ADAM_WD, ADAM_STEP = 0.001, 0.9, 0.999, 1e-08, 0.01, 10
NEG_INF = float("-inf")
MIB = 1 << 20

PACK_ROWS = (("w_in", 704), ("w_out", 256), ("x_wq", 256), ("x_wk", 256), ("x_wv", 256), ("x_wo", 256),
             ("ffn_w_up", 1408), ("ffn_w_down", 704))
PACK_RL = sum(r for _, r in PACK_ROWS)


def _cp(vmem_mb=48):
    return pltpu.CompilerParams(vmem_limit_bytes=vmem_mb * MIB)


def _dot(a, b):
    return jnp.dot(a, b, preferred_element_type=F32)


def _dot_nt(a, b):
    return lax.dot_general(a, b, (((1,), (1,)), ((), ())), preferred_element_type=F32)


def _dot_tn(a, b):
    return lax.dot_general(a, b, (((0,), (0,)), ((), ())), preferred_element_type=F32)


def _dot_hilo(x, m):
    hi = x.astype(BF16)
    lo = (x - hi.astype(F32)).astype(BF16)
    return _dot(hi, m) + _dot(lo, m)


def _rowsum8(x):
    t, c = x.shape
    return x.reshape(t // 8, 8, c).sum(axis=0)


def _acc_out(ref, i, val):
    @pl.when(i == 0)
    def _():
        ref[...] = val

    @pl.when(i > 0)
    def _():
        ref[...] += val


def _tile(n, cap, mult=8):
    t = min(n, cap)
    while n % t or t % mult:
        t -= 1
    return t


def _rms_mm(x, g, w, *, tm, tn, out_dtype, name, carry=None):
    m, d = x.shape
    n_out = w.shape[1]

    def body(x_ref, g_ref, w_ref, n_ref, o_ref):
        @pl.when(pl.program_id(1) == 0)
        def _():
            xv = x_ref[...]
            r = lax.rsqrt(jnp.mean(xv * xv, axis=-1, keepdims=True) + EPS)
            n_ref[...] = (xv * r * g_ref[...]).astype(BF16)

        o_ref[...] = _dot(n_ref[...], w_ref[...]).astype(out_dtype)

    return _call(
        body, grid=(m // tm, n_out // tn), name=name, carry=carry,
        in_specs=[pl.BlockSpec((tm, d), lambda i, j: (i, 0)), pl.BlockSpec((1, d), lambda i, j: (0, 0)),
                  pl.BlockSpec((d, tn), lambda i, j: (0, j))],
        out_specs=[pl.BlockSpec((tm, d), lambda i, j: (i, 0)), pl.BlockSpec((tm, tn), lambda i, j: (i, j))],
        out_shape=[jax.ShapeDtypeStruct((m, d), BF16), jax.ShapeDtypeStruct((m, n_out), out_dtype)],
        args=(x, g, w))


def _mm_post(a, w, h, g, *, tm, name, carry=None):
    m, k = a.shape
    d = w.shape[1]

    def body(a_ref, w_ref, h_ref, g_ref, y_ref, ho_ref):
        y = _dot(a_ref[...], w_ref[...])
        y_ref[...] = y
        r = lax.rsqrt(jnp.mean(y * y, axis=-1, keepdims=True) + EPS)
        ho_ref[...] = h_ref[...] + y * r * g_ref[...]

    return _call(
        body, grid=(m // tm,), name=name, carry=carry,
        in_specs=[pl.BlockSpec((tm, k), lambda i: (i, 0)), pl.BlockSpec((k, d), lambda i: (0, 0)),
                  pl.BlockSpec((tm, d), lambda i: (i, 0)), pl.BlockSpec((1, d), lambda i: (0, 0))],
        out_specs=[pl.BlockSpec((tm, d), lambda i: (i, 0)), pl.BlockSpec((tm, d), lambda i: (i, 0))],
        out_shape=[jax.ShapeDtypeStruct((m, d), F32), jax.ShapeDtypeStruct((m, d), F32)],
        args=(a, w, h, g))


def _mm_nt(a, w, *, tm, tn, out_dtype, name):
    m, k = a.shape
    n_out = w.shape[0]

    def body(a_ref, w_ref, o_ref):
        o_ref[...] = _dot_nt(a_ref[...], w_ref[...]).astype(out_dtype)

    return pl.pallas_call(
        body, grid=(n_out // tn, m // tm), name=name,
        in_specs=[pl.BlockSpec((tm, k), lambda j, i: (i, 0)), pl.BlockSpec((tn, k), lambda j, i: (j, 0))],
        out_specs=pl.BlockSpec((tm, tn), lambda j, i: (i, j)),
        out_shape=jax.ShapeDtypeStruct((m, n_out), out_dtype),
        compiler_params=_cp())(a, w)


def _mm_tn(x, dy, *, tk, tn, tm, name):
    m, k = x.shape
    n_out = dy.shape[1]

    def body(x_ref, d_ref, o_ref):
        _acc_out(o_ref, pl.program_id(2), _dot_tn(x_ref[...], d_ref[...]))

    return pl.pallas_call(
        body, grid=(k // tk, n_out // tn, m // tm), name=name,
        in_specs=[pl.BlockSpec((tm, tk), lambda a, b, c: (c, a)), pl.BlockSpec((tm, tn), lambda a, b, c: (c, b))],
        out_specs=pl.BlockSpec((tk, tn), lambda a, b, c: (a, b)),
        out_shape=jax.ShapeDtypeStruct((k, n_out), F32),
        compiler_params=_cp())(x, dy)


def _rms_bwd(x, g, dout, res, *, out_dtype, tm, name, carry=None):
    m, d = x.shape
    has_res = res is not None

    def body(*refs):
        if has_res:
            x_ref, g_ref, d_ref, r_ref, dx_ref, dg_ref = refs
        else:
            x_ref, g_ref, d_ref, dx_ref, dg_ref = refs
        xv = x_ref[...]
        dv = d_ref[...].astype(F32)
        r = lax.rsqrt(jnp.mean(xv * xv, axis=-1, keepdims=True) + EPS)
        xh = xv * r
        dxh = dv * g_ref[...]
        dx = r * (dxh - xh * jnp.mean(dxh * xh, axis=-1, keepdims=True))
        if has_res:
            dx = dx + r_ref[...]
        dx_ref[...] = dx.astype(out_dtype)
        _acc_out(dg_ref, pl.program_id(0), _rowsum8(dv * xh))

    row = pl.BlockSpec((tm, d), lambda i: (i, 0))
    ins = [row, pl.BlockSpec((1, d), lambda i: (0, 0)), row] + ([row] if has_res else [])
    args = (x, g, dout) + ((res,) if has_res else ())
    return _call(
        body, grid=(m // tm,), name=name, carry=carry, in_specs=ins,
        out_specs=[row, pl.BlockSpec((8, d), lambda i: (0, 0))],
        out_shape=[jax.ShapeDtypeStruct((m, d), out_dtype), jax.ShapeDtypeStruct((8, d), F32)], args=args)


def _loss_grad(h, tgt, *, tm, name):
    m, d = h.shape

    def body(h_ref, t_ref, dh_ref, p_ref):
        e = h_ref[...] - t_ref[...]
        dh_ref[...] = e / d
        _acc_out(p_ref, pl.program_id(0), _rowsum8(e * e))

    row = pl.BlockSpec((tm, d), lambda i: (i, 0))
    return pl.pallas_call(
        body, grid=(m // tm,), name=name, in_specs=[row, row],
        out_specs=[row, pl.BlockSpec((8, d), lambda i: (0, 0))],
        out_shape=[jax.ShapeDtypeStruct((m, d), F32), jax.ShapeDtypeStruct((8, d), F32)],
        compiler_params=_cp())(h, tgt)


def _adamw(w, g, m, v, *, name):
    r, c = w.shape
    tr = _tile(r, 256)

    def body(w_ref, g_ref, m_ref, v_ref, d_ref, mo_ref, vo_ref):
        gv = g_ref[...]
        m2 = ADAM_B1 * m_ref[...] + (1.0 - ADAM_B1) * gv
        v2 = ADAM_B2 * v_ref[...] + (1.0 - ADAM_B2) * jnp.square(gv)
        m_hat = m2 / (1.0 - ADAM_B1 ** ADAM_STEP)
        v_hat = v2 / (1.0 - ADAM_B2 ** ADAM_STEP)
        d_ref[...] = -ADAM_LR * (m_hat / (jnp.sqrt(v_hat) + ADAM_EPS) + ADAM_WD * w_ref[...])
        mo_ref[...] = m2
        vo_ref[...] = v2

    blk = pl.BlockSpec((tr, c), lambda i: (i, 0))
    return pl.pallas_call(
        body, grid=(r // tr,), name=name, in_specs=[blk] * 4, out_specs=[blk] * 3,
        out_shape=[jax.ShapeDtypeStruct((r, c), F32)] * 3, compiler_params=_cp())(w, g, m, v)


def _head_masks():
    lane = lax.broadcasted_iota(jnp.int32, (BLK, LANES), 1)
    row = lax.broadcasted_iota(jnp.int32, (BLK, LANES), 0)
    return lane, row, lane < HD


def _sb_scores(q_a, k, before):
    z = _dot_nt(q_a, k)
    sp = jnp.log1p(jnp.exp(-jnp.abs(z)))
    ls_pos = jnp.minimum(z, 0.0) - sp
    lkeep = jnp.where(before, ls_pos - z, 0.0)
    return ls_pos, lkeep


SB_DEAD = -104.0


def _sb_alive(jj, i, carry):
    return jnp.logical_and(jj <= i, jnp.max(carry) > SB_DEAD)


SB_QB = 2
SB_ROWS = SB_QB * 2 * BLK


def _sb_before(jj):
    lane = lax.broadcasted_iota(jnp.int32, (SB_ROWS, LANES), 1)
    row = lax.broadcasted_iota(jnp.int32, (SB_ROWS, LANES), 0)
    below_diag = jj - (SB_QB - 1) + row // (2 * BLK)
    return jnp.logical_or(below_diag > 0, jnp.logical_and(below_diag == 0, lane < row % BLK))


def _sb_stack(x, lane_h):
    return jnp.concatenate([_stack_heads(x[b * BLK:(b + 1) * BLK], lane_h) for b in range(SB_QB)], axis=0)


def _sb_unstack(x, lane_h):
    return jnp.concatenate([jnp.where(lane_h, x[2 * b * BLK:(2 * b + 1) * BLK], x[(2 * b + 1) * BLK:(2 * b + 2) * BLK])
                            for b in range(SB_QB)], axis=0)


def _sb_fwd(u, *, name, carry=None):
    s_len = u.shape[0]
    qrows = SB_QB * BLK

    def body(q_ref, k_ref, v_ref, o_ref):
        top = pl.program_id(1) * SB_QB + SB_QB - 1
        lane, row, lane_h = _head_masks()
        suffix = (row > lane).astype(BF16)
        qs = _sb_stack(q_ref[...] * 0.125, lane_h)

        def step(state):
            jj, cc, acc = state
            off = pl.multiple_of((top - jj) * BLK, BLK)
            k = k_ref[pl.ds(off, BLK), :].astype(BF16)
            v = v_ref[pl.ds(off, BLK), :].astype(BF16)
            before = _sb_before(jj)
            ls_pos, lkeep = _sb_scores(qs, k, before)
            between = _dot_hilo(lkeep, suffix) + cc
            att = jnp.where(before, jnp.exp(ls_pos + between), 0.0)
            return jj + 1, cc + jnp.sum(lkeep, axis=1, keepdims=True), acc + _dot(att.astype(BF16), v)

        init = (jnp.int32(0), jnp.zeros((SB_ROWS, 1), F32), jnp.zeros((SB_ROWS, LANES), F32))
        acc = lax.while_loop(lambda st: _sb_alive(st[0], top, st[1]), step, init)[2]
        o_ref[...] = _sb_unstack(acc, lane_h).astype(BF16)

    return _call(
        body, grid=(2, s_len // qrows), name=name, carry=carry,
        in_specs=[pl.BlockSpec((qrows, LANES), lambda hp, i: (i, hp)),
                  pl.BlockSpec((s_len, LANES), lambda hp, i: (0, 2 + hp)),
                  pl.BlockSpec((s_len, LANES), lambda hp, i: (0, 4 + hp))],
        out_specs=[pl.BlockSpec((qrows, LANES), lambda hp, i: (i, hp))],
        out_shape=[jax.ShapeDtypeStruct((s_len, SB_W), BF16)], args=(u, u, u))


def _sb_bwd(u, dcat, *, name, carry=None):
    s_len = u.shape[0]
    nq = s_len // BLK
    qrows = SB_QB * BLK

    def body(q_ref, k_ref, v_ref, do_ref, dq_ref, dk_ref, dv_ref, g_scr, b_scr):
        step = pl.program_id(1)
        top = step * SB_QB + SB_QB - 1
        lane, row, lane_h = _head_masks()
        suffix = (row > lane).astype(BF16)
        prefix = (row < lane).astype(BF16)
        qf = q_ref[...]
        qs = _sb_stack(qf * 0.125, lane_h)
        qu = _sb_stack(qf, lane_h)
        dos = _sb_stack(do_ref[...], lane_h)

        @pl.when(step == 0)
        def _():
            dk_ref[...] = jnp.zeros_like(dk_ref)
            dv_ref[...] = jnp.zeros_like(dv_ref)

        def down(state):
            jj, cc = state
            j = top - jj
            off = pl.multiple_of(j * BLK, BLK)
            k = k_ref[pl.ds(off, BLK), :].astype(BF16)
            v = v_ref[pl.ds(off, BLK), :].astype(BF16)
            before = _sb_before(jj)
            ls_pos, lkeep = _sb_scores(qs, k, before)
            between = _dot_hilo(lkeep, suffix) + cc
            att = jnp.where(before, jnp.exp(ls_pos + between), 0.0)
            g_scr[j] = att * _dot_nt(dos, v)
            b_scr[j] = jnp.exp(ls_pos)
            dv_ref[pl.ds(off, BLK), :] += _dot_tn(att.astype(BF16), dos)
            return jj + 1, cc + jnp.sum(lkeep, axis=1, keepdims=True)

        zc = jnp.zeros((SB_ROWS, 1), F32)
        visited = lax.while_loop(lambda st: _sb_alive(st[0], top, st[1]), down, (jnp.int32(0), zc))[0]

        def up(j, carry):
            pc, dq = carry
            off = pl.multiple_of(j * BLK, BLK)
            k = k_ref[pl.ds(off, BLK), :].astype(BF16)
            g, beta = g_scr[j], b_scr[j]
            below = _dot_hilo(g, prefix) + pc
            dz = (jnp.where(_sb_before(top - j), g * (1.0 - beta) - beta * below, 0.0) * 0.125).astype(BF16)
            dk_ref[pl.ds(off, BLK), :] += _dot_tn(dz, qu)
            return pc + jnp.sum(g, axis=1, keepdims=True), dq + _dot(dz, k)

        dq = lax.fori_loop(top + 1 - visited, top + 1, up, (zc, jnp.zeros((SB_ROWS, LANES), F32)))[1]
        dq_ref[...] = _sb_unstack(dq, lane_h)

    col = lambda c0: pl.BlockSpec((s_len, LANES), lambda hp, i: (0, c0 + hp))
    blk = pl.BlockSpec((qrows, LANES), lambda hp, i: (i, hp))
    acc = pl.BlockSpec((s_len, LANES), lambda hp, i: (0, hp))
    return _call(
        body, grid=(2, s_len // qrows), name=name, carry=carry, in_specs=[blk, col(2), col(4), blk],
        out_specs=[blk, acc, acc], out_shape=[jax.ShapeDtypeStruct((s_len, SB_W), F32)] * 3,
        scratch_shapes=[pltpu.VMEM((nq, SB_ROWS, LANES), F32), pltpu.VMEM((nq, SB_ROWS, LANES), F32)],
        vmem_mb=56, args=(u, u, u, dcat))


CV_T = 512
CV_H = 32


def _cv_specs(s_len):
    cur = lambda c: pl.BlockSpec((CV_T, CV_W), lambda i: (i, c))
    prev = lambda c: pl.BlockSpec((CV_H, CV_W), lambda i: (jnp.maximum(i * (CV_T // CV_H) - 1, 0), c))
    nxt = lambda c: pl.BlockSpec((CV_H, CV_W),
                                 lambda i: (jnp.minimum((i + 1) * (CV_T // CV_H), s_len // CV_H - 1), c))
    full = lambda r: pl.BlockSpec((r, CV_W), lambda i: (0, 0))
    return cur, prev, nxt, full


def _glu_into(gp_ref, val_ref, gate_ref, valp_ref, gatep_ref, i):
    gp_ref[0:CV_H, :] = jnp.where(i > 0, valp_ref[...] * jax.nn.sigmoid(gatep_ref[...]), 0.0)
    gp_ref[CV_H:, :] = val_ref[...] * jax.nn.sigmoid(gate_ref[...])


def _cv_fwd(u, cv_w, cv_b, ln_g, ln_b, pw_w, pw_b, *, name):
    s_len = u.shape[0]
    cur, prev, _, full = _cv_specs(s_len)

    def body(val_ref, gate_ref, valp_ref, gatep_ref, w_ref, b_ref, g_ref, be_ref, pw_ref, pb_ref,
             o_ref, c_ref, gp_ref):
        _glu_into(gp_ref, val_ref, gate_ref, valp_ref, gatep_ref, pl.program_id(0))
        acc = jnp.zeros((CV_T, CV_W), F32) + b_ref[...]
        for k in range(CV_K):
            acc = acc + w_ref[k:k + 1, :] * gp_ref[pl.ds(CV_H - CV_K + 1 + k, CV_T), :]
        c_ref[...] = acc
        mu = jnp.mean(acc, axis=-1, keepdims=True)
        xc = acc - mu
        xh = xc * lax.rsqrt(jnp.mean(xc * xc, axis=-1, keepdims=True) + EPS)
        a = xh * g_ref[...] + be_ref[...]
        s = a * jax.nn.sigmoid(a)
        o_ref[...] = (_dot(s.astype(BF16), pw_ref[...]) + pb_ref[...]).astype(BF16)

    return pl.pallas_call(
        body, grid=(s_len // CV_T,), name=name,
        in_specs=[cur(3), cur(4), prev(3), prev(4), full(CV_K), full(1), full(1), full(1), full(CV_W), full(1)],
        out_specs=[cur(0), cur(0)],
        out_shape=[jax.ShapeDtypeStruct((s_len, CV_W), BF16), jax.ShapeDtypeStruct((s_len, CV_W), F32)],
        scratch_shapes=[pltpu.VMEM((CV_T + CV_H, CV_W), F32)], compiler_params=_cp())(
            u, u, u, u, cv_w, cv_b, ln_g, ln_b, pw_w, pw_b)


def _cv_bwd_local(c, dcat, ln_g, ln_b, pw_w, *, name):
    s_len = c.shape[0]
    cur, _, _, full = _cv_specs(s_len)

    def body(c_ref, db_ref, g_ref, be_ref, pw_ref, dc_ref, dpw_ref, vec_ref):
        i = pl.program_id(0)
        cv = c_ref[...]
        db = db_ref[...]
        mu = jnp.mean(cv, axis=-1, keepdims=True)
        xc = cv - mu
        rstd = lax.rsqrt(jnp.mean(xc * xc, axis=-1, keepdims=True) + EPS)
        xh = xc * rstd
        a = xh * g_ref[...] + be_ref[...]
        sg = jax.nn.sigmoid(a)
        s = a * sg
        dbb = db.astype(BF16)
        ds = _dot_nt(dbb, pw_ref[...])
        da = ds * (sg * (1.0 + a * (1.0 - sg)))
        dxh = da * g_ref[...]
        dc_ref[...] = rstd * (dxh - jnp.mean(dxh, axis=-1, keepdims=True)
                              - xh * jnp.mean(dxh * xh, axis=-1, keepdims=True))
        _acc_out(dpw_ref, i, _dot_tn(s.astype(BF16), dbb))
        _acc_out(vec_ref, i, jnp.concatenate([_rowsum8(db), _rowsum8(da * xh), _rowsum8(da)], axis=0))

    return pl.pallas_call(
        body, grid=(s_len // CV_T,), name=name,
        in_specs=[cur(0), cur(1), full(1), full(1), full(CV_W)],
        out_specs=[cur(0), full(CV_W), full(24)],
        out_shape=[jax.ShapeDtypeStruct((s_len, CV_W), F32), jax.ShapeDtypeStruct((CV_W, CV_W), F32),
                   jax.ShapeDtypeStruct((24, CV_W), F32)], compiler_params=_cp())(c, dcat, ln_g, ln_b, pw_w)


def _cv_bwd_conv(u, dc, cv_w, *, name):
    s_len = u.shape[0]
    cur, prev, nxt, full = _cv_specs(s_len)
    last = s_len // CV_T - 1

    def body(val_ref, gate_ref, valp_ref, gatep_ref, dc_ref, dcn_ref, w_ref, du_ref, dw_ref, dbias_ref,
             gp_ref, dcp_ref):
        i = pl.program_id(0)
        _glu_into(gp_ref, val_ref, gate_ref, valp_ref, gatep_ref, i)
        dcv = dc_ref[...]
        dcp_ref[0:CV_T, :] = dcv
        dcp_ref[CV_T:, :] = jnp.where(i < last, dcn_ref[...], 0.0)
        dg = jnp.zeros((CV_T, CV_W), F32)
        parts = []
        for k in range(CV_K):
            dg = dg + w_ref[k:k + 1, :] * dcp_ref[pl.ds(CV_K - 1 - k, CV_T), :]
            parts.append(_rowsum8(dcv * gp_ref[pl.ds(CV_H - CV_K + 1 + k, CV_T), :]))
        _acc_out(dw_ref, i, jnp.concatenate(parts, axis=0))
        _acc_out(dbias_ref, i, _rowsum8(dcv))
        val = val_ref[...]
        sg = jax.nn.sigmoid(gate_ref[...])
        du_ref[:, 0:CV_W] = (dg * sg).astype(BF16)
        du_ref[:, CV_W:] = (dg * val * sg * (1.0 - sg)).astype(BF16)

    return pl.pallas_call(
        body, grid=(s_len // CV_T,), name=name,
        in_specs=[cur(3), cur(4), prev(3), prev(4), cur(0), nxt(0), full(CV_K)],
        out_specs=[pl.BlockSpec((CV_T, 2 * CV_W), lambda i: (i, 0)), full(CV_K * 8), full(8)],
        out_shape=[jax.ShapeDtypeStruct((s_len, 2 * CV_W), BF16), jax.ShapeDtypeStruct((CV_K * 8, CV_W), F32),
                   jax.ShapeDtypeStruct((8, CV_W), F32)],
        scratch_shapes=[pltpu.VMEM((CV_T + CV_H, CV_W), F32), pltpu.VMEM((CV_T + CV_H, CV_W), F32)],
        compiler_params=_cp())(u, u, u, u, dc, dc, cv_w)


def _rope_tables(pos_col, inv_freq_row, *, name):
    s_len = pos_col.shape[0]

    def body(p_ref, f_ref, cos_ref, sin_ref):
        ang = p_ref[...].astype(F32) * f_ref[...]
        lane = lax.broadcasted_iota(jnp.int32, (s_len, LANES), 1)
        sn = jnp.sin(ang)
        cos_ref[...] = jnp.cos(ang)
        sin_ref[...] = jnp.where(lane % HD < HD // 2, -sn, sn)

    return pl.pallas_call(body, name=name, out_shape=[jax.ShapeDtypeStruct((s_len, LANES), F32)] * 2,
                          compiler_params=_cp())(pos_col, inv_freq_row)


def _rot_half(x):
    lane = lax.broadcasted_iota(jnp.int32, x.shape, 1)
    return jnp.where(lane % HD < HD // 2, pltpu.roll(x, LANES - HD // 2, 1), pltpu.roll(x, HD // 2, 1))


def _permute_rows(dst_ref, src_ref, d, dtype):
    s_len = src_ref.shape[0]
    seg = s_len // d
    if d == 1:
        dst_ref[...] = src_ref[...].astype(dtype)
        return
    for r in range(d):
        dst_ref[r * seg:(r + 1) * seg, :] = src_ref[pl.ds(r, seg, stride=d), :].astype(dtype)


def _unpermute_rows(dst_ref, src_ref, d):
    s_len = src_ref.shape[0]
    seg = s_len // d
    if d == 1:
        dst_ref[...] = src_ref[...]
        return
    for r in range(d):
        dst_ref[pl.ds(r, seg, stride=d), :] = src_ref[r * seg:(r + 1) * seg, :]


def _rope_perm(u, cos, sin, *, name):
    s_len = u.shape[0]

    def body(x_ref, cos_ref, sin_ref, o_ref, scr):
        a = pl.program_id(0)
        x = x_ref[...]
        rot = a < 2
        scr[...] = x * jnp.where(rot, cos_ref[...], 1.0) + _rot_half(x) * jnp.where(rot, sin_ref[...], 0.0)
        for n, d in enumerate(DILATIONS):
            _permute_rows(o_ref.at[n], scr, d, BF16)

    tab = pl.BlockSpec((s_len, LANES), lambda a, cb: (0, 0))
    return pl.pallas_call(
        body, grid=(3, 4), name=name,
        in_specs=[pl.BlockSpec((s_len, LANES), lambda a, cb: (0, 10 + 4 * a + cb)), tab, tab],
        out_specs=pl.BlockSpec((None, 3, s_len, LANES), lambda a, cb: (a, 0, 0, cb)),
        out_shape=jax.ShapeDtypeStruct((3, 3, s_len, DL_W), BF16),
        scratch_shapes=[pltpu.VMEM((s_len, LANES), F32)], compiler_params=_cp())(u, cos, sin)


DL_UNROLL = 4


def _dl_band(rows):
    lane = lax.broadcasted_iota(jnp.int32, (rows, LANES), 1)
    row = lax.broadcasted_iota(jnp.int32, (rows, LANES), 0) % BLK
    return lane <= row, lane >= row


def _dl_first(s_len, n, i):
    nb = jnp.where(n == 0, s_len // BLK, jnp.where(n == 1, s_len // (BLK * DILATIONS[1]),
                                                   s_len // (BLK * DILATIONS[2])))
    return lax.rem(i, nb) == 0


def _stack_heads(x, lane_h):
    return jnp.concatenate([jnp.where(lane_h, x, 0.0), jnp.where(lane_h, 0.0, x)], axis=0).astype(BF16)


def _dl_rows(i):
    cur = pl.ds(pl.multiple_of(i * BLK, BLK), BLK)
    prev = pl.ds(pl.multiple_of(jnp.maximum(i - 1, 0) * BLK, BLK), BLK)
    return cur, prev


def _dl_in_specs(s_len):
    return [pl.BlockSpec((None, None, s_len, LANES), functools.partial(lambda a, n, hp: (a, n, 0, hp), a))
            for a in range(3)]


def _dl_fwd(qkv, *, name, carry=None):
    s_len = qkv.shape[2]

    def body(q_ref, k_ref, v_ref, o_ref, l_ref):
        n = pl.program_id(0)
        lane_h = _head_masks()[2]
        band_c, band_p = _dl_band(2 * BLK)
        ones = jnp.ones((BLK, LANES), BF16)

        @pl.loop(0, s_len // BLK, step=DL_UNROLL)
        def _(i0):
            blocks = [i0 + t for t in range(DL_UNROLL)]
            rows = [_dl_rows(i) for i in blocks]
            scores = []
            for cur, prev in rows:
                qs = _stack_heads(q_ref[cur, :] * 0.125, lane_h)
                scores.append((_dot_nt(qs, k_ref[cur, :]), _dot_nt(qs, k_ref[prev, :])))
            probs = []
            for i, (sc, sp) in zip(blocks, scores):
                sc = jnp.where(band_c, sc, NEG_INF)
                sp = jnp.where(jnp.logical_and(band_p, jnp.logical_not(_dl_first(s_len, n, i))), sp, NEG_INF)
                m = jnp.max(jnp.maximum(sc, sp), axis=1, keepdims=True)
                probs.append((jnp.exp(sc - m).astype(BF16), jnp.exp(sp - m).astype(BF16), m))
            for (cur, prev), (pc, pp, m) in zip(rows, probs):
                r = (_dot(pc, jnp.concatenate([v_ref[cur, :], ones], axis=1))
                     + _dot(pp, jnp.concatenate([v_ref[prev, :], ones], axis=1)))
                den = jnp.where(lane_h, r[:BLK, LANES:], r[BLK:, LANES:])
                o_ref[cur, :] = jnp.where(lane_h, r[:BLK, :LANES], r[BLK:, :LANES]) / den
                l_ref[cur, :] = jnp.where(lane_h, m[:BLK], m[BLK:]) + jnp.log(den)

    out = pl.BlockSpec((None, s_len, LANES), lambda n, hp: (n, 0, hp))
    return _call(
        body, grid=(3, 4), name=name, carry=carry, in_specs=_dl_in_specs(s_len), out_specs=[out, out],
        out_shape=[jax.ShapeDtypeStruct((3, s_len, DL_W), F32)] * 2, args=(qkv, qkv, qkv))


def _dl_mix(o_p, l_p, *, name, carry=None):
    s_len = o_p.shape[1]

    def body(o_ref, l_ref, ob_ref, of_ref, lt_ref, o_scr, l_scr):
        n = pl.program_id(1)
        for k, d in enumerate(DILATIONS):
            @pl.when(n == k)
            def _(k=k, d=d):
                _unpermute_rows(o_scr.at[k], o_ref, d)
                _unpermute_rows(l_scr.at[k], l_ref, d)

        @pl.when(n == 2)
        def _():
            l0, l1, l2 = l_scr[0], l_scr[1], l_scr[2]
            m = jnp.maximum(jnp.maximum(l0, l1), l2)
            e0, e1, e2 = jnp.exp(l0 - m), jnp.exp(l1 - m), jnp.exp(l2 - m)
            den = e0 + e1 + e2
            o = (e0 / den) * o_scr[0] + (e1 / den) * o_scr[1] + (e2 / den) * o_scr[2]
            of_ref[...] = o
            ob_ref[...] = o.astype(BF16)
            lt_ref[...] = m + jnp.log(den)

    inb = pl.BlockSpec((None, s_len, LANES), lambda cb, n: (n, 0, cb))
    outb = pl.BlockSpec((s_len, LANES), lambda cb, n: (0, cb))
    return _call(
        body, grid=(4, 3), name=name, carry=carry, in_specs=[inb, inb], out_specs=[outb, outb, outb],
        out_shape=[jax.ShapeDtypeStruct((s_len, DL_W), BF16), jax.ShapeDtypeStruct((s_len, DL_W), F32),
                   jax.ShapeDtypeStruct((s_len, DL_W), F32)],
        scratch_shapes=[pltpu.VMEM((3, s_len, LANES), F32), pltpu.VMEM((3, s_len, LANES), F32)], args=(o_p, l_p))


def _dl_bwd_prep(dcat, o, lse, *, name):
    s_len = o.shape[0]

    def body(do_ref, o_ref, l_ref, dop_ref, st_ref, d_scr):
        n = pl.program_id(1)

        @pl.when(n == 0)
        def _():
            r0 = lax.broadcasted_iota(jnp.int32, (LANES, LANES), 0) // HD
            r1 = lax.broadcasted_iota(jnp.int32, (LANES, LANES), 1) // HD
            d_scr[...] = _dot_hilo(do_ref[...] * o_ref[...], (r0 == r1).astype(BF16))

        for k, d in enumerate(DILATIONS):
            @pl.when(n == k)
            def _(d=d):
                _permute_rows(dop_ref, do_ref, d, BF16)
                _permute_rows(st_ref.at[0], d_scr, d, F32)
                _permute_rows(st_ref.at[1], l_ref, d, F32)

    nat = lambda c0: pl.BlockSpec((s_len, LANES), lambda cb, n: (0, c0 + cb))
    return pl.pallas_call(
        body, grid=(4, 3), name=name, in_specs=[nat(4), nat(0), nat(0)],
        out_specs=[pl.BlockSpec((None, s_len, LANES), lambda cb, n: (n, 0, cb)),
                   pl.BlockSpec((2, None, s_len, LANES), lambda cb, n: (0, n, 0, cb))],
        out_shape=[jax.ShapeDtypeStruct((3, s_len, DL_W), BF16), jax.ShapeDtypeStruct((2, 3, s_len, DL_W), F32)],
        scratch_shapes=[pltpu.VMEM((s_len, LANES), F32)], compiler_params=_cp())(dcat, o, lse)


def _dl_bwd(qkv, dop, stats, *, name, carry=None):
    s_len = qkv.shape[2]

    def body(q_ref, k_ref, v_ref, do_ref, st_ref, cur_ref, prev_ref):
        n = pl.program_id(0)
        lane_h = _head_masks()[2]
        band_c, band_p = _dl_band(2 * BLK)

        def per_head(x):
            xr = pltpu.roll(x, HD, 1)
            return jnp.concatenate([jnp.where(lane_h, x, xr), jnp.where(lane_h, xr, x)], axis=0)

        @pl.loop(0, s_len // BLK, step=DL_UNROLL)
        def _(i0):
            blocks = [i0 + t for t in range(DL_UNROLL)]
            rows = [_dl_rows(i) for i in blocks]
            stage1 = []
            for cur, prev in rows:
                qs = _stack_heads(q_ref[cur, :] * 0.125, lane_h)
                dos = _stack_heads(do_ref[cur, :], lane_h)
                kc, kp, vc, vp = k_ref[cur, :], k_ref[prev, :], v_ref[cur, :], v_ref[prev, :]
                stage1.append((qs, dos, _dot_nt(qs, kc), _dot_nt(qs, kp), _dot_nt(dos, vc), _dot_nt(dos, vp)))
            stage2 = []
            for i, (cur, prev), (qs, dos, sc, sp, dpc, dpp) in zip(blocks, rows, stage1):
                lse, delta = per_head(st_ref[1, cur, :]), per_head(st_ref[0, cur, :])
                pc = jnp.where(band_c, jnp.exp(sc - lse), 0.0)
                pp = jnp.where(jnp.logical_and(band_p, jnp.logical_not(_dl_first(s_len, n, i))), jnp.exp(sp - lse), 0.0)
                stage2.append((pc.astype(BF16), pp.astype(BF16), (pc * (dpc - delta)).astype(BF16),
                               (pp * (dpp - delta)).astype(BF16)))
            for (cur, prev), (qs, dos, *_), (pc, pp, dsc, dsp) in zip(rows, stage1, stage2):
                dq = _dot(dsc, k_ref[cur, :]) + _dot(dsp, k_ref[prev, :])
                cur_ref[0, cur, :] = jnp.where(lane_h, dq[:BLK], dq[BLK:]) * 0.125
                cur_ref[1, cur, :] = _dot_tn(dsc, qs)
                cur_ref[2, cur, :] = _dot_tn(pc, dos)
                prev_ref[0, cur, :] = _dot_tn(dsp, qs)
                prev_ref[1, cur, :] = _dot_tn(pp, dos)

    return _call(
        body, grid=(3, 4), name=name, carry=carry,
        in_specs=_dl_in_specs(s_len) + [pl.BlockSpec((None, s_len, LANES), lambda n, hp: (n, 0, hp)),
                                        pl.BlockSpec((2, None, s_len, LANES), lambda n, hp: (0, n, 0, hp))],
        out_specs=[pl.BlockSpec((3, None, s_len, LANES), lambda n, hp: (0, n, 0, hp)),
                   pl.BlockSpec((2, None, s_len, LANES), lambda n, hp: (0, n, 0, hp))],
        out_shape=[jax.ShapeDtypeStruct((3, 3, s_len, DL_W), F32), jax.ShapeDtypeStruct((2, 3, s_len, DL_W), F32)],
        vmem_mb=56, args=(qkv, qkv, qkv, dop, stats))


def _dl_bwd_finish(cur, prev, cos, sin, *, name):
    s_len = cur.shape[2]

    def body(c_ref, p_ref, cos_ref, sin_ref, o_ref, p_scr, u_scr, acc):
        a, n = pl.program_id(0), pl.program_id(2)
        has_prev = jnp.where(a > 0, 1.0, 0.0)
        p_scr[...] = c_ref[...]
        p_scr[0:s_len - BLK, :] += has_prev * p_ref[BLK:, :]
        for k, d in enumerate(DILATIONS):
            @pl.when(n == k)
            def _(k=k, d=d):
                if k == 0:
                    acc[...] = p_scr[...]
                else:
                    _unpermute_rows(u_scr, p_scr, d)
                    acc[...] += u_scr[...]

        @pl.when(n == 2)
        def _():
            dy = acc[...]
            rot = a < 2
            o_ref[...] = (dy * jnp.where(rot, cos_ref[...], 1.0)
                          + _rot_half(dy * jnp.where(rot, sin_ref[...], 0.0))).astype(BF16)

    tab = pl.BlockSpec((s_len, LANES), lambda a, cb, n: (0, 0))
    return pl.pallas_call(
        body, grid=(3, 4, 3), name=name,
        in_specs=[pl.BlockSpec((None, None, s_len, LANES), lambda a, cb, n: (a, n, 0, cb)),
                  pl.BlockSpec((None, None, s_len, LANES), lambda a, cb, n: (jnp.maximum(a - 1, 0), n, 0, cb)),
                  tab, tab],
        out_specs=pl.BlockSpec((s_len, LANES), lambda a, cb, n: (0, 4 * a + cb)),
        out_shape=jax.ShapeDtypeStruct((s_len, 3 * DL_W), BF16),
        scratch_shapes=[pltpu.VMEM((s_len, LANES), F32)] * 3, compiler_params=_cp())(cur, prev, cos, sin)


XA_T = 256


def _xa_probs(q, k):
    s = _dot_nt(q, k) * (X_HD ** -0.5)
    e = jnp.exp(s - jnp.max(s, axis=1, keepdims=True))
    return e / jnp.sum(e, axis=1, keepdims=True)


def _xa_fwd(q, k, v, *, name):
    s_len, d = q.shape
    nm = k.shape[0]

    def body(q_ref, k_ref, v_ref, o_ref):
        for h in range(X_HEADS):
            cs = slice(h * X_HD, (h + 1) * X_HD)
            p = _xa_probs(q_ref[:, cs], k_ref[:, cs])
            o_ref[:, cs] = _dot(p.astype(BF16), v_ref[:, cs]).astype(BF16)

    row = pl.BlockSpec((XA_T, d), lambda i: (i, 0))
    full = pl.BlockSpec((nm, d), lambda i: (0, 0))
    return pl.pallas_call(body, grid=(s_len // XA_T,), name=name, in_specs=[row, full, full], out_specs=row,
                          out_shape=jax.ShapeDtypeStruct((s_len, d), BF16), compiler_params=_cp())(q, k, v)


def _xa_bwd(q, k, v, do, *, name, carry=None):
    s_len, d = q.shape
    nm = k.shape[0]

    def body(q_ref, k_ref, v_ref, do_ref, dq_ref, dk_ref, dv_ref):
        i = pl.program_id(0)
        for h in range(X_HEADS):
            cs = slice(h * X_HD, (h + 1) * X_HD)
            qh, kh, vh, doh = q_ref[:, cs], k_ref[:, cs], v_ref[:, cs], do_ref[:, cs]
            p = _xa_probs(qh, kh)
            dp = _dot_nt(doh, vh)
            ds = (p * (dp - jnp.sum(dp * p, axis=1, keepdims=True)) * (X_HD ** -0.5)).astype(BF16)
            dq_ref[:, cs] = _dot(ds, kh).astype(BF16)
            dkh, dvh = _dot_tn(ds, qh), _dot_tn(p.astype(BF16), doh)

            @pl.when(i == 0)
            def _(cs=cs, dkh=dkh, dvh=dvh):
                dk_ref[:, cs] = dkh
                dv_ref[:, cs] = dvh

            @pl.when(i > 0)
            def _(cs=cs, dkh=dkh, dvh=dvh):
                dk_ref[:, cs] += dkh
                dv_ref[:, cs] += dvh

    row = pl.BlockSpec((XA_T, d), lambda i: (i, 0))
    full = pl.BlockSpec((nm, d), lambda i: (0, 0))
    return _call(
        body, grid=(s_len // XA_T,), name=name, carry=carry, in_specs=[row, full, full, row],
        out_specs=[row, full, full],
        out_shape=[jax.ShapeDtypeStruct((s_len, d), BF16), jax.ShapeDtypeStruct((nm, d), F32),
                   jax.ShapeDtypeStruct((nm, d), F32)], args=(q, k, v, do))


FF_TM, FF_TN, FF_H = 512, 256, 8
GELU_K, GELU_C = 0.7978845608028654, 0.044715


FF_STRIP = 64


def _ff_conv(e_ref, w_ref, b_ref, rows, r0=0):
    return (w_ref[0:1, :] * e_ref[pl.ds(FF_H - 2 + r0, rows), :] + w_ref[1:2, :] * e_ref[pl.ds(FF_H - 1 + r0, rows), :]
            + w_ref[2:3, :] * e_ref[pl.ds(FF_H + r0, rows), :] + b_ref[...])


def _strips(total, size):
    return [(r0, min(size, total - r0)) for r0 in range(0, total, size)]


def _ff_gate_fwd(up, conv_w, conv_b, *, name, carry=None):
    s_len = up.shape[0]
    nj = D_FF // FF_TN

    def body(g_ref, v_ref, gp_ref, vp_ref, wg_ref, wv_ref, bg_ref, bv_ref, o_ref, eg, ev):
        i = pl.program_id(0)
        for e, cur, prev in ((eg, g_ref, gp_ref), (ev, v_ref, vp_ref)):
            e[0:FF_H, :] = jnp.where(i > 0, prev[...], 0.0)
            e[FF_H:, :] = cur[...]
        for r0, rows in _strips(FF_TM, FF_STRIP):
            gate = _ff_conv(eg, wg_ref, bg_ref, rows, r0)
            val = _ff_conv(ev, wv_ref, bv_ref, rows, r0)
            t = jnp.tanh(GELU_K * (gate + GELU_C * gate * gate * gate))
            o_ref[r0:r0 + rows, :] = (0.5 * gate * (1.0 + t) * val).astype(BF16)

    cur = lambda c0: pl.BlockSpec((FF_TM, FF_TN), lambda i, j: (i, c0 + j))
    prev = lambda c0: pl.BlockSpec((FF_H, FF_TN), lambda i, j: (jnp.maximum(i * (FF_TM // FF_H) - 1, 0), c0 + j))
    par = lambda r, c0: pl.BlockSpec((r, FF_TN), lambda i, j: (0, c0 + j))
    return _call(
        body, grid=(s_len // FF_TM, nj), name=name, carry=carry,
        in_specs=[cur(0), cur(nj), prev(0), prev(nj), par(3, 0), par(3, nj), par(1, 0), par(1, nj)],
        out_specs=[cur(0)], out_shape=[jax.ShapeDtypeStruct((s_len, D_FF), BF16)],
        scratch_shapes=[pltpu.VMEM((FF_TM + FF_H, FF_TN), F32)] * 2,
        args=(up, up, up, up, conv_w, conv_w, conv_b, conv_b))


def _ff_gate_bwd(up, dact, conv_w, conv_b, *, name, carry=None):
    s_len = up.shape[0]
    nj = D_FF // FF_TN
    last = s_len // FF_TM - 1
    ext = FF_TM + FF_H

    def body(g_ref, v_ref, gp_ref, vp_ref, gn_ref, vn_ref, da_ref, dan_ref, wg_ref, wv_ref, bg_ref, bv_ref,
             dg_ref, dv_ref, dw_ref, db_ref, eg, ev, sg, sv):
        i = pl.program_id(1)
        for e, cur, prev, nxt in ((eg, g_ref, gp_ref, gn_ref), (ev, v_ref, vp_ref, vn_ref)):
            e[0:FF_H, :] = jnp.where(i > 0, prev[...], 0.0)
            e[FF_H:FF_H + FF_TM, :] = cur[...]
            e[FF_H + FF_TM:, :] = nxt[...]
        for r0, rows in _strips(ext, FF_STRIP):
            gate = _ff_conv(eg, wg_ref, bg_ref, rows, r0)
            val = _ff_conv(ev, wv_ref, bv_ref, rows, r0)
            dact = da_ref[r0:r0 + rows, :] if r0 < FF_TM else jnp.where(i < last, dan_ref[...], 0.0)
            t = jnp.tanh(GELU_K * (gate + GELU_C * gate * gate * gate))
            half = 0.5 * (1.0 + t)
            dgelu = half + 0.5 * gate * (1.0 - t * t) * GELU_K * (1.0 + 3.0 * GELU_C * gate * gate)
            sg[r0:r0 + rows, :] = dact * val * dgelu
            sv[r0:r0 + rows, :] = dact * (gate * half)
        for part, (s, e, w_ref, out) in enumerate(((sg, eg, wg_ref, dg_ref), (sv, ev, wv_ref, dv_ref))):
            taps, bias = [jnp.zeros((8, FF_TN), F32)] * 3, jnp.zeros((8, FF_TN), F32)
            for r0, rows in _strips(FF_TM, FF_STRIP):
                d0 = s[pl.ds(r0, rows), :]
                out[r0:r0 + rows, :] = (w_ref[2:3, :] * d0 + w_ref[1:2, :] * s[pl.ds(r0 + 1, rows), :]
                                        + w_ref[0:1, :] * s[pl.ds(r0 + 2, rows), :]).astype(BF16)
                taps = [taps[k] + _rowsum8(d0 * e[pl.ds(FF_H - 2 + k + r0, rows), :]) for k in range(3)]
                bias = bias + _rowsum8(d0)
            _acc_out(dw_ref.at[part], i, jnp.concatenate(taps, axis=0))
            _acc_out(db_ref.at[part], i, bias)

    cur = lambda c0: pl.BlockSpec((FF_TM, FF_TN), lambda j, i: (i, c0 + j))
    prev = lambda c0: pl.BlockSpec((FF_H, FF_TN), lambda j, i: (jnp.maximum(i * (FF_TM // FF_H) - 1, 0), c0 + j))
    nxt = lambda c0: pl.BlockSpec(
        (FF_H, FF_TN), lambda j, i: (jnp.minimum((i + 1) * (FF_TM // FF_H), s_len // FF_H - 1), c0 + j))
    par = lambda r, c0: pl.BlockSpec((r, FF_TN), lambda j, i: (0, c0 + j))
    return _call(
        body, grid=(nj, s_len // FF_TM), name=name, carry=carry,
        in_specs=[cur(0), cur(nj), prev(0), prev(nj), nxt(0), nxt(nj), cur(0), nxt(0),
                  par(3, 0), par(3, nj), par(1, 0), par(1, nj)],
        out_specs=[cur(0), cur(0), pl.BlockSpec((2, 24, FF_TN), lambda j, i: (0, 0, j)),
                   pl.BlockSpec((2, 8, FF_TN), lambda j, i: (0, 0, j))],
        out_shape=[jax.ShapeDtypeStruct((s_len, D_FF), BF16), jax.ShapeDtypeStruct((s_len, D_FF), BF16),
                   jax.ShapeDtypeStruct((2, 24, D_FF), F32), jax.ShapeDtypeStruct((2, 8, D_FF), F32)],
        scratch_shapes=[pltpu.VMEM((FF_TM + 2 * FF_H, FF_TN), F32)] * 2 + [pltpu.VMEM((ext, FF_TN), F32)] * 2,
        args=(up, up, up, up, up, up, dact, dact, conv_w, conv_w, conv_b, conv_b))


def _place():
    x, y, c = lax.axis_index("x"), lax.axis_index("y"), lax.axis_index("c")
    return x, y, c, [(1 - x, y), (x, 1 - y), (1 - x, 1 - y)]


def _remote(src, dst, send_sem, recv_sem, dev):
    return pltpu.make_async_remote_copy(src_ref=src, dst_ref=dst, send_sem=send_sem, recv_sem=recv_sem,
                                        device_id=dev, device_id_type=MESH)


_ANY = pl.BlockSpec(memory_space=pl.ANY)


N_SEMS = 8
SEM_BASE_2 = 4


class _Exchange:
    def __init__(self, operands, out_shapes, start, wait, aliases=None):
        self.operands, self.out_shapes, self.start, self.wait = list(operands), list(out_shapes), start, wait
        self.aliases = aliases or {}


def _sem_scratch():
    return [pltpu.SemaphoreType.DMA((N_SEMS,)), pltpu.SemaphoreType.DMA((N_SEMS,)), pltpu.SemaphoreType.DMA]


def _run_exchange(ex, *, name):
    k, n = len(ex.operands), len(ex.out_shapes)

    def body(*refs):
        ins, outs, sems = refs[:k], refs[k:k + n], refs[k + n:]
        ex.start(ins, outs, *sems)
        ex.wait(ins, outs, *sems)

    return pl.pallas_call(body, name=name, in_specs=[_ANY] * k, out_specs=[_ANY] * n, out_shape=ex.out_shapes,
                          scratch_shapes=_sem_scratch(), input_output_aliases=ex.aliases,
                          compiler_params=_cp(16))(*ex.operands)


def _call(body, *, grid, in_specs, out_specs, out_shape, args, name, scratch_shapes=(), vmem_mb=48, carry=None):
    scratch_shapes = list(scratch_shapes)
    if carry is None:
        return pl.pallas_call(body, grid=grid, name=name, in_specs=in_specs, out_specs=out_specs, out_shape=out_shape,
                              scratch_shapes=scratch_shapes, compiler_params=_cp(vmem_mb))(*args)
    n_in, n_out, n_scr = len(in_specs), len(out_shape), len(scratch_shapes)
    k_in, k_out = len(carry.operands), len(carry.out_shapes)

    def wrapped(*refs):
        ins, refs = refs[:n_in], refs[n_in:]
        cin, refs = refs[:k_in], refs[k_in:]
        outs, refs = refs[:n_out], refs[n_out:]
        cout, refs = refs[:k_out], refs[k_out:]
        scratch, sems = refs[:n_scr], refs[n_scr:]
        ids = [pl.program_id(a) for a in range(len(grid))]
        first = functools.reduce(jnp.logical_and, [i == 0 for i in ids])
        last = functools.reduce(jnp.logical_and, [i == g - 1 for i, g in zip(ids, grid)])

        @pl.when(first)
        def _():
            carry.start(cin, cout, *sems)

        body(*ins, *outs, *scratch)

        @pl.when(last)
        def _():
            carry.wait(cin, cout, *sems)

    aliases = {n_in + i: n_out + o for i, o in carry.aliases.items()}
    return pl.pallas_call(
        wrapped, grid=grid, name=name, in_specs=list(in_specs) + [_ANY] * k_in,
        out_specs=list(out_specs) + [_ANY] * k_out, out_shape=list(out_shape) + carry.out_shapes,
        scratch_shapes=scratch_shapes + _sem_scratch(), input_output_aliases=aliases,
        compiler_params=_cp(vmem_mb))(*args, *carry.operands)


def _half_rows(ref_rows, c):
    half = ref_rows // 2
    return pl.ds(c * half, half)


def _ex_join(a, b):
    ka, na = len(a.operands), len(a.out_shapes)

    def start(ins, outs, *sems):
        a.start(ins[:ka], outs[:na], *sems)
        b.start(ins[ka:], outs[na:], *sems)

    def wait(ins, outs, *sems):
        a.wait(ins[:ka], outs[:na], *sems)
        b.wait(ins[ka:], outs[na:], *sems)

    aliases = dict(a.aliases)
    aliases.update({ka + i: na + o for i, o in b.aliases.items()})
    return _Exchange(a.operands + b.operands, a.out_shapes + b.out_shapes, start, wait, aliases)


def _ex_gather(pack, r0, rl, base=0):
    def copies(ins, outs, send, recv):
        x, y, c, chips = _place()
        rows = _half_rows(rl, c)
        src = ins[0].at[pl.ds(r0 + c * (rl // 2), rl // 2)]
        sends = [_remote(src, outs[0].at[2 * x + y, rows], send.at[base + k], recv.at[base + k], (px, py, c))
                 for k, (px, py) in enumerate(chips)]
        lands = [_remote(src, outs[0].at[2 * px + py, rows], send.at[base + k], recv.at[base + k], (px, py, c))
                 for k, (px, py) in enumerate(chips)]
        return sends, lands

    def mine(ins, outs, local):
        x, y, _, _ = _place()
        return pltpu.make_async_copy(ins[0].at[pl.ds(r0, rl)], outs[0].at[2 * x + y], local)

    def start(ins, outs, send, recv, local):
        mine(ins, outs, local).start()
        for cp in copies(ins, outs, send, recv)[0]:
            cp.start()

    def wait(ins, outs, send, recv, local):
        sends, lands = copies(ins, outs, send, recv)
        for cp in lands:
            cp.wait_recv()
        for cp in sends:
            cp.wait_send()
        mine(ins, outs, local).wait()

    return _Exchange([pack], [jax.ShapeDtypeStruct((4, rl, pack.shape[1]), pack.dtype)], start, wait)


def _ex_gather_forward(g, base=0):
    rl = g.shape[1]

    def copies(outs, send, recv):
        x, y, c, chips = _place()
        slabs = [(outs[0].at[2 * px + py, _half_rows(rl, c)], outs[0].at[2 * px + py, _half_rows(rl, 1 - c)])
                 for px, py in chips]
        sends = [_remote(a, a, send.at[base + k], recv.at[base + k], (x, y, 1 - c)) for k, (a, _) in enumerate(slabs)]
        lands = [_remote(b, b, send.at[base + k], recv.at[base + k], (x, y, 1 - c)) for k, (_, b) in enumerate(slabs)]
        return sends, lands

    def start(ins, outs, send, recv, local):
        for cp in copies(outs, send, recv)[0]:
            cp.start()

    def wait(ins, outs, send, recv, local):
        sends, lands = copies(outs, send, recv)
        for cp in lands:
            cp.wait_recv()
        for cp in sends:
            cp.wait_send()

    return _Exchange([g], [jax.ShapeDtypeStruct(g.shape, g.dtype)], start, wait, aliases={0: 0})


def _ex_swap_halves(gw, base=0):
    nb, rl, d = gw.shape

    def copies(ins, outs, send, recv):
        x, y, c, _ = _place()
        return [_remote(ins[0].at[j, _half_rows(rl, 1 - c)], outs[0].at[j], send.at[base + j], recv.at[base + j],
                        (x, y, 1 - c)) for j in range(nb)]

    def start(ins, outs, send, recv, local):
        for cp in copies(ins, outs, send, recv):
            cp.start()

    def wait(ins, outs, send, recv, local):
        for cp in copies(ins, outs, send, recv):
            cp.wait()

    return _Exchange([gw], [jax.ShapeDtypeStruct((nb, rl // 2, d), gw.dtype)], start, wait)


def _chip_sum(gw, got, c_arr, *, name):
    nchip, half, d = got.shape
    tr = _tile(half, 512)

    def body(c_ref, a_ref, b_ref, o32_ref, o16_ref):
        s = a_ref[...] + b_ref[...]
        o32_ref[...] = s
        o16_ref[...] = s.astype(BF16)

    blk = pl.BlockSpec((None, tr, d), lambda j, i, c_ref: (j, i, 0))
    return pl.pallas_call(
        body, name=name,
        grid_spec=pltpu.PrefetchScalarGridSpec(
            num_scalar_prefetch=1, grid=(nchip, half // tr),
            in_specs=[pl.BlockSpec((None, tr, d), lambda j, i, c_ref: (j, c_ref[0] * (half // tr) + i, 0)), blk],
            out_specs=[blk, blk]),
        out_shape=[jax.ShapeDtypeStruct((nchip, half, d), F32), jax.ShapeDtypeStruct((nchip, half, d), BF16)],
        compiler_params=_cp())(c_arr, gw, got)


def _ex_scatter(s16, base=0):
    def copies(ins, outs, send, recv):
        x, y, c, chips = _place()
        return [_remote(ins[0].at[2 * px + py], outs[0].at[k], send.at[base + k], recv.at[base + k], (px, py, c))
                for k, (px, py) in enumerate(chips)]

    def start(ins, outs, send, recv, local):
        for cp in copies(ins, outs, send, recv):
            cp.start()

    def wait(ins, outs, send, recv, local):
        for cp in copies(ins, outs, send, recv):
            cp.wait()

    return _Exchange([s16], [jax.ShapeDtypeStruct((3,) + s16.shape[1:], s16.dtype)], start, wait)


def _mesh_sum(s32, got, j_arr, *, name):
    _, rl, d = s32.shape
    tr = _tile(rl, 512)

    def body(j_ref, a_ref, b_ref, o_ref):
        o_ref[...] = ((a_ref[...] + b_ref[0].astype(F32)) + b_ref[1].astype(F32)) + b_ref[2].astype(F32)

    return pl.pallas_call(
        body, name=name,
        grid_spec=pltpu.PrefetchScalarGridSpec(
            num_scalar_prefetch=1, grid=(rl // tr,),
            in_specs=[pl.BlockSpec((None, tr, d), lambda i, j_ref: (j_ref[0], i, 0)),
                      pl.BlockSpec((3, tr, d), lambda i, j_ref: (0, i, 0))],
            out_specs=pl.BlockSpec((tr, d), lambda i, j_ref: (i, 0))),
        out_shape=jax.ShapeDtypeStruct((rl, d), F32), compiler_params=_cp())(j_arr, s32, got)


def _ex_share_halves(ghalf):
    half, d = ghalf.shape

    def copies(ins, outs, send, recv, local):
        x, y, c, _ = _place()
        there = outs[0].at[_half_rows(2 * half, c)]
        back = outs[0].at[_half_rows(2 * half, 1 - c)]
        return (_remote(ins[0], there, send.at[0], recv.at[0], (x, y, 1 - c)),
                _remote(ins[0], back, send.at[0], recv.at[0], (x, y, 1 - c)), pltpu.make_async_copy(ins[0], there, local))

    def start(ins, outs, send, recv, local):
        out, _, mine = copies(ins, outs, send, recv, local)
        mine.start()
        out.start()

    def wait(ins, outs, send, recv, local):
        out, back, mine = copies(ins, outs, send, recv, local)
        back.wait_recv()
        out.wait_send()
        mine.wait()

    return _Exchange([ghalf], [jax.ShapeDtypeStruct((2 * half, d), ghalf.dtype)], start, wait)


class _ReduceScatter:
    def __init__(self, gw, c_arr, j_arr, tag):
        self.gw, self.c_arr, self.j_arr, self.tag = gw, c_arr, j_arr, tag

    def swap(self, base=0):
        return _ex_swap_halves(self.gw, base)

    def after_swap(self, got, base=0):
        self.s32, s16 = _chip_sum(self.gw, got, self.c_arr, name=f"rs_chip_sum{self.tag}")
        return _ex_scatter(s16, base)

    def after_scatter(self, got16):
        ghalf = _mesh_sum(self.s32, got16, self.j_arr, name=f"rs_mesh_sum{self.tag}")
        return _run_exchange(_ex_share_halves(ghalf), name=f"rs_share{self.tag}")[0]

    def run(self):
        got, = _run_exchange(self.swap(), name=f"rs_swap{self.tag}")
        got16, = _run_exchange(self.after_swap(got), name=f"rs_scatter{self.tag}")
        return self.after_scatter(got16)


def _all_reduce_small(vec, *, name):
    rows, d = vec.shape

    def body(x_ref, o_ref, gat, send_sems, recv_sems, local_sem):
        x, y, c, chips = _place()
        me, sibling = (x, y, c), (x, y, 1 - c)

        def slot(px, py, pc):
            return gat.at[4 * px + 2 * py + pc]

        def copy(k, block, to, src=None):
            return _remote(slot(*block) if src is None else src, slot(*block), send_sems.at[k], recv_sems.at[k], to)

        mine = pltpu.make_async_copy(x_ref, slot(*me), local_sem)
        mine.start()
        first = [copy(0, me, sibling, src=x_ref)]
        first += [copy(1 + j, me, (*chip, c), src=x_ref) for j, chip in enumerate(chips)]
        for cp in first:
            cp.start()
        passed = [copy(4 + j, (*chip, c), sibling) for j, chip in enumerate(chips)]
        for j, chip in enumerate(chips):
            copy(1 + j, (*chip, c), me).wait_recv()
            passed[j].start()
        copy(0, sibling, me).wait_recv()
        for j, chip in enumerate(chips):
            copy(4 + j, (*chip, 1 - c), me).wait_recv()
        for cp in first + passed:
            cp.wait_send()
        mine.wait()
        acc = gat[0]
        for dev in range(1, 8):
            acc = acc + gat[dev]
        o_ref[...] = acc

    vm = pl.BlockSpec(memory_space=pltpu.VMEM)
    return pl.pallas_call(
        body, name=name, in_specs=[vm], out_specs=vm, out_shape=jax.ShapeDtypeStruct((rows, d), F32),
        scratch_shapes=[pltpu.VMEM((8, rows, d), F32), pltpu.SemaphoreType.DMA((7,)), pltpu.SemaphoreType.DMA((7,)),
                        pltpu.SemaphoreType.DMA],
        compiler_params=_cp(32))(vec)


COL_SHARDED = ("w_in", "ffn_w_up")


def _to_pack_rows(name, shard):
    return shard.reshape(-1, D_MODEL)


def _full_from_blocks(name, blocks):
    rows = blocks.shape[1]
    if name in COL_SHARDED:
        return blocks.reshape(4, D_MODEL, rows).transpose(1, 0, 2).reshape(D_MODEL, 4 * rows)
    return blocks.reshape(4 * rows, D_MODEL)


def _blocks_from_full(name, full):
    if name in COL_SHARDED:
        cols = full.shape[1] // 4
        return full.reshape(D_MODEL, 4, cols).transpose(1, 0, 2).reshape(4, cols, D_MODEL)
    return full.reshape(4, full.shape[0] // 4, D_MODEL)


def _row(v):
    return v.reshape(1, -1)


SMALL = (("mix_norm_pre", (1024,), None), ("cv_w", (31, 256), 1), ("cv_b", (256,), None), ("cv_ln_g", (256,), None),
         ("cv_ln_b", (256,), None), ("cv_pw_w", (256, 256), 0), ("cv_pw_b", (256,), None),
         ("mix_norm_post", (1024,), None), ("x_norm_pre", (1024,), None), ("mem_norm", (1024,), None),
         ("x_norm_post", (1024,), None), ("ffn_norm_pre", (1024,), None), ("ffn_conv_w", (3, 5632), 1),
         ("ffn_conv_b", (5632,), None), ("ffn_norm_post", (1024,), None))
BIG = tuple(n for n, _ in PACK_ROWS)
WEIGHT_ORDER = ("mix_norm_pre", "w_in", "cv_w", "cv_b", "cv_ln_g", "cv_ln_b", "cv_pw_w", "cv_pw_b", "w_out",
                "mix_norm_post", "x_norm_pre", "mem_norm", "x_wq", "x_wk", "x_wv", "x_wo", "x_norm_post",
                "ffn_norm_pre", "ffn_w_up", "ffn_conv_w", "ffn_conv_b", "ffn_w_down", "ffn_norm_post")


def _flat_rows(parts):
    v = jnp.concatenate([p.reshape(-1) for p in parts])
    rows = -(-v.shape[0] // (8 * D_MODEL)) * 8
    return jnp.pad(v, (0, rows * D_MODEL - v.shape[0])).reshape(rows, D_MODEL)


REST_GROUP = ("w_in", "w_out")
XA_GROUP = ("x_wq", "x_wk", "x_wv", "x_wo")
FFN_GROUP = ("ffn_w_up", "ffn_w_down")


class _Weights:
    FIRST = (0, 704)
    OWN = ((704, 768), (1472, 1920), (3392, 704))
    NEXT = ((0, 960), (960, 1024), (1984, 1408), (3392, 704))
    SLOTS = ("mix_in", "sb_fwd", "dl_fwd", "dl_mix", "ffn_up", "ffn_gate", "ffn_down")

    def __init__(self, packs):
        self.packs, self.pieces, self.landed, self.plan = packs, {}, None, {}
        for slot, piece in zip(self.SLOTS[:3], self.OWN):
            self.plan[(0, slot)] = (0,) + piece
        for l in range(len(packs) - 1):
            for slot, piece in zip(self.SLOTS[3:], self.NEXT):
                self.plan[(l, slot)] = (l + 1,) + piece
        first = _run_exchange(_ex_gather(packs[0], *self.FIRST), name="gather_first")[0]
        self.pieces[(0,) + self.FIRST] = _run_exchange(_ex_gather_forward(first), name="gather_first_forward")[0]

    def ride(self, layer, slot, call):
        start, todo, ex = self.plan.get((layer, slot)), [], None
        if start is not None:
            ex = _ex_gather(self.packs[start[0]], start[1], start[2])
            todo.append(("landed", start))
        if self.landed is not None:
            key, buf = self.landed
            forward = _ex_gather_forward(buf, SEM_BASE_2 if ex is not None else 0)
            ex = forward if ex is None else _ex_join(ex, forward)
            todo.append(("piece", key))
            self.landed = None
        outs = list(call(carry=ex))
        n = len(outs) - len(todo)
        for (kind, key), buf in zip(todo, outs[n:]):
            if kind == "landed":
                self.landed = (key, buf)
            else:
                self.pieces[key] = buf
        return outs[:n]

    def weight(self, layer, name):
        off = 0
        for n, rows in PACK_ROWS:
            if n == name:
                break
            off += rows
        for (l, r0, nrows), buf in self.pieces.items():
            if l == layer and r0 <= off < r0 + nrows:
                return _full_from_blocks(name, buf[:, off - r0:off - r0 + rows, :])
        raise KeyError(f"{name} of layer {layer} is not gathered yet")


class _Params:
    def __init__(self, weights, layer, small):
        self.weights, self.layer, self.small, self.cache = weights, layer, small, {}

    def __getitem__(self, name):
        if name in self.small:
            return self.small[name]
        if name not in self.cache:
            self.cache[name] = self.weights.weight(self.layer, name)
        return self.cache[name]


def _layer_fwd(h0, mem, p, cos, sin, tag, ride):
    sv = {"h0": h0}
    n1, u = ride("mix_in", functools.partial(_rms_mm, h0, _row(p["mix_norm_pre"]), p["w_in"], tm=1024, tn=1408,
                                             out_dtype=F32, name=f"mix_in{tag}"))
    a_out, = ride("sb_fwd", functools.partial(_sb_fwd, u, name=f"sb_fwd{tag}"))
    b_out, c = _cv_fwd(u, p["cv_w"], _row(p["cv_b"]), _row(p["cv_ln_g"]), _row(p["cv_ln_b"]),
                       p["cv_pw_w"].astype(BF16), _row(p["cv_pw_b"]), name=f"cv_fwd{tag}")
    qkv = _rope_perm(u, cos, sin, name=f"rope_perm{tag}")
    o_p, l_p = ride("dl_fwd", functools.partial(_dl_fwd, qkv, name=f"dl_fwd{tag}"))
    c_out, o_dl, lse = ride("dl_mix", functools.partial(_dl_mix, o_p, l_p, name=f"dl_mix{tag}"))
    cat = jnp.concatenate([a_out, b_out, c_out], axis=1)
    y1, h1 = _mm_post(cat, p["w_out"], h0, _row(p["mix_norm_post"]), tm=256, name=f"mix_out{tag}")
    sv.update(n1=n1, u=u, c=c, qkv=qkv, o_dl=o_dl, lse=lse, cat=cat, y1=y1, h1=h1)

    n2, q = _rms_mm(h1, _row(p["x_norm_pre"]), p["x_wq"], tm=512, tn=1024, out_dtype=BF16, name=f"xa_q{tag}")
    wkv = jnp.concatenate([p["x_wk"], p["x_wv"]], axis=1)
    mem_n, kv = _rms_mm(mem, _row(p["mem_norm"]), wkv, tm=mem.shape[0], tn=1024, out_dtype=BF16, name=f"xa_kv{tag}")
    k, v = kv[:, :D_MODEL], kv[:, D_MODEL:]
    o_x = _xa_fwd(q, k, v, name=f"xa_fwd{tag}")
    y2, h2 = _mm_post(o_x, p["x_wo"], h1, _row(p["x_norm_post"]), tm=256, name=f"xa_out{tag}")
    sv.update(n2=n2, q=q, mem_n=mem_n, k=k, v=v, o_x=o_x, y2=y2, h2=h2, wkv=wkv)

    n3, up = ride("ffn_up", functools.partial(_rms_mm, h2, _row(p["ffn_norm_pre"]), p["ffn_w_up"], tm=1024, tn=1408,
                                              out_dtype=F32, name=f"ffn_up{tag}"))
    act, = ride("ffn_gate", functools.partial(_ff_gate_fwd, up, p["ffn_conv_w"], _row(p["ffn_conv_b"]),
                                              name=f"ffn_gate{tag}"))
    y3, h3 = ride("ffn_down", functools.partial(_mm_post, act, p["ffn_w_down"], h2, _row(p["ffn_norm_post"]), tm=256,
                                                name=f"ffn_down{tag}"))
    sv.update(n3=n3, up=up, act=act, y3=y3)
    return h3, sv


def _layer_bwd(dh3, mem, p, sv, cos, sin, tag, riding, new_rs):
    g = {}
    s8 = lambda part: part.sum(axis=0)
    rode = None

    dy3, dgp = _rms_bwd(sv["y3"], _row(p["ffn_norm_post"]), dh3, None, out_dtype=BF16, tm=256, name=f"ffn_post_b{tag}")
    g["ffn_norm_post"] = s8(dgp)
    dact = _mm_nt(dy3, p["ffn_w_down"], tm=512, tn=1408, out_dtype=F32, name=f"ffn_down_bx{tag}")
    g["ffn_w_down"] = _mm_tn(sv["act"], dy3, tk=1408, tn=1024, tm=512, name=f"ffn_down_bw{tag}")
    dgu, dvu, dcw, dcb, *got = _ff_gate_bwd(sv["up"], dact, p["ffn_conv_w"], _row(p["ffn_conv_b"]),
                                            name=f"ffn_gate_b{tag}", carry=riding.swap() if riding else None)
    scatter = riding.after_swap(got[0]) if riding else None
    g["ffn_conv_w"] = jnp.concatenate([dcw[0], dcw[1]], axis=1).reshape(3, 8, 2 * D_FF).sum(axis=1)
    g["ffn_conv_b"] = jnp.concatenate([dcb[0], dcb[1]], axis=1).sum(axis=0)
    dup = jnp.concatenate([dgu, dvu], axis=1)
    dn3 = _mm_nt(dup, p["ffn_w_up"], tm=256, tn=512, out_dtype=F32, name=f"ffn_up_bx{tag}")
    g["ffn_w_up"] = _mm_tn(sv["n3"], dup, tk=512, tn=1408, tm=512, name=f"ffn_up_bw{tag}")
    ffn_rs = new_rs(FFN_GROUP, g, f"{tag}_ffn")
    dh2, dgp = _rms_bwd(sv["h2"], _row(p["ffn_norm_pre"]), dn3, dh3, out_dtype=F32, tm=256, name=f"ffn_pre_b{tag}")
    g["ffn_norm_pre"] = s8(dgp)

    dy2, dgp = _rms_bwd(sv["y2"], _row(p["x_norm_post"]), dh2, None, out_dtype=BF16, tm=256, name=f"xa_post_b{tag}")
    g["x_norm_post"] = s8(dgp)
    do_x = _mm_nt(dy2, p["x_wo"], tm=512, tn=1024, out_dtype=BF16, name=f"xa_out_bx{tag}")
    g["x_wo"] = _mm_tn(sv["o_x"], dy2, tk=512, tn=1024, tm=512, name=f"xa_out_bw{tag}")
    dq, dk, dv, got = _xa_bwd(sv["q"], sv["k"], sv["v"], do_x, name=f"xa_bwd{tag}", carry=ffn_rs.swap())
    ffn_scatter = ffn_rs.after_swap(got)
    dn2 = _mm_nt(dq, p["x_wq"], tm=512, tn=1024, out_dtype=F32, name=f"xa_q_bx{tag}")
    g["x_wq"] = _mm_tn(sv["n2"], dq, tk=512, tn=1024, tm=512, name=f"xa_q_bw{tag}")
    dkv = jnp.concatenate([dk, dv], axis=1).astype(BF16)
    nm = mem.shape[0]
    dmem_n = _mm_nt(dkv, sv["wkv"], tm=nm, tn=1024, out_dtype=F32, name=f"xa_kv_bx{tag}")
    dwkv = _mm_tn(sv["mem_n"], dkv, tk=512, tn=2048, tm=nm, name=f"xa_kv_bw{tag}")
    g["x_wk"], g["x_wv"] = dwkv[:, :D_MODEL], dwkv[:, D_MODEL:]
    _, dgp = _rms_bwd(mem, _row(p["mem_norm"]), dmem_n, None, out_dtype=BF16, tm=nm, name=f"xa_mem_b{tag}")
    g["mem_norm"] = s8(dgp)
    xa_rs = new_rs(XA_GROUP, g, f"{tag}_xa")
    dh1, dgp, got = _rms_bwd(sv["h1"], _row(p["x_norm_pre"]), dn2, dh2, out_dtype=F32, tm=256, name=f"xa_pre_b{tag}",
                             carry=xa_rs.swap())
    xa_scatter = xa_rs.after_swap(got, SEM_BASE_2 if riding else 0)
    g["x_norm_pre"] = s8(dgp)

    dy1, dgp = _rms_bwd(sv["y1"], _row(p["mix_norm_post"]), dh1, None, out_dtype=BF16, tm=256, name=f"mix_post_b{tag}")
    g["mix_norm_post"] = s8(dgp)
    dcat = _mm_nt(dy1, p["w_out"], tm=512, tn=1024, out_dtype=F32, name=f"mix_out_bx{tag}")
    g["w_out"] = _mm_tn(sv["cat"], dy1, tk=512, tn=1024, tm=512, name=f"mix_out_bw{tag}")
    u = sv["u"]
    dq_sb, dk_sb, dv_sb, *got = _sb_bwd(u, dcat, name=f"sb_bwd{tag}",
                                        carry=_ex_join(scatter, xa_scatter) if riding else xa_scatter)
    if riding:
        rode = riding.after_scatter(got[0])
    xa_rows = xa_rs.after_scatter(got[-1])
    pw_b16 = p["cv_pw_w"].astype(BF16)
    dc, dpw, vec = _cv_bwd_local(sv["c"], dcat, _row(p["cv_ln_g"]), _row(p["cv_ln_b"]), pw_b16, name=f"cv_bwd_a{tag}")
    g["cv_pw_w"] = dpw
    vec = vec.reshape(3, 8, CV_W).sum(axis=1)
    g["cv_pw_b"], g["cv_ln_g"], g["cv_ln_b"] = vec[0], vec[1], vec[2]
    du_cv, dcw, dcb = _cv_bwd_conv(u, dc, p["cv_w"], name=f"cv_bwd_b{tag}")
    g["cv_w"] = dcw.reshape(CV_K, 8, CV_W).sum(axis=1)
    g["cv_b"] = dcb.sum(axis=0)
    dop, stats = _dl_bwd_prep(dcat, sv["o_dl"], sv["lse"], name=f"dl_prep_b{tag}")
    cur, prev, got = _dl_bwd(sv["qkv"], dop, stats, name=f"dl_bwd{tag}", carry=ffn_scatter)
    ffn_rows = ffn_rs.after_scatter(got)
    du_dl = _dl_bwd_finish(cur, prev, cos, sin, name=f"dl_fin_b{tag}")
    du = jnp.concatenate([dq_sb.astype(BF16), dk_sb.astype(BF16), dv_sb.astype(BF16), du_cv, du_dl], axis=1)
    dn1 = _mm_nt(du, p["w_in"], tm=512, tn=512, out_dtype=F32, name=f"mix_in_bx{tag}")
    g["w_in"] = _mm_tn(sv["n1"], du, tk=512, tn=1408, tm=512, name=f"mix_in_bw{tag}")
    dh0, dgp = _rms_bwd(sv["h0"], _row(p["mix_norm_pre"]), dn1, dh1, out_dtype=F32, tm=256, name=f"mix_pre_b{tag}")
    g["mix_norm_pre"] = s8(dgp)
    return dh0, g, (xa_rows, ffn_rows), rode


def _step(x, mem, positions, loss_target, w, m, v):
    depth = w["w_in"].shape[0]
    xi, yi, ci = lax.axis_index("x"), lax.axis_index("y"), lax.axis_index("c")
    chip = 2 * xi + yi
    h = x[0]
    mem0 = mem[0]
    s_len = h.shape[0]

    packs = [jnp.concatenate([_to_pack_rows(n, w[n][l]) for n in BIG], axis=0).astype(BF16) for l in range(depth)]

    small_w = []
    for l in range(depth):
        for n, shape, axis in SMALL:
            if axis is not None:
                full = jnp.zeros(shape, F32)
                full = lax.dynamic_update_slice_in_dim(full, w[n][l], chip * w[n][l].shape[axis], axis)
                small_w.append(full * jnp.where(ci == 0, 1.0, 0.0))
    small_w_sum = _all_reduce_small(_flat_rows(small_w), name="gather_small_weights")
    small_full, off = [{} for _ in range(depth)], 0
    for l in range(depth):
        for n, shape, axis in SMALL:
            if axis is not None:
                size = int(np.prod(shape))
                small_full[l][n] = small_w_sum.reshape(-1)[off:off + size].reshape(shape)
                off += size
    weights = _Weights(packs)
    params = [_Params(weights, l, {n: small_full[l].get(n, w[n][l]) for n, _, _ in SMALL}) for l in range(depth)]

    inv_freq = ROPE_THETA ** (-jnp.arange(HD // 2, dtype=F32) / (HD // 2))
    cos, sin = _rope_tables(positions.reshape(s_len, 1), jnp.tile(inv_freq, 4).reshape(1, LANES), name="rope_tables")

    saved = []
    for l in range(depth):
        h, sv = _layer_fwd(h, mem0, params[l], cos, sin, f"_l{l}", functools.partial(weights.ride, l))
        saved.append(sv)
    dh, sq = _loss_grad(h, loss_target[0], tm=256, name="loss_grad")
    loss = lax.psum(0.5 * jnp.sum(sq) / D_MODEL, ("x", "y", "c"))

    c_arr, j_arr = jnp.reshape(ci, (1,)).astype(jnp.int32), jnp.reshape(chip, (1,)).astype(jnp.int32)

    def new_rs(names, g, tag):
        gw = jnp.concatenate([_blocks_from_full(n, g[n]) for n in names], axis=1)
        return _ReduceScatter(gw, c_arr, j_arr, tag)

    grads, later_rows, rest_rows, pending = [None] * depth, [None] * depth, [None] * depth, None
    for l in reversed(range(depth)):
        dh, grads[l], later_rows[l], rode = _layer_bwd(dh, mem0, params[l], saved[l], cos, sin, f"_l{l}", pending, new_rs)
        if pending is not None:
            rest_rows[l + 1] = rode
        pending = new_rs(REST_GROUP, grads[l], f"_l{l}_rest")
    rest_rows[0] = pending.run()
    gfull = [jnp.concatenate([rest_rows[l], *later_rows[l]], axis=0) for l in range(depth)]
    grad_x = dh[None]

    out_g, out_d, out_m, out_v = {}, {}, {}, {}
    off = 0
    for n, rows in PACK_ROWS:
        shard_shape = w[n].shape
        g_n = jnp.stack([gl[off:off + rows, :] for gl in gfull]).reshape(shard_shape)
        off += rows
        flat = lambda a: a.reshape(-1, shard_shape[-1])
        d_n, m_n, v_n = _adamw(flat(w[n]), flat(g_n), flat(m[n]), flat(v[n]), name=f"adamw_{n}")
        out_g[n], out_d[n], out_m[n], out_v[n] = g_n, d_n.reshape(shard_shape), m_n.reshape(shard_shape), v_n.reshape(shard_shape)

    g_small = _all_reduce_small(_flat_rows([grads[l][n] for l in range(depth) for n, _, _ in SMALL]),
                                name="all_reduce_small_grads").reshape(-1)
    local_g, off = {}, 0
    for l in range(depth):
        for n, shape, axis in SMALL:
            size = int(np.prod(shape))
            full = g_small[off:off + size].reshape(shape)
            off += size
            if axis is not None:
                blk = w[n].shape[1 + axis]
                full = lax.dynamic_slice_in_dim(full, chip * blk, blk, axis)
            local_g.setdefault(n, []).append(full)
    names = [n for n, _, _ in SMALL]
    g_loc = {n: jnp.stack(local_g[n]) for n in names}
    d_s, m_s, v_s = _adamw(_flat_rows([w[n] for n in names]), _flat_rows([g_loc[n] for n in names]),
                           _flat_rows([m[n] for n in names]), _flat_rows([v[n] for n in names]), name="adamw_small")
    off = 0
    for n in names:
        size = int(np.prod(w[n].shape))
        take = lambda a: a.reshape(-1)[off:off + size].reshape(w[n].shape)
        out_g[n], out_d[n], out_m[n], out_v[n] = g_loc[n], take(d_s), take(m_s), take(v_s)
        off += size

    outs = [loss, grad_x]
    for group in (out_g, out_d, out_m, out_v):
        outs += [group[n] for n in WEIGHT_ORDER]
    return tuple(outs)


def kernel(x, mem, positions, mix_norm_pre, w_in, cv_w, cv_b, cv_ln_g, cv_ln_b, cv_pw_w, cv_pw_b, w_out, mix_norm_post, x_norm_pre, mem_norm, x_wq, x_wk, x_wv, x_wo, x_norm_post, ffn_norm_pre, ffn_w_up, ffn_conv_w, ffn_conv_b, ffn_w_down, ffn_norm_post, loss_target, m_mix_norm_pre, m_w_in, m_cv_w, m_cv_b, m_cv_ln_g, m_cv_ln_b, m_cv_pw_w, m_cv_pw_b, m_w_out, m_mix_norm_post, m_x_norm_pre, m_mem_norm, m_x_wq, m_x_wk, m_x_wv, m_x_wo, m_x_norm_post, m_ffn_norm_pre, m_ffn_w_up, m_ffn_conv_w, m_ffn_conv_b, m_ffn_w_down, m_ffn_norm_post, v_mix_norm_pre, v_w_in, v_cv_w, v_cv_b, v_cv_ln_g, v_cv_ln_b, v_cv_pw_w, v_cv_pw_b, v_w_out, v_mix_norm_post, v_x_norm_pre, v_mem_norm, v_x_wq, v_x_wk, v_x_wv, v_x_wo, v_x_norm_post, v_ffn_norm_pre, v_ffn_w_up, v_ffn_conv_w, v_ffn_conv_b, v_ffn_w_down, v_ffn_norm_post):
    w = dict(zip(WEIGHT_ORDER, (mix_norm_pre, w_in, cv_w, cv_b, cv_ln_g, cv_ln_b, cv_pw_w, cv_pw_b, w_out, mix_norm_post, x_norm_pre, mem_norm, x_wq, x_wk, x_wv, x_wo, x_norm_post, ffn_norm_pre, ffn_w_up, ffn_conv_w, ffn_conv_b, ffn_w_down, ffn_norm_post)))
    m = dict(zip(WEIGHT_ORDER, (m_mix_norm_pre, m_w_in, m_cv_w, m_cv_b, m_cv_ln_g, m_cv_ln_b, m_cv_pw_w, m_cv_pw_b, m_w_out, m_mix_norm_post, m_x_norm_pre, m_mem_norm, m_x_wq, m_x_wk, m_x_wv, m_x_wo, m_x_norm_post, m_ffn_norm_pre, m_ffn_w_up, m_ffn_conv_w, m_ffn_conv_b, m_ffn_w_down, m_ffn_norm_post)))
    v = dict(zip(WEIGHT_ORDER, (v_mix_norm_pre, v_w_in, v_cv_w, v_cv_b, v_cv_ln_g, v_cv_ln_b, v_cv_pw_w, v_cv_pw_b, v_w_out, v_mix_norm_post, v_x_norm_pre, v_mem_norm, v_x_wq, v_x_wk, v_x_wv, v_x_wo, v_x_norm_post, v_ffn_norm_pre, v_ffn_w_up, v_ffn_conv_w, v_ffn_conv_b, v_ffn_w_down, v_ffn_norm_post)))
    return _step(x, mem, positions, loss_target, w, m, v)
```

```python
import functools

import jax
import jax.numpy as jnp
import numpy as np
from jax import lax
from jax.experimental import pallas as pl
from jax.experimental.pallas import tpu as pltpu

F32, BF16 = jnp.float32, jnp.bfloat16
MESH = pl.DeviceIdType.MESH
EPS = 1e-6
LANES = 128
BLK = 128
HD = 64
D_MODEL = 1024
D_FF = 2816
SB_W, CV_W, DL_W = 256, 256, 512
CV_K = 31
ROPE_THETA = 10000.0
DILATIONS = (1, 4, 16)
X_HEADS, X_HD = 4, 256
ADAM_LR, ADAM_B1, ADAM_B2, ADAM_EPS, ADAM_WD, ADAM_STEP = 0.001, 0.9, 0.999, 1e-08, 0.01, 10
NEG_INF = float("-inf")
MIB = 1 << 20

PACK_ROWS = (("w_in", 704), ("w_out", 256), ("x_wq", 256), ("x_wk", 256), ("x_wv", 256), ("x_wo", 256),
             ("ffn_w_up", 1408), ("ffn_w_down", 704))
PACK_RL = sum(r for _, r in PACK_ROWS)


def _cp(vmem_mb=48):
    return pltpu.CompilerParams(vmem_limit_bytes=vmem_mb * MIB)


def _dot(a, b):
    return jnp.dot(a, b, preferred_element_type=F32)


def _dot_nt(a, b):
    return lax.dot_general(a, b, (((1,), (1,)), ((), ())), preferred_element_type=F32)


def _dot_tn(a, b):
    return lax.dot_general(a, b, (((0,), (0,)), ((), ())), preferred_element_type=F32)


def _dot_hilo(x, m):
    hi = x.astype(BF16)
    lo = (x - hi.astype(F32)).astype(BF16)
    return _dot(hi, m) + _dot(lo, m)


def _rowsum8(x):
    t, c = x.shape
    return x.reshape(t // 8, 8, c).sum(axis=0)


def _acc_out(ref, i, val):
    @pl.when(i == 0)
    def _():
        ref[...] = val

    @pl.when(i > 0)
    def _():
        ref[...] += val


def _tile(n, cap, mult=8):
    t = min(n, cap)
    while n % t or t % mult:
        t -= 1
    return t


def _rms_mm(x, g, w, *, tm, tn, out_dtype, name, carry=None):
    m, d = x.shape
    n_out = w.shape[1]

    def body(x_ref, g_ref, w_ref, n_ref, o_ref):
        @pl.when(pl.program_id(1) == 0)
        def _():
            xv = x_ref[...]
            r = lax.rsqrt(jnp.mean(xv * xv, axis=-1, keepdims=True) + EPS)
            n_ref[...] = (xv * r * g_ref[...]).astype(BF16)

        o_ref[...] = _dot(n_ref[...], w_ref[...]).astype(out_dtype)

    return _call(
        body, grid=(m // tm, n_out // tn), name=name, carry=carry,
        in_specs=[pl.BlockSpec((tm, d), lambda i, j: (i, 0)), pl.BlockSpec((1, d), lambda i, j: (0, 0)),
                  pl.BlockSpec((d, tn), lambda i, j: (0, j))],
        out_specs=[pl.BlockSpec((tm, d), lambda i, j: (i, 0)), pl.BlockSpec((tm, tn), lambda i, j: (i, j))],
        out_shape=[jax.ShapeDtypeStruct((m, d), BF16), jax.ShapeDtypeStruct((m, n_out), out_dtype)],
        args=(x, g, w))


def _mm_post(a, w, h, g, *, tm, name, carry=None):
    m, k = a.shape
    d = w.shape[1]

    def body(a_ref, w_ref, h_ref, g_ref, y_ref, ho_ref):
        y = _dot(a_ref[...], w_ref[...])
        y_ref[...] = y
        r = lax.rsqrt(jnp.mean(y * y, axis=-1, keepdims=True) + EPS)
        ho_ref[...] = h_ref[...] + y * r * g_ref[...]

    return _call(
        body, grid=(m // tm,), name=name, carry=carry,
        in_specs=[pl.BlockSpec((tm, k), lambda i: (i, 0)), pl.BlockSpec((k, d), lambda i: (0, 0)),
                  pl.BlockSpec((tm, d), lambda i: (i, 0)), pl.BlockSpec((1, d), lambda i: (0, 0))],
        out_specs=[pl.BlockSpec((tm, d), lambda i: (i, 0)), pl.BlockSpec((tm, d), lambda i: (i, 0))],
        out_shape=[jax.ShapeDtypeStruct((m, d), F32), jax.ShapeDtypeStruct((m, d), F32)],
        args=(a, w, h, g))


def _mm_nt(a, w, *, tm, tn, out_dtype, name):
    m, k = a.shape
    n_out = w.shape[0]

    def body(a_ref, w_ref, o_ref):
        o_ref[...] = _dot_nt(a_ref[...], w_ref[...]).astype(out_dtype)

    return pl.pallas_call(
        body, grid=(n_out // tn, m // tm), name=name,
        in_specs=[pl.BlockSpec((tm, k), lambda j, i: (i, 0)), pl.BlockSpec((tn, k), lambda j, i: (j, 0))],
        out_specs=pl.BlockSpec((tm, tn), lambda j, i: (i, j)),
        out_shape=jax.ShapeDtypeStruct((m, n_out), out_dtype),
        compiler_params=_cp())(a, w)


def _mm_tn(x, dy, *, tk, tn, tm, name):
    m, k = x.shape
    n_out = dy.shape[1]

    def body(x_ref, d_ref, o_ref):
        _acc_out(o_ref, pl.program_id(2), _dot_tn(x_ref[...], d_ref[...]))

    return pl.pallas_call(
        body, grid=(k // tk, n_out // tn, m // tm), name=name,
        in_specs=[pl.BlockSpec((tm, tk), lambda a, b, c: (c, a)), pl.BlockSpec((tm, tn), lambda a, b, c: (c, b))],
        out_specs=pl.BlockSpec((tk, tn), lambda a, b, c: (a, b)),
        out_shape=jax.ShapeDtypeStruct((k, n_out), F32),
        compiler_params=_cp())(x, dy)


def _rms_bwd(x, g, dout, res, *, out_dtype, tm, name, carry=None):
    m, d = x.shape
    has_res = res is not None

    def body(*refs):
        if has_res:
            x_ref, g_ref, d_ref, r_ref, dx_ref, dg_ref = refs
        else:
            x_ref, g_ref, d_ref, dx_ref, dg_ref = refs
        xv = x_ref[...]
        dv = d_ref[...].astype(F32)
        r = lax.rsqrt(jnp.mean(xv * xv, axis=-1, keepdims=True) + EPS)
        xh = xv * r
        dxh = dv * g_ref[...]
        dx = r * (dxh - xh * jnp.mean(dxh * xh, axis=-1, keepdims=True))
        if has_res:
            dx = dx + r_ref[...]
        dx_ref[...] = dx.astype(out_dtype)
        _acc_out(dg_ref, pl.program_id(0), _rowsum8(dv * xh))

    row = pl.BlockSpec((tm, d), lambda i: (i, 0))
    ins = [row, pl.BlockSpec((1, d), lambda i: (0, 0)), row] + ([row] if has_res else [])
    args = (x, g, dout) + ((res,) if has_res else ())
    return _call(
        body, grid=(m // tm,), name=name, carry=carry, in_specs=ins,
        out_specs=[row, pl.BlockSpec((8, d), lambda i: (0, 0))],
        out_shape=[jax.ShapeDtypeStruct((m, d), out_dtype), jax.ShapeDtypeStruct((8, d), F32)], args=args)


def _loss_grad(h, tgt, *, tm, name):
    m, d = h.shape

    def body(h_ref, t_ref, dh_ref, p_ref):
        e = h_ref[...] - t_ref[...]
        dh_ref[...] = e / d
        _acc_out(p_ref, pl.program_id(0), _rowsum8(e * e))

    row = pl.BlockSpec((tm, d), lambda i: (i, 0))
    return pl.pallas_call(
        body, grid=(m // tm,), name=name, in_specs=[row, row],
        out_specs=[row, pl.BlockSpec((8, d), lambda i: (0, 0))],
        out_shape=[jax.ShapeDtypeStruct((m, d), F32), jax.ShapeDtypeStruct((8, d), F32)],
        compiler_params=_cp())(h, tgt)


def _adamw(w, g, m, v, *, name):
    r, c = w.shape
    tr = _tile(r, 256)

    def body(w_ref, g_ref, m_ref, v_ref, d_ref, mo_ref, vo_ref):
        gv = g_ref[...]
        m2 = ADAM_B1 * m_ref[...] + (1.0 - ADAM_B1) * gv
        v2 = ADAM_B2 * v_ref[...] + (1.0 - ADAM_B2) * jnp.square(gv)
        m_hat = m2 / (1.0 - ADAM_B1 ** ADAM_STEP)
        v_hat = v2 / (1.0 - ADAM_B2 ** ADAM_STEP)
        d_ref[...] = -ADAM_LR * (m_hat / (jnp.sqrt(v_hat) + ADAM_EPS) + ADAM_WD * w_ref[...])
        mo_ref[...] = m2
        vo_ref[...] = v2

    blk = pl.BlockSpec((tr, c), lambda i: (i, 0))
    return pl.pallas_call(
        body, grid=(r // tr,), name=name, in_specs=[blk] * 4, out_specs=[blk] * 3,
        out_shape=[jax.ShapeDtypeStruct((r, c), F32)] * 3, compiler_params=_cp())(w, g, m, v)


def _head_masks():
    lane = lax.broadcasted_iota(jnp.int32, (BLK, LANES), 1)
    row = lax.broadcasted_iota(jnp.int32, (BLK, LANES), 0)
    return lane, row, lane < HD


def _sb_scores(q_a, k, before):
    z = _dot_nt(q_a, k)
    sp = jnp.log1p(jnp.exp(-jnp.abs(z)))
    ls_pos = jnp.minimum(z, 0.0) - sp
    lkeep = jnp.where(before, ls_pos - z, 0.0)
    return ls_pos, lkeep


SB_DEAD = -104.0


def _sb_alive(jj, i, carry):
    return jnp.logical_and(jj <= i, jnp.max(carry) > SB_DEAD)


SB_QB = 2
SB_ROWS = SB_QB * 2 * BLK


def _sb_before(jj):
    lane = lax.broadcasted_iota(jnp.int32, (SB_ROWS, LANES), 1)
    row = lax.broadcasted_iota(jnp.int32, (SB_ROWS, LANES), 0)
    below_diag = jj - (SB_QB - 1) + row // (2 * BLK)
    return jnp.logical_or(below_diag > 0, jnp.logical_and(below_diag == 0, lane < row % BLK))


def _sb_stack(x, lane_h):
    return jnp.concatenate([_stack_heads(x[b * BLK:(b + 1) * BLK], lane_h) for b in range(SB_QB)], axis=0)


def _sb_unstack(x, lane_h):
    return jnp.concatenate([jnp.where(lane_h, x[2 * b * BLK:(2 * b + 1) * BLK], x[(2 * b + 1) * BLK:(2 * b + 2) * BLK])
                            for b in range(SB_QB)], axis=0)


def _sb_fwd(u, *, name, carry=None):
    s_len = u.shape[0]
    qrows = SB_QB * BLK

    def body(q_ref, k_ref, v_ref, o_ref):
        top = pl.program_id(1) * SB_QB + SB_QB - 1
        lane, row, lane_h = _head_masks()
        suffix = (row > lane).astype(BF16)
        qs = _sb_stack(q_ref[...] * 0.125, lane_h)

        def step(state):
            jj, cc, acc = state
            off = pl.multiple_of((top - jj) * BLK, BLK)
            k = k_ref[pl.ds(off, BLK), :].astype(BF16)
            v = v_ref[pl.ds(off, BLK), :].astype(BF16)
            before = _sb_before(jj)
            ls_pos, lkeep = _sb_scores(qs, k, before)
            between = _dot_hilo(lkeep, suffix) + cc
            att = jnp.where(before, jnp.exp(ls_pos + between), 0.0)
            return jj + 1, cc + jnp.sum(lkeep, axis=1, keepdims=True), acc + _dot(att.astype(BF16), v)

        init = (jnp.int32(0), jnp.zeros((SB_ROWS, 1), F32), jnp.zeros((SB_ROWS, LANES), F32))
        acc = lax.while_loop(lambda st: _sb_alive(st[0], top, st[1]), step, init)[2]
        o_ref[...] = _sb_unstack(acc, lane_h).astype(BF16)

    return _call(
        body, grid=(2, s_len // qrows), name=name, carry=carry,
        in_specs=[pl.BlockSpec((qrows, LANES), lambda hp, i: (i, hp)),
                  pl.BlockSpec((s_len, LANES), lambda hp, i: (0, 2 + hp)),
                  pl.BlockSpec((s_len, LANES), lambda hp, i: (0, 4 + hp))],
        out_specs=[pl.BlockSpec((qrows, LANES), lambda hp, i: (i, hp))],
        out_shape=[jax.ShapeDtypeStruct((s_len, SB_W), BF16)], args=(u, u, u))


def _sb_bwd(u, dcat, *, name, carry=None):
    s_len = u.shape[0]
    nq = s_len // BLK
    qrows = SB_QB * BLK

    def body(q_ref, k_ref, v_ref, do_ref, dq_ref, dk_ref, dv_ref, g_scr, b_scr):
        step = pl.program_id(1)
        top = step * SB_QB + SB_QB - 1
        lane, row, lane_h = _head_masks()
        suffix = (row > lane).astype(BF16)
        prefix = (row < lane).astype(BF16)
        qf = q_ref[...]
        qs = _sb_stack(qf * 0.125, lane_h)
        qu = _sb_stack(qf, lane_h)
        dos = _sb_stack(do_ref[...], lane_h)

        @pl.when(step == 0)
        def _():
            dk_ref[...] = jnp.zeros_like(dk_ref)
            dv_ref[...] = jnp.zeros_like(dv_ref)

        def down(state):
            jj, cc = state
            j = top - jj
            off = pl.multiple_of(j * BLK, BLK)
            k = k_ref[pl.ds(off, BLK), :].astype(BF16)
            v = v_ref[pl.ds(off, BLK), :].astype(BF16)
            before = _sb_before(jj)
            ls_pos, lkeep = _sb_scores(qs, k, before)
            between = _dot_hilo(lkeep, suffix) + cc
            att = jnp.where(before, jnp.exp(ls_pos + between), 0.0)
            g_scr[j] = att * _dot_nt(dos, v)
            b_scr[j] = jnp.exp(ls_pos)
            dv_ref[pl.ds(off, BLK), :] += _dot_tn(att.astype(BF16), dos)
            return jj + 1, cc + jnp.sum(lkeep, axis=1, keepdims=True)

        zc = jnp.zeros((SB_ROWS, 1), F32)
        visited = lax.while_loop(lambda st: _sb_alive(st[0], top, st[1]), down, (jnp.int32(0), zc))[0]

        def up(j, carry):
            pc, dq = carry
            off = pl.multiple_of(j * BLK, BLK)
            k = k_ref[pl.ds(off, BLK), :].astype(BF16)
            g, beta = g_scr[j], b_scr[j]
            below = _dot_hilo(g, prefix) + pc
            dz = (jnp.where(_sb_before(top - j), g * (1.0 - beta) - beta * below, 0.0) * 0.125).astype(BF16)
            dk_ref[pl.ds(off, BLK), :] += _dot_tn(dz, qu)
            return pc + jnp.sum(g, axis=1, keepdims=True), dq + _dot(dz, k)

        dq = lax.fori_loop(top + 1 - visited, top + 1, up, (zc, jnp.zeros((SB_ROWS, LANES), F32)))[1]
        dq_ref[...] = _sb_unstack(dq, lane_h)

    col = lambda c0: pl.BlockSpec((s_len, LANES), lambda hp, i: (0, c0 + hp))
    blk = pl.BlockSpec((qrows, LANES), lambda hp, i: (i, hp))
    acc = pl.BlockSpec((s_len, LANES), lambda hp, i: (0, hp))
    return _call(
        body, grid=(2, s_len // qrows), name=name, carry=carry, in_specs=[blk, col(2), col(4), blk],
        out_specs=[blk, acc, acc], out_shape=[jax.ShapeDtypeStruct((s_len, SB_W), F32)] * 3,
        scratch_shapes=[pltpu.VMEM((nq, SB_ROWS, LANES), F32), pltpu.VMEM((nq, SB_ROWS, LANES), F32)],
        vmem_mb=56, args=(u, u, u, dcat))


CV_T = 512
CV_H = 32


def _cv_specs(s_len):
    cur = lambda c: pl.BlockSpec((CV_T, CV_W), lambda i: (i, c))
    prev = lambda c: pl.BlockSpec((CV_H, CV_W), lambda i: (jnp.maximum(i * (CV_T // CV_H) - 1, 0), c))
    nxt = lambda c: pl.BlockSpec((CV_H, CV_W),
                                 lambda i: (jnp.minimum((i + 1) * (CV_T // CV_H), s_len // CV_H - 1), c))
    full = lambda r: pl.BlockSpec((r, CV_W), lambda i: (0, 0))
    return cur, prev, nxt, full


def _glu_into(gp_ref, val_ref, gate_ref, valp_ref, gatep_ref, i):
    gp_ref[0:CV_H, :] = jnp.where(i > 0, valp_ref[...] * jax.nn.sigmoid(gatep_ref[...]), 0.0)
    gp_ref[CV_H:, :] = val_ref[...] * jax.nn.sigmoid(gate_ref[...])


def _cv_fwd(u, cv_w, cv_b, ln_g, ln_b, pw_w, pw_b, *, name):
    s_len = u.shape[0]
    cur, prev, _, full = _cv_specs(s_len)

    def body(val_ref, gate_ref, valp_ref, gatep_ref, w_ref, b_ref, g_ref, be_ref, pw_ref, pb_ref,
             o_ref, c_ref, gp_ref):
        _glu_into(gp_ref, val_ref, gate_ref, valp_ref, gatep_ref, pl.program_id(0))
        acc = jnp.zeros((CV_T, CV_W), F32) + b_ref[...]
        for k in range(CV_K):
            acc = acc + w_ref[k:k + 1, :] * gp_ref[pl.ds(CV_H - CV_K + 1 + k, CV_T), :]
        c_ref[...] = acc
        mu = jnp.mean(acc, axis=-1, keepdims=True)
        xc = acc - mu
        xh = xc * lax.rsqrt(jnp.mean(xc * xc, axis=-1, keepdims=True) + EPS)
        a = xh * g_ref[...] + be_ref[...]
        s = a * jax.nn.sigmoid(a)
        o_ref[...] = (_dot(s.astype(BF16), pw_ref[...]) + pb_ref[...]).astype(BF16)

    return pl.pallas_call(
        body, grid=(s_len // CV_T,), name=name,
        in_specs=[cur(3), cur(4), prev(3), prev(4), full(CV_K), full(1), full(1), full(1), full(CV_W), full(1)],
        out_specs=[cur(0), cur(0)],
        out_shape=[jax.ShapeDtypeStruct((s_len, CV_W), BF16), jax.ShapeDtypeStruct((s_len, CV_W), F32)],
        scratch_shapes=[pltpu.VMEM((CV_T + CV_H, CV_W), F32)], compiler_params=_cp())(
            u, u, u, u, cv_w, cv_b, ln_g, ln_b, pw_w, pw_b)


def _cv_bwd_local(c, dcat, ln_g, ln_b, pw_w, *, name):
    s_len = c.shape[0]
    cur, _, _, full = _cv_specs(s_len)

    def body(c_ref, db_ref, g_ref, be_ref, pw_ref, dc_ref, dpw_ref, vec_ref):
        i = pl.program_id(0)
        cv = c_ref[...]
        db = db_ref[...]
        mu = jnp.mean(cv, axis=-1, keepdims=True)
        xc = cv - mu
        rstd = lax.rsqrt(jnp.mean(xc * xc, axis=-1, keepdims=True) + EPS)
        xh = xc * rstd
        a = xh * g_ref[...] + be_ref[...]
        sg = jax.nn.sigmoid(a)
        s = a * sg
        dbb = db.astype(BF16)
        ds = _dot_nt(dbb, pw_ref[...])
        da = ds * (sg * (1.0 + a * (1.0 - sg)))
        dxh = da * g_ref[...]
        dc_ref[...] = rstd * (dxh - jnp.mean(dxh, axis=-1, keepdims=True)
                              - xh * jnp.mean(dxh * xh, axis=-1, keepdims=True))
        _acc_out(dpw_ref, i, _dot_tn(s.astype(BF16), dbb))
        _acc_out(vec_ref, i, jnp.concatenate([_rowsum8(db), _rowsum8(da * xh), _rowsum8(da)], axis=0))

    return pl.pallas_call(
        body, grid=(s_len // CV_T,), name=name,
        in_specs=[cur(0), cur(1), full(1), full(1), full(CV_W)],
        out_specs=[cur(0), full(CV_W), full(24)],
        out_shape=[jax.ShapeDtypeStruct((s_len, CV_W), F32), jax.ShapeDtypeStruct((CV_W, CV_W), F32),
                   jax.ShapeDtypeStruct((24, CV_W), F32)], compiler_params=_cp())(c, dcat, ln_g, ln_b, pw_w)


def _cv_bwd_conv(u, dc, cv_w, *, name):
    s_len = u.shape[0]
    cur, prev, nxt, full = _cv_specs(s_len)
    last = s_len // CV_T - 1

    def body(val_ref, gate_ref, valp_ref, gatep_ref, dc_ref, dcn_ref, w_ref, du_ref, dw_ref, dbias_ref,
             gp_ref, dcp_ref):
        i = pl.program_id(0)
        _glu_into(gp_ref, val_ref, gate_ref, valp_ref, gatep_ref, i)
        dcv = dc_ref[...]
        dcp_ref[0:CV_T, :] = dcv
        dcp_ref[CV_T:, :] = jnp.where(i < last, dcn_ref[...], 0.0)
        dg = jnp.zeros((CV_T, CV_W), F32)
        parts = []
        for k in range(CV_K):
            dg = dg + w_ref[k:k + 1, :] * dcp_ref[pl.ds(CV_K - 1 - k, CV_T), :]
            parts.append(_rowsum8(dcv * gp_ref[pl.ds(CV_H - CV_K + 1 + k, CV_T), :]))
        _acc_out(dw_ref, i, jnp.concatenate(parts, axis=0))
        _acc_out(dbias_ref, i, _rowsum8(dcv))
        val = val_ref[...]
        sg = jax.nn.sigmoid(gate_ref[...])
        du_ref[:, 0:CV_W] = (dg * sg).astype(BF16)
        du_ref[:, CV_W:] = (dg * val * sg * (1.0 - sg)).astype(BF16)

    return pl.pallas_call(
        body, grid=(s_len // CV_T,), name=name,
        in_specs=[cur(3), cur(4), prev(3), prev(4), cur(0), nxt(0), full(CV_K)],
        out_specs=[pl.BlockSpec((CV_T, 2 * CV_W), lambda i: (i, 0)), full(CV_K * 8), full(8)],
        out_shape=[jax.ShapeDtypeStruct((s_len, 2 * CV_W), BF16), jax.ShapeDtypeStruct((CV_K * 8, CV_W), F32),
                   jax.ShapeDtypeStruct((8, CV_W), F32)],
        scratch_shapes=[pltpu.VMEM((CV_T + CV_H, CV_W), F32), pltpu.VMEM((CV_T + CV_H, CV_W), F32)],
        compiler_params=_cp())(u, u, u, u, dc, dc, cv_w)


def _rope_tables(pos_col, inv_freq_row, *, name):
    s_len = pos_col.shape[0]

    def body(p_ref, f_ref, cos_ref, sin_ref):
        ang = p_ref[...].astype(F32) * f_ref[...]
        lane = lax.broadcasted_iota(jnp.int32, (s_len, LANES), 1)
        sn = jnp.sin(ang)
        cos_ref[...] = jnp.cos(ang)
        sin_ref[...] = jnp.where(lane % HD < HD // 2, -sn, sn)

    return pl.pallas_call(body, name=name, out_shape=[jax.ShapeDtypeStruct((s_len, LANES), F32)] * 2,
                          compiler_params=_cp())(pos_col, inv_freq_row)


def _rot_half(x):
    lane = lax.broadcasted_iota(jnp.int32, x.shape, 1)
    return jnp.where(lane % HD < HD // 2, pltpu.roll(x, LANES - HD // 2, 1), pltpu.roll(x, HD // 2, 1))


def _permute_rows(dst_ref, src_ref, d, dtype):
    s_len = src_ref.shape[0]
    seg = s_len // d
    if d == 1:
        dst_ref[...] = src_ref[...].astype(dtype)
        return
    for r in range(d):
        dst_ref[r * seg:(r + 1) * seg, :] = src_ref[pl.ds(r, seg, stride=d), :].astype(dtype)


def _unpermute_rows(dst_ref, src_ref, d):
    s_len = src_ref.shape[0]
    seg = s_len // d
    if d == 1:
        dst_ref[...] = src_ref[...]
        return
    for r in range(d):
        dst_ref[pl.ds(r, seg, stride=d), :] = src_ref[r * seg:(r + 1) * seg, :]


def _rope_perm(u, cos, sin, *, name):
    s_len = u.shape[0]

    def body(x_ref, cos_ref, sin_ref, o_ref, scr):
        a = pl.program_id(0)
        x = x_ref[...]
        rot = a < 2
        scr[...] = x * jnp.where(rot, cos_ref[...], 1.0) + _rot_half(x) * jnp.where(rot, sin_ref[...], 0.0)
        for n, d in enumerate(DILATIONS):
            _permute_rows(o_ref.at[n], scr, d, BF16)

    tab = pl.BlockSpec((s_len, LANES), lambda a, cb: (0, 0))
    return pl.pallas_call(
        body, grid=(3, 4), name=name,
        in_specs=[pl.BlockSpec((s_len, LANES), lambda a, cb: (0, 10 + 4 * a + cb)), tab, tab],
        out_specs=pl.BlockSpec((None, 3, s_len, LANES), lambda a, cb: (a, 0, 0, cb)),
        out_shape=jax.ShapeDtypeStruct((3, 3, s_len, DL_W), BF16),
        scratch_shapes=[pltpu.VMEM((s_len, LANES), F32)], compiler_params=_cp())(u, cos, sin)


DL_UNROLL = 4


def _dl_band(rows):
    lane = lax.broadcasted_iota(jnp.int32, (rows, LANES), 1)
    row = lax.broadcasted_iota(jnp.int32, (rows, LANES), 0) % BLK
    return lane <= row, lane >= row


def _dl_first(s_len, n, i):
    nb = jnp.where(n == 0, s_len // BLK, jnp.where(n == 1, s_len // (BLK * DILATIONS[1]),
                                                   s_len // (BLK * DILATIONS[2])))
    return lax.rem(i, nb) == 0


def _stack_heads(x, lane_h):
    return jnp.concatenate([jnp.where(lane_h, x, 0.0), jnp.where(lane_h, 0.0, x)], axis=0).astype(BF16)


def _dl_rows(i):
    cur = pl.ds(pl.multiple_of(i * BLK, BLK), BLK)
    prev = pl.ds(pl.multiple_of(jnp.maximum(i - 1, 0) * BLK, BLK), BLK)
    return cur, prev


def _dl_in_specs(s_len):
    return [pl.BlockSpec((None, None, s_len, LANES), functools.partial(lambda a, n, hp: (a, n, 0, hp), a))
            for a in range(3)]


def _dl_fwd(qkv, *, name, carry=None):
    s_len = qkv.shape[2]

    def body(q_ref, k_ref, v_ref, o_ref, l_ref):
        n = pl.program_id(0)
        lane_h = _head_masks()[2]
        band_c, band_p = _dl_band(2 * BLK)
        ones = jnp.ones((BLK, LANES), BF16)

        @pl.loop(0, s_len // BLK, step=DL_UNROLL)
        def _(i0):
            blocks = [i0 + t for t in range(DL_UNROLL)]
            rows = [_dl_rows(i) for i in blocks]
            scores = []
            for cur, prev in rows:
                qs = _stack_heads(q_ref[cur, :] * 0.125, lane_h)
                scores.append((_dot_nt(qs, k_ref[cur, :]), _dot_nt(qs, k_ref[prev, :])))
            probs = []
            for i, (sc, sp) in zip(blocks, scores):
                sc = jnp.where(band_c, sc, NEG_INF)
                sp = jnp.where(jnp.logical_and(band_p, jnp.logical_not(_dl_first(s_len, n, i))), sp, NEG_INF)
                m = jnp.max(jnp.maximum(sc, sp), axis=1, keepdims=True)
                probs.append((jnp.exp(sc - m).astype(BF16), jnp.exp(sp - m).astype(BF16), m))
            for (cur, prev), (pc, pp, m) in zip(rows, probs):
                r = (_dot(pc, jnp.concatenate([v_ref[cur, :], ones], axis=1))
                     + _dot(pp, jnp.concatenate([v_ref[prev, :], ones], axis=1)))
                den = jnp.where(lane_h, r[:BLK, LANES:], r[BLK:, LANES:])
                o_ref[cur, :] = jnp.where(lane_h, r[:BLK, :LANES], r[BLK:, :LANES]) / den
                l_ref[cur, :] = jnp.where(lane_h, m[:BLK], m[BLK:]) + jnp.log(den)

    out = pl.BlockSpec((None, s_len, LANES), lambda n, hp: (n, 0, hp))
    return _call(
        body, grid=(3, 4), name=name, carry=carry, in_specs=_dl_in_specs(s_len), out_specs=[out, out],
        out_shape=[jax.ShapeDtypeStruct((3, s_len, DL_W), F32)] * 2, args=(qkv, qkv, qkv))


def _dl_mix(o_p, l_p, *, name, carry=None):
    s_len = o_p.shape[1]

    def body(o_ref, l_ref, ob_ref, of_ref, lt_ref, o_scr, l_scr):
        n = pl.program_id(1)
        for k, d in enumerate(DILATIONS):
            @pl.when(n == k)
            def _(k=k, d=d):
                _unpermute_rows(o_scr.at[k], o_ref, d)
                _unpermute_rows(l_scr.at[k], l_ref, d)

        @pl.when(n == 2)
        def _():
            l0, l1, l2 = l_scr[0], l_scr[1], l_scr[2]
            m = jnp.maximum(jnp.maximum(l0, l1), l2)
            e0, e1, e2 = jnp.exp(l0 - m), jnp.exp(l1 - m), jnp.exp(l2 - m)
            den = e0 + e1 + e2
            o = (e0 / den) * o_scr[0] + (e1 / den) * o_scr[1] + (e2 / den) * o_scr[2]
            of_ref[...] = o
            ob_ref[...] = o.astype(BF16)
            lt_ref[...] = m + jnp.log(den)

    inb = pl.BlockSpec((None, s_len, LANES), lambda cb, n: (n, 0, cb))
    outb = pl.BlockSpec((s_len, LANES), lambda cb, n: (0, cb))
    return _call(
        body, grid=(4, 3), name=name, carry=carry, in_specs=[inb, inb], out_specs=[outb, outb, outb],
        out_shape=[jax.ShapeDtypeStruct((s_len, DL_W), BF16), jax.ShapeDtypeStruct((s_len, DL_W), F32),
                   jax.ShapeDtypeStruct((s_len, DL_W), F32)],
        scratch_shapes=[pltpu.VMEM((3, s_len, LANES), F32), pltpu.VMEM((3, s_len, LANES), F32)], args=(o_p, l_p))


def _dl_bwd_prep(dcat, o, lse, *, name):
    s_len = o.shape[0]

    def body(do_ref, o_ref, l_ref, dop_ref, st_ref, d_scr):
        n = pl.program_id(1)

        @pl.when(n == 0)
        def _():
            r0 = lax.broadcasted_iota(jnp.int32, (LANES, LANES), 0) // HD
            r1 = lax.broadcasted_iota(jnp.int32, (LANES, LANES), 1) // HD
            d_scr[...] = _dot_hilo(do_ref[...] * o_ref[...], (r0 == r1).astype(BF16))

        for k, d in enumerate(DILATIONS):
            @pl.when(n == k)
            def _(d=d):
                _permute_rows(dop_ref, do_ref, d, BF16)
                _permute_rows(st_ref.at[0], d_scr, d, F32)
                _permute_rows(st_ref.at[1], l_ref, d, F32)

    nat = lambda c0: pl.BlockSpec((s_len, LANES), lambda cb, n: (0, c0 + cb))
    return pl.pallas_call(
        body, grid=(4, 3), name=name, in_specs=[nat(4), nat(0), nat(0)],
        out_specs=[pl.BlockSpec((None, s_len, LANES), lambda cb, n: (n, 0, cb)),
                   pl.BlockSpec((2, None, s_len, LANES), lambda cb, n: (0, n, 0, cb))],
        out_shape=[jax.ShapeDtypeStruct((3, s_len, DL_W), BF16), jax.ShapeDtypeStruct((2, 3, s_len, DL_W), F32)],
        scratch_shapes=[pltpu.VMEM((s_len, LANES), F32)], compiler_params=_cp())(dcat, o, lse)


def _dl_bwd(qkv, dop, stats, *, name, carry=None):
    s_len = qkv.shape[2]

    def body(q_ref, k_ref, v_ref, do_ref, st_ref, cur_ref, prev_ref):
        n = pl.program_id(0)
        lane_h = _head_masks()[2]
        band_c, band_p = _dl_band(2 * BLK)

        def per_head(x):
            xr = pltpu.roll(x, HD, 1)
            return jnp.concatenate([jnp.where(lane_h, x, xr), jnp.where(lane_h, xr, x)], axis=0)

        @pl.loop(0, s_len // BLK, step=DL_UNROLL)
        def _(i0):
            blocks = [i0 + t for t in range(DL_UNROLL)]
            rows = [_dl_rows(i) for i in blocks]
            stage1 = []
            for cur, prev in rows:
                qs = _stack_heads(q_ref[cur, :] * 0.125, lane_h)
                dos = _stack_heads(do_ref[cur, :], lane_h)
                kc, kp, vc, vp = k_ref[cur, :], k_ref[prev, :], v_ref[cur, :], v_ref[prev, :]
                stage1.append((qs, dos, _dot_nt(qs, kc), _dot_nt(qs, kp), _dot_nt(dos, vc), _dot_nt(dos, vp)))
            stage2 = []
            for i, (cur, prev), (qs, dos, sc, sp, dpc, dpp) in zip(blocks, rows, stage1):
                lse, delta = per_head(st_ref[1, cur, :]), per_head(st_ref[0, cur, :])
                pc = jnp.where(band_c, jnp.exp(sc - lse), 0.0)
                pp = jnp.where(jnp.logical_and(band_p, jnp.logical_not(_dl_first(s_len, n, i))), jnp.exp(sp - lse), 0.0)
                stage2.append((pc.astype(BF16), pp.astype(BF16), (pc * (dpc - delta)).astype(BF16),
                               (pp * (dpp - delta)).astype(BF16)))
            for (cur, prev), (qs, dos, *_), (pc, pp, dsc, dsp) in zip(rows, stage1, stage2):
                dq = _dot(dsc, k_ref[cur, :]) + _dot(dsp, k_ref[prev, :])
                cur_ref[0, cur, :] = jnp.where(lane_h, dq[:BLK], dq[BLK:]) * 0.125
                cur_ref[1, cur, :] = _dot_tn(dsc, qs)
                cur_ref[2, cur, :] = _dot_tn(pc, dos)
                prev_ref[0, cur, :] = _dot_tn(dsp, qs)
                prev_ref[1, cur, :] = _dot_tn(pp, dos)

    return _call(
        body, grid=(3, 4), name=name, carry=carry,
        in_specs=_dl_in_specs(s_len) + [pl.BlockSpec((None, s_len, LANES), lambda n, hp: (n, 0, hp)),
                                        pl.BlockSpec((2, None, s_len, LANES), lambda n, hp: (0, n, 0, hp))],
        out_specs=[pl.BlockSpec((3, None, s_len, LANES), lambda n, hp: (0, n, 0, hp)),
                   pl.BlockSpec((2, None, s_len, LANES), lambda n, hp: (0, n, 0, hp))],
        out_shape=[jax.ShapeDtypeStruct((3, 3, s_len, DL_W), F32), jax.ShapeDtypeStruct((2, 3, s_len, DL_W), F32)],
        vmem_mb=56, args=(qkv, qkv, qkv, dop, stats))


def _dl_bwd_finish(cur, prev, cos, sin, *, name):
    s_len = cur.shape[2]

    def body(c_ref, p_ref, cos_ref, sin_ref, o_ref, p_scr, u_scr, acc):
        a, n = pl.program_id(0), pl.program_id(2)
        has_prev = jnp.where(a > 0, 1.0, 0.0)
        p_scr[...] = c_ref[...]
        p_scr[0:s_len - BLK, :] += has_prev * p_ref[BLK:, :]
        for k, d in enumerate(DILATIONS):
            @pl.when(n == k)
            def _(k=k, d=d):
                if k == 0:
                    acc[...] = p_scr[...]
                else:
                    _unpermute_rows(u_scr, p_scr, d)
                    acc[...] += u_scr[...]

        @pl.when(n == 2)
        def _():
            dy = acc[...]
            rot = a < 2
            o_ref[...] = (dy * jnp.where(rot, cos_ref[...], 1.0)
                          + _rot_half(dy * jnp.where(rot, sin_ref[...], 0.0))).astype(BF16)

    tab = pl.BlockSpec((s_len, LANES), lambda a, cb, n: (0, 0))
    return pl.pallas_call(
        body, grid=(3, 4, 3), name=name,
        in_specs=[pl.BlockSpec((None, None, s_len, LANES), lambda a, cb, n: (a, n, 0, cb)),
                  pl.BlockSpec((None, None, s_len, LANES), lambda a, cb, n: (jnp.maximum(a - 1, 0), n, 0, cb)),
                  tab, tab],
        out_specs=pl.BlockSpec((s_len, LANES), lambda a, cb, n: (0, 4 * a + cb)),
        out_shape=jax.ShapeDtypeStruct((s_len, 3 * DL_W), BF16),
        scratch_shapes=[pltpu.VMEM((s_len, LANES), F32)] * 3, compiler_params=_cp())(cur, prev, cos, sin)


XA_T = 256


def _xa_probs(q, k):
    s = _dot_nt(q, k) * (X_HD ** -0.5)
    e = jnp.exp(s - jnp.max(s, axis=1, keepdims=True))
    return e / jnp.sum(e, axis=1, keepdims=True)


def _xa_fwd(q, k, v, *, name):
    s_len, d = q.shape
    nm = k.shape[0]

    def body(q_ref, k_ref, v_ref, o_ref):
        for h in range(X_HEADS):
            cs = slice(h * X_HD, (h + 1) * X_HD)
            p = _xa_probs(q_ref[:, cs], k_ref[:, cs])
            o_ref[:, cs] = _dot(p.astype(BF16), v_ref[:, cs]).astype(BF16)

    row = pl.BlockSpec((XA_T, d), lambda i: (i, 0))
    full = pl.BlockSpec((nm, d), lambda i: (0, 0))
    return pl.pallas_call(body, grid=(s_len // XA_T,), name=name, in_specs=[row, full, full], out_specs=row,
                          out_shape=jax.ShapeDtypeStruct((s_len, d), BF16), compiler_params=_cp())(q, k, v)


def _xa_bwd(q, k, v, do, *, name, carry=None):
    s_len, d = q.shape
    nm = k.shape[0]

    def body(q_ref, k_ref, v_ref, do_ref, dq_ref, dk_ref, dv_ref):
        i = pl.program_id(0)
        for h in range(X_HEADS):
            cs = slice(h * X_HD, (h + 1) * X_HD)
            qh, kh, vh, doh = q_ref[:, cs], k_ref[:, cs], v_ref[:, cs], do_ref[:, cs]
            p = _xa_probs(qh, kh)
            dp = _dot_nt(doh, vh)
            ds = (p * (dp - jnp.sum(dp * p, axis=1, keepdims=True)) * (X_HD ** -0.5)).astype(BF16)
            dq_ref[:, cs] = _dot(ds, kh).astype(BF16)
            dkh, dvh = _dot_tn(ds, qh), _dot_tn(p.astype(BF16), doh)

            @pl.when(i == 0)
            def _(cs=cs, dkh=dkh, dvh=dvh):
                dk_ref[:, cs] = dkh
                dv_ref[:, cs] = dvh

            @pl.when(i > 0)
            def _(cs=cs, dkh=dkh, dvh=dvh):
                dk_ref[:, cs] += dkh
                dv_ref[:, cs] += dvh

    row = pl.BlockSpec((XA_T, d), lambda i: (i, 0))
    full = pl.BlockSpec((nm, d), lambda i: (0, 0))
    return _call(
        body, grid=(s_len // XA_T,), name=name, carry=carry, in_specs=[row, full, full, row],
        out_specs=[row, full, full],
        out_shape=[jax.ShapeDtypeStruct((s_len, d), BF16), jax.ShapeDtypeStruct((nm, d), F32),
                   jax.ShapeDtypeStruct((nm, d), F32)], args=(q, k, v, do))


FF_TM, FF_TN, FF_H = 512, 256, 8
GELU_K, GELU_C = 0.7978845608028654, 0.044715


FF_STRIP = 64


def _ff_conv(e_ref, w_ref, b_ref, rows, r0=0):
    return (w_ref[0:1, :] * e_ref[pl.ds(FF_H - 2 + r0, rows), :] + w_ref[1:2, :] * e_ref[pl.ds(FF_H - 1 + r0, rows), :]
            + w_ref[2:3, :] * e_ref[pl.ds(FF_H + r0, rows), :] + b_ref[...])


def _strips(total, size):
    return [(r0, min(size, total - r0)) for r0 in range(0, total, size)]


def _ff_gate_fwd(up, conv_w, conv_b, *, name, carry=None):
    s_len = up.shape[0]
    nj = D_FF // FF_TN

    def body(g_ref, v_ref, gp_ref, vp_ref, wg_ref, wv_ref, bg_ref, bv_ref, o_ref, eg, ev):
        i = pl.program_id(0)
        for e, cur, prev in ((eg, g_ref, gp_ref), (ev, v_ref, vp_ref)):
            e[0:FF_H, :] = jnp.where(i > 0, prev[...], 0.0)
            e[FF_H:, :] = cur[...]
        for r0, rows in _strips(FF_TM, FF_STRIP):
            gate = _ff_conv(eg, wg_ref, bg_ref, rows, r0)
            val = _ff_conv(ev, wv_ref, bv_ref, rows, r0)
            t = jnp.tanh(GELU_K * (gate + GELU_C * gate * gate * gate))
            o_ref[r0:r0 + rows, :] = (0.5 * gate * (1.0 + t) * val).astype(BF16)

    cur = lambda c0: pl.BlockSpec((FF_TM, FF_TN), lambda i, j: (i, c0 + j))
    prev = lambda c0: pl.BlockSpec((FF_H, FF_TN), lambda i, j: (jnp.maximum(i * (FF_TM // FF_H) - 1, 0), c0 + j))
    par = lambda r, c0: pl.BlockSpec((r, FF_TN), lambda i, j: (0, c0 + j))
    return _call(
        body, grid=(s_len // FF_TM, nj), name=name, carry=carry,
        in_specs=[cur(0), cur(nj), prev(0), prev(nj), par(3, 0), par(3, nj), par(1, 0), par(1, nj)],
        out_specs=[cur(0)], out_shape=[jax.ShapeDtypeStruct((s_len, D_FF), BF16)],
        scratch_shapes=[pltpu.VMEM((FF_TM + FF_H, FF_TN), F32)] * 2,
        args=(up, up, up, up, conv_w, conv_w, conv_b, conv_b))


def _ff_gate_bwd(up, dact, conv_w, conv_b, *, name, carry=None):
    s_len = up.shape[0]
    nj = D_FF // FF_TN
    last = s_len // FF_TM - 1
    ext = FF_TM + FF_H

    def body(g_ref, v_ref, gp_ref, vp_ref, gn_ref, vn_ref, da_ref, dan_ref, wg_ref, wv_ref, bg_ref, bv_ref,
             dg_ref, dv_ref, dw_ref, db_ref, eg, ev, sg, sv):
        i = pl.program_id(1)
        for e, cur, prev, nxt in ((eg, g_ref, gp_ref, gn_ref), (ev, v_ref, vp_ref, vn_ref)):
            e[0:FF_H, :] = jnp.where(i > 0, prev[...], 0.0)
            e[FF_H:FF_H + FF_TM, :] = cur[...]
            e[FF_H + FF_TM:, :] = nxt[...]
        for r0, rows in _strips(ext, FF_STRIP):
            gate = _ff_conv(eg, wg_ref, bg_ref, rows, r0)
            val = _ff_conv(ev, wv_ref, bv_ref, rows, r0)
            dact = da_ref[r0:r0 + rows, :] if r0 < FF_TM else jnp.where(i < last, dan_ref[...], 0.0)
            t = jnp.tanh(GELU_K * (gate + GELU_C * gate * gate * gate))
            half = 0.5 * (1.0 + t)
            dgelu = half + 0.5 * gate * (1.0 - t * t) * GELU_K * (1.0 + 3.0 * GELU_C * gate * gate)
            sg[r0:r0 + rows, :] = dact * val * dgelu
            sv[r0:r0 + rows, :] = dact * (gate * half)
        for part, (s, e, w_ref, out) in enumerate(((sg, eg, wg_ref, dg_ref), (sv, ev, wv_ref, dv_ref))):
            taps, bias = [jnp.zeros((8, FF_TN), F32)] * 3, jnp.zeros((8, FF_TN), F32)
            for r0, rows in _strips(FF_TM, FF_STRIP):
                d0 = s[pl.ds(r0, rows), :]
                out[r0:r0 + rows, :] = (w_ref[2:3, :] * d0 + w_ref[1:2, :] * s[pl.ds(r0 + 1, rows), :]
                                        + w_ref[0:1, :] * s[pl.ds(r0 + 2, rows), :]).astype(BF16)
                taps = [taps[k] + _rowsum8(d0 * e[pl.ds(FF_H - 2 + k + r0, rows), :]) for k in range(3)]
                bias = bias + _rowsum8(d0)
            _acc_out(dw_ref.at[part], i, jnp.concatenate(taps, axis=0))
            _acc_out(db_ref.at[part], i, bias)

    cur = lambda c0: pl.BlockSpec((FF_TM, FF_TN), lambda j, i: (i, c0 + j))
    prev = lambda c0: pl.BlockSpec((FF_H, FF_TN), lambda j, i: (jnp.maximum(i * (FF_TM // FF_H) - 1, 0), c0 + j))
    nxt = lambda c0: pl.BlockSpec(
        (FF_H, FF_TN), lambda j, i: (jnp.minimum((i + 1) * (FF_TM // FF_H), s_len // FF_H - 1), c0 + j))
    par = lambda r, c0: pl.BlockSpec((r, FF_TN), lambda j, i: (0, c0 + j))
    return _call(
        body, grid=(nj, s_len // FF_TM), name=name, carry=carry,
        in_specs=[cur(0), cur(nj), prev(0), prev(nj), nxt(0), nxt(nj), cur(0), nxt(0),
                  par(3, 0), par(3, nj), par(1, 0), par(1, nj)],
        out_specs=[cur(0), cur(0), pl.BlockSpec((2, 24, FF_TN), lambda j, i: (0, 0, j)),
                   pl.BlockSpec((2, 8, FF_TN), lambda j, i: (0, 0, j))],
        out_shape=[jax.ShapeDtypeStruct((s_len, D_FF), BF16), jax.ShapeDtypeStruct((s_len, D_FF), BF16),
                   jax.ShapeDtypeStruct((2, 24, D_FF), F32), jax.ShapeDtypeStruct((2, 8, D_FF), F32)],
        scratch_shapes=[pltpu.VMEM((FF_TM + 2 * FF_H, FF_TN), F32)] * 2 + [pltpu.VMEM((ext, FF_TN), F32)] * 2,
        args=(up, up, up, up, up, up, dact, dact, conv_w, conv_w, conv_b, conv_b))


def _place():
    x, y, c = lax.axis_index("x"), lax.axis_index("y"), lax.axis_index("c")
    return x, y, c, [(1 - x, y), (x, 1 - y), (1 - x, 1 - y)]


def _remote(src, dst, send_sem, recv_sem, dev):
    return pltpu.make_async_remote_copy(src_ref=src, dst_ref=dst, send_sem=send_sem, recv_sem=recv_sem,
                                        device_id=dev, device_id_type=MESH)


_ANY = pl.BlockSpec(memory_space=pl.ANY)


N_SEMS = 8
SEM_BASE_2 = 4


class _Exchange:
    def __init__(self, operands, out_shapes, start, wait, aliases=None):
        self.operands, self.out_shapes, self.start, self.wait = list(operands), list(out_shapes), start, wait
        self.aliases = aliases or {}


def _sem_scratch():
    return [pltpu.SemaphoreType.DMA((N_SEMS,)), pltpu.SemaphoreType.DMA((N_SEMS,)), pltpu.SemaphoreType.DMA]


def _run_exchange(ex, *, name):
    k, n = len(ex.operands), len(ex.out_shapes)

    def body(*refs):
        ins, outs, sems = refs[:k], refs[k:k + n], refs[k + n:]
        ex.start(ins, outs, *sems)
        ex.wait(ins, outs, *sems)

    return pl.pallas_call(body, name=name, in_specs=[_ANY] * k, out_specs=[_ANY] * n, out_shape=ex.out_shapes,
                          scratch_shapes=_sem_scratch(), input_output_aliases=ex.aliases,
                          compiler_params=_cp(16))(*ex.operands)


def _call(body, *, grid, in_specs, out_specs, out_shape, args, name, scratch_shapes=(), vmem_mb=48, carry=None):
    scratch_shapes = list(scratch_shapes)
    if carry is None:
        return pl.pallas_call(body, grid=grid, name=name, in_specs=in_specs, out_specs=out_specs, out_shape=out_shape,
                              scratch_shapes=scratch_shapes, compiler_params=_cp(vmem_mb))(*args)
    n_in, n_out, n_scr = len(in_specs), len(out_shape), len(scratch_shapes)
    k_in, k_out = len(carry.operands), len(carry.out_shapes)

    def wrapped(*refs):
        ins, refs = refs[:n_in], refs[n_in:]
        cin, refs = refs[:k_in], refs[k_in:]
        outs, refs = refs[:n_out], refs[n_out:]
        cout, refs = refs[:k_out], refs[k_out:]
        scratch, sems = refs[:n_scr], refs[n_scr:]
        ids = [pl.program_id(a) for a in range(len(grid))]
        first = functools.reduce(jnp.logical_and, [i == 0 for i in ids])
        last = functools.reduce(jnp.logical_and, [i == g - 1 for i, g in zip(ids, grid)])

        @pl.when(first)
        def _():
            carry.start(cin, cout, *sems)

        body(*ins, *outs, *scratch)

        @pl.when(last)
        def _():
            carry.wait(cin, cout, *sems)

    aliases = {n_in + i: n_out + o for i, o in carry.aliases.items()}
    return pl.pallas_call(
        wrapped, grid=grid, name=name, in_specs=list(in_specs) + [_ANY] * k_in,
        out_specs=list(out_specs) + [_ANY] * k_out, out_shape=list(out_shape) + carry.out_shapes,
        scratch_shapes=scratch_shapes + _sem_scratch(), input_output_aliases=aliases,
        compiler_params=_cp(vmem_mb))(*args, *carry.operands)


def _half_rows(ref_rows, c):
    half = ref_rows // 2
    return pl.ds(c * half, half)


def _ex_join(a, b):
    ka, na = len(a.operands), len(a.out_shapes)

    def start(ins, outs, *sems):
        a.start(ins[:ka], outs[:na], *sems)
        b.start(ins[ka:], outs[na:], *sems)

    def wait(ins, outs, *sems):
        a.wait(ins[:ka], outs[:na], *sems)
        b.wait(ins[ka:], outs[na:], *sems)

    aliases = dict(a.aliases)
    aliases.update({ka + i: na + o for i, o in b.aliases.items()})
    return _Exchange(a.operands + b.operands, a.out_shapes + b.out_shapes, start, wait, aliases)


def _ex_gather(pack, r0, rl, base=0):
    def copies(ins, outs, send, recv):
        x, y, c, chips = _place()
        rows = _half_rows(rl, c)
        src = ins[0].at[pl.ds(r0 + c * (rl // 2), rl // 2)]
        sends = [_remote(src, outs[0].at[2 * x + y, rows], send.at[base + k], recv.at[base + k], (px, py, c))
                 for k, (px, py) in enumerate(chips)]
        lands = [_remote(src, outs[0].at[2 * px + py, rows], send.at[base + k], recv.at[base + k], (px, py, c))
                 for k, (px, py) in enumerate(chips)]
        return sends, lands

    def mine(ins, outs, local):
        x, y, _, _ = _place()
        return pltpu.make_async_copy(ins[0].at[pl.ds(r0, rl)], outs[0].at[2 * x + y], local)

    def start(ins, outs, send, recv, local):
        mine(ins, outs, local).start()
        for cp in copies(ins, outs, send, recv)[0]:
            cp.start()

    def wait(ins, outs, send, recv, local):
        sends, lands = copies(ins, outs, send, recv)
        for cp in lands:
            cp.wait_recv()
        for cp in sends:
            cp.wait_send()
        mine(ins, outs, local).wait()

    return _Exchange([pack], [jax.ShapeDtypeStruct((4, rl, pack.shape[1]), pack.dtype)], start, wait)


def _ex_gather_forward(g, base=0):
    rl = g.shape[1]

    def copies(outs, send, recv):
        x, y, c, chips = _place()
        slabs = [(outs[0].at[2 * px + py, _half_rows(rl, c)], outs[0].at[2 * px + py, _half_rows(rl, 1 - c)])
                 for px, py in chips]
        sends = [_remote(a, a, send.at[base + k], recv.at[base + k], (x, y, 1 - c)) for k, (a, _) in enumerate(slabs)]
        lands = [_remote(b, b, send.at[base + k], recv.at[base + k], (x, y, 1 - c)) for k, (_, b) in enumerate(slabs)]
        return sends, lands

    def start(ins, outs, send, recv, local):
        for cp in copies(outs, send, recv)[0]:
            cp.start()

    def wait(ins, outs, send, recv, local):
        sends, lands = copies(outs, send, recv)
        for cp in lands:
            cp.wait_recv()
        for cp in sends:
            cp.wait_send()

    return _Exchange([g], [jax.ShapeDtypeStruct(g.shape, g.dtype)], start, wait, aliases={0: 0})


def _ex_swap_halves(gw, base=0):
    nb, rl, d = gw.shape

    def copies(ins, outs, send, recv):
        x, y, c, _ = _place()
        return [_remote(ins[0].at[j, _half_rows(rl, 1 - c)], outs[0].at[j], send.at[base + j], recv.at[base + j],
                        (x, y, 1 - c)) for j in range(nb)]

    def start(ins, outs, send, recv, local):
        for cp in copies(ins, outs, send, recv):
            cp.start()

    def wait(ins, outs, send, recv, local):
        for cp in copies(ins, outs, send, recv):
            cp.wait()

    return _Exchange([gw], [jax.ShapeDtypeStruct((nb, rl // 2, d), gw.dtype)], start, wait)


def _chip_sum(gw, got, c_arr, *, name):
    nchip, half, d = got.shape
    tr = _tile(half, 512)

    def body(c_ref, a_ref, b_ref, o32_ref, o16_ref):
        s = a_ref[...] + b_ref[...]
        o32_ref[...] = s
        o16_ref[...] = s.astype(BF16)

    blk = pl.BlockSpec((None, tr, d), lambda j, i, c_ref: (j, i, 0))
    return pl.pallas_call(
        body, name=name,
        grid_spec=pltpu.PrefetchScalarGridSpec(
            num_scalar_prefetch=1, grid=(nchip, half // tr),
            in_specs=[pl.BlockSpec((None, tr, d), lambda j, i, c_ref: (j, c_ref[0] * (half // tr) + i, 0)), blk],
            out_specs=[blk, blk]),
        out_shape=[jax.ShapeDtypeStruct((nchip, half, d), F32), jax.ShapeDtypeStruct((nchip, half, d), BF16)],
        compiler_params=_cp())(c_arr, gw, got)


def _ex_scatter(s16, base=0):
    def copies(ins, outs, send, recv):
        x, y, c, chips = _place()
        return [_remote(ins[0].at[2 * px + py], outs[0].at[k], send.at[base + k], recv.at[base + k], (px, py, c))
                for k, (px, py) in enumerate(chips)]

    def start(ins, outs, send, recv, local):
        for cp in copies(ins, outs, send, recv):
            cp.start()

    def wait(ins, outs, send, recv, local):
        for cp in copies(ins, outs, send, recv):
            cp.wait()

    return _Exchange([s16], [jax.ShapeDtypeStruct((3,) + s16.shape[1:], s16.dtype)], start, wait)


def _mesh_sum(s32, got, j_arr, *, name):
    _, rl, d = s32.shape
    tr = _tile(rl, 512)

    def body(j_ref, a_ref, b_ref, o_ref):
        o_ref[...] = ((a_ref[...] + b_ref[0].astype(F32)) + b_ref[1].astype(F32)) + b_ref[2].astype(F32)

    return pl.pallas_call(
        body, name=name,
        grid_spec=pltpu.PrefetchScalarGridSpec(
            num_scalar_prefetch=1, grid=(rl // tr,),
            in_specs=[pl.BlockSpec((None, tr, d), lambda i, j_ref: (j_ref[0], i, 0)),
                      pl.BlockSpec((3, tr, d), lambda i, j_ref: (0, i, 0))],
            out_specs=pl.BlockSpec((tr, d), lambda i, j_ref: (i, 0))),
        out_shape=jax.ShapeDtypeStruct((rl, d), F32), compiler_params=_cp())(j_arr, s32, got)


def _ex_share_halves(ghalf):
    half, d = ghalf.shape

    def copies(ins, outs, send, recv, local):
        x, y, c, _ = _place()
        there = outs[0].at[_half_rows(2 * half, c)]
        back = outs[0].at[_half_rows(2 * half, 1 - c)]
        return (_remote(ins[0], there, send.at[0], recv.at[0], (x, y, 1 - c)),
                _remote(ins[0], back, send.at[0], recv.at[0], (x, y, 1 - c)), pltpu.make_async_copy(ins[0], there, local))

    def start(ins, outs, send, recv, local):
        out, _, mine = copies(ins, outs, send, recv, local)
        mine.start()
        out.start()

    def wait(ins, outs, send, recv, local):
        out, back, mine = copies(ins, outs, send, recv, local)
        back.wait_recv()
        out.wait_send()
        mine.wait()

    return _Exchange([ghalf], [jax.ShapeDtypeStruct((2 * half, d), ghalf.dtype)], start, wait)


class _ReduceScatter:
    def __init__(self, gw, c_arr, j_arr, tag):
        self.gw, self.c_arr, self.j_arr, self.tag = gw, c_arr, j_arr, tag

    def swap(self, base=0):
        return _ex_swap_halves(self.gw, base)

    def after_swap(self, got, base=0):
        self.s32, s16 = _chip_sum(self.gw, got, self.c_arr, name=f"rs_chip_sum{self.tag}")
        return _ex_scatter(s16, base)

    def after_scatter(self, got16):
        ghalf = _mesh_sum(self.s32, got16, self.j_arr, name=f"rs_mesh_sum{self.tag}")
        return _run_exchange(_ex_share_halves(ghalf), name=f"rs_share{self.tag}")[0]

    def run(self):
        got, = _run_exchange(self.swap(), name=f"rs_swap{self.tag}")
        got16, = _run_exchange(self.after_swap(got), name=f"rs_scatter{self.tag}")
        return self.after_scatter(got16)


def _all_reduce_small(vec, *, name):
    rows, d = vec.shape

    def body(x_ref, o_ref, gat, send_sems, recv_sems, local_sem):
        x, y, c, chips = _place()
        me, sibling = (x, y, c), (x, y, 1 - c)

        def slot(px, py, pc):
            return gat.at[4 * px + 2 * py + pc]

        def copy(k, block, to, src=None):
            return _remote(slot(*block) if src is None else src, slot(*block), send_sems.at[k], recv_sems.at[k], to)

        mine = pltpu.make_async_copy(x_ref, slot(*me), local_sem)
        mine.start()
        first = [copy(0, me, sibling, src=x_ref)]
        first += [copy(1 + j, me, (*chip, c), src=x_ref) for j, chip in enumerate(chips)]
        for cp in first:
            cp.start()
        passed = [copy(4 + j, (*chip, c), sibling) for j, chip in enumerate(chips)]
        for j, chip in enumerate(chips):
            copy(1 + j, (*chip, c), me).wait_recv()
            passed[j].start()
        copy(0, sibling, me).wait_recv()
        for j, chip in enumerate(chips):
            copy(4 + j, (*chip, 1 - c), me).wait_recv()
        for cp in first + passed:
            cp.wait_send()
        mine.wait()
        acc = gat[0]
        for dev in range(1, 8):
            acc = acc + gat[dev]
        o_ref[...] = acc

    vm = pl.BlockSpec(memory_space=pltpu.VMEM)
    return pl.pallas_call(
        body, name=name, in_specs=[vm], out_specs=vm, out_shape=jax.ShapeDtypeStruct((rows, d), F32),
        scratch_shapes=[pltpu.VMEM((8, rows, d), F32), pltpu.SemaphoreType.DMA((7,)), pltpu.SemaphoreType.DMA((7,)),
                        pltpu.SemaphoreType.DMA],
        compiler_params=_cp(32))(vec)


COL_SHARDED = ("w_in", "ffn_w_up")


def _to_pack_rows(name, shard):
    return shard.reshape(-1, D_MODEL)


def _full_from_blocks(name, blocks):
    rows = blocks.shape[1]
    if name in COL_SHARDED:
        return blocks.reshape(4, D_MODEL, rows).transpose(1, 0, 2).reshape(D_MODEL, 4 * rows)
    return blocks.reshape(4 * rows, D_MODEL)


def _blocks_from_full(name, full):
    if name in COL_SHARDED:
        cols = full.shape[1] // 4
        return full.reshape(D_MODEL, 4, cols).transpose(1, 0, 2).reshape(4, cols, D_MODEL)
    return full.reshape(4, full.shape[0] // 4, D_MODEL)


def _row(v):
    return v.reshape(1, -1)


SMALL = (("mix_norm_pre", (1024,), None), ("cv_w", (31, 256), 1), ("cv_b", (256,), None), ("cv_ln_g", (256,), None),
         ("cv_ln_b", (256,), None), ("cv_pw_w", (256, 256), 0), ("cv_pw_b", (256,), None),
         ("mix_norm_post", (1024,), None), ("x_norm_pre", (1024,), None), ("mem_norm", (1024,), None),
         ("x_norm_post", (1024,), None), ("ffn_norm_pre", (1024,), None), ("ffn_conv_w", (3, 5632), 1),
         ("ffn_conv_b", (5632,), None), ("ffn_norm_post", (1024,), None))
BIG = tuple(n for n, _ in PACK_ROWS)
WEIGHT_ORDER = ("mix_norm_pre", "w_in", "cv_w", "cv_b", "cv_ln_g", "cv_ln_b", "cv_pw_w", "cv_pw_b", "w_out",
                "mix_norm_post", "x_norm_pre", "mem_norm", "x_wq", "x_wk", "x_wv", "x_wo", "x_norm_post",
                "ffn_norm_pre", "ffn_w_up", "ffn_conv_w", "ffn_conv_b", "ffn_w_down", "ffn_norm_post")


def _flat_rows(parts):
    v = jnp.concatenate([p.reshape(-1) for p in parts])
    rows = -(-v.shape[0] // (8 * D_MODEL)) * 8
    return jnp.pad(v, (0, rows * D_MODEL - v.shape[0])).reshape(rows, D_MODEL)


REST_GROUP = ("w_in", "w_out")
XA_GROUP = ("x_wq", "x_wk", "x_wv", "x_wo")
FFN_GROUP = ("ffn_w_up", "ffn_w_down")


class _Weights:
    FIRST = (0, 704)
    OWN = ((704, 768), (1472, 1920), (3392, 704))
    NEXT = ((0, 960), (960, 1024), (1984, 1408), (3392, 704))
    SLOTS = ("mix_in", "sb_fwd", "dl_fwd", "dl_mix", "ffn_up", "ffn_gate", "ffn_down")

    def __init__(self, packs):
        self.packs, self.pieces, self.landed, self.plan = packs, {}, None, {}
        for slot, piece in zip(self.SLOTS[:3], self.OWN):
            self.plan[(0, slot)] = (0,) + piece
        for l in range(len(packs) - 1):
            for slot, piece in zip(self.SLOTS[3:], self.NEXT):
                self.plan[(l, slot)] = (l + 1,) + piece
        first = _run_exchange(_ex_gather(packs[0], *self.FIRST), name="gather_first")[0]
        self.pieces[(0,) + self.FIRST] = _run_exchange(_ex_gather_forward(first), name="gather_first_forward")[0]

    def ride(self, layer, slot, call):
        start, todo, ex = self.plan.get((layer, slot)), [], None
        if start is not None:
            ex = _ex_gather(self.packs[start[0]], start[1], start[2])
            todo.append(("landed", start))
        if self.landed is not None:
            key, buf = self.landed
            forward = _ex_gather_forward(buf, SEM_BASE_2 if ex is not None else 0)
            ex = forward if ex is None else _ex_join(ex, forward)
            todo.append(("piece", key))
            self.landed = None
        outs = list(call(carry=ex))
        n = len(outs) - len(todo)
        for (kind, key), buf in zip(todo, outs[n:]):
            if kind == "landed":
                self.landed = (key, buf)
            else:
                self.pieces[key] = buf
        return outs[:n]

    def weight(self, layer, name):
        off = 0
        for n, rows in PACK_ROWS:
            if n == name:
                break
            off += rows
        for (l, r0, nrows), buf in self.pieces.items():
            if l == layer and r0 <= off < r0 + nrows:
                return _full_from_blocks(name, buf[:, off - r0:off - r0 + rows, :])
        raise KeyError(f"{name} of layer {layer} is not gathered yet")


class _Params:
    def __init__(self, weights, layer, small):
        self.weights, self.layer, self.small, self.cache = weights, layer, small, {}

    def __getitem__(self, name):
        if name in self.small:
            return self.small[name]
        if name not in self.cache:
            self.cache[name] = self.weights.weight(self.layer, name)
        return self.cache[name]


def _layer_fwd(h0, mem, p, cos, sin, tag, ride):
    sv = {"h0": h0}
    n1, u = ride("mix_in", functools.partial(_rms_mm, h0, _row(p["mix_norm_pre"]), p["w_in"], tm=1024, tn=1408,
                                             out_dtype=F32, name=f"mix_in{tag}"))
    a_out, = ride("sb_fwd", functools.partial(_sb_fwd, u, name=f"sb_fwd{tag}"))
    b_out, c = _cv_fwd(u, p["cv_w"], _row(p["cv_b"]), _row(p["cv_ln_g"]), _row(p["cv_ln_b"]),
                       p["cv_pw_w"].astype(BF16), _row(p["cv_pw_b"]), name=f"cv_fwd{tag}")
    qkv = _rope_perm(u, cos, sin, name=f"rope_perm{tag}")
    o_p, l_p = ride("dl_fwd", functools.partial(_dl_fwd, qkv, name=f"dl_fwd{tag}"))
    c_out, o_dl, lse = ride("dl_mix", functools.partial(_dl_mix, o_p, l_p, name=f"dl_mix{tag}"))
    cat = jnp.concatenate([a_out, b_out, c_out], axis=1)
    y1, h1 = _mm_post(cat, p["w_out"], h0, _row(p["mix_norm_post"]), tm=512, name=f"mix_out{tag}")
    sv.update(n1=n1, u=u, c=c, qkv=qkv, o_dl=o_dl, lse=lse, cat=cat, y1=y1, h1=h1)

    n2, q = _rms_mm(h1, _row(p["x_norm_pre"]), p["x_wq"], tm=512, tn=1024, out_dtype=BF16, name=f"xa_q{tag}")
    wkv = jnp.concatenate([p["x_wk"], p["x_wv"]], axis=1)
    mem_n, kv = _rms_mm(mem, _row(p["mem_norm"]), wkv, tm=mem.shape[0], tn=1024, out_dtype=BF16, name=f"xa_kv{tag}")
    k, v = kv[:, :D_MODEL], kv[:, D_MODEL:]
    o_x = _xa_fwd(q, k, v, name=f"xa_fwd{tag}")
    y2, h2 = _mm_post(o_x, p["x_wo"], h1, _row(p["x_norm_post"]), tm=512, name=f"xa_out{tag}")
    sv.update(n2=n2, q=q, mem_n=mem_n, k=k, v=v, o_x=o_x, y2=y2, h2=h2, wkv=wkv)

    n3, up = ride("ffn_up", functools.partial(_rms_mm, h2, _row(p["ffn_norm_pre"]), p["ffn_w_up"], tm=1024, tn=1408,
                                              out_dtype=F32, name=f"ffn_up{tag}"))
    act, = ride("ffn_gate", functools.partial(_ff_gate_fwd, up, p["ffn_conv_w"], _row(p["ffn_conv_b"]),
                                              name=f"ffn_gate{tag}"))
    y3, h3 = ride("ffn_down", functools.partial(_mm_post, act, p["ffn_w_down"], h2, _row(p["ffn_norm_post"]), tm=512,
                                                name=f"ffn_down{tag}"))
    sv.update(n3=n3, up=up, act=act, y3=y3)
    return h3, sv


def _layer_bwd(dh3, mem, p, sv, cos, sin, tag, riding, new_rs):
    g = {}
    s8 = lambda part: part.sum(axis=0)
    rode = None

    dy3, dgp = _rms_bwd(sv["y3"], _row(p["ffn_norm_post"]), dh3, None, out_dtype=BF16, tm=512, name=f"ffn_post_b{tag}")
    g["ffn_norm_post"] = s8(dgp)
    dact = _mm_nt(dy3, p["ffn_w_down"], tm=512, tn=1408, out_dtype=F32, name=f"ffn_down_bx{tag}")
    g["ffn_w_down"] = _mm_tn(sv["act"], dy3, tk=1408, tn=1024, tm=1024, name=f"ffn_down_bw{tag}")
    dgu, dvu, dcw, dcb, *got = _ff_gate_bwd(sv["up"], dact, p["ffn_conv_w"], _row(p["ffn_conv_b"]),
                                            name=f"ffn_gate_b{tag}", carry=riding.swap() if riding else None)
    scatter = riding.after_swap(got[0]) if riding else None
    g["ffn_conv_w"] = jnp.concatenate([dcw[0], dcw[1]], axis=1).reshape(3, 8, 2 * D_FF).sum(axis=1)
    g["ffn_conv_b"] = jnp.concatenate([dcb[0], dcb[1]], axis=1).sum(axis=0)
    dup = jnp.concatenate([dgu, dvu], axis=1)
    dn3 = _mm_nt(dup, p["ffn_w_up"], tm=256, tn=512, out_dtype=F32, name=f"ffn_up_bx{tag}")
    g["ffn_w_up"] = _mm_tn(sv["n3"], dup, tk=512, tn=1408, tm=1024, name=f"ffn_up_bw{tag}")
    ffn_rs = new_rs(FFN_GROUP, g, f"{tag}_ffn")
    dh2, dgp = _rms_bwd(sv["h2"], _row(p["ffn_norm_pre"]), dn3, dh3, out_dtype=F32, tm=512, name=f"ffn_pre_b{tag}")
    g["ffn_norm_pre"] = s8(dgp)

    dy2, dgp = _rms_bwd(sv["y2"], _row(p["x_norm_post"]), dh2, None, out_dtype=BF16, tm=512, name=f"xa_post_b{tag}")
    g["x_norm_post"] = s8(dgp)
    do_x = _mm_nt(dy2, p["x_wo"], tm=512, tn=1024, out_dtype=BF16, name=f"xa_out_bx{tag}")
    g["x_wo"] = _mm_tn(sv["o_x"], dy2, tk=512, tn=1024, tm=1024, name=f"xa_out_bw{tag}")
    dq, dk, dv, got = _xa_bwd(sv["q"], sv["k"], sv["v"], do_x, name=f"xa_bwd{tag}", carry=ffn_rs.swap())
    ffn_scatter = ffn_rs.after_swap(got)
    dn2 = _mm_nt(dq, p["x_wq"], tm=512, tn=1024, out_dtype=F32, name=f"xa_q_bx{tag}")
    g["x_wq"] = _mm_tn(sv["n2"], dq, tk=512, tn=1024, tm=1024, name=f"xa_q_bw{tag}")
    dkv = jnp.concatenate([dk, dv], axis=1).astype(BF16)
    nm = mem.shape[0]
    dmem_n = _mm_nt(dkv, sv["wkv"], tm=nm, tn=1024, out_dtype=F32, name=f"xa_kv_bx{tag}")
    dwkv = _mm_tn(sv["mem_n"], dkv, tk=512, tn=2048, tm=nm, name=f"xa_kv_bw{tag}")
    g["x_wk"], g["x_wv"] = dwkv[:, :D_MODEL], dwkv[:, D_MODEL:]
    _, dgp = _rms_bwd(mem, _row(p["mem_norm"]), dmem_n, None, out_dtype=BF16, tm=nm, name=f"xa_mem_b{tag}")
    g["mem_norm"] = s8(dgp)
    xa_rs = new_rs(XA_GROUP, g, f"{tag}_xa")
    dh1, dgp, got = _rms_bwd(sv["h1"], _row(p["x_norm_pre"]), dn2, dh2, out_dtype=F32, tm=512, name=f"xa_pre_b{tag}",
                             carry=xa_rs.swap())
    xa_scatter = xa_rs.after_swap(got, SEM_BASE_2 if riding else 0)
    g["x_norm_pre"] = s8(dgp)

    dy1, dgp = _rms_bwd(sv["y1"], _row(p["mix_norm_post"]), dh1, None, out_dtype=BF16, tm=512, name=f"mix_post_b{tag}")
    g["mix_norm_post"] = s8(dgp)
    dcat = _mm_nt(dy1, p["w_out"], tm=512, tn=1024, out_dtype=F32, name=f"mix_out_bx{tag}")
    g["w_out"] = _mm_tn(sv["cat"], dy1, tk=512, tn=1024, tm=1024, name=f"mix_out_bw{tag}")
    u = sv["u"]
    dq_sb, dk_sb, dv_sb, *got = _sb_bwd(u, dcat, name=f"sb_bwd{tag}",
                                        carry=_ex_join(scatter, xa_scatter) if riding else xa_scatter)
    if riding:
        rode = riding.after_scatter(got[0])
    xa_rows = xa_rs.after_scatter(got[-1])
    pw_b16 = p["cv_pw_w"].astype(BF16)
    dc, dpw, vec = _cv_bwd_local(sv["c"], dcat, _row(p["cv_ln_g"]), _row(p["cv_ln_b"]), pw_b16, name=f"cv_bwd_a{tag}")
    g["cv_pw_w"] = dpw
    vec = vec.reshape(3, 8, CV_W).sum(axis=1)
    g["cv_pw_b"], g["cv_ln_g"], g["cv_ln_b"] = vec[0], vec[1], vec[2]
    du_cv, dcw, dcb = _cv_bwd_conv(u, dc, p["cv_w"], name=f"cv_bwd_b{tag}")
    g["cv_w"] = dcw.reshape(CV_K, 8, CV_W).sum(axis=1)
    g["cv_b"] = dcb.sum(axis=0)
    dop, stats = _dl_bwd_prep(dcat, sv["o_dl"], sv["lse"], name=f"dl_prep_b{tag}")
    cur, prev, got = _dl_bwd(sv["qkv"], dop, stats, name=f"dl_bwd{tag}", carry=ffn_scatter)
    ffn_rows = ffn_rs.after_scatter(got)
    du_dl = _dl_bwd_finish(cur, prev, cos, sin, name=f"dl_fin_b{tag}")
    du = jnp.concatenate([dq_sb.astype(BF16), dk_sb.astype(BF16), dv_sb.astype(BF16), du_cv, du_dl], axis=1)
    dn1 = _mm_nt(du, p["w_in"], tm=512, tn=512, out_dtype=F32, name=f"mix_in_bx{tag}")
    g["w_in"] = _mm_tn(sv["n1"], du, tk=512, tn=1408, tm=1024, name=f"mix_in_bw{tag}")
    dh0, dgp = _rms_bwd(sv["h0"], _row(p["mix_norm_pre"]), dn1, dh1, out_dtype=F32, tm=512, name=f"mix_pre_b{tag}")
    g["mix_norm_pre"] = s8(dgp)
    return dh0, g, (xa_rows, ffn_rows), rode


def _step(x, mem, positions, loss_target, w, m, v):
    depth = w["w_in"].shape[0]
    xi, yi, ci = lax.axis_index("x"), lax.axis_index("y"), lax.axis_index("c")
    chip = 2 * xi + yi
    h = x[0]
    mem0 = mem[0]
    s_len = h.shape[0]

    packs = [jnp.concatenate([_to_pack_rows(n, w[n][l]) for n in BIG], axis=0).astype(BF16) for l in range(depth)]

    small_w = []
    for l in range(depth):
        for n, shape, axis in SMALL:
            if axis is not None:
                full = jnp.zeros(shape, F32)
                full = lax.dynamic_update_slice_in_dim(full, w[n][l], chip * w[n][l].shape[axis], axis)
                small_w.append(full * jnp.where(ci == 0, 1.0, 0.0))
    small_w_sum = _all_reduce_small(_flat_rows(small_w), name="gather_small_weights")
    small_full, off = [{} for _ in range(depth)], 0
    for l in range(depth):
        for n, shape, axis in SMALL:
            if axis is not None:
                size = int(np.prod(shape))
                small_full[l][n] = small_w_sum.reshape(-1)[off:off + size].reshape(shape)
                off += size
    weights = _Weights(packs)
    params = [_Params(weights, l, {n: small_full[l].get(n, w[n][l]) for n, _, _ in SMALL}) for l in range(depth)]

    inv_freq = ROPE_THETA ** (-jnp.arange(HD // 2, dtype=F32) / (HD // 2))
    cos, sin = _rope_tables(positions.reshape(s_len, 1), jnp.tile(inv_freq, 4).reshape(1, LANES), name="rope_tables")

    saved = []
    for l in range(depth):
        h, sv = _layer_fwd(h, mem0, params[l], cos, sin, f"_l{l}", functools.partial(weights.ride, l))
        saved.append(sv)
    dh, sq = _loss_grad(h, loss_target[0], tm=512, name="loss_grad")
    loss = lax.psum(0.5 * jnp.sum(sq) / D_MODEL, ("x", "y", "c"))

    c_arr, j_arr = jnp.reshape(ci, (1,)).astype(jnp.int32), jnp.reshape(chip, (1,)).astype(jnp.int32)

    def new_rs(names, g, tag):
        gw = jnp.concatenate([_blocks_from_full(n, g[n]) for n in names], axis=1)
        return _ReduceScatter(gw, c_arr, j_arr, tag)

    grads, later_rows, rest_rows, pending = [None] * depth, [None] * depth, [None] * depth, None
    for l in reversed(range(depth)):
        dh, grads[l], later_rows[l], rode = _layer_bwd(dh, mem0, params[l], saved[l], cos, sin, f"_l{l}", pending, new_rs)
        if pending is not None:
            rest_rows[l + 1] = rode
        pending = new_rs(REST_GROUP, grads[l], f"_l{l}_rest")
    rest_rows[0] = pending.run()
    gfull = [jnp.concatenate([rest_rows[l], *later_rows[l]], axis=0) for l in range(depth)]
    grad_x = dh[None]

    out_g, out_d, out_m, out_v = {}, {}, {}, {}
    off = 0
    for n, rows in PACK_ROWS:
        shard_shape = w[n].shape
        g_n = jnp.stack([gl[off:off + rows, :] for gl in gfull]).reshape(shard_shape)
        off += rows
        flat = lambda a: a.reshape(-1, shard_shape[-1])
        d_n, m_n, v_n = _adamw(flat(w[n]), flat(g_n), flat(m[n]), flat(v[n]), name=f"adamw_{n}")
        out_g[n], out_d[n], out_m[n], out_v[n] = g_n, d_n.reshape(shard_shape), m_n.reshape(shard_shape), v_n.reshape(shard_shape)

    g_small = _all_reduce_small(_flat_rows([grads[l][n] for l in range(depth) for n, _, _ in SMALL]),
                                name="all_reduce_small_grads").reshape(-1)
    local_g, off = {}, 0
    for l in range(depth):
        for n, shape, axis in SMALL:
            size = int(np.prod(shape))
            full = g_small[off:off + size].reshape(shape)
            off += size
            if axis is not None:
                blk = w[n].shape[1 + axis]
                full = lax.dynamic_slice_in_dim(full, chip * blk, blk, axis)
            local_g.setdefault(n, []).append(full)
    names = [n for n, _, _ in SMALL]
    g_loc = {n: jnp.stack(local_g[n]) for n in names}
    d_s, m_s, v_s = _adamw(_flat_rows([w[n] for n in names]), _flat_rows([g_loc[n] for n in names]),
                           _flat_rows([m[n] for n in names]), _flat_rows([v[n] for n in names]), name="adamw_small")
    off = 0
    for n in names:
        size = int(np.prod(w[n].shape))
        take = lambda a: a.reshape(-1)[off:off + size].reshape(w[n].shape)
        out_g[n], out_d[n], out_m[n], out_v[n] = g_loc[n], take(d_s), take(m_s), take(v_s)
        off += size

    outs = [loss, grad_x]
    for group in (out_g, out_d, out_m, out_v):
        outs += [group[n] for n in WEIGHT_ORDER]
    return tuple(outs)


def kernel(x, mem, positions, mix_norm_pre, w_in, cv_w, cv_b, cv_ln_g, cv_ln_b, cv_pw_w, cv_pw_b, w_out, mix_norm_post, x_norm_pre, mem_norm, x_wq, x_wk, x_wv, x_wo, x_norm_post, ffn_norm_pre, ffn_w_up, ffn_conv_w, ffn_conv_b, ffn_w_down, ffn_norm_post, loss_target, m_mix_norm_pre, m_w_in, m_cv_w, m_cv_b, m_cv_ln_g, m_cv_ln_b, m_cv_pw_w, m_cv_pw_b, m_w_out, m_mix_norm_post, m_x_norm_pre, m_mem_norm, m_x_wq, m_x_wk, m_x_wv, m_x_wo, m_x_norm_post, m_ffn_norm_pre, m_ffn_w_up, m_ffn_conv_w, m_ffn_conv_b, m_ffn_w_down, m_ffn_norm_post, v_mix_norm_pre, v_w_in, v_cv_w, v_cv_b, v_cv_ln_g, v_cv_ln_b, v_cv_pw_w, v_cv_pw_b, v_w_out, v_mix_norm_post, v_x_norm_pre, v_mem_norm, v_x_wq, v_x_wk, v_x_wv, v_x_wo, v_x_norm_post, v_ffn_norm_pre, v_ffn_w_up, v_ffn_conv_w, v_ffn_conv_b, v_ffn_w_down, v_ffn_norm_post):
    w = dict(zip(WEIGHT_ORDER, (mix_norm_pre, w_in, cv_w, cv_b, cv_ln_g, cv_ln_b, cv_pw_w, cv_pw_b, w_out, mix_norm_post, x_norm_pre, mem_norm, x_wq, x_wk, x_wv, x_wo, x_norm_post, ffn_norm_pre, ffn_w_up, ffn_conv_w, ffn_conv_b, ffn_w_down, ffn_norm_post)))
    m = dict(zip(WEIGHT_ORDER, (m_mix_norm_pre, m_w_in, m_cv_w, m_cv_b, m_cv_ln_g, m_cv_ln_b, m_cv_pw_w, m_cv_pw_b, m_w_out, m_mix_norm_post, m_x_norm_pre, m_mem_norm, m_x_wq, m_x_wk, m_x_wv, m_x_wo, m_x_norm_post, m_ffn_norm_pre, m_ffn_w_up, m_ffn_conv_w, m_ffn_conv_b, m_ffn_w_down, m_ffn_norm_post)))
    v = dict(zip(WEIGHT_ORDER, (v_mix_norm_pre, v_w_in, v_cv_w, v_cv_b, v_cv_ln_g, v_cv_ln_b, v_cv_pw_w, v_cv_pw_b, v_w_out, v_mix_norm_post, v_x_norm_pre, v_mem_norm, v_x_wq, v_x_wk, v_x_wv, v_x_wo, v_x_norm_post, v_ffn_norm_pre, v_ffn_w_up, v_ffn_conv_w, v_ffn_conv_b, v_ffn_w_down, v_ffn_norm_post)))
    return _step(x, mem, positions, loss_target, w, m, v)
```

```python
import functools

import jax
import jax.numpy as jnp
import numpy as np
from jax import lax
from jax.experimental import pallas as pl
from jax.experimental.pallas import tpu as pltpu

F32, BF16 = jnp.float32, jnp.bfloat16
MESH = pl.DeviceIdType.MESH
EPS = 1e-6
LANES = 128
BLK = 128
HD = 64
D_MODEL = 1024
D_FF = 2816
SB_W, CV_W, DL_W = 256, 256, 512
CV_K = 31
ROPE_THETA = 10000.0
DILATIONS = (1, 4, 16)
X_HEADS, X_HD = 4, 256
ADAM_LR, ADAM_B1, ADAM_B2, ADAM_EPS, ADAM_WD, ADAM_STEP = 0.001, 0.9, 0.999, 1e-08, 0.01, 10
NEG_INF = float("-inf")
MIB = 1 << 20

PACK_ROWS = (("w_in", 704), ("w_out", 256), ("x_wq", 256), ("x_wk", 256), ("x_wv", 256), ("x_wo", 256),
             ("ffn_w_up", 1408), ("ffn_w_down", 704))
PACK_RL = sum(r for _, r in PACK_ROWS)


def _cp(vmem_mb=48):
    return pltpu.CompilerParams(vmem_limit_bytes=vmem_mb * MIB)


def _dot(a, b):
    return jnp.dot(a, b, preferred_element_type=F32)


def _dot_nt(a, b):
    return lax.dot_general(a, b, (((1,), (1,)), ((), ())), preferred_element_type=F32)


def _dot_tn(a, b):
    return lax.dot_general(a, b, (((0,), (0,)), ((), ())), preferred_element_type=F32)


def _dot_hilo(x, m):
    hi = x.astype(BF16)
    lo = (x - hi.astype(F32)).astype(BF16)
    return _dot(hi, m) + _dot(lo, m)


def _rowsum8(x):
    t, c = x.shape
    return x.reshape(t // 8, 8, c).sum(axis=0)


def _acc_out(ref, i, val):
    @pl.when(i == 0)
    def _():
        ref[...] = val

    @pl.when(i > 0)
    def _():
        ref[...] += val


def _tile(n, cap, mult=8):
    t = min(n, cap)
    while n % t or t % mult:
        t -= 1
    return t


def _rms_mm(x, g, w, *, tm, tn, out_dtype, name, carry=None):
    m, d = x.shape
    n_out = w.shape[1]

    def body(x_ref, g_ref, w_ref, n_ref, o_ref):
        @pl.when(pl.program_id(1) == 0)
        def _():
            xv = x_ref[...]
            r = lax.rsqrt(jnp.mean(xv * xv, axis=-1, keepdims=True) + EPS)
            n_ref[...] = (xv * r * g_ref[...]).astype(BF16)

        o_ref[...] = _dot(n_ref[...], w_ref[...]).astype(out_dtype)

    return _call(
        body, grid=(m // tm, n_out // tn), name=name, carry=carry,
        in_specs=[pl.BlockSpec((tm, d), lambda i, j: (i, 0)), pl.BlockSpec((1, d), lambda i, j: (0, 0)),
                  pl.BlockSpec((d, tn), lambda i, j: (0, j))],
        out_specs=[pl.BlockSpec((tm, d), lambda i, j: (i, 0)), pl.BlockSpec((tm, tn), lambda i, j: (i, j))],
        out_shape=[jax.ShapeDtypeStruct((m, d), BF16), jax.ShapeDtypeStruct((m, n_out), out_dtype)],
        args=(x, g, w))


def _mm_post(a, w, h, g, *, tm, name, carry=None):
    m, k = a.shape
    d = w.shape[1]

    def body(a_ref, w_ref, h_ref, g_ref, y_ref, ho_ref):
        y = _dot(a_ref[...], w_ref[...])
        y_ref[...] = y
        r = lax.rsqrt(jnp.mean(y * y, axis=-1, keepdims=True) + EPS)
        ho_ref[...] = h_ref[...] + y * r * g_ref[...]

    return _call(
        body, grid=(m // tm,), name=name, carry=carry,
        in_specs=[pl.BlockSpec((tm, k), lambda i: (i, 0)), pl.BlockSpec((k, d), lambda i: (0, 0)),
                  pl.BlockSpec((tm, d), lambda i: (i, 0)), pl.BlockSpec((1, d), lambda i: (0, 0))],
        out_specs=[pl.BlockSpec((tm, d), lambda i: (i, 0)), pl.BlockSpec((tm, d), lambda i: (i, 0))],
        out_shape=[jax.ShapeDtypeStruct((m, d), F32), jax.ShapeDtypeStruct((m, d), F32)],
        args=(a, w, h, g))


def _mm_nt(a, w, *, tm, tn, out_dtype, name):
    m, k = a.shape
    n_out = w.shape[0]

    def body(a_ref, w_ref, o_ref):
        o_ref[...] = _dot_nt(a_ref[...], w_ref[...]).astype(out_dtype)

    return pl.pallas_call(
        body, grid=(n_out // tn, m // tm), name=name,
        in_specs=[pl.BlockSpec((tm, k), lambda j, i: (i, 0)), pl.BlockSpec((tn, k), lambda j, i: (j, 0))],
        out_specs=pl.BlockSpec((tm, tn), lambda j, i: (i, j)),
        out_shape=jax.ShapeDtypeStruct((m, n_out), out_dtype),
        compiler_params=_cp())(a, w)


def _mm_tn(x, dy, *, tk, tn, tm, name):
    m, k = x.shape
    n_out = dy.shape[1]

    def body(x_ref, d_ref, o_ref):
        _acc_out(o_ref, pl.program_id(2), _dot_tn(x_ref[...], d_ref[...]))

    return pl.pallas_call(
        body, grid=(k // tk, n_out // tn, m // tm), name=name,
        in_specs=[pl.BlockSpec((tm, tk), lambda a, b, c: (c, a)), pl.BlockSpec((tm, tn), lambda a, b, c: (c, b))],
        out_specs=pl.BlockSpec((tk, tn), lambda a, b, c: (a, b)),
        out_shape=jax.ShapeDtypeStruct((k, n_out), F32),
        compiler_params=_cp())(x, dy)


def _rms_bwd(x, g, dout, res, *, out_dtype, tm, name, carry=None):
    m, d = x.shape
    has_res = res is not None

    def body(*refs):
        if has_res:
            x_ref, g_ref, d_ref, r_ref, dx_ref, dg_ref = refs
        else:
            x_ref, g_ref, d_ref, dx_ref, dg_ref = refs
        xv = x_ref[...]
        dv = d_ref[...].astype(F32)
        r = lax.rsqrt(jnp.mean(xv * xv, axis=-1, keepdims=True) + EPS)
        xh = xv * r
        dxh = dv * g_ref[...]
        dx = r * (dxh - xh * jnp.mean(dxh * xh, axis=-1, keepdims=True))
        if has_res:
            dx = dx + r_ref[...]
        dx_ref[...] = dx.astype(out_dtype)
        _acc_out(dg_ref, pl.program_id(0), _rowsum8(dv * xh))

    row = pl.BlockSpec((tm, d), lambda i: (i, 0))
    ins = [row, pl.BlockSpec((1, d), lambda i: (0, 0)), row] + ([row] if has_res else [])
    args = (x, g, dout) + ((res,) if has_res else ())
    return _call(
        body, grid=(m // tm,), name=name, carry=carry, in_specs=ins,
        out_specs=[row, pl.BlockSpec((8, d), lambda i: (0, 0))],
        out_shape=[jax.ShapeDtypeStruct((m, d), out_dtype), jax.ShapeDtypeStruct((8, d), F32)], args=args)


def _loss_grad(h, tgt, *, tm, name):
    m, d = h.shape

    def body(h_ref, t_ref, dh_ref, p_ref):
        e = h_ref[...] - t_ref[...]
        dh_ref[...] = e / d
        _acc_out(p_ref, pl.program_id(0), _rowsum8(e * e))

    row = pl.BlockSpec((tm, d), lambda i: (i, 0))
    return pl.pallas_call(
        body, grid=(m // tm,), name=name, in_specs=[row, row],
        out_specs=[row, pl.BlockSpec((8, d), lambda i: (0, 0))],
        out_shape=[jax.ShapeDtypeStruct((m, d), F32), jax.ShapeDtypeStruct((8, d), F32)],
        compiler_params=_cp())(h, tgt)


def _adamw(w, g, m, v, *, name):
    r, c = w.shape
    tr = _tile(r, 256)

    def body(w_ref, g_ref, m_ref, v_ref, d_ref, mo_ref, vo_ref):
        gv = g_ref[...]
        m2 = ADAM_B1 * m_ref[...] + (1.0 - ADAM_B1) * gv
        v2 = ADAM_B2 * v_ref[...] + (1.0 - ADAM_B2) * jnp.square(gv)
        m_hat = m2 / (1.0 - ADAM_B1 ** ADAM_STEP)
        v_hat = v2 / (1.0 - ADAM_B2 ** ADAM_STEP)
        d_ref[...] = -ADAM_LR * (m_hat / (jnp.sqrt(v_hat) + ADAM_EPS) + ADAM_WD * w_ref[...])
        mo_ref[...] = m2
        vo_ref[...] = v2

    blk = pl.BlockSpec((tr, c), lambda i: (i, 0))
    return pl.pallas_call(
        body, grid=(r // tr,), name=name, in_specs=[blk] * 4, out_specs=[blk] * 3,
        out_shape=[jax.ShapeDtypeStruct((r, c), F32)] * 3, compiler_params=_cp())(w, g, m, v)


def _head_masks():
    lane = lax.broadcasted_iota(jnp.int32, (BLK, LANES), 1)
    row = lax.broadcasted_iota(jnp.int32, (BLK, LANES), 0)
    return lane, row, lane < HD


def _sb_scores(q_a, k, before):
    z = _dot_nt(q_a, k)
    sp = jnp.log1p(jnp.exp(-jnp.abs(z)))
    ls_pos = jnp.minimum(z, 0.0) - sp
    lkeep = jnp.where(before, ls_pos - z, 0.0)
    return ls_pos, lkeep


SB_DEAD = -104.0


def _sb_alive(jj, i, carry):
    return jnp.logical_and(jj <= i, jnp.max(carry) > SB_DEAD)


SB_QB_FWD = 2
SB_QB = 2


def _sb_before(jj, qb=SB_QB):
    lane = lax.broadcasted_iota(jnp.int32, (qb * 2 * BLK, LANES), 1)
    row = lax.broadcasted_iota(jnp.int32, (qb * 2 * BLK, LANES), 0)
    below_diag = jj - (qb - 1) + row // (2 * BLK)
    return jnp.logical_or(below_diag > 0, jnp.logical_and(below_diag == 0, lane < row % BLK))


def _sb_stack(x, lane_h, qb=SB_QB):
    return jnp.concatenate([_stack_heads(x[b * BLK:(b + 1) * BLK], lane_h) for b in range(qb)], axis=0)


def _sb_unstack(x, lane_h, qb=SB_QB):
    return jnp.concatenate([jnp.where(lane_h, x[2 * b * BLK:(2 * b + 1) * BLK], x[(2 * b + 1) * BLK:(2 * b + 2) * BLK])
                            for b in range(qb)], axis=0)


SB_ROWS = SB_QB * 2 * BLK


def _sb_fwd(u, *, name, carry=None):
    s_len = u.shape[0]
    qb = SB_QB_FWD
    qrows, rows = qb * BLK, qb * 2 * BLK

    def body(q_ref, k_ref, v_ref, o_ref):
        top = pl.program_id(0) * qb + qb - 1
        lane, row, lane_h = _head_masks()
        suffix = (row > lane).astype(BF16)
        pairs = [slice(hp * LANES, (hp + 1) * LANES) for hp in range(2)]
        qs = [_sb_stack(q_ref[:, cs] * 0.125, lane_h, qb) for cs in pairs]

        def step(state):
            jj, ccs, accs = state[0], state[1:3], state[3:5]
            rows_k = pl.ds(pl.multiple_of((top - jj) * BLK, BLK), BLK)
            before = _sb_before(jj, qb)
            scores = [_sb_scores(q, k_ref[rows_k, cs].astype(BF16), before) for q, cs in zip(qs, pairs)]
            between = [_dot_hilo(lkeep, suffix) + cc for (_, lkeep), cc in zip(scores, ccs)]
            atts = [jnp.where(before, jnp.exp(ls_pos + b), 0.0).astype(BF16) for (ls_pos, _), b in zip(scores, between)]
            new_cc = [cc + jnp.sum(lkeep, axis=1, keepdims=True) for (_, lkeep), cc in zip(scores, ccs)]
            new_acc = [acc + _dot(a, v_ref[rows_k, cs].astype(BF16)) for a, acc, cs in zip(atts, accs, pairs)]
            return (jj + 1, *new_cc, *new_acc)

        zc, za = jnp.zeros((rows, 1), F32), jnp.zeros((rows, LANES), F32)
        res = lax.while_loop(lambda st: _sb_alive(st[0], top, jnp.maximum(st[1], st[2])), step,
                             (jnp.int32(0), zc, zc, za, za))
        for hp, cs in enumerate(pairs):
            o_ref[:, cs] = _sb_unstack(res[3 + hp], lane_h, qb).astype(BF16)

    wide = 2 * LANES
    return _call(
        body, grid=(s_len // qrows,), name=name, carry=carry,
        in_specs=[pl.BlockSpec((qrows, wide), lambda i: (i, 0)), pl.BlockSpec((s_len, wide), lambda i: (0, 1)),
                  pl.BlockSpec((s_len, wide), lambda i: (0, 2))],
        out_specs=[pl.BlockSpec((qrows, wide), lambda i: (i, 0))],
        out_shape=[jax.ShapeDtypeStruct((s_len, SB_W), BF16)], args=(u, u, u))


def _sb_bwd(u, dcat, *, name, carry=None):
    s_len = u.shape[0]
    nq = s_len // BLK
    qrows = SB_QB * BLK

    def body(q_ref, k_ref, v_ref, do_ref, dq_ref, dk_ref, dv_ref, g_scr, b_scr):
        step = pl.program_id(1)
        top = step * SB_QB + SB_QB - 1
        lane, row, lane_h = _head_masks()
        suffix = (row > lane).astype(BF16)
        prefix = (row < lane).astype(BF16)
        qf = q_ref[...]
        qs = _sb_stack(qf * 0.125, lane_h)
        qu = _sb_stack(qf, lane_h)
        dos = _sb_stack(do_ref[...], lane_h)

        @pl.when(step == 0)
        def _():
            dk_ref[...] = jnp.zeros_like(dk_ref)
            dv_ref[...] = jnp.zeros_like(dv_ref)

        def down(state):
            jj, cc = state
            j = top - jj
            off = pl.multiple_of(j * BLK, BLK)
            k = k_ref[pl.ds(off, BLK), :].astype(BF16)
            v = v_ref[pl.ds(off, BLK), :].astype(BF16)
            before = _sb_before(jj)
            ls_pos, lkeep = _sb_scores(qs, k, before)
            between = _dot_hilo(lkeep, suffix) + cc
            att = jnp.where(before, jnp.exp(ls_pos + between), 0.0)
            g_scr[j] = att * _dot_nt(dos, v)
            b_scr[j] = jnp.exp(ls_pos)
            dv_ref[pl.ds(off, BLK), :] += _dot_tn(att.astype(BF16), dos)
            return jj + 1, cc + jnp.sum(lkeep, axis=1, keepdims=True)

        zc = jnp.zeros((SB_ROWS, 1), F32)
        visited = lax.while_loop(lambda st: _sb_alive(st[0], top, st[1]), down, (jnp.int32(0), zc))[0]

        def up(j, carry):
            pc, dq = carry
            off = pl.multiple_of(j * BLK, BLK)
            k = k_ref[pl.ds(off, BLK), :].astype(BF16)
            g, beta = g_scr[j], b_scr[j]
            below = _dot_hilo(g, prefix) + pc
            dz = (jnp.where(_sb_before(top - j), g * (1.0 - beta) - beta * below, 0.0) * 0.125).astype(BF16)
            dk_ref[pl.ds(off, BLK), :] += _dot_tn(dz, qu)
            return pc + jnp.sum(g, axis=1, keepdims=True), dq + _dot(dz, k)

        dq = lax.fori_loop(top + 1 - visited, top + 1, up, (zc, jnp.zeros((SB_ROWS, LANES), F32)))[1]
        dq_ref[...] = _sb_unstack(dq, lane_h)

    col = lambda c0: pl.BlockSpec((s_len, LANES), lambda hp, i: (0, c0 + hp))
    blk = pl.BlockSpec((qrows, LANES), lambda hp, i: (i, hp))
    acc = pl.BlockSpec((s_len, LANES), lambda hp, i: (0, hp))
    return _call(
        body, grid=(2, s_len // qrows), name=name, carry=carry, in_specs=[blk, col(2), col(4), blk],
        out_specs=[blk, acc, acc], out_shape=[jax.ShapeDtypeStruct((s_len, SB_W), F32)] * 3,
        scratch_shapes=[pltpu.VMEM((nq, SB_ROWS, LANES), F32), pltpu.VMEM((nq, SB_ROWS, LANES), F32)],
        vmem_mb=56, args=(u, u, u, dcat))


CV_T = 512
CV_H = 32


def _cv_specs(s_len):
    cur = lambda c: pl.BlockSpec((CV_T, CV_W), lambda i: (i, c))
    prev = lambda c: pl.BlockSpec((CV_H, CV_W), lambda i: (jnp.maximum(i * (CV_T // CV_H) - 1, 0), c))
    nxt = lambda c: pl.BlockSpec((CV_H, CV_W),
                                 lambda i: (jnp.minimum((i + 1) * (CV_T // CV_H), s_len // CV_H - 1), c))
    full = lambda r: pl.BlockSpec((r, CV_W), lambda i: (0, 0))
    return cur, prev, nxt, full


def _glu_into(gp_ref, val_ref, gate_ref, valp_ref, gatep_ref, i):
    gp_ref[0:CV_H, :] = jnp.where(i > 0, valp_ref[...] * jax.nn.sigmoid(gatep_ref[...]), 0.0)
    gp_ref[CV_H:, :] = val_ref[...] * jax.nn.sigmoid(gate_ref[...])


def _cv_fwd(u, cv_w, cv_b, ln_g, ln_b, pw_w, pw_b, *, name):
    s_len = u.shape[0]
    cur, prev, _, full = _cv_specs(s_len)

    def body(val_ref, gate_ref, valp_ref, gatep_ref, w_ref, b_ref, g_ref, be_ref, pw_ref, pb_ref,
             o_ref, c_ref, gp_ref):
        _glu_into(gp_ref, val_ref, gate_ref, valp_ref, gatep_ref, pl.program_id(0))
        acc = jnp.zeros((CV_T, CV_W), F32) + b_ref[...]
        for k in range(CV_K):
            acc = acc + w_ref[k:k + 1, :] * gp_ref[pl.ds(CV_H - CV_K + 1 + k, CV_T), :]
        c_ref[...] = acc
        mu = jnp.mean(acc, axis=-1, keepdims=True)
        xc = acc - mu
        xh = xc * lax.rsqrt(jnp.mean(xc * xc, axis=-1, keepdims=True) + EPS)
        a = xh * g_ref[...] + be_ref[...]
        s = a * jax.nn.sigmoid(a)
        o_ref[...] = (_dot(s.astype(BF16), pw_ref[...]) + pb_ref[...]).astype(BF16)

    return pl.pallas_call(
        body, grid=(s_len // CV_T,), name=name,
        in_specs=[cur(3), cur(4), prev(3), prev(4), full(CV_K), full(1), full(1), full(1), full(CV_W), full(1)],
        out_specs=[cur(0), cur(0)],
        out_shape=[jax.ShapeDtypeStruct((s_len, CV_W), BF16), jax.ShapeDtypeStruct((s_len, CV_W), F32)],
        scratch_shapes=[pltpu.VMEM((CV_T + CV_H, CV_W), F32)], compiler_params=_cp())(
            u, u, u, u, cv_w, cv_b, ln_g, ln_b, pw_w, pw_b)


def _cv_bwd_local(c, dcat, ln_g, ln_b, pw_w, *, name):
    s_len = c.shape[0]
    cur, _, _, full = _cv_specs(s_len)

    def body(c_ref, db_ref, g_ref, be_ref, pw_ref, dc_ref, dpw_ref, vec_ref):
        i = pl.program_id(0)
        cv = c_ref[...]
        db = db_ref[...]
        mu = jnp.mean(cv, axis=-1, keepdims=True)
        xc = cv - mu
        rstd = lax.rsqrt(jnp.mean(xc * xc, axis=-1, keepdims=True) + EPS)
        xh = xc * rstd
        a = xh * g_ref[...] + be_ref[...]
        sg = jax.nn.sigmoid(a)
        s = a * sg
        dbb = db.astype(BF16)
        ds = _dot_nt(dbb, pw_ref[...])
        da = ds * (sg * (1.0 + a * (1.0 - sg)))
        dxh = da * g_ref[...]
        dc_ref[...] = rstd * (dxh - jnp.mean(dxh, axis=-1, keepdims=True)
                              - xh * jnp.mean(dxh * xh, axis=-1, keepdims=True))
        _acc_out(dpw_ref, i, _dot_tn(s.astype(BF16), dbb))
        _acc_out(vec_ref, i, jnp.concatenate([_rowsum8(db), _rowsum8(da * xh), _rowsum8(da)], axis=0))

    return pl.pallas_call(
        body, grid=(s_len // CV_T,), name=name,
        in_specs=[cur(0), cur(1), full(1), full(1), full(CV_W)],
        out_specs=[cur(0), full(CV_W), full(24)],
        out_shape=[jax.ShapeDtypeStruct((s_len, CV_W), F32), jax.ShapeDtypeStruct((CV_W, CV_W), F32),
                   jax.ShapeDtypeStruct((24, CV_W), F32)], compiler_params=_cp())(c, dcat, ln_g, ln_b, pw_w)


def _cv_bwd_conv(u, dc, cv_w, *, name):
    s_len = u.shape[0]
    cur, prev, nxt, full = _cv_specs(s_len)
    last = s_len // CV_T - 1

    def body(val_ref, gate_ref, valp_ref, gatep_ref, dc_ref, dcn_ref, w_ref, du_ref, dw_ref, dbias_ref,
             gp_ref, dcp_ref):
        i = pl.program_id(0)
        _glu_into(gp_ref, val_ref, gate_ref, valp_ref, gatep_ref, i)
        dcv = dc_ref[...]
        dcp_ref[0:CV_T, :] = dcv
        dcp_ref[CV_T:, :] = jnp.where(i < last, dcn_ref[...], 0.0)
        dg = jnp.zeros((CV_T, CV_W), F32)
        parts = []
        for k in range(CV_K):
            dg = dg + w_ref[k:k + 1, :] * dcp_ref[pl.ds(CV_K - 1 - k, CV_T), :]
            parts.append(_rowsum8(dcv * gp_ref[pl.ds(CV_H - CV_K + 1 + k, CV_T), :]))
        _acc_out(dw_ref, i, jnp.concatenate(parts, axis=0))
        _acc_out(dbias_ref, i, _rowsum8(dcv))
        val = val_ref[...]
        sg = jax.nn.sigmoid(gate_ref[...])
        du_ref[:, 0:CV_W] = (dg * sg).astype(BF16)
        du_ref[:, CV_W:] = (dg * val * sg * (1.0 - sg)).astype(BF16)

    return pl.pallas_call(
        body, grid=(s_len // CV_T,), name=name,
        in_specs=[cur(3), cur(4), prev(3), prev(4), cur(0), nxt(0), full(CV_K)],
        out_specs=[pl.BlockSpec((CV_T, 2 * CV_W), lambda i: (i, 0)), full(CV_K * 8), full(8)],
        out_shape=[jax.ShapeDtypeStruct((s_len, 2 * CV_W), BF16), jax.ShapeDtypeStruct((CV_K * 8, CV_W), F32),
                   jax.ShapeDtypeStruct((8, CV_W), F32)],
        scratch_shapes=[pltpu.VMEM((CV_T + CV_H, CV_W), F32), pltpu.VMEM((CV_T + CV_H, CV_W), F32)],
        compiler_params=_cp())(u, u, u, u, dc, dc, cv_w)


def _rope_tables(pos_col, inv_freq_row, *, name):
    s_len = pos_col.shape[0]

    def body(p_ref, f_ref, cos_ref, sin_ref):
        ang = p_ref[...].astype(F32) * f_ref[...]
        lane = lax.broadcasted_iota(jnp.int32, (s_len, LANES), 1)
        sn = jnp.sin(ang)
        cos_ref[...] = jnp.cos(ang)
        sin_ref[...] = jnp.where(lane % HD < HD // 2, -sn, sn)

    return pl.pallas_call(body, name=name, out_shape=[jax.ShapeDtypeStruct((s_len, LANES), F32)] * 2,
                          compiler_params=_cp())(pos_col, inv_freq_row)


def _rot_half(x):
    lane = lax.broadcasted_iota(jnp.int32, x.shape, 1)
    return jnp.where(lane % HD < HD // 2, pltpu.roll(x, LANES - HD // 2, 1), pltpu.roll(x, HD // 2, 1))


def _permute_rows(dst_ref, src_ref, d, dtype):
    s_len = src_ref.shape[0]
    seg = s_len // d
    if d == 1:
        dst_ref[...] = src_ref[...].astype(dtype)
        return
    for r in range(d):
        dst_ref[r * seg:(r + 1) * seg, :] = src_ref[pl.ds(r, seg, stride=d), :].astype(dtype)


def _unpermute_rows(dst_ref, src_ref, d):
    s_len = src_ref.shape[0]
    seg = s_len // d
    if d == 1:
        dst_ref[...] = src_ref[...]
        return
    for r in range(d):
        dst_ref[pl.ds(r, seg, stride=d), :] = src_ref[r * seg:(r + 1) * seg, :]


def _rope_perm(u, cos, sin, *, name):
    s_len = u.shape[0]

    def body(x_ref, cos_ref, sin_ref, o_ref, scr):
        a = pl.program_id(0)
        x = x_ref[...]
        rot = a < 2
        scr[...] = x * jnp.where(rot, cos_ref[...], 1.0) + _rot_half(x) * jnp.where(rot, sin_ref[...], 0.0)
        for n, d in enumerate(DILATIONS):
            _permute_rows(o_ref.at[n], scr, d, BF16)

    tab = pl.BlockSpec((s_len, LANES), lambda a, cb: (0, 0))
    return pl.pallas_call(
        body, grid=(3, 4), name=name,
        in_specs=[pl.BlockSpec((s_len, LANES), lambda a, cb: (0, 10 + 4 * a + cb)), tab, tab],
        out_specs=pl.BlockSpec((None, 3, s_len, LANES), lambda a, cb: (a, 0, 0, cb)),
        out_shape=jax.ShapeDtypeStruct((3, 3, s_len, DL_W), BF16),
        scratch_shapes=[pltpu.VMEM((s_len, LANES), F32)], compiler_params=_cp())(u, cos, sin)


DL_UNROLL = 4


def _dl_band(rows):
    lane = lax.broadcasted_iota(jnp.int32, (rows, LANES), 1)
    row = lax.broadcasted_iota(jnp.int32, (rows, LANES), 0) % BLK
    return lane <= row, lane >= row


def _dl_first(s_len, n, i):
    nb = jnp.where(n == 0, s_len // BLK, jnp.where(n == 1, s_len // (BLK * DILATIONS[1]),
                                                   s_len // (BLK * DILATIONS[2])))
    return lax.rem(i, nb) == 0


def _stack_heads(x, lane_h):
    return jnp.concatenate([jnp.where(lane_h, x, 0.0), jnp.where(lane_h, 0.0, x)], axis=0).astype(BF16)


def _dl_rows(i):
    cur = pl.ds(pl.multiple_of(i * BLK, BLK), BLK)
    prev = pl.ds(pl.multiple_of(jnp.maximum(i - 1, 0) * BLK, BLK), BLK)
    return cur, prev


def _dl_in_specs(s_len):
    return [pl.BlockSpec((None, None, s_len, LANES), functools.partial(lambda a, n, hp: (a, n, 0, hp), a))
            for a in range(3)]


def _dl_fwd(qkv, *, name, carry=None):
    s_len = qkv.shape[2]

    def body(q_ref, k_ref, v_ref, o_ref, l_ref):
        n = pl.program_id(0)
        lane_h = _head_masks()[2]
        band_c, band_p = _dl_band(2 * BLK)
        ones = jnp.ones((BLK, LANES), BF16)

        @pl.loop(0, s_len // BLK, step=DL_UNROLL)
        def _(i0):
            blocks = [i0 + t for t in range(DL_UNROLL)]
            rows = [_dl_rows(i) for i in blocks]
            scores = []
            for cur, prev in rows:
                qs = _stack_heads(q_ref[cur, :] * 0.125, lane_h)
                scores.append((_dot_nt(qs, k_ref[cur, :]), _dot_nt(qs, k_ref[prev, :])))
            probs = []
            for i, (sc, sp) in zip(blocks, scores):
                sc = jnp.where(band_c, sc, NEG_INF)
                sp = jnp.where(jnp.logical_and(band_p, jnp.logical_not(_dl_first(s_len, n, i))), sp, NEG_INF)
                m = jnp.max(jnp.maximum(sc, sp), axis=1, keepdims=True)
                probs.append((jnp.exp(sc - m).astype(BF16), jnp.exp(sp - m).astype(BF16), m))
            for (cur, prev), (pc, pp, m) in zip(rows, probs):
                r = (_dot(pc, jnp.concatenate([v_ref[cur, :], ones], axis=1))
                     + _dot(pp, jnp.concatenate([v_ref[prev, :], ones], axis=1)))
                den = jnp.where(lane_h, r[:BLK, LANES:], r[BLK:, LANES:])
                o_ref[cur, :] = jnp.where(lane_h, r[:BLK, :LANES], r[BLK:, :LANES]) / den
                l_ref[cur, :] = jnp.where(lane_h, m[:BLK], m[BLK:]) + jnp.log(den)

    out = pl.BlockSpec((None, s_len, LANES), lambda n, hp: (n, 0, hp))
    return _call(
        body, grid=(3, 4), name=name, carry=carry, in_specs=_dl_in_specs(s_len), out_specs=[out, out],
        out_shape=[jax.ShapeDtypeStruct((3, s_len, DL_W), F32)] * 2, args=(qkv, qkv, qkv))


def _dl_mix(o_p, l_p, *, name, carry=None):
    s_len = o_p.shape[1]

    def body(o_ref, l_ref, ob_ref, of_ref, lt_ref, o_scr, l_scr):
        n = pl.program_id(1)
        for k, d in enumerate(DILATIONS):
            @pl.when(n == k)
            def _(k=k, d=d):
                _unpermute_rows(o_scr.at[k], o_ref, d)
                _unpermute_rows(l_scr.at[k], l_ref, d)

        @pl.when(n == 2)
        def _():
            l0, l1, l2 = l_scr[0], l_scr[1], l_scr[2]
            m = jnp.maximum(jnp.maximum(l0, l1), l2)
            e0, e1, e2 = jnp.exp(l0 - m), jnp.exp(l1 - m), jnp.exp(l2 - m)
            den = e0 + e1 + e2
            o = (e0 / den) * o_scr[0] + (e1 / den) * o_scr[1] + (e2 / den) * o_scr[2]
            of_ref[...] = o
            ob_ref[...] = o.astype(BF16)
            lt_ref[...] = m + jnp.log(den)

    inb = pl.BlockSpec((None, s_len, LANES), lambda cb, n: (n, 0, cb))
    outb = pl.BlockSpec((s_len, LANES), lambda cb, n: (0, cb))
    return _call(
        body, grid=(4, 3), name=name, carry=carry, in_specs=[inb, inb], out_specs=[outb, outb, outb],
        out_shape=[jax.ShapeDtypeStruct((s_len, DL_W), BF16), jax.ShapeDtypeStruct((s_len, DL_W), F32),
                   jax.ShapeDtypeStruct((s_len, DL_W), F32)],
        scratch_shapes=[pltpu.VMEM((3, s_len, LANES), F32), pltpu.VMEM((3, s_len, LANES), F32)], args=(o_p, l_p))


def _dl_bwd_prep(dcat, o, lse, *, name):
    s_len = o.shape[0]

    def body(do_ref, o_ref, l_ref, dop_ref, st_ref, d_scr):
        n = pl.program_id(1)

        @pl.when(n == 0)
        def _():
            r0 = lax.broadcasted_iota(jnp.int32, (LANES, LANES), 0) // HD
            r1 = lax.broadcasted_iota(jnp.int32, (LANES, LANES), 1) // HD
            d_scr[...] = _dot_hilo(do_ref[...] * o_ref[...], (r0 == r1).astype(BF16))

        for k, d in enumerate(DILATIONS):
            @pl.when(n == k)
            def _(d=d):
                _permute_rows(dop_ref, do_ref, d, BF16)
                _permute_rows(st_ref.at[0], d_scr, d, F32)
                _permute_rows(st_ref.at[1], l_ref, d, F32)

    nat = lambda c0: pl.BlockSpec((s_len, LANES), lambda cb, n: (0, c0 + cb))
    return pl.pallas_call(
        body, grid=(4, 3), name=name, in_specs=[nat(4), nat(0), nat(0)],
        out_specs=[pl.BlockSpec((None, s_len, LANES), lambda cb, n: (n, 0, cb)),
                   pl.BlockSpec((2, None, s_len, LANES), lambda cb, n: (0, n, 0, cb))],
        out_shape=[jax.ShapeDtypeStruct((3, s_len, DL_W), BF16), jax.ShapeDtypeStruct((2, 3, s_len, DL_W), F32)],
        scratch_shapes=[pltpu.VMEM((s_len, LANES), F32)], compiler_params=_cp())(dcat, o, lse)


def _dl_bwd(qkv, dop, stats, *, name, carry=None):
    s_len = qkv.shape[2]

    def body(q_ref, k_ref, v_ref, do_ref, st_ref, cur_ref, prev_ref):
        n = pl.program_id(0)
        lane_h = _head_masks()[2]
        band_c, band_p = _dl_band(2 * BLK)

        def per_head(x):
            xr = pltpu.roll(x, HD, 1)
            return jnp.concatenate([jnp.where(lane_h, x, xr), jnp.where(lane_h, xr, x)], axis=0)

        @pl.loop(0, s_len // BLK, step=DL_UNROLL)
        def _(i0):
            blocks = [i0 + t for t in range(DL_UNROLL)]
            rows = [_dl_rows(i) for i in blocks]
            stage1 = []
            for cur, prev in rows:
                qs = _stack_heads(q_ref[cur, :] * 0.125, lane_h)
                dos = _stack_heads(do_ref[cur, :], lane_h)
                kc, kp, vc, vp = k_ref[cur, :], k_ref[prev, :], v_ref[cur, :], v_ref[prev, :]
                stage1.append((qs, dos, _dot_nt(qs, kc), _dot_nt(qs, kp), _dot_nt(dos, vc), _dot_nt(dos, vp)))
            stage2 = []
            for i, (cur, prev), (qs, dos, sc, sp, dpc, dpp) in zip(blocks, rows, stage1):
                lse, delta = per_head(st_ref[1, cur, :]), per_head(st_ref[0, cur, :])
                pc = jnp.where(band_c, jnp.exp(sc - lse), 0.0)
                pp = jnp.where(jnp.logical_and(band_p, jnp.logical_not(_dl_first(s_len, n, i))), jnp.exp(sp - lse), 0.0)
                stage2.append((pc.astype(BF16), pp.astype(BF16), (pc * (dpc - delta)).astype(BF16),
                               (pp * (dpp - delta)).astype(BF16)))
            for (cur, prev), (qs, dos, *_), (pc, pp, dsc, dsp) in zip(rows, stage1, stage2):
                dq = _dot(dsc, k_ref[cur, :]) + _dot(dsp, k_ref[prev, :])
                cur_ref[0, cur, :] = jnp.where(lane_h, dq[:BLK], dq[BLK:]) * 0.125
                cur_ref[1, cur, :] = _dot_tn(dsc, qs)
                cur_ref[2, cur, :] = _dot_tn(pc, dos)
                prev_ref[0, cur, :] = _dot_tn(dsp, qs)
                prev_ref[1, cur, :] = _dot_tn(pp, dos)

    return _call(
        body, grid=(3, 4), name=name, carry=carry,
        in_specs=_dl_in_specs(s_len) + [pl.BlockSpec((None, s_len, LANES), lambda n, hp: (n, 0, hp)),
                                        pl.BlockSpec((2, None, s_len, LANES), lambda n, hp: (0, n, 0, hp))],
        out_specs=[pl.BlockSpec((3, None, s_len, LANES), lambda n, hp: (0, n, 0, hp)),
                   pl.BlockSpec((2, None, s_len, LANES), lambda n, hp: (0, n, 0, hp))],
        out_shape=[jax.ShapeDtypeStruct((3, 3, s_len, DL_W), F32), jax.ShapeDtypeStruct((2, 3, s_len, DL_W), F32)],
        vmem_mb=56, args=(qkv, qkv, qkv, dop, stats))


def _dl_bwd_finish(cur, prev, cos, sin, *, name):
    s_len = cur.shape[2]

    def body(c_ref, p_ref, cos_ref, sin_ref, o_ref, p_scr, u_scr, acc):
        a, n = pl.program_id(0), pl.program_id(2)
        has_prev = jnp.where(a > 0, 1.0, 0.0)
        p_scr[...] = c_ref[...]
        p_scr[0:s_len - BLK, :] += has_prev * p_ref[BLK:, :]
        for k, d in enumerate(DILATIONS):
            @pl.when(n == k)
            def _(k=k, d=d):
                if k == 0:
                    acc[...] = p_scr[...]
                else:
                    _unpermute_rows(u_scr, p_scr, d)
                    acc[...] += u_scr[...]

        @pl.when(n == 2)
        def _():
            dy = acc[...]
            rot = a < 2
            o_ref[...] = (dy * jnp.where(rot, cos_ref[...], 1.0)
                          + _rot_half(dy * jnp.where(rot, sin_ref[...], 0.0))).astype(BF16)

    tab = pl.BlockSpec((s_len, LANES), lambda a, cb, n: (0, 0))
    return pl.pallas_call(
        body, grid=(3, 4, 3), name=name,
        in_specs=[pl.BlockSpec((None, None, s_len, LANES), lambda a, cb, n: (a, n, 0, cb)),
                  pl.BlockSpec((None, None, s_len, LANES), lambda a, cb, n: (jnp.maximum(a - 1, 0), n, 0, cb)),
                  tab, tab],
        out_specs=pl.BlockSpec((s_len, LANES), lambda a, cb, n: (0, 4 * a + cb)),
        out_shape=jax.ShapeDtypeStruct((s_len, 3 * DL_W), BF16),
        scratch_shapes=[pltpu.VMEM((s_len, LANES), F32)] * 3, compiler_params=_cp())(cur, prev, cos, sin)


XA_T = 256


def _xa_probs(q, k):
    s = _dot_nt(q, k) * (X_HD ** -0.5)
    e = jnp.exp(s - jnp.max(s, axis=1, keepdims=True))
    return e / jnp.sum(e, axis=1, keepdims=True)


def _xa_fwd(q, k, v, *, name):
    s_len, d = q.shape
    nm = k.shape[0]

    def body(q_ref, k_ref, v_ref, o_ref):
        for h in range(X_HEADS):
            cs = slice(h * X_HD, (h + 1) * X_HD)
            p = _xa_probs(q_ref[:, cs], k_ref[:, cs])
            o_ref[:, cs] = _dot(p.astype(BF16), v_ref[:, cs]).astype(BF16)

    row = pl.BlockSpec((XA_T, d), lambda i: (i, 0))
    full = pl.BlockSpec((nm, d), lambda i: (0, 0))
    return pl.pallas_call(body, grid=(s_len // XA_T,), name=name, in_specs=[row, full, full], out_specs=row,
                          out_shape=jax.ShapeDtypeStruct((s_len, d), BF16), compiler_params=_cp())(q, k, v)


def _xa_bwd(q, k, v, do, *, name, carry=None):
    s_len, d = q.shape
    nm = k.shape[0]

    def body(q_ref, k_ref, v_ref, do_ref, dq_ref, dk_ref, dv_ref):
        i = pl.program_id(0)
        for h in range(X_HEADS):
            cs = slice(h * X_HD, (h + 1) * X_HD)
            qh, kh, vh, doh = q_ref[:, cs], k_ref[:, cs], v_ref[:, cs], do_ref[:, cs]
            p = _xa_probs(qh, kh)
            dp = _dot_nt(doh, vh)
            ds = (p * (dp - jnp.sum(dp * p, axis=1, keepdims=True)) * (X_HD ** -0.5)).astype(BF16)
            dq_ref[:, cs] = _dot(ds, kh).astype(BF16)
            dkh, dvh = _dot_tn(ds, qh), _dot_tn(p.astype(BF16), doh)

            @pl.when(i == 0)
            def _(cs=cs, dkh=dkh, dvh=dvh):
                dk_ref[:, cs] = dkh
                dv_ref[:, cs] = dvh

            @pl.when(i > 0)
            def _(cs=cs, dkh=dkh, dvh=dvh):
                dk_ref[:, cs] += dkh
                dv_ref[:, cs] += dvh

    row = pl.BlockSpec((XA_T, d), lambda i: (i, 0))
    full = pl.BlockSpec((nm, d), lambda i: (0, 0))
    return _call(
        body, grid=(s_len // XA_T,), name=name, carry=carry, in_specs=[row, full, full, row],
        out_specs=[row, full, full],
        out_shape=[jax.ShapeDtypeStruct((s_len, d), BF16), jax.ShapeDtypeStruct((nm, d), F32),
                   jax.ShapeDtypeStruct((nm, d), F32)], args=(q, k, v, do))


FF_TM, FF_TN, FF_H = 512, 256, 8
GELU_K, GELU_C = 0.7978845608028654, 0.044715


FF_STRIP = 64


def _ff_conv(e_ref, w_ref, b_ref, rows, r0=0):
    return (w_ref[0:1, :] * e_ref[pl.ds(FF_H - 2 + r0, rows), :] + w_ref[1:2, :] * e_ref[pl.ds(FF_H - 1 + r0, rows), :]
            + w_ref[2:3, :] * e_ref[pl.ds(FF_H + r0, rows), :] + b_ref[...])


def _strips(total, size):
    return [(r0, min(size, total - r0)) for r0 in range(0, total, size)]


def _ff_gate_fwd(up, conv_w, conv_b, *, name, carry=None):
    s_len = up.shape[0]
    nj = D_FF // FF_TN

    def body(g_ref, v_ref, gp_ref, vp_ref, wg_ref, wv_ref, bg_ref, bv_ref, o_ref, eg, ev):
        i = pl.program_id(0)
        for e, cur, prev in ((eg, g_ref, gp_ref), (ev, v_ref, vp_ref)):
            e[0:FF_H, :] = jnp.where(i > 0, prev[...], 0.0)
            e[FF_H:, :] = cur[...]
        for r0, rows in _strips(FF_TM, FF_STRIP):
            gate = _ff_conv(eg, wg_ref, bg_ref, rows, r0)
            val = _ff_conv(ev, wv_ref, bv_ref, rows, r0)
            t = jnp.tanh(GELU_K * (gate + GELU_C * gate * gate * gate))
            o_ref[r0:r0 + rows, :] = (0.5 * gate * (1.0 + t) * val).astype(BF16)

    cur = lambda c0: pl.BlockSpec((FF_TM, FF_TN), lambda i, j: (i, c0 + j))
    prev = lambda c0: pl.BlockSpec((FF_H, FF_TN), lambda i, j: (jnp.maximum(i * (FF_TM // FF_H) - 1, 0), c0 + j))
    par = lambda r, c0: pl.BlockSpec((r, FF_TN), lambda i, j: (0, c0 + j))
    return _call(
        body, grid=(s_len // FF_TM, nj), name=name, carry=carry,
        in_specs=[cur(0), cur(nj), prev(0), prev(nj), par(3, 0), par(3, nj), par(1, 0), par(1, nj)],
        out_specs=[cur(0)], out_shape=[jax.ShapeDtypeStruct((s_len, D_FF), BF16)],
        scratch_shapes=[pltpu.VMEM((FF_TM + FF_H, FF_TN), F32)] * 2,
        args=(up, up, up, up, conv_w, conv_w, conv_b, conv_b))


def _ff_gate_bwd(up, dact, conv_w, conv_b, *, name, carry=None):
    s_len = up.shape[0]
    nj = D_FF // FF_TN
    last = s_len // FF_TM - 1
    ext = FF_TM + FF_H

    def body(g_ref, v_ref, gp_ref, vp_ref, gn_ref, vn_ref, da_ref, dan_ref, wg_ref, wv_ref, bg_ref, bv_ref,
             dg_ref, dv_ref, dw_ref, db_ref, eg, ev, sg, sv):
        i = pl.program_id(1)
        for e, cur, prev, nxt in ((eg, g_ref, gp_ref, gn_ref), (ev, v_ref, vp_ref, vn_ref)):
            e[0:FF_H, :] = jnp.where(i > 0, prev[...], 0.0)
            e[FF_H:FF_H + FF_TM, :] = cur[...]
            e[FF_H + FF_TM:, :] = nxt[...]
        for r0, rows in _strips(ext, FF_STRIP):
            gate = _ff_conv(eg, wg_ref, bg_ref, rows, r0)
            val = _ff_conv(ev, wv_ref, bv_ref, rows, r0)
            dact = da_ref[r0:r0 + rows, :] if r0 < FF_TM else jnp.where(i < last, dan_ref[...], 0.0)
            t = jnp.tanh(GELU_K * (gate + GELU_C * gate * gate * gate))
            half = 0.5 * (1.0 + t)
            dgelu = half + 0.5 * gate * (1.0 - t * t) * GELU_K * (1.0 + 3.0 * GELU_C * gate * gate)
            sg[r0:r0 + rows, :] = dact * val * dgelu
            sv[r0:r0 + rows, :] = dact * (gate * half)
        for part, (s, e, w_ref, out) in enumerate(((sg, eg, wg_ref, dg_ref), (sv, ev, wv_ref, dv_ref))):
            taps, bias = [jnp.zeros((8, FF_TN), F32)] * 3, jnp.zeros((8, FF_TN), F32)
            for r0, rows in _strips(FF_TM, FF_STRIP):
                d0 = s[pl.ds(r0, rows), :]
                out[r0:r0 + rows, :] = (w_ref[2:3, :] * d0 + w_ref[1:2, :] * s[pl.ds(r0 + 1, rows), :]
                                        + w_ref[0:1, :] * s[pl.ds(r0 + 2, rows), :]).astype(BF16)
                taps = [taps[k] + _rowsum8(d0 * e[pl.ds(FF_H - 2 + k + r0, rows), :]) for k in range(3)]
                bias = bias + _rowsum8(d0)
            _acc_out(dw_ref.at[part], i, jnp.concatenate(taps, axis=0))
            _acc_out(db_ref.at[part], i, bias)

    cur = lambda c0: pl.BlockSpec((FF_TM, FF_TN), lambda j, i: (i, c0 + j))
    prev = lambda c0: pl.BlockSpec((FF_H, FF_TN), lambda j, i: (jnp.maximum(i * (FF_TM // FF_H) - 1, 0), c0 + j))
    nxt = lambda c0: pl.BlockSpec(
        (FF_H, FF_TN), lambda j, i: (jnp.minimum((i + 1) * (FF_TM // FF_H), s_len // FF_H - 1), c0 + j))
    par = lambda r, c0: pl.BlockSpec((r, FF_TN), lambda j, i: (0, c0 + j))
    return _call(
        body, grid=(nj, s_len // FF_TM), name=name, carry=carry,
        in_specs=[cur(0), cur(nj), prev(0), prev(nj), nxt(0), nxt(nj), cur(0), nxt(0),
                  par(3, 0), par(3, nj), par(1, 0), par(1, nj)],
        out_specs=[cur(0), cur(0), pl.BlockSpec((2, 24, FF_TN), lambda j, i: (0, 0, j)),
                   pl.BlockSpec((2, 8, FF_TN), lambda j, i: (0, 0, j))],
        out_shape=[jax.ShapeDtypeStruct((s_len, D_FF), BF16), jax.ShapeDtypeStruct((s_len, D_FF), BF16),
                   jax.ShapeDtypeStruct((2, 24, D_FF), F32), jax.ShapeDtypeStruct((2, 8, D_FF), F32)],
        scratch_shapes=[pltpu.VMEM((FF_TM + 2 * FF_H, FF_TN), F32)] * 2 + [pltpu.VMEM((ext, FF_TN), F32)] * 2,
        args=(up, up, up, up, up, up, dact, dact, conv_w, conv_w, conv_b, conv_b))


def _place():
    x, y, c = lax.axis_index("x"), lax.axis_index("y"), lax.axis_index("c")
    return x, y, c, [(1 - x, y), (x, 1 - y), (1 - x, 1 - y)]


def _remote(src, dst, send_sem, recv_sem, dev):
    return pltpu.make_async_remote_copy(src_ref=src, dst_ref=dst, send_sem=send_sem, recv_sem=recv_sem,
                                        device_id=dev, device_id_type=MESH)


_ANY = pl.BlockSpec(memory_space=pl.ANY)


N_SEMS = 8
SEM_BASE_2 = 4


class _Exchange:
    def __init__(self, operands, out_shapes, start, wait, aliases=None):
        self.operands, self.out_shapes, self.start, self.wait = list(operands), list(out_shapes), start, wait
        self.aliases = aliases or {}


def _sem_scratch():
    return [pltpu.SemaphoreType.DMA((N_SEMS,)), pltpu.SemaphoreType.DMA((N_SEMS,)), pltpu.SemaphoreType.DMA]


def _run_exchange(ex, *, name):
    k, n = len(ex.operands), len(ex.out_shapes)

    def body(*refs):
        ins, outs, sems = refs[:k], refs[k:k + n], refs[k + n:]
        ex.start(ins, outs, *sems)
        ex.wait(ins, outs, *sems)

    return pl.pallas_call(body, name=name, in_specs=[_ANY] * k, out_specs=[_ANY] * n, out_shape=ex.out_shapes,
                          scratch_shapes=_sem_scratch(), input_output_aliases=ex.aliases,
                          compiler_params=_cp(16))(*ex.operands)


def _call(body, *, grid, in_specs, out_specs, out_shape, args, name, scratch_shapes=(), vmem_mb=48, carry=None):
    scratch_shapes = list(scratch_shapes)
    if carry is None:
        return pl.pallas_call(body, grid=grid, name=name, in_specs=in_specs, out_specs=out_specs, out_shape=out_shape,
                              scratch_shapes=scratch_shapes, compiler_params=_cp(vmem_mb))(*args)
    n_in, n_out, n_scr = len(in_specs), len(out_shape), len(scratch_shapes)
    k_in, k_out = len(carry.operands), len(carry.out_shapes)

    def wrapped(*refs):
        ins, refs = refs[:n_in], refs[n_in:]
        cin, refs = refs[:k_in], refs[k_in:]
        outs, refs = refs[:n_out], refs[n_out:]
        cout, refs = refs[:k_out], refs[k_out:]
        scratch, sems = refs[:n_scr], refs[n_scr:]
        ids = [pl.program_id(a) for a in range(len(grid))]
        first = functools.reduce(jnp.logical_and, [i == 0 for i in ids])
        last = functools.reduce(jnp.logical_and, [i == g - 1 for i, g in zip(ids, grid)])

        @pl.when(first)
        def _():
            carry.start(cin, cout, *sems)

        body(*ins, *outs, *scratch)

        @pl.when(last)
        def _():
            carry.wait(cin, cout, *sems)

    aliases = {n_in + i: n_out + o for i, o in carry.aliases.items()}
    return pl.pallas_call(
        wrapped, grid=grid, name=name, in_specs=list(in_specs) + [_ANY] * k_in,
        out_specs=list(out_specs) + [_ANY] * k_out, out_shape=list(out_shape) + carry.out_shapes,
        scratch_shapes=scratch_shapes + _sem_scratch(), input_output_aliases=aliases,
        compiler_params=_cp(vmem_mb))(*args, *carry.operands)


def _half_rows(ref_rows, c):
    half = ref_rows // 2
    return pl.ds(c * half, half)


def _ex_join(a, b):
    ka, na = len(a.operands), len(a.out_shapes)

    def start(ins, outs, *sems):
        a.start(ins[:ka], outs[:na], *sems)
        b.start(ins[ka:], outs[na:], *sems)

    def wait(ins, outs, *sems):
        a.wait(ins[:ka], outs[:na], *sems)
        b.wait(ins[ka:], outs[na:], *sems)

    aliases = dict(a.aliases)
    aliases.update({ka + i: na + o for i, o in b.aliases.items()})
    return _Exchange(a.operands + b.operands, a.out_shapes + b.out_shapes, start, wait, aliases)


def _ex_gather(pack, r0, rl, base=0):
    def copies(ins, outs, send, recv):
        x, y, c, chips = _place()
        rows = _half_rows(rl, c)
        src = ins[0].at[pl.ds(r0 + c * (rl // 2), rl // 2)]
        sends = [_remote(src, outs[0].at[2 * x + y, rows], send.at[base + k], recv.at[base + k], (px, py, c))
                 for k, (px, py) in enumerate(chips)]
        lands = [_remote(src, outs[0].at[2 * px + py, rows], send.at[base + k], recv.at[base + k], (px, py, c))
                 for k, (px, py) in enumerate(chips)]
        return sends, lands

    def mine(ins, outs, local):
        x, y, _, _ = _place()
        return pltpu.make_async_copy(ins[0].at[pl.ds(r0, rl)], outs[0].at[2 * x + y], local)

    def start(ins, outs, send, recv, local):
        mine(ins, outs, local).start()
        for cp in copies(ins, outs, send, recv)[0]:
            cp.start()

    def wait(ins, outs, send, recv, local):
        sends, lands = copies(ins, outs, send, recv)
        for cp in lands:
            cp.wait_recv()
        for cp in sends:
            cp.wait_send()
        mine(ins, outs, local).wait()

    return _Exchange([pack], [jax.ShapeDtypeStruct((4, rl, pack.shape[1]), pack.dtype)], start, wait)


def _ex_gather_forward(g, base=0):
    rl = g.shape[1]

    def copies(outs, send, recv):
        x, y, c, chips = _place()
        slabs = [(outs[0].at[2 * px + py, _half_rows(rl, c)], outs[0].at[2 * px + py, _half_rows(rl, 1 - c)])
                 for px, py in chips]
        sends = [_remote(a, a, send.at[base + k], recv.at[base + k], (x, y, 1 - c)) for k, (a, _) in enumerate(slabs)]
        lands = [_remote(b, b, send.at[base + k], recv.at[base + k], (x, y, 1 - c)) for k, (_, b) in enumerate(slabs)]
        return sends, lands

    def start(ins, outs, send, recv, local):
        for cp in copies(outs, send, recv)[0]:
            cp.start()

    def wait(ins, outs, send, recv, local):
        sends, lands = copies(outs, send, recv)
        for cp in lands:
            cp.wait_recv()
        for cp in sends:
            cp.wait_send()

    return _Exchange([g], [jax.ShapeDtypeStruct(g.shape, g.dtype)], start, wait, aliases={0: 0})


def _ex_swap_halves(gw, base=0):
    nb, rl, d = gw.shape

    def copies(ins, outs, send, recv):
        x, y, c, _ = _place()
        return [_remote(ins[0].at[j, _half_rows(rl, 1 - c)], outs[0].at[j], send.at[base + j], recv.at[base + j],
                        (x, y, 1 - c)) for j in range(nb)]

    def start(ins, outs, send, recv, local):
        for cp in copies(ins, outs, send, recv):
            cp.start()

    def wait(ins, outs, send, recv, local):
        for cp in copies(ins, outs, send, recv):
            cp.wait()

    return _Exchange([gw], [jax.ShapeDtypeStruct((nb, rl // 2, d), gw.dtype)], start, wait)


def _chip_sum(gw, got, c_arr, *, name):
    nchip, half, d = got.shape
    tr = _tile(half, 512)

    def body(c_ref, a_ref, b_ref, o32_ref, o16_ref):
        s = a_ref[...] + b_ref[...]
        o32_ref[...] = s
        o16_ref[...] = s.astype(BF16)

    blk = pl.BlockSpec((None, tr, d), lambda j, i, c_ref: (j, i, 0))
    return pl.pallas_call(
        body, name=name,
        grid_spec=pltpu.PrefetchScalarGridSpec(
            num_scalar_prefetch=1, grid=(nchip, half // tr),
            in_specs=[pl.BlockSpec((None, tr, d), lambda j, i, c_ref: (j, c_ref[0] * (half // tr) + i, 0)), blk],
            out_specs=[blk, blk]),
        out_shape=[jax.ShapeDtypeStruct((nchip, half, d), F32), jax.ShapeDtypeStruct((nchip, half, d), BF16)],
        compiler_params=_cp())(c_arr, gw, got)


def _ex_scatter(s16, base=0):
    def copies(ins, outs, send, recv):
        x, y, c, chips = _place()
        return [_remote(ins[0].at[2 * px + py], outs[0].at[k], send.at[base + k], recv.at[base + k], (px, py, c))
                for k, (px, py) in enumerate(chips)]

    def start(ins, outs, send, recv, local):
        for cp in copies(ins, outs, send, recv):
            cp.start()

    def wait(ins, outs, send, recv, local):
        for cp in copies(ins, outs, send, recv):
            cp.wait()

    return _Exchange([s16], [jax.ShapeDtypeStruct((3,) + s16.shape[1:], s16.dtype)], start, wait)


def _mesh_sum(s32, got, j_arr, *, name):
    _, rl, d = s32.shape
    tr = _tile(rl, 512)

    def body(j_ref, a_ref, b_ref, o_ref):
        o_ref[...] = ((a_ref[...] + b_ref[0].astype(F32)) + b_ref[1].astype(F32)) + b_ref[2].astype(F32)

    return pl.pallas_call(
        body, name=name,
        grid_spec=pltpu.PrefetchScalarGridSpec(
            num_scalar_prefetch=1, grid=(rl // tr,),
            in_specs=[pl.BlockSpec((None, tr, d), lambda i, j_ref: (j_ref[0], i, 0)),
                      pl.BlockSpec((3, tr, d), lambda i, j_ref: (0, i, 0))],
            out_specs=pl.BlockSpec((tr, d), lambda i, j_ref: (i, 0))),
        out_shape=jax.ShapeDtypeStruct((rl, d), F32), compiler_params=_cp())(j_arr, s32, got)


def _ex_share_halves(ghalf):
    half, d = ghalf.shape

    def copies(ins, outs, send, recv, local):
        x, y, c, _ = _place()
        there = outs[0].at[_half_rows(2 * half, c)]
        back = outs[0].at[_half_rows(2 * half, 1 - c)]
        return (_remote(ins[0], there, send.at[0], recv.at[0], (x, y, 1 - c)),
                _remote(ins[0], back, send.at[0], recv.at[0], (x, y, 1 - c)), pltpu.make_async_copy(ins[0], there, local))

    def start(ins, outs, send, recv, local):
        out, _, mine = copies(ins, outs, send, recv, local)
        mine.start()
        out.start()

    def wait(ins, outs, send, recv, local):
        out, back, mine = copies(ins, outs, send, recv, local)
        back.wait_recv()
        out.wait_send()
        mine.wait()

    return _Exchange([ghalf], [jax.ShapeDtypeStruct((2 * half, d), ghalf.dtype)], start, wait)


class _ReduceScatter:
    def __init__(self, gw, c_arr, j_arr, tag):
        self.gw, self.c_arr, self.j_arr, self.tag = gw, c_arr, j_arr, tag

    def swap(self, base=0):
        return _ex_swap_halves(self.gw, base)

    def after_swap(self, got, base=0):
        self.s32, s16 = _chip_sum(self.gw, got, self.c_arr, name=f"rs_chip_sum{self.tag}")
        return _ex_scatter(s16, base)

    def after_scatter(self, got16):
        ghalf = _mesh_sum(self.s32, got16, self.j_arr, name=f"rs_mesh_sum{self.tag}")
        return _run_exchange(_ex_share_halves(ghalf), name=f"rs_share{self.tag}")[0]

    def run(self):
        got, = _run_exchange(self.swap(), name=f"rs_swap{self.tag}")
        got16, = _run_exchange(self.after_swap(got), name=f"rs_scatter{self.tag}")
        return self.after_scatter(got16)


def _all_reduce_small(vec, *, name):
    rows, d = vec.shape

    def body(x_ref, o_ref, gat, send_sems, recv_sems, local_sem):
        x, y, c, chips = _place()
        me, sibling = (x, y, c), (x, y, 1 - c)

        def slot(px, py, pc):
            return gat.at[4 * px + 2 * py + pc]

        def copy(k, block, to, src=None):
            return _remote(slot(*block) if src is None else src, slot(*block), send_sems.at[k], recv_sems.at[k], to)

        mine = pltpu.make_async_copy(x_ref, slot(*me), local_sem)
        mine.start()
        first = [copy(0, me, sibling, src=x_ref)]
        first += [copy(1 + j, me, (*chip, c), src=x_ref) for j, chip in enumerate(chips)]
        for cp in first:
            cp.start()
        passed = [copy(4 + j, (*chip, c), sibling) for j, chip in enumerate(chips)]
        for j, chip in enumerate(chips):
            copy(1 + j, (*chip, c), me).wait_recv()
            passed[j].start()
        copy(0, sibling, me).wait_recv()
        for j, chip in enumerate(chips):
            copy(4 + j, (*chip, 1 - c), me).wait_recv()
        for cp in first + passed:
            cp.wait_send()
        mine.wait()
        acc = gat[0]
        for dev in range(1, 8):
            acc = acc + gat[dev]
        o_ref[...] = acc

    vm = pl.BlockSpec(memory_space=pltpu.VMEM)
    return pl.pallas_call(
        body, name=name, in_specs=[vm], out_specs=vm, out_shape=jax.ShapeDtypeStruct((rows, d), F32),
        scratch_shapes=[pltpu.VMEM((8, rows, d), F32), pltpu.SemaphoreType.DMA((7,)), pltpu.SemaphoreType.DMA((7,)),
                        pltpu.SemaphoreType.DMA],
        compiler_params=_cp(32))(vec)


COL_SHARDED = ("w_in", "ffn_w_up")


def _to_pack_rows(name, shard):
    return shard.reshape(-1, D_MODEL)


def _full_from_blocks(name, blocks):
    rows = blocks.shape[1]
    if name in COL_SHARDED:
        return blocks.reshape(4, D_MODEL, rows).transpose(1, 0, 2).reshape(D_MODEL, 4 * rows)
    return blocks.reshape(4 * rows, D_MODEL)


def _blocks_from_full(name, full):
    if name in COL_SHARDED:
        cols = full.shape[1] // 4
        return full.reshape(D_MODEL, 4, cols).transpose(1, 0, 2).reshape(4, cols, D_MODEL)
    return full.reshape(4, full.shape[0] // 4, D_MODEL)


def _row(v):
    return v.reshape(1, -1)


SMALL = (("mix_norm_pre", (1024,), None), ("cv_w", (31, 256), 1), ("cv_b", (256,), None), ("cv_ln_g", (256,), None),
         ("cv_ln_b", (256,), None), ("cv_pw_w", (256, 256), 0), ("cv_pw_b", (256,), None),
         ("mix_norm_post", (1024,), None), ("x_norm_pre", (1024,), None), ("mem_norm", (1024,), None),
         ("x_norm_post", (1024,), None), ("ffn_norm_pre", (1024,), None), ("ffn_conv_w", (3, 5632), 1),
         ("ffn_conv_b", (5632,), None), ("ffn_norm_post", (1024,), None))
BIG = tuple(n for n, _ in PACK_ROWS)
WEIGHT_ORDER = ("mix_norm_pre", "w_in", "cv_w", "cv_b", "cv_ln_g", "cv_ln_b", "cv_pw_w", "cv_pw_b", "w_out",
                "mix_norm_post", "x_norm_pre", "mem_norm", "x_wq", "x_wk", "x_wv", "x_wo", "x_norm_post",
                "ffn_norm_pre", "ffn_w_up", "ffn_conv_w", "ffn_conv_b", "ffn_w_down", "ffn_norm_post")


def _flat_rows(parts):
    v = jnp.concatenate([p.reshape(-1) for p in parts])
    rows = -(-v.shape[0] // (8 * D_MODEL)) * 8
    return jnp.pad(v, (0, rows * D_MODEL - v.shape[0])).reshape(rows, D_MODEL)


REST_GROUP = ("w_in", "w_out")
XA_GROUP = ("x_wq", "x_wk", "x_wv", "x_wo")
FFN_GROUP = ("ffn_w_up", "ffn_w_down")


class _Weights:
    FIRST = (0, 704)
    OWN = ((704, 768), (1472, 1920), (3392, 704))
    NEXT = ((0, 960), (960, 1024), (1984, 1408), (3392, 704))
    SLOTS = ("mix_in", "sb_fwd", "dl_fwd", "dl_mix", "ffn_up", "ffn_gate", "ffn_down")

    def __init__(self, packs):
        self.packs, self.pieces, self.landed, self.plan = packs, {}, None, {}
        for slot, piece in zip(self.SLOTS[:3], self.OWN):
            self.plan[(0, slot)] = (0,) + piece
        for l in range(len(packs) - 1):
            for slot, piece in zip(self.SLOTS[3:], self.NEXT):
                self.plan[(l, slot)] = (l + 1,) + piece
        first = _run_exchange(_ex_gather(packs[0], *self.FIRST), name="gather_first")[0]
        self.pieces[(0,) + self.FIRST] = _run_exchange(_ex_gather_forward(first), name="gather_first_forward")[0]

    def ride(self, layer, slot, call):
        start, todo, ex = self.plan.get((layer, slot)), [], None
        if start is not None:
            ex = _ex_gather(self.packs[start[0]], start[1], start[2])
            todo.append(("landed", start))
        if self.landed is not None:
            key, buf = self.landed
            forward = _ex_gather_forward(buf, SEM_BASE_2 if ex is not None else 0)
            ex = forward if ex is None else _ex_join(ex, forward)
            todo.append(("piece", key))
            self.landed = None
        outs = list(call(carry=ex))
        n = len(outs) - len(todo)
        for (kind, key), buf in zip(todo, outs[n:]):
            if kind == "landed":
                self.landed = (key, buf)
            else:
                self.pieces[key] = buf
        return outs[:n]

    def weight(self, layer, name):
        off = 0
        for n, rows in PACK_ROWS:
            if n == name:
                break
            off += rows
        for (l, r0, nrows), buf in self.pieces.items():
            if l == layer and r0 <= off < r0 + nrows:
                return _full_from_blocks(name, buf[:, off - r0:off - r0 + rows, :])
        raise KeyError(f"{name} of layer {layer} is not gathered yet")


class _Params:
    def __init__(self, weights, layer, small):
        self.weights, self.layer, self.small, self.cache = weights, layer, small, {}

    def __getitem__(self, name):
        if name in self.small:
            return self.small[name]
        if name not in self.cache:
            self.cache[name] = self.weights.weight(self.layer, name)
        return self.cache[name]


def _layer_fwd(h0, mem, p, cos, sin, tag, ride):
    sv = {"h0": h0}
    n1, u = ride("mix_in", functools.partial(_rms_mm, h0, _row(p["mix_norm_pre"]), p["w_in"], tm=1024, tn=1408,
                                             out_dtype=F32, name=f"mix_in{tag}"))
    a_out, = ride("sb_fwd", functools.partial(_sb_fwd, u, name=f"sb_fwd{tag}"))
    b_out, c = _cv_fwd(u, p["cv_w"], _row(p["cv_b"]), _row(p["cv_ln_g"]), _row(p["cv_ln_b"]),
                       p["cv_pw_w"].astype(BF16), _row(p["cv_pw_b"]), name=f"cv_fwd{tag}")
    qkv = _rope_perm(u, cos, sin, name=f"rope_perm{tag}")
    o_p, l_p = ride("dl_fwd", functools.partial(_dl_fwd, qkv, name=f"dl_fwd{tag}"))
    c_out, o_dl, lse = ride("dl_mix", functools.partial(_dl_mix, o_p, l_p, name=f"dl_mix{tag}"))
    cat = jnp.concatenate([a_out, b_out, c_out], axis=1)
    y1, h1 = _mm_post(cat, p["w_out"], h0, _row(p["mix_norm_post"]), tm=512, name=f"mix_out{tag}")
    sv.update(n1=n1, u=u, c=c, qkv=qkv, o_dl=o_dl, lse=lse, cat=cat, y1=y1, h1=h1)

    n2, q = _rms_mm(h1, _row(p["x_norm_pre"]), p["x_wq"], tm=512, tn=1024, out_dtype=BF16, name=f"xa_q{tag}")
    wkv = jnp.concatenate([p["x_wk"], p["x_wv"]], axis=1)
    mem_n, kv = _rms_mm(mem, _row(p["mem_norm"]), wkv, tm=mem.shape[0], tn=1024, out_dtype=BF16, name=f"xa_kv{tag}")
    k, v = kv[:, :D_MODEL], kv[:, D_MODEL:]
    o_x = _xa_fwd(q, k, v, name=f"xa_fwd{tag}")
    y2, h2 = _mm_post(o_x, p["x_wo"], h1, _row(p["x_norm_post"]), tm=512, name=f"xa_out{tag}")
    sv.update(n2=n2, q=q, mem_n=mem_n, k=k, v=v, o_x=o_x, y2=y2, h2=h2, wkv=wkv)

    n3, up = ride("ffn_up", functools.partial(_rms_mm, h2, _row(p["ffn_norm_pre"]), p["ffn_w_up"], tm=1024, tn=1408,
                                              out_dtype=F32, name=f"ffn_up{tag}"))
    act, = ride("ffn_gate", functools.partial(_ff_gate_fwd, up, p["ffn_conv_w"], _row(p["ffn_conv_b"]),
                                              name=f"ffn_gate{tag}"))
    y3, h3 = ride("ffn_down", functools.partial(_mm_post, act, p["ffn_w_down"], h2, _row(p["ffn_norm_post"]), tm=512,
                                                name=f"ffn_down{tag}"))
    sv.update(n3=n3, up=up, act=act, y3=y3)
    return h3, sv


def _layer_bwd(dh3, mem, p, sv, cos, sin, tag, riding, new_rs):
    g = {}
    s8 = lambda part: part.sum(axis=0)
    rode = None

    dy3, dgp = _rms_bwd(sv["y3"], _row(p["ffn_norm_post"]), dh3, None, out_dtype=BF16, tm=512, name=f"ffn_post_b{tag}")
    g["ffn_norm_post"] = s8(dgp)
    dact = _mm_nt(dy3, p["ffn_w_down"], tm=512, tn=1408, out_dtype=F32, name=f"ffn_down_bx{tag}")
    g["ffn_w_down"] = _mm_tn(sv["act"], dy3, tk=1408, tn=1024, tm=2048, name=f"ffn_down_bw{tag}")
    dgu, dvu, dcw, dcb, *got = _ff_gate_bwd(sv["up"], dact, p["ffn_conv_w"], _row(p["ffn_conv_b"]),
                                            name=f"ffn_gate_b{tag}", carry=riding.swap() if riding else None)
    scatter = riding.after_swap(got[0]) if riding else None
    g["ffn_conv_w"] = jnp.concatenate([dcw[0], dcw[1]], axis=1).reshape(3, 8, 2 * D_FF).sum(axis=1)
    g["ffn_conv_b"] = jnp.concatenate([dcb[0], dcb[1]], axis=1).sum(axis=0)
    dup = jnp.concatenate([dgu, dvu], axis=1)
    dn3 = _mm_nt(dup, p["ffn_w_up"], tm=256, tn=512, out_dtype=F32, name=f"ffn_up_bx{tag}")
    g["ffn_w_up"] = _mm_tn(sv["n3"], dup, tk=512, tn=1408, tm=2048, name=f"ffn_up_bw{tag}")
    ffn_rs = new_rs(FFN_GROUP, g, f"{tag}_ffn")
    dh2, dgp = _rms_bwd(sv["h2"], _row(p["ffn_norm_pre"]), dn3, dh3, out_dtype=F32, tm=512, name=f"ffn_pre_b{tag}")
    g["ffn_norm_pre"] = s8(dgp)

    dy2, dgp = _rms_bwd(sv["y2"], _row(p["x_norm_post"]), dh2, None, out_dtype=BF16, tm=512, name=f"xa_post_b{tag}")
    g["x_norm_post"] = s8(dgp)
    do_x = _mm_nt(dy2, p["x_wo"], tm=512, tn=1024, out_dtype=BF16, name=f"xa_out_bx{tag}")
    g["x_wo"] = _mm_tn(sv["o_x"], dy2, tk=512, tn=1024, tm=2048, name=f"xa_out_bw{tag}")
    dq, dk, dv, got = _xa_bwd(sv["q"], sv["k"], sv["v"], do_x, name=f"xa_bwd{tag}", carry=ffn_rs.swap())
    ffn_scatter = ffn_rs.after_swap(got)
    dn2 = _mm_nt(dq, p["x_wq"], tm=512, tn=1024, out_dtype=F32, name=f"xa_q_bx{tag}")
    g["x_wq"] = _mm_tn(sv["n2"], dq, tk=512, tn=1024, tm=2048, name=f"xa_q_bw{tag}")
    dkv = jnp.concatenate([dk, dv], axis=1).astype(BF16)
    nm = mem.shape[0]
    dmem_n = _mm_nt(dkv, sv["wkv"], tm=nm, tn=1024, out_dtype=F32, name=f"xa_kv_bx{tag}")
    dwkv = _mm_tn(sv["mem_n"], dkv, tk=512, tn=2048, tm=nm, name=f"xa_kv_bw{tag}")
    g["x_wk"], g["x_wv"] = dwkv[:, :D_MODEL], dwkv[:, D_MODEL:]
    _, dgp = _rms_bwd(mem, _row(p["mem_norm"]), dmem_n, None, out_dtype=BF16, tm=nm, name=f"xa_mem_b{tag}")
    g["mem_norm"] = s8(dgp)
    xa_rs = new_rs(XA_GROUP, g, f"{tag}_xa")
    dh1, dgp, got = _rms_bwd(sv["h1"], _row(p["x_norm_pre"]), dn2, dh2, out_dtype=F32, tm=512, name=f"xa_pre_b{tag}",
                             carry=xa_rs.swap())
    xa_scatter = xa_rs.after_swap(got, SEM_BASE_2 if riding else 0)
    g["x_norm_pre"] = s8(dgp)

    dy1, dgp = _rms_bwd(sv["y1"], _row(p["mix_norm_post"]), dh1, None, out_dtype=BF16, tm=512, name=f"mix_post_b{tag}")
    g["mix_norm_post"] = s8(dgp)
    dcat = _mm_nt(dy1, p["w_out"], tm=512, tn=1024, out_dtype=F32, name=f"mix_out_bx{tag}")
    g["w_out"] = _mm_tn(sv["cat"], dy1, tk=512, tn=1024, tm=2048, name=f"mix_out_bw{tag}")
    u = sv["u"]
    dq_sb, dk_sb, dv_sb, *got = _sb_bwd(u, dcat, name=f"sb_bwd{tag}",
                                        carry=_ex_join(scatter, xa_scatter) if riding else xa_scatter)
    if riding:
        rode = riding.after_scatter(got[0])
    xa_rows = xa_rs.after_scatter(got[-1])
    pw_b16 = p["cv_pw_w"].astype(BF16)
    dc, dpw, vec = _cv_bwd_local(sv["c"], dcat, _row(p["cv_ln_g"]), _row(p["cv_ln_b"]), pw_b16, name=f"cv_bwd_a{tag}")
    g["cv_pw_w"] = dpw
    vec = vec.reshape(3, 8, CV_W).sum(axis=1)
    g["cv_pw_b"], g["cv_ln_g"], g["cv_ln_b"] = vec[0], vec[1], vec[2]
    du_cv, dcw, dcb = _cv_bwd_conv(u, dc, p["cv_w"], name=f"cv_bwd_b{tag}")
    g["cv_w"] = dcw.reshape(CV_K, 8, CV_W).sum(axis=1)
    g["cv_b"] = dcb.sum(axis=0)
    dop, stats = _dl_bwd_prep(dcat, sv["o_dl"], sv["lse"], name=f"dl_prep_b{tag}")
    cur, prev, got = _dl_bwd(sv["qkv"], dop, stats, name=f"dl_bwd{tag}", carry=ffn_scatter)
    ffn_rows = ffn_rs.after_scatter(got)
    du_dl = _dl_bwd_finish(cur, prev, cos, sin, name=f"dl_fin_b{tag}")
    du = jnp.concatenate([dq_sb.astype(BF16), dk_sb.astype(BF16), dv_sb.astype(BF16), du_cv, du_dl], axis=1)
    dn1 = _mm_nt(du, p["w_in"], tm=512, tn=512, out_dtype=F32, name=f"mix_in_bx{tag}")
    g["w_in"] = _mm_tn(sv["n1"], du, tk=512, tn=1408, tm=2048, name=f"mix_in_bw{tag}")
    dh0, dgp = _rms_bwd(sv["h0"], _row(p["mix_norm_pre"]), dn1, dh1, out_dtype=F32, tm=512, name=f"mix_pre_b{tag}")
    g["mix_norm_pre"] = s8(dgp)
    return dh0, g, (xa_rows, ffn_rows), rode


def _step(x, mem, positions, loss_target, w, m, v):
    depth = w["w_in"].shape[0]
    xi, yi, ci = lax.axis_index("x"), lax.axis_index("y"), lax.axis_index("c")
    chip = 2 * xi + yi
    h = x[0]
    mem0 = mem[0]
    s_len = h.shape[0]

    packs = [jnp.concatenate([_to_pack_rows(n, w[n][l]) for n in BIG], axis=0).astype(BF16) for l in range(depth)]

    small_w = []
    for l in range(depth):
        for n, shape, axis in SMALL:
            if axis is not None:
                full = jnp.zeros(shape, F32)
                full = lax.dynamic_update_slice_in_dim(full, w[n][l], chip * w[n][l].shape[axis], axis)
                small_w.append(full * jnp.where(ci == 0, 1.0, 0.0))
    small_w_sum = _all_reduce_small(_flat_rows(small_w), name="gather_small_weights")
    small_full, off = [{} for _ in range(depth)], 0
    for l in range(depth):
        for n, shape, axis in SMALL:
            if axis is not None:
                size = int(np.prod(shape))
                small_full[l][n] = small_w_sum.reshape(-1)[off:off + size].reshape(shape)
                off += size
    weights = _Weights(packs)
    params = [_Params(weights, l, {n: small_full[l].get(n, w[n][l]) for n, _, _ in SMALL}) for l in range(depth)]

    inv_freq = ROPE_THETA ** (-jnp.arange(HD // 2, dtype=F32) / (HD // 2))
    cos, sin = _rope_tables(positions.reshape(s_len, 1), jnp.tile(inv_freq, 4).reshape(1, LANES), name="rope_tables")

    saved = []
    for l in range(depth):
        h, sv = _layer_fwd(h, mem0, params[l], cos, sin, f"_l{l}", functools.partial(weights.ride, l))
        saved.append(sv)
    dh, sq = _loss_grad(h, loss_target[0], tm=512, name="loss_grad")
    loss = lax.psum(0.5 * jnp.sum(sq) / D_MODEL, ("x", "y", "c"))

    c_arr, j_arr = jnp.reshape(ci, (1,)).astype(jnp.int32), jnp.reshape(chip, (1,)).astype(jnp.int32)

    def new_rs(names, g, tag):
        gw = jnp.concatenate([_blocks_from_full(n, g[n]) for n in names], axis=1)
        return _ReduceScatter(gw, c_arr, j_arr, tag)

    grads, later_rows, rest_rows, pending = [None] * depth, [None] * depth, [None] * depth, None
    for l in reversed(range(depth)):
        dh, grads[l], later_rows[l], rode = _layer_bwd(dh, mem0, params[l], saved[l], cos, sin, f"_l{l}", pending, new_rs)
        if pending is not None:
            rest_rows[l + 1] = rode
        pending = new_rs(REST_GROUP, grads[l], f"_l{l}_rest")
    rest_rows[0] = pending.run()
    gfull = [jnp.concatenate([rest_rows[l], *later_rows[l]], axis=0) for l in range(depth)]
    grad_x = dh[None]

    out_g, out_d, out_m, out_v = {}, {}, {}, {}
    off = 0
    for n, rows in PACK_ROWS:
        shard_shape = w[n].shape
        g_n = jnp.stack([gl[off:off + rows, :] for gl in gfull]).reshape(shard_shape)
        off += rows
        flat = lambda a: a.reshape(-1, shard_shape[-1])
        d_n, m_n, v_n = _adamw(flat(w[n]), flat(g_n), flat(m[n]), flat(v[n]), name=f"adamw_{n}")
        out_g[n], out_d[n], out_m[n], out_v[n] = g_n, d_n.reshape(shard_shape), m_n.reshape(shard_shape), v_n.reshape(shard_shape)

    g_small = _all_reduce_small(_flat_rows([grads[l][n] for l in range(depth) for n, _, _ in SMALL]),
                                name="all_reduce_small_grads").reshape(-1)
    local_g, off = {}, 0
    for l in range(depth):
        for n, shape, axis in SMALL:
            size = int(np.prod(shape))
            full = g_small[off:off + size].reshape(shape)
            off += size
            if axis is not None:
                blk = w[n].shape[1 + axis]
                full = lax.dynamic_slice_in_dim(full, chip * blk, blk, axis)
            local_g.setdefault(n, []).append(full)
    names = [n for n, _, _ in SMALL]
    g_loc = {n: jnp.stack(local_g[n]) for n in names}
    d_s, m_s, v_s = _adamw(_flat_rows([w[n] for n in names]), _flat_rows([g_loc[n] for n in names]),
                           _flat_rows([m[n] for n in names]), _flat_rows([v[n] for n in names]), name="adamw_small")
    off = 0
    for n in names:
        size = int(np.prod(w[n].shape))
        take = lambda a: a.reshape(-1)[off:off + size].reshape(w[n].shape)
        out_g[n], out_d[n], out_m[n], out_v[n] = g_loc[n], take(d_s), take(m_s), take(v_s)
        off += size

    outs = [loss, grad_x]
    for group in (out_g, out_d, out_m, out_v):
        outs += [group[n] for n in WEIGHT_ORDER]
    return tuple(outs)


def kernel(x, mem, positions, mix_norm_pre, w_in, cv_w, cv_b, cv_ln_g, cv_ln_b, cv_pw_w, cv_pw_b, w_out, mix_norm_post, x_norm_pre, mem_norm, x_wq, x_wk, x_wv, x_wo, x_norm_post, ffn_norm_pre, ffn_w_up, ffn_conv_w, ffn_conv_b, ffn_w_down, ffn_norm_post, loss_target, m_mix_norm_pre, m_w_in, m_cv_w, m_cv_b, m_cv_ln_g, m_cv_ln_b, m_cv_pw_w, m_cv_pw_b, m_w_out, m_mix_norm_post, m_x_norm_pre, m_mem_norm, m_x_wq, m_x_wk, m_x_wv, m_x_wo, m_x_norm_post, m_ffn_norm_pre, m_ffn_w_up, m_ffn_conv_w, m_ffn_conv_b, m_ffn_w_down, m_ffn_norm_post, v_mix_norm_pre, v_w_in, v_cv_w, v_cv_b, v_cv_ln_g, v_cv_ln_b, v_cv_pw_w, v_cv_pw_b, v_w_out, v_mix_norm_post, v_x_norm_pre, v_mem_norm, v_x_wq, v_x_wk, v_x_wv, v_x_wo, v_x_norm_post, v_ffn_norm_pre, v_ffn_w_up, v_ffn_conv_w, v_ffn_conv_b, v_ffn_w_down, v_ffn_norm_post):
    w = dict(zip(WEIGHT_ORDER, (mix_norm_pre, w_in, cv_w, cv_b, cv_ln_g, cv_ln_b, cv_pw_w, cv_pw_b, w_out, mix_norm_post, x_norm_pre, mem_norm, x_wq, x_wk, x_wv, x_wo, x_norm_post, ffn_norm_pre, ffn_w_up, ffn_conv_w, ffn_conv_b, ffn_w_down, ffn_norm_post)))
    m = dict(zip(WEIGHT_ORDER, (m_mix_norm_pre, m_w_in, m_cv_w, m_cv_b, m_cv_ln_g, m_cv_ln_b, m_cv_pw_w, m_cv_pw_b, m_w_out, m_mix_norm_post, m_x_norm_pre, m_mem_norm, m_x_wq, m_x_wk, m_x_wv, m_x_wo, m_x_norm_post, m_ffn_norm_pre, m_ffn_w_up, m_ffn_conv_w, m_ffn_conv_b, m_ffn_w_down, m_ffn_norm_post)))
    v = dict(zip(WEIGHT_ORDER, (v_mix_norm_pre, v_w_in, v_cv_w, v_cv_b, v_cv_ln_g, v_cv_ln_b, v_cv_pw_w, v_cv_pw_b, v_w_out, v_mix_norm_post, v_x_norm_pre, v_mem_norm, v_x_wq, v_x_wk, v_x_wv, v_x_wo, v_x_norm_post, v_ffn_norm_pre, v_ffn_w_up, v_ffn_conv_w, v_ffn_conv_b, v_ffn_w_down, v_ffn_norm_post)))
    return _step(x, mem, positions, loss_target, w, m, v)
```

```python
import functools

import jax
import jax.numpy as jnp
import numpy as np
from jax import lax
from jax.experimental import pallas as pl
from jax.experimental.pallas import tpu as pltpu

F32, BF16 = jnp.float32, jnp.bfloat16
MESH = pl.DeviceIdType.MESH
EPS = 1e-6
LANES = 128
BLK = 128
HD = 64
D_MODEL = 1024
D_FF = 2816
SB_W, CV_W, DL_W = 256, 256, 512
CV_K = 31
ROPE_THETA = 10000.0
DILATIONS = (1, 4, 16)
X_HEADS, X_HD = 4, 256
ADAM_LR, ADAM_B1, ADAM_B2, ADAM_EPS, ADAM_WD, ADAM_STEP = 0.001, 0.9, 0.999, 1e-08, 0.01, 10
NEG_INF = float("-inf")
MIB = 1 << 20

PACK_ROWS = (("w_in", 704), ("w_out", 256), ("x_wq", 256), ("x_wk", 256), ("x_wv", 256), ("x_wo", 256),
             ("ffn_w_up", 1408), ("ffn_w_down", 704))
PACK_RL = sum(r for _, r in PACK_ROWS)


def _cp(vmem_mb=48):
    return pltpu.CompilerParams(vmem_limit_bytes=vmem_mb * MIB)


def _dot(a, b):
    return jnp.dot(a, b, preferred_element_type=F32)


def _dot_nt(a, b):
    return lax.dot_general(a, b, (((1,), (1,)), ((), ())), preferred_element_type=F32)


def _dot_tn(a, b):
    return lax.dot_general(a, b, (((0,), (0,)), ((), ())), preferred_element_type=F32)


def _dot_hilo(x, m):
    hi = x.astype(BF16)
    lo = (x - hi.astype(F32)).astype(BF16)
    return _dot(hi, m) + _dot(lo, m)


def _rowsum8(x):
    t, c = x.shape
    return x.reshape(t // 8, 8, c).sum(axis=0)


def _acc_out(ref, i, val):
    @pl.when(i == 0)
    def _():
        ref[...] = val

    @pl.when(i > 0)
    def _():
        ref[...] += val


def _tile(n, cap, mult=8):
    t = min(n, cap)
    while n % t or t % mult:
        t -= 1
    return t


def _rms_mm(x, g, w, *, tm, tn, out_dtype, name, carry=None):
    m, d = x.shape
    n_out = w.shape[1]

    def body(x_ref, g_ref, w_ref, n_ref, o_ref):
        @pl.when(pl.program_id(1) == 0)
        def _():
            xv = x_ref[...]
            r = lax.rsqrt(jnp.mean(xv * xv, axis=-1, keepdims=True) + EPS)
            n_ref[...] = (xv * r * g_ref[...]).astype(BF16)

        o_ref[...] = _dot(n_ref[...], w_ref[...]).astype(out_dtype)

    return _call(
        body, grid=(m // tm, n_out // tn), name=name, carry=carry,
        in_specs=[pl.BlockSpec((tm, d), lambda i, j: (i, 0)), pl.BlockSpec((1, d), lambda i, j: (0, 0)),
                  pl.BlockSpec((d, tn), lambda i, j: (0, j))],
        out_specs=[pl.BlockSpec((tm, d), lambda i, j: (i, 0)), pl.BlockSpec((tm, tn), lambda i, j: (i, j))],
        out_shape=[jax.ShapeDtypeStruct((m, d), BF16), jax.ShapeDtypeStruct((m, n_out), out_dtype)],
        args=(x, g, w))


def _mm_post(a, w, h, g, *, tm, name, carry=None):
    m, k = a.shape
    d = w.shape[1]

    def body(a_ref, w_ref, h_ref, g_ref, y_ref, ho_ref):
        y = _dot(a_ref[...], w_ref[...])
        y_ref[...] = y
        r = lax.rsqrt(jnp.mean(y * y, axis=-1, keepdims=True) + EPS)
        ho_ref[...] = h_ref[...] + y * r * g_ref[...]

    return _call(
        body, grid=(m // tm,), name=name, carry=carry,
        in_specs=[pl.BlockSpec((tm, k), lambda i: (i, 0)), pl.BlockSpec((k, d), lambda i: (0, 0)),
                  pl.BlockSpec((tm, d), lambda i: (i, 0)), pl.BlockSpec((1, d), lambda i: (0, 0))],
        out_specs=[pl.BlockSpec((tm, d), lambda i: (i, 0)), pl.BlockSpec((tm, d), lambda i: (i, 0))],
        out_shape=[jax.ShapeDtypeStruct((m, d), F32), jax.ShapeDtypeStruct((m, d), F32)],
        args=(a, w, h, g))


def _mm_nt(a, w, *, tm, tn, out_dtype, name):
    m, k = a.shape
    n_out = w.shape[0]

    def body(a_ref, w_ref, o_ref):
        o_ref[...] = _dot_nt(a_ref[...], w_ref[...]).astype(out_dtype)

    return pl.pallas_call(
        body, grid=(n_out // tn, m // tm), name=name,
        in_specs=[pl.BlockSpec((tm, k), lambda j, i: (i, 0)), pl.BlockSpec((tn, k), lambda j, i: (j, 0))],
        out_specs=pl.BlockSpec((tm, tn), lambda j, i: (i, j)),
        out_shape=jax.ShapeDtypeStruct((m, n_out), out_dtype),
        compiler_params=_cp())(a, w)


def _mm_tn(x, dy, *, tk, tn, tm, name):
    m, k = x.shape
    n_out = dy.shape[1]

    def body(x_ref, d_ref, o_ref):
        _acc_out(o_ref, pl.program_id(2), _dot_tn(x_ref[...], d_ref[...]))

    return pl.pallas_call(
        body, grid=(k // tk, n_out // tn, m // tm), name=name,
        in_specs=[pl.BlockSpec((tm, tk), lambda a, b, c: (c, a)), pl.BlockSpec((tm, tn), lambda a, b, c: (c, b))],
        out_specs=pl.BlockSpec((tk, tn), lambda a, b, c: (a, b)),
        out_shape=jax.ShapeDtypeStruct((k, n_out), F32),
        compiler_params=_cp())(x, dy)


def _rms_bwd(x, g, dout, res, *, out_dtype, tm, name, carry=None):
    m, d = x.shape
    has_res = res is not None

    def body(*refs):
        if has_res:
            x_ref, g_ref, d_ref, r_ref, dx_ref, dg_ref = refs
        else:
            x_ref, g_ref, d_ref, dx_ref, dg_ref = refs
        xv = x_ref[...]
        dv = d_ref[...].astype(F32)
        r = lax.rsqrt(jnp.mean(xv * xv, axis=-1, keepdims=True) + EPS)
        xh = xv * r
        dxh = dv * g_ref[...]
        dx = r * (dxh - xh * jnp.mean(dxh * xh, axis=-1, keepdims=True))
        if has_res:
            dx = dx + r_ref[...]
        dx_ref[...] = dx.astype(out_dtype)
        _acc_out(dg_ref, pl.program_id(0), _rowsum8(dv * xh))

    row = pl.BlockSpec((tm, d), lambda i: (i, 0))
    ins = [row, pl.BlockSpec((1, d), lambda i: (0, 0)), row] + ([row] if has_res else [])
    args = (x, g, dout) + ((res,) if has_res else ())
    return _call(
        body, grid=(m // tm,), name=name, carry=carry, in_specs=ins,
        out_specs=[row, pl.BlockSpec((8, d), lambda i: (0, 0))],
        out_shape=[jax.ShapeDtypeStruct((m, d), out_dtype), jax.ShapeDtypeStruct((8, d), F32)], args=args)


def _loss_grad(h, tgt, *, tm, name):
    m, d = h.shape

    def body(h_ref, t_ref, dh_ref, p_ref):
        e = h_ref[...] - t_ref[...]
        dh_ref[...] = e / d
        _acc_out(p_ref, pl.program_id(0), _rowsum8(e * e))

    row = pl.BlockSpec((tm, d), lambda i: (i, 0))
    return pl.pallas_call(
        body, grid=(m // tm,), name=name, in_specs=[row, row],
        out_specs=[row, pl.BlockSpec((8, d), lambda i: (0, 0))],
        out_shape=[jax.ShapeDtypeStruct((m, d), F32), jax.ShapeDtypeStruct((8, d), F32)],
        compiler_params=_cp())(h, tgt)


def _adamw(w, g, m, v, *, name, carry=None):
    r, c = w.shape
    tr = _tile(r, 256)

    def body(w_ref, g_ref, m_ref, v_ref, d_ref, mo_ref, vo_ref):
        gv = g_ref[...]
        m2 = ADAM_B1 * m_ref[...] + (1.0 - ADAM_B1) * gv
        v2 = ADAM_B2 * v_ref[...] + (1.0 - ADAM_B2) * jnp.square(gv)
        m_hat = m2 / (1.0 - ADAM_B1 ** ADAM_STEP)
        v_hat = v2 / (1.0 - ADAM_B2 ** ADAM_STEP)
        d_ref[...] = -ADAM_LR * (m_hat / (jnp.sqrt(v_hat) + ADAM_EPS) + ADAM_WD * w_ref[...])
        mo_ref[...] = m2
        vo_ref[...] = v2

    blk = pl.BlockSpec((tr, c), lambda i: (i, 0))
    return _call(body, grid=(r // tr,), name=name, carry=carry, in_specs=[blk] * 4, out_specs=[blk] * 3,
                 out_shape=[jax.ShapeDtypeStruct((r, c), F32)] * 3, args=(w, g, m, v))


def _head_masks():
    lane = lax.broadcasted_iota(jnp.int32, (BLK, LANES), 1)
    row = lax.broadcasted_iota(jnp.int32, (BLK, LANES), 0)
    return lane, row, lane < HD


def _sb_scores(q_a, k, before):
    z = _dot_nt(q_a, k)
    sp = jnp.log1p(jnp.exp(-jnp.abs(z)))
    ls_pos = jnp.minimum(z, 0.0) - sp
    lkeep = jnp.where(before, ls_pos - z, 0.0)
    return ls_pos, lkeep


SB_DEAD = -104.0


def _sb_alive(jj, i, carry):
    return jnp.logical_and(jj <= i, jnp.max(carry) > SB_DEAD)


SB_QB_FWD = 2
SB_QB = 2


def _sb_before(jj, qb=SB_QB):
    lane = lax.broadcasted_iota(jnp.int32, (qb * 2 * BLK, LANES), 1)
    row = lax.broadcasted_iota(jnp.int32, (qb * 2 * BLK, LANES), 0)
    below_diag = jj - (qb - 1) + row // (2 * BLK)
    return jnp.logical_or(below_diag > 0, jnp.logical_and(below_diag == 0, lane < row % BLK))


def _sb_stack(x, lane_h, qb=SB_QB):
    return jnp.concatenate([_stack_heads(x[b * BLK:(b + 1) * BLK], lane_h) for b in range(qb)], axis=0)


def _sb_unstack(x, lane_h, qb=SB_QB):
    return jnp.concatenate([jnp.where(lane_h, x[2 * b * BLK:(2 * b + 1) * BLK], x[(2 * b + 1) * BLK:(2 * b + 2) * BLK])
                            for b in range(qb)], axis=0)


SB_ROWS = SB_QB * 2 * BLK


def _sb_fwd(u, *, name, carry=None):
    s_len = u.shape[0]
    qb = SB_QB_FWD
    qrows, rows = qb * BLK, qb * 2 * BLK

    def body(q_ref, k_ref, v_ref, o_ref):
        top = pl.program_id(0) * qb + qb - 1
        lane, row, lane_h = _head_masks()
        suffix = (row > lane).astype(BF16)
        pairs = [slice(hp * LANES, (hp + 1) * LANES) for hp in range(2)]
        qs = [_sb_stack(q_ref[:, cs] * 0.125, lane_h, qb) for cs in pairs]

        def step(state):
            jj, ccs, accs = state[0], state[1:3], state[3:5]
            rows_k = pl.ds(pl.multiple_of((top - jj) * BLK, BLK), BLK)
            before = _sb_before(jj, qb)
            scores = [_sb_scores(q, k_ref[rows_k, cs].astype(BF16), before) for q, cs in zip(qs, pairs)]
            between = [_dot_hilo(lkeep, suffix) + cc for (_, lkeep), cc in zip(scores, ccs)]
            atts = [jnp.where(before, jnp.exp(ls_pos + b), 0.0).astype(BF16) for (ls_pos, _), b in zip(scores, between)]
            new_cc = [cc + jnp.sum(lkeep, axis=1, keepdims=True) for (_, lkeep), cc in zip(scores, ccs)]
            new_acc = [acc + _dot(a, v_ref[rows_k, cs].astype(BF16)) for a, acc, cs in zip(atts, accs, pairs)]
            return (jj + 1, *new_cc, *new_acc)

        zc, za = jnp.zeros((rows, 1), F32), jnp.zeros((rows, LANES), F32)
        res = lax.while_loop(lambda st: _sb_alive(st[0], top, jnp.maximum(st[1], st[2])), step,
                             (jnp.int32(0), zc, zc, za, za))
        for hp, cs in enumerate(pairs):
            o_ref[:, cs] = _sb_unstack(res[3 + hp], lane_h, qb).astype(BF16)

    wide = 2 * LANES
    return _call(
        body, grid=(s_len // qrows,), name=name, carry=carry,
        in_specs=[pl.BlockSpec((qrows, wide), lambda i: (i, 0)), pl.BlockSpec((s_len, wide), lambda i: (0, 1)),
                  pl.BlockSpec((s_len, wide), lambda i: (0, 2))],
        out_specs=[pl.BlockSpec((qrows, wide), lambda i: (i, 0))],
        out_shape=[jax.ShapeDtypeStruct((s_len, SB_W), BF16)], args=(u, u, u))


def _sb_bwd(u, dcat, *, name, carry=None):
    s_len = u.shape[0]
    nq = s_len // BLK
    qrows = SB_QB * BLK

    def body(q_ref, k_ref, v_ref, do_ref, dq_ref, dk_ref, dv_ref, g_scr, b_scr):
        step = pl.program_id(1)
        top = step * SB_QB + SB_QB - 1
        lane, row, lane_h = _head_masks()
        suffix = (row > lane).astype(BF16)
        prefix = (row < lane).astype(BF16)
        qf = q_ref[...]
        qs = _sb_stack(qf * 0.125, lane_h)
        qu = _sb_stack(qf, lane_h)
        dos = _sb_stack(do_ref[...], lane_h)

        @pl.when(step == 0)
        def _():
            dk_ref[...] = jnp.zeros_like(dk_ref)
            dv_ref[...] = jnp.zeros_like(dv_ref)

        def down(state):
            jj, cc = state
            j = top - jj
            off = pl.multiple_of(j * BLK, BLK)
            k = k_ref[pl.ds(off, BLK), :].astype(BF16)
            v = v_ref[pl.ds(off, BLK), :].astype(BF16)
            before = _sb_before(jj)
            ls_pos, lkeep = _sb_scores(qs, k, before)
            between = _dot_hilo(lkeep, suffix) + cc
            att = jnp.where(before, jnp.exp(ls_pos + between), 0.0)
            g_scr[j] = att * _dot_nt(dos, v)
            b_scr[j] = jnp.exp(ls_pos)
            dv_ref[pl.ds(off, BLK), :] += _dot_tn(att.astype(BF16), dos)
            return jj + 1, cc + jnp.sum(lkeep, axis=1, keepdims=True)

        zc = jnp.zeros((SB_ROWS, 1), F32)
        visited = lax.while_loop(lambda st: _sb_alive(st[0], top, st[1]), down, (jnp.int32(0), zc))[0]

        def up(j, carry):
            pc, dq = carry
            off = pl.multiple_of(j * BLK, BLK)
            k = k_ref[pl.ds(off, BLK), :].astype(BF16)
            g, beta = g_scr[j], b_scr[j]
            below = _dot_hilo(g, prefix) + pc
            dz = (jnp.where(_sb_before(top - j), g * (1.0 - beta) - beta * below, 0.0) * 0.125).astype(BF16)
            dk_ref[pl.ds(off, BLK), :] += _dot_tn(dz, qu)
            return pc + jnp.sum(g, axis=1, keepdims=True), dq + _dot(dz, k)

        dq = lax.fori_loop(top + 1 - visited, top + 1, up, (zc, jnp.zeros((SB_ROWS, LANES), F32)))[1]
        dq_ref[...] = _sb_unstack(dq, lane_h)

    col = lambda c0: pl.BlockSpec((s_len, LANES), lambda hp, i: (0, c0 + hp))
    blk = pl.BlockSpec((qrows, LANES), lambda hp, i: (i, hp))
    acc = pl.BlockSpec((s_len, LANES), lambda hp, i: (0, hp))
    return _call(
        body, grid=(2, s_len // qrows), name=name, carry=carry, in_specs=[blk, col(2), col(4), blk],
        out_specs=[blk, acc, acc], out_shape=[jax.ShapeDtypeStruct((s_len, SB_W), F32)] * 3,
        scratch_shapes=[pltpu.VMEM((nq, SB_ROWS, LANES), F32), pltpu.VMEM((nq, SB_ROWS, LANES), F32)],
        vmem_mb=56, args=(u, u, u, dcat))


CV_T = 512
CV_H = 32


def _cv_specs(s_len):
    cur = lambda c: pl.BlockSpec((CV_T, CV_W), lambda i: (i, c))
    prev = lambda c: pl.BlockSpec((CV_H, CV_W), lambda i: (jnp.maximum(i * (CV_T // CV_H) - 1, 0), c))
    nxt = lambda c: pl.BlockSpec((CV_H, CV_W),
                                 lambda i: (jnp.minimum((i + 1) * (CV_T // CV_H), s_len // CV_H - 1), c))
    full = lambda r: pl.BlockSpec((r, CV_W), lambda i: (0, 0))
    return cur, prev, nxt, full


def _glu_into(gp_ref, val_ref, gate_ref, valp_ref, gatep_ref, i):
    gp_ref[0:CV_H, :] = jnp.where(i > 0, valp_ref[...] * jax.nn.sigmoid(gatep_ref[...]), 0.0)
    gp_ref[CV_H:, :] = val_ref[...] * jax.nn.sigmoid(gate_ref[...])


def _cv_fwd(u, cv_w, cv_b, ln_g, ln_b, pw_w, pw_b, *, name):
    s_len = u.shape[0]
    cur, prev, _, full = _cv_specs(s_len)

    def body(val_ref, gate_ref, valp_ref, gatep_ref, w_ref, b_ref, g_ref, be_ref, pw_ref, pb_ref,
             o_ref, c_ref, gp_ref):
        _glu_into(gp_ref, val_ref, gate_ref, valp_ref, gatep_ref, pl.program_id(0))
        acc = jnp.zeros((CV_T, CV_W), F32) + b_ref[...]
        for k in range(CV_K):
            acc = acc + w_ref[k:k + 1, :] * gp_ref[pl.ds(CV_H - CV_K + 1 + k, CV_T), :]
        c_ref[...] = acc
        mu = jnp.mean(acc, axis=-1, keepdims=True)
        xc = acc - mu
        xh = xc * lax.rsqrt(jnp.mean(xc * xc, axis=-1, keepdims=True) + EPS)
        a = xh * g_ref[...] + be_ref[...]
        s = a * jax.nn.sigmoid(a)
        o_ref[...] = (_dot(s.astype(BF16), pw_ref[...]) + pb_ref[...]).astype(BF16)

    return pl.pallas_call(
        body, grid=(s_len // CV_T,), name=name,
        in_specs=[cur(3), cur(4), prev(3), prev(4), full(CV_K), full(1), full(1), full(1), full(CV_W), full(1)],
        out_specs=[cur(0), cur(0)],
        out_shape=[jax.ShapeDtypeStruct((s_len, CV_W), BF16), jax.ShapeDtypeStruct((s_len, CV_W), F32)],
        scratch_shapes=[pltpu.VMEM((CV_T + CV_H, CV_W), F32)], compiler_params=_cp())(
            u, u, u, u, cv_w, cv_b, ln_g, ln_b, pw_w, pw_b)


def _cv_bwd_local(c, dcat, ln_g, ln_b, pw_w, *, name):
    s_len = c.shape[0]
    cur, _, _, full = _cv_specs(s_len)

    def body(c_ref, db_ref, g_ref, be_ref, pw_ref, dc_ref, dpw_ref, vec_ref):
        i = pl.program_id(0)
        cv = c_ref[...]
        db = db_ref[...]
        mu = jnp.mean(cv, axis=-1, keepdims=True)
        xc = cv - mu
        rstd = lax.rsqrt(jnp.mean(xc * xc, axis=-1, keepdims=True) + EPS)
        xh = xc * rstd
        a = xh * g_ref[...] + be_ref[...]
        sg = jax.nn.sigmoid(a)
        s = a * sg
        dbb = db.astype(BF16)
        ds = _dot_nt(dbb, pw_ref[...])
        da = ds * (sg * (1.0 + a * (1.0 - sg)))
        dxh = da * g_ref[...]
        dc_ref[...] = rstd * (dxh - jnp.mean(dxh, axis=-1, keepdims=True)
                              - xh * jnp.mean(dxh * xh, axis=-1, keepdims=True))
        _acc_out(dpw_ref, i, _dot_tn(s.astype(BF16), dbb))
        _acc_out(vec_ref, i, jnp.concatenate([_rowsum8(db), _rowsum8(da * xh), _rowsum8(da)], axis=0))

    return pl.pallas_call(
        body, grid=(s_len // CV_T,), name=name,
        in_specs=[cur(0), cur(1), full(1), full(1), full(CV_W)],
        out_specs=[cur(0), full(CV_W), full(24)],
        out_shape=[jax.ShapeDtypeStruct((s_len, CV_W), F32), jax.ShapeDtypeStruct((CV_W, CV_W), F32),
                   jax.ShapeDtypeStruct((24, CV_W), F32)], compiler_params=_cp())(c, dcat, ln_g, ln_b, pw_w)


def _cv_bwd_conv(u, dc, cv_w, *, name):
    s_len = u.shape[0]
    cur, prev, nxt, full = _cv_specs(s_len)
    last = s_len // CV_T - 1

    def body(val_ref, gate_ref, valp_ref, gatep_ref, dc_ref, dcn_ref, w_ref, du_ref, dw_ref, dbias_ref,
             gp_ref, dcp_ref):
        i = pl.program_id(0)
        _glu_into(gp_ref, val_ref, gate_ref, valp_ref, gatep_ref, i)
        dcv = dc_ref[...]
        dcp_ref[0:CV_T, :] = dcv
        dcp_ref[CV_T:, :] = jnp.where(i < last, dcn_ref[...], 0.0)
        dg = jnp.zeros((CV_T, CV_W), F32)
        parts = []
        for k in range(CV_K):
            dg = dg + w_ref[k:k + 1, :] * dcp_ref[pl.ds(CV_K - 1 - k, CV_T), :]
            parts.append(_rowsum8(dcv * gp_ref[pl.ds(CV_H - CV_K + 1 + k, CV_T), :]))
        _acc_out(dw_ref, i, jnp.concatenate(parts, axis=0))
        _acc_out(dbias_ref, i, _rowsum8(dcv))
        val = val_ref[...]
        sg = jax.nn.sigmoid(gate_ref[...])
        du_ref[:, 0:CV_W] = (dg * sg).astype(BF16)
        du_ref[:, CV_W:] = (dg * val * sg * (1.0 - sg)).astype(BF16)

    return pl.pallas_call(
        body, grid=(s_len // CV_T,), name=name,
        in_specs=[cur(3), cur(4), prev(3), prev(4), cur(0), nxt(0), full(CV_K)],
        out_specs=[pl.BlockSpec((CV_T, 2 * CV_W), lambda i: (i, 0)), full(CV_K * 8), full(8)],
        out_shape=[jax.ShapeDtypeStruct((s_len, 2 * CV_W), BF16), jax.ShapeDtypeStruct((CV_K * 8, CV_W), F32),
                   jax.ShapeDtypeStruct((8, CV_W), F32)],
        scratch_shapes=[pltpu.VMEM((CV_T + CV_H, CV_W), F32), pltpu.VMEM((CV_T + CV_H, CV_W), F32)],
        compiler_params=_cp())(u, u, u, u, dc, dc, cv_w)


def _rope_tables(pos_col, inv_freq_row, *, name):
    s_len = pos_col.shape[0]

    def body(p_ref, f_ref, cos_ref, sin_ref):
        ang = p_ref[...].astype(F32) * f_ref[...]
        lane = lax.broadcasted_iota(jnp.int32, (s_len, LANES), 1)
        sn = jnp.sin(ang)
        cos_ref[...] = jnp.cos(ang)
        sin_ref[...] = jnp.where(lane % HD < HD // 2, -sn, sn)

    return pl.pallas_call(body, name=name, out_shape=[jax.ShapeDtypeStruct((s_len, LANES), F32)] * 2,
                          compiler_params=_cp())(pos_col, inv_freq_row)


def _rot_half(x):
    lane = lax.broadcasted_iota(jnp.int32, x.shape, 1)
    return jnp.where(lane % HD < HD // 2, pltpu.roll(x, LANES - HD // 2, 1), pltpu.roll(x, HD // 2, 1))


def _permute_rows(dst_ref, src_ref, d, dtype):
    s_len = src_ref.shape[0]
    seg = s_len // d
    if d == 1:
        dst_ref[...] = src_ref[...].astype(dtype)
        return
    for r in range(d):
        dst_ref[r * seg:(r + 1) * seg, :] = src_ref[pl.ds(r, seg, stride=d), :].astype(dtype)


def _unpermute_rows(dst_ref, src_ref, d):
    s_len = src_ref.shape[0]
    seg = s_len // d
    if d == 1:
        dst_ref[...] = src_ref[...]
        return
    for r in range(d):
        dst_ref[pl.ds(r, seg, stride=d), :] = src_ref[r * seg:(r + 1) * seg, :]


def _rope_perm(u, cos, sin, *, name):
    s_len = u.shape[0]

    def body(x_ref, cos_ref, sin_ref, o_ref, scr):
        a = pl.program_id(0)
        x = x_ref[...]
        rot = a < 2
        scr[...] = x * jnp.where(rot, cos_ref[...], 1.0) + _rot_half(x) * jnp.where(rot, sin_ref[...], 0.0)
        for n, d in enumerate(DILATIONS):
            _permute_rows(o_ref.at[n], scr, d, BF16)

    tab = pl.BlockSpec((s_len, LANES), lambda a, cb: (0, 0))
    return pl.pallas_call(
        body, grid=(3, 4), name=name,
        in_specs=[pl.BlockSpec((s_len, LANES), lambda a, cb: (0, 10 + 4 * a + cb)), tab, tab],
        out_specs=pl.BlockSpec((None, 3, s_len, LANES), lambda a, cb: (a, 0, 0, cb)),
        out_shape=jax.ShapeDtypeStruct((3, 3, s_len, DL_W), BF16),
        scratch_shapes=[pltpu.VMEM((s_len, LANES), F32)], compiler_params=_cp())(u, cos, sin)


DL_UNROLL = 4


def _dl_band(rows):
    lane = lax.broadcasted_iota(jnp.int32, (rows, LANES), 1)
    row = lax.broadcasted_iota(jnp.int32, (rows, LANES), 0) % BLK
    return lane <= row, lane >= row


def _dl_first(s_len, n, i):
    nb = jnp.where(n == 0, s_len // BLK, jnp.where(n == 1, s_len // (BLK * DILATIONS[1]),
                                                   s_len // (BLK * DILATIONS[2])))
    return lax.rem(i, nb) == 0


def _stack_heads(x, lane_h):
    return jnp.concatenate([jnp.where(lane_h, x, 0.0), jnp.where(lane_h, 0.0, x)], axis=0).astype(BF16)


def _dl_rows(i):
    cur = pl.ds(pl.multiple_of(i * BLK, BLK), BLK)
    prev = pl.ds(pl.multiple_of(jnp.maximum(i - 1, 0) * BLK, BLK), BLK)
    return cur, prev


def _dl_in_specs(s_len):
    return [pl.BlockSpec((None, None, s_len, LANES), functools.partial(lambda a, n, hp: (a, n, 0, hp), a))
            for a in range(3)]


def _dl_fwd(qkv, *, name, carry=None):
    s_len = qkv.shape[2]

    def body(q_ref, k_ref, v_ref, o_ref, l_ref):
        n = pl.program_id(0)
        lane_h = _head_masks()[2]
        band_c, band_p = _dl_band(2 * BLK)
        ones = jnp.ones((BLK, LANES), BF16)

        @pl.loop(0, s_len // BLK, step=DL_UNROLL)
        def _(i0):
            blocks = [i0 + t for t in range(DL_UNROLL)]
            rows = [_dl_rows(i) for i in blocks]
            scores = []
            for cur, prev in rows:
                qs = _stack_heads(q_ref[cur, :] * 0.125, lane_h)
                scores.append((_dot_nt(qs, k_ref[cur, :]), _dot_nt(qs, k_ref[prev, :])))
            probs = []
            for i, (sc, sp) in zip(blocks, scores):
                sc = jnp.where(band_c, sc, NEG_INF)
                sp = jnp.where(jnp.logical_and(band_p, jnp.logical_not(_dl_first(s_len, n, i))), sp, NEG_INF)
                m = jnp.max(jnp.maximum(sc, sp), axis=1, keepdims=True)
                probs.append((jnp.exp(sc - m).astype(BF16), jnp.exp(sp - m).astype(BF16), m))
            for (cur, prev), (pc, pp, m) in zip(rows, probs):
                r = (_dot(pc, jnp.concatenate([v_ref[cur, :], ones], axis=1))
                     + _dot(pp, jnp.concatenate([v_ref[prev, :], ones], axis=1)))
                den = jnp.where(lane_h, r[:BLK, LANES:], r[BLK:, LANES:])
                o_ref[cur, :] = jnp.where(lane_h, r[:BLK, :LANES], r[BLK:, :LANES]) / den
                l_ref[cur, :] = jnp.where(lane_h, m[:BLK], m[BLK:]) + jnp.log(den)

    out = pl.BlockSpec((None, s_len, LANES), lambda n, hp: (n, 0, hp))
    return _call(
        body, grid=(3, 4), name=name, carry=carry, in_specs=_dl_in_specs(s_len), out_specs=[out, out],
        out_shape=[jax.ShapeDtypeStruct((3, s_len, DL_W), F32)] * 2, args=(qkv, qkv, qkv))


def _dl_mix(o_p, l_p, *, name, carry=None):
    s_len = o_p.shape[1]

    def body(o_ref, l_ref, ob_ref, of_ref, lt_ref, o_scr, l_scr):
        n = pl.program_id(1)
        for k, d in enumerate(DILATIONS):
            @pl.when(n == k)
            def _(k=k, d=d):
                _unpermute_rows(o_scr.at[k], o_ref, d)
                _unpermute_rows(l_scr.at[k], l_ref, d)

        @pl.when(n == 2)
        def _():
            l0, l1, l2 = l_scr[0], l_scr[1], l_scr[2]
            m = jnp.maximum(jnp.maximum(l0, l1), l2)
            e0, e1, e2 = jnp.exp(l0 - m), jnp.exp(l1 - m), jnp.exp(l2 - m)
            den = e0 + e1 + e2
            o = (e0 / den) * o_scr[0] + (e1 / den) * o_scr[1] + (e2 / den) * o_scr[2]
            of_ref[...] = o
            ob_ref[...] = o.astype(BF16)
            lt_ref[...] = m + jnp.log(den)

    inb = pl.BlockSpec((None, s_len, LANES), lambda cb, n: (n, 0, cb))
    outb = pl.BlockSpec((s_len, LANES), lambda cb, n: (0, cb))
    return _call(
        body, grid=(4, 3), name=name, carry=carry, in_specs=[inb, inb], out_specs=[outb, outb, outb],
        out_shape=[jax.ShapeDtypeStruct((s_len, DL_W), BF16), jax.ShapeDtypeStruct((s_len, DL_W), F32),
                   jax.ShapeDtypeStruct((s_len, DL_W), F32)],
        scratch_shapes=[pltpu.VMEM((3, s_len, LANES), F32), pltpu.VMEM((3, s_len, LANES), F32)], args=(o_p, l_p))


def _dl_bwd_prep(dcat, o, lse, *, name):
    s_len = o.shape[0]

    def body(do_ref, o_ref, l_ref, dop_ref, st_ref, d_scr):
        n = pl.program_id(1)

        @pl.when(n == 0)
        def _():
            r0 = lax.broadcasted_iota(jnp.int32, (LANES, LANES), 0) // HD
            r1 = lax.broadcasted_iota(jnp.int32, (LANES, LANES), 1) // HD
            d_scr[...] = _dot_hilo(do_ref[...] * o_ref[...], (r0 == r1).astype(BF16))

        for k, d in enumerate(DILATIONS):
            @pl.when(n == k)
            def _(d=d):
                _permute_rows(dop_ref, do_ref, d, BF16)
                _permute_rows(st_ref.at[0], d_scr, d, F32)
                _permute_rows(st_ref.at[1], l_ref, d, F32)

    nat = lambda c0: pl.BlockSpec((s_len, LANES), lambda cb, n: (0, c0 + cb))
    return pl.pallas_call(
        body, grid=(4, 3), name=name, in_specs=[nat(4), nat(0), nat(0)],
        out_specs=[pl.BlockSpec((None, s_len, LANES), lambda cb, n: (n, 0, cb)),
                   pl.BlockSpec((2, None, s_len, LANES), lambda cb, n: (0, n, 0, cb))],
        out_shape=[jax.ShapeDtypeStruct((3, s_len, DL_W), BF16), jax.ShapeDtypeStruct((2, 3, s_len, DL_W), F32)],
        scratch_shapes=[pltpu.VMEM((s_len, LANES), F32)], compiler_params=_cp())(dcat, o, lse)


def _dl_bwd(qkv, dop, stats, *, name, carry=None):
    s_len = qkv.shape[2]

    def body(q_ref, k_ref, v_ref, do_ref, st_ref, cur_ref, prev_ref):
        n = pl.program_id(0)
        lane_h = _head_masks()[2]
        band_c, band_p = _dl_band(2 * BLK)

        def per_head(x):
            xr = pltpu.roll(x, HD, 1)
            return jnp.concatenate([jnp.where(lane_h, x, xr), jnp.where(lane_h, xr, x)], axis=0)

        @pl.loop(0, s_len // BLK, step=DL_UNROLL)
        def _(i0):
            blocks = [i0 + t for t in range(DL_UNROLL)]
            rows = [_dl_rows(i) for i in blocks]
            stage1 = []
            for cur, prev in rows:
                qs = _stack_heads(q_ref[cur, :] * 0.125, lane_h)
                dos = _stack_heads(do_ref[cur, :], lane_h)
                kc, kp, vc, vp = k_ref[cur, :], k_ref[prev, :], v_ref[cur, :], v_ref[prev, :]
                stage1.append((qs, dos, _dot_nt(qs, kc), _dot_nt(qs, kp), _dot_nt(dos, vc), _dot_nt(dos, vp)))
            stage2 = []
            for i, (cur, prev), (qs, dos, sc, sp, dpc, dpp) in zip(blocks, rows, stage1):
                lse, delta = per_head(st_ref[1, cur, :]), per_head(st_ref[0, cur, :])
                pc = jnp.where(band_c, jnp.exp(sc - lse), 0.0)
                pp = jnp.where(jnp.logical_and(band_p, jnp.logical_not(_dl_first(s_len, n, i))), jnp.exp(sp - lse), 0.0)
                stage2.append((pc.astype(BF16), pp.astype(BF16), (pc * (dpc - delta)).astype(BF16),
                               (pp * (dpp - delta)).astype(BF16)))
            for (cur, prev), (qs, dos, *_), (pc, pp, dsc, dsp) in zip(rows, stage1, stage2):
                dq = _dot(dsc, k_ref[cur, :]) + _dot(dsp, k_ref[prev, :])
                cur_ref[0, cur, :] = jnp.where(lane_h, dq[:BLK], dq[BLK:]) * 0.125
                cur_ref[1, cur, :] = _dot_tn(dsc, qs)
                cur_ref[2, cur, :] = _dot_tn(pc, dos)
                prev_ref[0, cur, :] = _dot_tn(dsp, qs)
                prev_ref[1, cur, :] = _dot_tn(pp, dos)

    return _call(
        body, grid=(3, 4), name=name, carry=carry,
        in_specs=_dl_in_specs(s_len) + [pl.BlockSpec((None, s_len, LANES), lambda n, hp: (n, 0, hp)),
                                        pl.BlockSpec((2, None, s_len, LANES), lambda n, hp: (0, n, 0, hp))],
        out_specs=[pl.BlockSpec((3, None, s_len, LANES), lambda n, hp: (0, n, 0, hp)),
                   pl.BlockSpec((2, None, s_len, LANES), lambda n, hp: (0, n, 0, hp))],
        out_shape=[jax.ShapeDtypeStruct((3, 3, s_len, DL_W), F32), jax.ShapeDtypeStruct((2, 3, s_len, DL_W), F32)],
        vmem_mb=56, args=(qkv, qkv, qkv, dop, stats))


def _dl_bwd_finish(cur, prev, cos, sin, *, name):
    s_len = cur.shape[2]

    def body(c_ref, p_ref, cos_ref, sin_ref, o_ref, p_scr, u_scr, acc):
        a, n = pl.program_id(0), pl.program_id(2)
        has_prev = jnp.where(a > 0, 1.0, 0.0)
        p_scr[...] = c_ref[...]
        p_scr[0:s_len - BLK, :] += has_prev * p_ref[BLK:, :]
        for k, d in enumerate(DILATIONS):
            @pl.when(n == k)
            def _(k=k, d=d):
                if k == 0:
                    acc[...] = p_scr[...]
                else:
                    _unpermute_rows(u_scr, p_scr, d)
                    acc[...] += u_scr[...]

        @pl.when(n == 2)
        def _():
            dy = acc[...]
            rot = a < 2
            o_ref[...] = (dy * jnp.where(rot, cos_ref[...], 1.0)
                          + _rot_half(dy * jnp.where(rot, sin_ref[...], 0.0))).astype(BF16)

    tab = pl.BlockSpec((s_len, LANES), lambda a, cb, n: (0, 0))
    return pl.pallas_call(
        body, grid=(3, 4, 3), name=name,
        in_specs=[pl.BlockSpec((None, None, s_len, LANES), lambda a, cb, n: (a, n, 0, cb)),
                  pl.BlockSpec((None, None, s_len, LANES), lambda a, cb, n: (jnp.maximum(a - 1, 0), n, 0, cb)),
                  tab, tab],
        out_specs=pl.BlockSpec((s_len, LANES), lambda a, cb, n: (0, 4 * a + cb)),
        out_shape=jax.ShapeDtypeStruct((s_len, 3 * DL_W), BF16),
        scratch_shapes=[pltpu.VMEM((s_len, LANES), F32)] * 3, compiler_params=_cp())(cur, prev, cos, sin)


XA_T = 256


def _xa_probs(q, k):
    s = _dot_nt(q, k) * (X_HD ** -0.5)
    e = jnp.exp(s - jnp.max(s, axis=1, keepdims=True))
    return e / jnp.sum(e, axis=1, keepdims=True)


def _xa_fwd(q, k, v, *, name):
    s_len, d = q.shape
    nm = k.shape[0]

    def body(q_ref, k_ref, v_ref, o_ref):
        for h in range(X_HEADS):
            cs = slice(h * X_HD, (h + 1) * X_HD)
            p = _xa_probs(q_ref[:, cs], k_ref[:, cs])
            o_ref[:, cs] = _dot(p.astype(BF16), v_ref[:, cs]).astype(BF16)

    row = pl.BlockSpec((XA_T, d), lambda i: (i, 0))
    full = pl.BlockSpec((nm, d), lambda i: (0, 0))
    return pl.pallas_call(body, grid=(s_len // XA_T,), name=name, in_specs=[row, full, full], out_specs=row,
                          out_shape=jax.ShapeDtypeStruct((s_len, d), BF16), compiler_params=_cp())(q, k, v)


def _xa_bwd(q, k, v, do, *, name, carry=None):
    s_len, d = q.shape
    nm = k.shape[0]

    def body(q_ref, k_ref, v_ref, do_ref, dq_ref, dk_ref, dv_ref):
        i = pl.program_id(0)
        for h in range(X_HEADS):
            cs = slice(h * X_HD, (h + 1) * X_HD)
            qh, kh, vh, doh = q_ref[:, cs], k_ref[:, cs], v_ref[:, cs], do_ref[:, cs]
            p = _xa_probs(qh, kh)
            dp = _dot_nt(doh, vh)
            ds = (p * (dp - jnp.sum(dp * p, axis=1, keepdims=True)) * (X_HD ** -0.5)).astype(BF16)
            dq_ref[:, cs] = _dot(ds, kh).astype(BF16)
            dkh, dvh = _dot_tn(ds, qh), _dot_tn(p.astype(BF16), doh)

            @pl.when(i == 0)
            def _(cs=cs, dkh=dkh, dvh=dvh):
                dk_ref[:, cs] = dkh
                dv_ref[:, cs] = dvh

            @pl.when(i > 0)
            def _(cs=cs, dkh=dkh, dvh=dvh):
                dk_ref[:, cs] += dkh
                dv_ref[:, cs] += dvh

    row = pl.BlockSpec((XA_T, d), lambda i: (i, 0))
    full = pl.BlockSpec((nm, d), lambda i: (0, 0))
    return _call(
        body, grid=(s_len // XA_T,), name=name, carry=carry, in_specs=[row, full, full, row],
        out_specs=[row, full, full],
        out_shape=[jax.ShapeDtypeStruct((s_len, d), BF16), jax.ShapeDtypeStruct((nm, d), F32),
                   jax.ShapeDtypeStruct((nm, d), F32)], args=(q, k, v, do))


FF_TM, FF_TN, FF_H = 512, 256, 8
GELU_K, GELU_C = 0.7978845608028654, 0.044715


FF_STRIP = 64


def _ff_conv(e_ref, w_ref, b_ref, rows, r0=0):
    return (w_ref[0:1, :] * e_ref[pl.ds(FF_H - 2 + r0, rows), :] + w_ref[1:2, :] * e_ref[pl.ds(FF_H - 1 + r0, rows), :]
            + w_ref[2:3, :] * e_ref[pl.ds(FF_H + r0, rows), :] + b_ref[...])


def _strips(total, size):
    return [(r0, min(size, total - r0)) for r0 in range(0, total, size)]


def _ff_gate_fwd(up, conv_w, conv_b, *, name, carry=None):
    s_len = up.shape[0]
    nj = D_FF // FF_TN

    def body(g_ref, v_ref, gp_ref, vp_ref, wg_ref, wv_ref, bg_ref, bv_ref, o_ref, eg, ev):
        i = pl.program_id(0)
        for e, cur, prev in ((eg, g_ref, gp_ref), (ev, v_ref, vp_ref)):
            e[0:FF_H, :] = jnp.where(i > 0, prev[...], 0.0)
            e[FF_H:, :] = cur[...]
        for r0, rows in _strips(FF_TM, FF_STRIP):
            gate = _ff_conv(eg, wg_ref, bg_ref, rows, r0)
            val = _ff_conv(ev, wv_ref, bv_ref, rows, r0)
            t = jnp.tanh(GELU_K * (gate + GELU_C * gate * gate * gate))
            o_ref[r0:r0 + rows, :] = (0.5 * gate * (1.0 + t) * val).astype(BF16)

    cur = lambda c0: pl.BlockSpec((FF_TM, FF_TN), lambda i, j: (i, c0 + j))
    prev = lambda c0: pl.BlockSpec((FF_H, FF_TN), lambda i, j: (jnp.maximum(i * (FF_TM // FF_H) - 1, 0), c0 + j))
    par = lambda r, c0: pl.BlockSpec((r, FF_TN), lambda i, j: (0, c0 + j))
    return _call(
        body, grid=(s_len // FF_TM, nj), name=name, carry=carry,
        in_specs=[cur(0), cur(nj), prev(0), prev(nj), par(3, 0), par(3, nj), par(1, 0), par(1, nj)],
        out_specs=[cur(0)], out_shape=[jax.ShapeDtypeStruct((s_len, D_FF), BF16)],
        scratch_shapes=[pltpu.VMEM((FF_TM + FF_H, FF_TN), F32)] * 2,
        args=(up, up, up, up, conv_w, conv_w, conv_b, conv_b))


def _ff_gate_bwd(up, dact, conv_w, conv_b, *, name, carry=None):
    s_len = up.shape[0]
    nj = D_FF // FF_TN
    last = s_len // FF_TM - 1
    ext = FF_TM + FF_H

    def body(g_ref, v_ref, gp_ref, vp_ref, gn_ref, vn_ref, da_ref, dan_ref, wg_ref, wv_ref, bg_ref, bv_ref,
             dg_ref, dv_ref, dw_ref, db_ref, eg, ev, sg, sv):
        i = pl.program_id(1)
        for e, cur, prev, nxt in ((eg, g_ref, gp_ref, gn_ref), (ev, v_ref, vp_ref, vn_ref)):
            e[0:FF_H, :] = jnp.where(i > 0, prev[...], 0.0)
            e[FF_H:FF_H + FF_TM, :] = cur[...]
            e[FF_H + FF_TM:, :] = nxt[...]
        for r0, rows in _strips(ext, FF_STRIP):
            gate = _ff_conv(eg, wg_ref, bg_ref, rows, r0)
            val = _ff_conv(ev, wv_ref, bv_ref, rows, r0)
            dact = da_ref[r0:r0 + rows, :] if r0 < FF_TM else jnp.where(i < last, dan_ref[...], 0.0)
            t = jnp.tanh(GELU_K * (gate + GELU_C * gate * gate * gate))
            half = 0.5 * (1.0 + t)
            dgelu = half + 0.5 * gate * (1.0 - t * t) * GELU_K * (1.0 + 3.0 * GELU_C * gate * gate)
            sg[r0:r0 + rows, :] = dact * val * dgelu
            sv[r0:r0 + rows, :] = dact * (gate * half)
        for part, (s, e, w_ref, out) in enumerate(((sg, eg, wg_ref, dg_ref), (sv, ev, wv_ref, dv_ref))):
            taps, bias = [jnp.zeros((8, FF_TN), F32)] * 3, jnp.zeros((8, FF_TN), F32)
            for r0, rows in _strips(FF_TM, FF_STRIP):
                d0 = s[pl.ds(r0, rows), :]
                out[r0:r0 + rows, :] = (w_ref[2:3, :] * d0 + w_ref[1:2, :] * s[pl.ds(r0 + 1, rows), :]
                                        + w_ref[0:1, :] * s[pl.ds(r0 + 2, rows), :]).astype(BF16)
                taps = [taps[k] + _rowsum8(d0 * e[pl.ds(FF_H - 2 + k + r0, rows), :]) for k in range(3)]
                bias = bias + _rowsum8(d0)
            _acc_out(dw_ref.at[part], i, jnp.concatenate(taps, axis=0))
            _acc_out(db_ref.at[part], i, bias)

    cur = lambda c0: pl.BlockSpec((FF_TM, FF_TN), lambda j, i: (i, c0 + j))
    prev = lambda c0: pl.BlockSpec((FF_H, FF_TN), lambda j, i: (jnp.maximum(i * (FF_TM // FF_H) - 1, 0), c0 + j))
    nxt = lambda c0: pl.BlockSpec(
        (FF_H, FF_TN), lambda j, i: (jnp.minimum((i + 1) * (FF_TM // FF_H), s_len // FF_H - 1), c0 + j))
    par = lambda r, c0: pl.BlockSpec((r, FF_TN), lambda j, i: (0, c0 + j))
    return _call(
        body, grid=(nj, s_len // FF_TM), name=name, carry=carry,
        in_specs=[cur(0), cur(nj), prev(0), prev(nj), nxt(0), nxt(nj), cur(0), nxt(0),
                  par(3, 0), par(3, nj), par(1, 0), par(1, nj)],
        out_specs=[cur(0), cur(0), pl.BlockSpec((2, 24, FF_TN), lambda j, i: (0, 0, j)),
                   pl.BlockSpec((2, 8, FF_TN), lambda j, i: (0, 0, j))],
        out_shape=[jax.ShapeDtypeStruct((s_len, D_FF), BF16), jax.ShapeDtypeStruct((s_len, D_FF), BF16),
                   jax.ShapeDtypeStruct((2, 24, D_FF), F32), jax.ShapeDtypeStruct((2, 8, D_FF), F32)],
        scratch_shapes=[pltpu.VMEM((FF_TM + 2 * FF_H, FF_TN), F32)] * 2 + [pltpu.VMEM((ext, FF_TN), F32)] * 2,
        args=(up, up, up, up, up, up, dact, dact, conv_w, conv_w, conv_b, conv_b))


def _place():
    x, y, c = lax.axis_index("x"), lax.axis_index("y"), lax.axis_index("c")
    return x, y, c, [(1 - x, y), (x, 1 - y), (1 - x, 1 - y)]


def _remote(src, dst, send_sem, recv_sem, dev):
    return pltpu.make_async_remote_copy(src_ref=src, dst_ref=dst, send_sem=send_sem, recv_sem=recv_sem,
                                        device_id=dev, device_id_type=MESH)


_ANY = pl.BlockSpec(memory_space=pl.ANY)


N_SEMS = 8
SEM_BASE_2 = 4


class _Exchange:
    def __init__(self, operands, out_shapes, start, wait, aliases=None):
        self.operands, self.out_shapes, self.start, self.wait = list(operands), list(out_shapes), start, wait
        self.aliases = aliases or {}


def _sem_scratch():
    return [pltpu.SemaphoreType.DMA((N_SEMS,)), pltpu.SemaphoreType.DMA((N_SEMS,)), pltpu.SemaphoreType.DMA]


def _run_exchange(ex, *, name):
    k, n = len(ex.operands), len(ex.out_shapes)

    def body(*refs):
        ins, outs, sems = refs[:k], refs[k:k + n], refs[k + n:]
        ex.start(ins, outs, *sems)
        ex.wait(ins, outs, *sems)

    return pl.pallas_call(body, name=name, in_specs=[_ANY] * k, out_specs=[_ANY] * n, out_shape=ex.out_shapes,
                          scratch_shapes=_sem_scratch(), input_output_aliases=ex.aliases,
                          compiler_params=_cp(16))(*ex.operands)


def _call(body, *, grid, in_specs, out_specs, out_shape, args, name, scratch_shapes=(), vmem_mb=48, carry=None):
    scratch_shapes = list(scratch_shapes)
    if carry is None:
        return pl.pallas_call(body, grid=grid, name=name, in_specs=in_specs, out_specs=out_specs, out_shape=out_shape,
                              scratch_shapes=scratch_shapes, compiler_params=_cp(vmem_mb))(*args)
    n_in, n_out, n_scr = len(in_specs), len(out_shape), len(scratch_shapes)
    k_in, k_out = len(carry.operands), len(carry.out_shapes)

    def wrapped(*refs):
        ins, refs = refs[:n_in], refs[n_in:]
        cin, refs = refs[:k_in], refs[k_in:]
        outs, refs = refs[:n_out], refs[n_out:]
        cout, refs = refs[:k_out], refs[k_out:]
        scratch, sems = refs[:n_scr], refs[n_scr:]
        ids = [pl.program_id(a) for a in range(len(grid))]
        first = functools.reduce(jnp.logical_and, [i == 0 for i in ids])
        last = functools.reduce(jnp.logical_and, [i == g - 1 for i, g in zip(ids, grid)])

        @pl.when(first)
        def _():
            carry.start(cin, cout, *sems)

        body(*ins, *outs, *scratch)

        @pl.when(last)
        def _():
            carry.wait(cin, cout, *sems)

    aliases = {n_in + i: n_out + o for i, o in carry.aliases.items()}
    return pl.pallas_call(
        wrapped, grid=grid, name=name, in_specs=list(in_specs) + [_ANY] * k_in,
        out_specs=list(out_specs) + [_ANY] * k_out, out_shape=list(out_shape) + carry.out_shapes,
        scratch_shapes=scratch_shapes + _sem_scratch(), input_output_aliases=aliases,
        compiler_params=_cp(vmem_mb))(*args, *carry.operands)


def _half_rows(ref_rows, c):
    half = ref_rows // 2
    return pl.ds(c * half, half)


def _ex_join(a, b):
    ka, na = len(a.operands), len(a.out_shapes)

    def start(ins, outs, *sems):
        a.start(ins[:ka], outs[:na], *sems)
        b.start(ins[ka:], outs[na:], *sems)

    def wait(ins, outs, *sems):
        a.wait(ins[:ka], outs[:na], *sems)
        b.wait(ins[ka:], outs[na:], *sems)

    aliases = dict(a.aliases)
    aliases.update({ka + i: na + o for i, o in b.aliases.items()})
    return _Exchange(a.operands + b.operands, a.out_shapes + b.out_shapes, start, wait, aliases)


def _ex_gather(pack, r0, rl, base=0):
    def copies(ins, outs, send, recv):
        x, y, c, chips = _place()
        rows = _half_rows(rl, c)
        src = ins[0].at[pl.ds(r0 + c * (rl // 2), rl // 2)]
        sends = [_remote(src, outs[0].at[2 * x + y, rows], send.at[base + k], recv.at[base + k], (px, py, c))
                 for k, (px, py) in enumerate(chips)]
        lands = [_remote(src, outs[0].at[2 * px + py, rows], send.at[base + k], recv.at[base + k], (px, py, c))
                 for k, (px, py) in enumerate(chips)]
        return sends, lands

    def mine(ins, outs, local):
        x, y, _, _ = _place()
        return pltpu.make_async_copy(ins[0].at[pl.ds(r0, rl)], outs[0].at[2 * x + y], local)

    def start(ins, outs, send, recv, local):
        mine(ins, outs, local).start()
        for cp in copies(ins, outs, send, recv)[0]:
            cp.start()

    def wait(ins, outs, send, recv, local):
        sends, lands = copies(ins, outs, send, recv)
        for cp in lands:
            cp.wait_recv()
        for cp in sends:
            cp.wait_send()
        mine(ins, outs, local).wait()

    return _Exchange([pack], [jax.ShapeDtypeStruct((4, rl, pack.shape[1]), pack.dtype)], start, wait)


def _ex_gather_forward(g, base=0):
    rl = g.shape[1]

    def copies(outs, send, recv):
        x, y, c, chips = _place()
        slabs = [(outs[0].at[2 * px + py, _half_rows(rl, c)], outs[0].at[2 * px + py, _half_rows(rl, 1 - c)])
                 for px, py in chips]
        sends = [_remote(a, a, send.at[base + k], recv.at[base + k], (x, y, 1 - c)) for k, (a, _) in enumerate(slabs)]
        lands = [_remote(b, b, send.at[base + k], recv.at[base + k], (x, y, 1 - c)) for k, (_, b) in enumerate(slabs)]
        return sends, lands

    def start(ins, outs, send, recv, local):
        for cp in copies(outs, send, recv)[0]:
            cp.start()

    def wait(ins, outs, send, recv, local):
        sends, lands = copies(outs, send, recv)
        for cp in lands:
            cp.wait_recv()
        for cp in sends:
            cp.wait_send()

    return _Exchange([g], [jax.ShapeDtypeStruct(g.shape, g.dtype)], start, wait, aliases={0: 0})


def _ex_swap_halves(gw, base=0):
    nb, rl, d = gw.shape

    def copies(ins, outs, send, recv):
        x, y, c, _ = _place()
        return [_remote(ins[0].at[j, _half_rows(rl, 1 - c)], outs[0].at[j], send.at[base + j], recv.at[base + j],
                        (x, y, 1 - c)) for j in range(nb)]

    def start(ins, outs, send, recv, local):
        for cp in copies(ins, outs, send, recv):
            cp.start()

    def wait(ins, outs, send, recv, local):
        for cp in copies(ins, outs, send, recv):
            cp.wait()

    return _Exchange([gw], [jax.ShapeDtypeStruct((nb, rl // 2, d), gw.dtype)], start, wait)


def _chip_sum(gw, got, c_arr, *, name):
    nchip, half, d = got.shape
    tr = _tile(half, 512)

    def body(c_ref, a_ref, b_ref, o32_ref, o16_ref):
        s = a_ref[...] + b_ref[...]
        o32_ref[...] = s
        o16_ref[...] = s.astype(BF16)

    blk = pl.BlockSpec((None, tr, d), lambda j, i, c_ref: (j, i, 0))
    return pl.pallas_call(
        body, name=name,
        grid_spec=pltpu.PrefetchScalarGridSpec(
            num_scalar_prefetch=1, grid=(nchip, half // tr),
            in_specs=[pl.BlockSpec((None, tr, d), lambda j, i, c_ref: (j, c_ref[0] * (half // tr) + i, 0)), blk],
            out_specs=[blk, blk]),
        out_shape=[jax.ShapeDtypeStruct((nchip, half, d), F32), jax.ShapeDtypeStruct((nchip, half, d), BF16)],
        compiler_params=_cp())(c_arr, gw, got)


def _ex_scatter(s16, base=0):
    def copies(ins, outs, send, recv):
        x, y, c, chips = _place()
        return [_remote(ins[0].at[2 * px + py], outs[0].at[k], send.at[base + k], recv.at[base + k], (px, py, c))
                for k, (px, py) in enumerate(chips)]

    def start(ins, outs, send, recv, local):
        for cp in copies(ins, outs, send, recv):
            cp.start()

    def wait(ins, outs, send, recv, local):
        for cp in copies(ins, outs, send, recv):
            cp.wait()

    return _Exchange([s16], [jax.ShapeDtypeStruct((3,) + s16.shape[1:], s16.dtype)], start, wait)


def _mesh_sum(s32, got, j_arr, *, name):
    _, rl, d = s32.shape
    tr = _tile(rl, 512)

    def body(j_ref, a_ref, b_ref, o_ref):
        o_ref[...] = ((a_ref[...] + b_ref[0].astype(F32)) + b_ref[1].astype(F32)) + b_ref[2].astype(F32)

    return pl.pallas_call(
        body, name=name,
        grid_spec=pltpu.PrefetchScalarGridSpec(
            num_scalar_prefetch=1, grid=(rl // tr,),
            in_specs=[pl.BlockSpec((None, tr, d), lambda i, j_ref: (j_ref[0], i, 0)),
                      pl.BlockSpec((3, tr, d), lambda i, j_ref: (0, i, 0))],
            out_specs=pl.BlockSpec((tr, d), lambda i, j_ref: (i, 0))),
        out_shape=jax.ShapeDtypeStruct((rl, d), F32), compiler_params=_cp())(j_arr, s32, got)


def _ex_share_halves(ghalf):
    half, d = ghalf.shape

    def copies(ins, outs, send, recv, local):
        x, y, c, _ = _place()
        there = outs[0].at[_half_rows(2 * half, c)]
        back = outs[0].at[_half_rows(2 * half, 1 - c)]
        return (_remote(ins[0], there, send.at[0], recv.at[0], (x, y, 1 - c)),
                _remote(ins[0], back, send.at[0], recv.at[0], (x, y, 1 - c)), pltpu.make_async_copy(ins[0], there, local))

    def start(ins, outs, send, recv, local):
        out, _, mine = copies(ins, outs, send, recv, local)
        mine.start()
        out.start()

    def wait(ins, outs, send, recv, local):
        out, back, mine = copies(ins, outs, send, recv, local)
        back.wait_recv()
        out.wait_send()
        mine.wait()

    return _Exchange([ghalf], [jax.ShapeDtypeStruct((2 * half, d), ghalf.dtype)], start, wait)


class _ReduceScatter:
    def __init__(self, gw, c_arr, j_arr, tag):
        self.gw, self.c_arr, self.j_arr, self.tag = gw, c_arr, j_arr, tag

    def swap(self, base=0):
        return _ex_swap_halves(self.gw, base)

    def after_swap(self, got, base=0):
        self.s32, s16 = _chip_sum(self.gw, got, self.c_arr, name=f"rs_chip_sum{self.tag}")
        return _ex_scatter(s16, base)

    def after_scatter(self, got16):
        ghalf = _mesh_sum(self.s32, got16, self.j_arr, name=f"rs_mesh_sum{self.tag}")
        return _run_exchange(_ex_share_halves(ghalf), name=f"rs_share{self.tag}")[0]

    def run(self):
        got, = _run_exchange(self.swap(), name=f"rs_swap{self.tag}")
        got16, = _run_exchange(self.after_swap(got), name=f"rs_scatter{self.tag}")
        return self.after_scatter(got16)


def _all_reduce_small(vec, *, name):
    rows, d = vec.shape

    def body(x_ref, o_ref, gat, send_sems, recv_sems, local_sem):
        x, y, c, chips = _place()
        me, sibling = (x, y, c), (x, y, 1 - c)

        def slot(px, py, pc):
            return gat.at[4 * px + 2 * py + pc]

        def copy(k, block, to, src=None):
            return _remote(slot(*block) if src is None else src, slot(*block), send_sems.at[k], recv_sems.at[k], to)

        mine = pltpu.make_async_copy(x_ref, slot(*me), local_sem)
        mine.start()
        first = [copy(0, me, sibling, src=x_ref)]
        first += [copy(1 + j, me, (*chip, c), src=x_ref) for j, chip in enumerate(chips)]
        for cp in first:
            cp.start()
        passed = [copy(4 + j, (*chip, c), sibling) for j, chip in enumerate(chips)]
        for j, chip in enumerate(chips):
            copy(1 + j, (*chip, c), me).wait_recv()
            passed[j].start()
        copy(0, sibling, me).wait_recv()
        for j, chip in enumerate(chips):
            copy(4 + j, (*chip, 1 - c), me).wait_recv()
        for cp in first + passed:
            cp.wait_send()
        mine.wait()
        acc = gat[0]
        for dev in range(1, 8):
            acc = acc + gat[dev]
        o_ref[...] = acc

    vm = pl.BlockSpec(memory_space=pltpu.VMEM)
    return pl.pallas_call(
        body, name=name, in_specs=[vm], out_specs=vm, out_shape=jax.ShapeDtypeStruct((rows, d), F32),
        scratch_shapes=[pltpu.VMEM((8, rows, d), F32), pltpu.SemaphoreType.DMA((7,)), pltpu.SemaphoreType.DMA((7,)),
                        pltpu.SemaphoreType.DMA],
        compiler_params=_cp(32))(vec)


COL_SHARDED = ("w_in", "ffn_w_up")


def _to_pack_rows(name, shard):
    return shard.reshape(-1, D_MODEL)


def _full_from_blocks(name, blocks):
    rows = blocks.shape[1]
    if name in COL_SHARDED:
        return blocks.reshape(4, D_MODEL, rows).transpose(1, 0, 2).reshape(D_MODEL, 4 * rows)
    return blocks.reshape(4 * rows, D_MODEL)


def _blocks_from_full(name, full):
    if name in COL_SHARDED:
        cols = full.shape[1] // 4
        return full.reshape(D_MODEL, 4, cols).transpose(1, 0, 2).reshape(4, cols, D_MODEL)
    return full.reshape(4, full.shape[0] // 4, D_MODEL)


def _row(v):
    return v.reshape(1, -1)


SMALL = (("mix_norm_pre", (1024,), None), ("cv_w", (31, 256), 1), ("cv_b", (256,), None), ("cv_ln_g", (256,), None),
         ("cv_ln_b", (256,), None), ("cv_pw_w", (256, 256), 0), ("cv_pw_b", (256,), None),
         ("mix_norm_post", (1024,), None), ("x_norm_pre", (1024,), None), ("mem_norm", (1024,), None),
         ("x_norm_post", (1024,), None), ("ffn_norm_pre", (1024,), None), ("ffn_conv_w", (3, 5632), 1),
         ("ffn_conv_b", (5632,), None), ("ffn_norm_post", (1024,), None))
BIG = tuple(n for n, _ in PACK_ROWS)
WEIGHT_ORDER = ("mix_norm_pre", "w_in", "cv_w", "cv_b", "cv_ln_g", "cv_ln_b", "cv_pw_w", "cv_pw_b", "w_out",
                "mix_norm_post", "x_norm_pre", "mem_norm", "x_wq", "x_wk", "x_wv", "x_wo", "x_norm_post",
                "ffn_norm_pre", "ffn_w_up", "ffn_conv_w", "ffn_conv_b", "ffn_w_down", "ffn_norm_post")


def _flat_rows(parts):
    v = jnp.concatenate([p.reshape(-1) for p in parts])
    rows = -(-v.shape[0] // (8 * D_MODEL)) * 8
    return jnp.pad(v, (0, rows * D_MODEL - v.shape[0])).reshape(rows, D_MODEL)


REST_GROUP = ("w_in", "w_out")
XA_GROUP = ("x_wq", "x_wk", "x_wv", "x_wo")
FFN_GROUP = ("ffn_w_up", "ffn_w_down")


class _Weights:
    FIRST = (0, 704)
    OWN = ((704, 1280), (1984, 1408), (3392, 704))
    NEXT = ((0, 960), (960, 1024), (1984, 1408), (3392, 704))
    SLOTS = ("mix_in", "sb_fwd", "dl_fwd", "dl_mix", "ffn_up", "ffn_gate", "ffn_down")

    def __init__(self, packs):
        self.packs, self.pieces, self.landed, self.plan = packs, {}, None, {}
        for slot, piece in zip(self.SLOTS[:3], self.OWN):
            self.plan[(0, slot)] = (0,) + piece
        for l in range(len(packs) - 1):
            for slot, piece in zip(self.SLOTS[3:], self.NEXT):
                self.plan[(l, slot)] = (l + 1,) + piece
        first = _run_exchange(_ex_gather(packs[0], *self.FIRST), name="gather_first")[0]
        self.pieces[(0,) + self.FIRST] = _run_exchange(_ex_gather_forward(first), name="gather_first_forward")[0]

    def ride(self, layer, slot, call):
        start, todo, ex = self.plan.get((layer, slot)), [], None
        if start is not None:
            ex = _ex_gather(self.packs[start[0]], start[1], start[2])
            todo.append(("landed", start))
        if self.landed is not None:
            key, buf = self.landed
            forward = _ex_gather_forward(buf, SEM_BASE_2 if ex is not None else 0)
            ex = forward if ex is None else _ex_join(ex, forward)
            todo.append(("piece", key))
            self.landed = None
        outs = list(call(carry=ex))
        n = len(outs) - len(todo)
        for (kind, key), buf in zip(todo, outs[n:]):
            if kind == "landed":
                self.landed = (key, buf)
            else:
                self.pieces[key] = buf
        return outs[:n]

    def weight(self, layer, name):
        off = 0
        for n, rows in PACK_ROWS:
            if n == name:
                break
            off += rows
        for (l, r0, nrows), buf in self.pieces.items():
            if l == layer and r0 <= off < r0 + nrows:
                return _full_from_blocks(name, buf[:, off - r0:off - r0 + rows, :])
        raise KeyError(f"{name} of layer {layer} is not gathered yet")


class _Params:
    def __init__(self, weights, layer, small):
        self.weights, self.layer, self.small, self.cache = weights, layer, small, {}

    def __getitem__(self, name):
        if name in self.small:
            return self.small[name]
        if name not in self.cache:
            self.cache[name] = self.weights.weight(self.layer, name)
        return self.cache[name]


def _layer_fwd(h0, mem, p, cos, sin, tag, ride):
    sv = {"h0": h0}
    n1, u = ride("mix_in", functools.partial(_rms_mm, h0, _row(p["mix_norm_pre"]), p["w_in"], tm=1024, tn=1408,
                                             out_dtype=F32, name=f"mix_in{tag}"))
    a_out, = ride("sb_fwd", functools.partial(_sb_fwd, u, name=f"sb_fwd{tag}"))
    b_out, c = _cv_fwd(u, p["cv_w"], _row(p["cv_b"]), _row(p["cv_ln_g"]), _row(p["cv_ln_b"]),
                       p["cv_pw_w"].astype(BF16), _row(p["cv_pw_b"]), name=f"cv_fwd{tag}")
    qkv = _rope_perm(u, cos, sin, name=f"rope_perm{tag}")
    o_p, l_p = ride("dl_fwd", functools.partial(_dl_fwd, qkv, name=f"dl_fwd{tag}"))
    c_out, o_dl, lse = ride("dl_mix", functools.partial(_dl_mix, o_p, l_p, name=f"dl_mix{tag}"))
    cat = jnp.concatenate([a_out, b_out, c_out], axis=1)
    y1, h1 = _mm_post(cat, p["w_out"], h0, _row(p["mix_norm_post"]), tm=512, name=f"mix_out{tag}")
    sv.update(n1=n1, u=u, c=c, qkv=qkv, o_dl=o_dl, lse=lse, cat=cat, y1=y1, h1=h1)

    n2, q = _rms_mm(h1, _row(p["x_norm_pre"]), p["x_wq"], tm=512, tn=1024, out_dtype=BF16, name=f"xa_q{tag}")
    wkv = jnp.concatenate([p["x_wk"], p["x_wv"]], axis=1)
    mem_n, kv = _rms_mm(mem, _row(p["mem_norm"]), wkv, tm=mem.shape[0], tn=1024, out_dtype=BF16, name=f"xa_kv{tag}")
    k, v = kv[:, :D_MODEL], kv[:, D_MODEL:]
    o_x = _xa_fwd(q, k, v, name=f"xa_fwd{tag}")
    y2, h2 = _mm_post(o_x, p["x_wo"], h1, _row(p["x_norm_post"]), tm=512, name=f"xa_out{tag}")
    sv.update(n2=n2, q=q, mem_n=mem_n, k=k, v=v, o_x=o_x, y2=y2, h2=h2, wkv=wkv)

    n3, up = ride("ffn_up", functools.partial(_rms_mm, h2, _row(p["ffn_norm_pre"]), p["ffn_w_up"], tm=1024, tn=1408,
                                              out_dtype=F32, name=f"ffn_up{tag}"))
    act, = ride("ffn_gate", functools.partial(_ff_gate_fwd, up, p["ffn_conv_w"], _row(p["ffn_conv_b"]),
                                              name=f"ffn_gate{tag}"))
    y3, h3 = ride("ffn_down", functools.partial(_mm_post, act, p["ffn_w_down"], h2, _row(p["ffn_norm_post"]), tm=512,
                                                name=f"ffn_down{tag}"))
    sv.update(n3=n3, up=up, act=act, y3=y3)
    return h3, sv


def _layer_bwd(dh3, mem, p, sv, cos, sin, tag, riding, new_rs):
    g = {}
    s8 = lambda part: part.sum(axis=0)
    rode = None

    dy3, dgp = _rms_bwd(sv["y3"], _row(p["ffn_norm_post"]), dh3, None, out_dtype=BF16, tm=512, name=f"ffn_post_b{tag}")
    g["ffn_norm_post"] = s8(dgp)
    dact = _mm_nt(dy3, p["ffn_w_down"], tm=512, tn=1408, out_dtype=F32, name=f"ffn_down_bx{tag}")
    g["ffn_w_down"] = _mm_tn(sv["act"], dy3, tk=1408, tn=1024, tm=2048, name=f"ffn_down_bw{tag}")
    dgu, dvu, dcw, dcb, *got = _ff_gate_bwd(sv["up"], dact, p["ffn_conv_w"], _row(p["ffn_conv_b"]),
                                            name=f"ffn_gate_b{tag}", carry=riding.swap() if riding else None)
    scatter = riding.after_swap(got[0]) if riding else None
    g["ffn_conv_w"] = jnp.concatenate([dcw[0], dcw[1]], axis=1).reshape(3, 8, 2 * D_FF).sum(axis=1)
    g["ffn_conv_b"] = jnp.concatenate([dcb[0], dcb[1]], axis=1).sum(axis=0)
    dup = jnp.concatenate([dgu, dvu], axis=1)
    dn3 = _mm_nt(dup, p["ffn_w_up"], tm=256, tn=512, out_dtype=F32, name=f"ffn_up_bx{tag}")
    g["ffn_w_up"] = _mm_tn(sv["n3"], dup, tk=512, tn=1408, tm=2048, name=f"ffn_up_bw{tag}")
    ffn_rs = new_rs(FFN_GROUP, g, f"{tag}_ffn")
    dh2, dgp = _rms_bwd(sv["h2"], _row(p["ffn_norm_pre"]), dn3, dh3, out_dtype=F32, tm=512, name=f"ffn_pre_b{tag}")
    g["ffn_norm_pre"] = s8(dgp)

    dy2, dgp = _rms_bwd(sv["y2"], _row(p["x_norm_post"]), dh2, None, out_dtype=BF16, tm=512, name=f"xa_post_b{tag}")
    g["x_norm_post"] = s8(dgp)
    do_x = _mm_nt(dy2, p["x_wo"], tm=512, tn=1024, out_dtype=BF16, name=f"xa_out_bx{tag}")
    g["x_wo"] = _mm_tn(sv["o_x"], dy2, tk=512, tn=1024, tm=2048, name=f"xa_out_bw{tag}")
    dq, dk, dv, got = _xa_bwd(sv["q"], sv["k"], sv["v"], do_x, name=f"xa_bwd{tag}", carry=ffn_rs.swap())
    ffn_scatter = ffn_rs.after_swap(got)
    dn2 = _mm_nt(dq, p["x_wq"], tm=512, tn=1024, out_dtype=F32, name=f"xa_q_bx{tag}")
    g["x_wq"] = _mm_tn(sv["n2"], dq, tk=512, tn=1024, tm=2048, name=f"xa_q_bw{tag}")
    dkv = jnp.concatenate([dk, dv], axis=1).astype(BF16)
    nm = mem.shape[0]
    dmem_n = _mm_nt(dkv, sv["wkv"], tm=nm, tn=1024, out_dtype=F32, name=f"xa_kv_bx{tag}")
    dwkv = _mm_tn(sv["mem_n"], dkv, tk=512, tn=2048, tm=nm, name=f"xa_kv_bw{tag}")
    g["x_wk"], g["x_wv"] = dwkv[:, :D_MODEL], dwkv[:, D_MODEL:]
    _, dgp = _rms_bwd(mem, _row(p["mem_norm"]), dmem_n, None, out_dtype=BF16, tm=nm, name=f"xa_mem_b{tag}")
    g["mem_norm"] = s8(dgp)
    xa_rs = new_rs(XA_GROUP, g, f"{tag}_xa")
    dh1, dgp, got = _rms_bwd(sv["h1"], _row(p["x_norm_pre"]), dn2, dh2, out_dtype=F32, tm=512, name=f"xa_pre_b{tag}",
                             carry=xa_rs.swap())
    xa_scatter = xa_rs.after_swap(got, SEM_BASE_2 if riding else 0)
    g["x_norm_pre"] = s8(dgp)

    dy1, dgp = _rms_bwd(sv["y1"], _row(p["mix_norm_post"]), dh1, None, out_dtype=BF16, tm=512, name=f"mix_post_b{tag}")
    g["mix_norm_post"] = s8(dgp)
    dcat = _mm_nt(dy1, p["w_out"], tm=512, tn=1024, out_dtype=F32, name=f"mix_out_bx{tag}")
    g["w_out"] = _mm_tn(sv["cat"], dy1, tk=512, tn=1024, tm=2048, name=f"mix_out_bw{tag}")
    u = sv["u"]
    dq_sb, dk_sb, dv_sb, *got = _sb_bwd(u, dcat, name=f"sb_bwd{tag}",
                                        carry=_ex_join(scatter, xa_scatter) if riding else xa_scatter)
    if riding:
        rode = riding.after_scatter(got[0])
    xa_rows = xa_rs.after_scatter(got[-1])
    pw_b16 = p["cv_pw_w"].astype(BF16)
    dc, dpw, vec = _cv_bwd_local(sv["c"], dcat, _row(p["cv_ln_g"]), _row(p["cv_ln_b"]), pw_b16, name=f"cv_bwd_a{tag}")
    g["cv_pw_w"] = dpw
    vec = vec.reshape(3, 8, CV_W).sum(axis=1)
    g["cv_pw_b"], g["cv_ln_g"], g["cv_ln_b"] = vec[0], vec[1], vec[2]
    du_cv, dcw, dcb = _cv_bwd_conv(u, dc, p["cv_w"], name=f"cv_bwd_b{tag}")
    g["cv_w"] = dcw.reshape(CV_K, 8, CV_W).sum(axis=1)
    g["cv_b"] = dcb.sum(axis=0)
    dop, stats = _dl_bwd_prep(dcat, sv["o_dl"], sv["lse"], name=f"dl_prep_b{tag}")
    cur, prev, got = _dl_bwd(sv["qkv"], dop, stats, name=f"dl_bwd{tag}", carry=ffn_scatter)
    ffn_rows = ffn_rs.after_scatter(got)
    du_dl = _dl_bwd_finish(cur, prev, cos, sin, name=f"dl_fin_b{tag}")
    du = jnp.concatenate([dq_sb.astype(BF16), dk_sb.astype(BF16), dv_sb.astype(BF16), du_cv, du_dl], axis=1)
    dn1 = _mm_nt(du, p["w_in"], tm=512, tn=512, out_dtype=F32, name=f"mix_in_bx{tag}")
    g["w_in"] = _mm_tn(sv["n1"], du, tk=512, tn=1408, tm=2048, name=f"mix_in_bw{tag}")
    dh0, dgp = _rms_bwd(sv["h0"], _row(p["mix_norm_pre"]), dn1, dh1, out_dtype=F32, tm=512, name=f"mix_pre_b{tag}")
    g["mix_norm_pre"] = s8(dgp)
    return dh0, g, (xa_rows, ffn_rows), rode


def _step(x, mem, positions, loss_target, w, m, v):
    depth = w["w_in"].shape[0]
    xi, yi, ci = lax.axis_index("x"), lax.axis_index("y"), lax.axis_index("c")
    chip = 2 * xi + yi
    h = x[0]
    mem0 = mem[0]
    s_len = h.shape[0]

    packs = [jnp.concatenate([_to_pack_rows(n, w[n][l]) for n in BIG], axis=0).astype(BF16) for l in range(depth)]

    small_w = []
    for l in range(depth):
        for n, shape, axis in SMALL:
            if axis is not None:
                full = jnp.zeros(shape, F32)
                full = lax.dynamic_update_slice_in_dim(full, w[n][l], chip * w[n][l].shape[axis], axis)
                small_w.append(full * jnp.where(ci == 0, 1.0, 0.0))
    small_w_sum = _all_reduce_small(_flat_rows(small_w), name="gather_small_weights")
    small_full, off = [{} for _ in range(depth)], 0
    for l in range(depth):
        for n, shape, axis in SMALL:
            if axis is not None:
                size = int(np.prod(shape))
                small_full[l][n] = small_w_sum.reshape(-1)[off:off + size].reshape(shape)
                off += size
    weights = _Weights(packs)
    params = [_Params(weights, l, {n: small_full[l].get(n, w[n][l]) for n, _, _ in SMALL}) for l in range(depth)]

    inv_freq = ROPE_THETA ** (-jnp.arange(HD // 2, dtype=F32) / (HD // 2))
    cos, sin = _rope_tables(positions.reshape(s_len, 1), jnp.tile(inv_freq, 4).reshape(1, LANES), name="rope_tables")

    saved = []
    for l in range(depth):
        h, sv = _layer_fwd(h, mem0, params[l], cos, sin, f"_l{l}", functools.partial(weights.ride, l))
        saved.append(sv)
    dh, sq = _loss_grad(h, loss_target[0], tm=512, name="loss_grad")
    loss = lax.psum(0.5 * jnp.sum(sq) / D_MODEL, ("x", "y", "c"))

    c_arr, j_arr = jnp.reshape(ci, (1,)).astype(jnp.int32), jnp.reshape(chip, (1,)).astype(jnp.int32)

    def new_rs(names, g, tag):
        gw = jnp.concatenate([_blocks_from_full(n, g[n]) for n in names], axis=1)
        return _ReduceScatter(gw, c_arr, j_arr, tag)

    grads, later_rows, rest_rows, pending = [None] * depth, [None] * depth, [None] * depth, None
    for l in reversed(range(depth)):
        dh, grads[l], later_rows[l], rode = _layer_bwd(dh, mem0, params[l], saved[l], cos, sin, f"_l{l}", pending, new_rs)
        if pending is not None:
            rest_rows[l + 1] = rode
        pending = new_rs(REST_GROUP, grads[l], f"_l{l}_rest")
    grad_x = dh[None]

    out_g, out_d, out_m, out_v = {}, {}, {}, {}
    pack_off, off = {}, 0
    for n, rows in PACK_ROWS:
        pack_off[n] = (off, rows)
        off += rows

    def reduced(l, n):
        start, rows = pack_off[n]
        for names, block in ((REST_GROUP, rest_rows[l]), (XA_GROUP, later_rows[l][0]), (FFN_GROUP, later_rows[l][1])):
            if n in names:
                return block[start - pack_off[names[0]][0]:][:rows]

    def update(n, carry=None):
        shard_shape = w[n].shape
        g_n = jnp.stack([reduced(l, n) for l in range(depth)]).reshape(shard_shape)
        flat = lambda a: a.reshape(-1, shard_shape[-1])
        d_n, m_n, v_n, *rode = _adamw(flat(w[n]), flat(g_n), flat(m[n]), flat(v[n]), name=f"adamw_{n}", carry=carry)
        out_g[n], out_d[n], out_m[n], out_v[n] = g_n, d_n.reshape(shard_shape), m_n.reshape(shard_shape), v_n.reshape(shard_shape)
        return rode

    got, = update("ffn_w_down", pending.swap())
    got16, = update("ffn_w_up", pending.after_swap(got))
    rest_rows[0] = pending.after_scatter(got16)
    for n, _ in PACK_ROWS:
        if n not in FFN_GROUP:
            update(n)

    g_small = _all_reduce_small(_flat_rows([grads[l][n] for l in range(depth) for n, _, _ in SMALL]),
                                name="all_reduce_small_grads").reshape(-1)
    local_g, off = {}, 0
    for l in range(depth):
        for n, shape, axis in SMALL:
            size = int(np.prod(shape))
            full = g_small[off:off + size].reshape(shape)
            off += size
            if axis is not None:
                blk = w[n].shape[1 + axis]
                full = lax.dynamic_slice_in_dim(full, chip * blk, blk, axis)
            local_g.setdefault(n, []).append(full)
    names = [n for n, _, _ in SMALL]
    g_loc = {n: jnp.stack(local_g[n]) for n in names}
    d_s, m_s, v_s = _adamw(_flat_rows([w[n] for n in names]), _flat_rows([g_loc[n] for n in names]),
                           _flat_rows([m[n] for n in names]), _flat_rows([v[n] for n in names]), name="adamw_small")
    off = 0
    for n in names:
        size = int(np.prod(w[n].shape))
        take = lambda a: a.reshape(-1)[off:off + size].reshape(w[n].shape)
        out_g[n], out_d[n], out_m[n], out_v[n] = g_loc[n], take(d_s), take(m_s), take(v_s)
        off += size

    outs = [loss, grad_x]
    for group in (out_g, out_d, out_m, out_v):
        outs += [group[n] for n in WEIGHT_ORDER]
    return tuple(outs)


def kernel(x, mem, positions, mix_norm_pre, w_in, cv_w, cv_b, cv_ln_g, cv_ln_b, cv_pw_w, cv_pw_b, w_out, mix_norm_post, x_norm_pre, mem_norm, x_wq, x_wk, x_wv, x_wo, x_norm_post, ffn_norm_pre, ffn_w_up, ffn_conv_w, ffn_conv_b, ffn_w_down, ffn_norm_post, loss_target, m_mix_norm_pre, m_w_in, m_cv_w, m_cv_b, m_cv_ln_g, m_cv_ln_b, m_cv_pw_w, m_cv_pw_b, m_w_out, m_mix_norm_post, m_x_norm_pre, m_mem_norm, m_x_wq, m_x_wk, m_x_wv, m_x_wo, m_x_norm_post, m_ffn_norm_pre, m_ffn_w_up, m_ffn_conv_w, m_ffn_conv_b, m_ffn_w_down, m_ffn_norm_post, v_mix_norm_pre, v_w_in, v_cv_w, v_cv_b, v_cv_ln_g, v_cv_ln_b, v_cv_pw_w, v_cv_pw_b, v_w_out, v_mix_norm_post, v_x_norm_pre, v_mem_norm, v_x_wq, v_x_wk, v_x_wv, v_x_wo, v_x_norm_post, v_ffn_norm_pre, v_ffn_w_up, v_ffn_conv_w, v_ffn_conv_b, v_ffn_w_down, v_ffn_norm_post):
    w = dict(zip(WEIGHT_ORDER, (mix_norm_pre, w_in, cv_w, cv_b, cv_ln_g, cv_ln_b, cv_pw_w, cv_pw_b, w_out, mix_norm_post, x_norm_pre, mem_norm, x_wq, x_wk, x_wv, x_wo, x_norm_post, ffn_norm_pre, ffn_w_up, ffn_conv_w, ffn_conv_b, ffn_w_down, ffn_norm_post)))
    m = dict(zip(WEIGHT_ORDER, (m_mix_norm_pre, m_w_in, m_cv_w, m_cv_b, m_cv_ln_g, m_cv_ln_b, m_cv_pw_w, m_cv_pw_b, m_w_out, m_mix_norm_post, m_x_norm_pre, m_mem_norm, m_x_wq, m_x_wk, m_x_wv, m_x_wo, m_x_norm_post, m_ffn_norm_pre, m_ffn_w_up, m_ffn_conv_w, m_ffn_conv_b, m_ffn_w_down, m_ffn_norm_post)))
    v = dict(zip(WEIGHT_ORDER, (v_mix_norm_pre, v_w_in, v_cv_w, v_cv_b, v_cv_ln_g, v_cv_ln_b, v_cv_pw_w, v_cv_pw_b, v_w_out, v_mix_norm_post, v_x_norm_pre, v_mem_norm, v_x_wq, v_x_wk, v_x_wv, v_x_wo, v_x_norm_post, v_ffn_norm_pre, v_ffn_w_up, v_ffn_conv_w, v_ffn_conv_b, v_ffn_w_down, v_ffn_norm_post)))
    return _step(x, mem, positions, loss_target, w, m, v)
```

```python
import functools

import jax
import jax.numpy as jnp
import numpy as np
from jax import lax
from jax.experimental import pallas as pl
from jax.experimental.pallas import tpu as pltpu

F32, BF16 = jnp.float32, jnp.bfloat16
MESH = pl.DeviceIdType.MESH
EPS = 1e-6
LANES = 128
BLK = 128
HD = 64
D_MODEL = 1024
D_FF = 2816
SB_W, CV_W, DL_W = 256, 256, 512
CV_K = 31
ROPE_THETA = 10000.0
DILATIONS = (1, 4, 16)
X_HEADS, X_HD = 4, 256
ADAM_LR, ADAM_B1, ADAM_B2, ADAM_EPS, ADAM_WD, ADAM_STEP = 0.001, 0.9, 0.999, 1e-08, 0.01, 10
NEG_INF = float("-inf")
MIB = 1 << 20

PACK_ROWS = (("w_in", 704), ("w_out", 256), ("x_wq", 256), ("x_wk", 256), ("x_wv", 256), ("x_wo", 256),
             ("ffn_w_up", 1408), ("ffn_w_down", 704))
PACK_RL = sum(r for _, r in PACK_ROWS)
SHARDED_SMALL = (("cv_w", (31, 64), 1), ("ffn_conv_w", (3, 1408), 1), ("cv_pw_w", (64, 256), 0))
SMALL_ROWS = 32
WEIGHT_PACK = (PACK_ROWS[0], ("small", 2 * SMALL_ROWS)) + PACK_ROWS[1:]


def _cp(vmem_mb=48):
    return pltpu.CompilerParams(vmem_limit_bytes=vmem_mb * MIB)


def _dot(a, b):
    return jnp.dot(a, b, preferred_element_type=F32)


def _dot_nt(a, b):
    return lax.dot_general(a, b, (((1,), (1,)), ((), ())), preferred_element_type=F32)


def _dot_tn(a, b):
    return lax.dot_general(a, b, (((0,), (0,)), ((), ())), preferred_element_type=F32)


def _dot_hilo(x, m):
    hi = x.astype(BF16)
    lo = (x - hi.astype(F32)).astype(BF16)
    return _dot(hi, m) + _dot(lo, m)


def _rowsum8(x):
    t, c = x.shape
    return x.reshape(t // 8, 8, c).sum(axis=0)


def _acc_out(ref, i, val):
    @pl.when(i == 0)
    def _():
        ref[...] = val

    @pl.when(i > 0)
    def _():
        ref[...] += val


def _tile(n, cap, mult=8):
    t = min(n, cap)
    while n % t or t % mult:
        t -= 1
    return t


def _rms_mm(x, g, w, *, tm, tn, out_dtype, name, carry=None):
    m, d = x.shape
    n_out = w.shape[1]

    def body(x_ref, g_ref, w_ref, n_ref, o_ref):
        @pl.when(pl.program_id(1) == 0)
        def _():
            xv = x_ref[...]
            r = lax.rsqrt(jnp.mean(xv * xv, axis=-1, keepdims=True) + EPS)
            n_ref[...] = (xv * r * g_ref[...]).astype(BF16)

        o_ref[...] = _dot(n_ref[...], w_ref[...]).astype(out_dtype)

    return _call(
        body, grid=(m // tm, n_out // tn), name=name, carry=carry,
        in_specs=[pl.BlockSpec((tm, d), lambda i, j: (i, 0)), pl.BlockSpec((1, d), lambda i, j: (0, 0)),
                  pl.BlockSpec((d, tn), lambda i, j: (0, j))],
        out_specs=[pl.BlockSpec((tm, d), lambda i, j: (i, 0)), pl.BlockSpec((tm, tn), lambda i, j: (i, j))],
        out_shape=[jax.ShapeDtypeStruct((m, d), BF16), jax.ShapeDtypeStruct((m, n_out), out_dtype)],
        args=(x, g, w))


def _mm_post(a, w, h, g, *, tm, name, carry=None):
    m, k = a.shape
    d = w.shape[1]

    def body(a_ref, w_ref, h_ref, g_ref, y_ref, ho_ref):
        y = _dot(a_ref[...], w_ref[...])
        y_ref[...] = y
        r = lax.rsqrt(jnp.mean(y * y, axis=-1, keepdims=True) + EPS)
        ho_ref[...] = h_ref[...] + y * r * g_ref[...]

    return _call(
        body, grid=(m // tm,), name=name, carry=carry,
        in_specs=[pl.BlockSpec((tm, k), lambda i: (i, 0)), pl.BlockSpec((k, d), lambda i: (0, 0)),
                  pl.BlockSpec((tm, d), lambda i: (i, 0)), pl.BlockSpec((1, d), lambda i: (0, 0))],
        out_specs=[pl.BlockSpec((tm, d), lambda i: (i, 0)), pl.BlockSpec((tm, d), lambda i: (i, 0))],
        out_shape=[jax.ShapeDtypeStruct((m, d), F32), jax.ShapeDtypeStruct((m, d), F32)],
        args=(a, w, h, g))


def _mm_nt(a, w, *, tm, tn, out_dtype, name):
    m, k = a.shape
    n_out = w.shape[0]

    def body(a_ref, w_ref, o_ref):
        o_ref[...] = _dot_nt(a_ref[...], w_ref[...]).astype(out_dtype)

    return pl.pallas_call(
        body, grid=(n_out // tn, m // tm), name=name,
        in_specs=[pl.BlockSpec((tm, k), lambda j, i: (i, 0)), pl.BlockSpec((tn, k), lambda j, i: (j, 0))],
        out_specs=pl.BlockSpec((tm, tn), lambda j, i: (i, j)),
        out_shape=jax.ShapeDtypeStruct((m, n_out), out_dtype),
        compiler_params=_cp())(a, w)


def _mm_tn(x, dy, *, tk, tn, tm, name):
    m, k = x.shape
    n_out = dy.shape[1]

    def body(x_ref, d_ref, o_ref):
        _acc_out(o_ref, pl.program_id(2), _dot_tn(x_ref[...], d_ref[...]))

    return pl.pallas_call(
        body, grid=(k // tk, n_out // tn, m // tm), name=name,
        in_specs=[pl.BlockSpec((tm, tk), lambda a, b, c: (c, a)), pl.BlockSpec((tm, tn), lambda a, b, c: (c, b))],
        out_specs=pl.BlockSpec((tk, tn), lambda a, b, c: (a, b)),
        out_shape=jax.ShapeDtypeStruct((k, n_out), F32),
        compiler_params=_cp())(x, dy)


def _rms_bwd(x, g, dout, res, *, out_dtype, tm, name, carry=None):
    m, d = x.shape
    has_res = res is not None

    def body(*refs):
        if has_res:
            x_ref, g_ref, d_ref, r_ref, dx_ref, dg_ref = refs
        else:
            x_ref, g_ref, d_ref, dx_ref, dg_ref = refs
        xv = x_ref[...]
        dv = d_ref[...].astype(F32)
        r = lax.rsqrt(jnp.mean(xv * xv, axis=-1, keepdims=True) + EPS)
        xh = xv * r
        dxh = dv * g_ref[...]
        dx = r * (dxh - xh * jnp.mean(dxh * xh, axis=-1, keepdims=True))
        if has_res:
            dx = dx + r_ref[...]
        dx_ref[...] = dx.astype(out_dtype)
        _acc_out(dg_ref, pl.program_id(0), _rowsum8(dv * xh))

    row = pl.BlockSpec((tm, d), lambda i: (i, 0))
    ins = [row, pl.BlockSpec((1, d), lambda i: (0, 0)), row] + ([row] if has_res else [])
    args = (x, g, dout) + ((res,) if has_res else ())
    return _call(
        body, grid=(m // tm,), name=name, carry=carry, in_specs=ins,
        out_specs=[row, pl.BlockSpec((8, d), lambda i: (0, 0))],
        out_shape=[jax.ShapeDtypeStruct((m, d), out_dtype), jax.ShapeDtypeStruct((8, d), F32)], args=args)


def _loss_grad(h, tgt, *, tm, name):
    m, d = h.shape

    def body(h_ref, t_ref, dh_ref, p_ref):
        e = h_ref[...] - t_ref[...]
        dh_ref[...] = e / d
        _acc_out(p_ref, pl.program_id(0), _rowsum8(e * e))

    row = pl.BlockSpec((tm, d), lambda i: (i, 0))
    return pl.pallas_call(
        body, grid=(m // tm,), name=name, in_specs=[row, row],
        out_specs=[row, pl.BlockSpec((8, d), lambda i: (0, 0))],
        out_shape=[jax.ShapeDtypeStruct((m, d), F32), jax.ShapeDtypeStruct((8, d), F32)],
        compiler_params=_cp())(h, tgt)


def _adamw(w, g, m, v, *, name, carry=None):
    r, c = w.shape
    tr = _tile(r, 256)

    def body(w_ref, g_ref, m_ref, v_ref, d_ref, mo_ref, vo_ref):
        gv = g_ref[...]
        m2 = ADAM_B1 * m_ref[...] + (1.0 - ADAM_B1) * gv
        v2 = ADAM_B2 * v_ref[...] + (1.0 - ADAM_B2) * jnp.square(gv)
        m_hat = m2 / (1.0 - ADAM_B1 ** ADAM_STEP)
        v_hat = v2 / (1.0 - ADAM_B2 ** ADAM_STEP)
        d_ref[...] = -ADAM_LR * (m_hat / (jnp.sqrt(v_hat) + ADAM_EPS) + ADAM_WD * w_ref[...])
        mo_ref[...] = m2
        vo_ref[...] = v2

    blk = pl.BlockSpec((tr, c), lambda i: (i, 0))
    return _call(body, grid=(r // tr,), name=name, carry=carry, in_specs=[blk] * 4, out_specs=[blk] * 3,
                 out_shape=[jax.ShapeDtypeStruct((r, c), F32)] * 3, args=(w, g, m, v))


def _head_masks():
    lane = lax.broadcasted_iota(jnp.int32, (BLK, LANES), 1)
    row = lax.broadcasted_iota(jnp.int32, (BLK, LANES), 0)
    return lane, row, lane < HD


def _sb_scores(q_a, k, before):
    z = _dot_nt(q_a, k)
    sp = jnp.log1p(jnp.exp(-jnp.abs(z)))
    ls_pos = jnp.minimum(z, 0.0) - sp
    lkeep = jnp.where(before, ls_pos - z, 0.0)
    return ls_pos, lkeep


SB_DEAD = -104.0


def _sb_alive(jj, i, carry):
    return jnp.logical_and(jj <= i, jnp.max(carry) > SB_DEAD)


SB_QB_FWD = 2
SB_QB = 2


def _sb_before(jj, qb=SB_QB):
    lane = lax.broadcasted_iota(jnp.int32, (qb * 2 * BLK, LANES), 1)
    row = lax.broadcasted_iota(jnp.int32, (qb * 2 * BLK, LANES), 0)
    below_diag = jj - (qb - 1) + row // (2 * BLK)
    return jnp.logical_or(below_diag > 0, jnp.logical_and(below_diag == 0, lane < row % BLK))


def _sb_stack(x, lane_h, qb=SB_QB):
    return jnp.concatenate([_stack_heads(x[b * BLK:(b + 1) * BLK], lane_h) for b in range(qb)], axis=0)


def _sb_unstack(x, lane_h, qb=SB_QB):
    return jnp.concatenate([jnp.where(lane_h, x[2 * b * BLK:(2 * b + 1) * BLK], x[(2 * b + 1) * BLK:(2 * b + 2) * BLK])
                            for b in range(qb)], axis=0)


SB_ROWS = SB_QB * 2 * BLK


def _sb_fwd(u, *, name, carry=None):
    s_len = u.shape[0]
    qb = SB_QB_FWD
    qrows, rows = qb * BLK, qb * 2 * BLK

    def body(q_ref, k_ref, v_ref, o_ref):
        top = pl.program_id(0) * qb + qb - 1
        lane, row, lane_h = _head_masks()
        suffix = (row > lane).astype(BF16)
        pairs = [slice(hp * LANES, (hp + 1) * LANES) for hp in range(2)]
        qs = [_sb_stack(q_ref[:, cs] * 0.125, lane_h, qb) for cs in pairs]

        def step(state):
            jj, ccs, accs = state[0], state[1:3], state[3:5]
            rows_k = pl.ds(pl.multiple_of((top - jj) * BLK, BLK), BLK)
            before = _sb_before(jj, qb)
            scores = [_sb_scores(q, k_ref[rows_k, cs].astype(BF16), before) for q, cs in zip(qs, pairs)]
            between = [_dot_hilo(lkeep, suffix) + cc for (_, lkeep), cc in zip(scores, ccs)]
            atts = [jnp.where(before, jnp.exp(ls_pos + b), 0.0).astype(BF16) for (ls_pos, _), b in zip(scores, between)]
            new_cc = [cc + jnp.sum(lkeep, axis=1, keepdims=True) for (_, lkeep), cc in zip(scores, ccs)]
            new_acc = [acc + _dot(a, v_ref[rows_k, cs].astype(BF16)) for a, acc, cs in zip(atts, accs, pairs)]
            return (jj + 1, *new_cc, *new_acc)

        zc, za = jnp.zeros((rows, 1), F32), jnp.zeros((rows, LANES), F32)
        res = lax.while_loop(lambda st: _sb_alive(st[0], top, jnp.maximum(st[1], st[2])), step,
                             (jnp.int32(0), zc, zc, za, za))
        for hp, cs in enumerate(pairs):
            o_ref[:, cs] = _sb_unstack(res[3 + hp], lane_h, qb).astype(BF16)

    wide = 2 * LANES
    return _call(
        body, grid=(s_len // qrows,), name=name, carry=carry,
        in_specs=[pl.BlockSpec((qrows, wide), lambda i: (i, 0)), pl.BlockSpec((s_len, wide), lambda i: (0, 1)),
                  pl.BlockSpec((s_len, wide), lambda i: (0, 2))],
        out_specs=[pl.BlockSpec((qrows, wide), lambda i: (i, 0))],
        out_shape=[jax.ShapeDtypeStruct((s_len, SB_W), BF16)], args=(u, u, u))


def _sb_bwd(u, dcat, *, name, carry=None):
    s_len = u.shape[0]
    nq = s_len // BLK
    qrows = SB_QB * BLK

    def body(q_ref, k_ref, v_ref, do_ref, dq_ref, dk_ref, dv_ref, g_scr, b_scr):
        step = pl.program_id(1)
        top = step * SB_QB + SB_QB - 1
        lane, row, lane_h = _head_masks()
        suffix = (row > lane).astype(BF16)
        prefix = (row < lane).astype(BF16)
        qf = q_ref[...]
        qs = _sb_stack(qf * 0.125, lane_h)
        qu = _sb_stack(qf, lane_h)
        dos = _sb_stack(do_ref[...], lane_h)

        @pl.when(step == 0)
        def _():
            dk_ref[...] = jnp.zeros_like(dk_ref)
            dv_ref[...] = jnp.zeros_like(dv_ref)

        def down(state):
            jj, cc = state
            j = top - jj
            off = pl.multiple_of(j * BLK, BLK)
            k = k_ref[pl.ds(off, BLK), :].astype(BF16)
            v = v_ref[pl.ds(off, BLK), :].astype(BF16)
            before = _sb_before(jj)
            ls_pos, lkeep = _sb_scores(qs, k, before)
            between = _dot_hilo(lkeep, suffix) + cc
            att = jnp.where(before, jnp.exp(ls_pos + between), 0.0)
            g_scr[j] = att * _dot_nt(dos, v)
            b_scr[j] = jnp.exp(ls_pos)
            dv_ref[pl.ds(off, BLK), :] += _dot_tn(att.astype(BF16), dos)
            return jj + 1, cc + jnp.sum(lkeep, axis=1, keepdims=True)

        zc = jnp.zeros((SB_ROWS, 1), F32)
        visited = lax.while_loop(lambda st: _sb_alive(st[0], top, st[1]), down, (jnp.int32(0), zc))[0]

        def up(j, carry):
            pc, dq = carry
            off = pl.multiple_of(j * BLK, BLK)
            k = k_ref[pl.ds(off, BLK), :].astype(BF16)
            g, beta = g_scr[j], b_scr[j]
            below = _dot_hilo(g, prefix) + pc
            dz = (jnp.where(_sb_before(top - j), g * (1.0 - beta) - beta * below, 0.0) * 0.125).astype(BF16)
            dk_ref[pl.ds(off, BLK), :] += _dot_tn(dz, qu)
            return pc + jnp.sum(g, axis=1, keepdims=True), dq + _dot(dz, k)

        dq = lax.fori_loop(top + 1 - visited, top + 1, up, (zc, jnp.zeros((SB_ROWS, LANES), F32)))[1]
        dq_ref[...] = _sb_unstack(dq, lane_h)

    col = lambda c0: pl.BlockSpec((s_len, LANES), lambda hp, i: (0, c0 + hp))
    blk = pl.BlockSpec((qrows, LANES), lambda hp, i: (i, hp))
    acc = pl.BlockSpec((s_len, LANES), lambda hp, i: (0, hp))
    return _call(
        body, grid=(2, s_len // qrows), name=name, carry=carry, in_specs=[blk, col(2), col(4), blk],
        out_specs=[blk, acc, acc], out_shape=[jax.ShapeDtypeStruct((s_len, SB_W), F32)] * 3,
        scratch_shapes=[pltpu.VMEM((nq, SB_ROWS, LANES), F32), pltpu.VMEM((nq, SB_ROWS, LANES), F32)],
        vmem_mb=56, args=(u, u, u, dcat))


CV_T = 512
CV_H = 32


def _cv_specs(s_len):
    cur = lambda c: pl.BlockSpec((CV_T, CV_W), lambda i: (i, c))
    prev = lambda c: pl.BlockSpec((CV_H, CV_W), lambda i: (jnp.maximum(i * (CV_T // CV_H) - 1, 0), c))
    nxt = lambda c: pl.BlockSpec((CV_H, CV_W),
                                 lambda i: (jnp.minimum((i + 1) * (CV_T // CV_H), s_len // CV_H - 1), c))
    full = lambda r: pl.BlockSpec((r, CV_W), lambda i: (0, 0))
    return cur, prev, nxt, full


def _glu_into(gp_ref, val_ref, gate_ref, valp_ref, gatep_ref, i):
    gp_ref[0:CV_H, :] = jnp.where(i > 0, valp_ref[...] * jax.nn.sigmoid(gatep_ref[...]), 0.0)
    gp_ref[CV_H:, :] = val_ref[...] * jax.nn.sigmoid(gate_ref[...])


def _cv_fwd(u, cv_w, cv_b, ln_g, ln_b, pw_w, pw_b, *, name):
    s_len = u.shape[0]
    cur, prev, _, full = _cv_specs(s_len)

    def body(val_ref, gate_ref, valp_ref, gatep_ref, w_ref, b_ref, g_ref, be_ref, pw_ref, pb_ref,
             o_ref, c_ref, gp_ref):
        _glu_into(gp_ref, val_ref, gate_ref, valp_ref, gatep_ref, pl.program_id(0))
        acc = jnp.zeros((CV_T, CV_W), F32) + b_ref[...]
        for k in range(CV_K):
            acc = acc + w_ref[k:k + 1, :] * gp_ref[pl.ds(CV_H - CV_K + 1 + k, CV_T), :]
        c_ref[...] = acc
        mu = jnp.mean(acc, axis=-1, keepdims=True)
        xc = acc - mu
        xh = xc * lax.rsqrt(jnp.mean(xc * xc, axis=-1, keepdims=True) + EPS)
        a = xh * g_ref[...] + be_ref[...]
        s = a * jax.nn.sigmoid(a)
        o_ref[...] = (_dot(s.astype(BF16), pw_ref[...]) + pb_ref[...]).astype(BF16)

    return pl.pallas_call(
        body, grid=(s_len // CV_T,), name=name,
        in_specs=[cur(3), cur(4), prev(3), prev(4), full(CV_K), full(1), full(1), full(1), full(CV_W), full(1)],
        out_specs=[cur(0), cur(0)],
        out_shape=[jax.ShapeDtypeStruct((s_len, CV_W), BF16), jax.ShapeDtypeStruct((s_len, CV_W), F32)],
        scratch_shapes=[pltpu.VMEM((CV_T + CV_H, CV_W), F32)], compiler_params=_cp())(
            u, u, u, u, cv_w, cv_b, ln_g, ln_b, pw_w, pw_b)


def _cv_bwd_local(c, dcat, ln_g, ln_b, pw_w, *, name):
    s_len = c.shape[0]
    cur, _, _, full = _cv_specs(s_len)

    def body(c_ref, db_ref, g_ref, be_ref, pw_ref, dc_ref, dpw_ref, vec_ref):
        i = pl.program_id(0)
        cv = c_ref[...]
        db = db_ref[...]
        mu = jnp.mean(cv, axis=-1, keepdims=True)
        xc = cv - mu
        rstd = lax.rsqrt(jnp.mean(xc * xc, axis=-1, keepdims=True) + EPS)
        xh = xc * rstd
        a = xh * g_ref[...] + be_ref[...]
        sg = jax.nn.sigmoid(a)
        s = a * sg
        dbb = db.astype(BF16)
        ds = _dot_nt(dbb, pw_ref[...])
        da = ds * (sg * (1.0 + a * (1.0 - sg)))
        dxh = da * g_ref[...]
        dc_ref[...] = rstd * (dxh - jnp.mean(dxh, axis=-1, keepdims=True)
                              - xh * jnp.mean(dxh * xh, axis=-1, keepdims=True))
        _acc_out(dpw_ref, i, _dot_tn(s.astype(BF16), dbb))
        _acc_out(vec_ref, i, jnp.concatenate([_rowsum8(db), _rowsum8(da * xh), _rowsum8(da)], axis=0))

    return pl.pallas_call(
        body, grid=(s_len // CV_T,), name=name,
        in_specs=[cur(0), cur(1), full(1), full(1), full(CV_W)],
        out_specs=[cur(0), full(CV_W), full(24)],
        out_shape=[jax.ShapeDtypeStruct((s_len, CV_W), F32), jax.ShapeDtypeStruct((CV_W, CV_W), F32),
                   jax.ShapeDtypeStruct((24, CV_W), F32)], compiler_params=_cp())(c, dcat, ln_g, ln_b, pw_w)


def _cv_bwd_conv(u, dc, cv_w, *, name):
    s_len = u.shape[0]
    cur, prev, nxt, full = _cv_specs(s_len)
    last = s_len // CV_T - 1

    def body(val_ref, gate_ref, valp_ref, gatep_ref, dc_ref, dcn_ref, w_ref, du_ref, dw_ref, dbias_ref,
             gp_ref, dcp_ref):
        i = pl.program_id(0)
        _glu_into(gp_ref, val_ref, gate_ref, valp_ref, gatep_ref, i)
        dcv = dc_ref[...]
        dcp_ref[0:CV_T, :] = dcv
        dcp_ref[CV_T:, :] = jnp.where(i < last, dcn_ref[...], 0.0)
        dg = jnp.zeros((CV_T, CV_W), F32)
        parts = []
        for k in range(CV_K):
            dg = dg + w_ref[k:k + 1, :] * dcp_ref[pl.ds(CV_K - 1 - k, CV_T), :]
            parts.append(_rowsum8(dcv * gp_ref[pl.ds(CV_H - CV_K + 1 + k, CV_T), :]))
        _acc_out(dw_ref, i, jnp.concatenate(parts, axis=0))
        _acc_out(dbias_ref, i, _rowsum8(dcv))
        val = val_ref[...]
        sg = jax.nn.sigmoid(gate_ref[...])
        du_ref[:, 0:CV_W] = (dg * sg).astype(BF16)
        du_ref[:, CV_W:] = (dg * val * sg * (1.0 - sg)).astype(BF16)

    return pl.pallas_call(
        body, grid=(s_len // CV_T,), name=name,
        in_specs=[cur(3), cur(4), prev(3), prev(4), cur(0), nxt(0), full(CV_K)],
        out_specs=[pl.BlockSpec((CV_T, 2 * CV_W), lambda i: (i, 0)), full(CV_K * 8), full(8)],
        out_shape=[jax.ShapeDtypeStruct((s_len, 2 * CV_W), BF16), jax.ShapeDtypeStruct((CV_K * 8, CV_W), F32),
                   jax.ShapeDtypeStruct((8, CV_W), F32)],
        scratch_shapes=[pltpu.VMEM((CV_T + CV_H, CV_W), F32), pltpu.VMEM((CV_T + CV_H, CV_W), F32)],
        compiler_params=_cp())(u, u, u, u, dc, dc, cv_w)


def _rope_tables(pos_col, inv_freq_row, *, name):
    s_len = pos_col.shape[0]

    def body(p_ref, f_ref, cos_ref, sin_ref):
        ang = p_ref[...].astype(F32) * f_ref[...]
        lane = lax.broadcasted_iota(jnp.int32, (s_len, LANES), 1)
        sn = jnp.sin(ang)
        cos_ref[...] = jnp.cos(ang)
        sin_ref[...] = jnp.where(lane % HD < HD // 2, -sn, sn)

    return pl.pallas_call(body, name=name, out_shape=[jax.ShapeDtypeStruct((s_len, LANES), F32)] * 2,
                          compiler_params=_cp())(pos_col, inv_freq_row)


def _rot_half(x):
    lane = lax.broadcasted_iota(jnp.int32, x.shape, 1)
    return jnp.where(lane % HD < HD // 2, pltpu.roll(x, LANES - HD // 2, 1), pltpu.roll(x, HD // 2, 1))


def _permute_rows(dst_ref, src_ref, d, dtype):
    s_len = src_ref.shape[0]
    seg = s_len // d
    if d == 1:
        dst_ref[...] = src_ref[...].astype(dtype)
        return
    for r in range(d):
        dst_ref[r * seg:(r + 1) * seg, :] = src_ref[pl.ds(r, seg, stride=d), :].astype(dtype)


def _unpermute_rows(dst_ref, src_ref, d):
    s_len = src_ref.shape[0]
    seg = s_len // d
    if d == 1:
        dst_ref[...] = src_ref[...]
        return
    for r in range(d):
        dst_ref[pl.ds(r, seg, stride=d), :] = src_ref[r * seg:(r + 1) * seg, :]


def _rope_perm(u, cos, sin, *, name):
    s_len = u.shape[0]

    def body(x_ref, cos_ref, sin_ref, o_ref, scr):
        a = pl.program_id(0)
        x = x_ref[...]
        rot = a < 2
        scr[...] = x * jnp.where(rot, cos_ref[...], 1.0) + _rot_half(x) * jnp.where(rot, sin_ref[...], 0.0)
        for n, d in enumerate(DILATIONS):
            _permute_rows(o_ref.at[n], scr, d, BF16)

    tab = pl.BlockSpec((s_len, LANES), lambda a, cb: (0, 0))
    return pl.pallas_call(
        body, grid=(3, 4), name=name,
        in_specs=[pl.BlockSpec((s_len, LANES), lambda a, cb: (0, 10 + 4 * a + cb)), tab, tab],
        out_specs=pl.BlockSpec((None, 3, s_len, LANES), lambda a, cb: (a, 0, 0, cb)),
        out_shape=jax.ShapeDtypeStruct((3, 3, s_len, DL_W), BF16),
        scratch_shapes=[pltpu.VMEM((s_len, LANES), F32)], compiler_params=_cp())(u, cos, sin)


DL_UNROLL = 4


def _dl_band(rows):
    lane = lax.broadcasted_iota(jnp.int32, (rows, LANES), 1)
    row = lax.broadcasted_iota(jnp.int32, (rows, LANES), 0) % BLK
    return lane <= row, lane >= row


def _dl_first(s_len, n, i):
    nb = jnp.where(n == 0, s_len // BLK, jnp.where(n == 1, s_len // (BLK * DILATIONS[1]),
                                                   s_len // (BLK * DILATIONS[2])))
    return lax.rem(i, nb) == 0


def _stack_heads(x, lane_h):
    return jnp.concatenate([jnp.where(lane_h, x, 0.0), jnp.where(lane_h, 0.0, x)], axis=0).astype(BF16)


def _dl_rows(i):
    cur = pl.ds(pl.multiple_of(i * BLK, BLK), BLK)
    prev = pl.ds(pl.multiple_of(jnp.maximum(i - 1, 0) * BLK, BLK), BLK)
    return cur, prev


def _dl_in_specs(s_len):
    return [pl.BlockSpec((None, None, s_len, LANES), functools.partial(lambda a, n, hp: (a, n, 0, hp), a))
            for a in range(3)]


def _dl_fwd(qkv, *, name, carry=None):
    s_len = qkv.shape[2]

    def body(q_ref, k_ref, v_ref, o_ref, l_ref):
        n = pl.program_id(0)
        lane_h = _head_masks()[2]
        band_c, band_p = _dl_band(2 * BLK)
        ones = jnp.ones((BLK, LANES), BF16)

        @pl.loop(0, s_len // BLK, step=DL_UNROLL)
        def _(i0):
            blocks = [i0 + t for t in range(DL_UNROLL)]
            rows = [_dl_rows(i) for i in blocks]
            scores = []
            for cur, prev in rows:
                qs = _stack_heads(q_ref[cur, :] * 0.125, lane_h)
                scores.append((_dot_nt(qs, k_ref[cur, :]), _dot_nt(qs, k_ref[prev, :])))
            probs = []
            for i, (sc, sp) in zip(blocks, scores):
                sc = jnp.where(band_c, sc, NEG_INF)
                sp = jnp.where(jnp.logical_and(band_p, jnp.logical_not(_dl_first(s_len, n, i))), sp, NEG_INF)
                m = jnp.max(jnp.maximum(sc, sp), axis=1, keepdims=True)
                probs.append((jnp.exp(sc - m).astype(BF16), jnp.exp(sp - m).astype(BF16), m))
            for (cur, prev), (pc, pp, m) in zip(rows, probs):
                r = (_dot(pc, jnp.concatenate([v_ref[cur, :], ones], axis=1))
                     + _dot(pp, jnp.concatenate([v_ref[prev, :], ones], axis=1)))
                den = jnp.where(lane_h, r[:BLK, LANES:], r[BLK:, LANES:])
                o_ref[cur, :] = jnp.where(lane_h, r[:BLK, :LANES], r[BLK:, :LANES]) / den
                l_ref[cur, :] = jnp.where(lane_h, m[:BLK], m[BLK:]) + jnp.log(den)

    out = pl.BlockSpec((None, s_len, LANES), lambda n, hp: (n, 0, hp))
    return _call(
        body, grid=(3, 4), name=name, carry=carry, in_specs=_dl_in_specs(s_len), out_specs=[out, out],
        out_shape=[jax.ShapeDtypeStruct((3, s_len, DL_W), F32)] * 2, args=(qkv, qkv, qkv))


def _dl_mix(o_p, l_p, *, name, carry=None):
    s_len = o_p.shape[1]

    def body(o_ref, l_ref, ob_ref, of_ref, lt_ref, o_scr, l_scr):
        n = pl.program_id(1)
        for k, d in enumerate(DILATIONS):
            @pl.when(n == k)
            def _(k=k, d=d):
                _unpermute_rows(o_scr.at[k], o_ref, d)
                _unpermute_rows(l_scr.at[k], l_ref, d)

        @pl.when(n == 2)
        def _():
            l0, l1, l2 = l_scr[0], l_scr[1], l_scr[2]
            m = jnp.maximum(jnp.maximum(l0, l1), l2)
            e0, e1, e2 = jnp.exp(l0 - m), jnp.exp(l1 - m), jnp.exp(l2 - m)
            den = e0 + e1 + e2
            o = (e0 / den) * o_scr[0] + (e1 / den) * o_scr[1] + (e2 / den) * o_scr[2]
            of_ref[...] = o
            ob_ref[...] = o.astype(BF16)
            lt_ref[...] = m + jnp.log(den)

    inb = pl.BlockSpec((None, s_len, LANES), lambda cb, n: (n, 0, cb))
    outb = pl.BlockSpec((s_len, LANES), lambda cb, n: (0, cb))
    return _call(
        body, grid=(4, 3), name=name, carry=carry, in_specs=[inb, inb], out_specs=[outb, outb, outb],
        out_shape=[jax.ShapeDtypeStruct((s_len, DL_W), BF16), jax.ShapeDtypeStruct((s_len, DL_W), F32),
                   jax.ShapeDtypeStruct((s_len, DL_W), F32)],
        scratch_shapes=[pltpu.VMEM((3, s_len, LANES), F32), pltpu.VMEM((3, s_len, LANES), F32)], args=(o_p, l_p))


def _dl_bwd_prep(dcat, o, lse, *, name):
    s_len = o.shape[0]

    def body(do_ref, o_ref, l_ref, dop_ref, st_ref, d_scr):
        n = pl.program_id(1)

        @pl.when(n == 0)
        def _():
            r0 = lax.broadcasted_iota(jnp.int32, (LANES, LANES), 0) // HD
            r1 = lax.broadcasted_iota(jnp.int32, (LANES, LANES), 1) // HD
            d_scr[...] = _dot_hilo(do_ref[...] * o_ref[...], (r0 == r1).astype(BF16))

        for k, d in enumerate(DILATIONS):
            @pl.when(n == k)
            def _(d=d):
                _permute_rows(dop_ref, do_ref, d, BF16)
                _permute_rows(st_ref.at[0], d_scr, d, F32)
                _permute_rows(st_ref.at[1], l_ref, d, F32)

    nat = lambda c0: pl.BlockSpec((s_len, LANES), lambda cb, n: (0, c0 + cb))
    return pl.pallas_call(
        body, grid=(4, 3), name=name, in_specs=[nat(4), nat(0), nat(0)],
        out_specs=[pl.BlockSpec((None, s_len, LANES), lambda cb, n: (n, 0, cb)),
                   pl.BlockSpec((2, None, s_len, LANES), lambda cb, n: (0, n, 0, cb))],
        out_shape=[jax.ShapeDtypeStruct((3, s_len, DL_W), BF16), jax.ShapeDtypeStruct((2, 3, s_len, DL_W), F32)],
        scratch_shapes=[pltpu.VMEM((s_len, LANES), F32)], compiler_params=_cp())(dcat, o, lse)


def _dl_bwd(qkv, dop, stats, *, name, carry=None):
    s_len = qkv.shape[2]

    def body(q_ref, k_ref, v_ref, do_ref, st_ref, cur_ref, prev_ref):
        n = pl.program_id(0)
        lane_h = _head_masks()[2]
        band_c, band_p = _dl_band(2 * BLK)

        def per_head(x):
            xr = pltpu.roll(x, HD, 1)
            return jnp.concatenate([jnp.where(lane_h, x, xr), jnp.where(lane_h, xr, x)], axis=0)

        @pl.loop(0, s_len // BLK, step=DL_UNROLL)
        def _(i0):
            blocks = [i0 + t for t in range(DL_UNROLL)]
            rows = [_dl_rows(i) for i in blocks]
            stage1 = []
            for cur, prev in rows:
                qs = _stack_heads(q_ref[cur, :] * 0.125, lane_h)
                dos = _stack_heads(do_ref[cur, :], lane_h)
                kc, kp, vc, vp = k_ref[cur, :], k_ref[prev, :], v_ref[cur, :], v_ref[prev, :]
                stage1.append((qs, dos, _dot_nt(qs, kc), _dot_nt(qs, kp), _dot_nt(dos, vc), _dot_nt(dos, vp)))
            stage2 = []
            for i, (cur, prev), (qs, dos, sc, sp, dpc, dpp) in zip(blocks, rows, stage1):
                lse, delta = per_head(st_ref[1, cur, :]), per_head(st_ref[0, cur, :])
                pc = jnp.where(band_c, jnp.exp(sc - lse), 0.0)
                pp = jnp.where(jnp.logical_and(band_p, jnp.logical_not(_dl_first(s_len, n, i))), jnp.exp(sp - lse), 0.0)
                stage2.append((pc.astype(BF16), pp.astype(BF16), (pc * (dpc - delta)).astype(BF16),
                               (pp * (dpp - delta)).astype(BF16)))
            for (cur, prev), (qs, dos, *_), (pc, pp, dsc, dsp) in zip(rows, stage1, stage2):
                dq = _dot(dsc, k_ref[cur, :]) + _dot(dsp, k_ref[prev, :])
                cur_ref[0, cur, :] = jnp.where(lane_h, dq[:BLK], dq[BLK:]) * 0.125
                cur_ref[1, cur, :] = _dot_tn(dsc, qs)
                cur_ref[2, cur, :] = _dot_tn(pc, dos)
                prev_ref[0, cur, :] = _dot_tn(dsp, qs)
                prev_ref[1, cur, :] = _dot_tn(pp, dos)

    return _call(
        body, grid=(3, 4), name=name, carry=carry,
        in_specs=_dl_in_specs(s_len) + [pl.BlockSpec((None, s_len, LANES), lambda n, hp: (n, 0, hp)),
                                        pl.BlockSpec((2, None, s_len, LANES), lambda n, hp: (0, n, 0, hp))],
        out_specs=[pl.BlockSpec((3, None, s_len, LANES), lambda n, hp: (0, n, 0, hp)),
                   pl.BlockSpec((2, None, s_len, LANES), lambda n, hp: (0, n, 0, hp))],
        out_shape=[jax.ShapeDtypeStruct((3, 3, s_len, DL_W), F32), jax.ShapeDtypeStruct((2, 3, s_len, DL_W), F32)],
        vmem_mb=56, args=(qkv, qkv, qkv, dop, stats))


def _dl_bwd_finish(cur, prev, cos, sin, *, name):
    s_len = cur.shape[2]

    def body(c_ref, p_ref, cos_ref, sin_ref, o_ref, p_scr, u_scr, acc):
        a, n = pl.program_id(0), pl.program_id(2)
        has_prev = jnp.where(a > 0, 1.0, 0.0)
        p_scr[...] = c_ref[...]
        p_scr[0:s_len - BLK, :] += has_prev * p_ref[BLK:, :]
        for k, d in enumerate(DILATIONS):
            @pl.when(n == k)
            def _(k=k, d=d):
                if k == 0:
                    acc[...] = p_scr[...]
                else:
                    _unpermute_rows(u_scr, p_scr, d)
                    acc[...] += u_scr[...]

        @pl.when(n == 2)
        def _():
            dy = acc[...]
            rot = a < 2
            o_ref[...] = (dy * jnp.where(rot, cos_ref[...], 1.0)
                          + _rot_half(dy * jnp.where(rot, sin_ref[...], 0.0))).astype(BF16)

    tab = pl.BlockSpec((s_len, LANES), lambda a, cb, n: (0, 0))
    return pl.pallas_call(
        body, grid=(3, 4, 3), name=name,
        in_specs=[pl.BlockSpec((None, None, s_len, LANES), lambda a, cb, n: (a, n, 0, cb)),
                  pl.BlockSpec((None, None, s_len, LANES), lambda a, cb, n: (jnp.maximum(a - 1, 0), n, 0, cb)),
                  tab, tab],
        out_specs=pl.BlockSpec((s_len, LANES), lambda a, cb, n: (0, 4 * a + cb)),
        out_shape=jax.ShapeDtypeStruct((s_len, 3 * DL_W), BF16),
        scratch_shapes=[pltpu.VMEM((s_len, LANES), F32)] * 3, compiler_params=_cp())(cur, prev, cos, sin)


XA_T = 256


def _xa_probs(q, k):
    s = _dot_nt(q, k) * (X_HD ** -0.5)
    e = jnp.exp(s - jnp.max(s, axis=1, keepdims=True))
    return e / jnp.sum(e, axis=1, keepdims=True)


def _xa_fwd(q, k, v, *, name):
    s_len, d = q.shape
    nm = k.shape[0]

    def body(q_ref, k_ref, v_ref, o_ref):
        for h in range(X_HEADS):
            cs = slice(h * X_HD, (h + 1) * X_HD)
            p = _xa_probs(q_ref[:, cs], k_ref[:, cs])
            o_ref[:, cs] = _dot(p.astype(BF16), v_ref[:, cs]).astype(BF16)

    row = pl.BlockSpec((XA_T, d), lambda i: (i, 0))
    full = pl.BlockSpec((nm, d), lambda i: (0, 0))
    return pl.pallas_call(body, grid=(s_len // XA_T,), name=name, in_specs=[row, full, full], out_specs=row,
                          out_shape=jax.ShapeDtypeStruct((s_len, d), BF16), compiler_params=_cp())(q, k, v)


def _xa_bwd(q, k, v, do, *, name, carry=None):
    s_len, d = q.shape
    nm = k.shape[0]

    def body(q_ref, k_ref, v_ref, do_ref, dq_ref, dk_ref, dv_ref):
        i = pl.program_id(0)
        for h in range(X_HEADS):
            cs = slice(h * X_HD, (h + 1) * X_HD)
            qh, kh, vh, doh = q_ref[:, cs], k_ref[:, cs], v_ref[:, cs], do_ref[:, cs]
            p = _xa_probs(qh, kh)
            dp = _dot_nt(doh, vh)
            ds = (p * (dp - jnp.sum(dp * p, axis=1, keepdims=True)) * (X_HD ** -0.5)).astype(BF16)
            dq_ref[:, cs] = _dot(ds, kh).astype(BF16)
            dkh, dvh = _dot_tn(ds, qh), _dot_tn(p.astype(BF16), doh)

            @pl.when(i == 0)
            def _(cs=cs, dkh=dkh, dvh=dvh):
                dk_ref[:, cs] = dkh
                dv_ref[:, cs] = dvh

            @pl.when(i > 0)
            def _(cs=cs, dkh=dkh, dvh=dvh):
                dk_ref[:, cs] += dkh
                dv_ref[:, cs] += dvh

    row = pl.BlockSpec((XA_T, d), lambda i: (i, 0))
    full = pl.BlockSpec((nm, d), lambda i: (0, 0))
    return _call(
        body, grid=(s_len // XA_T,), name=name, carry=carry, in_specs=[row, full, full, row],
        out_specs=[row, full, full],
        out_shape=[jax.ShapeDtypeStruct((s_len, d), BF16), jax.ShapeDtypeStruct((nm, d), F32),
                   jax.ShapeDtypeStruct((nm, d), F32)], args=(q, k, v, do))


FF_TM, FF_TN, FF_H = 512, 256, 8
GELU_K, GELU_C = 0.7978845608028654, 0.044715


FF_STRIP = 64


def _ff_conv(e_ref, w_ref, b_ref, rows, r0=0):
    return (w_ref[0:1, :] * e_ref[pl.ds(FF_H - 2 + r0, rows), :] + w_ref[1:2, :] * e_ref[pl.ds(FF_H - 1 + r0, rows), :]
            + w_ref[2:3, :] * e_ref[pl.ds(FF_H + r0, rows), :] + b_ref[...])


def _strips(total, size):
    return [(r0, min(size, total - r0)) for r0 in range(0, total, size)]


def _ff_gate_fwd(up, conv_w, conv_b, *, name, carry=None):
    s_len = up.shape[0]
    nj = D_FF // FF_TN

    def body(g_ref, v_ref, gp_ref, vp_ref, wg_ref, wv_ref, bg_ref, bv_ref, o_ref, eg, ev):
        i = pl.program_id(0)
        for e, cur, prev in ((eg, g_ref, gp_ref), (ev, v_ref, vp_ref)):
            e[0:FF_H, :] = jnp.where(i > 0, prev[...], 0.0)
            e[FF_H:, :] = cur[...]
        for r0, rows in _strips(FF_TM, FF_STRIP):
            gate = _ff_conv(eg, wg_ref, bg_ref, rows, r0)
            val = _ff_conv(ev, wv_ref, bv_ref, rows, r0)
            t = jnp.tanh(GELU_K * (gate + GELU_C * gate * gate * gate))
            o_ref[r0:r0 + rows, :] = (0.5 * gate * (1.0 + t) * val).astype(BF16)

    cur = lambda c0: pl.BlockSpec((FF_TM, FF_TN), lambda i, j: (i, c0 + j))
    prev = lambda c0: pl.BlockSpec((FF_H, FF_TN), lambda i, j: (jnp.maximum(i * (FF_TM // FF_H) - 1, 0), c0 + j))
    par = lambda r, c0: pl.BlockSpec((r, FF_TN), lambda i, j: (0, c0 + j))
    return _call(
        body, grid=(s_len // FF_TM, nj), name=name, carry=carry,
        in_specs=[cur(0), cur(nj), prev(0), prev(nj), par(3, 0), par(3, nj), par(1, 0), par(1, nj)],
        out_specs=[cur(0)], out_shape=[jax.ShapeDtypeStruct((s_len, D_FF), BF16)],
        scratch_shapes=[pltpu.VMEM((FF_TM + FF_H, FF_TN), F32)] * 2,
        args=(up, up, up, up, conv_w, conv_w, conv_b, conv_b))


def _ff_gate_bwd(up, dact, conv_w, conv_b, *, name, carry=None):
    s_len = up.shape[0]
    nj = D_FF // FF_TN
    last = s_len // FF_TM - 1
    ext = FF_TM + FF_H

    def body(g_ref, v_ref, gp_ref, vp_ref, gn_ref, vn_ref, da_ref, dan_ref, wg_ref, wv_ref, bg_ref, bv_ref,
             dg_ref, dv_ref, dw_ref, db_ref, eg, ev, sg, sv):
        i = pl.program_id(1)
        for e, cur, prev, nxt in ((eg, g_ref, gp_ref, gn_ref), (ev, v_ref, vp_ref, vn_ref)):
            e[0:FF_H, :] = jnp.where(i > 0, prev[...], 0.0)
            e[FF_H:FF_H + FF_TM, :] = cur[...]
            e[FF_H + FF_TM:, :] = nxt[...]
        for r0, rows in _strips(ext, FF_STRIP):
            gate = _ff_conv(eg, wg_ref, bg_ref, rows, r0)
            val = _ff_conv(ev, wv_ref, bv_ref, rows, r0)
            dact = da_ref[r0:r0 + rows, :] if r0 < FF_TM else jnp.where(i < last, dan_ref[...], 0.0)
            t = jnp.tanh(GELU_K * (gate + GELU_C * gate * gate * gate))
            half = 0.5 * (1.0 + t)
            dgelu = half + 0.5 * gate * (1.0 - t * t) * GELU_K * (1.0 + 3.0 * GELU_C * gate * gate)
            sg[r0:r0 + rows, :] = dact * val * dgelu
            sv[r0:r0 + rows, :] = dact * (gate * half)
        for part, (s, e, w_ref, out) in enumerate(((sg, eg, wg_ref, dg_ref), (sv, ev, wv_ref, dv_ref))):
            taps, bias = [jnp.zeros((8, FF_TN), F32)] * 3, jnp.zeros((8, FF_TN), F32)
            for r0, rows in _strips(FF_TM, FF_STRIP):
                d0 = s[pl.ds(r0, rows), :]
                out[r0:r0 + rows, :] = (w_ref[2:3, :] * d0 + w_ref[1:2, :] * s[pl.ds(r0 + 1, rows), :]
                                        + w_ref[0:1, :] * s[pl.ds(r0 + 2, rows), :]).astype(BF16)
                taps = [taps[k] + _rowsum8(d0 * e[pl.ds(FF_H - 2 + k + r0, rows), :]) for k in range(3)]
                bias = bias + _rowsum8(d0)
            _acc_out(dw_ref.at[part], i, jnp.concatenate(taps, axis=0))
            _acc_out(db_ref.at[part], i, bias)

    cur = lambda c0: pl.BlockSpec((FF_TM, FF_TN), lambda j, i: (i, c0 + j))
    prev = lambda c0: pl.BlockSpec((FF_H, FF_TN), lambda j, i: (jnp.maximum(i * (FF_TM // FF_H) - 1, 0), c0 + j))
    nxt = lambda c0: pl.BlockSpec(
        (FF_H, FF_TN), lambda j, i: (jnp.minimum((i + 1) * (FF_TM // FF_H), s_len // FF_H - 1), c0 + j))
    par = lambda r, c0: pl.BlockSpec((r, FF_TN), lambda j, i: (0, c0 + j))
    return _call(
        body, grid=(nj, s_len // FF_TM), name=name, carry=carry,
        in_specs=[cur(0), cur(nj), prev(0), prev(nj), nxt(0), nxt(nj), cur(0), nxt(0),
                  par(3, 0), par(3, nj), par(1, 0), par(1, nj)],
        out_specs=[cur(0), cur(0), pl.BlockSpec((2, 24, FF_TN), lambda j, i: (0, 0, j)),
                   pl.BlockSpec((2, 8, FF_TN), lambda j, i: (0, 0, j))],
        out_shape=[jax.ShapeDtypeStruct((s_len, D_FF), BF16), jax.ShapeDtypeStruct((s_len, D_FF), BF16),
                   jax.ShapeDtypeStruct((2, 24, D_FF), F32), jax.ShapeDtypeStruct((2, 8, D_FF), F32)],
        scratch_shapes=[pltpu.VMEM((FF_TM + 2 * FF_H, FF_TN), F32)] * 2 + [pltpu.VMEM((ext, FF_TN), F32)] * 2,
        args=(up, up, up, up, up, up, dact, dact, conv_w, conv_w, conv_b, conv_b))


def _place():
    x, y, c = lax.axis_index("x"), lax.axis_index("y"), lax.axis_index("c")
    return x, y, c, [(1 - x, y), (x, 1 - y), (1 - x, 1 - y)]


def _remote(src, dst, send_sem, recv_sem, dev):
    return pltpu.make_async_remote_copy(src_ref=src, dst_ref=dst, send_sem=send_sem, recv_sem=recv_sem,
                                        device_id=dev, device_id_type=MESH)


_ANY = pl.BlockSpec(memory_space=pl.ANY)


N_SEMS = 8
SEM_BASE_2 = 4


class _Exchange:
    def __init__(self, operands, out_shapes, start, wait, aliases=None):
        self.operands, self.out_shapes, self.start, self.wait = list(operands), list(out_shapes), start, wait
        self.aliases = aliases or {}


def _sem_scratch():
    return [pltpu.SemaphoreType.DMA((N_SEMS,)), pltpu.SemaphoreType.DMA((N_SEMS,)), pltpu.SemaphoreType.DMA]


def _run_exchange(ex, *, name):
    k, n = len(ex.operands), len(ex.out_shapes)

    def body(*refs):
        ins, outs, sems = refs[:k], refs[k:k + n], refs[k + n:]
        ex.start(ins, outs, *sems)
        ex.wait(ins, outs, *sems)

    return pl.pallas_call(body, name=name, in_specs=[_ANY] * k, out_specs=[_ANY] * n, out_shape=ex.out_shapes,
                          scratch_shapes=_sem_scratch(), input_output_aliases=ex.aliases,
                          compiler_params=_cp(16))(*ex.operands)


def _call(body, *, grid, in_specs, out_specs, out_shape, args, name, scratch_shapes=(), vmem_mb=48, carry=None):
    scratch_shapes = list(scratch_shapes)
    if carry is None:
        return pl.pallas_call(body, grid=grid, name=name, in_specs=in_specs, out_specs=out_specs, out_shape=out_shape,
                              scratch_shapes=scratch_shapes, compiler_params=_cp(vmem_mb))(*args)
    n_in, n_out, n_scr = len(in_specs), len(out_shape), len(scratch_shapes)
    k_in, k_out = len(carry.operands), len(carry.out_shapes)

    def wrapped(*refs):
        ins, refs = refs[:n_in], refs[n_in:]
        cin, refs = refs[:k_in], refs[k_in:]
        outs, refs = refs[:n_out], refs[n_out:]
        cout, refs = refs[:k_out], refs[k_out:]
        scratch, sems = refs[:n_scr], refs[n_scr:]
        ids = [pl.program_id(a) for a in range(len(grid))]
        first = functools.reduce(jnp.logical_and, [i == 0 for i in ids])
        last = functools.reduce(jnp.logical_and, [i == g - 1 for i, g in zip(ids, grid)])

        @pl.when(first)
        def _():
            carry.start(cin, cout, *sems)

        body(*ins, *outs, *scratch)

        @pl.when(last)
        def _():
            carry.wait(cin, cout, *sems)

    aliases = {n_in + i: n_out + o for i, o in carry.aliases.items()}
    return pl.pallas_call(
        wrapped, grid=grid, name=name, in_specs=list(in_specs) + [_ANY] * k_in,
        out_specs=list(out_specs) + [_ANY] * k_out, out_shape=list(out_shape) + carry.out_shapes,
        scratch_shapes=scratch_shapes + _sem_scratch(), input_output_aliases=aliases,
        compiler_params=_cp(vmem_mb))(*args, *carry.operands)


def _half_rows(ref_rows, c):
    half = ref_rows // 2
    return pl.ds(c * half, half)


def _ex_join(a, b):
    ka, na = len(a.operands), len(a.out_shapes)

    def start(ins, outs, *sems):
        a.start(ins[:ka], outs[:na], *sems)
        b.start(ins[ka:], outs[na:], *sems)

    def wait(ins, outs, *sems):
        a.wait(ins[:ka], outs[:na], *sems)
        b.wait(ins[ka:], outs[na:], *sems)

    aliases = dict(a.aliases)
    aliases.update({ka + i: na + o for i, o in b.aliases.items()})
    return _Exchange(a.operands + b.operands, a.out_shapes + b.out_shapes, start, wait, aliases)


def _ex_gather(pack, r0, rl, base=0):
    def copies(ins, outs, send, recv):
        x, y, c, chips = _place()
        rows = _half_rows(rl, c)
        src = ins[0].at[pl.ds(r0 + c * (rl // 2), rl // 2)]
        sends = [_remote(src, outs[0].at[2 * x + y, rows], send.at[base + k], recv.at[base + k], (px, py, c))
                 for k, (px, py) in enumerate(chips)]
        lands = [_remote(src, outs[0].at[2 * px + py, rows], send.at[base + k], recv.at[base + k], (px, py, c))
                 for k, (px, py) in enumerate(chips)]
        return sends, lands

    def mine(ins, outs, local):
        x, y, _, _ = _place()
        return pltpu.make_async_copy(ins[0].at[pl.ds(r0, rl)], outs[0].at[2 * x + y], local)

    def start(ins, outs, send, recv, local):
        mine(ins, outs, local).start()
        for cp in copies(ins, outs, send, recv)[0]:
            cp.start()

    def wait(ins, outs, send, recv, local):
        sends, lands = copies(ins, outs, send, recv)
        for cp in lands:
            cp.wait_recv()
        for cp in sends:
            cp.wait_send()
        mine(ins, outs, local).wait()

    return _Exchange([pack], [jax.ShapeDtypeStruct((4, rl, pack.shape[1]), pack.dtype)], start, wait)


def _ex_gather_forward(g, base=0):
    rl = g.shape[1]

    def copies(outs, send, recv):
        x, y, c, chips = _place()
        slabs = [(outs[0].at[2 * px + py, _half_rows(rl, c)], outs[0].at[2 * px + py, _half_rows(rl, 1 - c)])
                 for px, py in chips]
        sends = [_remote(a, a, send.at[base + k], recv.at[base + k], (x, y, 1 - c)) for k, (a, _) in enumerate(slabs)]
        lands = [_remote(b, b, send.at[base + k], recv.at[base + k], (x, y, 1 - c)) for k, (_, b) in enumerate(slabs)]
        return sends, lands

    def start(ins, outs, send, recv, local):
        for cp in copies(outs, send, recv)[0]:
            cp.start()

    def wait(ins, outs, send, recv, local):
        sends, lands = copies(outs, send, recv)
        for cp in lands:
            cp.wait_recv()
        for cp in sends:
            cp.wait_send()

    return _Exchange([g], [jax.ShapeDtypeStruct(g.shape, g.dtype)], start, wait, aliases={0: 0})


def _ex_swap_halves(gw, base=0):
    nb, rl, d = gw.shape

    def copies(ins, outs, send, recv):
        x, y, c, _ = _place()
        return [_remote(ins[0].at[j, _half_rows(rl, 1 - c)], outs[0].at[j], send.at[base + j], recv.at[base + j],
                        (x, y, 1 - c)) for j in range(nb)]

    def start(ins, outs, send, recv, local):
        for cp in copies(ins, outs, send, recv):
            cp.start()

    def wait(ins, outs, send, recv, local):
        for cp in copies(ins, outs, send, recv):
            cp.wait()

    return _Exchange([gw], [jax.ShapeDtypeStruct((nb, rl // 2, d), gw.dtype)], start, wait)


def _chip_sum(gw, got, c_arr, *, name):
    nchip, half, d = got.shape
    tr = _tile(half, 512)

    def body(c_ref, a_ref, b_ref, o32_ref, o16_ref):
        s = a_ref[...] + b_ref[...]
        o32_ref[...] = s
        o16_ref[...] = s.astype(BF16)

    blk = pl.BlockSpec((None, tr, d), lambda j, i, c_ref: (j, i, 0))
    return pl.pallas_call(
        body, name=name,
        grid_spec=pltpu.PrefetchScalarGridSpec(
            num_scalar_prefetch=1, grid=(nchip, half // tr),
            in_specs=[pl.BlockSpec((None, tr, d), lambda j, i, c_ref: (j, c_ref[0] * (half // tr) + i, 0)), blk],
            out_specs=[blk, blk]),
        out_shape=[jax.ShapeDtypeStruct((nchip, half, d), F32), jax.ShapeDtypeStruct((nchip, half, d), BF16)],
        compiler_params=_cp())(c_arr, gw, got)


def _ex_scatter(s16, base=0):
    def copies(ins, outs, send, recv):
        x, y, c, chips = _place()
        return [_remote(ins[0].at[2 * px + py], outs[0].at[k], send.at[base + k], recv.at[base + k], (px, py, c))
                for k, (px, py) in enumerate(chips)]

    def start(ins, outs, send, recv, local):
        for cp in copies(ins, outs, send, recv):
            cp.start()

    def wait(ins, outs, send, recv, local):
        for cp in copies(ins, outs, send, recv):
            cp.wait()

    return _Exchange([s16], [jax.ShapeDtypeStruct((3,) + s16.shape[1:], s16.dtype)], start, wait)


def _mesh_sum(s32, got, j_arr, *, name):
    _, rl, d = s32.shape
    tr = _tile(rl, 512)

    def body(j_ref, a_ref, b_ref, o_ref):
        o_ref[...] = ((a_ref[...] + b_ref[0].astype(F32)) + b_ref[1].astype(F32)) + b_ref[2].astype(F32)

    return pl.pallas_call(
        body, name=name,
        grid_spec=pltpu.PrefetchScalarGridSpec(
            num_scalar_prefetch=1, grid=(rl // tr,),
            in_specs=[pl.BlockSpec((None, tr, d), lambda i, j_ref: (j_ref[0], i, 0)),
                      pl.BlockSpec((3, tr, d), lambda i, j_ref: (0, i, 0))],
            out_specs=pl.BlockSpec((tr, d), lambda i, j_ref: (i, 0))),
        out_shape=jax.ShapeDtypeStruct((rl, d), F32), compiler_params=_cp())(j_arr, s32, got)


def _ex_share_halves(ghalf):
    half, d = ghalf.shape

    def copies(ins, outs, send, recv, local):
        x, y, c, _ = _place()
        there = outs[0].at[_half_rows(2 * half, c)]
        back = outs[0].at[_half_rows(2 * half, 1 - c)]
        return (_remote(ins[0], there, send.at[0], recv.at[0], (x, y, 1 - c)),
                _remote(ins[0], back, send.at[0], recv.at[0], (x, y, 1 - c)), pltpu.make_async_copy(ins[0], there, local))

    def start(ins, outs, send, recv, local):
        out, _, mine = copies(ins, outs, send, recv, local)
        mine.start()
        out.start()

    def wait(ins, outs, send, recv, local):
        out, back, mine = copies(ins, outs, send, recv, local)
        back.wait_recv()
        out.wait_send()
        mine.wait()

    return _Exchange([ghalf], [jax.ShapeDtypeStruct((2 * half, d), ghalf.dtype)], start, wait)


class _ReduceScatter:
    def __init__(self, gw, c_arr, j_arr, tag):
        self.gw, self.c_arr, self.j_arr, self.tag = gw, c_arr, j_arr, tag

    def swap(self, base=0):
        return _ex_swap_halves(self.gw, base)

    def after_swap(self, got, base=0):
        self.s32, s16 = _chip_sum(self.gw, got, self.c_arr, name=f"rs_chip_sum{self.tag}")
        return _ex_scatter(s16, base)

    def after_scatter(self, got16):
        ghalf = _mesh_sum(self.s32, got16, self.j_arr, name=f"rs_mesh_sum{self.tag}")
        return _run_exchange(_ex_share_halves(ghalf), name=f"rs_share{self.tag}")[0]

    def run(self):
        got, = _run_exchange(self.swap(), name=f"rs_swap{self.tag}")
        got16, = _run_exchange(self.after_swap(got), name=f"rs_scatter{self.tag}")
        return self.after_scatter(got16)


def _all_reduce_small(vec, *, name):
    rows, d = vec.shape

    def body(x_ref, o_ref, gat, send_sems, recv_sems, local_sem):
        x, y, c, chips = _place()
        me, sibling = (x, y, c), (x, y, 1 - c)

        def slot(px, py, pc):
            return gat.at[4 * px + 2 * py + pc]

        def copy(k, block, to, src=None):
            return _remote(slot(*block) if src is None else src, slot(*block), send_sems.at[k], recv_sems.at[k], to)

        mine = pltpu.make_async_copy(x_ref, slot(*me), local_sem)
        mine.start()
        first = [copy(0, me, sibling, src=x_ref)]
        first += [copy(1 + j, me, (*chip, c), src=x_ref) for j, chip in enumerate(chips)]
        for cp in first:
            cp.start()
        passed = [copy(4 + j, (*chip, c), sibling) for j, chip in enumerate(chips)]
        for j, chip in enumerate(chips):
            copy(1 + j, (*chip, c), me).wait_recv()
            passed[j].start()
        copy(0, sibling, me).wait_recv()
        for j, chip in enumerate(chips):
            copy(4 + j, (*chip, 1 - c), me).wait_recv()
        for cp in first + passed:
            cp.wait_send()
        mine.wait()
        acc = gat[0]
        for dev in range(1, 8):
            acc = acc + gat[dev]
        o_ref[...] = acc

    vm = pl.BlockSpec(memory_space=pltpu.VMEM)
    return pl.pallas_call(
        body, name=name, in_specs=[vm], out_specs=vm, out_shape=jax.ShapeDtypeStruct((rows, d), F32),
        scratch_shapes=[pltpu.VMEM((8, rows, d), F32), pltpu.SemaphoreType.DMA((7,)), pltpu.SemaphoreType.DMA((7,)),
                        pltpu.SemaphoreType.DMA],
        compiler_params=_cp(32))(vec)


COL_SHARDED = ("w_in", "ffn_w_up")


def _to_pack_rows(name, shard):
    return shard.reshape(-1, D_MODEL)


def _full_from_blocks(name, blocks):
    rows = blocks.shape[1]
    if name in COL_SHARDED:
        return blocks.reshape(4, D_MODEL, rows).transpose(1, 0, 2).reshape(D_MODEL, 4 * rows)
    return blocks.reshape(4 * rows, D_MODEL)


def _blocks_from_full(name, full):
    if name in COL_SHARDED:
        cols = full.shape[1] // 4
        return full.reshape(D_MODEL, 4, cols).transpose(1, 0, 2).reshape(4, cols, D_MODEL)
    return full.reshape(4, full.shape[0] // 4, D_MODEL)


def _row(v):
    return v.reshape(1, -1)


SMALL = (("mix_norm_pre", (1024,), None), ("cv_w", (31, 256), 1), ("cv_b", (256,), None), ("cv_ln_g", (256,), None),
         ("cv_ln_b", (256,), None), ("cv_pw_w", (256, 256), 0), ("cv_pw_b", (256,), None),
         ("mix_norm_post", (1024,), None), ("x_norm_pre", (1024,), None), ("mem_norm", (1024,), None),
         ("x_norm_post", (1024,), None), ("ffn_norm_pre", (1024,), None), ("ffn_conv_w", (3, 5632), 1),
         ("ffn_conv_b", (5632,), None), ("ffn_norm_post", (1024,), None))
BIG = tuple(n for n, _ in PACK_ROWS)
WEIGHT_ORDER = ("mix_norm_pre", "w_in", "cv_w", "cv_b", "cv_ln_g", "cv_ln_b", "cv_pw_w", "cv_pw_b", "w_out",
                "mix_norm_post", "x_norm_pre", "mem_norm", "x_wq", "x_wk", "x_wv", "x_wo", "x_norm_post",
                "ffn_norm_pre", "ffn_w_up", "ffn_conv_w", "ffn_conv_b", "ffn_w_down", "ffn_norm_post")


def _flat_rows(parts):
    v = jnp.concatenate([p.reshape(-1) for p in parts])
    rows = -(-v.shape[0] // (8 * D_MODEL)) * 8
    return jnp.pad(v, (0, rows * D_MODEL - v.shape[0])).reshape(rows, D_MODEL)


def _small_to_rows(blocks):
    v = jnp.concatenate([b.reshape(-1) for b in blocks])
    return jnp.pad(v, (0, SMALL_ROWS * D_MODEL - v.shape[0])).reshape(SMALL_ROWS, D_MODEL)


def _small_from_rows(rows):
    flat, out, off = rows.reshape(-1), [], 0
    for _, shape, _ in SHARDED_SMALL:
        size = int(np.prod(shape))
        out.append(flat[off:off + size].reshape(shape))
        off += size
    return out


def _chip_block(full, j, shape, axis):
    return lax.slice_in_dim(full, j * shape[axis], (j + 1) * shape[axis], axis=axis)


REST_GROUP = ("w_in", "w_out")
XA_GROUP = ("x_wq", "x_wk", "x_wv", "x_wo")
FFN_GROUP = ("ffn_w_up", "ffn_w_down")


class _Weights:
    FIRST = (0, 768)
    OWN = ((768, 1280), (2048, 1408), (3456, 704))
    NEXT = ((0, 1024), (1024, 1024), (2048, 1408), (3456, 704))
    SLOTS = ("mix_in", "sb_fwd", "dl_fwd", "dl_mix", "ffn_up", "ffn_gate", "ffn_down")

    def __init__(self, packs):
        self.packs, self.pieces, self.landed, self.plan = packs, {}, None, {}
        for slot, piece in zip(self.SLOTS[:3], self.OWN):
            self.plan[(0, slot)] = (0,) + piece
        for l in range(len(packs) - 1):
            for slot, piece in zip(self.SLOTS[3:], self.NEXT):
                self.plan[(l, slot)] = (l + 1,) + piece
        first = _run_exchange(_ex_gather(packs[0], *self.FIRST), name="gather_first")[0]
        self.pieces[(0,) + self.FIRST] = _run_exchange(_ex_gather_forward(first), name="gather_first_forward")[0]

    def ride(self, layer, slot, call):
        start, todo, ex = self.plan.get((layer, slot)), [], None
        if start is not None:
            ex = _ex_gather(self.packs[start[0]], start[1], start[2])
            todo.append(("landed", start))
        if self.landed is not None:
            key, buf = self.landed
            forward = _ex_gather_forward(buf, SEM_BASE_2 if ex is not None else 0)
            ex = forward if ex is None else _ex_join(ex, forward)
            todo.append(("piece", key))
            self.landed = None
        outs = list(call(carry=ex))
        n = len(outs) - len(todo)
        for (kind, key), buf in zip(todo, outs[n:]):
            if kind == "landed":
                self.landed = (key, buf)
            else:
                self.pieces[key] = buf
        return outs[:n]

    def rows_of(self, layer, name):
        off = 0
        for n, rows in WEIGHT_PACK:
            if n == name:
                break
            off += rows
        for (l, r0, nrows), buf in self.pieces.items():
            if l == layer and r0 <= off < r0 + nrows:
                return buf[:, off - r0:off - r0 + rows, :]
        raise KeyError(f"{name} of layer {layer} is not gathered yet")

    def weight(self, layer, name):
        return _full_from_blocks(name, self.rows_of(layer, name))

    def small(self, layer):
        bits = self.rows_of(layer, "small").astype(jnp.bfloat16).reshape(4, SMALL_ROWS, D_MODEL, 2)
        per_chip = [_small_from_rows(r) for r in lax.bitcast_convert_type(bits, F32)]
        return {n: jnp.concatenate([blocks[k] for blocks in per_chip], axis=axis)
                for k, (n, _, axis) in enumerate(SHARDED_SMALL)}


class _Params:
    def __init__(self, weights, layer, small):
        self.weights, self.layer, self.small, self.cache = weights, layer, small, {}

    def __getitem__(self, name):
        if name in self.small:
            return self.small[name]
        if name not in self.cache:
            if name in [n for n, _, _ in SHARDED_SMALL]:
                self.cache.update(self.weights.small(self.layer))
            else:
                self.cache[name] = self.weights.weight(self.layer, name)
        return self.cache[name]


def _layer_fwd(h0, mem, p, cos, sin, tag, ride):
    sv = {"h0": h0}
    n1, u = ride("mix_in", functools.partial(_rms_mm, h0, _row(p["mix_norm_pre"]), p["w_in"], tm=1024, tn=1408,
                                             out_dtype=F32, name=f"mix_in{tag}"))
    a_out, = ride("sb_fwd", functools.partial(_sb_fwd, u, name=f"sb_fwd{tag}"))
    b_out, c = _cv_fwd(u, p["cv_w"], _row(p["cv_b"]), _row(p["cv_ln_g"]), _row(p["cv_ln_b"]),
                       p["cv_pw_w"].astype(BF16), _row(p["cv_pw_b"]), name=f"cv_fwd{tag}")
    qkv = _rope_perm(u, cos, sin, name=f"rope_perm{tag}")
    o_p, l_p = ride("dl_fwd", functools.partial(_dl_fwd, qkv, name=f"dl_fwd{tag}"))
    c_out, o_dl, lse = ride("dl_mix", functools.partial(_dl_mix, o_p, l_p, name=f"dl_mix{tag}"))
    cat = jnp.concatenate([a_out, b_out, c_out], axis=1)
    y1, h1 = _mm_post(cat, p["w_out"], h0, _row(p["mix_norm_post"]), tm=512, name=f"mix_out{tag}")
    sv.update(n1=n1, u=u, c=c, qkv=qkv, o_dl=o_dl, lse=lse, cat=cat, y1=y1, h1=h1)

    n2, q = _rms_mm(h1, _row(p["x_norm_pre"]), p["x_wq"], tm=512, tn=1024, out_dtype=BF16, name=f"xa_q{tag}")
    wkv = jnp.concatenate([p["x_wk"], p["x_wv"]], axis=1)
    mem_n, kv = _rms_mm(mem, _row(p["mem_norm"]), wkv, tm=mem.shape[0], tn=1024, out_dtype=BF16, name=f"xa_kv{tag}")
    k, v = kv[:, :D_MODEL], kv[:, D_MODEL:]
    o_x = _xa_fwd(q, k, v, name=f"xa_fwd{tag}")
    y2, h2 = _mm_post(o_x, p["x_wo"], h1, _row(p["x_norm_post"]), tm=512, name=f"xa_out{tag}")
    sv.update(n2=n2, q=q, mem_n=mem_n, k=k, v=v, o_x=o_x, y2=y2, h2=h2, wkv=wkv)

    n3, up = ride("ffn_up", functools.partial(_rms_mm, h2, _row(p["ffn_norm_pre"]), p["ffn_w_up"], tm=1024, tn=1408,
                                              out_dtype=F32, name=f"ffn_up{tag}"))
    act, = ride("ffn_gate", functools.partial(_ff_gate_fwd, up, p["ffn_conv_w"], _row(p["ffn_conv_b"]),
                                              name=f"ffn_gate{tag}"))
    y3, h3 = ride("ffn_down", functools.partial(_mm_post, act, p["ffn_w_down"], h2, _row(p["ffn_norm_post"]), tm=512,
                                                name=f"ffn_down{tag}"))
    sv.update(n3=n3, up=up, act=act, y3=y3)
    return h3, sv


def _layer_bwd(dh3, mem, p, sv, cos, sin, tag, riding, new_rs):
    g = {}
    s8 = lambda part: part.sum(axis=0)
    rode = None

    dy3, dgp = _rms_bwd(sv["y3"], _row(p["ffn_norm_post"]), dh3, None, out_dtype=BF16, tm=512, name=f"ffn_post_b{tag}")
    g["ffn_norm_post"] = s8(dgp)
    dact = _mm_nt(dy3, p["ffn_w_down"], tm=512, tn=1408, out_dtype=F32, name=f"ffn_down_bx{tag}")
    g["ffn_w_down"] = _mm_tn(sv["act"], dy3, tk=1408, tn=1024, tm=2048, name=f"ffn_down_bw{tag}")
    dgu, dvu, dcw, dcb, *got = _ff_gate_bwd(sv["up"], dact, p["ffn_conv_w"], _row(p["ffn_conv_b"]),
                                            name=f"ffn_gate_b{tag}", carry=riding.swap() if riding else None)
    scatter = riding.after_swap(got[0]) if riding else None
    g["ffn_conv_w"] = jnp.concatenate([dcw[0], dcw[1]], axis=1).reshape(3, 8, 2 * D_FF).sum(axis=1)
    g["ffn_conv_b"] = jnp.concatenate([dcb[0], dcb[1]], axis=1).sum(axis=0)
    dup = jnp.concatenate([dgu, dvu], axis=1)
    dn3 = _mm_nt(dup, p["ffn_w_up"], tm=256, tn=512, out_dtype=F32, name=f"ffn_up_bx{tag}")
    g["ffn_w_up"] = _mm_tn(sv["n3"], dup, tk=512, tn=1408, tm=2048, name=f"ffn_up_bw{tag}")
    ffn_rs = new_rs(FFN_GROUP, g, f"{tag}_ffn")
    dh2, dgp = _rms_bwd(sv["h2"], _row(p["ffn_norm_pre"]), dn3, dh3, out_dtype=F32, tm=512, name=f"ffn_pre_b{tag}")
    g["ffn_norm_pre"] = s8(dgp)

    dy2, dgp = _rms_bwd(sv["y2"], _row(p["x_norm_post"]), dh2, None, out_dtype=BF16, tm=512, name=f"xa_post_b{tag}")
    g["x_norm_post"] = s8(dgp)
    do_x = _mm_nt(dy2, p["x_wo"], tm=512, tn=1024, out_dtype=BF16, name=f"xa_out_bx{tag}")
    g["x_wo"] = _mm_tn(sv["o_x"], dy2, tk=512, tn=1024, tm=2048, name=f"xa_out_bw{tag}")
    dq, dk, dv, got = _xa_bwd(sv["q"], sv["k"], sv["v"], do_x, name=f"xa_bwd{tag}", carry=ffn_rs.swap())
    ffn_scatter = ffn_rs.after_swap(got)
    dn2 = _mm_nt(dq, p["x_wq"], tm=512, tn=1024, out_dtype=F32, name=f"xa_q_bx{tag}")
    g["x_wq"] = _mm_tn(sv["n2"], dq, tk=512, tn=1024, tm=2048, name=f"xa_q_bw{tag}")
    dkv = jnp.concatenate([dk, dv], axis=1).astype(BF16)
    nm = mem.shape[0]
    dmem_n = _mm_nt(dkv, sv["wkv"], tm=nm, tn=1024, out_dtype=F32, name=f"xa_kv_bx{tag}")
    dwkv = _mm_tn(sv["mem_n"], dkv, tk=512, tn=2048, tm=nm, name=f"xa_kv_bw{tag}")
    g["x_wk"], g["x_wv"] = dwkv[:, :D_MODEL], dwkv[:, D_MODEL:]
    _, dgp = _rms_bwd(mem, _row(p["mem_norm"]), dmem_n, None, out_dtype=BF16, tm=nm, name=f"xa_mem_b{tag}")
    g["mem_norm"] = s8(dgp)
    xa_rs = new_rs(XA_GROUP, g, f"{tag}_xa")
    dh1, dgp, got = _rms_bwd(sv["h1"], _row(p["x_norm_pre"]), dn2, dh2, out_dtype=F32, tm=512, name=f"xa_pre_b{tag}",
                             carry=xa_rs.swap())
    xa_scatter = xa_rs.after_swap(got, SEM_BASE_2 if riding else 0)
    g["x_norm_pre"] = s8(dgp)

    dy1, dgp = _rms_bwd(sv["y1"], _row(p["mix_norm_post"]), dh1, None, out_dtype=BF16, tm=512, name=f"mix_post_b{tag}")
    g["mix_norm_post"] = s8(dgp)
    dcat = _mm_nt(dy1, p["w_out"], tm=512, tn=1024, out_dtype=F32, name=f"mix_out_bx{tag}")
    g["w_out"] = _mm_tn(sv["cat"], dy1, tk=512, tn=1024, tm=2048, name=f"mix_out_bw{tag}")
    u = sv["u"]
    dq_sb, dk_sb, dv_sb, *got = _sb_bwd(u, dcat, name=f"sb_bwd{tag}",
                                        carry=_ex_join(scatter, xa_scatter) if riding else xa_scatter)
    if riding:
        rode = riding.after_scatter(got[0])
    xa_rows = xa_rs.after_scatter(got[-1])
    pw_b16 = p["cv_pw_w"].astype(BF16)
    dc, dpw, vec = _cv_bwd_local(sv["c"], dcat, _row(p["cv_ln_g"]), _row(p["cv_ln_b"]), pw_b16, name=f"cv_bwd_a{tag}")
    g["cv_pw_w"] = dpw
    vec = vec.reshape(3, 8, CV_W).sum(axis=1)
    g["cv_pw_b"], g["cv_ln_g"], g["cv_ln_b"] = vec[0], vec[1], vec[2]
    du_cv, dcw, dcb = _cv_bwd_conv(u, dc, p["cv_w"], name=f"cv_bwd_b{tag}")
    g["cv_w"] = dcw.reshape(CV_K, 8, CV_W).sum(axis=1)
    g["cv_b"] = dcb.sum(axis=0)
    dop, stats = _dl_bwd_prep(dcat, sv["o_dl"], sv["lse"], name=f"dl_prep_b{tag}")
    cur, prev, got = _dl_bwd(sv["qkv"], dop, stats, name=f"dl_bwd{tag}", carry=ffn_scatter)
    ffn_rows = ffn_rs.after_scatter(got)
    du_dl = _dl_bwd_finish(cur, prev, cos, sin, name=f"dl_fin_b{tag}")
    du = jnp.concatenate([dq_sb.astype(BF16), dk_sb.astype(BF16), dv_sb.astype(BF16), du_cv, du_dl], axis=1)
    dn1 = _mm_nt(du, p["w_in"], tm=512, tn=512, out_dtype=F32, name=f"mix_in_bx{tag}")
    g["w_in"] = _mm_tn(sv["n1"], du, tk=512, tn=1408, tm=2048, name=f"mix_in_bw{tag}")
    dh0, dgp = _rms_bwd(sv["h0"], _row(p["mix_norm_pre"]), dn1, dh1, out_dtype=F32, tm=512, name=f"mix_pre_b{tag}")
    g["mix_norm_pre"] = s8(dgp)
    return dh0, g, (xa_rows, ffn_rows), rode


def _step(x, mem, positions, loss_target, w, m, v):
    depth = w["w_in"].shape[0]
    xi, yi, ci = lax.axis_index("x"), lax.axis_index("y"), lax.axis_index("c")
    chip = 2 * xi + yi
    h = x[0]
    mem0 = mem[0]
    s_len = h.shape[0]

    def pack_rows(n, l):
        if n == "small":
            rows = _small_to_rows([w[name][l] for name, _, _ in SHARDED_SMALL])
            return lax.bitcast_convert_type(rows, jnp.bfloat16).reshape(2 * SMALL_ROWS, D_MODEL)
        return _to_pack_rows(n, w[n][l]).astype(BF16)

    packs = [jnp.concatenate([pack_rows(n, l) for n, _ in WEIGHT_PACK], axis=0) for l in range(depth)]
    weights = _Weights(packs)
    params = [_Params(weights, l, {n: w[n][l] for n, _, axis in SMALL if axis is None}) for l in range(depth)]

    inv_freq = ROPE_THETA ** (-jnp.arange(HD // 2, dtype=F32) / (HD // 2))
    cos, sin = _rope_tables(positions.reshape(s_len, 1), jnp.tile(inv_freq, 4).reshape(1, LANES), name="rope_tables")

    saved = []
    for l in range(depth):
        h, sv = _layer_fwd(h, mem0, params[l], cos, sin, f"_l{l}", functools.partial(weights.ride, l))
        saved.append(sv)
    dh, sq = _loss_grad(h, loss_target[0], tm=512, name="loss_grad")
    loss = lax.psum(0.5 * jnp.sum(sq) / D_MODEL, ("x", "y", "c"))

    c_arr, j_arr = jnp.reshape(ci, (1,)).astype(jnp.int32), jnp.reshape(chip, (1,)).astype(jnp.int32)

    def new_rs(names, g, tag):
        blocks = [_blocks_from_full(n, g[n]) for n in names]
        if names is REST_GROUP:
            blocks.append(jnp.stack([_small_to_rows([_chip_block(g[n], j, shape, axis) for n, shape, axis in SHARDED_SMALL])
                                     for j in range(4)]))
        return _ReduceScatter(jnp.concatenate(blocks, axis=1), c_arr, j_arr, tag)

    grads, later_rows, rest_rows, pending = [None] * depth, [None] * depth, [None] * depth, None
    for l in reversed(range(depth)):
        dh, grads[l], later_rows[l], rode = _layer_bwd(dh, mem0, params[l], saved[l], cos, sin, f"_l{l}", pending, new_rs)
        if pending is not None:
            rest_rows[l + 1] = rode
        pending = new_rs(REST_GROUP, grads[l], f"_l{l}_rest")
    grad_x = dh[None]

    out_g, out_d, out_m, out_v = {}, {}, {}, {}
    pack_off, off = {}, 0
    for n, rows in PACK_ROWS:
        pack_off[n] = (off, rows)
        off += rows

    def reduced(l, n):
        start, rows = pack_off[n]
        for names, block in ((REST_GROUP, rest_rows[l]), (XA_GROUP, later_rows[l][0]), (FFN_GROUP, later_rows[l][1])):
            if n in names:
                return block[start - pack_off[names[0]][0]:][:rows]

    def update(n, carry=None):
        shard_shape = w[n].shape
        g_n = jnp.stack([reduced(l, n) for l in range(depth)]).reshape(shard_shape)
        flat = lambda a: a.reshape(-1, shard_shape[-1])
        d_n, m_n, v_n, *rode = _adamw(flat(w[n]), flat(g_n), flat(m[n]), flat(v[n]), name=f"adamw_{n}", carry=carry)
        out_g[n], out_d[n], out_m[n], out_v[n] = g_n, d_n.reshape(shard_shape), m_n.reshape(shard_shape), v_n.reshape(shard_shape)
        return rode

    got, = update("ffn_w_down", pending.swap())
    got16, = update("ffn_w_up", pending.after_swap(got))
    rest_rows[0] = pending.after_scatter(got16)
    for n, _ in PACK_ROWS:
        if n not in FFN_GROUP:
            update(n)

    g_small = _all_reduce_small(_flat_rows([grads[l][n] for l in range(depth) for n, _, axis in SMALL if axis is None]),
                                name="all_reduce_small_grads").reshape(-1)
    local_g, off = {}, 0
    for l in range(depth):
        for n, shape, axis in SMALL:
            if axis is None:
                size = int(np.prod(shape))
                local_g.setdefault(n, []).append(g_small[off:off + size].reshape(shape))
                off += size
        small_rows = rest_rows[l][sum(pack_off[n][1] for n in REST_GROUP):]
        for (n, _, _), block in zip(SHARDED_SMALL, _small_from_rows(small_rows)):
            local_g.setdefault(n, []).append(block)
    names = [n for n, _, _ in SMALL]
    g_loc = {n: jnp.stack(local_g[n]) for n in names}
    d_s, m_s, v_s = _adamw(_flat_rows([w[n] for n in names]), _flat_rows([g_loc[n] for n in names]),
                           _flat_rows([m[n] for n in names]), _flat_rows([v[n] for n in names]), name="adamw_small")
    off = 0
    for n in names:
        size = int(np.prod(w[n].shape))
        take = lambda a: a.reshape(-1)[off:off + size].reshape(w[n].shape)
        out_g[n], out_d[n], out_m[n], out_v[n] = g_loc[n], take(d_s), take(m_s), take(v_s)
        off += size

    outs = [loss, grad_x]
    for group in (out_g, out_d, out_m, out_v):
        outs += [group[n] for n in WEIGHT_ORDER]
    return tuple(outs)


def kernel(x, mem, positions, mix_norm_pre, w_in, cv_w, cv_b, cv_ln_g, cv_ln_b, cv_pw_w, cv_pw_b, w_out, mix_norm_post, x_norm_pre, mem_norm, x_wq, x_wk, x_wv, x_wo, x_norm_post, ffn_norm_pre, ffn_w_up, ffn_conv_w, ffn_conv_b, ffn_w_down, ffn_norm_post, loss_target, m_mix_norm_pre, m_w_in, m_cv_w, m_cv_b, m_cv_ln_g, m_cv_ln_b, m_cv_pw_w, m_cv_pw_b, m_w_out, m_mix_norm_post, m_x_norm_pre, m_mem_norm, m_x_wq, m_x_wk, m_x_wv, m_x_wo, m_x_norm_post, m_ffn_norm_pre, m_ffn_w_up, m_ffn_conv_w, m_ffn_conv_b, m_ffn_w_down, m_ffn_norm_post, v_mix_norm_pre, v_w_in, v_cv_w, v_cv_b, v_cv_ln_g, v_cv_ln_b, v_cv_pw_w, v_cv_pw_b, v_w_out, v_mix_norm_post, v_x_norm_pre, v_mem_norm, v_x_wq, v_x_wk, v_x_wv, v_x_wo, v_x_norm_post, v_ffn_norm_pre, v_ffn_w_up, v_ffn_conv_w, v_ffn_conv_b, v_ffn_w_down, v_ffn_norm_post):
    w = dict(zip(WEIGHT_ORDER, (mix_norm_pre, w_in, cv_w, cv_b, cv_ln_g, cv_ln_b, cv_pw_w, cv_pw_b, w_out, mix_norm_post, x_norm_pre, mem_norm, x_wq, x_wk, x_wv, x_wo, x_norm_post, ffn_norm_pre, ffn_w_up, ffn_conv_w, ffn_conv_b, ffn_w_down, ffn_norm_post)))
    m = dict(zip(WEIGHT_ORDER, (m_mix_norm_pre, m_w_in, m_cv_w, m_cv_b, m_cv_ln_g, m_cv_ln_b, m_cv_pw_w, m_cv_pw_b, m_w_out, m_mix_norm_post, m_x_norm_pre, m_mem_norm, m_x_wq, m_x_wk, m_x_wv, m_x_wo, m_x_norm_post, m_ffn_norm_pre, m_ffn_w_up, m_ffn_conv_w, m_ffn_conv_b, m_ffn_w_down, m_ffn_norm_post)))
    v = dict(zip(WEIGHT_ORDER, (v_mix_norm_pre, v_w_in, v_cv_w, v_cv_b, v_cv_ln_g, v_cv_ln_b, v_cv_pw_w, v_cv_pw_b, v_w_out, v_mix_norm_post, v_x_norm_pre, v_mem_norm, v_x_wq, v_x_wk, v_x_wv, v_x_wo, v_x_norm_post, v_ffn_norm_pre, v_ffn_w_up, v_ffn_conv_w, v_ffn_conv_b, v_ffn_w_down, v_ffn_norm_post)))
    return _step(x, mem, positions, loss_target, w, m, v)
```

```python
import functools

import jax
import jax.numpy as jnp
import numpy as np
from jax import lax
from jax.experimental import pallas as pl
from jax.experimental.pallas import tpu as pltpu

F32, BF16 = jnp.float32, jnp.bfloat16
MESH = pl.DeviceIdType.MESH
EPS = 1e-6
LANES = 128
BLK = 128
HD = 64
D_MODEL = 1024
D_FF = 2816
SB_W, CV_W, DL_W = 256, 256, 512
CV_K = 31
ROPE_THETA = 10000.0
DILATIONS = (1, 4, 16)
X_HEADS, X_HD = 4, 256
ADAM_LR, ADAM_B1, ADAM_B2, ADAM_EPS, ADAM_WD, ADAM_STEP = 0.001, 0.9, 0.999, 1e-08, 0.01, 10
NEG_INF = float("-inf")
MIB = 1 << 20

PACK_ROWS = (("w_in", 704), ("w_out", 256), ("x_wq", 256), ("x_wk", 256), ("x_wv", 256), ("x_wo", 256),
             ("ffn_w_up", 1408), ("ffn_w_down", 704))
PACK_RL = sum(r for _, r in PACK_ROWS)
SHARDED_SMALL = (("cv_w", (31, 64), 1), ("ffn_conv_w", (3, 1408), 1), ("cv_pw_w", (64, 256), 0))
SMALL_ROWS = 32
WEIGHT_PACK = (PACK_ROWS[0], ("small", 2 * SMALL_ROWS)) + PACK_ROWS[1:]


def _cp(vmem_mb=48):
    return pltpu.CompilerParams(vmem_limit_bytes=vmem_mb * MIB)


def _dot(a, b):
    return jnp.dot(a, b, preferred_element_type=F32)


def _dot_nt(a, b):
    return lax.dot_general(a, b, (((1,), (1,)), ((), ())), preferred_element_type=F32)


def _dot_tn(a, b):
    return lax.dot_general(a, b, (((0,), (0,)), ((), ())), preferred_element_type=F32)


def _dot_hilo(x, m):
    hi = x.astype(BF16)
    lo = (x - hi.astype(F32)).astype(BF16)
    return _dot(hi, m) + _dot(lo, m)


def _rowsum8(x):
    t, c = x.shape
    return x.reshape(t // 8, 8, c).sum(axis=0)


def _acc_out(ref, i, val):
    @pl.when(i == 0)
    def _():
        ref[...] = val

    @pl.when(i > 0)
    def _():
        ref[...] += val


def _tile(n, cap, mult=8):
    t = min(n, cap)
    while n % t or t % mult:
        t -= 1
    return t


def _rms_mm(x, g, w, *, tm, tn, out_dtype, name, carry=None):
    m, d = x.shape
    n_out = w.shape[1]

    def body(x_ref, g_ref, w_ref, n_ref, o_ref):
        @pl.when(pl.program_id(1) == 0)
        def _():
            xv = x_ref[...]
            r = lax.rsqrt(jnp.mean(xv * xv, axis=-1, keepdims=True) + EPS)
            n_ref[...] = (xv * r * g_ref[...]).astype(BF16)

        o_ref[...] = _dot(n_ref[...], w_ref[...]).astype(out_dtype)

    return _call(
        body, grid=(m // tm, n_out // tn), name=name, carry=carry,
        in_specs=[pl.BlockSpec((tm, d), lambda i, j: (i, 0)), pl.BlockSpec((1, d), lambda i, j: (0, 0)),
                  pl.BlockSpec((d, tn), lambda i, j: (0, j))],
        out_specs=[pl.BlockSpec((tm, d), lambda i, j: (i, 0)), pl.BlockSpec((tm, tn), lambda i, j: (i, j))],
        out_shape=[jax.ShapeDtypeStruct((m, d), BF16), jax.ShapeDtypeStruct((m, n_out), out_dtype)],
        args=(x, g, w))


def _mm_post(a, w, h, g, *, tm, name, carry=None):
    m, k = a.shape
    d = w.shape[1]

    def body(a_ref, w_ref, h_ref, g_ref, y_ref, ho_ref):
        y = _dot(a_ref[...], w_ref[...])
        y_ref[...] = y
        r = lax.rsqrt(jnp.mean(y * y, axis=-1, keepdims=True) + EPS)
        ho_ref[...] = h_ref[...] + y * r * g_ref[...]

    return _call(
        body, grid=(m // tm,), name=name, carry=carry,
        in_specs=[pl.BlockSpec((tm, k), lambda i: (i, 0)), pl.BlockSpec((k, d), lambda i: (0, 0)),
                  pl.BlockSpec((tm, d), lambda i: (i, 0)), pl.BlockSpec((1, d), lambda i: (0, 0))],
        out_specs=[pl.BlockSpec((tm, d), lambda i: (i, 0)), pl.BlockSpec((tm, d), lambda i: (i, 0))],
        out_shape=[jax.ShapeDtypeStruct((m, d), F32), jax.ShapeDtypeStruct((m, d), F32)],
        args=(a, w, h, g))


def _mm_nt(a, w, *, tm, tn, out_dtype, name):
    m, k = a.shape
    n_out = w.shape[0]

    def body(a_ref, w_ref, o_ref):
        o_ref[...] = _dot_nt(a_ref[...], w_ref[...]).astype(out_dtype)

    return pl.pallas_call(
        body, grid=(n_out // tn, m // tm), name=name,
        in_specs=[pl.BlockSpec((tm, k), lambda j, i: (i, 0)), pl.BlockSpec((tn, k), lambda j, i: (j, 0))],
        out_specs=pl.BlockSpec((tm, tn), lambda j, i: (i, j)),
        out_shape=jax.ShapeDtypeStruct((m, n_out), out_dtype),
        compiler_params=_cp())(a, w)


def _mm_tn(x, dy, *, tk, tn, tm, name):
    m, k = x.shape
    n_out = dy.shape[1]

    def body(x_ref, d_ref, o_ref):
        _acc_out(o_ref, pl.program_id(2), _dot_tn(x_ref[...], d_ref[...]))

    return pl.pallas_call(
        body, grid=(k // tk, n_out // tn, m // tm), name=name,
        in_specs=[pl.BlockSpec((tm, tk), lambda a, b, c: (c, a)), pl.BlockSpec((tm, tn), lambda a, b, c: (c, b))],
        out_specs=pl.BlockSpec((tk, tn), lambda a, b, c: (a, b)),
        out_shape=jax.ShapeDtypeStruct((k, n_out), F32),
        compiler_params=_cp())(x, dy)


def _rms_bwd(x, g, dout, res, *, out_dtype, tm, name, carry=None):
    m, d = x.shape
    has_res = res is not None

    def body(*refs):
        if has_res:
            x_ref, g_ref, d_ref, r_ref, dx_ref, dg_ref = refs
        else:
            x_ref, g_ref, d_ref, dx_ref, dg_ref = refs
        xv = x_ref[...]
        dv = d_ref[...].astype(F32)
        r = lax.rsqrt(jnp.mean(xv * xv, axis=-1, keepdims=True) + EPS)
        xh = xv * r
        dxh = dv * g_ref[...]
        dx = r * (dxh - xh * jnp.mean(dxh * xh, axis=-1, keepdims=True))
        if has_res:
            dx = dx + r_ref[...]
        dx_ref[...] = dx.astype(out_dtype)
        _acc_out(dg_ref, pl.program_id(0), _rowsum8(dv * xh))

    row = pl.BlockSpec((tm, d), lambda i: (i, 0))
    ins = [row, pl.BlockSpec((1, d), lambda i: (0, 0)), row] + ([row] if has_res else [])
    args = (x, g, dout) + ((res,) if has_res else ())
    return _call(
        body, grid=(m // tm,), name=name, carry=carry, in_specs=ins,
        out_specs=[row, pl.BlockSpec((8, d), lambda i: (0, 0))],
        out_shape=[jax.ShapeDtypeStruct((m, d), out_dtype), jax.ShapeDtypeStruct((8, d), F32)], args=args)


def _loss_grad(h, tgt, *, tm, name):
    m, d = h.shape

    def body(h_ref, t_ref, dh_ref, p_ref):
        e = h_ref[...] - t_ref[...]
        dh_ref[...] = e / d
        _acc_out(p_ref, pl.program_id(0), _rowsum8(e * e))

    row = pl.BlockSpec((tm, d), lambda i: (i, 0))
    return pl.pallas_call(
        body, grid=(m // tm,), name=name, in_specs=[row, row],
        out_specs=[row, pl.BlockSpec((8, d), lambda i: (0, 0))],
        out_shape=[jax.ShapeDtypeStruct((m, d), F32), jax.ShapeDtypeStruct((8, d), F32)],
        compiler_params=_cp())(h, tgt)


def _adamw(w, g, m, v, *, name, carry=None):
    r, c = w.shape
    tr = _tile(r, 256)

    def body(w_ref, g_ref, m_ref, v_ref, d_ref, mo_ref, vo_ref):
        gv = g_ref[...]
        m2 = ADAM_B1 * m_ref[...] + (1.0 - ADAM_B1) * gv
        v2 = ADAM_B2 * v_ref[...] + (1.0 - ADAM_B2) * jnp.square(gv)
        m_hat = m2 / (1.0 - ADAM_B1 ** ADAM_STEP)
        v_hat = v2 / (1.0 - ADAM_B2 ** ADAM_STEP)
        d_ref[...] = -ADAM_LR * (m_hat / (jnp.sqrt(v_hat) + ADAM_EPS) + ADAM_WD * w_ref[...])
        mo_ref[...] = m2
        vo_ref[...] = v2

    blk = pl.BlockSpec((tr, c), lambda i: (i, 0))
    return _call(body, grid=(r // tr,), name=name, carry=carry, in_specs=[blk] * 4, out_specs=[blk] * 3,
                 out_shape=[jax.ShapeDtypeStruct((r, c), F32)] * 3, args=(w, g, m, v))


def _head_masks():
    lane = lax.broadcasted_iota(jnp.int32, (BLK, LANES), 1)
    row = lax.broadcasted_iota(jnp.int32, (BLK, LANES), 0)
    return lane, row, lane < HD


def _sb_scores(q_a, k, before):
    z = _dot_nt(q_a, k)
    sp = jnp.log1p(jnp.exp(-jnp.abs(z)))
    ls_pos = jnp.minimum(z, 0.0) - sp
    lkeep = jnp.where(before, ls_pos - z, 0.0)
    return ls_pos, lkeep


SB_DEAD = -104.0


def _sb_alive(jj, i, carry):
    return jnp.logical_and(jj <= i, jnp.max(carry) > SB_DEAD)


SB_QB_FWD = 2
SB_QB = 2


def _sb_before(jj, qb=SB_QB):
    lane = lax.broadcasted_iota(jnp.int32, (qb * 2 * BLK, LANES), 1)
    row = lax.broadcasted_iota(jnp.int32, (qb * 2 * BLK, LANES), 0)
    below_diag = jj - (qb - 1) + row // (2 * BLK)
    return jnp.logical_or(below_diag > 0, jnp.logical_and(below_diag == 0, lane < row % BLK))


def _sb_stack(x, lane_h, qb=SB_QB):
    return jnp.concatenate([_stack_heads(x[b * BLK:(b + 1) * BLK], lane_h) for b in range(qb)], axis=0)


def _sb_unstack(x, lane_h, qb=SB_QB):
    return jnp.concatenate([jnp.where(lane_h, x[2 * b * BLK:(2 * b + 1) * BLK], x[(2 * b + 1) * BLK:(2 * b + 2) * BLK])
                            for b in range(qb)], axis=0)


SB_ROWS = SB_QB * 2 * BLK


def _sb_fwd(u, *, name, carry=None):
    s_len = u.shape[0]
    qb = SB_QB_FWD
    qrows, rows = qb * BLK, qb * 2 * BLK

    def body(q_ref, k_ref, v_ref, o_ref):
        top = pl.program_id(0) * qb + qb - 1
        lane, row, lane_h = _head_masks()
        suffix = (row > lane).astype(BF16)
        pairs = [slice(hp * LANES, (hp + 1) * LANES) for hp in range(2)]
        qs = [_sb_stack(q_ref[:, cs] * 0.125, lane_h, qb) for cs in pairs]

        def step(state):
            jj, ccs, accs = state[0], state[1:3], state[3:5]
            rows_k = pl.ds(pl.multiple_of((top - jj) * BLK, BLK), BLK)
            before = _sb_before(jj, qb)
            scores = [_sb_scores(q, k_ref[rows_k, cs].astype(BF16), before) for q, cs in zip(qs, pairs)]
            between = [_dot_hilo(lkeep, suffix) + cc for (_, lkeep), cc in zip(scores, ccs)]
            atts = [jnp.where(before, jnp.exp(ls_pos + b), 0.0).astype(BF16) for (ls_pos, _), b in zip(scores, between)]
            new_cc = [cc + jnp.sum(lkeep, axis=1, keepdims=True) for (_, lkeep), cc in zip(scores, ccs)]
            new_acc = [acc + _dot(a, v_ref[rows_k, cs].astype(BF16)) for a, acc, cs in zip(atts, accs, pairs)]
            return (jj + 1, *new_cc, *new_acc)

        zc, za = jnp.zeros((rows, 1), F32), jnp.zeros((rows, LANES), F32)
        res = lax.while_loop(lambda st: _sb_alive(st[0], top, jnp.maximum(st[1], st[2])), step,
                             (jnp.int32(0), zc, zc, za, za))
        for hp, cs in enumerate(pairs):
            o_ref[:, cs] = _sb_unstack(res[3 + hp], lane_h, qb).astype(BF16)

    wide = 2 * LANES
    return _call(
        body, grid=(s_len // qrows,), name=name, carry=carry,
        in_specs=[pl.BlockSpec((qrows, wide), lambda i: (i, 0)), pl.BlockSpec((s_len, wide), lambda i: (0, 1)),
                  pl.BlockSpec((s_len, wide), lambda i: (0, 2))],
        out_specs=[pl.BlockSpec((qrows, wide), lambda i: (i, 0))],
        out_shape=[jax.ShapeDtypeStruct((s_len, SB_W), BF16)], args=(u, u, u))


def _sb_bwd(u, dcat, *, name, carry=None):
    s_len = u.shape[0]
    nq = s_len // BLK
    qrows = SB_QB * BLK

    def body(q_ref, k_ref, v_ref, do_ref, dq_ref, dk_ref, dv_ref, g_scr, b_scr):
        step = pl.program_id(1)
        top = step * SB_QB + SB_QB - 1
        lane, row, lane_h = _head_masks()
        suffix = (row > lane).astype(BF16)
        prefix = (row < lane).astype(BF16)
        qf = q_ref[...]
        qs = _sb_stack(qf * 0.125, lane_h)
        qu = _sb_stack(qf, lane_h)
        dos = _sb_stack(do_ref[...], lane_h)

        @pl.when(step == 0)
        def _():
            dk_ref[...] = jnp.zeros_like(dk_ref)
            dv_ref[...] = jnp.zeros_like(dv_ref)

        def down(state):
            jj, cc = state
            j = top - jj
            off = pl.multiple_of(j * BLK, BLK)
            k = k_ref[pl.ds(off, BLK), :].astype(BF16)
            v = v_ref[pl.ds(off, BLK), :].astype(BF16)
            before = _sb_before(jj)
            ls_pos, lkeep = _sb_scores(qs, k, before)
            between = _dot_hilo(lkeep, suffix) + cc
            att = jnp.where(before, jnp.exp(ls_pos + between), 0.0)
            g_scr[j] = att * _dot_nt(dos, v)
            b_scr[j] = jnp.exp(ls_pos)
            dv_ref[pl.ds(off, BLK), :] += _dot_tn(att.astype(BF16), dos)
            return jj + 1, cc + jnp.sum(lkeep, axis=1, keepdims=True)

        zc = jnp.zeros((SB_ROWS, 1), F32)
        visited = lax.while_loop(lambda st: _sb_alive(st[0], top, st[1]), down, (jnp.int32(0), zc))[0]

        def up(j, carry):
            pc, dq = carry
            off = pl.multiple_of(j * BLK, BLK)
            k = k_ref[pl.ds(off, BLK), :].astype(BF16)
            g, beta = g_scr[j], b_scr[j]
            below = _dot_hilo(g, prefix) + pc
            dz = (jnp.where(_sb_before(top - j), g * (1.0 - beta) - beta * below, 0.0) * 0.125).astype(BF16)
            dk_ref[pl.ds(off, BLK), :] += _dot_tn(dz, qu)
            return pc + jnp.sum(g, axis=1, keepdims=True), dq + _dot(dz, k)

        dq = lax.fori_loop(top + 1 - visited, top + 1, up, (zc, jnp.zeros((SB_ROWS, LANES), F32)))[1]
        dq_ref[...] = _sb_unstack(dq, lane_h)

    col = lambda c0: pl.BlockSpec((s_len, LANES), lambda hp, i: (0, c0 + hp))
    blk = pl.BlockSpec((qrows, LANES), lambda hp, i: (i, hp))
    acc = pl.BlockSpec((s_len, LANES), lambda hp, i: (0, hp))
    return _call(
        body, grid=(2, s_len // qrows), name=name, carry=carry, in_specs=[blk, col(2), col(4), blk],
        out_specs=[blk, acc, acc], out_shape=[jax.ShapeDtypeStruct((s_len, SB_W), F32)] * 3,
        scratch_shapes=[pltpu.VMEM((nq, SB_ROWS, LANES), F32), pltpu.VMEM((nq, SB_ROWS, LANES), F32)],
        vmem_mb=56, args=(u, u, u, dcat))


CV_T = 512
CV_H = 32


def _cv_specs(s_len):
    cur = lambda c: pl.BlockSpec((CV_T, CV_W), lambda i: (i, c))
    prev = lambda c: pl.BlockSpec((CV_H, CV_W), lambda i: (jnp.maximum(i * (CV_T // CV_H) - 1, 0), c))
    nxt = lambda c: pl.BlockSpec((CV_H, CV_W),
                                 lambda i: (jnp.minimum((i + 1) * (CV_T // CV_H), s_len // CV_H - 1), c))
    full = lambda r: pl.BlockSpec((r, CV_W), lambda i: (0, 0))
    return cur, prev, nxt, full


def _glu_into(gp_ref, val_ref, gate_ref, valp_ref, gatep_ref, i):
    gp_ref[0:CV_H, :] = jnp.where(i > 0, valp_ref[...] * jax.nn.sigmoid(gatep_ref[...]), 0.0)
    gp_ref[CV_H:, :] = val_ref[...] * jax.nn.sigmoid(gate_ref[...])


def _cv_fwd(u, cv_w, cv_b, ln_g, ln_b, pw_w, pw_b, *, name):
    s_len = u.shape[0]
    cur, prev, _, full = _cv_specs(s_len)

    def body(val_ref, gate_ref, valp_ref, gatep_ref, w_ref, b_ref, g_ref, be_ref, pw_ref, pb_ref,
             o_ref, c_ref, gp_ref):
        _glu_into(gp_ref, val_ref, gate_ref, valp_ref, gatep_ref, pl.program_id(0))
        acc = jnp.zeros((CV_T, CV_W), F32) + b_ref[...]
        for k in range(CV_K):
            acc = acc + w_ref[k:k + 1, :] * gp_ref[pl.ds(CV_H - CV_K + 1 + k, CV_T), :]
        c_ref[...] = acc
        mu = jnp.mean(acc, axis=-1, keepdims=True)
        xc = acc - mu
        xh = xc * lax.rsqrt(jnp.mean(xc * xc, axis=-1, keepdims=True) + EPS)
        a = xh * g_ref[...] + be_ref[...]
        s = a * jax.nn.sigmoid(a)
        o_ref[...] = (_dot(s.astype(BF16), pw_ref[...]) + pb_ref[...]).astype(BF16)

    return pl.pallas_call(
        body, grid=(s_len // CV_T,), name=name,
        in_specs=[cur(3), cur(4), prev(3), prev(4), full(CV_K), full(1), full(1), full(1), full(CV_W), full(1)],
        out_specs=[cur(0), cur(0)],
        out_shape=[jax.ShapeDtypeStruct((s_len, CV_W), BF16), jax.ShapeDtypeStruct((s_len, CV_W), F32)],
        scratch_shapes=[pltpu.VMEM((CV_T + CV_H, CV_W), F32)], compiler_params=_cp())(
            u, u, u, u, cv_w, cv_b, ln_g, ln_b, pw_w, pw_b)


def _cv_bwd_local(c, dcat, ln_g, ln_b, pw_w, *, name):
    s_len = c.shape[0]
    cur, _, _, full = _cv_specs(s_len)

    def body(c_ref, db_ref, g_ref, be_ref, pw_ref, dc_ref, dpw_ref, vec_ref):
        i = pl.program_id(0)
        cv = c_ref[...]
        db = db_ref[...]
        mu = jnp.mean(cv, axis=-1, keepdims=True)
        xc = cv - mu
        rstd = lax.rsqrt(jnp.mean(xc * xc, axis=-1, keepdims=True) + EPS)
        xh = xc * rstd
        a = xh * g_ref[...] + be_ref[...]
        sg = jax.nn.sigmoid(a)
        s = a * sg
        dbb = db.astype(BF16)
        ds = _dot_nt(dbb, pw_ref[...])
        da = ds * (sg * (1.0 + a * (1.0 - sg)))
        dxh = da * g_ref[...]
        dc_ref[...] = rstd * (dxh - jnp.mean(dxh, axis=-1, keepdims=True)
                              - xh * jnp.mean(dxh * xh, axis=-1, keepdims=True))
        _acc_out(dpw_ref, i, _dot_tn(s.astype(BF16), dbb))
        _acc_out(vec_ref, i, jnp.concatenate([_rowsum8(db), _rowsum8(da * xh), _rowsum8(da)], axis=0))

    return pl.pallas_call(
        body, grid=(s_len // CV_T,), name=name,
        in_specs=[cur(0), cur(1), full(1), full(1), full(CV_W)],
        out_specs=[cur(0), full(CV_W), full(24)],
        out_shape=[jax.ShapeDtypeStruct((s_len, CV_W), F32), jax.ShapeDtypeStruct((CV_W, CV_W), F32),
                   jax.ShapeDtypeStruct((24, CV_W), F32)], compiler_params=_cp())(c, dcat, ln_g, ln_b, pw_w)


def _cv_bwd_conv(u, dc, cv_w, *, name):
    s_len = u.shape[0]
    cur, prev, nxt, full = _cv_specs(s_len)
    last = s_len // CV_T - 1

    def body(val_ref, gate_ref, valp_ref, gatep_ref, dc_ref, dcn_ref, w_ref, du_ref, dw_ref, dbias_ref,
             gp_ref, dcp_ref):
        i = pl.program_id(0)
        _glu_into(gp_ref, val_ref, gate_ref, valp_ref, gatep_ref, i)
        dcv = dc_ref[...]
        dcp_ref[0:CV_T, :] = dcv
        dcp_ref[CV_T:, :] = jnp.where(i < last, dcn_ref[...], 0.0)
        dg = jnp.zeros((CV_T, CV_W), F32)
        parts = []
        for k in range(CV_K):
            dg = dg + w_ref[k:k + 1, :] * dcp_ref[pl.ds(CV_K - 1 - k, CV_T), :]
            parts.append(_rowsum8(dcv * gp_ref[pl.ds(CV_H - CV_K + 1 + k, CV_T), :]))
        _acc_out(dw_ref, i, jnp.concatenate(parts, axis=0))
        _acc_out(dbias_ref, i, _rowsum8(dcv))
        val = val_ref[...]
        sg = jax.nn.sigmoid(gate_ref[...])
        du_ref[:, 0:CV_W] = (dg * sg).astype(BF16)
        du_ref[:, CV_W:] = (dg * val * sg * (1.0 - sg)).astype(BF16)

    return pl.pallas_call(
        body, grid=(s_len // CV_T,), name=name,
        in_specs=[cur(3), cur(4), prev(3), prev(4), cur(0), nxt(0), full(CV_K)],
        out_specs=[pl.BlockSpec((CV_T, 2 * CV_W), lambda i: (i, 0)), full(CV_K * 8), full(8)],
        out_shape=[jax.ShapeDtypeStruct((s_len, 2 * CV_W), BF16), jax.ShapeDtypeStruct((CV_K * 8, CV_W), F32),
                   jax.ShapeDtypeStruct((8, CV_W), F32)],
        scratch_shapes=[pltpu.VMEM((CV_T + CV_H, CV_W), F32), pltpu.VMEM((CV_T + CV_H, CV_W), F32)],
        compiler_params=_cp())(u, u, u, u, dc, dc, cv_w)


def _rope_tables(pos_col, inv_freq_row, *, name):
    s_len = pos_col.shape[0]

    def body(p_ref, f_ref, cos_ref, sin_ref):
        ang = p_ref[...].astype(F32) * f_ref[...]
        lane = lax.broadcasted_iota(jnp.int32, (s_len, LANES), 1)
        sn = jnp.sin(ang)
        cos_ref[...] = jnp.cos(ang)
        sin_ref[...] = jnp.where(lane % HD < HD // 2, -sn, sn)

    return pl.pallas_call(body, name=name, out_shape=[jax.ShapeDtypeStruct((s_len, LANES), F32)] * 2,
                          compiler_params=_cp())(pos_col, inv_freq_row)


def _rot_half(x):
    lane = lax.broadcasted_iota(jnp.int32, x.shape, 1)
    return jnp.where(lane % HD < HD // 2, pltpu.roll(x, LANES - HD // 2, 1), pltpu.roll(x, HD // 2, 1))


def _permute_rows(dst_ref, src_ref, d, dtype):
    s_len = src_ref.shape[0]
    seg = s_len // d
    if d == 1:
        dst_ref[...] = src_ref[...].astype(dtype)
        return
    for r in range(d):
        dst_ref[r * seg:(r + 1) * seg, :] = src_ref[pl.ds(r, seg, stride=d), :].astype(dtype)


def _unpermute_rows(dst_ref, src_ref, d):
    s_len = src_ref.shape[0]
    seg = s_len // d
    if d == 1:
        dst_ref[...] = src_ref[...]
        return
    for r in range(d):
        dst_ref[pl.ds(r, seg, stride=d), :] = src_ref[r * seg:(r + 1) * seg, :]


def _rope_perm(u, cos, sin, *, name):
    s_len = u.shape[0]

    def body(x_ref, cos_ref, sin_ref, o_ref, scr):
        a = pl.program_id(0)
        x = x_ref[...]
        rot = a < 2
        scr[...] = x * jnp.where(rot, cos_ref[...], 1.0) + _rot_half(x) * jnp.where(rot, sin_ref[...], 0.0)
        for n, d in enumerate(DILATIONS):
            _permute_rows(o_ref.at[n], scr, d, BF16)

    tab = pl.BlockSpec((s_len, LANES), lambda a, cb: (0, 0))
    return pl.pallas_call(
        body, grid=(3, 4), name=name,
        in_specs=[pl.BlockSpec((s_len, LANES), lambda a, cb: (0, 10 + 4 * a + cb)), tab, tab],
        out_specs=pl.BlockSpec((None, 3, s_len, LANES), lambda a, cb: (a, 0, 0, cb)),
        out_shape=jax.ShapeDtypeStruct((3, 3, s_len, DL_W), BF16),
        scratch_shapes=[pltpu.VMEM((s_len, LANES), F32)], compiler_params=_cp())(u, cos, sin)


DL_UNROLL = 4


def _dl_band(rows):
    lane = lax.broadcasted_iota(jnp.int32, (rows, LANES), 1)
    row = lax.broadcasted_iota(jnp.int32, (rows, LANES), 0) % BLK
    return lane <= row, lane >= row


def _dl_first(s_len, n, i):
    nb = jnp.where(n == 0, s_len // BLK, jnp.where(n == 1, s_len // (BLK * DILATIONS[1]),
                                                   s_len // (BLK * DILATIONS[2])))
    return lax.rem(i, nb) == 0


def _stack_heads(x, lane_h):
    return jnp.concatenate([jnp.where(lane_h, x, 0.0), jnp.where(lane_h, 0.0, x)], axis=0).astype(BF16)


def _dl_rows(i):
    cur = pl.ds(pl.multiple_of(i * BLK, BLK), BLK)
    prev = pl.ds(pl.multiple_of(jnp.maximum(i - 1, 0) * BLK, BLK), BLK)
    return cur, prev


def _dl_in_specs(s_len):
    return [pl.BlockSpec((None, None, s_len, LANES), functools.partial(lambda a, n, hp: (a, n, 0, hp), a))
            for a in range(3)]


def _dl_fwd(qkv, *, name, carry=None):
    s_len = qkv.shape[2]

    def body(q_ref, k_ref, v_ref, o_ref, l_ref):
        n = pl.program_id(0)
        lane_h = _head_masks()[2]
        band_c, band_p = _dl_band(2 * BLK)
        ones = jnp.ones((BLK, LANES), BF16)

        @pl.loop(0, s_len // BLK, step=DL_UNROLL)
        def _(i0):
            blocks = [i0 + t for t in range(DL_UNROLL)]
            rows = [_dl_rows(i) for i in blocks]
            scores = []
            for cur, prev in rows:
                qs = _stack_heads(q_ref[cur, :] * 0.125, lane_h)
                scores.append((_dot_nt(qs, k_ref[cur, :]), _dot_nt(qs, k_ref[prev, :])))
            probs = []
            for i, (sc, sp) in zip(blocks, scores):
                sc = jnp.where(band_c, sc, NEG_INF)
                sp = jnp.where(jnp.logical_and(band_p, jnp.logical_not(_dl_first(s_len, n, i))), sp, NEG_INF)
                m = jnp.max(jnp.maximum(sc, sp), axis=1, keepdims=True)
                probs.append((jnp.exp(sc - m).astype(BF16), jnp.exp(sp - m).astype(BF16), m))
            for (cur, prev), (pc, pp, m) in zip(rows, probs):
                r = (_dot(pc, jnp.concatenate([v_ref[cur, :], ones], axis=1))
                     + _dot(pp, jnp.concatenate([v_ref[prev, :], ones], axis=1)))
                den = jnp.where(lane_h, r[:BLK, LANES:], r[BLK:, LANES:])
                o_ref[cur, :] = jnp.where(lane_h, r[:BLK, :LANES], r[BLK:, :LANES]) / den
                l_ref[cur, :] = jnp.where(lane_h, m[:BLK], m[BLK:]) + jnp.log(den)

    out = pl.BlockSpec((None, s_len, LANES), lambda n, hp: (n, 0, hp))
    return _call(
        body, grid=(3, 4), name=name, carry=carry, in_specs=_dl_in_specs(s_len), out_specs=[out, out],
        out_shape=[jax.ShapeDtypeStruct((3, s_len, DL_W), F32)] * 2, args=(qkv, qkv, qkv))


def _dl_mix(o_p, l_p, *, name, carry=None):
    s_len = o_p.shape[1]

    def body(o_ref, l_ref, ob_ref, of_ref, lt_ref, o_scr, l_scr):
        n = pl.program_id(1)
        for k, d in enumerate(DILATIONS):
            @pl.when(n == k)
            def _(k=k, d=d):
                _unpermute_rows(o_scr.at[k], o_ref, d)
                _unpermute_rows(l_scr.at[k], l_ref, d)

        @pl.when(n == 2)
        def _():
            l0, l1, l2 = l_scr[0], l_scr[1], l_scr[2]
            m = jnp.maximum(jnp.maximum(l0, l1), l2)
            e0, e1, e2 = jnp.exp(l0 - m), jnp.exp(l1 - m), jnp.exp(l2 - m)
            den = e0 + e1 + e2
            o = (e0 / den) * o_scr[0] + (e1 / den) * o_scr[1] + (e2 / den) * o_scr[2]
            of_ref[...] = o
            ob_ref[...] = o.astype(BF16)
            lt_ref[...] = m + jnp.log(den)

    inb = pl.BlockSpec((None, s_len, LANES), lambda cb, n: (n, 0, cb))
    outb = pl.BlockSpec((s_len, LANES), lambda cb, n: (0, cb))
    return _call(
        body, grid=(4, 3), name=name, carry=carry, in_specs=[inb, inb], out_specs=[outb, outb, outb],
        out_shape=[jax.ShapeDtypeStruct((s_len, DL_W), BF16), jax.ShapeDtypeStruct((s_len, DL_W), F32),
                   jax.ShapeDtypeStruct((s_len, DL_W), F32)],
        scratch_shapes=[pltpu.VMEM((3, s_len, LANES), F32), pltpu.VMEM((3, s_len, LANES), F32)], args=(o_p, l_p))


def _dl_bwd_prep(dcat, o, lse, *, name):
    s_len = o.shape[0]

    def body(do_ref, o_ref, l_ref, dop_ref, st_ref, d_scr):
        n = pl.program_id(1)

        @pl.when(n == 0)
        def _():
            r0 = lax.broadcasted_iota(jnp.int32, (LANES, LANES), 0) // HD
            r1 = lax.broadcasted_iota(jnp.int32, (LANES, LANES), 1) // HD
            d_scr[...] = _dot_hilo(do_ref[...] * o_ref[...], (r0 == r1).astype(BF16))

        for k, d in enumerate(DILATIONS):
            @pl.when(n == k)
            def _(d=d):
                _permute_rows(dop_ref, do_ref, d, BF16)
                _permute_rows(st_ref.at[0], d_scr, d, F32)
                _permute_rows(st_ref.at[1], l_ref, d, F32)

    nat = lambda c0: pl.BlockSpec((s_len, LANES), lambda cb, n: (0, c0 + cb))
    return pl.pallas_call(
        body, grid=(4, 3), name=name, in_specs=[nat(4), nat(0), nat(0)],
        out_specs=[pl.BlockSpec((None, s_len, LANES), lambda cb, n: (n, 0, cb)),
                   pl.BlockSpec((2, None, s_len, LANES), lambda cb, n: (0, n, 0, cb))],
        out_shape=[jax.ShapeDtypeStruct((3, s_len, DL_W), BF16), jax.ShapeDtypeStruct((2, 3, s_len, DL_W), F32)],
        scratch_shapes=[pltpu.VMEM((s_len, LANES), F32)], compiler_params=_cp())(dcat, o, lse)


def _dl_bwd(qkv, dop, stats, *, name, carry=None):
    s_len = qkv.shape[2]

    def body(q_ref, k_ref, v_ref, do_ref, st_ref, cur_ref, prev_ref):
        n = pl.program_id(0)
        lane_h = _head_masks()[2]
        band_c, band_p = _dl_band(2 * BLK)

        def per_head(x):
            xr = pltpu.roll(x, HD, 1)
            return jnp.concatenate([jnp.where(lane_h, x, xr), jnp.where(lane_h, xr, x)], axis=0)

        @pl.loop(0, s_len // BLK, step=DL_UNROLL)
        def _(i0):
            blocks = [i0 + t for t in range(DL_UNROLL)]
            rows = [_dl_rows(i) for i in blocks]
            stage1 = []
            for cur, prev in rows:
                qs = _stack_heads(q_ref[cur, :] * 0.125, lane_h)
                dos = _stack_heads(do_ref[cur, :], lane_h)
                kc, kp, vc, vp = k_ref[cur, :], k_ref[prev, :], v_ref[cur, :], v_ref[prev, :]
                stage1.append((qs, dos, _dot_nt(qs, kc), _dot_nt(qs, kp), _dot_nt(dos, vc), _dot_nt(dos, vp)))
            stage2 = []
            for i, (cur, prev), (qs, dos, sc, sp, dpc, dpp) in zip(blocks, rows, stage1):
                lse, delta = per_head(st_ref[1, cur, :]), per_head(st_ref[0, cur, :])
                pc = jnp.where(band_c, jnp.exp(sc - lse), 0.0)
                pp = jnp.where(jnp.logical_and(band_p, jnp.logical_not(_dl_first(s_len, n, i))), jnp.exp(sp - lse), 0.0)
                stage2.append((pc.astype(BF16), pp.astype(BF16), (pc * (dpc - delta)).astype(BF16),
                               (pp * (dpp - delta)).astype(BF16)))
            for (cur, prev), (qs, dos, *_), (pc, pp, dsc, dsp) in zip(rows, stage1, stage2):
                dq = _dot(dsc, k_ref[cur, :]) + _dot(dsp, k_ref[prev, :])
                cur_ref[0, cur, :] = jnp.where(lane_h, dq[:BLK], dq[BLK:]) * 0.125
                cur_ref[1, cur, :] = _dot_tn(dsc, qs)
                cur_ref[2, cur, :] = _dot_tn(pc, dos)
                prev_ref[0, cur, :] = _dot_tn(dsp, qs)
                prev_ref[1, cur, :] = _dot_tn(pp, dos)

    return _call(
        body, grid=(3, 4), name=name, carry=carry,
        in_specs=_dl_in_specs(s_len) + [pl.BlockSpec((None, s_len, LANES), lambda n, hp: (n, 0, hp)),
                                        pl.BlockSpec((2, None, s_len, LANES), lambda n, hp: (0, n, 0, hp))],
        out_specs=[pl.BlockSpec((3, None, s_len, LANES), lambda n, hp: (0, n, 0, hp)),
                   pl.BlockSpec((2, None, s_len, LANES), lambda n, hp: (0, n, 0, hp))],
        out_shape=[jax.ShapeDtypeStruct((3, 3, s_len, DL_W), F32), jax.ShapeDtypeStruct((2, 3, s_len, DL_W), F32)],
        vmem_mb=56, args=(qkv, qkv, qkv, dop, stats))


def _dl_bwd_finish(cur, prev, cos, sin, *, name):
    s_len = cur.shape[2]

    def body(c_ref, p_ref, cos_ref, sin_ref, o_ref, p_scr, u_scr, acc):
        a, n = pl.program_id(0), pl.program_id(2)
        has_prev = jnp.where(a > 0, 1.0, 0.0)
        p_scr[...] = c_ref[...]
        p_scr[0:s_len - BLK, :] += has_prev * p_ref[BLK:, :]
        for k, d in enumerate(DILATIONS):
            @pl.when(n == k)
            def _(k=k, d=d):
                if k == 0:
                    acc[...] = p_scr[...]
                else:
                    _unpermute_rows(u_scr, p_scr, d)
                    acc[...] += u_scr[...]

        @pl.when(n == 2)
        def _():
            dy = acc[...]
            rot = a < 2
            o_ref[...] = (dy * jnp.where(rot, cos_ref[...], 1.0)
                          + _rot_half(dy * jnp.where(rot, sin_ref[...], 0.0))).astype(BF16)

    tab = pl.BlockSpec((s_len, LANES), lambda a, cb, n: (0, 0))
    return pl.pallas_call(
        body, grid=(3, 4, 3), name=name,
        in_specs=[pl.BlockSpec((None, None, s_len, LANES), lambda a, cb, n: (a, n, 0, cb)),
                  pl.BlockSpec((None, None, s_len, LANES), lambda a, cb, n: (jnp.maximum(a - 1, 0), n, 0, cb)),
                  tab, tab],
        out_specs=pl.BlockSpec((s_len, LANES), lambda a, cb, n: (0, 4 * a + cb)),
        out_shape=jax.ShapeDtypeStruct((s_len, 3 * DL_W), BF16),
        scratch_shapes=[pltpu.VMEM((s_len, LANES), F32)] * 3, compiler_params=_cp())(cur, prev, cos, sin)


XA_T = 256


def _xa_probs(q, k):
    s = _dot_nt(q, k) * (X_HD ** -0.5)
    e = jnp.exp(s - jnp.max(s, axis=1, keepdims=True))
    return e / jnp.sum(e, axis=1, keepdims=True)


def _xa_fwd(q, k, v, *, name):
    s_len, d = q.shape
    nm = k.shape[0]

    def body(q_ref, k_ref, v_ref, o_ref):
        for h in range(X_HEADS):
            cs = slice(h * X_HD, (h + 1) * X_HD)
            p = _xa_probs(q_ref[:, cs], k_ref[:, cs])
            o_ref[:, cs] = _dot(p.astype(BF16), v_ref[:, cs]).astype(BF16)

    row = pl.BlockSpec((XA_T, d), lambda i: (i, 0))
    full = pl.BlockSpec((nm, d), lambda i: (0, 0))
    return pl.pallas_call(body, grid=(s_len // XA_T,), name=name, in_specs=[row, full, full], out_specs=row,
                          out_shape=jax.ShapeDtypeStruct((s_len, d), BF16), compiler_params=_cp())(q, k, v)


def _xa_bwd(q, k, v, do, *, name, carry=None):
    s_len, d = q.shape
    nm = k.shape[0]

    def body(q_ref, k_ref, v_ref, do_ref, dq_ref, dk_ref, dv_ref):
        i = pl.program_id(0)
        for h in range(X_HEADS):
            cs = slice(h * X_HD, (h + 1) * X_HD)
            qh, kh, vh, doh = q_ref[:, cs], k_ref[:, cs], v_ref[:, cs], do_ref[:, cs]
            p = _xa_probs(qh, kh)
            dp = _dot_nt(doh, vh)
            ds = (p * (dp - jnp.sum(dp * p, axis=1, keepdims=True)) * (X_HD ** -0.5)).astype(BF16)
            dq_ref[:, cs] = _dot(ds, kh).astype(BF16)
            dkh, dvh = _dot_tn(ds, qh), _dot_tn(p.astype(BF16), doh)

            @pl.when(i == 0)
            def _(cs=cs, dkh=dkh, dvh=dvh):
                dk_ref[:, cs] = dkh
                dv_ref[:, cs] = dvh

            @pl.when(i > 0)
            def _(cs=cs, dkh=dkh, dvh=dvh):
                dk_ref[:, cs] += dkh
                dv_ref[:, cs] += dvh

    row = pl.BlockSpec((XA_T, d), lambda i: (i, 0))
    full = pl.BlockSpec((nm, d), lambda i: (0, 0))
    return _call(
        body, grid=(s_len // XA_T,), name=name, carry=carry, in_specs=[row, full, full, row],
        out_specs=[row, full, full],
        out_shape=[jax.ShapeDtypeStruct((s_len, d), BF16), jax.ShapeDtypeStruct((nm, d), F32),
                   jax.ShapeDtypeStruct((nm, d), F32)], args=(q, k, v, do))


FF_TM, FF_TN, FF_H = 512, 256, 8
GELU_K, GELU_C = 0.7978845608028654, 0.044715


FF_STRIP = 64


def _ff_conv(e_ref, w_ref, b_ref, rows, r0=0):
    return (w_ref[0:1, :] * e_ref[pl.ds(FF_H - 2 + r0, rows), :] + w_ref[1:2, :] * e_ref[pl.ds(FF_H - 1 + r0, rows), :]
            + w_ref[2:3, :] * e_ref[pl.ds(FF_H + r0, rows), :] + b_ref[...])


def _strips(total, size):
    return [(r0, min(size, total - r0)) for r0 in range(0, total, size)]


def _ff_gate_fwd(up, conv_w, conv_b, *, name, carry=None):
    s_len = up.shape[0]
    nj = D_FF // FF_TN

    def body(g_ref, v_ref, gp_ref, vp_ref, wg_ref, wv_ref, bg_ref, bv_ref, o_ref, eg, ev):
        i = pl.program_id(0)
        for e, cur, prev in ((eg, g_ref, gp_ref), (ev, v_ref, vp_ref)):
            e[0:FF_H, :] = jnp.where(i > 0, prev[...], 0.0)
            e[FF_H:, :] = cur[...]
        for r0, rows in _strips(FF_TM, FF_STRIP):
            gate = _ff_conv(eg, wg_ref, bg_ref, rows, r0)
            val = _ff_conv(ev, wv_ref, bv_ref, rows, r0)
            t = jnp.tanh(GELU_K * (gate + GELU_C * gate * gate * gate))
            o_ref[r0:r0 + rows, :] = (0.5 * gate * (1.0 + t) * val).astype(BF16)

    cur = lambda c0: pl.BlockSpec((FF_TM, FF_TN), lambda i, j: (i, c0 + j))
    prev = lambda c0: pl.BlockSpec((FF_H, FF_TN), lambda i, j: (jnp.maximum(i * (FF_TM // FF_H) - 1, 0), c0 + j))
    par = lambda r, c0: pl.BlockSpec((r, FF_TN), lambda i, j: (0, c0 + j))
    return _call(
        body, grid=(s_len // FF_TM, nj), name=name, carry=carry,
        in_specs=[cur(0), cur(nj), prev(0), prev(nj), par(3, 0), par(3, nj), par(1, 0), par(1, nj)],
        out_specs=[cur(0)], out_shape=[jax.ShapeDtypeStruct((s_len, D_FF), BF16)],
        scratch_shapes=[pltpu.VMEM((FF_TM + FF_H, FF_TN), F32)] * 2,
        args=(up, up, up, up, conv_w, conv_w, conv_b, conv_b))


def _ff_gate_bwd(up, dact, conv_w, conv_b, *, name, carry=None):
    s_len = up.shape[0]
    nj = D_FF // FF_TN
    last = s_len // FF_TM - 1
    ext = FF_TM + FF_H

    def body(g_ref, v_ref, gp_ref, vp_ref, gn_ref, vn_ref, da_ref, dan_ref, wg_ref, wv_ref, bg_ref, bv_ref,
             dg_ref, dv_ref, dw_ref, db_ref, eg, ev, sg, sv):
        i = pl.program_id(1)
        for e, cur, prev, nxt in ((eg, g_ref, gp_ref, gn_ref), (ev, v_ref, vp_ref, vn_ref)):
            e[0:FF_H, :] = jnp.where(i > 0, prev[...], 0.0)
            e[FF_H:FF_H + FF_TM, :] = cur[...]
            e[FF_H + FF_TM:, :] = nxt[...]
        for r0, rows in _strips(ext, FF_STRIP):
            gate = _ff_conv(eg, wg_ref, bg_ref, rows, r0)
            val = _ff_conv(ev, wv_ref, bv_ref, rows, r0)
            dact = da_ref[r0:r0 + rows, :] if r0 < FF_TM else jnp.where(i < last, dan_ref[...], 0.0)
            t = jnp.tanh(GELU_K * (gate + GELU_C * gate * gate * gate))
            half = 0.5 * (1.0 + t)
            dgelu = half + 0.5 * gate * (1.0 - t * t) * GELU_K * (1.0 + 3.0 * GELU_C * gate * gate)
            sg[r0:r0 + rows, :] = dact * val * dgelu
            sv[r0:r0 + rows, :] = dact * (gate * half)
        for part, (s, e, w_ref, out) in enumerate(((sg, eg, wg_ref, dg_ref), (sv, ev, wv_ref, dv_ref))):
            taps, bias = [jnp.zeros((8, FF_TN), F32)] * 3, jnp.zeros((8, FF_TN), F32)
            for r0, rows in _strips(FF_TM, FF_STRIP):
                d0 = s[pl.ds(r0, rows), :]
                out[r0:r0 + rows, :] = (w_ref[2:3, :] * d0 + w_ref[1:2, :] * s[pl.ds(r0 + 1, rows), :]
                                        + w_ref[0:1, :] * s[pl.ds(r0 + 2, rows), :]).astype(BF16)
                taps = [taps[k] + _rowsum8(d0 * e[pl.ds(FF_H - 2 + k + r0, rows), :]) for k in range(3)]
                bias = bias + _rowsum8(d0)
            _acc_out(dw_ref.at[part], i, jnp.concatenate(taps, axis=0))
            _acc_out(db_ref.at[part], i, bias)

    cur = lambda c0: pl.BlockSpec((FF_TM, FF_TN), lambda j, i: (i, c0 + j))
    prev = lambda c0: pl.BlockSpec((FF_H, FF_TN), lambda j, i: (jnp.maximum(i * (FF_TM // FF_H) - 1, 0), c0 + j))
    nxt = lambda c0: pl.BlockSpec(
        (FF_H, FF_TN), lambda j, i: (jnp.minimum((i + 1) * (FF_TM // FF_H), s_len // FF_H - 1), c0 + j))
    par = lambda r, c0: pl.BlockSpec((r, FF_TN), lambda j, i: (0, c0 + j))
    return _call(
        body, grid=(nj, s_len // FF_TM), name=name, carry=carry,
        in_specs=[cur(0), cur(nj), prev(0), prev(nj), nxt(0), nxt(nj), cur(0), nxt(0),
                  par(3, 0), par(3, nj), par(1, 0), par(1, nj)],
        out_specs=[cur(0), cur(0), pl.BlockSpec((2, 24, FF_TN), lambda j, i: (0, 0, j)),
                   pl.BlockSpec((2, 8, FF_TN), lambda j, i: (0, 0, j))],
        out_shape=[jax.ShapeDtypeStruct((s_len, D_FF), BF16), jax.ShapeDtypeStruct((s_len, D_FF), BF16),
                   jax.ShapeDtypeStruct((2, 24, D_FF), F32), jax.ShapeDtypeStruct((2, 8, D_FF), F32)],
        scratch_shapes=[pltpu.VMEM((FF_TM + 2 * FF_H, FF_TN), F32)] * 2 + [pltpu.VMEM((ext, FF_TN), F32)] * 2,
        args=(up, up, up, up, up, up, dact, dact, conv_w, conv_w, conv_b, conv_b))


def _place():
    x, y, c = lax.axis_index("x"), lax.axis_index("y"), lax.axis_index("c")
    return x, y, c, [(1 - x, y), (x, 1 - y), (1 - x, 1 - y)]


def _remote(src, dst, send_sem, recv_sem, dev):
    return pltpu.make_async_remote_copy(src_ref=src, dst_ref=dst, send_sem=send_sem, recv_sem=recv_sem,
                                        device_id=dev, device_id_type=MESH)


_ANY = pl.BlockSpec(memory_space=pl.ANY)


N_SEMS = 8
SEM_BASE_2 = 4


class _Exchange:
    def __init__(self, operands, out_shapes, start, wait, aliases=None):
        self.operands, self.out_shapes, self.start, self.wait = list(operands), list(out_shapes), start, wait
        self.aliases = aliases or {}


def _sem_scratch():
    return [pltpu.SemaphoreType.DMA((N_SEMS,)), pltpu.SemaphoreType.DMA((N_SEMS,)), pltpu.SemaphoreType.DMA]


def _run_exchange(ex, *, name):
    k, n = len(ex.operands), len(ex.out_shapes)

    def body(*refs):
        ins, outs, sems = refs[:k], refs[k:k + n], refs[k + n:]
        ex.start(ins, outs, *sems)
        ex.wait(ins, outs, *sems)

    return pl.pallas_call(body, name=name, in_specs=[_ANY] * k, out_specs=[_ANY] * n, out_shape=ex.out_shapes,
                          scratch_shapes=_sem_scratch(), input_output_aliases=ex.aliases,
                          compiler_params=_cp(16))(*ex.operands)


def _call(body, *, grid, in_specs, out_specs, out_shape, args, name, scratch_shapes=(), vmem_mb=48, carry=None):
    scratch_shapes = list(scratch_shapes)
    if carry is None:
        return pl.pallas_call(body, grid=grid, name=name, in_specs=in_specs, out_specs=out_specs, out_shape=out_shape,
                              scratch_shapes=scratch_shapes, compiler_params=_cp(vmem_mb))(*args)
    n_in, n_out, n_scr = len(in_specs), len(out_shape), len(scratch_shapes)
    k_in, k_out = len(carry.operands), len(carry.out_shapes)

    def wrapped(*refs):
        ins, refs = refs[:n_in], refs[n_in:]
        cin, refs = refs[:k_in], refs[k_in:]
        outs, refs = refs[:n_out], refs[n_out:]
        cout, refs = refs[:k_out], refs[k_out:]
        scratch, sems = refs[:n_scr], refs[n_scr:]
        ids = [pl.program_id(a) for a in range(len(grid))]
        first = functools.reduce(jnp.logical_and, [i == 0 for i in ids])
        last = functools.reduce(jnp.logical_and, [i == g - 1 for i, g in zip(ids, grid)])

        @pl.when(first)
        def _():
            carry.start(cin, cout, *sems)

        body(*ins, *outs, *scratch)

        @pl.when(last)
        def _():
            carry.wait(cin, cout, *sems)

    aliases = {n_in + i: n_out + o for i, o in carry.aliases.items()}
    return pl.pallas_call(
        wrapped, grid=grid, name=name, in_specs=list(in_specs) + [_ANY] * k_in,
        out_specs=list(out_specs) + [_ANY] * k_out, out_shape=list(out_shape) + carry.out_shapes,
        scratch_shapes=scratch_shapes + _sem_scratch(), input_output_aliases=aliases,
        compiler_params=_cp(vmem_mb))(*args, *carry.operands)


def _half_rows(ref_rows, c):
    half = ref_rows // 2
    return pl.ds(c * half, half)


def _ex_join(a, b):
    ka, na = len(a.operands), len(a.out_shapes)

    def start(ins, outs, *sems):
        a.start(ins[:ka], outs[:na], *sems)
        b.start(ins[ka:], outs[na:], *sems)

    def wait(ins, outs, *sems):
        a.wait(ins[:ka], outs[:na], *sems)
        b.wait(ins[ka:], outs[na:], *sems)

    aliases = dict(a.aliases)
    aliases.update({ka + i: na + o for i, o in b.aliases.items()})
    return _Exchange(a.operands + b.operands, a.out_shapes + b.out_shapes, start, wait, aliases)


def _ex_gather(pack, r0, rl, base=0):
    def copies(ins, outs, send, recv):
        x, y, c, chips = _place()
        rows = _half_rows(rl, c)
        src = ins[0].at[pl.ds(r0 + c * (rl // 2), rl // 2)]
        sends = [_remote(src, outs[0].at[2 * x + y, rows], send.at[base + k], recv.at[base + k], (px, py, c))
                 for k, (px, py) in enumerate(chips)]
        lands = [_remote(src, outs[0].at[2 * px + py, rows], send.at[base + k], recv.at[base + k], (px, py, c))
                 for k, (px, py) in enumerate(chips)]
        return sends, lands

    def mine(ins, outs, local):
        x, y, _, _ = _place()
        return pltpu.make_async_copy(ins[0].at[pl.ds(r0, rl)], outs[0].at[2 * x + y], local)

    def start(ins, outs, send, recv, local):
        mine(ins, outs, local).start()
        for cp in copies(ins, outs, send, recv)[0]:
            cp.start()

    def wait(ins, outs, send, recv, local):
        sends, lands = copies(ins, outs, send, recv)
        for cp in lands:
            cp.wait_recv()
        for cp in sends:
            cp.wait_send()
        mine(ins, outs, local).wait()

    return _Exchange([pack], [jax.ShapeDtypeStruct((4, rl, pack.shape[1]), pack.dtype)], start, wait)


def _ex_gather_forward(g, base=0):
    rl = g.shape[1]

    def copies(outs, send, recv):
        x, y, c, chips = _place()
        slabs = [(outs[0].at[2 * px + py, _half_rows(rl, c)], outs[0].at[2 * px + py, _half_rows(rl, 1 - c)])
                 for px, py in chips]
        sends = [_remote(a, a, send.at[base + k], recv.at[base + k], (x, y, 1 - c)) for k, (a, _) in enumerate(slabs)]
        lands = [_remote(b, b, send.at[base + k], recv.at[base + k], (x, y, 1 - c)) for k, (_, b) in enumerate(slabs)]
        return sends, lands

    def start(ins, outs, send, recv, local):
        for cp in copies(outs, send, recv)[0]:
            cp.start()

    def wait(ins, outs, send, recv, local):
        sends, lands = copies(outs, send, recv)
        for cp in lands:
            cp.wait_recv()
        for cp in sends:
            cp.wait_send()

    return _Exchange([g], [jax.ShapeDtypeStruct(g.shape, g.dtype)], start, wait, aliases={0: 0})


def _ex_swap_halves(gw, base=0):
    nb, rl, d = gw.shape

    def copies(ins, outs, send, recv):
        x, y, c, _ = _place()
        return [_remote(ins[0].at[j, _half_rows(rl, 1 - c)], outs[0].at[j], send.at[base + j], recv.at[base + j],
                        (x, y, 1 - c)) for j in range(nb)]

    def start(ins, outs, send, recv, local):
        for cp in copies(ins, outs, send, recv):
            cp.start()

    def wait(ins, outs, send, recv, local):
        for cp in copies(ins, outs, send, recv):
            cp.wait()

    return _Exchange([gw], [jax.ShapeDtypeStruct((nb, rl // 2, d), gw.dtype)], start, wait)


def _chip_sum(gw, got, c_arr, *, name):
    nchip, half, d = got.shape
    tr = _tile(half, 512)

    def body(c_ref, a_ref, b_ref, o32_ref, o16_ref):
        s = a_ref[...] + b_ref[...]
        o32_ref[...] = s
        o16_ref[...] = s.astype(BF16)

    blk = pl.BlockSpec((None, tr, d), lambda j, i, c_ref: (j, i, 0))
    return pl.pallas_call(
        body, name=name,
        grid_spec=pltpu.PrefetchScalarGridSpec(
            num_scalar_prefetch=1, grid=(nchip, half // tr),
            in_specs=[pl.BlockSpec((None, tr, d), lambda j, i, c_ref: (j, c_ref[0] * (half // tr) + i, 0)), blk],
            out_specs=[blk, blk]),
        out_shape=[jax.ShapeDtypeStruct((nchip, half, d), F32), jax.ShapeDtypeStruct((nchip, half, d), BF16)],
        compiler_params=_cp())(c_arr, gw, got)


def _ex_scatter(s16, base=0):
    def copies(ins, outs, send, recv):
        x, y, c, chips = _place()
        return [_remote(ins[0].at[2 * px + py], outs[0].at[k], send.at[base + k], recv.at[base + k], (px, py, c))
                for k, (px, py) in enumerate(chips)]

    def start(ins, outs, send, recv, local):
        for cp in copies(ins, outs, send, recv):
            cp.start()

    def wait(ins, outs, send, recv, local):
        for cp in copies(ins, outs, send, recv):
            cp.wait()

    return _Exchange([s16], [jax.ShapeDtypeStruct((3,) + s16.shape[1:], s16.dtype)], start, wait)


def _mesh_sum(s32, got, j_arr, *, name):
    _, rl, d = s32.shape
    tr = _tile(rl, 512)

    def body(j_ref, a_ref, b_ref, o_ref):
        o_ref[...] = ((a_ref[...] + b_ref[0].astype(F32)) + b_ref[1].astype(F32)) + b_ref[2].astype(F32)

    return pl.pallas_call(
        body, name=name,
        grid_spec=pltpu.PrefetchScalarGridSpec(
            num_scalar_prefetch=1, grid=(rl // tr,),
            in_specs=[pl.BlockSpec((None, tr, d), lambda i, j_ref: (j_ref[0], i, 0)),
                      pl.BlockSpec((3, tr, d), lambda i, j_ref: (0, i, 0))],
            out_specs=pl.BlockSpec((tr, d), lambda i, j_ref: (i, 0))),
        out_shape=jax.ShapeDtypeStruct((rl, d), F32), compiler_params=_cp())(j_arr, s32, got)


def _ex_share_halves(ghalf):
    half, d = ghalf.shape

    def copies(ins, outs, send, recv, local):
        x, y, c, _ = _place()
        there = outs[0].at[_half_rows(2 * half, c)]
        back = outs[0].at[_half_rows(2 * half, 1 - c)]
        return (_remote(ins[0], there, send.at[0], recv.at[0], (x, y, 1 - c)),
                _remote(ins[0], back, send.at[0], recv.at[0], (x, y, 1 - c)), pltpu.make_async_copy(ins[0], there, local))

    def start(ins, outs, send, recv, local):
        out, _, mine = copies(ins, outs, send, recv, local)
        mine.start()
        out.start()

    def wait(ins, outs, send, recv, local):
        out, back, mine = copies(ins, outs, send, recv, local)
        back.wait_recv()
        out.wait_send()
        mine.wait()

    return _Exchange([ghalf], [jax.ShapeDtypeStruct((2 * half, d), ghalf.dtype)], start, wait)


class _ReduceScatter:
    def __init__(self, gw, c_arr, j_arr, tag):
        self.gw, self.c_arr, self.j_arr, self.tag = gw, c_arr, j_arr, tag

    def swap(self, base=0):
        return _ex_swap_halves(self.gw, base)

    def after_swap(self, got, base=0):
        self.s32, s16 = _chip_sum(self.gw, got, self.c_arr, name=f"rs_chip_sum{self.tag}")
        return _ex_scatter(s16, base)

    def after_scatter(self, got16):
        ghalf = _mesh_sum(self.s32, got16, self.j_arr, name=f"rs_mesh_sum{self.tag}")
        return _run_exchange(_ex_share_halves(ghalf), name=f"rs_share{self.tag}")[0]

    def run(self):
        got, = _run_exchange(self.swap(), name=f"rs_swap{self.tag}")
        got16, = _run_exchange(self.after_swap(got), name=f"rs_scatter{self.tag}")
        return self.after_scatter(got16)


def _all_reduce_small(vec, *, name):
    rows, d = vec.shape

    def body(x_ref, o_ref, gat, send_sems, recv_sems, local_sem):
        x, y, c, chips = _place()
        me, sibling = (x, y, c), (x, y, 1 - c)

        def slot(px, py, pc):
            return gat.at[4 * px + 2 * py + pc]

        def copy(k, block, to, src=None):
            return _remote(slot(*block) if src is None else src, slot(*block), send_sems.at[k], recv_sems.at[k], to)

        mine = pltpu.make_async_copy(x_ref, slot(*me), local_sem)
        mine.start()
        first = [copy(0, me, sibling, src=x_ref)]
        first += [copy(1 + j, me, (*chip, c), src=x_ref) for j, chip in enumerate(chips)]
        for cp in first:
            cp.start()
        passed = [copy(4 + j, (*chip, c), sibling) for j, chip in enumerate(chips)]
        for j, chip in enumerate(chips):
            copy(1 + j, (*chip, c), me).wait_recv()
            passed[j].start()
        copy(0, sibling, me).wait_recv()
        for j, chip in enumerate(chips):
            copy(4 + j, (*chip, 1 - c), me).wait_recv()
        for cp in first + passed:
            cp.wait_send()
        mine.wait()
        acc = gat[0]
        for dev in range(1, 8):
            acc = acc + gat[dev]
        o_ref[...] = acc

    vm = pl.BlockSpec(memory_space=pltpu.VMEM)
    return pl.pallas_call(
        body, name=name, in_specs=[vm], out_specs=vm, out_shape=jax.ShapeDtypeStruct((rows, d), F32),
        scratch_shapes=[pltpu.VMEM((8, rows, d), F32), pltpu.SemaphoreType.DMA((7,)), pltpu.SemaphoreType.DMA((7,)),
                        pltpu.SemaphoreType.DMA],
        compiler_params=_cp(32))(vec)


COL_SHARDED = ("w_in", "ffn_w_up")


def _to_pack_rows(name, shard):
    return shard.reshape(-1, D_MODEL)


def _full_from_blocks(name, blocks):
    rows = blocks.shape[1]
    if name in COL_SHARDED:
        return blocks.reshape(4, D_MODEL, rows).transpose(1, 0, 2).reshape(D_MODEL, 4 * rows)
    return blocks.reshape(4 * rows, D_MODEL)


def _blocks_from_full(name, full):
    if name in COL_SHARDED:
        cols = full.shape[1] // 4
        return full.reshape(D_MODEL, 4, cols).transpose(1, 0, 2).reshape(4, cols, D_MODEL)
    return full.reshape(4, full.shape[0] // 4, D_MODEL)


def _row(v):
    return v.reshape(1, -1)


SMALL = (("mix_norm_pre", (1024,), None), ("cv_w", (31, 256), 1), ("cv_b", (256,), None), ("cv_ln_g", (256,), None),
         ("cv_ln_b", (256,), None), ("cv_pw_w", (256, 256), 0), ("cv_pw_b", (256,), None),
         ("mix_norm_post", (1024,), None), ("x_norm_pre", (1024,), None), ("mem_norm", (1024,), None),
         ("x_norm_post", (1024,), None), ("ffn_norm_pre", (1024,), None), ("ffn_conv_w", (3, 5632), 1),
         ("ffn_conv_b", (5632,), None), ("ffn_norm_post", (1024,), None))
BIG = tuple(n for n, _ in PACK_ROWS)
WEIGHT_ORDER = ("mix_norm_pre", "w_in", "cv_w", "cv_b", "cv_ln_g", "cv_ln_b", "cv_pw_w", "cv_pw_b", "w_out",
                "mix_norm_post", "x_norm_pre", "mem_norm", "x_wq", "x_wk", "x_wv", "x_wo", "x_norm_post",
                "ffn_norm_pre", "ffn_w_up", "ffn_conv_w", "ffn_conv_b", "ffn_w_down", "ffn_norm_post")


def _flat_rows(parts):
    v = jnp.concatenate([p.reshape(-1) for p in parts])
    rows = -(-v.shape[0] // (8 * D_MODEL)) * 8
    return jnp.pad(v, (0, rows * D_MODEL - v.shape[0])).reshape(rows, D_MODEL)


def _small_to_rows(blocks):
    v = jnp.concatenate([b.reshape(-1) for b in blocks])
    return jnp.pad(v, (0, SMALL_ROWS * D_MODEL - v.shape[0])).reshape(SMALL_ROWS, D_MODEL)


def _small_from_rows(rows):
    flat, out, off = rows.reshape(-1), [], 0
    for _, shape, _ in SHARDED_SMALL:
        size = int(np.prod(shape))
        out.append(flat[off:off + size].reshape(shape))
        off += size
    return out


def _chip_block(full, j, shape, axis):
    return lax.slice_in_dim(full, j * shape[axis], (j + 1) * shape[axis], axis=axis)


REST_GROUP = ("w_in", "w_out")
XA_GROUP = ("x_wq", "x_wk", "x_wv", "x_wo")
FFN_GROUP = ("ffn_w_up", "ffn_w_down")


class _Weights:
    FIRST = (0, 768)
    OWN = ((768, 1280), (2048, 1408), (3456, 704))
    NEXT = ((0, 1024), (1024, 1024), (2048, 1408), (3456, 704))
    SLOTS = ("mix_in", "sb_fwd", "dl_fwd", "dl_mix", "ffn_up", "ffn_gate", "ffn_down")

    def __init__(self, packs):
        self.packs, self.pieces, self.landed, self.plan = packs, {}, None, {}
        for slot, piece in zip(self.SLOTS[:3], self.OWN):
            self.plan[(0, slot)] = (0,) + piece
        for l in range(len(packs) - 1):
            for slot, piece in zip(self.SLOTS[3:], self.NEXT):
                self.plan[(l, slot)] = (l + 1,) + piece
        first = _run_exchange(_ex_gather(packs[0], *self.FIRST), name="gather_first")[0]
        self.pieces[(0,) + self.FIRST] = _run_exchange(_ex_gather_forward(first), name="gather_first_forward")[0]

    def ride(self, layer, slot, call):
        start, todo, ex = self.plan.get((layer, slot)), [], None
        if start is not None:
            ex = _ex_gather(self.packs[start[0]], start[1], start[2])
            todo.append(("landed", start))
        if self.landed is not None:
            key, buf = self.landed
            forward = _ex_gather_forward(buf, SEM_BASE_2 if ex is not None else 0)
            ex = forward if ex is None else _ex_join(ex, forward)
            todo.append(("piece", key))
            self.landed = None
        outs = list(call(carry=ex))
        n = len(outs) - len(todo)
        for (kind, key), buf in zip(todo, outs[n:]):
            if kind == "landed":
                self.landed = (key, buf)
            else:
                self.pieces[key] = buf
        return outs[:n]

    def rows_of(self, layer, name):
        off = 0
        for n, rows in WEIGHT_PACK:
            if n == name:
                break
            off += rows
        for (l, r0, nrows), buf in self.pieces.items():
            if l == layer and r0 <= off < r0 + nrows:
                return buf[:, off - r0:off - r0 + rows, :]
        raise KeyError(f"{name} of layer {layer} is not gathered yet")

    def weight(self, layer, name):
        return _full_from_blocks(name, self.rows_of(layer, name))

    def small(self, layer):
        planes = lax.bitcast_convert_type(self.rows_of(layer, "small").astype(jnp.bfloat16), jnp.uint16)
        planes = planes.astype(jnp.uint32)
        bits = (planes[:, :SMALL_ROWS] << 16) | planes[:, SMALL_ROWS:]
        per_chip = [_small_from_rows(r) for r in lax.bitcast_convert_type(bits, F32)]
        return {n: jnp.concatenate([blocks[k] for blocks in per_chip], axis=axis)
                for k, (n, _, axis) in enumerate(SHARDED_SMALL)}


class _Params:
    def __init__(self, weights, layer, small):
        self.weights, self.layer, self.small, self.cache = weights, layer, small, {}

    def __getitem__(self, name):
        if name in self.small:
            return self.small[name]
        if name not in self.cache:
            if name in [n for n, _, _ in SHARDED_SMALL]:
                self.cache.update(self.weights.small(self.layer))
            else:
                self.cache[name] = self.weights.weight(self.layer, name)
        return self.cache[name]


def _layer_fwd(h0, mem, p, cos, sin, tag, ride):
    sv = {"h0": h0}
    n1, u = ride("mix_in", functools.partial(_rms_mm, h0, _row(p["mix_norm_pre"]), p["w_in"], tm=1024, tn=1408,
                                             out_dtype=F32, name=f"mix_in{tag}"))
    a_out, = ride("sb_fwd", functools.partial(_sb_fwd, u, name=f"sb_fwd{tag}"))
    b_out, c = _cv_fwd(u, p["cv_w"], _row(p["cv_b"]), _row(p["cv_ln_g"]), _row(p["cv_ln_b"]),
                       p["cv_pw_w"].astype(BF16), _row(p["cv_pw_b"]), name=f"cv_fwd{tag}")
    qkv = _rope_perm(u, cos, sin, name=f"rope_perm{tag}")
    o_p, l_p = ride("dl_fwd", functools.partial(_dl_fwd, qkv, name=f"dl_fwd{tag}"))
    c_out, o_dl, lse = ride("dl_mix", functools.partial(_dl_mix, o_p, l_p, name=f"dl_mix{tag}"))
    cat = jnp.concatenate([a_out, b_out, c_out], axis=1)
    y1, h1 = _mm_post(cat, p["w_out"], h0, _row(p["mix_norm_post"]), tm=512, name=f"mix_out{tag}")
    sv.update(n1=n1, u=u, c=c, qkv=qkv, o_dl=o_dl, lse=lse, cat=cat, y1=y1, h1=h1)

    n2, q = _rms_mm(h1, _row(p["x_norm_pre"]), p["x_wq"], tm=512, tn=1024, out_dtype=BF16, name=f"xa_q{tag}")
    wkv = jnp.concatenate([p["x_wk"], p["x_wv"]], axis=1)
    mem_n, kv = _rms_mm(mem, _row(p["mem_norm"]), wkv, tm=mem.shape[0], tn=1024, out_dtype=BF16, name=f"xa_kv{tag}")
    k, v = kv[:, :D_MODEL], kv[:, D_MODEL:]
    o_x = _xa_fwd(q, k, v, name=f"xa_fwd{tag}")
    y2, h2 = _mm_post(o_x, p["x_wo"], h1, _row(p["x_norm_post"]), tm=512, name=f"xa_out{tag}")
    sv.update(n2=n2, q=q, mem_n=mem_n, k=k, v=v, o_x=o_x, y2=y2, h2=h2, wkv=wkv)

    n3, up = ride("ffn_up", functools.partial(_rms_mm, h2, _row(p["ffn_norm_pre"]), p["ffn_w_up"], tm=1024, tn=1408,
                                              out_dtype=F32, name=f"ffn_up{tag}"))
    act, = ride("ffn_gate", functools.partial(_ff_gate_fwd, up, p["ffn_conv_w"], _row(p["ffn_conv_b"]),
                                              name=f"ffn_gate{tag}"))
    y3, h3 = ride("ffn_down", functools.partial(_mm_post, act, p["ffn_w_down"], h2, _row(p["ffn_norm_post"]), tm=512,
                                                name=f"ffn_down{tag}"))
    sv.update(n3=n3, up=up, act=act, y3=y3)
    return h3, sv


def _layer_bwd(dh3, mem, p, sv, cos, sin, tag, riding, new_rs):
    g = {}
    s8 = lambda part: part.sum(axis=0)
    rode = None

    dy3, dgp = _rms_bwd(sv["y3"], _row(p["ffn_norm_post"]), dh3, None, out_dtype=BF16, tm=512, name=f"ffn_post_b{tag}")
    g["ffn_norm_post"] = s8(dgp)
    dact = _mm_nt(dy3, p["ffn_w_down"], tm=512, tn=1408, out_dtype=F32, name=f"ffn_down_bx{tag}")
    g["ffn_w_down"] = _mm_tn(sv["act"], dy3, tk=1408, tn=1024, tm=2048, name=f"ffn_down_bw{tag}")
    dgu, dvu, dcw, dcb, *got = _ff_gate_bwd(sv["up"], dact, p["ffn_conv_w"], _row(p["ffn_conv_b"]),
                                            name=f"ffn_gate_b{tag}", carry=riding.swap() if riding else None)
    scatter = riding.after_swap(got[0]) if riding else None
    g["ffn_conv_w"] = jnp.concatenate([dcw[0], dcw[1]], axis=1).reshape(3, 8, 2 * D_FF).sum(axis=1)
    g["ffn_conv_b"] = jnp.concatenate([dcb[0], dcb[1]], axis=1).sum(axis=0)
    dup = jnp.concatenate([dgu, dvu], axis=1)
    dn3 = _mm_nt(dup, p["ffn_w_up"], tm=256, tn=512, out_dtype=F32, name=f"ffn_up_bx{tag}")
    g["ffn_w_up"] = _mm_tn(sv["n3"], dup, tk=512, tn=1408, tm=2048, name=f"ffn_up_bw{tag}")
    ffn_rs = new_rs(FFN_GROUP, g, f"{tag}_ffn")
    dh2, dgp = _rms_bwd(sv["h2"], _row(p["ffn_norm_pre"]), dn3, dh3, out_dtype=F32, tm=512, name=f"ffn_pre_b{tag}")
    g["ffn_norm_pre"] = s8(dgp)

    dy2, dgp = _rms_bwd(sv["y2"], _row(p["x_norm_post"]), dh2, None, out_dtype=BF16, tm=512, name=f"xa_post_b{tag}")
    g["x_norm_post"] = s8(dgp)
    do_x = _mm_nt(dy2, p["x_wo"], tm=512, tn=1024, out_dtype=BF16, name=f"xa_out_bx{tag}")
    g["x_wo"] = _mm_tn(sv["o_x"], dy2, tk=512, tn=1024, tm=2048, name=f"xa_out_bw{tag}")
    dq, dk, dv, got = _xa_bwd(sv["q"], sv["k"], sv["v"], do_x, name=f"xa_bwd{tag}", carry=ffn_rs.swap())
    ffn_scatter = ffn_rs.after_swap(got)
    dn2 = _mm_nt(dq, p["x_wq"], tm=512, tn=1024, out_dtype=F32, name=f"xa_q_bx{tag}")
    g["x_wq"] = _mm_tn(sv["n2"], dq, tk=512, tn=1024, tm=2048, name=f"xa_q_bw{tag}")
    dkv = jnp.concatenate([dk, dv], axis=1).astype(BF16)
    nm = mem.shape[0]
    dmem_n = _mm_nt(dkv, sv["wkv"], tm=nm, tn=1024, out_dtype=F32, name=f"xa_kv_bx{tag}")
    dwkv = _mm_tn(sv["mem_n"], dkv, tk=512, tn=2048, tm=nm, name=f"xa_kv_bw{tag}")
    g["x_wk"], g["x_wv"] = dwkv[:, :D_MODEL], dwkv[:, D_MODEL:]
    _, dgp = _rms_bwd(mem, _row(p["mem_norm"]), dmem_n, None, out_dtype=BF16, tm=nm, name=f"xa_mem_b{tag}")
    g["mem_norm"] = s8(dgp)
    xa_rs = new_rs(XA_GROUP, g, f"{tag}_xa")
    dh1, dgp, got = _rms_bwd(sv["h1"], _row(p["x_norm_pre"]), dn2, dh2, out_dtype=F32, tm=512, name=f"xa_pre_b{tag}",
                             carry=xa_rs.swap())
    xa_scatter = xa_rs.after_swap(got, SEM_BASE_2 if riding else 0)
    g["x_norm_pre"] = s8(dgp)

    dy1, dgp = _rms_bwd(sv["y1"], _row(p["mix_norm_post"]), dh1, None, out_dtype=BF16, tm=512, name=f"mix_post_b{tag}")
    g["mix_norm_post"] = s8(dgp)
    dcat = _mm_nt(dy1, p["w_out"], tm=512, tn=1024, out_dtype=F32, name=f"mix_out_bx{tag}")
    g["w_out"] = _mm_tn(sv["cat"], dy1, tk=512, tn=1024, tm=2048, name=f"mix_out_bw{tag}")
    u = sv["u"]
    dq_sb, dk_sb, dv_sb, *got = _sb_bwd(u, dcat, name=f"sb_bwd{tag}",
                                        carry=_ex_join(scatter, xa_scatter) if riding else xa_scatter)
    if riding:
        rode = riding.after_scatter(got[0])
    xa_rows = xa_rs.after_scatter(got[-1])
    pw_b16 = p["cv_pw_w"].astype(BF16)
    dc, dpw, vec = _cv_bwd_local(sv["c"], dcat, _row(p["cv_ln_g"]), _row(p["cv_ln_b"]), pw_b16, name=f"cv_bwd_a{tag}")
    g["cv_pw_w"] = dpw
    vec = vec.reshape(3, 8, CV_W).sum(axis=1)
    g["cv_pw_b"], g["cv_ln_g"], g["cv_ln_b"] = vec[0], vec[1], vec[2]
    du_cv, dcw, dcb = _cv_bwd_conv(u, dc, p["cv_w"], name=f"cv_bwd_b{tag}")
    g["cv_w"] = dcw.reshape(CV_K, 8, CV_W).sum(axis=1)
    g["cv_b"] = dcb.sum(axis=0)
    dop, stats = _dl_bwd_prep(dcat, sv["o_dl"], sv["lse"], name=f"dl_prep_b{tag}")
    cur, prev, got = _dl_bwd(sv["qkv"], dop, stats, name=f"dl_bwd{tag}", carry=ffn_scatter)
    ffn_rows = ffn_rs.after_scatter(got)
    du_dl = _dl_bwd_finish(cur, prev, cos, sin, name=f"dl_fin_b{tag}")
    du = jnp.concatenate([dq_sb.astype(BF16), dk_sb.astype(BF16), dv_sb.astype(BF16), du_cv, du_dl], axis=1)
    dn1 = _mm_nt(du, p["w_in"], tm=512, tn=512, out_dtype=F32, name=f"mix_in_bx{tag}")
    g["w_in"] = _mm_tn(sv["n1"], du, tk=512, tn=1408, tm=2048, name=f"mix_in_bw{tag}")
    dh0, dgp = _rms_bwd(sv["h0"], _row(p["mix_norm_pre"]), dn1, dh1, out_dtype=F32, tm=512, name=f"mix_pre_b{tag}")
    g["mix_norm_pre"] = s8(dgp)
    return dh0, g, (xa_rows, ffn_rows), rode


def _step(x, mem, positions, loss_target, w, m, v):
    depth = w["w_in"].shape[0]
    xi, yi, ci = lax.axis_index("x"), lax.axis_index("y"), lax.axis_index("c")
    chip = 2 * xi + yi
    h = x[0]
    mem0 = mem[0]
    s_len = h.shape[0]

    def pack_rows(n, l):
        if n == "small":
            bits = lax.bitcast_convert_type(_small_to_rows([w[name][l] for name, _, _ in SHARDED_SMALL]), jnp.uint32)
            planes = [(bits >> 16).astype(jnp.uint16), (bits & 0xFFFF).astype(jnp.uint16)]
            return jnp.concatenate([lax.bitcast_convert_type(p, jnp.bfloat16) for p in planes], axis=0)
        return _to_pack_rows(n, w[n][l]).astype(BF16)

    packs = [jnp.concatenate([pack_rows(n, l) for n, _ in WEIGHT_PACK], axis=0) for l in range(depth)]
    weights = _Weights(packs)
    params = [_Params(weights, l, {n: w[n][l] for n, _, axis in SMALL if axis is None}) for l in range(depth)]

    inv_freq = ROPE_THETA ** (-jnp.arange(HD // 2, dtype=F32) / (HD // 2))
    cos, sin = _rope_tables(positions.reshape(s_len, 1), jnp.tile(inv_freq, 4).reshape(1, LANES), name="rope_tables")

    saved = []
    for l in range(depth):
        h, sv = _layer_fwd(h, mem0, params[l], cos, sin, f"_l{l}", functools.partial(weights.ride, l))
        saved.append(sv)
    dh, sq = _loss_grad(h, loss_target[0], tm=512, name="loss_grad")
    loss = lax.psum(0.5 * jnp.sum(sq) / D_MODEL, ("x", "y", "c"))

    c_arr, j_arr = jnp.reshape(ci, (1,)).astype(jnp.int32), jnp.reshape(chip, (1,)).astype(jnp.int32)

    def new_rs(names, g, tag):
        blocks = [_blocks_from_full(n, g[n]) for n in names]
        if names is REST_GROUP:
            blocks.append(jnp.stack([_small_to_rows([_chip_block(g[n], j, shape, axis) for n, shape, axis in SHARDED_SMALL])
                                     for j in range(4)]))
        return _ReduceScatter(jnp.concatenate(blocks, axis=1), c_arr, j_arr, tag)

    grads, later_rows, rest_rows, pending = [None] * depth, [None] * depth, [None] * depth, None
    for l in reversed(range(depth)):
        dh, grads[l], later_rows[l], rode = _layer_bwd(dh, mem0, params[l], saved[l], cos, sin, f"_l{l}", pending, new_rs)
        if pending is not None:
            rest_rows[l + 1] = rode
        pending = new_rs(REST_GROUP, grads[l], f"_l{l}_rest")
    grad_x = dh[None]

    out_g, out_d, out_m, out_v = {}, {}, {}, {}
    pack_off, off = {}, 0
    for n, rows in PACK_ROWS:
        pack_off[n] = (off, rows)
        off += rows

    def reduced(l, n):
        start, rows = pack_off[n]
        for names, block in ((REST_GROUP, rest_rows[l]), (XA_GROUP, later_rows[l][0]), (FFN_GROUP, later_rows[l][1])):
            if n in names:
                return block[start - pack_off[names[0]][0]:][:rows]

    def update(n, carry=None):
        shard_shape = w[n].shape
        g_n = jnp.stack([reduced(l, n) for l in range(depth)]).reshape(shard_shape)
        flat = lambda a: a.reshape(-1, shard_shape[-1])
        d_n, m_n, v_n, *rode = _adamw(flat(w[n]), flat(g_n), flat(m[n]), flat(v[n]), name=f"adamw_{n}", carry=carry)
        out_g[n], out_d[n], out_m[n], out_v[n] = g_n, d_n.reshape(shard_shape), m_n.reshape(shard_shape), v_n.reshape(shard_shape)
        return rode

    got, = update("ffn_w_down", pending.swap())
    got16, = update("ffn_w_up", pending.after_swap(got))
    rest_rows[0] = pending.after_scatter(got16)
    for n, _ in PACK_ROWS:
        if n not in FFN_GROUP:
            update(n)

    g_small = _all_reduce_small(_flat_rows([grads[l][n] for l in range(depth) for n, _, axis in SMALL if axis is None]),
                                name="all_reduce_small_grads").reshape(-1)
    local_g, off = {}, 0
    for l in range(depth):
        for n, shape, axis in SMALL:
            if axis is None:
                size = int(np.prod(shape))
                local_g.setdefault(n, []).append(g_small[off:off + size].reshape(shape))
                off += size
        small_rows = rest_rows[l][sum(pack_off[n][1] for n in REST_GROUP):]
        for (n, _, _), block in zip(SHARDED_SMALL, _small_from_rows(small_rows)):
            local_g.setdefault(n, []).append(block)
    names = [n for n, _, _ in SMALL]
    g_loc = {n: jnp.stack(local_g[n]) for n in names}
    d_s, m_s, v_s = _adamw(_flat_rows([w[n] for n in names]), _flat_rows([g_loc[n] for n in names]),
                           _flat_rows([m[n] for n in names]), _flat_rows([v[n] for n in names]), name="adamw_small")
    off = 0
    for n in names:
        size = int(np.prod(w[n].shape))
        take = lambda a: a.reshape(-1)[off:off + size].reshape(w[n].shape)
        out_g[n], out_d[n], out_m[n], out_v[n] = g_loc[n], take(d_s), take(m_s), take(v_s)
        off += size

    outs = [loss, grad_x]
    for group in (out_g, out_d, out_m, out_v):
        outs += [group[n] for n in WEIGHT_ORDER]
    return tuple(outs)


def kernel(x, mem, positions, mix_norm_pre, w_in, cv_w, cv_b, cv_ln_g, cv_ln_b, cv_pw_w, cv_pw_b, w_out, mix_norm_post, x_norm_pre, mem_norm, x_wq, x_wk, x_wv, x_wo, x_norm_post, ffn_norm_pre, ffn_w_up, ffn_conv_w, ffn_conv_b, ffn_w_down, ffn_norm_post, loss_target, m_mix_norm_pre, m_w_in, m_cv_w, m_cv_b, m_cv_ln_g, m_cv_ln_b, m_cv_pw_w, m_cv_pw_b, m_w_out, m_mix_norm_post, m_x_norm_pre, m_mem_norm, m_x_wq, m_x_wk, m_x_wv, m_x_wo, m_x_norm_post, m_ffn_norm_pre, m_ffn_w_up, m_ffn_conv_w, m_ffn_conv_b, m_ffn_w_down, m_ffn_norm_post, v_mix_norm_pre, v_w_in, v_cv_w, v_cv_b, v_cv_ln_g, v_cv_ln_b, v_cv_pw_w, v_cv_pw_b, v_w_out, v_mix_norm_post, v_x_norm_pre, v_mem_norm, v_x_wq, v_x_wk, v_x_wv, v_x_wo, v_x_norm_post, v_ffn_norm_pre, v_ffn_w_up, v_ffn_conv_w, v_ffn_conv_b, v_ffn_w_down, v_ffn_norm_post):
    w = dict(zip(WEIGHT_ORDER, (mix_norm_pre, w_in, cv_w, cv_b, cv_ln_g, cv_ln_b, cv_pw_w, cv_pw_b, w_out, mix_norm_post, x_norm_pre, mem_norm, x_wq, x_wk, x_wv, x_wo, x_norm_post, ffn_norm_pre, ffn_w_up, ffn_conv_w, ffn_conv_b, ffn_w_down, ffn_norm_post)))
    m = dict(zip(WEIGHT_ORDER, (m_mix_norm_pre, m_w_in, m_cv_w, m_cv_b, m_cv_ln_g, m_cv_ln_b, m_cv_pw_w, m_cv_pw_b, m_w_out, m_mix_norm_post, m_x_norm_pre, m_mem_norm, m_x_wq, m_x_wk, m_x_wv, m_x_wo, m_x_norm_post, m_ffn_norm_pre, m_ffn_w_up, m_ffn_conv_w, m_ffn_conv_b, m_ffn_w_down, m_ffn_norm_post)))
    v = dict(zip(WEIGHT_ORDER, (v_mix_norm_pre, v_w_in, v_cv_w, v_cv_b, v_cv_ln_g, v_cv_ln_b, v_cv_pw_w, v_cv_pw_b, v_w_out, v_mix_norm_post, v_x_norm_pre, v_mem_norm, v_x_wq, v_x_wk, v_x_wv, v_x_wo, v_x_norm_post, v_ffn_norm_pre, v_ffn_w_up, v_ffn_conv_w, v_ffn_conv_b, v_ffn_w_down, v_ffn_norm_post)))
    return _step(x, mem, positions, loss_target, w, m, v)
```

```python
import functools

import jax
import jax.numpy as jnp
import numpy as np
from jax import lax
from jax.experimental import pallas as pl
from jax.experimental.pallas import tpu as pltpu

F32, BF16 = jnp.float32, jnp.bfloat16
MESH = pl.DeviceIdType.MESH
EPS = 1e-6
LANES = 128
BLK = 128
HD = 64
D_MODEL = 1024
D_FF = 2816
SB_W, CV_W, DL_W = 256, 256, 512
CV_K = 31
ROPE_THETA = 10000.0
DILATIONS = (1, 4, 16)
X_HEADS, X_HD = 4, 256
ADAM_LR, ADAM_B1, ADAM_B2, ADAM_EPS, ADAM_WD, ADAM_STEP = 0.001, 0.9, 0.999, 1e-08, 0.01, 10
NEG_INF = float("-inf")
MIB = 1 << 20

PACK_ROWS = (("w_in", 704), ("w_out", 256), ("x_wq", 256), ("x_wk", 256), ("x_wv", 256), ("x_wo", 256),
             ("ffn_w_up", 1408), ("ffn_w_down", 704))
PACK_RL = sum(r for _, r in PACK_ROWS)
SHARDED_SMALL = (("cv_w", (31, 64), 1), ("ffn_conv_w", (3, 1408), 1), ("cv_pw_w", (64, 256), 0))
SMALL_ROWS = 32
WEIGHT_PACK = (PACK_ROWS[0], ("small", 2 * SMALL_ROWS)) + PACK_ROWS[1:]


def _cp(vmem_mb=48):
    return pltpu.CompilerParams(vmem_limit_bytes=vmem_mb * MIB)


def _dot(a, b):
    return jnp.dot(a, b, preferred_element_type=F32)


def _dot_nt(a, b):
    return lax.dot_general(a, b, (((1,), (1,)), ((), ())), preferred_element_type=F32)


def _dot_tn(a, b):
    return lax.dot_general(a, b, (((0,), (0,)), ((), ())), preferred_element_type=F32)


def _dot_hilo(x, m):
    hi = x.astype(BF16)
    lo = (x - hi.astype(F32)).astype(BF16)
    return _dot(hi, m) + _dot(lo, m)


def _rowsum8(x):
    t, c = x.shape
    return x.reshape(t // 8, 8, c).sum(axis=0)


def _acc_out(ref, i, val):
    @pl.when(i == 0)
    def _():
        ref[...] = val

    @pl.when(i > 0)
    def _():
        ref[...] += val


def _tile(n, cap, mult=8):
    t = min(n, cap)
    while n % t or t % mult:
        t -= 1
    return t


def _rms_mm(x, g, w, *, tm, tn, out_dtype, name, carry=None):
    m, d = x.shape
    n_out = w.shape[1]

    def body(x_ref, g_ref, w_ref, n_ref, o_ref):
        @pl.when(pl.program_id(1) == 0)
        def _():
            xv = x_ref[...]
            r = lax.rsqrt(jnp.mean(xv * xv, axis=-1, keepdims=True) + EPS)
            n_ref[...] = (xv * r * g_ref[...]).astype(BF16)

        o_ref[...] = _dot(n_ref[...], w_ref[...]).astype(out_dtype)

    return _call(
        body, grid=(m // tm, n_out // tn), name=name, carry=carry,
        in_specs=[pl.BlockSpec((tm, d), lambda i, j: (i, 0)), pl.BlockSpec((1, d), lambda i, j: (0, 0)),
                  pl.BlockSpec((d, tn), lambda i, j: (0, j))],
        out_specs=[pl.BlockSpec((tm, d), lambda i, j: (i, 0)), pl.BlockSpec((tm, tn), lambda i, j: (i, j))],
        out_shape=[jax.ShapeDtypeStruct((m, d), BF16), jax.ShapeDtypeStruct((m, n_out), out_dtype)],
        args=(x, g, w))


def _mm_post(a, w, h, g, *, tm, name, carry=None):
    m, k = a.shape
    d = w.shape[1]

    def body(a_ref, w_ref, h_ref, g_ref, y_ref, ho_ref):
        y = _dot(a_ref[...], w_ref[...])
        y_ref[...] = y
        r = lax.rsqrt(jnp.mean(y * y, axis=-1, keepdims=True) + EPS)
        ho_ref[...] = h_ref[...] + y * r * g_ref[...]

    return _call(
        body, grid=(m // tm,), name=name, carry=carry,
        in_specs=[pl.BlockSpec((tm, k), lambda i: (i, 0)), pl.BlockSpec((k, d), lambda i: (0, 0)),
                  pl.BlockSpec((tm, d), lambda i: (i, 0)), pl.BlockSpec((1, d), lambda i: (0, 0))],
        out_specs=[pl.BlockSpec((tm, d), lambda i: (i, 0)), pl.BlockSpec((tm, d), lambda i: (i, 0))],
        out_shape=[jax.ShapeDtypeStruct((m, d), F32), jax.ShapeDtypeStruct((m, d), F32)],
        args=(a, w, h, g))


def _mm_nt(a, w, *, tm, tn, out_dtype, name):
    m, k = a.shape
    n_out = w.shape[0]

    def body(a_ref, w_ref, o_ref):
        o_ref[...] = _dot_nt(a_ref[...], w_ref[...]).astype(out_dtype)

    return pl.pallas_call(
        body, grid=(n_out // tn, m // tm), name=name,
        in_specs=[pl.BlockSpec((tm, k), lambda j, i: (i, 0)), pl.BlockSpec((tn, k), lambda j, i: (j, 0))],
        out_specs=pl.BlockSpec((tm, tn), lambda j, i: (i, j)),
        out_shape=jax.ShapeDtypeStruct((m, n_out), out_dtype),
        compiler_params=_cp())(a, w)


def _mm_tn(x, dy, *, tk, tn, tm, name):
    m, k = x.shape
    n_out = dy.shape[1]

    def body(x_ref, d_ref, o_ref):
        _acc_out(o_ref, pl.program_id(2), _dot_tn(x_ref[...], d_ref[...]))

    return pl.pallas_call(
        body, grid=(k // tk, n_out // tn, m // tm), name=name,
        in_specs=[pl.BlockSpec((tm, tk), lambda a, b, c: (c, a)), pl.BlockSpec((tm, tn), lambda a, b, c: (c, b))],
        out_specs=pl.BlockSpec((tk, tn), lambda a, b, c: (a, b)),
        out_shape=jax.ShapeDtypeStruct((k, n_out), F32),
        compiler_params=_cp())(x, dy)


def _rms_bwd(x, g, dout, res, *, out_dtype, tm, name, carry=None):
    m, d = x.shape
    has_res = res is not None

    def body(*refs):
        if has_res:
            x_ref, g_ref, d_ref, r_ref, dx_ref, dg_ref = refs
        else:
            x_ref, g_ref, d_ref, dx_ref, dg_ref = refs
        xv = x_ref[...]
        dv = d_ref[...].astype(F32)
        r = lax.rsqrt(jnp.mean(xv * xv, axis=-1, keepdims=True) + EPS)
        xh = xv * r
        dxh = dv * g_ref[...]
        dx = r * (dxh - xh * jnp.mean(dxh * xh, axis=-1, keepdims=True))
        if has_res:
            dx = dx + r_ref[...]
        dx_ref[...] = dx.astype(out_dtype)
        _acc_out(dg_ref, pl.program_id(0), _rowsum8(dv * xh))

    row = pl.BlockSpec((tm, d), lambda i: (i, 0))
    ins = [row, pl.BlockSpec((1, d), lambda i: (0, 0)), row] + ([row] if has_res else [])
    args = (x, g, dout) + ((res,) if has_res else ())
    return _call(
        body, grid=(m // tm,), name=name, carry=carry, in_specs=ins,
        out_specs=[row, pl.BlockSpec((8, d), lambda i: (0, 0))],
        out_shape=[jax.ShapeDtypeStruct((m, d), out_dtype), jax.ShapeDtypeStruct((8, d), F32)], args=args)


def _loss_grad(h, tgt, *, tm, name):
    m, d = h.shape

    def body(h_ref, t_ref, dh_ref, p_ref):
        e = h_ref[...] - t_ref[...]
        dh_ref[...] = e / d
        _acc_out(p_ref, pl.program_id(0), _rowsum8(e * e))

    row = pl.BlockSpec((tm, d), lambda i: (i, 0))
    return pl.pallas_call(
        body, grid=(m // tm,), name=name, in_specs=[row, row],
        out_specs=[row, pl.BlockSpec((8, d), lambda i: (0, 0))],
        out_shape=[jax.ShapeDtypeStruct((m, d), F32), jax.ShapeDtypeStruct((8, d), F32)],
        compiler_params=_cp())(h, tgt)


def _adamw(w, g, m, v, *, name, carry=None):
    r, c = w.shape
    tr = _tile(r, 256)

    def body(w_ref, g_ref, m_ref, v_ref, d_ref, mo_ref, vo_ref):
        gv = g_ref[...]
        m2 = ADAM_B1 * m_ref[...] + (1.0 - ADAM_B1) * gv
        v2 = ADAM_B2 * v_ref[...] + (1.0 - ADAM_B2) * jnp.square(gv)
        m_hat = m2 / (1.0 - ADAM_B1 ** ADAM_STEP)
        v_hat = v2 / (1.0 - ADAM_B2 ** ADAM_STEP)
        d_ref[...] = -ADAM_LR * (m_hat / (jnp.sqrt(v_hat) + ADAM_EPS) + ADAM_WD * w_ref[...])
        mo_ref[...] = m2
        vo_ref[...] = v2

    blk = pl.BlockSpec((tr, c), lambda i: (i, 0))
    return _call(body, grid=(r // tr,), name=name, carry=carry, in_specs=[blk] * 4, out_specs=[blk] * 3,
                 out_shape=[jax.ShapeDtypeStruct((r, c), F32)] * 3, args=(w, g, m, v))


def _head_masks():
    lane = lax.broadcasted_iota(jnp.int32, (BLK, LANES), 1)
    row = lax.broadcasted_iota(jnp.int32, (BLK, LANES), 0)
    return lane, row, lane < HD


def _sb_scores(q_a, k, before):
    z = _dot_nt(q_a, k)
    sp = jnp.log1p(jnp.exp(-jnp.abs(z)))
    ls_pos = jnp.minimum(z, 0.0) - sp
    lkeep = jnp.where(before, ls_pos - z, 0.0)
    return ls_pos, lkeep


SB_DEAD = -104.0


def _sb_alive(jj, i, carry):
    return jnp.logical_and(jj <= i, jnp.max(carry) > SB_DEAD)


SB_QB_FWD = 2
SB_QB = 2


def _sb_before(jj, qb=SB_QB):
    lane = lax.broadcasted_iota(jnp.int32, (qb * 2 * BLK, LANES), 1)
    row = lax.broadcasted_iota(jnp.int32, (qb * 2 * BLK, LANES), 0)
    below_diag = jj - (qb - 1) + row // (2 * BLK)
    return jnp.logical_or(below_diag > 0, jnp.logical_and(below_diag == 0, lane < row % BLK))


def _sb_stack(x, lane_h, qb=SB_QB):
    return jnp.concatenate([_stack_heads(x[b * BLK:(b + 1) * BLK], lane_h) for b in range(qb)], axis=0)


def _sb_unstack(x, lane_h, qb=SB_QB):
    return jnp.concatenate([jnp.where(lane_h, x[2 * b * BLK:(2 * b + 1) * BLK], x[(2 * b + 1) * BLK:(2 * b + 2) * BLK])
                            for b in range(qb)], axis=0)


SB_ROWS = SB_QB * 2 * BLK


def _sb_fwd(u, *, name, carry=None):
    s_len = u.shape[0]
    qb = SB_QB_FWD
    qrows, rows = qb * BLK, qb * 2 * BLK

    def body(q_ref, k_ref, v_ref, o_ref):
        top = pl.program_id(0) * qb + qb - 1
        lane, row, lane_h = _head_masks()
        suffix = (row > lane).astype(BF16)
        pairs = [slice(hp * LANES, (hp + 1) * LANES) for hp in range(2)]
        qs = [_sb_stack(q_ref[:, cs] * 0.125, lane_h, qb) for cs in pairs]

        def step(state):
            jj, ccs, accs = state[0], state[1:3], state[3:5]
            rows_k = pl.ds(pl.multiple_of((top - jj) * BLK, BLK), BLK)
            before = _sb_before(jj, qb)
            scores = [_sb_scores(q, k_ref[rows_k, cs].astype(BF16), before) for q, cs in zip(qs, pairs)]
            between = [_dot_hilo(lkeep, suffix) + cc for (_, lkeep), cc in zip(scores, ccs)]
            atts = [jnp.where(before, jnp.exp(ls_pos + b), 0.0).astype(BF16) for (ls_pos, _), b in zip(scores, between)]
            new_cc = [cc + jnp.sum(lkeep, axis=1, keepdims=True) for (_, lkeep), cc in zip(scores, ccs)]
            new_acc = [acc + _dot(a, v_ref[rows_k, cs].astype(BF16)) for a, acc, cs in zip(atts, accs, pairs)]
            return (jj + 1, *new_cc, *new_acc)

        zc, za = jnp.zeros((rows, 1), F32), jnp.zeros((rows, LANES), F32)
        res = lax.while_loop(lambda st: _sb_alive(st[0], top, jnp.maximum(st[1], st[2])), step,
                             (jnp.int32(0), zc, zc, za, za))
        for hp, cs in enumerate(pairs):
            o_ref[:, cs] = _sb_unstack(res[3 + hp], lane_h, qb).astype(BF16)

    wide = 2 * LANES
    return _call(
        body, grid=(s_len // qrows,), name=name, carry=carry,
        in_specs=[pl.BlockSpec((qrows, wide), lambda i: (i, 0)), pl.BlockSpec((s_len, wide), lambda i: (0, 1)),
                  pl.BlockSpec((s_len, wide), lambda i: (0, 2))],
        out_specs=[pl.BlockSpec((qrows, wide), lambda i: (i, 0))],
        out_shape=[jax.ShapeDtypeStruct((s_len, SB_W), BF16)], args=(u, u, u))


def _sb_bwd(u, dcat, *, name, carry=None):
    s_len = u.shape[0]
    nq = s_len // BLK
    qrows = SB_QB * BLK

    def body(q_ref, k_ref, v_ref, do_ref, dq_ref, dk_ref, dv_ref, g_scr, b_scr):
        step = pl.program_id(1)
        top = step * SB_QB + SB_QB - 1
        lane, row, lane_h = _head_masks()
        suffix = (row > lane).astype(BF16)
        prefix = (row < lane).astype(BF16)
        qf = q_ref[...]
        qs = _sb_stack(qf * 0.125, lane_h)
        qu = _sb_stack(qf, lane_h)
        dos = _sb_stack(do_ref[...], lane_h)

        @pl.when(step == 0)
        def _():
            dk_ref[...] = jnp.zeros_like(dk_ref)
            dv_ref[...] = jnp.zeros_like(dv_ref)

        def down(state):
            jj, cc = state
            j = top - jj
            off = pl.multiple_of(j * BLK, BLK)
            k = k_ref[pl.ds(off, BLK), :].astype(BF16)
            v = v_ref[pl.ds(off, BLK), :].astype(BF16)
            before = _sb_before(jj)
            ls_pos, lkeep = _sb_scores(qs, k, before)
            between = _dot_hilo(lkeep, suffix) + cc
            att = jnp.where(before, jnp.exp(ls_pos + between), 0.0)
            g_scr[j] = att * _dot_nt(dos, v)
            b_scr[j] = jnp.exp(ls_pos)
            dv_ref[pl.ds(off, BLK), :] += _dot_tn(att.astype(BF16), dos)
            return jj + 1, cc + jnp.sum(lkeep, axis=1, keepdims=True)

        zc = jnp.zeros((SB_ROWS, 1), F32)
        visited = lax.while_loop(lambda st: _sb_alive(st[0], top, st[1]), down, (jnp.int32(0), zc))[0]

        def up(j, carry):
            pc, dq = carry
            off = pl.multiple_of(j * BLK, BLK)
            k = k_ref[pl.ds(off, BLK), :].astype(BF16)
            g, beta = g_scr[j], b_scr[j]
            below = _dot_hilo(g, prefix) + pc
            dz = (jnp.where(_sb_before(top - j), g * (1.0 - beta) - beta * below, 0.0) * 0.125).astype(BF16)
            dk_ref[pl.ds(off, BLK), :] += _dot_tn(dz, qu)
            return pc + jnp.sum(g, axis=1, keepdims=True), dq + _dot(dz, k)

        dq = lax.fori_loop(top + 1 - visited, top + 1, up, (zc, jnp.zeros((SB_ROWS, LANES), F32)))[1]
        dq_ref[...] = _sb_unstack(dq, lane_h)

    col = lambda c0: pl.BlockSpec((s_len, LANES), lambda hp, i: (0, c0 + hp))
    blk = pl.BlockSpec((qrows, LANES), lambda hp, i: (i, hp))
    acc = pl.BlockSpec((s_len, LANES), lambda hp, i: (0, hp))
    return _call(
        body, grid=(2, s_len // qrows), name=name, carry=carry, in_specs=[blk, col(2), col(4), blk],
        out_specs=[blk, acc, acc], out_shape=[jax.ShapeDtypeStruct((s_len, SB_W), F32)] * 3,
        scratch_shapes=[pltpu.VMEM((nq, SB_ROWS, LANES), F32), pltpu.VMEM((nq, SB_ROWS, LANES), F32)],
        vmem_mb=56, args=(u, u, u, dcat))


CV_T = 512
CV_H = 32


def _cv_specs(s_len):
    cur = lambda c: pl.BlockSpec((CV_T, CV_W), lambda i: (i, c))
    prev = lambda c: pl.BlockSpec((CV_H, CV_W), lambda i: (jnp.maximum(i * (CV_T // CV_H) - 1, 0), c))
    nxt = lambda c: pl.BlockSpec((CV_H, CV_W),
                                 lambda i: (jnp.minimum((i + 1) * (CV_T // CV_H), s_len // CV_H - 1), c))
    full = lambda r: pl.BlockSpec((r, CV_W), lambda i: (0, 0))
    return cur, prev, nxt, full


def _glu_into(gp_ref, val_ref, gate_ref, valp_ref, gatep_ref, i):
    gp_ref[0:CV_H, :] = jnp.where(i > 0, valp_ref[...] * jax.nn.sigmoid(gatep_ref[...]), 0.0)
    gp_ref[CV_H:, :] = val_ref[...] * jax.nn.sigmoid(gate_ref[...])


def _cv_fwd(u, cv_w, cv_b, ln_g, ln_b, pw_w, pw_b, *, name):
    s_len = u.shape[0]
    cur, prev, _, full = _cv_specs(s_len)

    def body(val_ref, gate_ref, valp_ref, gatep_ref, w_ref, b_ref, g_ref, be_ref, pw_ref, pb_ref,
             o_ref, c_ref, gp_ref):
        _glu_into(gp_ref, val_ref, gate_ref, valp_ref, gatep_ref, pl.program_id(0))
        acc = jnp.zeros((CV_T, CV_W), F32) + b_ref[...]
        for k in range(CV_K):
            acc = acc + w_ref[k:k + 1, :] * gp_ref[pl.ds(CV_H - CV_K + 1 + k, CV_T), :]
        c_ref[...] = acc
        mu = jnp.mean(acc, axis=-1, keepdims=True)
        xc = acc - mu
        xh = xc * lax.rsqrt(jnp.mean(xc * xc, axis=-1, keepdims=True) + EPS)
        a = xh * g_ref[...] + be_ref[...]
        s = a * jax.nn.sigmoid(a)
        o_ref[...] = (_dot(s.astype(BF16), pw_ref[...]) + pb_ref[...]).astype(BF16)

    return pl.pallas_call(
        body, grid=(s_len // CV_T,), name=name,
        in_specs=[cur(3), cur(4), prev(3), prev(4), full(CV_K), full(1), full(1), full(1), full(CV_W), full(1)],
        out_specs=[cur(0), cur(0)],
        out_shape=[jax.ShapeDtypeStruct((s_len, CV_W), BF16), jax.ShapeDtypeStruct((s_len, CV_W), F32)],
        scratch_shapes=[pltpu.VMEM((CV_T + CV_H, CV_W), F32)], compiler_params=_cp())(
            u, u, u, u, cv_w, cv_b, ln_g, ln_b, pw_w, pw_b)


def _cv_bwd_local(c, dcat, ln_g, ln_b, pw_w, *, name):
    s_len = c.shape[0]
    cur, _, _, full = _cv_specs(s_len)

    def body(c_ref, db_ref, g_ref, be_ref, pw_ref, dc_ref, dpw_ref, vec_ref):
        i = pl.program_id(0)
        cv = c_ref[...]
        db = db_ref[...]
        mu = jnp.mean(cv, axis=-1, keepdims=True)
        xc = cv - mu
        rstd = lax.rsqrt(jnp.mean(xc * xc, axis=-1, keepdims=True) + EPS)
        xh = xc * rstd
        a = xh * g_ref[...] + be_ref[...]
        sg = jax.nn.sigmoid(a)
        s = a * sg
        dbb = db.astype(BF16)
        ds = _dot_nt(dbb, pw_ref[...])
        da = ds * (sg * (1.0 + a * (1.0 - sg)))
        dxh = da * g_ref[...]
        dc_ref[...] = rstd * (dxh - jnp.mean(dxh, axis=-1, keepdims=True)
                              - xh * jnp.mean(dxh * xh, axis=-1, keepdims=True))
        _acc_out(dpw_ref, i, _dot_tn(s.astype(BF16), dbb))
        _acc_out(vec_ref, i, jnp.concatenate([_rowsum8(db), _rowsum8(da * xh), _rowsum8(da)], axis=0))

    return pl.pallas_call(
        body, grid=(s_len // CV_T,), name=name,
        in_specs=[cur(0), cur(1), full(1), full(1), full(CV_W)],
        out_specs=[cur(0), full(CV_W), full(24)],
        out_shape=[jax.ShapeDtypeStruct((s_len, CV_W), F32), jax.ShapeDtypeStruct((CV_W, CV_W), F32),
                   jax.ShapeDtypeStruct((24, CV_W), F32)], compiler_params=_cp())(c, dcat, ln_g, ln_b, pw_w)


def _cv_bwd_conv(u, dc, cv_w, *, name):
    s_len = u.shape[0]
    cur, prev, nxt, full = _cv_specs(s_len)
    last = s_len // CV_T - 1

    def body(val_ref, gate_ref, valp_ref, gatep_ref, dc_ref, dcn_ref, w_ref, du_ref, dw_ref, dbias_ref,
             gp_ref, dcp_ref):
        i = pl.program_id(0)
        _glu_into(gp_ref, val_ref, gate_ref, valp_ref, gatep_ref, i)
        dcv = dc_ref[...]
        dcp_ref[0:CV_T, :] = dcv
        dcp_ref[CV_T:, :] = jnp.where(i < last, dcn_ref[...], 0.0)
        dg = jnp.zeros((CV_T, CV_W), F32)
        parts = []
        for k in range(CV_K):
            dg = dg + w_ref[k:k + 1, :] * dcp_ref[pl.ds(CV_K - 1 - k, CV_T), :]
            parts.append(_rowsum8(dcv * gp_ref[pl.ds(CV_H - CV_K + 1 + k, CV_T), :]))
        _acc_out(dw_ref, i, jnp.concatenate(parts, axis=0))
        _acc_out(dbias_ref, i, _rowsum8(dcv))
        val = val_ref[...]
        sg = jax.nn.sigmoid(gate_ref[...])
        du_ref[:, 0:CV_W] = (dg * sg).astype(BF16)
        du_ref[:, CV_W:] = (dg * val * sg * (1.0 - sg)).astype(BF16)

    return pl.pallas_call(
        body, grid=(s_len // CV_T,), name=name,
        in_specs=[cur(3), cur(4), prev(3), prev(4), cur(0), nxt(0), full(CV_K)],
        out_specs=[pl.BlockSpec((CV_T, 2 * CV_W), lambda i: (i, 0)), full(CV_K * 8), full(8)],
        out_shape=[jax.ShapeDtypeStruct((s_len, 2 * CV_W), BF16), jax.ShapeDtypeStruct((CV_K * 8, CV_W), F32),
                   jax.ShapeDtypeStruct((8, CV_W), F32)],
        scratch_shapes=[pltpu.VMEM((CV_T + CV_H, CV_W), F32), pltpu.VMEM((CV_T + CV_H, CV_W), F32)],
        compiler_params=_cp())(u, u, u, u, dc, dc, cv_w)


def _rope_tables(pos_col, inv_freq_row, *, name):
    s_len = pos_col.shape[0]

    def body(p_ref, f_ref, cos_ref, sin_ref):
        ang = p_ref[...].astype(F32) * f_ref[...]
        lane = lax.broadcasted_iota(jnp.int32, (s_len, LANES), 1)
        sn = jnp.sin(ang)
        cos_ref[...] = jnp.cos(ang)
        sin_ref[...] = jnp.where(lane % HD < HD // 2, -sn, sn)

    return pl.pallas_call(body, name=name, out_shape=[jax.ShapeDtypeStruct((s_len, LANES), F32)] * 2,
                          compiler_params=_cp())(pos_col, inv_freq_row)


def _rot_half(x):
    lane = lax.broadcasted_iota(jnp.int32, x.shape, 1)
    return jnp.where(lane % HD < HD // 2, pltpu.roll(x, LANES - HD // 2, 1), pltpu.roll(x, HD // 2, 1))


def _permute_rows(dst_ref, src_ref, d, dtype):
    s_len = src_ref.shape[0]
    seg = s_len // d
    if d == 1:
        dst_ref[...] = src_ref[...].astype(dtype)
        return
    for r in range(d):
        dst_ref[r * seg:(r + 1) * seg, :] = src_ref[pl.ds(r, seg, stride=d), :].astype(dtype)


def _unpermute_rows(dst_ref, src_ref, d):
    s_len = src_ref.shape[0]
    seg = s_len // d
    if d == 1:
        dst_ref[...] = src_ref[...]
        return
    for r in range(d):
        dst_ref[pl.ds(r, seg, stride=d), :] = src_ref[r * seg:(r + 1) * seg, :]


def _rope_perm(u, cos, sin, *, name):
    s_len = u.shape[0]

    def body(x_ref, cos_ref, sin_ref, o_ref, scr):
        a = pl.program_id(0)
        x = x_ref[...]
        rot = a < 2
        scr[...] = x * jnp.where(rot, cos_ref[...], 1.0) + _rot_half(x) * jnp.where(rot, sin_ref[...], 0.0)
        for n, d in enumerate(DILATIONS):
            _permute_rows(o_ref.at[n], scr, d, BF16)

    tab = pl.BlockSpec((s_len, LANES), lambda a, cb: (0, 0))
    return pl.pallas_call(
        body, grid=(3, 4), name=name,
        in_specs=[pl.BlockSpec((s_len, LANES), lambda a, cb: (0, 10 + 4 * a + cb)), tab, tab],
        out_specs=pl.BlockSpec((None, 3, s_len, LANES), lambda a, cb: (a, 0, 0, cb)),
        out_shape=jax.ShapeDtypeStruct((3, 3, s_len, DL_W), BF16),
        scratch_shapes=[pltpu.VMEM((s_len, LANES), F32)], compiler_params=_cp())(u, cos, sin)


DL_UNROLL = 4


def _dl_band(rows):
    lane = lax.broadcasted_iota(jnp.int32, (rows, LANES), 1)
    row = lax.broadcasted_iota(jnp.int32, (rows, LANES), 0) % BLK
    return lane <= row, lane >= row


def _dl_first(s_len, n, i):
    nb = jnp.where(n == 0, s_len // BLK, jnp.where(n == 1, s_len // (BLK * DILATIONS[1]),
                                                   s_len // (BLK * DILATIONS[2])))
    return lax.rem(i, nb) == 0


def _stack_heads(x, lane_h):
    return jnp.concatenate([jnp.where(lane_h, x, 0.0), jnp.where(lane_h, 0.0, x)], axis=0).astype(BF16)


def _dl_rows(i):
    cur = pl.ds(pl.multiple_of(i * BLK, BLK), BLK)
    prev = pl.ds(pl.multiple_of(jnp.maximum(i - 1, 0) * BLK, BLK), BLK)
    return cur, prev


def _dl_in_specs(s_len):
    return [pl.BlockSpec((None, None, s_len, LANES), functools.partial(lambda a, n, hp: (a, n, 0, hp), a))
            for a in range(3)]


def _dl_fwd(qkv, *, name, carry=None):
    s_len = qkv.shape[2]

    def body(q_ref, k_ref, v_ref, o_ref, l_ref):
        n = pl.program_id(0)
        lane_h = _head_masks()[2]
        band_c, band_p = _dl_band(2 * BLK)
        ones = jnp.ones((BLK, LANES), BF16)

        @pl.loop(0, s_len // BLK, step=DL_UNROLL)
        def _(i0):
            blocks = [i0 + t for t in range(DL_UNROLL)]
            rows = [_dl_rows(i) for i in blocks]
            scores = []
            for cur, prev in rows:
                qs = _stack_heads(q_ref[cur, :] * 0.125, lane_h)
                scores.append((_dot_nt(qs, k_ref[cur, :]), _dot_nt(qs, k_ref[prev, :])))
            probs = []
            for i, (sc, sp) in zip(blocks, scores):
                sc = jnp.where(band_c, sc, NEG_INF)
                sp = jnp.where(jnp.logical_and(band_p, jnp.logical_not(_dl_first(s_len, n, i))), sp, NEG_INF)
                m = jnp.max(jnp.maximum(sc, sp), axis=1, keepdims=True)
                probs.append((jnp.exp(sc - m).astype(BF16), jnp.exp(sp - m).astype(BF16), m))
            for (cur, prev), (pc, pp, m) in zip(rows, probs):
                r = (_dot(pc, jnp.concatenate([v_ref[cur, :], ones], axis=1))
                     + _dot(pp, jnp.concatenate([v_ref[prev, :], ones], axis=1)))
                den = jnp.where(lane_h, r[:BLK, LANES:], r[BLK:, LANES:])
                o_ref[cur, :] = jnp.where(lane_h, r[:BLK, :LANES], r[BLK:, :LANES]) / den
                l_ref[cur, :] = jnp.where(lane_h, m[:BLK], m[BLK:]) + jnp.log(den)

    out = pl.BlockSpec((None, s_len, LANES), lambda n, hp: (n, 0, hp))
    return _call(
        body, grid=(3, 4), name=name, carry=carry, in_specs=_dl_in_specs(s_len), out_specs=[out, out],
        out_shape=[jax.ShapeDtypeStruct((3, s_len, DL_W), F32)] * 2, args=(qkv, qkv, qkv))


def _dl_mix(o_p, l_p, *, name, carry=None):
    s_len = o_p.shape[1]

    def body(o_ref, l_ref, ob_ref, of_ref, lt_ref, o_scr, l_scr):
        n = pl.program_id(1)
        for k, d in enumerate(DILATIONS):
            @pl.when(n == k)
            def _(k=k, d=d):
                _unpermute_rows(o_scr.at[k], o_ref, d)
                _unpermute_rows(l_scr.at[k], l_ref, d)

        @pl.when(n == 2)
        def _():
            l0, l1, l2 = l_scr[0], l_scr[1], l_scr[2]
            m = jnp.maximum(jnp.maximum(l0, l1), l2)
            e0, e1, e2 = jnp.exp(l0 - m), jnp.exp(l1 - m), jnp.exp(l2 - m)
            den = e0 + e1 + e2
            o = (e0 / den) * o_scr[0] + (e1 / den) * o_scr[1] + (e2 / den) * o_scr[2]
            of_ref[...] = o
            ob_ref[...] = o.astype(BF16)
            lt_ref[...] = m + jnp.log(den)

    inb = pl.BlockSpec((None, s_len, LANES), lambda cb, n: (n, 0, cb))
    outb = pl.BlockSpec((s_len, LANES), lambda cb, n: (0, cb))
    return _call(
        body, grid=(4, 3), name=name, carry=carry, in_specs=[inb, inb], out_specs=[outb, outb, outb],
        out_shape=[jax.ShapeDtypeStruct((s_len, DL_W), BF16), jax.ShapeDtypeStruct((s_len, DL_W), F32),
                   jax.ShapeDtypeStruct((s_len, DL_W), F32)],
        scratch_shapes=[pltpu.VMEM((3, s_len, LANES), F32), pltpu.VMEM((3, s_len, LANES), F32)], args=(o_p, l_p))


def _dl_bwd_prep(dcat, o, lse, *, name):
    s_len = o.shape[0]

    def body(do_ref, o_ref, l_ref, dop_ref, st_ref, d_scr):
        n = pl.program_id(1)

        @pl.when(n == 0)
        def _():
            r0 = lax.broadcasted_iota(jnp.int32, (LANES, LANES), 0) // HD
            r1 = lax.broadcasted_iota(jnp.int32, (LANES, LANES), 1) // HD
            d_scr[...] = _dot_hilo(do_ref[...] * o_ref[...], (r0 == r1).astype(BF16))

        for k, d in enumerate(DILATIONS):
            @pl.when(n == k)
            def _(d=d):
                _permute_rows(dop_ref, do_ref, d, BF16)
                _permute_rows(st_ref.at[0], d_scr, d, F32)
                _permute_rows(st_ref.at[1], l_ref, d, F32)

    nat = lambda c0: pl.BlockSpec((s_len, LANES), lambda cb, n: (0, c0 + cb))
    return pl.pallas_call(
        body, grid=(4, 3), name=name, in_specs=[nat(4), nat(0), nat(0)],
        out_specs=[pl.BlockSpec((None, s_len, LANES), lambda cb, n: (n, 0, cb)),
                   pl.BlockSpec((2, None, s_len, LANES), lambda cb, n: (0, n, 0, cb))],
        out_shape=[jax.ShapeDtypeStruct((3, s_len, DL_W), BF16), jax.ShapeDtypeStruct((2, 3, s_len, DL_W), F32)],
        scratch_shapes=[pltpu.VMEM((s_len, LANES), F32)], compiler_params=_cp())(dcat, o, lse)


def _dl_bwd(qkv, dop, stats, *, name, carry=None):
    s_len = qkv.shape[2]

    def body(q_ref, k_ref, v_ref, do_ref, st_ref, cur_ref, prev_ref):
        n = pl.program_id(0)
        lane_h = _head_masks()[2]
        band_c, band_p = _dl_band(2 * BLK)

        def per_head(x):
            xr = pltpu.roll(x, HD, 1)
            return jnp.concatenate([jnp.where(lane_h, x, xr), jnp.where(lane_h, xr, x)], axis=0)

        @pl.loop(0, s_len // BLK, step=DL_UNROLL)
        def _(i0):
            blocks = [i0 + t for t in range(DL_UNROLL)]
            rows = [_dl_rows(i) for i in blocks]
            stage1 = []
            for cur, prev in rows:
                qs = _stack_heads(q_ref[cur, :] * 0.125, lane_h)
                dos = _stack_heads(do_ref[cur, :], lane_h)
                kc, kp, vc, vp = k_ref[cur, :], k_ref[prev, :], v_ref[cur, :], v_ref[prev, :]
                stage1.append((qs, dos, _dot_nt(qs, kc), _dot_nt(qs, kp), _dot_nt(dos, vc), _dot_nt(dos, vp)))
            stage2 = []
            for i, (cur, prev), (qs, dos, sc, sp, dpc, dpp) in zip(blocks, rows, stage1):
                lse, delta = per_head(st_ref[1, cur, :]), per_head(st_ref[0, cur, :])
                pc = jnp.where(band_c, jnp.exp(sc - lse), 0.0)
                pp = jnp.where(jnp.logical_and(band_p, jnp.logical_not(_dl_first(s_len, n, i))), jnp.exp(sp - lse), 0.0)
                stage2.append((pc.astype(BF16), pp.astype(BF16), (pc * (dpc - delta)).astype(BF16),
                               (pp * (dpp - delta)).astype(BF16)))
            for (cur, prev), (qs, dos, *_), (pc, pp, dsc, dsp) in zip(rows, stage1, stage2):
                dq = _dot(dsc, k_ref[cur, :]) + _dot(dsp, k_ref[prev, :])
                cur_ref[0, cur, :] = jnp.where(lane_h, dq[:BLK], dq[BLK:]) * 0.125
                cur_ref[1, cur, :] = _dot_tn(dsc, qs)
                cur_ref[2, cur, :] = _dot_tn(pc, dos)
                prev_ref[0, cur, :] = _dot_tn(dsp, qs)
                prev_ref[1, cur, :] = _dot_tn(pp, dos)

    return _call(
        body, grid=(3, 4), name=name, carry=carry,
        in_specs=_dl_in_specs(s_len) + [pl.BlockSpec((None, s_len, LANES), lambda n, hp: (n, 0, hp)),
                                        pl.BlockSpec((2, None, s_len, LANES), lambda n, hp: (0, n, 0, hp))],
        out_specs=[pl.BlockSpec((3, None, s_len, LANES), lambda n, hp: (0, n, 0, hp)),
                   pl.BlockSpec((2, None, s_len, LANES), lambda n, hp: (0, n, 0, hp))],
        out_shape=[jax.ShapeDtypeStruct((3, 3, s_len, DL_W), F32), jax.ShapeDtypeStruct((2, 3, s_len, DL_W), F32)],
        vmem_mb=56, args=(qkv, qkv, qkv, dop, stats))


def _dl_bwd_finish(cur, prev, cos, sin, *, name):
    s_len = cur.shape[2]

    def body(c_ref, p_ref, cos_ref, sin_ref, o_ref, p_scr, u_scr, acc):
        a, n = pl.program_id(0), pl.program_id(2)
        has_prev = jnp.where(a > 0, 1.0, 0.0)
        p_scr[...] = c_ref[...]
        p_scr[0:s_len - BLK, :] += has_prev * p_ref[BLK:, :]
        for k, d in enumerate(DILATIONS):
            @pl.when(n == k)
            def _(k=k, d=d):
                if k == 0:
                    acc[...] = p_scr[...]
                else:
                    _unpermute_rows(u_scr, p_scr, d)
                    acc[...] += u_scr[...]

        @pl.when(n == 2)
        def _():
            dy = acc[...]
            rot = a < 2
            o_ref[...] = (dy * jnp.where(rot, cos_ref[...], 1.0)
                          + _rot_half(dy * jnp.where(rot, sin_ref[...], 0.0))).astype(BF16)

    tab = pl.BlockSpec((s_len, LANES), lambda a, cb, n: (0, 0))
    return pl.pallas_call(
        body, grid=(3, 4, 3), name=name,
        in_specs=[pl.BlockSpec((None, None, s_len, LANES), lambda a, cb, n: (a, n, 0, cb)),
                  pl.BlockSpec((None, None, s_len, LANES), lambda a, cb, n: (jnp.maximum(a - 1, 0), n, 0, cb)),
                  tab, tab],
        out_specs=pl.BlockSpec((s_len, LANES), lambda a, cb, n: (0, 4 * a + cb)),
        out_shape=jax.ShapeDtypeStruct((s_len, 3 * DL_W), BF16),
        scratch_shapes=[pltpu.VMEM((s_len, LANES), F32)] * 3, compiler_params=_cp())(cur, prev, cos, sin)


XA_T = 256


def _xa_probs(q, k):
    s = _dot_nt(q, k) * (X_HD ** -0.5)
    e = jnp.exp(s - jnp.max(s, axis=1, keepdims=True))
    return e / jnp.sum(e, axis=1, keepdims=True)


def _xa_fwd(q, k, v, *, name):
    s_len, d = q.shape
    nm = k.shape[0]

    def body(q_ref, k_ref, v_ref, o_ref):
        for h in range(X_HEADS):
            cs = slice(h * X_HD, (h + 1) * X_HD)
            p = _xa_probs(q_ref[:, cs], k_ref[:, cs])
            o_ref[:, cs] = _dot(p.astype(BF16), v_ref[:, cs]).astype(BF16)

    row = pl.BlockSpec((XA_T, d), lambda i: (i, 0))
    full = pl.BlockSpec((nm, d), lambda i: (0, 0))
    return pl.pallas_call(body, grid=(s_len // XA_T,), name=name, in_specs=[row, full, full], out_specs=row,
                          out_shape=jax.ShapeDtypeStruct((s_len, d), BF16), compiler_params=_cp())(q, k, v)


def _xa_bwd(q, k, v, do, *, name, carry=None):
    s_len, d = q.shape
    nm = k.shape[0]

    def body(q_ref, k_ref, v_ref, do_ref, dq_ref, dk_ref, dv_ref):
        i = pl.program_id(0)
        for h in range(X_HEADS):
            cs = slice(h * X_HD, (h + 1) * X_HD)
            qh, kh, vh, doh = q_ref[:, cs], k_ref[:, cs], v_ref[:, cs], do_ref[:, cs]
            p = _xa_probs(qh, kh)
            dp = _dot_nt(doh, vh)
            ds = (p * (dp - jnp.sum(dp * p, axis=1, keepdims=True)) * (X_HD ** -0.5)).astype(BF16)
            dq_ref[:, cs] = _dot(ds, kh).astype(BF16)
            dkh, dvh = _dot_tn(ds, qh), _dot_tn(p.astype(BF16), doh)

            @pl.when(i == 0)
            def _(cs=cs, dkh=dkh, dvh=dvh):
                dk_ref[:, cs] = dkh
                dv_ref[:, cs] = dvh

            @pl.when(i > 0)
            def _(cs=cs, dkh=dkh, dvh=dvh):
                dk_ref[:, cs] += dkh
                dv_ref[:, cs] += dvh

    row = pl.BlockSpec((XA_T, d), lambda i: (i, 0))
    full = pl.BlockSpec((nm, d), lambda i: (0, 0))
    return _call(
        body, grid=(s_len // XA_T,), name=name, carry=carry, in_specs=[row, full, full, row],
        out_specs=[row, full, full],
        out_shape=[jax.ShapeDtypeStruct((s_len, d), BF16), jax.ShapeDtypeStruct((nm, d), F32),
                   jax.ShapeDtypeStruct((nm, d), F32)], args=(q, k, v, do))


FF_TM, FF_TN, FF_H = 512, 256, 8
GELU_K, GELU_C = 0.7978845608028654, 0.044715


FF_STRIP = 64


def _ff_conv(e_ref, w_ref, b_ref, rows, r0=0):
    return (w_ref[0:1, :] * e_ref[pl.ds(FF_H - 2 + r0, rows), :] + w_ref[1:2, :] * e_ref[pl.ds(FF_H - 1 + r0, rows), :]
            + w_ref[2:3, :] * e_ref[pl.ds(FF_H + r0, rows), :] + b_ref[...])


def _strips(total, size):
    return [(r0, min(size, total - r0)) for r0 in range(0, total, size)]


def _ff_gate_fwd(up, conv_w, conv_b, *, name, carry=None):
    s_len = up.shape[0]
    nj = D_FF // FF_TN

    def body(g_ref, v_ref, gp_ref, vp_ref, wg_ref, wv_ref, bg_ref, bv_ref, o_ref, eg, ev):
        i = pl.program_id(0)
        for e, cur, prev in ((eg, g_ref, gp_ref), (ev, v_ref, vp_ref)):
            e[0:FF_H, :] = jnp.where(i > 0, prev[...], 0.0)
            e[FF_H:, :] = cur[...]
        for r0, rows in _strips(FF_TM, FF_STRIP):
            gate = _ff_conv(eg, wg_ref, bg_ref, rows, r0)
            val = _ff_conv(ev, wv_ref, bv_ref, rows, r0)
            t = jnp.tanh(GELU_K * (gate + GELU_C * gate * gate * gate))
            o_ref[r0:r0 + rows, :] = (0.5 * gate * (1.0 + t) * val).astype(BF16)

    cur = lambda c0: pl.BlockSpec((FF_TM, FF_TN), lambda i, j: (i, c0 + j))
    prev = lambda c0: pl.BlockSpec((FF_H, FF_TN), lambda i, j: (jnp.maximum(i * (FF_TM // FF_H) - 1, 0), c0 + j))
    par = lambda r, c0: pl.BlockSpec((r, FF_TN), lambda i, j: (0, c0 + j))
    return _call(
        body, grid=(s_len // FF_TM, nj), name=name, carry=carry,
        in_specs=[cur(0), cur(nj), prev(0), prev(nj), par(3, 0), par(3, nj), par(1, 0), par(1, nj)],
        out_specs=[cur(0)], out_shape=[jax.ShapeDtypeStruct((s_len, D_FF), BF16)],
        scratch_shapes=[pltpu.VMEM((FF_TM + FF_H, FF_TN), F32)] * 2,
        args=(up, up, up, up, conv_w, conv_w, conv_b, conv_b))


def _ff_gate_bwd(up, dact, conv_w, conv_b, *, name, carry=None):
    s_len = up.shape[0]
    nj = D_FF // FF_TN
    last = s_len // FF_TM - 1
    ext = FF_TM + FF_H

    def body(g_ref, v_ref, gp_ref, vp_ref, gn_ref, vn_ref, da_ref, dan_ref, wg_ref, wv_ref, bg_ref, bv_ref,
             dg_ref, dv_ref, dw_ref, db_ref, eg, ev, sg, sv):
        i = pl.program_id(1)
        for e, cur, prev, nxt in ((eg, g_ref, gp_ref, gn_ref), (ev, v_ref, vp_ref, vn_ref)):
            e[0:FF_H, :] = jnp.where(i > 0, prev[...], 0.0)
            e[FF_H:FF_H + FF_TM, :] = cur[...]
            e[FF_H + FF_TM:, :] = nxt[...]
        for r0, rows in _strips(ext, FF_STRIP):
            gate = _ff_conv(eg, wg_ref, bg_ref, rows, r0)
            val = _ff_conv(ev, wv_ref, bv_ref, rows, r0)
            dact = da_ref[r0:r0 + rows, :] if r0 < FF_TM else jnp.where(i < last, dan_ref[...], 0.0)
            t = jnp.tanh(GELU_K * (gate + GELU_C * gate * gate * gate))
            half = 0.5 * (1.0 + t)
            dgelu = half + 0.5 * gate * (1.0 - t * t) * GELU_K * (1.0 + 3.0 * GELU_C * gate * gate)
            sg[r0:r0 + rows, :] = dact * val * dgelu
            sv[r0:r0 + rows, :] = dact * (gate * half)
        for part, (s, e, w_ref, out) in enumerate(((sg, eg, wg_ref, dg_ref), (sv, ev, wv_ref, dv_ref))):
            taps, bias = [jnp.zeros((8, FF_TN), F32)] * 3, jnp.zeros((8, FF_TN), F32)
            for r0, rows in _strips(FF_TM, FF_STRIP):
                d0 = s[pl.ds(r0, rows), :]
                out[r0:r0 + rows, :] = (w_ref[2:3, :] * d0 + w_ref[1:2, :] * s[pl.ds(r0 + 1, rows), :]
                                        + w_ref[0:1, :] * s[pl.ds(r0 + 2, rows), :]).astype(BF16)
                taps = [taps[k] + _rowsum8(d0 * e[pl.ds(FF_H - 2 + k + r0, rows), :]) for k in range(3)]
                bias = bias + _rowsum8(d0)
            _acc_out(dw_ref.at[part], i, jnp.concatenate(taps, axis=0))
            _acc_out(db_ref.at[part], i, bias)

    cur = lambda c0: pl.BlockSpec((FF_TM, FF_TN), lambda j, i: (i, c0 + j))
    prev = lambda c0: pl.BlockSpec((FF_H, FF_TN), lambda j, i: (jnp.maximum(i * (FF_TM // FF_H) - 1, 0), c0 + j))
    nxt = lambda c0: pl.BlockSpec(
        (FF_H, FF_TN), lambda j, i: (jnp.minimum((i + 1) * (FF_TM // FF_H), s_len // FF_H - 1), c0 + j))
    par = lambda r, c0: pl.BlockSpec((r, FF_TN), lambda j, i: (0, c0 + j))
    return _call(
        body, grid=(nj, s_len // FF_TM), name=name, carry=carry,
        in_specs=[cur(0), cur(nj), prev(0), prev(nj), nxt(0), nxt(nj), cur(0), nxt(0),
                  par(3, 0), par(3, nj), par(1, 0), par(1, nj)],
        out_specs=[cur(0), cur(0), pl.BlockSpec((2, 24, FF_TN), lambda j, i: (0, 0, j)),
                   pl.BlockSpec((2, 8, FF_TN), lambda j, i: (0, 0, j))],
        out_shape=[jax.ShapeDtypeStruct((s_len, D_FF), BF16), jax.ShapeDtypeStruct((s_len, D_FF), BF16),
                   jax.ShapeDtypeStruct((2, 24, D_FF), F32), jax.ShapeDtypeStruct((2, 8, D_FF), F32)],
        scratch_shapes=[pltpu.VMEM((FF_TM + 2 * FF_H, FF_TN), F32)] * 2 + [pltpu.VMEM((ext, FF_TN), F32)] * 2,
        args=(up, up, up, up, up, up, dact, dact, conv_w, conv_w, conv_b, conv_b))


def _place():
    x, y, c = lax.axis_index("x"), lax.axis_index("y"), lax.axis_index("c")
    return x, y, c, [(1 - x, y), (x, 1 - y), (1 - x, 1 - y)]


def _remote(src, dst, send_sem, recv_sem, dev):
    return pltpu.make_async_remote_copy(src_ref=src, dst_ref=dst, send_sem=send_sem, recv_sem=recv_sem,
                                        device_id=dev, device_id_type=MESH)


_ANY = pl.BlockSpec(memory_space=pl.ANY)


N_SEMS = 8
SEM_BASE_2 = 4


class _Exchange:
    def __init__(self, operands, out_shapes, start, wait, aliases=None):
        self.operands, self.out_shapes, self.start, self.wait = list(operands), list(out_shapes), start, wait
        self.aliases = aliases or {}


def _sem_scratch():
    return [pltpu.SemaphoreType.DMA((N_SEMS,)), pltpu.SemaphoreType.DMA((N_SEMS,)), pltpu.SemaphoreType.DMA]


def _run_exchange(ex, *, name):
    k, n = len(ex.operands), len(ex.out_shapes)

    def body(*refs):
        ins, outs, sems = refs[:k], refs[k:k + n], refs[k + n:]
        ex.start(ins, outs, *sems)
        ex.wait(ins, outs, *sems)

    return pl.pallas_call(body, name=name, in_specs=[_ANY] * k, out_specs=[_ANY] * n, out_shape=ex.out_shapes,
                          scratch_shapes=_sem_scratch(), input_output_aliases=ex.aliases,
                          compiler_params=_cp(16))(*ex.operands)


def _call(body, *, grid, in_specs, out_specs, out_shape, args, name, scratch_shapes=(), vmem_mb=48, carry=None):
    scratch_shapes = list(scratch_shapes)
    if carry is None:
        return pl.pallas_call(body, grid=grid, name=name, in_specs=in_specs, out_specs=out_specs, out_shape=out_shape,
                              scratch_shapes=scratch_shapes, compiler_params=_cp(vmem_mb))(*args)
    n_in, n_out, n_scr = len(in_specs), len(out_shape), len(scratch_shapes)
    k_in, k_out = len(carry.operands), len(carry.out_shapes)

    def wrapped(*refs):
        ins, refs = refs[:n_in], refs[n_in:]
        cin, refs = refs[:k_in], refs[k_in:]
        outs, refs = refs[:n_out], refs[n_out:]
        cout, refs = refs[:k_out], refs[k_out:]
        scratch, sems = refs[:n_scr], refs[n_scr:]
        ids = [pl.program_id(a) for a in range(len(grid))]
        first = functools.reduce(jnp.logical_and, [i == 0 for i in ids])
        last = functools.reduce(jnp.logical_and, [i == g - 1 for i, g in zip(ids, grid)])

        @pl.when(first)
        def _():
            carry.start(cin, cout, *sems)

        body(*ins, *outs, *scratch)

        @pl.when(last)
        def _():
            carry.wait(cin, cout, *sems)

    aliases = {n_in + i: n_out + o for i, o in carry.aliases.items()}
    return pl.pallas_call(
        wrapped, grid=grid, name=name, in_specs=list(in_specs) + [_ANY] * k_in,
        out_specs=list(out_specs) + [_ANY] * k_out, out_shape=list(out_shape) + carry.out_shapes,
        scratch_shapes=scratch_shapes + _sem_scratch(), input_output_aliases=aliases,
        compiler_params=_cp(vmem_mb))(*args, *carry.operands)


def _half_rows(ref_rows, c):
    half = ref_rows // 2
    return pl.ds(c * half, half)


def _ex_join(a, b):
    ka, na = len(a.operands), len(a.out_shapes)

    def start(ins, outs, *sems):
        a.start(ins[:ka], outs[:na], *sems)
        b.start(ins[ka:], outs[na:], *sems)

    def wait(ins, outs, *sems):
        a.wait(ins[:ka], outs[:na], *sems)
        b.wait(ins[ka:], outs[na:], *sems)

    aliases = dict(a.aliases)
    aliases.update({ka + i: na + o for i, o in b.aliases.items()})
    return _Exchange(a.operands + b.operands, a.out_shapes + b.out_shapes, start, wait, aliases)


def _ex_gather(pack, r0, rl, base=0):
    def copies(ins, outs, send, recv):
        x, y, c, chips = _place()
        rows = _half_rows(rl, c)
        src = ins[0].at[pl.ds(r0 + c * (rl // 2), rl // 2)]
        sends = [_remote(src, outs[0].at[2 * x + y, rows], send.at[base + k], recv.at[base + k], (px, py, c))
                 for k, (px, py) in enumerate(chips)]
        lands = [_remote(src, outs[0].at[2 * px + py, rows], send.at[base + k], recv.at[base + k], (px, py, c))
                 for k, (px, py) in enumerate(chips)]
        return sends, lands

    def mine(ins, outs, local):
        x, y, _, _ = _place()
        return pltpu.make_async_copy(ins[0].at[pl.ds(r0, rl)], outs[0].at[2 * x + y], local)

    def start(ins, outs, send, recv, local):
        mine(ins, outs, local).start()
        for cp in copies(ins, outs, send, recv)[0]:
            cp.start()

    def wait(ins, outs, send, recv, local):
        sends, lands = copies(ins, outs, send, recv)
        for cp in lands:
            cp.wait_recv()
        for cp in sends:
            cp.wait_send()
        mine(ins, outs, local).wait()

    return _Exchange([pack], [jax.ShapeDtypeStruct((4, rl, pack.shape[1]), pack.dtype)], start, wait)


def _ex_gather_forward(g, base=0):
    rl = g.shape[1]

    def copies(outs, send, recv):
        x, y, c, chips = _place()
        slabs = [(outs[0].at[2 * px + py, _half_rows(rl, c)], outs[0].at[2 * px + py, _half_rows(rl, 1 - c)])
                 for px, py in chips]
        sends = [_remote(a, a, send.at[base + k], recv.at[base + k], (x, y, 1 - c)) for k, (a, _) in enumerate(slabs)]
        lands = [_remote(b, b, send.at[base + k], recv.at[base + k], (x, y, 1 - c)) for k, (_, b) in enumerate(slabs)]
        return sends, lands

    def start(ins, outs, send, recv, local):
        for cp in copies(outs, send, recv)[0]:
            cp.start()

    def wait(ins, outs, send, recv, local):
        sends, lands = copies(outs, send, recv)
        for cp in lands:
            cp.wait_recv()
        for cp in sends:
            cp.wait_send()

    return _Exchange([g], [jax.ShapeDtypeStruct(g.shape, g.dtype)], start, wait, aliases={0: 0})


def _ex_swap_halves(gw, base=0):
    nb, rl, d = gw.shape

    def copies(ins, outs, send, recv):
        x, y, c, _ = _place()
        return [_remote(ins[0].at[j, _half_rows(rl, 1 - c)], outs[0].at[j], send.at[base + j], recv.at[base + j],
                        (x, y, 1 - c)) for j in range(nb)]

    def start(ins, outs, send, recv, local):
        for cp in copies(ins, outs, send, recv):
            cp.start()

    def wait(ins, outs, send, recv, local):
        for cp in copies(ins, outs, send, recv):
            cp.wait()

    return _Exchange([gw], [jax.ShapeDtypeStruct((nb, rl // 2, d), gw.dtype)], start, wait)


def _chip_sum(gw, got, c_arr, *, name):
    nchip, half, d = got.shape
    tr = _tile(half, 512)

    def body(c_ref, a_ref, b_ref, o32_ref, o16_ref):
        s = a_ref[...] + b_ref[...]
        o32_ref[...] = s
        o16_ref[...] = s.astype(BF16)

    blk = pl.BlockSpec((None, tr, d), lambda j, i, c_ref: (j, i, 0))
    return pl.pallas_call(
        body, name=name,
        grid_spec=pltpu.PrefetchScalarGridSpec(
            num_scalar_prefetch=1, grid=(nchip, half // tr),
            in_specs=[pl.BlockSpec((None, tr, d), lambda j, i, c_ref: (j, c_ref[0] * (half // tr) + i, 0)), blk],
            out_specs=[blk, blk]),
        out_shape=[jax.ShapeDtypeStruct((nchip, half, d), F32), jax.ShapeDtypeStruct((nchip, half, d), BF16)],
        compiler_params=_cp())(c_arr, gw, got)


def _ex_scatter(s16, base=0):
    def copies(ins, outs, send, recv):
        x, y, c, chips = _place()
        return [_remote(ins[0].at[2 * px + py], outs[0].at[k], send.at[base + k], recv.at[base + k], (px, py, c))
                for k, (px, py) in enumerate(chips)]

    def start(ins, outs, send, recv, local):
        for cp in copies(ins, outs, send, recv):
            cp.start()

    def wait(ins, outs, send, recv, local):
        for cp in copies(ins, outs, send, recv):
            cp.wait()

    return _Exchange([s16], [jax.ShapeDtypeStruct((3,) + s16.shape[1:], s16.dtype)], start, wait)


def _mesh_sum(s32, got, j_arr, *, name):
    _, rl, d = s32.shape
    tr = _tile(rl, 512)

    def body(j_ref, a_ref, b_ref, o_ref):
        o_ref[...] = ((a_ref[...] + b_ref[0].astype(F32)) + b_ref[1].astype(F32)) + b_ref[2].astype(F32)

    return pl.pallas_call(
        body, name=name,
        grid_spec=pltpu.PrefetchScalarGridSpec(
            num_scalar_prefetch=1, grid=(rl // tr,),
            in_specs=[pl.BlockSpec((None, tr, d), lambda i, j_ref: (j_ref[0], i, 0)),
                      pl.BlockSpec((3, tr, d), lambda i, j_ref: (0, i, 0))],
            out_specs=pl.BlockSpec((tr, d), lambda i, j_ref: (i, 0))),
        out_shape=jax.ShapeDtypeStruct((rl, d), F32), compiler_params=_cp())(j_arr, s32, got)


def _ex_share_halves(ghalf):
    half, d = ghalf.shape

    def copies(ins, outs, send, recv, local):
        x, y, c, _ = _place()
        there = outs[0].at[_half_rows(2 * half, c)]
        back = outs[0].at[_half_rows(2 * half, 1 - c)]
        return (_remote(ins[0], there, send.at[0], recv.at[0], (x, y, 1 - c)),
                _remote(ins[0], back, send.at[0], recv.at[0], (x, y, 1 - c)), pltpu.make_async_copy(ins[0], there, local))

    def start(ins, outs, send, recv, local):
        out, _, mine = copies(ins, outs, send, recv, local)
        mine.start()
        out.start()

    def wait(ins, outs, send, recv, local):
        out, back, mine = copies(ins, outs, send, recv, local)
        back.wait_recv()
        out.wait_send()
        mine.wait()

    return _Exchange([ghalf], [jax.ShapeDtypeStruct((2 * half, d), ghalf.dtype)], start, wait)


class _ReduceScatter:
    def __init__(self, gw, c_arr, j_arr, tag):
        self.gw, self.c_arr, self.j_arr, self.tag = gw, c_arr, j_arr, tag

    def swap(self, base=0):
        return _ex_swap_halves(self.gw, base)

    def after_swap(self, got, base=0):
        self.s32, s16 = _chip_sum(self.gw, got, self.c_arr, name=f"rs_chip_sum{self.tag}")
        return _ex_scatter(s16, base)

    def after_scatter(self, got16):
        ghalf = _mesh_sum(self.s32, got16, self.j_arr, name=f"rs_mesh_sum{self.tag}")
        return _run_exchange(_ex_share_halves(ghalf), name=f"rs_share{self.tag}")[0]

    def run(self):
        got, = _run_exchange(self.swap(), name=f"rs_swap{self.tag}")
        got16, = _run_exchange(self.after_swap(got), name=f"rs_scatter{self.tag}")
        return self.after_scatter(got16)


def _all_reduce_small(vec, *, name):
    rows, d = vec.shape

    def body(x_ref, o_ref, gat, send_sems, recv_sems, local_sem):
        x, y, c, chips = _place()
        me, sibling = (x, y, c), (x, y, 1 - c)

        def slot(px, py, pc):
            return gat.at[4 * px + 2 * py + pc]

        def copy(k, block, to, src=None):
            return _remote(slot(*block) if src is None else src, slot(*block), send_sems.at[k], recv_sems.at[k], to)

        mine = pltpu.make_async_copy(x_ref, slot(*me), local_sem)
        mine.start()
        first = [copy(0, me, sibling, src=x_ref)]
        first += [copy(1 + j, me, (*chip, c), src=x_ref) for j, chip in enumerate(chips)]
        for cp in first:
            cp.start()
        passed = [copy(4 + j, (*chip, c), sibling) for j, chip in enumerate(chips)]
        for j, chip in enumerate(chips):
            copy(1 + j, (*chip, c), me).wait_recv()
            passed[j].start()
        copy(0, sibling, me).wait_recv()
        for j, chip in enumerate(chips):
            copy(4 + j, (*chip, 1 - c), me).wait_recv()
        for cp in first + passed:
            cp.wait_send()
        mine.wait()
        acc = gat[0]
        for dev in range(1, 8):
            acc = acc + gat[dev]
        o_ref[...] = acc

    vm = pl.BlockSpec(memory_space=pltpu.VMEM)
    return pl.pallas_call(
        body, name=name, in_specs=[vm], out_specs=vm, out_shape=jax.ShapeDtypeStruct((rows, d), F32),
        scratch_shapes=[pltpu.VMEM((8, rows, d), F32), pltpu.SemaphoreType.DMA((7,)), pltpu.SemaphoreType.DMA((7,)),
                        pltpu.SemaphoreType.DMA],
        compiler_params=_cp(32))(vec)


COL_SHARDED = ("w_in", "ffn_w_up")


def _to_pack_rows(name, shard):
    return shard.reshape(-1, D_MODEL)


def _full_from_blocks(name, blocks):
    rows = blocks.shape[1]
    if name in COL_SHARDED:
        return blocks.reshape(4, D_MODEL, rows).transpose(1, 0, 2).reshape(D_MODEL, 4 * rows)
    return blocks.reshape(4 * rows, D_MODEL)


def _blocks_from_full(name, full):
    if name in COL_SHARDED:
        cols = full.shape[1] // 4
        return full.reshape(D_MODEL, 4, cols).transpose(1, 0, 2).reshape(4, cols, D_MODEL)
    return full.reshape(4, full.shape[0] // 4, D_MODEL)


def _row(v):
    return v.reshape(1, -1)


SMALL = (("mix_norm_pre", (1024,), None), ("cv_w", (31, 256), 1), ("cv_b", (256,), None), ("cv_ln_g", (256,), None),
         ("cv_ln_b", (256,), None), ("cv_pw_w", (256, 256), 0), ("cv_pw_b", (256,), None),
         ("mix_norm_post", (1024,), None), ("x_norm_pre", (1024,), None), ("mem_norm", (1024,), None),
         ("x_norm_post", (1024,), None), ("ffn_norm_pre", (1024,), None), ("ffn_conv_w", (3, 5632), 1),
         ("ffn_conv_b", (5632,), None), ("ffn_norm_post", (1024,), None))
BIG = tuple(n for n, _ in PACK_ROWS)
WEIGHT_ORDER = ("mix_norm_pre", "w_in", "cv_w", "cv_b", "cv_ln_g", "cv_ln_b", "cv_pw_w", "cv_pw_b", "w_out",
                "mix_norm_post", "x_norm_pre", "mem_norm", "x_wq", "x_wk", "x_wv", "x_wo", "x_norm_post",
                "ffn_norm_pre", "ffn_w_up", "ffn_conv_w", "ffn_conv_b", "ffn_w_down", "ffn_norm_post")


def _flat_rows(parts):
    v = jnp.concatenate([p.reshape(-1) for p in parts])
    rows = -(-v.shape[0] // (8 * D_MODEL)) * 8
    return jnp.pad(v, (0, rows * D_MODEL - v.shape[0])).reshape(rows, D_MODEL)


def _small_to_rows(blocks):
    v = jnp.concatenate([b.reshape(-1) for b in blocks])
    return jnp.pad(v, (0, SMALL_ROWS * D_MODEL - v.shape[0])).reshape(SMALL_ROWS, D_MODEL)


def _small_from_rows(rows):
    flat, out, off = rows.reshape(-1), [], 0
    for _, shape, _ in SHARDED_SMALL:
        size = int(np.prod(shape))
        out.append(flat[off:off + size].reshape(shape))
        off += size
    return out


def _chip_block(full, j, shape, axis):
    return lax.slice_in_dim(full, j * shape[axis], (j + 1) * shape[axis], axis=axis)


REST_GROUP = ("w_in", "w_out")
XA_GROUP = ("x_wq", "x_wk", "x_wv", "x_wo")
FFN_GROUP = ("ffn_w_up", "ffn_w_down")


class _Weights:
    FIRST = (0, 768)
    OWN = ((768, 1024), (1792, 1664), (3456, 704))
    NEXT = ((0, 1024), (1024, 1024), (2048, 1408), (3456, 704))
    SLOTS = ("mix_in", "sb_fwd", "dl_fwd", "dl_mix", "ffn_up", "ffn_gate", "ffn_down")

    def __init__(self, packs):
        self.packs, self.pieces, self.landed, self.plan = packs, {}, None, {}
        for slot, piece in zip(self.SLOTS[:3], self.OWN):
            self.plan[(0, slot)] = (0,) + piece
        for l in range(len(packs) - 1):
            for slot, piece in zip(self.SLOTS[3:], self.NEXT):
                self.plan[(l, slot)] = (l + 1,) + piece
        first = _run_exchange(_ex_gather(packs[0], *self.FIRST), name="gather_first")[0]
        self.pieces[(0,) + self.FIRST] = _run_exchange(_ex_gather_forward(first), name="gather_first_forward")[0]

    def ride(self, layer, slot, call):
        start, todo, ex = self.plan.get((layer, slot)), [], None
        if start is not None:
            ex = _ex_gather(self.packs[start[0]], start[1], start[2])
            todo.append(("landed", start))
        if self.landed is not None:
            key, buf = self.landed
            forward = _ex_gather_forward(buf, SEM_BASE_2 if ex is not None else 0)
            ex = forward if ex is None else _ex_join(ex, forward)
            todo.append(("piece", key))
            self.landed = None
        outs = list(call(carry=ex))
        n = len(outs) - len(todo)
        for (kind, key), buf in zip(todo, outs[n:]):
            if kind == "landed":
                self.landed = (key, buf)
            else:
                self.pieces[key] = buf
        return outs[:n]

    def rows_of(self, layer, name):
        off = 0
        for n, rows in WEIGHT_PACK:
            if n == name:
                break
            off += rows
        for (l, r0, nrows), buf in self.pieces.items():
            if l == layer and r0 <= off < r0 + nrows:
                return buf[:, off - r0:off - r0 + rows, :]
        raise KeyError(f"{name} of layer {layer} is not gathered yet")

    def weight(self, layer, name):
        return _full_from_blocks(name, self.rows_of(layer, name))

    def small(self, layer):
        planes = lax.bitcast_convert_type(self.rows_of(layer, "small").astype(jnp.bfloat16), jnp.uint16)
        planes = planes.astype(jnp.uint32)
        bits = (planes[:, :SMALL_ROWS] << 16) | planes[:, SMALL_ROWS:]
        per_chip = [_small_from_rows(r) for r in lax.bitcast_convert_type(bits, F32)]
        return {n: jnp.concatenate([blocks[k] for blocks in per_chip], axis=axis)
                for k, (n, _, axis) in enumerate(SHARDED_SMALL)}


class _Params:
    def __init__(self, weights, layer, small):
        self.weights, self.layer, self.small, self.cache = weights, layer, small, {}

    def __getitem__(self, name):
        if name in self.small:
            return self.small[name]
        if name not in self.cache:
            if name in [n for n, _, _ in SHARDED_SMALL]:
                self.cache.update(self.weights.small(self.layer))
            else:
                self.cache[name] = self.weights.weight(self.layer, name)
        return self.cache[name]


def _layer_fwd(h0, mem, p, cos, sin, tag, ride):
    sv = {"h0": h0}
    n1, u = ride("mix_in", functools.partial(_rms_mm, h0, _row(p["mix_norm_pre"]), p["w_in"], tm=1024, tn=1408,
                                             out_dtype=F32, name=f"mix_in{tag}"))
    a_out, = ride("sb_fwd", functools.partial(_sb_fwd, u, name=f"sb_fwd{tag}"))
    b_out, c = _cv_fwd(u, p["cv_w"], _row(p["cv_b"]), _row(p["cv_ln_g"]), _row(p["cv_ln_b"]),
                       p["cv_pw_w"].astype(BF16), _row(p["cv_pw_b"]), name=f"cv_fwd{tag}")
    qkv = _rope_perm(u, cos, sin, name=f"rope_perm{tag}")
    o_p, l_p = ride("dl_fwd", functools.partial(_dl_fwd, qkv, name=f"dl_fwd{tag}"))
    c_out, o_dl, lse = ride("dl_mix", functools.partial(_dl_mix, o_p, l_p, name=f"dl_mix{tag}"))
    cat = jnp.concatenate([a_out, b_out, c_out], axis=1)
    y1, h1 = _mm_post(cat, p["w_out"], h0, _row(p["mix_norm_post"]), tm=512, name=f"mix_out{tag}")
    sv.update(n1=n1, u=u, c=c, qkv=qkv, o_dl=o_dl, lse=lse, cat=cat, y1=y1, h1=h1)

    n2, q = _rms_mm(h1, _row(p["x_norm_pre"]), p["x_wq"], tm=512, tn=1024, out_dtype=BF16, name=f"xa_q{tag}")
    wkv = jnp.concatenate([p["x_wk"], p["x_wv"]], axis=1)
    mem_n, kv = _rms_mm(mem, _row(p["mem_norm"]), wkv, tm=mem.shape[0], tn=1024, out_dtype=BF16, name=f"xa_kv{tag}")
    k, v = kv[:, :D_MODEL], kv[:, D_MODEL:]
    o_x = _xa_fwd(q, k, v, name=f"xa_fwd{tag}")
    y2, h2 = _mm_post(o_x, p["x_wo"], h1, _row(p["x_norm_post"]), tm=512, name=f"xa_out{tag}")
    sv.update(n2=n2, q=q, mem_n=mem_n, k=k, v=v, o_x=o_x, y2=y2, h2=h2, wkv=wkv)

    n3, up = ride("ffn_up", functools.partial(_rms_mm, h2, _row(p["ffn_norm_pre"]), p["ffn_w_up"], tm=1024, tn=1408,
                                              out_dtype=F32, name=f"ffn_up{tag}"))
    act, = ride("ffn_gate", functools.partial(_ff_gate_fwd, up, p["ffn_conv_w"], _row(p["ffn_conv_b"]),
                                              name=f"ffn_gate{tag}"))
    y3, h3 = ride("ffn_down", functools.partial(_mm_post, act, p["ffn_w_down"], h2, _row(p["ffn_norm_post"]), tm=512,
                                                name=f"ffn_down{tag}"))
    sv.update(n3=n3, up=up, act=act, y3=y3)
    return h3, sv


def _layer_bwd(dh3, mem, p, sv, cos, sin, tag, riding, new_rs):
    g = {}
    s8 = lambda part: part.sum(axis=0)
    rode = None

    dy3, dgp = _rms_bwd(sv["y3"], _row(p["ffn_norm_post"]), dh3, None, out_dtype=BF16, tm=512, name=f"ffn_post_b{tag}")
    g["ffn_norm_post"] = s8(dgp)
    dact = _mm_nt(dy3, p["ffn_w_down"], tm=512, tn=1408, out_dtype=F32, name=f"ffn_down_bx{tag}")
    g["ffn_w_down"] = _mm_tn(sv["act"], dy3, tk=1408, tn=1024, tm=2048, name=f"ffn_down_bw{tag}")
    dgu, dvu, dcw, dcb, *got = _ff_gate_bwd(sv["up"], dact, p["ffn_conv_w"], _row(p["ffn_conv_b"]),
                                            name=f"ffn_gate_b{tag}", carry=riding.swap() if riding else None)
    scatter = riding.after_swap(got[0]) if riding else None
    g["ffn_conv_w"] = jnp.concatenate([dcw[0], dcw[1]], axis=1).reshape(3, 8, 2 * D_FF).sum(axis=1)
    g["ffn_conv_b"] = jnp.concatenate([dcb[0], dcb[1]], axis=1).sum(axis=0)
    dup = jnp.concatenate([dgu, dvu], axis=1)
    dn3 = _mm_nt(dup, p["ffn_w_up"], tm=256, tn=512, out_dtype=F32, name=f"ffn_up_bx{tag}")
    g["ffn_w_up"] = _mm_tn(sv["n3"], dup, tk=512, tn=1408, tm=2048, name=f"ffn_up_bw{tag}")
    ffn_rs = new_rs(FFN_GROUP, g, f"{tag}_ffn")
    dh2, dgp = _rms_bwd(sv["h2"], _row(p["ffn_norm_pre"]), dn3, dh3, out_dtype=F32, tm=512, name=f"ffn_pre_b{tag}")
    g["ffn_norm_pre"] = s8(dgp)

    dy2, dgp = _rms_bwd(sv["y2"], _row(p["x_norm_post"]), dh2, None, out_dtype=BF16, tm=512, name=f"xa_post_b{tag}")
    g["x_norm_post"] = s8(dgp)
    do_x = _mm_nt(dy2, p["x_wo"], tm=512, tn=1024, out_dtype=BF16, name=f"xa_out_bx{tag}")
    g["x_wo"] = _mm_tn(sv["o_x"], dy2, tk=512, tn=1024, tm=2048, name=f"xa_out_bw{tag}")
    dq, dk, dv, got = _xa_bwd(sv["q"], sv["k"], sv["v"], do_x, name=f"xa_bwd{tag}", carry=ffn_rs.swap())
    ffn_scatter = ffn_rs.after_swap(got)
    dn2 = _mm_nt(dq, p["x_wq"], tm=512, tn=1024, out_dtype=F32, name=f"xa_q_bx{tag}")
    g["x_wq"] = _mm_tn(sv["n2"], dq, tk=512, tn=1024, tm=2048, name=f"xa_q_bw{tag}")
    dkv = jnp.concatenate([dk, dv], axis=1).astype(BF16)
    nm = mem.shape[0]
    dmem_n = _mm_nt(dkv, sv["wkv"], tm=nm, tn=1024, out_dtype=F32, name=f"xa_kv_bx{tag}")
    dwkv = _mm_tn(sv["mem_n"], dkv, tk=512, tn=2048, tm=nm, name=f"xa_kv_bw{tag}")
    g["x_wk"], g["x_wv"] = dwkv[:, :D_MODEL], dwkv[:, D_MODEL:]
    _, dgp = _rms_bwd(mem, _row(p["mem_norm"]), dmem_n, None, out_dtype=BF16, tm=nm, name=f"xa_mem_b{tag}")
    g["mem_norm"] = s8(dgp)
    xa_rs = new_rs(XA_GROUP, g, f"{tag}_xa")
    dh1, dgp, got = _rms_bwd(sv["h1"], _row(p["x_norm_pre"]), dn2, dh2, out_dtype=F32, tm=512, name=f"xa_pre_b{tag}",
                             carry=xa_rs.swap())
    xa_scatter = xa_rs.after_swap(got, SEM_BASE_2 if riding else 0)
    g["x_norm_pre"] = s8(dgp)

    dy1, dgp = _rms_bwd(sv["y1"], _row(p["mix_norm_post"]), dh1, None, out_dtype=BF16, tm=512, name=f"mix_post_b{tag}")
    g["mix_norm_post"] = s8(dgp)
    dcat = _mm_nt(dy1, p["w_out"], tm=512, tn=1024, out_dtype=F32, name=f"mix_out_bx{tag}")
    g["w_out"] = _mm_tn(sv["cat"], dy1, tk=512, tn=1024, tm=2048, name=f"mix_out_bw{tag}")
    u = sv["u"]
    dq_sb, dk_sb, dv_sb, *got = _sb_bwd(u, dcat, name=f"sb_bwd{tag}",
                                        carry=_ex_join(scatter, xa_scatter) if riding else xa_scatter)
    if riding:
        rode = riding.after_scatter(got[0])
    xa_rows = xa_rs.after_scatter(got[-1])
    pw_b16 = p["cv_pw_w"].astype(BF16)
    dc, dpw, vec = _cv_bwd_local(sv["c"], dcat, _row(p["cv_ln_g"]), _row(p["cv_ln_b"]), pw_b16, name=f"cv_bwd_a{tag}")
    g["cv_pw_w"] = dpw
    vec = vec.reshape(3, 8, CV_W).sum(axis=1)
    g["cv_pw_b"], g["cv_ln_g"], g["cv_ln_b"] = vec[0], vec[1], vec[2]
    du_cv, dcw, dcb = _cv_bwd_conv(u, dc, p["cv_w"], name=f"cv_bwd_b{tag}")
    g["cv_w"] = dcw.reshape(CV_K, 8, CV_W).sum(axis=1)
    g["cv_b"] = dcb.sum(axis=0)
    dop, stats = _dl_bwd_prep(dcat, sv["o_dl"], sv["lse"], name=f"dl_prep_b{tag}")
    cur, prev, got = _dl_bwd(sv["qkv"], dop, stats, name=f"dl_bwd{tag}", carry=ffn_scatter)
    ffn_rows = ffn_rs.after_scatter(got)
    du_dl = _dl_bwd_finish(cur, prev, cos, sin, name=f"dl_fin_b{tag}")
    du = jnp.concatenate([dq_sb.astype(BF16), dk_sb.astype(BF16), dv_sb.astype(BF16), du_cv, du_dl], axis=1)
    dn1 = _mm_nt(du, p["w_in"], tm=512, tn=512, out_dtype=F32, name=f"mix_in_bx{tag}")
    g["w_in"] = _mm_tn(sv["n1"], du, tk=512, tn=1408, tm=2048, name=f"mix_in_bw{tag}")
    dh0, dgp = _rms_bwd(sv["h0"], _row(p["mix_norm_pre"]), dn1, dh1, out_dtype=F32, tm=512, name=f"mix_pre_b{tag}")
    g["mix_norm_pre"] = s8(dgp)
    return dh0, g, (xa_rows, ffn_rows), rode


def _step(x, mem, positions, loss_target, w, m, v):
    depth = w["w_in"].shape[0]
    xi, yi, ci = lax.axis_index("x"), lax.axis_index("y"), lax.axis_index("c")
    chip = 2 * xi + yi
    h = x[0]
    mem0 = mem[0]
    s_len = h.shape[0]

    def pack_rows(n, l):
        if n == "small":
            bits = lax.bitcast_convert_type(_small_to_rows([w[name][l] for name, _, _ in SHARDED_SMALL]), jnp.uint32)
            planes = [(bits >> 16).astype(jnp.uint16), (bits & 0xFFFF).astype(jnp.uint16)]
            return jnp.concatenate([lax.bitcast_convert_type(p, jnp.bfloat16) for p in planes], axis=0)
        return _to_pack_rows(n, w[n][l]).astype(BF16)

    packs = [jnp.concatenate([pack_rows(n, l) for n, _ in WEIGHT_PACK], axis=0) for l in range(depth)]
    weights = _Weights(packs)
    params = [_Params(weights, l, {n: w[n][l] for n, _, axis in SMALL if axis is None}) for l in range(depth)]

    inv_freq = ROPE_THETA ** (-jnp.arange(HD // 2, dtype=F32) / (HD // 2))
    cos, sin = _rope_tables(positions.reshape(s_len, 1), jnp.tile(inv_freq, 4).reshape(1, LANES), name="rope_tables")

    saved = []
    for l in range(depth):
        h, sv = _layer_fwd(h, mem0, params[l], cos, sin, f"_l{l}", functools.partial(weights.ride, l))
        saved.append(sv)
    dh, sq = _loss_grad(h, loss_target[0], tm=512, name="loss_grad")
    loss = lax.psum(0.5 * jnp.sum(sq) / D_MODEL, ("x", "y", "c"))

    c_arr, j_arr = jnp.reshape(ci, (1,)).astype(jnp.int32), jnp.reshape(chip, (1,)).astype(jnp.int32)

    def new_rs(names, g, tag):
        blocks = [_blocks_from_full(n, g[n]) for n in names]
        if names is REST_GROUP:
            blocks.append(jnp.stack([_small_to_rows([_chip_block(g[n], j, shape, axis) for n, shape, axis in SHARDED_SMALL])
                                     for j in range(4)]))
        return _ReduceScatter(jnp.concatenate(blocks, axis=1), c_arr, j_arr, tag)

    grads, later_rows, rest_rows, pending = [None] * depth, [None] * depth, [None] * depth, None
    for l in reversed(range(depth)):
        dh, grads[l], later_rows[l], rode = _layer_bwd(dh, mem0, params[l], saved[l], cos, sin, f"_l{l}", pending, new_rs)
        if pending is not None:
            rest_rows[l + 1] = rode
        pending = new_rs(REST_GROUP, grads[l], f"_l{l}_rest")
    grad_x = dh[None]

    out_g, out_d, out_m, out_v = {}, {}, {}, {}
    pack_off, off = {}, 0
    for n, rows in PACK_ROWS:
        pack_off[n] = (off, rows)
        off += rows

    def reduced(l, n):
        start, rows = pack_off[n]
        for names, block in ((REST_GROUP, rest_rows[l]), (XA_GROUP, later_rows[l][0]), (FFN_GROUP, later_rows[l][1])):
            if n in names:
                return block[start - pack_off[names[0]][0]:][:rows]

    def update(n, carry=None):
        shard_shape = w[n].shape
        g_n = jnp.stack([reduced(l, n) for l in range(depth)]).reshape(shard_shape)
        flat = lambda a: a.reshape(-1, shard_shape[-1])
        d_n, m_n, v_n, *rode = _adamw(flat(w[n]), flat(g_n), flat(m[n]), flat(v[n]), name=f"adamw_{n}", carry=carry)
        out_g[n], out_d[n], out_m[n], out_v[n] = g_n, d_n.reshape(shard_shape), m_n.reshape(shard_shape), v_n.reshape(shard_shape)
        return rode

    rest_rows[0] = pending.run()
    for n, _ in PACK_ROWS:
        update(n)

    g_small = _all_reduce_small(_flat_rows([grads[l][n] for l in range(depth) for n, _, axis in SMALL if axis is None]),
                                name="all_reduce_small_grads").reshape(-1)
    local_g, off = {}, 0
    for l in range(depth):
        for n, shape, axis in SMALL:
            if axis is None:
                size = int(np.prod(shape))
                local_g.setdefault(n, []).append(g_small[off:off + size].reshape(shape))
                off += size
        small_rows = rest_rows[l][sum(pack_off[n][1] for n in REST_GROUP):]
        for (n, _, _), block in zip(SHARDED_SMALL, _small_from_rows(small_rows)):
            local_g.setdefault(n, []).append(block)
    names = [n for n, _, _ in SMALL]
    g_loc = {n: jnp.stack(local_g[n]) for n in names}
    d_s, m_s, v_s = _adamw(_flat_rows([w[n] for n in names]), _flat_rows([g_loc[n] for n in names]),
                           _flat_rows([m[n] for n in names]), _flat_rows([v[n] for n in names]), name="adamw_small")
    off = 0
    for n in names:
        size = int(np.prod(w[n].shape))
        take = lambda a: a.reshape(-1)[off:off + size].reshape(w[n].shape)
        out_g[n], out_d[n], out_m[n], out_v[n] = g_loc[n], take(d_s), take(m_s), take(v_s)
        off += size

    outs = [loss, grad_x]
    for group in (out_g, out_d, out_m, out_v):
        outs += [group[n] for n in WEIGHT_ORDER]
    return tuple(outs)


def kernel(x, mem, positions, mix_norm_pre, w_in, cv_w, cv_b, cv_ln_g, cv_ln_b, cv_pw_w, cv_pw_b, w_out, mix_norm_post, x_norm_pre, mem_norm, x_wq, x_wk, x_wv, x_wo, x_norm_post, ffn_norm_pre, ffn_w_up, ffn_conv_w, ffn_conv_b, ffn_w_down, ffn_norm_post, loss_target, m_mix_norm_pre, m_w_in, m_cv_w, m_cv_b, m_cv_ln_g, m_cv_ln_b, m_cv_pw_w, m_cv_pw_b, m_w_out, m_mix_norm_post, m_x_norm_pre, m_mem_norm, m_x_wq, m_x_wk, m_x_wv, m_x_wo, m_x_norm_post, m_ffn_norm_pre, m_ffn_w_up, m_ffn_conv_w, m_ffn_conv_b, m_ffn_w_down, m_ffn_norm_post, v_mix_norm_pre, v_w_in, v_cv_w, v_cv_b, v_cv_ln_g, v_cv_ln_b, v_cv_pw_w, v_cv_pw_b, v_w_out, v_mix_norm_post, v_x_norm_pre, v_mem_norm, v_x_wq, v_x_wk, v_x_wv, v_x_wo, v_x_norm_post, v_ffn_norm_pre, v_ffn_w_up, v_ffn_conv_w, v_ffn_conv_b, v_ffn_w_down, v_ffn_norm_post):
    w = dict(zip(WEIGHT_ORDER, (mix_norm_pre, w_in, cv_w, cv_b, cv_ln_g, cv_ln_b, cv_pw_w, cv_pw_b, w_out, mix_norm_post, x_norm_pre, mem_norm, x_wq, x_wk, x_wv, x_wo, x_norm_post, ffn_norm_pre, ffn_w_up, ffn_conv_w, ffn_conv_b, ffn_w_down, ffn_norm_post)))
    m = dict(zip(WEIGHT_ORDER, (m_mix_norm_pre, m_w_in, m_cv_w, m_cv_b, m_cv_ln_g, m_cv_ln_b, m_cv_pw_w, m_cv_pw_b, m_w_out, m_mix_norm_post, m_x_norm_pre, m_mem_norm, m_x_wq, m_x_wk, m_x_wv, m_x_wo, m_x_norm_post, m_ffn_norm_pre, m_ffn_w_up, m_ffn_conv_w, m_ffn_conv_b, m_ffn_w_down, m_ffn_norm_post)))
    v = dict(zip(WEIGHT_ORDER, (v_mix_norm_pre, v_w_in, v_cv_w, v_cv_b, v_cv_ln_g, v_cv_ln_b, v_cv_pw_w, v_cv_pw_b, v_w_out, v_mix_norm_post, v_x_norm_pre, v_mem_norm, v_x_wq, v_x_wk, v_x_wv, v_x_wo, v_x_norm_post, v_ffn_norm_pre, v_ffn_w_up, v_ffn_conv_w, v_ffn_conv_b, v_ffn_w_down, v_ffn_norm_post)))
    return _step(x, mem, positions, loss_target, w, m, v)
```

```python
import functools

import jax
import jax.numpy as jnp
import numpy as np
from jax import lax
from jax.experimental import pallas as pl
from jax.experimental.pallas import tpu as pltpu

F32, BF16 = jnp.float32, jnp.bfloat16
MESH = pl.DeviceIdType.MESH
EPS = 1e-6
LANES = 128
BLK = 128
HD = 64
D_MODEL = 1024
D_FF = 2816
SB_W, CV_W, DL_W = 256, 256, 512
CV_K = 31
ROPE_THETA = 10000.0
DILATIONS = (1, 4, 16)
X_HEADS, X_HD = 4, 256
ADAM_LR, ADAM_B1, ADAM_B2, ADAM_EPS, ADAM_WD, ADAM_STEP = 0.001, 0.9, 0.999, 1e-08, 0.01, 10
NEG_INF = float("-inf")
MIB = 1 << 20

PACK_ROWS = (("w_in", 704), ("w_out", 256), ("x_wq", 256), ("x_wk", 256), ("x_wv", 256), ("x_wo", 256),
             ("ffn_w_up", 1408), ("ffn_w_down", 704))
PACK_RL = sum(r for _, r in PACK_ROWS)
SHARDED_SMALL = (("cv_w", (31, 64), 1), ("ffn_conv_w", (3, 1408), 1), ("cv_pw_w", (64, 256), 0))
SMALL_ROWS = 32
WEIGHT_PACK = (PACK_ROWS[0], ("small", 2 * SMALL_ROWS)) + PACK_ROWS[1:]


def _cp(vmem_mb=48):
    return pltpu.CompilerParams(vmem_limit_bytes=vmem_mb * MIB)


def _dot(a, b):
    return jnp.dot(a, b, preferred_element_type=F32)


def _dot_nt(a, b):
    return lax.dot_general(a, b, (((1,), (1,)), ((), ())), preferred_element_type=F32)


def _dot_tn(a, b):
    return lax.dot_general(a, b, (((0,), (0,)), ((), ())), preferred_element_type=F32)


def _dot_hilo(x, m):
    hi = x.astype(BF16)
    lo = (x - hi.astype(F32)).astype(BF16)
    return _dot(hi, m) + _dot(lo, m)


def _rowsum8(x):
    t, c = x.shape
    return x.reshape(t // 8, 8, c).sum(axis=0)


def _acc_out(ref, i, val):
    @pl.when(i == 0)
    def _():
        ref[...] = val

    @pl.when(i > 0)
    def _():
        ref[...] += val


def _tile(n, cap, mult=8):
    t = min(n, cap)
    while n % t or t % mult:
        t -= 1
    return t


def _rms_mm(x, g, w, *, tm, tn, out_dtype, name, carry=None):
    m, d = x.shape
    n_out = w.shape[1]

    def body(x_ref, g_ref, w_ref, n_ref, o_ref):
        @pl.when(pl.program_id(1) == 0)
        def _():
            xv = x_ref[...]
            r = lax.rsqrt(jnp.mean(xv * xv, axis=-1, keepdims=True) + EPS)
            n_ref[...] = (xv * r * g_ref[...]).astype(BF16)

        o_ref[...] = _dot(n_ref[...], w_ref[...]).astype(out_dtype)

    return _call(
        body, grid=(m // tm, n_out // tn), name=name, carry=carry,
        in_specs=[pl.BlockSpec((tm, d), lambda i, j: (i, 0)), pl.BlockSpec((1, d), lambda i, j: (0, 0)),
                  pl.BlockSpec((d, tn), lambda i, j: (0, j))],
        out_specs=[pl.BlockSpec((tm, d), lambda i, j: (i, 0)), pl.BlockSpec((tm, tn), lambda i, j: (i, j))],
        out_shape=[jax.ShapeDtypeStruct((m, d), BF16), jax.ShapeDtypeStruct((m, n_out), out_dtype)],
        args=(x, g, w))


def _mm_post(a, w, h, g, *, tm, name, carry=None):
    m, k = a.shape
    d = w.shape[1]

    def body(a_ref, w_ref, h_ref, g_ref, y_ref, ho_ref):
        y = _dot(a_ref[...], w_ref[...])
        y_ref[...] = y
        r = lax.rsqrt(jnp.mean(y * y, axis=-1, keepdims=True) + EPS)
        ho_ref[...] = h_ref[...] + y * r * g_ref[...]

    return _call(
        body, grid=(m // tm,), name=name, carry=carry,
        in_specs=[pl.BlockSpec((tm, k), lambda i: (i, 0)), pl.BlockSpec((k, d), lambda i: (0, 0)),
                  pl.BlockSpec((tm, d), lambda i: (i, 0)), pl.BlockSpec((1, d), lambda i: (0, 0))],
        out_specs=[pl.BlockSpec((tm, d), lambda i: (i, 0)), pl.BlockSpec((tm, d), lambda i: (i, 0))],
        out_shape=[jax.ShapeDtypeStruct((m, d), F32), jax.ShapeDtypeStruct((m, d), F32)],
        args=(a, w, h, g))


def _mm_nt(a, w, *, tm, tn, out_dtype, name):
    m, k = a.shape
    n_out = w.shape[0]

    def body(a_ref, w_ref, o_ref):
        o_ref[...] = _dot_nt(a_ref[...], w_ref[...]).astype(out_dtype)

    return pl.pallas_call(
        body, grid=(n_out // tn, m // tm), name=name,
        in_specs=[pl.BlockSpec((tm, k), lambda j, i: (i, 0)), pl.BlockSpec((tn, k), lambda j, i: (j, 0))],
        out_specs=pl.BlockSpec((tm, tn), lambda j, i: (i, j)),
        out_shape=jax.ShapeDtypeStruct((m, n_out), out_dtype),
        compiler_params=_cp())(a, w)


def _mm_tn(x, dy, *, tk, tn, tm, name):
    m, k = x.shape
    n_out = dy.shape[1]

    def body(x_ref, d_ref, o_ref):
        _acc_out(o_ref, pl.program_id(2), _dot_tn(x_ref[...], d_ref[...]))

    return pl.pallas_call(
        body, grid=(k // tk, n_out // tn, m // tm), name=name,
        in_specs=[pl.BlockSpec((tm, tk), lambda a, b, c: (c, a)), pl.BlockSpec((tm, tn), lambda a, b, c: (c, b))],
        out_specs=pl.BlockSpec((tk, tn), lambda a, b, c: (a, b)),
        out_shape=jax.ShapeDtypeStruct((k, n_out), F32),
        compiler_params=_cp())(x, dy)


def _rms_bwd(x, g, dout, res, *, out_dtype, tm, name, carry=None):
    m, d = x.shape
    has_res = res is not None

    def body(*refs):
        if has_res:
            x_ref, g_ref, d_ref, r_ref, dx_ref, dg_ref = refs
        else:
            x_ref, g_ref, d_ref, dx_ref, dg_ref = refs
        xv = x_ref[...]
        dv = d_ref[...].astype(F32)
        r = lax.rsqrt(jnp.mean(xv * xv, axis=-1, keepdims=True) + EPS)
        xh = xv * r
        dxh = dv * g_ref[...]
        dx = r * (dxh - xh * jnp.mean(dxh * xh, axis=-1, keepdims=True))
        if has_res:
            dx = dx + r_ref[...]
        dx_ref[...] = dx.astype(out_dtype)
        _acc_out(dg_ref, pl.program_id(0), _rowsum8(dv * xh))

    row = pl.BlockSpec((tm, d), lambda i: (i, 0))
    ins = [row, pl.BlockSpec((1, d), lambda i: (0, 0)), row] + ([row] if has_res else [])
    args = (x, g, dout) + ((res,) if has_res else ())
    return _call(
        body, grid=(m // tm,), name=name, carry=carry, in_specs=ins,
        out_specs=[row, pl.BlockSpec((8, d), lambda i: (0, 0))],
        out_shape=[jax.ShapeDtypeStruct((m, d), out_dtype), jax.ShapeDtypeStruct((8, d), F32)], args=args)


def _loss_grad(h, tgt, *, tm, name):
    m, d = h.shape

    def body(h_ref, t_ref, dh_ref, p_ref):
        e = h_ref[...] - t_ref[...]
        dh_ref[...] = e / d
        _acc_out(p_ref, pl.program_id(0), _rowsum8(e * e))

    row = pl.BlockSpec((tm, d), lambda i: (i, 0))
    return pl.pallas_call(
        body, grid=(m // tm,), name=name, in_specs=[row, row],
        out_specs=[row, pl.BlockSpec((8, d), lambda i: (0, 0))],
        out_shape=[jax.ShapeDtypeStruct((m, d), F32), jax.ShapeDtypeStruct((8, d), F32)],
        compiler_params=_cp())(h, tgt)


def _adamw(w, g, m, v, *, name, carry=None):
    r, c = w.shape
    tr = _tile(r, 256)

    def body(w_ref, g_ref, m_ref, v_ref, d_ref, mo_ref, vo_ref):
        gv = g_ref[...]
        m2 = ADAM_B1 * m_ref[...] + (1.0 - ADAM_B1) * gv
        v2 = ADAM_B2 * v_ref[...] + (1.0 - ADAM_B2) * jnp.square(gv)
        m_hat = m2 / (1.0 - ADAM_B1 ** ADAM_STEP)
        v_hat = v2 / (1.0 - ADAM_B2 ** ADAM_STEP)
        d_ref[...] = -ADAM_LR * (m_hat / (jnp.sqrt(v_hat) + ADAM_EPS) + ADAM_WD * w_ref[...])
        mo_ref[...] = m2
        vo_ref[...] = v2

    blk = pl.BlockSpec((tr, c), lambda i: (i, 0))
    return _call(body, grid=(r // tr,), name=name, carry=carry, in_specs=[blk] * 4, out_specs=[blk] * 3,
                 out_shape=[jax.ShapeDtypeStruct((r, c), F32)] * 3, args=(w, g, m, v))


def _head_masks():
    lane = lax.broadcasted_iota(jnp.int32, (BLK, LANES), 1)
    row = lax.broadcasted_iota(jnp.int32, (BLK, LANES), 0)
    return lane, row, lane < HD


def _sb_scores(q_a, k, before):
    z = _dot_nt(q_a, k)
    sp = jnp.log1p(jnp.exp(-jnp.abs(z)))
    ls_pos = jnp.minimum(z, 0.0) - sp
    lkeep = jnp.where(before, ls_pos - z, 0.0)
    return ls_pos, lkeep


SB_DEAD = -104.0


def _sb_alive(jj, i, carry):
    return jnp.logical_and(jj <= i, jnp.max(carry) > SB_DEAD)


SB_QB_FWD = 2
SB_QB = 2


def _sb_before(jj, qb=SB_QB):
    lane = lax.broadcasted_iota(jnp.int32, (qb * 2 * BLK, LANES), 1)
    row = lax.broadcasted_iota(jnp.int32, (qb * 2 * BLK, LANES), 0)
    below_diag = jj - (qb - 1) + row // (2 * BLK)
    return jnp.logical_or(below_diag > 0, jnp.logical_and(below_diag == 0, lane < row % BLK))


def _sb_stack(x, lane_h, qb=SB_QB):
    return jnp.concatenate([_stack_heads(x[b * BLK:(b + 1) * BLK], lane_h) for b in range(qb)], axis=0)


def _sb_unstack(x, lane_h, qb=SB_QB):
    return jnp.concatenate([jnp.where(lane_h, x[2 * b * BLK:(2 * b + 1) * BLK], x[(2 * b + 1) * BLK:(2 * b + 2) * BLK])
                            for b in range(qb)], axis=0)


SB_ROWS = SB_QB * 2 * BLK


def _sb_fwd(u, *, name, carry=None):
    s_len = u.shape[0]
    qb = SB_QB_FWD
    qrows, rows = qb * BLK, qb * 2 * BLK

    def body(q_ref, k_ref, v_ref, o_ref):
        top = pl.program_id(0) * qb + qb - 1
        lane, row, lane_h = _head_masks()
        suffix = (row > lane).astype(BF16)
        pairs = [slice(hp * LANES, (hp + 1) * LANES) for hp in range(2)]
        qs = [_sb_stack(q_ref[:, cs] * 0.125, lane_h, qb) for cs in pairs]

        def step(state):
            jj, ccs, accs = state[0], state[1:3], state[3:5]
            rows_k = pl.ds(pl.multiple_of((top - jj) * BLK, BLK), BLK)
            before = _sb_before(jj, qb)
            scores = [_sb_scores(q, k_ref[rows_k, cs].astype(BF16), before) for q, cs in zip(qs, pairs)]
            between = [_dot_hilo(lkeep, suffix) + cc for (_, lkeep), cc in zip(scores, ccs)]
            atts = [jnp.where(before, jnp.exp(ls_pos + b), 0.0).astype(BF16) for (ls_pos, _), b in zip(scores, between)]
            new_cc = [cc + jnp.sum(lkeep, axis=1, keepdims=True) for (_, lkeep), cc in zip(scores, ccs)]
            new_acc = [acc + _dot(a, v_ref[rows_k, cs].astype(BF16)) for a, acc, cs in zip(atts, accs, pairs)]
            return (jj + 1, *new_cc, *new_acc)

        zc, za = jnp.zeros((rows, 1), F32), jnp.zeros((rows, LANES), F32)
        res = lax.while_loop(lambda st: _sb_alive(st[0], top, jnp.maximum(st[1], st[2])), step,
                             (jnp.int32(0), zc, zc, za, za))
        for hp, cs in enumerate(pairs):
            o_ref[:, cs] = _sb_unstack(res[3 + hp], lane_h, qb).astype(BF16)

    wide = 2 * LANES
    return _call(
        body, grid=(s_len // qrows,), name=name, carry=carry,
        in_specs=[pl.BlockSpec((qrows, wide), lambda i: (i, 0)), pl.BlockSpec((s_len, wide), lambda i: (0, 1)),
                  pl.BlockSpec((s_len, wide), lambda i: (0, 2))],
        out_specs=[pl.BlockSpec((qrows, wide), lambda i: (i, 0))],
        out_shape=[jax.ShapeDtypeStruct((s_len, SB_W), BF16)], args=(u, u, u))


def _sb_bwd(u, dcat, *, name, carry=None):
    s_len = u.shape[0]
    nq = s_len // BLK
    qrows = SB_QB * BLK

    def body(q_ref, k_ref, v_ref, do_ref, dq_ref, dk_ref, dv_ref, g_scr, b_scr):
        step = pl.program_id(1)
        top = step * SB_QB + SB_QB - 1
        lane, row, lane_h = _head_masks()
        suffix = (row > lane).astype(BF16)
        prefix = (row < lane).astype(BF16)
        qf = q_ref[...]
        qs = _sb_stack(qf * 0.125, lane_h)
        qu = _sb_stack(qf, lane_h)
        dos = _sb_stack(do_ref[...], lane_h)

        @pl.when(step == 0)
        def _():
            dk_ref[...] = jnp.zeros_like(dk_ref)
            dv_ref[...] = jnp.zeros_like(dv_ref)

        def down(state):
            jj, cc = state
            j = top - jj
            off = pl.multiple_of(j * BLK, BLK)
            k = k_ref[pl.ds(off, BLK), :].astype(BF16)
            v = v_ref[pl.ds(off, BLK), :].astype(BF16)
            before = _sb_before(jj)
            ls_pos, lkeep = _sb_scores(qs, k, before)
            between = _dot_hilo(lkeep, suffix) + cc
            att = jnp.where(before, jnp.exp(ls_pos + between), 0.0)
            g_scr[j] = att * _dot_nt(dos, v)
            b_scr[j] = jnp.exp(ls_pos)
            dv_ref[pl.ds(off, BLK), :] += _dot_tn(att.astype(BF16), dos)
            return jj + 1, cc + jnp.sum(lkeep, axis=1, keepdims=True)

        zc = jnp.zeros((SB_ROWS, 1), F32)
        visited = lax.while_loop(lambda st: _sb_alive(st[0], top, st[1]), down, (jnp.int32(0), zc))[0]

        def up(j, carry):
            pc, dq = carry
            off = pl.multiple_of(j * BLK, BLK)
            k = k_ref[pl.ds(off, BLK), :].astype(BF16)
            g, beta = g_scr[j], b_scr[j]
            below = _dot_hilo(g, prefix) + pc
            dz = (jnp.where(_sb_before(top - j), g * (1.0 - beta) - beta * below, 0.0) * 0.125).astype(BF16)
            dk_ref[pl.ds(off, BLK), :] += _dot_tn(dz, qu)
            return pc + jnp.sum(g, axis=1, keepdims=True), dq + _dot(dz, k)

        dq = lax.fori_loop(top + 1 - visited, top + 1, up, (zc, jnp.zeros((SB_ROWS, LANES), F32)))[1]
        dq_ref[...] = _sb_unstack(dq, lane_h)

    col = lambda c0: pl.BlockSpec((s_len, LANES), lambda hp, i: (0, c0 + hp))
    blk = pl.BlockSpec((qrows, LANES), lambda hp, i: (i, hp))
    acc = pl.BlockSpec((s_len, LANES), lambda hp, i: (0, hp))
    return _call(
        body, grid=(2, s_len // qrows), name=name, carry=carry, in_specs=[blk, col(2), col(4), blk],
        out_specs=[blk, acc, acc], out_shape=[jax.ShapeDtypeStruct((s_len, SB_W), F32)] * 3,
        scratch_shapes=[pltpu.VMEM((nq, SB_ROWS, LANES), F32), pltpu.VMEM((nq, SB_ROWS, LANES), F32)],
        vmem_mb=56, args=(u, u, u, dcat))


CV_T = 512
CV_H = 32
CV_STRIP = 64


def _cv_specs(s_len):
    cur = lambda c: pl.BlockSpec((CV_T, CV_W), lambda i: (i, c))
    prev = lambda c: pl.BlockSpec((CV_H, CV_W), lambda i: (jnp.maximum(i * (CV_T // CV_H) - 1, 0), c))
    nxt = lambda c: pl.BlockSpec((CV_H, CV_W),
                                 lambda i: (jnp.minimum((i + 1) * (CV_T // CV_H), s_len // CV_H - 1), c))
    full = lambda r: pl.BlockSpec((r, CV_W), lambda i: (0, 0))
    return cur, prev, nxt, full


def _glu_into(gp_ref, val_ref, gate_ref, valp_ref, gatep_ref, i):
    gp_ref[0:CV_H, :] = jnp.where(i > 0, valp_ref[...] * jax.nn.sigmoid(gatep_ref[...]), 0.0)
    gp_ref[CV_H:, :] = val_ref[...] * jax.nn.sigmoid(gate_ref[...])


def _cv_fwd(u, cv_w, cv_b, ln_g, ln_b, pw_w, pw_b, *, name):
    s_len = u.shape[0]
    cur, prev, _, full = _cv_specs(s_len)

    def body(val_ref, gate_ref, valp_ref, gatep_ref, w_ref, b_ref, g_ref, be_ref, pw_ref, pb_ref,
             o_ref, c_ref, gp_ref):
        _glu_into(gp_ref, val_ref, gate_ref, valp_ref, gatep_ref, pl.program_id(0))
        for r0, rows in _strips(CV_T, CV_STRIP):
            acc = jnp.zeros((rows, CV_W), F32) + b_ref[...]
            for k in range(CV_K):
                acc = acc + w_ref[k:k + 1, :] * gp_ref[pl.ds(CV_H - CV_K + 1 + k + r0, rows), :]
            c_ref[r0:r0 + rows, :] = acc
        acc = c_ref[...]
        mu = jnp.mean(acc, axis=-1, keepdims=True)
        xc = acc - mu
        xh = xc * lax.rsqrt(jnp.mean(xc * xc, axis=-1, keepdims=True) + EPS)
        a = xh * g_ref[...] + be_ref[...]
        s = a * jax.nn.sigmoid(a)
        o_ref[...] = (_dot(s.astype(BF16), pw_ref[...]) + pb_ref[...]).astype(BF16)

    return pl.pallas_call(
        body, grid=(s_len // CV_T,), name=name,
        in_specs=[cur(3), cur(4), prev(3), prev(4), full(CV_K), full(1), full(1), full(1), full(CV_W), full(1)],
        out_specs=[cur(0), cur(0)],
        out_shape=[jax.ShapeDtypeStruct((s_len, CV_W), BF16), jax.ShapeDtypeStruct((s_len, CV_W), F32)],
        scratch_shapes=[pltpu.VMEM((CV_T + CV_H, CV_W), F32)], compiler_params=_cp())(
            u, u, u, u, cv_w, cv_b, ln_g, ln_b, pw_w, pw_b)


def _cv_bwd_local(c, dcat, ln_g, ln_b, pw_w, *, name):
    s_len = c.shape[0]
    cur, _, _, full = _cv_specs(s_len)

    def body(c_ref, db_ref, g_ref, be_ref, pw_ref, dc_ref, dpw_ref, vec_ref):
        i = pl.program_id(0)
        cv = c_ref[...]
        db = db_ref[...]
        mu = jnp.mean(cv, axis=-1, keepdims=True)
        xc = cv - mu
        rstd = lax.rsqrt(jnp.mean(xc * xc, axis=-1, keepdims=True) + EPS)
        xh = xc * rstd
        a = xh * g_ref[...] + be_ref[...]
        sg = jax.nn.sigmoid(a)
        s = a * sg
        dbb = db.astype(BF16)
        ds = _dot_nt(dbb, pw_ref[...])
        da = ds * (sg * (1.0 + a * (1.0 - sg)))
        dxh = da * g_ref[...]
        dc_ref[...] = rstd * (dxh - jnp.mean(dxh, axis=-1, keepdims=True)
                              - xh * jnp.mean(dxh * xh, axis=-1, keepdims=True))
        _acc_out(dpw_ref, i, _dot_tn(s.astype(BF16), dbb))
        _acc_out(vec_ref, i, jnp.concatenate([_rowsum8(db), _rowsum8(da * xh), _rowsum8(da)], axis=0))

    return pl.pallas_call(
        body, grid=(s_len // CV_T,), name=name,
        in_specs=[cur(0), cur(1), full(1), full(1), full(CV_W)],
        out_specs=[cur(0), full(CV_W), full(24)],
        out_shape=[jax.ShapeDtypeStruct((s_len, CV_W), F32), jax.ShapeDtypeStruct((CV_W, CV_W), F32),
                   jax.ShapeDtypeStruct((24, CV_W), F32)], compiler_params=_cp())(c, dcat, ln_g, ln_b, pw_w)


def _cv_bwd_conv(u, dc, cv_w, *, name):
    s_len = u.shape[0]
    cur, prev, nxt, full = _cv_specs(s_len)
    last = s_len // CV_T - 1

    def body(val_ref, gate_ref, valp_ref, gatep_ref, dc_ref, dcn_ref, w_ref, du_ref, dw_ref, dbias_ref,
             gp_ref, dcp_ref):
        i = pl.program_id(0)
        _glu_into(gp_ref, val_ref, gate_ref, valp_ref, gatep_ref, i)
        dcv = dc_ref[...]
        dcp_ref[0:CV_T, :] = dcv
        dcp_ref[CV_T:, :] = jnp.where(i < last, dcn_ref[...], 0.0)
        strips = _strips(CV_T, CV_STRIP)
        for r0, rows in strips:
            dg = jnp.zeros((rows, CV_W), F32)
            for k in range(CV_K):
                dg = dg + w_ref[k:k + 1, :] * dcp_ref[pl.ds(CV_K - 1 - k + r0, rows), :]
            sg = jax.nn.sigmoid(gate_ref[r0:r0 + rows, :])
            du_ref[r0:r0 + rows, 0:CV_W] = (dg * sg).astype(BF16)
            du_ref[r0:r0 + rows, CV_W:] = (dg * val_ref[r0:r0 + rows, :] * sg * (1.0 - sg)).astype(BF16)
        parts = []
        for k in range(CV_K):
            part = jnp.zeros((8, CV_W), F32)
            for r0, rows in strips:
                part = part + _rowsum8(dc_ref[r0:r0 + rows, :] * gp_ref[pl.ds(CV_H - CV_K + 1 + k + r0, rows), :])
            parts.append(part)
        _acc_out(dw_ref, i, jnp.concatenate(parts, axis=0))
        _acc_out(dbias_ref, i, _rowsum8(dcv))

    return pl.pallas_call(
        body, grid=(s_len // CV_T,), name=name,
        in_specs=[cur(3), cur(4), prev(3), prev(4), cur(0), nxt(0), full(CV_K)],
        out_specs=[pl.BlockSpec((CV_T, 2 * CV_W), lambda i: (i, 0)), full(CV_K * 8), full(8)],
        out_shape=[jax.ShapeDtypeStruct((s_len, 2 * CV_W), BF16), jax.ShapeDtypeStruct((CV_K * 8, CV_W), F32),
                   jax.ShapeDtypeStruct((8, CV_W), F32)],
        scratch_shapes=[pltpu.VMEM((CV_T + CV_H, CV_W), F32), pltpu.VMEM((CV_T + CV_H, CV_W), F32)],
        compiler_params=_cp())(u, u, u, u, dc, dc, cv_w)


def _rope_tables(pos_col, inv_freq_row, *, name):
    s_len = pos_col.shape[0]

    def body(p_ref, f_ref, cos_ref, sin_ref):
        ang = p_ref[...].astype(F32) * f_ref[...]
        lane = lax.broadcasted_iota(jnp.int32, (s_len, LANES), 1)
        sn = jnp.sin(ang)
        cos_ref[...] = jnp.cos(ang)
        sin_ref[...] = jnp.where(lane % HD < HD // 2, -sn, sn)

    return pl.pallas_call(body, name=name, out_shape=[jax.ShapeDtypeStruct((s_len, LANES), F32)] * 2,
                          compiler_params=_cp())(pos_col, inv_freq_row)


def _rot_half(x):
    lane = lax.broadcasted_iota(jnp.int32, x.shape, 1)
    return jnp.where(lane % HD < HD // 2, pltpu.roll(x, LANES - HD // 2, 1), pltpu.roll(x, HD // 2, 1))


def _permute_rows(dst_ref, src_ref, d, dtype):
    s_len = src_ref.shape[0]
    seg = s_len // d
    if d == 1:
        dst_ref[...] = src_ref[...].astype(dtype)
        return
    for r in range(d):
        dst_ref[r * seg:(r + 1) * seg, :] = src_ref[pl.ds(r, seg, stride=d), :].astype(dtype)


def _unpermute_rows(dst_ref, src_ref, d):
    s_len = src_ref.shape[0]
    seg = s_len // d
    if d == 1:
        dst_ref[...] = src_ref[...]
        return
    for r in range(d):
        dst_ref[pl.ds(r, seg, stride=d), :] = src_ref[r * seg:(r + 1) * seg, :]


def _rope_perm(u, cos, sin, *, name):
    s_len = u.shape[0]

    def body(x_ref, cos_ref, sin_ref, o_ref, scr):
        a = pl.program_id(0)
        x = x_ref[...]
        rot = a < 2
        scr[...] = x * jnp.where(rot, cos_ref[...], 1.0) + _rot_half(x) * jnp.where(rot, sin_ref[...], 0.0)
        for n, d in enumerate(DILATIONS):
            _permute_rows(o_ref.at[n], scr, d, BF16)

    tab = pl.BlockSpec((s_len, LANES), lambda a, cb: (0, 0))
    return pl.pallas_call(
        body, grid=(3, 4), name=name,
        in_specs=[pl.BlockSpec((s_len, LANES), lambda a, cb: (0, 10 + 4 * a + cb)), tab, tab],
        out_specs=pl.BlockSpec((None, 3, s_len, LANES), lambda a, cb: (a, 0, 0, cb)),
        out_shape=jax.ShapeDtypeStruct((3, 3, s_len, DL_W), BF16),
        scratch_shapes=[pltpu.VMEM((s_len, LANES), F32)], compiler_params=_cp())(u, cos, sin)


DL_UNROLL = 4


def _dl_band(rows):
    lane = lax.broadcasted_iota(jnp.int32, (rows, LANES), 1)
    row = lax.broadcasted_iota(jnp.int32, (rows, LANES), 0) % BLK
    return lane <= row, lane >= row


def _dl_first(s_len, n, i):
    nb = jnp.where(n == 0, s_len // BLK, jnp.where(n == 1, s_len // (BLK * DILATIONS[1]),
                                                   s_len // (BLK * DILATIONS[2])))
    return lax.rem(i, nb) == 0


def _stack_heads(x, lane_h):
    return jnp.concatenate([jnp.where(lane_h, x, 0.0), jnp.where(lane_h, 0.0, x)], axis=0).astype(BF16)


def _dl_rows(i):
    cur = pl.ds(pl.multiple_of(i * BLK, BLK), BLK)
    prev = pl.ds(pl.multiple_of(jnp.maximum(i - 1, 0) * BLK, BLK), BLK)
    return cur, prev


def _dl_in_specs(s_len):
    return [pl.BlockSpec((None, None, s_len, LANES), functools.partial(lambda a, n, hp: (a, n, 0, hp), a))
            for a in range(3)]


def _dl_fwd(qkv, *, name, carry=None):
    s_len = qkv.shape[2]

    def body(q_ref, k_ref, v_ref, o_ref, l_ref):
        n = pl.program_id(0)
        lane_h = _head_masks()[2]
        band_c, band_p = _dl_band(2 * BLK)
        ones = jnp.ones((BLK, LANES), BF16)

        @pl.loop(0, s_len // BLK, step=DL_UNROLL)
        def _(i0):
            blocks = [i0 + t for t in range(DL_UNROLL)]
            rows = [_dl_rows(i) for i in blocks]
            scores = []
            for cur, prev in rows:
                qs = _stack_heads(q_ref[cur, :] * 0.125, lane_h)
                scores.append((_dot_nt(qs, k_ref[cur, :]), _dot_nt(qs, k_ref[prev, :])))
            probs = []
            for i, (sc, sp) in zip(blocks, scores):
                sc = jnp.where(band_c, sc, NEG_INF)
                sp = jnp.where(jnp.logical_and(band_p, jnp.logical_not(_dl_first(s_len, n, i))), sp, NEG_INF)
                m = jnp.max(jnp.maximum(sc, sp), axis=1, keepdims=True)
                probs.append((jnp.exp(sc - m).astype(BF16), jnp.exp(sp - m).astype(BF16), m))
            for (cur, prev), (pc, pp, m) in zip(rows, probs):
                r = (_dot(pc, jnp.concatenate([v_ref[cur, :], ones], axis=1))
                     + _dot(pp, jnp.concatenate([v_ref[prev, :], ones], axis=1)))
                den = jnp.where(lane_h, r[:BLK, LANES:], r[BLK:, LANES:])
                o_ref[cur, :] = jnp.where(lane_h, r[:BLK, :LANES], r[BLK:, :LANES]) / den
                l_ref[cur, :] = jnp.where(lane_h, m[:BLK], m[BLK:]) + jnp.log(den)

    out = pl.BlockSpec((None, s_len, LANES), lambda n, hp: (n, 0, hp))
    return _call(
        body, grid=(3, 4), name=name, carry=carry, in_specs=_dl_in_specs(s_len), out_specs=[out, out],
        out_shape=[jax.ShapeDtypeStruct((3, s_len, DL_W), F32)] * 2, args=(qkv, qkv, qkv))


def _dl_mix(o_p, l_p, *, name, carry=None):
    s_len = o_p.shape[1]

    def body(o_ref, l_ref, ob_ref, of_ref, lt_ref, o_scr, l_scr):
        n = pl.program_id(1)
        for k, d in enumerate(DILATIONS):
            @pl.when(n == k)
            def _(k=k, d=d):
                _unpermute_rows(o_scr.at[k], o_ref, d)
                _unpermute_rows(l_scr.at[k], l_ref, d)

        @pl.when(n == 2)
        def _():
            l0, l1, l2 = l_scr[0], l_scr[1], l_scr[2]
            m = jnp.maximum(jnp.maximum(l0, l1), l2)
            e0, e1, e2 = jnp.exp(l0 - m), jnp.exp(l1 - m), jnp.exp(l2 - m)
            den = e0 + e1 + e2
            o = (e0 / den) * o_scr[0] + (e1 / den) * o_scr[1] + (e2 / den) * o_scr[2]
            of_ref[...] = o
            ob_ref[...] = o.astype(BF16)
            lt_ref[...] = m + jnp.log(den)

    inb = pl.BlockSpec((None, s_len, LANES), lambda cb, n: (n, 0, cb))
    outb = pl.BlockSpec((s_len, LANES), lambda cb, n: (0, cb))
    return _call(
        body, grid=(4, 3), name=name, carry=carry, in_specs=[inb, inb], out_specs=[outb, outb, outb],
        out_shape=[jax.ShapeDtypeStruct((s_len, DL_W), BF16), jax.ShapeDtypeStruct((s_len, DL_W), F32),
                   jax.ShapeDtypeStruct((s_len, DL_W), F32)],
        scratch_shapes=[pltpu.VMEM((3, s_len, LANES), F32), pltpu.VMEM((3, s_len, LANES), F32)], args=(o_p, l_p))


def _dl_bwd_prep(dcat, o, lse, *, name):
    s_len = o.shape[0]

    def body(do_ref, o_ref, l_ref, dop_ref, st_ref, d_scr):
        n = pl.program_id(1)

        @pl.when(n == 0)
        def _():
            r0 = lax.broadcasted_iota(jnp.int32, (LANES, LANES), 0) // HD
            r1 = lax.broadcasted_iota(jnp.int32, (LANES, LANES), 1) // HD
            d_scr[...] = _dot_hilo(do_ref[...] * o_ref[...], (r0 == r1).astype(BF16))

        for k, d in enumerate(DILATIONS):
            @pl.when(n == k)
            def _(d=d):
                _permute_rows(dop_ref, do_ref, d, BF16)
                _permute_rows(st_ref.at[0], d_scr, d, F32)
                _permute_rows(st_ref.at[1], l_ref, d, F32)

    nat = lambda c0: pl.BlockSpec((s_len, LANES), lambda cb, n: (0, c0 + cb))
    return pl.pallas_call(
        body, grid=(4, 3), name=name, in_specs=[nat(4), nat(0), nat(0)],
        out_specs=[pl.BlockSpec((None, s_len, LANES), lambda cb, n: (n, 0, cb)),
                   pl.BlockSpec((2, None, s_len, LANES), lambda cb, n: (0, n, 0, cb))],
        out_shape=[jax.ShapeDtypeStruct((3, s_len, DL_W), BF16), jax.ShapeDtypeStruct((2, 3, s_len, DL_W), F32)],
        scratch_shapes=[pltpu.VMEM((s_len, LANES), F32)], compiler_params=_cp())(dcat, o, lse)


def _dl_bwd(qkv, dop, stats, *, name, carry=None):
    s_len = qkv.shape[2]

    def body(q_ref, k_ref, v_ref, do_ref, st_ref, cur_ref, prev_ref):
        n = pl.program_id(0)
        lane_h = _head_masks()[2]
        band_c, band_p = _dl_band(2 * BLK)

        def per_head(x):
            xr = pltpu.roll(x, HD, 1)
            return jnp.concatenate([jnp.where(lane_h, x, xr), jnp.where(lane_h, xr, x)], axis=0)

        @pl.loop(0, s_len // BLK, step=DL_UNROLL)
        def _(i0):
            blocks = [i0 + t for t in range(DL_UNROLL)]
            rows = [_dl_rows(i) for i in blocks]
            stage1 = []
            for cur, prev in rows:
                qs = _stack_heads(q_ref[cur, :] * 0.125, lane_h)
                dos = _stack_heads(do_ref[cur, :], lane_h)
                kc, kp, vc, vp = k_ref[cur, :], k_ref[prev, :], v_ref[cur, :], v_ref[prev, :]
                stage1.append((qs, dos, _dot_nt(qs, kc), _dot_nt(qs, kp), _dot_nt(dos, vc), _dot_nt(dos, vp)))
            stage2 = []
            for i, (cur, prev), (qs, dos, sc, sp, dpc, dpp) in zip(blocks, rows, stage1):
                lse, delta = per_head(st_ref[1, cur, :]), per_head(st_ref[0, cur, :])
                pc = jnp.where(band_c, jnp.exp(sc - lse), 0.0)
                pp = jnp.where(jnp.logical_and(band_p, jnp.logical_not(_dl_first(s_len, n, i))), jnp.exp(sp - lse), 0.0)
                stage2.append((pc.astype(BF16), pp.astype(BF16), (pc * (dpc - delta)).astype(BF16),
                               (pp * (dpp - delta)).astype(BF16)))
            for (cur, prev), (qs, dos, *_), (pc, pp, dsc, dsp) in zip(rows, stage1, stage2):
                dq = _dot(dsc, k_ref[cur, :]) + _dot(dsp, k_ref[prev, :])
                cur_ref[0, cur, :] = jnp.where(lane_h, dq[:BLK], dq[BLK:]) * 0.125
                cur_ref[1, cur, :] = _dot_tn(dsc, qs)
                cur_ref[2, cur, :] = _dot_tn(pc, dos)
                prev_ref[0, cur, :] = _dot_tn(dsp, qs)
                prev_ref[1, cur, :] = _dot_tn(pp, dos)

    return _call(
        body, grid=(3, 4), name=name, carry=carry,
        in_specs=_dl_in_specs(s_len) + [pl.BlockSpec((None, s_len, LANES), lambda n, hp: (n, 0, hp)),
                                        pl.BlockSpec((2, None, s_len, LANES), lambda n, hp: (0, n, 0, hp))],
        out_specs=[pl.BlockSpec((3, None, s_len, LANES), lambda n, hp: (0, n, 0, hp)),
                   pl.BlockSpec((2, None, s_len, LANES), lambda n, hp: (0, n, 0, hp))],
        out_shape=[jax.ShapeDtypeStruct((3, 3, s_len, DL_W), F32), jax.ShapeDtypeStruct((2, 3, s_len, DL_W), F32)],
        vmem_mb=56, args=(qkv, qkv, qkv, dop, stats))


def _dl_bwd_finish(cur, prev, cos, sin, *, name):
    s_len = cur.shape[2]

    def body(c_ref, p_ref, cos_ref, sin_ref, o_ref, p_scr, u_scr, acc):
        a, n = pl.program_id(0), pl.program_id(2)
        has_prev = jnp.where(a > 0, 1.0, 0.0)
        p_scr[...] = c_ref[...]
        p_scr[0:s_len - BLK, :] += has_prev * p_ref[BLK:, :]
        for k, d in enumerate(DILATIONS):
            @pl.when(n == k)
            def _(k=k, d=d):
                if k == 0:
                    acc[...] = p_scr[...]
                else:
                    _unpermute_rows(u_scr, p_scr, d)
                    acc[...] += u_scr[...]

        @pl.when(n == 2)
        def _():
            dy = acc[...]
            rot = a < 2
            o_ref[...] = (dy * jnp.where(rot, cos_ref[...], 1.0)
                          + _rot_half(dy * jnp.where(rot, sin_ref[...], 0.0))).astype(BF16)

    tab = pl.BlockSpec((s_len, LANES), lambda a, cb, n: (0, 0))
    return pl.pallas_call(
        body, grid=(3, 4, 3), name=name,
        in_specs=[pl.BlockSpec((None, None, s_len, LANES), lambda a, cb, n: (a, n, 0, cb)),
                  pl.BlockSpec((None, None, s_len, LANES), lambda a, cb, n: (jnp.maximum(a - 1, 0), n, 0, cb)),
                  tab, tab],
        out_specs=pl.BlockSpec((s_len, LANES), lambda a, cb, n: (0, 4 * a + cb)),
        out_shape=jax.ShapeDtypeStruct((s_len, 3 * DL_W), BF16),
        scratch_shapes=[pltpu.VMEM((s_len, LANES), F32)] * 3, compiler_params=_cp())(cur, prev, cos, sin)


XA_T = 256


def _xa_probs(q, k):
    s = _dot_nt(q, k) * (X_HD ** -0.5)
    e = jnp.exp(s - jnp.max(s, axis=1, keepdims=True))
    return e / jnp.sum(e, axis=1, keepdims=True)


def _xa_fwd(q, k, v, *, name):
    s_len, d = q.shape
    nm = k.shape[0]

    def body(q_ref, k_ref, v_ref, o_ref):
        for h in range(X_HEADS):
            cs = slice(h * X_HD, (h + 1) * X_HD)
            p = _xa_probs(q_ref[:, cs], k_ref[:, cs])
            o_ref[:, cs] = _dot(p.astype(BF16), v_ref[:, cs]).astype(BF16)

    row = pl.BlockSpec((XA_T, d), lambda i: (i, 0))
    full = pl.BlockSpec((nm, d), lambda i: (0, 0))
    return pl.pallas_call(body, grid=(s_len // XA_T,), name=name, in_specs=[row, full, full], out_specs=row,
                          out_shape=jax.ShapeDtypeStruct((s_len, d), BF16), compiler_params=_cp())(q, k, v)


def _xa_bwd(q, k, v, do, *, name, carry=None):
    s_len, d = q.shape
    nm = k.shape[0]

    def body(q_ref, k_ref, v_ref, do_ref, dq_ref, dk_ref, dv_ref):
        i = pl.program_id(0)
        for h in range(X_HEADS):
            cs = slice(h * X_HD, (h + 1) * X_HD)
            qh, kh, vh, doh = q_ref[:, cs], k_ref[:, cs], v_ref[:, cs], do_ref[:, cs]
            p = _xa_probs(qh, kh)
            dp = _dot_nt(doh, vh)
            ds = (p * (dp - jnp.sum(dp * p, axis=1, keepdims=True)) * (X_HD ** -0.5)).astype(BF16)
            dq_ref[:, cs] = _dot(ds, kh).astype(BF16)
            dkh, dvh = _dot_tn(ds, qh), _dot_tn(p.astype(BF16), doh)

            @pl.when(i == 0)
            def _(cs=cs, dkh=dkh, dvh=dvh):
                dk_ref[:, cs] = dkh
                dv_ref[:, cs] = dvh

            @pl.when(i > 0)
            def _(cs=cs, dkh=dkh, dvh=dvh):
                dk_ref[:, cs] += dkh
                dv_ref[:, cs] += dvh

    row = pl.BlockSpec((XA_T, d), lambda i: (i, 0))
    full = pl.BlockSpec((nm, d), lambda i: (0, 0))
    return _call(
        body, grid=(s_len // XA_T,), name=name, carry=carry, in_specs=[row, full, full, row],
        out_specs=[row, full, full],
        out_shape=[jax.ShapeDtypeStruct((s_len, d), BF16), jax.ShapeDtypeStruct((nm, d), F32),
                   jax.ShapeDtypeStruct((nm, d), F32)], args=(q, k, v, do))


FF_TM, FF_TN, FF_H = 512, 256, 8
GELU_K, GELU_C = 0.7978845608028654, 0.044715


FF_STRIP = 64


def _ff_conv(e_ref, w_ref, b_ref, rows, r0=0):
    return (w_ref[0:1, :] * e_ref[pl.ds(FF_H - 2 + r0, rows), :] + w_ref[1:2, :] * e_ref[pl.ds(FF_H - 1 + r0, rows), :]
            + w_ref[2:3, :] * e_ref[pl.ds(FF_H + r0, rows), :] + b_ref[...])


def _strips(total, size):
    return [(r0, min(size, total - r0)) for r0 in range(0, total, size)]


def _ff_gate_fwd(up, conv_w, conv_b, *, name, carry=None):
    s_len = up.shape[0]
    nj = D_FF // FF_TN

    def body(g_ref, v_ref, gp_ref, vp_ref, wg_ref, wv_ref, bg_ref, bv_ref, o_ref, eg, ev):
        i = pl.program_id(0)
        for e, cur, prev in ((eg, g_ref, gp_ref), (ev, v_ref, vp_ref)):
            e[0:FF_H, :] = jnp.where(i > 0, prev[...], 0.0)
            e[FF_H:, :] = cur[...]
        for r0, rows in _strips(FF_TM, FF_STRIP):
            gate = _ff_conv(eg, wg_ref, bg_ref, rows, r0)
            val = _ff_conv(ev, wv_ref, bv_ref, rows, r0)
            t = jnp.tanh(GELU_K * (gate + GELU_C * gate * gate * gate))
            o_ref[r0:r0 + rows, :] = (0.5 * gate * (1.0 + t) * val).astype(BF16)

    cur = lambda c0: pl.BlockSpec((FF_TM, FF_TN), lambda i, j: (i, c0 + j))
    prev = lambda c0: pl.BlockSpec((FF_H, FF_TN), lambda i, j: (jnp.maximum(i * (FF_TM // FF_H) - 1, 0), c0 + j))
    par = lambda r, c0: pl.BlockSpec((r, FF_TN), lambda i, j: (0, c0 + j))
    return _call(
        body, grid=(s_len // FF_TM, nj), name=name, carry=carry,
        in_specs=[cur(0), cur(nj), prev(0), prev(nj), par(3, 0), par(3, nj), par(1, 0), par(1, nj)],
        out_specs=[cur(0)], out_shape=[jax.ShapeDtypeStruct((s_len, D_FF), BF16)],
        scratch_shapes=[pltpu.VMEM((FF_TM + FF_H, FF_TN), F32)] * 2,
        args=(up, up, up, up, conv_w, conv_w, conv_b, conv_b))


def _ff_gate_bwd(up, dact, conv_w, conv_b, *, name, carry=None):
    s_len = up.shape[0]
    nj = D_FF // FF_TN
    last = s_len // FF_TM - 1
    ext = FF_TM + FF_H

    def body(g_ref, v_ref, gp_ref, vp_ref, gn_ref, vn_ref, da_ref, dan_ref, wg_ref, wv_ref, bg_ref, bv_ref,
             dg_ref, dv_ref, dw_ref, db_ref, eg, ev, sg, sv):
        i = pl.program_id(1)
        for e, cur, prev, nxt in ((eg, g_ref, gp_ref, gn_ref), (ev, v_ref, vp_ref, vn_ref)):
            e[0:FF_H, :] = jnp.where(i > 0, prev[...], 0.0)
            e[FF_H:FF_H + FF_TM, :] = cur[...]
            e[FF_H + FF_TM:, :] = nxt[...]
        for r0, rows in _strips(ext, FF_STRIP):
            gate = _ff_conv(eg, wg_ref, bg_ref, rows, r0)
            val = _ff_conv(ev, wv_ref, bv_ref, rows, r0)
            dact = da_ref[r0:r0 + rows, :] if r0 < FF_TM else jnp.where(i < last, dan_ref[...], 0.0)
            t = jnp.tanh(GELU_K * (gate + GELU_C * gate * gate * gate))
            half = 0.5 * (1.0 + t)
            dgelu = half + 0.5 * gate * (1.0 - t * t) * GELU_K * (1.0 + 3.0 * GELU_C * gate * gate)
            sg[r0:r0 + rows, :] = dact * val * dgelu
            sv[r0:r0 + rows, :] = dact * (gate * half)
        for part, (s, e, w_ref, out) in enumerate(((sg, eg, wg_ref, dg_ref), (sv, ev, wv_ref, dv_ref))):
            taps, bias = [jnp.zeros((8, FF_TN), F32)] * 3, jnp.zeros((8, FF_TN), F32)
            for r0, rows in _strips(FF_TM, FF_STRIP):
                d0 = s[pl.ds(r0, rows), :]
                out[r0:r0 + rows, :] = (w_ref[2:3, :] * d0 + w_ref[1:2, :] * s[pl.ds(r0 + 1, rows), :]
                                        + w_ref[0:1, :] * s[pl.ds(r0 + 2, rows), :]).astype(BF16)
                taps = [taps[k] + _rowsum8(d0 * e[pl.ds(FF_H - 2 + k + r0, rows), :]) for k in range(3)]
                bias = bias + _rowsum8(d0)
            _acc_out(dw_ref.at[part], i, jnp.concatenate(taps, axis=0))
            _acc_out(db_ref.at[part], i, bias)

    cur = lambda c0: pl.BlockSpec((FF_TM, FF_TN), lambda j, i: (i, c0 + j))
    prev = lambda c0: pl.BlockSpec((FF_H, FF_TN), lambda j, i: (jnp.maximum(i * (FF_TM // FF_H) - 1, 0), c0 + j))
    nxt = lambda c0: pl.BlockSpec(
        (FF_H, FF_TN), lambda j, i: (jnp.minimum((i + 1) * (FF_TM // FF_H), s_len // FF_H - 1), c0 + j))
    par = lambda r, c0: pl.BlockSpec((r, FF_TN), lambda j, i: (0, c0 + j))
    return _call(
        body, grid=(nj, s_len // FF_TM), name=name, carry=carry,
        in_specs=[cur(0), cur(nj), prev(0), prev(nj), nxt(0), nxt(nj), cur(0), nxt(0),
                  par(3, 0), par(3, nj), par(1, 0), par(1, nj)],
        out_specs=[cur(0), cur(0), pl.BlockSpec((2, 24, FF_TN), lambda j, i: (0, 0, j)),
                   pl.BlockSpec((2, 8, FF_TN), lambda j, i: (0, 0, j))],
        out_shape=[jax.ShapeDtypeStruct((s_len, D_FF), BF16), jax.ShapeDtypeStruct((s_len, D_FF), BF16),
                   jax.ShapeDtypeStruct((2, 24, D_FF), F32), jax.ShapeDtypeStruct((2, 8, D_FF), F32)],
        scratch_shapes=[pltpu.VMEM((FF_TM + 2 * FF_H, FF_TN), F32)] * 2 + [pltpu.VMEM((ext, FF_TN), F32)] * 2,
        args=(up, up, up, up, up, up, dact, dact, conv_w, conv_w, conv_b, conv_b))


def _place():
    x, y, c = lax.axis_index("x"), lax.axis_index("y"), lax.axis_index("c")
    return x, y, c, [(1 - x, y), (x, 1 - y), (1 - x, 1 - y)]


def _remote(src, dst, send_sem, recv_sem, dev):
    return pltpu.make_async_remote_copy(src_ref=src, dst_ref=dst, send_sem=send_sem, recv_sem=recv_sem,
                                        device_id=dev, device_id_type=MESH)


_ANY = pl.BlockSpec(memory_space=pl.ANY)


N_SEMS = 8
SEM_BASE_2 = 4


class _Exchange:
    def __init__(self, operands, out_shapes, start, wait, aliases=None):
        self.operands, self.out_shapes, self.start, self.wait = list(operands), list(out_shapes), start, wait
        self.aliases = aliases or {}


def _sem_scratch():
    return [pltpu.SemaphoreType.DMA((N_SEMS,)), pltpu.SemaphoreType.DMA((N_SEMS,)), pltpu.SemaphoreType.DMA]


def _run_exchange(ex, *, name):
    k, n = len(ex.operands), len(ex.out_shapes)

    def body(*refs):
        ins, outs, sems = refs[:k], refs[k:k + n], refs[k + n:]
        ex.start(ins, outs, *sems)
        ex.wait(ins, outs, *sems)

    return pl.pallas_call(body, name=name, in_specs=[_ANY] * k, out_specs=[_ANY] * n, out_shape=ex.out_shapes,
                          scratch_shapes=_sem_scratch(), input_output_aliases=ex.aliases,
                          compiler_params=_cp(16))(*ex.operands)


def _call(body, *, grid, in_specs, out_specs, out_shape, args, name, scratch_shapes=(), vmem_mb=48, carry=None):
    scratch_shapes = list(scratch_shapes)
    if carry is None:
        return pl.pallas_call(body, grid=grid, name=name, in_specs=in_specs, out_specs=out_specs, out_shape=out_shape,
                              scratch_shapes=scratch_shapes, compiler_params=_cp(vmem_mb))(*args)
    n_in, n_out, n_scr = len(in_specs), len(out_shape), len(scratch_shapes)
    k_in, k_out = len(carry.operands), len(carry.out_shapes)

    def wrapped(*refs):
        ins, refs = refs[:n_in], refs[n_in:]
        cin, refs = refs[:k_in], refs[k_in:]
        outs, refs = refs[:n_out], refs[n_out:]
        cout, refs = refs[:k_out], refs[k_out:]
        scratch, sems = refs[:n_scr], refs[n_scr:]
        ids = [pl.program_id(a) for a in range(len(grid))]
        first = functools.reduce(jnp.logical_and, [i == 0 for i in ids])
        last = functools.reduce(jnp.logical_and, [i == g - 1 for i, g in zip(ids, grid)])

        @pl.when(first)
        def _():
            carry.start(cin, cout, *sems)

        body(*ins, *outs, *scratch)

        @pl.when(last)
        def _():
            carry.wait(cin, cout, *sems)

    aliases = {n_in + i: n_out + o for i, o in carry.aliases.items()}
    return pl.pallas_call(
        wrapped, grid=grid, name=name, in_specs=list(in_specs) + [_ANY] * k_in,
        out_specs=list(out_specs) + [_ANY] * k_out, out_shape=list(out_shape) + carry.out_shapes,
        scratch_shapes=scratch_shapes + _sem_scratch(), input_output_aliases=aliases,
        compiler_params=_cp(vmem_mb))(*args, *carry.operands)


def _half_rows(ref_rows, c):
    half = ref_rows // 2
    return pl.ds(c * half, half)


def _ex_join(a, b):
    ka, na = len(a.operands), len(a.out_shapes)

    def start(ins, outs, *sems):
        a.start(ins[:ka], outs[:na], *sems)
        b.start(ins[ka:], outs[na:], *sems)

    def wait(ins, outs, *sems):
        a.wait(ins[:ka], outs[:na], *sems)
        b.wait(ins[ka:], outs[na:], *sems)

    aliases = dict(a.aliases)
    aliases.update({ka + i: na + o for i, o in b.aliases.items()})
    return _Exchange(a.operands + b.operands, a.out_shapes + b.out_shapes, start, wait, aliases)


def _ex_gather(pack, r0, rl, base=0):
    def copies(ins, outs, send, recv):
        x, y, c, chips = _place()
        rows = _half_rows(rl, c)
        src = ins[0].at[pl.ds(r0 + c * (rl // 2), rl // 2)]
        sends = [_remote(src, outs[0].at[2 * x + y, rows], send.at[base + k], recv.at[base + k], (px, py, c))
                 for k, (px, py) in enumerate(chips)]
        lands = [_remote(src, outs[0].at[2 * px + py, rows], send.at[base + k], recv.at[base + k], (px, py, c))
                 for k, (px, py) in enumerate(chips)]
        return sends, lands

    def mine(ins, outs, local):
        x, y, _, _ = _place()
        return pltpu.make_async_copy(ins[0].at[pl.ds(r0, rl)], outs[0].at[2 * x + y], local)

    def start(ins, outs, send, recv, local):
        mine(ins, outs, local).start()
        for cp in copies(ins, outs, send, recv)[0]:
            cp.start()

    def wait(ins, outs, send, recv, local):
        sends, lands = copies(ins, outs, send, recv)
        for cp in lands:
            cp.wait_recv()
        for cp in sends:
            cp.wait_send()
        mine(ins, outs, local).wait()

    return _Exchange([pack], [jax.ShapeDtypeStruct((4, rl, pack.shape[1]), pack.dtype)], start, wait)


def _ex_gather_forward(g, base=0):
    rl = g.shape[1]

    def copies(outs, send, recv):
        x, y, c, chips = _place()
        slabs = [(outs[0].at[2 * px + py, _half_rows(rl, c)], outs[0].at[2 * px + py, _half_rows(rl, 1 - c)])
                 for px, py in chips]
        sends = [_remote(a, a, send.at[base + k], recv.at[base + k], (x, y, 1 - c)) for k, (a, _) in enumerate(slabs)]
        lands = [_remote(b, b, send.at[base + k], recv.at[base + k], (x, y, 1 - c)) for k, (_, b) in enumerate(slabs)]
        return sends, lands

    def start(ins, outs, send, recv, local):
        for cp in copies(outs, send, recv)[0]:
            cp.start()

    def wait(ins, outs, send, recv, local):
        sends, lands = copies(outs, send, recv)
        for cp in lands:
            cp.wait_recv()
        for cp in sends:
            cp.wait_send()

    return _Exchange([g], [jax.ShapeDtypeStruct(g.shape, g.dtype)], start, wait, aliases={0: 0})


def _ex_swap_halves(gw, base=0):
    nb, rl, d = gw.shape

    def copies(ins, outs, send, recv):
        x, y, c, _ = _place()
        return [_remote(ins[0].at[j, _half_rows(rl, 1 - c)], outs[0].at[j], send.at[base + j], recv.at[base + j],
                        (x, y, 1 - c)) for j in range(nb)]

    def start(ins, outs, send, recv, local):
        for cp in copies(ins, outs, send, recv):
            cp.start()

    def wait(ins, outs, send, recv, local):
        for cp in copies(ins, outs, send, recv):
            cp.wait()

    return _Exchange([gw], [jax.ShapeDtypeStruct((nb, rl // 2, d), gw.dtype)], start, wait)


def _chip_sum(gw, got, c_arr, *, name):
    nchip, half, d = got.shape
    tr = _tile(half, 512)

    def body(c_ref, a_ref, b_ref, o32_ref, o16_ref):
        s = a_ref[...] + b_ref[...]
        o32_ref[...] = s
        o16_ref[...] = s.astype(BF16)

    blk = pl.BlockSpec((None, tr, d), lambda j, i, c_ref: (j, i, 0))
    return pl.pallas_call(
        body, name=name,
        grid_spec=pltpu.PrefetchScalarGridSpec(
            num_scalar_prefetch=1, grid=(nchip, half // tr),
            in_specs=[pl.BlockSpec((None, tr, d), lambda j, i, c_ref: (j, c_ref[0] * (half // tr) + i, 0)), blk],
            out_specs=[blk, blk]),
        out_shape=[jax.ShapeDtypeStruct((nchip, half, d), F32), jax.ShapeDtypeStruct((nchip, half, d), BF16)],
        compiler_params=_cp())(c_arr, gw, got)


def _ex_scatter(s16, base=0):
    def copies(ins, outs, send, recv):
        x, y, c, chips = _place()
        return [_remote(ins[0].at[2 * px + py], outs[0].at[k], send.at[base + k], recv.at[base + k], (px, py, c))
                for k, (px, py) in enumerate(chips)]

    def start(ins, outs, send, recv, local):
        for cp in copies(ins, outs, send, recv):
            cp.start()

    def wait(ins, outs, send, recv, local):
        for cp in copies(ins, outs, send, recv):
            cp.wait()

    return _Exchange([s16], [jax.ShapeDtypeStruct((3,) + s16.shape[1:], s16.dtype)], start, wait)


def _mesh_sum(s32, got, j_arr, *, name):
    _, rl, d = s32.shape
    tr = _tile(rl, 512)

    def body(j_ref, a_ref, b_ref, o_ref):
        o_ref[...] = ((a_ref[...] + b_ref[0].astype(F32)) + b_ref[1].astype(F32)) + b_ref[2].astype(F32)

    return pl.pallas_call(
        body, name=name,
        grid_spec=pltpu.PrefetchScalarGridSpec(
            num_scalar_prefetch=1, grid=(rl // tr,),
            in_specs=[pl.BlockSpec((None, tr, d), lambda i, j_ref: (j_ref[0], i, 0)),
                      pl.BlockSpec((3, tr, d), lambda i, j_ref: (0, i, 0))],
            out_specs=pl.BlockSpec((tr, d), lambda i, j_ref: (i, 0))),
        out_shape=jax.ShapeDtypeStruct((rl, d), F32), compiler_params=_cp())(j_arr, s32, got)


def _ex_share_halves(ghalf):
    half, d = ghalf.shape

    def copies(ins, outs, send, recv, local):
        x, y, c, _ = _place()
        there = outs[0].at[_half_rows(2 * half, c)]
        back = outs[0].at[_half_rows(2 * half, 1 - c)]
        return (_remote(ins[0], there, send.at[0], recv.at[0], (x, y, 1 - c)),
                _remote(ins[0], back, send.at[0], recv.at[0], (x, y, 1 - c)), pltpu.make_async_copy(ins[0], there, local))

    def start(ins, outs, send, recv, local):
        out, _, mine = copies(ins, outs, send, recv, local)
        mine.start()
        out.start()

    def wait(ins, outs, send, recv, local):
        out, back, mine = copies(ins, outs, send, recv, local)
        back.wait_recv()
        out.wait_send()
        mine.wait()

    return _Exchange([ghalf], [jax.ShapeDtypeStruct((2 * half, d), ghalf.dtype)], start, wait)


class _ReduceScatter:
    def __init__(self, gw, c_arr, j_arr, tag):
        self.gw, self.c_arr, self.j_arr, self.tag = gw, c_arr, j_arr, tag

    def swap(self, base=0):
        return _ex_swap_halves(self.gw, base)

    def after_swap(self, got, base=0):
        self.s32, s16 = _chip_sum(self.gw, got, self.c_arr, name=f"rs_chip_sum{self.tag}")
        return _ex_scatter(s16, base)

    def after_scatter(self, got16):
        ghalf = _mesh_sum(self.s32, got16, self.j_arr, name=f"rs_mesh_sum{self.tag}")
        return _run_exchange(_ex_share_halves(ghalf), name=f"rs_share{self.tag}")[0]

    def run(self):
        got, = _run_exchange(self.swap(), name=f"rs_swap{self.tag}")
        got16, = _run_exchange(self.after_swap(got), name=f"rs_scatter{self.tag}")
        return self.after_scatter(got16)


def _all_reduce_small(vec, *, name):
    rows, d = vec.shape

    def body(x_ref, o_ref, gat, send_sems, recv_sems, local_sem):
        x, y, c, chips = _place()
        me, sibling = (x, y, c), (x, y, 1 - c)

        def slot(px, py, pc):
            return gat.at[4 * px + 2 * py + pc]

        def copy(k, block, to, src=None):
            return _remote(slot(*block) if src is None else src, slot(*block), send_sems.at[k], recv_sems.at[k], to)

        mine = pltpu.make_async_copy(x_ref, slot(*me), local_sem)
        mine.start()
        first = [copy(0, me, sibling, src=x_ref)]
        first += [copy(1 + j, me, (*chip, c), src=x_ref) for j, chip in enumerate(chips)]
        for cp in first:
            cp.start()
        passed = [copy(4 + j, (*chip, c), sibling) for j, chip in enumerate(chips)]
        for j, chip in enumerate(chips):
            copy(1 + j, (*chip, c), me).wait_recv()
            passed[j].start()
        copy(0, sibling, me).wait_recv()
        for j, chip in enumerate(chips):
            copy(4 + j, (*chip, 1 - c), me).wait_recv()
        for cp in first + passed:
            cp.wait_send()
        mine.wait()
        acc = gat[0]
        for dev in range(1, 8):
            acc = acc + gat[dev]
        o_ref[...] = acc

    vm = pl.BlockSpec(memory_space=pltpu.VMEM)
    return pl.pallas_call(
        body, name=name, in_specs=[vm], out_specs=vm, out_shape=jax.ShapeDtypeStruct((rows, d), F32),
        scratch_shapes=[pltpu.VMEM((8, rows, d), F32), pltpu.SemaphoreType.DMA((7,)), pltpu.SemaphoreType.DMA((7,)),
                        pltpu.SemaphoreType.DMA],
        compiler_params=_cp(32))(vec)


COL_SHARDED = ("w_in", "ffn_w_up")


def _to_pack_rows(name, shard):
    return shard.reshape(-1, D_MODEL)


def _full_from_blocks(name, blocks):
    rows = blocks.shape[1]
    if name in COL_SHARDED:
        return blocks.reshape(4, D_MODEL, rows).transpose(1, 0, 2).reshape(D_MODEL, 4 * rows)
    return blocks.reshape(4 * rows, D_MODEL)


def _blocks_from_full(name, full):
    if name in COL_SHARDED:
        cols = full.shape[1] // 4
        return full.reshape(D_MODEL, 4, cols).transpose(1, 0, 2).reshape(4, cols, D_MODEL)
    return full.reshape(4, full.shape[0] // 4, D_MODEL)


def _row(v):
    return v.reshape(1, -1)


SMALL = (("mix_norm_pre", (1024,), None), ("cv_w", (31, 256), 1), ("cv_b", (256,), None), ("cv_ln_g", (256,), None),
         ("cv_ln_b", (256,), None), ("cv_pw_w", (256, 256), 0), ("cv_pw_b", (256,), None),
         ("mix_norm_post", (1024,), None), ("x_norm_pre", (1024,), None), ("mem_norm", (1024,), None),
         ("x_norm_post", (1024,), None), ("ffn_norm_pre", (1024,), None), ("ffn_conv_w", (3, 5632), 1),
         ("ffn_conv_b", (5632,), None), ("ffn_norm_post", (1024,), None))
BIG = tuple(n for n, _ in PACK_ROWS)
WEIGHT_ORDER = ("mix_norm_pre", "w_in", "cv_w", "cv_b", "cv_ln_g", "cv_ln_b", "cv_pw_w", "cv_pw_b", "w_out",
                "mix_norm_post", "x_norm_pre", "mem_norm", "x_wq", "x_wk", "x_wv", "x_wo", "x_norm_post",
                "ffn_norm_pre", "ffn_w_up", "ffn_conv_w", "ffn_conv_b", "ffn_w_down", "ffn_norm_post")


def _flat_rows(parts):
    v = jnp.concatenate([p.reshape(-1) for p in parts])
    rows = -(-v.shape[0] // (8 * D_MODEL)) * 8
    return jnp.pad(v, (0, rows * D_MODEL - v.shape[0])).reshape(rows, D_MODEL)


def _small_to_rows(blocks):
    v = jnp.concatenate([b.reshape(-1) for b in blocks])
    return jnp.pad(v, (0, SMALL_ROWS * D_MODEL - v.shape[0])).reshape(SMALL_ROWS, D_MODEL)


def _small_from_rows(rows):
    flat, out, off = rows.reshape(-1), [], 0
    for _, shape, _ in SHARDED_SMALL:
        size = int(np.prod(shape))
        out.append(flat[off:off + size].reshape(shape))
        off += size
    return out


def _chip_block(full, j, shape, axis):
    return lax.slice_in_dim(full, j * shape[axis], (j + 1) * shape[axis], axis=axis)


REST_GROUP = ("w_in", "w_out")
XA_GROUP = ("x_wq", "x_wk", "x_wv", "x_wo")
FFN_GROUP = ("ffn_w_up", "ffn_w_down")


class _Weights:
    FIRST = (0, 768)
    OWN = ((768, 1024), (1792, 1664), (3456, 704))
    NEXT = ((0, 1024), (1024, 1024), (2048, 1408), (3456, 704))
    SLOTS = ("mix_in", "sb_fwd", "dl_fwd", "dl_mix", "ffn_up", "ffn_gate", "ffn_down")

    def __init__(self, packs):
        self.packs, self.pieces, self.landed, self.plan = packs, {}, None, {}
        for slot, piece in zip(self.SLOTS[:3], self.OWN):
            self.plan[(0, slot)] = (0,) + piece
        for l in range(len(packs) - 1):
            for slot, piece in zip(self.SLOTS[3:], self.NEXT):
                self.plan[(l, slot)] = (l + 1,) + piece
        first = _run_exchange(_ex_gather(packs[0], *self.FIRST), name="gather_first")[0]
        self.pieces[(0,) + self.FIRST] = _run_exchange(_ex_gather_forward(first), name="gather_first_forward")[0]

    def ride(self, layer, slot, call):
        start, todo, ex = self.plan.get((layer, slot)), [], None
        if start is not None:
            ex = _ex_gather(self.packs[start[0]], start[1], start[2])
            todo.append(("landed", start))
        if self.landed is not None:
            key, buf = self.landed
            forward = _ex_gather_forward(buf, SEM_BASE_2 if ex is not None else 0)
            ex = forward if ex is None else _ex_join(ex, forward)
            todo.append(("piece", key))
            self.landed = None
        outs = list(call(carry=ex))
        n = len(outs) - len(todo)
        for (kind, key), buf in zip(todo, outs[n:]):
            if kind == "landed":
                self.landed = (key, buf)
            else:
                self.pieces[key] = buf
        return outs[:n]

    def rows_of(self, layer, name):
        off = 0
        for n, rows in WEIGHT_PACK:
            if n == name:
                break
            off += rows
        for (l, r0, nrows), buf in self.pieces.items():
            if l == layer and r0 <= off < r0 + nrows:
                return buf[:, off - r0:off - r0 + rows, :]
        raise KeyError(f"{name} of layer {layer} is not gathered yet")

    def weight(self, layer, name):
        return _full_from_blocks(name, self.rows_of(layer, name))

    def small(self, layer):
        planes = lax.bitcast_convert_type(self.rows_of(layer, "small").astype(jnp.bfloat16), jnp.uint16)
        planes = planes.astype(jnp.uint32)
        bits = (planes[:, :SMALL_ROWS] << 16) | planes[:, SMALL_ROWS:]
        per_chip = [_small_from_rows(r) for r in lax.bitcast_convert_type(bits, F32)]
        return {n: jnp.concatenate([blocks[k] for blocks in per_chip], axis=axis)
                for k, (n, _, axis) in enumerate(SHARDED_SMALL)}


class _Params:
    def __init__(self, weights, layer, small):
        self.weights, self.layer, self.small, self.cache = weights, layer, small, {}

    def __getitem__(self, name):
        if name in self.small:
            return self.small[name]
        if name not in self.cache:
            if name in [n for n, _, _ in SHARDED_SMALL]:
                self.cache.update(self.weights.small(self.layer))
            else:
                self.cache[name] = self.weights.weight(self.layer, name)
        return self.cache[name]


def _layer_fwd(h0, mem, p, cos, sin, tag, ride):
    sv = {"h0": h0}
    n1, u = ride("mix_in", functools.partial(_rms_mm, h0, _row(p["mix_norm_pre"]), p["w_in"], tm=1024, tn=1408,
                                             out_dtype=F32, name=f"mix_in{tag}"))
    a_out, = ride("sb_fwd", functools.partial(_sb_fwd, u, name=f"sb_fwd{tag}"))
    b_out, c = _cv_fwd(u, p["cv_w"], _row(p["cv_b"]), _row(p["cv_ln_g"]), _row(p["cv_ln_b"]),
                       p["cv_pw_w"].astype(BF16), _row(p["cv_pw_b"]), name=f"cv_fwd{tag}")
    qkv = _rope_perm(u, cos, sin, name=f"rope_perm{tag}")
    o_p, l_p = ride("dl_fwd", functools.partial(_dl_fwd, qkv, name=f"dl_fwd{tag}"))
    c_out, o_dl, lse = ride("dl_mix", functools.partial(_dl_mix, o_p, l_p, name=f"dl_mix{tag}"))
    cat = jnp.concatenate([a_out, b_out, c_out], axis=1)
    y1, h1 = _mm_post(cat, p["w_out"], h0, _row(p["mix_norm_post"]), tm=512, name=f"mix_out{tag}")
    sv.update(n1=n1, u=u, c=c, qkv=qkv, o_dl=o_dl, lse=lse, cat=cat, y1=y1, h1=h1)

    n2, q = _rms_mm(h1, _row(p["x_norm_pre"]), p["x_wq"], tm=512, tn=1024, out_dtype=BF16, name=f"xa_q{tag}")
    wkv = jnp.concatenate([p["x_wk"], p["x_wv"]], axis=1)
    mem_n, kv = _rms_mm(mem, _row(p["mem_norm"]), wkv, tm=mem.shape[0], tn=1024, out_dtype=BF16, name=f"xa_kv{tag}")
    k, v = kv[:, :D_MODEL], kv[:, D_MODEL:]
    o_x = _xa_fwd(q, k, v, name=f"xa_fwd{tag}")
    y2, h2 = _mm_post(o_x, p["x_wo"], h1, _row(p["x_norm_post"]), tm=512, name=f"xa_out{tag}")
    sv.update(n2=n2, q=q, mem_n=mem_n, k=k, v=v, o_x=o_x, y2=y2, h2=h2, wkv=wkv)

    n3, up = ride("ffn_up", functools.partial(_rms_mm, h2, _row(p["ffn_norm_pre"]), p["ffn_w_up"], tm=1024, tn=1408,
                                              out_dtype=F32, name=f"ffn_up{tag}"))
    act, = ride("ffn_gate", functools.partial(_ff_gate_fwd, up, p["ffn_conv_w"], _row(p["ffn_conv_b"]),
                                              name=f"ffn_gate{tag}"))
    y3, h3 = ride("ffn_down", functools.partial(_mm_post, act, p["ffn_w_down"], h2, _row(p["ffn_norm_post"]), tm=512,
                                                name=f"ffn_down{tag}"))
    sv.update(n3=n3, up=up, act=act, y3=y3)
    return h3, sv


def _layer_bwd(dh3, mem, p, sv, cos, sin, tag, riding, new_rs):
    g = {}
    s8 = lambda part: part.sum(axis=0)
    rode = None

    dy3, dgp = _rms_bwd(sv["y3"], _row(p["ffn_norm_post"]), dh3, None, out_dtype=BF16, tm=512, name=f"ffn_post_b{tag}")
    g["ffn_norm_post"] = s8(dgp)
    dact = _mm_nt(dy3, p["ffn_w_down"], tm=512, tn=1408, out_dtype=F32, name=f"ffn_down_bx{tag}")
    g["ffn_w_down"] = _mm_tn(sv["act"], dy3, tk=1408, tn=1024, tm=2048, name=f"ffn_down_bw{tag}")
    dgu, dvu, dcw, dcb, *got = _ff_gate_bwd(sv["up"], dact, p["ffn_conv_w"], _row(p["ffn_conv_b"]),
                                            name=f"ffn_gate_b{tag}", carry=riding.swap() if riding else None)
    scatter = riding.after_swap(got[0]) if riding else None
    g["ffn_conv_w"] = jnp.concatenate([dcw[0], dcw[1]], axis=1).reshape(3, 8, 2 * D_FF).sum(axis=1)
    g["ffn_conv_b"] = jnp.concatenate([dcb[0], dcb[1]], axis=1).sum(axis=0)
    dup = jnp.concatenate([dgu, dvu], axis=1)
    dn3 = _mm_nt(dup, p["ffn_w_up"], tm=256, tn=512, out_dtype=F32, name=f"ffn_up_bx{tag}")
    g["ffn_w_up"] = _mm_tn(sv["n3"], dup, tk=512, tn=1408, tm=2048, name=f"ffn_up_bw{tag}")
    ffn_rs = new_rs(FFN_GROUP, g, f"{tag}_ffn")
    dh2, dgp = _rms_bwd(sv["h2"], _row(p["ffn_norm_pre"]), dn3, dh3, out_dtype=F32, tm=512, name=f"ffn_pre_b{tag}")
    g["ffn_norm_pre"] = s8(dgp)

    dy2, dgp = _rms_bwd(sv["y2"], _row(p["x_norm_post"]), dh2, None, out_dtype=BF16, tm=512, name=f"xa_post_b{tag}")
    g["x_norm_post"] = s8(dgp)
    do_x = _mm_nt(dy2, p["x_wo"], tm=512, tn=1024, out_dtype=BF16, name=f"xa_out_bx{tag}")
    g["x_wo"] = _mm_tn(sv["o_x"], dy2, tk=512, tn=1024, tm=2048, name=f"xa_out_bw{tag}")
    dq, dk, dv, got = _xa_bwd(sv["q"], sv["k"], sv["v"], do_x, name=f"xa_bwd{tag}", carry=ffn_rs.swap())
    ffn_scatter = ffn_rs.after_swap(got)
    dn2 = _mm_nt(dq, p["x_wq"], tm=512, tn=1024, out_dtype=F32, name=f"xa_q_bx{tag}")
    g["x_wq"] = _mm_tn(sv["n2"], dq, tk=512, tn=1024, tm=2048, name=f"xa_q_bw{tag}")
    dkv = jnp.concatenate([dk, dv], axis=1).astype(BF16)
    nm = mem.shape[0]
    dmem_n = _mm_nt(dkv, sv["wkv"], tm=nm, tn=1024, out_dtype=F32, name=f"xa_kv_bx{tag}")
    dwkv = _mm_tn(sv["mem_n"], dkv, tk=512, tn=2048, tm=nm, name=f"xa_kv_bw{tag}")
    g["x_wk"], g["x_wv"] = dwkv[:, :D_MODEL], dwkv[:, D_MODEL:]
    _, dgp = _rms_bwd(mem, _row(p["mem_norm"]), dmem_n, None, out_dtype=BF16, tm=nm, name=f"xa_mem_b{tag}")
    g["mem_norm"] = s8(dgp)
    xa_rs = new_rs(XA_GROUP, g, f"{tag}_xa")
    dh1, dgp, got = _rms_bwd(sv["h1"], _row(p["x_norm_pre"]), dn2, dh2, out_dtype=F32, tm=512, name=f"xa_pre_b{tag}",
                             carry=xa_rs.swap())
    xa_scatter = xa_rs.after_swap(got, SEM_BASE_2 if riding else 0)
    g["x_norm_pre"] = s8(dgp)

    dy1, dgp = _rms_bwd(sv["y1"], _row(p["mix_norm_post"]), dh1, None, out_dtype=BF16, tm=512, name=f"mix_post_b{tag}")
    g["mix_norm_post"] = s8(dgp)
    dcat = _mm_nt(dy1, p["w_out"], tm=512, tn=1024, out_dtype=F32, name=f"mix_out_bx{tag}")
    g["w_out"] = _mm_tn(sv["cat"], dy1, tk=512, tn=1024, tm=2048, name=f"mix_out_bw{tag}")
    u = sv["u"]
    dq_sb, dk_sb, dv_sb, *got = _sb_bwd(u, dcat, name=f"sb_bwd{tag}",
                                        carry=_ex_join(scatter, xa_scatter) if riding else xa_scatter)
    if riding:
        rode = riding.after_scatter(got[0])
    xa_rows = xa_rs.after_scatter(got[-1])
    pw_b16 = p["cv_pw_w"].astype(BF16)
    dc, dpw, vec = _cv_bwd_local(sv["c"], dcat, _row(p["cv_ln_g"]), _row(p["cv_ln_b"]), pw_b16, name=f"cv_bwd_a{tag}")
    g["cv_pw_w"] = dpw
    vec = vec.reshape(3, 8, CV_W).sum(axis=1)
    g["cv_pw_b"], g["cv_ln_g"], g["cv_ln_b"] = vec[0], vec[1], vec[2]
    du_cv, dcw, dcb = _cv_bwd_conv(u, dc, p["cv_w"], name=f"cv_bwd_b{tag}")
    g["cv_w"] = dcw.reshape(CV_K, 8, CV_W).sum(axis=1)
    g["cv_b"] = dcb.sum(axis=0)
    dop, stats = _dl_bwd_prep(dcat, sv["o_dl"], sv["lse"], name=f"dl_prep_b{tag}")
    cur, prev, got = _dl_bwd(sv["qkv"], dop, stats, name=f"dl_bwd{tag}", carry=ffn_scatter)
    ffn_rows = ffn_rs.after_scatter(got)
    du_dl = _dl_bwd_finish(cur, prev, cos, sin, name=f"dl_fin_b{tag}")
    du = jnp.concatenate([dq_sb.astype(BF16), dk_sb.astype(BF16), dv_sb.astype(BF16), du_cv, du_dl], axis=1)
    dn1 = _mm_nt(du, p["w_in"], tm=512, tn=512, out_dtype=F32, name=f"mix_in_bx{tag}")
    g["w_in"] = _mm_tn(sv["n1"], du, tk=512, tn=1408, tm=2048, name=f"mix_in_bw{tag}")
    dh0, dgp = _rms_bwd(sv["h0"], _row(p["mix_norm_pre"]), dn1, dh1, out_dtype=F32, tm=512, name=f"mix_pre_b{tag}")
    g["mix_norm_pre"] = s8(dgp)
    return dh0, g, (xa_rows, ffn_rows), rode


def _step(x, mem, positions, loss_target, w, m, v):
    depth = w["w_in"].shape[0]
    xi, yi, ci = lax.axis_index("x"), lax.axis_index("y"), lax.axis_index("c")
    chip = 2 * xi + yi
    h = x[0]
    mem0 = mem[0]
    s_len = h.shape[0]

    def pack_rows(n, l):
        if n == "small":
            bits = lax.bitcast_convert_type(_small_to_rows([w[name][l] for name, _, _ in SHARDED_SMALL]), jnp.uint32)
            planes = [(bits >> 16).astype(jnp.uint16), (bits & 0xFFFF).astype(jnp.uint16)]
            return jnp.concatenate([lax.bitcast_convert_type(p, jnp.bfloat16) for p in planes], axis=0)
        return _to_pack_rows(n, w[n][l]).astype(BF16)

    packs = [jnp.concatenate([pack_rows(n, l) for n, _ in WEIGHT_PACK], axis=0) for l in range(depth)]
    weights = _Weights(packs)
    params = [_Params(weights, l, {n: w[n][l] for n, _, axis in SMALL if axis is None}) for l in range(depth)]

    inv_freq = ROPE_THETA ** (-jnp.arange(HD // 2, dtype=F32) / (HD // 2))
    cos, sin = _rope_tables(positions.reshape(s_len, 1), jnp.tile(inv_freq, 4).reshape(1, LANES), name="rope_tables")

    saved = []
    for l in range(depth):
        h, sv = _layer_fwd(h, mem0, params[l], cos, sin, f"_l{l}", functools.partial(weights.ride, l))
        saved.append(sv)
    dh, sq = _loss_grad(h, loss_target[0], tm=512, name="loss_grad")
    loss = lax.psum(0.5 * jnp.sum(sq) / D_MODEL, ("x", "y", "c"))

    c_arr, j_arr = jnp.reshape(ci, (1,)).astype(jnp.int32), jnp.reshape(chip, (1,)).astype(jnp.int32)

    def new_rs(names, g, tag):
        blocks = [_blocks_from_full(n, g[n]) for n in names]
        if names is REST_GROUP:
            blocks.append(jnp.stack([_small_to_rows([_chip_block(g[n], j, shape, axis) for n, shape, axis in SHARDED_SMALL])
                                     for j in range(4)]))
        return _ReduceScatter(jnp.concatenate(blocks, axis=1), c_arr, j_arr, tag)

    grads, later_rows, rest_rows, pending = [None] * depth, [None] * depth, [None] * depth, None
    for l in reversed(range(depth)):
        dh, grads[l], later_rows[l], rode = _layer_bwd(dh, mem0, params[l], saved[l], cos, sin, f"_l{l}", pending, new_rs)
        if pending is not None:
            rest_rows[l + 1] = rode
        pending = new_rs(REST_GROUP, grads[l], f"_l{l}_rest")
    grad_x = dh[None]

    out_g, out_d, out_m, out_v = {}, {}, {}, {}
    pack_off, off = {}, 0
    for n, rows in PACK_ROWS:
        pack_off[n] = (off, rows)
        off += rows

    def reduced(l, n):
        start, rows = pack_off[n]
        for names, block in ((REST_GROUP, rest_rows[l]), (XA_GROUP, later_rows[l][0]), (FFN_GROUP, later_rows[l][1])):
            if n in names:
                return block[start - pack_off[names[0]][0]:][:rows]

    def update(n, carry=None):
        shard_shape = w[n].shape
        g_n = jnp.stack([reduced(l, n) for l in range(depth)]).reshape(shard_shape)
        flat = lambda a: a.reshape(-1, shard_shape[-1])
        d_n, m_n, v_n, *rode = _adamw(flat(w[n]), flat(g_n), flat(m[n]), flat(v[n]), name=f"adamw_{n}", carry=carry)
        out_g[n], out_d[n], out_m[n], out_v[n] = g_n, d_n.reshape(shard_shape), m_n.reshape(shard_shape), v_n.reshape(shard_shape)
        return rode

    rest_rows[0] = pending.run()
    for n, _ in PACK_ROWS:
        update(n)

    g_small = _all_reduce_small(_flat_rows([grads[l][n] for l in range(depth) for n, _, axis in SMALL if axis is None]),
                                name="all_reduce_small_grads").reshape(-1)
    local_g, off = {}, 0
    for l in range(depth):
        for n, shape, axis in SMALL:
            if axis is None:
                size = int(np.prod(shape))
                local_g.setdefault(n, []).append(g_small[off:off + size].reshape(shape))
                off += size
        small_rows = rest_rows[l][sum(pack_off[n][1] for n in REST_GROUP):]
        for (n, _, _), block in zip(SHARDED_SMALL, _small_from_rows(small_rows)):
            local_g.setdefault(n, []).append(block)
    names = [n for n, _, _ in SMALL]
    g_loc = {n: jnp.stack(local_g[n]) for n in names}
    d_s, m_s, v_s = _adamw(_flat_rows([w[n] for n in names]), _flat_rows([g_loc[n] for n in names]),
                           _flat_rows([m[n] for n in names]), _flat_rows([v[n] for n in names]), name="adamw_small")
    off = 0
    for n in names:
        size = int(np.prod(w[n].shape))
        take = lambda a: a.reshape(-1)[off:off + size].reshape(w[n].shape)
        out_g[n], out_d[n], out_m[n], out_v[n] = g_loc[n], take(d_s), take(m_s), take(v_s)
        off += size

    outs = [loss, grad_x]
    for group in (out_g, out_d, out_m, out_v):
        outs += [group[n] for n in WEIGHT_ORDER]
    return tuple(outs)


def kernel(x, mem, positions, mix_norm_pre, w_in, cv_w, cv_b, cv_ln_g, cv_ln_b, cv_pw_w, cv_pw_b, w_out, mix_norm_post, x_norm_pre, mem_norm, x_wq, x_wk, x_wv, x_wo, x_norm_post, ffn_norm_pre, ffn_w_up, ffn_conv_w, ffn_conv_b, ffn_w_down, ffn_norm_post, loss_target, m_mix_norm_pre, m_w_in, m_cv_w, m_cv_b, m_cv_ln_g, m_cv_ln_b, m_cv_pw_w, m_cv_pw_b, m_w_out, m_mix_norm_post, m_x_norm_pre, m_mem_norm, m_x_wq, m_x_wk, m_x_wv, m_x_wo, m_x_norm_post, m_ffn_norm_pre, m_ffn_w_up, m_ffn_conv_w, m_ffn_conv_b, m_ffn_w_down, m_ffn_norm_post, v_mix_norm_pre, v_w_in, v_cv_w, v_cv_b, v_cv_ln_g, v_cv_ln_b, v_cv_pw_w, v_cv_pw_b, v_w_out, v_mix_norm_post, v_x_norm_pre, v_mem_norm, v_x_wq, v_x_wk, v_x_wv, v_x_wo, v_x_norm_post, v_ffn_norm_pre, v_ffn_w_up, v_ffn_conv_w, v_ffn_conv_b, v_ffn_w_down, v_ffn_norm_post):
    w = dict(zip(WEIGHT_ORDER, (mix_norm_pre, w_in, cv_w, cv_b, cv_ln_g, cv_ln_b, cv_pw_w, cv_pw_b, w_out, mix_norm_post, x_norm_pre, mem_norm, x_wq, x_wk, x_wv, x_wo, x_norm_post, ffn_norm_pre, ffn_w_up, ffn_conv_w, ffn_conv_b, ffn_w_down, ffn_norm_post)))
    m = dict(zip(WEIGHT_ORDER, (m_mix_norm_pre, m_w_in, m_cv_w, m_cv_b, m_cv_ln_g, m_cv_ln_b, m_cv_pw_w, m_cv_pw_b, m_w_out, m_mix_norm_post, m_x_norm_pre, m_mem_norm, m_x_wq, m_x_wk, m_x_wv, m_x_wo, m_x_norm_post, m_ffn_norm_pre, m_ffn_w_up, m_ffn_conv_w, m_ffn_conv_b, m_ffn_w_down, m_ffn_norm_post)))
    v = dict(zip(WEIGHT_ORDER, (v_mix_norm_pre, v_w_in, v_cv_w, v_cv_b, v_cv_ln_g, v_cv_ln_b, v_cv_pw_w, v_cv_pw_b, v_w_out, v_mix_norm_post, v_x_norm_pre, v_mem_norm, v_x_wq, v_x_wk, v_x_wv, v_x_wo, v_x_norm_post, v_ffn_norm_pre, v_ffn_w_up, v_ffn_conv_w, v_ffn_conv_b, v_ffn_w_down, v_ffn_norm_post)))
    return _step(x, mem, positions, loss_target, w, m, v)
```

```python
import functools

import jax
import jax.numpy as jnp
import numpy as np
from jax import lax
from jax.experimental import pallas as pl
from jax.experimental.pallas import tpu as pltpu

F32, BF16 = jnp.float32, jnp.bfloat16
MESH = pl.DeviceIdType.MESH
EPS = 1e-6
LANES = 128
BLK = 128
HD = 64
D_MODEL = 1024
D_FF = 2816
SB_W, CV_W, DL_W = 256, 256, 512
CV_K = 31
ROPE_THETA = 10000.0
DILATIONS = (1, 4, 16)
X_HEADS, X_HD = 4, 256
ADAM_LR, ADAM_B1, ADAM_B2, ADAM_EPS, ADAM_WD, ADAM_STEP = 0.001, 0.9, 0.999, 1e-08, 0.01, 10
NEG_INF = float("-inf")
MIB = 1 << 20

PACK_ROWS = (("w_in", 704), ("w_out", 256), ("x_wq", 256), ("x_wk", 256), ("x_wv", 256), ("x_wo", 256),
             ("ffn_w_up", 1408), ("ffn_w_down", 704))
PACK_RL = sum(r for _, r in PACK_ROWS)
SHARDED_SMALL = (("cv_w", (31, 64), 1), ("ffn_conv_w", (3, 1408), 1), ("cv_pw_w", (64, 256), 0))
SMALL_ROWS = 32
WEIGHT_PACK = (PACK_ROWS[0], ("small", 2 * SMALL_ROWS)) + PACK_ROWS[1:]


def _cp(vmem_mb=48):
    return pltpu.CompilerParams(vmem_limit_bytes=vmem_mb * MIB)


def _dot(a, b):
    return jnp.dot(a, b, preferred_element_type=F32)


def _dot_nt(a, b):
    return lax.dot_general(a, b, (((1,), (1,)), ((), ())), preferred_element_type=F32)


def _dot_tn(a, b):
    return lax.dot_general(a, b, (((0,), (0,)), ((), ())), preferred_element_type=F32)


def _dot_hilo(x, m):
    hi = x.astype(BF16)
    lo = (x - hi.astype(F32)).astype(BF16)
    return _dot(hi, m) + _dot(lo, m)


def _rowsum8(x):
    t, c = x.shape
    return x.reshape(t // 8, 8, c).sum(axis=0)


def _acc_out(ref, i, val):
    @pl.when(i == 0)
    def _():
        ref[...] = val

    @pl.when(i > 0)
    def _():
        ref[...] += val


def _tile(n, cap, mult=8):
    t = min(n, cap)
    while n % t or t % mult:
        t -= 1
    return t


def _rms_mm(x, g, w, *, tm, tn, out_dtype, name, carry=None):
    m, d = x.shape
    n_out = w.shape[1]

    def body(x_ref, g_ref, w_ref, n_ref, o_ref):
        @pl.when(pl.program_id(1) == 0)
        def _():
            xv = x_ref[...]
            r = lax.rsqrt(jnp.mean(xv * xv, axis=-1, keepdims=True) + EPS)
            n_ref[...] = (xv * r * g_ref[...]).astype(BF16)

        o_ref[...] = _dot(n_ref[...], w_ref[...]).astype(out_dtype)

    return _call(
        body, grid=(m // tm, n_out // tn), name=name, carry=carry,
        in_specs=[pl.BlockSpec((tm, d), lambda i, j: (i, 0)), pl.BlockSpec((1, d), lambda i, j: (0, 0)),
                  pl.BlockSpec((d, tn), lambda i, j: (0, j))],
        out_specs=[pl.BlockSpec((tm, d), lambda i, j: (i, 0)), pl.BlockSpec((tm, tn), lambda i, j: (i, j))],
        out_shape=[jax.ShapeDtypeStruct((m, d), BF16), jax.ShapeDtypeStruct((m, n_out), out_dtype)],
        args=(x, g, w))


def _mm_post(a, w, h, g, *, tm, name, carry=None):
    m, k = a.shape
    d = w.shape[1]

    def body(a_ref, w_ref, h_ref, g_ref, y_ref, ho_ref):
        y = _dot(a_ref[...], w_ref[...])
        y_ref[...] = y
        r = lax.rsqrt(jnp.mean(y * y, axis=-1, keepdims=True) + EPS)
        ho_ref[...] = h_ref[...] + y * r * g_ref[...]

    return _call(
        body, grid=(m // tm,), name=name, carry=carry,
        in_specs=[pl.BlockSpec((tm, k), lambda i: (i, 0)), pl.BlockSpec((k, d), lambda i: (0, 0)),
                  pl.BlockSpec((tm, d), lambda i: (i, 0)), pl.BlockSpec((1, d), lambda i: (0, 0))],
        out_specs=[pl.BlockSpec((tm, d), lambda i: (i, 0)), pl.BlockSpec((tm, d), lambda i: (i, 0))],
        out_shape=[jax.ShapeDtypeStruct((m, d), F32), jax.ShapeDtypeStruct((m, d), F32)],
        args=(a, w, h, g))


def _mm_nt(a, w, *, tm, tn, out_dtype, name):
    m, k = a.shape
    n_out = w.shape[0]

    def body(a_ref, w_ref, o_ref):
        o_ref[...] = _dot_nt(a_ref[...], w_ref[...]).astype(out_dtype)

    return pl.pallas_call(
        body, grid=(n_out // tn, m // tm), name=name,
        in_specs=[pl.BlockSpec((tm, k), lambda j, i: (i, 0)), pl.BlockSpec((tn, k), lambda j, i: (j, 0))],
        out_specs=pl.BlockSpec((tm, tn), lambda j, i: (i, j)),
        out_shape=jax.ShapeDtypeStruct((m, n_out), out_dtype),
        compiler_params=_cp())(a, w)


def _mm_tn(x, dy, *, tk, tn, tm, name):
    m, k = x.shape
    n_out = dy.shape[1]

    def body(x_ref, d_ref, o_ref):
        _acc_out(o_ref, pl.program_id(2), _dot_tn(x_ref[...], d_ref[...]))

    return pl.pallas_call(
        body, grid=(k // tk, n_out // tn, m // tm), name=name,
        in_specs=[pl.BlockSpec((tm, tk), lambda a, b, c: (c, a)), pl.BlockSpec((tm, tn), lambda a, b, c: (c, b))],
        out_specs=pl.BlockSpec((tk, tn), lambda a, b, c: (a, b)),
        out_shape=jax.ShapeDtypeStruct((k, n_out), F32),
        compiler_params=_cp())(x, dy)


def _rms_bwd(x, g, dout, res, *, out_dtype, tm, name, carry=None):
    m, d = x.shape
    has_res = res is not None

    def body(*refs):
        if has_res:
            x_ref, g_ref, d_ref, r_ref, dx_ref, dg_ref = refs
        else:
            x_ref, g_ref, d_ref, dx_ref, dg_ref = refs
        xv = x_ref[...]
        dv = d_ref[...].astype(F32)
        r = lax.rsqrt(jnp.mean(xv * xv, axis=-1, keepdims=True) + EPS)
        xh = xv * r
        dxh = dv * g_ref[...]
        dx = r * (dxh - xh * jnp.mean(dxh * xh, axis=-1, keepdims=True))
        if has_res:
            dx = dx + r_ref[...]
        dx_ref[...] = dx.astype(out_dtype)
        _acc_out(dg_ref, pl.program_id(0), _rowsum8(dv * xh))

    row = pl.BlockSpec((tm, d), lambda i: (i, 0))
    ins = [row, pl.BlockSpec((1, d), lambda i: (0, 0)), row] + ([row] if has_res else [])
    args = (x, g, dout) + ((res,) if has_res else ())
    return _call(
        body, grid=(m // tm,), name=name, carry=carry, in_specs=ins,
        out_specs=[row, pl.BlockSpec((8, d), lambda i: (0, 0))],
        out_shape=[jax.ShapeDtypeStruct((m, d), out_dtype), jax.ShapeDtypeStruct((8, d), F32)], args=args)


def _loss_grad(h, tgt, *, tm, name):
    m, d = h.shape

    def body(h_ref, t_ref, dh_ref, p_ref):
        e = h_ref[...] - t_ref[...]
        dh_ref[...] = e / d
        _acc_out(p_ref, pl.program_id(0), _rowsum8(e * e))

    row = pl.BlockSpec((tm, d), lambda i: (i, 0))
    return pl.pallas_call(
        body, grid=(m // tm,), name=name, in_specs=[row, row],
        out_specs=[row, pl.BlockSpec((8, d), lambda i: (0, 0))],
        out_shape=[jax.ShapeDtypeStruct((m, d), F32), jax.ShapeDtypeStruct((8, d), F32)],
        compiler_params=_cp())(h, tgt)


def _adamw(w, g, m, v, *, name, carry=None):
    r, c = w.shape
    tr = _tile(r, 256)

    def body(w_ref, g_ref, m_ref, v_ref, d_ref, mo_ref, vo_ref):
        gv = g_ref[...]
        m2 = ADAM_B1 * m_ref[...] + (1.0 - ADAM_B1) * gv
        v2 = ADAM_B2 * v_ref[...] + (1.0 - ADAM_B2) * jnp.square(gv)
        m_hat = m2 / (1.0 - ADAM_B1 ** ADAM_STEP)
        v_hat = v2 / (1.0 - ADAM_B2 ** ADAM_STEP)
        d_ref[...] = -ADAM_LR * (m_hat / (jnp.sqrt(v_hat) + ADAM_EPS) + ADAM_WD * w_ref[...])
        mo_ref[...] = m2
        vo_ref[...] = v2

    blk = pl.BlockSpec((tr, c), lambda i: (i, 0))
    return _call(body, grid=(r // tr,), name=name, carry=carry, in_specs=[blk] * 4, out_specs=[blk] * 3,
                 out_shape=[jax.ShapeDtypeStruct((r, c), F32)] * 3, args=(w, g, m, v))


def _head_masks():
    lane = lax.broadcasted_iota(jnp.int32, (BLK, LANES), 1)
    row = lax.broadcasted_iota(jnp.int32, (BLK, LANES), 0)
    return lane, row, lane < HD


def _sb_scores(q_a, k, before):
    z = _dot_nt(q_a, k)
    sp = jnp.log1p(jnp.exp(-jnp.abs(z)))
    ls_pos = jnp.minimum(z, 0.0) - sp
    lkeep = jnp.where(before, ls_pos - z, 0.0)
    return ls_pos, lkeep


SB_DEAD = -104.0


def _sb_alive(jj, i, carry):
    return jnp.logical_and(jj <= i, jnp.max(carry) > SB_DEAD)


SB_QB_FWD = 2
SB_QB = 2


def _sb_before(jj, qb=SB_QB):
    lane = lax.broadcasted_iota(jnp.int32, (qb * 2 * BLK, LANES), 1)
    row = lax.broadcasted_iota(jnp.int32, (qb * 2 * BLK, LANES), 0)
    below_diag = jj - (qb - 1) + row // (2 * BLK)
    return jnp.logical_or(below_diag > 0, jnp.logical_and(below_diag == 0, lane < row % BLK))


def _sb_stack(x, lane_h, qb=SB_QB):
    return jnp.concatenate([_stack_heads(x[b * BLK:(b + 1) * BLK], lane_h) for b in range(qb)], axis=0)


def _sb_unstack(x, lane_h, qb=SB_QB):
    return jnp.concatenate([jnp.where(lane_h, x[2 * b * BLK:(2 * b + 1) * BLK], x[(2 * b + 1) * BLK:(2 * b + 2) * BLK])
                            for b in range(qb)], axis=0)


SB_ROWS = SB_QB * 2 * BLK


def _sb_fwd(u, *, name, carry=None):
    s_len = u.shape[0]
    qb = SB_QB_FWD
    qrows, rows = qb * BLK, qb * 2 * BLK

    def body(q_ref, k_ref, v_ref, o_ref):
        top = pl.program_id(0) * qb + qb - 1
        lane, row, lane_h = _head_masks()
        suffix = (row > lane).astype(BF16)
        pairs = [slice(hp * LANES, (hp + 1) * LANES) for hp in range(2)]
        qs = [_sb_stack(q_ref[:, cs] * 0.125, lane_h, qb) for cs in pairs]

        def step(state):
            jj, ccs, accs = state[0], state[1:3], state[3:5]
            rows_k = pl.ds(pl.multiple_of((top - jj) * BLK, BLK), BLK)
            before = _sb_before(jj, qb)
            scores = [_sb_scores(q, k_ref[rows_k, cs].astype(BF16), before) for q, cs in zip(qs, pairs)]
            between = [_dot_hilo(lkeep, suffix) + cc for (_, lkeep), cc in zip(scores, ccs)]
            atts = [jnp.where(before, jnp.exp(ls_pos + b), 0.0).astype(BF16) for (ls_pos, _), b in zip(scores, between)]
            new_cc = [cc + jnp.sum(lkeep, axis=1, keepdims=True) for (_, lkeep), cc in zip(scores, ccs)]
            new_acc = [acc + _dot(a, v_ref[rows_k, cs].astype(BF16)) for a, acc, cs in zip(atts, accs, pairs)]
            return (jj + 1, *new_cc, *new_acc)

        zc, za = jnp.zeros((rows, 1), F32), jnp.zeros((rows, LANES), F32)
        res = lax.while_loop(lambda st: _sb_alive(st[0], top, jnp.maximum(st[1], st[2])), step,
                             (jnp.int32(0), zc, zc, za, za))
        for hp, cs in enumerate(pairs):
            o_ref[:, cs] = _sb_unstack(res[3 + hp], lane_h, qb).astype(BF16)

    wide = 2 * LANES
    return _call(
        body, grid=(s_len // qrows,), name=name, carry=carry,
        in_specs=[pl.BlockSpec((qrows, wide), lambda i: (i, 0)), pl.BlockSpec((s_len, wide), lambda i: (0, 1)),
                  pl.BlockSpec((s_len, wide), lambda i: (0, 2))],
        out_specs=[pl.BlockSpec((qrows, wide), lambda i: (i, 0))],
        out_shape=[jax.ShapeDtypeStruct((s_len, SB_W), BF16)], args=(u, u, u))


def _sb_bwd(u, dcat, *, name, carry=None):
    s_len = u.shape[0]
    nq = s_len // BLK
    qrows = SB_QB * BLK

    def body(q_ref, k_ref, v_ref, do_ref, dq_ref, dk_ref, dv_ref, g_scr, b_scr):
        step = pl.program_id(1)
        top = step * SB_QB + SB_QB - 1
        lane, row, lane_h = _head_masks()
        suffix = (row > lane).astype(BF16)
        prefix = (row < lane).astype(BF16)
        qf = q_ref[...]
        qs = _sb_stack(qf * 0.125, lane_h)
        qu = _sb_stack(qf, lane_h)
        dos = _sb_stack(do_ref[...], lane_h)

        @pl.when(step == 0)
        def _():
            dk_ref[...] = jnp.zeros_like(dk_ref)
            dv_ref[...] = jnp.zeros_like(dv_ref)

        def down(state):
            jj, cc = state
            j = top - jj
            off = pl.multiple_of(j * BLK, BLK)
            k = k_ref[pl.ds(off, BLK), :].astype(BF16)
            v = v_ref[pl.ds(off, BLK), :].astype(BF16)
            before = _sb_before(jj)
            ls_pos, lkeep = _sb_scores(qs, k, before)
            between = _dot_hilo(lkeep, suffix) + cc
            att = jnp.where(before, jnp.exp(ls_pos + between), 0.0)
            g_scr[j] = att * _dot_nt(dos, v)
            b_scr[j] = jnp.exp(ls_pos)
            dv_ref[pl.ds(off, BLK), :] += _dot_tn(att.astype(BF16), dos)
            return jj + 1, cc + jnp.sum(lkeep, axis=1, keepdims=True)

        zc = jnp.zeros((SB_ROWS, 1), F32)
        visited = lax.while_loop(lambda st: _sb_alive(st[0], top, st[1]), down, (jnp.int32(0), zc))[0]

        def up(j, carry):
            pc, dq = carry
            off = pl.multiple_of(j * BLK, BLK)
            k = k_ref[pl.ds(off, BLK), :].astype(BF16)
            g, beta = g_scr[j], b_scr[j]
            below = _dot_hilo(g, prefix) + pc
            dz = (jnp.where(_sb_before(top - j), g * (1.0 - beta) - beta * below, 0.0) * 0.125).astype(BF16)
            dk_ref[pl.ds(off, BLK), :] += _dot_tn(dz, qu)
            return pc + jnp.sum(g, axis=1, keepdims=True), dq + _dot(dz, k)

        dq = lax.fori_loop(top + 1 - visited, top + 1, up, (zc, jnp.zeros((SB_ROWS, LANES), F32)))[1]
        dq_ref[...] = _sb_unstack(dq, lane_h)

    col = lambda c0: pl.BlockSpec((s_len, LANES), lambda hp, i: (0, c0 + hp))
    blk = pl.BlockSpec((qrows, LANES), lambda hp, i: (i, hp))
    acc = pl.BlockSpec((s_len, LANES), lambda hp, i: (0, hp))
    return _call(
        body, grid=(2, s_len // qrows), name=name, carry=carry, in_specs=[blk, col(2), col(4), blk],
        out_specs=[blk, acc, acc], out_shape=[jax.ShapeDtypeStruct((s_len, SB_W), F32)] * 3,
        scratch_shapes=[pltpu.VMEM((nq, SB_ROWS, LANES), F32), pltpu.VMEM((nq, SB_ROWS, LANES), F32)],
        vmem_mb=56, args=(u, u, u, dcat))


CV_T = 512
CV_H = 32
CV_STRIP = 64


def _cv_specs(s_len):
    cur = lambda c: pl.BlockSpec((CV_T, CV_W), lambda i: (i, c))
    prev = lambda c: pl.BlockSpec((CV_H, CV_W), lambda i: (jnp.maximum(i * (CV_T // CV_H) - 1, 0), c))
    nxt = lambda c: pl.BlockSpec((CV_H, CV_W),
                                 lambda i: (jnp.minimum((i + 1) * (CV_T // CV_H), s_len // CV_H - 1), c))
    full = lambda r: pl.BlockSpec((r, CV_W), lambda i: (0, 0))
    return cur, prev, nxt, full


def _glu_into(gp_ref, val_ref, gate_ref, valp_ref, gatep_ref, i):
    gp_ref[0:CV_H, :] = jnp.where(i > 0, valp_ref[...] * jax.nn.sigmoid(gatep_ref[...]), 0.0)
    gp_ref[CV_H:, :] = val_ref[...] * jax.nn.sigmoid(gate_ref[...])


CV_SH = CV_T + CV_H - 8


def _shifted_copies(sh_ref, slab_ref):
    for r in range(1, 8):
        sh_ref[r - 1] = slab_ref[pl.ds(r, CV_SH), :]


def _tap(sh_ref, slab_ref, off, r0, rows):
    if off % 8 == 0:
        return slab_ref[pl.ds(off + r0, rows), :]
    return sh_ref[off % 8 - 1, pl.ds(off - off % 8 + r0, rows), :]


def _cv_fwd(u, cv_w, cv_b, ln_g, ln_b, pw_w, pw_b, *, name):
    s_len = u.shape[0]
    cur, prev, _, full = _cv_specs(s_len)

    def body(val_ref, gate_ref, valp_ref, gatep_ref, w_ref, b_ref, g_ref, be_ref, pw_ref, pb_ref,
             o_ref, c_ref, gp_ref, sh_ref):
        _glu_into(gp_ref, val_ref, gate_ref, valp_ref, gatep_ref, pl.program_id(0))
        _shifted_copies(sh_ref, gp_ref)
        for r0, rows in _strips(CV_T, CV_STRIP):
            acc = jnp.zeros((rows, CV_W), F32) + b_ref[...]
            for k in range(CV_K):
                acc = acc + w_ref[k:k + 1, :] * _tap(sh_ref, gp_ref, CV_H - CV_K + 1 + k, r0, rows)
            c_ref[r0:r0 + rows, :] = acc
        acc = c_ref[...]
        mu = jnp.mean(acc, axis=-1, keepdims=True)
        xc = acc - mu
        xh = xc * lax.rsqrt(jnp.mean(xc * xc, axis=-1, keepdims=True) + EPS)
        a = xh * g_ref[...] + be_ref[...]
        s = a * jax.nn.sigmoid(a)
        o_ref[...] = (_dot(s.astype(BF16), pw_ref[...]) + pb_ref[...]).astype(BF16)

    return pl.pallas_call(
        body, grid=(s_len // CV_T,), name=name,
        in_specs=[cur(3), cur(4), prev(3), prev(4), full(CV_K), full(1), full(1), full(1), full(CV_W), full(1)],
        out_specs=[cur(0), cur(0)],
        out_shape=[jax.ShapeDtypeStruct((s_len, CV_W), BF16), jax.ShapeDtypeStruct((s_len, CV_W), F32)],
        scratch_shapes=[pltpu.VMEM((CV_T + CV_H, CV_W), F32), pltpu.VMEM((7, CV_SH, CV_W), F32)],
        compiler_params=_cp())(u, u, u, u, cv_w, cv_b, ln_g, ln_b, pw_w, pw_b)


def _cv_bwd_local(c, dcat, ln_g, ln_b, pw_w, *, name):
    s_len = c.shape[0]
    cur, _, _, full = _cv_specs(s_len)

    def body(c_ref, db_ref, g_ref, be_ref, pw_ref, dc_ref, dpw_ref, vec_ref):
        i = pl.program_id(0)
        cv = c_ref[...]
        db = db_ref[...]
        mu = jnp.mean(cv, axis=-1, keepdims=True)
        xc = cv - mu
        rstd = lax.rsqrt(jnp.mean(xc * xc, axis=-1, keepdims=True) + EPS)
        xh = xc * rstd
        a = xh * g_ref[...] + be_ref[...]
        sg = jax.nn.sigmoid(a)
        s = a * sg
        dbb = db.astype(BF16)
        ds = _dot_nt(dbb, pw_ref[...])
        da = ds * (sg * (1.0 + a * (1.0 - sg)))
        dxh = da * g_ref[...]
        dc_ref[...] = rstd * (dxh - jnp.mean(dxh, axis=-1, keepdims=True)
                              - xh * jnp.mean(dxh * xh, axis=-1, keepdims=True))
        _acc_out(dpw_ref, i, _dot_tn(s.astype(BF16), dbb))
        _acc_out(vec_ref, i, jnp.concatenate([_rowsum8(db), _rowsum8(da * xh), _rowsum8(da)], axis=0))

    return pl.pallas_call(
        body, grid=(s_len // CV_T,), name=name,
        in_specs=[cur(0), cur(1), full(1), full(1), full(CV_W)],
        out_specs=[cur(0), full(CV_W), full(24)],
        out_shape=[jax.ShapeDtypeStruct((s_len, CV_W), F32), jax.ShapeDtypeStruct((CV_W, CV_W), F32),
                   jax.ShapeDtypeStruct((24, CV_W), F32)], compiler_params=_cp())(c, dcat, ln_g, ln_b, pw_w)


def _cv_bwd_conv(u, dc, cv_w, *, name):
    s_len = u.shape[0]
    cur, prev, nxt, full = _cv_specs(s_len)
    last = s_len // CV_T - 1

    def body(val_ref, gate_ref, valp_ref, gatep_ref, dc_ref, dcn_ref, w_ref, du_ref, dw_ref, dbias_ref,
             gp_ref, dcp_ref, gsh_ref, dsh_ref):
        i = pl.program_id(0)
        _glu_into(gp_ref, val_ref, gate_ref, valp_ref, gatep_ref, i)
        dcv = dc_ref[...]
        dcp_ref[0:CV_T, :] = dcv
        dcp_ref[CV_T:, :] = jnp.where(i < last, dcn_ref[...], 0.0)
        _shifted_copies(gsh_ref, gp_ref)
        _shifted_copies(dsh_ref, dcp_ref)
        strips = _strips(CV_T, CV_STRIP)
        for r0, rows in strips:
            dg = jnp.zeros((rows, CV_W), F32)
            for k in range(CV_K):
                dg = dg + w_ref[k:k + 1, :] * _tap(dsh_ref, dcp_ref, CV_K - 1 - k, r0, rows)
            sg = jax.nn.sigmoid(gate_ref[r0:r0 + rows, :])
            du_ref[r0:r0 + rows, 0:CV_W] = (dg * sg).astype(BF16)
            du_ref[r0:r0 + rows, CV_W:] = (dg * val_ref[r0:r0 + rows, :] * sg * (1.0 - sg)).astype(BF16)
        parts = []
        for k in range(CV_K):
            part = jnp.zeros((8, CV_W), F32)
            for r0, rows in strips:
                part = part + _rowsum8(dc_ref[r0:r0 + rows, :] * _tap(gsh_ref, gp_ref, CV_H - CV_K + 1 + k, r0, rows))
            parts.append(part)
        _acc_out(dw_ref, i, jnp.concatenate(parts, axis=0))
        _acc_out(dbias_ref, i, _rowsum8(dcv))

    return pl.pallas_call(
        body, grid=(s_len // CV_T,), name=name,
        in_specs=[cur(3), cur(4), prev(3), prev(4), cur(0), nxt(0), full(CV_K)],
        out_specs=[pl.BlockSpec((CV_T, 2 * CV_W), lambda i: (i, 0)), full(CV_K * 8), full(8)],
        out_shape=[jax.ShapeDtypeStruct((s_len, 2 * CV_W), BF16), jax.ShapeDtypeStruct((CV_K * 8, CV_W), F32),
                   jax.ShapeDtypeStruct((8, CV_W), F32)],
        scratch_shapes=[pltpu.VMEM((CV_T + CV_H, CV_W), F32)] * 2 + [pltpu.VMEM((7, CV_SH, CV_W), F32)] * 2,
        compiler_params=_cp())(u, u, u, u, dc, dc, cv_w)


def _rope_tables(pos_col, inv_freq_row, *, name):
    s_len = pos_col.shape[0]

    def body(p_ref, f_ref, cos_ref, sin_ref):
        ang = p_ref[...].astype(F32) * f_ref[...]
        lane = lax.broadcasted_iota(jnp.int32, (s_len, LANES), 1)
        sn = jnp.sin(ang)
        cos_ref[...] = jnp.cos(ang)
        sin_ref[...] = jnp.where(lane % HD < HD // 2, -sn, sn)

    return pl.pallas_call(body, name=name, out_shape=[jax.ShapeDtypeStruct((s_len, LANES), F32)] * 2,
                          compiler_params=_cp())(pos_col, inv_freq_row)


def _rot_half(x):
    lane = lax.broadcasted_iota(jnp.int32, x.shape, 1)
    return jnp.where(lane % HD < HD // 2, pltpu.roll(x, LANES - HD // 2, 1), pltpu.roll(x, HD // 2, 1))


def _permute_rows(dst_ref, src_ref, d, dtype):
    s_len = src_ref.shape[0]
    seg = s_len // d
    if d == 1:
        dst_ref[...] = src_ref[...].astype(dtype)
        return
    for r in range(d):
        dst_ref[r * seg:(r + 1) * seg, :] = src_ref[pl.ds(r, seg, stride=d), :].astype(dtype)


def _unpermute_rows(dst_ref, src_ref, d):
    s_len = src_ref.shape[0]
    seg = s_len // d
    if d == 1:
        dst_ref[...] = src_ref[...]
        return
    for r in range(d):
        dst_ref[pl.ds(r, seg, stride=d), :] = src_ref[r * seg:(r + 1) * seg, :]


def _rope_perm(u, cos, sin, *, name):
    s_len = u.shape[0]

    def body(x_ref, cos_ref, sin_ref, o_ref, scr):
        a = pl.program_id(0)
        x = x_ref[...]
        rot = a < 2
        scr[...] = x * jnp.where(rot, cos_ref[...], 1.0) + _rot_half(x) * jnp.where(rot, sin_ref[...], 0.0)
        for n, d in enumerate(DILATIONS):
            _permute_rows(o_ref.at[n], scr, d, BF16)

    tab = pl.BlockSpec((s_len, LANES), lambda a, cb: (0, 0))
    return pl.pallas_call(
        body, grid=(3, 4), name=name,
        in_specs=[pl.BlockSpec((s_len, LANES), lambda a, cb: (0, 10 + 4 * a + cb)), tab, tab],
        out_specs=pl.BlockSpec((None, 3, s_len, LANES), lambda a, cb: (a, 0, 0, cb)),
        out_shape=jax.ShapeDtypeStruct((3, 3, s_len, DL_W), BF16),
        scratch_shapes=[pltpu.VMEM((s_len, LANES), F32)], compiler_params=_cp())(u, cos, sin)


DL_UNROLL = 4


def _dl_band(rows):
    lane = lax.broadcasted_iota(jnp.int32, (rows, LANES), 1)
    row = lax.broadcasted_iota(jnp.int32, (rows, LANES), 0) % BLK
    return lane <= row, lane >= row


def _dl_first(s_len, n, i):
    nb = jnp.where(n == 0, s_len // BLK, jnp.where(n == 1, s_len // (BLK * DILATIONS[1]),
                                                   s_len // (BLK * DILATIONS[2])))
    return lax.rem(i, nb) == 0


def _stack_heads(x, lane_h):
    return jnp.concatenate([jnp.where(lane_h, x, 0.0), jnp.where(lane_h, 0.0, x)], axis=0).astype(BF16)


def _dl_rows(i):
    cur = pl.ds(pl.multiple_of(i * BLK, BLK), BLK)
    prev = pl.ds(pl.multiple_of(jnp.maximum(i - 1, 0) * BLK, BLK), BLK)
    return cur, prev


def _dl_in_specs(s_len):
    return [pl.BlockSpec((None, None, s_len, LANES), functools.partial(lambda a, n, hp: (a, n, 0, hp), a))
            for a in range(3)]


def _dl_fwd(qkv, *, name, carry=None):
    s_len = qkv.shape[2]

    def body(q_ref, k_ref, v_ref, o_ref, l_ref):
        n = pl.program_id(0)
        lane_h = _head_masks()[2]
        band_c, band_p = _dl_band(2 * BLK)
        ones = jnp.ones((BLK, LANES), BF16)

        @pl.loop(0, s_len // BLK, step=DL_UNROLL)
        def _(i0):
            blocks = [i0 + t for t in range(DL_UNROLL)]
            rows = [_dl_rows(i) for i in blocks]
            scores = []
            for cur, prev in rows:
                qs = _stack_heads(q_ref[cur, :] * 0.125, lane_h)
                scores.append((_dot_nt(qs, k_ref[cur, :]), _dot_nt(qs, k_ref[prev, :])))
            probs = []
            for i, (sc, sp) in zip(blocks, scores):
                sc = jnp.where(band_c, sc, NEG_INF)
                sp = jnp.where(jnp.logical_and(band_p, jnp.logical_not(_dl_first(s_len, n, i))), sp, NEG_INF)
                m = jnp.max(jnp.maximum(sc, sp), axis=1, keepdims=True)
                probs.append((jnp.exp(sc - m).astype(BF16), jnp.exp(sp - m).astype(BF16), m))
            for (cur, prev), (pc, pp, m) in zip(rows, probs):
                r = (_dot(pc, jnp.concatenate([v_ref[cur, :], ones], axis=1))
                     + _dot(pp, jnp.concatenate([v_ref[prev, :], ones], axis=1)))
                den = jnp.where(lane_h, r[:BLK, LANES:], r[BLK:, LANES:])
                o_ref[cur, :] = jnp.where(lane_h, r[:BLK, :LANES], r[BLK:, :LANES]) / den
                l_ref[cur, :] = jnp.where(lane_h, m[:BLK], m[BLK:]) + jnp.log(den)

    out = pl.BlockSpec((None, s_len, LANES), lambda n, hp: (n, 0, hp))
    return _call(
        body, grid=(3, 4), name=name, carry=carry, in_specs=_dl_in_specs(s_len), out_specs=[out, out],
        out_shape=[jax.ShapeDtypeStruct((3, s_len, DL_W), F32)] * 2, args=(qkv, qkv, qkv))


def _dl_mix(o_p, l_p, *, name, carry=None):
    s_len = o_p.shape[1]

    def body(o_ref, l_ref, ob_ref, of_ref, lt_ref, o_scr, l_scr):
        n = pl.program_id(1)
        for k, d in enumerate(DILATIONS):
            @pl.when(n == k)
            def _(k=k, d=d):
                _unpermute_rows(o_scr.at[k], o_ref, d)
                _unpermute_rows(l_scr.at[k], l_ref, d)

        @pl.when(n == 2)
        def _():
            l0, l1, l2 = l_scr[0], l_scr[1], l_scr[2]
            m = jnp.maximum(jnp.maximum(l0, l1), l2)
            e0, e1, e2 = jnp.exp(l0 - m), jnp.exp(l1 - m), jnp.exp(l2 - m)
            den = e0 + e1 + e2
            o = (e0 / den) * o_scr[0] + (e1 / den) * o_scr[1] + (e2 / den) * o_scr[2]
            of_ref[...] = o
            ob_ref[...] = o.astype(BF16)
            lt_ref[...] = m + jnp.log(den)

    inb = pl.BlockSpec((None, s_len, LANES), lambda cb, n: (n, 0, cb))
    outb = pl.BlockSpec((s_len, LANES), lambda cb, n: (0, cb))
    return _call(
        body, grid=(4, 3), name=name, carry=carry, in_specs=[inb, inb], out_specs=[outb, outb, outb],
        out_shape=[jax.ShapeDtypeStruct((s_len, DL_W), BF16), jax.ShapeDtypeStruct((s_len, DL_W), F32),
                   jax.ShapeDtypeStruct((s_len, DL_W), F32)],
        scratch_shapes=[pltpu.VMEM((3, s_len, LANES), F32), pltpu.VMEM((3, s_len, LANES), F32)], args=(o_p, l_p))


def _dl_bwd_prep(dcat, o, lse, *, name):
    s_len = o.shape[0]

    def body(do_ref, o_ref, l_ref, dop_ref, st_ref, d_scr):
        n = pl.program_id(1)

        @pl.when(n == 0)
        def _():
            r0 = lax.broadcasted_iota(jnp.int32, (LANES, LANES), 0) // HD
            r1 = lax.broadcasted_iota(jnp.int32, (LANES, LANES), 1) // HD
            d_scr[...] = _dot_hilo(do_ref[...] * o_ref[...], (r0 == r1).astype(BF16))

        for k, d in enumerate(DILATIONS):
            @pl.when(n == k)
            def _(d=d):
                _permute_rows(dop_ref, do_ref, d, BF16)
                _permute_rows(st_ref.at[0], d_scr, d, F32)
                _permute_rows(st_ref.at[1], l_ref, d, F32)

    nat = lambda c0: pl.BlockSpec((s_len, LANES), lambda cb, n: (0, c0 + cb))
    return pl.pallas_call(
        body, grid=(4, 3), name=name, in_specs=[nat(4), nat(0), nat(0)],
        out_specs=[pl.BlockSpec((None, s_len, LANES), lambda cb, n: (n, 0, cb)),
                   pl.BlockSpec((2, None, s_len, LANES), lambda cb, n: (0, n, 0, cb))],
        out_shape=[jax.ShapeDtypeStruct((3, s_len, DL_W), BF16), jax.ShapeDtypeStruct((2, 3, s_len, DL_W), F32)],
        scratch_shapes=[pltpu.VMEM((s_len, LANES), F32)], compiler_params=_cp())(dcat, o, lse)


def _dl_bwd(qkv, dop, stats, *, name, carry=None):
    s_len = qkv.shape[2]

    def body(q_ref, k_ref, v_ref, do_ref, st_ref, cur_ref, prev_ref):
        n = pl.program_id(0)
        lane_h = _head_masks()[2]
        band_c, band_p = _dl_band(2 * BLK)

        def per_head(x):
            xr = pltpu.roll(x, HD, 1)
            return jnp.concatenate([jnp.where(lane_h, x, xr), jnp.where(lane_h, xr, x)], axis=0)

        @pl.loop(0, s_len // BLK, step=DL_UNROLL)
        def _(i0):
            blocks = [i0 + t for t in range(DL_UNROLL)]
            rows = [_dl_rows(i) for i in blocks]
            stage1 = []
            for cur, prev in rows:
                qs = _stack_heads(q_ref[cur, :] * 0.125, lane_h)
                dos = _stack_heads(do_ref[cur, :], lane_h)
                kc, kp, vc, vp = k_ref[cur, :], k_ref[prev, :], v_ref[cur, :], v_ref[prev, :]
                stage1.append((qs, dos, _dot_nt(qs, kc), _dot_nt(qs, kp), _dot_nt(dos, vc), _dot_nt(dos, vp)))
            stage2 = []
            for i, (cur, prev), (qs, dos, sc, sp, dpc, dpp) in zip(blocks, rows, stage1):
                lse, delta = per_head(st_ref[1, cur, :]), per_head(st_ref[0, cur, :])
                pc = jnp.where(band_c, jnp.exp(sc - lse), 0.0)
                pp = jnp.where(jnp.logical_and(band_p, jnp.logical_not(_dl_first(s_len, n, i))), jnp.exp(sp - lse), 0.0)
                stage2.append((pc.astype(BF16), pp.astype(BF16), (pc * (dpc - delta)).astype(BF16),
                               (pp * (dpp - delta)).astype(BF16)))
            for (cur, prev), (qs, dos, *_), (pc, pp, dsc, dsp) in zip(rows, stage1, stage2):
                dq = _dot(dsc, k_ref[cur, :]) + _dot(dsp, k_ref[prev, :])
                cur_ref[0, cur, :] = jnp.where(lane_h, dq[:BLK], dq[BLK:]) * 0.125
                cur_ref[1, cur, :] = _dot_tn(dsc, qs)
                cur_ref[2, cur, :] = _dot_tn(pc, dos)
                prev_ref[0, cur, :] = _dot_tn(dsp, qs)
                prev_ref[1, cur, :] = _dot_tn(pp, dos)

    return _call(
        body, grid=(3, 4), name=name, carry=carry,
        in_specs=_dl_in_specs(s_len) + [pl.BlockSpec((None, s_len, LANES), lambda n, hp: (n, 0, hp)),
                                        pl.BlockSpec((2, None, s_len, LANES), lambda n, hp: (0, n, 0, hp))],
        out_specs=[pl.BlockSpec((3, None, s_len, LANES), lambda n, hp: (0, n, 0, hp)),
                   pl.BlockSpec((2, None, s_len, LANES), lambda n, hp: (0, n, 0, hp))],
        out_shape=[jax.ShapeDtypeStruct((3, 3, s_len, DL_W), F32), jax.ShapeDtypeStruct((2, 3, s_len, DL_W), F32)],
        vmem_mb=56, args=(qkv, qkv, qkv, dop, stats))


def _dl_bwd_finish(cur, prev, cos, sin, *, name):
    s_len = cur.shape[2]

    def body(c_ref, p_ref, cos_ref, sin_ref, o_ref, p_scr, u_scr, acc):
        a, n = pl.program_id(0), pl.program_id(2)
        has_prev = jnp.where(a > 0, 1.0, 0.0)
        p_scr[...] = c_ref[...]
        p_scr[0:s_len - BLK, :] += has_prev * p_ref[BLK:, :]
        for k, d in enumerate(DILATIONS):
            @pl.when(n == k)
            def _(k=k, d=d):
                if k == 0:
                    acc[...] = p_scr[...]
                else:
                    _unpermute_rows(u_scr, p_scr, d)
                    acc[...] += u_scr[...]

        @pl.when(n == 2)
        def _():
            dy = acc[...]
            rot = a < 2
            o_ref[...] = (dy * jnp.where(rot, cos_ref[...], 1.0)
                          + _rot_half(dy * jnp.where(rot, sin_ref[...], 0.0))).astype(BF16)

    tab = pl.BlockSpec((s_len, LANES), lambda a, cb, n: (0, 0))
    return pl.pallas_call(
        body, grid=(3, 4, 3), name=name,
        in_specs=[pl.BlockSpec((None, None, s_len, LANES), lambda a, cb, n: (a, n, 0, cb)),
                  pl.BlockSpec((None, None, s_len, LANES), lambda a, cb, n: (jnp.maximum(a - 1, 0), n, 0, cb)),
                  tab, tab],
        out_specs=pl.BlockSpec((s_len, LANES), lambda a, cb, n: (0, 4 * a + cb)),
        out_shape=jax.ShapeDtypeStruct((s_len, 3 * DL_W), BF16),
        scratch_shapes=[pltpu.VMEM((s_len, LANES), F32)] * 3, compiler_params=_cp())(cur, prev, cos, sin)


XA_T = 256


def _xa_probs(q, k):
    s = _dot_nt(q, k) * (X_HD ** -0.5)
    e = jnp.exp(s - jnp.max(s, axis=1, keepdims=True))
    return e / jnp.sum(e, axis=1, keepdims=True)


def _xa_fwd(q, k, v, *, name):
    s_len, d = q.shape
    nm = k.shape[0]

    def body(q_ref, k_ref, v_ref, o_ref):
        for h in range(X_HEADS):
            cs = slice(h * X_HD, (h + 1) * X_HD)
            p = _xa_probs(q_ref[:, cs], k_ref[:, cs])
            o_ref[:, cs] = _dot(p.astype(BF16), v_ref[:, cs]).astype(BF16)

    row = pl.BlockSpec((XA_T, d), lambda i: (i, 0))
    full = pl.BlockSpec((nm, d), lambda i: (0, 0))
    return pl.pallas_call(body, grid=(s_len // XA_T,), name=name, in_specs=[row, full, full], out_specs=row,
                          out_shape=jax.ShapeDtypeStruct((s_len, d), BF16), compiler_params=_cp())(q, k, v)


def _xa_bwd(q, k, v, do, *, name, carry=None):
    s_len, d = q.shape
    nm = k.shape[0]

    def body(q_ref, k_ref, v_ref, do_ref, dq_ref, dk_ref, dv_ref):
        i = pl.program_id(0)
        for h in range(X_HEADS):
            cs = slice(h * X_HD, (h + 1) * X_HD)
            qh, kh, vh, doh = q_ref[:, cs], k_ref[:, cs], v_ref[:, cs], do_ref[:, cs]
            p = _xa_probs(qh, kh)
            dp = _dot_nt(doh, vh)
            ds = (p * (dp - jnp.sum(dp * p, axis=1, keepdims=True)) * (X_HD ** -0.5)).astype(BF16)
            dq_ref[:, cs] = _dot(ds, kh).astype(BF16)
            dkh, dvh = _dot_tn(ds, qh), _dot_tn(p.astype(BF16), doh)

            @pl.when(i == 0)
            def _(cs=cs, dkh=dkh, dvh=dvh):
                dk_ref[:, cs] = dkh
                dv_ref[:, cs] = dvh

            @pl.when(i > 0)
            def _(cs=cs, dkh=dkh, dvh=dvh):
                dk_ref[:, cs] += dkh
                dv_ref[:, cs] += dvh

    row = pl.BlockSpec((XA_T, d), lambda i: (i, 0))
    full = pl.BlockSpec((nm, d), lambda i: (0, 0))
    return _call(
        body, grid=(s_len // XA_T,), name=name, carry=carry, in_specs=[row, full, full, row],
        out_specs=[row, full, full],
        out_shape=[jax.ShapeDtypeStruct((s_len, d), BF16), jax.ShapeDtypeStruct((nm, d), F32),
                   jax.ShapeDtypeStruct((nm, d), F32)], args=(q, k, v, do))


FF_TM, FF_TN, FF_H = 512, 256, 8
GELU_K, GELU_C = 0.7978845608028654, 0.044715


FF_STRIP = 64


def _ff_conv(e_ref, w_ref, b_ref, rows, r0=0):
    return (w_ref[0:1, :] * e_ref[pl.ds(FF_H - 2 + r0, rows), :] + w_ref[1:2, :] * e_ref[pl.ds(FF_H - 1 + r0, rows), :]
            + w_ref[2:3, :] * e_ref[pl.ds(FF_H + r0, rows), :] + b_ref[...])


def _strips(total, size):
    return [(r0, min(size, total - r0)) for r0 in range(0, total, size)]


def _ff_gate_fwd(up, conv_w, conv_b, *, name, carry=None):
    s_len = up.shape[0]
    nj = D_FF // FF_TN

    def body(g_ref, v_ref, gp_ref, vp_ref, wg_ref, wv_ref, bg_ref, bv_ref, o_ref, eg, ev):
        i = pl.program_id(0)
        for e, cur, prev in ((eg, g_ref, gp_ref), (ev, v_ref, vp_ref)):
            e[0:FF_H, :] = jnp.where(i > 0, prev[...], 0.0)
            e[FF_H:, :] = cur[...]
        for r0, rows in _strips(FF_TM, FF_STRIP):
            gate = _ff_conv(eg, wg_ref, bg_ref, rows, r0)
            val = _ff_conv(ev, wv_ref, bv_ref, rows, r0)
            t = jnp.tanh(GELU_K * (gate + GELU_C * gate * gate * gate))
            o_ref[r0:r0 + rows, :] = (0.5 * gate * (1.0 + t) * val).astype(BF16)

    cur = lambda c0: pl.BlockSpec((FF_TM, FF_TN), lambda i, j: (i, c0 + j))
    prev = lambda c0: pl.BlockSpec((FF_H, FF_TN), lambda i, j: (jnp.maximum(i * (FF_TM // FF_H) - 1, 0), c0 + j))
    par = lambda r, c0: pl.BlockSpec((r, FF_TN), lambda i, j: (0, c0 + j))
    return _call(
        body, grid=(s_len // FF_TM, nj), name=name, carry=carry,
        in_specs=[cur(0), cur(nj), prev(0), prev(nj), par(3, 0), par(3, nj), par(1, 0), par(1, nj)],
        out_specs=[cur(0)], out_shape=[jax.ShapeDtypeStruct((s_len, D_FF), BF16)],
        scratch_shapes=[pltpu.VMEM((FF_TM + FF_H, FF_TN), F32)] * 2,
        args=(up, up, up, up, conv_w, conv_w, conv_b, conv_b))


def _ff_gate_bwd(up, dact, conv_w, conv_b, *, name, carry=None):
    s_len = up.shape[0]
    nj = D_FF // FF_TN
    last = s_len // FF_TM - 1
    ext = FF_TM + FF_H

    def body(g_ref, v_ref, gp_ref, vp_ref, gn_ref, vn_ref, da_ref, dan_ref, wg_ref, wv_ref, bg_ref, bv_ref,
             dg_ref, dv_ref, dw_ref, db_ref, eg, ev, sg, sv):
        i = pl.program_id(1)
        for e, cur, prev, nxt in ((eg, g_ref, gp_ref, gn_ref), (ev, v_ref, vp_ref, vn_ref)):
            e[0:FF_H, :] = jnp.where(i > 0, prev[...], 0.0)
            e[FF_H:FF_H + FF_TM, :] = cur[...]
            e[FF_H + FF_TM:, :] = nxt[...]
        for r0, rows in _strips(ext, FF_STRIP):
            gate = _ff_conv(eg, wg_ref, bg_ref, rows, r0)
            val = _ff_conv(ev, wv_ref, bv_ref, rows, r0)
            dact = da_ref[r0:r0 + rows, :] if r0 < FF_TM else jnp.where(i < last, dan_ref[...], 0.0)
            t = jnp.tanh(GELU_K * (gate + GELU_C * gate * gate * gate))
            half = 0.5 * (1.0 + t)
            dgelu = half + 0.5 * gate * (1.0 - t * t) * GELU_K * (1.0 + 3.0 * GELU_C * gate * gate)
            sg[r0:r0 + rows, :] = dact * val * dgelu
            sv[r0:r0 + rows, :] = dact * (gate * half)
        for part, (s, e, w_ref, out) in enumerate(((sg, eg, wg_ref, dg_ref), (sv, ev, wv_ref, dv_ref))):
            taps, bias = [jnp.zeros((8, FF_TN), F32)] * 3, jnp.zeros((8, FF_TN), F32)
            for r0, rows in _strips(FF_TM, FF_STRIP):
                d0 = s[pl.ds(r0, rows), :]
                out[r0:r0 + rows, :] = (w_ref[2:3, :] * d0 + w_ref[1:2, :] * s[pl.ds(r0 + 1, rows), :]
                                        + w_ref[0:1, :] * s[pl.ds(r0 + 2, rows), :]).astype(BF16)
                taps = [taps[k] + _rowsum8(d0 * e[pl.ds(FF_H - 2 + k + r0, rows), :]) for k in range(3)]
                bias = bias + _rowsum8(d0)
            _acc_out(dw_ref.at[part], i, jnp.concatenate(taps, axis=0))
            _acc_out(db_ref.at[part], i, bias)

    cur = lambda c0: pl.BlockSpec((FF_TM, FF_TN), lambda j, i: (i, c0 + j))
    prev = lambda c0: pl.BlockSpec((FF_H, FF_TN), lambda j, i: (jnp.maximum(i * (FF_TM // FF_H) - 1, 0), c0 + j))
    nxt = lambda c0: pl.BlockSpec(
        (FF_H, FF_TN), lambda j, i: (jnp.minimum((i + 1) * (FF_TM // FF_H), s_len // FF_H - 1), c0 + j))
    par = lambda r, c0: pl.BlockSpec((r, FF_TN), lambda j, i: (0, c0 + j))
    return _call(
        body, grid=(nj, s_len // FF_TM), name=name, carry=carry,
        in_specs=[cur(0), cur(nj), prev(0), prev(nj), nxt(0), nxt(nj), cur(0), nxt(0),
                  par(3, 0), par(3, nj), par(1, 0), par(1, nj)],
        out_specs=[cur(0), cur(0), pl.BlockSpec((2, 24, FF_TN), lambda j, i: (0, 0, j)),
                   pl.BlockSpec((2, 8, FF_TN), lambda j, i: (0, 0, j))],
        out_shape=[jax.ShapeDtypeStruct((s_len, D_FF), BF16), jax.ShapeDtypeStruct((s_len, D_FF), BF16),
                   jax.ShapeDtypeStruct((2, 24, D_FF), F32), jax.ShapeDtypeStruct((2, 8, D_FF), F32)],
        scratch_shapes=[pltpu.VMEM((FF_TM + 2 * FF_H, FF_TN), F32)] * 2 + [pltpu.VMEM((ext, FF_TN), F32)] * 2,
        args=(up, up, up, up, up, up, dact, dact, conv_w, conv_w, conv_b, conv_b))


def _place():
    x, y, c = lax.axis_index("x"), lax.axis_index("y"), lax.axis_index("c")
    return x, y, c, [(1 - x, y), (x, 1 - y), (1 - x, 1 - y)]


def _remote(src, dst, send_sem, recv_sem, dev):
    return pltpu.make_async_remote_copy(src_ref=src, dst_ref=dst, send_sem=send_sem, recv_sem=recv_sem,
                                        device_id=dev, device_id_type=MESH)


_ANY = pl.BlockSpec(memory_space=pl.ANY)


N_SEMS = 8
SEM_BASE_2 = 4


class _Exchange:
    def __init__(self, operands, out_shapes, start, wait, aliases=None):
        self.operands, self.out_shapes, self.start, self.wait = list(operands), list(out_shapes), start, wait
        self.aliases = aliases or {}


def _sem_scratch():
    return [pltpu.SemaphoreType.DMA((N_SEMS,)), pltpu.SemaphoreType.DMA((N_SEMS,)), pltpu.SemaphoreType.DMA]


def _run_exchange(ex, *, name):
    k, n = len(ex.operands), len(ex.out_shapes)

    def body(*refs):
        ins, outs, sems = refs[:k], refs[k:k + n], refs[k + n:]
        ex.start(ins, outs, *sems)
        ex.wait(ins, outs, *sems)

    return pl.pallas_call(body, name=name, in_specs=[_ANY] * k, out_specs=[_ANY] * n, out_shape=ex.out_shapes,
                          scratch_shapes=_sem_scratch(), input_output_aliases=ex.aliases,
                          compiler_params=_cp(16))(*ex.operands)


def _call(body, *, grid, in_specs, out_specs, out_shape, args, name, scratch_shapes=(), vmem_mb=48, carry=None):
    scratch_shapes = list(scratch_shapes)
    if carry is None:
        return pl.pallas_call(body, grid=grid, name=name, in_specs=in_specs, out_specs=out_specs, out_shape=out_shape,
                              scratch_shapes=scratch_shapes, compiler_params=_cp(vmem_mb))(*args)
    n_in, n_out, n_scr = len(in_specs), len(out_shape), len(scratch_shapes)
    k_in, k_out = len(carry.operands), len(carry.out_shapes)

    def wrapped(*refs):
        ins, refs = refs[:n_in], refs[n_in:]
        cin, refs = refs[:k_in], refs[k_in:]
        outs, refs = refs[:n_out], refs[n_out:]
        cout, refs = refs[:k_out], refs[k_out:]
        scratch, sems = refs[:n_scr], refs[n_scr:]
        ids = [pl.program_id(a) for a in range(len(grid))]
        first = functools.reduce(jnp.logical_and, [i == 0 for i in ids])
        last = functools.reduce(jnp.logical_and, [i == g - 1 for i, g in zip(ids, grid)])

        @pl.when(first)
        def _():
            carry.start(cin, cout, *sems)

        body(*ins, *outs, *scratch)

        @pl.when(last)
        def _():
            carry.wait(cin, cout, *sems)

    aliases = {n_in + i: n_out + o for i, o in carry.aliases.items()}
    return pl.pallas_call(
        wrapped, grid=grid, name=name, in_specs=list(in_specs) + [_ANY] * k_in,
        out_specs=list(out_specs) + [_ANY] * k_out, out_shape=list(out_shape) + carry.out_shapes,
        scratch_shapes=scratch_shapes + _sem_scratch(), input_output_aliases=aliases,
        compiler_params=_cp(vmem_mb))(*args, *carry.operands)


def _half_rows(ref_rows, c):
    half = ref_rows // 2
    return pl.ds(c * half, half)


def _ex_join(a, b):
    ka, na = len(a.operands), len(a.out_shapes)

    def start(ins, outs, *sems):
        a.start(ins[:ka], outs[:na], *sems)
        b.start(ins[ka:], outs[na:], *sems)

    def wait(ins, outs, *sems):
        a.wait(ins[:ka], outs[:na], *sems)
        b.wait(ins[ka:], outs[na:], *sems)

    aliases = dict(a.aliases)
    aliases.update({ka + i: na + o for i, o in b.aliases.items()})
    return _Exchange(a.operands + b.operands, a.out_shapes + b.out_shapes, start, wait, aliases)


def _ex_gather(pack, r0, rl, base=0):
    def copies(ins, outs, send, recv):
        x, y, c, chips = _place()
        rows = _half_rows(rl, c)
        src = ins[0].at[pl.ds(r0 + c * (rl // 2), rl // 2)]
        sends = [_remote(src, outs[0].at[2 * x + y, rows], send.at[base + k], recv.at[base + k], (px, py, c))
                 for k, (px, py) in enumerate(chips)]
        lands = [_remote(src, outs[0].at[2 * px + py, rows], send.at[base + k], recv.at[base + k], (px, py, c))
                 for k, (px, py) in enumerate(chips)]
        return sends, lands

    def mine(ins, outs, local):
        x, y, _, _ = _place()
        return pltpu.make_async_copy(ins[0].at[pl.ds(r0, rl)], outs[0].at[2 * x + y], local)

    def start(ins, outs, send, recv, local):
        mine(ins, outs, local).start()
        for cp in copies(ins, outs, send, recv)[0]:
            cp.start()

    def wait(ins, outs, send, recv, local):
        sends, lands = copies(ins, outs, send, recv)
        for cp in lands:
            cp.wait_recv()
        for cp in sends:
            cp.wait_send()
        mine(ins, outs, local).wait()

    return _Exchange([pack], [jax.ShapeDtypeStruct((4, rl, pack.shape[1]), pack.dtype)], start, wait)


def _ex_gather_forward(g, base=0):
    rl = g.shape[1]

    def copies(outs, send, recv):
        x, y, c, chips = _place()
        slabs = [(outs[0].at[2 * px + py, _half_rows(rl, c)], outs[0].at[2 * px + py, _half_rows(rl, 1 - c)])
                 for px, py in chips]
        sends = [_remote(a, a, send.at[base + k], recv.at[base + k], (x, y, 1 - c)) for k, (a, _) in enumerate(slabs)]
        lands = [_remote(b, b, send.at[base + k], recv.at[base + k], (x, y, 1 - c)) for k, (_, b) in enumerate(slabs)]
        return sends, lands

    def start(ins, outs, send, recv, local):
        for cp in copies(outs, send, recv)[0]:
            cp.start()

    def wait(ins, outs, send, recv, local):
        sends, lands = copies(outs, send, recv)
        for cp in lands:
            cp.wait_recv()
        for cp in sends:
            cp.wait_send()

    return _Exchange([g], [jax.ShapeDtypeStruct(g.shape, g.dtype)], start, wait, aliases={0: 0})


def _ex_swap_halves(gw, base=0):
    nb, rl, d = gw.shape

    def copies(ins, outs, send, recv):
        x, y, c, _ = _place()
        return [_remote(ins[0].at[j, _half_rows(rl, 1 - c)], outs[0].at[j], send.at[base + j], recv.at[base + j],
                        (x, y, 1 - c)) for j in range(nb)]

    def start(ins, outs, send, recv, local):
        for cp in copies(ins, outs, send, recv):
            cp.start()

    def wait(ins, outs, send, recv, local):
        for cp in copies(ins, outs, send, recv):
            cp.wait()

    return _Exchange([gw], [jax.ShapeDtypeStruct((nb, rl // 2, d), gw.dtype)], start, wait)


def _chip_sum(gw, got, c_arr, *, name):
    nchip, half, d = got.shape
    tr = _tile(half, 512)

    def body(c_ref, a_ref, b_ref, o32_ref, o16_ref):
        s = a_ref[...] + b_ref[...]
        o32_ref[...] = s
        o16_ref[...] = s.astype(BF16)

    blk = pl.BlockSpec((None, tr, d), lambda j, i, c_ref: (j, i, 0))
    return pl.pallas_call(
        body, name=name,
        grid_spec=pltpu.PrefetchScalarGridSpec(
            num_scalar_prefetch=1, grid=(nchip, half // tr),
            in_specs=[pl.BlockSpec((None, tr, d), lambda j, i, c_ref: (j, c_ref[0] * (half // tr) + i, 0)), blk],
            out_specs=[blk, blk]),
        out_shape=[jax.ShapeDtypeStruct((nchip, half, d), F32), jax.ShapeDtypeStruct((nchip, half, d), BF16)],
        compiler_params=_cp())(c_arr, gw, got)


def _ex_scatter(s16, base=0):
    def copies(ins, outs, send, recv):
        x, y, c, chips = _place()
        return [_remote(ins[0].at[2 * px + py], outs[0].at[k], send.at[base + k], recv.at[base + k], (px, py, c))
                for k, (px, py) in enumerate(chips)]

    def start(ins, outs, send, recv, local):
        for cp in copies(ins, outs, send, recv):
            cp.start()

    def wait(ins, outs, send, recv, local):
        for cp in copies(ins, outs, send, recv):
            cp.wait()

    return _Exchange([s16], [jax.ShapeDtypeStruct((3,) + s16.shape[1:], s16.dtype)], start, wait)


def _mesh_sum(s32, got, j_arr, *, name):
    _, rl, d = s32.shape
    tr = _tile(rl, 512)

    def body(j_ref, a_ref, b_ref, o_ref):
        o_ref[...] = ((a_ref[...] + b_ref[0].astype(F32)) + b_ref[1].astype(F32)) + b_ref[2].astype(F32)

    return pl.pallas_call(
        body, name=name,
        grid_spec=pltpu.PrefetchScalarGridSpec(
            num_scalar_prefetch=1, grid=(rl // tr,),
            in_specs=[pl.BlockSpec((None, tr, d), lambda i, j_ref: (j_ref[0], i, 0)),
                      pl.BlockSpec((3, tr, d), lambda i, j_ref: (0, i, 0))],
            out_specs=pl.BlockSpec((tr, d), lambda i, j_ref: (i, 0))),
        out_shape=jax.ShapeDtypeStruct((rl, d), F32), compiler_params=_cp())(j_arr, s32, got)


def _ex_share_halves(ghalf):
    half, d = ghalf.shape

    def copies(ins, outs, send, recv, local):
        x, y, c, _ = _place()
        there = outs[0].at[_half_rows(2 * half, c)]
        back = outs[0].at[_half_rows(2 * half, 1 - c)]
        return (_remote(ins[0], there, send.at[0], recv.at[0], (x, y, 1 - c)),
                _remote(ins[0], back, send.at[0], recv.at[0], (x, y, 1 - c)), pltpu.make_async_copy(ins[0], there, local))

    def start(ins, outs, send, recv, local):
        out, _, mine = copies(ins, outs, send, recv, local)
        mine.start()
        out.start()

    def wait(ins, outs, send, recv, local):
        out, back, mine = copies(ins, outs, send, recv, local)
        back.wait_recv()
        out.wait_send()
        mine.wait()

    return _Exchange([ghalf], [jax.ShapeDtypeStruct((2 * half, d), ghalf.dtype)], start, wait)


class _ReduceScatter:
    def __init__(self, gw, c_arr, j_arr, tag):
        self.gw, self.c_arr, self.j_arr, self.tag = gw, c_arr, j_arr, tag

    def swap(self, base=0):
        return _ex_swap_halves(self.gw, base)

    def after_swap(self, got, base=0):
        self.s32, s16 = _chip_sum(self.gw, got, self.c_arr, name=f"rs_chip_sum{self.tag}")
        return _ex_scatter(s16, base)

    def after_scatter(self, got16):
        ghalf = _mesh_sum(self.s32, got16, self.j_arr, name=f"rs_mesh_sum{self.tag}")
        return _run_exchange(_ex_share_halves(ghalf), name=f"rs_share{self.tag}")[0]

    def run(self):
        got, = _run_exchange(self.swap(), name=f"rs_swap{self.tag}")
        got16, = _run_exchange(self.after_swap(got), name=f"rs_scatter{self.tag}")
        return self.after_scatter(got16)


def _all_reduce_small(vec, *, name):
    rows, d = vec.shape

    def body(x_ref, o_ref, gat, send_sems, recv_sems, local_sem):
        x, y, c, chips = _place()
        me, sibling = (x, y, c), (x, y, 1 - c)

        def slot(px, py, pc):
            return gat.at[4 * px + 2 * py + pc]

        def copy(k, block, to, src=None):
            return _remote(slot(*block) if src is None else src, slot(*block), send_sems.at[k], recv_sems.at[k], to)

        mine = pltpu.make_async_copy(x_ref, slot(*me), local_sem)
        mine.start()
        first = [copy(0, me, sibling, src=x_ref)]
        first += [copy(1 + j, me, (*chip, c), src=x_ref) for j, chip in enumerate(chips)]
        for cp in first:
            cp.start()
        passed = [copy(4 + j, (*chip, c), sibling) for j, chip in enumerate(chips)]
        for j, chip in enumerate(chips):
            copy(1 + j, (*chip, c), me).wait_recv()
            passed[j].start()
        copy(0, sibling, me).wait_recv()
        for j, chip in enumerate(chips):
            copy(4 + j, (*chip, 1 - c), me).wait_recv()
        for cp in first + passed:
            cp.wait_send()
        mine.wait()
        acc = gat[0]
        for dev in range(1, 8):
            acc = acc + gat[dev]
        o_ref[...] = acc

    vm = pl.BlockSpec(memory_space=pltpu.VMEM)
    return pl.pallas_call(
        body, name=name, in_specs=[vm], out_specs=vm, out_shape=jax.ShapeDtypeStruct((rows, d), F32),
        scratch_shapes=[pltpu.VMEM((8, rows, d), F32), pltpu.SemaphoreType.DMA((7,)), pltpu.SemaphoreType.DMA((7,)),
                        pltpu.SemaphoreType.DMA],
        compiler_params=_cp(32))(vec)


COL_SHARDED = ("w_in", "ffn_w_up")


def _to_pack_rows(name, shard):
    return shard.reshape(-1, D_MODEL)


def _full_from_blocks(name, blocks):
    rows = blocks.shape[1]
    if name in COL_SHARDED:
        return blocks.reshape(4, D_MODEL, rows).transpose(1, 0, 2).reshape(D_MODEL, 4 * rows)
    return blocks.reshape(4 * rows, D_MODEL)


def _blocks_from_full(name, full):
    if name in COL_SHARDED:
        cols = full.shape[1] // 4
        return full.reshape(D_MODEL, 4, cols).transpose(1, 0, 2).reshape(4, cols, D_MODEL)
    return full.reshape(4, full.shape[0] // 4, D_MODEL)


def _row(v):
    return v.reshape(1, -1)


SMALL = (("mix_norm_pre", (1024,), None), ("cv_w", (31, 256), 1), ("cv_b", (256,), None), ("cv_ln_g", (256,), None),
         ("cv_ln_b", (256,), None), ("cv_pw_w", (256, 256), 0), ("cv_pw_b", (256,), None),
         ("mix_norm_post", (1024,), None), ("x_norm_pre", (1024,), None), ("mem_norm", (1024,), None),
         ("x_norm_post", (1024,), None), ("ffn_norm_pre", (1024,), None), ("ffn_conv_w", (3, 5632), 1),
         ("ffn_conv_b", (5632,), None), ("ffn_norm_post", (1024,), None))
BIG = tuple(n for n, _ in PACK_ROWS)
WEIGHT_ORDER = ("mix_norm_pre", "w_in", "cv_w", "cv_b", "cv_ln_g", "cv_ln_b", "cv_pw_w", "cv_pw_b", "w_out",
                "mix_norm_post", "x_norm_pre", "mem_norm", "x_wq", "x_wk", "x_wv", "x_wo", "x_norm_post",
                "ffn_norm_pre", "ffn_w_up", "ffn_conv_w", "ffn_conv_b", "ffn_w_down", "ffn_norm_post")


def _flat_rows(parts):
    v = jnp.concatenate([p.reshape(-1) for p in parts])
    rows = -(-v.shape[0] // (8 * D_MODEL)) * 8
    return jnp.pad(v, (0, rows * D_MODEL - v.shape[0])).reshape(rows, D_MODEL)


def _small_to_rows(blocks):
    v = jnp.concatenate([b.reshape(-1) for b in blocks])
    return jnp.pad(v, (0, SMALL_ROWS * D_MODEL - v.shape[0])).reshape(SMALL_ROWS, D_MODEL)


def _small_from_rows(rows):
    flat, out, off = rows.reshape(-1), [], 0
    for _, shape, _ in SHARDED_SMALL:
        size = int(np.prod(shape))
        out.append(flat[off:off + size].reshape(shape))
        off += size
    return out


def _chip_block(full, j, shape, axis):
    return lax.slice_in_dim(full, j * shape[axis], (j + 1) * shape[axis], axis=axis)


REST_GROUP = ("w_in", "w_out")
XA_GROUP = ("x_wq", "x_wk", "x_wv", "x_wo")
FFN_GROUP = ("ffn_w_up", "ffn_w_down")


class _Weights:
    FIRST = (0, 768)
    OWN = ((768, 1024), (1792, 1664), (3456, 704))
    NEXT = ((0, 1024), (1024, 1024), (2048, 1408), (3456, 704))
    SLOTS = ("mix_in", "sb_fwd", "dl_fwd", "dl_mix", "ffn_up", "ffn_gate", "ffn_down")

    def __init__(self, packs):
        self.packs, self.pieces, self.landed, self.plan = packs, {}, None, {}
        for slot, piece in zip(self.SLOTS[:3], self.OWN):
            self.plan[(0, slot)] = (0,) + piece
        for l in range(len(packs) - 1):
            for slot, piece in zip(self.SLOTS[3:], self.NEXT):
                self.plan[(l, slot)] = (l + 1,) + piece
        first = _run_exchange(_ex_gather(packs[0], *self.FIRST), name="gather_first")[0]
        self.pieces[(0,) + self.FIRST] = _run_exchange(_ex_gather_forward(first), name="gather_first_forward")[0]

    def ride(self, layer, slot, call):
        start, todo, ex = self.plan.get((layer, slot)), [], None
        if start is not None:
            ex = _ex_gather(self.packs[start[0]], start[1], start[2])
            todo.append(("landed", start))
        if self.landed is not None:
            key, buf = self.landed
            forward = _ex_gather_forward(buf, SEM_BASE_2 if ex is not None else 0)
            ex = forward if ex is None else _ex_join(ex, forward)
            todo.append(("piece", key))
            self.landed = None
        outs = list(call(carry=ex))
        n = len(outs) - len(todo)
        for (kind, key), buf in zip(todo, outs[n:]):
            if kind == "landed":
                self.landed = (key, buf)
            else:
                self.pieces[key] = buf
        return outs[:n]

    def rows_of(self, layer, name):
        off = 0
        for n, rows in WEIGHT_PACK:
            if n == name:
                break
            off += rows
        for (l, r0, nrows), buf in self.pieces.items():
            if l == layer and r0 <= off < r0 + nrows:
                return buf[:, off - r0:off - r0 + rows, :]
        raise KeyError(f"{name} of layer {layer} is not gathered yet")

    def weight(self, layer, name):
        return _full_from_blocks(name, self.rows_of(layer, name))

    def small(self, layer):
        planes = lax.bitcast_convert_type(self.rows_of(layer, "small").astype(jnp.bfloat16), jnp.uint16)
        planes = planes.astype(jnp.uint32)
        bits = (planes[:, :SMALL_ROWS] << 16) | planes[:, SMALL_ROWS:]
        per_chip = [_small_from_rows(r) for r in lax.bitcast_convert_type(bits, F32)]
        return {n: jnp.concatenate([blocks[k] for blocks in per_chip], axis=axis)
                for k, (n, _, axis) in enumerate(SHARDED_SMALL)}


class _Params:
    def __init__(self, weights, layer, small):
        self.weights, self.layer, self.small, self.cache = weights, layer, small, {}

    def __getitem__(self, name):
        if name in self.small:
            return self.small[name]
        if name not in self.cache:
            if name in [n for n, _, _ in SHARDED_SMALL]:
                self.cache.update(self.weights.small(self.layer))
            else:
                self.cache[name] = self.weights.weight(self.layer, name)
        return self.cache[name]


def _layer_fwd(h0, mem, p, cos, sin, tag, ride):
    sv = {"h0": h0}
    n1, u = ride("mix_in", functools.partial(_rms_mm, h0, _row(p["mix_norm_pre"]), p["w_in"], tm=1024, tn=1408,
                                             out_dtype=F32, name=f"mix_in{tag}"))
    a_out, = ride("sb_fwd", functools.partial(_sb_fwd, u, name=f"sb_fwd{tag}"))
    b_out, c = _cv_fwd(u, p["cv_w"], _row(p["cv_b"]), _row(p["cv_ln_g"]), _row(p["cv_ln_b"]),
                       p["cv_pw_w"].astype(BF16), _row(p["cv_pw_b"]), name=f"cv_fwd{tag}")
    qkv = _rope_perm(u, cos, sin, name=f"rope_perm{tag}")
    o_p, l_p = ride("dl_fwd", functools.partial(_dl_fwd, qkv, name=f"dl_fwd{tag}"))
    c_out, o_dl, lse = ride("dl_mix", functools.partial(_dl_mix, o_p, l_p, name=f"dl_mix{tag}"))
    cat = jnp.concatenate([a_out, b_out, c_out], axis=1)
    y1, h1 = _mm_post(cat, p["w_out"], h0, _row(p["mix_norm_post"]), tm=512, name=f"mix_out{tag}")
    sv.update(n1=n1, u=u, c=c, qkv=qkv, o_dl=o_dl, lse=lse, cat=cat, y1=y1, h1=h1)

    n2, q = _rms_mm(h1, _row(p["x_norm_pre"]), p["x_wq"], tm=512, tn=1024, out_dtype=BF16, name=f"xa_q{tag}")
    wkv = jnp.concatenate([p["x_wk"], p["x_wv"]], axis=1)
    mem_n, kv = _rms_mm(mem, _row(p["mem_norm"]), wkv, tm=mem.shape[0], tn=1024, out_dtype=BF16, name=f"xa_kv{tag}")
    k, v = kv[:, :D_MODEL], kv[:, D_MODEL:]
    o_x = _xa_fwd(q, k, v, name=f"xa_fwd{tag}")
    y2, h2 = _mm_post(o_x, p["x_wo"], h1, _row(p["x_norm_post"]), tm=512, name=f"xa_out{tag}")
    sv.update(n2=n2, q=q, mem_n=mem_n, k=k, v=v, o_x=o_x, y2=y2, h2=h2, wkv=wkv)

    n3, up = ride("ffn_up", functools.partial(_rms_mm, h2, _row(p["ffn_norm_pre"]), p["ffn_w_up"], tm=1024, tn=1408,
                                              out_dtype=F32, name=f"ffn_up{tag}"))
    act, = ride("ffn_gate", functools.partial(_ff_gate_fwd, up, p["ffn_conv_w"], _row(p["ffn_conv_b"]),
                                              name=f"ffn_gate{tag}"))
    y3, h3 = ride("ffn_down", functools.partial(_mm_post, act, p["ffn_w_down"], h2, _row(p["ffn_norm_post"]), tm=512,
                                                name=f"ffn_down{tag}"))
    sv.update(n3=n3, up=up, act=act, y3=y3)
    return h3, sv


def _layer_bwd(dh3, mem, p, sv, cos, sin, tag, riding, new_rs):
    g = {}
    s8 = lambda part: part.sum(axis=0)
    rode = None

    dy3, dgp = _rms_bwd(sv["y3"], _row(p["ffn_norm_post"]), dh3, None, out_dtype=BF16, tm=512, name=f"ffn_post_b{tag}")
    g["ffn_norm_post"] = s8(dgp)
    dact = _mm_nt(dy3, p["ffn_w_down"], tm=512, tn=1408, out_dtype=F32, name=f"ffn_down_bx{tag}")
    g["ffn_w_down"] = _mm_tn(sv["act"], dy3, tk=1408, tn=1024, tm=2048, name=f"ffn_down_bw{tag}")
    dgu, dvu, dcw, dcb, *got = _ff_gate_bwd(sv["up"], dact, p["ffn_conv_w"], _row(p["ffn_conv_b"]),
                                            name=f"ffn_gate_b{tag}", carry=riding.swap() if riding else None)
    scatter = riding.after_swap(got[0]) if riding else None
    g["ffn_conv_w"] = jnp.concatenate([dcw[0], dcw[1]], axis=1).reshape(3, 8, 2 * D_FF).sum(axis=1)
    g["ffn_conv_b"] = jnp.concatenate([dcb[0], dcb[1]], axis=1).sum(axis=0)
    dup = jnp.concatenate([dgu, dvu], axis=1)
    dn3 = _mm_nt(dup, p["ffn_w_up"], tm=256, tn=512, out_dtype=F32, name=f"ffn_up_bx{tag}")
    g["ffn_w_up"] = _mm_tn(sv["n3"], dup, tk=512, tn=1408, tm=2048, name=f"ffn_up_bw{tag}")
    ffn_rs = new_rs(FFN_GROUP, g, f"{tag}_ffn")
    dh2, dgp = _rms_bwd(sv["h2"], _row(p["ffn_norm_pre"]), dn3, dh3, out_dtype=F32, tm=512, name=f"ffn_pre_b{tag}")
    g["ffn_norm_pre"] = s8(dgp)

    dy2, dgp = _rms_bwd(sv["y2"], _row(p["x_norm_post"]), dh2, None, out_dtype=BF16, tm=512, name=f"xa_post_b{tag}")
    g["x_norm_post"] = s8(dgp)
    do_x = _mm_nt(dy2, p["x_wo"], tm=512, tn=1024, out_dtype=BF16, name=f"xa_out_bx{tag}")
    g["x_wo"] = _mm_tn(sv["o_x"], dy2, tk=512, tn=1024, tm=2048, name=f"xa_out_bw{tag}")
    dq, dk, dv, got = _xa_bwd(sv["q"], sv["k"], sv["v"], do_x, name=f"xa_bwd{tag}", carry=ffn_rs.swap())
    ffn_scatter = ffn_rs.after_swap(got)
    dn2 = _mm_nt(dq, p["x_wq"], tm=512, tn=1024, out_dtype=F32, name=f"xa_q_bx{tag}")
    g["x_wq"] = _mm_tn(sv["n2"], dq, tk=512, tn=1024, tm=2048, name=f"xa_q_bw{tag}")
    dkv = jnp.concatenate([dk, dv], axis=1).astype(BF16)
    nm = mem.shape[0]
    dmem_n = _mm_nt(dkv, sv["wkv"], tm=nm, tn=1024, out_dtype=F32, name=f"xa_kv_bx{tag}")
    dwkv = _mm_tn(sv["mem_n"], dkv, tk=512, tn=2048, tm=nm, name=f"xa_kv_bw{tag}")
    g["x_wk"], g["x_wv"] = dwkv[:, :D_MODEL], dwkv[:, D_MODEL:]
    _, dgp = _rms_bwd(mem, _row(p["mem_norm"]), dmem_n, None, out_dtype=BF16, tm=nm, name=f"xa_mem_b{tag}")
    g["mem_norm"] = s8(dgp)
    xa_rs = new_rs(XA_GROUP, g, f"{tag}_xa")
    dh1, dgp, got = _rms_bwd(sv["h1"], _row(p["x_norm_pre"]), dn2, dh2, out_dtype=F32, tm=512, name=f"xa_pre_b{tag}",
                             carry=xa_rs.swap())
    xa_scatter = xa_rs.after_swap(got, SEM_BASE_2 if riding else 0)
    g["x_norm_pre"] = s8(dgp)

    dy1, dgp = _rms_bwd(sv["y1"], _row(p["mix_norm_post"]), dh1, None, out_dtype=BF16, tm=512, name=f"mix_post_b{tag}")
    g["mix_norm_post"] = s8(dgp)
    dcat = _mm_nt(dy1, p["w_out"], tm=512, tn=1024, out_dtype=F32, name=f"mix_out_bx{tag}")
    g["w_out"] = _mm_tn(sv["cat"], dy1, tk=512, tn=1024, tm=2048, name=f"mix_out_bw{tag}")
    u = sv["u"]
    dq_sb, dk_sb, dv_sb, *got = _sb_bwd(u, dcat, name=f"sb_bwd{tag}",
                                        carry=_ex_join(scatter, xa_scatter) if riding else xa_scatter)
    if riding:
        rode = riding.after_scatter(got[0])
    xa_rows = xa_rs.after_scatter(got[-1])
    pw_b16 = p["cv_pw_w"].astype(BF16)
    dc, dpw, vec = _cv_bwd_local(sv["c"], dcat, _row(p["cv_ln_g"]), _row(p["cv_ln_b"]), pw_b16, name=f"cv_bwd_a{tag}")
    g["cv_pw_w"] = dpw
    vec = vec.reshape(3, 8, CV_W).sum(axis=1)
    g["cv_pw_b"], g["cv_ln_g"], g["cv_ln_b"] = vec[0], vec[1], vec[2]
    du_cv, dcw, dcb = _cv_bwd_conv(u, dc, p["cv_w"], name=f"cv_bwd_b{tag}")
    g["cv_w"] = dcw.reshape(CV_K, 8, CV_W).sum(axis=1)
    g["cv_b"] = dcb.sum(axis=0)
    dop, stats = _dl_bwd_prep(dcat, sv["o_dl"], sv["lse"], name=f"dl_prep_b{tag}")
    cur, prev, got = _dl_bwd(sv["qkv"], dop, stats, name=f"dl_bwd{tag}", carry=ffn_scatter)
    ffn_rows = ffn_rs.after_scatter(got)
    du_dl = _dl_bwd_finish(cur, prev, cos, sin, name=f"dl_fin_b{tag}")
    du = jnp.concatenate([dq_sb.astype(BF16), dk_sb.astype(BF16), dv_sb.astype(BF16), du_cv, du_dl], axis=1)
    dn1 = _mm_nt(du, p["w_in"], tm=512, tn=512, out_dtype=F32, name=f"mix_in_bx{tag}")
    g["w_in"] = _mm_tn(sv["n1"], du, tk=512, tn=1408, tm=2048, name=f"mix_in_bw{tag}")
    dh0, dgp = _rms_bwd(sv["h0"], _row(p["mix_norm_pre"]), dn1, dh1, out_dtype=F32, tm=512, name=f"mix_pre_b{tag}")
    g["mix_norm_pre"] = s8(dgp)
    return dh0, g, (xa_rows, ffn_rows), rode


def _step(x, mem, positions, loss_target, w, m, v):
    depth = w["w_in"].shape[0]
    xi, yi, ci = lax.axis_index("x"), lax.axis_index("y"), lax.axis_index("c")
    chip = 2 * xi + yi
    h = x[0]
    mem0 = mem[0]
    s_len = h.shape[0]

    def pack_rows(n, l):
        if n == "small":
            bits = lax.bitcast_convert_type(_small_to_rows([w[name][l] for name, _, _ in SHARDED_SMALL]), jnp.uint32)
            planes = [(bits >> 16).astype(jnp.uint16), (bits & 0xFFFF).astype(jnp.uint16)]
            return jnp.concatenate([lax.bitcast_convert_type(p, jnp.bfloat16) for p in planes], axis=0)
        return _to_pack_rows(n, w[n][l]).astype(BF16)

    packs = [jnp.concatenate([pack_rows(n, l) for n, _ in WEIGHT_PACK], axis=0) for l in range(depth)]
    weights = _Weights(packs)
    params = [_Params(weights, l, {n: w[n][l] for n, _, axis in SMALL if axis is None}) for l in range(depth)]

    inv_freq = ROPE_THETA ** (-jnp.arange(HD // 2, dtype=F32) / (HD // 2))
    cos, sin = _rope_tables(positions.reshape(s_len, 1), jnp.tile(inv_freq, 4).reshape(1, LANES), name="rope_tables")

    saved = []
    for l in range(depth):
        h, sv = _layer_fwd(h, mem0, params[l], cos, sin, f"_l{l}", functools.partial(weights.ride, l))
        saved.append(sv)
    dh, sq = _loss_grad(h, loss_target[0], tm=512, name="loss_grad")
    loss = lax.psum(0.5 * jnp.sum(sq) / D_MODEL, ("x", "y", "c"))

    c_arr, j_arr = jnp.reshape(ci, (1,)).astype(jnp.int32), jnp.reshape(chip, (1,)).astype(jnp.int32)

    def new_rs(names, g, tag):
        blocks = [_blocks_from_full(n, g[n]) for n in names]
        if names is REST_GROUP:
            blocks.append(jnp.stack([_small_to_rows([_chip_block(g[n], j, shape, axis) for n, shape, axis in SHARDED_SMALL])
                                     for j in range(4)]))
        return _ReduceScatter(jnp.concatenate(blocks, axis=1), c_arr, j_arr, tag)

    grads, later_rows, rest_rows, pending = [None] * depth, [None] * depth, [None] * depth, None
    for l in reversed(range(depth)):
        dh, grads[l], later_rows[l], rode = _layer_bwd(dh, mem0, params[l], saved[l], cos, sin, f"_l{l}", pending, new_rs)
        if pending is not None:
            rest_rows[l + 1] = rode
        pending = new_rs(REST_GROUP, grads[l], f"_l{l}_rest")
    grad_x = dh[None]

    out_g, out_d, out_m, out_v = {}, {}, {}, {}
    pack_off, off = {}, 0
    for n, rows in PACK_ROWS:
        pack_off[n] = (off, rows)
        off += rows

    def reduced(l, n):
        start, rows = pack_off[n]
        for names, block in ((REST_GROUP, rest_rows[l]), (XA_GROUP, later_rows[l][0]), (FFN_GROUP, later_rows[l][1])):
            if n in names:
                return block[start - pack_off[names[0]][0]:][:rows]

    def update(n, carry=None):
        shard_shape = w[n].shape
        g_n = jnp.stack([reduced(l, n) for l in range(depth)]).reshape(shard_shape)
        flat = lambda a: a.reshape(-1, shard_shape[-1])
        d_n, m_n, v_n, *rode = _adamw(flat(w[n]), flat(g_n), flat(m[n]), flat(v[n]), name=f"adamw_{n}", carry=carry)
        out_g[n], out_d[n], out_m[n], out_v[n] = g_n, d_n.reshape(shard_shape), m_n.reshape(shard_shape), v_n.reshape(shard_shape)
        return rode

    rest_rows[0] = pending.run()
    for n, _ in PACK_ROWS:
        update(n)

    g_small = _all_reduce_small(_flat_rows([grads[l][n] for l in range(depth) for n, _, axis in SMALL if axis is None]),
                                name="all_reduce_small_grads").reshape(-1)
    local_g, off = {}, 0
    for l in range(depth):
        for n, shape, axis in SMALL:
            if axis is None:
                size = int(np.prod(shape))
                local_g.setdefault(n, []).append(g_small[off:off + size].reshape(shape))
                off += size
        small_rows = rest_rows[l][sum(pack_off[n][1] for n in REST_GROUP):]
        for (n, _, _), block in zip(SHARDED_SMALL, _small_from_rows(small_rows)):
            local_g.setdefault(n, []).append(block)
    names = [n for n, _, _ in SMALL]
    g_loc = {n: jnp.stack(local_g[n]) for n in names}
    d_s, m_s, v_s = _adamw(_flat_rows([w[n] for n in names]), _flat_rows([g_loc[n] for n in names]),
                           _flat_rows([m[n] for n in names]), _flat_rows([v[n] for n in names]), name="adamw_small")
    off = 0
    for n in names:
        size = int(np.prod(w[n].shape))
        take = lambda a: a.reshape(-1)[off:off + size].reshape(w[n].shape)
        out_g[n], out_d[n], out_m[n], out_v[n] = g_loc[n], take(d_s), take(m_s), take(v_s)
        off += size

    outs = [loss, grad_x]
    for group in (out_g, out_d, out_m, out_v):
        outs += [group[n] for n in WEIGHT_ORDER]
    return tuple(outs)


def kernel(x, mem, positions, mix_norm_pre, w_in, cv_w, cv_b, cv_ln_g, cv_ln_b, cv_pw_w, cv_pw_b, w_out, mix_norm_post, x_norm_pre, mem_norm, x_wq, x_wk, x_wv, x_wo, x_norm_post, ffn_norm_pre, ffn_w_up, ffn_conv_w, ffn_conv_b, ffn_w_down, ffn_norm_post, loss_target, m_mix_norm_pre, m_w_in, m_cv_w, m_cv_b, m_cv_ln_g, m_cv_ln_b, m_cv_pw_w, m_cv_pw_b, m_w_out, m_mix_norm_post, m_x_norm_pre, m_mem_norm, m_x_wq, m_x_wk, m_x_wv, m_x_wo, m_x_norm_post, m_ffn_norm_pre, m_ffn_w_up, m_ffn_conv_w, m_ffn_conv_b, m_ffn_w_down, m_ffn_norm_post, v_mix_norm_pre, v_w_in, v_cv_w, v_cv_b, v_cv_ln_g, v_cv_ln_b, v_cv_pw_w, v_cv_pw_b, v_w_out, v_mix_norm_post, v_x_norm_pre, v_mem_norm, v_x_wq, v_x_wk, v_x_wv, v_x_wo, v_x_norm_post, v_ffn_norm_pre, v_ffn_w_up, v_ffn_conv_w, v_ffn_conv_b, v_ffn_w_down, v_ffn_norm_post):
    w = dict(zip(WEIGHT_ORDER, (mix_norm_pre, w_in, cv_w, cv_b, cv_ln_g, cv_ln_b, cv_pw_w, cv_pw_b, w_out, mix_norm_post, x_norm_pre, mem_norm, x_wq, x_wk, x_wv, x_wo, x_norm_post, ffn_norm_pre, ffn_w_up, ffn_conv_w, ffn_conv_b, ffn_w_down, ffn_norm_post)))
    m = dict(zip(WEIGHT_ORDER, (m_mix_norm_pre, m_w_in, m_cv_w, m_cv_b, m_cv_ln_g, m_cv_ln_b, m_cv_pw_w, m_cv_pw_b, m_w_out, m_mix_norm_post, m_x_norm_pre, m_mem_norm, m_x_wq, m_x_wk, m_x_wv, m_x_wo, m_x_norm_post, m_ffn_norm_pre, m_ffn_w_up, m_ffn_conv_w, m_ffn_conv_b, m_ffn_w_down, m_ffn_norm_post)))
    v = dict(zip(WEIGHT_ORDER, (v_mix_norm_pre, v_w_in, v_cv_w, v_cv_b, v_cv_ln_g, v_cv_ln_b, v_cv_pw_w, v_cv_pw_b, v_w_out, v_mix_norm_post, v_x_norm_pre, v_mem_norm, v_x_wq, v_x_wk, v_x_wv, v_x_wo, v_x_norm_post, v_ffn_norm_pre, v_ffn_w_up, v_ffn_conv_w, v_ffn_conv_b, v_ffn_w_down, v_ffn_norm_post)))
    return _step(x, mem, positions, loss_target, w, m, v)
```

```python
import functools

import jax
import jax.numpy as jnp
import numpy as np
from jax import lax
from jax.experimental import pallas as pl
from jax.experimental.pallas import tpu as pltpu

F32, BF16 = jnp.float32, jnp.bfloat16
MESH = pl.DeviceIdType.MESH
EPS = 1e-6
LANES = 128
BLK = 128
HD = 64
D_MODEL = 1024
D_FF = 2816
SB_W, CV_W, DL_W = 256, 256, 512
CV_K = 31
ROPE_THETA = 10000.0
DILATIONS = (1, 4, 16)
X_HEADS, X_HD = 4, 256
ADAM_LR, ADAM_B1, ADAM_B2, ADAM_EPS, ADAM_WD, ADAM_STEP = 0.001, 0.9, 0.999, 1e-08, 0.01, 10
NEG_INF = float("-inf")
MIB = 1 << 20

PACK_ROWS = (("w_in", 704), ("w_out", 256), ("x_wq", 256), ("x_wk", 256), ("x_wv", 256), ("x_wo", 256),
             ("ffn_w_up", 1408), ("ffn_w_down", 704))
PACK_RL = sum(r for _, r in PACK_ROWS)
SHARDED_SMALL = (("cv_w", (31, 64), 1), ("ffn_conv_w", (3, 1408), 1), ("cv_pw_w", (64, 256), 0))
SMALL_ROWS = 32
WEIGHT_PACK = (PACK_ROWS[0], ("small", 2 * SMALL_ROWS)) + PACK_ROWS[1:]


def _cp(vmem_mb=48):
    return pltpu.CompilerParams(vmem_limit_bytes=vmem_mb * MIB)


def _dot(a, b):
    return jnp.dot(a, b, preferred_element_type=F32)


def _dot_nt(a, b):
    return lax.dot_general(a, b, (((1,), (1,)), ((), ())), preferred_element_type=F32)


def _dot_tn(a, b):
    return lax.dot_general(a, b, (((0,), (0,)), ((), ())), preferred_element_type=F32)


def _dot_hilo(x, m):
    hi = x.astype(BF16)
    lo = (x - hi.astype(F32)).astype(BF16)
    return _dot(hi, m) + _dot(lo, m)


def _rowsum8(x):
    t, c = x.shape
    return x.reshape(t // 8, 8, c).sum(axis=0)


def _acc_out(ref, i, val):
    @pl.when(i == 0)
    def _():
        ref[...] = val

    @pl.when(i > 0)
    def _():
        ref[...] += val


def _tile(n, cap, mult=8):
    t = min(n, cap)
    while n % t or t % mult:
        t -= 1
    return t


def _rms_mm(x, g, w, *, tm, tn, out_dtype, name, carry=None):
    m, d = x.shape
    n_out = w.shape[1]

    def body(x_ref, g_ref, w_ref, n_ref, o_ref):
        @pl.when(pl.program_id(1) == 0)
        def _():
            xv = x_ref[...]
            r = lax.rsqrt(jnp.mean(xv * xv, axis=-1, keepdims=True) + EPS)
            n_ref[...] = (xv * r * g_ref[...]).astype(BF16)

        o_ref[...] = _dot(n_ref[...], w_ref[...]).astype(out_dtype)

    return _call(
        body, grid=(m // tm, n_out // tn), name=name, carry=carry,
        in_specs=[pl.BlockSpec((tm, d), lambda i, j: (i, 0)), pl.BlockSpec((1, d), lambda i, j: (0, 0)),
                  pl.BlockSpec((d, tn), lambda i, j: (0, j))],
        out_specs=[pl.BlockSpec((tm, d), lambda i, j: (i, 0)), pl.BlockSpec((tm, tn), lambda i, j: (i, j))],
        out_shape=[jax.ShapeDtypeStruct((m, d), BF16), jax.ShapeDtypeStruct((m, n_out), out_dtype)],
        args=(x, g, w))


def _mm_post(a, w, h, g, *, tm, name, carry=None):
    m, k = a.shape
    d = w.shape[1]

    def body(a_ref, w_ref, h_ref, g_ref, y_ref, ho_ref):
        y = _dot(a_ref[...], w_ref[...])
        y_ref[...] = y
        r = lax.rsqrt(jnp.mean(y * y, axis=-1, keepdims=True) + EPS)
        ho_ref[...] = h_ref[...] + y * r * g_ref[...]

    return _call(
        body, grid=(m // tm,), name=name, carry=carry,
        in_specs=[pl.BlockSpec((tm, k), lambda i: (i, 0)), pl.BlockSpec((k, d), lambda i: (0, 0)),
                  pl.BlockSpec((tm, d), lambda i: (i, 0)), pl.BlockSpec((1, d), lambda i: (0, 0))],
        out_specs=[pl.BlockSpec((tm, d), lambda i: (i, 0)), pl.BlockSpec((tm, d), lambda i: (i, 0))],
        out_shape=[jax.ShapeDtypeStruct((m, d), F32), jax.ShapeDtypeStruct((m, d), F32)],
        args=(a, w, h, g))


def _mm_nt(a, w, *, tm, tn, out_dtype, name):
    m, k = a.shape
    n_out = w.shape[0]

    def body(a_ref, w_ref, o_ref):
        o_ref[...] = _dot_nt(a_ref[...], w_ref[...]).astype(out_dtype)

    return pl.pallas_call(
        body, grid=(n_out // tn, m // tm), name=name,
        in_specs=[pl.BlockSpec((tm, k), lambda j, i: (i, 0)), pl.BlockSpec((tn, k), lambda j, i: (j, 0))],
        out_specs=pl.BlockSpec((tm, tn), lambda j, i: (i, j)),
        out_shape=jax.ShapeDtypeStruct((m, n_out), out_dtype),
        compiler_params=_cp())(a, w)


def _mm_tn(x, dy, *, tk, tn, tm, name):
    m, k = x.shape
    n_out = dy.shape[1]

    def body(x_ref, d_ref, o_ref):
        _acc_out(o_ref, pl.program_id(2), _dot_tn(x_ref[...], d_ref[...]))

    return pl.pallas_call(
        body, grid=(k // tk, n_out // tn, m // tm), name=name,
        in_specs=[pl.BlockSpec((tm, tk), lambda a, b, c: (c, a)), pl.BlockSpec((tm, tn), lambda a, b, c: (c, b))],
        out_specs=pl.BlockSpec((tk, tn), lambda a, b, c: (a, b)),
        out_shape=jax.ShapeDtypeStruct((k, n_out), F32),
        compiler_params=_cp())(x, dy)


def _rms_bwd(x, g, dout, res, *, out_dtype, tm, name, carry=None):
    m, d = x.shape
    has_res = res is not None

    def body(*refs):
        if has_res:
            x_ref, g_ref, d_ref, r_ref, dx_ref, dg_ref = refs
        else:
            x_ref, g_ref, d_ref, dx_ref, dg_ref = refs
        xv = x_ref[...]
        dv = d_ref[...].astype(F32)
        r = lax.rsqrt(jnp.mean(xv * xv, axis=-1, keepdims=True) + EPS)
        xh = xv * r
        dxh = dv * g_ref[...]
        dx = r * (dxh - xh * jnp.mean(dxh * xh, axis=-1, keepdims=True))
        if has_res:
            dx = dx + r_ref[...]
        dx_ref[...] = dx.astype(out_dtype)
        _acc_out(dg_ref, pl.program_id(0), _rowsum8(dv * xh))

    row = pl.BlockSpec((tm, d), lambda i: (i, 0))
    ins = [row, pl.BlockSpec((1, d), lambda i: (0, 0)), row] + ([row] if has_res else [])
    args = (x, g, dout) + ((res,) if has_res else ())
    return _call(
        body, grid=(m // tm,), name=name, carry=carry, in_specs=ins,
        out_specs=[row, pl.BlockSpec((8, d), lambda i: (0, 0))],
        out_shape=[jax.ShapeDtypeStruct((m, d), out_dtype), jax.ShapeDtypeStruct((8, d), F32)], args=args)


def _loss_grad(h, tgt, *, tm, name):
    m, d = h.shape

    def body(h_ref, t_ref, dh_ref, p_ref):
        e = h_ref[...] - t_ref[...]
        dh_ref[...] = e / d
        _acc_out(p_ref, pl.program_id(0), _rowsum8(e * e))

    row = pl.BlockSpec((tm, d), lambda i: (i, 0))
    return pl.pallas_call(
        body, grid=(m // tm,), name=name, in_specs=[row, row],
        out_specs=[row, pl.BlockSpec((8, d), lambda i: (0, 0))],
        out_shape=[jax.ShapeDtypeStruct((m, d), F32), jax.ShapeDtypeStruct((8, d), F32)],
        compiler_params=_cp())(h, tgt)


def _adamw(w, g, m, v, *, name, carry=None):
    r, c = w.shape
    tr = _tile(r, 256)

    def body(w_ref, g_ref, m_ref, v_ref, d_ref, mo_ref, vo_ref):
        gv = g_ref[...]
        m2 = ADAM_B1 * m_ref[...] + (1.0 - ADAM_B1) * gv
        v2 = ADAM_B2 * v_ref[...] + (1.0 - ADAM_B2) * jnp.square(gv)
        m_hat = m2 / (1.0 - ADAM_B1 ** ADAM_STEP)
        v_hat = v2 / (1.0 - ADAM_B2 ** ADAM_STEP)
        d_ref[...] = -ADAM_LR * (m_hat / (jnp.sqrt(v_hat) + ADAM_EPS) + ADAM_WD * w_ref[...])
        mo_ref[...] = m2
        vo_ref[...] = v2

    blk = pl.BlockSpec((tr, c), lambda i: (i, 0))
    return _call(body, grid=(r // tr,), name=name, carry=carry, in_specs=[blk] * 4, out_specs=[blk] * 3,
                 out_shape=[jax.ShapeDtypeStruct((r, c), F32)] * 3, args=(w, g, m, v))


def _head_masks():
    lane = lax.broadcasted_iota(jnp.int32, (BLK, LANES), 1)
    row = lax.broadcasted_iota(jnp.int32, (BLK, LANES), 0)
    return lane, row, lane < HD


def _sb_scores(q_a, k, before):
    z = _dot_nt(q_a, k)
    sp = jnp.log1p(jnp.exp(-jnp.abs(z)))
    ls_pos = jnp.minimum(z, 0.0) - sp
    lkeep = jnp.where(before, ls_pos - z, 0.0)
    return ls_pos, lkeep


SB_DEAD = -104.0


def _sb_alive(jj, i, carry):
    return jnp.logical_and(jj <= i, jnp.max(carry) > SB_DEAD)


SB_QB_FWD = 2
SB_QB = 2


def _sb_before(jj, qb=SB_QB):
    lane = lax.broadcasted_iota(jnp.int32, (qb * 2 * BLK, LANES), 1)
    row = lax.broadcasted_iota(jnp.int32, (qb * 2 * BLK, LANES), 0)
    below_diag = jj - (qb - 1) + row // (2 * BLK)
    return jnp.logical_or(below_diag > 0, jnp.logical_and(below_diag == 0, lane < row % BLK))


def _sb_stack(x, lane_h, qb=SB_QB):
    return jnp.concatenate([_stack_heads(x[b * BLK:(b + 1) * BLK], lane_h) for b in range(qb)], axis=0)


def _sb_unstack(x, lane_h, qb=SB_QB):
    return jnp.concatenate([jnp.where(lane_h, x[2 * b * BLK:(2 * b + 1) * BLK], x[(2 * b + 1) * BLK:(2 * b + 2) * BLK])
                            for b in range(qb)], axis=0)


SB_ROWS = SB_QB * 2 * BLK


def _sb_fwd(u, *, name, carry=None):
    s_len = u.shape[0]
    qb = SB_QB_FWD
    qrows, rows = qb * BLK, qb * 2 * BLK

    def body(q_ref, k_ref, v_ref, o_ref):
        top = pl.program_id(0) * qb + qb - 1
        lane, row, lane_h = _head_masks()
        suffix = (row > lane).astype(BF16)
        pairs = [slice(hp * LANES, (hp + 1) * LANES) for hp in range(2)]
        qs = [_sb_stack(q_ref[:, cs] * 0.125, lane_h, qb) for cs in pairs]

        def step(state):
            jj, ccs, accs = state[0], state[1:3], state[3:5]
            rows_k = pl.ds(pl.multiple_of((top - jj) * BLK, BLK), BLK)
            before = _sb_before(jj, qb)
            scores = [_sb_scores(q, k_ref[rows_k, cs].astype(BF16), before) for q, cs in zip(qs, pairs)]
            between = [_dot_hilo(lkeep, suffix) + cc for (_, lkeep), cc in zip(scores, ccs)]
            atts = [jnp.where(before, jnp.exp(ls_pos + b), 0.0).astype(BF16) for (ls_pos, _), b in zip(scores, between)]
            new_cc = [cc + jnp.sum(lkeep, axis=1, keepdims=True) for (_, lkeep), cc in zip(scores, ccs)]
            new_acc = [acc + _dot(a, v_ref[rows_k, cs].astype(BF16)) for a, acc, cs in zip(atts, accs, pairs)]
            return (jj + 1, *new_cc, *new_acc)

        zc, za = jnp.zeros((rows, 1), F32), jnp.zeros((rows, LANES), F32)
        res = lax.while_loop(lambda st: _sb_alive(st[0], top, jnp.maximum(st[1], st[2])), step,
                             (jnp.int32(0), zc, zc, za, za))
        for hp, cs in enumerate(pairs):
            o_ref[:, cs] = _sb_unstack(res[3 + hp], lane_h, qb).astype(BF16)

    wide = 2 * LANES
    return _call(
        body, grid=(s_len // qrows,), name=name, carry=carry,
        in_specs=[pl.BlockSpec((qrows, wide), lambda i: (i, 0)), pl.BlockSpec((s_len, wide), lambda i: (0, 1)),
                  pl.BlockSpec((s_len, wide), lambda i: (0, 2))],
        out_specs=[pl.BlockSpec((qrows, wide), lambda i: (i, 0))],
        out_shape=[jax.ShapeDtypeStruct((s_len, SB_W), BF16)], args=(u, u, u))


def _sb_bwd(u, dcat, *, name, carry=None):
    s_len = u.shape[0]
    nq = s_len // BLK
    qrows = SB_QB * BLK

    def body(q_ref, k_ref, v_ref, do_ref, dq_ref, dk_ref, dv_ref, g_scr, b_scr):
        step = pl.program_id(1)
        top = step * SB_QB + SB_QB - 1
        lane, row, lane_h = _head_masks()
        suffix = (row > lane).astype(BF16)
        prefix = (row < lane).astype(BF16)
        qf = q_ref[...]
        qs = _sb_stack(qf * 0.125, lane_h)
        qu = _sb_stack(qf, lane_h)
        dos = _sb_stack(do_ref[...], lane_h)

        @pl.when(step == 0)
        def _():
            dk_ref[...] = jnp.zeros_like(dk_ref)
            dv_ref[...] = jnp.zeros_like(dv_ref)

        def down(state):
            jj, cc = state
            j = top - jj
            off = pl.multiple_of(j * BLK, BLK)
            k = k_ref[pl.ds(off, BLK), :].astype(BF16)
            v = v_ref[pl.ds(off, BLK), :].astype(BF16)
            before = _sb_before(jj)
            ls_pos, lkeep = _sb_scores(qs, k, before)
            between = _dot_hilo(lkeep, suffix) + cc
            att = jnp.where(before, jnp.exp(ls_pos + between), 0.0)
            g_scr[j] = att * _dot_nt(dos, v)
            b_scr[j] = jnp.exp(ls_pos)
            dv_ref[pl.ds(off, BLK), :] += _dot_tn(att.astype(BF16), dos)
            return jj + 1, cc + jnp.sum(lkeep, axis=1, keepdims=True)

        zc = jnp.zeros((SB_ROWS, 1), F32)
        visited = lax.while_loop(lambda st: _sb_alive(st[0], top, st[1]), down, (jnp.int32(0), zc))[0]

        def up(j, carry):
            pc, dq = carry
            off = pl.multiple_of(j * BLK, BLK)
            k = k_ref[pl.ds(off, BLK), :].astype(BF16)
            g, beta = g_scr[j], b_scr[j]
            below = _dot_hilo(g, prefix) + pc
            dz = (jnp.where(_sb_before(top - j), g * (1.0 - beta) - beta * below, 0.0) * 0.125).astype(BF16)
            dk_ref[pl.ds(off, BLK), :] += _dot_tn(dz, qu)
            return pc + jnp.sum(g, axis=1, keepdims=True), dq + _dot(dz, k)

        dq = lax.fori_loop(top + 1 - visited, top + 1, up, (zc, jnp.zeros((SB_ROWS, LANES), F32)))[1]
        dq_ref[...] = _sb_unstack(dq, lane_h)

    col = lambda c0: pl.BlockSpec((s_len, LANES), lambda hp, i: (0, c0 + hp))
    blk = pl.BlockSpec((qrows, LANES), lambda hp, i: (i, hp))
    acc = pl.BlockSpec((s_len, LANES), lambda hp, i: (0, hp))
    return _call(
        body, grid=(2, s_len // qrows), name=name, carry=carry, in_specs=[blk, col(2), col(4), blk],
        out_specs=[blk, acc, acc], out_shape=[jax.ShapeDtypeStruct((s_len, SB_W), F32)] * 3,
        scratch_shapes=[pltpu.VMEM((nq, SB_ROWS, LANES), F32), pltpu.VMEM((nq, SB_ROWS, LANES), F32)],
        vmem_mb=56, args=(u, u, u, dcat))


CV_T = 512
CV_H = 32
CV_STRIP = 64


def _cv_specs(s_len):
    cur = lambda c: pl.BlockSpec((CV_T, CV_W), lambda i: (i, c))
    prev = lambda c: pl.BlockSpec((CV_H, CV_W), lambda i: (jnp.maximum(i * (CV_T // CV_H) - 1, 0), c))
    nxt = lambda c: pl.BlockSpec((CV_H, CV_W),
                                 lambda i: (jnp.minimum((i + 1) * (CV_T // CV_H), s_len // CV_H - 1), c))
    full = lambda r: pl.BlockSpec((r, CV_W), lambda i: (0, 0))
    return cur, prev, nxt, full


def _glu_into(gp_ref, val_ref, gate_ref, valp_ref, gatep_ref, i):
    gp_ref[0:CV_H, :] = jnp.where(i > 0, valp_ref[...] * jax.nn.sigmoid(gatep_ref[...]), 0.0)
    gp_ref[CV_H:, :] = val_ref[...] * jax.nn.sigmoid(gate_ref[...])


CV_SH = CV_T + CV_H - 8


def _shifted_copies(sh_ref, slab_ref):
    for r in range(1, 8):
        sh_ref[r - 1] = slab_ref[pl.ds(r, CV_SH), :]


def _tap(sh_ref, slab_ref, off, r0, rows):
    if off % 8 == 0:
        return slab_ref[pl.ds(off + r0, rows), :]
    return sh_ref[off % 8 - 1, pl.ds(off - off % 8 + r0, rows), :]


def _cv_fwd(u, cv_w, cv_b, ln_g, ln_b, pw_w, pw_b, *, name):
    s_len = u.shape[0]
    cur, prev, _, full = _cv_specs(s_len)

    def body(val_ref, gate_ref, valp_ref, gatep_ref, w_ref, b_ref, g_ref, be_ref, pw_ref, pb_ref,
             o_ref, c_ref, gp_ref, sh_ref):
        _glu_into(gp_ref, val_ref, gate_ref, valp_ref, gatep_ref, pl.program_id(0))
        _shifted_copies(sh_ref, gp_ref)
        for r0, rows in _strips(CV_T, CV_STRIP):
            acc = jnp.zeros((rows, CV_W), F32) + b_ref[...]
            for k in range(CV_K):
                acc = acc + w_ref[k:k + 1, :] * _tap(sh_ref, gp_ref, CV_H - CV_K + 1 + k, r0, rows)
            c_ref[r0:r0 + rows, :] = acc
        acc = c_ref[...]
        mu = jnp.mean(acc, axis=-1, keepdims=True)
        xc = acc - mu
        xh = xc * lax.rsqrt(jnp.mean(xc * xc, axis=-1, keepdims=True) + EPS)
        a = xh * g_ref[...] + be_ref[...]
        s = a * jax.nn.sigmoid(a)
        o_ref[...] = (_dot(s.astype(BF16), pw_ref[...]) + pb_ref[...]).astype(BF16)

    return pl.pallas_call(
        body, grid=(s_len // CV_T,), name=name,
        in_specs=[cur(3), cur(4), prev(3), prev(4), full(CV_K), full(1), full(1), full(1), full(CV_W), full(1)],
        out_specs=[cur(0), cur(0)],
        out_shape=[jax.ShapeDtypeStruct((s_len, CV_W), BF16), jax.ShapeDtypeStruct((s_len, CV_W), F32)],
        scratch_shapes=[pltpu.VMEM((CV_T + CV_H, CV_W), F32), pltpu.VMEM((7, CV_SH, CV_W), F32)],
        compiler_params=_cp())(u, u, u, u, cv_w, cv_b, ln_g, ln_b, pw_w, pw_b)


def _cv_bwd_local(c, dcat, ln_g, ln_b, pw_w, *, name):
    s_len = c.shape[0]
    cur, _, _, full = _cv_specs(s_len)

    def body(c_ref, db_ref, g_ref, be_ref, pw_ref, dc_ref, dpw_ref, vec_ref):
        i = pl.program_id(0)
        cv = c_ref[...]
        db = db_ref[...]
        mu = jnp.mean(cv, axis=-1, keepdims=True)
        xc = cv - mu
        rstd = lax.rsqrt(jnp.mean(xc * xc, axis=-1, keepdims=True) + EPS)
        xh = xc * rstd
        a = xh * g_ref[...] + be_ref[...]
        sg = jax.nn.sigmoid(a)
        s = a * sg
        dbb = db.astype(BF16)
        ds = _dot_nt(dbb, pw_ref[...])
        da = ds * (sg * (1.0 + a * (1.0 - sg)))
        dxh = da * g_ref[...]
        dc_ref[...] = rstd * (dxh - jnp.mean(dxh, axis=-1, keepdims=True)
                              - xh * jnp.mean(dxh * xh, axis=-1, keepdims=True))
        _acc_out(dpw_ref, i, _dot_tn(s.astype(BF16), dbb))
        _acc_out(vec_ref, i, jnp.concatenate([_rowsum8(db), _rowsum8(da * xh), _rowsum8(da)], axis=0))

    return pl.pallas_call(
        body, grid=(s_len // CV_T,), name=name,
        in_specs=[cur(0), cur(1), full(1), full(1), full(CV_W)],
        out_specs=[cur(0), full(CV_W), full(24)],
        out_shape=[jax.ShapeDtypeStruct((s_len, CV_W), F32), jax.ShapeDtypeStruct((CV_W, CV_W), F32),
                   jax.ShapeDtypeStruct((24, CV_W), F32)], compiler_params=_cp())(c, dcat, ln_g, ln_b, pw_w)


def _cv_bwd_conv(u, dc, cv_w, *, name):
    s_len = u.shape[0]
    cur, prev, nxt, full = _cv_specs(s_len)
    last = s_len // CV_T - 1

    def body(val_ref, gate_ref, valp_ref, gatep_ref, dc_ref, dcn_ref, w_ref, du_ref, dw_ref, dbias_ref,
             gp_ref, dcp_ref, gsh_ref, dsh_ref):
        i = pl.program_id(0)
        _glu_into(gp_ref, val_ref, gate_ref, valp_ref, gatep_ref, i)
        dcv = dc_ref[...]
        dcp_ref[0:CV_T, :] = dcv
        dcp_ref[CV_T:, :] = jnp.where(i < last, dcn_ref[...], 0.0)
        _shifted_copies(gsh_ref, gp_ref)
        _shifted_copies(dsh_ref, dcp_ref)
        strips = _strips(CV_T, CV_STRIP)
        for r0, rows in strips:
            dg = jnp.zeros((rows, CV_W), F32)
            for k in range(CV_K):
                dg = dg + w_ref[k:k + 1, :] * _tap(dsh_ref, dcp_ref, CV_K - 1 - k, r0, rows)
            sg = jax.nn.sigmoid(gate_ref[r0:r0 + rows, :])
            du_ref[r0:r0 + rows, 0:CV_W] = (dg * sg).astype(BF16)
            du_ref[r0:r0 + rows, CV_W:] = (dg * val_ref[r0:r0 + rows, :] * sg * (1.0 - sg)).astype(BF16)
        parts = []
        for k in range(CV_K):
            part = jnp.zeros((8, CV_W), F32)
            for r0, rows in strips:
                part = part + _rowsum8(dc_ref[r0:r0 + rows, :] * _tap(gsh_ref, gp_ref, CV_H - CV_K + 1 + k, r0, rows))
            parts.append(part)
        _acc_out(dw_ref, i, jnp.concatenate(parts, axis=0))
        _acc_out(dbias_ref, i, _rowsum8(dcv))

    return pl.pallas_call(
        body, grid=(s_len // CV_T,), name=name,
        in_specs=[cur(3), cur(4), prev(3), prev(4), cur(0), nxt(0), full(CV_K)],
        out_specs=[pl.BlockSpec((CV_T, 2 * CV_W), lambda i: (i, 0)), full(CV_K * 8), full(8)],
        out_shape=[jax.ShapeDtypeStruct((s_len, 2 * CV_W), BF16), jax.ShapeDtypeStruct((CV_K * 8, CV_W), F32),
                   jax.ShapeDtypeStruct((8, CV_W), F32)],
        scratch_shapes=[pltpu.VMEM((CV_T + CV_H, CV_W), F32)] * 2 + [pltpu.VMEM((7, CV_SH, CV_W), F32)] * 2,
        compiler_params=_cp())(u, u, u, u, dc, dc, cv_w)


def _rope_tables(pos_col, inv_freq_row, *, name):
    s_len = pos_col.shape[0]

    def body(p_ref, f_ref, cos_ref, sin_ref):
        ang = p_ref[...].astype(F32) * f_ref[...]
        lane = lax.broadcasted_iota(jnp.int32, (s_len, LANES), 1)
        sn = jnp.sin(ang)
        cos_ref[...] = jnp.cos(ang)
        sin_ref[...] = jnp.where(lane % HD < HD // 2, -sn, sn)

    return pl.pallas_call(body, name=name, out_shape=[jax.ShapeDtypeStruct((s_len, LANES), F32)] * 2,
                          compiler_params=_cp())(pos_col, inv_freq_row)


def _rot_half(x):
    lane = lax.broadcasted_iota(jnp.int32, x.shape, 1)
    return jnp.where(lane % HD < HD // 2, pltpu.roll(x, LANES - HD // 2, 1), pltpu.roll(x, HD // 2, 1))


def _permute_rows(dst_ref, src_ref, d, dtype):
    s_len = src_ref.shape[0]
    seg = s_len // d
    if d == 1:
        dst_ref[...] = src_ref[...].astype(dtype)
        return
    for r in range(d):
        dst_ref[r * seg:(r + 1) * seg, :] = src_ref[pl.ds(r, seg, stride=d), :].astype(dtype)


def _unpermute_rows(dst_ref, src_ref, d):
    s_len = src_ref.shape[0]
    seg = s_len // d
    if d == 1:
        dst_ref[...] = src_ref[...]
        return
    for r in range(d):
        dst_ref[pl.ds(r, seg, stride=d), :] = src_ref[r * seg:(r + 1) * seg, :]


def _rope_perm(u, cos, sin, *, name):
    s_len = u.shape[0]

    def body(x_ref, cos_ref, sin_ref, o_ref, scr):
        a = pl.program_id(0)
        x = x_ref[...]
        rot = a < 2
        scr[...] = x * jnp.where(rot, cos_ref[...], 1.0) + _rot_half(x) * jnp.where(rot, sin_ref[...], 0.0)
        for n, d in enumerate(DILATIONS):
            _permute_rows(o_ref.at[n], scr, d, BF16)

    tab = pl.BlockSpec((s_len, LANES), lambda a, cb: (0, 0))
    return pl.pallas_call(
        body, grid=(3, 4), name=name,
        in_specs=[pl.BlockSpec((s_len, LANES), lambda a, cb: (0, 10 + 4 * a + cb)), tab, tab],
        out_specs=pl.BlockSpec((None, 3, s_len, LANES), lambda a, cb: (a, 0, 0, cb)),
        out_shape=jax.ShapeDtypeStruct((3, 3, s_len, DL_W), BF16),
        scratch_shapes=[pltpu.VMEM((s_len, LANES), F32)], compiler_params=_cp())(u, cos, sin)


DL_UNROLL = 4


def _dl_band(rows):
    lane = lax.broadcasted_iota(jnp.int32, (rows, LANES), 1)
    row = lax.broadcasted_iota(jnp.int32, (rows, LANES), 0) % BLK
    return lane <= row, lane >= row


def _dl_first(s_len, n, i):
    nb = jnp.where(n == 0, s_len // BLK, jnp.where(n == 1, s_len // (BLK * DILATIONS[1]),
                                                   s_len // (BLK * DILATIONS[2])))
    return lax.rem(i, nb) == 0


def _stack_heads(x, lane_h):
    return jnp.concatenate([jnp.where(lane_h, x, 0.0), jnp.where(lane_h, 0.0, x)], axis=0).astype(BF16)


def _dl_rows(i):
    cur = pl.ds(pl.multiple_of(i * BLK, BLK), BLK)
    prev = pl.ds(pl.multiple_of(jnp.maximum(i - 1, 0) * BLK, BLK), BLK)
    return cur, prev


def _dl_in_specs(s_len):
    return [pl.BlockSpec((None, None, s_len, LANES), functools.partial(lambda a, n, hp: (a, n, 0, hp), a))
            for a in range(3)]


def _dl_fwd(qkv, *, name, carry=None):
    s_len = qkv.shape[2]

    def body(q_ref, k_ref, v_ref, o_ref, l_ref):
        n = pl.program_id(0)
        lane_h = _head_masks()[2]
        band_c, band_p = _dl_band(2 * BLK)
        ones = jnp.ones((BLK, LANES), BF16)

        @pl.loop(0, s_len // BLK, step=DL_UNROLL)
        def _(i0):
            blocks = [i0 + t for t in range(DL_UNROLL)]
            rows = [_dl_rows(i) for i in blocks]
            scores = []
            for cur, prev in rows:
                qs = _stack_heads(q_ref[cur, :] * 0.125, lane_h)
                scores.append((_dot_nt(qs, k_ref[cur, :]), _dot_nt(qs, k_ref[prev, :])))
            probs = []
            for i, (sc, sp) in zip(blocks, scores):
                sc = jnp.where(band_c, sc, NEG_INF)
                sp = jnp.where(jnp.logical_and(band_p, jnp.logical_not(_dl_first(s_len, n, i))), sp, NEG_INF)
                m = jnp.max(jnp.maximum(sc, sp), axis=1, keepdims=True)
                probs.append((jnp.exp(sc - m).astype(BF16), jnp.exp(sp - m).astype(BF16), m))
            for (cur, prev), (pc, pp, m) in zip(rows, probs):
                r = (_dot(pc, jnp.concatenate([v_ref[cur, :], ones], axis=1))
                     + _dot(pp, jnp.concatenate([v_ref[prev, :], ones], axis=1)))
                den = jnp.where(lane_h, r[:BLK, LANES:], r[BLK:, LANES:])
                o_ref[cur, :] = jnp.where(lane_h, r[:BLK, :LANES], r[BLK:, :LANES]) / den
                l_ref[cur, :] = jnp.where(lane_h, m[:BLK], m[BLK:]) + jnp.log(den)

    out = pl.BlockSpec((None, s_len, LANES), lambda n, hp: (n, 0, hp))
    return _call(
        body, grid=(3, 4), name=name, carry=carry, in_specs=_dl_in_specs(s_len), out_specs=[out, out],
        out_shape=[jax.ShapeDtypeStruct((3, s_len, DL_W), F32)] * 2, args=(qkv, qkv, qkv))


def _dl_mix(o_p, l_p, *, name, carry=None):
    s_len = o_p.shape[1]

    def body(o_ref, l_ref, ob_ref, of_ref, lt_ref, o_scr, l_scr):
        n = pl.program_id(1)
        for k, d in enumerate(DILATIONS):
            @pl.when(n == k)
            def _(k=k, d=d):
                _unpermute_rows(o_scr.at[k], o_ref, d)
                _unpermute_rows(l_scr.at[k], l_ref, d)

        @pl.when(n == 2)
        def _():
            l0, l1, l2 = l_scr[0], l_scr[1], l_scr[2]
            m = jnp.maximum(jnp.maximum(l0, l1), l2)
            e0, e1, e2 = jnp.exp(l0 - m), jnp.exp(l1 - m), jnp.exp(l2 - m)
            den = e0 + e1 + e2
            o = (e0 / den) * o_scr[0] + (e1 / den) * o_scr[1] + (e2 / den) * o_scr[2]
            of_ref[...] = o
            ob_ref[...] = o.astype(BF16)
            lt_ref[...] = m + jnp.log(den)

    inb = pl.BlockSpec((None, s_len, LANES), lambda cb, n: (n, 0, cb))
    outb = pl.BlockSpec((s_len, LANES), lambda cb, n: (0, cb))
    return _call(
        body, grid=(4, 3), name=name, carry=carry, in_specs=[inb, inb], out_specs=[outb, outb, outb],
        out_shape=[jax.ShapeDtypeStruct((s_len, DL_W), BF16), jax.ShapeDtypeStruct((s_len, DL_W), F32),
                   jax.ShapeDtypeStruct((s_len, DL_W), F32)],
        scratch_shapes=[pltpu.VMEM((3, s_len, LANES), F32), pltpu.VMEM((3, s_len, LANES), F32)], args=(o_p, l_p))


def _dl_bwd_prep(dcat, o, lse, *, name):
    s_len = o.shape[0]

    def body(do_ref, o_ref, l_ref, dop_ref, st_ref, d_scr):
        n = pl.program_id(1)

        @pl.when(n == 0)
        def _():
            r0 = lax.broadcasted_iota(jnp.int32, (LANES, LANES), 0) // HD
            r1 = lax.broadcasted_iota(jnp.int32, (LANES, LANES), 1) // HD
            d_scr[...] = _dot_hilo(do_ref[...] * o_ref[...], (r0 == r1).astype(BF16))

        for k, d in enumerate(DILATIONS):
            @pl.when(n == k)
            def _(d=d):
                _permute_rows(dop_ref, do_ref, d, BF16)
                _permute_rows(st_ref.at[0], d_scr, d, F32)
                _permute_rows(st_ref.at[1], l_ref, d, F32)

    nat = lambda c0: pl.BlockSpec((s_len, LANES), lambda cb, n: (0, c0 + cb))
    return pl.pallas_call(
        body, grid=(4, 3), name=name, in_specs=[nat(4), nat(0), nat(0)],
        out_specs=[pl.BlockSpec((None, s_len, LANES), lambda cb, n: (n, 0, cb)),
                   pl.BlockSpec((2, None, s_len, LANES), lambda cb, n: (0, n, 0, cb))],
        out_shape=[jax.ShapeDtypeStruct((3, s_len, DL_W), BF16), jax.ShapeDtypeStruct((2, 3, s_len, DL_W), F32)],
        scratch_shapes=[pltpu.VMEM((s_len, LANES), F32)], compiler_params=_cp())(dcat, o, lse)


def _dl_bwd(qkv, dop, stats, *, name, carry=None):
    s_len = qkv.shape[2]

    def body(q_ref, k_ref, v_ref, do_ref, st_ref, cur_ref, prev_ref):
        n = pl.program_id(0)
        lane_h = _head_masks()[2]
        band_c, band_p = _dl_band(2 * BLK)

        def per_head(x):
            xr = pltpu.roll(x, HD, 1)
            return jnp.concatenate([jnp.where(lane_h, x, xr), jnp.where(lane_h, xr, x)], axis=0)

        @pl.loop(0, s_len // BLK, step=DL_UNROLL)
        def _(i0):
            blocks = [i0 + t for t in range(DL_UNROLL)]
            rows = [_dl_rows(i) for i in blocks]
            stage1 = []
            for cur, prev in rows:
                qs = _stack_heads(q_ref[cur, :] * 0.125, lane_h)
                dos = _stack_heads(do_ref[cur, :], lane_h)
                kc, kp, vc, vp = k_ref[cur, :], k_ref[prev, :], v_ref[cur, :], v_ref[prev, :]
                stage1.append((qs, dos, _dot_nt(qs, kc), _dot_nt(qs, kp), _dot_nt(dos, vc), _dot_nt(dos, vp)))
            stage2 = []
            for i, (cur, prev), (qs, dos, sc, sp, dpc, dpp) in zip(blocks, rows, stage1):
                lse, delta = per_head(st_ref[1, cur, :]), per_head(st_ref[0, cur, :])
                pc = jnp.where(band_c, jnp.exp(sc - lse), 0.0)
                pp = jnp.where(jnp.logical_and(band_p, jnp.logical_not(_dl_first(s_len, n, i))), jnp.exp(sp - lse), 0.0)
                stage2.append((pc.astype(BF16), pp.astype(BF16), (pc * (dpc - delta)).astype(BF16),
                               (pp * (dpp - delta)).astype(BF16)))
            for (cur, prev), (qs, dos, *_), (pc, pp, dsc, dsp) in zip(rows, stage1, stage2):
                dq = _dot(dsc, k_ref[cur, :]) + _dot(dsp, k_ref[prev, :])
                cur_ref[0, cur, :] = jnp.where(lane_h, dq[:BLK], dq[BLK:]) * 0.125
                cur_ref[1, cur, :] = _dot_tn(dsc, qs)
                cur_ref[2, cur, :] = _dot_tn(pc, dos)
                prev_ref[0, cur, :] = _dot_tn(dsp, qs)
                prev_ref[1, cur, :] = _dot_tn(pp, dos)

    return _call(
        body, grid=(3, 4), name=name, carry=carry,
        in_specs=_dl_in_specs(s_len) + [pl.BlockSpec((None, s_len, LANES), lambda n, hp: (n, 0, hp)),
                                        pl.BlockSpec((2, None, s_len, LANES), lambda n, hp: (0, n, 0, hp))],
        out_specs=[pl.BlockSpec((3, None, s_len, LANES), lambda n, hp: (0, n, 0, hp)),
                   pl.BlockSpec((2, None, s_len, LANES), lambda n, hp: (0, n, 0, hp))],
        out_shape=[jax.ShapeDtypeStruct((3, 3, s_len, DL_W), F32), jax.ShapeDtypeStruct((2, 3, s_len, DL_W), F32)],
        vmem_mb=56, args=(qkv, qkv, qkv, dop, stats))


def _dl_bwd_finish(cur, prev, cos, sin, *, name):
    s_len = cur.shape[2]

    def body(c_ref, p_ref, cos_ref, sin_ref, o_ref, p_scr, u_scr, acc):
        a, n = pl.program_id(0), pl.program_id(2)
        has_prev = jnp.where(a > 0, 1.0, 0.0)
        p_scr[...] = c_ref[...]
        p_scr[0:s_len - BLK, :] += has_prev * p_ref[BLK:, :]
        for k, d in enumerate(DILATIONS):
            @pl.when(n == k)
            def _(k=k, d=d):
                if k == 0:
                    acc[...] = p_scr[...]
                else:
                    _unpermute_rows(u_scr, p_scr, d)
                    acc[...] += u_scr[...]

        @pl.when(n == 2)
        def _():
            dy = acc[...]
            rot = a < 2
            o_ref[...] = (dy * jnp.where(rot, cos_ref[...], 1.0)
                          + _rot_half(dy * jnp.where(rot, sin_ref[...], 0.0))).astype(BF16)

    tab = pl.BlockSpec((s_len, LANES), lambda a, cb, n: (0, 0))
    return pl.pallas_call(
        body, grid=(3, 4, 3), name=name,
        in_specs=[pl.BlockSpec((None, None, s_len, LANES), lambda a, cb, n: (a, n, 0, cb)),
                  pl.BlockSpec((None, None, s_len, LANES), lambda a, cb, n: (jnp.maximum(a - 1, 0), n, 0, cb)),
                  tab, tab],
        out_specs=pl.BlockSpec((s_len, LANES), lambda a, cb, n: (0, 4 * a + cb)),
        out_shape=jax.ShapeDtypeStruct((s_len, 3 * DL_W), BF16),
        scratch_shapes=[pltpu.VMEM((s_len, LANES), F32)] * 3, compiler_params=_cp())(cur, prev, cos, sin)


XA_T = 512


def _xa_probs(q, k):
    s = _dot_nt(q, k) * (X_HD ** -0.5)
    e = jnp.exp(s - jnp.max(s, axis=1, keepdims=True))
    return e / jnp.sum(e, axis=1, keepdims=True)


def _xa_fwd(q, k, v, *, name):
    s_len, d = q.shape
    nm = k.shape[0]

    def body(q_ref, k_ref, v_ref, o_ref):
        for h in range(X_HEADS):
            cs = slice(h * X_HD, (h + 1) * X_HD)
            p = _xa_probs(q_ref[:, cs], k_ref[:, cs])
            o_ref[:, cs] = _dot(p.astype(BF16), v_ref[:, cs]).astype(BF16)

    row = pl.BlockSpec((XA_T, d), lambda i: (i, 0))
    full = pl.BlockSpec((nm, d), lambda i: (0, 0))
    return pl.pallas_call(body, grid=(s_len // XA_T,), name=name, in_specs=[row, full, full], out_specs=row,
                          out_shape=jax.ShapeDtypeStruct((s_len, d), BF16), compiler_params=_cp())(q, k, v)


def _xa_bwd(q, k, v, do, *, name, carry=None):
    s_len, d = q.shape
    nm = k.shape[0]

    def body(q_ref, k_ref, v_ref, do_ref, dq_ref, dk_ref, dv_ref):
        i = pl.program_id(0)
        for h in range(X_HEADS):
            cs = slice(h * X_HD, (h + 1) * X_HD)
            qh, kh, vh, doh = q_ref[:, cs], k_ref[:, cs], v_ref[:, cs], do_ref[:, cs]
            p = _xa_probs(qh, kh)
            dp = _dot_nt(doh, vh)
            ds = (p * (dp - jnp.sum(dp * p, axis=1, keepdims=True)) * (X_HD ** -0.5)).astype(BF16)
            dq_ref[:, cs] = _dot(ds, kh).astype(BF16)
            dkh, dvh = _dot_tn(ds, qh), _dot_tn(p.astype(BF16), doh)

            @pl.when(i == 0)
            def _(cs=cs, dkh=dkh, dvh=dvh):
                dk_ref[:, cs] = dkh
                dv_ref[:, cs] = dvh

            @pl.when(i > 0)
            def _(cs=cs, dkh=dkh, dvh=dvh):
                dk_ref[:, cs] += dkh
                dv_ref[:, cs] += dvh

    row = pl.BlockSpec((XA_T, d), lambda i: (i, 0))
    full = pl.BlockSpec((nm, d), lambda i: (0, 0))
    return _call(
        body, grid=(s_len // XA_T,), name=name, carry=carry, in_specs=[row, full, full, row],
        out_specs=[row, full, full],
        out_shape=[jax.ShapeDtypeStruct((s_len, d), BF16), jax.ShapeDtypeStruct((nm, d), F32),
                   jax.ShapeDtypeStruct((nm, d), F32)], args=(q, k, v, do))


FF_TM, FF_TN, FF_H = 512, 256, 8
GELU_K, GELU_C = 0.7978845608028654, 0.044715


FF_STRIP = 64


def _ff_conv(e_ref, w_ref, b_ref, rows, r0=0):
    return (w_ref[0:1, :] * e_ref[pl.ds(FF_H - 2 + r0, rows), :] + w_ref[1:2, :] * e_ref[pl.ds(FF_H - 1 + r0, rows), :]
            + w_ref[2:3, :] * e_ref[pl.ds(FF_H + r0, rows), :] + b_ref[...])


def _strips(total, size):
    return [(r0, min(size, total - r0)) for r0 in range(0, total, size)]


def _ff_gate_fwd(up, conv_w, conv_b, *, name, carry=None):
    s_len = up.shape[0]
    nj = D_FF // FF_TN

    def body(g_ref, v_ref, gp_ref, vp_ref, wg_ref, wv_ref, bg_ref, bv_ref, o_ref, eg, ev):
        i = pl.program_id(0)
        for e, cur, prev in ((eg, g_ref, gp_ref), (ev, v_ref, vp_ref)):
            e[0:FF_H, :] = jnp.where(i > 0, prev[...], 0.0)
            e[FF_H:, :] = cur[...]
        for r0, rows in _strips(FF_TM, FF_STRIP):
            gate = _ff_conv(eg, wg_ref, bg_ref, rows, r0)
            val = _ff_conv(ev, wv_ref, bv_ref, rows, r0)
            t = jnp.tanh(GELU_K * (gate + GELU_C * gate * gate * gate))
            o_ref[r0:r0 + rows, :] = (0.5 * gate * (1.0 + t) * val).astype(BF16)

    cur = lambda c0: pl.BlockSpec((FF_TM, FF_TN), lambda i, j: (i, c0 + j))
    prev = lambda c0: pl.BlockSpec((FF_H, FF_TN), lambda i, j: (jnp.maximum(i * (FF_TM // FF_H) - 1, 0), c0 + j))
    par = lambda r, c0: pl.BlockSpec((r, FF_TN), lambda i, j: (0, c0 + j))
    return _call(
        body, grid=(s_len // FF_TM, nj), name=name, carry=carry,
        in_specs=[cur(0), cur(nj), prev(0), prev(nj), par(3, 0), par(3, nj), par(1, 0), par(1, nj)],
        out_specs=[cur(0)], out_shape=[jax.ShapeDtypeStruct((s_len, D_FF), BF16)],
        scratch_shapes=[pltpu.VMEM((FF_TM + FF_H, FF_TN), F32)] * 2,
        args=(up, up, up, up, conv_w, conv_w, conv_b, conv_b))


def _ff_gate_bwd(up, dact, conv_w, conv_b, *, name, carry=None):
    s_len = up.shape[0]
    nj = D_FF // FF_TN
    last = s_len // FF_TM - 1
    ext = FF_TM + FF_H

    def body(g_ref, v_ref, gp_ref, vp_ref, gn_ref, vn_ref, da_ref, dan_ref, wg_ref, wv_ref, bg_ref, bv_ref,
             dg_ref, dv_ref, dw_ref, db_ref, eg, ev, sg, sv):
        i = pl.program_id(1)
        for e, cur, prev, nxt in ((eg, g_ref, gp_ref, gn_ref), (ev, v_ref, vp_ref, vn_ref)):
            e[0:FF_H, :] = jnp.where(i > 0, prev[...], 0.0)
            e[FF_H:FF_H + FF_TM, :] = cur[...]
            e[FF_H + FF_TM:, :] = nxt[...]
        for r0, rows in _strips(ext, FF_STRIP):
            gate = _ff_conv(eg, wg_ref, bg_ref, rows, r0)
            val = _ff_conv(ev, wv_ref, bv_ref, rows, r0)
            dact = da_ref[r0:r0 + rows, :] if r0 < FF_TM else jnp.where(i < last, dan_ref[...], 0.0)
            t = jnp.tanh(GELU_K * (gate + GELU_C * gate * gate * gate))
            half = 0.5 * (1.0 + t)
            dgelu = half + 0.5 * gate * (1.0 - t * t) * GELU_K * (1.0 + 3.0 * GELU_C * gate * gate)
            sg[r0:r0 + rows, :] = dact * val * dgelu
            sv[r0:r0 + rows, :] = dact * (gate * half)
        for part, (s, e, w_ref, out) in enumerate(((sg, eg, wg_ref, dg_ref), (sv, ev, wv_ref, dv_ref))):
            taps, bias = [jnp.zeros((8, FF_TN), F32)] * 3, jnp.zeros((8, FF_TN), F32)
            for r0, rows in _strips(FF_TM, FF_STRIP):
                d0 = s[pl.ds(r0, rows), :]
                out[r0:r0 + rows, :] = (w_ref[2:3, :] * d0 + w_ref[1:2, :] * s[pl.ds(r0 + 1, rows), :]
                                        + w_ref[0:1, :] * s[pl.ds(r0 + 2, rows), :]).astype(BF16)
                taps = [taps[k] + _rowsum8(d0 * e[pl.ds(FF_H - 2 + k + r0, rows), :]) for k in range(3)]
                bias = bias + _rowsum8(d0)
            _acc_out(dw_ref.at[part], i, jnp.concatenate(taps, axis=0))
            _acc_out(db_ref.at[part], i, bias)

    cur = lambda c0: pl.BlockSpec((FF_TM, FF_TN), lambda j, i: (i, c0 + j))
    prev = lambda c0: pl.BlockSpec((FF_H, FF_TN), lambda j, i: (jnp.maximum(i * (FF_TM // FF_H) - 1, 0), c0 + j))
    nxt = lambda c0: pl.BlockSpec(
        (FF_H, FF_TN), lambda j, i: (jnp.minimum((i + 1) * (FF_TM // FF_H), s_len // FF_H - 1), c0 + j))
    par = lambda r, c0: pl.BlockSpec((r, FF_TN), lambda j, i: (0, c0 + j))
    return _call(
        body, grid=(nj, s_len // FF_TM), name=name, carry=carry,
        in_specs=[cur(0), cur(nj), prev(0), prev(nj), nxt(0), nxt(nj), cur(0), nxt(0),
                  par(3, 0), par(3, nj), par(1, 0), par(1, nj)],
        out_specs=[cur(0), cur(0), pl.BlockSpec((2, 24, FF_TN), lambda j, i: (0, 0, j)),
                   pl.BlockSpec((2, 8, FF_TN), lambda j, i: (0, 0, j))],
        out_shape=[jax.ShapeDtypeStruct((s_len, D_FF), BF16), jax.ShapeDtypeStruct((s_len, D_FF), BF16),
                   jax.ShapeDtypeStruct((2, 24, D_FF), F32), jax.ShapeDtypeStruct((2, 8, D_FF), F32)],
        scratch_shapes=[pltpu.VMEM((FF_TM + 2 * FF_H, FF_TN), F32)] * 2 + [pltpu.VMEM((ext, FF_TN), F32)] * 2,
        args=(up, up, up, up, up, up, dact, dact, conv_w, conv_w, conv_b, conv_b))


def _place():
    x, y, c = lax.axis_index("x"), lax.axis_index("y"), lax.axis_index("c")
    return x, y, c, [(1 - x, y), (x, 1 - y), (1 - x, 1 - y)]


def _remote(src, dst, send_sem, recv_sem, dev):
    return pltpu.make_async_remote_copy(src_ref=src, dst_ref=dst, send_sem=send_sem, recv_sem=recv_sem,
                                        device_id=dev, device_id_type=MESH)


_ANY = pl.BlockSpec(memory_space=pl.ANY)


N_SEMS = 8
SEM_BASE_2 = 4


class _Exchange:
    def __init__(self, operands, out_shapes, start, wait, aliases=None):
        self.operands, self.out_shapes, self.start, self.wait = list(operands), list(out_shapes), start, wait
        self.aliases = aliases or {}


def _sem_scratch():
    return [pltpu.SemaphoreType.DMA((N_SEMS,)), pltpu.SemaphoreType.DMA((N_SEMS,)), pltpu.SemaphoreType.DMA]


def _run_exchange(ex, *, name):
    k, n = len(ex.operands), len(ex.out_shapes)

    def body(*refs):
        ins, outs, sems = refs[:k], refs[k:k + n], refs[k + n:]
        ex.start(ins, outs, *sems)
        ex.wait(ins, outs, *sems)

    return pl.pallas_call(body, name=name, in_specs=[_ANY] * k, out_specs=[_ANY] * n, out_shape=ex.out_shapes,
                          scratch_shapes=_sem_scratch(), input_output_aliases=ex.aliases,
                          compiler_params=_cp(16))(*ex.operands)


def _call(body, *, grid, in_specs, out_specs, out_shape, args, name, scratch_shapes=(), vmem_mb=48, carry=None):
    scratch_shapes = list(scratch_shapes)
    if carry is None:
        return pl.pallas_call(body, grid=grid, name=name, in_specs=in_specs, out_specs=out_specs, out_shape=out_shape,
                              scratch_shapes=scratch_shapes, compiler_params=_cp(vmem_mb))(*args)
    n_in, n_out, n_scr = len(in_specs), len(out_shape), len(scratch_shapes)
    k_in, k_out = len(carry.operands), len(carry.out_shapes)

    def wrapped(*refs):
        ins, refs = refs[:n_in], refs[n_in:]
        cin, refs = refs[:k_in], refs[k_in:]
        outs, refs = refs[:n_out], refs[n_out:]
        cout, refs = refs[:k_out], refs[k_out:]
        scratch, sems = refs[:n_scr], refs[n_scr:]
        ids = [pl.program_id(a) for a in range(len(grid))]
        first = functools.reduce(jnp.logical_and, [i == 0 for i in ids])
        last = functools.reduce(jnp.logical_and, [i == g - 1 for i, g in zip(ids, grid)])

        @pl.when(first)
        def _():
            carry.start(cin, cout, *sems)

        body(*ins, *outs, *scratch)

        @pl.when(last)
        def _():
            carry.wait(cin, cout, *sems)

    aliases = {n_in + i: n_out + o for i, o in carry.aliases.items()}
    return pl.pallas_call(
        wrapped, grid=grid, name=name, in_specs=list(in_specs) + [_ANY] * k_in,
        out_specs=list(out_specs) + [_ANY] * k_out, out_shape=list(out_shape) + carry.out_shapes,
        scratch_shapes=scratch_shapes + _sem_scratch(), input_output_aliases=aliases,
        compiler_params=_cp(vmem_mb))(*args, *carry.operands)


def _half_rows(ref_rows, c):
    half = ref_rows // 2
    return pl.ds(c * half, half)


def _ex_join(a, b):
    ka, na = len(a.operands), len(a.out_shapes)

    def start(ins, outs, *sems):
        a.start(ins[:ka], outs[:na], *sems)
        b.start(ins[ka:], outs[na:], *sems)

    def wait(ins, outs, *sems):
        a.wait(ins[:ka], outs[:na], *sems)
        b.wait(ins[ka:], outs[na:], *sems)

    aliases = dict(a.aliases)
    aliases.update({ka + i: na + o for i, o in b.aliases.items()})
    return _Exchange(a.operands + b.operands, a.out_shapes + b.out_shapes, start, wait, aliases)


def _ex_gather(pack, r0, rl, base=0):
    def copies(ins, outs, send, recv):
        x, y, c, chips = _place()
        rows = _half_rows(rl, c)
        src = ins[0].at[pl.ds(r0 + c * (rl // 2), rl // 2)]
        sends = [_remote(src, outs[0].at[2 * x + y, rows], send.at[base + k], recv.at[base + k], (px, py, c))
                 for k, (px, py) in enumerate(chips)]
        lands = [_remote(src, outs[0].at[2 * px + py, rows], send.at[base + k], recv.at[base + k], (px, py, c))
                 for k, (px, py) in enumerate(chips)]
        return sends, lands

    def mine(ins, outs, local):
        x, y, _, _ = _place()
        return pltpu.make_async_copy(ins[0].at[pl.ds(r0, rl)], outs[0].at[2 * x + y], local)

    def start(ins, outs, send, recv, local):
        mine(ins, outs, local).start()
        for cp in copies(ins, outs, send, recv)[0]:
            cp.start()

    def wait(ins, outs, send, recv, local):
        sends, lands = copies(ins, outs, send, recv)
        for cp in lands:
            cp.wait_recv()
        for cp in sends:
            cp.wait_send()
        mine(ins, outs, local).wait()

    return _Exchange([pack], [jax.ShapeDtypeStruct((4, rl, pack.shape[1]), pack.dtype)], start, wait)


def _ex_gather_forward(g, base=0):
    rl = g.shape[1]

    def copies(outs, send, recv):
        x, y, c, chips = _place()
        slabs = [(outs[0].at[2 * px + py, _half_rows(rl, c)], outs[0].at[2 * px + py, _half_rows(rl, 1 - c)])
                 for px, py in chips]
        sends = [_remote(a, a, send.at[base + k], recv.at[base + k], (x, y, 1 - c)) for k, (a, _) in enumerate(slabs)]
        lands = [_remote(b, b, send.at[base + k], recv.at[base + k], (x, y, 1 - c)) for k, (_, b) in enumerate(slabs)]
        return sends, lands

    def start(ins, outs, send, recv, local):
        for cp in copies(outs, send, recv)[0]:
            cp.start()

    def wait(ins, outs, send, recv, local):
        sends, lands = copies(outs, send, recv)
        for cp in lands:
            cp.wait_recv()
        for cp in sends:
            cp.wait_send()

    return _Exchange([g], [jax.ShapeDtypeStruct(g.shape, g.dtype)], start, wait, aliases={0: 0})


def _ex_swap_halves(gw, base=0):
    nb, rl, d = gw.shape

    def copies(ins, outs, send, recv):
        x, y, c, _ = _place()
        return [_remote(ins[0].at[j, _half_rows(rl, 1 - c)], outs[0].at[j], send.at[base + j], recv.at[base + j],
                        (x, y, 1 - c)) for j in range(nb)]

    def start(ins, outs, send, recv, local):
        for cp in copies(ins, outs, send, recv):
            cp.start()

    def wait(ins, outs, send, recv, local):
        for cp in copies(ins, outs, send, recv):
            cp.wait()

    return _Exchange([gw], [jax.ShapeDtypeStruct((nb, rl // 2, d), gw.dtype)], start, wait)


def _chip_sum(gw, got, c_arr, *, name):
    nchip, half, d = got.shape
    tr = _tile(half, 512)

    def body(c_ref, a_ref, b_ref, o32_ref, o16_ref):
        s = a_ref[...] + b_ref[...]
        o32_ref[...] = s
        o16_ref[...] = s.astype(BF16)

    blk = pl.BlockSpec((None, tr, d), lambda j, i, c_ref: (j, i, 0))
    return pl.pallas_call(
        body, name=name,
        grid_spec=pltpu.PrefetchScalarGridSpec(
            num_scalar_prefetch=1, grid=(nchip, half // tr),
            in_specs=[pl.BlockSpec((None, tr, d), lambda j, i, c_ref: (j, c_ref[0] * (half // tr) + i, 0)), blk],
            out_specs=[blk, blk]),
        out_shape=[jax.ShapeDtypeStruct((nchip, half, d), F32), jax.ShapeDtypeStruct((nchip, half, d), BF16)],
        compiler_params=_cp())(c_arr, gw, got)


def _ex_scatter(s16, base=0):
    def copies(ins, outs, send, recv):
        x, y, c, chips = _place()
        return [_remote(ins[0].at[2 * px + py], outs[0].at[k], send.at[base + k], recv.at[base + k], (px, py, c))
                for k, (px, py) in enumerate(chips)]

    def start(ins, outs, send, recv, local):
        for cp in copies(ins, outs, send, recv):
            cp.start()

    def wait(ins, outs, send, recv, local):
        for cp in copies(ins, outs, send, recv):
            cp.wait()

    return _Exchange([s16], [jax.ShapeDtypeStruct((3,) + s16.shape[1:], s16.dtype)], start, wait)


def _mesh_sum(s32, got, j_arr, *, name):
    _, rl, d = s32.shape
    tr = _tile(rl, 512)

    def body(j_ref, a_ref, b_ref, o_ref):
        o_ref[...] = ((a_ref[...] + b_ref[0].astype(F32)) + b_ref[1].astype(F32)) + b_ref[2].astype(F32)

    return pl.pallas_call(
        body, name=name,
        grid_spec=pltpu.PrefetchScalarGridSpec(
            num_scalar_prefetch=1, grid=(rl // tr,),
            in_specs=[pl.BlockSpec((None, tr, d), lambda i, j_ref: (j_ref[0], i, 0)),
                      pl.BlockSpec((3, tr, d), lambda i, j_ref: (0, i, 0))],
            out_specs=pl.BlockSpec((tr, d), lambda i, j_ref: (i, 0))),
        out_shape=jax.ShapeDtypeStruct((rl, d), F32), compiler_params=_cp())(j_arr, s32, got)


def _ex_share_halves(ghalf):
    half, d = ghalf.shape

    def copies(ins, outs, send, recv, local):
        x, y, c, _ = _place()
        there = outs[0].at[_half_rows(2 * half, c)]
        back = outs[0].at[_half_rows(2 * half, 1 - c)]
        return (_remote(ins[0], there, send.at[0], recv.at[0], (x, y, 1 - c)),
                _remote(ins[0], back, send.at[0], recv.at[0], (x, y, 1 - c)), pltpu.make_async_copy(ins[0], there, local))

    def start(ins, outs, send, recv, local):
        out, _, mine = copies(ins, outs, send, recv, local)
        mine.start()
        out.start()

    def wait(ins, outs, send, recv, local):
        out, back, mine = copies(ins, outs, send, recv, local)
        back.wait_recv()
        out.wait_send()
        mine.wait()

    return _Exchange([ghalf], [jax.ShapeDtypeStruct((2 * half, d), ghalf.dtype)], start, wait)


class _ReduceScatter:
    def __init__(self, gw, c_arr, j_arr, tag):
        self.gw, self.c_arr, self.j_arr, self.tag = gw, c_arr, j_arr, tag

    def swap(self, base=0):
        return _ex_swap_halves(self.gw, base)

    def after_swap(self, got, base=0):
        self.s32, s16 = _chip_sum(self.gw, got, self.c_arr, name=f"rs_chip_sum{self.tag}")
        return _ex_scatter(s16, base)

    def after_scatter(self, got16):
        ghalf = _mesh_sum(self.s32, got16, self.j_arr, name=f"rs_mesh_sum{self.tag}")
        return _run_exchange(_ex_share_halves(ghalf), name=f"rs_share{self.tag}")[0]

    def run(self):
        got, = _run_exchange(self.swap(), name=f"rs_swap{self.tag}")
        got16, = _run_exchange(self.after_swap(got), name=f"rs_scatter{self.tag}")
        return self.after_scatter(got16)


def _all_reduce_small(vec, *, name):
    rows, d = vec.shape

    def body(x_ref, o_ref, gat, send_sems, recv_sems, local_sem):
        x, y, c, chips = _place()
        me, sibling = (x, y, c), (x, y, 1 - c)

        def slot(px, py, pc):
            return gat.at[4 * px + 2 * py + pc]

        def copy(k, block, to, src=None):
            return _remote(slot(*block) if src is None else src, slot(*block), send_sems.at[k], recv_sems.at[k], to)

        mine = pltpu.make_async_copy(x_ref, slot(*me), local_sem)
        mine.start()
        first = [copy(0, me, sibling, src=x_ref)]
        first += [copy(1 + j, me, (*chip, c), src=x_ref) for j, chip in enumerate(chips)]
        for cp in first:
            cp.start()
        passed = [copy(4 + j, (*chip, c), sibling) for j, chip in enumerate(chips)]
        for j, chip in enumerate(chips):
            copy(1 + j, (*chip, c), me).wait_recv()
            passed[j].start()
        copy(0, sibling, me).wait_recv()
        for j, chip in enumerate(chips):
            copy(4 + j, (*chip, 1 - c), me).wait_recv()
        for cp in first + passed:
            cp.wait_send()
        mine.wait()
        acc = gat[0]
        for dev in range(1, 8):
            acc = acc + gat[dev]
        o_ref[...] = acc

    vm = pl.BlockSpec(memory_space=pltpu.VMEM)
    return pl.pallas_call(
        body, name=name, in_specs=[vm], out_specs=vm, out_shape=jax.ShapeDtypeStruct((rows, d), F32),
        scratch_shapes=[pltpu.VMEM((8, rows, d), F32), pltpu.SemaphoreType.DMA((7,)), pltpu.SemaphoreType.DMA((7,)),
                        pltpu.SemaphoreType.DMA],
        compiler_params=_cp(32))(vec)


COL_SHARDED = ("w_in", "ffn_w_up")


def _to_pack_rows(name, shard):
    return shard.reshape(-1, D_MODEL)


def _full_from_blocks(name, blocks):
    rows = blocks.shape[1]
    if name in COL_SHARDED:
        return blocks.reshape(4, D_MODEL, rows).transpose(1, 0, 2).reshape(D_MODEL, 4 * rows)
    return blocks.reshape(4 * rows, D_MODEL)


def _blocks_from_full(name, full):
    if name in COL_SHARDED:
        cols = full.shape[1] // 4
        return full.reshape(D_MODEL, 4, cols).transpose(1, 0, 2).reshape(4, cols, D_MODEL)
    return full.reshape(4, full.shape[0] // 4, D_MODEL)


def _row(v):
    return v.reshape(1, -1)


SMALL = (("mix_norm_pre", (1024,), None), ("cv_w", (31, 256), 1), ("cv_b", (256,), None), ("cv_ln_g", (256,), None),
         ("cv_ln_b", (256,), None), ("cv_pw_w", (256, 256), 0), ("cv_pw_b", (256,), None),
         ("mix_norm_post", (1024,), None), ("x_norm_pre", (1024,), None), ("mem_norm", (1024,), None),
         ("x_norm_post", (1024,), None), ("ffn_norm_pre", (1024,), None), ("ffn_conv_w", (3, 5632), 1),
         ("ffn_conv_b", (5632,), None), ("ffn_norm_post", (1024,), None))
BIG = tuple(n for n, _ in PACK_ROWS)
WEIGHT_ORDER = ("mix_norm_pre", "w_in", "cv_w", "cv_b", "cv_ln_g", "cv_ln_b", "cv_pw_w", "cv_pw_b", "w_out",
                "mix_norm_post", "x_norm_pre", "mem_norm", "x_wq", "x_wk", "x_wv", "x_wo", "x_norm_post",
                "ffn_norm_pre", "ffn_w_up", "ffn_conv_w", "ffn_conv_b", "ffn_w_down", "ffn_norm_post")


def _flat_rows(parts):
    v = jnp.concatenate([p.reshape(-1) for p in parts])
    rows = -(-v.shape[0] // (8 * D_MODEL)) * 8
    return jnp.pad(v, (0, rows * D_MODEL - v.shape[0])).reshape(rows, D_MODEL)


def _small_to_rows(blocks):
    v = jnp.concatenate([b.reshape(-1) for b in blocks])
    return jnp.pad(v, (0, SMALL_ROWS * D_MODEL - v.shape[0])).reshape(SMALL_ROWS, D_MODEL)


def _small_from_rows(rows):
    flat, out, off = rows.reshape(-1), [], 0
    for _, shape, _ in SHARDED_SMALL:
        size = int(np.prod(shape))
        out.append(flat[off:off + size].reshape(shape))
        off += size
    return out


def _chip_block(full, j, shape, axis):
    return lax.slice_in_dim(full, j * shape[axis], (j + 1) * shape[axis], axis=axis)


REST_GROUP = ("w_in", "w_out")
XA_GROUP = ("x_wq", "x_wk", "x_wv", "x_wo")
FFN_GROUP = ("ffn_w_up", "ffn_w_down")


class _Weights:
    FIRST = (0, 768)
    OWN = ((768, 1024), (1792, 1664), (3456, 704))
    NEXT = ((0, 1024), (1024, 1024), (2048, 1408), (3456, 704))
    SLOTS = ("mix_in", "sb_fwd", "dl_fwd", "dl_mix", "ffn_up", "ffn_gate", "ffn_down")

    def __init__(self, packs):
        self.packs, self.pieces, self.landed, self.plan = packs, {}, None, {}
        for slot, piece in zip(self.SLOTS[:3], self.OWN):
            self.plan[(0, slot)] = (0,) + piece
        for l in range(len(packs) - 1):
            for slot, piece in zip(self.SLOTS[3:], self.NEXT):
                self.plan[(l, slot)] = (l + 1,) + piece
        first = _run_exchange(_ex_gather(packs[0], *self.FIRST), name="gather_first")[0]
        self.pieces[(0,) + self.FIRST] = _run_exchange(_ex_gather_forward(first), name="gather_first_forward")[0]

    def ride(self, layer, slot, call):
        start, todo, ex = self.plan.get((layer, slot)), [], None
        if start is not None:
            ex = _ex_gather(self.packs[start[0]], start[1], start[2])
            todo.append(("landed", start))
        if self.landed is not None:
            key, buf = self.landed
            forward = _ex_gather_forward(buf, SEM_BASE_2 if ex is not None else 0)
            ex = forward if ex is None else _ex_join(ex, forward)
            todo.append(("piece", key))
            self.landed = None
        outs = list(call(carry=ex))
        n = len(outs) - len(todo)
        for (kind, key), buf in zip(todo, outs[n:]):
            if kind == "landed":
                self.landed = (key, buf)
            else:
                self.pieces[key] = buf
        return outs[:n]

    def rows_of(self, layer, name):
        off = 0
        for n, rows in WEIGHT_PACK:
            if n == name:
                break
            off += rows
        for (l, r0, nrows), buf in self.pieces.items():
            if l == layer and r0 <= off < r0 + nrows:
                return buf[:, off - r0:off - r0 + rows, :]
        raise KeyError(f"{name} of layer {layer} is not gathered yet")

    def weight(self, layer, name):
        return _full_from_blocks(name, self.rows_of(layer, name))

    def small(self, layer):
        planes = lax.bitcast_convert_type(self.rows_of(layer, "small").astype(jnp.bfloat16), jnp.uint16)
        planes = planes.astype(jnp.uint32)
        bits = (planes[:, :SMALL_ROWS] << 16) | planes[:, SMALL_ROWS:]
        per_chip = [_small_from_rows(r) for r in lax.bitcast_convert_type(bits, F32)]
        return {n: jnp.concatenate([blocks[k] for blocks in per_chip], axis=axis)
                for k, (n, _, axis) in enumerate(SHARDED_SMALL)}


class _Params:
    def __init__(self, weights, layer, small):
        self.weights, self.layer, self.small, self.cache = weights, layer, small, {}

    def __getitem__(self, name):
        if name in self.small:
            return self.small[name]
        if name not in self.cache:
            if name in [n for n, _, _ in SHARDED_SMALL]:
                self.cache.update(self.weights.small(self.layer))
            else:
                self.cache[name] = self.weights.weight(self.layer, name)
        return self.cache[name]


def _layer_fwd(h0, mem, p, cos, sin, tag, ride):
    sv = {"h0": h0}
    n1, u = ride("mix_in", functools.partial(_rms_mm, h0, _row(p["mix_norm_pre"]), p["w_in"], tm=1024, tn=1408,
                                             out_dtype=F32, name=f"mix_in{tag}"))
    a_out, = ride("sb_fwd", functools.partial(_sb_fwd, u, name=f"sb_fwd{tag}"))
    b_out, c = _cv_fwd(u, p["cv_w"], _row(p["cv_b"]), _row(p["cv_ln_g"]), _row(p["cv_ln_b"]),
                       p["cv_pw_w"].astype(BF16), _row(p["cv_pw_b"]), name=f"cv_fwd{tag}")
    qkv = _rope_perm(u, cos, sin, name=f"rope_perm{tag}")
    o_p, l_p = ride("dl_fwd", functools.partial(_dl_fwd, qkv, name=f"dl_fwd{tag}"))
    c_out, o_dl, lse = ride("dl_mix", functools.partial(_dl_mix, o_p, l_p, name=f"dl_mix{tag}"))
    cat = jnp.concatenate([a_out, b_out, c_out], axis=1)
    y1, h1 = _mm_post(cat, p["w_out"], h0, _row(p["mix_norm_post"]), tm=512, name=f"mix_out{tag}")
    sv.update(n1=n1, u=u, c=c, qkv=qkv, o_dl=o_dl, lse=lse, cat=cat, y1=y1, h1=h1)

    n2, q = _rms_mm(h1, _row(p["x_norm_pre"]), p["x_wq"], tm=512, tn=1024, out_dtype=BF16, name=f"xa_q{tag}")
    wkv = jnp.concatenate([p["x_wk"], p["x_wv"]], axis=1)
    mem_n, kv = _rms_mm(mem, _row(p["mem_norm"]), wkv, tm=mem.shape[0], tn=1024, out_dtype=BF16, name=f"xa_kv{tag}")
    k, v = kv[:, :D_MODEL], kv[:, D_MODEL:]
    o_x = _xa_fwd(q, k, v, name=f"xa_fwd{tag}")
    y2, h2 = _mm_post(o_x, p["x_wo"], h1, _row(p["x_norm_post"]), tm=512, name=f"xa_out{tag}")
    sv.update(n2=n2, q=q, mem_n=mem_n, k=k, v=v, o_x=o_x, y2=y2, h2=h2, wkv=wkv)

    n3, up = ride("ffn_up", functools.partial(_rms_mm, h2, _row(p["ffn_norm_pre"]), p["ffn_w_up"], tm=1024, tn=1408,
                                              out_dtype=F32, name=f"ffn_up{tag}"))
    act, = ride("ffn_gate", functools.partial(_ff_gate_fwd, up, p["ffn_conv_w"], _row(p["ffn_conv_b"]),
                                              name=f"ffn_gate{tag}"))
    y3, h3 = ride("ffn_down", functools.partial(_mm_post, act, p["ffn_w_down"], h2, _row(p["ffn_norm_post"]), tm=512,
                                                name=f"ffn_down{tag}"))
    sv.update(n3=n3, up=up, act=act, y3=y3)
    return h3, sv


def _layer_bwd(dh3, mem, p, sv, cos, sin, tag, riding, new_rs):
    g = {}
    s8 = lambda part: part.sum(axis=0)
    rode = None

    dy3, dgp = _rms_bwd(sv["y3"], _row(p["ffn_norm_post"]), dh3, None, out_dtype=BF16, tm=512, name=f"ffn_post_b{tag}")
    g["ffn_norm_post"] = s8(dgp)
    dact = _mm_nt(dy3, p["ffn_w_down"], tm=512, tn=1408, out_dtype=F32, name=f"ffn_down_bx{tag}")
    g["ffn_w_down"] = _mm_tn(sv["act"], dy3, tk=1408, tn=1024, tm=2048, name=f"ffn_down_bw{tag}")
    dgu, dvu, dcw, dcb, *got = _ff_gate_bwd(sv["up"], dact, p["ffn_conv_w"], _row(p["ffn_conv_b"]),
                                            name=f"ffn_gate_b{tag}", carry=riding.swap() if riding else None)
    scatter = riding.after_swap(got[0]) if riding else None
    g["ffn_conv_w"] = jnp.concatenate([dcw[0], dcw[1]], axis=1).reshape(3, 8, 2 * D_FF).sum(axis=1)
    g["ffn_conv_b"] = jnp.concatenate([dcb[0], dcb[1]], axis=1).sum(axis=0)
    dup = jnp.concatenate([dgu, dvu], axis=1)
    dn3 = _mm_nt(dup, p["ffn_w_up"], tm=256, tn=512, out_dtype=F32, name=f"ffn_up_bx{tag}")
    g["ffn_w_up"] = _mm_tn(sv["n3"], dup, tk=512, tn=1408, tm=2048, name=f"ffn_up_bw{tag}")
    ffn_rs = new_rs(FFN_GROUP, g, f"{tag}_ffn")
    dh2, dgp = _rms_bwd(sv["h2"], _row(p["ffn_norm_pre"]), dn3, dh3, out_dtype=F32, tm=512, name=f"ffn_pre_b{tag}")
    g["ffn_norm_pre"] = s8(dgp)

    dy2, dgp = _rms_bwd(sv["y2"], _row(p["x_norm_post"]), dh2, None, out_dtype=BF16, tm=512, name=f"xa_post_b{tag}")
    g["x_norm_post"] = s8(dgp)
    do_x = _mm_nt(dy2, p["x_wo"], tm=512, tn=1024, out_dtype=BF16, name=f"xa_out_bx{tag}")
    g["x_wo"] = _mm_tn(sv["o_x"], dy2, tk=512, tn=1024, tm=2048, name=f"xa_out_bw{tag}")
    dq, dk, dv, got = _xa_bwd(sv["q"], sv["k"], sv["v"], do_x, name=f"xa_bwd{tag}", carry=ffn_rs.swap())
    ffn_scatter = ffn_rs.after_swap(got)
    dn2 = _mm_nt(dq, p["x_wq"], tm=512, tn=1024, out_dtype=F32, name=f"xa_q_bx{tag}")
    g["x_wq"] = _mm_tn(sv["n2"], dq, tk=512, tn=1024, tm=2048, name=f"xa_q_bw{tag}")
    dkv = jnp.concatenate([dk, dv], axis=1).astype(BF16)
    nm = mem.shape[0]
    dmem_n = _mm_nt(dkv, sv["wkv"], tm=nm, tn=1024, out_dtype=F32, name=f"xa_kv_bx{tag}")
    dwkv = _mm_tn(sv["mem_n"], dkv, tk=512, tn=2048, tm=nm, name=f"xa_kv_bw{tag}")
    g["x_wk"], g["x_wv"] = dwkv[:, :D_MODEL], dwkv[:, D_MODEL:]
    _, dgp = _rms_bwd(mem, _row(p["mem_norm"]), dmem_n, None, out_dtype=BF16, tm=nm, name=f"xa_mem_b{tag}")
    g["mem_norm"] = s8(dgp)
    xa_rs = new_rs(XA_GROUP, g, f"{tag}_xa")
    dh1, dgp, got = _rms_bwd(sv["h1"], _row(p["x_norm_pre"]), dn2, dh2, out_dtype=F32, tm=512, name=f"xa_pre_b{tag}",
                             carry=xa_rs.swap())
    xa_scatter = xa_rs.after_swap(got, SEM_BASE_2 if riding else 0)
    g["x_norm_pre"] = s8(dgp)

    dy1, dgp = _rms_bwd(sv["y1"], _row(p["mix_norm_post"]), dh1, None, out_dtype=BF16, tm=512, name=f"mix_post_b{tag}")
    g["mix_norm_post"] = s8(dgp)
    dcat = _mm_nt(dy1, p["w_out"], tm=512, tn=1024, out_dtype=F32, name=f"mix_out_bx{tag}")
    g["w_out"] = _mm_tn(sv["cat"], dy1, tk=512, tn=1024, tm=2048, name=f"mix_out_bw{tag}")
    u = sv["u"]
    dq_sb, dk_sb, dv_sb, *got = _sb_bwd(u, dcat, name=f"sb_bwd{tag}",
                                        carry=_ex_join(scatter, xa_scatter) if riding else xa_scatter)
    if riding:
        rode = riding.after_scatter(got[0])
    xa_rows = xa_rs.after_scatter(got[-1])
    pw_b16 = p["cv_pw_w"].astype(BF16)
    dc, dpw, vec = _cv_bwd_local(sv["c"], dcat, _row(p["cv_ln_g"]), _row(p["cv_ln_b"]), pw_b16, name=f"cv_bwd_a{tag}")
    g["cv_pw_w"] = dpw
    vec = vec.reshape(3, 8, CV_W).sum(axis=1)
    g["cv_pw_b"], g["cv_ln_g"], g["cv_ln_b"] = vec[0], vec[1], vec[2]
    du_cv, dcw, dcb = _cv_bwd_conv(u, dc, p["cv_w"], name=f"cv_bwd_b{tag}")
    g["cv_w"] = dcw.reshape(CV_K, 8, CV_W).sum(axis=1)
    g["cv_b"] = dcb.sum(axis=0)
    dop, stats = _dl_bwd_prep(dcat, sv["o_dl"], sv["lse"], name=f"dl_prep_b{tag}")
    cur, prev, got = _dl_bwd(sv["qkv"], dop, stats, name=f"dl_bwd{tag}", carry=ffn_scatter)
    ffn_rows = ffn_rs.after_scatter(got)
    du_dl = _dl_bwd_finish(cur, prev, cos, sin, name=f"dl_fin_b{tag}")
    du = jnp.concatenate([dq_sb.astype(BF16), dk_sb.astype(BF16), dv_sb.astype(BF16), du_cv, du_dl], axis=1)
    dn1 = _mm_nt(du, p["w_in"], tm=512, tn=512, out_dtype=F32, name=f"mix_in_bx{tag}")
    g["w_in"] = _mm_tn(sv["n1"], du, tk=512, tn=1408, tm=2048, name=f"mix_in_bw{tag}")
    dh0, dgp = _rms_bwd(sv["h0"], _row(p["mix_norm_pre"]), dn1, dh1, out_dtype=F32, tm=512, name=f"mix_pre_b{tag}")
    g["mix_norm_pre"] = s8(dgp)
    return dh0, g, (xa_rows, ffn_rows), rode


def _step(x, mem, positions, loss_target, w, m, v):
    depth = w["w_in"].shape[0]
    xi, yi, ci = lax.axis_index("x"), lax.axis_index("y"), lax.axis_index("c")
    chip = 2 * xi + yi
    h = x[0]
    mem0 = mem[0]
    s_len = h.shape[0]

    def pack_rows(n, l):
        if n == "small":
            bits = lax.bitcast_convert_type(_small_to_rows([w[name][l] for name, _, _ in SHARDED_SMALL]), jnp.uint32)
            planes = [(bits >> 16).astype(jnp.uint16), (bits & 0xFFFF).astype(jnp.uint16)]
            return jnp.concatenate([lax.bitcast_convert_type(p, jnp.bfloat16) for p in planes], axis=0)
        return _to_pack_rows(n, w[n][l]).astype(BF16)

    packs = [jnp.concatenate([pack_rows(n, l) for n, _ in WEIGHT_PACK], axis=0) for l in range(depth)]
    weights = _Weights(packs)
    params = [_Params(weights, l, {n: w[n][l] for n, _, axis in SMALL if axis is None}) for l in range(depth)]

    inv_freq = ROPE_THETA ** (-jnp.arange(HD // 2, dtype=F32) / (HD // 2))
    cos, sin = _rope_tables(positions.reshape(s_len, 1), jnp.tile(inv_freq, 4).reshape(1, LANES), name="rope_tables")

    saved = []
    for l in range(depth):
        h, sv = _layer_fwd(h, mem0, params[l], cos, sin, f"_l{l}", functools.partial(weights.ride, l))
        saved.append(sv)
    dh, sq = _loss_grad(h, loss_target[0], tm=512, name="loss_grad")
    loss = lax.psum(0.5 * jnp.sum(sq) / D_MODEL, ("x", "y", "c"))

    c_arr, j_arr = jnp.reshape(ci, (1,)).astype(jnp.int32), jnp.reshape(chip, (1,)).astype(jnp.int32)

    def new_rs(names, g, tag):
        blocks = [_blocks_from_full(n, g[n]) for n in names]
        if names is REST_GROUP:
            blocks.append(jnp.stack([_small_to_rows([_chip_block(g[n], j, shape, axis) for n, shape, axis in SHARDED_SMALL])
                                     for j in range(4)]))
        return _ReduceScatter(jnp.concatenate(blocks, axis=1), c_arr, j_arr, tag)

    grads, later_rows, rest_rows, pending = [None] * depth, [None] * depth, [None] * depth, None
    for l in reversed(range(depth)):
        dh, grads[l], later_rows[l], rode = _layer_bwd(dh, mem0, params[l], saved[l], cos, sin, f"_l{l}", pending, new_rs)
        if pending is not None:
            rest_rows[l + 1] = rode
        pending = new_rs(REST_GROUP, grads[l], f"_l{l}_rest")
    grad_x = dh[None]

    out_g, out_d, out_m, out_v = {}, {}, {}, {}
    pack_off, off = {}, 0
    for n, rows in PACK_ROWS:
        pack_off[n] = (off, rows)
        off += rows

    def reduced(l, n):
        start, rows = pack_off[n]
        for names, block in ((REST_GROUP, rest_rows[l]), (XA_GROUP, later_rows[l][0]), (FFN_GROUP, later_rows[l][1])):
            if n in names:
                return block[start - pack_off[names[0]][0]:][:rows]

    def update(n, carry=None):
        shard_shape = w[n].shape
        g_n = jnp.stack([reduced(l, n) for l in range(depth)]).reshape(shard_shape)
        flat = lambda a: a.reshape(-1, shard_shape[-1])
        d_n, m_n, v_n, *rode = _adamw(flat(w[n]), flat(g_n), flat(m[n]), flat(v[n]), name=f"adamw_{n}", carry=carry)
        out_g[n], out_d[n], out_m[n], out_v[n] = g_n, d_n.reshape(shard_shape), m_n.reshape(shard_shape), v_n.reshape(shard_shape)
        return rode

    rest_rows[0] = pending.run()
    for n, _ in PACK_ROWS:
        update(n)

    g_small = _all_reduce_small(_flat_rows([grads[l][n] for l in range(depth) for n, _, axis in SMALL if axis is None]),
                                name="all_reduce_small_grads").reshape(-1)
    local_g, off = {}, 0
    for l in range(depth):
        for n, shape, axis in SMALL:
            if axis is None:
                size = int(np.prod(shape))
                local_g.setdefault(n, []).append(g_small[off:off + size].reshape(shape))
                off += size
        small_rows = rest_rows[l][sum(pack_off[n][1] for n in REST_GROUP):]
        for (n, _, _), block in zip(SHARDED_SMALL, _small_from_rows(small_rows)):
            local_g.setdefault(n, []).append(block)
    names = [n for n, _, _ in SMALL]
    g_loc = {n: jnp.stack(local_g[n]) for n in names}
    d_s, m_s, v_s = _adamw(_flat_rows([w[n] for n in names]), _flat_rows([g_loc[n] for n in names]),
                           _flat_rows([m[n] for n in names]), _flat_rows([v[n] for n in names]), name="adamw_small")
    off = 0
    for n in names:
        size = int(np.prod(w[n].shape))
        take = lambda a: a.reshape(-1)[off:off + size].reshape(w[n].shape)
        out_g[n], out_d[n], out_m[n], out_v[n] = g_loc[n], take(d_s), take(m_s), take(v_s)
        off += size

    outs = [loss, grad_x]
    for group in (out_g, out_d, out_m, out_v):
        outs += [group[n] for n in WEIGHT_ORDER]
    return tuple(outs)


def kernel(x, mem, positions, mix_norm_pre, w_in, cv_w, cv_b, cv_ln_g, cv_ln_b, cv_pw_w, cv_pw_b, w_out, mix_norm_post, x_norm_pre, mem_norm, x_wq, x_wk, x_wv, x_wo, x_norm_post, ffn_norm_pre, ffn_w_up, ffn_conv_w, ffn_conv_b, ffn_w_down, ffn_norm_post, loss_target, m_mix_norm_pre, m_w_in, m_cv_w, m_cv_b, m_cv_ln_g, m_cv_ln_b, m_cv_pw_w, m_cv_pw_b, m_w_out, m_mix_norm_post, m_x_norm_pre, m_mem_norm, m_x_wq, m_x_wk, m_x_wv, m_x_wo, m_x_norm_post, m_ffn_norm_pre, m_ffn_w_up, m_ffn_conv_w, m_ffn_conv_b, m_ffn_w_down, m_ffn_norm_post, v_mix_norm_pre, v_w_in, v_cv_w, v_cv_b, v_cv_ln_g, v_cv_ln_b, v_cv_pw_w, v_cv_pw_b, v_w_out, v_mix_norm_post, v_x_norm_pre, v_mem_norm, v_x_wq, v_x_wk, v_x_wv, v_x_wo, v_x_norm_post, v_ffn_norm_pre, v_ffn_w_up, v_ffn_conv_w, v_ffn_conv_b, v_ffn_w_down, v_ffn_norm_post):
    w = dict(zip(WEIGHT_ORDER, (mix_norm_pre, w_in, cv_w, cv_b, cv_ln_g, cv_ln_b, cv_pw_w, cv_pw_b, w_out, mix_norm_post, x_norm_pre, mem_norm, x_wq, x_wk, x_wv, x_wo, x_norm_post, ffn_norm_pre, ffn_w_up, ffn_conv_w, ffn_conv_b, ffn_w_down, ffn_norm_post)))
    m = dict(zip(WEIGHT_ORDER, (m_mix_norm_pre, m_w_in, m_cv_w, m_cv_b, m_cv_ln_g, m_cv_ln_b, m_cv_pw_w, m_cv_pw_b, m_w_out, m_mix_norm_post, m_x_norm_pre, m_mem_norm, m_x_wq, m_x_wk, m_x_wv, m_x_wo, m_x_norm_post, m_ffn_norm_pre, m_ffn_w_up, m_ffn_conv_w, m_ffn_conv_b, m_ffn_w_down, m_ffn_norm_post)))
    v = dict(zip(WEIGHT_ORDER, (v_mix_norm_pre, v_w_in, v_cv_w, v_cv_b, v_cv_ln_g, v_cv_ln_b, v_cv_pw_w, v_cv_pw_b, v_w_out, v_mix_norm_post, v_x_norm_pre, v_mem_norm, v_x_wq, v_x_wk, v_x_wv, v_x_wo, v_x_norm_post, v_ffn_norm_pre, v_ffn_w_up, v_ffn_conv_w, v_ffn_conv_b, v_ffn_w_down, v_ffn_norm_post)))
    return _step(x, mem, positions, loss_target, w, m, v)
```

```python
import functools

import jax
import jax.numpy as jnp
import numpy as np
from jax import lax
from jax.experimental import pallas as pl
from jax.experimental.pallas import tpu as pltpu

F32, BF16 = jnp.float32, jnp.bfloat16
MESH = pl.DeviceIdType.MESH
EPS = 1e-6
LANES = 128
BLK = 128
HD = 64
D_MODEL = 1024
D_FF = 2816
SB_W, CV_W, DL_W = 256, 256, 512
CV_K = 31
ROPE_THETA = 10000.0
DILATIONS = (1, 4, 16)
X_HEADS, X_HD = 4, 256
ADAM_LR, ADAM_B1, ADAM_B2, ADAM_EPS, ADAM_WD, ADAM_STEP = 0.001, 0.9, 0.999, 1e-08, 0.01, 10
NEG_INF = float("-inf")
MIB = 1 << 20

PACK_ROWS = (("w_in", 704), ("w_out", 256), ("x_wq", 256), ("x_wk", 256), ("x_wv", 256), ("x_wo", 256),
             ("ffn_w_up", 1408), ("ffn_w_down", 704))
PACK_RL = sum(r for _, r in PACK_ROWS)
SHARDED_SMALL = (("cv_w", (31, 64), 1), ("ffn_conv_w", (3, 1408), 1), ("cv_pw_w", (64, 256), 0))
SMALL_ROWS = 32
WEIGHT_PACK = (PACK_ROWS[0], ("small", 2 * SMALL_ROWS)) + PACK_ROWS[1:]


def _cp(vmem_mb=48):
    return pltpu.CompilerParams(vmem_limit_bytes=vmem_mb * MIB)


def _dot(a, b):
    return jnp.dot(a, b, preferred_element_type=F32)


def _dot_nt(a, b):
    return lax.dot_general(a, b, (((1,), (1,)), ((), ())), preferred_element_type=F32)


def _dot_tn(a, b):
    return lax.dot_general(a, b, (((0,), (0,)), ((), ())), preferred_element_type=F32)


def _dot_hilo(x, m):
    hi = x.astype(BF16)
    lo = (x - hi.astype(F32)).astype(BF16)
    return _dot(hi, m) + _dot(lo, m)


def _rowsum8(x):
    t, c = x.shape
    return x.reshape(t // 8, 8, c).sum(axis=0)


def _acc_out(ref, i, val):
    @pl.when(i == 0)
    def _():
        ref[...] = val

    @pl.when(i > 0)
    def _():
        ref[...] += val


def _tile(n, cap, mult=8):
    t = min(n, cap)
    while n % t or t % mult:
        t -= 1
    return t


def _rms_mm(x, g, w, *, tm, tn, out_dtype, name, carry=None):
    m, d = x.shape
    n_out = w.shape[1]

    def body(x_ref, g_ref, w_ref, n_ref, o_ref):
        @pl.when(pl.program_id(1) == 0)
        def _():
            xv = x_ref[...]
            r = lax.rsqrt(jnp.mean(xv * xv, axis=-1, keepdims=True) + EPS)
            n_ref[...] = (xv * r * g_ref[...]).astype(BF16)

        o_ref[...] = _dot(n_ref[...], w_ref[...]).astype(out_dtype)

    return _call(
        body, grid=(m // tm, n_out // tn), name=name, carry=carry,
        in_specs=[pl.BlockSpec((tm, d), lambda i, j: (i, 0)), pl.BlockSpec((1, d), lambda i, j: (0, 0)),
                  pl.BlockSpec((d, tn), lambda i, j: (0, j))],
        out_specs=[pl.BlockSpec((tm, d), lambda i, j: (i, 0)), pl.BlockSpec((tm, tn), lambda i, j: (i, j))],
        out_shape=[jax.ShapeDtypeStruct((m, d), BF16), jax.ShapeDtypeStruct((m, n_out), out_dtype)],
        args=(x, g, w))


def _mm_post(a, w, h, g, *, tm, name, carry=None):
    m, k = a.shape
    d = w.shape[1]

    def body(a_ref, w_ref, h_ref, g_ref, y_ref, ho_ref):
        y = _dot(a_ref[...], w_ref[...])
        y_ref[...] = y
        r = lax.rsqrt(jnp.mean(y * y, axis=-1, keepdims=True) + EPS)
        ho_ref[...] = h_ref[...] + y * r * g_ref[...]

    return _call(
        body, grid=(m // tm,), name=name, carry=carry,
        in_specs=[pl.BlockSpec((tm, k), lambda i: (i, 0)), pl.BlockSpec((k, d), lambda i: (0, 0)),
                  pl.BlockSpec((tm, d), lambda i: (i, 0)), pl.BlockSpec((1, d), lambda i: (0, 0))],
        out_specs=[pl.BlockSpec((tm, d), lambda i: (i, 0)), pl.BlockSpec((tm, d), lambda i: (i, 0))],
        out_shape=[jax.ShapeDtypeStruct((m, d), F32), jax.ShapeDtypeStruct((m, d), F32)],
        args=(a, w, h, g))


def _mm_nt(a, w, *, tm, tn, out_dtype, name):
    m, k = a.shape
    n_out = w.shape[0]

    def body(a_ref, w_ref, o_ref):
        o_ref[...] = _dot_nt(a_ref[...], w_ref[...]).astype(out_dtype)

    return pl.pallas_call(
        body, grid=(n_out // tn, m // tm), name=name,
        in_specs=[pl.BlockSpec((tm, k), lambda j, i: (i, 0)), pl.BlockSpec((tn, k), lambda j, i: (j, 0))],
        out_specs=pl.BlockSpec((tm, tn), lambda j, i: (i, j)),
        out_shape=jax.ShapeDtypeStruct((m, n_out), out_dtype),
        compiler_params=_cp())(a, w)


def _mm_tn(x, dy, *, tk, tn, tm, name):
    m, k = x.shape
    n_out = dy.shape[1]

    def body(x_ref, d_ref, o_ref):
        _acc_out(o_ref, pl.program_id(2), _dot_tn(x_ref[...], d_ref[...]))

    return pl.pallas_call(
        body, grid=(k // tk, n_out // tn, m // tm), name=name,
        in_specs=[pl.BlockSpec((tm, tk), lambda a, b, c: (c, a)), pl.BlockSpec((tm, tn), lambda a, b, c: (c, b))],
        out_specs=pl.BlockSpec((tk, tn), lambda a, b, c: (a, b)),
        out_shape=jax.ShapeDtypeStruct((k, n_out), F32),
        compiler_params=_cp())(x, dy)


def _rms_bwd(x, g, dout, res, *, out_dtype, tm, name, carry=None):
    m, d = x.shape
    has_res = res is not None

    def body(*refs):
        if has_res:
            x_ref, g_ref, d_ref, r_ref, dx_ref, dg_ref = refs
        else:
            x_ref, g_ref, d_ref, dx_ref, dg_ref = refs
        xv = x_ref[...]
        dv = d_ref[...].astype(F32)
        r = lax.rsqrt(jnp.mean(xv * xv, axis=-1, keepdims=True) + EPS)
        xh = xv * r
        dxh = dv * g_ref[...]
        dx = r * (dxh - xh * jnp.mean(dxh * xh, axis=-1, keepdims=True))
        if has_res:
            dx = dx + r_ref[...]
        dx_ref[...] = dx.astype(out_dtype)
        _acc_out(dg_ref, pl.program_id(0), _rowsum8(dv * xh))

    row = pl.BlockSpec((tm, d), lambda i: (i, 0))
    ins = [row, pl.BlockSpec((1, d), lambda i: (0, 0)), row] + ([row] if has_res else [])
    args = (x, g, dout) + ((res,) if has_res else ())
    return _call(
        body, grid=(m // tm,), name=name, carry=carry, in_specs=ins,
        out_specs=[row, pl.BlockSpec((8, d), lambda i: (0, 0))],
        out_shape=[jax.ShapeDtypeStruct((m, d), out_dtype), jax.ShapeDtypeStruct((8, d), F32)], args=args)


def _loss_grad(h, tgt, *, tm, name):
    m, d = h.shape

    def body(h_ref, t_ref, dh_ref, p_ref):
        e = h_ref[...] - t_ref[...]
        dh_ref[...] = e / d
        _acc_out(p_ref, pl.program_id(0), _rowsum8(e * e))

    row = pl.BlockSpec((tm, d), lambda i: (i, 0))
    return pl.pallas_call(
        body, grid=(m // tm,), name=name, in_specs=[row, row],
        out_specs=[row, pl.BlockSpec((8, d), lambda i: (0, 0))],
        out_shape=[jax.ShapeDtypeStruct((m, d), F32), jax.ShapeDtypeStruct((8, d), F32)],
        compiler_params=_cp())(h, tgt)


def _adamw(w, g, m, v, *, name, carry=None):
    r, c = w.shape
    tr = _tile(r, 256)

    def body(w_ref, g_ref, m_ref, v_ref, d_ref, mo_ref, vo_ref):
        gv = g_ref[...]
        m2 = ADAM_B1 * m_ref[...] + (1.0 - ADAM_B1) * gv
        v2 = ADAM_B2 * v_ref[...] + (1.0 - ADAM_B2) * jnp.square(gv)
        m_hat = m2 / (1.0 - ADAM_B1 ** ADAM_STEP)
        v_hat = v2 / (1.0 - ADAM_B2 ** ADAM_STEP)
        d_ref[...] = -ADAM_LR * (m_hat / (jnp.sqrt(v_hat) + ADAM_EPS) + ADAM_WD * w_ref[...])
        mo_ref[...] = m2
        vo_ref[...] = v2

    blk = pl.BlockSpec((tr, c), lambda i: (i, 0))
    return _call(body, grid=(r // tr,), name=name, carry=carry, in_specs=[blk] * 4, out_specs=[blk] * 3,
                 out_shape=[jax.ShapeDtypeStruct((r, c), F32)] * 3, args=(w, g, m, v))


def _head_masks():
    lane = lax.broadcasted_iota(jnp.int32, (BLK, LANES), 1)
    row = lax.broadcasted_iota(jnp.int32, (BLK, LANES), 0)
    return lane, row, lane < HD


def _sb_scores(q_a, k, before):
    z = _dot_nt(q_a, k)
    sp = jnp.log1p(jnp.exp(-jnp.abs(z)))
    ls_pos = jnp.minimum(z, 0.0) - sp
    lkeep = jnp.where(before, ls_pos - z, 0.0)
    return ls_pos, lkeep


SB_DEAD = -104.0


def _sb_alive(jj, i, carry):
    return jnp.logical_and(jj <= i, jnp.max(carry) > SB_DEAD)


SB_QB_FWD = 2
SB_QB = 2


def _sb_before(jj, qb=SB_QB):
    lane = lax.broadcasted_iota(jnp.int32, (qb * 2 * BLK, LANES), 1)
    row = lax.broadcasted_iota(jnp.int32, (qb * 2 * BLK, LANES), 0)
    below_diag = jj - (qb - 1) + row // (2 * BLK)
    return jnp.logical_or(below_diag > 0, jnp.logical_and(below_diag == 0, lane < row % BLK))


def _sb_stack(x, lane_h, qb=SB_QB):
    return jnp.concatenate([_stack_heads(x[b * BLK:(b + 1) * BLK], lane_h) for b in range(qb)], axis=0)


def _sb_unstack(x, lane_h, qb=SB_QB):
    return jnp.concatenate([jnp.where(lane_h, x[2 * b * BLK:(2 * b + 1) * BLK], x[(2 * b + 1) * BLK:(2 * b + 2) * BLK])
                            for b in range(qb)], axis=0)


SB_ROWS = SB_QB * 2 * BLK


def _sb_fwd(u, *, name, carry=None):
    s_len = u.shape[0]
    qb = SB_QB_FWD
    qrows, rows = qb * BLK, qb * 2 * BLK

    def body(q_ref, k_ref, v_ref, o_ref):
        top = pl.program_id(0) * qb + qb - 1
        lane, row, lane_h = _head_masks()
        suffix = (row > lane).astype(BF16)
        pairs = [slice(hp * LANES, (hp + 1) * LANES) for hp in range(2)]
        qs = [_sb_stack(q_ref[:, cs] * 0.125, lane_h, qb) for cs in pairs]

        def step(state):
            jj, ccs, accs = state[0], state[1:3], state[3:5]
            rows_k = pl.ds(pl.multiple_of((top - jj) * BLK, BLK), BLK)
            before = _sb_before(jj, qb)
            scores = [_sb_scores(q, k_ref[rows_k, cs].astype(BF16), before) for q, cs in zip(qs, pairs)]
            between = [_dot_hilo(lkeep, suffix) + cc for (_, lkeep), cc in zip(scores, ccs)]
            atts = [jnp.where(before, jnp.exp(ls_pos + b), 0.0).astype(BF16) for (ls_pos, _), b in zip(scores, between)]
            new_cc = [cc + jnp.sum(lkeep, axis=1, keepdims=True) for (_, lkeep), cc in zip(scores, ccs)]
            new_acc = [acc + _dot(a, v_ref[rows_k, cs].astype(BF16)) for a, acc, cs in zip(atts, accs, pairs)]
            return (jj + 1, *new_cc, *new_acc)

        zc, za = jnp.zeros((rows, 1), F32), jnp.zeros((rows, LANES), F32)
        res = lax.while_loop(lambda st: _sb_alive(st[0], top, jnp.maximum(st[1], st[2])), step,
                             (jnp.int32(0), zc, zc, za, za))
        for hp, cs in enumerate(pairs):
            o_ref[:, cs] = _sb_unstack(res[3 + hp], lane_h, qb).astype(BF16)

    wide = 2 * LANES
    return _call(
        body, grid=(s_len // qrows,), name=name, carry=carry,
        in_specs=[pl.BlockSpec((qrows, wide), lambda i: (i, 0)), pl.BlockSpec((s_len, wide), lambda i: (0, 1)),
                  pl.BlockSpec((s_len, wide), lambda i: (0, 2))],
        out_specs=[pl.BlockSpec((qrows, wide), lambda i: (i, 0))],
        out_shape=[jax.ShapeDtypeStruct((s_len, SB_W), BF16)], args=(u, u, u))


def _sb_bwd(u, dcat, *, name, carry=None):
    s_len = u.shape[0]
    nq = s_len // BLK
    qrows = SB_QB * BLK

    def body(q_ref, k_ref, v_ref, do_ref, dq_ref, dk_ref, dv_ref, g_scr, b_scr):
        step = pl.program_id(1)
        top = step * SB_QB + SB_QB - 1
        lane, row, lane_h = _head_masks()
        suffix = (row > lane).astype(BF16)
        prefix = (row < lane).astype(BF16)
        qf = q_ref[...]
        qs = _sb_stack(qf * 0.125, lane_h)
        qu = _sb_stack(qf, lane_h)
        dos = _sb_stack(do_ref[...], lane_h)

        @pl.when(step == 0)
        def _():
            dk_ref[...] = jnp.zeros_like(dk_ref)
            dv_ref[...] = jnp.zeros_like(dv_ref)

        def down(state):
            jj, cc = state
            j = top - jj
            off = pl.multiple_of(j * BLK, BLK)
            k = k_ref[pl.ds(off, BLK), :].astype(BF16)
            v = v_ref[pl.ds(off, BLK), :].astype(BF16)
            before = _sb_before(jj)
            ls_pos, lkeep = _sb_scores(qs, k, before)
            between = _dot_hilo(lkeep, suffix) + cc
            att = jnp.where(before, jnp.exp(ls_pos + between), 0.0)
            g_scr[j] = att * _dot_nt(dos, v)
            b_scr[j] = jnp.exp(ls_pos)
            dv_ref[pl.ds(off, BLK), :] += _dot_tn(att.astype(BF16), dos)
            return jj + 1, cc + jnp.sum(lkeep, axis=1, keepdims=True)

        zc = jnp.zeros((SB_ROWS, 1), F32)
        visited = lax.while_loop(lambda st: _sb_alive(st[0], top, st[1]), down, (jnp.int32(0), zc))[0]

        def up(j, carry):
            pc, dq = carry
            off = pl.multiple_of(j * BLK, BLK)
            k = k_ref[pl.ds(off, BLK), :].astype(BF16)
            g, beta = g_scr[j], b_scr[j]
            below = _dot_hilo(g, prefix) + pc
            dz = (jnp.where(_sb_before(top - j), g * (1.0 - beta) - beta * below, 0.0) * 0.125).astype(BF16)
            dk_ref[pl.ds(off, BLK), :] += _dot_tn(dz, qu)
            return pc + jnp.sum(g, axis=1, keepdims=True), dq + _dot(dz, k)

        dq = lax.fori_loop(top + 1 - visited, top + 1, up, (zc, jnp.zeros((SB_ROWS, LANES), F32)))[1]
        dq_ref[...] = _sb_unstack(dq, lane_h)

    col = lambda c0: pl.BlockSpec((s_len, LANES), lambda hp, i: (0, c0 + hp))
    blk = pl.BlockSpec((qrows, LANES), lambda hp, i: (i, hp))
    acc = pl.BlockSpec((s_len, LANES), lambda hp, i: (0, hp))
    return _call(
        body, grid=(2, s_len // qrows), name=name, carry=carry, in_specs=[blk, col(2), col(4), blk],
        out_specs=[blk, acc, acc], out_shape=[jax.ShapeDtypeStruct((s_len, SB_W), F32)] * 3,
        scratch_shapes=[pltpu.VMEM((nq, SB_ROWS, LANES), F32), pltpu.VMEM((nq, SB_ROWS, LANES), F32)],
        vmem_mb=56, args=(u, u, u, dcat))


CV_T = 512
CV_H = 32
CV_STRIP = 64


def _cv_specs(s_len):
    cur = lambda c: pl.BlockSpec((CV_T, CV_W), lambda i: (i, c))
    prev = lambda c: pl.BlockSpec((CV_H, CV_W), lambda i: (jnp.maximum(i * (CV_T // CV_H) - 1, 0), c))
    nxt = lambda c: pl.BlockSpec((CV_H, CV_W),
                                 lambda i: (jnp.minimum((i + 1) * (CV_T // CV_H), s_len // CV_H - 1), c))
    full = lambda r: pl.BlockSpec((r, CV_W), lambda i: (0, 0))
    return cur, prev, nxt, full


def _glu_into(gp_ref, val_ref, gate_ref, valp_ref, gatep_ref, i):
    gp_ref[0:CV_H, :] = jnp.where(i > 0, valp_ref[...] * jax.nn.sigmoid(gatep_ref[...]), 0.0)
    gp_ref[CV_H:, :] = val_ref[...] * jax.nn.sigmoid(gate_ref[...])


CV_SH = CV_T + CV_H - 8


def _shifted_copies(sh_ref, slab_ref):
    for r in range(1, 8):
        sh_ref[r - 1] = slab_ref[pl.ds(r, CV_SH), :]


def _tap(sh_ref, slab_ref, off, r0, rows):
    if off % 8 == 0:
        return slab_ref[pl.ds(off + r0, rows), :]
    return sh_ref[off % 8 - 1, pl.ds(off - off % 8 + r0, rows), :]


def _cv_fwd(u, cv_w, cv_b, ln_g, ln_b, pw_w, pw_b, *, name):
    s_len = u.shape[0]
    cur, prev, _, full = _cv_specs(s_len)

    def body(val_ref, gate_ref, valp_ref, gatep_ref, w_ref, b_ref, g_ref, be_ref, pw_ref, pb_ref,
             o_ref, c_ref, gp_ref, sh_ref):
        _glu_into(gp_ref, val_ref, gate_ref, valp_ref, gatep_ref, pl.program_id(0))
        _shifted_copies(sh_ref, gp_ref)
        for r0, rows in _strips(CV_T, CV_STRIP):
            acc = jnp.zeros((rows, CV_W), F32) + b_ref[...]
            for k in range(CV_K):
                acc = acc + w_ref[k:k + 1, :] * _tap(sh_ref, gp_ref, CV_H - CV_K + 1 + k, r0, rows)
            c_ref[r0:r0 + rows, :] = acc
        acc = c_ref[...]
        mu = jnp.mean(acc, axis=-1, keepdims=True)
        xc = acc - mu
        xh = xc * lax.rsqrt(jnp.mean(xc * xc, axis=-1, keepdims=True) + EPS)
        a = xh * g_ref[...] + be_ref[...]
        s = a * jax.nn.sigmoid(a)
        o_ref[...] = (_dot(s.astype(BF16), pw_ref[...]) + pb_ref[...]).astype(BF16)

    return pl.pallas_call(
        body, grid=(s_len // CV_T,), name=name,
        in_specs=[cur(3), cur(4), prev(3), prev(4), full(CV_K), full(1), full(1), full(1), full(CV_W), full(1)],
        out_specs=[cur(0), cur(0)],
        out_shape=[jax.ShapeDtypeStruct((s_len, CV_W), BF16), jax.ShapeDtypeStruct((s_len, CV_W), F32)],
        scratch_shapes=[pltpu.VMEM((CV_T + CV_H, CV_W), F32), pltpu.VMEM((7, CV_SH, CV_W), F32)],
        compiler_params=_cp())(u, u, u, u, cv_w, cv_b, ln_g, ln_b, pw_w, pw_b)


def _cv_bwd_local(c, dcat, ln_g, ln_b, pw_w, *, name):
    s_len = c.shape[0]
    cur, _, _, full = _cv_specs(s_len)

    def body(c_ref, db_ref, g_ref, be_ref, pw_ref, dc_ref, dpw_ref, vec_ref):
        i = pl.program_id(0)
        cv = c_ref[...]
        db = db_ref[...]
        mu = jnp.mean(cv, axis=-1, keepdims=True)
        xc = cv - mu
        rstd = lax.rsqrt(jnp.mean(xc * xc, axis=-1, keepdims=True) + EPS)
        xh = xc * rstd
        a = xh * g_ref[...] + be_ref[...]
        sg = jax.nn.sigmoid(a)
        s = a * sg
        dbb = db.astype(BF16)
        ds = _dot_nt(dbb, pw_ref[...])
        da = ds * (sg * (1.0 + a * (1.0 - sg)))
        dxh = da * g_ref[...]
        dc_ref[...] = rstd * (dxh - jnp.mean(dxh, axis=-1, keepdims=True)
                              - xh * jnp.mean(dxh * xh, axis=-1, keepdims=True))
        _acc_out(dpw_ref, i, _dot_tn(s.astype(BF16), dbb))
        _acc_out(vec_ref, i, jnp.concatenate([_rowsum8(db), _rowsum8(da * xh), _rowsum8(da)], axis=0))

    return pl.pallas_call(
        body, grid=(s_len // CV_T,), name=name,
        in_specs=[cur(0), cur(1), full(1), full(1), full(CV_W)],
        out_specs=[cur(0), full(CV_W), full(24)],
        out_shape=[jax.ShapeDtypeStruct((s_len, CV_W), F32), jax.ShapeDtypeStruct((CV_W, CV_W), F32),
                   jax.ShapeDtypeStruct((24, CV_W), F32)], compiler_params=_cp())(c, dcat, ln_g, ln_b, pw_w)


def _cv_bwd_conv(u, dc, cv_w, *, name):
    s_len = u.shape[0]
    cur, prev, nxt, full = _cv_specs(s_len)
    last = s_len // CV_T - 1

    def body(val_ref, gate_ref, valp_ref, gatep_ref, dc_ref, dcn_ref, w_ref, du_ref, dw_ref, dbias_ref,
             gp_ref, dcp_ref, gsh_ref, dsh_ref):
        i = pl.program_id(0)
        _glu_into(gp_ref, val_ref, gate_ref, valp_ref, gatep_ref, i)
        dcv = dc_ref[...]
        dcp_ref[0:CV_T, :] = dcv
        dcp_ref[CV_T:, :] = jnp.where(i < last, dcn_ref[...], 0.0)
        _shifted_copies(gsh_ref, gp_ref)
        _shifted_copies(dsh_ref, dcp_ref)
        strips = _strips(CV_T, CV_STRIP)
        for r0, rows in strips:
            dg = jnp.zeros((rows, CV_W), F32)
            for k in range(CV_K):
                dg = dg + w_ref[k:k + 1, :] * _tap(dsh_ref, dcp_ref, CV_K - 1 - k, r0, rows)
            sg = jax.nn.sigmoid(gate_ref[r0:r0 + rows, :])
            du_ref[r0:r0 + rows, 0:CV_W] = (dg * sg).astype(BF16)
            du_ref[r0:r0 + rows, CV_W:] = (dg * val_ref[r0:r0 + rows, :] * sg * (1.0 - sg)).astype(BF16)
        parts = []
        for k in range(CV_K):
            part = jnp.zeros((8, CV_W), F32)
            for r0, rows in strips:
                part = part + _rowsum8(dc_ref[r0:r0 + rows, :] * _tap(gsh_ref, gp_ref, CV_H - CV_K + 1 + k, r0, rows))
            parts.append(part)
        _acc_out(dw_ref, i, jnp.concatenate(parts, axis=0))
        _acc_out(dbias_ref, i, _rowsum8(dcv))

    return pl.pallas_call(
        body, grid=(s_len // CV_T,), name=name,
        in_specs=[cur(3), cur(4), prev(3), prev(4), cur(0), nxt(0), full(CV_K)],
        out_specs=[pl.BlockSpec((CV_T, 2 * CV_W), lambda i: (i, 0)), full(CV_K * 8), full(8)],
        out_shape=[jax.ShapeDtypeStruct((s_len, 2 * CV_W), BF16), jax.ShapeDtypeStruct((CV_K * 8, CV_W), F32),
                   jax.ShapeDtypeStruct((8, CV_W), F32)],
        scratch_shapes=[pltpu.VMEM((CV_T + CV_H, CV_W), F32)] * 2 + [pltpu.VMEM((7, CV_SH, CV_W), F32)] * 2,
        compiler_params=_cp())(u, u, u, u, dc, dc, cv_w)


def _rope_tables(pos_col, inv_freq_row, *, name):
    s_len = pos_col.shape[0]

    def body(p_ref, f_ref, cos_ref, sin_ref):
        ang = p_ref[...].astype(F32) * f_ref[...]
        lane = lax.broadcasted_iota(jnp.int32, (s_len, LANES), 1)
        sn = jnp.sin(ang)
        cos_ref[...] = jnp.cos(ang)
        sin_ref[...] = jnp.where(lane % HD < HD // 2, -sn, sn)

    return pl.pallas_call(body, name=name, out_shape=[jax.ShapeDtypeStruct((s_len, LANES), F32)] * 2,
                          compiler_params=_cp())(pos_col, inv_freq_row)


def _rot_half(x):
    lane = lax.broadcasted_iota(jnp.int32, x.shape, 1)
    return jnp.where(lane % HD < HD // 2, pltpu.roll(x, LANES - HD // 2, 1), pltpu.roll(x, HD // 2, 1))


def _permute_rows(dst_ref, src_ref, d, dtype):
    s_len = src_ref.shape[0]
    seg = s_len // d
    if d == 1:
        dst_ref[...] = src_ref[...].astype(dtype)
        return
    for r in range(d):
        dst_ref[r * seg:(r + 1) * seg, :] = src_ref[pl.ds(r, seg, stride=d), :].astype(dtype)


def _unpermute_rows(dst_ref, src_ref, d):
    s_len = src_ref.shape[0]
    seg = s_len // d
    if d == 1:
        dst_ref[...] = src_ref[...]
        return
    for r in range(d):
        dst_ref[pl.ds(r, seg, stride=d), :] = src_ref[r * seg:(r + 1) * seg, :]


def _rope_perm(u, cos, sin, *, name):
    s_len = u.shape[0]

    def body(x_ref, cos_ref, sin_ref, o_ref, scr):
        a = pl.program_id(0)
        x = x_ref[...]
        rot = a < 2
        scr[...] = x * jnp.where(rot, cos_ref[...], 1.0) + _rot_half(x) * jnp.where(rot, sin_ref[...], 0.0)
        for n, d in enumerate(DILATIONS):
            _permute_rows(o_ref.at[n], scr, d, BF16)

    tab = pl.BlockSpec((s_len, LANES), lambda a, cb: (0, 0))
    return pl.pallas_call(
        body, grid=(3, 4), name=name,
        in_specs=[pl.BlockSpec((s_len, LANES), lambda a, cb: (0, 10 + 4 * a + cb)), tab, tab],
        out_specs=pl.BlockSpec((None, 3, s_len, LANES), lambda a, cb: (a, 0, 0, cb)),
        out_shape=jax.ShapeDtypeStruct((3, 3, s_len, DL_W), BF16),
        scratch_shapes=[pltpu.VMEM((s_len, LANES), F32)], compiler_params=_cp())(u, cos, sin)


DL_UNROLL = 4


def _dl_band(rows):
    lane = lax.broadcasted_iota(jnp.int32, (rows, LANES), 1)
    row = lax.broadcasted_iota(jnp.int32, (rows, LANES), 0) % BLK
    return lane <= row, lane >= row


def _dl_first(s_len, n, i):
    nb = jnp.where(n == 0, s_len // BLK, jnp.where(n == 1, s_len // (BLK * DILATIONS[1]),
                                                   s_len // (BLK * DILATIONS[2])))
    return lax.rem(i, nb) == 0


def _stack_heads(x, lane_h):
    return jnp.concatenate([jnp.where(lane_h, x, 0.0), jnp.where(lane_h, 0.0, x)], axis=0).astype(BF16)


def _dl_rows(i):
    cur = pl.ds(pl.multiple_of(i * BLK, BLK), BLK)
    prev = pl.ds(pl.multiple_of(jnp.maximum(i - 1, 0) * BLK, BLK), BLK)
    return cur, prev


def _dl_in_specs(s_len):
    return [pl.BlockSpec((None, None, s_len, LANES), functools.partial(lambda a, n, hp: (a, n, 0, hp), a))
            for a in range(3)]


def _dl_fwd(qkv, *, name, carry=None):
    s_len = qkv.shape[2]

    def body(q_ref, k_ref, v_ref, o_ref, l_ref):
        n = pl.program_id(0)
        lane_h = _head_masks()[2]
        band_c, band_p = _dl_band(2 * BLK)
        ones = jnp.ones((BLK, LANES), BF16)

        @pl.loop(0, s_len // BLK, step=DL_UNROLL)
        def _(i0):
            blocks = [i0 + t for t in range(DL_UNROLL)]
            rows = [_dl_rows(i) for i in blocks]
            scores = []
            for cur, prev in rows:
                qs = _stack_heads(q_ref[cur, :] * 0.125, lane_h)
                scores.append((_dot_nt(qs, k_ref[cur, :]), _dot_nt(qs, k_ref[prev, :])))
            probs = []
            for i, (sc, sp) in zip(blocks, scores):
                sc = jnp.where(band_c, sc, NEG_INF)
                sp = jnp.where(jnp.logical_and(band_p, jnp.logical_not(_dl_first(s_len, n, i))), sp, NEG_INF)
                m = jnp.max(jnp.maximum(sc, sp), axis=1, keepdims=True)
                probs.append((jnp.exp(sc - m).astype(BF16), jnp.exp(sp - m).astype(BF16), m))
            for (cur, prev), (pc, pp, m) in zip(rows, probs):
                r = (_dot(pc, jnp.concatenate([v_ref[cur, :], ones], axis=1))
                     + _dot(pp, jnp.concatenate([v_ref[prev, :], ones], axis=1)))
                den = jnp.where(lane_h, r[:BLK, LANES:], r[BLK:, LANES:])
                o_ref[cur, :] = jnp.where(lane_h, r[:BLK, :LANES], r[BLK:, :LANES]) / den
                l_ref[cur, :] = jnp.where(lane_h, m[:BLK], m[BLK:]) + jnp.log(den)

    out = pl.BlockSpec((None, s_len, LANES), lambda n, hp: (n, 0, hp))
    return _call(
        body, grid=(3, 4), name=name, carry=carry, in_specs=_dl_in_specs(s_len), out_specs=[out, out],
        out_shape=[jax.ShapeDtypeStruct((3, s_len, DL_W), F32)] * 2, args=(qkv, qkv, qkv))


def _dl_mix(o_p, l_p, *, name, carry=None):
    s_len = o_p.shape[1]

    def body(o_ref, l_ref, ob_ref, of_ref, lt_ref, o_scr, l_scr):
        n = pl.program_id(1)
        for k, d in enumerate(DILATIONS):
            @pl.when(n == k)
            def _(k=k, d=d):
                _unpermute_rows(o_scr.at[k], o_ref, d)
                _unpermute_rows(l_scr.at[k], l_ref, d)

        @pl.when(n == 2)
        def _():
            l0, l1, l2 = l_scr[0], l_scr[1], l_scr[2]
            m = jnp.maximum(jnp.maximum(l0, l1), l2)
            e0, e1, e2 = jnp.exp(l0 - m), jnp.exp(l1 - m), jnp.exp(l2 - m)
            den = e0 + e1 + e2
            o = (e0 / den) * o_scr[0] + (e1 / den) * o_scr[1] + (e2 / den) * o_scr[2]
            of_ref[...] = o
            ob_ref[...] = o.astype(BF16)
            lt_ref[...] = m + jnp.log(den)

    inb = pl.BlockSpec((None, s_len, LANES), lambda cb, n: (n, 0, cb))
    outb = pl.BlockSpec((s_len, LANES), lambda cb, n: (0, cb))
    return _call(
        body, grid=(4, 3), name=name, carry=carry, in_specs=[inb, inb], out_specs=[outb, outb, outb],
        out_shape=[jax.ShapeDtypeStruct((s_len, DL_W), BF16), jax.ShapeDtypeStruct((s_len, DL_W), F32),
                   jax.ShapeDtypeStruct((s_len, DL_W), F32)],
        scratch_shapes=[pltpu.VMEM((3, s_len, LANES), F32), pltpu.VMEM((3, s_len, LANES), F32)], args=(o_p, l_p))


def _dl_bwd_prep(dcat, o, lse, *, name):
    s_len = o.shape[0]

    def body(do_ref, o_ref, l_ref, dop_ref, st_ref, d_scr):
        n = pl.program_id(1)

        @pl.when(n == 0)
        def _():
            r0 = lax.broadcasted_iota(jnp.int32, (LANES, LANES), 0) // HD
            r1 = lax.broadcasted_iota(jnp.int32, (LANES, LANES), 1) // HD
            d_scr[...] = _dot_hilo(do_ref[...] * o_ref[...], (r0 == r1).astype(BF16))

        for k, d in enumerate(DILATIONS):
            @pl.when(n == k)
            def _(d=d):
                _permute_rows(dop_ref, do_ref, d, BF16)
                _permute_rows(st_ref.at[0], d_scr, d, F32)
                _permute_rows(st_ref.at[1], l_ref, d, F32)

    nat = lambda c0: pl.BlockSpec((s_len, LANES), lambda cb, n: (0, c0 + cb))
    return pl.pallas_call(
        body, grid=(4, 3), name=name, in_specs=[nat(4), nat(0), nat(0)],
        out_specs=[pl.BlockSpec((None, s_len, LANES), lambda cb, n: (n, 0, cb)),
                   pl.BlockSpec((2, None, s_len, LANES), lambda cb, n: (0, n, 0, cb))],
        out_shape=[jax.ShapeDtypeStruct((3, s_len, DL_W), BF16), jax.ShapeDtypeStruct((2, 3, s_len, DL_W), F32)],
        scratch_shapes=[pltpu.VMEM((s_len, LANES), F32)], compiler_params=_cp())(dcat, o, lse)


def _dl_bwd(qkv, dop, stats, *, name, carry=None):
    s_len = qkv.shape[2]

    def body(q_ref, k_ref, v_ref, do_ref, st_ref, cur_ref, prev_ref):
        n = pl.program_id(0)
        lane_h = _head_masks()[2]
        band_c, band_p = _dl_band(2 * BLK)

        def per_head(x):
            xr = pltpu.roll(x, HD, 1)
            return jnp.concatenate([jnp.where(lane_h, x, xr), jnp.where(lane_h, xr, x)], axis=0)

        @pl.loop(0, s_len // BLK, step=DL_UNROLL)
        def _(i0):
            blocks = [i0 + t for t in range(DL_UNROLL)]
            rows = [_dl_rows(i) for i in blocks]
            stage1 = []
            for cur, prev in rows:
                qs = _stack_heads(q_ref[cur, :] * 0.125, lane_h)
                dos = _stack_heads(do_ref[cur, :], lane_h)
                kc, kp, vc, vp = k_ref[cur, :], k_ref[prev, :], v_ref[cur, :], v_ref[prev, :]
                stage1.append((qs, dos, _dot_nt(qs, kc), _dot_nt(qs, kp), _dot_nt(dos, vc), _dot_nt(dos, vp)))
            stage2 = []
            for i, (cur, prev), (qs, dos, sc, sp, dpc, dpp) in zip(blocks, rows, stage1):
                lse, delta = per_head(st_ref[1, cur, :]), per_head(st_ref[0, cur, :])
                pc = jnp.where(band_c, jnp.exp(sc - lse), 0.0)
                pp = jnp.where(jnp.logical_and(band_p, jnp.logical_not(_dl_first(s_len, n, i))), jnp.exp(sp - lse), 0.0)
                stage2.append((pc.astype(BF16), pp.astype(BF16), (pc * (dpc - delta)).astype(BF16),
                               (pp * (dpp - delta)).astype(BF16)))
            for (cur, prev), (qs, dos, *_), (pc, pp, dsc, dsp) in zip(rows, stage1, stage2):
                dq = _dot(dsc, k_ref[cur, :]) + _dot(dsp, k_ref[prev, :])
                cur_ref[0, cur, :] = jnp.where(lane_h, dq[:BLK], dq[BLK:]) * 0.125
                cur_ref[1, cur, :] = _dot_tn(dsc, qs)
                cur_ref[2, cur, :] = _dot_tn(pc, dos)
                prev_ref[0, cur, :] = _dot_tn(dsp, qs)
                prev_ref[1, cur, :] = _dot_tn(pp, dos)

    return _call(
        body, grid=(3, 4), name=name, carry=carry,
        in_specs=_dl_in_specs(s_len) + [pl.BlockSpec((None, s_len, LANES), lambda n, hp: (n, 0, hp)),
                                        pl.BlockSpec((2, None, s_len, LANES), lambda n, hp: (0, n, 0, hp))],
        out_specs=[pl.BlockSpec((3, None, s_len, LANES), lambda n, hp: (0, n, 0, hp)),
                   pl.BlockSpec((2, None, s_len, LANES), lambda n, hp: (0, n, 0, hp))],
        out_shape=[jax.ShapeDtypeStruct((3, 3, s_len, DL_W), F32), jax.ShapeDtypeStruct((2, 3, s_len, DL_W), F32)],
        vmem_mb=56, args=(qkv, qkv, qkv, dop, stats))


def _dl_bwd_finish(cur, prev, cos, sin, *, name):
    s_len = cur.shape[2]

    def body(c_ref, p_ref, cos_ref, sin_ref, o_ref, p_scr, u_scr, acc):
        a, n = pl.program_id(0), pl.program_id(2)
        has_prev = jnp.where(a > 0, 1.0, 0.0)
        p_scr[...] = c_ref[...]
        p_scr[0:s_len - BLK, :] += has_prev * p_ref[BLK:, :]
        for k, d in enumerate(DILATIONS):
            @pl.when(n == k)
            def _(k=k, d=d):
                if k == 0:
                    acc[...] = p_scr[...]
                else:
                    _unpermute_rows(u_scr, p_scr, d)
                    acc[...] += u_scr[...]

        @pl.when(n == 2)
        def _():
            dy = acc[...]
            rot = a < 2
            o_ref[...] = (dy * jnp.where(rot, cos_ref[...], 1.0)
                          + _rot_half(dy * jnp.where(rot, sin_ref[...], 0.0))).astype(BF16)

    tab = pl.BlockSpec((s_len, LANES), lambda a, cb, n: (0, 0))
    return pl.pallas_call(
        body, grid=(3, 4, 3), name=name,
        in_specs=[pl.BlockSpec((None, None, s_len, LANES), lambda a, cb, n: (a, n, 0, cb)),
                  pl.BlockSpec((None, None, s_len, LANES), lambda a, cb, n: (jnp.maximum(a - 1, 0), n, 0, cb)),
                  tab, tab],
        out_specs=pl.BlockSpec((s_len, LANES), lambda a, cb, n: (0, 4 * a + cb)),
        out_shape=jax.ShapeDtypeStruct((s_len, 3 * DL_W), BF16),
        scratch_shapes=[pltpu.VMEM((s_len, LANES), F32)] * 3, compiler_params=_cp())(cur, prev, cos, sin)


XA_T = 1024


def _xa_probs(q, k):
    s = _dot_nt(q, k) * (X_HD ** -0.5)
    e = jnp.exp(s - jnp.max(s, axis=1, keepdims=True))
    return e / jnp.sum(e, axis=1, keepdims=True)


def _xa_fwd(q, k, v, *, name):
    s_len, d = q.shape
    nm = k.shape[0]

    def body(q_ref, k_ref, v_ref, o_ref):
        for h in range(X_HEADS):
            cs = slice(h * X_HD, (h + 1) * X_HD)
            p = _xa_probs(q_ref[:, cs], k_ref[:, cs])
            o_ref[:, cs] = _dot(p.astype(BF16), v_ref[:, cs]).astype(BF16)

    row = pl.BlockSpec((XA_T, d), lambda i: (i, 0))
    full = pl.BlockSpec((nm, d), lambda i: (0, 0))
    return pl.pallas_call(body, grid=(s_len // XA_T,), name=name, in_specs=[row, full, full], out_specs=row,
                          out_shape=jax.ShapeDtypeStruct((s_len, d), BF16), compiler_params=_cp())(q, k, v)


def _xa_bwd(q, k, v, do, *, name, carry=None):
    s_len, d = q.shape
    nm = k.shape[0]

    def body(q_ref, k_ref, v_ref, do_ref, dq_ref, dk_ref, dv_ref):
        i = pl.program_id(0)
        for h in range(X_HEADS):
            cs = slice(h * X_HD, (h + 1) * X_HD)
            qh, kh, vh, doh = q_ref[:, cs], k_ref[:, cs], v_ref[:, cs], do_ref[:, cs]
            p = _xa_probs(qh, kh)
            dp = _dot_nt(doh, vh)
            ds = (p * (dp - jnp.sum(dp * p, axis=1, keepdims=True)) * (X_HD ** -0.5)).astype(BF16)
            dq_ref[:, cs] = _dot(ds, kh).astype(BF16)
            dkh, dvh = _dot_tn(ds, qh), _dot_tn(p.astype(BF16), doh)

            @pl.when(i == 0)
            def _(cs=cs, dkh=dkh, dvh=dvh):
                dk_ref[:, cs] = dkh
                dv_ref[:, cs] = dvh

            @pl.when(i > 0)
            def _(cs=cs, dkh=dkh, dvh=dvh):
                dk_ref[:, cs] += dkh
                dv_ref[:, cs] += dvh

    row = pl.BlockSpec((XA_T, d), lambda i: (i, 0))
    full = pl.BlockSpec((nm, d), lambda i: (0, 0))
    return _call(
        body, grid=(s_len // XA_T,), name=name, carry=carry, in_specs=[row, full, full, row],
        out_specs=[row, full, full],
        out_shape=[jax.ShapeDtypeStruct((s_len, d), BF16), jax.ShapeDtypeStruct((nm, d), F32),
                   jax.ShapeDtypeStruct((nm, d), F32)], args=(q, k, v, do))


FF_TM, FF_TN, FF_H = 512, 256, 8
GELU_K, GELU_C = 0.7978845608028654, 0.044715


FF_STRIP = 64


def _ff_conv(e_ref, w_ref, b_ref, rows, r0=0):
    return (w_ref[0:1, :] * e_ref[pl.ds(FF_H - 2 + r0, rows), :] + w_ref[1:2, :] * e_ref[pl.ds(FF_H - 1 + r0, rows), :]
            + w_ref[2:3, :] * e_ref[pl.ds(FF_H + r0, rows), :] + b_ref[...])


def _strips(total, size):
    return [(r0, min(size, total - r0)) for r0 in range(0, total, size)]


def _ff_gate_fwd(up, conv_w, conv_b, *, name, carry=None):
    s_len = up.shape[0]
    nj = D_FF // FF_TN

    def body(g_ref, v_ref, gp_ref, vp_ref, wg_ref, wv_ref, bg_ref, bv_ref, o_ref, eg, ev):
        i = pl.program_id(0)
        for e, cur, prev in ((eg, g_ref, gp_ref), (ev, v_ref, vp_ref)):
            e[0:FF_H, :] = jnp.where(i > 0, prev[...], 0.0)
            e[FF_H:, :] = cur[...]
        for r0, rows in _strips(FF_TM, FF_STRIP):
            gate = _ff_conv(eg, wg_ref, bg_ref, rows, r0)
            val = _ff_conv(ev, wv_ref, bv_ref, rows, r0)
            t = jnp.tanh(GELU_K * (gate + GELU_C * gate * gate * gate))
            o_ref[r0:r0 + rows, :] = (0.5 * gate * (1.0 + t) * val).astype(BF16)

    cur = lambda c0: pl.BlockSpec((FF_TM, FF_TN), lambda i, j: (i, c0 + j))
    prev = lambda c0: pl.BlockSpec((FF_H, FF_TN), lambda i, j: (jnp.maximum(i * (FF_TM // FF_H) - 1, 0), c0 + j))
    par = lambda r, c0: pl.BlockSpec((r, FF_TN), lambda i, j: (0, c0 + j))
    return _call(
        body, grid=(s_len // FF_TM, nj), name=name, carry=carry,
        in_specs=[cur(0), cur(nj), prev(0), prev(nj), par(3, 0), par(3, nj), par(1, 0), par(1, nj)],
        out_specs=[cur(0)], out_shape=[jax.ShapeDtypeStruct((s_len, D_FF), BF16)],
        scratch_shapes=[pltpu.VMEM((FF_TM + FF_H, FF_TN), F32)] * 2,
        args=(up, up, up, up, conv_w, conv_w, conv_b, conv_b))


def _ff_gate_bwd(up, dact, conv_w, conv_b, *, name, carry=None):
    s_len = up.shape[0]
    nj = D_FF // FF_TN
    last = s_len // FF_TM - 1
    ext = FF_TM + FF_H

    def body(g_ref, v_ref, gp_ref, vp_ref, gn_ref, vn_ref, da_ref, dan_ref, wg_ref, wv_ref, bg_ref, bv_ref,
             dg_ref, dv_ref, dw_ref, db_ref, eg, ev, sg, sv):
        i = pl.program_id(1)
        for e, cur, prev, nxt in ((eg, g_ref, gp_ref, gn_ref), (ev, v_ref, vp_ref, vn_ref)):
            e[0:FF_H, :] = jnp.where(i > 0, prev[...], 0.0)
            e[FF_H:FF_H + FF_TM, :] = cur[...]
            e[FF_H + FF_TM:, :] = nxt[...]
        for r0, rows in _strips(ext, FF_STRIP):
            gate = _ff_conv(eg, wg_ref, bg_ref, rows, r0)
            val = _ff_conv(ev, wv_ref, bv_ref, rows, r0)
            dact = da_ref[r0:r0 + rows, :] if r0 < FF_TM else jnp.where(i < last, dan_ref[...], 0.0)
            t = jnp.tanh(GELU_K * (gate + GELU_C * gate * gate * gate))
            half = 0.5 * (1.0 + t)
            dgelu = half + 0.5 * gate * (1.0 - t * t) * GELU_K * (1.0 + 3.0 * GELU_C * gate * gate)
            sg[r0:r0 + rows, :] = dact * val * dgelu
            sv[r0:r0 + rows, :] = dact * (gate * half)
        for part, (s, e, w_ref, out) in enumerate(((sg, eg, wg_ref, dg_ref), (sv, ev, wv_ref, dv_ref))):
            taps, bias = [jnp.zeros((8, FF_TN), F32)] * 3, jnp.zeros((8, FF_TN), F32)
            for r0, rows in _strips(FF_TM, FF_STRIP):
                d0 = s[pl.ds(r0, rows), :]
                out[r0:r0 + rows, :] = (w_ref[2:3, :] * d0 + w_ref[1:2, :] * s[pl.ds(r0 + 1, rows), :]
                                        + w_ref[0:1, :] * s[pl.ds(r0 + 2, rows), :]).astype(BF16)
                taps = [taps[k] + _rowsum8(d0 * e[pl.ds(FF_H - 2 + k + r0, rows), :]) for k in range(3)]
                bias = bias + _rowsum8(d0)
            _acc_out(dw_ref.at[part], i, jnp.concatenate(taps, axis=0))
            _acc_out(db_ref.at[part], i, bias)

    cur = lambda c0: pl.BlockSpec((FF_TM, FF_TN), lambda j, i: (i, c0 + j))
    prev = lambda c0: pl.BlockSpec((FF_H, FF_TN), lambda j, i: (jnp.maximum(i * (FF_TM // FF_H) - 1, 0), c0 + j))
    nxt = lambda c0: pl.BlockSpec(
        (FF_H, FF_TN), lambda j, i: (jnp.minimum((i + 1) * (FF_TM // FF_H), s_len // FF_H - 1), c0 + j))
    par = lambda r, c0: pl.BlockSpec((r, FF_TN), lambda j, i: (0, c0 + j))
    return _call(
        body, grid=(nj, s_len // FF_TM), name=name, carry=carry,
        in_specs=[cur(0), cur(nj), prev(0), prev(nj), nxt(0), nxt(nj), cur(0), nxt(0),
                  par(3, 0), par(3, nj), par(1, 0), par(1, nj)],
        out_specs=[cur(0), cur(0), pl.BlockSpec((2, 24, FF_TN), lambda j, i: (0, 0, j)),
                   pl.BlockSpec((2, 8, FF_TN), lambda j, i: (0, 0, j))],
        out_shape=[jax.ShapeDtypeStruct((s_len, D_FF), BF16), jax.ShapeDtypeStruct((s_len, D_FF), BF16),
                   jax.ShapeDtypeStruct((2, 24, D_FF), F32), jax.ShapeDtypeStruct((2, 8, D_FF), F32)],
        scratch_shapes=[pltpu.VMEM((FF_TM + 2 * FF_H, FF_TN), F32)] * 2 + [pltpu.VMEM((ext, FF_TN), F32)] * 2,
        args=(up, up, up, up, up, up, dact, dact, conv_w, conv_w, conv_b, conv_b))


def _place():
    x, y, c = lax.axis_index("x"), lax.axis_index("y"), lax.axis_index("c")
    return x, y, c, [(1 - x, y), (x, 1 - y), (1 - x, 1 - y)]


def _remote(src, dst, send_sem, recv_sem, dev):
    return pltpu.make_async_remote_copy(src_ref=src, dst_ref=dst, send_sem=send_sem, recv_sem=recv_sem,
                                        device_id=dev, device_id_type=MESH)


_ANY = pl.BlockSpec(memory_space=pl.ANY)


N_SEMS = 8
SEM_BASE_2 = 4


class _Exchange:
    def __init__(self, operands, out_shapes, start, wait, aliases=None):
        self.operands, self.out_shapes, self.start, self.wait = list(operands), list(out_shapes), start, wait
        self.aliases = aliases or {}


def _sem_scratch():
    return [pltpu.SemaphoreType.DMA((N_SEMS,)), pltpu.SemaphoreType.DMA((N_SEMS,)), pltpu.SemaphoreType.DMA]


def _run_exchange(ex, *, name):
    k, n = len(ex.operands), len(ex.out_shapes)

    def body(*refs):
        ins, outs, sems = refs[:k], refs[k:k + n], refs[k + n:]
        ex.start(ins, outs, *sems)
        ex.wait(ins, outs, *sems)

    return pl.pallas_call(body, name=name, in_specs=[_ANY] * k, out_specs=[_ANY] * n, out_shape=ex.out_shapes,
                          scratch_shapes=_sem_scratch(), input_output_aliases=ex.aliases,
                          compiler_params=_cp(16))(*ex.operands)


def _call(body, *, grid, in_specs, out_specs, out_shape, args, name, scratch_shapes=(), vmem_mb=48, carry=None):
    scratch_shapes = list(scratch_shapes)
    if carry is None:
        return pl.pallas_call(body, grid=grid, name=name, in_specs=in_specs, out_specs=out_specs, out_shape=out_shape,
                              scratch_shapes=scratch_shapes, compiler_params=_cp(vmem_mb))(*args)
    n_in, n_out, n_scr = len(in_specs), len(out_shape), len(scratch_shapes)
    k_in, k_out = len(carry.operands), len(carry.out_shapes)

    def wrapped(*refs):
        ins, refs = refs[:n_in], refs[n_in:]
        cin, refs = refs[:k_in], refs[k_in:]
        outs, refs = refs[:n_out], refs[n_out:]
        cout, refs = refs[:k_out], refs[k_out:]
        scratch, sems = refs[:n_scr], refs[n_scr:]
        ids = [pl.program_id(a) for a in range(len(grid))]
        first = functools.reduce(jnp.logical_and, [i == 0 for i in ids])
        last = functools.reduce(jnp.logical_and, [i == g - 1 for i, g in zip(ids, grid)])

        @pl.when(first)
        def _():
            carry.start(cin, cout, *sems)

        body(*ins, *outs, *scratch)

        @pl.when(last)
        def _():
            carry.wait(cin, cout, *sems)

    aliases = {n_in + i: n_out + o for i, o in carry.aliases.items()}
    return pl.pallas_call(
        wrapped, grid=grid, name=name, in_specs=list(in_specs) + [_ANY] * k_in,
        out_specs=list(out_specs) + [_ANY] * k_out, out_shape=list(out_shape) + carry.out_shapes,
        scratch_shapes=scratch_shapes + _sem_scratch(), input_output_aliases=aliases,
        compiler_params=_cp(vmem_mb))(*args, *carry.operands)


def _half_rows(ref_rows, c):
    half = ref_rows // 2
    return pl.ds(c * half, half)


def _ex_join(a, b):
    ka, na = len(a.operands), len(a.out_shapes)

    def start(ins, outs, *sems):
        a.start(ins[:ka], outs[:na], *sems)
        b.start(ins[ka:], outs[na:], *sems)

    def wait(ins, outs, *sems):
        a.wait(ins[:ka], outs[:na], *sems)
        b.wait(ins[ka:], outs[na:], *sems)

    aliases = dict(a.aliases)
    aliases.update({ka + i: na + o for i, o in b.aliases.items()})
    return _Exchange(a.operands + b.operands, a.out_shapes + b.out_shapes, start, wait, aliases)


def _ex_gather(pack, r0, rl, base=0):
    def copies(ins, outs, send, recv):
        x, y, c, chips = _place()
        rows = _half_rows(rl, c)
        src = ins[0].at[pl.ds(r0 + c * (rl // 2), rl // 2)]
        sends = [_remote(src, outs[0].at[2 * x + y, rows], send.at[base + k], recv.at[base + k], (px, py, c))
                 for k, (px, py) in enumerate(chips)]
        lands = [_remote(src, outs[0].at[2 * px + py, rows], send.at[base + k], recv.at[base + k], (px, py, c))
                 for k, (px, py) in enumerate(chips)]
        return sends, lands

    def mine(ins, outs, local):
        x, y, _, _ = _place()
        return pltpu.make_async_copy(ins[0].at[pl.ds(r0, rl)], outs[0].at[2 * x + y], local)

    def start(ins, outs, send, recv, local):
        mine(ins, outs, local).start()
        for cp in copies(ins, outs, send, recv)[0]:
            cp.start()

    def wait(ins, outs, send, recv, local):
        sends, lands = copies(ins, outs, send, recv)
        for cp in lands:
            cp.wait_recv()
        for cp in sends:
            cp.wait_send()
        mine(ins, outs, local).wait()

    return _Exchange([pack], [jax.ShapeDtypeStruct((4, rl, pack.shape[1]), pack.dtype)], start, wait)


def _ex_gather_forward(g, base=0):
    rl = g.shape[1]

    def copies(outs, send, recv):
        x, y, c, chips = _place()
        slabs = [(outs[0].at[2 * px + py, _half_rows(rl, c)], outs[0].at[2 * px + py, _half_rows(rl, 1 - c)])
                 for px, py in chips]
        sends = [_remote(a, a, send.at[base + k], recv.at[base + k], (x, y, 1 - c)) for k, (a, _) in enumerate(slabs)]
        lands = [_remote(b, b, send.at[base + k], recv.at[base + k], (x, y, 1 - c)) for k, (_, b) in enumerate(slabs)]
        return sends, lands

    def start(ins, outs, send, recv, local):
        for cp in copies(outs, send, recv)[0]:
            cp.start()

    def wait(ins, outs, send, recv, local):
        sends, lands = copies(outs, send, recv)
        for cp in lands:
            cp.wait_recv()
        for cp in sends:
            cp.wait_send()

    return _Exchange([g], [jax.ShapeDtypeStruct(g.shape, g.dtype)], start, wait, aliases={0: 0})


def _ex_swap_halves(gw, base=0):
    nb, rl, d = gw.shape

    def copies(ins, outs, send, recv):
        x, y, c, _ = _place()
        return [_remote(ins[0].at[j, _half_rows(rl, 1 - c)], outs[0].at[j], send.at[base + j], recv.at[base + j],
                        (x, y, 1 - c)) for j in range(nb)]

    def start(ins, outs, send, recv, local):
        for cp in copies(ins, outs, send, recv):
            cp.start()

    def wait(ins, outs, send, recv, local):
        for cp in copies(ins, outs, send, recv):
            cp.wait()

    return _Exchange([gw], [jax.ShapeDtypeStruct((nb, rl // 2, d), gw.dtype)], start, wait)


def _chip_sum(gw, got, c_arr, *, name):
    nchip, half, d = got.shape
    tr = _tile(half, 512)

    def body(c_ref, a_ref, b_ref, o32_ref, o16_ref):
        s = a_ref[...] + b_ref[...]
        o32_ref[...] = s
        o16_ref[...] = s.astype(BF16)

    blk = pl.BlockSpec((None, tr, d), lambda j, i, c_ref: (j, i, 0))
    return pl.pallas_call(
        body, name=name,
        grid_spec=pltpu.PrefetchScalarGridSpec(
            num_scalar_prefetch=1, grid=(nchip, half // tr),
            in_specs=[pl.BlockSpec((None, tr, d), lambda j, i, c_ref: (j, c_ref[0] * (half // tr) + i, 0)), blk],
            out_specs=[blk, blk]),
        out_shape=[jax.ShapeDtypeStruct((nchip, half, d), F32), jax.ShapeDtypeStruct((nchip, half, d), BF16)],
        compiler_params=_cp())(c_arr, gw, got)


def _ex_scatter(s16, base=0):
    def copies(ins, outs, send, recv):
        x, y, c, chips = _place()
        return [_remote(ins[0].at[2 * px + py], outs[0].at[k], send.at[base + k], recv.at[base + k], (px, py, c))
                for k, (px, py) in enumerate(chips)]

    def start(ins, outs, send, recv, local):
        for cp in copies(ins, outs, send, recv):
            cp.start()

    def wait(ins, outs, send, recv, local):
        for cp in copies(ins, outs, send, recv):
            cp.wait()

    return _Exchange([s16], [jax.ShapeDtypeStruct((3,) + s16.shape[1:], s16.dtype)], start, wait)


def _mesh_sum(s32, got, j_arr, *, name):
    _, rl, d = s32.shape
    tr = _tile(rl, 512)

    def body(j_ref, a_ref, b_ref, o_ref):
        o_ref[...] = ((a_ref[...] + b_ref[0].astype(F32)) + b_ref[1].astype(F32)) + b_ref[2].astype(F32)

    return pl.pallas_call(
        body, name=name,
        grid_spec=pltpu.PrefetchScalarGridSpec(
            num_scalar_prefetch=1, grid=(rl // tr,),
            in_specs=[pl.BlockSpec((None, tr, d), lambda i, j_ref: (j_ref[0], i, 0)),
                      pl.BlockSpec((3, tr, d), lambda i, j_ref: (0, i, 0))],
            out_specs=pl.BlockSpec((tr, d), lambda i, j_ref: (i, 0))),
        out_shape=jax.ShapeDtypeStruct((rl, d), F32), compiler_params=_cp())(j_arr, s32, got)


def _ex_share_halves(ghalf):
    half, d = ghalf.shape

    def copies(ins, outs, send, recv, local):
        x, y, c, _ = _place()
        there = outs[0].at[_half_rows(2 * half, c)]
        back = outs[0].at[_half_rows(2 * half, 1 - c)]
        return (_remote(ins[0], there, send.at[0], recv.at[0], (x, y, 1 - c)),
                _remote(ins[0], back, send.at[0], recv.at[0], (x, y, 1 - c)), pltpu.make_async_copy(ins[0], there, local))

    def start(ins, outs, send, recv, local):
        out, _, mine = copies(ins, outs, send, recv, local)
        mine.start()
        out.start()

    def wait(ins, outs, send, recv, local):
        out, back, mine = copies(ins, outs, send, recv, local)
        back.wait_recv()
        out.wait_send()
        mine.wait()

    return _Exchange([ghalf], [jax.ShapeDtypeStruct((2 * half, d), ghalf.dtype)], start, wait)


class _ReduceScatter:
    def __init__(self, gw, c_arr, j_arr, tag):
        self.gw, self.c_arr, self.j_arr, self.tag = gw, c_arr, j_arr, tag

    def swap(self, base=0):
        return _ex_swap_halves(self.gw, base)

    def after_swap(self, got, base=0):
        self.s32, s16 = _chip_sum(self.gw, got, self.c_arr, name=f"rs_chip_sum{self.tag}")
        return _ex_scatter(s16, base)

    def after_scatter(self, got16):
        ghalf = _mesh_sum(self.s32, got16, self.j_arr, name=f"rs_mesh_sum{self.tag}")
        return _run_exchange(_ex_share_halves(ghalf), name=f"rs_share{self.tag}")[0]

    def run(self):
        got, = _run_exchange(self.swap(), name=f"rs_swap{self.tag}")
        got16, = _run_exchange(self.after_swap(got), name=f"rs_scatter{self.tag}")
        return self.after_scatter(got16)


def _all_reduce_small(vec, *, name):
    rows, d = vec.shape

    def body(x_ref, o_ref, gat, send_sems, recv_sems, local_sem):
        x, y, c, chips = _place()
        me, sibling = (x, y, c), (x, y, 1 - c)

        def slot(px, py, pc):
            return gat.at[4 * px + 2 * py + pc]

        def copy(k, block, to, src=None):
            return _remote(slot(*block) if src is None else src, slot(*block), send_sems.at[k], recv_sems.at[k], to)

        mine = pltpu.make_async_copy(x_ref, slot(*me), local_sem)
        mine.start()
        first = [copy(0, me, sibling, src=x_ref)]
        first += [copy(1 + j, me, (*chip, c), src=x_ref) for j, chip in enumerate(chips)]
        for cp in first:
            cp.start()
        passed = [copy(4 + j, (*chip, c), sibling) for j, chip in enumerate(chips)]
        for j, chip in enumerate(chips):
            copy(1 + j, (*chip, c), me).wait_recv()
            passed[j].start()
        copy(0, sibling, me).wait_recv()
        for j, chip in enumerate(chips):
            copy(4 + j, (*chip, 1 - c), me).wait_recv()
        for cp in first + passed:
            cp.wait_send()
        mine.wait()
        acc = gat[0]
        for dev in range(1, 8):
            acc = acc + gat[dev]
        o_ref[...] = acc

    vm = pl.BlockSpec(memory_space=pltpu.VMEM)
    return pl.pallas_call(
        body, name=name, in_specs=[vm], out_specs=vm, out_shape=jax.ShapeDtypeStruct((rows, d), F32),
        scratch_shapes=[pltpu.VMEM((8, rows, d), F32), pltpu.SemaphoreType.DMA((7,)), pltpu.SemaphoreType.DMA((7,)),
                        pltpu.SemaphoreType.DMA],
        compiler_params=_cp(32))(vec)


COL_SHARDED = ("w_in", "ffn_w_up")


def _to_pack_rows(name, shard):
    return shard.reshape(-1, D_MODEL)


def _full_from_blocks(name, blocks):
    rows = blocks.shape[1]
    if name in COL_SHARDED:
        return blocks.reshape(4, D_MODEL, rows).transpose(1, 0, 2).reshape(D_MODEL, 4 * rows)
    return blocks.reshape(4 * rows, D_MODEL)


def _blocks_from_full(name, full):
    if name in COL_SHARDED:
        cols = full.shape[1] // 4
        return full.reshape(D_MODEL, 4, cols).transpose(1, 0, 2).reshape(4, cols, D_MODEL)
    return full.reshape(4, full.shape[0] // 4, D_MODEL)


def _row(v):
    return v.reshape(1, -1)


SMALL = (("mix_norm_pre", (1024,), None), ("cv_w", (31, 256), 1), ("cv_b", (256,), None), ("cv_ln_g", (256,), None),
         ("cv_ln_b", (256,), None), ("cv_pw_w", (256, 256), 0), ("cv_pw_b", (256,), None),
         ("mix_norm_post", (1024,), None), ("x_norm_pre", (1024,), None), ("mem_norm", (1024,), None),
         ("x_norm_post", (1024,), None), ("ffn_norm_pre", (1024,), None), ("ffn_conv_w", (3, 5632), 1),
         ("ffn_conv_b", (5632,), None), ("ffn_norm_post", (1024,), None))
BIG = tuple(n for n, _ in PACK_ROWS)
WEIGHT_ORDER = ("mix_norm_pre", "w_in", "cv_w", "cv_b", "cv_ln_g", "cv_ln_b", "cv_pw_w", "cv_pw_b", "w_out",
                "mix_norm_post", "x_norm_pre", "mem_norm", "x_wq", "x_wk", "x_wv", "x_wo", "x_norm_post",
                "ffn_norm_pre", "ffn_w_up", "ffn_conv_w", "ffn_conv_b", "ffn_w_down", "ffn_norm_post")


def _flat_rows(parts):
    v = jnp.concatenate([p.reshape(-1) for p in parts])
    rows = -(-v.shape[0] // (8 * D_MODEL)) * 8
    return jnp.pad(v, (0, rows * D_MODEL - v.shape[0])).reshape(rows, D_MODEL)


def _small_to_rows(blocks):
    v = jnp.concatenate([b.reshape(-1) for b in blocks])
    return jnp.pad(v, (0, SMALL_ROWS * D_MODEL - v.shape[0])).reshape(SMALL_ROWS, D_MODEL)


def _small_from_rows(rows):
    flat, out, off = rows.reshape(-1), [], 0
    for _, shape, _ in SHARDED_SMALL:
        size = int(np.prod(shape))
        out.append(flat[off:off + size].reshape(shape))
        off += size
    return out


def _chip_block(full, j, shape, axis):
    return lax.slice_in_dim(full, j * shape[axis], (j + 1) * shape[axis], axis=axis)


REST_GROUP = ("w_in", "w_out")
XA_GROUP = ("x_wq", "x_wk", "x_wv", "x_wo")
FFN_GROUP = ("ffn_w_up", "ffn_w_down")


class _Weights:
    FIRST = (0, 768)
    OWN = ((768, 1024), (1792, 1664), (3456, 704))
    NEXT = ((0, 1024), (1024, 1024), (2048, 1408), (3456, 704))
    SLOTS = ("mix_in", "sb_fwd", "dl_fwd", "dl_mix", "ffn_up", "ffn_gate", "ffn_down")

    def __init__(self, packs):
        self.packs, self.pieces, self.landed, self.plan = packs, {}, None, {}
        for slot, piece in zip(self.SLOTS[:3], self.OWN):
            self.plan[(0, slot)] = (0,) + piece
        for l in range(len(packs) - 1):
            for slot, piece in zip(self.SLOTS[3:], self.NEXT):
                self.plan[(l, slot)] = (l + 1,) + piece
        first = _run_exchange(_ex_gather(packs[0], *self.FIRST), name="gather_first")[0]
        self.pieces[(0,) + self.FIRST] = _run_exchange(_ex_gather_forward(first), name="gather_first_forward")[0]

    def ride(self, layer, slot, call):
        start, todo, ex = self.plan.get((layer, slot)), [], None
        if start is not None:
            ex = _ex_gather(self.packs[start[0]], start[1], start[2])
            todo.append(("landed", start))
        if self.landed is not None:
            key, buf = self.landed
            forward = _ex_gather_forward(buf, SEM_BASE_2 if ex is not None else 0)
            ex = forward if ex is None else _ex_join(ex, forward)
            todo.append(("piece", key))
            self.landed = None
        outs = list(call(carry=ex))
        n = len(outs) - len(todo)
        for (kind, key), buf in zip(todo, outs[n:]):
            if kind == "landed":
                self.landed = (key, buf)
            else:
                self.pieces[key] = buf
        return outs[:n]

    def rows_of(self, layer, name):
        off = 0
        for n, rows in WEIGHT_PACK:
            if n == name:
                break
            off += rows
        for (l, r0, nrows), buf in self.pieces.items():
            if l == layer and r0 <= off < r0 + nrows:
                return buf[:, off - r0:off - r0 + rows, :]
        raise KeyError(f"{name} of layer {layer} is not gathered yet")

    def weight(self, layer, name):
        return _full_from_blocks(name, self.rows_of(layer, name))

    def small(self, layer):
        planes = lax.bitcast_convert_type(self.rows_of(layer, "small").astype(jnp.bfloat16), jnp.uint16)
        planes = planes.astype(jnp.uint32)
        bits = (planes[:, :SMALL_ROWS] << 16) | planes[:, SMALL_ROWS:]
        per_chip = [_small_from_rows(r) for r in lax.bitcast_convert_type(bits, F32)]
        return {n: jnp.concatenate([blocks[k] for blocks in per_chip], axis=axis)
                for k, (n, _, axis) in enumerate(SHARDED_SMALL)}


class _Params:
    def __init__(self, weights, layer, small):
        self.weights, self.layer, self.small, self.cache = weights, layer, small, {}

    def __getitem__(self, name):
        if name in self.small:
            return self.small[name]
        if name not in self.cache:
            if name in [n for n, _, _ in SHARDED_SMALL]:
                self.cache.update(self.weights.small(self.layer))
            else:
                self.cache[name] = self.weights.weight(self.layer, name)
        return self.cache[name]


def _layer_fwd(h0, mem, p, cos, sin, tag, ride):
    sv = {"h0": h0}
    n1, u = ride("mix_in", functools.partial(_rms_mm, h0, _row(p["mix_norm_pre"]), p["w_in"], tm=1024, tn=1408,
                                             out_dtype=F32, name=f"mix_in{tag}"))
    a_out, = ride("sb_fwd", functools.partial(_sb_fwd, u, name=f"sb_fwd{tag}"))
    b_out, c = _cv_fwd(u, p["cv_w"], _row(p["cv_b"]), _row(p["cv_ln_g"]), _row(p["cv_ln_b"]),
                       p["cv_pw_w"].astype(BF16), _row(p["cv_pw_b"]), name=f"cv_fwd{tag}")
    qkv = _rope_perm(u, cos, sin, name=f"rope_perm{tag}")
    o_p, l_p = ride("dl_fwd", functools.partial(_dl_fwd, qkv, name=f"dl_fwd{tag}"))
    c_out, o_dl, lse = ride("dl_mix", functools.partial(_dl_mix, o_p, l_p, name=f"dl_mix{tag}"))
    cat = jnp.concatenate([a_out, b_out, c_out], axis=1)
    y1, h1 = _mm_post(cat, p["w_out"], h0, _row(p["mix_norm_post"]), tm=512, name=f"mix_out{tag}")
    sv.update(n1=n1, u=u, c=c, qkv=qkv, o_dl=o_dl, lse=lse, cat=cat, y1=y1, h1=h1)

    n2, q = _rms_mm(h1, _row(p["x_norm_pre"]), p["x_wq"], tm=512, tn=1024, out_dtype=BF16, name=f"xa_q{tag}")
    wkv = jnp.concatenate([p["x_wk"], p["x_wv"]], axis=1)
    mem_n, kv = _rms_mm(mem, _row(p["mem_norm"]), wkv, tm=mem.shape[0], tn=1024, out_dtype=BF16, name=f"xa_kv{tag}")
    k, v = kv[:, :D_MODEL], kv[:, D_MODEL:]
    o_x = _xa_fwd(q, k, v, name=f"xa_fwd{tag}")
    y2, h2 = _mm_post(o_x, p["x_wo"], h1, _row(p["x_norm_post"]), tm=512, name=f"xa_out{tag}")
    sv.update(n2=n2, q=q, mem_n=mem_n, k=k, v=v, o_x=o_x, y2=y2, h2=h2, wkv=wkv)

    n3, up = ride("ffn_up", functools.partial(_rms_mm, h2, _row(p["ffn_norm_pre"]), p["ffn_w_up"], tm=1024, tn=1408,
                                              out_dtype=F32, name=f"ffn_up{tag}"))
    act, = ride("ffn_gate", functools.partial(_ff_gate_fwd, up, p["ffn_conv_w"], _row(p["ffn_conv_b"]),
                                              name=f"ffn_gate{tag}"))
    y3, h3 = ride("ffn_down", functools.partial(_mm_post, act, p["ffn_w_down"], h2, _row(p["ffn_norm_post"]), tm=512,
                                                name=f"ffn_down{tag}"))
    sv.update(n3=n3, up=up, act=act, y3=y3)
    return h3, sv


def _layer_bwd(dh3, mem, p, sv, cos, sin, tag, riding, new_rs):
    g = {}
    s8 = lambda part: part.sum(axis=0)
    rode = None

    dy3, dgp = _rms_bwd(sv["y3"], _row(p["ffn_norm_post"]), dh3, None, out_dtype=BF16, tm=512, name=f"ffn_post_b{tag}")
    g["ffn_norm_post"] = s8(dgp)
    dact = _mm_nt(dy3, p["ffn_w_down"], tm=512, tn=1408, out_dtype=F32, name=f"ffn_down_bx{tag}")
    g["ffn_w_down"] = _mm_tn(sv["act"], dy3, tk=1408, tn=1024, tm=2048, name=f"ffn_down_bw{tag}")
    dgu, dvu, dcw, dcb, *got = _ff_gate_bwd(sv["up"], dact, p["ffn_conv_w"], _row(p["ffn_conv_b"]),
                                            name=f"ffn_gate_b{tag}", carry=riding.swap() if riding else None)
    scatter = riding.after_swap(got[0]) if riding else None
    g["ffn_conv_w"] = jnp.concatenate([dcw[0], dcw[1]], axis=1).reshape(3, 8, 2 * D_FF).sum(axis=1)
    g["ffn_conv_b"] = jnp.concatenate([dcb[0], dcb[1]], axis=1).sum(axis=0)
    dup = jnp.concatenate([dgu, dvu], axis=1)
    dn3 = _mm_nt(dup, p["ffn_w_up"], tm=256, tn=512, out_dtype=F32, name=f"ffn_up_bx{tag}")
    g["ffn_w_up"] = _mm_tn(sv["n3"], dup, tk=512, tn=1408, tm=2048, name=f"ffn_up_bw{tag}")
    ffn_rs = new_rs(FFN_GROUP, g, f"{tag}_ffn")
    dh2, dgp = _rms_bwd(sv["h2"], _row(p["ffn_norm_pre"]), dn3, dh3, out_dtype=F32, tm=512, name=f"ffn_pre_b{tag}")
    g["ffn_norm_pre"] = s8(dgp)

    dy2, dgp = _rms_bwd(sv["y2"], _row(p["x_norm_post"]), dh2, None, out_dtype=BF16, tm=512, name=f"xa_post_b{tag}")
    g["x_norm_post"] = s8(dgp)
    do_x = _mm_nt(dy2, p["x_wo"], tm=512, tn=1024, out_dtype=BF16, name=f"xa_out_bx{tag}")
    g["x_wo"] = _mm_tn(sv["o_x"], dy2, tk=512, tn=1024, tm=2048, name=f"xa_out_bw{tag}")
    dq, dk, dv, got = _xa_bwd(sv["q"], sv["k"], sv["v"], do_x, name=f"xa_bwd{tag}", carry=ffn_rs.swap())
    ffn_scatter = ffn_rs.after_swap(got)
    dn2 = _mm_nt(dq, p["x_wq"], tm=512, tn=1024, out_dtype=F32, name=f"xa_q_bx{tag}")
    g["x_wq"] = _mm_tn(sv["n2"], dq, tk=512, tn=1024, tm=2048, name=f"xa_q_bw{tag}")
    dkv = jnp.concatenate([dk, dv], axis=1).astype(BF16)
    nm = mem.shape[0]
    dmem_n = _mm_nt(dkv, sv["wkv"], tm=nm, tn=1024, out_dtype=F32, name=f"xa_kv_bx{tag}")
    dwkv = _mm_tn(sv["mem_n"], dkv, tk=512, tn=2048, tm=nm, name=f"xa_kv_bw{tag}")
    g["x_wk"], g["x_wv"] = dwkv[:, :D_MODEL], dwkv[:, D_MODEL:]
    _, dgp = _rms_bwd(mem, _row(p["mem_norm"]), dmem_n, None, out_dtype=BF16, tm=nm, name=f"xa_mem_b{tag}")
    g["mem_norm"] = s8(dgp)
    xa_rs = new_rs(XA_GROUP, g, f"{tag}_xa")
    dh1, dgp, got = _rms_bwd(sv["h1"], _row(p["x_norm_pre"]), dn2, dh2, out_dtype=F32, tm=512, name=f"xa_pre_b{tag}",
                             carry=xa_rs.swap())
    xa_scatter = xa_rs.after_swap(got, SEM_BASE_2 if riding else 0)
    g["x_norm_pre"] = s8(dgp)

    dy1, dgp = _rms_bwd(sv["y1"], _row(p["mix_norm_post"]), dh1, None, out_dtype=BF16, tm=512, name=f"mix_post_b{tag}")
    g["mix_norm_post"] = s8(dgp)
    dcat = _mm_nt(dy1, p["w_out"], tm=512, tn=1024, out_dtype=F32, name=f"mix_out_bx{tag}")
    g["w_out"] = _mm_tn(sv["cat"], dy1, tk=512, tn=1024, tm=2048, name=f"mix_out_bw{tag}")
    u = sv["u"]
    dq_sb, dk_sb, dv_sb, *got = _sb_bwd(u, dcat, name=f"sb_bwd{tag}",
                                        carry=_ex_join(scatter, xa_scatter) if riding else xa_scatter)
    if riding:
        rode = riding.after_scatter(got[0])
    xa_rows = xa_rs.after_scatter(got[-1])
    pw_b16 = p["cv_pw_w"].astype(BF16)
    dc, dpw, vec = _cv_bwd_local(sv["c"], dcat, _row(p["cv_ln_g"]), _row(p["cv_ln_b"]), pw_b16, name=f"cv_bwd_a{tag}")
    g["cv_pw_w"] = dpw
    vec = vec.reshape(3, 8, CV_W).sum(axis=1)
    g["cv_pw_b"], g["cv_ln_g"], g["cv_ln_b"] = vec[0], vec[1], vec[2]
    du_cv, dcw, dcb = _cv_bwd_conv(u, dc, p["cv_w"], name=f"cv_bwd_b{tag}")
    g["cv_w"] = dcw.reshape(CV_K, 8, CV_W).sum(axis=1)
    g["cv_b"] = dcb.sum(axis=0)
    dop, stats = _dl_bwd_prep(dcat, sv["o_dl"], sv["lse"], name=f"dl_prep_b{tag}")
    cur, prev, got = _dl_bwd(sv["qkv"], dop, stats, name=f"dl_bwd{tag}", carry=ffn_scatter)
    ffn_rows = ffn_rs.after_scatter(got)
    du_dl = _dl_bwd_finish(cur, prev, cos, sin, name=f"dl_fin_b{tag}")
    du = jnp.concatenate([dq_sb.astype(BF16), dk_sb.astype(BF16), dv_sb.astype(BF16), du_cv, du_dl], axis=1)
    dn1 = _mm_nt(du, p["w_in"], tm=512, tn=512, out_dtype=F32, name=f"mix_in_bx{tag}")
    g["w_in"] = _mm_tn(sv["n1"], du, tk=512, tn=1408, tm=2048, name=f"mix_in_bw{tag}")
    dh0, dgp = _rms_bwd(sv["h0"], _row(p["mix_norm_pre"]), dn1, dh1, out_dtype=F32, tm=512, name=f"mix_pre_b{tag}")
    g["mix_norm_pre"] = s8(dgp)
    return dh0, g, (xa_rows, ffn_rows), rode


def _step(x, mem, positions, loss_target, w, m, v):
    depth = w["w_in"].shape[0]
    xi, yi, ci = lax.axis_index("x"), lax.axis_index("y"), lax.axis_index("c")
    chip = 2 * xi + yi
    h = x[0]
    mem0 = mem[0]
    s_len = h.shape[0]

    def pack_rows(n, l):
        if n == "small":
            bits = lax.bitcast_convert_type(_small_to_rows([w[name][l] for name, _, _ in SHARDED_SMALL]), jnp.uint32)
            planes = [(bits >> 16).astype(jnp.uint16), (bits & 0xFFFF).astype(jnp.uint16)]
            return jnp.concatenate([lax.bitcast_convert_type(p, jnp.bfloat16) for p in planes], axis=0)
        return _to_pack_rows(n, w[n][l]).astype(BF16)

    packs = [jnp.concatenate([pack_rows(n, l) for n, _ in WEIGHT_PACK], axis=0) for l in range(depth)]
    weights = _Weights(packs)
    params = [_Params(weights, l, {n: w[n][l] for n, _, axis in SMALL if axis is None}) for l in range(depth)]

    inv_freq = ROPE_THETA ** (-jnp.arange(HD // 2, dtype=F32) / (HD // 2))
    cos, sin = _rope_tables(positions.reshape(s_len, 1), jnp.tile(inv_freq, 4).reshape(1, LANES), name="rope_tables")

    saved = []
    for l in range(depth):
        h, sv = _layer_fwd(h, mem0, params[l], cos, sin, f"_l{l}", functools.partial(weights.ride, l))
        saved.append(sv)
    dh, sq = _loss_grad(h, loss_target[0], tm=512, name="loss_grad")
    loss = lax.psum(0.5 * jnp.sum(sq) / D_MODEL, ("x", "y", "c"))

    c_arr, j_arr = jnp.reshape(ci, (1,)).astype(jnp.int32), jnp.reshape(chip, (1,)).astype(jnp.int32)

    def new_rs(names, g, tag):
        blocks = [_blocks_from_full(n, g[n]) for n in names]
        if names is REST_GROUP:
            blocks.append(jnp.stack([_small_to_rows([_chip_block(g[n], j, shape, axis) for n, shape, axis in SHARDED_SMALL])
                                     for j in range(4)]))
        return _ReduceScatter(jnp.concatenate(blocks, axis=1), c_arr, j_arr, tag)

    grads, later_rows, rest_rows, pending = [None] * depth, [None] * depth, [None] * depth, None
    for l in reversed(range(depth)):
        dh, grads[l], later_rows[l], rode = _layer_bwd(dh, mem0, params[l], saved[l], cos, sin, f"_l{l}", pending, new_rs)
        if pending is not None:
            rest_rows[l + 1] = rode
        pending = new_rs(REST_GROUP, grads[l], f"_l{l}_rest")
    grad_x = dh[None]

    out_g, out_d, out_m, out_v = {}, {}, {}, {}
    pack_off, off = {}, 0
    for n, rows in PACK_ROWS:
        pack_off[n] = (off, rows)
        off += rows

    def reduced(l, n):
        start, rows = pack_off[n]
        for names, block in ((REST_GROUP, rest_rows[l]), (XA_GROUP, later_rows[l][0]), (FFN_GROUP, later_rows[l][1])):
            if n in names:
                return block[start - pack_off[names[0]][0]:][:rows]

    def update(n, carry=None):
        shard_shape = w[n].shape
        g_n = jnp.stack([reduced(l, n) for l in range(depth)]).reshape(shard_shape)
        flat = lambda a: a.reshape(-1, shard_shape[-1])
        d_n, m_n, v_n, *rode = _adamw(flat(w[n]), flat(g_n), flat(m[n]), flat(v[n]), name=f"adamw_{n}", carry=carry)
        out_g[n], out_d[n], out_m[n], out_v[n] = g_n, d_n.reshape(shard_shape), m_n.reshape(shard_shape), v_n.reshape(shard_shape)
        return rode

    rest_rows[0] = pending.run()
    for n, _ in PACK_ROWS:
        update(n)

    g_small = _all_reduce_small(_flat_rows([grads[l][n] for l in range(depth) for n, _, axis in SMALL if axis is None]),
                                name="all_reduce_small_grads").reshape(-1)
    local_g, off = {}, 0
    for l in range(depth):
        for n, shape, axis in SMALL:
            if axis is None:
                size = int(np.prod(shape))
                local_g.setdefault(n, []).append(g_small[off:off + size].reshape(shape))
                off += size
        small_rows = rest_rows[l][sum(pack_off[n][1] for n in REST_GROUP):]
        for (n, _, _), block in zip(SHARDED_SMALL, _small_from_rows(small_rows)):
            local_g.setdefault(n, []).append(block)
    names = [n for n, _, _ in SMALL]
    g_loc = {n: jnp.stack(local_g[n]) for n in names}
    d_s, m_s, v_s = _adamw(_flat_rows([w[n] for n in names]), _flat_rows([g_loc[n] for n in names]),
                           _flat_rows([m[n] for n in names]), _flat_rows([v[n] for n in names]), name="adamw_small")
    off = 0
    for n in names:
        size = int(np.prod(w[n].shape))
        take = lambda a: a.reshape(-1)[off:off + size].reshape(w[n].shape)
        out_g[n], out_d[n], out_m[n], out_v[n] = g_loc[n], take(d_s), take(m_s), take(v_s)
        off += size

    outs = [loss, grad_x]
    for group in (out_g, out_d, out_m, out_v):
        outs += [group[n] for n in WEIGHT_ORDER]
    return tuple(outs)


def kernel(x, mem, positions, mix_norm_pre, w_in, cv_w, cv_b, cv_ln_g, cv_ln_b, cv_pw_w, cv_pw_b, w_out, mix_norm_post, x_norm_pre, mem_norm, x_wq, x_wk, x_wv, x_wo, x_norm_post, ffn_norm_pre, ffn_w_up, ffn_conv_w, ffn_conv_b, ffn_w_down, ffn_norm_post, loss_target, m_mix_norm_pre, m_w_in, m_cv_w, m_cv_b, m_cv_ln_g, m_cv_ln_b, m_cv_pw_w, m_cv_pw_b, m_w_out, m_mix_norm_post, m_x_norm_pre, m_mem_norm, m_x_wq, m_x_wk, m_x_wv, m_x_wo, m_x_norm_post, m_ffn_norm_pre, m_ffn_w_up, m_ffn_conv_w, m_ffn_conv_b, m_ffn_w_down, m_ffn_norm_post, v_mix_norm_pre, v_w_in, v_cv_w, v_cv_b, v_cv_ln_g, v_cv_ln_b, v_cv_pw_w, v_cv_pw_b, v_w_out, v_mix_norm_post, v_x_norm_pre, v_mem_norm, v_x_wq, v_x_wk, v_x_wv, v_x_wo, v_x_norm_post, v_ffn_norm_pre, v_ffn_w_up, v_ffn_conv_w, v_ffn_conv_b, v_ffn_w_down, v_ffn_norm_post):
    w = dict(zip(WEIGHT_ORDER, (mix_norm_pre, w_in, cv_w, cv_b, cv_ln_g, cv_ln_b, cv_pw_w, cv_pw_b, w_out, mix_norm_post, x_norm_pre, mem_norm, x_wq, x_wk, x_wv, x_wo, x_norm_post, ffn_norm_pre, ffn_w_up, ffn_conv_w, ffn_conv_b, ffn_w_down, ffn_norm_post)))
    m = dict(zip(WEIGHT_ORDER, (m_mix_norm_pre, m_w_in, m_cv_w, m_cv_b, m_cv_ln_g, m_cv_ln_b, m_cv_pw_w, m_cv_pw_b, m_w_out, m_mix_norm_post, m_x_norm_pre, m_mem_norm, m_x_wq, m_x_wk, m_x_wv, m_x_wo, m_x_norm_post, m_ffn_norm_pre, m_ffn_w_up, m_ffn_conv_w, m_ffn_conv_b, m_ffn_w_down, m_ffn_norm_post)))
    v = dict(zip(WEIGHT_ORDER, (v_mix_norm_pre, v_w_in, v_cv_w, v_cv_b, v_cv_ln_g, v_cv_ln_b, v_cv_pw_w, v_cv_pw_b, v_w_out, v_mix_norm_post, v_x_norm_pre, v_mem_norm, v_x_wq, v_x_wk, v_x_wv, v_x_wo, v_x_norm_post, v_ffn_norm_pre, v_ffn_w_up, v_ffn_conv_w, v_ffn_conv_b, v_ffn_w_down, v_ffn_norm_post)))
    return _step(x, mem, positions, loss_target, w, m, v)
```

```python
import functools

import jax
import jax.numpy as jnp
import numpy as np
from jax import lax
from jax.experimental import pallas as pl
from jax.experimental.pallas import tpu as pltpu

F32, BF16 = jnp.float32, jnp.bfloat16
MESH = pl.DeviceIdType.MESH
EPS = 1e-6
LANES = 128
BLK = 128
HD = 64
D_MODEL = 1024
D_FF = 2816
SB_W, CV_W, DL_W = 256, 256, 512
CV_K = 31
ROPE_THETA = 10000.0
DILATIONS = (1, 4, 16)
X_HEADS, X_HD = 4, 256
ADAM_LR, ADAM_B1, ADAM_B2, ADAM_EPS, ADAM_WD, ADAM_STEP = 0.001, 0.9, 0.999, 1e-08, 0.01, 10
NEG_INF = float("-inf")
MIB = 1 << 20

PACK_ROWS = (("w_in", 704), ("w_out", 256), ("x_wq", 256), ("x_wk", 256), ("x_wv", 256), ("x_wo", 256),
             ("ffn_w_up", 1408), ("ffn_w_down", 704))
PACK_RL = sum(r for _, r in PACK_ROWS)
SHARDED_SMALL = (("cv_w", (31, 64), 1), ("ffn_conv_w", (3, 1408), 1), ("cv_pw_w", (64, 256), 0))
SMALL_ROWS = 32
WEIGHT_PACK = (PACK_ROWS[0], ("small", 2 * SMALL_ROWS)) + PACK_ROWS[1:]


def _cp(vmem_mb=48):
    return pltpu.CompilerParams(vmem_limit_bytes=vmem_mb * MIB)


def _dot(a, b):
    return jnp.dot(a, b, preferred_element_type=F32)


def _dot_nt(a, b):
    return lax.dot_general(a, b, (((1,), (1,)), ((), ())), preferred_element_type=F32)


def _dot_tn(a, b):
    return lax.dot_general(a, b, (((0,), (0,)), ((), ())), preferred_element_type=F32)


def _dot_hilo(x, m):
    hi = x.astype(BF16)
    lo = (x - hi.astype(F32)).astype(BF16)
    return _dot(hi, m) + _dot(lo, m)


def _rowsum8(x):
    t, c = x.shape
    return x.reshape(t // 8, 8, c).sum(axis=0)


def _acc_out(ref, i, val):
    @pl.when(i == 0)
    def _():
        ref[...] = val

    @pl.when(i > 0)
    def _():
        ref[...] += val


def _tile(n, cap, mult=8):
    t = min(n, cap)
    while n % t or t % mult:
        t -= 1
    return t


def _rms_mm(x, g, w, *, tm, tn, out_dtype, name, carry=None):
    m, d = x.shape
    n_out = w.shape[1]

    def body(x_ref, g_ref, w_ref, n_ref, o_ref):
        @pl.when(pl.program_id(1) == 0)
        def _():
            xv = x_ref[...]
            r = lax.rsqrt(jnp.mean(xv * xv, axis=-1, keepdims=True) + EPS)
            n_ref[...] = (xv * r * g_ref[...]).astype(BF16)

        o_ref[...] = _dot(n_ref[...], w_ref[...]).astype(out_dtype)

    return _call(
        body, grid=(m // tm, n_out // tn), name=name, carry=carry,
        in_specs=[pl.BlockSpec((tm, d), lambda i, j: (i, 0)), pl.BlockSpec((1, d), lambda i, j: (0, 0)),
                  pl.BlockSpec((d, tn), lambda i, j: (0, j))],
        out_specs=[pl.BlockSpec((tm, d), lambda i, j: (i, 0)), pl.BlockSpec((tm, tn), lambda i, j: (i, j))],
        out_shape=[jax.ShapeDtypeStruct((m, d), BF16), jax.ShapeDtypeStruct((m, n_out), out_dtype)],
        args=(x, g, w))


def _mm_post(a, w, h, g, *, tm, name, carry=None):
    m, k = a.shape
    d = w.shape[1]

    def body(a_ref, w_ref, h_ref, g_ref, y_ref, ho_ref):
        y = _dot(a_ref[...], w_ref[...])
        y_ref[...] = y
        r = lax.rsqrt(jnp.mean(y * y, axis=-1, keepdims=True) + EPS)
        ho_ref[...] = h_ref[...] + y * r * g_ref[...]

    return _call(
        body, grid=(m // tm,), name=name, carry=carry,
        in_specs=[pl.BlockSpec((tm, k), lambda i: (i, 0)), pl.BlockSpec((k, d), lambda i: (0, 0)),
                  pl.BlockSpec((tm, d), lambda i: (i, 0)), pl.BlockSpec((1, d), lambda i: (0, 0))],
        out_specs=[pl.BlockSpec((tm, d), lambda i: (i, 0)), pl.BlockSpec((tm, d), lambda i: (i, 0))],
        out_shape=[jax.ShapeDtypeStruct((m, d), F32), jax.ShapeDtypeStruct((m, d), F32)],
        args=(a, w, h, g))


def _mm_nt(a, w, *, tm, tn, out_dtype, name):
    m, k = a.shape
    n_out = w.shape[0]

    def body(a_ref, w_ref, o_ref):
        o_ref[...] = _dot_nt(a_ref[...], w_ref[...]).astype(out_dtype)

    return pl.pallas_call(
        body, grid=(n_out // tn, m // tm), name=name,
        in_specs=[pl.BlockSpec((tm, k), lambda j, i: (i, 0)), pl.BlockSpec((tn, k), lambda j, i: (j, 0))],
        out_specs=pl.BlockSpec((tm, tn), lambda j, i: (i, j)),
        out_shape=jax.ShapeDtypeStruct((m, n_out), out_dtype),
        compiler_params=_cp())(a, w)


def _mm_tn(x, dy, *, tk, tn, tm, name):
    m, k = x.shape
    n_out = dy.shape[1]

    def body(x_ref, d_ref, o_ref):
        _acc_out(o_ref, pl.program_id(2), _dot_tn(x_ref[...], d_ref[...]))

    return pl.pallas_call(
        body, grid=(k // tk, n_out // tn, m // tm), name=name,
        in_specs=[pl.BlockSpec((tm, tk), lambda a, b, c: (c, a)), pl.BlockSpec((tm, tn), lambda a, b, c: (c, b))],
        out_specs=pl.BlockSpec((tk, tn), lambda a, b, c: (a, b)),
        out_shape=jax.ShapeDtypeStruct((k, n_out), F32),
        compiler_params=_cp())(x, dy)


def _rms_bwd(x, g, dout, res, *, out_dtype, tm, name, carry=None):
    m, d = x.shape
    has_res = res is not None

    def body(*refs):
        if has_res:
            x_ref, g_ref, d_ref, r_ref, dx_ref, dg_ref = refs
        else:
            x_ref, g_ref, d_ref, dx_ref, dg_ref = refs
        xv = x_ref[...]
        dv = d_ref[...].astype(F32)
        r = lax.rsqrt(jnp.mean(xv * xv, axis=-1, keepdims=True) + EPS)
        xh = xv * r
        dxh = dv * g_ref[...]
        dx = r * (dxh - xh * jnp.mean(dxh * xh, axis=-1, keepdims=True))
        if has_res:
            dx = dx + r_ref[...]
        dx_ref[...] = dx.astype(out_dtype)
        _acc_out(dg_ref, pl.program_id(0), _rowsum8(dv * xh))

    row = pl.BlockSpec((tm, d), lambda i: (i, 0))
    ins = [row, pl.BlockSpec((1, d), lambda i: (0, 0)), row] + ([row] if has_res else [])
    args = (x, g, dout) + ((res,) if has_res else ())
    return _call(
        body, grid=(m // tm,), name=name, carry=carry, in_specs=ins,
        out_specs=[row, pl.BlockSpec((8, d), lambda i: (0, 0))],
        out_shape=[jax.ShapeDtypeStruct((m, d), out_dtype), jax.ShapeDtypeStruct((8, d), F32)], args=args)


def _loss_grad(h, tgt, *, tm, name):
    m, d = h.shape

    def body(h_ref, t_ref, dh_ref, p_ref):
        e = h_ref[...] - t_ref[...]
        dh_ref[...] = e / d
        _acc_out(p_ref, pl.program_id(0), _rowsum8(e * e))

    row = pl.BlockSpec((tm, d), lambda i: (i, 0))
    return pl.pallas_call(
        body, grid=(m // tm,), name=name, in_specs=[row, row],
        out_specs=[row, pl.BlockSpec((8, d), lambda i: (0, 0))],
        out_shape=[jax.ShapeDtypeStruct((m, d), F32), jax.ShapeDtypeStruct((8, d), F32)],
        compiler_params=_cp())(h, tgt)


def _adamw(w, g, m, v, *, name, carry=None):
    r, c = w.shape
    tr = _tile(r, 256)

    def body(w_ref, g_ref, m_ref, v_ref, d_ref, mo_ref, vo_ref):
        gv = g_ref[...]
        m2 = ADAM_B1 * m_ref[...] + (1.0 - ADAM_B1) * gv
        v2 = ADAM_B2 * v_ref[...] + (1.0 - ADAM_B2) * jnp.square(gv)
        m_hat = m2 / (1.0 - ADAM_B1 ** ADAM_STEP)
        v_hat = v2 / (1.0 - ADAM_B2 ** ADAM_STEP)
        d_ref[...] = -ADAM_LR * (m_hat / (jnp.sqrt(v_hat) + ADAM_EPS) + ADAM_WD * w_ref[...])
        mo_ref[...] = m2
        vo_ref[...] = v2

    blk = pl.BlockSpec((tr, c), lambda i: (i, 0))
    return _call(body, grid=(r // tr,), name=name, carry=carry, in_specs=[blk] * 4, out_specs=[blk] * 3,
                 out_shape=[jax.ShapeDtypeStruct((r, c), F32)] * 3, args=(w, g, m, v))


def _head_masks():
    lane = lax.broadcasted_iota(jnp.int32, (BLK, LANES), 1)
    row = lax.broadcasted_iota(jnp.int32, (BLK, LANES), 0)
    return lane, row, lane < HD


def _sb_scores(q_a, k, before):
    z = _dot_nt(q_a, k)
    sp = jnp.log1p(jnp.exp(-jnp.abs(z)))
    ls_pos = jnp.minimum(z, 0.0) - sp
    lkeep = jnp.where(before, ls_pos - z, 0.0)
    return ls_pos, lkeep


SB_DEAD = -104.0


def _sb_alive(jj, i, carry):
    return jnp.logical_and(jj <= i, jnp.max(carry) > SB_DEAD)


SB_QB_FWD = 2
SB_QB = 2


def _sb_before(jj, qb=SB_QB):
    lane = lax.broadcasted_iota(jnp.int32, (qb * 2 * BLK, LANES), 1)
    row = lax.broadcasted_iota(jnp.int32, (qb * 2 * BLK, LANES), 0)
    below_diag = jj - (qb - 1) + row // (2 * BLK)
    return jnp.logical_or(below_diag > 0, jnp.logical_and(below_diag == 0, lane < row % BLK))


def _sb_stack(x, lane_h, qb=SB_QB):
    return jnp.concatenate([_stack_heads(x[b * BLK:(b + 1) * BLK], lane_h) for b in range(qb)], axis=0)


def _sb_unstack(x, lane_h, qb=SB_QB):
    return jnp.concatenate([jnp.where(lane_h, x[2 * b * BLK:(2 * b + 1) * BLK], x[(2 * b + 1) * BLK:(2 * b + 2) * BLK])
                            for b in range(qb)], axis=0)


SB_ROWS = SB_QB * 2 * BLK


def _sb_fwd(u, *, name, carry=None):
    s_len = u.shape[0]
    qb = SB_QB_FWD
    qrows, rows = qb * BLK, qb * 2 * BLK

    def body(q_ref, k_ref, v_ref, o_ref):
        top = pl.program_id(0) * qb + qb - 1
        lane, row, lane_h = _head_masks()
        suffix = (row > lane).astype(BF16)
        pairs = [slice(hp * LANES, (hp + 1) * LANES) for hp in range(2)]
        qs = [_sb_stack(q_ref[:, cs] * 0.125, lane_h, qb) for cs in pairs]

        def step(state):
            jj, ccs, accs = state[0], state[1:3], state[3:5]
            rows_k = pl.ds(pl.multiple_of((top - jj) * BLK, BLK), BLK)
            before = _sb_before(jj, qb)
            scores = [_sb_scores(q, k_ref[rows_k, cs].astype(BF16), before) for q, cs in zip(qs, pairs)]
            between = [_dot_hilo(lkeep, suffix) + cc for (_, lkeep), cc in zip(scores, ccs)]
            atts = [jnp.where(before, jnp.exp(ls_pos + b), 0.0).astype(BF16) for (ls_pos, _), b in zip(scores, between)]
            new_cc = [cc + jnp.sum(lkeep, axis=1, keepdims=True) for (_, lkeep), cc in zip(scores, ccs)]
            new_acc = [acc + _dot(a, v_ref[rows_k, cs].astype(BF16)) for a, acc, cs in zip(atts, accs, pairs)]
            return (jj + 1, *new_cc, *new_acc)

        zc, za = jnp.zeros((rows, 1), F32), jnp.zeros((rows, LANES), F32)
        res = lax.while_loop(lambda st: _sb_alive(st[0], top, jnp.maximum(st[1], st[2])), step,
                             (jnp.int32(0), zc, zc, za, za))
        for hp, cs in enumerate(pairs):
            o_ref[:, cs] = _sb_unstack(res[3 + hp], lane_h, qb).astype(BF16)

    wide = 2 * LANES
    return _call(
        body, grid=(s_len // qrows,), name=name, carry=carry,
        in_specs=[pl.BlockSpec((qrows, wide), lambda i: (i, 0)), pl.BlockSpec((s_len, wide), lambda i: (0, 1)),
                  pl.BlockSpec((s_len, wide), lambda i: (0, 2))],
        out_specs=[pl.BlockSpec((qrows, wide), lambda i: (i, 0))],
        out_shape=[jax.ShapeDtypeStruct((s_len, SB_W), BF16)], args=(u, u, u))


def _sb_bwd(u, dcat, *, name, carry=None):
    s_len = u.shape[0]
    nq = s_len // BLK
    qrows = SB_QB * BLK

    def body(q_ref, k_ref, v_ref, do_ref, dq_ref, dk_ref, dv_ref, g_scr, b_scr):
        step = pl.program_id(1)
        top = step * SB_QB + SB_QB - 1
        lane, row, lane_h = _head_masks()
        suffix = (row > lane).astype(BF16)
        prefix = (row < lane).astype(BF16)
        qf = q_ref[...]
        qs = _sb_stack(qf * 0.125, lane_h)
        qu = _sb_stack(qf, lane_h)
        dos = _sb_stack(do_ref[...], lane_h)

        @pl.when(step == 0)
        def _():
            dk_ref[...] = jnp.zeros_like(dk_ref)
            dv_ref[...] = jnp.zeros_like(dv_ref)

        def down(state):
            jj, cc = state
            j = top - jj
            off = pl.multiple_of(j * BLK, BLK)
            k = k_ref[pl.ds(off, BLK), :].astype(BF16)
            v = v_ref[pl.ds(off, BLK), :].astype(BF16)
            before = _sb_before(jj)
            ls_pos, lkeep = _sb_scores(qs, k, before)
            between = _dot_hilo(lkeep, suffix) + cc
            att = jnp.where(before, jnp.exp(ls_pos + between), 0.0)
            g_scr[j] = att * _dot_nt(dos, v)
            b_scr[j] = jnp.exp(ls_pos)
            dv_ref[pl.ds(off, BLK), :] += _dot_tn(att.astype(BF16), dos)
            return jj + 1, cc + jnp.sum(lkeep, axis=1, keepdims=True)

        zc = jnp.zeros((SB_ROWS, 1), F32)
        visited = lax.while_loop(lambda st: _sb_alive(st[0], top, st[1]), down, (jnp.int32(0), zc))[0]

        def up(j, carry):
            pc, dq = carry
            off = pl.multiple_of(j * BLK, BLK)
            k = k_ref[pl.ds(off, BLK), :].astype(BF16)
            g, beta = g_scr[j], b_scr[j]
            below = _dot_hilo(g, prefix) + pc
            dz = (jnp.where(_sb_before(top - j), g * (1.0 - beta) - beta * below, 0.0) * 0.125).astype(BF16)
            dk_ref[pl.ds(off, BLK), :] += _dot_tn(dz, qu)
            return pc + jnp.sum(g, axis=1, keepdims=True), dq + _dot(dz, k)

        dq = lax.fori_loop(top + 1 - visited, top + 1, up, (zc, jnp.zeros((SB_ROWS, LANES), F32)))[1]
        dq_ref[...] = _sb_unstack(dq, lane_h)

    col = lambda c0: pl.BlockSpec((s_len, LANES), lambda hp, i: (0, c0 + hp))
    blk = pl.BlockSpec((qrows, LANES), lambda hp, i: (i, hp))
    acc = pl.BlockSpec((s_len, LANES), lambda hp, i: (0, hp))
    return _call(
        body, grid=(2, s_len // qrows), name=name, carry=carry, in_specs=[blk, col(2), col(4), blk],
        out_specs=[blk, acc, acc], out_shape=[jax.ShapeDtypeStruct((s_len, SB_W), F32)] * 3,
        scratch_shapes=[pltpu.VMEM((nq, SB_ROWS, LANES), F32), pltpu.VMEM((nq, SB_ROWS, LANES), F32)],
        vmem_mb=56, args=(u, u, u, dcat))


CV_T = 512
CV_H = 32
CV_STRIP = 64


def _cv_specs(s_len):
    cur = lambda c: pl.BlockSpec((CV_T, CV_W), lambda i: (i, c))
    prev = lambda c: pl.BlockSpec((CV_H, CV_W), lambda i: (jnp.maximum(i * (CV_T // CV_H) - 1, 0), c))
    nxt = lambda c: pl.BlockSpec((CV_H, CV_W),
                                 lambda i: (jnp.minimum((i + 1) * (CV_T // CV_H), s_len // CV_H - 1), c))
    full = lambda r: pl.BlockSpec((r, CV_W), lambda i: (0, 0))
    return cur, prev, nxt, full


def _glu_into(gp_ref, val_ref, gate_ref, valp_ref, gatep_ref, i):
    gp_ref[0:CV_H, :] = jnp.where(i > 0, valp_ref[...] * jax.nn.sigmoid(gatep_ref[...]), 0.0)
    gp_ref[CV_H:, :] = val_ref[...] * jax.nn.sigmoid(gate_ref[...])


CV_SH = CV_T + CV_H - 8


def _shifted_copies(sh_ref, slab_ref):
    for r in range(1, 8):
        sh_ref[r - 1] = slab_ref[pl.ds(r, CV_SH), :]


def _tap(sh_ref, slab_ref, off, r0, rows):
    if off % 8 == 0:
        return slab_ref[pl.ds(off + r0, rows), :]
    return sh_ref[off % 8 - 1, pl.ds(off - off % 8 + r0, rows), :]


def _cv_fwd(u, cv_w, cv_b, ln_g, ln_b, pw_w, pw_b, *, name):
    s_len = u.shape[0]
    cur, prev, _, full = _cv_specs(s_len)

    def body(val_ref, gate_ref, valp_ref, gatep_ref, w_ref, b_ref, g_ref, be_ref, pw_ref, pb_ref,
             o_ref, c_ref, gp_ref, sh_ref):
        _glu_into(gp_ref, val_ref, gate_ref, valp_ref, gatep_ref, pl.program_id(0))
        _shifted_copies(sh_ref, gp_ref)
        for r0, rows in _strips(CV_T, CV_STRIP):
            acc = jnp.zeros((rows, CV_W), F32) + b_ref[...]
            for k in range(CV_K):
                acc = acc + w_ref[k:k + 1, :] * _tap(sh_ref, gp_ref, CV_H - CV_K + 1 + k, r0, rows)
            c_ref[r0:r0 + rows, :] = acc
        acc = c_ref[...]
        mu = jnp.mean(acc, axis=-1, keepdims=True)
        xc = acc - mu
        xh = xc * lax.rsqrt(jnp.mean(xc * xc, axis=-1, keepdims=True) + EPS)
        a = xh * g_ref[...] + be_ref[...]
        s = a * jax.nn.sigmoid(a)
        o_ref[...] = (_dot(s.astype(BF16), pw_ref[...]) + pb_ref[...]).astype(BF16)

    return pl.pallas_call(
        body, grid=(s_len // CV_T,), name=name,
        in_specs=[cur(3), cur(4), prev(3), prev(4), full(CV_K), full(1), full(1), full(1), full(CV_W), full(1)],
        out_specs=[cur(0), cur(0)],
        out_shape=[jax.ShapeDtypeStruct((s_len, CV_W), BF16), jax.ShapeDtypeStruct((s_len, CV_W), F32)],
        scratch_shapes=[pltpu.VMEM((CV_T + CV_H, CV_W), F32), pltpu.VMEM((7, CV_SH, CV_W), F32)],
        compiler_params=_cp())(u, u, u, u, cv_w, cv_b, ln_g, ln_b, pw_w, pw_b)


def _cv_bwd_local(c, dcat, ln_g, ln_b, pw_w, *, name):
    s_len = c.shape[0]
    cur, _, _, full = _cv_specs(s_len)

    def body(c_ref, db_ref, g_ref, be_ref, pw_ref, dc_ref, dpw_ref, vec_ref):
        i = pl.program_id(0)
        cv = c_ref[...]
        db = db_ref[...]
        mu = jnp.mean(cv, axis=-1, keepdims=True)
        xc = cv - mu
        rstd = lax.rsqrt(jnp.mean(xc * xc, axis=-1, keepdims=True) + EPS)
        xh = xc * rstd
        a = xh * g_ref[...] + be_ref[...]
        sg = jax.nn.sigmoid(a)
        s = a * sg
        dbb = db.astype(BF16)
        ds = _dot_nt(dbb, pw_ref[...])
        da = ds * (sg * (1.0 + a * (1.0 - sg)))
        dxh = da * g_ref[...]
        dc_ref[...] = rstd * (dxh - jnp.mean(dxh, axis=-1, keepdims=True)
                              - xh * jnp.mean(dxh * xh, axis=-1, keepdims=True))
        _acc_out(dpw_ref, i, _dot_tn(s.astype(BF16), dbb))
        _acc_out(vec_ref, i, jnp.concatenate([_rowsum8(db), _rowsum8(da * xh), _rowsum8(da)], axis=0))

    return pl.pallas_call(
        body, grid=(s_len // CV_T,), name=name,
        in_specs=[cur(0), cur(1), full(1), full(1), full(CV_W)],
        out_specs=[cur(0), full(CV_W), full(24)],
        out_shape=[jax.ShapeDtypeStruct((s_len, CV_W), F32), jax.ShapeDtypeStruct((CV_W, CV_W), F32),
                   jax.ShapeDtypeStruct((24, CV_W), F32)], compiler_params=_cp())(c, dcat, ln_g, ln_b, pw_w)


def _cv_bwd_conv(u, dc, cv_w, *, name):
    s_len = u.shape[0]
    cur, prev, nxt, full = _cv_specs(s_len)
    last = s_len // CV_T - 1

    def body(val_ref, gate_ref, valp_ref, gatep_ref, dc_ref, dcn_ref, w_ref, du_ref, dw_ref, dbias_ref,
             gp_ref, dcp_ref, gsh_ref, dsh_ref):
        i = pl.program_id(0)
        _glu_into(gp_ref, val_ref, gate_ref, valp_ref, gatep_ref, i)
        dcv = dc_ref[...]
        dcp_ref[0:CV_T, :] = dcv
        dcp_ref[CV_T:, :] = jnp.where(i < last, dcn_ref[...], 0.0)
        _shifted_copies(gsh_ref, gp_ref)
        _shifted_copies(dsh_ref, dcp_ref)
        strips = _strips(CV_T, CV_STRIP)
        for r0, rows in strips:
            dg = jnp.zeros((rows, CV_W), F32)
            for k in range(CV_K):
                dg = dg + w_ref[k:k + 1, :] * _tap(dsh_ref, dcp_ref, CV_K - 1 - k, r0, rows)
            sg = jax.nn.sigmoid(gate_ref[r0:r0 + rows, :])
            du_ref[r0:r0 + rows, 0:CV_W] = (dg * sg).astype(BF16)
            du_ref[r0:r0 + rows, CV_W:] = (dg * val_ref[r0:r0 + rows, :] * sg * (1.0 - sg)).astype(BF16)
        parts = []
        for k in range(CV_K):
            part = jnp.zeros((8, CV_W), F32)
            for r0, rows in strips:
                part = part + _rowsum8(dc_ref[r0:r0 + rows, :] * _tap(gsh_ref, gp_ref, CV_H - CV_K + 1 + k, r0, rows))
            parts.append(part)
        _acc_out(dw_ref, i, jnp.concatenate(parts, axis=0))
        _acc_out(dbias_ref, i, _rowsum8(dcv))

    return pl.pallas_call(
        body, grid=(s_len // CV_T,), name=name,
        in_specs=[cur(3), cur(4), prev(3), prev(4), cur(0), nxt(0), full(CV_K)],
        out_specs=[pl.BlockSpec((CV_T, 2 * CV_W), lambda i: (i, 0)), full(CV_K * 8), full(8)],
        out_shape=[jax.ShapeDtypeStruct((s_len, 2 * CV_W), BF16), jax.ShapeDtypeStruct((CV_K * 8, CV_W), F32),
                   jax.ShapeDtypeStruct((8, CV_W), F32)],
        scratch_shapes=[pltpu.VMEM((CV_T + CV_H, CV_W), F32)] * 2 + [pltpu.VMEM((7, CV_SH, CV_W), F32)] * 2,
        compiler_params=_cp())(u, u, u, u, dc, dc, cv_w)


def _rope_tables(pos_col, inv_freq_row, *, name):
    s_len = pos_col.shape[0]

    def body(p_ref, f_ref, cos_ref, sin_ref):
        ang = p_ref[...].astype(F32) * f_ref[...]
        lane = lax.broadcasted_iota(jnp.int32, (s_len, LANES), 1)
        sn = jnp.sin(ang)
        cos_ref[...] = jnp.cos(ang)
        sin_ref[...] = jnp.where(lane % HD < HD // 2, -sn, sn)

    return pl.pallas_call(body, name=name, out_shape=[jax.ShapeDtypeStruct((s_len, LANES), F32)] * 2,
                          compiler_params=_cp())(pos_col, inv_freq_row)


def _rot_half(x):
    lane = lax.broadcasted_iota(jnp.int32, x.shape, 1)
    return jnp.where(lane % HD < HD // 2, pltpu.roll(x, LANES - HD // 2, 1), pltpu.roll(x, HD // 2, 1))


def _permute_rows(dst_ref, src_ref, d, dtype):
    s_len = src_ref.shape[0]
    seg = s_len // d
    if d == 1:
        dst_ref[...] = src_ref[...].astype(dtype)
        return
    for r in range(d):
        dst_ref[r * seg:(r + 1) * seg, :] = src_ref[pl.ds(r, seg, stride=d), :].astype(dtype)


def _unpermute_rows(dst_ref, src_ref, d):
    s_len = src_ref.shape[0]
    seg = s_len // d
    if d == 1:
        dst_ref[...] = src_ref[...]
        return
    for r in range(d):
        dst_ref[pl.ds(r, seg, stride=d), :] = src_ref[r * seg:(r + 1) * seg, :]


def _rope_perm(u, cos, sin, *, name):
    s_len = u.shape[0]

    def body(x_ref, cos_ref, sin_ref, o_ref, scr):
        a = pl.program_id(0)
        x = x_ref[...]
        rot = a < 2
        scr[...] = x * jnp.where(rot, cos_ref[...], 1.0) + _rot_half(x) * jnp.where(rot, sin_ref[...], 0.0)
        for n, d in enumerate(DILATIONS):
            _permute_rows(o_ref.at[n], scr, d, BF16)

    tab = pl.BlockSpec((s_len, LANES), lambda a, cb: (0, 0))
    return pl.pallas_call(
        body, grid=(3, 4), name=name,
        in_specs=[pl.BlockSpec((s_len, LANES), lambda a, cb: (0, 10 + 4 * a + cb)), tab, tab],
        out_specs=pl.BlockSpec((None, 3, s_len, LANES), lambda a, cb: (a, 0, 0, cb)),
        out_shape=jax.ShapeDtypeStruct((3, 3, s_len, DL_W), BF16),
        scratch_shapes=[pltpu.VMEM((s_len, LANES), F32)], compiler_params=_cp())(u, cos, sin)


DL_UNROLL = 4


def _dl_band(rows):
    lane = lax.broadcasted_iota(jnp.int32, (rows, LANES), 1)
    row = lax.broadcasted_iota(jnp.int32, (rows, LANES), 0) % BLK
    return lane <= row, lane >= row


def _dl_first(s_len, n, i):
    nb = jnp.where(n == 0, s_len // BLK, jnp.where(n == 1, s_len // (BLK * DILATIONS[1]),
                                                   s_len // (BLK * DILATIONS[2])))
    return lax.rem(i, nb) == 0


def _stack_heads(x, lane_h):
    return jnp.concatenate([jnp.where(lane_h, x, 0.0), jnp.where(lane_h, 0.0, x)], axis=0).astype(BF16)


def _dl_rows(i):
    cur = pl.ds(pl.multiple_of(i * BLK, BLK), BLK)
    prev = pl.ds(pl.multiple_of(jnp.maximum(i - 1, 0) * BLK, BLK), BLK)
    return cur, prev


def _dl_in_specs(s_len):
    return [pl.BlockSpec((None, None, s_len, LANES), functools.partial(lambda a, n, hp: (a, n, 0, hp), a))
            for a in range(3)]


def _dl_fwd(qkv, *, name, carry=None):
    s_len = qkv.shape[2]

    def body(q_ref, k_ref, v_ref, o_ref, l_ref):
        n = pl.program_id(0)
        lane_h = _head_masks()[2]
        band_c, band_p = _dl_band(2 * BLK)
        ones = jnp.ones((BLK, LANES), BF16)

        @pl.loop(0, s_len // BLK, step=DL_UNROLL)
        def _(i0):
            blocks = [i0 + t for t in range(DL_UNROLL)]
            rows = [_dl_rows(i) for i in blocks]
            scores = []
            for cur, prev in rows:
                qs = _stack_heads(q_ref[cur, :] * 0.125, lane_h)
                scores.append((_dot_nt(qs, k_ref[cur, :]), _dot_nt(qs, k_ref[prev, :])))
            probs = []
            for i, (sc, sp) in zip(blocks, scores):
                sc = jnp.where(band_c, sc, NEG_INF)
                sp = jnp.where(jnp.logical_and(band_p, jnp.logical_not(_dl_first(s_len, n, i))), sp, NEG_INF)
                m = jnp.max(jnp.maximum(sc, sp), axis=1, keepdims=True)
                probs.append((jnp.exp(sc - m).astype(BF16), jnp.exp(sp - m).astype(BF16), m))
            for (cur, prev), (pc, pp, m) in zip(rows, probs):
                r = (_dot(pc, jnp.concatenate([v_ref[cur, :], ones], axis=1))
                     + _dot(pp, jnp.concatenate([v_ref[prev, :], ones], axis=1)))
                den = jnp.where(lane_h, r[:BLK, LANES:], r[BLK:, LANES:])
                o_ref[cur, :] = jnp.where(lane_h, r[:BLK, :LANES], r[BLK:, :LANES]) / den
                l_ref[cur, :] = jnp.where(lane_h, m[:BLK], m[BLK:]) + jnp.log(den)

    out = pl.BlockSpec((None, s_len, LANES), lambda n, hp: (n, 0, hp))
    return _call(
        body, grid=(3, 4), name=name, carry=carry, in_specs=_dl_in_specs(s_len), out_specs=[out, out],
        out_shape=[jax.ShapeDtypeStruct((3, s_len, DL_W), F32)] * 2, args=(qkv, qkv, qkv))


def _dl_mix(o_p, l_p, *, name, carry=None):
    s_len = o_p.shape[1]

    def body(o_ref, l_ref, ob_ref, of_ref, lt_ref, o_scr, l_scr):
        n = pl.program_id(1)
        for k, d in enumerate(DILATIONS):
            @pl.when(n == k)
            def _(k=k, d=d):
                _unpermute_rows(o_scr.at[k], o_ref, d)
                _unpermute_rows(l_scr.at[k], l_ref, d)

        @pl.when(n == 2)
        def _():
            l0, l1, l2 = l_scr[0], l_scr[1], l_scr[2]
            m = jnp.maximum(jnp.maximum(l0, l1), l2)
            e0, e1, e2 = jnp.exp(l0 - m), jnp.exp(l1 - m), jnp.exp(l2 - m)
            den = e0 + e1 + e2
            o = (e0 / den) * o_scr[0] + (e1 / den) * o_scr[1] + (e2 / den) * o_scr[2]
            of_ref[...] = o
            ob_ref[...] = o.astype(BF16)
            lt_ref[...] = m + jnp.log(den)

    inb = pl.BlockSpec((None, s_len, LANES), lambda cb, n: (n, 0, cb))
    outb = pl.BlockSpec((s_len, LANES), lambda cb, n: (0, cb))
    return _call(
        body, grid=(4, 3), name=name, carry=carry, in_specs=[inb, inb], out_specs=[outb, outb, outb],
        out_shape=[jax.ShapeDtypeStruct((s_len, DL_W), BF16), jax.ShapeDtypeStruct((s_len, DL_W), F32),
                   jax.ShapeDtypeStruct((s_len, DL_W), F32)],
        scratch_shapes=[pltpu.VMEM((3, s_len, LANES), F32), pltpu.VMEM((3, s_len, LANES), F32)], args=(o_p, l_p))


def _dl_bwd_prep(dcat, o, lse, *, name):
    s_len = o.shape[0]

    def body(do_ref, o_ref, l_ref, dop_ref, st_ref, d_scr):
        n = pl.program_id(1)

        @pl.when(n == 0)
        def _():
            r0 = lax.broadcasted_iota(jnp.int32, (LANES, LANES), 0) // HD
            r1 = lax.broadcasted_iota(jnp.int32, (LANES, LANES), 1) // HD
            d_scr[...] = _dot_hilo(do_ref[...] * o_ref[...], (r0 == r1).astype(BF16))

        for k, d in enumerate(DILATIONS):
            @pl.when(n == k)
            def _(d=d):
                _permute_rows(dop_ref, do_ref, d, BF16)
                _permute_rows(st_ref.at[0], d_scr, d, F32)
                _permute_rows(st_ref.at[1], l_ref, d, F32)

    nat = lambda c0: pl.BlockSpec((s_len, LANES), lambda cb, n: (0, c0 + cb))
    return pl.pallas_call(
        body, grid=(4, 3), name=name, in_specs=[nat(4), nat(0), nat(0)],
        out_specs=[pl.BlockSpec((None, s_len, LANES), lambda cb, n: (n, 0, cb)),
                   pl.BlockSpec((2, None, s_len, LANES), lambda cb, n: (0, n, 0, cb))],
        out_shape=[jax.ShapeDtypeStruct((3, s_len, DL_W), BF16), jax.ShapeDtypeStruct((2, 3, s_len, DL_W), F32)],
        scratch_shapes=[pltpu.VMEM((s_len, LANES), F32)], compiler_params=_cp())(dcat, o, lse)


def _dl_bwd(qkv, dop, stats, *, name, carry=None):
    s_len = qkv.shape[2]

    def body(q_ref, k_ref, v_ref, do_ref, st_ref, cur_ref, prev_ref):
        n = pl.program_id(0)
        lane_h = _head_masks()[2]
        band_c, band_p = _dl_band(2 * BLK)

        def per_head(x):
            xr = pltpu.roll(x, HD, 1)
            return jnp.concatenate([jnp.where(lane_h, x, xr), jnp.where(lane_h, xr, x)], axis=0)

        @pl.loop(0, s_len // BLK, step=DL_UNROLL)
        def _(i0):
            blocks = [i0 + t for t in range(DL_UNROLL)]
            rows = [_dl_rows(i) for i in blocks]
            stage1 = []
            for cur, prev in rows:
                qs = _stack_heads(q_ref[cur, :] * 0.125, lane_h)
                dos = _stack_heads(do_ref[cur, :], lane_h)
                kc, kp, vc, vp = k_ref[cur, :], k_ref[prev, :], v_ref[cur, :], v_ref[prev, :]
                stage1.append((qs, dos, _dot_nt(qs, kc), _dot_nt(qs, kp), _dot_nt(dos, vc), _dot_nt(dos, vp)))
            stage2 = []
            for i, (cur, prev), (qs, dos, sc, sp, dpc, dpp) in zip(blocks, rows, stage1):
                lse, delta = per_head(st_ref[1, cur, :]), per_head(st_ref[0, cur, :])
                pc = jnp.where(band_c, jnp.exp(sc - lse), 0.0)
                pp = jnp.where(jnp.logical_and(band_p, jnp.logical_not(_dl_first(s_len, n, i))), jnp.exp(sp - lse), 0.0)
                stage2.append((pc.astype(BF16), pp.astype(BF16), (pc * (dpc - delta)).astype(BF16),
                               (pp * (dpp - delta)).astype(BF16)))
            for (cur, prev), (qs, dos, *_), (pc, pp, dsc, dsp) in zip(rows, stage1, stage2):
                dq = _dot(dsc, k_ref[cur, :]) + _dot(dsp, k_ref[prev, :])
                cur_ref[0, cur, :] = jnp.where(lane_h, dq[:BLK], dq[BLK:]) * 0.125
                cur_ref[1, cur, :] = _dot_tn(dsc, qs)
                cur_ref[2, cur, :] = _dot_tn(pc, dos)
                prev_ref[0, cur, :] = _dot_tn(dsp, qs)
                prev_ref[1, cur, :] = _dot_tn(pp, dos)

    return _call(
        body, grid=(3, 4), name=name, carry=carry,
        in_specs=_dl_in_specs(s_len) + [pl.BlockSpec((None, s_len, LANES), lambda n, hp: (n, 0, hp)),
                                        pl.BlockSpec((2, None, s_len, LANES), lambda n, hp: (0, n, 0, hp))],
        out_specs=[pl.BlockSpec((3, None, s_len, LANES), lambda n, hp: (0, n, 0, hp)),
                   pl.BlockSpec((2, None, s_len, LANES), lambda n, hp: (0, n, 0, hp))],
        out_shape=[jax.ShapeDtypeStruct((3, 3, s_len, DL_W), F32), jax.ShapeDtypeStruct((2, 3, s_len, DL_W), F32)],
        vmem_mb=56, args=(qkv, qkv, qkv, dop, stats))


def _dl_bwd_finish(cur, prev, cos, sin, *, name):
    s_len = cur.shape[2]

    def body(c_ref, p_ref, cos_ref, sin_ref, o_ref, p_scr, u_scr, acc):
        a, n = pl.program_id(0), pl.program_id(2)
        has_prev = jnp.where(a > 0, 1.0, 0.0)
        p_scr[...] = c_ref[...]
        p_scr[0:s_len - BLK, :] += has_prev * p_ref[BLK:, :]
        for k, d in enumerate(DILATIONS):
            @pl.when(n == k)
            def _(k=k, d=d):
                if k == 0:
                    acc[...] = p_scr[...]
                else:
                    _unpermute_rows(u_scr, p_scr, d)
                    acc[...] += u_scr[...]

        @pl.when(n == 2)
        def _():
            dy = acc[...]
            rot = a < 2
            o_ref[...] = (dy * jnp.where(rot, cos_ref[...], 1.0)
                          + _rot_half(dy * jnp.where(rot, sin_ref[...], 0.0))).astype(BF16)

    tab = pl.BlockSpec((s_len, LANES), lambda a, cb, n: (0, 0))
    return pl.pallas_call(
        body, grid=(3, 4, 3), name=name,
        in_specs=[pl.BlockSpec((None, None, s_len, LANES), lambda a, cb, n: (a, n, 0, cb)),
                  pl.BlockSpec((None, None, s_len, LANES), lambda a, cb, n: (jnp.maximum(a - 1, 0), n, 0, cb)),
                  tab, tab],
        out_specs=pl.BlockSpec((s_len, LANES), lambda a, cb, n: (0, 4 * a + cb)),
        out_shape=jax.ShapeDtypeStruct((s_len, 3 * DL_W), BF16),
        scratch_shapes=[pltpu.VMEM((s_len, LANES), F32)] * 3, compiler_params=_cp())(cur, prev, cos, sin)


XA_T = 1024


def _xa_probs(q, k):
    s = _dot_nt(q, k) * (X_HD ** -0.5)
    e = jnp.exp(s - jnp.max(s, axis=1, keepdims=True))
    return e / jnp.sum(e, axis=1, keepdims=True)


def _xa_fwd(q, k, v, *, name):
    s_len, d = q.shape
    nm = k.shape[0]

    def body(q_ref, k_ref, v_ref, o_ref):
        for h in range(X_HEADS):
            cs = slice(h * X_HD, (h + 1) * X_HD)
            p = _xa_probs(q_ref[:, cs], k_ref[:, cs])
            o_ref[:, cs] = _dot(p.astype(BF16), v_ref[:, cs]).astype(BF16)

    row = pl.BlockSpec((XA_T, d), lambda i: (i, 0))
    full = pl.BlockSpec((nm, d), lambda i: (0, 0))
    return pl.pallas_call(body, grid=(s_len // XA_T,), name=name, in_specs=[row, full, full], out_specs=row,
                          out_shape=jax.ShapeDtypeStruct((s_len, d), BF16), compiler_params=_cp())(q, k, v)


def _xa_bwd(q, k, v, do, *, name, carry=None):
    s_len, d = q.shape
    nm = k.shape[0]

    def body(q_ref, k_ref, v_ref, do_ref, dq_ref, dk_ref, dv_ref):
        i = pl.program_id(0)
        for h in range(X_HEADS):
            cs = slice(h * X_HD, (h + 1) * X_HD)
            qh, kh, vh, doh = q_ref[:, cs], k_ref[:, cs], v_ref[:, cs], do_ref[:, cs]
            p = _xa_probs(qh, kh)
            dp = _dot_nt(doh, vh)
            ds = (p * (dp - jnp.sum(dp * p, axis=1, keepdims=True)) * (X_HD ** -0.5)).astype(BF16)
            dq_ref[:, cs] = _dot(ds, kh).astype(BF16)
            dkh, dvh = _dot_tn(ds, qh), _dot_tn(p.astype(BF16), doh)

            @pl.when(i == 0)
            def _(cs=cs, dkh=dkh, dvh=dvh):
                dk_ref[:, cs] = dkh
                dv_ref[:, cs] = dvh

            @pl.when(i > 0)
            def _(cs=cs, dkh=dkh, dvh=dvh):
                dk_ref[:, cs] += dkh
                dv_ref[:, cs] += dvh

    row = pl.BlockSpec((XA_T, d), lambda i: (i, 0))
    full = pl.BlockSpec((nm, d), lambda i: (0, 0))
    return _call(
        body, grid=(s_len // XA_T,), name=name, carry=carry, in_specs=[row, full, full, row],
        out_specs=[row, full, full],
        out_shape=[jax.ShapeDtypeStruct((s_len, d), BF16), jax.ShapeDtypeStruct((nm, d), F32),
                   jax.ShapeDtypeStruct((nm, d), F32)], args=(q, k, v, do))


FF_TM, FF_TN, FF_H = 512, 256, 8
GELU_K, GELU_C = 0.7978845608028654, 0.044715


FF_STRIP = 64


def _ff_conv(e_ref, w_ref, b_ref, rows, r0=0):
    return (w_ref[0:1, :] * e_ref[pl.ds(FF_H - 2 + r0, rows), :] + w_ref[1:2, :] * e_ref[pl.ds(FF_H - 1 + r0, rows), :]
            + w_ref[2:3, :] * e_ref[pl.ds(FF_H + r0, rows), :] + b_ref[...])


def _strips(total, size):
    return [(r0, min(size, total - r0)) for r0 in range(0, total, size)]


def _ff_gate_fwd(up, conv_w, conv_b, *, name, carry=None):
    s_len = up.shape[0]
    nj = D_FF // FF_TN

    def body(g_ref, v_ref, gp_ref, vp_ref, wg_ref, wv_ref, bg_ref, bv_ref, o_ref, eg, ev):
        i = pl.program_id(0)
        for e, cur, prev in ((eg, g_ref, gp_ref), (ev, v_ref, vp_ref)):
            e[0:FF_H, :] = jnp.where(i > 0, prev[...], 0.0)
            e[FF_H:, :] = cur[...]
        for r0, rows in _strips(FF_TM, FF_STRIP):
            gate = _ff_conv(eg, wg_ref, bg_ref, rows, r0)
            val = _ff_conv(ev, wv_ref, bv_ref, rows, r0)
            t = jnp.tanh(GELU_K * (gate + GELU_C * gate * gate * gate))
            o_ref[r0:r0 + rows, :] = (0.5 * gate * (1.0 + t) * val).astype(BF16)

    cur = lambda c0: pl.BlockSpec((FF_TM, FF_TN), lambda i, j: (i, c0 + j))
    prev = lambda c0: pl.BlockSpec((FF_H, FF_TN), lambda i, j: (jnp.maximum(i * (FF_TM // FF_H) - 1, 0), c0 + j))
    par = lambda r, c0: pl.BlockSpec((r, FF_TN), lambda i, j: (0, c0 + j))
    return _call(
        body, grid=(s_len // FF_TM, nj), name=name, carry=carry,
        in_specs=[cur(0), cur(nj), prev(0), prev(nj), par(3, 0), par(3, nj), par(1, 0), par(1, nj)],
        out_specs=[cur(0)], out_shape=[jax.ShapeDtypeStruct((s_len, D_FF), BF16)],
        scratch_shapes=[pltpu.VMEM((FF_TM + FF_H, FF_TN), F32)] * 2,
        args=(up, up, up, up, conv_w, conv_w, conv_b, conv_b))


def _ff_gate_bwd(up, dact, conv_w, conv_b, *, name, carry=None):
    s_len = up.shape[0]
    nj = D_FF // FF_TN
    last = s_len // FF_TM - 1
    ext = FF_TM + FF_H

    def body(g_ref, v_ref, gp_ref, vp_ref, gn_ref, vn_ref, da_ref, dan_ref, wg_ref, wv_ref, bg_ref, bv_ref,
             dg_ref, dv_ref, dw_ref, db_ref, eg, ev, sg, sv):
        i = pl.program_id(1)
        for e, cur, prev, nxt in ((eg, g_ref, gp_ref, gn_ref), (ev, v_ref, vp_ref, vn_ref)):
            e[0:FF_H, :] = jnp.where(i > 0, prev[...], 0.0)
            e[FF_H:FF_H + FF_TM, :] = cur[...]
            e[FF_H + FF_TM:, :] = nxt[...]
        for r0, rows in _strips(ext, FF_STRIP):
            gate = _ff_conv(eg, wg_ref, bg_ref, rows, r0)
            val = _ff_conv(ev, wv_ref, bv_ref, rows, r0)
            dact = da_ref[r0:r0 + rows, :] if r0 < FF_TM else jnp.where(i < last, dan_ref[...], 0.0)
            t = jnp.tanh(GELU_K * (gate + GELU_C * gate * gate * gate))
            half = 0.5 * (1.0 + t)
            dgelu = half + 0.5 * gate * (1.0 - t * t) * GELU_K * (1.0 + 3.0 * GELU_C * gate * gate)
            sg[r0:r0 + rows, :] = dact * val * dgelu
            sv[r0:r0 + rows, :] = dact * (gate * half)
        for part, (s, e, w_ref, out) in enumerate(((sg, eg, wg_ref, dg_ref), (sv, ev, wv_ref, dv_ref))):
            taps, bias = [jnp.zeros((8, FF_TN), F32)] * 3, jnp.zeros((8, FF_TN), F32)
            for r0, rows in _strips(FF_TM, FF_STRIP):
                d0 = s[pl.ds(r0, rows), :]
                out[r0:r0 + rows, :] = (w_ref[2:3, :] * d0 + w_ref[1:2, :] * s[pl.ds(r0 + 1, rows), :]
                                        + w_ref[0:1, :] * s[pl.ds(r0 + 2, rows), :]).astype(BF16)
                taps = [taps[k] + _rowsum8(d0 * e[pl.ds(FF_H - 2 + k + r0, rows), :]) for k in range(3)]
                bias = bias + _rowsum8(d0)
            _acc_out(dw_ref.at[part], i, jnp.concatenate(taps, axis=0))
            _acc_out(db_ref.at[part], i, bias)

    cur = lambda c0: pl.BlockSpec((FF_TM, FF_TN), lambda j, i: (i, c0 + j))
    prev = lambda c0: pl.BlockSpec((FF_H, FF_TN), lambda j, i: (jnp.maximum(i * (FF_TM // FF_H) - 1, 0), c0 + j))
    nxt = lambda c0: pl.BlockSpec(
        (FF_H, FF_TN), lambda j, i: (jnp.minimum((i + 1) * (FF_TM // FF_H), s_len // FF_H - 1), c0 + j))
    par = lambda r, c0: pl.BlockSpec((r, FF_TN), lambda j, i: (0, c0 + j))
    return _call(
        body, grid=(nj, s_len // FF_TM), name=name, carry=carry,
        in_specs=[cur(0), cur(nj), prev(0), prev(nj), nxt(0), nxt(nj), cur(0), nxt(0),
                  par(3, 0), par(3, nj), par(1, 0), par(1, nj)],
        out_specs=[cur(0), cur(0), pl.BlockSpec((2, 24, FF_TN), lambda j, i: (0, 0, j)),
                   pl.BlockSpec((2, 8, FF_TN), lambda j, i: (0, 0, j))],
        out_shape=[jax.ShapeDtypeStruct((s_len, D_FF), BF16), jax.ShapeDtypeStruct((s_len, D_FF), BF16),
                   jax.ShapeDtypeStruct((2, 24, D_FF), F32), jax.ShapeDtypeStruct((2, 8, D_FF), F32)],
        scratch_shapes=[pltpu.VMEM((FF_TM + 2 * FF_H, FF_TN), F32)] * 2 + [pltpu.VMEM((ext, FF_TN), F32)] * 2,
        args=(up, up, up, up, up, up, dact, dact, conv_w, conv_w, conv_b, conv_b))


def _place():
    x, y, c = lax.axis_index("x"), lax.axis_index("y"), lax.axis_index("c")
    return x, y, c, [(1 - x, y), (x, 1 - y), (1 - x, 1 - y)]


def _remote(src, dst, send_sem, recv_sem, dev):
    return pltpu.make_async_remote_copy(src_ref=src, dst_ref=dst, send_sem=send_sem, recv_sem=recv_sem,
                                        device_id=dev, device_id_type=MESH)


_ANY = pl.BlockSpec(memory_space=pl.ANY)


N_SEMS = 8
SEM_BASE_2 = 4


class _Exchange:
    def __init__(self, operands, out_shapes, start, wait, aliases=None):
        self.operands, self.out_shapes, self.start, self.wait = list(operands), list(out_shapes), start, wait
        self.aliases = aliases or {}


def _sem_scratch():
    return [pltpu.SemaphoreType.DMA((N_SEMS,)), pltpu.SemaphoreType.DMA((N_SEMS,)), pltpu.SemaphoreType.DMA]


def _run_exchange(ex, *, name):
    k, n = len(ex.operands), len(ex.out_shapes)

    def body(*refs):
        ins, outs, sems = refs[:k], refs[k:k + n], refs[k + n:]
        ex.start(ins, outs, *sems)
        ex.wait(ins, outs, *sems)

    return pl.pallas_call(body, name=name, in_specs=[_ANY] * k, out_specs=[_ANY] * n, out_shape=ex.out_shapes,
                          scratch_shapes=_sem_scratch(), input_output_aliases=ex.aliases,
                          compiler_params=_cp(16))(*ex.operands)


def _call(body, *, grid, in_specs, out_specs, out_shape, args, name, scratch_shapes=(), vmem_mb=48, carry=None):
    scratch_shapes = list(scratch_shapes)
    if carry is None:
        return pl.pallas_call(body, grid=grid, name=name, in_specs=in_specs, out_specs=out_specs, out_shape=out_shape,
                              scratch_shapes=scratch_shapes, compiler_params=_cp(vmem_mb))(*args)
    n_in, n_out, n_scr = len(in_specs), len(out_shape), len(scratch_shapes)
    k_in, k_out = len(carry.operands), len(carry.out_shapes)

    def wrapped(*refs):
        ins, refs = refs[:n_in], refs[n_in:]
        cin, refs = refs[:k_in], refs[k_in:]
        outs, refs = refs[:n_out], refs[n_out:]
        cout, refs = refs[:k_out], refs[k_out:]
        scratch, sems = refs[:n_scr], refs[n_scr:]
        ids = [pl.program_id(a) for a in range(len(grid))]
        first = functools.reduce(jnp.logical_and, [i == 0 for i in ids])
        last = functools.reduce(jnp.logical_and, [i == g - 1 for i, g in zip(ids, grid)])

        @pl.when(first)
        def _():
            carry.start(cin, cout, *sems)

        body(*ins, *outs, *scratch)

        @pl.when(last)
        def _():
            carry.wait(cin, cout, *sems)

    aliases = {n_in + i: n_out + o for i, o in carry.aliases.items()}
    return pl.pallas_call(
        wrapped, grid=grid, name=name, in_specs=list(in_specs) + [_ANY] * k_in,
        out_specs=list(out_specs) + [_ANY] * k_out, out_shape=list(out_shape) + carry.out_shapes,
        scratch_shapes=scratch_shapes + _sem_scratch(), input_output_aliases=aliases,
        compiler_params=_cp(vmem_mb))(*args, *carry.operands)


def _half_rows(ref_rows, c):
    half = ref_rows // 2
    return pl.ds(c * half, half)


def _ex_join(a, b):
    ka, na = len(a.operands), len(a.out_shapes)

    def start(ins, outs, *sems):
        a.start(ins[:ka], outs[:na], *sems)
        b.start(ins[ka:], outs[na:], *sems)

    def wait(ins, outs, *sems):
        a.wait(ins[:ka], outs[:na], *sems)
        b.wait(ins[ka:], outs[na:], *sems)

    aliases = dict(a.aliases)
    aliases.update({ka + i: na + o for i, o in b.aliases.items()})
    return _Exchange(a.operands + b.operands, a.out_shapes + b.out_shapes, start, wait, aliases)


def _ex_gather(pack, r0, rl, base=0):
    def copies(ins, outs, send, recv):
        x, y, c, chips = _place()
        rows = _half_rows(rl, c)
        src = ins[0].at[pl.ds(r0 + c * (rl // 2), rl // 2)]
        sends = [_remote(src, outs[0].at[2 * x + y, rows], send.at[base + k], recv.at[base + k], (px, py, c))
                 for k, (px, py) in enumerate(chips)]
        lands = [_remote(src, outs[0].at[2 * px + py, rows], send.at[base + k], recv.at[base + k], (px, py, c))
                 for k, (px, py) in enumerate(chips)]
        return sends, lands

    def mine(ins, outs, local):
        x, y, _, _ = _place()
        return pltpu.make_async_copy(ins[0].at[pl.ds(r0, rl)], outs[0].at[2 * x + y], local)

    def start(ins, outs, send, recv, local):
        mine(ins, outs, local).start()
        for cp in copies(ins, outs, send, recv)[0]:
            cp.start()

    def wait(ins, outs, send, recv, local):
        sends, lands = copies(ins, outs, send, recv)
        for cp in lands:
            cp.wait_recv()
        for cp in sends:
            cp.wait_send()
        mine(ins, outs, local).wait()

    return _Exchange([pack], [jax.ShapeDtypeStruct((4, rl, pack.shape[1]), pack.dtype)], start, wait)


def _ex_gather_forward(g, base=0):
    rl = g.shape[1]

    def copies(outs, send, recv):
        x, y, c, chips = _place()
        slabs = [(outs[0].at[2 * px + py, _half_rows(rl, c)], outs[0].at[2 * px + py, _half_rows(rl, 1 - c)])
                 for px, py in chips]
        sends = [_remote(a, a, send.at[base + k], recv.at[base + k], (x, y, 1 - c)) for k, (a, _) in enumerate(slabs)]
        lands = [_remote(b, b, send.at[base + k], recv.at[base + k], (x, y, 1 - c)) for k, (_, b) in enumerate(slabs)]
        return sends, lands

    def start(ins, outs, send, recv, local):
        for cp in copies(outs, send, recv)[0]:
            cp.start()

    def wait(ins, outs, send, recv, local):
        sends, lands = copies(outs, send, recv)
        for cp in lands:
            cp.wait_recv()
        for cp in sends:
            cp.wait_send()

    return _Exchange([g], [jax.ShapeDtypeStruct(g.shape, g.dtype)], start, wait, aliases={0: 0})


def _ex_swap_halves(gw, base=0):
    nb, rl, d = gw.shape

    def copies(ins, outs, send, recv):
        x, y, c, _ = _place()
        return [_remote(ins[0].at[j, _half_rows(rl, 1 - c)], outs[0].at[j], send.at[base + j], recv.at[base + j],
                        (x, y, 1 - c)) for j in range(nb)]

    def start(ins, outs, send, recv, local):
        for cp in copies(ins, outs, send, recv):
            cp.start()

    def wait(ins, outs, send, recv, local):
        for cp in copies(ins, outs, send, recv):
            cp.wait()

    return _Exchange([gw], [jax.ShapeDtypeStruct((nb, rl // 2, d), gw.dtype)], start, wait)


def _chip_sum(gw, got, c_arr, *, name):
    nchip, half, d = got.shape
    tr = _tile(half, 512)

    def body(c_ref, a_ref, b_ref, o32_ref, o16_ref):
        s = a_ref[...] + b_ref[...]
        o32_ref[...] = s
        o16_ref[...] = s.astype(BF16)

    blk = pl.BlockSpec((None, tr, d), lambda j, i, c_ref: (j, i, 0))
    return pl.pallas_call(
        body, name=name,
        grid_spec=pltpu.PrefetchScalarGridSpec(
            num_scalar_prefetch=1, grid=(nchip, half // tr),
            in_specs=[pl.BlockSpec((None, tr, d), lambda j, i, c_ref: (j, c_ref[0] * (half // tr) + i, 0)), blk],
            out_specs=[blk, blk]),
        out_shape=[jax.ShapeDtypeStruct((nchip, half, d), F32), jax.ShapeDtypeStruct((nchip, half, d), BF16)],
        compiler_params=_cp())(c_arr, gw, got)


def _ex_scatter(s16, base=0):
    def copies(ins, outs, send, recv):
        x, y, c, chips = _place()
        return [_remote(ins[0].at[2 * px + py], outs[0].at[k], send.at[base + k], recv.at[base + k], (px, py, c))
                for k, (px, py) in enumerate(chips)]

    def start(ins, outs, send, recv, local):
        for cp in copies(ins, outs, send, recv):
            cp.start()

    def wait(ins, outs, send, recv, local):
        for cp in copies(ins, outs, send, recv):
            cp.wait()

    return _Exchange([s16], [jax.ShapeDtypeStruct((3,) + s16.shape[1:], s16.dtype)], start, wait)


def _mesh_sum(s32, got, j_arr, *, name):
    _, rl, d = s32.shape
    tr = _tile(rl, 512)

    def body(j_ref, a_ref, b_ref, o_ref):
        o_ref[...] = ((a_ref[...] + b_ref[0].astype(F32)) + b_ref[1].astype(F32)) + b_ref[2].astype(F32)

    return pl.pallas_call(
        body, name=name,
        grid_spec=pltpu.PrefetchScalarGridSpec(
            num_scalar_prefetch=1, grid=(rl // tr,),
            in_specs=[pl.BlockSpec((None, tr, d), lambda i, j_ref: (j_ref[0], i, 0)),
                      pl.BlockSpec((3, tr, d), lambda i, j_ref: (0, i, 0))],
            out_specs=pl.BlockSpec((tr, d), lambda i, j_ref: (i, 0))),
        out_shape=jax.ShapeDtypeStruct((rl, d), F32), compiler_params=_cp())(j_arr, s32, got)


def _ex_share_halves(ghalf):
    half, d = ghalf.shape

    def copies(ins, outs, send, recv, local):
        x, y, c, _ = _place()
        there = outs[0].at[_half_rows(2 * half, c)]
        back = outs[0].at[_half_rows(2 * half, 1 - c)]
        return (_remote(ins[0], there, send.at[0], recv.at[0], (x, y, 1 - c)),
                _remote(ins[0], back, send.at[0], recv.at[0], (x, y, 1 - c)), pltpu.make_async_copy(ins[0], there, local))

    def start(ins, outs, send, recv, local):
        out, _, mine = copies(ins, outs, send, recv, local)
        mine.start()
        out.start()

    def wait(ins, outs, send, recv, local):
        out, back, mine = copies(ins, outs, send, recv, local)
        back.wait_recv()
        out.wait_send()
        mine.wait()

    return _Exchange([ghalf], [jax.ShapeDtypeStruct((2 * half, d), ghalf.dtype)], start, wait)


class _ReduceScatter:
    def __init__(self, gw, c_arr, j_arr, tag):
        self.gw, self.c_arr, self.j_arr, self.tag = gw, c_arr, j_arr, tag

    def swap(self, base=0):
        return _ex_swap_halves(self.gw, base)

    def after_swap(self, got, base=0):
        self.s32, s16 = _chip_sum(self.gw, got, self.c_arr, name=f"rs_chip_sum{self.tag}")
        return _ex_scatter(s16, base)

    def after_scatter(self, got16):
        ghalf = _mesh_sum(self.s32, got16, self.j_arr, name=f"rs_mesh_sum{self.tag}")
        return _run_exchange(_ex_share_halves(ghalf), name=f"rs_share{self.tag}")[0]

    def run(self):
        got, = _run_exchange(self.swap(), name=f"rs_swap{self.tag}")
        got16, = _run_exchange(self.after_swap(got), name=f"rs_scatter{self.tag}")
        return self.after_scatter(got16)


def _all_reduce_small(vec, *, name):
    rows, d = vec.shape

    def body(x_ref, o_ref, gat, send_sems, recv_sems, local_sem):
        x, y, c, chips = _place()
        me, sibling = (x, y, c), (x, y, 1 - c)

        def slot(px, py, pc):
            return gat.at[4 * px + 2 * py + pc]

        def copy(k, block, to, src=None):
            return _remote(slot(*block) if src is None else src, slot(*block), send_sems.at[k], recv_sems.at[k], to)

        mine = pltpu.make_async_copy(x_ref, slot(*me), local_sem)
        mine.start()
        first = [copy(0, me, sibling, src=x_ref)]
        first += [copy(1 + j, me, (*chip, c), src=x_ref) for j, chip in enumerate(chips)]
        for cp in first:
            cp.start()
        passed = [copy(4 + j, (*chip, c), sibling) for j, chip in enumerate(chips)]
        for j, chip in enumerate(chips):
            copy(1 + j, (*chip, c), me).wait_recv()
            passed[j].start()
        copy(0, sibling, me).wait_recv()
        for j, chip in enumerate(chips):
            copy(4 + j, (*chip, 1 - c), me).wait_recv()
        for cp in first + passed:
            cp.wait_send()
        mine.wait()
        acc = gat[0]
        for dev in range(1, 8):
            acc = acc + gat[dev]
        o_ref[...] = acc

    vm = pl.BlockSpec(memory_space=pltpu.VMEM)
    return pl.pallas_call(
        body, name=name, in_specs=[vm], out_specs=vm, out_shape=jax.ShapeDtypeStruct((rows, d), F32),
        scratch_shapes=[pltpu.VMEM((8, rows, d), F32), pltpu.SemaphoreType.DMA((7,)), pltpu.SemaphoreType.DMA((7,)),
                        pltpu.SemaphoreType.DMA],
        compiler_params=_cp(32))(vec)


COL_SHARDED = ("w_in", "ffn_w_up")


def _to_pack_rows(name, shard):
    return shard.reshape(-1, D_MODEL)


def _full_from_blocks(name, blocks):
    rows = blocks.shape[1]
    if name in COL_SHARDED:
        return blocks.reshape(4, D_MODEL, rows).transpose(1, 0, 2).reshape(D_MODEL, 4 * rows)
    return blocks.reshape(4 * rows, D_MODEL)


def _blocks_from_full(name, full):
    if name in COL_SHARDED:
        cols = full.shape[1] // 4
        return full.reshape(D_MODEL, 4, cols).transpose(1, 0, 2).reshape(4, cols, D_MODEL)
    return full.reshape(4, full.shape[0] // 4, D_MODEL)


def _row(v):
    return v.reshape(1, -1)


SMALL = (("mix_norm_pre", (1024,), None), ("cv_w", (31, 256), 1), ("cv_b", (256,), None), ("cv_ln_g", (256,), None),
         ("cv_ln_b", (256,), None), ("cv_pw_w", (256, 256), 0), ("cv_pw_b", (256,), None),
         ("mix_norm_post", (1024,), None), ("x_norm_pre", (1024,), None), ("mem_norm", (1024,), None),
         ("x_norm_post", (1024,), None), ("ffn_norm_pre", (1024,), None), ("ffn_conv_w", (3, 5632), 1),
         ("ffn_conv_b", (5632,), None), ("ffn_norm_post", (1024,), None))
BIG = tuple(n for n, _ in PACK_ROWS)
WEIGHT_ORDER = ("mix_norm_pre", "w_in", "cv_w", "cv_b", "cv_ln_g", "cv_ln_b", "cv_pw_w", "cv_pw_b", "w_out",
                "mix_norm_post", "x_norm_pre", "mem_norm", "x_wq", "x_wk", "x_wv", "x_wo", "x_norm_post",
                "ffn_norm_pre", "ffn_w_up", "ffn_conv_w", "ffn_conv_b", "ffn_w_down", "ffn_norm_post")


def _flat_rows(parts):
    v = jnp.concatenate([p.reshape(-1) for p in parts])
    rows = -(-v.shape[0] // (8 * D_MODEL)) * 8
    return jnp.pad(v, (0, rows * D_MODEL - v.shape[0])).reshape(rows, D_MODEL)


def _small_to_rows(blocks):
    v = jnp.concatenate([b.reshape(-1) for b in blocks])
    return jnp.pad(v, (0, SMALL_ROWS * D_MODEL - v.shape[0])).reshape(SMALL_ROWS, D_MODEL)


def _small_from_rows(rows):
    flat, out, off = rows.reshape(-1), [], 0
    for _, shape, _ in SHARDED_SMALL:
        size = int(np.prod(shape))
        out.append(flat[off:off + size].reshape(shape))
        off += size
    return out


def _chip_block(full, j, shape, axis):
    return lax.slice_in_dim(full, j * shape[axis], (j + 1) * shape[axis], axis=axis)


REST_GROUP = ("w_in", "w_out")
XA_GROUP = ("x_wq", "x_wk", "x_wv", "x_wo")
FFN_GROUP = ("ffn_w_up", "ffn_w_down")


class _Weights:
    FIRST = (0, 768)
    OWN = ((768, 1024), (1792, 1664), (3456, 704))
    NEXT = ((0, 1024), (1024, 1024), (2048, 1408), (3456, 704))
    SLOTS = ("mix_in", "sb_fwd", "dl_fwd", "dl_mix", "ffn_up", "ffn_gate", "ffn_down")

    def __init__(self, packs):
        self.packs, self.pieces, self.landed, self.plan = packs, {}, None, {}
        for slot, piece in zip(self.SLOTS[:3], self.OWN):
            self.plan[(0, slot)] = (0,) + piece
        for l in range(len(packs) - 1):
            for slot, piece in zip(self.SLOTS[3:], self.NEXT):
                self.plan[(l, slot)] = (l + 1,) + piece
        first = _run_exchange(_ex_gather(packs[0], *self.FIRST), name="gather_first")[0]
        self.pieces[(0,) + self.FIRST] = _run_exchange(_ex_gather_forward(first), name="gather_first_forward")[0]

    def ride(self, layer, slot, call):
        start, todo, ex = self.plan.get((layer, slot)), [], None
        if start is not None:
            ex = _ex_gather(self.packs[start[0]], start[1], start[2])
            todo.append(("landed", start))
        if self.landed is not None:
            key, buf = self.landed
            forward = _ex_gather_forward(buf, SEM_BASE_2 if ex is not None else 0)
            ex = forward if ex is None else _ex_join(ex, forward)
            todo.append(("piece", key))
            self.landed = None
        outs = list(call(carry=ex))
        n = len(outs) - len(todo)
        for (kind, key), buf in zip(todo, outs[n:]):
            if kind == "landed":
                self.landed = (key, buf)
            else:
                self.pieces[key] = buf
        return outs[:n]

    def rows_of(self, layer, name):
        off = 0
        for n, rows in WEIGHT_PACK:
            if n == name:
                break
            off += rows
        for (l, r0, nrows), buf in self.pieces.items():
            if l == layer and r0 <= off < r0 + nrows:
                return buf[:, off - r0:off - r0 + rows, :]
        raise KeyError(f"{name} of layer {layer} is not gathered yet")

    def weight(self, layer, name):
        return _full_from_blocks(name, self.rows_of(layer, name))

    def small(self, layer):
        planes = lax.bitcast_convert_type(self.rows_of(layer, "small").astype(jnp.bfloat16), jnp.uint16)
        planes = planes.astype(jnp.uint32)
        bits = (planes[:, :SMALL_ROWS] << 16) | planes[:, SMALL_ROWS:]
        per_chip = [_small_from_rows(r) for r in lax.bitcast_convert_type(bits, F32)]
        return {n: jnp.concatenate([blocks[k] for blocks in per_chip], axis=axis)
                for k, (n, _, axis) in enumerate(SHARDED_SMALL)}


class _Params:
    def __init__(self, weights, layer, small):
        self.weights, self.layer, self.small, self.cache = weights, layer, small, {}

    def __getitem__(self, name):
        if name in self.small:
            return self.small[name]
        if name not in self.cache:
            if name in [n for n, _, _ in SHARDED_SMALL]:
                self.cache.update(self.weights.small(self.layer))
            else:
                self.cache[name] = self.weights.weight(self.layer, name)
        return self.cache[name]


def _layer_fwd(h0, mem, p, cos, sin, tag, ride):
    sv = {"h0": h0}
    n1, u = ride("mix_in", functools.partial(_rms_mm, h0, _row(p["mix_norm_pre"]), p["w_in"], tm=1024, tn=1408,
                                             out_dtype=F32, name=f"mix_in{tag}"))
    a_out, = ride("sb_fwd", functools.partial(_sb_fwd, u, name=f"sb_fwd{tag}"))
    b_out, c = _cv_fwd(u, p["cv_w"], _row(p["cv_b"]), _row(p["cv_ln_g"]), _row(p["cv_ln_b"]),
                       p["cv_pw_w"].astype(BF16), _row(p["cv_pw_b"]), name=f"cv_fwd{tag}")
    qkv = _rope_perm(u, cos, sin, name=f"rope_perm{tag}")
    o_p, l_p = ride("dl_fwd", functools.partial(_dl_fwd, qkv, name=f"dl_fwd{tag}"))
    c_out, o_dl, lse = ride("dl_mix", functools.partial(_dl_mix, o_p, l_p, name=f"dl_mix{tag}"))
    cat = jnp.concatenate([a_out, b_out, c_out], axis=1)
    y1, h1 = _mm_post(cat, p["w_out"], h0, _row(p["mix_norm_post"]), tm=512, name=f"mix_out{tag}")
    sv.update(n1=n1, u=u, c=c, qkv=qkv, o_dl=o_dl, lse=lse, cat=cat, y1=y1, h1=h1)

    n2, q = _rms_mm(h1, _row(p["x_norm_pre"]), p["x_wq"], tm=512, tn=1024, out_dtype=BF16, name=f"xa_q{tag}")
    wkv = jnp.concatenate([p["x_wk"], p["x_wv"]], axis=1)
    mem_n, kv = _rms_mm(mem, _row(p["mem_norm"]), wkv, tm=mem.shape[0], tn=1024, out_dtype=BF16, name=f"xa_kv{tag}")
    k, v = kv[:, :D_MODEL], kv[:, D_MODEL:]
    o_x = _xa_fwd(q, k, v, name=f"xa_fwd{tag}")
    y2, h2 = _mm_post(o_x, p["x_wo"], h1, _row(p["x_norm_post"]), tm=512, name=f"xa_out{tag}")
    sv.update(n2=n2, q=q, mem_n=mem_n, k=k, v=v, o_x=o_x, y2=y2, h2=h2, wkv=wkv)

    n3, up = ride("ffn_up", functools.partial(_rms_mm, h2, _row(p["ffn_norm_pre"]), p["ffn_w_up"], tm=1024, tn=1408,
                                              out_dtype=F32, name=f"ffn_up{tag}"))
    act, = ride("ffn_gate", functools.partial(_ff_gate_fwd, up, p["ffn_conv_w"], _row(p["ffn_conv_b"]),
                                              name=f"ffn_gate{tag}"))
    y3, h3 = ride("ffn_down", functools.partial(_mm_post, act, p["ffn_w_down"], h2, _row(p["ffn_norm_post"]), tm=512,
                                                name=f"ffn_down{tag}"))
    sv.update(n3=n3, up=up, act=act, y3=y3)
    return h3, sv


def _layer_bwd(dh3, mem, p, sv, cos, sin, tag, riding, new_rs):
    g = {}
    s8 = lambda part: part.sum(axis=0)
    rode = None

    dy3, dgp = _rms_bwd(sv["y3"], _row(p["ffn_norm_post"]), dh3, None, out_dtype=BF16, tm=1024, name=f"ffn_post_b{tag}")
    g["ffn_norm_post"] = s8(dgp)
    dact = _mm_nt(dy3, p["ffn_w_down"], tm=512, tn=1408, out_dtype=F32, name=f"ffn_down_bx{tag}")
    g["ffn_w_down"] = _mm_tn(sv["act"], dy3, tk=1408, tn=1024, tm=2048, name=f"ffn_down_bw{tag}")
    dgu, dvu, dcw, dcb, *got = _ff_gate_bwd(sv["up"], dact, p["ffn_conv_w"], _row(p["ffn_conv_b"]),
                                            name=f"ffn_gate_b{tag}", carry=riding.swap() if riding else None)
    scatter = riding.after_swap(got[0]) if riding else None
    g["ffn_conv_w"] = jnp.concatenate([dcw[0], dcw[1]], axis=1).reshape(3, 8, 2 * D_FF).sum(axis=1)
    g["ffn_conv_b"] = jnp.concatenate([dcb[0], dcb[1]], axis=1).sum(axis=0)
    dup = jnp.concatenate([dgu, dvu], axis=1)
    dn3 = _mm_nt(dup, p["ffn_w_up"], tm=256, tn=512, out_dtype=F32, name=f"ffn_up_bx{tag}")
    g["ffn_w_up"] = _mm_tn(sv["n3"], dup, tk=512, tn=1408, tm=2048, name=f"ffn_up_bw{tag}")
    ffn_rs = new_rs(FFN_GROUP, g, f"{tag}_ffn")
    dh2, dgp = _rms_bwd(sv["h2"], _row(p["ffn_norm_pre"]), dn3, dh3, out_dtype=F32, tm=1024, name=f"ffn_pre_b{tag}")
    g["ffn_norm_pre"] = s8(dgp)

    dy2, dgp = _rms_bwd(sv["y2"], _row(p["x_norm_post"]), dh2, None, out_dtype=BF16, tm=1024, name=f"xa_post_b{tag}")
    g["x_norm_post"] = s8(dgp)
    do_x = _mm_nt(dy2, p["x_wo"], tm=512, tn=1024, out_dtype=BF16, name=f"xa_out_bx{tag}")
    g["x_wo"] = _mm_tn(sv["o_x"], dy2, tk=512, tn=1024, tm=2048, name=f"xa_out_bw{tag}")
    dq, dk, dv, got = _xa_bwd(sv["q"], sv["k"], sv["v"], do_x, name=f"xa_bwd{tag}", carry=ffn_rs.swap())
    ffn_scatter = ffn_rs.after_swap(got)
    dn2 = _mm_nt(dq, p["x_wq"], tm=512, tn=1024, out_dtype=F32, name=f"xa_q_bx{tag}")
    g["x_wq"] = _mm_tn(sv["n2"], dq, tk=512, tn=1024, tm=2048, name=f"xa_q_bw{tag}")
    dkv = jnp.concatenate([dk, dv], axis=1).astype(BF16)
    nm = mem.shape[0]
    dmem_n = _mm_nt(dkv, sv["wkv"], tm=nm, tn=1024, out_dtype=F32, name=f"xa_kv_bx{tag}")
    dwkv = _mm_tn(sv["mem_n"], dkv, tk=512, tn=2048, tm=nm, name=f"xa_kv_bw{tag}")
    g["x_wk"], g["x_wv"] = dwkv[:, :D_MODEL], dwkv[:, D_MODEL:]
    _, dgp = _rms_bwd(mem, _row(p["mem_norm"]), dmem_n, None, out_dtype=BF16, tm=nm, name=f"xa_mem_b{tag}")
    g["mem_norm"] = s8(dgp)
    xa_rs = new_rs(XA_GROUP, g, f"{tag}_xa")
    dh1, dgp, got = _rms_bwd(sv["h1"], _row(p["x_norm_pre"]), dn2, dh2, out_dtype=F32, tm=1024, name=f"xa_pre_b{tag}",
                             carry=xa_rs.swap())
    xa_scatter = xa_rs.after_swap(got, SEM_BASE_2 if riding else 0)
    g["x_norm_pre"] = s8(dgp)

    dy1, dgp = _rms_bwd(sv["y1"], _row(p["mix_norm_post"]), dh1, None, out_dtype=BF16, tm=1024, name=f"mix_post_b{tag}")
    g["mix_norm_post"] = s8(dgp)
    dcat = _mm_nt(dy1, p["w_out"], tm=512, tn=1024, out_dtype=F32, name=f"mix_out_bx{tag}")
    g["w_out"] = _mm_tn(sv["cat"], dy1, tk=512, tn=1024, tm=2048, name=f"mix_out_bw{tag}")
    u = sv["u"]
    dq_sb, dk_sb, dv_sb, *got = _sb_bwd(u, dcat, name=f"sb_bwd{tag}",
                                        carry=_ex_join(scatter, xa_scatter) if riding else xa_scatter)
    if riding:
        rode = riding.after_scatter(got[0])
    xa_rows = xa_rs.after_scatter(got[-1])
    pw_b16 = p["cv_pw_w"].astype(BF16)
    dc, dpw, vec = _cv_bwd_local(sv["c"], dcat, _row(p["cv_ln_g"]), _row(p["cv_ln_b"]), pw_b16, name=f"cv_bwd_a{tag}")
    g["cv_pw_w"] = dpw
    vec = vec.reshape(3, 8, CV_W).sum(axis=1)
    g["cv_pw_b"], g["cv_ln_g"], g["cv_ln_b"] = vec[0], vec[1], vec[2]
    du_cv, dcw, dcb = _cv_bwd_conv(u, dc, p["cv_w"], name=f"cv_bwd_b{tag}")
    g["cv_w"] = dcw.reshape(CV_K, 8, CV_W).sum(axis=1)
    g["cv_b"] = dcb.sum(axis=0)
    dop, stats = _dl_bwd_prep(dcat, sv["o_dl"], sv["lse"], name=f"dl_prep_b{tag}")
    cur, prev, got = _dl_bwd(sv["qkv"], dop, stats, name=f"dl_bwd{tag}", carry=ffn_scatter)
    ffn_rows = ffn_rs.after_scatter(got)
    du_dl = _dl_bwd_finish(cur, prev, cos, sin, name=f"dl_fin_b{tag}")
    du = jnp.concatenate([dq_sb.astype(BF16), dk_sb.astype(BF16), dv_sb.astype(BF16), du_cv, du_dl], axis=1)
    dn1 = _mm_nt(du, p["w_in"], tm=512, tn=512, out_dtype=F32, name=f"mix_in_bx{tag}")
    g["w_in"] = _mm_tn(sv["n1"], du, tk=512, tn=1408, tm=2048, name=f"mix_in_bw{tag}")
    dh0, dgp = _rms_bwd(sv["h0"], _row(p["mix_norm_pre"]), dn1, dh1, out_dtype=F32, tm=1024, name=f"mix_pre_b{tag}")
    g["mix_norm_pre"] = s8(dgp)
    return dh0, g, (xa_rows, ffn_rows), rode


def _step(x, mem, positions, loss_target, w, m, v):
    depth = w["w_in"].shape[0]
    xi, yi, ci = lax.axis_index("x"), lax.axis_index("y"), lax.axis_index("c")
    chip = 2 * xi + yi
    h = x[0]
    mem0 = mem[0]
    s_len = h.shape[0]

    def pack_rows(n, l):
        if n == "small":
            bits = lax.bitcast_convert_type(_small_to_rows([w[name][l] for name, _, _ in SHARDED_SMALL]), jnp.uint32)
            planes = [(bits >> 16).astype(jnp.uint16), (bits & 0xFFFF).astype(jnp.uint16)]
            return jnp.concatenate([lax.bitcast_convert_type(p, jnp.bfloat16) for p in planes], axis=0)
        return _to_pack_rows(n, w[n][l]).astype(BF16)

    packs = [jnp.concatenate([pack_rows(n, l) for n, _ in WEIGHT_PACK], axis=0) for l in range(depth)]
    weights = _Weights(packs)
    params = [_Params(weights, l, {n: w[n][l] for n, _, axis in SMALL if axis is None}) for l in range(depth)]

    inv_freq = ROPE_THETA ** (-jnp.arange(HD // 2, dtype=F32) / (HD // 2))
    cos, sin = _rope_tables(positions.reshape(s_len, 1), jnp.tile(inv_freq, 4).reshape(1, LANES), name="rope_tables")

    saved = []
    for l in range(depth):
        h, sv = _layer_fwd(h, mem0, params[l], cos, sin, f"_l{l}", functools.partial(weights.ride, l))
        saved.append(sv)
    dh, sq = _loss_grad(h, loss_target[0], tm=512, name="loss_grad")
    loss = lax.psum(0.5 * jnp.sum(sq) / D_MODEL, ("x", "y", "c"))

    c_arr, j_arr = jnp.reshape(ci, (1,)).astype(jnp.int32), jnp.reshape(chip, (1,)).astype(jnp.int32)

    def new_rs(names, g, tag):
        blocks = [_blocks_from_full(n, g[n]) for n in names]
        if names is REST_GROUP:
            blocks.append(jnp.stack([_small_to_rows([_chip_block(g[n], j, shape, axis) for n, shape, axis in SHARDED_SMALL])
                                     for j in range(4)]))
        return _ReduceScatter(jnp.concatenate(blocks, axis=1), c_arr, j_arr, tag)

    grads, later_rows, rest_rows, pending = [None] * depth, [None] * depth, [None] * depth, None
    for l in reversed(range(depth)):
        dh, grads[l], later_rows[l], rode = _layer_bwd(dh, mem0, params[l], saved[l], cos, sin, f"_l{l}", pending, new_rs)
        if pending is not None:
            rest_rows[l + 1] = rode
        pending = new_rs(REST_GROUP, grads[l], f"_l{l}_rest")
    grad_x = dh[None]

    out_g, out_d, out_m, out_v = {}, {}, {}, {}
    pack_off, off = {}, 0
    for n, rows in PACK_ROWS:
        pack_off[n] = (off, rows)
        off += rows

    def reduced(l, n):
        start, rows = pack_off[n]
        for names, block in ((REST_GROUP, rest_rows[l]), (XA_GROUP, later_rows[l][0]), (FFN_GROUP, later_rows[l][1])):
            if n in names:
                return block[start - pack_off[names[0]][0]:][:rows]

    def update(n, carry=None):
        shard_shape = w[n].shape
        g_n = jnp.stack([reduced(l, n) for l in range(depth)]).reshape(shard_shape)
        flat = lambda a: a.reshape(-1, shard_shape[-1])
        d_n, m_n, v_n, *rode = _adamw(flat(w[n]), flat(g_n), flat(m[n]), flat(v[n]), name=f"adamw_{n}", carry=carry)
        out_g[n], out_d[n], out_m[n], out_v[n] = g_n, d_n.reshape(shard_shape), m_n.reshape(shard_shape), v_n.reshape(shard_shape)
        return rode

    rest_rows[0] = pending.run()
    for n, _ in PACK_ROWS:
        update(n)

    g_small = _all_reduce_small(_flat_rows([grads[l][n] for l in range(depth) for n, _, axis in SMALL if axis is None]),
                                name="all_reduce_small_grads").reshape(-1)
    local_g, off = {}, 0
    for l in range(depth):
        for n, shape, axis in SMALL:
            if axis is None:
                size = int(np.prod(shape))
                local_g.setdefault(n, []).append(g_small[off:off + size].reshape(shape))
                off += size
        small_rows = rest_rows[l][sum(pack_off[n][1] for n in REST_GROUP):]
        for (n, _, _), block in zip(SHARDED_SMALL, _small_from_rows(small_rows)):
            local_g.setdefault(n, []).append(block)
    names = [n for n, _, _ in SMALL]
    g_loc = {n: jnp.stack(local_g[n]) for n in names}
    d_s, m_s, v_s = _adamw(_flat_rows([w[n] for n in names]), _flat_rows([g_loc[n] for n in names]),
                           _flat_rows([m[n] for n in names]), _flat_rows([v[n] for n in names]), name="adamw_small")
    off = 0
    for n in names:
        size = int(np.prod(w[n].shape))
        take = lambda a: a.reshape(-1)[off:off + size].reshape(w[n].shape)
        out_g[n], out_d[n], out_m[n], out_v[n] = g_loc[n], take(d_s), take(m_s), take(v_s)
        off += size

    outs = [loss, grad_x]
    for group in (out_g, out_d, out_m, out_v):
        outs += [group[n] for n in WEIGHT_ORDER]
    return tuple(outs)


def kernel(x, mem, positions, mix_norm_pre, w_in, cv_w, cv_b, cv_ln_g, cv_ln_b, cv_pw_w, cv_pw_b, w_out, mix_norm_post, x_norm_pre, mem_norm, x_wq, x_wk, x_wv, x_wo, x_norm_post, ffn_norm_pre, ffn_w_up, ffn_conv_w, ffn_conv_b, ffn_w_down, ffn_norm_post, loss_target, m_mix_norm_pre, m_w_in, m_cv_w, m_cv_b, m_cv_ln_g, m_cv_ln_b, m_cv_pw_w, m_cv_pw_b, m_w_out, m_mix_norm_post, m_x_norm_pre, m_mem_norm, m_x_wq, m_x_wk, m_x_wv, m_x_wo, m_x_norm_post, m_ffn_norm_pre, m_ffn_w_up, m_ffn_conv_w, m_ffn_conv_b, m_ffn_w_down, m_ffn_norm_post, v_mix_norm_pre, v_w_in, v_cv_w, v_cv_b, v_cv_ln_g, v_cv_ln_b, v_cv_pw_w, v_cv_pw_b, v_w_out, v_mix_norm_post, v_x_norm_pre, v_mem_norm, v_x_wq, v_x_wk, v_x_wv, v_x_wo, v_x_norm_post, v_ffn_norm_pre, v_ffn_w_up, v_ffn_conv_w, v_ffn_conv_b, v_ffn_w_down, v_ffn_norm_post):
    w = dict(zip(WEIGHT_ORDER, (mix_norm_pre, w_in, cv_w, cv_b, cv_ln_g, cv_ln_b, cv_pw_w, cv_pw_b, w_out, mix_norm_post, x_norm_pre, mem_norm, x_wq, x_wk, x_wv, x_wo, x_norm_post, ffn_norm_pre, ffn_w_up, ffn_conv_w, ffn_conv_b, ffn_w_down, ffn_norm_post)))
    m = dict(zip(WEIGHT_ORDER, (m_mix_norm_pre, m_w_in, m_cv_w, m_cv_b, m_cv_ln_g, m_cv_ln_b, m_cv_pw_w, m_cv_pw_b, m_w_out, m_mix_norm_post, m_x_norm_pre, m_mem_norm, m_x_wq, m_x_wk, m_x_wv, m_x_wo, m_x_norm_post, m_ffn_norm_pre, m_ffn_w_up, m_ffn_conv_w, m_ffn_conv_b, m_ffn_w_down, m_ffn_norm_post)))
    v = dict(zip(WEIGHT_ORDER, (v_mix_norm_pre, v_w_in, v_cv_w, v_cv_b, v_cv_ln_g, v_cv_ln_b, v_cv_pw_w, v_cv_pw_b, v_w_out, v_mix_norm_post, v_x_norm_pre, v_mem_norm, v_x_wq, v_x_wk, v_x_wv, v_x_wo, v_x_norm_post, v_ffn_norm_pre, v_ffn_w_up, v_ffn_conv_w, v_ffn_conv_b, v_ffn_w_down, v_ffn_norm_post)))
    return _step(x, mem, positions, loss_target, w, m, v)
```

```python
import functools

import jax
import jax.numpy as jnp
import numpy as np
from jax import lax
from jax.experimental import pallas as pl
from jax.experimental.pallas import tpu as pltpu

F32, BF16 = jnp.float32, jnp.bfloat16
MESH = pl.DeviceIdType.MESH
EPS = 1e-6
LANES = 128
BLK = 128
HD = 64
D_MODEL = 1024
D_FF = 2816
SB_W, CV_W, DL_W = 256, 256, 512
CV_K = 31
ROPE_THETA = 10000.0
DILATIONS = (1, 4, 16)
X_HEADS, X_HD = 4, 256
ADAM_LR, ADAM_B1, ADAM_B2, ADAM_EPS, ADAM_WD, ADAM_STEP = 0.001, 0.9, 0.999, 1e-08, 0.01, 10
NEG_INF = float("-inf")
MIB = 1 << 20

PACK_ROWS = (("w_in", 704), ("w_out", 256), ("x_wq", 256), ("x_wk", 256), ("x_wv", 256), ("x_wo", 256),
             ("ffn_w_up", 1408), ("ffn_w_down", 704))
PACK_RL = sum(r for _, r in PACK_ROWS)
SHARDED_SMALL = (("cv_w", (31, 64), 1), ("ffn_conv_w", (3, 1408), 1), ("cv_pw_w", (64, 256), 0))
SMALL_ROWS = 32
WEIGHT_PACK = (PACK_ROWS[0], ("small", 2 * SMALL_ROWS)) + PACK_ROWS[1:]


def _cp(vmem_mb=48):
    return pltpu.CompilerParams(vmem_limit_bytes=vmem_mb * MIB)


def _dot(a, b):
    return jnp.dot(a, b, preferred_element_type=F32)


def _dot_nt(a, b):
    return lax.dot_general(a, b, (((1,), (1,)), ((), ())), preferred_element_type=F32)


def _dot_tn(a, b):
    return lax.dot_general(a, b, (((0,), (0,)), ((), ())), preferred_element_type=F32)


def _dot_hilo(x, m):
    hi = x.astype(BF16)
    lo = (x - hi.astype(F32)).astype(BF16)
    return _dot(hi, m) + _dot(lo, m)


def _rowsum8(x):
    t, c = x.shape
    return x.reshape(t // 8, 8, c).sum(axis=0)


def _acc_out(ref, i, val):
    @pl.when(i == 0)
    def _():
        ref[...] = val

    @pl.when(i > 0)
    def _():
        ref[...] += val


def _tile(n, cap, mult=8):
    t = min(n, cap)
    while n % t or t % mult:
        t -= 1
    return t


def _rms_mm(x, g, w, *, tm, tn, out_dtype, name, carry=None):
    m, d = x.shape
    n_out = w.shape[1]

    def body(x_ref, g_ref, w_ref, n_ref, o_ref):
        @pl.when(pl.program_id(1) == 0)
        def _():
            xv = x_ref[...]
            r = lax.rsqrt(jnp.mean(xv * xv, axis=-1, keepdims=True) + EPS)
            n_ref[...] = (xv * r * g_ref[...]).astype(BF16)

        o_ref[...] = _dot(n_ref[...], w_ref[...]).astype(out_dtype)

    return _call(
        body, grid=(m // tm, n_out // tn), name=name, carry=carry,
        in_specs=[pl.BlockSpec((tm, d), lambda i, j: (i, 0)), pl.BlockSpec((1, d), lambda i, j: (0, 0)),
                  pl.BlockSpec((d, tn), lambda i, j: (0, j))],
        out_specs=[pl.BlockSpec((tm, d), lambda i, j: (i, 0)), pl.BlockSpec((tm, tn), lambda i, j: (i, j))],
        out_shape=[jax.ShapeDtypeStruct((m, d), BF16), jax.ShapeDtypeStruct((m, n_out), out_dtype)],
        args=(x, g, w))


def _mm_post(a, w, h, g, *, tm, name, carry=None):
    m, k = a.shape
    d = w.shape[1]

    def body(a_ref, w_ref, h_ref, g_ref, y_ref, ho_ref):
        y = _dot(a_ref[...], w_ref[...])
        y_ref[...] = y
        r = lax.rsqrt(jnp.mean(y * y, axis=-1, keepdims=True) + EPS)
        ho_ref[...] = h_ref[...] + y * r * g_ref[...]

    return _call(
        body, grid=(m // tm,), name=name, carry=carry,
        in_specs=[pl.BlockSpec((tm, k), lambda i: (i, 0)), pl.BlockSpec((k, d), lambda i: (0, 0)),
                  pl.BlockSpec((tm, d), lambda i: (i, 0)), pl.BlockSpec((1, d), lambda i: (0, 0))],
        out_specs=[pl.BlockSpec((tm, d), lambda i: (i, 0)), pl.BlockSpec((tm, d), lambda i: (i, 0))],
        out_shape=[jax.ShapeDtypeStruct((m, d), F32), jax.ShapeDtypeStruct((m, d), F32)],
        args=(a, w, h, g))


def _mm_nt(a, w, *, tm, tn, out_dtype, name):
    m, k = a.shape
    n_out = w.shape[0]

    def body(a_ref, w_ref, o_ref):
        o_ref[...] = _dot_nt(a_ref[...], w_ref[...]).astype(out_dtype)

    return pl.pallas_call(
        body, grid=(n_out // tn, m // tm), name=name,
        in_specs=[pl.BlockSpec((tm, k), lambda j, i: (i, 0)), pl.BlockSpec((tn, k), lambda j, i: (j, 0))],
        out_specs=pl.BlockSpec((tm, tn), lambda j, i: (i, j)),
        out_shape=jax.ShapeDtypeStruct((m, n_out), out_dtype),
        compiler_params=_cp())(a, w)


def _mm_tn(x, dy, *, tk, tn, tm, name):
    m, k = x.shape
    n_out = dy.shape[1]

    def body(x_ref, d_ref, o_ref):
        _acc_out(o_ref, pl.program_id(2), _dot_tn(x_ref[...], d_ref[...]))

    return pl.pallas_call(
        body, grid=(k // tk, n_out // tn, m // tm), name=name,
        in_specs=[pl.BlockSpec((tm, tk), lambda a, b, c: (c, a)), pl.BlockSpec((tm, tn), lambda a, b, c: (c, b))],
        out_specs=pl.BlockSpec((tk, tn), lambda a, b, c: (a, b)),
        out_shape=jax.ShapeDtypeStruct((k, n_out), F32),
        compiler_params=_cp())(x, dy)


def _rms_bwd(x, g, dout, res, *, out_dtype, tm, name, carry=None):
    m, d = x.shape
    has_res = res is not None

    def body(*refs):
        if has_res:
            x_ref, g_ref, d_ref, r_ref, dx_ref, dg_ref = refs
        else:
            x_ref, g_ref, d_ref, dx_ref, dg_ref = refs
        xv = x_ref[...]
        dv = d_ref[...].astype(F32)
        r = lax.rsqrt(jnp.mean(xv * xv, axis=-1, keepdims=True) + EPS)
        xh = xv * r
        dxh = dv * g_ref[...]
        dx = r * (dxh - xh * jnp.mean(dxh * xh, axis=-1, keepdims=True))
        if has_res:
            dx = dx + r_ref[...]
        dx_ref[...] = dx.astype(out_dtype)
        _acc_out(dg_ref, pl.program_id(0), _rowsum8(dv * xh))

    row = pl.BlockSpec((tm, d), lambda i: (i, 0))
    ins = [row, pl.BlockSpec((1, d), lambda i: (0, 0)), row] + ([row] if has_res else [])
    args = (x, g, dout) + ((res,) if has_res else ())
    return _call(
        body, grid=(m // tm,), name=name, carry=carry, in_specs=ins,
        out_specs=[row, pl.BlockSpec((8, d), lambda i: (0, 0))],
        out_shape=[jax.ShapeDtypeStruct((m, d), out_dtype), jax.ShapeDtypeStruct((8, d), F32)], args=args)


def _loss_grad(h, tgt, *, tm, name):
    m, d = h.shape

    def body(h_ref, t_ref, dh_ref, p_ref):
        e = h_ref[...] - t_ref[...]
        dh_ref[...] = e / d
        _acc_out(p_ref, pl.program_id(0), _rowsum8(e * e))

    row = pl.BlockSpec((tm, d), lambda i: (i, 0))
    return pl.pallas_call(
        body, grid=(m // tm,), name=name, in_specs=[row, row],
        out_specs=[row, pl.BlockSpec((8, d), lambda i: (0, 0))],
        out_shape=[jax.ShapeDtypeStruct((m, d), F32), jax.ShapeDtypeStruct((8, d), F32)],
        compiler_params=_cp())(h, tgt)


def _adamw(w, g, m, v, *, name, carry=None):
    r, c = w.shape
    tr = _tile(r, 256)

    def body(w_ref, g_ref, m_ref, v_ref, d_ref, mo_ref, vo_ref):
        gv = g_ref[...]
        m2 = ADAM_B1 * m_ref[...] + (1.0 - ADAM_B1) * gv
        v2 = ADAM_B2 * v_ref[...] + (1.0 - ADAM_B2) * jnp.square(gv)
        m_hat = m2 / (1.0 - ADAM_B1 ** ADAM_STEP)
        v_hat = v2 / (1.0 - ADAM_B2 ** ADAM_STEP)
        d_ref[...] = -ADAM_LR * (m_hat / (jnp.sqrt(v_hat) + ADAM_EPS) + ADAM_WD * w_ref[...])
        mo_ref[...] = m2
        vo_ref[...] = v2

    blk = pl.BlockSpec((tr, c), lambda i: (i, 0))
    return _call(body, grid=(r // tr,), name=name, carry=carry, in_specs=[blk] * 4, out_specs=[blk] * 3,
                 out_shape=[jax.ShapeDtypeStruct((r, c), F32)] * 3, args=(w, g, m, v))


def _head_masks():
    lane = lax.broadcasted_iota(jnp.int32, (BLK, LANES), 1)
    row = lax.broadcasted_iota(jnp.int32, (BLK, LANES), 0)
    return lane, row, lane < HD


def _sb_scores(q_a, k, before):
    z = _dot_nt(q_a, k)
    sp = jnp.log1p(jnp.exp(-jnp.abs(z)))
    ls_pos = jnp.minimum(z, 0.0) - sp
    lkeep = jnp.where(before, ls_pos - z, 0.0)
    return ls_pos, lkeep


SB_DEAD = -104.0


def _sb_alive(jj, i, carry):
    return jnp.logical_and(jj <= i, jnp.max(carry) > SB_DEAD)


SB_QB_FWD = 2
SB_QB = 2


def _sb_before(jj, qb=SB_QB):
    lane = lax.broadcasted_iota(jnp.int32, (qb * 2 * BLK, LANES), 1)
    row = lax.broadcasted_iota(jnp.int32, (qb * 2 * BLK, LANES), 0)
    below_diag = jj - (qb - 1) + row // (2 * BLK)
    return jnp.logical_or(below_diag > 0, jnp.logical_and(below_diag == 0, lane < row % BLK))


def _sb_stack(x, lane_h, qb=SB_QB):
    return jnp.concatenate([_stack_heads(x[b * BLK:(b + 1) * BLK], lane_h) for b in range(qb)], axis=0)


def _sb_unstack(x, lane_h, qb=SB_QB):
    return jnp.concatenate([jnp.where(lane_h, x[2 * b * BLK:(2 * b + 1) * BLK], x[(2 * b + 1) * BLK:(2 * b + 2) * BLK])
                            for b in range(qb)], axis=0)


SB_ROWS = SB_QB * 2 * BLK


def _sb_fwd(u, *, name, carry=None):
    s_len = u.shape[0]
    qb = SB_QB_FWD
    qrows, rows = qb * BLK, qb * 2 * BLK

    def body(q_ref, k_ref, v_ref, o_ref):
        top = pl.program_id(0) * qb + qb - 1
        lane, row, lane_h = _head_masks()
        suffix = (row > lane).astype(BF16)
        pairs = [slice(hp * LANES, (hp + 1) * LANES) for hp in range(2)]
        qs = [_sb_stack(q_ref[:, cs] * 0.125, lane_h, qb) for cs in pairs]

        def step(state):
            jj, ccs, accs = state[0], state[1:3], state[3:5]
            rows_k = pl.ds(pl.multiple_of((top - jj) * BLK, BLK), BLK)
            before = _sb_before(jj, qb)
            scores = [_sb_scores(q, k_ref[rows_k, cs].astype(BF16), before) for q, cs in zip(qs, pairs)]
            between = [_dot_hilo(lkeep, suffix) + cc for (_, lkeep), cc in zip(scores, ccs)]
            atts = [jnp.where(before, jnp.exp(ls_pos + b), 0.0).astype(BF16) for (ls_pos, _), b in zip(scores, between)]
            new_cc = [cc + jnp.sum(lkeep, axis=1, keepdims=True) for (_, lkeep), cc in zip(scores, ccs)]
            new_acc = [acc + _dot(a, v_ref[rows_k, cs].astype(BF16)) for a, acc, cs in zip(atts, accs, pairs)]
            return (jj + 1, *new_cc, *new_acc)

        zc, za = jnp.zeros((rows, 1), F32), jnp.zeros((rows, LANES), F32)
        res = lax.while_loop(lambda st: _sb_alive(st[0], top, jnp.maximum(st[1], st[2])), step,
                             (jnp.int32(0), zc, zc, za, za))
        for hp, cs in enumerate(pairs):
            o_ref[:, cs] = _sb_unstack(res[3 + hp], lane_h, qb).astype(BF16)

    wide = 2 * LANES
    return _call(
        body, grid=(s_len // qrows,), name=name, carry=carry,
        in_specs=[pl.BlockSpec((qrows, wide), lambda i: (i, 0)), pl.BlockSpec((s_len, wide), lambda i: (0, 1)),
                  pl.BlockSpec((s_len, wide), lambda i: (0, 2))],
        out_specs=[pl.BlockSpec((qrows, wide), lambda i: (i, 0))],
        out_shape=[jax.ShapeDtypeStruct((s_len, SB_W), BF16)], args=(u, u, u))


def _sb_bwd(u, dcat, *, name, carry=None):
    s_len = u.shape[0]
    nq = s_len // BLK
    qrows = SB_QB * BLK

    def body(q_ref, k_ref, v_ref, do_ref, dq_ref, dk_ref, dv_ref, g_scr, b_scr):
        step = pl.program_id(1)
        top = step * SB_QB + SB_QB - 1
        lane, row, lane_h = _head_masks()
        suffix = (row > lane).astype(BF16)
        prefix = (row < lane).astype(BF16)
        qf = q_ref[...]
        qs = _sb_stack(qf * 0.125, lane_h)
        qu = _sb_stack(qf, lane_h)
        dos = _sb_stack(do_ref[...], lane_h)

        @pl.when(step == 0)
        def _():
            dk_ref[...] = jnp.zeros_like(dk_ref)
            dv_ref[...] = jnp.zeros_like(dv_ref)

        def down(state):
            jj, cc = state
            j = top - jj
            off = pl.multiple_of(j * BLK, BLK)
            k = k_ref[pl.ds(off, BLK), :].astype(BF16)
            v = v_ref[pl.ds(off, BLK), :].astype(BF16)
            before = _sb_before(jj)
            ls_pos, lkeep = _sb_scores(qs, k, before)
            between = _dot_hilo(lkeep, suffix) + cc
            att = jnp.where(before, jnp.exp(ls_pos + between), 0.0)
            g_scr[j] = att * _dot_nt(dos, v)
            b_scr[j] = jnp.exp(ls_pos)
            dv_ref[pl.ds(off, BLK), :] += _dot_tn(att.astype(BF16), dos)
            return jj + 1, cc + jnp.sum(lkeep, axis=1, keepdims=True)

        zc = jnp.zeros((SB_ROWS, 1), F32)
        visited = lax.while_loop(lambda st: _sb_alive(st[0], top, st[1]), down, (jnp.int32(0), zc))[0]

        def up(j, carry):
            pc, dq = carry
            off = pl.multiple_of(j * BLK, BLK)
            k = k_ref[pl.ds(off, BLK), :].astype(BF16)
            g, beta = g_scr[j], b_scr[j]
            below = _dot_hilo(g, prefix) + pc
            dz = (jnp.where(_sb_before(top - j), g * (1.0 - beta) - beta * below, 0.0) * 0.125).astype(BF16)
            dk_ref[pl.ds(off, BLK), :] += _dot_tn(dz, qu)
            return pc + jnp.sum(g, axis=1, keepdims=True), dq + _dot(dz, k)

        dq = lax.fori_loop(top + 1 - visited, top + 1, up, (zc, jnp.zeros((SB_ROWS, LANES), F32)))[1]
        dq_ref[...] = _sb_unstack(dq, lane_h)

    col = lambda c0: pl.BlockSpec((s_len, LANES), lambda hp, i: (0, c0 + hp))
    blk = pl.BlockSpec((qrows, LANES), lambda hp, i: (i, hp))
    acc = pl.BlockSpec((s_len, LANES), lambda hp, i: (0, hp))
    return _call(
        body, grid=(2, s_len // qrows), name=name, carry=carry, in_specs=[blk, col(2), col(4), blk],
        out_specs=[blk, acc, acc], out_shape=[jax.ShapeDtypeStruct((s_len, SB_W), F32)] * 3,
        scratch_shapes=[pltpu.VMEM((nq, SB_ROWS, LANES), F32), pltpu.VMEM((nq, SB_ROWS, LANES), F32)],
        vmem_mb=56, args=(u, u, u, dcat))


CV_T = 512
CV_H = 32
CV_STRIP = 64


def _cv_specs(s_len):
    cur = lambda c: pl.BlockSpec((CV_T, CV_W), lambda i: (i, c))
    prev = lambda c: pl.BlockSpec((CV_H, CV_W), lambda i: (jnp.maximum(i * (CV_T // CV_H) - 1, 0), c))
    nxt = lambda c: pl.BlockSpec((CV_H, CV_W),
                                 lambda i: (jnp.minimum((i + 1) * (CV_T // CV_H), s_len // CV_H - 1), c))
    full = lambda r: pl.BlockSpec((r, CV_W), lambda i: (0, 0))
    return cur, prev, nxt, full


def _glu_into(gp_ref, val_ref, gate_ref, valp_ref, gatep_ref, i):
    gp_ref[0:CV_H, :] = jnp.where(i > 0, valp_ref[...] * jax.nn.sigmoid(gatep_ref[...]), 0.0)
    gp_ref[CV_H:, :] = val_ref[...] * jax.nn.sigmoid(gate_ref[...])


CV_SH = CV_T + CV_H - 8


def _shifted_copies(sh_ref, slab_ref):
    for r in range(1, 8):
        sh_ref[r - 1] = slab_ref[pl.ds(r, CV_SH), :]


def _tap(sh_ref, slab_ref, off, r0, rows):
    if off % 8 == 0:
        return slab_ref[pl.ds(off + r0, rows), :]
    return sh_ref[off % 8 - 1, pl.ds(off - off % 8 + r0, rows), :]


def _cv_fwd(u, cv_w, cv_b, ln_g, ln_b, pw_w, pw_b, *, name):
    s_len = u.shape[0]
    cur, prev, _, full = _cv_specs(s_len)

    def body(val_ref, gate_ref, valp_ref, gatep_ref, w_ref, b_ref, g_ref, be_ref, pw_ref, pb_ref,
             o_ref, c_ref, gp_ref, sh_ref):
        _glu_into(gp_ref, val_ref, gate_ref, valp_ref, gatep_ref, pl.program_id(0))
        _shifted_copies(sh_ref, gp_ref)
        for r0, rows in _strips(CV_T, CV_STRIP):
            acc = jnp.zeros((rows, CV_W), F32) + b_ref[...]
            for k in range(CV_K):
                acc = acc + w_ref[k:k + 1, :] * _tap(sh_ref, gp_ref, CV_H - CV_K + 1 + k, r0, rows)
            c_ref[r0:r0 + rows, :] = acc
        acc = c_ref[...]
        mu = jnp.mean(acc, axis=-1, keepdims=True)
        xc = acc - mu
        xh = xc * lax.rsqrt(jnp.mean(xc * xc, axis=-1, keepdims=True) + EPS)
        a = xh * g_ref[...] + be_ref[...]
        s = a * jax.nn.sigmoid(a)
        o_ref[...] = (_dot(s.astype(BF16), pw_ref[...]) + pb_ref[...]).astype(BF16)

    return pl.pallas_call(
        body, grid=(s_len // CV_T,), name=name,
        in_specs=[cur(3), cur(4), prev(3), prev(4), full(CV_K), full(1), full(1), full(1), full(CV_W), full(1)],
        out_specs=[cur(0), cur(0)],
        out_shape=[jax.ShapeDtypeStruct((s_len, CV_W), BF16), jax.ShapeDtypeStruct((s_len, CV_W), F32)],
        scratch_shapes=[pltpu.VMEM((CV_T + CV_H, CV_W), F32), pltpu.VMEM((7, CV_SH, CV_W), F32)],
        compiler_params=_cp())(u, u, u, u, cv_w, cv_b, ln_g, ln_b, pw_w, pw_b)


def _cv_bwd_local(c, dcat, ln_g, ln_b, pw_w, *, name):
    s_len = c.shape[0]
    cur, _, _, full = _cv_specs(s_len)

    def body(c_ref, db_ref, g_ref, be_ref, pw_ref, dc_ref, dpw_ref, vec_ref):
        i = pl.program_id(0)
        cv = c_ref[...]
        db = db_ref[...]
        mu = jnp.mean(cv, axis=-1, keepdims=True)
        xc = cv - mu
        rstd = lax.rsqrt(jnp.mean(xc * xc, axis=-1, keepdims=True) + EPS)
        xh = xc * rstd
        a = xh * g_ref[...] + be_ref[...]
        sg = jax.nn.sigmoid(a)
        s = a * sg
        dbb = db.astype(BF16)
        ds = _dot_nt(dbb, pw_ref[...])
        da = ds * (sg * (1.0 + a * (1.0 - sg)))
        dxh = da * g_ref[...]
        dc_ref[...] = rstd * (dxh - jnp.mean(dxh, axis=-1, keepdims=True)
                              - xh * jnp.mean(dxh * xh, axis=-1, keepdims=True))
        _acc_out(dpw_ref, i, _dot_tn(s.astype(BF16), dbb))
        _acc_out(vec_ref, i, jnp.concatenate([_rowsum8(db), _rowsum8(da * xh), _rowsum8(da)], axis=0))

    return pl.pallas_call(
        body, grid=(s_len // CV_T,), name=name,
        in_specs=[cur(0), cur(1), full(1), full(1), full(CV_W)],
        out_specs=[cur(0), full(CV_W), full(24)],
        out_shape=[jax.ShapeDtypeStruct((s_len, CV_W), F32), jax.ShapeDtypeStruct((CV_W, CV_W), F32),
                   jax.ShapeDtypeStruct((24, CV_W), F32)], compiler_params=_cp())(c, dcat, ln_g, ln_b, pw_w)


def _cv_bwd_conv(u, dc, cv_w, *, name):
    s_len = u.shape[0]
    cur, prev, nxt, full = _cv_specs(s_len)
    last = s_len // CV_T - 1

    def body(val_ref, gate_ref, valp_ref, gatep_ref, dc_ref, dcn_ref, w_ref, du_ref, dw_ref, dbias_ref,
             gp_ref, dcp_ref, gsh_ref, dsh_ref):
        i = pl.program_id(0)
        _glu_into(gp_ref, val_ref, gate_ref, valp_ref, gatep_ref, i)
        dcv = dc_ref[...]
        dcp_ref[0:CV_T, :] = dcv
        dcp_ref[CV_T:, :] = jnp.where(i < last, dcn_ref[...], 0.0)
        _shifted_copies(gsh_ref, gp_ref)
        _shifted_copies(dsh_ref, dcp_ref)
        strips = _strips(CV_T, CV_STRIP)
        for r0, rows in strips:
            dg = jnp.zeros((rows, CV_W), F32)
            for k in range(CV_K):
                dg = dg + w_ref[k:k + 1, :] * _tap(dsh_ref, dcp_ref, CV_K - 1 - k, r0, rows)
            sg = jax.nn.sigmoid(gate_ref[r0:r0 + rows, :])
            du_ref[r0:r0 + rows, 0:CV_W] = (dg * sg).astype(BF16)
            du_ref[r0:r0 + rows, CV_W:] = (dg * val_ref[r0:r0 + rows, :] * sg * (1.0 - sg)).astype(BF16)
        parts = []
        for k in range(CV_K):
            part = jnp.zeros((8, CV_W), F32)
            for r0, rows in strips:
                part = part + _rowsum8(dc_ref[r0:r0 + rows, :] * _tap(gsh_ref, gp_ref, CV_H - CV_K + 1 + k, r0, rows))
            parts.append(part)
        _acc_out(dw_ref, i, jnp.concatenate(parts, axis=0))
        _acc_out(dbias_ref, i, _rowsum8(dcv))

    return pl.pallas_call(
        body, grid=(s_len // CV_T,), name=name,
        in_specs=[cur(3), cur(4), prev(3), prev(4), cur(0), nxt(0), full(CV_K)],
        out_specs=[pl.BlockSpec((CV_T, 2 * CV_W), lambda i: (i, 0)), full(CV_K * 8), full(8)],
        out_shape=[jax.ShapeDtypeStruct((s_len, 2 * CV_W), BF16), jax.ShapeDtypeStruct((CV_K * 8, CV_W), F32),
                   jax.ShapeDtypeStruct((8, CV_W), F32)],
        scratch_shapes=[pltpu.VMEM((CV_T + CV_H, CV_W), F32)] * 2 + [pltpu.VMEM((7, CV_SH, CV_W), F32)] * 2,
        compiler_params=_cp())(u, u, u, u, dc, dc, cv_w)


def _rope_tables(pos_col, inv_freq_row, *, name):
    s_len = pos_col.shape[0]

    def body(p_ref, f_ref, cos_ref, sin_ref):
        ang = p_ref[...].astype(F32) * f_ref[...]
        lane = lax.broadcasted_iota(jnp.int32, (s_len, LANES), 1)
        sn = jnp.sin(ang)
        cos_ref[...] = jnp.cos(ang)
        sin_ref[...] = jnp.where(lane % HD < HD // 2, -sn, sn)

    return pl.pallas_call(body, name=name, out_shape=[jax.ShapeDtypeStruct((s_len, LANES), F32)] * 2,
                          compiler_params=_cp())(pos_col, inv_freq_row)


def _rot_half(x):
    lane = lax.broadcasted_iota(jnp.int32, x.shape, 1)
    return jnp.where(lane % HD < HD // 2, pltpu.roll(x, LANES - HD // 2, 1), pltpu.roll(x, HD // 2, 1))


def _permute_rows(dst_ref, src_ref, d, dtype):
    s_len = src_ref.shape[0]
    seg = s_len // d
    if d == 1:
        dst_ref[...] = src_ref[...].astype(dtype)
        return
    for r in range(d):
        dst_ref[r * seg:(r + 1) * seg, :] = src_ref[pl.ds(r, seg, stride=d), :].astype(dtype)


def _unpermute_rows(dst_ref, src_ref, d):
    s_len = src_ref.shape[0]
    seg = s_len // d
    if d == 1:
        dst_ref[...] = src_ref[...]
        return
    for r in range(d):
        dst_ref[pl.ds(r, seg, stride=d), :] = src_ref[r * seg:(r + 1) * seg, :]


def _rope_perm(u, cos, sin, *, name):
    s_len = u.shape[0]

    def body(x_ref, cos_ref, sin_ref, o_ref, scr):
        a = pl.program_id(0)
        x = x_ref[...]
        rot = a < 2
        scr[...] = x * jnp.where(rot, cos_ref[...], 1.0) + _rot_half(x) * jnp.where(rot, sin_ref[...], 0.0)
        for n, d in enumerate(DILATIONS):
            _permute_rows(o_ref.at[n], scr, d, BF16)

    tab = pl.BlockSpec((s_len, LANES), lambda a, cb: (0, 0))
    return pl.pallas_call(
        body, grid=(3, 4), name=name,
        in_specs=[pl.BlockSpec((s_len, LANES), lambda a, cb: (0, 10 + 4 * a + cb)), tab, tab],
        out_specs=pl.BlockSpec((None, 3, s_len, LANES), lambda a, cb: (a, 0, 0, cb)),
        out_shape=jax.ShapeDtypeStruct((3, 3, s_len, DL_W), BF16),
        scratch_shapes=[pltpu.VMEM((s_len, LANES), F32)], compiler_params=_cp())(u, cos, sin)


DL_UNROLL = 4


def _dl_band(rows):
    lane = lax.broadcasted_iota(jnp.int32, (rows, LANES), 1)
    row = lax.broadcasted_iota(jnp.int32, (rows, LANES), 0) % BLK
    return lane <= row, lane >= row


def _dl_first(s_len, n, i):
    nb = jnp.where(n == 0, s_len // BLK, jnp.where(n == 1, s_len // (BLK * DILATIONS[1]),
                                                   s_len // (BLK * DILATIONS[2])))
    return lax.rem(i, nb) == 0


def _stack_heads(x, lane_h):
    return jnp.concatenate([jnp.where(lane_h, x, 0.0), jnp.where(lane_h, 0.0, x)], axis=0).astype(BF16)


def _dl_rows(i):
    cur = pl.ds(pl.multiple_of(i * BLK, BLK), BLK)
    prev = pl.ds(pl.multiple_of(jnp.maximum(i - 1, 0) * BLK, BLK), BLK)
    return cur, prev


def _dl_in_specs(s_len):
    return [pl.BlockSpec((None, None, s_len, LANES), functools.partial(lambda a, n, hp: (a, n, 0, hp), a))
            for a in range(3)]


def _dl_fwd(qkv, *, name, carry=None):
    s_len = qkv.shape[2]

    def body(q_ref, k_ref, v_ref, o_ref, l_ref):
        n = pl.program_id(0)
        lane_h = _head_masks()[2]
        band_c, band_p = _dl_band(2 * BLK)
        ones = jnp.ones((BLK, LANES), BF16)

        @pl.loop(0, s_len // BLK, step=DL_UNROLL)
        def _(i0):
            blocks = [i0 + t for t in range(DL_UNROLL)]
            rows = [_dl_rows(i) for i in blocks]
            scores = []
            for cur, prev in rows:
                qs = _stack_heads(q_ref[cur, :] * 0.125, lane_h)
                scores.append((_dot_nt(qs, k_ref[cur, :]), _dot_nt(qs, k_ref[prev, :])))
            probs = []
            for i, (sc, sp) in zip(blocks, scores):
                sc = jnp.where(band_c, sc, NEG_INF)
                sp = jnp.where(jnp.logical_and(band_p, jnp.logical_not(_dl_first(s_len, n, i))), sp, NEG_INF)
                m = jnp.max(jnp.maximum(sc, sp), axis=1, keepdims=True)
                probs.append((jnp.exp(sc - m).astype(BF16), jnp.exp(sp - m).astype(BF16), m))
            for (cur, prev), (pc, pp, m) in zip(rows, probs):
                r = (_dot(pc, jnp.concatenate([v_ref[cur, :], ones], axis=1))
                     + _dot(pp, jnp.concatenate([v_ref[prev, :], ones], axis=1)))
                den = jnp.where(lane_h, r[:BLK, LANES:], r[BLK:, LANES:])
                o_ref[cur, :] = jnp.where(lane_h, r[:BLK, :LANES], r[BLK:, :LANES]) / den
                l_ref[cur, :] = jnp.where(lane_h, m[:BLK], m[BLK:]) + jnp.log(den)

    out = pl.BlockSpec((None, s_len, LANES), lambda n, hp: (n, 0, hp))
    return _call(
        body, grid=(3, 4), name=name, carry=carry, in_specs=_dl_in_specs(s_len), out_specs=[out, out],
        out_shape=[jax.ShapeDtypeStruct((3, s_len, DL_W), F32)] * 2, args=(qkv, qkv, qkv))


def _dl_mix(o_p, l_p, *, name, carry=None):
    s_len = o_p.shape[1]

    def body(o_ref, l_ref, ob_ref, of_ref, lt_ref, o_scr, l_scr):
        n = pl.program_id(1)
        for k, d in enumerate(DILATIONS):
            @pl.when(n == k)
            def _(k=k, d=d):
                _unpermute_rows(o_scr.at[k], o_ref, d)
                _unpermute_rows(l_scr.at[k], l_ref, d)

        @pl.when(n == 2)
        def _():
            l0, l1, l2 = l_scr[0], l_scr[1], l_scr[2]
            m = jnp.maximum(jnp.maximum(l0, l1), l2)
            e0, e1, e2 = jnp.exp(l0 - m), jnp.exp(l1 - m), jnp.exp(l2 - m)
            den = e0 + e1 + e2
            o = (e0 / den) * o_scr[0] + (e1 / den) * o_scr[1] + (e2 / den) * o_scr[2]
            of_ref[...] = o
            ob_ref[...] = o.astype(BF16)
            lt_ref[...] = m + jnp.log(den)

    inb = pl.BlockSpec((None, s_len, LANES), lambda cb, n: (n, 0, cb))
    outb = pl.BlockSpec((s_len, LANES), lambda cb, n: (0, cb))
    return _call(
        body, grid=(4, 3), name=name, carry=carry, in_specs=[inb, inb], out_specs=[outb, outb, outb],
        out_shape=[jax.ShapeDtypeStruct((s_len, DL_W), BF16), jax.ShapeDtypeStruct((s_len, DL_W), F32),
                   jax.ShapeDtypeStruct((s_len, DL_W), F32)],
        scratch_shapes=[pltpu.VMEM((3, s_len, LANES), F32), pltpu.VMEM((3, s_len, LANES), F32)], args=(o_p, l_p))


def _dl_bwd_prep(dcat, o, lse, *, name):
    s_len = o.shape[0]

    def body(do_ref, o_ref, l_ref, dop_ref, st_ref, d_scr):
        n = pl.program_id(1)

        @pl.when(n == 0)
        def _():
            r0 = lax.broadcasted_iota(jnp.int32, (LANES, LANES), 0) // HD
            r1 = lax.broadcasted_iota(jnp.int32, (LANES, LANES), 1) // HD
            d_scr[...] = _dot_hilo(do_ref[...] * o_ref[...], (r0 == r1).astype(BF16))

        for k, d in enumerate(DILATIONS):
            @pl.when(n == k)
            def _(d=d):
                _permute_rows(dop_ref, do_ref, d, BF16)
                _permute_rows(st_ref.at[0], d_scr, d, F32)
                _permute_rows(st_ref.at[1], l_ref, d, F32)

    nat = lambda c0: pl.BlockSpec((s_len, LANES), lambda cb, n: (0, c0 + cb))
    return pl.pallas_call(
        body, grid=(4, 3), name=name, in_specs=[nat(4), nat(0), nat(0)],
        out_specs=[pl.BlockSpec((None, s_len, LANES), lambda cb, n: (n, 0, cb)),
                   pl.BlockSpec((2, None, s_len, LANES), lambda cb, n: (0, n, 0, cb))],
        out_shape=[jax.ShapeDtypeStruct((3, s_len, DL_W), BF16), jax.ShapeDtypeStruct((2, 3, s_len, DL_W), F32)],
        scratch_shapes=[pltpu.VMEM((s_len, LANES), F32)], compiler_params=_cp())(dcat, o, lse)


def _dl_bwd(qkv, dop, stats, *, name, carry=None):
    s_len = qkv.shape[2]

    def body(q_ref, k_ref, v_ref, do_ref, st_ref, cur_ref, prev_ref):
        n = pl.program_id(0)
        lane_h = _head_masks()[2]
        band_c, band_p = _dl_band(2 * BLK)

        def per_head(x):
            xr = pltpu.roll(x, HD, 1)
            return jnp.concatenate([jnp.where(lane_h, x, xr), jnp.where(lane_h, xr, x)], axis=0)

        @pl.loop(0, s_len // BLK, step=DL_UNROLL)
        def _(i0):
            blocks = [i0 + t for t in range(DL_UNROLL)]
            rows = [_dl_rows(i) for i in blocks]
            stage1 = []
            for cur, prev in rows:
                qs = _stack_heads(q_ref[cur, :] * 0.125, lane_h)
                dos = _stack_heads(do_ref[cur, :], lane_h)
                kc, kp, vc, vp = k_ref[cur, :], k_ref[prev, :], v_ref[cur, :], v_ref[prev, :]
                stage1.append((qs, dos, _dot_nt(qs, kc), _dot_nt(qs, kp), _dot_nt(dos, vc), _dot_nt(dos, vp)))
            stage2 = []
            for i, (cur, prev), (qs, dos, sc, sp, dpc, dpp) in zip(blocks, rows, stage1):
                lse, delta = per_head(st_ref[1, cur, :]), per_head(st_ref[0, cur, :])
                pc = jnp.where(band_c, jnp.exp(sc - lse), 0.0)
                pp = jnp.where(jnp.logical_and(band_p, jnp.logical_not(_dl_first(s_len, n, i))), jnp.exp(sp - lse), 0.0)
                stage2.append((pc.astype(BF16), pp.astype(BF16), (pc * (dpc - delta)).astype(BF16),
                               (pp * (dpp - delta)).astype(BF16)))
            for (cur, prev), (qs, dos, *_), (pc, pp, dsc, dsp) in zip(rows, stage1, stage2):
                dq = _dot(dsc, k_ref[cur, :]) + _dot(dsp, k_ref[prev, :])
                cur_ref[0, cur, :] = jnp.where(lane_h, dq[:BLK], dq[BLK:]) * 0.125
                cur_ref[1, cur, :] = _dot_tn(dsc, qs)
                cur_ref[2, cur, :] = _dot_tn(pc, dos)
                prev_ref[0, cur, :] = _dot_tn(dsp, qs)
                prev_ref[1, cur, :] = _dot_tn(pp, dos)

    return _call(
        body, grid=(3, 4), name=name, carry=carry,
        in_specs=_dl_in_specs(s_len) + [pl.BlockSpec((None, s_len, LANES), lambda n, hp: (n, 0, hp)),
                                        pl.BlockSpec((2, None, s_len, LANES), lambda n, hp: (0, n, 0, hp))],
        out_specs=[pl.BlockSpec((3, None, s_len, LANES), lambda n, hp: (0, n, 0, hp)),
                   pl.BlockSpec((2, None, s_len, LANES), lambda n, hp: (0, n, 0, hp))],
        out_shape=[jax.ShapeDtypeStruct((3, 3, s_len, DL_W), F32), jax.ShapeDtypeStruct((2, 3, s_len, DL_W), F32)],
        vmem_mb=56, args=(qkv, qkv, qkv, dop, stats))


def _dl_bwd_finish(cur, prev, cos, sin, *, name):
    s_len = cur.shape[2]

    def body(c_ref, p_ref, cos_ref, sin_ref, o_ref, p_scr, u_scr, acc):
        a, n = pl.program_id(0), pl.program_id(2)
        has_prev = jnp.where(a > 0, 1.0, 0.0)
        p_scr[...] = c_ref[...]
        p_scr[0:s_len - BLK, :] += has_prev * p_ref[BLK:, :]
        for k, d in enumerate(DILATIONS):
            @pl.when(n == k)
            def _(k=k, d=d):
                if k == 0:
                    acc[...] = p_scr[...]
                else:
                    _unpermute_rows(u_scr, p_scr, d)
                    acc[...] += u_scr[...]

        @pl.when(n == 2)
        def _():
            dy = acc[...]
            rot = a < 2
            o_ref[...] = (dy * jnp.where(rot, cos_ref[...], 1.0)
                          + _rot_half(dy * jnp.where(rot, sin_ref[...], 0.0))).astype(BF16)

    tab = pl.BlockSpec((s_len, LANES), lambda a, cb, n: (0, 0))
    return pl.pallas_call(
        body, grid=(3, 4, 3), name=name,
        in_specs=[pl.BlockSpec((None, None, s_len, LANES), lambda a, cb, n: (a, n, 0, cb)),
                  pl.BlockSpec((None, None, s_len, LANES), lambda a, cb, n: (jnp.maximum(a - 1, 0), n, 0, cb)),
                  tab, tab],
        out_specs=pl.BlockSpec((s_len, LANES), lambda a, cb, n: (0, 4 * a + cb)),
        out_shape=jax.ShapeDtypeStruct((s_len, 3 * DL_W), BF16),
        scratch_shapes=[pltpu.VMEM((s_len, LANES), F32)] * 3, compiler_params=_cp())(cur, prev, cos, sin)


XA_T = 1024


def _xa_probs(q, k):
    s = _dot_nt(q, k) * (X_HD ** -0.5)
    e = jnp.exp(s - jnp.max(s, axis=1, keepdims=True))
    return e / jnp.sum(e, axis=1, keepdims=True)


def _xa_fwd(q, k, v, *, name):
    s_len, d = q.shape
    nm = k.shape[0]

    def body(q_ref, k_ref, v_ref, o_ref):
        for h in range(X_HEADS):
            cs = slice(h * X_HD, (h + 1) * X_HD)
            p = _xa_probs(q_ref[:, cs], k_ref[:, cs])
            o_ref[:, cs] = _dot(p.astype(BF16), v_ref[:, cs]).astype(BF16)

    row = pl.BlockSpec((XA_T, d), lambda i: (i, 0))
    full = pl.BlockSpec((nm, d), lambda i: (0, 0))
    return pl.pallas_call(body, grid=(s_len // XA_T,), name=name, in_specs=[row, full, full], out_specs=row,
                          out_shape=jax.ShapeDtypeStruct((s_len, d), BF16), compiler_params=_cp())(q, k, v)


def _xa_bwd(q, k, v, do, *, name, carry=None):
    s_len, d = q.shape
    nm = k.shape[0]

    def body(q_ref, k_ref, v_ref, do_ref, dq_ref, dk_ref, dv_ref):
        i = pl.program_id(0)
        for h in range(X_HEADS):
            cs = slice(h * X_HD, (h + 1) * X_HD)
            qh, kh, vh, doh = q_ref[:, cs], k_ref[:, cs], v_ref[:, cs], do_ref[:, cs]
            p = _xa_probs(qh, kh)
            dp = _dot_nt(doh, vh)
            ds = (p * (dp - jnp.sum(dp * p, axis=1, keepdims=True)) * (X_HD ** -0.5)).astype(BF16)
            dq_ref[:, cs] = _dot(ds, kh).astype(BF16)
            dkh, dvh = _dot_tn(ds, qh), _dot_tn(p.astype(BF16), doh)

            @pl.when(i == 0)
            def _(cs=cs, dkh=dkh, dvh=dvh):
                dk_ref[:, cs] = dkh
                dv_ref[:, cs] = dvh

            @pl.when(i > 0)
            def _(cs=cs, dkh=dkh, dvh=dvh):
                dk_ref[:, cs] += dkh
                dv_ref[:, cs] += dvh

    row = pl.BlockSpec((XA_T, d), lambda i: (i, 0))
    full = pl.BlockSpec((nm, d), lambda i: (0, 0))
    return _call(
        body, grid=(s_len // XA_T,), name=name, carry=carry, in_specs=[row, full, full, row],
        out_specs=[row, full, full],
        out_shape=[jax.ShapeDtypeStruct((s_len, d), BF16), jax.ShapeDtypeStruct((nm, d), F32),
                   jax.ShapeDtypeStruct((nm, d), F32)], args=(q, k, v, do))


FF_TM, FF_TN, FF_H = 512, 256, 8
GELU_K, GELU_C = 0.7978845608028654, 0.044715


FF_STRIP = 64


def _ff_conv(e_ref, w_ref, b_ref, rows, r0=0):
    return (w_ref[0:1, :] * e_ref[pl.ds(FF_H - 2 + r0, rows), :] + w_ref[1:2, :] * e_ref[pl.ds(FF_H - 1 + r0, rows), :]
            + w_ref[2:3, :] * e_ref[pl.ds(FF_H + r0, rows), :] + b_ref[...])


def _strips(total, size):
    return [(r0, min(size, total - r0)) for r0 in range(0, total, size)]


def _ff_gate_fwd(up, conv_w, conv_b, *, name, carry=None):
    s_len = up.shape[0]
    nj = D_FF // FF_TN

    def body(g_ref, v_ref, gp_ref, vp_ref, wg_ref, wv_ref, bg_ref, bv_ref, o_ref, eg, ev):
        i = pl.program_id(0)
        for e, cur, prev in ((eg, g_ref, gp_ref), (ev, v_ref, vp_ref)):
            e[0:FF_H, :] = jnp.where(i > 0, prev[...], 0.0)
            e[FF_H:, :] = cur[...]
        for r0, rows in _strips(FF_TM, FF_STRIP):
            gate = _ff_conv(eg, wg_ref, bg_ref, rows, r0)
            val = _ff_conv(ev, wv_ref, bv_ref, rows, r0)
            t = jnp.tanh(GELU_K * (gate + GELU_C * gate * gate * gate))
            o_ref[r0:r0 + rows, :] = (0.5 * gate * (1.0 + t) * val).astype(BF16)

    cur = lambda c0: pl.BlockSpec((FF_TM, FF_TN), lambda i, j: (i, c0 + j))
    prev = lambda c0: pl.BlockSpec((FF_H, FF_TN), lambda i, j: (jnp.maximum(i * (FF_TM // FF_H) - 1, 0), c0 + j))
    par = lambda r, c0: pl.BlockSpec((r, FF_TN), lambda i, j: (0, c0 + j))
    return _call(
        body, grid=(s_len // FF_TM, nj), name=name, carry=carry,
        in_specs=[cur(0), cur(nj), prev(0), prev(nj), par(3, 0), par(3, nj), par(1, 0), par(1, nj)],
        out_specs=[cur(0)], out_shape=[jax.ShapeDtypeStruct((s_len, D_FF), BF16)],
        scratch_shapes=[pltpu.VMEM((FF_TM + FF_H, FF_TN), F32)] * 2,
        args=(up, up, up, up, conv_w, conv_w, conv_b, conv_b))


def _ff_gate_bwd(up, dact, conv_w, conv_b, *, name, carry=None):
    s_len = up.shape[0]
    nj = D_FF // FF_TN
    last = s_len // FF_TM - 1
    ext = FF_TM + FF_H

    def body(g_ref, v_ref, gp_ref, vp_ref, gn_ref, vn_ref, da_ref, dan_ref, wg_ref, wv_ref, bg_ref, bv_ref,
             dg_ref, dv_ref, dw_ref, db_ref, eg, ev, sg, sv):
        i = pl.program_id(1)
        for e, cur, prev, nxt in ((eg, g_ref, gp_ref, gn_ref), (ev, v_ref, vp_ref, vn_ref)):
            e[0:FF_H, :] = jnp.where(i > 0, prev[...], 0.0)
            e[FF_H:FF_H + FF_TM, :] = cur[...]
            e[FF_H + FF_TM:, :] = nxt[...]
        for r0, rows in _strips(ext, FF_STRIP):
            gate = _ff_conv(eg, wg_ref, bg_ref, rows, r0)
            val = _ff_conv(ev, wv_ref, bv_ref, rows, r0)
            dact = da_ref[r0:r0 + rows, :] if r0 < FF_TM else jnp.where(i < last, dan_ref[...], 0.0)
            g2, kg = gate * gate, GELU_K * gate
            t = jnp.tanh(kg * (1.0 + GELU_C * g2))
            half = 0.5 + 0.5 * t
            dgelu = half + (0.5 * kg) * ((1.0 - t * t) * (1.0 + (3.0 * GELU_C) * g2))
            sg[r0:r0 + rows, :] = dact * val * dgelu
            sv[r0:r0 + rows, :] = dact * (gate * half)
        for part, (s, e, w_ref, out) in enumerate(((sg, eg, wg_ref, dg_ref), (sv, ev, wv_ref, dv_ref))):
            taps, bias = [jnp.zeros((8, FF_TN), F32)] * 3, jnp.zeros((8, FF_TN), F32)
            for r0, rows in _strips(FF_TM, FF_STRIP):
                d0 = s[pl.ds(r0, rows), :]
                out[r0:r0 + rows, :] = (w_ref[2:3, :] * d0 + w_ref[1:2, :] * s[pl.ds(r0 + 1, rows), :]
                                        + w_ref[0:1, :] * s[pl.ds(r0 + 2, rows), :]).astype(BF16)
                taps = [taps[k] + _rowsum8(d0 * e[pl.ds(FF_H - 2 + k + r0, rows), :]) for k in range(3)]
                bias = bias + _rowsum8(d0)
            _acc_out(dw_ref.at[part], i, jnp.concatenate(taps, axis=0))
            _acc_out(db_ref.at[part], i, bias)

    cur = lambda c0: pl.BlockSpec((FF_TM, FF_TN), lambda j, i: (i, c0 + j))
    prev = lambda c0: pl.BlockSpec((FF_H, FF_TN), lambda j, i: (jnp.maximum(i * (FF_TM // FF_H) - 1, 0), c0 + j))
    nxt = lambda c0: pl.BlockSpec(
        (FF_H, FF_TN), lambda j, i: (jnp.minimum((i + 1) * (FF_TM // FF_H), s_len // FF_H - 1), c0 + j))
    par = lambda r, c0: pl.BlockSpec((r, FF_TN), lambda j, i: (0, c0 + j))
    return _call(
        body, grid=(nj, s_len // FF_TM), name=name, carry=carry,
        in_specs=[cur(0), cur(nj), prev(0), prev(nj), nxt(0), nxt(nj), cur(0), nxt(0),
                  par(3, 0), par(3, nj), par(1, 0), par(1, nj)],
        out_specs=[cur(0), cur(0), pl.BlockSpec((2, 24, FF_TN), lambda j, i: (0, 0, j)),
                   pl.BlockSpec((2, 8, FF_TN), lambda j, i: (0, 0, j))],
        out_shape=[jax.ShapeDtypeStruct((s_len, D_FF), BF16), jax.ShapeDtypeStruct((s_len, D_FF), BF16),
                   jax.ShapeDtypeStruct((2, 24, D_FF), F32), jax.ShapeDtypeStruct((2, 8, D_FF), F32)],
        scratch_shapes=[pltpu.VMEM((FF_TM + 2 * FF_H, FF_TN), F32)] * 2 + [pltpu.VMEM((ext, FF_TN), F32)] * 2,
        args=(up, up, up, up, up, up, dact, dact, conv_w, conv_w, conv_b, conv_b))


def _place():
    x, y, c = lax.axis_index("x"), lax.axis_index("y"), lax.axis_index("c")
    return x, y, c, [(1 - x, y), (x, 1 - y), (1 - x, 1 - y)]


def _remote(src, dst, send_sem, recv_sem, dev):
    return pltpu.make_async_remote_copy(src_ref=src, dst_ref=dst, send_sem=send_sem, recv_sem=recv_sem,
                                        device_id=dev, device_id_type=MESH)


_ANY = pl.BlockSpec(memory_space=pl.ANY)


N_SEMS = 8
SEM_BASE_2 = 4


class _Exchange:
    def __init__(self, operands, out_shapes, start, wait, aliases=None):
        self.operands, self.out_shapes, self.start, self.wait = list(operands), list(out_shapes), start, wait
        self.aliases = aliases or {}


def _sem_scratch():
    return [pltpu.SemaphoreType.DMA((N_SEMS,)), pltpu.SemaphoreType.DMA((N_SEMS,)), pltpu.SemaphoreType.DMA]


def _run_exchange(ex, *, name):
    k, n = len(ex.operands), len(ex.out_shapes)

    def body(*refs):
        ins, outs, sems = refs[:k], refs[k:k + n], refs[k + n:]
        ex.start(ins, outs, *sems)
        ex.wait(ins, outs, *sems)

    return pl.pallas_call(body, name=name, in_specs=[_ANY] * k, out_specs=[_ANY] * n, out_shape=ex.out_shapes,
                          scratch_shapes=_sem_scratch(), input_output_aliases=ex.aliases,
                          compiler_params=_cp(16))(*ex.operands)


def _call(body, *, grid, in_specs, out_specs, out_shape, args, name, scratch_shapes=(), vmem_mb=48, carry=None):
    scratch_shapes = list(scratch_shapes)
    if carry is None:
        return pl.pallas_call(body, grid=grid, name=name, in_specs=in_specs, out_specs=out_specs, out_shape=out_shape,
                              scratch_shapes=scratch_shapes, compiler_params=_cp(vmem_mb))(*args)
    n_in, n_out, n_scr = len(in_specs), len(out_shape), len(scratch_shapes)
    k_in, k_out = len(carry.operands), len(carry.out_shapes)

    def wrapped(*refs):
        ins, refs = refs[:n_in], refs[n_in:]
        cin, refs = refs[:k_in], refs[k_in:]
        outs, refs = refs[:n_out], refs[n_out:]
        cout, refs = refs[:k_out], refs[k_out:]
        scratch, sems = refs[:n_scr], refs[n_scr:]
        ids = [pl.program_id(a) for a in range(len(grid))]
        first = functools.reduce(jnp.logical_and, [i == 0 for i in ids])
        last = functools.reduce(jnp.logical_and, [i == g - 1 for i, g in zip(ids, grid)])

        @pl.when(first)
        def _():
            carry.start(cin, cout, *sems)

        body(*ins, *outs, *scratch)

        @pl.when(last)
        def _():
            carry.wait(cin, cout, *sems)

    aliases = {n_in + i: n_out + o for i, o in carry.aliases.items()}
    return pl.pallas_call(
        wrapped, grid=grid, name=name, in_specs=list(in_specs) + [_ANY] * k_in,
        out_specs=list(out_specs) + [_ANY] * k_out, out_shape=list(out_shape) + carry.out_shapes,
        scratch_shapes=scratch_shapes + _sem_scratch(), input_output_aliases=aliases,
        compiler_params=_cp(vmem_mb))(*args, *carry.operands)


def _half_rows(ref_rows, c):
    half = ref_rows // 2
    return pl.ds(c * half, half)


def _ex_join(a, b):
    ka, na = len(a.operands), len(a.out_shapes)

    def start(ins, outs, *sems):
        a.start(ins[:ka], outs[:na], *sems)
        b.start(ins[ka:], outs[na:], *sems)

    def wait(ins, outs, *sems):
        a.wait(ins[:ka], outs[:na], *sems)
        b.wait(ins[ka:], outs[na:], *sems)

    aliases = dict(a.aliases)
    aliases.update({ka + i: na + o for i, o in b.aliases.items()})
    return _Exchange(a.operands + b.operands, a.out_shapes + b.out_shapes, start, wait, aliases)


def _ex_gather(pack, r0, rl, base=0):
    def copies(ins, outs, send, recv):
        x, y, c, chips = _place()
        rows = _half_rows(rl, c)
        src = ins[0].at[pl.ds(r0 + c * (rl // 2), rl // 2)]
        sends = [_remote(src, outs[0].at[2 * x + y, rows], send.at[base + k], recv.at[base + k], (px, py, c))
                 for k, (px, py) in enumerate(chips)]
        lands = [_remote(src, outs[0].at[2 * px + py, rows], send.at[base + k], recv.at[base + k], (px, py, c))
                 for k, (px, py) in enumerate(chips)]
        return sends, lands

    def mine(ins, outs, local):
        x, y, _, _ = _place()
        return pltpu.make_async_copy(ins[0].at[pl.ds(r0, rl)], outs[0].at[2 * x + y], local)

    def start(ins, outs, send, recv, local):
        mine(ins, outs, local).start()
        for cp in copies(ins, outs, send, recv)[0]:
            cp.start()

    def wait(ins, outs, send, recv, local):
        sends, lands = copies(ins, outs, send, recv)
        for cp in lands:
            cp.wait_recv()
        for cp in sends:
            cp.wait_send()
        mine(ins, outs, local).wait()

    return _Exchange([pack], [jax.ShapeDtypeStruct((4, rl, pack.shape[1]), pack.dtype)], start, wait)


def _ex_gather_forward(g, base=0):
    rl = g.shape[1]

    def copies(outs, send, recv):
        x, y, c, chips = _place()
        slabs = [(outs[0].at[2 * px + py, _half_rows(rl, c)], outs[0].at[2 * px + py, _half_rows(rl, 1 - c)])
                 for px, py in chips]
        sends = [_remote(a, a, send.at[base + k], recv.at[base + k], (x, y, 1 - c)) for k, (a, _) in enumerate(slabs)]
        lands = [_remote(b, b, send.at[base + k], recv.at[base + k], (x, y, 1 - c)) for k, (_, b) in enumerate(slabs)]
        return sends, lands

    def start(ins, outs, send, recv, local):
        for cp in copies(outs, send, recv)[0]:
            cp.start()

    def wait(ins, outs, send, recv, local):
        sends, lands = copies(outs, send, recv)
        for cp in lands:
            cp.wait_recv()
        for cp in sends:
            cp.wait_send()

    return _Exchange([g], [jax.ShapeDtypeStruct(g.shape, g.dtype)], start, wait, aliases={0: 0})


def _ex_swap_halves(gw, base=0):
    nb, rl, d = gw.shape

    def copies(ins, outs, send, recv):
        x, y, c, _ = _place()
        return [_remote(ins[0].at[j, _half_rows(rl, 1 - c)], outs[0].at[j], send.at[base + j], recv.at[base + j],
                        (x, y, 1 - c)) for j in range(nb)]

    def start(ins, outs, send, recv, local):
        for cp in copies(ins, outs, send, recv):
            cp.start()

    def wait(ins, outs, send, recv, local):
        for cp in copies(ins, outs, send, recv):
            cp.wait()

    return _Exchange([gw], [jax.ShapeDtypeStruct((nb, rl // 2, d), gw.dtype)], start, wait)


def _chip_sum(gw, got, c_arr, *, name):
    nchip, half, d = got.shape
    tr = _tile(half, 512)

    def body(c_ref, a_ref, b_ref, o32_ref, o16_ref):
        s = a_ref[...] + b_ref[...]
        o32_ref[...] = s
        o16_ref[...] = s.astype(BF16)

    blk = pl.BlockSpec((None, tr, d), lambda j, i, c_ref: (j, i, 0))
    return pl.pallas_call(
        body, name=name,
        grid_spec=pltpu.PrefetchScalarGridSpec(
            num_scalar_prefetch=1, grid=(nchip, half // tr),
            in_specs=[pl.BlockSpec((None, tr, d), lambda j, i, c_ref: (j, c_ref[0] * (half // tr) + i, 0)), blk],
            out_specs=[blk, blk]),
        out_shape=[jax.ShapeDtypeStruct((nchip, half, d), F32), jax.ShapeDtypeStruct((nchip, half, d), BF16)],
        compiler_params=_cp())(c_arr, gw, got)


def _ex_scatter(s16, base=0):
    def copies(ins, outs, send, recv):
        x, y, c, chips = _place()
        return [_remote(ins[0].at[2 * px + py], outs[0].at[k], send.at[base + k], recv.at[base + k], (px, py, c))
                for k, (px, py) in enumerate(chips)]

    def start(ins, outs, send, recv, local):
        for cp in copies(ins, outs, send, recv):
            cp.start()

    def wait(ins, outs, send, recv, local):
        for cp in copies(ins, outs, send, recv):
            cp.wait()

    return _Exchange([s16], [jax.ShapeDtypeStruct((3,) + s16.shape[1:], s16.dtype)], start, wait)


def _mesh_sum(s32, got, j_arr, *, name):
    _, rl, d = s32.shape
    tr = _tile(rl, 512)

    def body(j_ref, a_ref, b_ref, o_ref):
        o_ref[...] = ((a_ref[...] + b_ref[0].astype(F32)) + b_ref[1].astype(F32)) + b_ref[2].astype(F32)

    return pl.pallas_call(
        body, name=name,
        grid_spec=pltpu.PrefetchScalarGridSpec(
            num_scalar_prefetch=1, grid=(rl // tr,),
            in_specs=[pl.BlockSpec((None, tr, d), lambda i, j_ref: (j_ref[0], i, 0)),
                      pl.BlockSpec((3, tr, d), lambda i, j_ref: (0, i, 0))],
            out_specs=pl.BlockSpec((tr, d), lambda i, j_ref: (i, 0))),
        out_shape=jax.ShapeDtypeStruct((rl, d), F32), compiler_params=_cp())(j_arr, s32, got)


def _ex_share_halves(ghalf):
    half, d = ghalf.shape

    def copies(ins, outs, send, recv, local):
        x, y, c, _ = _place()
        there = outs[0].at[_half_rows(2 * half, c)]
        back = outs[0].at[_half_rows(2 * half, 1 - c)]
        return (_remote(ins[0], there, send.at[0], recv.at[0], (x, y, 1 - c)),
                _remote(ins[0], back, send.at[0], recv.at[0], (x, y, 1 - c)), pltpu.make_async_copy(ins[0], there, local))

    def start(ins, outs, send, recv, local):
        out, _, mine = copies(ins, outs, send, recv, local)
        mine.start()
        out.start()

    def wait(ins, outs, send, recv, local):
        out, back, mine = copies(ins, outs, send, recv, local)
        back.wait_recv()
        out.wait_send()
        mine.wait()

    return _Exchange([ghalf], [jax.ShapeDtypeStruct((2 * half, d), ghalf.dtype)], start, wait)


class _ReduceScatter:
    def __init__(self, gw, c_arr, j_arr, tag):
        self.gw, self.c_arr, self.j_arr, self.tag = gw, c_arr, j_arr, tag

    def swap(self, base=0):
        return _ex_swap_halves(self.gw, base)

    def after_swap(self, got, base=0):
        self.s32, s16 = _chip_sum(self.gw, got, self.c_arr, name=f"rs_chip_sum{self.tag}")
        return _ex_scatter(s16, base)

    def after_scatter(self, got16):
        ghalf = _mesh_sum(self.s32, got16, self.j_arr, name=f"rs_mesh_sum{self.tag}")
        return _run_exchange(_ex_share_halves(ghalf), name=f"rs_share{self.tag}")[0]

    def run(self):
        got, = _run_exchange(self.swap(), name=f"rs_swap{self.tag}")
        got16, = _run_exchange(self.after_swap(got), name=f"rs_scatter{self.tag}")
        return self.after_scatter(got16)


def _all_reduce_small(vec, *, name):
    rows, d = vec.shape

    def body(x_ref, o_ref, gat, send_sems, recv_sems, local_sem):
        x, y, c, chips = _place()
        me, sibling = (x, y, c), (x, y, 1 - c)

        def slot(px, py, pc):
            return gat.at[4 * px + 2 * py + pc]

        def copy(k, block, to, src=None):
            return _remote(slot(*block) if src is None else src, slot(*block), send_sems.at[k], recv_sems.at[k], to)

        mine = pltpu.make_async_copy(x_ref, slot(*me), local_sem)
        mine.start()
        first = [copy(0, me, sibling, src=x_ref)]
        first += [copy(1 + j, me, (*chip, c), src=x_ref) for j, chip in enumerate(chips)]
        for cp in first:
            cp.start()
        passed = [copy(4 + j, (*chip, c), sibling) for j, chip in enumerate(chips)]
        for j, chip in enumerate(chips):
            copy(1 + j, (*chip, c), me).wait_recv()
            passed[j].start()
        copy(0, sibling, me).wait_recv()
        for j, chip in enumerate(chips):
            copy(4 + j, (*chip, 1 - c), me).wait_recv()
        for cp in first + passed:
            cp.wait_send()
        mine.wait()
        acc = gat[0]
        for dev in range(1, 8):
            acc = acc + gat[dev]
        o_ref[...] = acc

    vm = pl.BlockSpec(memory_space=pltpu.VMEM)
    return pl.pallas_call(
        body, name=name, in_specs=[vm], out_specs=vm, out_shape=jax.ShapeDtypeStruct((rows, d), F32),
        scratch_shapes=[pltpu.VMEM((8, rows, d), F32), pltpu.SemaphoreType.DMA((7,)), pltpu.SemaphoreType.DMA((7,)),
                        pltpu.SemaphoreType.DMA],
        compiler_params=_cp(32))(vec)


COL_SHARDED = ("w_in", "ffn_w_up")


def _to_pack_rows(name, shard):
    return shard.reshape(-1, D_MODEL)


def _full_from_blocks(name, blocks):
    rows = blocks.shape[1]
    if name in COL_SHARDED:
        return blocks.reshape(4, D_MODEL, rows).transpose(1, 0, 2).reshape(D_MODEL, 4 * rows)
    return blocks.reshape(4 * rows, D_MODEL)


def _blocks_from_full(name, full):
    if name in COL_SHARDED:
        cols = full.shape[1] // 4
        return full.reshape(D_MODEL, 4, cols).transpose(1, 0, 2).reshape(4, cols, D_MODEL)
    return full.reshape(4, full.shape[0] // 4, D_MODEL)


def _row(v):
    return v.reshape(1, -1)


SMALL = (("mix_norm_pre", (1024,), None), ("cv_w", (31, 256), 1), ("cv_b", (256,), None), ("cv_ln_g", (256,), None),
         ("cv_ln_b", (256,), None), ("cv_pw_w", (256, 256), 0), ("cv_pw_b", (256,), None),
         ("mix_norm_post", (1024,), None), ("x_norm_pre", (1024,), None), ("mem_norm", (1024,), None),
         ("x_norm_post", (1024,), None), ("ffn_norm_pre", (1024,), None), ("ffn_conv_w", (3, 5632), 1),
         ("ffn_conv_b", (5632,), None), ("ffn_norm_post", (1024,), None))
BIG = tuple(n for n, _ in PACK_ROWS)
WEIGHT_ORDER = ("mix_norm_pre", "w_in", "cv_w", "cv_b", "cv_ln_g", "cv_ln_b", "cv_pw_w", "cv_pw_b", "w_out",
                "mix_norm_post", "x_norm_pre", "mem_norm", "x_wq", "x_wk", "x_wv", "x_wo", "x_norm_post",
                "ffn_norm_pre", "ffn_w_up", "ffn_conv_w", "ffn_conv_b", "ffn_w_down", "ffn_norm_post")


def _flat_rows(parts):
    v = jnp.concatenate([p.reshape(-1) for p in parts])
    rows = -(-v.shape[0] // (8 * D_MODEL)) * 8
    return jnp.pad(v, (0, rows * D_MODEL - v.shape[0])).reshape(rows, D_MODEL)


def _small_to_rows(blocks):
    v = jnp.concatenate([b.reshape(-1) for b in blocks])
    return jnp.pad(v, (0, SMALL_ROWS * D_MODEL - v.shape[0])).reshape(SMALL_ROWS, D_MODEL)


def _small_from_rows(rows):
    flat, out, off = rows.reshape(-1), [], 0
    for _, shape, _ in SHARDED_SMALL:
        size = int(np.prod(shape))
        out.append(flat[off:off + size].reshape(shape))
        off += size
    return out


def _chip_block(full, j, shape, axis):
    return lax.slice_in_dim(full, j * shape[axis], (j + 1) * shape[axis], axis=axis)


REST_GROUP = ("w_in", "w_out")
XA_GROUP = ("x_wq", "x_wk", "x_wv", "x_wo")
FFN_GROUP = ("ffn_w_up", "ffn_w_down")


class _Weights:
    FIRST = (0, 768)
    OWN = ((768, 1024), (1792, 1664), (3456, 704))
    NEXT = ((0, 1024), (1024, 1024), (2048, 1408), (3456, 704))
    SLOTS = ("mix_in", "sb_fwd", "dl_fwd", "dl_mix", "ffn_up", "ffn_gate", "ffn_down")

    def __init__(self, packs):
        self.packs, self.pieces, self.landed, self.plan = packs, {}, None, {}
        for slot, piece in zip(self.SLOTS[:3], self.OWN):
            self.plan[(0, slot)] = (0,) + piece
        for l in range(len(packs) - 1):
            for slot, piece in zip(self.SLOTS[3:], self.NEXT):
                self.plan[(l, slot)] = (l + 1,) + piece
        first = _run_exchange(_ex_gather(packs[0], *self.FIRST), name="gather_first")[0]
        self.pieces[(0,) + self.FIRST] = _run_exchange(_ex_gather_forward(first), name="gather_first_forward")[0]

    def ride(self, layer, slot, call):
        start, todo, ex = self.plan.get((layer, slot)), [], None
        if start is not None:
            ex = _ex_gather(self.packs[start[0]], start[1], start[2])
            todo.append(("landed", start))
        if self.landed is not None:
            key, buf = self.landed
            forward = _ex_gather_forward(buf, SEM_BASE_2 if ex is not None else 0)
            ex = forward if ex is None else _ex_join(ex, forward)
            todo.append(("piece", key))
            self.landed = None
        outs = list(call(carry=ex))
        n = len(outs) - len(todo)
        for (kind, key), buf in zip(todo, outs[n:]):
            if kind == "landed":
                self.landed = (key, buf)
            else:
                self.pieces[key] = buf
        return outs[:n]

    def rows_of(self, layer, name):
        off = 0
        for n, rows in WEIGHT_PACK:
            if n == name:
                break
            off += rows
        for (l, r0, nrows), buf in self.pieces.items():
            if l == layer and r0 <= off < r0 + nrows:
                return buf[:, off - r0:off - r0 + rows, :]
        raise KeyError(f"{name} of layer {layer} is not gathered yet")

    def weight(self, layer, name):
        return _full_from_blocks(name, self.rows_of(layer, name))

    def small(self, layer):
        planes = lax.bitcast_convert_type(self.rows_of(layer, "small").astype(jnp.bfloat16), jnp.uint16)
        planes = planes.astype(jnp.uint32)
        bits = (planes[:, :SMALL_ROWS] << 16) | planes[:, SMALL_ROWS:]
        per_chip = [_small_from_rows(r) for r in lax.bitcast_convert_type(bits, F32)]
        return {n: jnp.concatenate([blocks[k] for blocks in per_chip], axis=axis)
                for k, (n, _, axis) in enumerate(SHARDED_SMALL)}


class _Params:
    def __init__(self, weights, layer, small):
        self.weights, self.layer, self.small, self.cache = weights, layer, small, {}

    def __getitem__(self, name):
        if name in self.small:
            return self.small[name]
        if name not in self.cache:
            if name in [n for n, _, _ in SHARDED_SMALL]:
                self.cache.update(self.weights.small(self.layer))
            else:
                self.cache[name] = self.weights.weight(self.layer, name)
        return self.cache[name]


def _layer_fwd(h0, mem, p, cos, sin, tag, ride):
    sv = {"h0": h0}
    n1, u = ride("mix_in", functools.partial(_rms_mm, h0, _row(p["mix_norm_pre"]), p["w_in"], tm=1024, tn=1408,
                                             out_dtype=F32, name=f"mix_in{tag}"))
    a_out, = ride("sb_fwd", functools.partial(_sb_fwd, u, name=f"sb_fwd{tag}"))
    b_out, c = _cv_fwd(u, p["cv_w"], _row(p["cv_b"]), _row(p["cv_ln_g"]), _row(p["cv_ln_b"]),
                       p["cv_pw_w"].astype(BF16), _row(p["cv_pw_b"]), name=f"cv_fwd{tag}")
    qkv = _rope_perm(u, cos, sin, name=f"rope_perm{tag}")
    o_p, l_p = ride("dl_fwd", functools.partial(_dl_fwd, qkv, name=f"dl_fwd{tag}"))
    c_out, o_dl, lse = ride("dl_mix", functools.partial(_dl_mix, o_p, l_p, name=f"dl_mix{tag}"))
    cat = jnp.concatenate([a_out, b_out, c_out], axis=1)
    y1, h1 = _mm_post(cat, p["w_out"], h0, _row(p["mix_norm_post"]), tm=512, name=f"mix_out{tag}")
    sv.update(n1=n1, u=u, c=c, qkv=qkv, o_dl=o_dl, lse=lse, cat=cat, y1=y1, h1=h1)

    n2, q = _rms_mm(h1, _row(p["x_norm_pre"]), p["x_wq"], tm=512, tn=1024, out_dtype=BF16, name=f"xa_q{tag}")
    wkv = jnp.concatenate([p["x_wk"], p["x_wv"]], axis=1)
    mem_n, kv = _rms_mm(mem, _row(p["mem_norm"]), wkv, tm=mem.shape[0], tn=1024, out_dtype=BF16, name=f"xa_kv{tag}")
    k, v = kv[:, :D_MODEL], kv[:, D_MODEL:]
    o_x = _xa_fwd(q, k, v, name=f"xa_fwd{tag}")
    y2, h2 = _mm_post(o_x, p["x_wo"], h1, _row(p["x_norm_post"]), tm=512, name=f"xa_out{tag}")
    sv.update(n2=n2, q=q, mem_n=mem_n, k=k, v=v, o_x=o_x, y2=y2, h2=h2, wkv=wkv)

    n3, up = ride("ffn_up", functools.partial(_rms_mm, h2, _row(p["ffn_norm_pre"]), p["ffn_w_up"], tm=1024, tn=1408,
                                              out_dtype=F32, name=f"ffn_up{tag}"))
    act, = ride("ffn_gate", functools.partial(_ff_gate_fwd, up, p["ffn_conv_w"], _row(p["ffn_conv_b"]),
                                              name=f"ffn_gate{tag}"))
    y3, h3 = ride("ffn_down", functools.partial(_mm_post, act, p["ffn_w_down"], h2, _row(p["ffn_norm_post"]), tm=512,
                                                name=f"ffn_down{tag}"))
    sv.update(n3=n3, up=up, act=act, y3=y3)
    return h3, sv


def _layer_bwd(dh3, mem, p, sv, cos, sin, tag, riding, new_rs):
    g = {}
    s8 = lambda part: part.sum(axis=0)
    rode = None

    dy3, dgp = _rms_bwd(sv["y3"], _row(p["ffn_norm_post"]), dh3, None, out_dtype=BF16, tm=512, name=f"ffn_post_b{tag}")
    g["ffn_norm_post"] = s8(dgp)
    dact = _mm_nt(dy3, p["ffn_w_down"], tm=512, tn=1408, out_dtype=F32, name=f"ffn_down_bx{tag}")
    g["ffn_w_down"] = _mm_tn(sv["act"], dy3, tk=1408, tn=1024, tm=2048, name=f"ffn_down_bw{tag}")
    dgu, dvu, dcw, dcb, *got = _ff_gate_bwd(sv["up"], dact, p["ffn_conv_w"], _row(p["ffn_conv_b"]),
                                            name=f"ffn_gate_b{tag}", carry=riding.swap() if riding else None)
    scatter = riding.after_swap(got[0]) if riding else None
    g["ffn_conv_w"] = jnp.concatenate([dcw[0], dcw[1]], axis=1).reshape(3, 8, 2 * D_FF).sum(axis=1)
    g["ffn_conv_b"] = jnp.concatenate([dcb[0], dcb[1]], axis=1).sum(axis=0)
    dup = jnp.concatenate([dgu, dvu], axis=1)
    dn3 = _mm_nt(dup, p["ffn_w_up"], tm=256, tn=512, out_dtype=F32, name=f"ffn_up_bx{tag}")
    g["ffn_w_up"] = _mm_tn(sv["n3"], dup, tk=512, tn=1408, tm=2048, name=f"ffn_up_bw{tag}")
    ffn_rs = new_rs(FFN_GROUP, g, f"{tag}_ffn")
    dh2, dgp = _rms_bwd(sv["h2"], _row(p["ffn_norm_pre"]), dn3, dh3, out_dtype=F32, tm=512, name=f"ffn_pre_b{tag}")
    g["ffn_norm_pre"] = s8(dgp)

    dy2, dgp = _rms_bwd(sv["y2"], _row(p["x_norm_post"]), dh2, None, out_dtype=BF16, tm=512, name=f"xa_post_b{tag}")
    g["x_norm_post"] = s8(dgp)
    do_x = _mm_nt(dy2, p["x_wo"], tm=512, tn=1024, out_dtype=BF16, name=f"xa_out_bx{tag}")
    g["x_wo"] = _mm_tn(sv["o_x"], dy2, tk=512, tn=1024, tm=2048, name=f"xa_out_bw{tag}")
    dq, dk, dv, got = _xa_bwd(sv["q"], sv["k"], sv["v"], do_x, name=f"xa_bwd{tag}", carry=ffn_rs.swap())
    ffn_scatter = ffn_rs.after_swap(got)
    dn2 = _mm_nt(dq, p["x_wq"], tm=512, tn=1024, out_dtype=F32, name=f"xa_q_bx{tag}")
    g["x_wq"] = _mm_tn(sv["n2"], dq, tk=512, tn=1024, tm=2048, name=f"xa_q_bw{tag}")
    dkv = jnp.concatenate([dk, dv], axis=1).astype(BF16)
    nm = mem.shape[0]
    dmem_n = _mm_nt(dkv, sv["wkv"], tm=nm, tn=1024, out_dtype=F32, name=f"xa_kv_bx{tag}")
    dwkv = _mm_tn(sv["mem_n"], dkv, tk=512, tn=2048, tm=nm, name=f"xa_kv_bw{tag}")
    g["x_wk"], g["x_wv"] = dwkv[:, :D_MODEL], dwkv[:, D_MODEL:]
    _, dgp = _rms_bwd(mem, _row(p["mem_norm"]), dmem_n, None, out_dtype=BF16, tm=nm, name=f"xa_mem_b{tag}")
    g["mem_norm"] = s8(dgp)
    xa_rs = new_rs(XA_GROUP, g, f"{tag}_xa")
    dh1, dgp, got = _rms_bwd(sv["h1"], _row(p["x_norm_pre"]), dn2, dh2, out_dtype=F32, tm=512, name=f"xa_pre_b{tag}",
                             carry=xa_rs.swap())
    xa_scatter = xa_rs.after_swap(got, SEM_BASE_2 if riding else 0)
    g["x_norm_pre"] = s8(dgp)

    dy1, dgp = _rms_bwd(sv["y1"], _row(p["mix_norm_post"]), dh1, None, out_dtype=BF16, tm=512, name=f"mix_post_b{tag}")
    g["mix_norm_post"] = s8(dgp)
    dcat = _mm_nt(dy1, p["w_out"], tm=512, tn=1024, out_dtype=F32, name=f"mix_out_bx{tag}")
    g["w_out"] = _mm_tn(sv["cat"], dy1, tk=512, tn=1024, tm=2048, name=f"mix_out_bw{tag}")
    u = sv["u"]
    dq_sb, dk_sb, dv_sb, *got = _sb_bwd(u, dcat, name=f"sb_bwd{tag}",
                                        carry=_ex_join(scatter, xa_scatter) if riding else xa_scatter)
    if riding:
        rode = riding.after_scatter(got[0])
    xa_rows = xa_rs.after_scatter(got[-1])
    pw_b16 = p["cv_pw_w"].astype(BF16)
    dc, dpw, vec = _cv_bwd_local(sv["c"], dcat, _row(p["cv_ln_g"]), _row(p["cv_ln_b"]), pw_b16, name=f"cv_bwd_a{tag}")
    g["cv_pw_w"] = dpw
    vec = vec.reshape(3, 8, CV_W).sum(axis=1)
    g["cv_pw_b"], g["cv_ln_g"], g["cv_ln_b"] = vec[0], vec[1], vec[2]
    du_cv, dcw, dcb = _cv_bwd_conv(u, dc, p["cv_w"], name=f"cv_bwd_b{tag}")
    g["cv_w"] = dcw.reshape(CV_K, 8, CV_W).sum(axis=1)
    g["cv_b"] = dcb.sum(axis=0)
    dop, stats = _dl_bwd_prep(dcat, sv["o_dl"], sv["lse"], name=f"dl_prep_b{tag}")
    cur, prev, got = _dl_bwd(sv["qkv"], dop, stats, name=f"dl_bwd{tag}", carry=ffn_scatter)
    ffn_rows = ffn_rs.after_scatter(got)
    du_dl = _dl_bwd_finish(cur, prev, cos, sin, name=f"dl_fin_b{tag}")
    du = jnp.concatenate([dq_sb.astype(BF16), dk_sb.astype(BF16), dv_sb.astype(BF16), du_cv, du_dl], axis=1)
    dn1 = _mm_nt(du, p["w_in"], tm=512, tn=512, out_dtype=F32, name=f"mix_in_bx{tag}")
    g["w_in"] = _mm_tn(sv["n1"], du, tk=512, tn=1408, tm=2048, name=f"mix_in_bw{tag}")
    dh0, dgp = _rms_bwd(sv["h0"], _row(p["mix_norm_pre"]), dn1, dh1, out_dtype=F32, tm=512, name=f"mix_pre_b{tag}")
    g["mix_norm_pre"] = s8(dgp)
    return dh0, g, (xa_rows, ffn_rows), rode


def _step(x, mem, positions, loss_target, w, m, v):
    depth = w["w_in"].shape[0]
    xi, yi, ci = lax.axis_index("x"), lax.axis_index("y"), lax.axis_index("c")
    chip = 2 * xi + yi
    h = x[0]
    mem0 = mem[0]
    s_len = h.shape[0]

    def pack_rows(n, l):
        if n == "small":
            bits = lax.bitcast_convert_type(_small_to_rows([w[name][l] for name, _, _ in SHARDED_SMALL]), jnp.uint32)
            planes = [(bits >> 16).astype(jnp.uint16), (bits & 0xFFFF).astype(jnp.uint16)]
            return jnp.concatenate([lax.bitcast_convert_type(p, jnp.bfloat16) for p in planes], axis=0)
        return _to_pack_rows(n, w[n][l]).astype(BF16)

    packs = [jnp.concatenate([pack_rows(n, l) for n, _ in WEIGHT_PACK], axis=0) for l in range(depth)]
    weights = _Weights(packs)
    params = [_Params(weights, l, {n: w[n][l] for n, _, axis in SMALL if axis is None}) for l in range(depth)]

    inv_freq = ROPE_THETA ** (-jnp.arange(HD // 2, dtype=F32) / (HD // 2))
    cos, sin = _rope_tables(positions.reshape(s_len, 1), jnp.tile(inv_freq, 4).reshape(1, LANES), name="rope_tables")

    saved = []
    for l in range(depth):
        h, sv = _layer_fwd(h, mem0, params[l], cos, sin, f"_l{l}", functools.partial(weights.ride, l))
        saved.append(sv)
    dh, sq = _loss_grad(h, loss_target[0], tm=512, name="loss_grad")
    loss = lax.psum(0.5 * jnp.sum(sq) / D_MODEL, ("x", "y", "c"))

    c_arr, j_arr = jnp.reshape(ci, (1,)).astype(jnp.int32), jnp.reshape(chip, (1,)).astype(jnp.int32)

    def new_rs(names, g, tag):
        blocks = [_blocks_from_full(n, g[n]) for n in names]
        if names is REST_GROUP:
            blocks.append(jnp.stack([_small_to_rows([_chip_block(g[n], j, shape, axis) for n, shape, axis in SHARDED_SMALL])
                                     for j in range(4)]))
        return _ReduceScatter(jnp.concatenate(blocks, axis=1), c_arr, j_arr, tag)

    grads, later_rows, rest_rows, pending = [None] * depth, [None] * depth, [None] * depth, None
    for l in reversed(range(depth)):
        dh, grads[l], later_rows[l], rode = _layer_bwd(dh, mem0, params[l], saved[l], cos, sin, f"_l{l}", pending, new_rs)
        if pending is not None:
            rest_rows[l + 1] = rode
        pending = new_rs(REST_GROUP, grads[l], f"_l{l}_rest")
    grad_x = dh[None]

    out_g, out_d, out_m, out_v = {}, {}, {}, {}
    pack_off, off = {}, 0
    for n, rows in PACK_ROWS:
        pack_off[n] = (off, rows)
        off += rows

    def reduced(l, n):
        start, rows = pack_off[n]
        for names, block in ((REST_GROUP, rest_rows[l]), (XA_GROUP, later_rows[l][0]), (FFN_GROUP, later_rows[l][1])):
            if n in names:
                return block[start - pack_off[names[0]][0]:][:rows]

    def update(n, carry=None):
        shard_shape = w[n].shape
        g_n = jnp.stack([reduced(l, n) for l in range(depth)]).reshape(shard_shape)
        flat = lambda a: a.reshape(-1, shard_shape[-1])
        d_n, m_n, v_n, *rode = _adamw(flat(w[n]), flat(g_n), flat(m[n]), flat(v[n]), name=f"adamw_{n}", carry=carry)
        out_g[n], out_d[n], out_m[n], out_v[n] = g_n, d_n.reshape(shard_shape), m_n.reshape(shard_shape), v_n.reshape(shard_shape)
        return rode

    rest_rows[0] = pending.run()
    for n, _ in PACK_ROWS:
        update(n)

    g_small = _all_reduce_small(_flat_rows([grads[l][n] for l in range(depth) for n, _, axis in SMALL if axis is None]),
                                name="all_reduce_small_grads").reshape(-1)
    local_g, off = {}, 0
    for l in range(depth):
        for n, shape, axis in SMALL:
            if axis is None:
                size = int(np.prod(shape))
                local_g.setdefault(n, []).append(g_small[off:off + size].reshape(shape))
                off += size
        small_rows = rest_rows[l][sum(pack_off[n][1] for n in REST_GROUP):]
        for (n, _, _), block in zip(SHARDED_SMALL, _small_from_rows(small_rows)):
            local_g.setdefault(n, []).append(block)
    names = [n for n, _, _ in SMALL]
    g_loc = {n: jnp.stack(local_g[n]) for n in names}
    d_s, m_s, v_s = _adamw(_flat_rows([w[n] for n in names]), _flat_rows([g_loc[n] for n in names]),
                           _flat_rows([m[n] for n in names]), _flat_rows([v[n] for n in names]), name="adamw_small")
    off = 0
    for n in names:
        size = int(np.prod(w[n].shape))
        take = lambda a: a.reshape(-1)[off:off + size].reshape(w[n].shape)
        out_g[n], out_d[n], out_m[n], out_v[n] = g_loc[n], take(d_s), take(m_s), take(v_s)
        off += size

    outs = [loss, grad_x]
    for group in (out_g, out_d, out_m, out_v):
        outs += [group[n] for n in WEIGHT_ORDER]
    return tuple(outs)


def kernel(x, mem, positions, mix_norm_pre, w_in, cv_w, cv_b, cv_ln_g, cv_ln_b, cv_pw_w, cv_pw_b, w_out, mix_norm_post, x_norm_pre, mem_norm, x_wq, x_wk, x_wv, x_wo, x_norm_post, ffn_norm_pre, ffn_w_up, ffn_conv_w, ffn_conv_b, ffn_w_down, ffn_norm_post, loss_target, m_mix_norm_pre, m_w_in, m_cv_w, m_cv_b, m_cv_ln_g, m_cv_ln_b, m_cv_pw_w, m_cv_pw_b, m_w_out, m_mix_norm_post, m_x_norm_pre, m_mem_norm, m_x_wq, m_x_wk, m_x_wv, m_x_wo, m_x_norm_post, m_ffn_norm_pre, m_ffn_w_up, m_ffn_conv_w, m_ffn_conv_b, m_ffn_w_down, m_ffn_norm_post, v_mix_norm_pre, v_w_in, v_cv_w, v_cv_b, v_cv_ln_g, v_cv_ln_b, v_cv_pw_w, v_cv_pw_b, v_w_out, v_mix_norm_post, v_x_norm_pre, v_mem_norm, v_x_wq, v_x_wk, v_x_wv, v_x_wo, v_x_norm_post, v_ffn_norm_pre, v_ffn_w_up, v_ffn_conv_w, v_ffn_conv_b, v_ffn_w_down, v_ffn_norm_post):
    w = dict(zip(WEIGHT_ORDER, (mix_norm_pre, w_in, cv_w, cv_b, cv_ln_g, cv_ln_b, cv_pw_w, cv_pw_b, w_out, mix_norm_post, x_norm_pre, mem_norm, x_wq, x_wk, x_wv, x_wo, x_norm_post, ffn_norm_pre, ffn_w_up, ffn_conv_w, ffn_conv_b, ffn_w_down, ffn_norm_post)))
    m = dict(zip(WEIGHT_ORDER, (m_mix_norm_pre, m_w_in, m_cv_w, m_cv_b, m_cv_ln_g, m_cv_ln_b, m_cv_pw_w, m_cv_pw_b, m_w_out, m_mix_norm_post, m_x_norm_pre, m_mem_norm, m_x_wq, m_x_wk, m_x_wv, m_x_wo, m_x_norm_post, m_ffn_norm_pre, m_ffn_w_up, m_ffn_conv_w, m_ffn_conv_b, m_ffn_w_down, m_ffn_norm_post)))
    v = dict(zip(WEIGHT_ORDER, (v_mix_norm_pre, v_w_in, v_cv_w, v_cv_b, v_cv_ln_g, v_cv_ln_b, v_cv_pw_w, v_cv_pw_b, v_w_out, v_mix_norm_post, v_x_norm_pre, v_mem_norm, v_x_wq, v_x_wk, v_x_wv, v_x_wo, v_x_norm_post, v_ffn_norm_pre, v_ffn_w_up, v_ffn_conv_w, v_ffn_conv_b, v_ffn_w_down, v_ffn_norm_post)))
    return _step(x, mem, positions, loss_target, w, m, v)
```
